```python
import jax
import jax.numpy as jnp
from jax import lax
import numpy as np

D_MODEL = 1024
BATCH = 8
SEQ = 2048
DEPTH = 1

GRID_W = 64
CTX_LEN = 256
EPS = 1e-6
HG_HEADS = 4
HG_DIM = 128
HG_WIDTH = HG_HEADS * HG_DIM
CHUNK = 32
ATT_HEADS = 8
ATT_KV_HEADS = 2
HEAD_DIM = 64
ATT_WIDTH = ATT_HEADS * HEAD_DIM
KV_WIDTH = ATT_KV_HEADS * HEAD_DIM
WINDOW = 128
BLOCK = 128
ROPE_THETA = 10000.0
D_FF = ((8 * D_MODEL + 3 * 256 - 1) // (3 * 256)) * 256
CTX_COLS = 3 * HG_WIDTH + 2 * KV_WIDTH
IN_COLS = CTX_COLS + 2 * HG_WIDTH + ATT_WIDTH + 2 * D_MODEL

kernel_name = 'hybrid_hgrn2_swa_prefix_dit_block'


def _rmsnorm(x, w):
    xf = x.astype(jnp.float32)
    y = xf * lax.rsqrt(jnp.mean(xf * xf, axis=-1, keepdims=True) + EPS)
    return (y * w.astype(jnp.float32)).astype(x.dtype)


def _modulate(x, w, shift, scale):
    return _rmsnorm(x, w) * (1.0 + scale) + shift


def _heads(t, n_heads, head_dim):
    return t.reshape(t.shape[0], t.shape[1], n_heads, head_dim)


def _split_ctx_side(p):
    a, b, c_, d = HG_WIDTH, 2 * HG_WIDTH, 3 * HG_WIDTH, 3 * HG_WIDTH + KV_WIDTH
    return p[..., :a], p[..., a:b], p[..., b:c_], p[..., c_:d], p[..., d:CTX_COLS]


def _split_query_side(p):
    a = CTX_COLS + HG_WIDTH
    b = a + HG_WIDTH
    c_ = b + ATT_WIDTH
    return p[..., CTX_COLS:a], p[..., a:b], p[..., b:c_], p[..., c_:IN_COLS]


def _gla_chunkwise(q, k, v, log_f, s0):
    b_, s_, h_, dk = k.shape
    dv = v.shape[-1]
    n = s_ // CHUNK

    def chunks(t):
        return t.astype(jnp.float32).reshape(b_, n, CHUNK, h_, t.shape[-1]).transpose(0, 3, 1, 2, 4)

    k, v, log_f = chunks(k), chunks(v), chunks(log_f)
    cum = jnp.cumsum(log_f, axis=3)
    cum_last = cum[:, :, :, -1:, :]
    u = jnp.einsum('bhnck,bhncv->bhnkv', k * jnp.exp(cum_last - cum), v)
    decay = jnp.exp(cum_last[:, :, :, 0, :])

    def step(state, inp):
        d, un = inp
        return d[..., None] * state + un, state

    s_final, s_start = lax.scan(step, s0.astype(jnp.float32),
                                (jnp.moveaxis(decay, 2, 0), jnp.moveaxis(u, 2, 0)))
    if q is None:
        return None, s_final
    s_start = jnp.moveaxis(s_start, 0, 2)
    qd = chunks(q) * jnp.exp(cum)
    scores = jnp.einsum('bhnck,bhnsk->bhncs', qd, k * jnp.exp(-cum))
    lower_tri = jnp.tril(jnp.ones((CHUNK, CHUNK), dtype=bool))
    scores = jnp.where(lower_tri, scores, 0.0)
    o = (jnp.einsum('bhncs,bhnsv->bhncv', scores, v)
         + jnp.einsum('bhnck,bhnkv->bhncv', qd, s_start))
    o = o.transpose(0, 2, 3, 1, 4).reshape(b_, s_, h_, dv)
    return o, s_final


def _hgrn_query(q_raw):
    return _heads(jax.nn.silu(q_raw.astype(jnp.float32)) * HG_DIM ** -0.5, HG_HEADS, HG_DIM)


def _hgrn_direction(q, f_logit, inp, lb, s0, reverse):
    f = lb + (1.0 - lb) * jax.nn.sigmoid(f_logit.astype(jnp.float32))
    log_f = _heads(jnp.log(f), HG_HEADS, HG_DIM)
    k = _heads(1.0 - f, HG_HEADS, HG_DIM)
    v = _heads(inp, HG_HEADS, HG_DIM)
    if reverse:
        q = None if q is None else jnp.flip(q, 1)
        k, v, log_f = jnp.flip(k, 1), jnp.flip(v, 1), jnp.flip(log_f, 1)
    o, s = _gla_chunkwise(q, k, v, log_f, s0)
    if reverse and o is not None:
        o = jnp.flip(o, 1)
    return o, s


def _hgrn_readout(o, g_raw, norm_w, dtype):
    g = _heads(g_raw, HG_HEADS, HG_DIM).astype(jnp.float32)
    y = _rmsnorm(o, norm_w) * jax.nn.silu(g)
    return y.reshape(o.shape[0], o.shape[1], HG_WIDTH).astype(dtype)


def _rope_1d(t, pos):
    d = t.shape[-1]
    inv_freq = ROPE_THETA ** (-jnp.arange(0, d, 2, dtype=jnp.float32) / d)
    ang = pos.astype(jnp.float32)[:, None] * inv_freq[None, :]
    cos = jnp.cos(ang)[None, :, None, :]
    sin = jnp.sin(ang)[None, :, None, :]
    tf = t.astype(jnp.float32)
    t1, t2 = tf[..., : d // 2], tf[..., d // 2:]
    return jnp.concatenate([t1 * cos - t2 * sin, t1 * sin + t2 * cos], axis=-1)


def _axial_rope(t, rows, cols):
    half = t.shape[-1] // 2
    return jnp.concatenate([_rope_1d(t[..., :half], rows), _rope_1d(t[..., half:], cols)],
                           axis=-1).astype(t.dtype)


def _window_attention(q, k, v, kc, vc, sinks):
    b_, s_, hq, dh = q.shape
    hkv = k.shape[2]
    grp = hq // hkv
    nb = s_ // BLOCK
    f32 = jnp.float32
    qb = q.astype(f32).reshape(b_, nb, BLOCK, hkv, grp, dh) * dh ** -0.5

    def band(t):
        tp = jnp.pad(t.astype(f32), ((0, 0), (BLOCK, BLOCK), (0, 0), (0, 0)))
        tp = tp.reshape(b_, nb + 2, BLOCK, hkv, dh)
        return jnp.concatenate([tp[:, :-2], tp[:, 1:-1], tp[:, 2:]], axis=2)

    kw, vw = band(k), band(v)
    s_loc = jnp.einsum('bnqhgd,bnkhd->bnhgqk', qb, kw)
    qi = jnp.arange(BLOCK)[:, None]
    kj = jnp.arange(3 * BLOCK)[None, :]
    k_pos = (jnp.arange(nb)[:, None, None] - 1) * BLOCK + kj[None]
    valid = (jnp.abs(kj - BLOCK - qi) <= WINDOW)[None] & (k_pos >= 0) & (k_pos < s_)
    s_loc = jnp.where(valid[None, :, None, None], s_loc, -jnp.inf)
    s_ctx = jnp.einsum('bnqhgd,blhd->bnhgql', qb, kc.astype(f32))
    sink = jnp.broadcast_to(sinks.astype(f32).reshape(1, 1, hkv, grp, 1, 1), s_loc.shape[:-1] + (1,))
    p = jax.nn.softmax(jnp.concatenate([s_loc, s_ctx, sink], axis=-1), axis=-1)
    n_loc = 3 * BLOCK
    n_ctx = kc.shape[1]
    o = (jnp.einsum('bnhgqk,bnkhd->bnqhgd', p[..., :n_loc], vw)
         + jnp.einsum('bnhgql,blhd->bnqhgd', p[..., n_loc:n_loc + n_ctx], vc.astype(f32)))
    return o.reshape(b_, s_, hq * dh).astype(v.dtype)


def _context_attention(qc, kc, vc, sinks):
    b_, l_, hq, dh = qc.shape
    hkv = kc.shape[2]
    grp = hq // hkv
    f32 = jnp.float32
    qg = qc.astype(f32).reshape(b_, l_, hkv, grp, dh) * dh ** -0.5
    s = jnp.einsum('blhgd,bmhd->bhglm', qg, kc.astype(f32))
    sink = jnp.broadcast_to(sinks.astype(f32).reshape(1, hkv, grp, 1, 1), s.shape[:-1] + (1,))
    p = jax.nn.softmax(jnp.concatenate([s, sink], axis=-1), axis=-1)
    o = jnp.einsum('bhglm,bmhd->blhgd', p[..., :l_], vc.astype(f32))
    return o.reshape(b_, l_, hq * dh).astype(vc.dtype)


def _merge(y_hg, y_at, gates, w_bh, w_ba, w_o):
    g_hg, g_at = jnp.split(gates, 2, axis=-1)
    mixed = jax.nn.sigmoid(g_hg) * (y_hg @ w_bh) + jax.nn.sigmoid(g_at) * (y_at @ w_ba)
    return mixed @ w_o


def _swiglu(h, w_gate, w_up, w_down):
    return (jax.nn.silu(h @ w_gate) * (h @ w_up)) @ w_down


def _fwd_setup_inputs(seed: int = 0) -> dict:
    key = jax.random.key(seed)
    ks = jax.random.split(key, 20)
    f32 = jnp.float32

    def nrm(k, shape, scale):
        return jax.random.normal(k, shape, f32) * scale

    return {
        'x': nrm(ks[0], (BATCH, SEQ, D_MODEL), 1.0),
        'c': nrm(ks[1], (BATCH, D_MODEL), 1.0),
        'ctx': nrm(ks[2], (BATCH, CTX_LEN, D_MODEL), 1.0),
        'c_ctx': nrm(ks[3], (D_MODEL,), 1.0),
        'w_ada': nrm(ks[4], (DEPTH, D_MODEL, 6 * D_MODEL), 0.5 * D_MODEL ** -0.5),
        'b_ada': nrm(ks[5], (DEPTH, 6 * D_MODEL), 0.02),
        'norm_mix_w': 1.0 + nrm(ks[6], (DEPTH, D_MODEL), 0.02),
        'norm_ffn_w': 1.0 + nrm(ks[7], (DEPTH, D_MODEL), 0.02),
        'w_in': nrm(ks[8], (DEPTH, D_MODEL, IN_COLS), D_MODEL ** -0.5),
        'hgrn_lb_logits': nrm(ks[9], (2, DEPTH + 1, HG_WIDTH), 0.5),
        'hgrn_norm_w': 1.0 + nrm(ks[10], (DEPTH, HG_DIM), 0.02),
        'q_norm_w': 1.0 + nrm(ks[11], (DEPTH, HEAD_DIM), 0.02),
        'k_norm_w': 1.0 + nrm(ks[12], (DEPTH, HEAD_DIM), 0.02),
        'attn_sinks': nrm(ks[13], (DEPTH, ATT_HEADS), 0.5),
        'w_branch_hgrn': nrm(ks[14], (DEPTH, HG_WIDTH, D_MODEL), HG_WIDTH ** -0.5),
        'w_branch_attn': nrm(ks[15], (DEPTH, ATT_WIDTH, D_MODEL), ATT_WIDTH ** -0.5),
        'w_out': nrm(ks[16], (DEPTH, D_MODEL, D_MODEL), D_MODEL ** -0.5),
        'w_ffn_gate': nrm(ks[17], (DEPTH, D_MODEL, D_FF), D_MODEL ** -0.5),
        'w_ffn_up': nrm(ks[18], (DEPTH, D_MODEL, D_FF), D_MODEL ** -0.5),
        'w_ffn_down': nrm(ks[19], (DEPTH, D_FF, D_MODEL), D_FF ** -0.5),
    }


def _fwd_reference(x, c, ctx, c_ctx, w_ada, b_ada, norm_mix_w, norm_ffn_w, w_in, hgrn_lb_logits,
              hgrn_norm_w, q_norm_w, k_norm_w, attn_sinks, w_branch_hgrn, w_branch_attn,
              w_out, w_ffn_gate, w_ffn_up, w_ffn_down):
    n_lat = x.shape[1]
    ROWS = n_lat // GRID_W
    rows = jnp.repeat(jnp.arange(ROWS), GRID_W)
    cols = jnp.tile(jnp.arange(GRID_W), ROWS)
    lower_bounds = jnp.cumsum(jax.nn.softmax(hgrn_lb_logits.astype(jnp.float32), axis=1), axis=1)

    for layer in range(DEPTH):
        last = layer == DEPTH - 1
        mod = jax.nn.silu(c) @ w_ada[layer] + b_ada[layer]
        mod_c = jax.nn.silu(c_ctx) @ w_ada[layer] + b_ada[layer]
        sh1, sc1, g1, sh2, sc2, g2 = [m[:, None, :] for m in jnp.split(mod, 6, axis=-1)]
        csh1, csc1, cg1, csh2, csc2, cg2 = jnp.split(mod_c, 6, axis=-1)
        lb_fwd, lb_bwd = lower_bounds[0, layer], lower_bounds[1, layer]

        h = _modulate(x, norm_mix_w[layer], sh1, sc1)
        hc = _modulate(ctx, norm_mix_w[layer], csh1, csc1)
        p = h @ w_in[layer]
        pc = hc @ (w_in[layer, :, :CTX_COLS] if last else w_in[layer])

        cf_fwd, cf_bwd, c_inp, c_k, c_v = _split_ctx_side(pc)
        s0 = jnp.zeros((ctx.shape[0], HG_HEADS, HG_DIM, HG_DIM), jnp.float32)
        c_q_hg = None if last else _hgrn_query(_split_query_side(pc)[0])
        co_fwd, cs_fwd = _hgrn_direction(c_q_hg, cf_fwd, c_inp, lb_fwd, s0, False)
        co_bwd, cs_bwd = _hgrn_direction(c_q_hg, cf_bwd, c_inp, lb_bwd, s0, True)
        ck = _rmsnorm(_heads(c_k, ATT_KV_HEADS, HEAD_DIM), k_norm_w[layer])
        cv = _heads(c_v, ATT_KV_HEADS, HEAD_DIM)

        f_fwd, f_bwd, inp, k_raw, v_raw = _split_ctx_side(p)
        q_hg_raw, g_hg, q_raw, gates = _split_query_side(p)
        q_hg = _hgrn_query(q_hg_raw)
        o_fwd, _ = _hgrn_direction(q_hg, f_fwd, inp, lb_fwd, cs_fwd, False)
        o_bwd, _ = _hgrn_direction(q_hg, f_bwd, inp, lb_bwd, cs_bwd, True)
        y_hg = _hgrn_readout(o_fwd + o_bwd, g_hg, hgrn_norm_w[layer], x.dtype)
        q = _axial_rope(_rmsnorm(_heads(q_raw, ATT_HEADS, HEAD_DIM), q_norm_w[layer]), rows, cols)
        k = _axial_rope(_rmsnorm(_heads(k_raw, ATT_KV_HEADS, HEAD_DIM), k_norm_w[layer]), rows, cols)
        y_at = _window_attention(q, k, _heads(v_raw, ATT_KV_HEADS, HEAD_DIM), ck, cv, attn_sinks[layer])
        x_new = x + g1 * _merge(y_hg, y_at, gates, w_branch_hgrn[layer], w_branch_attn[layer], w_out[layer])
        x_new = x_new + g2 * _swiglu(_modulate(x_new, norm_ffn_w[layer], sh2, sc2),
                                     w_ffn_gate[layer], w_ffn_up[layer], w_ffn_down[layer])

        if not last:
            _, cg_hg, cq_raw, c_gates = _split_query_side(pc)
            cy_hg = _hgrn_readout(co_fwd + co_bwd, cg_hg, hgrn_norm_w[layer], ctx.dtype)
            cq = _rmsnorm(_heads(cq_raw, ATT_HEADS, HEAD_DIM), q_norm_w[layer])
            cy_at = _context_attention(cq, ck, cv, attn_sinks[layer])
            ctx = ctx + cg1 * _merge(cy_hg, cy_at, c_gates, w_branch_hgrn[layer],
                                     w_branch_attn[layer], w_out[layer])
            ctx = ctx + cg2 * _swiglu(_modulate(ctx, norm_ffn_w[layer], csh2, csc2),
                                      w_ffn_gate[layer], w_ffn_up[layer], w_ffn_down[layer])
        x = x_new
    return x


import jax as _jax
import jax.numpy as _jnp

TWIN_FORMAT = 'train_step'
FWD_PARAMS = ['x', 'c', 'ctx', 'c_ctx', 'w_ada', 'b_ada', 'norm_mix_w', 'norm_ffn_w', 'w_in', 'hgrn_lb_logits', 'hgrn_norm_w', 'q_norm_w', 'k_norm_w', 'attn_sinks', 'w_branch_hgrn', 'w_branch_attn', 'w_out', 'w_ffn_gate', 'w_ffn_up', 'w_ffn_down']
TWIN_WEIGHTS = ['c_ctx', 'w_ada', 'b_ada', 'norm_mix_w', 'norm_ffn_w', 'w_in', 'hgrn_lb_logits', 'hgrn_norm_w', 'q_norm_w', 'k_norm_w', 'attn_sinks', 'w_branch_hgrn', 'w_branch_attn', 'w_out', 'w_ffn_gate', 'w_ffn_up', 'w_ffn_down']
TWIN_DIFF_INPUT = 'x'
TWIN_INPUTS = ['x', 'c', 'ctx', 'c_ctx', 'w_ada', 'b_ada', 'norm_mix_w', 'norm_ffn_w', 'w_in', 'hgrn_lb_logits', 'hgrn_norm_w', 'q_norm_w', 'k_norm_w', 'attn_sinks', 'w_branch_hgrn', 'w_branch_attn', 'w_out', 'w_ffn_gate', 'w_ffn_up', 'w_ffn_down', 'loss_target', 'm_c_ctx', 'm_w_ada', 'm_b_ada', 'm_norm_mix_w', 'm_norm_ffn_w', 'm_w_in', 'm_hgrn_lb_logits', 'm_hgrn_norm_w', 'm_q_norm_w', 'm_k_norm_w', 'm_attn_sinks', 'm_w_branch_hgrn', 'm_w_branch_attn', 'm_w_out', 'm_w_ffn_gate', 'm_w_ffn_up', 'm_w_ffn_down', 'v_c_ctx', 'v_w_ada', 'v_b_ada', 'v_norm_mix_w', 'v_norm_ffn_w', 'v_w_in', 'v_hgrn_lb_logits', 'v_hgrn_norm_w', 'v_q_norm_w', 'v_k_norm_w', 'v_attn_sinks', 'v_w_branch_hgrn', 'v_w_branch_attn', 'v_w_out', 'v_w_ffn_gate', 'v_w_ffn_up', 'v_w_ffn_down']
TWIN_OUTPUTS = ['loss', 'grad_x', 'grad_c_ctx', 'grad_w_ada', 'grad_b_ada', 'grad_norm_mix_w', 'grad_norm_ffn_w', 'grad_w_in', 'grad_hgrn_lb_logits', 'grad_hgrn_norm_w', 'grad_q_norm_w', 'grad_k_norm_w', 'grad_attn_sinks', 'grad_w_branch_hgrn', 'grad_w_branch_attn', 'grad_w_out', 'grad_w_ffn_gate', 'grad_w_ffn_up', 'grad_w_ffn_down', 'delta_c_ctx', 'delta_w_ada', 'delta_b_ada', 'delta_norm_mix_w', 'delta_norm_ffn_w', 'delta_w_in', 'delta_hgrn_lb_logits', 'delta_hgrn_norm_w', 'delta_q_norm_w', 'delta_k_norm_w', 'delta_attn_sinks', 'delta_w_branch_hgrn', 'delta_w_branch_attn', 'delta_w_out', 'delta_w_ffn_gate', 'delta_w_ffn_up', 'delta_w_ffn_down', 'new_m_c_ctx', 'new_m_w_ada', 'new_m_b_ada', 'new_m_norm_mix_w', 'new_m_norm_ffn_w', 'new_m_w_in', 'new_m_hgrn_lb_logits', 'new_m_hgrn_norm_w', 'new_m_q_norm_w', 'new_m_k_norm_w', 'new_m_attn_sinks', 'new_m_w_branch_hgrn', 'new_m_w_branch_attn', 'new_m_w_out', 'new_m_w_ffn_gate', 'new_m_w_ffn_up', 'new_m_w_ffn_down', 'new_v_c_ctx', 'new_v_w_ada', 'new_v_b_ada', 'new_v_norm_mix_w', 'new_v_norm_ffn_w', 'new_v_w_in', 'new_v_hgrn_lb_logits', 'new_v_hgrn_norm_w', 'new_v_q_norm_w', 'new_v_k_norm_w', 'new_v_attn_sinks', 'new_v_w_branch_hgrn', 'new_v_w_branch_attn', 'new_v_w_out', 'new_v_w_ffn_gate', 'new_v_w_ffn_up', 'new_v_w_ffn_down']
TWIN_LEAF_KINDS = {'loss': 'loss', 'grad_x': 'grad_x', 'grad_c_ctx': 'grad_w', 'grad_w_ada': 'grad_w', 'grad_b_ada': 'grad_w', 'grad_norm_mix_w': 'grad_w', 'grad_norm_ffn_w': 'grad_w', 'grad_w_in': 'grad_w', 'grad_hgrn_lb_logits': 'grad_w', 'grad_hgrn_norm_w': 'grad_w', 'grad_q_norm_w': 'grad_w', 'grad_k_norm_w': 'grad_w', 'grad_attn_sinks': 'grad_w', 'grad_w_branch_hgrn': 'grad_w', 'grad_w_branch_attn': 'grad_w', 'grad_w_out': 'grad_w', 'grad_w_ffn_gate': 'grad_w', 'grad_w_ffn_up': 'grad_w', 'grad_w_ffn_down': 'grad_w', 'delta_c_ctx': 'delta_w', 'delta_w_ada': 'delta_w', 'delta_b_ada': 'delta_w', 'delta_norm_mix_w': 'delta_w', 'delta_norm_ffn_w': 'delta_w', 'delta_w_in': 'delta_w', 'delta_hgrn_lb_logits': 'delta_w', 'delta_hgrn_norm_w': 'delta_w', 'delta_q_norm_w': 'delta_w', 'delta_k_norm_w': 'delta_w', 'delta_attn_sinks': 'delta_w', 'delta_w_branch_hgrn': 'delta_w', 'delta_w_branch_attn': 'delta_w', 'delta_w_out': 'delta_w', 'delta_w_ffn_gate': 'delta_w', 'delta_w_ffn_up': 'delta_w', 'delta_w_ffn_down': 'delta_w', 'new_m_c_ctx': 'new_m', 'new_m_w_ada': 'new_m', 'new_m_b_ada': 'new_m', 'new_m_norm_mix_w': 'new_m', 'new_m_norm_ffn_w': 'new_m', 'new_m_w_in': 'new_m', 'new_m_hgrn_lb_logits': 'new_m', 'new_m_hgrn_norm_w': 'new_m', 'new_m_q_norm_w': 'new_m', 'new_m_k_norm_w': 'new_m', 'new_m_attn_sinks': 'new_m', 'new_m_w_branch_hgrn': 'new_m', 'new_m_w_branch_attn': 'new_m', 'new_m_w_out': 'new_m', 'new_m_w_ffn_gate': 'new_m', 'new_m_w_ffn_up': 'new_m', 'new_m_w_ffn_down': 'new_m', 'new_v_c_ctx': 'new_v', 'new_v_w_ada': 'new_v', 'new_v_b_ada': 'new_v', 'new_v_norm_mix_w': 'new_v', 'new_v_norm_ffn_w': 'new_v', 'new_v_w_in': 'new_v', 'new_v_hgrn_lb_logits': 'new_v', 'new_v_hgrn_norm_w': 'new_v', 'new_v_q_norm_w': 'new_v', 'new_v_k_norm_w': 'new_v', 'new_v_attn_sinks': 'new_v', 'new_v_w_branch_hgrn': 'new_v', 'new_v_w_branch_attn': 'new_v', 'new_v_w_out': 'new_v', 'new_v_w_ffn_gate': 'new_v', 'new_v_w_ffn_up': 'new_v', 'new_v_w_ffn_down': 'new_v'}


def _forward(args):
    return _fwd_reference(*[args[k] for k in FWD_PARAMS])


def _output_shape():
    out = _jax.eval_shape(lambda: _forward(_fwd_setup_inputs(0)))
    return out.shape, out.dtype

N_MICROBATCH = 1
ADAM_LR = 0.001
ADAM_B1 = 0.9
ADAM_B2 = 0.999
ADAM_EPS = 1e-08
ADAM_WD = 0.01
ADAM_STEP = 10
PER_EXAMPLE_BATCH_AXIS = {'x': 0, 'c': 0, 'ctx': 0, 'loss_target': 0}
SHARED_INPUTS = []
_WEIGHT_DTYPES = {'c_ctx': _jnp.float32, 'w_ada': _jnp.float32, 'b_ada': _jnp.float32, 'norm_mix_w': _jnp.float32, 'norm_ffn_w': _jnp.float32, 'w_in': _jnp.float32, 'hgrn_lb_logits': _jnp.float32, 'hgrn_norm_w': _jnp.float32, 'q_norm_w': _jnp.float32, 'k_norm_w': _jnp.float32, 'attn_sinks': _jnp.float32, 'w_branch_hgrn': _jnp.float32, 'w_branch_attn': _jnp.float32, 'w_out': _jnp.float32, 'w_ffn_gate': _jnp.float32, 'w_ffn_up': _jnp.float32, 'w_ffn_down': _jnp.float32}
MOMENT_SCALE = {'c_ctx': 3.352700e-02, 'w_ada': 4.270574e-01, 'b_ada': 9.349439e-01, 'norm_mix_w': 2.401510e-01, 'norm_ffn_w': 1.735065e+00, 'w_in': 3.278416e-02, 'hgrn_lb_logits': 1.573622e-03, 'hgrn_norm_w': 1.601659e+00, 'q_norm_w': 3.425586e-02, 'k_norm_w': 3.518600e-02, 'attn_sinks': 4.412515e-03, 'w_branch_hgrn': 3.198563e-02, 'w_branch_attn': 2.688092e-02, 'w_out': 3.774843e-02, 'w_ffn_gate': 4.481528e-02, 'w_ffn_up': 3.277084e-02, 'w_ffn_down': 5.065599e-02}


def _to_microbatches(a, axis):
    t = _jnp.moveaxis(a, axis, 0)
    t = t.reshape((N_MICROBATCH, t.shape[0] // N_MICROBATCH) + t.shape[1:])
    return _jnp.moveaxis(t, 1, axis + 1)


def setup_inputs(seed: int = 0) -> dict:
    inp = _fwd_setup_inputs(seed)
    key = _jax.random.fold_in(_jax.random.key(seed), 7919)
    shape, _ = _output_shape()
    out = dict(inp)
    out["loss_target"] = _jax.random.normal(_jax.random.fold_in(key, 0), shape, _jnp.float32)
    for i, name in enumerate(TWIN_WEIGHTS):
        w = inp[name].astype(_jnp.float32)
        if MOMENT_SCALE is None:
            s = _jnp.sqrt(_jnp.mean(_jnp.square(w)) + 1e-30)
        else:
            s = MOMENT_SCALE[name]
        km, kv = _jax.random.split(_jax.random.fold_in(key, i + 1))
        out[name] = w
        out["m_" + name] = s * _jax.random.normal(km, w.shape, _jnp.float32)
        out["v_" + name] = (s * s) * _jax.random.uniform(kv, w.shape, _jnp.float32, 0.5, 1.5)
    if N_MICROBATCH > 1:
        for name, axis in PER_EXAMPLE_BATCH_AXIS.items():
            out[name] = _to_microbatches(out[name], axis)
    return {'x': out['x'], 'c': out['c'], 'ctx': out['ctx'], 'c_ctx': out['c_ctx'], 'w_ada': out['w_ada'], 'b_ada': out['b_ada'], 'norm_mix_w': out['norm_mix_w'], 'norm_ffn_w': out['norm_ffn_w'], 'w_in': out['w_in'], 'hgrn_lb_logits': out['hgrn_lb_logits'], 'hgrn_norm_w': out['hgrn_norm_w'], 'q_norm_w': out['q_norm_w'], 'k_norm_w': out['k_norm_w'], 'attn_sinks': out['attn_sinks'], 'w_branch_hgrn': out['w_branch_hgrn'], 'w_branch_attn': out['w_branch_attn'], 'w_out': out['w_out'], 'w_ffn_gate': out['w_ffn_gate'], 'w_ffn_up': out['w_ffn_up'], 'w_ffn_down': out['w_ffn_down'], 'loss_target': out['loss_target'], 'm_c_ctx': out['m_c_ctx'], 'm_w_ada': out['m_w_ada'], 'm_b_ada': out['m_b_ada'], 'm_norm_mix_w': out['m_norm_mix_w'], 'm_norm_ffn_w': out['m_norm_ffn_w'], 'm_w_in': out['m_w_in'], 'm_hgrn_lb_logits': out['m_hgrn_lb_logits'], 'm_hgrn_norm_w': out['m_hgrn_norm_w'], 'm_q_norm_w': out['m_q_norm_w'], 'm_k_norm_w': out['m_k_norm_w'], 'm_attn_sinks': out['m_attn_sinks'], 'm_w_branch_hgrn': out['m_w_branch_hgrn'], 'm_w_branch_attn': out['m_w_branch_attn'], 'm_w_out': out['m_w_out'], 'm_w_ffn_gate': out['m_w_ffn_gate'], 'm_w_ffn_up': out['m_w_ffn_up'], 'm_w_ffn_down': out['m_w_ffn_down'], 'v_c_ctx': out['v_c_ctx'], 'v_w_ada': out['v_w_ada'], 'v_b_ada': out['v_b_ada'], 'v_norm_mix_w': out['v_norm_mix_w'], 'v_norm_ffn_w': out['v_norm_ffn_w'], 'v_w_in': out['v_w_in'], 'v_hgrn_lb_logits': out['v_hgrn_lb_logits'], 'v_hgrn_norm_w': out['v_hgrn_norm_w'], 'v_q_norm_w': out['v_q_norm_w'], 'v_k_norm_w': out['v_k_norm_w'], 'v_attn_sinks': out['v_attn_sinks'], 'v_w_branch_hgrn': out['v_w_branch_hgrn'], 'v_w_branch_attn': out['v_w_branch_attn'], 'v_w_out': out['v_w_out'], 'v_w_ffn_gate': out['v_w_ffn_gate'], 'v_w_ffn_up': out['v_w_ffn_up'], 'v_w_ffn_down': out['v_w_ffn_down']}


def _loss(weights, diff, rest, loss_target):
    with _jax.named_scope("forward"):
        args = {**rest, TWIN_DIFF_INPUT: diff, **{k: w.astype(_WEIGHT_DTYPES[k]) for k, w in weights.items()}}
        y = _forward(args)
    with _jax.named_scope("loss_head"):
        err = _jnp.square(y.astype(_jnp.float32) - loss_target)
        return 0.5 * _jnp.sum(_jnp.mean(err, axis=-1)) if err.ndim else 0.5 * err


def _adamw(w, g, m, v):
    m = ADAM_B1 * m + (1.0 - ADAM_B1) * g
    v = ADAM_B2 * v + (1.0 - ADAM_B2) * _jnp.square(g)
    m_hat = m / (1.0 - ADAM_B1 ** ADAM_STEP)
    v_hat = v / (1.0 - ADAM_B2 ** ADAM_STEP)
    delta = -ADAM_LR * (m_hat / (_jnp.sqrt(v_hat) + ADAM_EPS) + ADAM_WD * w)
    return delta, m, v


def reference(x, c, ctx, c_ctx, w_ada, b_ada, norm_mix_w, norm_ffn_w, w_in, hgrn_lb_logits, hgrn_norm_w, q_norm_w, k_norm_w, attn_sinks, w_branch_hgrn, w_branch_attn, w_out, w_ffn_gate, w_ffn_up, w_ffn_down, loss_target, m_c_ctx, m_w_ada, m_b_ada, m_norm_mix_w, m_norm_ffn_w, m_w_in, m_hgrn_lb_logits, m_hgrn_norm_w, m_q_norm_w, m_k_norm_w, m_attn_sinks, m_w_branch_hgrn, m_w_branch_attn, m_w_out, m_w_ffn_gate, m_w_ffn_up, m_w_ffn_down, v_c_ctx, v_w_ada, v_b_ada, v_norm_mix_w, v_norm_ffn_w, v_w_in, v_hgrn_lb_logits, v_hgrn_norm_w, v_q_norm_w, v_k_norm_w, v_attn_sinks, v_w_branch_hgrn, v_w_branch_attn, v_w_out, v_w_ffn_gate, v_w_ffn_up, v_w_ffn_down):
    given = dict(x=x, c=c, ctx=ctx, c_ctx=c_ctx, w_ada=w_ada, b_ada=b_ada, norm_mix_w=norm_mix_w, norm_ffn_w=norm_ffn_w, w_in=w_in, hgrn_lb_logits=hgrn_lb_logits, hgrn_norm_w=hgrn_norm_w, q_norm_w=q_norm_w, k_norm_w=k_norm_w, attn_sinks=attn_sinks, w_branch_hgrn=w_branch_hgrn, w_branch_attn=w_branch_attn, w_out=w_out, w_ffn_gate=w_ffn_gate, w_ffn_up=w_ffn_up, w_ffn_down=w_ffn_down, loss_target=loss_target, m_c_ctx=m_c_ctx, m_w_ada=m_w_ada, m_b_ada=m_b_ada, m_norm_mix_w=m_norm_mix_w, m_norm_ffn_w=m_norm_ffn_w, m_w_in=m_w_in, m_hgrn_lb_logits=m_hgrn_lb_logits, m_hgrn_norm_w=m_hgrn_norm_w, m_q_norm_w=m_q_norm_w, m_k_norm_w=m_k_norm_w, m_attn_sinks=m_attn_sinks, m_w_branch_hgrn=m_w_branch_hgrn, m_w_branch_attn=m_w_branch_attn, m_w_out=m_w_out, m_w_ffn_gate=m_w_ffn_gate, m_w_ffn_up=m_w_ffn_up, m_w_ffn_down=m_w_ffn_down, v_c_ctx=v_c_ctx, v_w_ada=v_w_ada, v_b_ada=v_b_ada, v_norm_mix_w=v_norm_mix_w, v_norm_ffn_w=v_norm_ffn_w, v_w_in=v_w_in, v_hgrn_lb_logits=v_hgrn_lb_logits, v_hgrn_norm_w=v_hgrn_norm_w, v_q_norm_w=v_q_norm_w, v_k_norm_w=v_k_norm_w, v_attn_sinks=v_attn_sinks, v_w_branch_hgrn=v_w_branch_hgrn, v_w_branch_attn=v_w_branch_attn, v_w_out=v_w_out, v_w_ffn_gate=v_w_ffn_gate, v_w_ffn_up=v_w_ffn_up, v_w_ffn_down=v_w_ffn_down)
    weights = {n: given[n] for n in TWIN_WEIGHTS}
    shared = {n: given[n] for n in SHARED_INPUTS}
    per_example = {n: given[n] for n in ['x', 'c', 'ctx']}
    grad_fn = _jax.value_and_grad(_loss, argnums=(0, 1))

    def one_microbatch(ex, loss_target):
        ex = dict(ex)
        diff = ex.pop(TWIN_DIFF_INPUT)
        return grad_fn(weights, diff, {**shared, **ex}, loss_target)

    if N_MICROBATCH == 1:
        loss, (grad_w, grad_x) = one_microbatch(per_example, given["loss_target"])
    else:
        def body(carry, xs):
            loss_sum, grad_sum = carry
            l_k, (gw_k, gx_k) = one_microbatch(xs[0], xs[1])
            with _jax.named_scope("update"):
                return (loss_sum + l_k, _jax.tree.map(_jnp.add, grad_sum, gw_k)), gx_k

        init = (_jnp.zeros((), _jnp.float32), _jax.tree.map(_jnp.zeros_like, weights))
        (loss, grad_w), grad_x = _jax.lax.scan(body, init, (per_example, given["loss_target"]))
    with _jax.named_scope("update"):
        delta_w, new_m, new_v = {}, {}, {}
        for n in TWIN_WEIGHTS:
            delta_w[n], new_m[n], new_v[n] = _adamw(weights[n], grad_w[n], given["m_" + n], given["v_" + n])
    return (loss, grad_x, *[grad_w[n] for n in TWIN_WEIGHTS], *[delta_w[n] for n in TWIN_WEIGHTS],
            *[new_m[n] for n in TWIN_WEIGHTS], *[new_v[n] for n in TWIN_WEIGHTS])
```

```python
import functools
import math

import numpy as np
import jax
import jax.numpy as jnp
from jax import lax
from jax.experimental import pallas as pl
from jax.experimental.pallas import tpu as pltpu

F32 = jnp.float32
BF16 = jnp.bfloat16

N_DEV = 8
D = 1024
S = 2048
L = 256
T = L + S
TM = 256
N_TILES = T // TM
N_LAT_TILES = S // TM
HG_HEADS = 4
HG_DIM = 128
HGW = 512
CHUNK = 32
N_CHUNKS = T // CHUNK
N_CTX_CHUNKS = L // CHUNK
N_LAT_CHUNKS = S // CHUNK
ATT_HEADS = 8
KV_HEADS = 2
HEAD_DIM = 64
ATW = 512
KVW = 128
BLOCK = 128
N_BLOCKS = S // BLOCK
GRID_W = 64
ROPE_THETA = 10000.0
D_FF = 2816
FF_BLK = D_FF // N_DEV
IN_COLS = 5376
IN_BLK = IN_COLS // N_DEV
ADA_BLK = 6 * D // N_DEV
EPS = 1e-6
WA, WB, WC = 2048, 1280, 2048

ADAM_LR = 0.001
ADAM_B1 = 0.9
ADAM_B2 = 0.999
ADAM_EPS = 1e-08
ADAM_WD = 0.01
ADAM_STEP = 10

VMEM_LIMIT = 56 * 1024 * 1024
MESH = pl.DeviceIdType.MESH


def _cp(sem=None, vmem=VMEM_LIMIT):
    return pltpu.CompilerParams(dimension_semantics=sem, vmem_limit_bytes=vmem)


def _full(shape):
    n = len(shape)
    return pl.BlockSpec(shape, lambda *_: (0,) * n)


_VMEM_WHOLE = pl.BlockSpec(memory_space=pltpu.VMEM)
_ANY = pl.BlockSpec(memory_space=pl.ANY)


def _sigmoid(v):
    return 1.0 / (1.0 + jnp.exp(-v))


def _dot(a, b):
    return jnp.dot(a, b, preferred_element_type=F32)


def _dot_nt(a, b):
    return lax.dot_general(a, b, (((1,), (1,)), ((), ())), preferred_element_type=F32)


def _dot_tn(a, b):
    return lax.dot_general(a, b, (((0,), (0,)), ((), ())), preferred_element_type=F32)


def _split3(v):
    hi = v.astype(BF16)
    r = v - hi.astype(F32)
    mid = r.astype(BF16)
    lo = (r - mid.astype(F32)).astype(BF16)
    return hi, mid, lo


def _dot_exact_rhs01(v, m01):
    hi, mid, lo = _split3(v)
    return _dot(hi, m01) + _dot(mid, m01) + _dot(lo, m01)


def _dot_exact_lhs01(m01, v):
    hi, mid, lo = _split3(v)
    return _dot(m01, hi) + _dot(m01, mid) + _dot(m01, lo)


def _dot_f32(a, b, dot=_dot):
    ah, am, al = _split3(a)
    bh, bm, bl = _split3(b)
    return (dot(ah, bh) + (dot(ah, bm) + dot(am, bh))
            + (dot(am, bm) + dot(ah, bl) + dot(al, bh)))


def _my_pos():
    return lax.axis_index("x"), lax.axis_index("y"), lax.axis_index("c")


def _all_gather(blocks, name, in_vmem):
    n = len(blocks)

    def body(*refs):
        ins, outs = refs[:n], refs[n:2 * n]
        send_sems, recv_sems, local_sems = refs[2 * n:]
        x, y, c = _my_pos()
        me, sibling = (x, y, c), (x, y, 1 - c)
        chips = [(1 - x, y), (x, 1 - y), (1 - x, 1 - y)]

        def slot(a, px, py, pc):
            return outs[a].at[4 * px + 2 * py + pc]

        def copy(a, k, block, to, src=None):
            return pltpu.make_async_remote_copy(
                src_ref=slot(a, *block) if src is None else src, dst_ref=slot(a, *block),
                send_sem=send_sems.at[a, k], recv_sem=recv_sems.at[a, k],
                device_id=to, device_id_type=MESH)

        mine = [pltpu.make_async_copy(ins[a], slot(a, *me), local_sems.at[a]) for a in range(n)]
        for cp in mine:
            cp.start()
        first = []
        for a in range(n):
            first.append(copy(a, 0, me, sibling, src=ins[a]))
            first += [copy(a, 1 + j, me, (*chip, c), src=ins[a]) for j, chip in enumerate(chips)]
        for cp in first:
            cp.start()
        passed = []
        for j, chip in enumerate(chips):
            for a in range(n):
                copy(a, 1 + j, (*chip, c), me).wait_recv()
                fwd = copy(a, 4 + j, (*chip, c), sibling)
                fwd.start()
                passed.append(fwd)
        for a in range(n):
            copy(a, 0, sibling, me).wait_recv()
            for j, chip in enumerate(chips):
                copy(a, 4 + j, (*chip, 1 - c), me).wait_recv()
        for cp in first + passed:
            cp.wait_send()
        for cp in mine:
            cp.wait()

    spec = _VMEM_WHOLE if in_vmem else _ANY
    outs = pl.pallas_call(
        body, name=name,
        out_shape=[jax.ShapeDtypeStruct((N_DEV,) + b.shape, b.dtype) for b in blocks],
        in_specs=[spec] * n, out_specs=[spec] * n,
        scratch_shapes=[pltpu.SemaphoreType.DMA((n, 7)), pltpu.SemaphoreType.DMA((n, 7)),
                        pltpu.SemaphoreType.DMA((n,))],
    )(*blocks)
    return outs


def _reduce_scatter_exchange(contribs, name):
    n = len(contribs)

    def body(*refs):
        ins, outs = refs[:n], refs[n:2 * n]
        send_sems, recv_sems, local_sems = refs[2 * n:]
        x, y, c = _my_pos()
        my_id = 4 * x + 2 * y + c

        def peer(k):
            kx, ky, kc = (k >> 2) & 1, (k >> 1) & 1, k & 1
            px = x + kx - 2 * kx * x
            py = y + ky - 2 * ky * y
            pc = c + kc - 2 * kc * c
            return (px, py, pc), 4 * px + 2 * py + pc

        mine = [pltpu.make_async_copy(ins[a].at[my_id], outs[a].at[my_id], local_sems.at[a])
                for a in range(n)]
        for cp in mine:
            cp.start()
        copies = []
        for k in (1, 2, 4, 3, 5, 6, 7):
            dev, pid = peer(k)
            for a in range(n):
                cp = pltpu.make_async_remote_copy(
                    src_ref=ins[a].at[pid], dst_ref=outs[a].at[my_id],
                    send_sem=send_sems.at[a, k - 1], recv_sem=recv_sems.at[a, k - 1],
                    device_id=dev, device_id_type=MESH)
                cp.start()
                copies.append((cp, a, k, pid))
        for cp, a, k, pid in copies:
            pltpu.make_async_remote_copy(
                src_ref=ins[a].at[pid], dst_ref=outs[a].at[pid],
                send_sem=send_sems.at[a, k - 1], recv_sem=recv_sems.at[a, k - 1],
                device_id=(x, y, c), device_id_type=MESH).wait_recv()
        for cp, a, k, pid in copies:
            cp.wait_send()
        for cp in mine:
            cp.wait()

    outs = pl.pallas_call(
        body, name=name,
        out_shape=[jax.ShapeDtypeStruct(b.shape, b.dtype) for b in contribs],
        in_specs=[_ANY] * n, out_specs=[_ANY] * n,
        scratch_shapes=[pltpu.SemaphoreType.DMA((n, 7)), pltpu.SemaphoreType.DMA((n, 7)),
                        pltpu.SemaphoreType.DMA((n,))],
    )(*contribs)
    return outs


def _mm_nn(a, b, *, tm, tn, out_dtype, name, row_off=0, rows=None):
    rows = a.shape[0] if rows is None else rows
    k, n = b.shape

    def body(a_ref, b_ref, o_ref):
        o_ref[...] = _dot(a_ref[...], b_ref[...]).astype(out_dtype)

    return pl.pallas_call(
        body, name=name, grid=(rows // tm, n // tn),
        in_specs=[pl.BlockSpec((tm, k), lambda i, j: (i + row_off, 0)),
                  pl.BlockSpec((k, tn), lambda i, j: (0, j))],
        out_specs=pl.BlockSpec((tm, tn), lambda i, j: (i, j)),
        out_shape=jax.ShapeDtypeStruct((rows, n), out_dtype),
        compiler_params=_cp(("parallel", "parallel")),
    )(a, b)


def _mm_tn(a, b, *, tk, nk, a_off, tm, tn, out_dtype, name):
    m, n = a.shape[1], b.shape[1]

    def body(a_ref, b_ref, o_ref, acc):
        kk = pl.program_id(2)

        @pl.when(kk == 0)
        def _():
            acc[...] = jnp.zeros_like(acc)

        acc[...] += _dot_tn(a_ref[...], b_ref[...])

        @pl.when(kk == nk - 1)
        def _():
            o_ref[...] = acc[...].astype(out_dtype)

    return pl.pallas_call(
        body, name=name, grid=(m // tm, n // tn, nk),
        in_specs=[pl.BlockSpec((tk, tm), lambda i, j, kk: (kk + a_off, i)),
                  pl.BlockSpec((tk, tn), lambda i, j, kk: (kk, j))],
        out_specs=pl.BlockSpec((tm, tn), lambda i, j, kk: (i, j)),
        out_shape=jax.ShapeDtypeStruct((m, n), out_dtype),
        scratch_shapes=[pltpu.VMEM((tm, tn), F32)],
        compiler_params=_cp(("parallel", "parallel", "arbitrary")),
    )(a, b)


def _mm_tn_blocked(a, b, *, a_blocked, name):
    if a_blocked:
        w, n = a.shape[2], b.shape[1]
        in_specs = [pl.BlockSpec((None, S, w), lambda j: (j, 0, 0)), _full((S, n))]
        out_shape, out_spec = (N_DEV, w, n), pl.BlockSpec((None, w, n), lambda j: (j, 0, 0))
    else:
        m, w = a.shape[1], b.shape[2]
        in_specs = [_full((S, m)), pl.BlockSpec((None, S, w), lambda j: (j, 0, 0))]
        out_shape, out_spec = (N_DEV, m, w), pl.BlockSpec((None, m, w), lambda j: (j, 0, 0))

    def body(a_ref, b_ref, o_ref):
        o_ref[...] = _dot_tn(a_ref[...], b_ref[...]).astype(BF16)

    return pl.pallas_call(
        body, name=name, grid=(N_DEV,), in_specs=in_specs, out_specs=out_spec,
        out_shape=jax.ShapeDtypeStruct(out_shape, BF16),
        compiler_params=_cp(("parallel",)),
    )(a, b)


def _ada_rows(cc, w_ada, b_cols):
    def body(c_ref, w_ref, b_ref, o_ref):
        cv = c_ref[...]
        o_ref[...] = _dot_f32(cv * _sigmoid(cv), w_ref[...]) + b_ref[...]

    return pl.pallas_call(
        body, name="ada_rows",
        in_specs=[_VMEM_WHOLE] * 3, out_specs=_VMEM_WHOLE,
        out_shape=jax.ShapeDtypeStruct((16, ADA_BLK), F32),
        compiler_params=_cp(),
    )(cc, w_ada, b_cols)


def _ada_grads(cc, dm_cols, w_ada):
    def body(c_ref, dm_ref, w_ref, gw_ref, dsc_ref):
        cv = c_ref[...]
        sc = cv * _sigmoid(cv)
        dm = dm_ref[...]
        gw_ref[...] = _dot_f32(sc, dm, dot=_dot_tn)
        dsc_ref[...] = _dot_f32(dm, w_ref[...], dot=_dot_nt)

    return pl.pallas_call(
        body, name="ada_grads",
        in_specs=[_VMEM_WHOLE] * 3, out_specs=[_VMEM_WHOLE] * 2,
        out_shape=[jax.ShapeDtypeStruct((D, ADA_BLK), F32), jax.ShapeDtypeStruct((16, D), F32)],
        compiler_params=_cp(),
    )(cc, dm_cols, w_ada)


def _lat(i):
    return jnp.maximum(i - 1, 0)


def _rms_mod(xv, nw, sh, sc):
    rstd = lax.rsqrt(jnp.mean(xv * xv, axis=-1, keepdims=True) + EPS)
    return (xv * rstd * nw) * (1.0 + sc) + sh


def _rms_mod_bwd(xv, nw, sc, dh):
    rstd = lax.rsqrt(jnp.mean(xv * xv, axis=-1, keepdims=True) + EPS)
    xhat = xv * rstd
    dn = dh * (1.0 + sc)
    dxhat = dn * nw
    dx = rstd * (dxhat - xhat * jnp.mean(dxhat * xhat, axis=-1, keepdims=True))
    return (dx, jnp.sum(dh, axis=0, keepdims=True), jnp.sum(dh * (xhat * nw), axis=0, keepdims=True),
            jnp.sum(dn * xhat, axis=0, keepdims=True))


def _norm_mod_all(ctx, x, nw, sh, sc):
    def body(ctx_ref, x_ref, nw_ref, sh_ref, sc_ref, o_ref):
        i = pl.program_id(0)
        sel = jnp.minimum(i, 1)
        xv = jnp.where(i == 0, ctx_ref[...], x_ref[...])
        o_ref[...] = _rms_mod(xv, nw_ref[...], sh_ref[pl.ds(sel, 1), :], sc_ref[pl.ds(sel, 1), :]).astype(BF16)

    return pl.pallas_call(
        body, name="norm_mod", grid=(N_TILES,),
        in_specs=[_full((TM, D)), pl.BlockSpec((TM, D), lambda i: (_lat(i), 0)),
                  _full((1, D)), _full((2, D)), _full((2, D))],
        out_specs=pl.BlockSpec((TM, D), lambda i: (i, 0)),
        out_shape=jax.ShapeDtypeStruct((T, D), BF16),
        compiler_params=_cp(("parallel",)),
    )(ctx, x, nw, sh, sc)


def _chunk_masks(reverse):
    row = lax.broadcasted_iota(jnp.int32, (TM, TM), 0)
    col = lax.broadcasted_iota(jnp.int32, (TM, TM), 1)
    same = (row // CHUNK) == (col // CHUNK)
    tri = same & ((col >= row) if reverse else (col <= row))
    return same, tri


def _chunk_order(i, reverse):
    if not reverse:
        return i
    return jnp.where(i < N_CTX_CHUNKS, N_CTX_CHUNKS - 1 - i, N_CHUNKS + N_CTX_CHUNKS - 1 - i)


def _decay_terms(z, lb, same01, tri01):
    f = lb + (1.0 - lb) * _sigmoid(z)
    g = jnp.log(f)
    hi, mid, lo = _split3(g)
    g3 = jnp.concatenate([hi, mid, lo], axis=1)
    b3 = _dot(tri01, g3)
    t3 = _dot(same01, g3)
    b = b3[:, :HG_DIM] + b3[:, HG_DIM:2 * HG_DIM] + b3[:, 2 * HG_DIM:]
    bt = t3[:, :HG_DIM] + t3[:, HG_DIM:2 * HG_DIM] + t3[:, 2 * HG_DIM:]
    return f, 1.0 - f, b, bt


def _hgrn_fwd(p_a, lbl):
    def body(p_ref, lbl_ref, o_ref, st_ref, qd_s, kd_s, kd2_s, v_s, ebt_s):
        for d in (0, 1):
            same, tri = _chunk_masks(d == 1)
            same01 = jnp.where(same, 1.0, 0.0).astype(BF16)
            tri01 = jnp.where(tri, 1.0, 0.0).astype(BF16)
            ll = lbl_ref[d]
            lb = _sigmoid(ll[0:1, :] - ll[1:2, :])

            def prep(r, carry):
                r0 = pl.multiple_of(r * TM, TM)
                z = p_ref[pl.ds(r0, TM), d * HG_DIM:(d + 1) * HG_DIM]
                _, k, b, bt = _decay_terms(z, lb, same01, tri01)
                kd2_s[pl.ds(r0, TM), :] = (k * jnp.exp(bt - b)).astype(BF16)
                ebt_s[pl.ds(r0, TM), :] = jnp.exp(bt)
                v_s[pl.ds(r0, TM), :] = p_ref[pl.ds(r0, TM), 2 * HG_DIM:3 * HG_DIM].astype(BF16)

                @pl.when(r >= 1)
                def _():
                    rl = pl.multiple_of(r0 - L, TM)
                    qr = p_ref[pl.ds(r0, TM), 3 * HG_DIM:4 * HG_DIM]
                    q = qr * _sigmoid(qr) * HG_DIM ** -0.5
                    qd_s[pl.ds(rl, TM), :] = (q * jnp.exp(b)).astype(BF16)
                    kd_s[pl.ds(rl, TM), :] = (k * jnp.exp(-b)).astype(BF16)

                return carry

            lax.fori_loop(0, N_TILES, prep, 0)

            def scan(i, st):
                nn = _chunk_order(i, d == 1)
                c0 = pl.multiple_of(nn * CHUNK, CHUNK)
                st_ref[d, nn] = st.astype(BF16)
                u = _dot_tn(v_s[pl.ds(c0, CHUNK), :], kd2_s[pl.ds(c0, CHUNK), :])
                return st * ebt_s[pl.ds(c0, 1), :] + u

            lax.fori_loop(0, N_CHUNKS, scan, jnp.zeros((HG_DIM, HG_DIM), F32))

            def outp(r, carry):
                r0 = pl.multiple_of(r * TM, TM)
                qd = qd_s[pl.ds(r0, TM), :]
                a = jnp.where(tri, _dot_nt(qd, kd_s[pl.ds(r0, TM), :]), 0.0)
                o = _dot(a.astype(BF16), v_s[pl.ds(r0 + L, TM), :])
                stb = st_ref[d, pl.ds(N_CTX_CHUNKS + r * (TM // CHUNK), TM // CHUNK)]
                inter = jnp.einsum('nck,nvk->ncv', qd.reshape(TM // CHUNK, CHUNK, HG_DIM), stb,
                                   preferred_element_type=F32)
                o = o + inter.reshape(TM, HG_DIM)
                if d == 0:
                    o_ref[pl.ds(r0, TM), :] = o
                else:
                    o_ref[pl.ds(r0, TM), :] += o
                return carry

            lax.fori_loop(0, N_LAT_TILES, outp, 0)

    return pl.pallas_call(
        body, name="hgrn_fwd", grid=(HG_HEADS,),
        in_specs=[pl.BlockSpec((T, 4 * HG_DIM), lambda h: (0, h)),
                  pl.BlockSpec((2, 2, HG_DIM), lambda h: (0, 0, h))],
        out_specs=[pl.BlockSpec((S, HG_DIM), lambda h: (0, h)),
                   pl.BlockSpec((2, None, N_CHUNKS, HG_DIM, HG_DIM), lambda h: (0, h, 0, 0, 0))],
        out_shape=[jax.ShapeDtypeStruct((S, HGW), F32),
                   jax.ShapeDtypeStruct((2, HG_HEADS, N_CHUNKS, HG_DIM, HG_DIM), BF16)],
        scratch_shapes=[pltpu.VMEM((S, HG_DIM), BF16), pltpu.VMEM((S, HG_DIM), BF16),
                        pltpu.VMEM((T, HG_DIM), BF16), pltpu.VMEM((T, HG_DIM), BF16),
                        pltpu.VMEM((T, HG_DIM), F32)],
        compiler_params=_cp(("parallel",)),
    )(p_a, lbl)


def _hgrn_bwd(p_a, lbl, d_o, st):
    cpt = TM // CHUNK

    def body(p_ref, lbl_ref, do_ref, st_ref, dp_ref, dlb_ref,
             b_s, bt_s, dbt_s, db_t, dk_t, dv_s, dq_s, qd_s, dst_s):
        dv_s[...] = jnp.zeros_like(dv_s)
        dq_s[...] = jnp.zeros_like(dq_s)
        for d in (0, 1):
            same, tri = _chunk_masks(d == 1)
            _, tri_rev = _chunk_masks(d == 0)
            same01 = jnp.where(same, 1.0, 0.0).astype(BF16)
            tri01 = jnp.where(tri, 1.0, 0.0).astype(BF16)
            later01 = jnp.where(tri_rev, 1.0, 0.0).astype(BF16)
            dbt_s[...] = jnp.zeros_like(dbt_s)
            ll = lbl_ref[d]
            lb = _sigmoid(ll[0:1, :] - ll[1:2, :])

            def prep(r, carry):
                r0 = pl.multiple_of(r * TM, TM)
                z = p_ref[pl.ds(r0, TM), d * HG_DIM:(d + 1) * HG_DIM]
                _, _, b, bt = _decay_terms(z, lb, same01, tri01)
                b_s[pl.ds(r0, TM), :] = b
                bt_s[pl.ds(r0, TM), :] = bt

                @pl.when(r >= 1)
                def _():
                    rl = pl.multiple_of(r0 - L, TM)
                    qr = p_ref[pl.ds(r0, TM), 3 * HG_DIM:4 * HG_DIM]
                    q = qr * _sigmoid(qr) * HG_DIM ** -0.5
                    qd_s[pl.ds(rl, TM), :] = (q * jnp.exp(b)).astype(BF16)

                return carry

            lax.fori_loop(0, N_TILES, prep, 0)

            def rscan(j, dst):
                i = N_CHUNKS - 1 - j
                nn = _chunk_order(i, d == 1)
                c0 = pl.multiple_of(nn * CHUNK, CHUNK)
                dst_s[nn] = dst.astype(BF16)
                after = st_ref[d, _chunk_order(jnp.minimum(i + 1, N_CHUNKS - 1), d == 1)].astype(F32)
                dbt_s[pl.ds(c0, 1), :] = jnp.sum(after * dst, axis=0, keepdims=True)
                decayed = dst * jnp.exp(bt_s[pl.ds(c0, 1), :])
                cl = pl.multiple_of(jnp.maximum(c0 - L, 0), CHUNK)
                w = _dot_tn(do_ref[pl.ds(cl, CHUNK), :].astype(BF16), qd_s[pl.ds(cl, CHUNK), :])
                return decayed + jnp.where(nn >= N_CTX_CHUNKS, w, 0.0)

            lax.fori_loop(0, N_CHUNKS, rscan, jnp.zeros((HG_DIM, HG_DIM), F32))

            def grads(r, dlb):
                r0 = pl.multiple_of(r * TM, TM)
                z = p_ref[pl.ds(r0, TM), d * HG_DIM:(d + 1) * HG_DIM]
                sz = _sigmoid(z)
                f = lb + (1.0 - lb) * sz
                k = 1.0 - f
                b = b_s[pl.ds(r0, TM), :]
                e2 = jnp.exp(bt_s[pl.ds(r0, TM), :] - b)
                vb = p_ref[pl.ds(r0, TM), 2 * HG_DIM:3 * HG_DIM].astype(BF16)
                dstb = dst_s[pl.ds(r * cpt, cpt)]
                kd2 = k * e2
                dkd2 = jnp.einsum('ncv,nvk->nck', vb.reshape(cpt, CHUNK, HG_DIM), dstb,
                                  preferred_element_type=F32).reshape(TM, HG_DIM)
                dv = jnp.einsum('nck,nvk->ncv', kd2.astype(BF16).reshape(cpt, CHUNK, HG_DIM), dstb,
                                preferred_element_type=F32).reshape(TM, HG_DIM)
                dk_t[...] = dkd2 * e2
                db_t[...] = -(kd2 * dkd2)
                dv_s[pl.ds(r0, TM), :] += dv

                @pl.when(r >= 1)
                def _():
                    rl = pl.multiple_of(r0 - L, TM)
                    eb = jnp.exp(b)
                    enb = jnp.exp(-b)
                    qr = p_ref[pl.ds(r0, TM), 3 * HG_DIM:4 * HG_DIM]
                    sq = _sigmoid(qr)
                    qdf = qr * sq * HG_DIM ** -0.5 * eb
                    kdf = k * enb
                    qd = qd_s[pl.ds(rl, TM), :]
                    kd = kdf.astype(BF16)
                    do = do_ref[pl.ds(rl, TM), :].astype(BF16)
                    a = jnp.where(tri, _dot_nt(qd, kd), 0.0).astype(BF16)
                    da = jnp.where(tri, _dot_nt(do, vb), 0.0).astype(BF16)
                    stb = st_ref[d, pl.ds(r * cpt, cpt)]
                    dqd = _dot(da, kd) + jnp.einsum(
                        'ncv,nvk->nck', do.reshape(cpt, CHUNK, HG_DIM), stb,
                        preferred_element_type=F32).reshape(TM, HG_DIM)
                    dkd = _dot_tn(da, qd)
                    dv_s[pl.ds(r0, TM), :] += _dot_tn(a, do)
                    dk_t[...] += dkd * enb
                    db_t[...] += qdf * dqd - kdf * dkd
                    dq_s[pl.ds(rl, TM), :] += dqd * eb * (HG_DIM ** -0.5) * (sq * (1.0 + qr * (1.0 - sq)))

                dg = _dot_exact_lhs01(later01, db_t[...]) + _dot_exact_lhs01(same01, dbt_s[pl.ds(r0, TM), :])
                df = dg / f - dk_t[...]
                dp_ref[pl.ds(r0, TM), d * HG_DIM:(d + 1) * HG_DIM] = (
                    df * (1.0 - lb) * sz * (1.0 - sz)).astype(BF16)
                return dlb + jnp.sum(df * (1.0 - sz), axis=0, keepdims=True)

            dlb_ref[pl.ds(d, 1), :] = lax.fori_loop(0, N_TILES, grads, jnp.zeros((1, HG_DIM), F32))

        dp_ref[:, 2 * HG_DIM:3 * HG_DIM] = dv_s[...].astype(BF16)
        dp_ref[pl.ds(0, L), 3 * HG_DIM:4 * HG_DIM] = jnp.zeros((L, HG_DIM), BF16)
        dp_ref[pl.ds(L, S), 3 * HG_DIM:4 * HG_DIM] = dq_s[...].astype(BF16)

    return pl.pallas_call(
        body, name="hgrn_bwd", grid=(HG_HEADS,),
        in_specs=[pl.BlockSpec((T, 4 * HG_DIM), lambda h: (0, h)),
                  pl.BlockSpec((2, 2, HG_DIM), lambda h: (0, 0, h)),
                  pl.BlockSpec((S, HG_DIM), lambda h: (0, h)),
                  pl.BlockSpec((2, None, N_CHUNKS, HG_DIM, HG_DIM), lambda h: (0, h, 0, 0, 0))],
        out_specs=[pl.BlockSpec((T, 4 * HG_DIM), lambda h: (0, h)),
                   pl.BlockSpec((2, HG_DIM), lambda h: (0, h))],
        out_shape=[jax.ShapeDtypeStruct((T, WA), BF16), jax.ShapeDtypeStruct((2, HGW), F32)],
        scratch_shapes=[pltpu.VMEM((T, HG_DIM), F32), pltpu.VMEM((T, HG_DIM), F32),
                        pltpu.VMEM((T, HG_DIM), F32), pltpu.VMEM((TM, HG_DIM), F32),
                        pltpu.VMEM((TM, HG_DIM), F32),
                        pltpu.VMEM((T, HG_DIM), F32), pltpu.VMEM((S, HG_DIM), F32),
                        pltpu.VMEM((S, HG_DIM), BF16),
                        pltpu.VMEM((N_CHUNKS, HG_DIM, HG_DIM), BF16)],
        compiler_params=_cp(("parallel",)),
    )(p_a, lbl, d_o, st)


def _rope_tables():
    t = np.arange(S)
    inv = ROPE_THETA ** (-np.arange(0, 32, 2, dtype=np.float64) / 32)
    lane = np.arange(64)
    pos = np.where(lane[None, :] < 32, (t // GRID_W)[:, None], (t % GRID_W)[:, None]).astype(np.float64)
    ang = pos * inv[(lane % 32) % 16][None, :]
    sign = np.where((lane % 32) < 16, -1.0, 1.0)[None, :]
    cos = np.tile(np.cos(ang), (1, 2)).astype(np.float32)
    sin = np.tile(np.sin(ang) * sign, (1, 2)).astype(np.float32)
    return jnp.asarray(cos), jnp.asarray(sin)


def _rope_partner(v):
    lane = lax.broadcasted_iota(jnp.int32, (1, 128), 1)
    first = (lane % 32) < 16
    slabs = []
    for j in range(v.shape[1] // 128):
        s = v[:, 128 * j:128 * (j + 1)]
        slabs.append(jnp.where(first, pltpu.roll(s, 112, 1), pltpu.roll(s, 16, 1)))
    return slabs[0] if len(slabs) == 1 else jnp.concatenate(slabs, axis=1)


def _group_ones(width, group):
    r = lax.broadcasted_iota(jnp.int32, (width, width), 0)
    c = lax.broadcasted_iota(jnp.int32, (width, width), 1)
    return jnp.where((r // group) == (c // group), 1.0, 0.0).astype(BF16)


def _group_mean(v, ones01, group):
    hi = v.astype(BF16)
    lo = (v - hi.astype(F32)).astype(BF16)
    return (_dot(hi, ones01) + _dot(lo, ones01)) * (1.0 / group)


def _rep_matrix():
    r = lax.broadcasted_iota(jnp.int32, (KVW, ATW), 0)
    c = lax.broadcasted_iota(jnp.int32, (KVW, ATW), 1)
    return jnp.where(r == HEAD_DIM * (c // 256) + c % HEAD_DIM, 1.0, 0.0).astype(BF16)


def _tile_lanes(v, reps):
    return jnp.concatenate([v] * reps, axis=1)


def _prep_fwd(p_b, o, cos, sin, hnw, qnw, knw):
    def body(p_ref, o_ref, cos_ref, sin_ref, hnw_ref, qnw_ref, knw_ref, y_ref, q_ref, k_ref, v_ref):
        i = pl.program_id(0)
        rep = _rep_matrix()
        ones_k = _group_ones(KVW, HEAD_DIM)
        kr = p_ref[:, 1024:1152]
        krstd = lax.rsqrt(_group_mean(kr * kr, ones_k, HEAD_DIM) + EPS)
        kn = kr * krstd * knw_ref[...]
        v_ref[...] = _dot(p_ref[:, 1152:1280].astype(BF16), rep).astype(BF16)

        @pl.when(i == 0)
        def _():
            k_ref[...] = _dot(kn.astype(BF16), rep).astype(BF16)

        @pl.when(i > 0)
        def _():
            cs, sn = cos_ref[...], sin_ref[...]
            kro = kn * cs + _rope_partner(kn) * sn
            k_ref[...] = _dot(kro.astype(BF16), rep).astype(BF16)
            qr = p_ref[:, 512:1024]
            qrstd = lax.rsqrt(_group_mean(qr * qr, _group_ones(ATW, HEAD_DIM), HEAD_DIM) + EPS)
            qn = qr * qrstd * qnw_ref[...]
            qro = qn * _tile_lanes(cs, 4) + _rope_partner(qn) * _tile_lanes(sn, 4)
            q_ref[...] = (qro * HEAD_DIM ** -0.5).astype(BF16)
            ys = []
            for h in range(HG_HEADS):
                oh = o_ref[:, HG_DIM * h:HG_DIM * (h + 1)]
                gh = p_ref[:, HG_DIM * h:HG_DIM * (h + 1)]
                rstd = lax.rsqrt(jnp.mean(oh * oh, axis=-1, keepdims=True) + EPS)
                ys.append(oh * rstd * hnw_ref[...] * (gh * _sigmoid(gh)))
            y_ref[...] = jnp.concatenate(ys, axis=1).astype(BF16)

    return pl.pallas_call(
        body, name="prep_fwd", grid=(N_TILES,),
        in_specs=[pl.BlockSpec((TM, WB), lambda i: (i, 0)),
                  pl.BlockSpec((TM, HGW), lambda i: (_lat(i), 0)),
                  pl.BlockSpec((TM, 128), lambda i: (_lat(i), 0)),
                  pl.BlockSpec((TM, 128), lambda i: (_lat(i), 0)),
                  _full((1, HG_DIM)), _full((1, ATW)), _full((1, KVW))],
        out_specs=[pl.BlockSpec((TM, HGW), lambda i: (_lat(i), 0)),
                   pl.BlockSpec((TM, ATW), lambda i: (_lat(i), 0)),
                   pl.BlockSpec((TM, ATW), lambda i: (i, 0)),
                   pl.BlockSpec((TM, ATW), lambda i: (i, 0))],
        out_shape=[jax.ShapeDtypeStruct((S, HGW), BF16), jax.ShapeDtypeStruct((S, ATW), BF16),
                   jax.ShapeDtypeStruct((T, ATW), BF16), jax.ShapeDtypeStruct((T, ATW), BF16)],
        compiler_params=_cp(("arbitrary",)),
    )(p_b, o, cos, sin, hnw, qnw, knw)


def _prep_bwd(p_b, o, cos, sin, hnw, qnw, knw, dy_hg, dq, dk_rep, dv_rep):
    def body(p_ref, o_ref, cos_ref, sin_ref, hnw_ref, qnw_ref, knw_ref, dy_ref, dq_ref, dk_ref, dv_ref,
             dp_ref, do_ref, acc_ref):
        i = pl.program_id(0)

        @pl.when(i == 0)
        def _():
            acc_ref[...] = jnp.zeros_like(acc_ref)

        rep = _rep_matrix()
        ones_k = _group_ones(KVW, HEAD_DIM)

        def fold(v):
            hi = v.astype(BF16)
            lo = (v - hi.astype(F32)).astype(BF16)
            return _dot_nt(hi, rep) + _dot_nt(lo, rep)

        kr = p_ref[:, 1024:1152]
        krstd = lax.rsqrt(_group_mean(kr * kr, ones_k, HEAD_DIM) + EPS)
        khat = kr * krstd
        kw = knw_ref[...]
        dkro = fold(dk_ref[...])
        dv = fold(dv_ref[...])

        def k_back(dkn):
            dkhat = dkn * kw
            dkr = krstd * (dkhat - khat * _group_mean(dkhat * khat, ones_k, HEAD_DIM))
            acc_ref[2:3, 0:KVW] += jnp.sum(dkn * khat, axis=0, keepdims=True)
            dp_ref[:, 1024:1152] = dkr.astype(BF16)
            dp_ref[:, 1152:1280] = dv.astype(BF16)

        @pl.when(i == 0)
        def _():
            k_back(dkro)
            dp_ref[:, 0:1024] = jnp.zeros((TM, 1024), BF16)

        @pl.when(i > 0)
        def _():
            cs, sn = cos_ref[...], sin_ref[...]
            k_back(dkro * cs + _rope_partner(dkro * sn))
            ones_q = _group_ones(ATW, HEAD_DIM)
            qr = p_ref[:, 512:1024]
            qrstd = lax.rsqrt(_group_mean(qr * qr, ones_q, HEAD_DIM) + EPS)
            qhat = qr * qrstd
            dqro = dq_ref[...] * HEAD_DIM ** -0.5
            dqn = dqro * _tile_lanes(cs, 4) + _rope_partner(dqro * _tile_lanes(sn, 4))
            dqhat = dqn * qnw_ref[...]
            dqr = qrstd * (dqhat - qhat * _group_mean(dqhat * qhat, ones_q, HEAD_DIM))
            acc_ref[1:2, :] += jnp.sum(dqn * qhat, axis=0, keepdims=True)
            dp_ref[:, 512:1024] = dqr.astype(BF16)
            dws = jnp.zeros((1, HG_DIM), F32)
            for h in range(HG_HEADS):
                sl = slice(HG_DIM * h, HG_DIM * (h + 1))
                oh, gh, dy = o_ref[:, sl], p_ref[:, sl], dy_ref[:, sl]
                rstd = lax.rsqrt(jnp.mean(oh * oh, axis=-1, keepdims=True) + EPS)
                ohat = oh * rstd
                sg = _sigmoid(gh)
                dp_ref[:, sl] = (dy * (ohat * hnw_ref[...]) * (sg * (1.0 + gh * (1.0 - sg)))).astype(BF16)
                dn = dy * (gh * sg)
                dws = dws + jnp.sum(dn * ohat, axis=0, keepdims=True)
                dohat = dn * hnw_ref[...]
                do_ref[:, sl] = rstd * (dohat - ohat * jnp.mean(dohat * ohat, axis=-1, keepdims=True))
            acc_ref[0:1, 0:HG_DIM] += dws

    return pl.pallas_call(
        body, name="prep_bwd", grid=(N_TILES,),
        in_specs=[pl.BlockSpec((TM, WB), lambda i: (i, 0)),
                  pl.BlockSpec((TM, HGW), lambda i: (_lat(i), 0)),
                  pl.BlockSpec((TM, 128), lambda i: (_lat(i), 0)),
                  pl.BlockSpec((TM, 128), lambda i: (_lat(i), 0)),
                  _full((1, HG_DIM)), _full((1, ATW)), _full((1, KVW)),
                  pl.BlockSpec((TM, HGW), lambda i: (_lat(i), 0)),
                  pl.BlockSpec((TM, ATW), lambda i: (_lat(i), 0)),
                  pl.BlockSpec((TM, ATW), lambda i: (i, 0)),
                  pl.BlockSpec((TM, ATW), lambda i: (i, 0))],
        out_specs=[pl.BlockSpec((TM, WB), lambda i: (i, 0)),
                   pl.BlockSpec((TM, HGW), lambda i: (_lat(i), 0)),
                   _full((8, ATW))],
        out_shape=[jax.ShapeDtypeStruct((T, WB), BF16), jax.ShapeDtypeStruct((S, HGW), F32),
                   jax.ShapeDtypeStruct((8, ATW), F32)],
        compiler_params=_cp(("arbitrary",)),
    )(p_b, o, cos, sin, hnw, qnw, knw, dy_hg, dq, dk_rep, dv_rep)


NEG = -1e30
_CTX_BLOCKS = L // BLOCK


def _attn_window_specs():
    prev = pl.BlockSpec((BLOCK, ATW), lambda i: (jnp.maximum(i - 1, 0) + _CTX_BLOCKS, 0))
    own = pl.BlockSpec((BLOCK, ATW), lambda i: (i + _CTX_BLOCKS, 0))
    nxt = pl.BlockSpec((BLOCK, ATW), lambda i: (jnp.minimum(i + 1, N_BLOCKS - 1) + _CTX_BLOCKS, 0))
    return [prev, own, nxt, _full((L, ATW))]


def _attn_valid(i):
    qi = lax.broadcasted_iota(jnp.int32, (4 * BLOCK, 3 * BLOCK), 0) % BLOCK
    kj = lax.broadcasted_iota(jnp.int32, (4 * BLOCK, 3 * BLOCK), 1)
    return ((jnp.abs(kj - BLOCK - qi) <= BLOCK) & ((kj >= BLOCK) | (i > 0))
            & ((kj < 2 * BLOCK) | (i < N_BLOCKS - 1)))


def _stack_heads(qg):
    lane = lax.broadcasted_iota(jnp.int32, (1, 256), 1) // HEAD_DIM
    return jnp.concatenate([jnp.where(lane == g, qg, jnp.zeros_like(qg)) for g in range(4)], axis=0)


def _unstack_heads(v4):
    lane = lax.broadcasted_iota(jnp.int32, (1, 256), 1) // HEAD_DIM
    out = jnp.where(lane == 0, v4[0:BLOCK], 0.0)
    for g in range(1, 4):
        out = out + jnp.where(lane == g, v4[g * BLOCK:(g + 1) * BLOCK], 0.0)
    return out


def _sink_rows(sink_ref, hk):
    return jnp.concatenate(
        [jnp.broadcast_to(sink_ref[0:1, 4 * hk + g:4 * hk + g + 1], (BLOCK, 1)) for g in range(4)], axis=0)


def _attn_fwd(q, k_rep, v_rep, sinks):
    def body(q_ref, kp, ko, kn, kc, vp, vo, vn, vc, sink_ref, y_ref, lse_ref):
        i = pl.program_id(0)
        valid = _attn_valid(i)
        lane8 = lax.broadcasted_iota(jnp.int32, (1, ATT_HEADS), 1)
        lse_out = jnp.zeros((BLOCK, ATT_HEADS), F32)
        for hk in range(KV_HEADS):
            sl = slice(256 * hk, 256 * (hk + 1))
            q4 = _stack_heads(q_ref[:, sl])
            kl = jnp.concatenate([kp[:, sl], ko[:, sl], kn[:, sl]], axis=0)
            vl = jnp.concatenate([vp[:, sl], vo[:, sl], vn[:, sl]], axis=0)
            s_loc = jnp.where(valid, _dot_nt(q4, kl), NEG)
            s_ctx = _dot_nt(q4, kc[:, sl])
            sink = _sink_rows(sink_ref, hk)
            m = jnp.maximum(jnp.maximum(jnp.max(s_loc, axis=1, keepdims=True),
                                        jnp.max(s_ctx, axis=1, keepdims=True)), sink)
            p_loc = jnp.exp(s_loc - m)
            p_ctx = jnp.exp(s_ctx - m)
            den = (jnp.sum(p_loc, axis=1, keepdims=True) + jnp.sum(p_ctx, axis=1, keepdims=True)
                   + jnp.exp(sink - m))
            o4 = (_dot(p_loc.astype(BF16), vl) + _dot(p_ctx.astype(BF16), vc[:, sl])) / den
            y_ref[:, sl] = _unstack_heads(o4).astype(BF16)
            lse4 = m + jnp.log(den)
            for g in range(4):
                lse_out = lse_out + jnp.where(lane8 == 4 * hk + g, lse4[g * BLOCK:(g + 1) * BLOCK], 0.0)
        lse_ref[...] = lse_out

    return pl.pallas_call(
        body, name="attn_fwd", grid=(N_BLOCKS,),
        in_specs=[pl.BlockSpec((BLOCK, ATW), lambda i: (i, 0))] + _attn_window_specs()
        + _attn_window_specs() + [_full((1, ATT_HEADS))],
        out_specs=[pl.BlockSpec((BLOCK, ATW), lambda i: (i, 0)),
                   pl.BlockSpec((BLOCK, ATT_HEADS), lambda i: (i, 0))],
        out_shape=[jax.ShapeDtypeStruct((S, ATW), BF16), jax.ShapeDtypeStruct((S, ATT_HEADS), F32)],
        compiler_params=_cp(("parallel",)),
    )(q, k_rep, k_rep, k_rep, k_rep, v_rep, v_rep, v_rep, v_rep, sinks)


def _attn_bwd(q, k_rep, v_rep, sinks, y_at, lse, dy):
    def body(q_ref, kp, ko, kn, kc, vp, vo, vn, vc, sink_ref, y_ref, lse_ref, dy_ref,
             dq_ref, dk_ref, dv_ref, dsink_ref, dk_acc, dv_acc):
        i = pl.program_id(0)

        @pl.when(i == 0)
        def _():
            dk_acc[...] = jnp.zeros_like(dk_acc)
            dv_acc[...] = jnp.zeros_like(dv_acc)
            dk_ref[pl.ds(0, L), :] = jnp.zeros((L, ATW), F32)
            dv_ref[pl.ds(0, L), :] = jnp.zeros((L, ATW), F32)
            dsink_ref[...] = jnp.zeros_like(dsink_ref)

        valid = _attn_valid(i)
        lane8 = lax.broadcasted_iota(jnp.int32, (1, ATT_HEADS), 1)
        w0 = pl.multiple_of(i * BLOCK, BLOCK)
        dsink = jnp.zeros((1, ATT_HEADS), F32)
        for hk in range(KV_HEADS):
            sl = slice(256 * hk, 256 * (hk + 1))
            q4 = _stack_heads(q_ref[:, sl])
            do4f = _stack_heads(dy_ref[:, sl])
            o4 = _stack_heads(y_ref[:, sl]).astype(F32)
            do4 = do4f.astype(BF16)
            kl = jnp.concatenate([kp[:, sl], ko[:, sl], kn[:, sl]], axis=0)
            vl = jnp.concatenate([vp[:, sl], vo[:, sl], vn[:, sl]], axis=0)
            lse4 = jnp.concatenate(
                [jnp.sum(jnp.where(lane8 == 4 * hk + g, lse_ref[...], 0.0), axis=1, keepdims=True)
                 for g in range(4)], axis=0)
            p_loc = jnp.where(valid, jnp.exp(_dot_nt(q4, kl) - lse4), 0.0)
            p_ctx = jnp.exp(_dot_nt(q4, kc[:, sl]) - lse4)
            delta = jnp.sum(do4f * o4, axis=1, keepdims=True)
            ds_loc = (p_loc * (_dot_nt(do4, vl) - delta)).astype(BF16)
            ds_ctx = (p_ctx * (_dot_nt(do4, vc[:, sl]) - delta)).astype(BF16)
            dq_ref[:, sl] = _unstack_heads(_dot(ds_loc, kl) + _dot(ds_ctx, kc[:, sl]))
            dk_acc[pl.ds(w0, 3 * BLOCK), sl] += _dot_tn(ds_loc, q4)
            dv_acc[pl.ds(w0, 3 * BLOCK), sl] += _dot_tn(p_loc.astype(BF16), do4)
            dk_ref[pl.ds(0, L), sl] += _dot_tn(ds_ctx, q4)
            dv_ref[pl.ds(0, L), sl] += _dot_tn(p_ctx.astype(BF16), do4)
            p_sink = jnp.exp(_sink_rows(sink_ref, hk) - lse4)
            for g in range(4):
                rows = slice(g * BLOCK, (g + 1) * BLOCK)
                dsink = dsink + jnp.where(lane8 == 4 * hk + g,
                                          -jnp.sum(p_sink[rows] * delta[rows], axis=0, keepdims=True), 0.0)
        dsink_ref[...] += dsink

        @pl.when(i == N_BLOCKS - 1)
        def _():
            dk_ref[pl.ds(L, S), :] = dk_acc[pl.ds(BLOCK, S), :]
            dv_ref[pl.ds(L, S), :] = dv_acc[pl.ds(BLOCK, S), :]

    row_q = pl.BlockSpec((BLOCK, ATW), lambda i: (i, 0))
    return pl.pallas_call(
        body, name="attn_bwd", grid=(N_BLOCKS,),
        in_specs=[row_q] + _attn_window_specs() + _attn_window_specs()
        + [_full((1, ATT_HEADS)), row_q, pl.BlockSpec((BLOCK, ATT_HEADS), lambda i: (i, 0)), row_q],
        out_specs=[row_q, _full((T, ATW)), _full((T, ATW)), _full((1, ATT_HEADS))],
        out_shape=[jax.ShapeDtypeStruct((S, ATW), F32), jax.ShapeDtypeStruct((T, ATW), F32),
                   jax.ShapeDtypeStruct((T, ATW), F32), jax.ShapeDtypeStruct((1, ATT_HEADS), F32)],
        scratch_shapes=[pltpu.VMEM((S + 2 * BLOCK, ATW), F32), pltpu.VMEM((S + 2 * BLOCK, ATW), F32)],
        compiler_params=_cp(("arbitrary",)),
    )(q, k_rep, k_rep, k_rep, k_rep, v_rep, v_rep, v_rep, v_rep, sinks, y_at, lse, dy)


def _merge_fwd(y_hg, y_at, p_c, x, w_bh, w_ba, w_out, g1, nfw, sh2, sc2):
    def body(yh_ref, ya_ref, g_ref, x_ref, wbh_ref, wba_ref, wo_ref, g1_ref, nfw_ref, sh_ref, sc_ref,
             a_ref, b_ref, mx_ref, r_ref, x1_ref, h2_ref):
        a = _dot(yh_ref[...], wbh_ref[...])
        b = _dot(ya_ref[...], wba_ref[...])
        mixed = (_sigmoid(g_ref[:, :D]) * a + _sigmoid(g_ref[:, D:]) * b).astype(BF16)
        r = _dot(mixed, wo_ref[...])
        x1 = x_ref[...] + g1_ref[...] * r
        a_ref[...] = a
        b_ref[...] = b
        mx_ref[...] = mixed
        r_ref[...] = r
        x1_ref[...] = x1
        h2_ref[...] = _rms_mod(x1, nfw_ref[...], sh_ref[...], sc_ref[...]).astype(BF16)

    row = lambda w: pl.BlockSpec((TM, w), lambda i: (i, 0))
    vec = _full((1, D))
    return pl.pallas_call(
        body, name="merge_fwd", grid=(N_LAT_TILES,),
        in_specs=[row(HGW), row(ATW), row(WC), row(D), _VMEM_WHOLE, _VMEM_WHOLE, _VMEM_WHOLE,
                  vec, vec, vec, vec],
        out_specs=[row(D)] * 6,
        out_shape=[jax.ShapeDtypeStruct((S, D), dt) for dt in (F32, F32, BF16, F32, F32, BF16)],
        compiler_params=_cp(("parallel",)),
    )(y_hg, y_at, p_c, x, w_bh, w_ba, w_out, g1, nfw, sh2, sc2)


def _merge_bwd(dx1, r, a, b, p_c, w_bh, w_ba, w_out, g1):
    def body(dx_ref, r_ref, a_ref, b_ref, g_ref, wbh_ref, wba_ref, wo_ref, g1_ref,
             dr_ref, da_ref, db_ref, dg_ref, dyh_ref, dya_ref, acc_ref):
        @pl.when(pl.program_id(0) == 0)
        def _():
            acc_ref[...] = jnp.zeros_like(acc_ref)

        dx1v = dx_ref[...]
        acc_ref[0:1, :] += jnp.sum(dx1v * r_ref[...], axis=0, keepdims=True)
        dr = (g1_ref[...] * dx1v).astype(BF16)
        dr_ref[...] = dr
        dmix = _dot_nt(dr, wo_ref[...])
        sh, sa = _sigmoid(g_ref[:, :D]), _sigmoid(g_ref[:, D:])
        da = (dmix * sh).astype(BF16)
        db = (dmix * sa).astype(BF16)
        da_ref[...] = da
        db_ref[...] = db
        dg_ref[:, :D] = (dmix * a_ref[...] * sh * (1.0 - sh)).astype(BF16)
        dg_ref[:, D:] = (dmix * b_ref[...] * sa * (1.0 - sa)).astype(BF16)
        dyh_ref[...] = _dot_nt(da, wbh_ref[...])
        dya_ref[...] = _dot_nt(db, wba_ref[...])

    row = lambda w: pl.BlockSpec((TM, w), lambda i: (i, 0))
    return pl.pallas_call(
        body, name="merge_bwd", grid=(N_LAT_TILES,),
        in_specs=[row(D), row(D), row(D), row(D), row(WC), _VMEM_WHOLE, _VMEM_WHOLE, _VMEM_WHOLE,
                  _full((1, D))],
        out_specs=[row(D), row(D), row(D), row(WC), row(HGW), row(ATW), _full((8, D))],
        out_shape=[jax.ShapeDtypeStruct((S, D), BF16), jax.ShapeDtypeStruct((S, D), BF16),
                   jax.ShapeDtypeStruct((S, D), BF16), jax.ShapeDtypeStruct((S, WC), BF16),
                   jax.ShapeDtypeStruct((S, HGW), F32), jax.ShapeDtypeStruct((S, ATW), F32),
                   jax.ShapeDtypeStruct((8, D), F32)],
        compiler_params=_cp(("arbitrary",)),
    )(dx1, r, a, b, p_c, w_bh, w_ba, w_out, g1)


def _ffn_fused(x1, h2, tgt, w_gate, w_up, w_down, g2, nfw, sc2):
    def body(x1_ref, h2_ref, t_ref, wg_ref, wu_ref, wd_ref, g2_ref, nfw_ref, sc_ref,
             act_ref, dgt_ref, dup_ref, df_ref, dx_ref, acc_ref, gs, us):
        @pl.when(pl.program_id(0) == 0)
        def _():
            acc_ref[...] = jnp.zeros_like(acc_ref)

        h2 = h2_ref[...]
        f = jnp.zeros((TM, D), F32)
        for j in range(N_DEV):
            g = _dot(h2, wg_ref[j])
            u = _dot(h2, wu_ref[j])
            gs[j] = g
            us[j] = u
            act = (g * _sigmoid(g) * u).astype(BF16)
            act_ref[j] = act
            f = f + _dot(act, wd_ref[j])
        x1v = x1_ref[...]
        g2 = g2_ref[...]
        diff = x1v + g2 * f - t_ref[...]
        dy = diff * (1.0 / D)
        df = (g2 * dy).astype(BF16)
        df_ref[...] = df
        dh2 = jnp.zeros((TM, D), F32)
        for j in range(N_DEV):
            g, u = gs[j], us[j]
            sg = _sigmoid(g)
            dact = _dot_nt(df, wd_ref[j])
            dgate = (dact * u * (sg * (1.0 + g * (1.0 - sg)))).astype(BF16)
            dup = (dact * (g * sg)).astype(BF16)
            dgt_ref[j] = dgate
            dup_ref[j] = dup
            dh2 = dh2 + _dot_nt(dgate, wg_ref[j]) + _dot_nt(dup, wu_ref[j])
        dx, dsh, dsc, dnw = _rms_mod_bwd(x1v, nfw_ref[...], sc_ref[...], dh2)
        dx_ref[...] = dy + dx
        acc_ref[0:1, :] += dsh
        acc_ref[1:2, :] += dsc
        acc_ref[2:3, :] += dnw
        acc_ref[3:4, :] += jnp.sum(dy * f, axis=0, keepdims=True)
        acc_ref[4:5, :] += 0.5 * jnp.sum(jnp.sum(diff * diff, axis=1, keepdims=True), axis=0,
                                         keepdims=True) * (1.0 / D)

    row = lambda dt_w: pl.BlockSpec((TM, dt_w), lambda i: (i, 0))
    blk = pl.BlockSpec((N_DEV, TM, FF_BLK), lambda i: (0, i, 0))
    vec = _full((1, D))
    return pl.pallas_call(
        body, name="ffn_fused", grid=(N_LAT_TILES,),
        in_specs=[row(D), row(D), row(D), _VMEM_WHOLE, _VMEM_WHOLE, _VMEM_WHOLE, vec, vec, vec],
        out_specs=[blk, blk, blk, row(D), row(D), _full((8, D))],
        out_shape=[jax.ShapeDtypeStruct((N_DEV, S, FF_BLK), BF16)] * 3
        + [jax.ShapeDtypeStruct((S, D), BF16), jax.ShapeDtypeStruct((S, D), F32),
           jax.ShapeDtypeStruct((8, D), F32)],
        scratch_shapes=[pltpu.VMEM((N_DEV, TM, FF_BLK), F32), pltpu.VMEM((N_DEV, TM, FF_BLK), F32)],
        compiler_params=_cp(("arbitrary",)),
    )(x1, h2, tgt, w_gate, w_up, w_down, g2, nfw, sc2)


def _input_bwd(dp_a, dp_b, dp_c, w_a, w_b, w_c, ctx, x, dx1, nw, sh, sc):
    def body(da_ref, db_ref, dc_ref, wa_ref, wb_ref, wc_ref, ctx_ref, x_ref, dx1_ref, nw_ref, sh_ref,
             sc_ref, gx_ref, acc_ref):
        i = pl.program_id(0)

        @pl.when(i == 0)
        def _():
            acc_ref[...] = jnp.zeros_like(acc_ref)

        dh = _dot_nt(da_ref[...], wa_ref[...]) + _dot_nt(db_ref[...], wb_ref[...])

        @pl.when(i == 0)
        def _():
            _, dsh, dsc, dnw = _rms_mod_bwd(ctx_ref[...], nw_ref[...], sc_ref[0:1, :], dh)
            acc_ref[3:4, :] += dsh
            acc_ref[4:5, :] += dsc
            acc_ref[2:3, :] += dnw

        @pl.when(i > 0)
        def _():
            dhl = dh + _dot_nt(dc_ref[...], wc_ref[...])
            dx, dsh, dsc, dnw = _rms_mod_bwd(x_ref[...], nw_ref[...], sc_ref[1:2, :], dhl)
            gx_ref[...] = dx1_ref[...] + dx
            acc_ref[0:1, :] += dsh
            acc_ref[1:2, :] += dsc
            acc_ref[2:3, :] += dnw

    lat = lambda w: pl.BlockSpec((TM, w), lambda i: (_lat(i), 0))
    return pl.pallas_call(
        body, name="input_bwd", grid=(N_TILES,),
        in_specs=[pl.BlockSpec((TM, WA), lambda i: (i, 0)), pl.BlockSpec((TM, WB), lambda i: (i, 0)),
                  lat(WC), _VMEM_WHOLE, _VMEM_WHOLE, _VMEM_WHOLE, _full((TM, D)), lat(D), lat(D),
                  _full((1, D)), _full((2, D)), _full((2, D))],
        out_specs=[lat(D), _full((8, D))],
        out_shape=[jax.ShapeDtypeStruct((S, D), F32), jax.ShapeDtypeStruct((8, D), F32)],
        compiler_params=_cp(("arbitrary",)),
    )(dp_a, dp_b, dp_c, w_a, w_b, w_c, ctx, x, dx1, nw, sh, sc)


_C1 = 1.0 - ADAM_B1 ** ADAM_STEP
_C2 = 1.0 - ADAM_B2 ** ADAM_STEP


def _adamw_math(w, g, m, v):
    m = ADAM_B1 * m + (1.0 - ADAM_B1) * g
    v = ADAM_B2 * v + (1.0 - ADAM_B2) * (g * g)
    m_hat = m / _C1
    v_hat = v / _C2
    delta = -ADAM_LR * (m_hat / (jnp.sqrt(v_hat) + ADAM_EPS) + ADAM_WD * w)
    return delta, m, v


def _adamw_sharded(terms, w, m, v, name, tr):
    rows, cols = w.shape

    def body(t_ref, w_ref, m_ref, v_ref, g_ref, d_ref, nm_ref, nv_ref):
        g = t_ref[0].astype(F32)
        for s in range(1, N_DEV):
            g = g + t_ref[s].astype(F32)
        g_ref[...] = g
        d_ref[...], nm_ref[...], nv_ref[...] = _adamw_math(w_ref[...], g, m_ref[...], v_ref[...])

    blk = pl.BlockSpec((tr, cols), lambda i: (i, 0))
    return pl.pallas_call(
        body, name=name, grid=(rows // tr,),
        in_specs=[pl.BlockSpec((N_DEV, tr, cols), lambda i: (0, i, 0)), blk, blk, blk],
        out_specs=[blk] * 4,
        out_shape=[jax.ShapeDtypeStruct((rows, cols), F32)] * 4,
        compiler_params=_cp(("parallel",)),
    )(terms, w, m, v)


def _adamw_plain(g, w, m, v, name):
    def body(g_ref, w_ref, m_ref, v_ref, d_ref, nm_ref, nv_ref):
        d_ref[...], nm_ref[...], nv_ref[...] = _adamw_math(w_ref[...], g_ref[...], m_ref[...], v_ref[...])

    return pl.pallas_call(
        body, name=name, in_specs=[_VMEM_WHOLE] * 4, out_specs=[_VMEM_WHOLE] * 3,
        out_shape=[jax.ShapeDtypeStruct(w.shape, F32)] * 3,
        compiler_params=_cp(),
    )(g, w, m, v)


SMALL_ROWS = 16
R_DMOD, R_DCTX, R_NMIX, R_NFFN, R_MISC, R_DLB, R_BADA01 = 0, 6, 8, 9, 10, 11, 13
M_HNW, M_QNW, M_KNW, M_SINK, M_LOSS = 0, 128, 256, 384, 512


def _pack_small(acc_in, acc_mg, acc_ffn, acc_prep, dsink, dlb):
    def body(in_ref, mg_ref, ff_ref, pp_ref, ds_ref, dlb_ref, o_ref):
        o_ref[...] = jnp.zeros_like(o_ref)
        o_ref[0:2, :] = in_ref[0:2, :]
        o_ref[2:3, :] = mg_ref[0:1, :]
        o_ref[3:5, :] = ff_ref[0:2, :]
        o_ref[5:6, :] = ff_ref[3:4, :]
        o_ref[6:8, :] = in_ref[3:5, :]
        o_ref[8:9, :] = in_ref[2:3, :]
        o_ref[9:10, :] = ff_ref[2:3, :]
        o_ref[10:11, M_HNW:M_HNW + HG_DIM] = pp_ref[0:1, 0:HG_DIM]
        r = lax.broadcasted_iota(jnp.int32, (ATW, 128), 0)
        c = lax.broadcasted_iota(jnp.int32, (ATW, 128), 1)
        fold = jnp.where((r % HEAD_DIM == c) & (c < HEAD_DIM), 1.0, 0.0).astype(BF16)
        qk = jnp.concatenate([pp_ref[1:2, :], pp_ref[2:3, :], jnp.zeros((6, ATW), F32)], axis=0)
        folded = _dot_exact_rhs01(qk, fold)
        o_ref[10:11, M_QNW:M_QNW + 128] = folded[0:1, :]
        o_ref[10:11, M_KNW:M_KNW + 128] = folded[1:2, :]
        o_ref[10:11, M_SINK:M_SINK + ATT_HEADS] = ds_ref[...]
        o_ref[10:11, M_LOSS:M_LOSS + 128] = ff_ref[4:5, 0:128]
        o_ref[11:13, 0:HGW] = dlb_ref[...]

    return pl.pallas_call(
        body, name="pack_small", in_specs=[_VMEM_WHOLE] * 6, out_specs=_VMEM_WHOLE,
        out_shape=jax.ShapeDtypeStruct((SMALL_ROWS, D), F32), compiler_params=_cp(),
    )(acc_in, acc_mg, acc_ffn, acc_prep, dsink, dlb)


def _sum_small(gathered):
    def body(g_ref, o_ref):
        tot = g_ref[0]
        for s in range(1, N_DEV):
            tot = tot + g_ref[s]
        o_ref[...] = tot
        o_ref[R_BADA01:R_BADA01 + 2, :] = tot[0:2, :] + tot[R_DCTX:R_DCTX + 2, :]

    return pl.pallas_call(
        body, name="sum_small", in_specs=[_VMEM_WHOLE], out_specs=_VMEM_WHOLE,
        out_shape=jax.ShapeDtypeStruct((SMALL_ROWS, D), F32), compiler_params=_cp(),
    )(gathered)


def _lb_grads(dlb, lbl):
    def body(d_ref, l_ref, o_ref):
        for d in (0, 1):
            ll = l_ref[d]
            lb = _sigmoid(ll[0:1, :] - ll[1:2, :])
            t = d_ref[d:d + 1, :] * lb * (1.0 - lb)
            o_ref[d, 0:1, :] = t
            o_ref[d, 1:2, :] = -t

    return pl.pallas_call(
        body, name="lb_grads", in_specs=[_VMEM_WHOLE] * 2, out_specs=_VMEM_WHOLE,
        out_shape=jax.ShapeDtypeStruct((2, 2, HGW), F32), compiler_params=_cp(),
    )(dlb, lbl)


def _c_ctx_grad(terms, c_ctx):
    def body(t_ref, c_ref, o_ref):
        tot = t_ref[0, 8:9, :]
        for s in range(1, N_DEV):
            tot = tot + t_ref[s, 8:9, :]
        cv = c_ref[...]
        sg = _sigmoid(cv)
        o_ref[...] = tot * (sg * (1.0 + cv * (1.0 - sg)))

    return pl.pallas_call(
        body, name="c_ctx_grad", in_specs=[_VMEM_WHOLE] * 2, out_specs=_VMEM_WHOLE,
        out_shape=jax.ShapeDtypeStruct((1, D), F32), compiler_params=_cp(),
    )(terms, c_ctx)


def _in_perm():
    fz, bz, inp, kk, vv, qhg, ghg, qat, gates = 0, 512, 1024, 1536, 1664, 1792, 2304, 2816, 3328
    cols = []
    for h in range(HG_HEADS):
        for base in (fz, bz, inp, qhg):
            cols += list(range(base + 128 * h, base + 128 * (h + 1)))
    cols += list(range(ghg, ghg + 512)) + list(range(qat, qat + 512))
    cols += list(range(kk, kk + 128)) + list(range(vv, vv + 128))
    cols += list(range(gates, gates + 2048))
    return np.asarray(cols, np.int32)


_PERM = _in_perm()
_INV_PERM = np.argsort(_PERM).astype(np.int32)


def _cols_from_blocks(g):
    return jnp.transpose(g, (1, 0, 2)).reshape(g.shape[1], N_DEV * g.shape[2])


def _blocks_from_cols(w):
    r, c = w.shape
    return jnp.transpose(w.reshape(r, N_DEV, c // N_DEV), (1, 0, 2))


def _local_step(x2, ctx2, tgt, lbl, sh_in, sc_in, gate1, sh2, sc2, gate2, norm_mix_w, norm_ffn_w,
                hgrn_norm_w, q_norm_w, k_norm_w, attn_sinks, w_a, w_b, w_c, w_bh, w_ba, w_o,
                g_gate, g_up, g_down):
    h_all = _norm_mod_all(ctx2, x2, norm_mix_w, sh_in, sc_in)
    p_a = _mm_nn(h_all, w_a, tm=768, tn=1024, out_dtype=F32, name="proj_a")
    p_b = _mm_nn(h_all, w_b, tm=768, tn=1280, out_dtype=F32, name="proj_b")
    p_c = _mm_nn(h_all, w_c, tm=256, tn=2048, out_dtype=F32, name="proj_c", row_off=1, rows=S)
    o, st = _hgrn_fwd(p_a, lbl)
    cos, sin = _rope_tables()
    qnw_t, knw_t = jnp.tile(q_norm_w, (1, ATT_HEADS)), jnp.tile(k_norm_w, (1, KV_HEADS))
    y_hg, qn, k_rep, v_rep = _prep_fwd(p_b, o, cos, sin, hgrn_norm_w, qnw_t, knw_t)
    y_at, lse = _attn_fwd(qn, k_rep, v_rep, attn_sinks)
    a, b, mixed, r, x1, h2 = _merge_fwd(y_hg, y_at, p_c, x2, w_bh, w_ba, w_o, gate1, norm_ffn_w, sh2, sc2)

    act, d_gate, d_up, d_f, dx1, acc_ffn = _ffn_fused(x1, h2, tgt, g_gate, g_up, g_down, gate2,
                                                      norm_ffn_w, sc2)
    t_down = _mm_tn_blocked(act, d_f, a_blocked=True, name="grad_down")
    t_gate = _mm_tn_blocked(h2, d_gate, a_blocked=False, name="grad_gate")
    t_up = _mm_tn_blocked(h2, d_up, a_blocked=False, name="grad_up")

    d_r, d_a, d_b, dp_c, dy_hg, dy_at, acc_mg = _merge_bwd(dx1, r, a, b, p_c, w_bh, w_ba, w_o, gate1)
    t_out = _mm_tn(mixed, d_r, tk=512, nk=4, a_off=0, tm=512, tn=1024, out_dtype=BF16, name="grad_out")
    t_bh = _mm_tn(y_hg, d_a, tk=512, nk=4, a_off=0, tm=512, tn=1024, out_dtype=BF16, name="grad_bh")
    t_ba = _mm_tn(y_at, d_b, tk=512, nk=4, a_off=0, tm=512, tn=1024, out_dtype=BF16, name="grad_ba")
    dq, dk_rep, dv_rep, dsink = _attn_bwd(qn, k_rep, v_rep, attn_sinks, y_at, lse, dy_at)
    dp_b, d_o, acc_prep = _prep_bwd(p_b, o, cos, sin, hgrn_norm_w, qnw_t, knw_t,
                                    dy_hg, dq, dk_rep, dv_rep)
    dp_a, dlb = _hgrn_bwd(p_a, lbl, d_o, st)
    grad_x, acc_in = _input_bwd(dp_a, dp_b, dp_c, w_a, w_b, w_c, ctx2, x2, dx1, norm_mix_w, sh_in, sc_in)
    t_a = _mm_tn(h_all, dp_a, tk=768, nk=3, a_off=0, tm=512, tn=1024, out_dtype=BF16, name="grad_in_a")
    t_b = _mm_tn(h_all, dp_b, tk=768, nk=3, a_off=0, tm=512, tn=1280, out_dtype=BF16, name="grad_in_b")
    t_c = _mm_tn(h_all, dp_c, tk=256, nk=8, a_off=1, tm=512, tn=1024, out_dtype=BF16, name="grad_in_c")
    t_in = _blocks_from_cols(jnp.concatenate([t_a, t_b, t_c], axis=1)[:, _INV_PERM])
    terms = [t_in, _blocks_from_cols(t_bh), _blocks_from_cols(t_ba), t_out.reshape(N_DEV, D // N_DEV, D),
             t_gate, t_up, t_down]
    return grad_x, _pack_small(acc_in, acc_mg, acc_ffn, acc_prep, dsink, dlb), terms


def kernel(x, c, ctx, c_ctx, w_ada, b_ada, norm_mix_w, norm_ffn_w, w_in, hgrn_lb_logits, hgrn_norm_w, q_norm_w, k_norm_w, attn_sinks, w_branch_hgrn, w_branch_attn, w_out, w_ffn_gate, w_ffn_up, w_ffn_down, loss_target, m_c_ctx, m_w_ada, m_b_ada, m_norm_mix_w, m_norm_ffn_w, m_w_in, m_hgrn_lb_logits, m_hgrn_norm_w, m_q_norm_w, m_k_norm_w, m_attn_sinks, m_w_branch_hgrn, m_w_branch_attn, m_w_out, m_w_ffn_gate, m_w_ffn_up, m_w_ffn_down, v_c_ctx, v_w_ada, v_b_ada, v_norm_mix_w, v_norm_ffn_w, v_w_in, v_hgrn_lb_logits, v_hgrn_norm_w, v_q_norm_w, v_k_norm_w, v_attn_sinks, v_w_branch_hgrn, v_w_branch_attn, v_w_out, v_w_ffn_gate, v_w_ffn_up, v_w_ffn_down):
    me = 4 * lax.axis_index("x") + 2 * lax.axis_index("y") + lax.axis_index("c")
    x2, ctx2, tgt = x[0], ctx[0], loss_target[0]
    w_ada2, w_in2 = w_ada[0], w_in[0]

    blk = jnp.zeros((8, D), F32).at[0].set(c[0]).at[1, :256].set(hgrn_lb_logits.reshape(256))
    (g0,) = _all_gather([blk], "gather_cond", True)
    cc = jnp.zeros((16, D), F32).at[:8].set(g0[:, 0, :]).at[8].set(c_ctx)
    lbl = jnp.transpose(g0[:, 1, :256].reshape(N_DEV, 2, 2, 64), (1, 2, 0, 3)).reshape(2, 2, HGW)

    b_cols = lax.dynamic_slice(b_ada, (0, me * ADA_BLK), (1, ADA_BLK))
    (g1,) = _all_gather([_ada_rows(cc, w_ada2, b_cols)], "gather_mod", True)
    mod_all = _cols_from_blocks(g1)
    mod = lax.dynamic_slice(mod_all, (me, 0), (1, 6 * D)).reshape(6, D)
    mod_c = mod_all[8].reshape(6, D)
    sh1, sc1, gate1, sh2, sc2, gate2 = [mod[k:k + 1] for k in range(6)]
    sh_in = jnp.concatenate([mod_c[0:1], sh1], axis=0)
    sc_in = jnp.concatenate([mod_c[1:2], sc1], axis=0)

    shards = [w_in2, w_branch_hgrn[0], w_branch_attn[0], w_out[0], w_ffn_gate[0], w_ffn_up[0],
              w_ffn_down[0]]
    g_in, g_bh, g_ba, g_out, g_gate, g_up, g_down = _all_gather(
        [s.astype(BF16) for s in shards], "gather_weights", False)
    w_in_full = _cols_from_blocks(g_in)[:, _PERM]
    w_a, w_b, w_c = w_in_full[:, :WA], w_in_full[:, WA:WA + WB], w_in_full[:, WA + WB:]
    w_bh, w_ba = _cols_from_blocks(g_bh), _cols_from_blocks(g_ba)
    w_o = g_out.reshape(D, D)

    grad_x, small, terms = _local_step(
        x2, ctx2, tgt, lbl, sh_in, sc_in, gate1, sh2, sc2, gate2, norm_mix_w, norm_ffn_w, hgrn_norm_w,
        q_norm_w, k_norm_w, attn_sinks, w_a, w_b, w_c, w_bh, w_ba, w_o, g_gate, g_up, g_down)

    r_in, r_bh, r_ba, r_out, r_gate, r_up, r_down = _reduce_scatter_exchange(terms, "scatter_grads")
    big = {}
    for nm, rr, ww, mm, vv, tr in (
            ("w_in", r_in, w_in2, m_w_in[0], v_w_in[0], 256),
            ("w_branch_hgrn", r_bh, w_branch_hgrn[0], m_w_branch_hgrn[0], v_w_branch_hgrn[0], 512),
            ("w_branch_attn", r_ba, w_branch_attn[0], m_w_branch_attn[0], v_w_branch_attn[0], 512),
            ("w_out", r_out, w_out[0], m_w_out[0], v_w_out[0], 128),
            ("w_ffn_gate", r_gate, w_ffn_gate[0], m_w_ffn_gate[0], v_w_ffn_gate[0], 256),
            ("w_ffn_up", r_up, w_ffn_up[0], m_w_ffn_up[0], v_w_ffn_up[0], 256),
            ("w_ffn_down", r_down, w_ffn_down[0], m_w_ffn_down[0], v_w_ffn_down[0], 352)):
        big[nm] = [t[None] for t in _adamw_sharded(rr, ww, mm, vv, "adamw_" + nm, tr)]

    (g2,) = _all_gather([small], "gather_small", True)
    tot = _sum_small(g2)
    dm = jnp.zeros((16, 6 * D), F32).at[:8].set(g2[:, R_DMOD:R_DMOD + 6, :].reshape(N_DEV, 6 * D))
    dm = dm.at[8, :2 * D].set(tot[R_DCTX:R_DCTX + 2].reshape(2 * D))
    dm_cols = lax.dynamic_slice(dm, (0, me * ADA_BLK), (16, ADA_BLK))
    g_w_ada, dsc_term = _ada_grads(cc, dm_cols, w_ada2)
    (g3,) = _all_gather([dsc_term], "gather_cctx", True)
    g_c_ctx = _c_ctx_grad(g3, c_ctx.reshape(1, D)).reshape(D)
    g_lbl = _lb_grads(tot[R_DLB:R_DLB + 2, :HGW], lbl)
    g_lb_mine = lax.dynamic_slice(g_lbl, (0, 0, me * 64), (2, 2, 64))
    misc = tot[R_MISC]
    loss = misc[M_LOSS]

    def pack_rep(bada, cctx, nmix, nffn, hnw, qnw, knw, snk):
        last = jnp.zeros((D,), F32).at[0:128].set(hnw.reshape(128)).at[128:192].set(qnw.reshape(64))
        last = last.at[256:320].set(knw.reshape(64)).at[384:392].set(snk.reshape(8))
        rows = jnp.concatenate([bada.reshape(6, D), cctx.reshape(1, D), nmix.reshape(1, D),
                                nffn.reshape(1, D), last[None]], axis=0)
        return jnp.concatenate([rows, jnp.zeros((6, D), F32)], axis=0)

    g_b_ada = jnp.concatenate([tot[R_BADA01:R_BADA01 + 2], tot[2:6]], axis=0)
    g_rep = pack_rep(g_b_ada, g_c_ctx, tot[R_NMIX], tot[R_NFFN],
                     misc[M_HNW:M_HNW + 128], misc[M_QNW:M_QNW + 64], misc[M_KNW:M_KNW + 64],
                     misc[M_SINK:M_SINK + 8])
    w_rep = pack_rep(b_ada, c_ctx, norm_mix_w, norm_ffn_w, hgrn_norm_w, q_norm_w, k_norm_w, attn_sinks)
    m_rep = pack_rep(m_b_ada, m_c_ctx, m_norm_mix_w, m_norm_ffn_w, m_hgrn_norm_w, m_q_norm_w,
                     m_k_norm_w, m_attn_sinks)
    v_rep_ = pack_rep(v_b_ada, v_c_ctx, v_norm_mix_w, v_norm_ffn_w, v_hgrn_norm_w, v_q_norm_w,
                      v_k_norm_w, v_attn_sinks)
    d_rep, nm_rep, nv_rep = _adamw_plain(g_rep, w_rep, m_rep, v_rep_, "adamw_small")

    def unpack_rep(t):
        last = t[9]
        return {"b_ada": t[0:6].reshape(1, 6 * D), "c_ctx": t[6], "norm_mix_w": t[7:8],
                "norm_ffn_w": t[8:9], "hgrn_norm_w": last[None, 0:128], "q_norm_w": last[None, 128:192],
                "k_norm_w": last[None, 256:320], "attn_sinks": last[None, 384:392]}

    rep = [unpack_rep(t) for t in (g_rep, d_rep, nm_rep, nv_rep)]

    d_ada, nm_ada, nv_ada = _adamw_plain(g_w_ada, w_ada2, m_w_ada[0], v_w_ada[0], "adamw_w_ada")
    ada = [t[None] for t in (g_w_ada, d_ada, nm_ada, nv_ada)]
    lb_w = hgrn_lb_logits.reshape(4, 64)
    d_lb, nm_lb, nv_lb = _adamw_plain(g_lb_mine.reshape(4, 64), lb_w, m_hgrn_lb_logits.reshape(4, 64),
                                      v_hgrn_lb_logits.reshape(4, 64), "adamw_lb")
    lbs = [t.reshape(2, 2, 64) for t in (g_lb_mine, d_lb, nm_lb, nv_lb)]

    names = ['c_ctx', 'w_ada', 'b_ada', 'norm_mix_w', 'norm_ffn_w', 'w_in', 'hgrn_lb_logits', 'hgrn_norm_w',
             'q_norm_w', 'k_norm_w', 'attn_sinks', 'w_branch_hgrn', 'w_branch_attn', 'w_out', 'w_ffn_gate',
             'w_ffn_up', 'w_ffn_down']
    outs = [loss, grad_x[None]]
    for kind in range(4):
        for nm in names:
            if nm == 'w_ada':
                outs.append(ada[kind])
            elif nm == 'hgrn_lb_logits':
                outs.append(lbs[kind])
            elif nm in big:
                outs.append(big[nm][kind])
            else:
                outs.append(rep[kind][nm])
    return tuple(outs)
```

```python
import functools
import math

import numpy as np
import jax
import jax.numpy as jnp
from jax import lax
from jax.experimental import pallas as pl
from jax.experimental.pallas import tpu as pltpu

F32 = jnp.float32
BF16 = jnp.bfloat16

N_DEV = 8
D = 1024
S = 2048
L = 256
T = L + S
TM = 256
N_TILES = T // TM
N_LAT_TILES = S // TM
HG_HEADS = 4
HG_DIM = 128
HGW = 512
CHUNK = 32
N_CHUNKS = T // CHUNK
N_CTX_CHUNKS = L // CHUNK
N_LAT_CHUNKS = S // CHUNK
ATT_HEADS = 8
KV_HEADS = 2
HEAD_DIM = 64
ATW = 512
KVW = 128
BLOCK = 128
N_BLOCKS = S // BLOCK
GRID_W = 64
ROPE_THETA = 10000.0
D_FF = 2816
FF_BLK = D_FF // N_DEV
IN_COLS = 5376
IN_BLK = IN_COLS // N_DEV
ADA_BLK = 6 * D // N_DEV
EPS = 1e-6
WA, WB, WC = 2048, 1280, 2048

ADAM_LR = 0.001
ADAM_B1 = 0.9
ADAM_B2 = 0.999
ADAM_EPS = 1e-08
ADAM_WD = 0.01
ADAM_STEP = 10

VMEM_LIMIT = 56 * 1024 * 1024
MESH = pl.DeviceIdType.MESH


def _cp(sem=None, vmem=VMEM_LIMIT):
    return pltpu.CompilerParams(dimension_semantics=sem, vmem_limit_bytes=vmem)


def _full(shape):
    n = len(shape)
    return pl.BlockSpec(shape, lambda *_: (0,) * n)


_VMEM_WHOLE = pl.BlockSpec(memory_space=pltpu.VMEM)
_ANY = pl.BlockSpec(memory_space=pl.ANY)


def _sigmoid(v):
    return 1.0 / (1.0 + jnp.exp(-v))


def _dot(a, b):
    return jnp.dot(a, b, preferred_element_type=F32)


def _dot_nt(a, b):
    return lax.dot_general(a, b, (((1,), (1,)), ((), ())), preferred_element_type=F32)


def _dot_tn(a, b):
    return lax.dot_general(a, b, (((0,), (0,)), ((), ())), preferred_element_type=F32)


def _split3(v):
    hi = v.astype(BF16)
    r = v - hi.astype(F32)
    mid = r.astype(BF16)
    lo = (r - mid.astype(F32)).astype(BF16)
    return hi, mid, lo


def _dot_exact_rhs01(v, m01):
    hi, mid, lo = _split3(v)
    return _dot(hi, m01) + _dot(mid, m01) + _dot(lo, m01)


def _dot_exact_lhs01(m01, v):
    hi, mid, lo = _split3(v)
    return _dot(m01, hi) + _dot(m01, mid) + _dot(m01, lo)


def _dot_f32(a, b, dot=_dot):
    ah, am, al = _split3(a)
    bh, bm, bl = _split3(b)
    return (dot(ah, bh) + (dot(ah, bm) + dot(am, bh))
            + (dot(am, bm) + dot(ah, bl) + dot(al, bh)))


def _my_pos():
    return lax.axis_index("x"), lax.axis_index("y"), lax.axis_index("c")


def _all_gather(blocks, name, in_vmem):
    n = len(blocks)

    def body(*refs):
        ins, outs = refs[:n], refs[n:2 * n]
        send_sems, recv_sems, local_sems = refs[2 * n:]
        x, y, c = _my_pos()
        me, sibling = (x, y, c), (x, y, 1 - c)
        chips = [(1 - x, y), (x, 1 - y), (1 - x, 1 - y)]

        def slot(a, px, py, pc):
            return outs[a].at[4 * px + 2 * py + pc]

        def copy(a, k, block, to, src=None):
            return pltpu.make_async_remote_copy(
                src_ref=slot(a, *block) if src is None else src, dst_ref=slot(a, *block),
                send_sem=send_sems.at[a, k], recv_sem=recv_sems.at[a, k],
                device_id=to, device_id_type=MESH)

        mine = [pltpu.make_async_copy(ins[a], slot(a, *me), local_sems.at[a]) for a in range(n)]
        for cp in mine:
            cp.start()
        first = []
        for a in range(n):
            first.append(copy(a, 0, me, sibling, src=ins[a]))
            first += [copy(a, 1 + j, me, (*chip, c), src=ins[a]) for j, chip in enumerate(chips)]
        for cp in first:
            cp.start()
        passed = []
        for j, chip in enumerate(chips):
            for a in range(n):
                copy(a, 1 + j, (*chip, c), me).wait_recv()
                fwd = copy(a, 4 + j, (*chip, c), sibling)
                fwd.start()
                passed.append(fwd)
        for a in range(n):
            copy(a, 0, sibling, me).wait_recv()
            for j, chip in enumerate(chips):
                copy(a, 4 + j, (*chip, 1 - c), me).wait_recv()
        for cp in first + passed:
            cp.wait_send()
        for cp in mine:
            cp.wait()

    spec = _VMEM_WHOLE if in_vmem else _ANY
    outs = pl.pallas_call(
        body, name=name,
        out_shape=[jax.ShapeDtypeStruct((N_DEV,) + b.shape, b.dtype) for b in blocks],
        in_specs=[spec] * n, out_specs=[spec] * n,
        scratch_shapes=[pltpu.SemaphoreType.DMA((n, 7)), pltpu.SemaphoreType.DMA((n, 7)),
                        pltpu.SemaphoreType.DMA((n,))],
    )(*blocks)
    return outs


N_CHIPS = 4


def _sibling_exchange(contribs, name):
    n = len(contribs)

    def body(*refs):
        ins, outs = refs[:n], refs[n:2 * n]
        send_sems, recv_sems = refs[2 * n:]
        x, y, c = _my_pos()
        copies = [pltpu.make_async_remote_copy(
            src_ref=ins[a].at[pl.ds(0, N_CHIPS), 1 - c], dst_ref=outs[a],
            send_sem=send_sems.at[a], recv_sem=recv_sems.at[a],
            device_id=(x, y, 1 - c), device_id_type=MESH) for a in range(n)]
        for cp in copies:
            cp.start()
        for cp in copies:
            cp.wait_recv()
        for cp in copies:
            cp.wait_send()

    return pl.pallas_call(
        body, name=name,
        out_shape=[jax.ShapeDtypeStruct((N_CHIPS,) + b.shape[2:], b.dtype) for b in contribs],
        in_specs=[_ANY] * n, out_specs=[_ANY] * n,
        scratch_shapes=[pltpu.SemaphoreType.DMA((n,)), pltpu.SemaphoreType.DMA((n,))],
    )(*contribs)


def _pair_sum(mine, theirs, name):
    _, _, rows, cols = mine.shape

    def body(m_ref, t_ref, o_ref):
        c = lax.axis_index("c")
        o_ref[...] = (m_ref[c].astype(F32) + t_ref[...].astype(F32)).astype(BF16)

    return pl.pallas_call(
        body, name=name, grid=(N_CHIPS,),
        in_specs=[pl.BlockSpec((None, 2, rows, cols), lambda q: (q, 0, 0, 0)),
                  pl.BlockSpec((None, rows, cols), lambda q: (q, 0, 0))],
        out_specs=pl.BlockSpec((None, rows, cols), lambda q: (q, 0, 0)),
        out_shape=jax.ShapeDtypeStruct((N_CHIPS, rows, cols), BF16),
        compiler_params=_cp(("parallel",)),
    )(mine, theirs)


def _chip_exchange(sums, name):
    n = len(sums)

    def body(*refs):
        ins, outs = refs[:n], refs[n:2 * n]
        send_sems, recv_sems, local_sems = refs[2 * n:]
        x, y, c = _my_pos()
        q_me = 2 * x + y
        chips = [(1 - x, y), (x, 1 - y), (1 - x, 1 - y)]
        mine = [pltpu.make_async_copy(ins[a].at[q_me], outs[a].at[q_me], local_sems.at[a]) for a in range(n)]
        for cp in mine:
            cp.start()
        copies = []
        for j, (px, py) in enumerate(chips):
            for a in range(n):
                cp = pltpu.make_async_remote_copy(
                    src_ref=ins[a].at[2 * px + py], dst_ref=outs[a].at[q_me],
                    send_sem=send_sems.at[a, j], recv_sem=recv_sems.at[a, j],
                    device_id=(px, py, c), device_id_type=MESH)
                cp.start()
                copies.append((cp, a, j, 2 * px + py))
        for cp, a, j, q in copies:
            pltpu.make_async_remote_copy(
                src_ref=ins[a].at[q], dst_ref=outs[a].at[q],
                send_sem=send_sems.at[a, j], recv_sem=recv_sems.at[a, j],
                device_id=(x, y, c), device_id_type=MESH).wait_recv()
        for cp, a, j, q in copies:
            cp.wait_send()
        for cp in mine:
            cp.wait()

    return pl.pallas_call(
        body, name=name,
        out_shape=[jax.ShapeDtypeStruct(b.shape, b.dtype) for b in sums],
        in_specs=[_ANY] * n, out_specs=[_ANY] * n,
        scratch_shapes=[pltpu.SemaphoreType.DMA((n, 3)), pltpu.SemaphoreType.DMA((n, 3)),
                        pltpu.SemaphoreType.DMA((n,))],
    )(*sums)


def _mm_nn(a, b, *, tm, tn, out_dtype, name, row_off=0, rows=None):
    rows = a.shape[0] if rows is None else rows
    k, n = b.shape

    def body(a_ref, b_ref, o_ref):
        o_ref[...] = _dot(a_ref[...], b_ref[...]).astype(out_dtype)

    return pl.pallas_call(
        body, name=name, grid=(rows // tm, n // tn),
        in_specs=[pl.BlockSpec((tm, k), lambda i, j: (i + row_off, 0)),
                  pl.BlockSpec((k, tn), lambda i, j: (0, j))],
        out_specs=pl.BlockSpec((tm, tn), lambda i, j: (i, j)),
        out_shape=jax.ShapeDtypeStruct((rows, n), out_dtype),
        compiler_params=_cp(("parallel", "parallel")),
    )(a, b)


def _mm_tn(a, b, *, tk, nk, a_off, tm, tn, out_dtype, name):
    m, n = a.shape[1], b.shape[1]

    def body(a_ref, b_ref, o_ref, acc):
        kk = pl.program_id(2)

        @pl.when(kk == 0)
        def _():
            acc[...] = jnp.zeros_like(acc)

        acc[...] += _dot_tn(a_ref[...], b_ref[...])

        @pl.when(kk == nk - 1)
        def _():
            o_ref[...] = acc[...].astype(out_dtype)

    return pl.pallas_call(
        body, name=name, grid=(m // tm, n // tn, nk),
        in_specs=[pl.BlockSpec((tk, tm), lambda i, j, kk: (kk + a_off, i)),
                  pl.BlockSpec((tk, tn), lambda i, j, kk: (kk, j))],
        out_specs=pl.BlockSpec((tm, tn), lambda i, j, kk: (i, j)),
        out_shape=jax.ShapeDtypeStruct((m, n), out_dtype),
        scratch_shapes=[pltpu.VMEM((tm, tn), F32)],
        compiler_params=_cp(("parallel", "parallel", "arbitrary")),
    )(a, b)


def _mm_tn_blocked(a, b, *, a_blocked, name):
    if a_blocked:
        w, n = a.shape[2], b.shape[1]
        in_specs = [pl.BlockSpec((None, S, w), lambda j: (j, 0, 0)), _full((S, n))]
        out_shape, out_spec = (N_DEV, w, n), pl.BlockSpec((None, w, n), lambda j: (j, 0, 0))
    else:
        m, w = a.shape[1], b.shape[2]
        in_specs = [_full((S, m)), pl.BlockSpec((None, S, w), lambda j: (j, 0, 0))]
        out_shape, out_spec = (N_DEV, m, w), pl.BlockSpec((None, m, w), lambda j: (j, 0, 0))

    def body(a_ref, b_ref, o_ref):
        o_ref[...] = _dot_tn(a_ref[...], b_ref[...]).astype(BF16)

    return pl.pallas_call(
        body, name=name, grid=(N_DEV,), in_specs=in_specs, out_specs=out_spec,
        out_shape=jax.ShapeDtypeStruct(out_shape, BF16),
        compiler_params=_cp(("parallel",)),
    )(a, b)


def _ada_rows(cc, w_ada, b_cols):
    def body(c_ref, w_ref, b_ref, o_ref):
        cv = c_ref[...]
        o_ref[...] = _dot_f32(cv * _sigmoid(cv), w_ref[...]) + b_ref[...]

    return pl.pallas_call(
        body, name="ada_rows",
        in_specs=[_VMEM_WHOLE] * 3, out_specs=_VMEM_WHOLE,
        out_shape=jax.ShapeDtypeStruct((16, ADA_BLK), F32),
        compiler_params=_cp(),
    )(cc, w_ada, b_cols)


def _ada_grads(cc, dm_cols, w_ada):
    def body(c_ref, dm_ref, w_ref, gw_ref, dsc_ref):
        cv = c_ref[...]
        sc = cv * _sigmoid(cv)
        dm = dm_ref[...]
        gw_ref[...] = _dot_f32(sc, dm, dot=_dot_tn)
        dsc_ref[...] = _dot_f32(dm, w_ref[...], dot=_dot_nt)

    return pl.pallas_call(
        body, name="ada_grads",
        in_specs=[_VMEM_WHOLE] * 3, out_specs=[_VMEM_WHOLE] * 2,
        out_shape=[jax.ShapeDtypeStruct((D, ADA_BLK), F32), jax.ShapeDtypeStruct((16, D), F32)],
        compiler_params=_cp(),
    )(cc, dm_cols, w_ada)


def _lat(i):
    return jnp.maximum(i - 1, 0)


def _rms_mod(xv, nw, sh, sc):
    rstd = lax.rsqrt(jnp.mean(xv * xv, axis=-1, keepdims=True) + EPS)
    return (xv * rstd * nw) * (1.0 + sc) + sh


def _rms_mod_bwd(xv, nw, sc, dh):
    rstd = lax.rsqrt(jnp.mean(xv * xv, axis=-1, keepdims=True) + EPS)
    xhat = xv * rstd
    dn = dh * (1.0 + sc)
    dxhat = dn * nw
    dx = rstd * (dxhat - xhat * jnp.mean(dxhat * xhat, axis=-1, keepdims=True))
    return (dx, jnp.sum(dh, axis=0, keepdims=True), jnp.sum(dh * (xhat * nw), axis=0, keepdims=True),
            jnp.sum(dn * xhat, axis=0, keepdims=True))


def _norm_mod_all(ctx, x, nw, sh, sc):
    def body(ctx_ref, x_ref, nw_ref, sh_ref, sc_ref, o_ref):
        i = pl.program_id(0)
        sel = jnp.minimum(i, 1)
        xv = jnp.where(i == 0, ctx_ref[...], x_ref[...])
        o_ref[...] = _rms_mod(xv, nw_ref[...], sh_ref[pl.ds(sel, 1), :], sc_ref[pl.ds(sel, 1), :]).astype(BF16)

    return pl.pallas_call(
        body, name="norm_mod", grid=(N_TILES,),
        in_specs=[_full((TM, D)), pl.BlockSpec((TM, D), lambda i: (_lat(i), 0)),
                  _full((1, D)), _full((2, D)), _full((2, D))],
        out_specs=pl.BlockSpec((TM, D), lambda i: (i, 0)),
        out_shape=jax.ShapeDtypeStruct((T, D), BF16),
        compiler_params=_cp(("parallel",)),
    )(ctx, x, nw, sh, sc)


def _chunk_masks(reverse):
    row = lax.broadcasted_iota(jnp.int32, (TM, TM), 0)
    col = lax.broadcasted_iota(jnp.int32, (TM, TM), 1)
    same = (row // CHUNK) == (col // CHUNK)
    tri = same & ((col >= row) if reverse else (col <= row))
    return same, tri


def _chunk_order(i, reverse):
    if not reverse:
        return i
    return jnp.where(i < N_CTX_CHUNKS, N_CTX_CHUNKS - 1 - i, N_CHUNKS + N_CTX_CHUNKS - 1 - i)


def _decay_terms(z, lb, same01, tri01):
    f = lb + (1.0 - lb) * _sigmoid(z)
    g = jnp.log(f)
    hi, mid, lo = _split3(g)
    g3 = jnp.concatenate([hi, mid, lo], axis=1)
    b3 = _dot(tri01, g3)
    t3 = _dot(same01, g3)
    b = b3[:, :HG_DIM] + b3[:, HG_DIM:2 * HG_DIM] + b3[:, 2 * HG_DIM:]
    bt = t3[:, :HG_DIM] + t3[:, HG_DIM:2 * HG_DIM] + t3[:, 2 * HG_DIM:]
    return f, 1.0 - f, b, bt


def _chunk_outer(a, b):
    n = TM // CHUNK
    return jnp.einsum('ncv,nck->nvk', a.reshape(n, CHUNK, HG_DIM), b.reshape(n, CHUNK, HG_DIM),
                      preferred_element_type=F32)


def _hgrn_fwd(p_a, lbl):
    def body(p_ref, lbl_ref, o_ref, st_ref, qd_s, kd_s, u_s, v_s, ebt_s):
        for d in (0, 1):
            same, tri = _chunk_masks(d == 1)
            same01 = jnp.where(same, 1.0, 0.0).astype(BF16)
            tri01 = jnp.where(tri, 1.0, 0.0).astype(BF16)
            ll = lbl_ref[d]
            lb = _sigmoid(ll[0:1, :] - ll[1:2, :])

            def prep(r, carry):
                r0 = pl.multiple_of(r * TM, TM)
                z = p_ref[pl.ds(r0, TM), d * HG_DIM:(d + 1) * HG_DIM]
                _, k, b, bt = _decay_terms(z, lb, same01, tri01)
                vb = p_ref[pl.ds(r0, TM), 2 * HG_DIM:3 * HG_DIM].astype(BF16)
                u_s[pl.ds(r * (TM // CHUNK), TM // CHUNK)] = _chunk_outer(vb, (k * jnp.exp(bt - b)).astype(BF16))
                ebt_s[pl.ds(r0, TM), :] = jnp.exp(bt)
                v_s[pl.ds(r0, TM), :] = vb

                @pl.when(r >= 1)
                def _():
                    rl = pl.multiple_of(r0 - L, TM)
                    qr = p_ref[pl.ds(r0, TM), 3 * HG_DIM:4 * HG_DIM]
                    q = qr * _sigmoid(qr) * HG_DIM ** -0.5
                    qd_s[pl.ds(rl, TM), :] = (q * jnp.exp(b)).astype(BF16)
                    kd_s[pl.ds(rl, TM), :] = (k * jnp.exp(-b)).astype(BF16)

                return carry

            lax.fori_loop(0, N_TILES, prep, 0)

            def scan(i, st):
                nn = _chunk_order(i, d == 1)
                c0 = pl.multiple_of(nn * CHUNK, CHUNK)
                st_ref[d, nn] = st.astype(BF16)
                return st * ebt_s[pl.ds(c0, 1), :] + u_s[nn]

            lax.fori_loop(0, N_CHUNKS, scan, jnp.zeros((HG_DIM, HG_DIM), F32))

            def outp(r, carry):
                r0 = pl.multiple_of(r * TM, TM)
                qd = qd_s[pl.ds(r0, TM), :]
                a = jnp.where(tri, _dot_nt(qd, kd_s[pl.ds(r0, TM), :]), 0.0)
                o = _dot(a.astype(BF16), v_s[pl.ds(r0 + L, TM), :])
                stb = st_ref[d, pl.ds(N_CTX_CHUNKS + r * (TM // CHUNK), TM // CHUNK)]
                inter = jnp.einsum('nck,nvk->ncv', qd.reshape(TM // CHUNK, CHUNK, HG_DIM), stb,
                                   preferred_element_type=F32)
                o = o + inter.reshape(TM, HG_DIM)
                if d == 0:
                    o_ref[pl.ds(r0, TM), :] = o
                else:
                    o_ref[pl.ds(r0, TM), :] += o
                return carry

            lax.fori_loop(0, N_LAT_TILES, outp, 0)

    return pl.pallas_call(
        body, name="hgrn_fwd", grid=(HG_HEADS,),
        in_specs=[pl.BlockSpec((T, 4 * HG_DIM), lambda h: (0, h)),
                  pl.BlockSpec((2, 2, HG_DIM), lambda h: (0, 0, h))],
        out_specs=[pl.BlockSpec((S, HG_DIM), lambda h: (0, h)),
                   pl.BlockSpec((2, None, N_CHUNKS, HG_DIM, HG_DIM), lambda h: (0, h, 0, 0, 0))],
        out_shape=[jax.ShapeDtypeStruct((S, HGW), F32),
                   jax.ShapeDtypeStruct((2, HG_HEADS, N_CHUNKS, HG_DIM, HG_DIM), BF16)],
        scratch_shapes=[pltpu.VMEM((S, HG_DIM), BF16), pltpu.VMEM((S, HG_DIM), BF16),
                        pltpu.VMEM((N_CHUNKS, HG_DIM, HG_DIM), F32), pltpu.VMEM((T, HG_DIM), BF16),
                        pltpu.VMEM((T, HG_DIM), F32)],
        compiler_params=_cp(("parallel",)),
    )(p_a, lbl)


def _hgrn_bwd(p_a, lbl, d_o, st):
    cpt = TM // CHUNK

    def body(p_ref, lbl_ref, do_ref, st_ref, dp_ref, dlb_ref,
             b_s, bt_s, dbt_s, db_t, dk_t, dv_s, dq_s, qd_s, dst_s, w_s):
        dv_s[...] = jnp.zeros_like(dv_s)
        dq_s[...] = jnp.zeros_like(dq_s)
        for d in (0, 1):
            same, tri = _chunk_masks(d == 1)
            _, tri_rev = _chunk_masks(d == 0)
            same01 = jnp.where(same, 1.0, 0.0).astype(BF16)
            tri01 = jnp.where(tri, 1.0, 0.0).astype(BF16)
            later01 = jnp.where(tri_rev, 1.0, 0.0).astype(BF16)
            dbt_s[...] = jnp.zeros_like(dbt_s)
            ll = lbl_ref[d]
            lb = _sigmoid(ll[0:1, :] - ll[1:2, :])

            def prep(r, carry):
                r0 = pl.multiple_of(r * TM, TM)
                z = p_ref[pl.ds(r0, TM), d * HG_DIM:(d + 1) * HG_DIM]
                _, _, b, bt = _decay_terms(z, lb, same01, tri01)
                b_s[pl.ds(r0, TM), :] = b
                bt_s[pl.ds(r0, TM), :] = bt

                @pl.when(r >= 1)
                def _():
                    rl = pl.multiple_of(r0 - L, TM)
                    qr = p_ref[pl.ds(r0, TM), 3 * HG_DIM:4 * HG_DIM]
                    q = qr * _sigmoid(qr) * HG_DIM ** -0.5
                    qd = (q * jnp.exp(b)).astype(BF16)
                    qd_s[pl.ds(rl, TM), :] = qd
                    w_s[pl.ds(r * cpt, cpt)] = _chunk_outer(do_ref[pl.ds(rl, TM), :].astype(BF16), qd)

                return carry

            w_s[pl.ds(0, N_CTX_CHUNKS)] = jnp.zeros((N_CTX_CHUNKS, HG_DIM, HG_DIM), F32)
            lax.fori_loop(0, N_TILES, prep, 0)

            def rscan(j, dst):
                i = N_CHUNKS - 1 - j
                nn = _chunk_order(i, d == 1)
                c0 = pl.multiple_of(nn * CHUNK, CHUNK)
                dst_s[nn] = dst.astype(BF16)
                after = st_ref[d, _chunk_order(jnp.minimum(i + 1, N_CHUNKS - 1), d == 1)].astype(F32)
                dbt_s[pl.ds(c0, 1), :] = jnp.sum(after * dst, axis=0, keepdims=True)
                return dst * jnp.exp(bt_s[pl.ds(c0, 1), :]) + w_s[nn]

            lax.fori_loop(0, N_CHUNKS, rscan, jnp.zeros((HG_DIM, HG_DIM), F32))

            def grads(r, dlb):
                r0 = pl.multiple_of(r * TM, TM)
                z = p_ref[pl.ds(r0, TM), d * HG_DIM:(d + 1) * HG_DIM]
                sz = _sigmoid(z)
                f = lb + (1.0 - lb) * sz
                k = 1.0 - f
                b = b_s[pl.ds(r0, TM), :]
                e2 = jnp.exp(bt_s[pl.ds(r0, TM), :] - b)
                vb = p_ref[pl.ds(r0, TM), 2 * HG_DIM:3 * HG_DIM].astype(BF16)
                dstb = dst_s[pl.ds(r * cpt, cpt)]
                kd2 = k * e2
                dkd2 = jnp.einsum('ncv,nvk->nck', vb.reshape(cpt, CHUNK, HG_DIM), dstb,
                                  preferred_element_type=F32).reshape(TM, HG_DIM)
                dv = jnp.einsum('nck,nvk->ncv', kd2.astype(BF16).reshape(cpt, CHUNK, HG_DIM), dstb,
                                preferred_element_type=F32).reshape(TM, HG_DIM)
                dk_t[...] = dkd2 * e2
                db_t[...] = -(kd2 * dkd2)
                dv_s[pl.ds(r0, TM), :] += dv

                @pl.when(r >= 1)
                def _():
                    rl = pl.multiple_of(r0 - L, TM)
                    eb = jnp.exp(b)
                    enb = jnp.exp(-b)
                    qr = p_ref[pl.ds(r0, TM), 3 * HG_DIM:4 * HG_DIM]
                    sq = _sigmoid(qr)
                    qdf = qr * sq * HG_DIM ** -0.5 * eb
                    kdf = k * enb
                    qd = qd_s[pl.ds(rl, TM), :]
                    kd = kdf.astype(BF16)
                    do = do_ref[pl.ds(rl, TM), :].astype(BF16)
                    a = jnp.where(tri, _dot_nt(qd, kd), 0.0).astype(BF16)
                    da = jnp.where(tri, _dot_nt(do, vb), 0.0).astype(BF16)
                    stb = st_ref[d, pl.ds(r * cpt, cpt)]
                    dqd = _dot(da, kd) + jnp.einsum(
                        'ncv,nvk->nck', do.reshape(cpt, CHUNK, HG_DIM), stb,
                        preferred_element_type=F32).reshape(TM, HG_DIM)
                    dkd = _dot_tn(da, qd)
                    dv_s[pl.ds(r0, TM), :] += _dot_tn(a, do)
                    dk_t[...] += dkd * enb
                    db_t[...] += qdf * dqd - kdf * dkd
                    dq_s[pl.ds(rl, TM), :] += dqd * eb * (HG_DIM ** -0.5) * (sq * (1.0 + qr * (1.0 - sq)))

                dg = _dot_exact_lhs01(later01, db_t[...]) + _dot_exact_lhs01(same01, dbt_s[pl.ds(r0, TM), :])
                df = dg / f - dk_t[...]
                dp_ref[pl.ds(r0, TM), d * HG_DIM:(d + 1) * HG_DIM] = (
                    df * (1.0 - lb) * sz * (1.0 - sz)).astype(BF16)
                return dlb + jnp.sum(df * (1.0 - sz), axis=0, keepdims=True)

            dlb_ref[pl.ds(d, 1), :] = lax.fori_loop(0, N_TILES, grads, jnp.zeros((1, HG_DIM), F32))

        dp_ref[:, 2 * HG_DIM:3 * HG_DIM] = dv_s[...].astype(BF16)
        dp_ref[pl.ds(0, L), 3 * HG_DIM:4 * HG_DIM] = jnp.zeros((L, HG_DIM), BF16)
        dp_ref[pl.ds(L, S), 3 * HG_DIM:4 * HG_DIM] = dq_s[...].astype(BF16)

    return pl.pallas_call(
        body, name="hgrn_bwd", grid=(HG_HEADS,),
        in_specs=[pl.BlockSpec((T, 4 * HG_DIM), lambda h: (0, h)),
                  pl.BlockSpec((2, 2, HG_DIM), lambda h: (0, 0, h)),
                  pl.BlockSpec((S, HG_DIM), lambda h: (0, h)),
                  pl.BlockSpec((2, None, N_CHUNKS, HG_DIM, HG_DIM), lambda h: (0, h, 0, 0, 0))],
        out_specs=[pl.BlockSpec((T, 4 * HG_DIM), lambda h: (0, h)),
                   pl.BlockSpec((2, HG_DIM), lambda h: (0, h))],
        out_shape=[jax.ShapeDtypeStruct((T, WA), BF16), jax.ShapeDtypeStruct((2, HGW), F32)],
        scratch_shapes=[pltpu.VMEM((T, HG_DIM), F32), pltpu.VMEM((T, HG_DIM), F32),
                        pltpu.VMEM((T, HG_DIM), F32), pltpu.VMEM((TM, HG_DIM), F32),
                        pltpu.VMEM((TM, HG_DIM), F32),
                        pltpu.VMEM((T, HG_DIM), F32), pltpu.VMEM((S, HG_DIM), F32),
                        pltpu.VMEM((S, HG_DIM), BF16),
                        pltpu.VMEM((N_CHUNKS, HG_DIM, HG_DIM), BF16),
                        pltpu.VMEM((N_CHUNKS, HG_DIM, HG_DIM), F32)],
        compiler_params=_cp(("parallel",)),
    )(p_a, lbl, d_o, st)


def _rope_tables():
    t = np.arange(S)
    inv = ROPE_THETA ** (-np.arange(0, 32, 2, dtype=np.float64) / 32)
    lane = np.arange(64)
    pos = np.where(lane[None, :] < 32, (t // GRID_W)[:, None], (t % GRID_W)[:, None]).astype(np.float64)
    ang = pos * inv[(lane % 32) % 16][None, :]
    sign = np.where((lane % 32) < 16, -1.0, 1.0)[None, :]
    cos = np.tile(np.cos(ang), (1, 2)).astype(np.float32)
    sin = np.tile(np.sin(ang) * sign, (1, 2)).astype(np.float32)
    return jnp.asarray(cos), jnp.asarray(sin)


def _rope_partner(v):
    lane = lax.broadcasted_iota(jnp.int32, (1, 128), 1)
    first = (lane % 32) < 16
    slabs = []
    for j in range(v.shape[1] // 128):
        s = v[:, 128 * j:128 * (j + 1)]
        slabs.append(jnp.where(first, pltpu.roll(s, 112, 1), pltpu.roll(s, 16, 1)))
    return slabs[0] if len(slabs) == 1 else jnp.concatenate(slabs, axis=1)


def _group_ones(width, group):
    r = lax.broadcasted_iota(jnp.int32, (width, width), 0)
    c = lax.broadcasted_iota(jnp.int32, (width, width), 1)
    return jnp.where((r // group) == (c // group), 1.0, 0.0).astype(BF16)


def _group_mean(v, ones01, group):
    hi = v.astype(BF16)
    lo = (v - hi.astype(F32)).astype(BF16)
    return (_dot(hi, ones01) + _dot(lo, ones01)) * (1.0 / group)


def _rep_matrix():
    r = lax.broadcasted_iota(jnp.int32, (KVW, ATW), 0)
    c = lax.broadcasted_iota(jnp.int32, (KVW, ATW), 1)
    return jnp.where(r == HEAD_DIM * (c // 256) + c % HEAD_DIM, 1.0, 0.0).astype(BF16)


def _tile_lanes(v, reps):
    return jnp.concatenate([v] * reps, axis=1)


def _prep_fwd(p_b, o, cos, sin, hnw, qnw, knw):
    def body(p_ref, o_ref, cos_ref, sin_ref, hnw_ref, qnw_ref, knw_ref, y_ref, q_ref, k_ref, v_ref):
        i = pl.program_id(0)
        rep = _rep_matrix()
        ones_k = _group_ones(KVW, HEAD_DIM)
        kr = p_ref[:, 1024:1152]
        krstd = lax.rsqrt(_group_mean(kr * kr, ones_k, HEAD_DIM) + EPS)
        kn = kr * krstd * knw_ref[...]
        v_ref[...] = _dot(p_ref[:, 1152:1280].astype(BF16), rep).astype(BF16)

        @pl.when(i == 0)
        def _():
            k_ref[...] = _dot(kn.astype(BF16), rep).astype(BF16)

        @pl.when(i > 0)
        def _():
            cs, sn = cos_ref[...], sin_ref[...]
            kro = kn * cs + _rope_partner(kn) * sn
            k_ref[...] = _dot(kro.astype(BF16), rep).astype(BF16)
            qr = p_ref[:, 512:1024]
            qrstd = lax.rsqrt(_group_mean(qr * qr, _group_ones(ATW, HEAD_DIM), HEAD_DIM) + EPS)
            qn = qr * qrstd * qnw_ref[...]
            qro = qn * _tile_lanes(cs, 4) + _rope_partner(qn) * _tile_lanes(sn, 4)
            q_ref[...] = (qro * HEAD_DIM ** -0.5).astype(BF16)
            ys = []
            for h in range(HG_HEADS):
                oh = o_ref[:, HG_DIM * h:HG_DIM * (h + 1)]
                gh = p_ref[:, HG_DIM * h:HG_DIM * (h + 1)]
                rstd = lax.rsqrt(jnp.mean(oh * oh, axis=-1, keepdims=True) + EPS)
                ys.append(oh * rstd * hnw_ref[...] * (gh * _sigmoid(gh)))
            y_ref[...] = jnp.concatenate(ys, axis=1).astype(BF16)

    return pl.pallas_call(
        body, name="prep_fwd", grid=(N_TILES,),
        in_specs=[pl.BlockSpec((TM, WB), lambda i: (i, 0)),
                  pl.BlockSpec((TM, HGW), lambda i: (_lat(i), 0)),
                  pl.BlockSpec((TM, 128), lambda i: (_lat(i), 0)),
                  pl.BlockSpec((TM, 128), lambda i: (_lat(i), 0)),
                  _full((1, HG_DIM)), _full((1, ATW)), _full((1, KVW))],
        out_specs=[pl.BlockSpec((TM, HGW), lambda i: (_lat(i), 0)),
                   pl.BlockSpec((TM, ATW), lambda i: (_lat(i), 0)),
                   pl.BlockSpec((TM, ATW), lambda i: (i, 0)),
                   pl.BlockSpec((TM, ATW), lambda i: (i, 0))],
        out_shape=[jax.ShapeDtypeStruct((S, HGW), BF16), jax.ShapeDtypeStruct((S, ATW), BF16),
                   jax.ShapeDtypeStruct((T, ATW), BF16), jax.ShapeDtypeStruct((T, ATW), BF16)],
        compiler_params=_cp(("arbitrary",)),
    )(p_b, o, cos, sin, hnw, qnw, knw)


def _prep_bwd(p_b, o, cos, sin, hnw, qnw, knw, dy_hg, dq, dk_rep, dv_rep):
    def body(p_ref, o_ref, cos_ref, sin_ref, hnw_ref, qnw_ref, knw_ref, dy_ref, dq_ref, dk_ref, dv_ref,
             dp_ref, do_ref, acc_ref):
        i = pl.program_id(0)

        @pl.when(i == 0)
        def _():
            acc_ref[...] = jnp.zeros_like(acc_ref)

        rep = _rep_matrix()
        ones_k = _group_ones(KVW, HEAD_DIM)

        def fold(v):
            hi = v.astype(BF16)
            lo = (v - hi.astype(F32)).astype(BF16)
            return _dot_nt(hi, rep) + _dot_nt(lo, rep)

        kr = p_ref[:, 1024:1152]
        krstd = lax.rsqrt(_group_mean(kr * kr, ones_k, HEAD_DIM) + EPS)
        khat = kr * krstd
        kw = knw_ref[...]
        dkro = fold(dk_ref[...])
        dv = fold(dv_ref[...])

        def k_back(dkn):
            dkhat = dkn * kw
            dkr = krstd * (dkhat - khat * _group_mean(dkhat * khat, ones_k, HEAD_DIM))
            acc_ref[2:3, 0:KVW] += jnp.sum(dkn * khat, axis=0, keepdims=True)
            dp_ref[:, 1024:1152] = dkr.astype(BF16)
            dp_ref[:, 1152:1280] = dv.astype(BF16)

        @pl.when(i == 0)
        def _():
            k_back(dkro)
            dp_ref[:, 0:1024] = jnp.zeros((TM, 1024), BF16)

        @pl.when(i > 0)
        def _():
            cs, sn = cos_ref[...], sin_ref[...]
            k_back(dkro * cs + _rope_partner(dkro * sn))
            ones_q = _group_ones(ATW, HEAD_DIM)
            qr = p_ref[:, 512:1024]
            qrstd = lax.rsqrt(_group_mean(qr * qr, ones_q, HEAD_DIM) + EPS)
            qhat = qr * qrstd
            dqro = dq_ref[...] * HEAD_DIM ** -0.5
            dqn = dqro * _tile_lanes(cs, 4) + _rope_partner(dqro * _tile_lanes(sn, 4))
            dqhat = dqn * qnw_ref[...]
            dqr = qrstd * (dqhat - qhat * _group_mean(dqhat * qhat, ones_q, HEAD_DIM))
            acc_ref[1:2, :] += jnp.sum(dqn * qhat, axis=0, keepdims=True)
            dp_ref[:, 512:1024] = dqr.astype(BF16)
            dws = jnp.zeros((1, HG_DIM), F32)
            for h in range(HG_HEADS):
                sl = slice(HG_DIM * h, HG_DIM * (h + 1))
                oh, gh, dy = o_ref[:, sl], p_ref[:, sl], dy_ref[:, sl]
                rstd = lax.rsqrt(jnp.mean(oh * oh, axis=-1, keepdims=True) + EPS)
                ohat = oh * rstd
                sg = _sigmoid(gh)
                dp_ref[:, sl] = (dy * (ohat * hnw_ref[...]) * (sg * (1.0 + gh * (1.0 - sg)))).astype(BF16)
                dn = dy * (gh * sg)
                dws = dws + jnp.sum(dn * ohat, axis=0, keepdims=True)
                dohat = dn * hnw_ref[...]
                do_ref[:, sl] = rstd * (dohat - ohat * jnp.mean(dohat * ohat, axis=-1, keepdims=True))
            acc_ref[0:1, 0:HG_DIM] += dws

    return pl.pallas_call(
        body, name="prep_bwd", grid=(N_TILES,),
        in_specs=[pl.BlockSpec((TM, WB), lambda i: (i, 0)),
                  pl.BlockSpec((TM, HGW), lambda i: (_lat(i), 0)),
                  pl.BlockSpec((TM, 128), lambda i: (_lat(i), 0)),
                  pl.BlockSpec((TM, 128), lambda i: (_lat(i), 0)),
                  _full((1, HG_DIM)), _full((1, ATW)), _full((1, KVW)),
                  pl.BlockSpec((TM, HGW), lambda i: (_lat(i), 0)),
                  pl.BlockSpec((TM, ATW), lambda i: (_lat(i), 0)),
                  pl.BlockSpec((TM, ATW), lambda i: (i, 0)),
                  pl.BlockSpec((TM, ATW), lambda i: (i, 0))],
        out_specs=[pl.BlockSpec((TM, WB), lambda i: (i, 0)),
                   pl.BlockSpec((TM, HGW), lambda i: (_lat(i), 0)),
                   _full((8, ATW))],
        out_shape=[jax.ShapeDtypeStruct((T, WB), BF16), jax.ShapeDtypeStruct((S, HGW), F32),
                   jax.ShapeDtypeStruct((8, ATW), F32)],
        compiler_params=_cp(("arbitrary",)),
    )(p_b, o, cos, sin, hnw, qnw, knw, dy_hg, dq, dk_rep, dv_rep)


NEG = -1e30
_CTX_BLOCKS = L // BLOCK


def _attn_window_specs():
    prev = pl.BlockSpec((BLOCK, ATW), lambda i: (jnp.maximum(i - 1, 0) + _CTX_BLOCKS, 0))
    own = pl.BlockSpec((BLOCK, ATW), lambda i: (i + _CTX_BLOCKS, 0))
    nxt = pl.BlockSpec((BLOCK, ATW), lambda i: (jnp.minimum(i + 1, N_BLOCKS - 1) + _CTX_BLOCKS, 0))
    return [prev, own, nxt, _full((L, ATW))]


def _attn_valid(i):
    qi = lax.broadcasted_iota(jnp.int32, (4 * BLOCK, 3 * BLOCK), 0) % BLOCK
    kj = lax.broadcasted_iota(jnp.int32, (4 * BLOCK, 3 * BLOCK), 1)
    return ((jnp.abs(kj - BLOCK - qi) <= BLOCK) & ((kj >= BLOCK) | (i > 0))
            & ((kj < 2 * BLOCK) | (i < N_BLOCKS - 1)))


def _stack_heads(qg):
    lane = lax.broadcasted_iota(jnp.int32, (1, 256), 1) // HEAD_DIM
    return jnp.concatenate([jnp.where(lane == g, qg, jnp.zeros_like(qg)) for g in range(4)], axis=0)


def _unstack_heads(v4):
    lane = lax.broadcasted_iota(jnp.int32, (1, 256), 1) // HEAD_DIM
    out = jnp.where(lane == 0, v4[0:BLOCK], 0.0)
    for g in range(1, 4):
        out = out + jnp.where(lane == g, v4[g * BLOCK:(g + 1) * BLOCK], 0.0)
    return out


def _sink_rows(sink_ref, hk):
    return jnp.concatenate(
        [jnp.broadcast_to(sink_ref[0:1, 4 * hk + g:4 * hk + g + 1], (BLOCK, 1)) for g in range(4)], axis=0)


def _attn_fwd(q, k_rep, v_rep, sinks):
    def body(q_ref, kp, ko, kn, kc, vp, vo, vn, vc, sink_ref, y_ref, lse_ref):
        i = pl.program_id(0)
        valid = _attn_valid(i)
        lane8 = lax.broadcasted_iota(jnp.int32, (1, ATT_HEADS), 1)
        lse_out = jnp.zeros((BLOCK, ATT_HEADS), F32)
        for hk in range(KV_HEADS):
            sl = slice(256 * hk, 256 * (hk + 1))
            q4 = _stack_heads(q_ref[:, sl])
            kl = jnp.concatenate([kp[:, sl], ko[:, sl], kn[:, sl]], axis=0)
            vl = jnp.concatenate([vp[:, sl], vo[:, sl], vn[:, sl]], axis=0)
            s_loc = jnp.where(valid, _dot_nt(q4, kl), NEG)
            s_ctx = _dot_nt(q4, kc[:, sl])
            sink = _sink_rows(sink_ref, hk)
            m = jnp.maximum(jnp.maximum(jnp.max(s_loc, axis=1, keepdims=True),
                                        jnp.max(s_ctx, axis=1, keepdims=True)), sink)
            p_loc = jnp.exp(s_loc - m)
            p_ctx = jnp.exp(s_ctx - m)
            den = (jnp.sum(p_loc, axis=1, keepdims=True) + jnp.sum(p_ctx, axis=1, keepdims=True)
                   + jnp.exp(sink - m))
            o4 = (_dot(p_loc.astype(BF16), vl) + _dot(p_ctx.astype(BF16), vc[:, sl])) / den
            y_ref[:, sl] = _unstack_heads(o4).astype(BF16)
            lse4 = m + jnp.log(den)
            for g in range(4):
                lse_out = lse_out + jnp.where(lane8 == 4 * hk + g, lse4[g * BLOCK:(g + 1) * BLOCK], 0.0)
        lse_ref[...] = lse_out

    return pl.pallas_call(
        body, name="attn_fwd", grid=(N_BLOCKS,),
        in_specs=[pl.BlockSpec((BLOCK, ATW), lambda i: (i, 0))] + _attn_window_specs()
        + _attn_window_specs() + [_full((1, ATT_HEADS))],
        out_specs=[pl.BlockSpec((BLOCK, ATW), lambda i: (i, 0)),
                   pl.BlockSpec((BLOCK, ATT_HEADS), lambda i: (i, 0))],
        out_shape=[jax.ShapeDtypeStruct((S, ATW), BF16), jax.ShapeDtypeStruct((S, ATT_HEADS), F32)],
        compiler_params=_cp(("parallel",)),
    )(q, k_rep, k_rep, k_rep, k_rep, v_rep, v_rep, v_rep, v_rep, sinks)


def _attn_bwd(q, k_rep, v_rep, sinks, y_at, lse, dy):
    def body(q_ref, kp, ko, kn, kc, vp, vo, vn, vc, sink_ref, y_ref, lse_ref, dy_ref,
             dq_ref, dk_ref, dv_ref, dsink_ref, dk_acc, dv_acc):
        i = pl.program_id(0)

        @pl.when(i == 0)
        def _():
            dk_acc[...] = jnp.zeros_like(dk_acc)
            dv_acc[...] = jnp.zeros_like(dv_acc)
            dk_ref[pl.ds(0, L), :] = jnp.zeros((L, ATW), F32)
            dv_ref[pl.ds(0, L), :] = jnp.zeros((L, ATW), F32)
            dsink_ref[...] = jnp.zeros_like(dsink_ref)

        valid = _attn_valid(i)
        lane8 = lax.broadcasted_iota(jnp.int32, (1, ATT_HEADS), 1)
        w0 = pl.multiple_of(i * BLOCK, BLOCK)
        dsink = jnp.zeros((1, ATT_HEADS), F32)
        for hk in range(KV_HEADS):
            sl = slice(256 * hk, 256 * (hk + 1))
            q4 = _stack_heads(q_ref[:, sl])
            do4f = _stack_heads(dy_ref[:, sl])
            o4 = _stack_heads(y_ref[:, sl]).astype(F32)
            do4 = do4f.astype(BF16)
            kl = jnp.concatenate([kp[:, sl], ko[:, sl], kn[:, sl]], axis=0)
            vl = jnp.concatenate([vp[:, sl], vo[:, sl], vn[:, sl]], axis=0)
            lse4 = jnp.concatenate(
                [jnp.sum(jnp.where(lane8 == 4 * hk + g, lse_ref[...], 0.0), axis=1, keepdims=True)
                 for g in range(4)], axis=0)
            p_loc = jnp.where(valid, jnp.exp(_dot_nt(q4, kl) - lse4), 0.0)
            p_ctx = jnp.exp(_dot_nt(q4, kc[:, sl]) - lse4)
            delta = jnp.sum(do4f * o4, axis=1, keepdims=True)
            ds_loc = (p_loc * (_dot_nt(do4, vl) - delta)).astype(BF16)
            ds_ctx = (p_ctx * (_dot_nt(do4, vc[:, sl]) - delta)).astype(BF16)
            dq_ref[:, sl] = _unstack_heads(_dot(ds_loc, kl) + _dot(ds_ctx, kc[:, sl]))
            dk_acc[pl.ds(w0, 3 * BLOCK), sl] += _dot_tn(ds_loc, q4)
            dv_acc[pl.ds(w0, 3 * BLOCK), sl] += _dot_tn(p_loc.astype(BF16), do4)
            dk_ref[pl.ds(0, L), sl] += _dot_tn(ds_ctx, q4)
            dv_ref[pl.ds(0, L), sl] += _dot_tn(p_ctx.astype(BF16), do4)
            p_sink = jnp.exp(_sink_rows(sink_ref, hk) - lse4)
            for g in range(4):
                rows = slice(g * BLOCK, (g + 1) * BLOCK)
                dsink = dsink + jnp.where(lane8 == 4 * hk + g,
                                          -jnp.sum(p_sink[rows] * delta[rows], axis=0, keepdims=True), 0.0)
        dsink_ref[...] += dsink

        @pl.when(i == N_BLOCKS - 1)
        def _():
            dk_ref[pl.ds(L, S), :] = dk_acc[pl.ds(BLOCK, S), :]
            dv_ref[pl.ds(L, S), :] = dv_acc[pl.ds(BLOCK, S), :]

    row_q = pl.BlockSpec((BLOCK, ATW), lambda i: (i, 0))
    return pl.pallas_call(
        body, name="attn_bwd", grid=(N_BLOCKS,),
        in_specs=[row_q] + _attn_window_specs() + _attn_window_specs()
        + [_full((1, ATT_HEADS)), row_q, pl.BlockSpec((BLOCK, ATT_HEADS), lambda i: (i, 0)), row_q],
        out_specs=[row_q, _full((T, ATW)), _full((T, ATW)), _full((1, ATT_HEADS))],
        out_shape=[jax.ShapeDtypeStruct((S, ATW), F32), jax.ShapeDtypeStruct((T, ATW), F32),
                   jax.ShapeDtypeStruct((T, ATW), F32), jax.ShapeDtypeStruct((1, ATT_HEADS), F32)],
        scratch_shapes=[pltpu.VMEM((S + 2 * BLOCK, ATW), F32), pltpu.VMEM((S + 2 * BLOCK, ATW), F32)],
        compiler_params=_cp(("arbitrary",)),
    )(q, k_rep, k_rep, k_rep, k_rep, v_rep, v_rep, v_rep, v_rep, sinks, y_at, lse, dy)


def _merge_fwd(y_hg, y_at, p_c, x, w_bh, w_ba, w_out, g1, nfw, sh2, sc2):
    def body(yh_ref, ya_ref, g_ref, x_ref, wbh_ref, wba_ref, wo_ref, g1_ref, nfw_ref, sh_ref, sc_ref,
             a_ref, b_ref, mx_ref, r_ref, x1_ref, h2_ref):
        a = _dot(yh_ref[...], wbh_ref[...])
        b = _dot(ya_ref[...], wba_ref[...])
        mixed = (_sigmoid(g_ref[:, :D]) * a + _sigmoid(g_ref[:, D:]) * b).astype(BF16)
        r = _dot(mixed, wo_ref[...])
        x1 = x_ref[...] + g1_ref[...] * r
        a_ref[...] = a
        b_ref[...] = b
        mx_ref[...] = mixed
        r_ref[...] = r
        x1_ref[...] = x1
        h2_ref[...] = _rms_mod(x1, nfw_ref[...], sh_ref[...], sc_ref[...]).astype(BF16)

    row = lambda w: pl.BlockSpec((TM, w), lambda i: (i, 0))
    vec = _full((1, D))
    return pl.pallas_call(
        body, name="merge_fwd", grid=(N_LAT_TILES,),
        in_specs=[row(HGW), row(ATW), row(WC), row(D), _VMEM_WHOLE, _VMEM_WHOLE, _VMEM_WHOLE,
                  vec, vec, vec, vec],
        out_specs=[row(D)] * 6,
        out_shape=[jax.ShapeDtypeStruct((S, D), dt) for dt in (F32, F32, BF16, F32, F32, BF16)],
        compiler_params=_cp(("parallel",)),
    )(y_hg, y_at, p_c, x, w_bh, w_ba, w_out, g1, nfw, sh2, sc2)


def _merge_bwd(dx1, r, a, b, p_c, w_bh, w_ba, w_out, g1):
    def body(dx_ref, r_ref, a_ref, b_ref, g_ref, wbh_ref, wba_ref, wo_ref, g1_ref,
             dr_ref, da_ref, db_ref, dg_ref, dyh_ref, dya_ref, acc_ref):
        @pl.when(pl.program_id(0) == 0)
        def _():
            acc_ref[...] = jnp.zeros_like(acc_ref)

        dx1v = dx_ref[...]
        acc_ref[0:1, :] += jnp.sum(dx1v * r_ref[...], axis=0, keepdims=True)
        dr = (g1_ref[...] * dx1v).astype(BF16)
        dr_ref[...] = dr
        dmix = _dot_nt(dr, wo_ref[...])
        sh, sa = _sigmoid(g_ref[:, :D]), _sigmoid(g_ref[:, D:])
        da = (dmix * sh).astype(BF16)
        db = (dmix * sa).astype(BF16)
        da_ref[...] = da
        db_ref[...] = db
        dg_ref[:, :D] = (dmix * a_ref[...] * sh * (1.0 - sh)).astype(BF16)
        dg_ref[:, D:] = (dmix * b_ref[...] * sa * (1.0 - sa)).astype(BF16)
        dyh_ref[...] = _dot_nt(da, wbh_ref[...])
        dya_ref[...] = _dot_nt(db, wba_ref[...])

    row = lambda w: pl.BlockSpec((TM, w), lambda i: (i, 0))
    return pl.pallas_call(
        body, name="merge_bwd", grid=(N_LAT_TILES,),
        in_specs=[row(D), row(D), row(D), row(D), row(WC), _VMEM_WHOLE, _VMEM_WHOLE, _VMEM_WHOLE,
                  _full((1, D))],
        out_specs=[row(D), row(D), row(D), row(WC), row(HGW), row(ATW), _full((8, D))],
        out_shape=[jax.ShapeDtypeStruct((S, D), BF16), jax.ShapeDtypeStruct((S, D), BF16),
                   jax.ShapeDtypeStruct((S, D), BF16), jax.ShapeDtypeStruct((S, WC), BF16),
                   jax.ShapeDtypeStruct((S, HGW), F32), jax.ShapeDtypeStruct((S, ATW), F32),
                   jax.ShapeDtypeStruct((8, D), F32)],
        compiler_params=_cp(("arbitrary",)),
    )(dx1, r, a, b, p_c, w_bh, w_ba, w_out, g1)


def _ffn_fused(x1, h2, tgt, w_gate, w_up, w_down, g2, nfw, sc2):
    def body(x1_ref, h2_ref, t_ref, wg_ref, wu_ref, wd_ref, g2_ref, nfw_ref, sc_ref,
             act_ref, dgt_ref, dup_ref, df_ref, dx_ref, acc_ref, gs, us):
        @pl.when(pl.program_id(0) == 0)
        def _():
            acc_ref[...] = jnp.zeros_like(acc_ref)

        h2 = h2_ref[...]
        f = jnp.zeros((TM, D), F32)
        for j in range(N_DEV):
            g = _dot(h2, wg_ref[j])
            u = _dot(h2, wu_ref[j])
            gs[j] = g
            us[j] = u
            act = (g * _sigmoid(g) * u).astype(BF16)
            act_ref[j] = act
            f = f + _dot(act, wd_ref[j])
        x1v = x1_ref[...]
        g2 = g2_ref[...]
        diff = x1v + g2 * f - t_ref[...]
        dy = diff * (1.0 / D)
        df = (g2 * dy).astype(BF16)
        df_ref[...] = df
        dh2 = jnp.zeros((TM, D), F32)
        for j in range(N_DEV):
            g, u = gs[j], us[j]
            sg = _sigmoid(g)
            dact = _dot_nt(df, wd_ref[j])
            dgate = (dact * u * (sg * (1.0 + g * (1.0 - sg)))).astype(BF16)
            dup = (dact * (g * sg)).astype(BF16)
            dgt_ref[j] = dgate
            dup_ref[j] = dup
            dh2 = dh2 + _dot_nt(dgate, wg_ref[j]) + _dot_nt(dup, wu_ref[j])
        dx, dsh, dsc, dnw = _rms_mod_bwd(x1v, nfw_ref[...], sc_ref[...], dh2)
        dx_ref[...] = dy + dx
        acc_ref[0:1, :] += dsh
        acc_ref[1:2, :] += dsc
        acc_ref[2:3, :] += dnw
        acc_ref[3:4, :] += jnp.sum(dy * f, axis=0, keepdims=True)
        acc_ref[4:5, :] += 0.5 * jnp.sum(jnp.sum(diff * diff, axis=1, keepdims=True), axis=0,
                                         keepdims=True) * (1.0 / D)

    row = lambda dt_w: pl.BlockSpec((TM, dt_w), lambda i: (i, 0))
    blk = pl.BlockSpec((N_DEV, TM, FF_BLK), lambda i: (0, i, 0))
    vec = _full((1, D))
    return pl.pallas_call(
        body, name="ffn_fused", grid=(N_LAT_TILES,),
        in_specs=[row(D), row(D), row(D), _VMEM_WHOLE, _VMEM_WHOLE, _VMEM_WHOLE, vec, vec, vec],
        out_specs=[blk, blk, blk, row(D), row(D), _full((8, D))],
        out_shape=[jax.ShapeDtypeStruct((N_DEV, S, FF_BLK), BF16)] * 3
        + [jax.ShapeDtypeStruct((S, D), BF16), jax.ShapeDtypeStruct((S, D), F32),
           jax.ShapeDtypeStruct((8, D), F32)],
        scratch_shapes=[pltpu.VMEM((N_DEV, TM, FF_BLK), F32), pltpu.VMEM((N_DEV, TM, FF_BLK), F32)],
        compiler_params=_cp(("arbitrary",)),
    )(x1, h2, tgt, w_gate, w_up, w_down, g2, nfw, sc2)


def _input_bwd(dp_a, dp_b, dp_c, w_a, w_b, w_c, ctx, x, dx1, nw, sh, sc):
    def body(da_ref, db_ref, dc_ref, wa_ref, wb_ref, wc_ref, ctx_ref, x_ref, dx1_ref, nw_ref, sh_ref,
             sc_ref, gx_ref, acc_ref):
        i = pl.program_id(0)

        @pl.when(i == 0)
        def _():
            acc_ref[...] = jnp.zeros_like(acc_ref)

        dh = _dot_nt(da_ref[...], wa_ref[...]) + _dot_nt(db_ref[...], wb_ref[...])

        @pl.when(i == 0)
        def _():
            _, dsh, dsc, dnw = _rms_mod_bwd(ctx_ref[...], nw_ref[...], sc_ref[0:1, :], dh)
            acc_ref[3:4, :] += dsh
            acc_ref[4:5, :] += dsc
            acc_ref[2:3, :] += dnw

        @pl.when(i > 0)
        def _():
            dhl = dh + _dot_nt(dc_ref[...], wc_ref[...])
            dx, dsh, dsc, dnw = _rms_mod_bwd(x_ref[...], nw_ref[...], sc_ref[1:2, :], dhl)
            gx_ref[...] = dx1_ref[...] + dx
            acc_ref[0:1, :] += dsh
            acc_ref[1:2, :] += dsc
            acc_ref[2:3, :] += dnw

    lat = lambda w: pl.BlockSpec((TM, w), lambda i: (_lat(i), 0))
    return pl.pallas_call(
        body, name="input_bwd", grid=(N_TILES,),
        in_specs=[pl.BlockSpec((TM, WA), lambda i: (i, 0)), pl.BlockSpec((TM, WB), lambda i: (i, 0)),
                  lat(WC), _VMEM_WHOLE, _VMEM_WHOLE, _VMEM_WHOLE, _full((TM, D)), lat(D), lat(D),
                  _full((1, D)), _full((2, D)), _full((2, D))],
        out_specs=[lat(D), _full((8, D))],
        out_shape=[jax.ShapeDtypeStruct((S, D), F32), jax.ShapeDtypeStruct((8, D), F32)],
        compiler_params=_cp(("arbitrary",)),
    )(dp_a, dp_b, dp_c, w_a, w_b, w_c, ctx, x, dx1, nw, sh, sc)


_C1 = 1.0 - ADAM_B1 ** ADAM_STEP
_C2 = 1.0 - ADAM_B2 ** ADAM_STEP


def _adamw_math(w, g, m, v):
    m = ADAM_B1 * m + (1.0 - ADAM_B1) * g
    v = ADAM_B2 * v + (1.0 - ADAM_B2) * (g * g)
    m_hat = m / _C1
    v_hat = v / _C2
    delta = -ADAM_LR * (m_hat / (jnp.sqrt(v_hat) + ADAM_EPS) + ADAM_WD * w)
    return delta, m, v


def _adamw_sharded(terms, w, m, v, name, tr):
    rows, cols = w.shape

    def body(t_ref, w_ref, m_ref, v_ref, g_ref, d_ref, nm_ref, nv_ref):
        g = t_ref[0].astype(F32)
        for s in range(1, N_CHIPS):
            g = g + t_ref[s].astype(F32)
        g_ref[...] = g
        d_ref[...], nm_ref[...], nv_ref[...] = _adamw_math(w_ref[...], g, m_ref[...], v_ref[...])

    blk = pl.BlockSpec((tr, cols), lambda i: (i, 0))
    return pl.pallas_call(
        body, name=name, grid=(rows // tr,),
        in_specs=[pl.BlockSpec((N_CHIPS, tr, cols), lambda i: (0, i, 0)), blk, blk, blk],
        out_specs=[blk] * 4,
        out_shape=[jax.ShapeDtypeStruct((rows, cols), F32)] * 4,
        compiler_params=_cp(("parallel",)),
    )(terms, w, m, v)


def _adamw_plain(g, w, m, v, name):
    def body(g_ref, w_ref, m_ref, v_ref, d_ref, nm_ref, nv_ref):
        d_ref[...], nm_ref[...], nv_ref[...] = _adamw_math(w_ref[...], g_ref[...], m_ref[...], v_ref[...])

    return pl.pallas_call(
        body, name=name, in_specs=[_VMEM_WHOLE] * 4, out_specs=[_VMEM_WHOLE] * 3,
        out_shape=[jax.ShapeDtypeStruct(w.shape, F32)] * 3,
        compiler_params=_cp(),
    )(g, w, m, v)


SMALL_ROWS = 16
R_DMOD, R_DCTX, R_NMIX, R_NFFN, R_MISC, R_DLB, R_BADA01 = 0, 6, 8, 9, 10, 11, 13
M_HNW, M_QNW, M_KNW, M_SINK, M_LOSS = 0, 128, 256, 384, 512


def _pack_small(acc_in, acc_mg, acc_ffn, acc_prep, dsink, dlb):
    def body(in_ref, mg_ref, ff_ref, pp_ref, ds_ref, dlb_ref, o_ref):
        o_ref[...] = jnp.zeros_like(o_ref)
        o_ref[0:2, :] = in_ref[0:2, :]
        o_ref[2:3, :] = mg_ref[0:1, :]
        o_ref[3:5, :] = ff_ref[0:2, :]
        o_ref[5:6, :] = ff_ref[3:4, :]
        o_ref[6:8, :] = in_ref[3:5, :]
        o_ref[8:9, :] = in_ref[2:3, :]
        o_ref[9:10, :] = ff_ref[2:3, :]
        o_ref[10:11, M_HNW:M_HNW + HG_DIM] = pp_ref[0:1, 0:HG_DIM]
        r = lax.broadcasted_iota(jnp.int32, (ATW, 128), 0)
        c = lax.broadcasted_iota(jnp.int32, (ATW, 128), 1)
        fold = jnp.where((r % HEAD_DIM == c) & (c < HEAD_DIM), 1.0, 0.0).astype(BF16)
        qk = jnp.concatenate([pp_ref[1:2, :], pp_ref[2:3, :], jnp.zeros((6, ATW), F32)], axis=0)
        folded = _dot_exact_rhs01(qk, fold)
        o_ref[10:11, M_QNW:M_QNW + 128] = folded[0:1, :]
        o_ref[10:11, M_KNW:M_KNW + 128] = folded[1:2, :]
        o_ref[10:11, M_SINK:M_SINK + ATT_HEADS] = ds_ref[...]
        o_ref[10:11, M_LOSS:M_LOSS + 128] = ff_ref[4:5, 0:128]
        o_ref[11:13, 0:HGW] = dlb_ref[...]

    return pl.pallas_call(
        body, name="pack_small", in_specs=[_VMEM_WHOLE] * 6, out_specs=_VMEM_WHOLE,
        out_shape=jax.ShapeDtypeStruct((SMALL_ROWS, D), F32), compiler_params=_cp(),
    )(acc_in, acc_mg, acc_ffn, acc_prep, dsink, dlb)


def _sum_small(gathered):
    def body(g_ref, o_ref):
        tot = g_ref[0]
        for s in range(1, N_DEV):
            tot = tot + g_ref[s]
        o_ref[...] = tot
        o_ref[R_BADA01:R_BADA01 + 2, :] = tot[0:2, :] + tot[R_DCTX:R_DCTX + 2, :]

    return pl.pallas_call(
        body, name="sum_small", in_specs=[_VMEM_WHOLE], out_specs=_VMEM_WHOLE,
        out_shape=jax.ShapeDtypeStruct((SMALL_ROWS, D), F32), compiler_params=_cp(),
    )(gathered)


def _lb_grads(dlb, lbl):
    def body(d_ref, l_ref, o_ref):
        for d in (0, 1):
            ll = l_ref[d]
            lb = _sigmoid(ll[0:1, :] - ll[1:2, :])
            t = d_ref[d:d + 1, :] * lb * (1.0 - lb)
            o_ref[d, 0:1, :] = t
            o_ref[d, 1:2, :] = -t

    return pl.pallas_call(
        body, name="lb_grads", in_specs=[_VMEM_WHOLE] * 2, out_specs=_VMEM_WHOLE,
        out_shape=jax.ShapeDtypeStruct((2, 2, HGW), F32), compiler_params=_cp(),
    )(dlb, lbl)


def _c_ctx_grad(terms, c_ctx):
    def body(t_ref, c_ref, o_ref):
        tot = t_ref[0, 8:9, :]
        for s in range(1, N_DEV):
            tot = tot + t_ref[s, 8:9, :]
        cv = c_ref[...]
        sg = _sigmoid(cv)
        o_ref[...] = tot * (sg * (1.0 + cv * (1.0 - sg)))

    return pl.pallas_call(
        body, name="c_ctx_grad", in_specs=[_VMEM_WHOLE] * 2, out_specs=_VMEM_WHOLE,
        out_shape=jax.ShapeDtypeStruct((1, D), F32), compiler_params=_cp(),
    )(terms, c_ctx)


def _in_perm():
    fz, bz, inp, kk, vv, qhg, ghg, qat, gates = 0, 512, 1024, 1536, 1664, 1792, 2304, 2816, 3328
    cols = []
    for h in range(HG_HEADS):
        for base in (fz, bz, inp, qhg):
            cols += list(range(base + 128 * h, base + 128 * (h + 1)))
    cols += list(range(ghg, ghg + 512)) + list(range(qat, qat + 512))
    cols += list(range(kk, kk + 128)) + list(range(vv, vv + 128))
    cols += list(range(gates, gates + 2048))
    return np.asarray(cols, np.int32)


_PERM = _in_perm()
_INV_PERM = np.argsort(_PERM).astype(np.int32)


def _take_cols(w, perm):
    cuts = [0] + [i for i in range(1, len(perm)) if perm[i] != perm[i - 1] + 1] + [len(perm)]
    return jnp.concatenate([w[:, int(perm[a]):int(perm[b - 1]) + 1] for a, b in zip(cuts[:-1], cuts[1:])],
                           axis=1)


def _cols_from_blocks(g):
    return jnp.transpose(g, (1, 0, 2)).reshape(g.shape[1], N_DEV * g.shape[2])


def _blocks_from_cols(w):
    r, c = w.shape
    return jnp.transpose(w.reshape(r, N_DEV, c // N_DEV), (1, 0, 2))


def _local_step(x2, ctx2, tgt, lbl, sh_in, sc_in, gate1, sh2, sc2, gate2, norm_mix_w, norm_ffn_w,
                hgrn_norm_w, q_norm_w, k_norm_w, attn_sinks, w_a, w_b, w_c, w_bh, w_ba, w_o,
                g_gate, g_up, g_down):
    h_all = _norm_mod_all(ctx2, x2, norm_mix_w, sh_in, sc_in)
    p_a = _mm_nn(h_all, w_a, tm=768, tn=1024, out_dtype=F32, name="proj_a")
    p_b = _mm_nn(h_all, w_b, tm=768, tn=1280, out_dtype=F32, name="proj_b")
    p_c = _mm_nn(h_all, w_c, tm=256, tn=2048, out_dtype=F32, name="proj_c", row_off=1, rows=S)
    o, st = _hgrn_fwd(p_a, lbl)
    cos, sin = _rope_tables()
    qnw_t, knw_t = jnp.tile(q_norm_w, (1, ATT_HEADS)), jnp.tile(k_norm_w, (1, KV_HEADS))
    y_hg, qn, k_rep, v_rep = _prep_fwd(p_b, o, cos, sin, hgrn_norm_w, qnw_t, knw_t)
    y_at, lse = _attn_fwd(qn, k_rep, v_rep, attn_sinks)
    a, b, mixed, r, x1, h2 = _merge_fwd(y_hg, y_at, p_c, x2, w_bh, w_ba, w_o, gate1, norm_ffn_w, sh2, sc2)

    act, d_gate, d_up, d_f, dx1, acc_ffn = _ffn_fused(x1, h2, tgt, g_gate, g_up, g_down, gate2,
                                                      norm_ffn_w, sc2)
    t_down = _mm_tn_blocked(act, d_f, a_blocked=True, name="grad_down")
    t_gate = _mm_tn_blocked(h2, d_gate, a_blocked=False, name="grad_gate")
    t_up = _mm_tn_blocked(h2, d_up, a_blocked=False, name="grad_up")

    d_r, d_a, d_b, dp_c, dy_hg, dy_at, acc_mg = _merge_bwd(dx1, r, a, b, p_c, w_bh, w_ba, w_o, gate1)
    t_out = _mm_tn(mixed, d_r, tk=512, nk=4, a_off=0, tm=512, tn=1024, out_dtype=BF16, name="grad_out")
    t_bh = _mm_tn(y_hg, d_a, tk=512, nk=4, a_off=0, tm=512, tn=1024, out_dtype=BF16, name="grad_bh")
    t_ba = _mm_tn(y_at, d_b, tk=512, nk=4, a_off=0, tm=512, tn=1024, out_dtype=BF16, name="grad_ba")
    dq, dk_rep, dv_rep, dsink = _attn_bwd(qn, k_rep, v_rep, attn_sinks, y_at, lse, dy_at)
    dp_b, d_o, acc_prep = _prep_bwd(p_b, o, cos, sin, hgrn_norm_w, qnw_t, knw_t,
                                    dy_hg, dq, dk_rep, dv_rep)
    dp_a, dlb = _hgrn_bwd(p_a, lbl, d_o, st)
    grad_x, acc_in = _input_bwd(dp_a, dp_b, dp_c, w_a, w_b, w_c, ctx2, x2, dx1, norm_mix_w, sh_in, sc_in)
    t_a = _mm_tn(h_all, dp_a, tk=768, nk=3, a_off=0, tm=512, tn=1024, out_dtype=BF16, name="grad_in_a")
    t_b = _mm_tn(h_all, dp_b, tk=768, nk=3, a_off=0, tm=512, tn=1280, out_dtype=BF16, name="grad_in_b")
    t_c = _mm_tn(h_all, dp_c, tk=256, nk=8, a_off=1, tm=512, tn=1024, out_dtype=BF16, name="grad_in_c")
    t_in = _blocks_from_cols(_take_cols(jnp.concatenate([t_a, t_b, t_c], axis=1), _INV_PERM))
    terms = [t_in, _blocks_from_cols(t_bh), _blocks_from_cols(t_ba), t_out.reshape(N_DEV, D // N_DEV, D),
             t_gate, t_up, t_down]
    return grad_x, _pack_small(acc_in, acc_mg, acc_ffn, acc_prep, dsink, dlb), terms


def kernel(x, c, ctx, c_ctx, w_ada, b_ada, norm_mix_w, norm_ffn_w, w_in, hgrn_lb_logits, hgrn_norm_w, q_norm_w, k_norm_w, attn_sinks, w_branch_hgrn, w_branch_attn, w_out, w_ffn_gate, w_ffn_up, w_ffn_down, loss_target, m_c_ctx, m_w_ada, m_b_ada, m_norm_mix_w, m_norm_ffn_w, m_w_in, m_hgrn_lb_logits, m_hgrn_norm_w, m_q_norm_w, m_k_norm_w, m_attn_sinks, m_w_branch_hgrn, m_w_branch_attn, m_w_out, m_w_ffn_gate, m_w_ffn_up, m_w_ffn_down, v_c_ctx, v_w_ada, v_b_ada, v_norm_mix_w, v_norm_ffn_w, v_w_in, v_hgrn_lb_logits, v_hgrn_norm_w, v_q_norm_w, v_k_norm_w, v_attn_sinks, v_w_branch_hgrn, v_w_branch_attn, v_w_out, v_w_ffn_gate, v_w_ffn_up, v_w_ffn_down):
    me = 4 * lax.axis_index("x") + 2 * lax.axis_index("y") + lax.axis_index("c")
    x2, ctx2, tgt = x[0], ctx[0], loss_target[0]
    w_ada2, w_in2 = w_ada[0], w_in[0]

    blk = jnp.zeros((8, D), F32).at[0].set(c[0]).at[1, :256].set(hgrn_lb_logits.reshape(256))
    (g0,) = _all_gather([blk], "gather_cond", True)
    cc = jnp.zeros((16, D), F32).at[:8].set(g0[:, 0, :]).at[8].set(c_ctx)
    lbl = jnp.transpose(g0[:, 1, :256].reshape(N_DEV, 2, 2, 64), (1, 2, 0, 3)).reshape(2, 2, HGW)

    b_cols = lax.dynamic_slice(b_ada, (0, me * ADA_BLK), (1, ADA_BLK))
    (g1,) = _all_gather([_ada_rows(cc, w_ada2, b_cols)], "gather_mod", True)
    mod_all = _cols_from_blocks(g1)
    mod = lax.dynamic_slice(mod_all, (me, 0), (1, 6 * D)).reshape(6, D)
    mod_c = mod_all[8].reshape(6, D)
    sh1, sc1, gate1, sh2, sc2, gate2 = [mod[k:k + 1] for k in range(6)]
    sh_in = jnp.concatenate([mod_c[0:1], sh1], axis=0)
    sc_in = jnp.concatenate([mod_c[1:2], sc1], axis=0)

    shards = [w_in2, w_branch_hgrn[0], w_branch_attn[0], w_out[0], w_ffn_gate[0], w_ffn_up[0],
              w_ffn_down[0]]
    g_in, g_bh, g_ba, g_out, g_gate, g_up, g_down = _all_gather(
        [s.astype(BF16) for s in shards], "gather_weights", False)
    w_in_full = _take_cols(_cols_from_blocks(g_in), _PERM)
    w_a, w_b, w_c = w_in_full[:, :WA], w_in_full[:, WA:WA + WB], w_in_full[:, WA + WB:]
    w_bh, w_ba = _cols_from_blocks(g_bh), _cols_from_blocks(g_ba)
    w_o = g_out.reshape(D, D)

    grad_x, small, terms = _local_step(
        x2, ctx2, tgt, lbl, sh_in, sc_in, gate1, sh2, sc2, gate2, norm_mix_w, norm_ffn_w, hgrn_norm_w,
        q_norm_w, k_norm_w, attn_sinks, w_a, w_b, w_c, w_bh, w_ba, w_o, g_gate, g_up, g_down)

    terms = [t.reshape((N_CHIPS, 2) + t.shape[1:]) for t in terms]
    from_sibling = _sibling_exchange(terms, "scatter_sibling")
    chip_terms = [_pair_sum(t, f, "pair_sum_%d" % a) for a, (t, f) in enumerate(zip(terms, from_sibling))]
    r_in, r_bh, r_ba, r_out, r_gate, r_up, r_down = _chip_exchange(chip_terms, "scatter_chips")
    big = {}
    for nm, rr, ww, mm, vv, tr in (
            ("w_in", r_in, w_in2, m_w_in[0], v_w_in[0], 256),
            ("w_branch_hgrn", r_bh, w_branch_hgrn[0], m_w_branch_hgrn[0], v_w_branch_hgrn[0], 512),
            ("w_branch_attn", r_ba, w_branch_attn[0], m_w_branch_attn[0], v_w_branch_attn[0], 512),
            ("w_out", r_out, w_out[0], m_w_out[0], v_w_out[0], 128),
            ("w_ffn_gate", r_gate, w_ffn_gate[0], m_w_ffn_gate[0], v_w_ffn_gate[0], 256),
            ("w_ffn_up", r_up, w_ffn_up[0], m_w_ffn_up[0], v_w_ffn_up[0], 256),
            ("w_ffn_down", r_down, w_ffn_down[0], m_w_ffn_down[0], v_w_ffn_down[0], 352)):
        big[nm] = [t[None] for t in _adamw_sharded(rr, ww, mm, vv, "adamw_" + nm, tr)]

    (g2,) = _all_gather([small], "gather_small", True)
    tot = _sum_small(g2)
    dm = jnp.zeros((16, 6 * D), F32).at[:8].set(g2[:, R_DMOD:R_DMOD + 6, :].reshape(N_DEV, 6 * D))
    dm = dm.at[8, :2 * D].set(tot[R_DCTX:R_DCTX + 2].reshape(2 * D))
    dm_cols = lax.dynamic_slice(dm, (0, me * ADA_BLK), (16, ADA_BLK))
    g_w_ada, dsc_term = _ada_grads(cc, dm_cols, w_ada2)
    (g3,) = _all_gather([dsc_term], "gather_cctx", True)
    g_c_ctx = _c_ctx_grad(g3, c_ctx.reshape(1, D)).reshape(D)
    g_lbl = _lb_grads(tot[R_DLB:R_DLB + 2, :HGW], lbl)
    g_lb_mine = lax.dynamic_slice(g_lbl, (0, 0, me * 64), (2, 2, 64))
    misc = tot[R_MISC]
    loss = misc[M_LOSS]

    def pack_rep(bada, cctx, nmix, nffn, hnw, qnw, knw, snk):
        last = jnp.zeros((D,), F32).at[0:128].set(hnw.reshape(128)).at[128:192].set(qnw.reshape(64))
        last = last.at[256:320].set(knw.reshape(64)).at[384:392].set(snk.reshape(8))
        rows = jnp.concatenate([bada.reshape(6, D), cctx.reshape(1, D), nmix.reshape(1, D),
                                nffn.reshape(1, D), last[None]], axis=0)
        return jnp.concatenate([rows, jnp.zeros((6, D), F32)], axis=0)

    g_b_ada = jnp.concatenate([tot[R_BADA01:R_BADA01 + 2], tot[2:6]], axis=0)
    g_rep = pack_rep(g_b_ada, g_c_ctx, tot[R_NMIX], tot[R_NFFN],
                     misc[M_HNW:M_HNW + 128], misc[M_QNW:M_QNW + 64], misc[M_KNW:M_KNW + 64],
                     misc[M_SINK:M_SINK + 8])
    w_rep = pack_rep(b_ada, c_ctx, norm_mix_w, norm_ffn_w, hgrn_norm_w, q_norm_w, k_norm_w, attn_sinks)
    m_rep = pack_rep(m_b_ada, m_c_ctx, m_norm_mix_w, m_norm_ffn_w, m_hgrn_norm_w, m_q_norm_w,
                     m_k_norm_w, m_attn_sinks)
    v_rep_ = pack_rep(v_b_ada, v_c_ctx, v_norm_mix_w, v_norm_ffn_w, v_hgrn_norm_w, v_q_norm_w,
                      v_k_norm_w, v_attn_sinks)
    d_rep, nm_rep, nv_rep = _adamw_plain(g_rep, w_rep, m_rep, v_rep_, "adamw_small")

    def unpack_rep(t):
        last = t[9]
        return {"b_ada": t[0:6].reshape(1, 6 * D), "c_ctx": t[6], "norm_mix_w": t[7:8],
                "norm_ffn_w": t[8:9], "hgrn_norm_w": last[None, 0:128], "q_norm_w": last[None, 128:192],
                "k_norm_w": last[None, 256:320], "attn_sinks": last[None, 384:392]}

    rep = [unpack_rep(t) for t in (g_rep, d_rep, nm_rep, nv_rep)]

    d_ada, nm_ada, nv_ada = _adamw_plain(g_w_ada, w_ada2, m_w_ada[0], v_w_ada[0], "adamw_w_ada")
    ada = [t[None] for t in (g_w_ada, d_ada, nm_ada, nv_ada)]
    lb_w = hgrn_lb_logits.reshape(4, 64)
    d_lb, nm_lb, nv_lb = _adamw_plain(g_lb_mine.reshape(4, 64), lb_w, m_hgrn_lb_logits.reshape(4, 64),
                                      v_hgrn_lb_logits.reshape(4, 64), "adamw_lb")
    lbs = [t.reshape(2, 2, 64) for t in (g_lb_mine, d_lb, nm_lb, nv_lb)]

    names = ['c_ctx', 'w_ada', 'b_ada', 'norm_mix_w', 'norm_ffn_w', 'w_in', 'hgrn_lb_logits', 'hgrn_norm_w',
             'q_norm_w', 'k_norm_w', 'attn_sinks', 'w_branch_hgrn', 'w_branch_attn', 'w_out', 'w_ffn_gate',
             'w_ffn_up', 'w_ffn_down']
    outs = [loss, grad_x[None]]
    for kind in range(4):
        for nm in names:
            if nm == 'w_ada':
                outs.append(ada[kind])
            elif nm == 'hgrn_lb_logits':
                outs.append(lbs[kind])
            elif nm in big:
                outs.append(big[nm][kind])
            else:
                outs.append(rep[kind][nm])
    return tuple(outs)
```

```python
import functools
import math

import numpy as np
import jax
import jax.numpy as jnp
from jax import lax
from jax.experimental import pallas as pl
from jax.experimental.pallas import tpu as pltpu

F32 = jnp.float32
BF16 = jnp.bfloat16

N_DEV = 8
D = 1024
S = 2048
L = 256
T = L + S
TM = 256
N_TILES = T // TM
N_LAT_TILES = S // TM
HG_HEADS = 4
HG_DIM = 128
HGW = 512
CHUNK = 32
N_CHUNKS = T // CHUNK
N_CTX_CHUNKS = L // CHUNK
N_LAT_CHUNKS = S // CHUNK
ATT_HEADS = 8
KV_HEADS = 2
HEAD_DIM = 64
ATW = 512
KVW = 128
BLOCK = 128
N_BLOCKS = S // BLOCK
GRID_W = 64
ROPE_THETA = 10000.0
D_FF = 2816
FF_BLK = D_FF // N_DEV
IN_COLS = 5376
IN_BLK = IN_COLS // N_DEV
ADA_BLK = 6 * D // N_DEV
EPS = 1e-6
WA, WB, WC = 2048, 1280, 2048

ADAM_LR = 0.001
ADAM_B1 = 0.9
ADAM_B2 = 0.999
ADAM_EPS = 1e-08
ADAM_WD = 0.01
ADAM_STEP = 10

VMEM_LIMIT = 56 * 1024 * 1024
MESH = pl.DeviceIdType.MESH


def _cp(sem=None, vmem=VMEM_LIMIT):
    return pltpu.CompilerParams(dimension_semantics=sem, vmem_limit_bytes=vmem)


def _full(shape):
    n = len(shape)
    return pl.BlockSpec(shape, lambda *_: (0,) * n)


_VMEM_WHOLE = pl.BlockSpec(memory_space=pltpu.VMEM)
_ANY = pl.BlockSpec(memory_space=pl.ANY)


def _sigmoid(v):
    return 1.0 / (1.0 + jnp.exp(-v))


def _dot(a, b):
    return jnp.dot(a, b, preferred_element_type=F32)


def _dot_nt(a, b):
    return lax.dot_general(a, b, (((1,), (1,)), ((), ())), preferred_element_type=F32)


def _dot_tn(a, b):
    return lax.dot_general(a, b, (((0,), (0,)), ((), ())), preferred_element_type=F32)


def _split3(v):
    hi = v.astype(BF16)
    r = v - hi.astype(F32)
    mid = r.astype(BF16)
    lo = (r - mid.astype(F32)).astype(BF16)
    return hi, mid, lo


def _dot_exact_rhs01(v, m01):
    hi, mid, lo = _split3(v)
    return _dot(hi, m01) + _dot(mid, m01) + _dot(lo, m01)


def _dot_exact_lhs01(m01, v):
    hi, mid, lo = _split3(v)
    return _dot(m01, hi) + _dot(m01, mid) + _dot(m01, lo)


def _dot_f32(a, b, dot=_dot):
    ah, am, al = _split3(a)
    bh, bm, bl = _split3(b)
    return (dot(ah, bh) + (dot(ah, bm) + dot(am, bh))
            + (dot(am, bm) + dot(ah, bl) + dot(al, bh)))


def _my_pos():
    return lax.axis_index("x"), lax.axis_index("y"), lax.axis_index("c")


def _all_gather(blocks, name, in_vmem):
    n = len(blocks)

    def body(*refs):
        ins, outs = refs[:n], refs[n:2 * n]
        send_sems, recv_sems, local_sems = refs[2 * n:]
        x, y, c = _my_pos()
        me, sibling = (x, y, c), (x, y, 1 - c)
        chips = [(1 - x, y), (x, 1 - y), (1 - x, 1 - y)]

        def slot(a, px, py, pc):
            return outs[a].at[4 * px + 2 * py + pc]

        def copy(a, k, block, to, src=None):
            return pltpu.make_async_remote_copy(
                src_ref=slot(a, *block) if src is None else src, dst_ref=slot(a, *block),
                send_sem=send_sems.at[a, k], recv_sem=recv_sems.at[a, k],
                device_id=to, device_id_type=MESH)

        mine = [pltpu.make_async_copy(ins[a], slot(a, *me), local_sems.at[a]) for a in range(n)]
        for cp in mine:
            cp.start()
        first = []
        for a in range(n):
            first.append(copy(a, 0, me, sibling, src=ins[a]))
            first += [copy(a, 1 + j, me, (*chip, c), src=ins[a]) for j, chip in enumerate(chips)]
        for cp in first:
            cp.start()
        passed = []
        for j, chip in enumerate(chips):
            for a in range(n):
                copy(a, 1 + j, (*chip, c), me).wait_recv()
                fwd = copy(a, 4 + j, (*chip, c), sibling)
                fwd.start()
                passed.append(fwd)
        for a in range(n):
            copy(a, 0, sibling, me).wait_recv()
            for j, chip in enumerate(chips):
                copy(a, 4 + j, (*chip, 1 - c), me).wait_recv()
        for cp in first + passed:
            cp.wait_send()
        for cp in mine:
            cp.wait()

    spec = _VMEM_WHOLE if in_vmem else _ANY
    outs = pl.pallas_call(
        body, name=name,
        out_shape=[jax.ShapeDtypeStruct((N_DEV,) + b.shape, b.dtype) for b in blocks],
        in_specs=[spec] * n, out_specs=[spec] * n,
        scratch_shapes=[pltpu.SemaphoreType.DMA((n, 7)), pltpu.SemaphoreType.DMA((n, 7)),
                        pltpu.SemaphoreType.DMA((n,))],
    )(*blocks)
    return outs


N_CHIPS = 4


def _sibling_exchange(contribs, name):
    n = len(contribs)

    def body(*refs):
        ins, outs = refs[:n], refs[n:2 * n]
        send_sems, recv_sems = refs[2 * n:]
        x, y, c = _my_pos()
        copies = [pltpu.make_async_remote_copy(
            src_ref=ins[a].at[pl.ds(0, N_CHIPS), 1 - c], dst_ref=outs[a],
            send_sem=send_sems.at[a], recv_sem=recv_sems.at[a],
            device_id=(x, y, 1 - c), device_id_type=MESH) for a in range(n)]
        for cp in copies:
            cp.start()
        for cp in copies:
            cp.wait_recv()
        for cp in copies:
            cp.wait_send()

    return pl.pallas_call(
        body, name=name,
        out_shape=[jax.ShapeDtypeStruct((N_CHIPS,) + b.shape[2:], b.dtype) for b in contribs],
        in_specs=[_ANY] * n, out_specs=[_ANY] * n,
        scratch_shapes=[pltpu.SemaphoreType.DMA((n,)), pltpu.SemaphoreType.DMA((n,))],
    )(*contribs)


def _pair_sum(mine, theirs, name):
    _, _, rows, cols = mine.shape

    def body(m_ref, t_ref, o_ref):
        c = lax.axis_index("c")
        o_ref[...] = (m_ref[c].astype(F32) + t_ref[...].astype(F32)).astype(BF16)

    return pl.pallas_call(
        body, name=name, grid=(N_CHIPS,),
        in_specs=[pl.BlockSpec((None, 2, rows, cols), lambda q: (q, 0, 0, 0)),
                  pl.BlockSpec((None, rows, cols), lambda q: (q, 0, 0))],
        out_specs=pl.BlockSpec((None, rows, cols), lambda q: (q, 0, 0)),
        out_shape=jax.ShapeDtypeStruct((N_CHIPS, rows, cols), BF16),
        compiler_params=_cp(("parallel",)),
    )(mine, theirs)


def _chip_exchange(sums, name):
    n = len(sums)

    def body(*refs):
        ins, outs = refs[:n], refs[n:2 * n]
        send_sems, recv_sems, local_sems = refs[2 * n:]
        x, y, c = _my_pos()
        q_me = 2 * x + y
        chips = [(1 - x, y), (x, 1 - y), (1 - x, 1 - y)]
        mine = [pltpu.make_async_copy(ins[a].at[q_me], outs[a].at[q_me], local_sems.at[a]) for a in range(n)]
        for cp in mine:
            cp.start()
        copies = []
        for j, (px, py) in enumerate(chips):
            for a in range(n):
                cp = pltpu.make_async_remote_copy(
                    src_ref=ins[a].at[2 * px + py], dst_ref=outs[a].at[q_me],
                    send_sem=send_sems.at[a, j], recv_sem=recv_sems.at[a, j],
                    device_id=(px, py, c), device_id_type=MESH)
                cp.start()
                copies.append((cp, a, j, 2 * px + py))
        for cp, a, j, q in copies:
            pltpu.make_async_remote_copy(
                src_ref=ins[a].at[q], dst_ref=outs[a].at[q],
                send_sem=send_sems.at[a, j], recv_sem=recv_sems.at[a, j],
                device_id=(x, y, c), device_id_type=MESH).wait_recv()
        for cp, a, j, q in copies:
            cp.wait_send()
        for cp in mine:
            cp.wait()

    return pl.pallas_call(
        body, name=name,
        out_shape=[jax.ShapeDtypeStruct(b.shape, b.dtype) for b in sums],
        in_specs=[_ANY] * n, out_specs=[_ANY] * n,
        scratch_shapes=[pltpu.SemaphoreType.DMA((n, 3)), pltpu.SemaphoreType.DMA((n, 3)),
                        pltpu.SemaphoreType.DMA((n,))],
    )(*sums)


def _mm_nt(a, bt, *, tm, tn, out_dtype, name, row_off=0, rows=None):
    rows = a.shape[0] if rows is None else rows
    n, k = bt.shape

    def body(a_ref, b_ref, o_ref):
        o_ref[...] = _dot_nt(a_ref[...], b_ref[...]).astype(out_dtype)

    return pl.pallas_call(
        body, name=name, grid=(rows // tm, n // tn),
        in_specs=[pl.BlockSpec((tm, k), lambda i, j: (i + row_off, 0)),
                  pl.BlockSpec((tn, k), lambda i, j: (j, 0))],
        out_specs=pl.BlockSpec((tm, tn), lambda i, j: (i, j)),
        out_shape=jax.ShapeDtypeStruct((rows, n), out_dtype),
        compiler_params=_cp(("parallel", "parallel")),
    )(a, bt)


def _mm_tn(a, b, *, tk, nk, tm, tn, out_dtype, name, a_off=0, b_off=0):
    m, n = a.shape[1], b.shape[1]

    def body(a_ref, b_ref, o_ref, acc):
        kk = pl.program_id(2)

        @pl.when(kk == 0)
        def _():
            acc[...] = jnp.zeros_like(acc)

        acc[...] += _dot_tn(a_ref[...], b_ref[...])

        @pl.when(kk == nk - 1)
        def _():
            o_ref[...] = acc[...].astype(out_dtype)

    return pl.pallas_call(
        body, name=name, grid=(m // tm, n // tn, nk),
        in_specs=[pl.BlockSpec((tk, tm), lambda i, j, kk: (kk + a_off, i)),
                  pl.BlockSpec((tk, tn), lambda i, j, kk: (kk + b_off, j))],
        out_specs=pl.BlockSpec((tm, tn), lambda i, j, kk: (i, j)),
        out_shape=jax.ShapeDtypeStruct((m, n), out_dtype),
        scratch_shapes=[pltpu.VMEM((tm, tn), F32)],
        compiler_params=_cp(("parallel", "parallel", "arbitrary")),
    )(a, b)


def _mm_tn_blocked(a, b, name):
    w, n = a.shape[2], b.shape[1]

    def body(a_ref, b_ref, o_ref):
        o_ref[...] = _dot_tn(a_ref[...], b_ref[...]).astype(BF16)

    return pl.pallas_call(
        body, name=name, grid=(N_DEV,),
        in_specs=[pl.BlockSpec((None, S, w), lambda j: (j, 0, 0)), _full((S, n))],
        out_specs=pl.BlockSpec((None, w, n), lambda j: (j, 0, 0)),
        out_shape=jax.ShapeDtypeStruct((N_DEV, w, n), BF16),
        compiler_params=_cp(("parallel",)),
    )(a, b)


def _ada_rows(cc, w_ada, b_cols):
    def body(c_ref, w_ref, b_ref, o_ref):
        cv = c_ref[...]
        o_ref[...] = _dot_f32(cv * _sigmoid(cv), w_ref[...]) + b_ref[...]

    return pl.pallas_call(
        body, name="ada_rows",
        in_specs=[_VMEM_WHOLE] * 3, out_specs=_VMEM_WHOLE,
        out_shape=jax.ShapeDtypeStruct((16, ADA_BLK), F32),
        compiler_params=_cp(),
    )(cc, w_ada, b_cols)


def _ada_grads(cc, dm_cols, w_ada):
    def body(c_ref, dm_ref, w_ref, gw_ref, dsc_ref):
        cv = c_ref[...]
        sc = cv * _sigmoid(cv)
        dm = dm_ref[...]
        gw_ref[...] = _dot_f32(sc, dm, dot=_dot_tn)
        dsc_ref[...] = _dot_f32(dm, w_ref[...], dot=_dot_nt)

    return pl.pallas_call(
        body, name="ada_grads",
        in_specs=[_VMEM_WHOLE] * 3, out_specs=[_VMEM_WHOLE] * 2,
        out_shape=[jax.ShapeDtypeStruct((D, ADA_BLK), F32), jax.ShapeDtypeStruct((16, D), F32)],
        compiler_params=_cp(),
    )(cc, dm_cols, w_ada)


def _lat(i):
    return jnp.maximum(i - 1, 0)


def _rms_mod(xv, nw, sh, sc):
    rstd = lax.rsqrt(jnp.mean(xv * xv, axis=-1, keepdims=True) + EPS)
    return (xv * rstd * nw) * (1.0 + sc) + sh


def _rms_mod_bwd(xv, nw, sc, dh):
    rstd = lax.rsqrt(jnp.mean(xv * xv, axis=-1, keepdims=True) + EPS)
    xhat = xv * rstd
    dn = dh * (1.0 + sc)
    dxhat = dn * nw
    dx = rstd * (dxhat - xhat * jnp.mean(dxhat * xhat, axis=-1, keepdims=True))
    return (dx, jnp.sum(dh, axis=0, keepdims=True), jnp.sum(dh * (xhat * nw), axis=0, keepdims=True),
            jnp.sum(dn * xhat, axis=0, keepdims=True))


def _norm_mod_all(ctx, x, nw, sh, sc):
    def body(ctx_ref, x_ref, nw_ref, sh_ref, sc_ref, o_ref):
        i = pl.program_id(0)
        sel = jnp.minimum(i, 1)
        xv = jnp.where(i == 0, ctx_ref[...], x_ref[...])
        o_ref[...] = _rms_mod(xv, nw_ref[...], sh_ref[pl.ds(sel, 1), :], sc_ref[pl.ds(sel, 1), :]).astype(BF16)

    return pl.pallas_call(
        body, name="norm_mod", grid=(N_TILES,),
        in_specs=[_full((TM, D)), pl.BlockSpec((TM, D), lambda i: (_lat(i), 0)),
                  _full((1, D)), _full((2, D)), _full((2, D))],
        out_specs=pl.BlockSpec((TM, D), lambda i: (i, 0)),
        out_shape=jax.ShapeDtypeStruct((T, D), BF16),
        compiler_params=_cp(("parallel",)),
    )(ctx, x, nw, sh, sc)


def _chunk_masks(reverse):
    row = lax.broadcasted_iota(jnp.int32, (TM, TM), 0)
    col = lax.broadcasted_iota(jnp.int32, (TM, TM), 1)
    same = (row // CHUNK) == (col // CHUNK)
    tri = same & ((col >= row) if reverse else (col <= row))
    return same, tri


def _chunk_order(i, reverse):
    if not reverse:
        return i
    return jnp.where(i < N_CTX_CHUNKS, N_CTX_CHUNKS - 1 - i, N_CHUNKS + N_CTX_CHUNKS - 1 - i)


def _decay_terms(z, lb, same01, tri01):
    f = lb + (1.0 - lb) * _sigmoid(z)
    g = jnp.log(f)
    hi, mid, lo = _split3(g)
    g3 = jnp.concatenate([hi, mid, lo], axis=1)
    b3 = _dot(tri01, g3)
    t3 = _dot(same01, g3)
    b = b3[:, :HG_DIM] + b3[:, HG_DIM:2 * HG_DIM] + b3[:, 2 * HG_DIM:]
    bt = t3[:, :HG_DIM] + t3[:, HG_DIM:2 * HG_DIM] + t3[:, 2 * HG_DIM:]
    return f, 1.0 - f, b, bt


def _chunk_outer(a, b):
    n = TM // CHUNK
    return jnp.einsum('ncv,nck->nvk', a.reshape(n, CHUNK, HG_DIM), b.reshape(n, CHUNK, HG_DIM),
                      preferred_element_type=F32)


def _hgrn_fwd(p_a, lbl):
    def body(p_ref, lbl_ref, o_ref, st_ref, qd_s, kd_s, u_s, v_s, ebt_s):
        for d in (0, 1):
            same, tri = _chunk_masks(d == 1)
            same01 = jnp.where(same, 1.0, 0.0).astype(BF16)
            tri01 = jnp.where(tri, 1.0, 0.0).astype(BF16)
            ll = lbl_ref[d]
            lb = _sigmoid(ll[0:1, :] - ll[1:2, :])

            def prep(r, carry):
                r0 = pl.multiple_of(r * TM, TM)
                z = p_ref[pl.ds(r0, TM), d * HG_DIM:(d + 1) * HG_DIM]
                _, k, b, bt = _decay_terms(z, lb, same01, tri01)
                vb = p_ref[pl.ds(r0, TM), 2 * HG_DIM:3 * HG_DIM].astype(BF16)
                u_s[pl.ds(r * (TM // CHUNK), TM // CHUNK)] = _chunk_outer(vb, (k * jnp.exp(bt - b)).astype(BF16))
                ebt_s[pl.ds(r0, TM), :] = jnp.exp(bt)
                v_s[pl.ds(r0, TM), :] = vb

                @pl.when(r >= 1)
                def _():
                    rl = pl.multiple_of(r0 - L, TM)
                    qr = p_ref[pl.ds(r0, TM), 3 * HG_DIM:4 * HG_DIM]
                    q = qr * _sigmoid(qr) * HG_DIM ** -0.5
                    qd_s[pl.ds(rl, TM), :] = (q * jnp.exp(b)).astype(BF16)
                    kd_s[pl.ds(rl, TM), :] = (k * jnp.exp(-b)).astype(BF16)

                return carry

            lax.fori_loop(0, N_TILES, prep, 0)

            def scan(i, st):
                nn = _chunk_order(i, d == 1)
                c0 = pl.multiple_of(nn * CHUNK, CHUNK)
                st_ref[d, nn] = st.astype(BF16)
                return st * ebt_s[pl.ds(c0, 1), :] + u_s[nn]

            lax.fori_loop(0, N_CHUNKS, scan, jnp.zeros((HG_DIM, HG_DIM), F32))

            def outp(r, carry):
                r0 = pl.multiple_of(r * TM, TM)
                qd = qd_s[pl.ds(r0, TM), :]
                a = jnp.where(tri, _dot_nt(qd, kd_s[pl.ds(r0, TM), :]), 0.0)
                o = _dot(a.astype(BF16), v_s[pl.ds(r0 + L, TM), :])
                stb = st_ref[d, pl.ds(N_CTX_CHUNKS + r * (TM // CHUNK), TM // CHUNK)]
                inter = jnp.einsum('nck,nvk->ncv', qd.reshape(TM // CHUNK, CHUNK, HG_DIM), stb,
                                   preferred_element_type=F32)
                o = o + inter.reshape(TM, HG_DIM)
                if d == 0:
                    o_ref[pl.ds(r0, TM), :] = o
                else:
                    o_ref[pl.ds(r0, TM), :] += o
                return carry

            lax.fori_loop(0, N_LAT_TILES, outp, 0)

    return pl.pallas_call(
        body, name="hgrn_fwd", grid=(HG_HEADS,),
        in_specs=[pl.BlockSpec((T, 4 * HG_DIM), lambda h: (0, h)),
                  pl.BlockSpec((2, 2, HG_DIM), lambda h: (0, 0, h))],
        out_specs=[pl.BlockSpec((S, HG_DIM), lambda h: (0, h)),
                   pl.BlockSpec((2, None, N_CHUNKS, HG_DIM, HG_DIM), lambda h: (0, h, 0, 0, 0))],
        out_shape=[jax.ShapeDtypeStruct((S, HGW), F32),
                   jax.ShapeDtypeStruct((2, HG_HEADS, N_CHUNKS, HG_DIM, HG_DIM), BF16)],
        scratch_shapes=[pltpu.VMEM((S, HG_DIM), BF16), pltpu.VMEM((S, HG_DIM), BF16),
                        pltpu.VMEM((N_CHUNKS, HG_DIM, HG_DIM), F32), pltpu.VMEM((T, HG_DIM), BF16),
                        pltpu.VMEM((T, HG_DIM), F32)],
        compiler_params=_cp(("parallel",)),
    )(p_a, lbl)


def _hgrn_bwd(p_a, lbl, d_o, st):
    cpt = TM // CHUNK

    def body(p_ref, lbl_ref, do_ref, st_ref, dp_ref, dlb_ref,
             b_s, bt_s, dbt_s, db_t, dk_t, dv_s, dq_s, qd_s, dst_s, w_s):
        dv_s[...] = jnp.zeros_like(dv_s)
        dq_s[...] = jnp.zeros_like(dq_s)
        for d in (0, 1):
            same, tri = _chunk_masks(d == 1)
            _, tri_rev = _chunk_masks(d == 0)
            same01 = jnp.where(same, 1.0, 0.0).astype(BF16)
            tri01 = jnp.where(tri, 1.0, 0.0).astype(BF16)
            later01 = jnp.where(tri_rev, 1.0, 0.0).astype(BF16)
            dbt_s[...] = jnp.zeros_like(dbt_s)
            ll = lbl_ref[d]
            lb = _sigmoid(ll[0:1, :] - ll[1:2, :])

            def prep(r, carry):
                r0 = pl.multiple_of(r * TM, TM)
                z = p_ref[pl.ds(r0, TM), d * HG_DIM:(d + 1) * HG_DIM]
                _, _, b, bt = _decay_terms(z, lb, same01, tri01)
                b_s[pl.ds(r0, TM), :] = b
                bt_s[pl.ds(r0, TM), :] = bt

                @pl.when(r >= 1)
                def _():
                    rl = pl.multiple_of(r0 - L, TM)
                    qr = p_ref[pl.ds(r0, TM), 3 * HG_DIM:4 * HG_DIM]
                    q = qr * _sigmoid(qr) * HG_DIM ** -0.5
                    qd = (q * jnp.exp(b)).astype(BF16)
                    qd_s[pl.ds(rl, TM), :] = qd
                    w_s[pl.ds(r * cpt, cpt)] = _chunk_outer(do_ref[pl.ds(rl, TM), :].astype(BF16), qd)

                return carry

            w_s[pl.ds(0, N_CTX_CHUNKS)] = jnp.zeros((N_CTX_CHUNKS, HG_DIM, HG_DIM), F32)
            lax.fori_loop(0, N_TILES, prep, 0)

            def rscan(j, dst):
                i = N_CHUNKS - 1 - j
                nn = _chunk_order(i, d == 1)
                c0 = pl.multiple_of(nn * CHUNK, CHUNK)
                dst_s[nn] = dst.astype(BF16)
                after = st_ref[d, _chunk_order(jnp.minimum(i + 1, N_CHUNKS - 1), d == 1)].astype(F32)
                dbt_s[pl.ds(c0, 1), :] = jnp.sum(after * dst, axis=0, keepdims=True)
                return dst * jnp.exp(bt_s[pl.ds(c0, 1), :]) + w_s[nn]

            lax.fori_loop(0, N_CHUNKS, rscan, jnp.zeros((HG_DIM, HG_DIM), F32))

            def grads(r, dlb):
                r0 = pl.multiple_of(r * TM, TM)
                z = p_ref[pl.ds(r0, TM), d * HG_DIM:(d + 1) * HG_DIM]
                sz = _sigmoid(z)
                f = lb + (1.0 - lb) * sz
                k = 1.0 - f
                b = b_s[pl.ds(r0, TM), :]
                e2 = jnp.exp(bt_s[pl.ds(r0, TM), :] - b)
                vb = p_ref[pl.ds(r0, TM), 2 * HG_DIM:3 * HG_DIM].astype(BF16)
                dstb = dst_s[pl.ds(r * cpt, cpt)]
                kd2 = k * e2
                dkd2 = jnp.einsum('ncv,nvk->nck', vb.reshape(cpt, CHUNK, HG_DIM), dstb,
                                  preferred_element_type=F32).reshape(TM, HG_DIM)
                dv = jnp.einsum('nck,nvk->ncv', kd2.astype(BF16).reshape(cpt, CHUNK, HG_DIM), dstb,
                                preferred_element_type=F32).reshape(TM, HG_DIM)
                dk_t[...] = dkd2 * e2
                db_t[...] = -(kd2 * dkd2)
                dv_s[pl.ds(r0, TM), :] += dv

                @pl.when(r >= 1)
                def _():
                    rl = pl.multiple_of(r0 - L, TM)
                    eb = jnp.exp(b)
                    enb = jnp.exp(-b)
                    qr = p_ref[pl.ds(r0, TM), 3 * HG_DIM:4 * HG_DIM]
                    sq = _sigmoid(qr)
                    qdf = qr * sq * HG_DIM ** -0.5 * eb
                    kdf = k * enb
                    qd = qd_s[pl.ds(rl, TM), :]
                    kd = kdf.astype(BF16)
                    do = do_ref[pl.ds(rl, TM), :].astype(BF16)
                    a = jnp.where(tri, _dot_nt(qd, kd), 0.0).astype(BF16)
                    da = jnp.where(tri, _dot_nt(do, vb), 0.0).astype(BF16)
                    stb = st_ref[d, pl.ds(r * cpt, cpt)]
                    dqd = _dot(da, kd) + jnp.einsum(
                        'ncv,nvk->nck', do.reshape(cpt, CHUNK, HG_DIM), stb,
                        preferred_element_type=F32).reshape(TM, HG_DIM)
                    dkd = _dot_tn(da, qd)
                    dv_s[pl.ds(r0, TM), :] += _dot_tn(a, do)
                    dk_t[...] += dkd * enb
                    db_t[...] += qdf * dqd - kdf * dkd
                    dq_s[pl.ds(rl, TM), :] += dqd * eb * (HG_DIM ** -0.5) * (sq * (1.0 + qr * (1.0 - sq)))

                dg = _dot_exact_lhs01(later01, db_t[...]) + _dot_exact_lhs01(same01, dbt_s[pl.ds(r0, TM), :])
                df = dg / f - dk_t[...]
                dp_ref[pl.ds(r0, TM), d * HG_DIM:(d + 1) * HG_DIM] = (
                    df * (1.0 - lb) * sz * (1.0 - sz)).astype(BF16)
                return dlb + jnp.sum(df * (1.0 - sz), axis=0, keepdims=True)

            dlb_ref[pl.ds(d, 1), :] = lax.fori_loop(0, N_TILES, grads, jnp.zeros((1, HG_DIM), F32))

        dp_ref[:, 2 * HG_DIM:3 * HG_DIM] = dv_s[...].astype(BF16)
        dp_ref[pl.ds(0, L), 3 * HG_DIM:4 * HG_DIM] = jnp.zeros((L, HG_DIM), BF16)
        dp_ref[pl.ds(L, S), 3 * HG_DIM:4 * HG_DIM] = dq_s[...].astype(BF16)

    return pl.pallas_call(
        body, name="hgrn_bwd", grid=(HG_HEADS,),
        in_specs=[pl.BlockSpec((T, 4 * HG_DIM), lambda h: (0, h)),
                  pl.BlockSpec((2, 2, HG_DIM), lambda h: (0, 0, h)),
                  pl.BlockSpec((S, HG_DIM), lambda h: (0, h)),
                  pl.BlockSpec((2, None, N_CHUNKS, HG_DIM, HG_DIM), lambda h: (0, h, 0, 0, 0))],
        out_specs=[pl.BlockSpec((T, 4 * HG_DIM), lambda h: (0, h)),
                   pl.BlockSpec((2, HG_DIM), lambda h: (0, h))],
        out_shape=[jax.ShapeDtypeStruct((T, WA), BF16), jax.ShapeDtypeStruct((2, HGW), F32)],
        scratch_shapes=[pltpu.VMEM((T, HG_DIM), F32), pltpu.VMEM((T, HG_DIM), F32),
                        pltpu.VMEM((T, HG_DIM), F32), pltpu.VMEM((TM, HG_DIM), F32),
                        pltpu.VMEM((TM, HG_DIM), F32),
                        pltpu.VMEM((T, HG_DIM), F32), pltpu.VMEM((S, HG_DIM), F32),
                        pltpu.VMEM((S, HG_DIM), BF16),
                        pltpu.VMEM((N_CHUNKS, HG_DIM, HG_DIM), BF16),
                        pltpu.VMEM((N_CHUNKS, HG_DIM, HG_DIM), F32)],
        compiler_params=_cp(("parallel",)),
    )(p_a, lbl, d_o, st)


def _rope_tables():
    t = np.arange(S)
    inv = ROPE_THETA ** (-np.arange(0, 32, 2, dtype=np.float64) / 32)
    lane = np.arange(64)
    pos = np.where(lane[None, :] < 32, (t // GRID_W)[:, None], (t % GRID_W)[:, None]).astype(np.float64)
    ang = pos * inv[(lane % 32) % 16][None, :]
    sign = np.where((lane % 32) < 16, -1.0, 1.0)[None, :]
    cos = np.tile(np.cos(ang), (1, 2)).astype(np.float32)
    sin = np.tile(np.sin(ang) * sign, (1, 2)).astype(np.float32)
    return jnp.asarray(cos), jnp.asarray(sin)


def _rope_partner(v):
    lane = lax.broadcasted_iota(jnp.int32, (1, 128), 1)
    first = (lane % 32) < 16
    slabs = []
    for j in range(v.shape[1] // 128):
        s = v[:, 128 * j:128 * (j + 1)]
        slabs.append(jnp.where(first, pltpu.roll(s, 112, 1), pltpu.roll(s, 16, 1)))
    return slabs[0] if len(slabs) == 1 else jnp.concatenate(slabs, axis=1)


def _group_ones(width, group):
    r = lax.broadcasted_iota(jnp.int32, (width, width), 0)
    c = lax.broadcasted_iota(jnp.int32, (width, width), 1)
    return jnp.where((r // group) == (c // group), 1.0, 0.0).astype(BF16)


def _group_mean(v, ones01, group):
    hi = v.astype(BF16)
    lo = (v - hi.astype(F32)).astype(BF16)
    return (_dot(hi, ones01) + _dot(lo, ones01)) * (1.0 / group)


def _rep_matrix():
    r = lax.broadcasted_iota(jnp.int32, (KVW, ATW), 0)
    c = lax.broadcasted_iota(jnp.int32, (KVW, ATW), 1)
    return jnp.where(r == HEAD_DIM * (c // 256) + c % HEAD_DIM, 1.0, 0.0).astype(BF16)


def _tile_lanes(v, reps):
    return jnp.concatenate([v] * reps, axis=1)


def _prep_fwd(p_b, o, cos, sin, hnw, qnw, knw):
    def body(p_ref, o_ref, cos_ref, sin_ref, hnw_ref, qnw_ref, knw_ref, y_ref, q_ref, k_ref, v_ref):
        i = pl.program_id(0)
        rep = _rep_matrix()
        ones_k = _group_ones(KVW, HEAD_DIM)
        kr = p_ref[:, 1024:1152]
        krstd = lax.rsqrt(_group_mean(kr * kr, ones_k, HEAD_DIM) + EPS)
        kn = kr * krstd * knw_ref[...]
        v_ref[...] = _dot(p_ref[:, 1152:1280].astype(BF16), rep).astype(BF16)

        @pl.when(i == 0)
        def _():
            k_ref[...] = _dot(kn.astype(BF16), rep).astype(BF16)

        @pl.when(i > 0)
        def _():
            cs, sn = cos_ref[...], sin_ref[...]
            kro = kn * cs + _rope_partner(kn) * sn
            k_ref[...] = _dot(kro.astype(BF16), rep).astype(BF16)
            qr = p_ref[:, 512:1024]
            qrstd = lax.rsqrt(_group_mean(qr * qr, _group_ones(ATW, HEAD_DIM), HEAD_DIM) + EPS)
            qn = qr * qrstd * qnw_ref[...]
            qro = qn * _tile_lanes(cs, 4) + _rope_partner(qn) * _tile_lanes(sn, 4)
            q_ref[...] = (qro * HEAD_DIM ** -0.5).astype(BF16)
            ys = []
            for h in range(HG_HEADS):
                oh = o_ref[:, HG_DIM * h:HG_DIM * (h + 1)]
                gh = p_ref[:, HG_DIM * h:HG_DIM * (h + 1)]
                rstd = lax.rsqrt(jnp.mean(oh * oh, axis=-1, keepdims=True) + EPS)
                ys.append(oh * rstd * hnw_ref[...] * (gh * _sigmoid(gh)))
            y_ref[...] = jnp.concatenate(ys, axis=1).astype(BF16)

    return pl.pallas_call(
        body, name="prep_fwd", grid=(N_TILES,),
        in_specs=[pl.BlockSpec((TM, WB), lambda i: (i, 0)),
                  pl.BlockSpec((TM, HGW), lambda i: (_lat(i), 0)),
                  pl.BlockSpec((TM, 128), lambda i: (_lat(i), 0)),
                  pl.BlockSpec((TM, 128), lambda i: (_lat(i), 0)),
                  _full((1, HG_DIM)), _full((1, ATW)), _full((1, KVW))],
        out_specs=[pl.BlockSpec((TM, HGW), lambda i: (_lat(i), 0)),
                   pl.BlockSpec((TM, ATW), lambda i: (_lat(i), 0)),
                   pl.BlockSpec((TM, ATW), lambda i: (i, 0)),
                   pl.BlockSpec((TM, ATW), lambda i: (i, 0))],
        out_shape=[jax.ShapeDtypeStruct((S, HGW), BF16), jax.ShapeDtypeStruct((S, ATW), BF16),
                   jax.ShapeDtypeStruct((T, ATW), BF16), jax.ShapeDtypeStruct((T, ATW), BF16)],
        compiler_params=_cp(("arbitrary",)),
    )(p_b, o, cos, sin, hnw, qnw, knw)


def _prep_bwd(p_b, o, cos, sin, hnw, qnw, knw, dy_hg, dq, dk_rep, dv_rep):
    def body(p_ref, o_ref, cos_ref, sin_ref, hnw_ref, qnw_ref, knw_ref, dy_ref, dq_ref, dk_ref, dv_ref,
             dp_ref, do_ref, acc_ref):
        i = pl.program_id(0)

        @pl.when(i == 0)
        def _():
            acc_ref[...] = jnp.zeros_like(acc_ref)

        rep = _rep_matrix()
        ones_k = _group_ones(KVW, HEAD_DIM)

        def fold(v):
            hi = v.astype(BF16)
            lo = (v - hi.astype(F32)).astype(BF16)
            return _dot_nt(hi, rep) + _dot_nt(lo, rep)

        kr = p_ref[:, 1024:1152]
        krstd = lax.rsqrt(_group_mean(kr * kr, ones_k, HEAD_DIM) + EPS)
        khat = kr * krstd
        kw = knw_ref[...]
        dkro = fold(dk_ref[...])
        dv = fold(dv_ref[...])

        def k_back(dkn):
            dkhat = dkn * kw
            dkr = krstd * (dkhat - khat * _group_mean(dkhat * khat, ones_k, HEAD_DIM))
            acc_ref[2:3, 0:KVW] += jnp.sum(dkn * khat, axis=0, keepdims=True)
            dp_ref[:, 1024:1152] = dkr.astype(BF16)
            dp_ref[:, 1152:1280] = dv.astype(BF16)

        @pl.when(i == 0)
        def _():
            k_back(dkro)
            dp_ref[:, 0:1024] = jnp.zeros((TM, 1024), BF16)

        @pl.when(i > 0)
        def _():
            cs, sn = cos_ref[...], sin_ref[...]
            k_back(dkro * cs + _rope_partner(dkro * sn))
            ones_q = _group_ones(ATW, HEAD_DIM)
            qr = p_ref[:, 512:1024]
            qrstd = lax.rsqrt(_group_mean(qr * qr, ones_q, HEAD_DIM) + EPS)
            qhat = qr * qrstd
            dqro = dq_ref[...] * HEAD_DIM ** -0.5
            dqn = dqro * _tile_lanes(cs, 4) + _rope_partner(dqro * _tile_lanes(sn, 4))
            dqhat = dqn * qnw_ref[...]
            dqr = qrstd * (dqhat - qhat * _group_mean(dqhat * qhat, ones_q, HEAD_DIM))
            acc_ref[1:2, :] += jnp.sum(dqn * qhat, axis=0, keepdims=True)
            dp_ref[:, 512:1024] = dqr.astype(BF16)
            dws = jnp.zeros((1, HG_DIM), F32)
            for h in range(HG_HEADS):
                sl = slice(HG_DIM * h, HG_DIM * (h + 1))
                oh, gh, dy = o_ref[:, sl], p_ref[:, sl], dy_ref[:, sl]
                rstd = lax.rsqrt(jnp.mean(oh * oh, axis=-1, keepdims=True) + EPS)
                ohat = oh * rstd
                sg = _sigmoid(gh)
                dp_ref[:, sl] = (dy * (ohat * hnw_ref[...]) * (sg * (1.0 + gh * (1.0 - sg)))).astype(BF16)
                dn = dy * (gh * sg)
                dws = dws + jnp.sum(dn * ohat, axis=0, keepdims=True)
                dohat = dn * hnw_ref[...]
                do_ref[:, sl] = rstd * (dohat - ohat * jnp.mean(dohat * ohat, axis=-1, keepdims=True))
            acc_ref[0:1, 0:HG_DIM] += dws

    return pl.pallas_call(
        body, name="prep_bwd", grid=(N_TILES,),
        in_specs=[pl.BlockSpec((TM, WB), lambda i: (i, 0)),
                  pl.BlockSpec((TM, HGW), lambda i: (_lat(i), 0)),
                  pl.BlockSpec((TM, 128), lambda i: (_lat(i), 0)),
                  pl.BlockSpec((TM, 128), lambda i: (_lat(i), 0)),
                  _full((1, HG_DIM)), _full((1, ATW)), _full((1, KVW)),
                  pl.BlockSpec((TM, HGW), lambda i: (_lat(i), 0)),
                  pl.BlockSpec((TM, ATW), lambda i: (_lat(i), 0)),
                  pl.BlockSpec((TM, ATW), lambda i: (i, 0)),
                  pl.BlockSpec((TM, ATW), lambda i: (i, 0))],
        out_specs=[pl.BlockSpec((TM, WB), lambda i: (i, 0)),
                   pl.BlockSpec((TM, HGW), lambda i: (_lat(i), 0)),
                   _full((8, ATW))],
        out_shape=[jax.ShapeDtypeStruct((T, WB), BF16), jax.ShapeDtypeStruct((S, HGW), F32),
                   jax.ShapeDtypeStruct((8, ATW), F32)],
        compiler_params=_cp(("arbitrary",)),
    )(p_b, o, cos, sin, hnw, qnw, knw, dy_hg, dq, dk_rep, dv_rep)


NEG = -1e30
_CTX_BLOCKS = L // BLOCK


def _attn_window_specs():
    prev = pl.BlockSpec((BLOCK, ATW), lambda i: (jnp.maximum(i - 1, 0) + _CTX_BLOCKS, 0))
    own = pl.BlockSpec((BLOCK, ATW), lambda i: (i + _CTX_BLOCKS, 0))
    nxt = pl.BlockSpec((BLOCK, ATW), lambda i: (jnp.minimum(i + 1, N_BLOCKS - 1) + _CTX_BLOCKS, 0))
    return [prev, own, nxt, _full((L, ATW))]


def _attn_valid(i):
    qi = lax.broadcasted_iota(jnp.int32, (4 * BLOCK, 3 * BLOCK), 0) % BLOCK
    kj = lax.broadcasted_iota(jnp.int32, (4 * BLOCK, 3 * BLOCK), 1)
    return ((jnp.abs(kj - BLOCK - qi) <= BLOCK) & ((kj >= BLOCK) | (i > 0))
            & ((kj < 2 * BLOCK) | (i < N_BLOCKS - 1)))


def _stack_heads(qg):
    lane = lax.broadcasted_iota(jnp.int32, (1, 256), 1) // HEAD_DIM
    return jnp.concatenate([jnp.where(lane == g, qg, jnp.zeros_like(qg)) for g in range(4)], axis=0)


def _unstack_heads(v4):
    lane = lax.broadcasted_iota(jnp.int32, (1, 256), 1) // HEAD_DIM
    out = jnp.where(lane == 0, v4[0:BLOCK], 0.0)
    for g in range(1, 4):
        out = out + jnp.where(lane == g, v4[g * BLOCK:(g + 1) * BLOCK], 0.0)
    return out


def _sink_rows(sink_ref, hk):
    return jnp.concatenate(
        [jnp.broadcast_to(sink_ref[0:1, 4 * hk + g:4 * hk + g + 1], (BLOCK, 1)) for g in range(4)], axis=0)


def _attn_fwd(q, k_rep, v_rep, sinks):
    def body(q_ref, kp, ko, kn, kc, vp, vo, vn, vc, sink_ref, y_ref, lse_ref):
        i = pl.program_id(0)
        valid = _attn_valid(i)
        lane8 = lax.broadcasted_iota(jnp.int32, (1, ATT_HEADS), 1)
        lse_out = jnp.zeros((BLOCK, ATT_HEADS), F32)
        for hk in range(KV_HEADS):
            sl = slice(256 * hk, 256 * (hk + 1))
            q4 = _stack_heads(q_ref[:, sl])
            kl = jnp.concatenate([kp[:, sl], ko[:, sl], kn[:, sl]], axis=0)
            vl = jnp.concatenate([vp[:, sl], vo[:, sl], vn[:, sl]], axis=0)
            s_loc = jnp.where(valid, _dot_nt(q4, kl), NEG)
            s_ctx = _dot_nt(q4, kc[:, sl])
            sink = _sink_rows(sink_ref, hk)
            m = jnp.maximum(jnp.maximum(jnp.max(s_loc, axis=1, keepdims=True),
                                        jnp.max(s_ctx, axis=1, keepdims=True)), sink)
            p_loc = jnp.exp(s_loc - m)
            p_ctx = jnp.exp(s_ctx - m)
            den = (jnp.sum(p_loc, axis=1, keepdims=True) + jnp.sum(p_ctx, axis=1, keepdims=True)
                   + jnp.exp(sink - m))
            o4 = (_dot(p_loc.astype(BF16), vl) + _dot(p_ctx.astype(BF16), vc[:, sl])) / den
            y_ref[:, sl] = _unstack_heads(o4).astype(BF16)
            lse4 = m + jnp.log(den)
            for g in range(4):
                lse_out = lse_out + jnp.where(lane8 == 4 * hk + g, lse4[g * BLOCK:(g + 1) * BLOCK], 0.0)
        lse_ref[...] = lse_out

    return pl.pallas_call(
        body, name="attn_fwd", grid=(N_BLOCKS,),
        in_specs=[pl.BlockSpec((BLOCK, ATW), lambda i: (i, 0))] + _attn_window_specs()
        + _attn_window_specs() + [_full((1, ATT_HEADS))],
        out_specs=[pl.BlockSpec((BLOCK, ATW), lambda i: (i, 0)),
                   pl.BlockSpec((BLOCK, ATT_HEADS), lambda i: (i, 0))],
        out_shape=[jax.ShapeDtypeStruct((S, ATW), BF16), jax.ShapeDtypeStruct((S, ATT_HEADS), F32)],
        compiler_params=_cp(("parallel",)),
    )(q, k_rep, k_rep, k_rep, k_rep, v_rep, v_rep, v_rep, v_rep, sinks)


def _attn_bwd(q, k_rep, v_rep, sinks, y_at, lse, dy):
    def body(q_ref, kp, ko, kn, kc, vp, vo, vn, vc, sink_ref, y_ref, lse_ref, dy_ref,
             dq_ref, dk_ref, dv_ref, dsink_ref, dk_acc, dv_acc):
        i = pl.program_id(0)

        @pl.when(i == 0)
        def _():
            dk_acc[...] = jnp.zeros_like(dk_acc)
            dv_acc[...] = jnp.zeros_like(dv_acc)
            dk_ref[pl.ds(0, L), :] = jnp.zeros((L, ATW), F32)
            dv_ref[pl.ds(0, L), :] = jnp.zeros((L, ATW), F32)
            dsink_ref[...] = jnp.zeros_like(dsink_ref)

        valid = _attn_valid(i)
        lane8 = lax.broadcasted_iota(jnp.int32, (1, ATT_HEADS), 1)
        w0 = pl.multiple_of(i * BLOCK, BLOCK)
        dsink = jnp.zeros((1, ATT_HEADS), F32)
        for hk in range(KV_HEADS):
            sl = slice(256 * hk, 256 * (hk + 1))
            q4 = _stack_heads(q_ref[:, sl])
            do4f = _stack_heads(dy_ref[:, sl])
            o4 = _stack_heads(y_ref[:, sl]).astype(F32)
            do4 = do4f.astype(BF16)
            kl = jnp.concatenate([kp[:, sl], ko[:, sl], kn[:, sl]], axis=0)
            vl = jnp.concatenate([vp[:, sl], vo[:, sl], vn[:, sl]], axis=0)
            lse4 = jnp.concatenate(
                [jnp.sum(jnp.where(lane8 == 4 * hk + g, lse_ref[...], 0.0), axis=1, keepdims=True)
                 for g in range(4)], axis=0)
            p_loc = jnp.where(valid, jnp.exp(_dot_nt(q4, kl) - lse4), 0.0)
            p_ctx = jnp.exp(_dot_nt(q4, kc[:, sl]) - lse4)
            delta = jnp.sum(do4f * o4, axis=1, keepdims=True)
            ds_loc = (p_loc * (_dot_nt(do4, vl) - delta)).astype(BF16)
            ds_ctx = (p_ctx * (_dot_nt(do4, vc[:, sl]) - delta)).astype(BF16)
            dq_ref[:, sl] = _unstack_heads(_dot(ds_loc, kl) + _dot(ds_ctx, kc[:, sl]))
            dk_acc[pl.ds(w0, 3 * BLOCK), sl] += _dot_tn(ds_loc, q4)
            dv_acc[pl.ds(w0, 3 * BLOCK), sl] += _dot_tn(p_loc.astype(BF16), do4)
            dk_ref[pl.ds(0, L), sl] += _dot_tn(ds_ctx, q4)
            dv_ref[pl.ds(0, L), sl] += _dot_tn(p_ctx.astype(BF16), do4)
            p_sink = jnp.exp(_sink_rows(sink_ref, hk) - lse4)
            for g in range(4):
                rows = slice(g * BLOCK, (g + 1) * BLOCK)
                dsink = dsink + jnp.where(lane8 == 4 * hk + g,
                                          -jnp.sum(p_sink[rows] * delta[rows], axis=0, keepdims=True), 0.0)
        dsink_ref[...] += dsink

        @pl.when(i == N_BLOCKS - 1)
        def _():
            dk_ref[pl.ds(L, S), :] = dk_acc[pl.ds(BLOCK, S), :]
            dv_ref[pl.ds(L, S), :] = dv_acc[pl.ds(BLOCK, S), :]

    row_q = pl.BlockSpec((BLOCK, ATW), lambda i: (i, 0))
    return pl.pallas_call(
        body, name="attn_bwd", grid=(N_BLOCKS,),
        in_specs=[row_q] + _attn_window_specs() + _attn_window_specs()
        + [_full((1, ATT_HEADS)), row_q, pl.BlockSpec((BLOCK, ATT_HEADS), lambda i: (i, 0)), row_q],
        out_specs=[row_q, _full((T, ATW)), _full((T, ATW)), _full((1, ATT_HEADS))],
        out_shape=[jax.ShapeDtypeStruct((S, ATW), F32), jax.ShapeDtypeStruct((T, ATW), F32),
                   jax.ShapeDtypeStruct((T, ATW), F32), jax.ShapeDtypeStruct((1, ATT_HEADS), F32)],
        scratch_shapes=[pltpu.VMEM((S + 2 * BLOCK, ATW), F32), pltpu.VMEM((S + 2 * BLOCK, ATW), F32)],
        compiler_params=_cp(("arbitrary",)),
    )(q, k_rep, k_rep, k_rep, k_rep, v_rep, v_rep, v_rep, v_rep, sinks, y_at, lse, dy)


def _merge_fwd(y_hg, y_at, p_c, x, w_bh, w_ba, w_out, g1, nfw, sh2, sc2):
    def body(yh_ref, ya_ref, g_ref, x_ref, wbh_ref, wba_ref, wo_ref, g1_ref, nfw_ref, sh_ref, sc_ref,
             a_ref, b_ref, mx_ref, r_ref, x1_ref, h2_ref):
        a = _dot_nt(yh_ref[...], wbh_ref[...])
        b = _dot_nt(ya_ref[...], wba_ref[...])
        mixed = (_sigmoid(g_ref[:, :D]) * a + _sigmoid(g_ref[:, D:]) * b).astype(BF16)
        r = _dot(mixed, wo_ref[...])
        x1 = x_ref[...] + g1_ref[...] * r
        a_ref[...] = a
        b_ref[...] = b
        mx_ref[...] = mixed
        r_ref[...] = r
        x1_ref[...] = x1
        h2_ref[...] = _rms_mod(x1, nfw_ref[...], sh_ref[...], sc_ref[...]).astype(BF16)

    row = lambda w: pl.BlockSpec((TM, w), lambda i: (i, 0))
    vec = _full((1, D))
    return pl.pallas_call(
        body, name="merge_fwd", grid=(N_LAT_TILES,),
        in_specs=[row(HGW), row(ATW), row(WC), row(D), _VMEM_WHOLE, _VMEM_WHOLE, _VMEM_WHOLE,
                  vec, vec, vec, vec],
        out_specs=[row(D)] * 6,
        out_shape=[jax.ShapeDtypeStruct((S, D), dt) for dt in (F32, F32, BF16, F32, F32, BF16)],
        compiler_params=_cp(("parallel",)),
    )(y_hg, y_at, p_c, x, w_bh, w_ba, w_out, g1, nfw, sh2, sc2)


def _merge_bwd(dx1, r, a, b, p_c, w_bh, w_ba, w_out, g1):
    def body(dx_ref, r_ref, a_ref, b_ref, g_ref, wbh_ref, wba_ref, wo_ref, g1_ref,
             dr_ref, da_ref, db_ref, dg_ref, dyh_ref, dya_ref, acc_ref):
        @pl.when(pl.program_id(0) == 0)
        def _():
            acc_ref[...] = jnp.zeros_like(acc_ref)

        dx1v = dx_ref[...]
        acc_ref[0:1, :] += jnp.sum(dx1v * r_ref[...], axis=0, keepdims=True)
        dr = (g1_ref[...] * dx1v).astype(BF16)
        dr_ref[...] = dr
        dmix = _dot_nt(dr, wo_ref[...])
        sh, sa = _sigmoid(g_ref[:, :D]), _sigmoid(g_ref[:, D:])
        da = (dmix * sh).astype(BF16)
        db = (dmix * sa).astype(BF16)
        da_ref[...] = da
        db_ref[...] = db
        dg_ref[:, :D] = (dmix * a_ref[...] * sh * (1.0 - sh)).astype(BF16)
        dg_ref[:, D:] = (dmix * b_ref[...] * sa * (1.0 - sa)).astype(BF16)
        dyh_ref[...] = _dot(da, wbh_ref[...])
        dya_ref[...] = _dot(db, wba_ref[...])

    row = lambda w: pl.BlockSpec((TM, w), lambda i: (i, 0))
    return pl.pallas_call(
        body, name="merge_bwd", grid=(N_LAT_TILES,),
        in_specs=[row(D), row(D), row(D), row(D), row(WC), _VMEM_WHOLE, _VMEM_WHOLE, _VMEM_WHOLE,
                  _full((1, D))],
        out_specs=[row(D), row(D), row(D), row(WC), row(HGW), row(ATW), _full((8, D))],
        out_shape=[jax.ShapeDtypeStruct((S, D), BF16), jax.ShapeDtypeStruct((S, D), BF16),
                   jax.ShapeDtypeStruct((S, D), BF16), jax.ShapeDtypeStruct((S, WC), BF16),
                   jax.ShapeDtypeStruct((S, HGW), F32), jax.ShapeDtypeStruct((S, ATW), F32),
                   jax.ShapeDtypeStruct((8, D), F32)],
        compiler_params=_cp(("arbitrary",)),
    )(dx1, r, a, b, p_c, w_bh, w_ba, w_out, g1)


def _ffn_fused(x1, h2, tgt, w_gate, w_up, w_down, g2, nfw, sc2):
    def body(x1_ref, h2_ref, t_ref, wg_ref, wu_ref, wd_ref, g2_ref, nfw_ref, sc_ref,
             act_ref, dgt_ref, dup_ref, df_ref, dx_ref, acc_ref, gs, us):
        @pl.when(pl.program_id(0) == 0)
        def _():
            acc_ref[...] = jnp.zeros_like(acc_ref)

        h2 = h2_ref[...]
        f = jnp.zeros((TM, D), F32)
        for j in range(N_DEV):
            g = _dot_nt(h2, wg_ref[j])
            u = _dot_nt(h2, wu_ref[j])
            gs[j] = g
            us[j] = u
            act = (g * _sigmoid(g) * u).astype(BF16)
            act_ref[j] = act
            f = f + _dot(act, wd_ref[j])
        x1v = x1_ref[...]
        g2 = g2_ref[...]
        diff = x1v + g2 * f - t_ref[...]
        dy = diff * (1.0 / D)
        df = (g2 * dy).astype(BF16)
        df_ref[...] = df
        dh2 = jnp.zeros((TM, D), F32)
        for j in range(N_DEV):
            g, u = gs[j], us[j]
            sg = _sigmoid(g)
            dact = _dot_nt(df, wd_ref[j])
            dgate = (dact * u * (sg * (1.0 + g * (1.0 - sg)))).astype(BF16)
            dup = (dact * (g * sg)).astype(BF16)
            dgt_ref[j] = dgate
            dup_ref[j] = dup
            dh2 = dh2 + _dot(dgate, wg_ref[j]) + _dot(dup, wu_ref[j])
        dx, dsh, dsc, dnw = _rms_mod_bwd(x1v, nfw_ref[...], sc_ref[...], dh2)
        dx_ref[...] = dy + dx
        acc_ref[0:1, :] += dsh
        acc_ref[1:2, :] += dsc
        acc_ref[2:3, :] += dnw
        acc_ref[3:4, :] += jnp.sum(dy * f, axis=0, keepdims=True)
        acc_ref[4:5, :] += 0.5 * jnp.sum(jnp.sum(diff * diff, axis=1, keepdims=True), axis=0,
                                         keepdims=True) * (1.0 / D)

    row = lambda dt_w: pl.BlockSpec((TM, dt_w), lambda i: (i, 0))
    blk = pl.BlockSpec((N_DEV, TM, FF_BLK), lambda i: (0, i, 0))
    vec = _full((1, D))
    return pl.pallas_call(
        body, name="ffn_fused", grid=(N_LAT_TILES,),
        in_specs=[row(D), row(D), row(D), _VMEM_WHOLE, _VMEM_WHOLE, _VMEM_WHOLE, vec, vec, vec],
        out_specs=[blk, blk, blk, row(D), row(D), _full((8, D))],
        out_shape=[jax.ShapeDtypeStruct((N_DEV, S, FF_BLK), BF16)] * 3
        + [jax.ShapeDtypeStruct((S, D), BF16), jax.ShapeDtypeStruct((S, D), F32),
           jax.ShapeDtypeStruct((8, D), F32)],
        scratch_shapes=[pltpu.VMEM((N_DEV, TM, FF_BLK), F32), pltpu.VMEM((N_DEV, TM, FF_BLK), F32)],
        compiler_params=_cp(("arbitrary",)),
    )(x1, h2, tgt, w_gate, w_up, w_down, g2, nfw, sc2)


def _input_bwd(dp_a, dp_b, dp_c, w_a, w_b, w_c, ctx, x, dx1, nw, sh, sc):
    def body(da_ref, db_ref, dc_ref, wa_ref, wb_ref, wc_ref, ctx_ref, x_ref, dx1_ref, nw_ref, sh_ref,
             sc_ref, gx_ref, acc_ref):
        i = pl.program_id(0)

        @pl.when(i == 0)
        def _():
            acc_ref[...] = jnp.zeros_like(acc_ref)

        dh = _dot(da_ref[...], wa_ref[...]) + _dot(db_ref[...], wb_ref[...])

        @pl.when(i == 0)
        def _():
            _, dsh, dsc, dnw = _rms_mod_bwd(ctx_ref[...], nw_ref[...], sc_ref[0:1, :], dh)
            acc_ref[3:4, :] += dsh
            acc_ref[4:5, :] += dsc
            acc_ref[2:3, :] += dnw

        @pl.when(i > 0)
        def _():
            dhl = dh + _dot(dc_ref[...], wc_ref[...])
            dx, dsh, dsc, dnw = _rms_mod_bwd(x_ref[...], nw_ref[...], sc_ref[1:2, :], dhl)
            gx_ref[...] = dx1_ref[...] + dx
            acc_ref[0:1, :] += dsh
            acc_ref[1:2, :] += dsc
            acc_ref[2:3, :] += dnw

    lat = lambda w: pl.BlockSpec((TM, w), lambda i: (_lat(i), 0))
    return pl.pallas_call(
        body, name="input_bwd", grid=(N_TILES,),
        in_specs=[pl.BlockSpec((TM, WA), lambda i: (i, 0)), pl.BlockSpec((TM, WB), lambda i: (i, 0)),
                  lat(WC), _VMEM_WHOLE, _VMEM_WHOLE, _VMEM_WHOLE, _full((TM, D)), lat(D), lat(D),
                  _full((1, D)), _full((2, D)), _full((2, D))],
        out_specs=[lat(D), _full((8, D))],
        out_shape=[jax.ShapeDtypeStruct((S, D), F32), jax.ShapeDtypeStruct((8, D), F32)],
        compiler_params=_cp(("arbitrary",)),
    )(dp_a, dp_b, dp_c, w_a, w_b, w_c, ctx, x, dx1, nw, sh, sc)


_C1 = 1.0 - ADAM_B1 ** ADAM_STEP
_C2 = 1.0 - ADAM_B2 ** ADAM_STEP


def _adamw_math(w, g, m, v):
    m = ADAM_B1 * m + (1.0 - ADAM_B1) * g
    v = ADAM_B2 * v + (1.0 - ADAM_B2) * (g * g)
    m_hat = m / _C1
    v_hat = v / _C2
    delta = -ADAM_LR * (m_hat / (jnp.sqrt(v_hat) + ADAM_EPS) + ADAM_WD * w)
    return delta, m, v


def _adamw_sharded(terms, w, m, v, name, tr):
    rows, cols = w.shape

    def body(t_ref, w_ref, m_ref, v_ref, g_ref, d_ref, nm_ref, nv_ref):
        g = t_ref[0].astype(F32)
        for s in range(1, N_CHIPS):
            g = g + t_ref[s].astype(F32)
        g_ref[...] = g
        d_ref[...], nm_ref[...], nv_ref[...] = _adamw_math(w_ref[...], g, m_ref[...], v_ref[...])

    blk = pl.BlockSpec((tr, cols), lambda i: (i, 0))
    return pl.pallas_call(
        body, name=name, grid=(rows // tr,),
        in_specs=[pl.BlockSpec((N_CHIPS, tr, cols), lambda i: (0, i, 0)), blk, blk, blk],
        out_specs=[blk] * 4,
        out_shape=[jax.ShapeDtypeStruct((rows, cols), F32)] * 4,
        compiler_params=_cp(("parallel",)),
    )(terms, w, m, v)


def _adamw_plain(g, w, m, v, name):
    def body(g_ref, w_ref, m_ref, v_ref, d_ref, nm_ref, nv_ref):
        d_ref[...], nm_ref[...], nv_ref[...] = _adamw_math(w_ref[...], g_ref[...], m_ref[...], v_ref[...])

    return pl.pallas_call(
        body, name=name, in_specs=[_VMEM_WHOLE] * 4, out_specs=[_VMEM_WHOLE] * 3,
        out_shape=[jax.ShapeDtypeStruct(w.shape, F32)] * 3,
        compiler_params=_cp(),
    )(g, w, m, v)


SMALL_ROWS = 16
R_DMOD, R_DCTX, R_NMIX, R_NFFN, R_MISC, R_DLB, R_BADA01 = 0, 6, 8, 9, 10, 11, 13
M_HNW, M_QNW, M_KNW, M_SINK, M_LOSS = 0, 128, 256, 384, 512


def _pack_small(acc_in, acc_mg, acc_ffn, acc_prep, dsink, dlb):
    def body(in_ref, mg_ref, ff_ref, pp_ref, ds_ref, dlb_ref, o_ref):
        o_ref[...] = jnp.zeros_like(o_ref)
        o_ref[0:2, :] = in_ref[0:2, :]
        o_ref[2:3, :] = mg_ref[0:1, :]
        o_ref[3:5, :] = ff_ref[0:2, :]
        o_ref[5:6, :] = ff_ref[3:4, :]
        o_ref[6:8, :] = in_ref[3:5, :]
        o_ref[8:9, :] = in_ref[2:3, :]
        o_ref[9:10, :] = ff_ref[2:3, :]
        o_ref[10:11, M_HNW:M_HNW + HG_DIM] = pp_ref[0:1, 0:HG_DIM]
        r = lax.broadcasted_iota(jnp.int32, (ATW, 128), 0)
        c = lax.broadcasted_iota(jnp.int32, (ATW, 128), 1)
        fold = jnp.where((r % HEAD_DIM == c) & (c < HEAD_DIM), 1.0, 0.0).astype(BF16)
        qk = jnp.concatenate([pp_ref[1:2, :], pp_ref[2:3, :], jnp.zeros((6, ATW), F32)], axis=0)
        folded = _dot_exact_rhs01(qk, fold)
        o_ref[10:11, M_QNW:M_QNW + 128] = folded[0:1, :]
        o_ref[10:11, M_KNW:M_KNW + 128] = folded[1:2, :]
        o_ref[10:11, M_SINK:M_SINK + ATT_HEADS] = ds_ref[...]
        o_ref[10:11, M_LOSS:M_LOSS + 128] = ff_ref[4:5, 0:128]
        o_ref[11:13, 0:HGW] = dlb_ref[...]

    return pl.pallas_call(
        body, name="pack_small", in_specs=[_VMEM_WHOLE] * 6, out_specs=_VMEM_WHOLE,
        out_shape=jax.ShapeDtypeStruct((SMALL_ROWS, D), F32), compiler_params=_cp(),
    )(acc_in, acc_mg, acc_ffn, acc_prep, dsink, dlb)


def _sum_small(gathered):
    def body(g_ref, o_ref):
        tot = g_ref[0]
        for s in range(1, N_DEV):
            tot = tot + g_ref[s]
        o_ref[...] = tot
        o_ref[R_BADA01:R_BADA01 + 2, :] = tot[0:2, :] + tot[R_DCTX:R_DCTX + 2, :]

    return pl.pallas_call(
        body, name="sum_small", in_specs=[_VMEM_WHOLE], out_specs=_VMEM_WHOLE,
        out_shape=jax.ShapeDtypeStruct((SMALL_ROWS, D), F32), compiler_params=_cp(),
    )(gathered)


_REP_NAMES = ("b_ada", "c_ctx", "norm_mix_w", "norm_ffn_w", "hgrn_norm_w", "q_norm_w", "k_norm_w", "attn_sinks")


def _adamw_replicated(tot, g_c_ctx, ws, ms, vs):
    n = len(_REP_NAMES)

    def body(*refs):
        tot_ref, gc_ref = refs[0], refs[1]
        w_refs, m_refs, v_refs = refs[2:2 + n], refs[2 + n:2 + 2 * n], refs[2 + 2 * n:2 + 3 * n]
        outs = refs[2 + 3 * n:]
        row = lambda r: tot_ref[r:r + 1, :]
        misc = row(R_MISC)
        grads = [jnp.concatenate([row(R_BADA01), row(R_BADA01 + 1)] + [row(k) for k in range(2, 6)], axis=1),
                 gc_ref[...], row(R_NMIX), row(R_NFFN),
                 misc[:, M_HNW:M_HNW + HG_DIM], misc[:, M_QNW:M_QNW + HEAD_DIM],
                 misc[:, M_KNW:M_KNW + HEAD_DIM], misc[:, M_SINK:M_SINK + ATT_HEADS]]
        for k in range(n):
            outs[k][...] = grads[k]
            outs[n + k][...], outs[2 * n + k][...], outs[3 * n + k][...] = _adamw_math(
                w_refs[k][...], grads[k], m_refs[k][...], v_refs[k][...])

    shapes = [jax.ShapeDtypeStruct(w.shape, F32) for w in ws]
    return pl.pallas_call(
        body, name="adamw_replicated", in_specs=[_VMEM_WHOLE] * (2 + 3 * n), out_specs=[_VMEM_WHOLE] * (4 * n),
        out_shape=shapes * 4, compiler_params=_cp(),
    )(tot, g_c_ctx, *ws, *ms, *vs)


def _lb_grads(dlb, lbl):
    def body(d_ref, l_ref, o_ref):
        for d in (0, 1):
            ll = l_ref[d]
            lb = _sigmoid(ll[0:1, :] - ll[1:2, :])
            t = d_ref[d:d + 1, :] * lb * (1.0 - lb)
            o_ref[d, 0:1, :] = t
            o_ref[d, 1:2, :] = -t

    return pl.pallas_call(
        body, name="lb_grads", in_specs=[_VMEM_WHOLE] * 2, out_specs=_VMEM_WHOLE,
        out_shape=jax.ShapeDtypeStruct((2, 2, HGW), F32), compiler_params=_cp(),
    )(dlb, lbl)


def _c_ctx_grad(terms, c_ctx):
    def body(t_ref, c_ref, o_ref):
        tot = t_ref[0, 8:9, :]
        for s in range(1, N_DEV):
            tot = tot + t_ref[s, 8:9, :]
        cv = c_ref[...]
        sg = _sigmoid(cv)
        o_ref[...] = tot * (sg * (1.0 + cv * (1.0 - sg)))

    return pl.pallas_call(
        body, name="c_ctx_grad", in_specs=[_VMEM_WHOLE] * 2, out_specs=_VMEM_WHOLE,
        out_shape=jax.ShapeDtypeStruct((1, D), F32), compiler_params=_cp(),
    )(terms, c_ctx)


def _in_perm():
    fz, bz, inp, kk, vv, qhg, ghg, qat, gates = 0, 512, 1024, 1536, 1664, 1792, 2304, 2816, 3328
    cols = []
    for h in range(HG_HEADS):
        for base in (fz, bz, inp, qhg):
            cols += list(range(base + 128 * h, base + 128 * (h + 1)))
    cols += list(range(ghg, ghg + 512)) + list(range(qat, qat + 512))
    cols += list(range(kk, kk + 128)) + list(range(vv, vv + 128))
    cols += list(range(gates, gates + 2048))
    return np.asarray(cols, np.int32)


_PERM = _in_perm()
_INV_PERM = np.argsort(_PERM).astype(np.int32)


def _take_rows(w, perm):
    cuts = [0] + [i for i in range(1, len(perm)) if perm[i] != perm[i - 1] + 1] + [len(perm)]
    return jnp.concatenate([w[int(perm[a]):int(perm[b - 1]) + 1] for a, b in zip(cuts[:-1], cuts[1:])],
                           axis=0)


def _cols_from_blocks(g):
    return jnp.transpose(g, (1, 0, 2)).reshape(g.shape[1], N_DEV * g.shape[2])


def _local_step(x2, ctx2, tgt, lbl, sh_in, sc_in, gate1, sh2, sc2, gate2, norm_mix_w, norm_ffn_w,
                hgrn_norm_w, q_norm_w, k_norm_w, attn_sinks, w_a, w_b, w_c, w_bh, w_ba, w_o,
                g_gate, g_up, g_down):
    h_all = _norm_mod_all(ctx2, x2, norm_mix_w, sh_in, sc_in)
    p_a = _mm_nt(h_all, w_a, tm=768, tn=1024, out_dtype=F32, name="proj_a")
    p_b = _mm_nt(h_all, w_b, tm=768, tn=1280, out_dtype=F32, name="proj_b")
    p_c = _mm_nt(h_all, w_c, tm=256, tn=2048, out_dtype=F32, name="proj_c", row_off=1, rows=S)
    o, st = _hgrn_fwd(p_a, lbl)
    cos, sin = _rope_tables()
    qnw_t, knw_t = jnp.tile(q_norm_w, (1, ATT_HEADS)), jnp.tile(k_norm_w, (1, KV_HEADS))
    y_hg, qn, k_rep, v_rep = _prep_fwd(p_b, o, cos, sin, hgrn_norm_w, qnw_t, knw_t)
    y_at, lse = _attn_fwd(qn, k_rep, v_rep, attn_sinks)
    a, b, mixed, r, x1, h2 = _merge_fwd(y_hg, y_at, p_c, x2, w_bh, w_ba, w_o, gate1, norm_ffn_w, sh2, sc2)

    act, d_gate, d_up, d_f, dx1, acc_ffn = _ffn_fused(x1, h2, tgt, g_gate, g_up, g_down, gate2,
                                                      norm_ffn_w, sc2)
    t_down = _mm_tn_blocked(act, d_f, "grad_down")
    t_gate = _mm_tn_blocked(d_gate, h2, "grad_gate")
    t_up = _mm_tn_blocked(d_up, h2, "grad_up")

    d_r, d_a, d_b, dp_c, dy_hg, dy_at, acc_mg = _merge_bwd(dx1, r, a, b, p_c, w_bh, w_ba, w_o, gate1)
    t_out = _mm_tn(mixed, d_r, tk=512, nk=4, tm=512, tn=1024, out_dtype=BF16, name="grad_out")
    t_bh = _mm_tn(d_a, y_hg, tk=512, nk=4, tm=512, tn=512, out_dtype=BF16, name="grad_bh")
    t_ba = _mm_tn(d_b, y_at, tk=512, nk=4, tm=512, tn=512, out_dtype=BF16, name="grad_ba")
    dq, dk_rep, dv_rep, dsink = _attn_bwd(qn, k_rep, v_rep, attn_sinks, y_at, lse, dy_at)
    dp_b, d_o, acc_prep = _prep_bwd(p_b, o, cos, sin, hgrn_norm_w, qnw_t, knw_t,
                                    dy_hg, dq, dk_rep, dv_rep)
    dp_a, dlb = _hgrn_bwd(p_a, lbl, d_o, st)
    grad_x, acc_in = _input_bwd(dp_a, dp_b, dp_c, w_a, w_b, w_c, ctx2, x2, dx1, norm_mix_w, sh_in, sc_in)
    t_a = _mm_tn(dp_a, h_all, tk=768, nk=3, tm=1024, tn=1024, out_dtype=BF16, name="grad_in_a")
    t_b = _mm_tn(dp_b, h_all, tk=768, nk=3, tm=640, tn=1024, out_dtype=BF16, name="grad_in_b")
    t_c = _mm_tn(dp_c, h_all, tk=256, nk=8, b_off=1, tm=1024, tn=1024, out_dtype=BF16, name="grad_in_c")
    t_in = _take_rows(jnp.concatenate([t_a, t_b, t_c], axis=0), _INV_PERM).reshape(N_DEV, IN_BLK, D)
    terms = [t_in, t_bh.reshape(N_DEV, D // N_DEV, HGW), t_ba.reshape(N_DEV, D // N_DEV, ATW),
             t_out.reshape(N_DEV, D // N_DEV, D), t_gate, t_up, t_down]
    return grad_x, _pack_small(acc_in, acc_mg, acc_ffn, acc_prep, dsink, dlb), terms


def kernel(x, c, ctx, c_ctx, w_ada, b_ada, norm_mix_w, norm_ffn_w, w_in, hgrn_lb_logits, hgrn_norm_w, q_norm_w, k_norm_w, attn_sinks, w_branch_hgrn, w_branch_attn, w_out, w_ffn_gate, w_ffn_up, w_ffn_down, loss_target, m_c_ctx, m_w_ada, m_b_ada, m_norm_mix_w, m_norm_ffn_w, m_w_in, m_hgrn_lb_logits, m_hgrn_norm_w, m_q_norm_w, m_k_norm_w, m_attn_sinks, m_w_branch_hgrn, m_w_branch_attn, m_w_out, m_w_ffn_gate, m_w_ffn_up, m_w_ffn_down, v_c_ctx, v_w_ada, v_b_ada, v_norm_mix_w, v_norm_ffn_w, v_w_in, v_hgrn_lb_logits, v_hgrn_norm_w, v_q_norm_w, v_k_norm_w, v_attn_sinks, v_w_branch_hgrn, v_w_branch_attn, v_w_out, v_w_ffn_gate, v_w_ffn_up, v_w_ffn_down):
    me = 4 * lax.axis_index("x") + 2 * lax.axis_index("y") + lax.axis_index("c")
    x2, ctx2, tgt = x[0], ctx[0], loss_target[0]
    w_ada2, w_in2 = w_ada[0], w_in[0]

    blk = jnp.zeros((8, D), F32).at[0].set(c[0]).at[1, :256].set(hgrn_lb_logits.reshape(256))
    (g0,) = _all_gather([blk], "gather_cond", True)
    cc = jnp.zeros((16, D), F32).at[:8].set(g0[:, 0, :]).at[8].set(c_ctx)
    lbl = jnp.transpose(g0[:, 1, :256].reshape(N_DEV, 2, 2, 64), (1, 2, 0, 3)).reshape(2, 2, HGW)

    b_cols = lax.dynamic_slice(b_ada, (0, me * ADA_BLK), (1, ADA_BLK))
    (g1,) = _all_gather([_ada_rows(cc, w_ada2, b_cols)], "gather_mod", True)
    mod_all = _cols_from_blocks(g1)
    mod = lax.dynamic_slice(mod_all, (me, 0), (1, 6 * D)).reshape(6, D)
    mod_c = mod_all[8].reshape(6, D)
    sh1, sc1, gate1, sh2, sc2, gate2 = [mod[k:k + 1] for k in range(6)]
    sh_in = jnp.concatenate([mod_c[0:1], sh1], axis=0)
    sc_in = jnp.concatenate([mod_c[1:2], sc1], axis=0)

    shards = [w_in2.T, w_branch_hgrn[0].T, w_branch_attn[0].T, w_out[0], w_ffn_gate[0].T, w_ffn_up[0].T,
              w_ffn_down[0]]
    g_in, g_bh, g_ba, g_out, g_gate, g_up, g_down = _all_gather(
        [s.astype(BF16) for s in shards], "gather_weights", False)
    w_in_t = _take_rows(g_in.reshape(IN_COLS, D), _PERM)
    w_a, w_b, w_c = w_in_t[:WA], w_in_t[WA:WA + WB], w_in_t[WA + WB:]
    w_bh, w_ba = g_bh.reshape(D, HGW), g_ba.reshape(D, ATW)
    w_o = g_out.reshape(D, D)

    grad_x, small, terms = _local_step(
        x2, ctx2, tgt, lbl, sh_in, sc_in, gate1, sh2, sc2, gate2, norm_mix_w, norm_ffn_w, hgrn_norm_w,
        q_norm_w, k_norm_w, attn_sinks, w_a, w_b, w_c, w_bh, w_ba, w_o, g_gate, g_up, g_down)

    terms = [t.reshape((N_CHIPS, 2) + t.shape[1:]) for t in terms]
    from_sibling = _sibling_exchange(terms, "scatter_sibling")
    chip_terms = [_pair_sum(t, f, "pair_sum_%d" % a) for a, (t, f) in enumerate(zip(terms, from_sibling))]
    r_in, r_bh, r_ba, r_out, r_gate, r_up, r_down = _chip_exchange(chip_terms, "scatter_chips")
    big = {}
    for nm, rr, ww, mm, vv, tr, transposed in (
            ("w_in", r_in, w_in2, m_w_in[0], v_w_in[0], 336, True),
            ("w_branch_hgrn", r_bh, w_branch_hgrn[0], m_w_branch_hgrn[0], v_w_branch_hgrn[0], 128, True),
            ("w_branch_attn", r_ba, w_branch_attn[0], m_w_branch_attn[0], v_w_branch_attn[0], 128, True),
            ("w_out", r_out, w_out[0], m_w_out[0], v_w_out[0], 128, False),
            ("w_ffn_gate", r_gate, w_ffn_gate[0], m_w_ffn_gate[0], v_w_ffn_gate[0], 352, True),
            ("w_ffn_up", r_up, w_ffn_up[0], m_w_ffn_up[0], v_w_ffn_up[0], 352, True),
            ("w_ffn_down", r_down, w_ffn_down[0], m_w_ffn_down[0], v_w_ffn_down[0], 352, False)):
        if transposed:
            res = _adamw_sharded(rr, ww.T, mm.T, vv.T, "adamw_" + nm, tr)
            big[nm] = [t.T[None] for t in res]
        else:
            big[nm] = [t[None] for t in _adamw_sharded(rr, ww, mm, vv, "adamw_" + nm, tr)]

    (g2,) = _all_gather([small], "gather_small", True)
    tot = _sum_small(g2)
    dm = jnp.zeros((16, 6 * D), F32).at[:8].set(g2[:, R_DMOD:R_DMOD + 6, :].reshape(N_DEV, 6 * D))
    dm = dm.at[8, :2 * D].set(tot[R_DCTX:R_DCTX + 2].reshape(2 * D))
    dm_cols = lax.dynamic_slice(dm, (0, me * ADA_BLK), (16, ADA_BLK))
    g_w_ada, dsc_term = _ada_grads(cc, dm_cols, w_ada2)
    (g3,) = _all_gather([dsc_term], "gather_cctx", True)
    g_c_ctx = _c_ctx_grad(g3, c_ctx.reshape(1, D))
    g_lbl = _lb_grads(tot[R_DLB:R_DLB + 2, :HGW], lbl)
    g_lb_mine = lax.dynamic_slice(g_lbl, (0, 0, me * 64), (2, 2, 64))
    misc = tot[R_MISC]
    loss = misc[M_LOSS]

    rep_out = _adamw_replicated(
        tot, g_c_ctx,
        [b_ada, c_ctx.reshape(1, D), norm_mix_w, norm_ffn_w, hgrn_norm_w, q_norm_w, k_norm_w, attn_sinks],
        [m_b_ada, m_c_ctx.reshape(1, D), m_norm_mix_w, m_norm_ffn_w, m_hgrn_norm_w, m_q_norm_w, m_k_norm_w,
         m_attn_sinks],
        [v_b_ada, v_c_ctx.reshape(1, D), v_norm_mix_w, v_norm_ffn_w, v_hgrn_norm_w, v_q_norm_w, v_k_norm_w,
         v_attn_sinks])
    rep = []
    for kind in range(4):
        vals = dict(zip(_REP_NAMES, rep_out[kind * len(_REP_NAMES):(kind + 1) * len(_REP_NAMES)]))
        vals["c_ctx"] = vals["c_ctx"].reshape(D)
        rep.append(vals)

    d_ada, nm_ada, nv_ada = _adamw_plain(g_w_ada, w_ada2, m_w_ada[0], v_w_ada[0], "adamw_w_ada")
    ada = [t[None] for t in (g_w_ada, d_ada, nm_ada, nv_ada)]
    lb_w = hgrn_lb_logits.reshape(4, 64)
    d_lb, nm_lb, nv_lb = _adamw_plain(g_lb_mine.reshape(4, 64), lb_w, m_hgrn_lb_logits.reshape(4, 64),
                                      v_hgrn_lb_logits.reshape(4, 64), "adamw_lb")
    lbs = [t.reshape(2, 2, 64) for t in (g_lb_mine, d_lb, nm_lb, nv_lb)]

    names = ['c_ctx', 'w_ada', 'b_ada', 'norm_mix_w', 'norm_ffn_w', 'w_in', 'hgrn_lb_logits', 'hgrn_norm_w',
             'q_norm_w', 'k_norm_w', 'attn_sinks', 'w_branch_hgrn', 'w_branch_attn', 'w_out', 'w_ffn_gate',
             'w_ffn_up', 'w_ffn_down']
    outs = [loss, grad_x[None]]
    for kind in range(4):
        for nm in names:
            if nm == 'w_ada':
                outs.append(ada[kind])
            elif nm == 'hgrn_lb_logits':
                outs.append(lbs[kind])
            elif nm in big:
                outs.append(big[nm][kind])
            else:
                outs.append(rep[kind][nm])
    return tuple(outs)
```

```python
import functools
import math

import numpy as np
import jax
import jax.numpy as jnp
from jax import lax
from jax.experimental import pallas as pl
from jax.experimental.pallas import tpu as pltpu

F32 = jnp.float32
BF16 = jnp.bfloat16

N_DEV = 8
D = 1024
S = 2048
L = 256
T = L + S
TM = 256
N_TILES = T // TM
N_LAT_TILES = S // TM
HG_HEADS = 4
HG_DIM = 128
HGW = 512
CHUNK = 32
N_CHUNKS = T // CHUNK
N_CTX_CHUNKS = L // CHUNK
N_LAT_CHUNKS = S // CHUNK
ATT_HEADS = 8
KV_HEADS = 2
HEAD_DIM = 64
ATW = 512
KVW = 128
BLOCK = 128
N_BLOCKS = S // BLOCK
GRID_W = 64
ROPE_THETA = 10000.0
D_FF = 2816
FF_BLK = D_FF // N_DEV
IN_COLS = 5376
IN_BLK = IN_COLS // N_DEV
ADA_BLK = 6 * D // N_DEV
EPS = 1e-6
WA, WB, WC = 2048, 1280, 2048

ADAM_LR = 0.001
ADAM_B1 = 0.9
ADAM_B2 = 0.999
ADAM_EPS = 1e-08
ADAM_WD = 0.01
ADAM_STEP = 10

VMEM_LIMIT = 56 * 1024 * 1024
MESH = pl.DeviceIdType.MESH


def _cp(sem=None, vmem=VMEM_LIMIT):
    return pltpu.CompilerParams(dimension_semantics=sem, vmem_limit_bytes=vmem)


def _full(shape):
    n = len(shape)
    return pl.BlockSpec(shape, lambda *_: (0,) * n)


_VMEM_WHOLE = pl.BlockSpec(memory_space=pltpu.VMEM)
_ANY = pl.BlockSpec(memory_space=pl.ANY)


def _sigmoid(v):
    return 1.0 / (1.0 + jnp.exp(-v))


def _dot(a, b):
    return jnp.dot(a, b, preferred_element_type=F32)


def _dot_nt(a, b):
    return lax.dot_general(a, b, (((1,), (1,)), ((), ())), preferred_element_type=F32)


def _dot_tn(a, b):
    return lax.dot_general(a, b, (((0,), (0,)), ((), ())), preferred_element_type=F32)


def _split3(v):
    hi = v.astype(BF16)
    r = v - hi.astype(F32)
    mid = r.astype(BF16)
    lo = (r - mid.astype(F32)).astype(BF16)
    return hi, mid, lo


def _dot_exact_rhs01(v, m01):
    hi, mid, lo = _split3(v)
    return _dot(hi, m01) + _dot(mid, m01) + _dot(lo, m01)


def _dot_exact_lhs01(m01, v):
    hi, mid, lo = _split3(v)
    return _dot(m01, hi) + _dot(m01, mid) + _dot(m01, lo)


def _dot_f32(a, b, dot=_dot):
    ah, am, al = _split3(a)
    bh, bm, bl = _split3(b)
    return (dot(ah, bh) + (dot(ah, bm) + dot(am, bh))
            + (dot(am, bm) + dot(ah, bl) + dot(al, bh)))


def _my_pos():
    return lax.axis_index("x"), lax.axis_index("y"), lax.axis_index("c")


class _Comm:
    def __init__(self, operands, out_shapes, sems, phases):
        self.operands, self.out_shapes, self.sems, self.phases = operands, out_shapes, sems, phases


def _gather_comm(blocks):
    n = len(blocks)

    def parts(ins, outs, sems):
        send_sems, recv_sems, local_sems = sems
        x, y, c = _my_pos()
        me, sibling = (x, y, c), (x, y, 1 - c)
        chips = [(1 - x, y), (x, 1 - y), (1 - x, 1 - y)]

        def slot(a, px, py, pc):
            return outs[a].at[4 * px + 2 * py + pc]

        def copy(a, k, block, to, src=None):
            return pltpu.make_async_remote_copy(
                src_ref=slot(a, *block) if src is None else src, dst_ref=slot(a, *block),
                send_sem=send_sems.at[a, k], recv_sem=recv_sems.at[a, k],
                device_id=to, device_id_type=MESH)

        mine = [pltpu.make_async_copy(ins[a], slot(a, *me), local_sems.at[a]) for a in range(n)]
        first = []
        for a in range(n):
            first.append(copy(a, 0, me, sibling, src=ins[a]))
            first += [copy(a, 1 + j, me, (*chip, c), src=ins[a]) for j, chip in enumerate(chips)]
        passed = [copy(a, 4 + j, (*chip, c), sibling) for j, chip in enumerate(chips) for a in range(n)]
        return c, me, sibling, chips, copy, mine, first, passed

    def start(ins, outs, sems):
        _, _, _, _, _, mine, first, _ = parts(ins, outs, sems)
        for cp in mine + first:
            cp.start()

    def forward(ins, outs, sems):
        c, me, _, chips, copy, _, _, passed = parts(ins, outs, sems)
        for j, chip in enumerate(chips):
            for a in range(n):
                copy(a, 1 + j, (*chip, c), me).wait_recv()
                passed[j * n + a].start()

    def finish(ins, outs, sems):
        c, me, sibling, chips, copy, mine, first, passed = parts(ins, outs, sems)
        for a in range(n):
            copy(a, 0, sibling, me).wait_recv()
            for j, chip in enumerate(chips):
                copy(a, 4 + j, (*chip, 1 - c), me).wait_recv()
        for cp in first + passed:
            cp.wait_send()
        for cp in mine:
            cp.wait()

    return _Comm(blocks, [jax.ShapeDtypeStruct((N_DEV,) + b.shape, b.dtype) for b in blocks],
                 [pltpu.SemaphoreType.DMA((n, 7)), pltpu.SemaphoreType.DMA((n, 7)), pltpu.SemaphoreType.DMA((n,))],
                 [start, forward, finish])


def _run_comm(comm, name, in_vmem=False):
    n_in, n_out = len(comm.operands), len(comm.out_shapes)

    def body(*refs):
        ins, outs, sems = refs[:n_in], refs[n_in:n_in + n_out], refs[n_in + n_out:]
        for phase in comm.phases:
            phase(ins, outs, sems)

    spec = _VMEM_WHOLE if in_vmem else _ANY
    return pl.pallas_call(
        body, name=name, out_shape=comm.out_shapes, in_specs=[spec] * n_in, out_specs=[spec] * n_out,
        scratch_shapes=comm.sems,
    )(*comm.operands)


def _carrier_call(body, comm, schedule, *, name, grid, in_specs, out_specs, out_shape, scratch_shapes, operands):
    n_in, n_out, n_scr = len(in_specs), len(out_specs), len(scratch_shapes)
    c_in, c_out = len(comm.operands), len(comm.out_shapes)

    def full_body(*refs):
        ins, refs = refs[:n_in], refs[n_in:]
        cins, refs = refs[:c_in], refs[c_in:]
        outs, refs = refs[:n_out], refs[n_out:]
        couts, refs = refs[:c_out], refs[c_out:]
        scr, csems = refs[:n_scr], refs[n_scr:]
        step = pl.program_id(0)

        def run(before):
            for (at, when_before), phase in zip(schedule, comm.phases):
                if when_before == before:
                    pl.when(step == at)(functools.partial(phase, cins, couts, csems))

        run(True)
        body(*ins, *outs, *scr)
        run(False)

    res = pl.pallas_call(
        full_body, name=name, grid=grid,
        in_specs=list(in_specs) + [_ANY] * c_in, out_specs=list(out_specs) + [_ANY] * c_out,
        out_shape=list(out_shape) + list(comm.out_shapes),
        scratch_shapes=list(scratch_shapes) + list(comm.sems),
        compiler_params=_cp(("arbitrary",)),
    )(*operands, *comm.operands)
    return res[:n_out], res[n_out:]


def _pcall(body, carried, *, name, grid, in_specs, out_specs, out_shape, scratch_shapes, operands):
    if carried is None:
        res = pl.pallas_call(body, name=name, grid=grid, in_specs=in_specs, out_specs=out_specs,
                             out_shape=out_shape, scratch_shapes=scratch_shapes,
                             compiler_params=_cp(("arbitrary",)))(*operands)
        return res, ()
    return _carrier_call(body, carried[0], carried[1], name=name, grid=grid, in_specs=in_specs,
                         out_specs=out_specs, out_shape=out_shape, scratch_shapes=scratch_shapes,
                         operands=operands)


def _all_gather(blocks, name, in_vmem):
    return _run_comm(_gather_comm(blocks), name, in_vmem)


N_CHIPS = 4


def _sibling_comm(contribs):
    n = len(contribs)

    def copies(ins, outs, sems):
        send_sems, recv_sems = sems
        x, y, c = _my_pos()
        return [pltpu.make_async_remote_copy(
            src_ref=ins[a].at[pl.ds(0, N_CHIPS), 1 - c], dst_ref=outs[a],
            send_sem=send_sems.at[a], recv_sem=recv_sems.at[a],
            device_id=(x, y, 1 - c), device_id_type=MESH) for a in range(n)]

    def start(ins, outs, sems):
        for cp in copies(ins, outs, sems):
            cp.start()

    def finish(ins, outs, sems):
        cps = copies(ins, outs, sems)
        for cp in cps:
            cp.wait_recv()
        for cp in cps:
            cp.wait_send()

    return _Comm(contribs, [jax.ShapeDtypeStruct((N_CHIPS,) + b.shape[2:], b.dtype) for b in contribs],
                 [pltpu.SemaphoreType.DMA((n,)), pltpu.SemaphoreType.DMA((n,))], [start, finish])


def _pair_sum(mine, theirs, name):
    _, _, rows, cols = mine.shape

    def body(m_ref, t_ref, o_ref):
        c = lax.axis_index("c")
        o_ref[...] = (m_ref[c].astype(F32) + t_ref[...].astype(F32)).astype(BF16)

    return pl.pallas_call(
        body, name=name, grid=(N_CHIPS,),
        in_specs=[pl.BlockSpec((None, 2, rows, cols), lambda q: (q, 0, 0, 0)),
                  pl.BlockSpec((None, rows, cols), lambda q: (q, 0, 0))],
        out_specs=pl.BlockSpec((None, rows, cols), lambda q: (q, 0, 0)),
        out_shape=jax.ShapeDtypeStruct((N_CHIPS, rows, cols), BF16),
        compiler_params=_cp(("parallel",)),
    )(mine, theirs)


def _chip_comm(sums):
    n = len(sums)

    def parts(ins, outs, sems):
        send_sems, recv_sems, local_sems = sems
        x, y, c = _my_pos()
        q_me = 2 * x + y
        chips = [(1 - x, y), (x, 1 - y), (1 - x, 1 - y)]
        mine = [pltpu.make_async_copy(ins[a].at[q_me], outs[a].at[q_me], local_sems.at[a]) for a in range(n)]
        sends, recvs = [], []
        for j, (px, py) in enumerate(chips):
            for a in range(n):
                q = 2 * px + py
                sends.append(pltpu.make_async_remote_copy(
                    src_ref=ins[a].at[q], dst_ref=outs[a].at[q_me],
                    send_sem=send_sems.at[a, j], recv_sem=recv_sems.at[a, j],
                    device_id=(px, py, c), device_id_type=MESH))
                recvs.append(pltpu.make_async_remote_copy(
                    src_ref=ins[a].at[q], dst_ref=outs[a].at[q],
                    send_sem=send_sems.at[a, j], recv_sem=recv_sems.at[a, j],
                    device_id=(x, y, c), device_id_type=MESH))
        return mine, sends, recvs

    def start(ins, outs, sems):
        mine, sends, _ = parts(ins, outs, sems)
        for cp in mine + sends:
            cp.start()

    def finish(ins, outs, sems):
        mine, sends, recvs = parts(ins, outs, sems)
        for cp in recvs:
            cp.wait_recv()
        for cp in sends:
            cp.wait_send()
        for cp in mine:
            cp.wait()

    return _Comm(sums, [jax.ShapeDtypeStruct(b.shape, b.dtype) for b in sums],
                 [pltpu.SemaphoreType.DMA((n, 3)), pltpu.SemaphoreType.DMA((n, 3)), pltpu.SemaphoreType.DMA((n,))],
                 [start, finish])


def _mm_nt(a, bt, *, tm, tn, out_dtype, name, row_off=0, rows=None):
    rows = a.shape[0] if rows is None else rows
    n, k = bt.shape

    def body(a_ref, b_ref, o_ref):
        o_ref[...] = _dot_nt(a_ref[...], b_ref[...]).astype(out_dtype)

    return pl.pallas_call(
        body, name=name, grid=(rows // tm, n // tn),
        in_specs=[pl.BlockSpec((tm, k), lambda i, j: (i + row_off, 0)),
                  pl.BlockSpec((tn, k), lambda i, j: (j, 0))],
        out_specs=pl.BlockSpec((tm, tn), lambda i, j: (i, j)),
        out_shape=jax.ShapeDtypeStruct((rows, n), out_dtype),
        compiler_params=_cp(("parallel", "parallel")),
    )(a, bt)


def _mm_tn(a, b, *, tk, nk, tm, tn, out_dtype, name, a_off=0, b_off=0):
    m, n = a.shape[1], b.shape[1]

    def body(a_ref, b_ref, o_ref, acc):
        kk = pl.program_id(2)

        @pl.when(kk == 0)
        def _():
            acc[...] = jnp.zeros_like(acc)

        acc[...] += _dot_tn(a_ref[...], b_ref[...])

        @pl.when(kk == nk - 1)
        def _():
            o_ref[...] = acc[...].astype(out_dtype)

    return pl.pallas_call(
        body, name=name, grid=(m // tm, n // tn, nk),
        in_specs=[pl.BlockSpec((tk, tm), lambda i, j, kk: (kk + a_off, i)),
                  pl.BlockSpec((tk, tn), lambda i, j, kk: (kk + b_off, j))],
        out_specs=pl.BlockSpec((tm, tn), lambda i, j, kk: (i, j)),
        out_shape=jax.ShapeDtypeStruct((m, n), out_dtype),
        scratch_shapes=[pltpu.VMEM((tm, tn), F32)],
        compiler_params=_cp(("parallel", "parallel", "arbitrary")),
    )(a, b)


def _mm_tn_blocked(a, b, name, carried=None):
    w, n = a.shape[2], b.shape[1]

    def body(a_ref, b_ref, o_ref):
        o_ref[...] = _dot_tn(a_ref[...], b_ref[...]).astype(BF16)

    (out,), extra = _pcall(
        body, carried, name=name, grid=(N_DEV,),
        in_specs=[pl.BlockSpec((None, S, w), lambda j: (j, 0, 0)), _full((S, n))],
        out_specs=[pl.BlockSpec((None, w, n), lambda j: (j, 0, 0))],
        out_shape=[jax.ShapeDtypeStruct((N_DEV, w, n), BF16)],
        scratch_shapes=[], operands=[a, b])
    return out, extra


def _ada_rows(cc, w_ada, b_cols):
    def body(c_ref, w_ref, b_ref, o_ref):
        cv = c_ref[...]
        o_ref[...] = _dot_f32(cv * _sigmoid(cv), w_ref[...]) + b_ref[...]

    return pl.pallas_call(
        body, name="ada_rows",
        in_specs=[_VMEM_WHOLE] * 3, out_specs=_VMEM_WHOLE,
        out_shape=jax.ShapeDtypeStruct((16, ADA_BLK), F32),
        compiler_params=_cp(),
    )(cc, w_ada, b_cols)


def _ada_grads(cc, dm_cols, w_ada):
    def body(c_ref, dm_ref, w_ref, gw_ref, dsc_ref):
        cv = c_ref[...]
        sc = cv * _sigmoid(cv)
        dm = dm_ref[...]
        gw_ref[...] = _dot_f32(sc, dm, dot=_dot_tn)
        dsc_ref[...] = _dot_f32(dm, w_ref[...], dot=_dot_nt)

    return pl.pallas_call(
        body, name="ada_grads",
        in_specs=[_VMEM_WHOLE] * 3, out_specs=[_VMEM_WHOLE] * 2,
        out_shape=[jax.ShapeDtypeStruct((D, ADA_BLK), F32), jax.ShapeDtypeStruct((16, D), F32)],
        compiler_params=_cp(),
    )(cc, dm_cols, w_ada)


def _lat(i):
    return jnp.maximum(i - 1, 0)


def _rms_mod(xv, nw, sh, sc):
    rstd = lax.rsqrt(jnp.mean(xv * xv, axis=-1, keepdims=True) + EPS)
    return (xv * rstd * nw) * (1.0 + sc) + sh


def _rms_mod_bwd(xv, nw, sc, dh):
    rstd = lax.rsqrt(jnp.mean(xv * xv, axis=-1, keepdims=True) + EPS)
    xhat = xv * rstd
    dn = dh * (1.0 + sc)
    dxhat = dn * nw
    dx = rstd * (dxhat - xhat * jnp.mean(dxhat * xhat, axis=-1, keepdims=True))
    return (dx, jnp.sum(dh, axis=0, keepdims=True), jnp.sum(dh * (xhat * nw), axis=0, keepdims=True),
            jnp.sum(dn * xhat, axis=0, keepdims=True))


def _norm_mod_all(ctx, x, nw, sh, sc):
    def body(ctx_ref, x_ref, nw_ref, sh_ref, sc_ref, o_ref):
        i = pl.program_id(0)
        sel = jnp.minimum(i, 1)
        xv = jnp.where(i == 0, ctx_ref[...], x_ref[...])
        o_ref[...] = _rms_mod(xv, nw_ref[...], sh_ref[pl.ds(sel, 1), :], sc_ref[pl.ds(sel, 1), :]).astype(BF16)

    return pl.pallas_call(
        body, name="norm_mod", grid=(N_TILES,),
        in_specs=[_full((TM, D)), pl.BlockSpec((TM, D), lambda i: (_lat(i), 0)),
                  _full((1, D)), _full((2, D)), _full((2, D))],
        out_specs=pl.BlockSpec((TM, D), lambda i: (i, 0)),
        out_shape=jax.ShapeDtypeStruct((T, D), BF16),
        compiler_params=_cp(("parallel",)),
    )(ctx, x, nw, sh, sc)


def _chunk_masks(reverse):
    row = lax.broadcasted_iota(jnp.int32, (TM, TM), 0)
    col = lax.broadcasted_iota(jnp.int32, (TM, TM), 1)
    same = (row // CHUNK) == (col // CHUNK)
    tri = same & ((col >= row) if reverse else (col <= row))
    return same, tri


def _chunk_order(i, reverse):
    if not reverse:
        return i
    return jnp.where(i < N_CTX_CHUNKS, N_CTX_CHUNKS - 1 - i, N_CHUNKS + N_CTX_CHUNKS - 1 - i)


def _decay_terms(z, lb, same01, tri01):
    f = lb + (1.0 - lb) * _sigmoid(z)
    g = jnp.log(f)
    hi, mid, lo = _split3(g)
    g3 = jnp.concatenate([hi, mid, lo], axis=1)
    b3 = _dot(tri01, g3)
    t3 = _dot(same01, g3)
    b = b3[:, :HG_DIM] + b3[:, HG_DIM:2 * HG_DIM] + b3[:, 2 * HG_DIM:]
    bt = t3[:, :HG_DIM] + t3[:, HG_DIM:2 * HG_DIM] + t3[:, 2 * HG_DIM:]
    return f, 1.0 - f, b, bt


def _chunk_outer(a, b):
    n = TM // CHUNK
    return jnp.einsum('ncv,nck->nvk', a.reshape(n, CHUNK, HG_DIM), b.reshape(n, CHUNK, HG_DIM),
                      preferred_element_type=F32)


def _hgrn_fwd(p_a, lbl, carried=None):
    def body(p_ref, lbl_ref, o_ref, st_ref, qd_s, kd_s, u_s, v_s, ebt_s):
        for d in (0, 1):
            same, tri = _chunk_masks(d == 1)
            same01 = jnp.where(same, 1.0, 0.0).astype(BF16)
            tri01 = jnp.where(tri, 1.0, 0.0).astype(BF16)
            ll = lbl_ref[d]
            lb = _sigmoid(ll[0:1, :] - ll[1:2, :])

            def prep(r, carry):
                r0 = pl.multiple_of(r * TM, TM)
                z = p_ref[pl.ds(r0, TM), d * HG_DIM:(d + 1) * HG_DIM]
                _, k, b, bt = _decay_terms(z, lb, same01, tri01)
                vb = p_ref[pl.ds(r0, TM), 2 * HG_DIM:3 * HG_DIM].astype(BF16)
                u_s[pl.ds(r * (TM // CHUNK), TM // CHUNK)] = _chunk_outer(vb, (k * jnp.exp(bt - b)).astype(BF16))
                ebt_s[pl.ds(r0, TM), :] = jnp.exp(bt)
                v_s[pl.ds(r0, TM), :] = vb

                @pl.when(r >= 1)
                def _():
                    rl = pl.multiple_of(r0 - L, TM)
                    qr = p_ref[pl.ds(r0, TM), 3 * HG_DIM:4 * HG_DIM]
                    q = qr * _sigmoid(qr) * HG_DIM ** -0.5
                    qd_s[pl.ds(rl, TM), :] = (q * jnp.exp(b)).astype(BF16)
                    kd_s[pl.ds(rl, TM), :] = (k * jnp.exp(-b)).astype(BF16)

                return carry

            lax.fori_loop(0, N_TILES, prep, 0)

            def scan(i, st):
                nn = _chunk_order(i, d == 1)
                c0 = pl.multiple_of(nn * CHUNK, CHUNK)
                st_ref[d, nn] = st.astype(BF16)
                return st * ebt_s[pl.ds(c0, 1), :] + u_s[nn]

            lax.fori_loop(0, N_CHUNKS, scan, jnp.zeros((HG_DIM, HG_DIM), F32))

            def outp(r, carry):
                r0 = pl.multiple_of(r * TM, TM)
                qd = qd_s[pl.ds(r0, TM), :]
                a = jnp.where(tri, _dot_nt(qd, kd_s[pl.ds(r0, TM), :]), 0.0)
                o = _dot(a.astype(BF16), v_s[pl.ds(r0 + L, TM), :])
                stb = st_ref[d, pl.ds(N_CTX_CHUNKS + r * (TM // CHUNK), TM // CHUNK)]
                inter = jnp.einsum('nck,nvk->ncv', qd.reshape(TM // CHUNK, CHUNK, HG_DIM), stb,
                                   preferred_element_type=F32)
                o = o + inter.reshape(TM, HG_DIM)
                if d == 0:
                    o_ref[pl.ds(r0, TM), :] = o
                else:
                    o_ref[pl.ds(r0, TM), :] += o
                return carry

            lax.fori_loop(0, N_LAT_TILES, outp, 0)

    return _pcall(
        body, carried, name="hgrn_fwd", grid=(HG_HEADS,),
        in_specs=[pl.BlockSpec((T, 4 * HG_DIM), lambda h: (0, h)),
                  pl.BlockSpec((2, 2, HG_DIM), lambda h: (0, 0, h))],
        out_specs=[pl.BlockSpec((S, HG_DIM), lambda h: (0, h)),
                   pl.BlockSpec((2, None, N_CHUNKS, HG_DIM, HG_DIM), lambda h: (0, h, 0, 0, 0))],
        out_shape=[jax.ShapeDtypeStruct((S, HGW), F32),
                   jax.ShapeDtypeStruct((2, HG_HEADS, N_CHUNKS, HG_DIM, HG_DIM), BF16)],
        scratch_shapes=[pltpu.VMEM((S, HG_DIM), BF16), pltpu.VMEM((S, HG_DIM), BF16),
                        pltpu.VMEM((N_CHUNKS, HG_DIM, HG_DIM), F32), pltpu.VMEM((T, HG_DIM), BF16),
                        pltpu.VMEM((T, HG_DIM), F32)],
        operands=[p_a, lbl])


def _hgrn_bwd(p_a, lbl, d_o, st, carried=None):
    cpt = TM // CHUNK

    def body(p_ref, lbl_ref, do_ref, st_ref, dp_ref, dlb_ref,
             b_s, bt_s, dbt_s, db_t, dk_t, dv_s, dq_s, qd_s, dst_s, w_s):
        dv_s[...] = jnp.zeros_like(dv_s)
        dq_s[...] = jnp.zeros_like(dq_s)
        for d in (0, 1):
            same, tri = _chunk_masks(d == 1)
            _, tri_rev = _chunk_masks(d == 0)
            same01 = jnp.where(same, 1.0, 0.0).astype(BF16)
            tri01 = jnp.where(tri, 1.0, 0.0).astype(BF16)
            later01 = jnp.where(tri_rev, 1.0, 0.0).astype(BF16)
            dbt_s[...] = jnp.zeros_like(dbt_s)
            ll = lbl_ref[d]
            lb = _sigmoid(ll[0:1, :] - ll[1:2, :])

            def prep(r, carry):
                r0 = pl.multiple_of(r * TM, TM)
                z = p_ref[pl.ds(r0, TM), d * HG_DIM:(d + 1) * HG_DIM]
                _, _, b, bt = _decay_terms(z, lb, same01, tri01)
                b_s[pl.ds(r0, TM), :] = b
                bt_s[pl.ds(r0, TM), :] = bt

                @pl.when(r >= 1)
                def _():
                    rl = pl.multiple_of(r0 - L, TM)
                    qr = p_ref[pl.ds(r0, TM), 3 * HG_DIM:4 * HG_DIM]
                    q = qr * _sigmoid(qr) * HG_DIM ** -0.5
                    qd = (q * jnp.exp(b)).astype(BF16)
                    qd_s[pl.ds(rl, TM), :] = qd
                    w_s[pl.ds(r * cpt, cpt)] = _chunk_outer(do_ref[pl.ds(rl, TM), :].astype(BF16), qd)

                return carry

            w_s[pl.ds(0, N_CTX_CHUNKS)] = jnp.zeros((N_CTX_CHUNKS, HG_DIM, HG_DIM), F32)
            lax.fori_loop(0, N_TILES, prep, 0)

            def rscan(j, dst):
                i = N_CHUNKS - 1 - j
                nn = _chunk_order(i, d == 1)
                c0 = pl.multiple_of(nn * CHUNK, CHUNK)
                dst_s[nn] = dst.astype(BF16)
                after = st_ref[d, _chunk_order(jnp.minimum(i + 1, N_CHUNKS - 1), d == 1)].astype(F32)
                dbt_s[pl.ds(c0, 1), :] = jnp.sum(after * dst, axis=0, keepdims=True)
                return dst * jnp.exp(bt_s[pl.ds(c0, 1), :]) + w_s[nn]

            lax.fori_loop(0, N_CHUNKS, rscan, jnp.zeros((HG_DIM, HG_DIM), F32))

            def grads(r, dlb):
                r0 = pl.multiple_of(r * TM, TM)
                z = p_ref[pl.ds(r0, TM), d * HG_DIM:(d + 1) * HG_DIM]
                sz = _sigmoid(z)
                f = lb + (1.0 - lb) * sz
                k = 1.0 - f
                b = b_s[pl.ds(r0, TM), :]
                e2 = jnp.exp(bt_s[pl.ds(r0, TM), :] - b)
                vb = p_ref[pl.ds(r0, TM), 2 * HG_DIM:3 * HG_DIM].astype(BF16)
                dstb = dst_s[pl.ds(r * cpt, cpt)]
                kd2 = k * e2
                dkd2 = jnp.einsum('ncv,nvk->nck', vb.reshape(cpt, CHUNK, HG_DIM), dstb,
                                  preferred_element_type=F32).reshape(TM, HG_DIM)
                dv = jnp.einsum('nck,nvk->ncv', kd2.astype(BF16).reshape(cpt, CHUNK, HG_DIM), dstb,
                                preferred_element_type=F32).reshape(TM, HG_DIM)
                dk_t[...] = dkd2 * e2
                db_t[...] = -(kd2 * dkd2)
                dv_s[pl.ds(r0, TM), :] += dv

                @pl.when(r >= 1)
                def _():
                    rl = pl.multiple_of(r0 - L, TM)
                    eb = jnp.exp(b)
                    enb = jnp.exp(-b)
                    qr = p_ref[pl.ds(r0, TM), 3 * HG_DIM:4 * HG_DIM]
                    sq = _sigmoid(qr)
                    qdf = qr * sq * HG_DIM ** -0.5 * eb
                    kdf = k * enb
                    qd = qd_s[pl.ds(rl, TM), :]
                    kd = kdf.astype(BF16)
                    do = do_ref[pl.ds(rl, TM), :].astype(BF16)
                    a = jnp.where(tri, _dot_nt(qd, kd), 0.0).astype(BF16)
                    da = jnp.where(tri, _dot_nt(do, vb), 0.0).astype(BF16)
                    stb = st_ref[d, pl.ds(r * cpt, cpt)]
                    dqd = _dot(da, kd) + jnp.einsum(
                        'ncv,nvk->nck', do.reshape(cpt, CHUNK, HG_DIM), stb,
                        preferred_element_type=F32).reshape(TM, HG_DIM)
                    dkd = _dot_tn(da, qd)
                    dv_s[pl.ds(r0, TM), :] += _dot_tn(a, do)
                    dk_t[...] += dkd * enb
                    db_t[...] += qdf * dqd - kdf * dkd
                    dq_s[pl.ds(rl, TM), :] += dqd * eb * (HG_DIM ** -0.5) * (sq * (1.0 + qr * (1.0 - sq)))

                dg = _dot_exact_lhs01(later01, db_t[...]) + _dot_exact_lhs01(same01, dbt_s[pl.ds(r0, TM), :])
                df = dg / f - dk_t[...]
                dp_ref[pl.ds(r0, TM), d * HG_DIM:(d + 1) * HG_DIM] = (
                    df * (1.0 - lb) * sz * (1.0 - sz)).astype(BF16)
                return dlb + jnp.sum(df * (1.0 - sz), axis=0, keepdims=True)

            dlb_ref[pl.ds(d, 1), :] = lax.fori_loop(0, N_TILES, grads, jnp.zeros((1, HG_DIM), F32))

        dp_ref[:, 2 * HG_DIM:3 * HG_DIM] = dv_s[...].astype(BF16)
        dp_ref[pl.ds(0, L), 3 * HG_DIM:4 * HG_DIM] = jnp.zeros((L, HG_DIM), BF16)
        dp_ref[pl.ds(L, S), 3 * HG_DIM:4 * HG_DIM] = dq_s[...].astype(BF16)

    return _pcall(
        body, carried, name="hgrn_bwd", grid=(HG_HEADS,),
        in_specs=[pl.BlockSpec((T, 4 * HG_DIM), lambda h: (0, h)),
                  pl.BlockSpec((2, 2, HG_DIM), lambda h: (0, 0, h)),
                  pl.BlockSpec((S, HG_DIM), lambda h: (0, h)),
                  pl.BlockSpec((2, None, N_CHUNKS, HG_DIM, HG_DIM), lambda h: (0, h, 0, 0, 0))],
        out_specs=[pl.BlockSpec((T, 4 * HG_DIM), lambda h: (0, h)),
                   pl.BlockSpec((2, HG_DIM), lambda h: (0, h))],
        out_shape=[jax.ShapeDtypeStruct((T, WA), BF16), jax.ShapeDtypeStruct((2, HGW), F32)],
        scratch_shapes=[pltpu.VMEM((T, HG_DIM), F32), pltpu.VMEM((T, HG_DIM), F32),
                        pltpu.VMEM((T, HG_DIM), F32), pltpu.VMEM((TM, HG_DIM), F32),
                        pltpu.VMEM((TM, HG_DIM), F32),
                        pltpu.VMEM((T, HG_DIM), F32), pltpu.VMEM((S, HG_DIM), F32),
                        pltpu.VMEM((S, HG_DIM), BF16),
                        pltpu.VMEM((N_CHUNKS, HG_DIM, HG_DIM), BF16),
                        pltpu.VMEM((N_CHUNKS, HG_DIM, HG_DIM), F32)],
        operands=[p_a, lbl, d_o, st])


def _rope_tables():
    t = np.arange(S)
    inv = ROPE_THETA ** (-np.arange(0, 32, 2, dtype=np.float64) / 32)
    lane = np.arange(64)
    pos = np.where(lane[None, :] < 32, (t // GRID_W)[:, None], (t % GRID_W)[:, None]).astype(np.float64)
    ang = pos * inv[(lane % 32) % 16][None, :]
    sign = np.where((lane % 32) < 16, -1.0, 1.0)[None, :]
    cos = np.tile(np.cos(ang), (1, 2)).astype(np.float32)
    sin = np.tile(np.sin(ang) * sign, (1, 2)).astype(np.float32)
    return jnp.asarray(cos), jnp.asarray(sin)


def _rope_partner(v):
    lane = lax.broadcasted_iota(jnp.int32, (1, 128), 1)
    first = (lane % 32) < 16
    slabs = []
    for j in range(v.shape[1] // 128):
        s = v[:, 128 * j:128 * (j + 1)]
        slabs.append(jnp.where(first, pltpu.roll(s, 112, 1), pltpu.roll(s, 16, 1)))
    return slabs[0] if len(slabs) == 1 else jnp.concatenate(slabs, axis=1)


def _group_ones(width, group):
    r = lax.broadcasted_iota(jnp.int32, (width, width), 0)
    c = lax.broadcasted_iota(jnp.int32, (width, width), 1)
    return jnp.where((r // group) == (c // group), 1.0, 0.0).astype(BF16)


def _group_mean(v, ones01, group):
    hi = v.astype(BF16)
    lo = (v - hi.astype(F32)).astype(BF16)
    return (_dot(hi, ones01) + _dot(lo, ones01)) * (1.0 / group)


def _rep_matrix():
    r = lax.broadcasted_iota(jnp.int32, (KVW, ATW), 0)
    c = lax.broadcasted_iota(jnp.int32, (KVW, ATW), 1)
    return jnp.where(r == HEAD_DIM * (c // 256) + c % HEAD_DIM, 1.0, 0.0).astype(BF16)


def _tile_lanes(v, reps):
    return jnp.concatenate([v] * reps, axis=1)


def _prep_fwd(p_b, o, cos, sin, hnw, qnw, knw):
    def body(p_ref, o_ref, cos_ref, sin_ref, hnw_ref, qnw_ref, knw_ref, y_ref, q_ref, k_ref, v_ref):
        i = pl.program_id(0)
        rep = _rep_matrix()
        ones_k = _group_ones(KVW, HEAD_DIM)
        kr = p_ref[:, 1024:1152]
        krstd = lax.rsqrt(_group_mean(kr * kr, ones_k, HEAD_DIM) + EPS)
        kn = kr * krstd * knw_ref[...]
        v_ref[...] = _dot(p_ref[:, 1152:1280].astype(BF16), rep).astype(BF16)

        @pl.when(i == 0)
        def _():
            k_ref[...] = _dot(kn.astype(BF16), rep).astype(BF16)

        @pl.when(i > 0)
        def _():
            cs, sn = cos_ref[...], sin_ref[...]
            kro = kn * cs + _rope_partner(kn) * sn
            k_ref[...] = _dot(kro.astype(BF16), rep).astype(BF16)
            qr = p_ref[:, 512:1024]
            qrstd = lax.rsqrt(_group_mean(qr * qr, _group_ones(ATW, HEAD_DIM), HEAD_DIM) + EPS)
            qn = qr * qrstd * qnw_ref[...]
            qro = qn * _tile_lanes(cs, 4) + _rope_partner(qn) * _tile_lanes(sn, 4)
            q_ref[...] = (qro * HEAD_DIM ** -0.5).astype(BF16)
            ys = []
            for h in range(HG_HEADS):
                oh = o_ref[:, HG_DIM * h:HG_DIM * (h + 1)]
                gh = p_ref[:, HG_DIM * h:HG_DIM * (h + 1)]
                rstd = lax.rsqrt(jnp.mean(oh * oh, axis=-1, keepdims=True) + EPS)
                ys.append(oh * rstd * hnw_ref[...] * (gh * _sigmoid(gh)))
            y_ref[...] = jnp.concatenate(ys, axis=1).astype(BF16)

    return pl.pallas_call(
        body, name="prep_fwd", grid=(N_TILES,),
        in_specs=[pl.BlockSpec((TM, WB), lambda i: (i, 0)),
                  pl.BlockSpec((TM, HGW), lambda i: (_lat(i), 0)),
                  pl.BlockSpec((TM, 128), lambda i: (_lat(i), 0)),
                  pl.BlockSpec((TM, 128), lambda i: (_lat(i), 0)),
                  _full((1, HG_DIM)), _full((1, ATW)), _full((1, KVW))],
        out_specs=[pl.BlockSpec((TM, HGW), lambda i: (_lat(i), 0)),
                   pl.BlockSpec((TM, ATW), lambda i: (_lat(i), 0)),
                   pl.BlockSpec((TM, ATW), lambda i: (i, 0)),
                   pl.BlockSpec((TM, ATW), lambda i: (i, 0))],
        out_shape=[jax.ShapeDtypeStruct((S, HGW), BF16), jax.ShapeDtypeStruct((S, ATW), BF16),
                   jax.ShapeDtypeStruct((T, ATW), BF16), jax.ShapeDtypeStruct((T, ATW), BF16)],
        compiler_params=_cp(("arbitrary",)),
    )(p_b, o, cos, sin, hnw, qnw, knw)


def _prep_bwd(p_b, o, cos, sin, hnw, qnw, knw, dy_hg, dq, dk_rep, dv_rep):
    def body(p_ref, o_ref, cos_ref, sin_ref, hnw_ref, qnw_ref, knw_ref, dy_ref, dq_ref, dk_ref, dv_ref,
             dp_ref, do_ref, acc_ref):
        i = pl.program_id(0)

        @pl.when(i == 0)
        def _():
            acc_ref[...] = jnp.zeros_like(acc_ref)

        rep = _rep_matrix()
        ones_k = _group_ones(KVW, HEAD_DIM)

        def fold(v):
            hi = v.astype(BF16)
            lo = (v - hi.astype(F32)).astype(BF16)
            return _dot_nt(hi, rep) + _dot_nt(lo, rep)

        kr = p_ref[:, 1024:1152]
        krstd = lax.rsqrt(_group_mean(kr * kr, ones_k, HEAD_DIM) + EPS)
        khat = kr * krstd
        kw = knw_ref[...]
        dkro = fold(dk_ref[...])
        dv = fold(dv_ref[...])

        def k_back(dkn):
            dkhat = dkn * kw
            dkr = krstd * (dkhat - khat * _group_mean(dkhat * khat, ones_k, HEAD_DIM))
            acc_ref[2:3, 0:KVW] += jnp.sum(dkn * khat, axis=0, keepdims=True)
            dp_ref[:, 1024:1152] = dkr.astype(BF16)
            dp_ref[:, 1152:1280] = dv.astype(BF16)

        @pl.when(i == 0)
        def _():
            k_back(dkro)
            dp_ref[:, 0:1024] = jnp.zeros((TM, 1024), BF16)

        @pl.when(i > 0)
        def _():
            cs, sn = cos_ref[...], sin_ref[...]
            k_back(dkro * cs + _rope_partner(dkro * sn))
            ones_q = _group_ones(ATW, HEAD_DIM)
            qr = p_ref[:, 512:1024]
            qrstd = lax.rsqrt(_group_mean(qr * qr, ones_q, HEAD_DIM) + EPS)
            qhat = qr * qrstd
            dqro = dq_ref[...] * HEAD_DIM ** -0.5
            dqn = dqro * _tile_lanes(cs, 4) + _rope_partner(dqro * _tile_lanes(sn, 4))
            dqhat = dqn * qnw_ref[...]
            dqr = qrstd * (dqhat - qhat * _group_mean(dqhat * qhat, ones_q, HEAD_DIM))
            acc_ref[1:2, :] += jnp.sum(dqn * qhat, axis=0, keepdims=True)
            dp_ref[:, 512:1024] = dqr.astype(BF16)
            dws = jnp.zeros((1, HG_DIM), F32)
            for h in range(HG_HEADS):
                sl = slice(HG_DIM * h, HG_DIM * (h + 1))
                oh, gh, dy = o_ref[:, sl], p_ref[:, sl], dy_ref[:, sl]
                rstd = lax.rsqrt(jnp.mean(oh * oh, axis=-1, keepdims=True) + EPS)
                ohat = oh * rstd
                sg = _sigmoid(gh)
                dp_ref[:, sl] = (dy * (ohat * hnw_ref[...]) * (sg * (1.0 + gh * (1.0 - sg)))).astype(BF16)
                dn = dy * (gh * sg)
                dws = dws + jnp.sum(dn * ohat, axis=0, keepdims=True)
                dohat = dn * hnw_ref[...]
                do_ref[:, sl] = rstd * (dohat - ohat * jnp.mean(dohat * ohat, axis=-1, keepdims=True))
            acc_ref[0:1, 0:HG_DIM] += dws

    return pl.pallas_call(
        body, name="prep_bwd", grid=(N_TILES,),
        in_specs=[pl.BlockSpec((TM, WB), lambda i: (i, 0)),
                  pl.BlockSpec((TM, HGW), lambda i: (_lat(i), 0)),
                  pl.BlockSpec((TM, 128), lambda i: (_lat(i), 0)),
                  pl.BlockSpec((TM, 128), lambda i: (_lat(i), 0)),
                  _full((1, HG_DIM)), _full((1, ATW)), _full((1, KVW)),
                  pl.BlockSpec((TM, HGW), lambda i: (_lat(i), 0)),
                  pl.BlockSpec((TM, ATW), lambda i: (_lat(i), 0)),
                  pl.BlockSpec((TM, ATW), lambda i: (i, 0)),
                  pl.BlockSpec((TM, ATW), lambda i: (i, 0))],
        out_specs=[pl.BlockSpec((TM, WB), lambda i: (i, 0)),
                   pl.BlockSpec((TM, HGW), lambda i: (_lat(i), 0)),
                   _full((8, ATW))],
        out_shape=[jax.ShapeDtypeStruct((T, WB), BF16), jax.ShapeDtypeStruct((S, HGW), F32),
                   jax.ShapeDtypeStruct((8, ATW), F32)],
        compiler_params=_cp(("arbitrary",)),
    )(p_b, o, cos, sin, hnw, qnw, knw, dy_hg, dq, dk_rep, dv_rep)


NEG = -1e30
_CTX_BLOCKS = L // BLOCK


def _attn_window_specs():
    prev = pl.BlockSpec((BLOCK, ATW), lambda i: (jnp.maximum(i - 1, 0) + _CTX_BLOCKS, 0))
    own = pl.BlockSpec((BLOCK, ATW), lambda i: (i + _CTX_BLOCKS, 0))
    nxt = pl.BlockSpec((BLOCK, ATW), lambda i: (jnp.minimum(i + 1, N_BLOCKS - 1) + _CTX_BLOCKS, 0))
    return [prev, own, nxt, _full((L, ATW))]


def _attn_valid(i):
    qi = lax.broadcasted_iota(jnp.int32, (4 * BLOCK, 3 * BLOCK), 0) % BLOCK
    kj = lax.broadcasted_iota(jnp.int32, (4 * BLOCK, 3 * BLOCK), 1)
    return ((jnp.abs(kj - BLOCK - qi) <= BLOCK) & ((kj >= BLOCK) | (i > 0))
            & ((kj < 2 * BLOCK) | (i < N_BLOCKS - 1)))


def _stack_heads(qg):
    lane = lax.broadcasted_iota(jnp.int32, (1, 256), 1) // HEAD_DIM
    return jnp.concatenate([jnp.where(lane == g, qg, jnp.zeros_like(qg)) for g in range(4)], axis=0)


def _unstack_heads(v4):
    lane = lax.broadcasted_iota(jnp.int32, (1, 256), 1) // HEAD_DIM
    out = jnp.where(lane == 0, v4[0:BLOCK], 0.0)
    for g in range(1, 4):
        out = out + jnp.where(lane == g, v4[g * BLOCK:(g + 1) * BLOCK], 0.0)
    return out


def _sink_rows(sink_ref, hk):
    return jnp.concatenate(
        [jnp.broadcast_to(sink_ref[0:1, 4 * hk + g:4 * hk + g + 1], (BLOCK, 1)) for g in range(4)], axis=0)


def _attn_fwd(q, k_rep, v_rep, sinks, carried=None):
    def body(q_ref, kp, ko, kn, kc, vp, vo, vn, vc, sink_ref, y_ref, lse_ref):
        i = pl.program_id(0)
        valid = _attn_valid(i)
        lane8 = lax.broadcasted_iota(jnp.int32, (1, ATT_HEADS), 1)
        lse_out = jnp.zeros((BLOCK, ATT_HEADS), F32)
        for hk in range(KV_HEADS):
            sl = slice(256 * hk, 256 * (hk + 1))
            q4 = _stack_heads(q_ref[:, sl])
            kl = jnp.concatenate([kp[:, sl], ko[:, sl], kn[:, sl]], axis=0)
            vl = jnp.concatenate([vp[:, sl], vo[:, sl], vn[:, sl]], axis=0)
            s_loc = jnp.where(valid, _dot_nt(q4, kl), NEG)
            s_ctx = _dot_nt(q4, kc[:, sl])
            sink = _sink_rows(sink_ref, hk)
            m = jnp.maximum(jnp.maximum(jnp.max(s_loc, axis=1, keepdims=True),
                                        jnp.max(s_ctx, axis=1, keepdims=True)), sink)
            p_loc = jnp.exp(s_loc - m)
            p_ctx = jnp.exp(s_ctx - m)
            den = (jnp.sum(p_loc, axis=1, keepdims=True) + jnp.sum(p_ctx, axis=1, keepdims=True)
                   + jnp.exp(sink - m))
            o4 = (_dot(p_loc.astype(BF16), vl) + _dot(p_ctx.astype(BF16), vc[:, sl])) / den
            y_ref[:, sl] = _unstack_heads(o4).astype(BF16)
            lse4 = m + jnp.log(den)
            for g in range(4):
                lse_out = lse_out + jnp.where(lane8 == 4 * hk + g, lse4[g * BLOCK:(g + 1) * BLOCK], 0.0)
        lse_ref[...] = lse_out

    return _pcall(
        body, carried, name="attn_fwd", grid=(N_BLOCKS,),
        in_specs=[pl.BlockSpec((BLOCK, ATW), lambda i: (i, 0))] + _attn_window_specs()
        + _attn_window_specs() + [_full((1, ATT_HEADS))],
        out_specs=[pl.BlockSpec((BLOCK, ATW), lambda i: (i, 0)),
                   pl.BlockSpec((BLOCK, ATT_HEADS), lambda i: (i, 0))],
        out_shape=[jax.ShapeDtypeStruct((S, ATW), BF16), jax.ShapeDtypeStruct((S, ATT_HEADS), F32)],
        scratch_shapes=[],
        operands=[q, k_rep, k_rep, k_rep, k_rep, v_rep, v_rep, v_rep, v_rep, sinks])


def _attn_bwd(q, k_rep, v_rep, sinks, y_at, lse, dy, carried=None):
    def body(q_ref, kp, ko, kn, kc, vp, vo, vn, vc, sink_ref, y_ref, lse_ref, dy_ref,
             dq_ref, dk_ref, dv_ref, dsink_ref, dk_acc, dv_acc):
        i = pl.program_id(0)

        @pl.when(i == 0)
        def _():
            dk_acc[...] = jnp.zeros_like(dk_acc)
            dv_acc[...] = jnp.zeros_like(dv_acc)
            dk_ref[pl.ds(0, L), :] = jnp.zeros((L, ATW), F32)
            dv_ref[pl.ds(0, L), :] = jnp.zeros((L, ATW), F32)
            dsink_ref[...] = jnp.zeros_like(dsink_ref)

        valid = _attn_valid(i)
        lane8 = lax.broadcasted_iota(jnp.int32, (1, ATT_HEADS), 1)
        w0 = pl.multiple_of(i * BLOCK, BLOCK)
        dsink = jnp.zeros((1, ATT_HEADS), F32)
        for hk in range(KV_HEADS):
            sl = slice(256 * hk, 256 * (hk + 1))
            q4 = _stack_heads(q_ref[:, sl])
            do4f = _stack_heads(dy_ref[:, sl])
            o4 = _stack_heads(y_ref[:, sl]).astype(F32)
            do4 = do4f.astype(BF16)
            kl = jnp.concatenate([kp[:, sl], ko[:, sl], kn[:, sl]], axis=0)
            vl = jnp.concatenate([vp[:, sl], vo[:, sl], vn[:, sl]], axis=0)
            lse4 = jnp.concatenate(
                [jnp.sum(jnp.where(lane8 == 4 * hk + g, lse_ref[...], 0.0), axis=1, keepdims=True)
                 for g in range(4)], axis=0)
            p_loc = jnp.where(valid, jnp.exp(_dot_nt(q4, kl) - lse4), 0.0)
            p_ctx = jnp.exp(_dot_nt(q4, kc[:, sl]) - lse4)
            delta = jnp.sum(do4f * o4, axis=1, keepdims=True)
            ds_loc = (p_loc * (_dot_nt(do4, vl) - delta)).astype(BF16)
            ds_ctx = (p_ctx * (_dot_nt(do4, vc[:, sl]) - delta)).astype(BF16)
            dq_ref[:, sl] = _unstack_heads(_dot(ds_loc, kl) + _dot(ds_ctx, kc[:, sl]))
            dk_acc[pl.ds(w0, 3 * BLOCK), sl] += _dot_tn(ds_loc, q4)
            dv_acc[pl.ds(w0, 3 * BLOCK), sl] += _dot_tn(p_loc.astype(BF16), do4)
            dk_ref[pl.ds(0, L), sl] += _dot_tn(ds_ctx, q4)
            dv_ref[pl.ds(0, L), sl] += _dot_tn(p_ctx.astype(BF16), do4)
            p_sink = jnp.exp(_sink_rows(sink_ref, hk) - lse4)
            for g in range(4):
                rows = slice(g * BLOCK, (g + 1) * BLOCK)
                dsink = dsink + jnp.where(lane8 == 4 * hk + g,
                                          -jnp.sum(p_sink[rows] * delta[rows], axis=0, keepdims=True), 0.0)
        dsink_ref[...] += dsink

        @pl.when(i == N_BLOCKS - 1)
        def _():
            dk_ref[pl.ds(L, S), :] = dk_acc[pl.ds(BLOCK, S), :]
            dv_ref[pl.ds(L, S), :] = dv_acc[pl.ds(BLOCK, S), :]

    row_q = pl.BlockSpec((BLOCK, ATW), lambda i: (i, 0))
    return _pcall(
        body, carried, name="attn_bwd", grid=(N_BLOCKS,),
        in_specs=[row_q] + _attn_window_specs() + _attn_window_specs()
        + [_full((1, ATT_HEADS)), row_q, pl.BlockSpec((BLOCK, ATT_HEADS), lambda i: (i, 0)), row_q],
        out_specs=[row_q, _full((T, ATW)), _full((T, ATW)), _full((1, ATT_HEADS))],
        out_shape=[jax.ShapeDtypeStruct((S, ATW), F32), jax.ShapeDtypeStruct((T, ATW), F32),
                   jax.ShapeDtypeStruct((T, ATW), F32), jax.ShapeDtypeStruct((1, ATT_HEADS), F32)],
        scratch_shapes=[pltpu.VMEM((S + 2 * BLOCK, ATW), F32), pltpu.VMEM((S + 2 * BLOCK, ATW), F32)],
        operands=[q, k_rep, k_rep, k_rep, k_rep, v_rep, v_rep, v_rep, v_rep, sinks, y_at, lse, dy])


def _merge_fwd(y_hg, y_at, p_c, x, w_bh, w_ba, w_out, g1, nfw, sh2, sc2):
    def body(yh_ref, ya_ref, g_ref, x_ref, wbh_ref, wba_ref, wo_ref, g1_ref, nfw_ref, sh_ref, sc_ref,
             a_ref, b_ref, mx_ref, r_ref, x1_ref, h2_ref):
        a = _dot_nt(yh_ref[...], wbh_ref[...])
        b = _dot_nt(ya_ref[...], wba_ref[...])
        mixed = (_sigmoid(g_ref[:, :D]) * a + _sigmoid(g_ref[:, D:]) * b).astype(BF16)
        r = _dot(mixed, wo_ref[...])
        x1 = x_ref[...] + g1_ref[...] * r
        a_ref[...] = a
        b_ref[...] = b
        mx_ref[...] = mixed
        r_ref[...] = r
        x1_ref[...] = x1
        h2_ref[...] = _rms_mod(x1, nfw_ref[...], sh_ref[...], sc_ref[...]).astype(BF16)

    row = lambda w: pl.BlockSpec((TM, w), lambda i: (i, 0))
    vec = _full((1, D))
    return pl.pallas_call(
        body, name="merge_fwd", grid=(N_LAT_TILES,),
        in_specs=[row(HGW), row(ATW), row(WC), row(D), _VMEM_WHOLE, _VMEM_WHOLE, _VMEM_WHOLE,
                  vec, vec, vec, vec],
        out_specs=[row(D)] * 6,
        out_shape=[jax.ShapeDtypeStruct((S, D), dt) for dt in (F32, F32, BF16, F32, F32, BF16)],
        compiler_params=_cp(("parallel",)),
    )(y_hg, y_at, p_c, x, w_bh, w_ba, w_out, g1, nfw, sh2, sc2)


def _merge_bwd(dx1, r, a, b, p_c, w_bh, w_ba, w_out, g1, carried=None):
    def body(dx_ref, r_ref, a_ref, b_ref, g_ref, wbh_ref, wba_ref, wo_ref, g1_ref,
             dr_ref, da_ref, db_ref, dg_ref, dyh_ref, dya_ref, acc_ref):
        @pl.when(pl.program_id(0) == 0)
        def _():
            acc_ref[...] = jnp.zeros_like(acc_ref)

        dx1v = dx_ref[...]
        acc_ref[0:1, :] += jnp.sum(dx1v * r_ref[...], axis=0, keepdims=True)
        dr = (g1_ref[...] * dx1v).astype(BF16)
        dr_ref[...] = dr
        dmix = _dot_nt(dr, wo_ref[...])
        sh, sa = _sigmoid(g_ref[:, :D]), _sigmoid(g_ref[:, D:])
        da = (dmix * sh).astype(BF16)
        db = (dmix * sa).astype(BF16)
        da_ref[...] = da
        db_ref[...] = db
        dg_ref[:, :D] = (dmix * a_ref[...] * sh * (1.0 - sh)).astype(BF16)
        dg_ref[:, D:] = (dmix * b_ref[...] * sa * (1.0 - sa)).astype(BF16)
        dyh_ref[...] = _dot(da, wbh_ref[...])
        dya_ref[...] = _dot(db, wba_ref[...])

    row = lambda w: pl.BlockSpec((TM, w), lambda i: (i, 0))
    return _pcall(
        body, carried, name="merge_bwd", grid=(N_LAT_TILES,),
        in_specs=[row(D), row(D), row(D), row(D), row(WC), _VMEM_WHOLE, _VMEM_WHOLE, _VMEM_WHOLE,
                  _full((1, D))],
        out_specs=[row(D), row(D), row(D), row(WC), row(HGW), row(ATW), _full((8, D))],
        out_shape=[jax.ShapeDtypeStruct((S, D), BF16), jax.ShapeDtypeStruct((S, D), BF16),
                   jax.ShapeDtypeStruct((S, D), BF16), jax.ShapeDtypeStruct((S, WC), BF16),
                   jax.ShapeDtypeStruct((S, HGW), F32), jax.ShapeDtypeStruct((S, ATW), F32),
                   jax.ShapeDtypeStruct((8, D), F32)],
        scratch_shapes=[], operands=[dx1, r, a, b, p_c, w_bh, w_ba, w_out, g1])


def _ffn_fused(x1, h2, tgt, w_gate, w_up, w_down, g2, nfw, sc2):
    def body(x1_ref, h2_ref, t_ref, wg_ref, wu_ref, wd_ref, g2_ref, nfw_ref, sc_ref,
             act_ref, dgt_ref, dup_ref, df_ref, dx_ref, acc_ref, gs, us):
        @pl.when(pl.program_id(0) == 0)
        def _():
            acc_ref[...] = jnp.zeros_like(acc_ref)

        h2 = h2_ref[...]
        f = jnp.zeros((TM, D), F32)
        for j in range(N_DEV):
            g = _dot_nt(h2, wg_ref[j])
            u = _dot_nt(h2, wu_ref[j])
            gs[j] = g
            us[j] = u
            act = (g * _sigmoid(g) * u).astype(BF16)
            act_ref[j] = act
            f = f + _dot(act, wd_ref[j])
        x1v = x1_ref[...]
        g2 = g2_ref[...]
        diff = x1v + g2 * f - t_ref[...]
        dy = diff * (1.0 / D)
        df = (g2 * dy).astype(BF16)
        df_ref[...] = df
        dh2 = jnp.zeros((TM, D), F32)
        for j in range(N_DEV):
            g, u = gs[j], us[j]
            sg = _sigmoid(g)
            dact = _dot_nt(df, wd_ref[j])
            dgate = (dact * u * (sg * (1.0 + g * (1.0 - sg)))).astype(BF16)
            dup = (dact * (g * sg)).astype(BF16)
            dgt_ref[j] = dgate
            dup_ref[j] = dup
            dh2 = dh2 + _dot(dgate, wg_ref[j]) + _dot(dup, wu_ref[j])
        dx, dsh, dsc, dnw = _rms_mod_bwd(x1v, nfw_ref[...], sc_ref[...], dh2)
        dx_ref[...] = dy + dx
        acc_ref[0:1, :] += dsh
        acc_ref[1:2, :] += dsc
        acc_ref[2:3, :] += dnw
        acc_ref[3:4, :] += jnp.sum(dy * f, axis=0, keepdims=True)
        acc_ref[4:5, :] += 0.5 * jnp.sum(jnp.sum(diff * diff, axis=1, keepdims=True), axis=0,
                                         keepdims=True) * (1.0 / D)

    row = lambda dt_w: pl.BlockSpec((TM, dt_w), lambda i: (i, 0))
    blk = pl.BlockSpec((N_DEV, TM, FF_BLK), lambda i: (0, i, 0))
    vec = _full((1, D))
    return pl.pallas_call(
        body, name="ffn_fused", grid=(N_LAT_TILES,),
        in_specs=[row(D), row(D), row(D), _VMEM_WHOLE, _VMEM_WHOLE, _VMEM_WHOLE, vec, vec, vec],
        out_specs=[blk, blk, blk, row(D), row(D), _full((8, D))],
        out_shape=[jax.ShapeDtypeStruct((N_DEV, S, FF_BLK), BF16)] * 3
        + [jax.ShapeDtypeStruct((S, D), BF16), jax.ShapeDtypeStruct((S, D), F32),
           jax.ShapeDtypeStruct((8, D), F32)],
        scratch_shapes=[pltpu.VMEM((N_DEV, TM, FF_BLK), F32), pltpu.VMEM((N_DEV, TM, FF_BLK), F32)],
        compiler_params=_cp(("arbitrary",)),
    )(x1, h2, tgt, w_gate, w_up, w_down, g2, nfw, sc2)


def _input_bwd(dp_a, dp_b, dp_c, w_a, w_b, w_c, ctx, x, dx1, nw, sh, sc):
    def body(da_ref, db_ref, dc_ref, wa_ref, wb_ref, wc_ref, ctx_ref, x_ref, dx1_ref, nw_ref, sh_ref,
             sc_ref, gx_ref, acc_ref):
        i = pl.program_id(0)

        @pl.when(i == 0)
        def _():
            acc_ref[...] = jnp.zeros_like(acc_ref)

        dh = _dot(da_ref[...], wa_ref[...]) + _dot(db_ref[...], wb_ref[...])

        @pl.when(i == 0)
        def _():
            _, dsh, dsc, dnw = _rms_mod_bwd(ctx_ref[...], nw_ref[...], sc_ref[0:1, :], dh)
            acc_ref[3:4, :] += dsh
            acc_ref[4:5, :] += dsc
            acc_ref[2:3, :] += dnw

        @pl.when(i > 0)
        def _():
            dhl = dh + _dot(dc_ref[...], wc_ref[...])
            dx, dsh, dsc, dnw = _rms_mod_bwd(x_ref[...], nw_ref[...], sc_ref[1:2, :], dhl)
            gx_ref[...] = dx1_ref[...] + dx
            acc_ref[0:1, :] += dsh
            acc_ref[1:2, :] += dsc
            acc_ref[2:3, :] += dnw

    lat = lambda w: pl.BlockSpec((TM, w), lambda i: (_lat(i), 0))
    return pl.pallas_call(
        body, name="input_bwd", grid=(N_TILES,),
        in_specs=[pl.BlockSpec((TM, WA), lambda i: (i, 0)), pl.BlockSpec((TM, WB), lambda i: (i, 0)),
                  lat(WC), _VMEM_WHOLE, _VMEM_WHOLE, _VMEM_WHOLE, _full((TM, D)), lat(D), lat(D),
                  _full((1, D)), _full((2, D)), _full((2, D))],
        out_specs=[lat(D), _full((8, D))],
        out_shape=[jax.ShapeDtypeStruct((S, D), F32), jax.ShapeDtypeStruct((8, D), F32)],
        compiler_params=_cp(("arbitrary",)),
    )(dp_a, dp_b, dp_c, w_a, w_b, w_c, ctx, x, dx1, nw, sh, sc)


_C1 = 1.0 - ADAM_B1 ** ADAM_STEP
_C2 = 1.0 - ADAM_B2 ** ADAM_STEP


def _adamw_math(w, g, m, v):
    m = ADAM_B1 * m + (1.0 - ADAM_B1) * g
    v = ADAM_B2 * v + (1.0 - ADAM_B2) * (g * g)
    m_hat = m / _C1
    v_hat = v / _C2
    delta = -ADAM_LR * (m_hat / (jnp.sqrt(v_hat) + ADAM_EPS) + ADAM_WD * w)
    return delta, m, v


def _adamw_sharded(terms, w, m, v, name, tr):
    rows, cols = w.shape

    def body(t_ref, w_ref, m_ref, v_ref, g_ref, d_ref, nm_ref, nv_ref):
        g = t_ref[0].astype(F32)
        for s in range(1, N_CHIPS):
            g = g + t_ref[s].astype(F32)
        g_ref[...] = g
        d_ref[...], nm_ref[...], nv_ref[...] = _adamw_math(w_ref[...], g, m_ref[...], v_ref[...])

    blk = pl.BlockSpec((tr, cols), lambda i: (i, 0))
    return pl.pallas_call(
        body, name=name, grid=(rows // tr,),
        in_specs=[pl.BlockSpec((N_CHIPS, tr, cols), lambda i: (0, i, 0)), blk, blk, blk],
        out_specs=[blk] * 4,
        out_shape=[jax.ShapeDtypeStruct((rows, cols), F32)] * 4,
        compiler_params=_cp(("parallel",)),
    )(terms, w, m, v)


def _adamw_plain(g, w, m, v, name):
    def body(g_ref, w_ref, m_ref, v_ref, d_ref, nm_ref, nv_ref):
        d_ref[...], nm_ref[...], nv_ref[...] = _adamw_math(w_ref[...], g_ref[...], m_ref[...], v_ref[...])

    return pl.pallas_call(
        body, name=name, in_specs=[_VMEM_WHOLE] * 4, out_specs=[_VMEM_WHOLE] * 3,
        out_shape=[jax.ShapeDtypeStruct(w.shape, F32)] * 3,
        compiler_params=_cp(),
    )(g, w, m, v)


SMALL_ROWS = 16
R_DMOD, R_DCTX, R_NMIX, R_NFFN, R_MISC, R_DLB, R_BADA01 = 0, 6, 8, 9, 10, 11, 13
M_HNW, M_QNW, M_KNW, M_SINK, M_LOSS = 0, 128, 256, 384, 512


def _pack_small(acc_in, acc_mg, acc_ffn, acc_prep, dsink, dlb):
    def body(in_ref, mg_ref, ff_ref, pp_ref, ds_ref, dlb_ref, o_ref):
        o_ref[...] = jnp.zeros_like(o_ref)
        o_ref[0:2, :] = in_ref[0:2, :]
        o_ref[2:3, :] = mg_ref[0:1, :]
        o_ref[3:5, :] = ff_ref[0:2, :]
        o_ref[5:6, :] = ff_ref[3:4, :]
        o_ref[6:8, :] = in_ref[3:5, :]
        o_ref[8:9, :] = in_ref[2:3, :]
        o_ref[9:10, :] = ff_ref[2:3, :]
        o_ref[10:11, M_HNW:M_HNW + HG_DIM] = pp_ref[0:1, 0:HG_DIM]
        r = lax.broadcasted_iota(jnp.int32, (ATW, 128), 0)
        c = lax.broadcasted_iota(jnp.int32, (ATW, 128), 1)
        fold = jnp.where((r % HEAD_DIM == c) & (c < HEAD_DIM), 1.0, 0.0).astype(BF16)
        qk = jnp.concatenate([pp_ref[1:2, :], pp_ref[2:3, :], jnp.zeros((6, ATW), F32)], axis=0)
        folded = _dot_exact_rhs01(qk, fold)
        o_ref[10:11, M_QNW:M_QNW + 128] = folded[0:1, :]
        o_ref[10:11, M_KNW:M_KNW + 128] = folded[1:2, :]
        o_ref[10:11, M_SINK:M_SINK + ATT_HEADS] = ds_ref[...]
        o_ref[10:11, M_LOSS:M_LOSS + 128] = ff_ref[4:5, 0:128]
        o_ref[11:13, 0:HGW] = dlb_ref[...]

    return pl.pallas_call(
        body, name="pack_small", in_specs=[_VMEM_WHOLE] * 6, out_specs=_VMEM_WHOLE,
        out_shape=jax.ShapeDtypeStruct((SMALL_ROWS, D), F32), compiler_params=_cp(),
    )(acc_in, acc_mg, acc_ffn, acc_prep, dsink, dlb)


def _sum_small(gathered):
    def body(g_ref, o_ref):
        tot = g_ref[0]
        for s in range(1, N_DEV):
            tot = tot + g_ref[s]
        o_ref[...] = tot
        o_ref[R_BADA01:R_BADA01 + 2, :] = tot[0:2, :] + tot[R_DCTX:R_DCTX + 2, :]

    return pl.pallas_call(
        body, name="sum_small", in_specs=[_VMEM_WHOLE], out_specs=_VMEM_WHOLE,
        out_shape=jax.ShapeDtypeStruct((SMALL_ROWS, D), F32), compiler_params=_cp(),
    )(gathered)


_REP_NAMES = ("b_ada", "c_ctx", "norm_mix_w", "norm_ffn_w", "hgrn_norm_w", "q_norm_w", "k_norm_w", "attn_sinks")


def _adamw_replicated(tot, g_c_ctx, ws, ms, vs):
    n = len(_REP_NAMES)

    def body(*refs):
        tot_ref, gc_ref = refs[0], refs[1]
        w_refs, m_refs, v_refs = refs[2:2 + n], refs[2 + n:2 + 2 * n], refs[2 + 2 * n:2 + 3 * n]
        outs = refs[2 + 3 * n:]
        row = lambda r: tot_ref[r:r + 1, :]
        misc = row(R_MISC)
        grads = [jnp.concatenate([row(R_BADA01), row(R_BADA01 + 1)] + [row(k) for k in range(2, 6)], axis=1),
                 gc_ref[...], row(R_NMIX), row(R_NFFN),
                 misc[:, M_HNW:M_HNW + HG_DIM], misc[:, M_QNW:M_QNW + HEAD_DIM],
                 misc[:, M_KNW:M_KNW + HEAD_DIM], misc[:, M_SINK:M_SINK + ATT_HEADS]]
        for k in range(n):
            outs[k][...] = grads[k]
            outs[n + k][...], outs[2 * n + k][...], outs[3 * n + k][...] = _adamw_math(
                w_refs[k][...], grads[k], m_refs[k][...], v_refs[k][...])

    shapes = [jax.ShapeDtypeStruct(w.shape, F32) for w in ws]
    return pl.pallas_call(
        body, name="adamw_replicated", in_specs=[_VMEM_WHOLE] * (2 + 3 * n), out_specs=[_VMEM_WHOLE] * (4 * n),
        out_shape=shapes * 4, compiler_params=_cp(),
    )(tot, g_c_ctx, *ws, *ms, *vs)


def _lb_grads(dlb, lbl):
    def body(d_ref, l_ref, o_ref):
        for d in (0, 1):
            ll = l_ref[d]
            lb = _sigmoid(ll[0:1, :] - ll[1:2, :])
            t = d_ref[d:d + 1, :] * lb * (1.0 - lb)
            o_ref[d, 0:1, :] = t
            o_ref[d, 1:2, :] = -t

    return pl.pallas_call(
        body, name="lb_grads", in_specs=[_VMEM_WHOLE] * 2, out_specs=_VMEM_WHOLE,
        out_shape=jax.ShapeDtypeStruct((2, 2, HGW), F32), compiler_params=_cp(),
    )(dlb, lbl)


def _c_ctx_grad(terms, c_ctx):
    def body(t_ref, c_ref, o_ref):
        tot = t_ref[0, 8:9, :]
        for s in range(1, N_DEV):
            tot = tot + t_ref[s, 8:9, :]
        cv = c_ref[...]
        sg = _sigmoid(cv)
        o_ref[...] = tot * (sg * (1.0 + cv * (1.0 - sg)))

    return pl.pallas_call(
        body, name="c_ctx_grad", in_specs=[_VMEM_WHOLE] * 2, out_specs=_VMEM_WHOLE,
        out_shape=jax.ShapeDtypeStruct((1, D), F32), compiler_params=_cp(),
    )(terms, c_ctx)


def _in_perm():
    fz, bz, inp, kk, vv, qhg, ghg, qat, gates = 0, 512, 1024, 1536, 1664, 1792, 2304, 2816, 3328
    cols = []
    for h in range(HG_HEADS):
        for base in (fz, bz, inp, qhg):
            cols += list(range(base + 128 * h, base + 128 * (h + 1)))
    cols += list(range(ghg, ghg + 512)) + list(range(qat, qat + 512))
    cols += list(range(kk, kk + 128)) + list(range(vv, vv + 128))
    cols += list(range(gates, gates + 2048))
    return np.asarray(cols, np.int32)


_PERM = _in_perm()
_INV_PERM = np.argsort(_PERM).astype(np.int32)


def _take_rows(w, perm):
    cuts = [0] + [i for i in range(1, len(perm)) if perm[i] != perm[i - 1] + 1] + [len(perm)]
    return jnp.concatenate([w[int(perm[a]):int(perm[b - 1]) + 1] for a, b in zip(cuts[:-1], cuts[1:])],
                           axis=0)


def _cols_from_blocks(g):
    return jnp.transpose(g, (1, 0, 2)).reshape(g.shape[1], N_DEV * g.shape[2])


def _local_step(x2, ctx2, tgt, lbl, sh_in, sc_in, gate1, sh2, sc2, gate2, norm_mix_w, norm_ffn_w,
                hgrn_norm_w, q_norm_w, k_norm_w, attn_sinks, w_a, w_b, w_c, s_bh, s_ba, s_out,
                s_gate, s_up, s_down):
    first_last = lambda n: [(0, True), (n - 1, False)]
    h_all = _norm_mod_all(ctx2, x2, norm_mix_w, sh_in, sc_in)
    p_a = _mm_nt(h_all, w_a, tm=768, tn=1024, out_dtype=F32, name="proj_a")
    (o, st), (g_bh, g_ba, g_out, g_gate) = _hgrn_fwd(
        p_a, lbl, (_gather_comm([s_bh, s_ba, s_out, s_gate]), [(0, True), (HG_HEADS - 1, True), (HG_HEADS - 1, False)]))
    p_b = _mm_nt(h_all, w_b, tm=768, tn=1280, out_dtype=F32, name="proj_b")
    p_c = _mm_nt(h_all, w_c, tm=256, tn=2048, out_dtype=F32, name="proj_c", row_off=1, rows=S)
    cos, sin = _rope_tables()
    qnw_t, knw_t = jnp.tile(q_norm_w, (1, ATT_HEADS)), jnp.tile(k_norm_w, (1, KV_HEADS))
    y_hg, qn, k_rep, v_rep = _prep_fwd(p_b, o, cos, sin, hgrn_norm_w, qnw_t, knw_t)
    (y_at, lse), (g_up, g_down) = _attn_fwd(
        qn, k_rep, v_rep, attn_sinks,
        (_gather_comm([s_up, s_down]), [(0, True), (N_BLOCKS - 3, True), (N_BLOCKS - 1, False)]))
    w_bh, w_ba, w_o = g_bh.reshape(D, HGW), g_ba.reshape(D, ATW), g_out.reshape(D, D)
    g_gate, g_up, g_down = [g.reshape(N_DEV, FF_BLK, D) for g in (g_gate, g_up, g_down)]
    a, b, mixed, r, x1, h2 = _merge_fwd(y_hg, y_at, p_c, x2, w_bh, w_ba, w_o, gate1, norm_ffn_w, sh2, sc2)

    act, d_gate, d_up, d_f, dx1, acc_ffn = _ffn_fused(x1, h2, tgt, g_gate, g_up, g_down, gate2,
                                                      norm_ffn_w, sc2)
    by_chip = lambda t: t.reshape((N_CHIPS, 2) + t.shape[1:])
    t_down, _ = _mm_tn_blocked(act, d_f, "grad_down")
    t_down = by_chip(t_down)
    t_gate, (f_down,) = _mm_tn_blocked(d_gate, h2, "grad_gate", (_sibling_comm([t_down]), first_last(N_DEV)))
    t_gate = by_chip(t_gate)
    t_up, (f_gate,) = _mm_tn_blocked(d_up, h2, "grad_up", (_sibling_comm([t_gate]), first_last(N_DEV)))
    t_up = by_chip(t_up)

    (d_r, d_a, d_b, dp_c, dy_hg, dy_at, acc_mg), (f_up,) = _merge_bwd(
        dx1, r, a, b, p_c, w_bh, w_ba, w_o, gate1, (_sibling_comm([t_up]), first_last(N_LAT_TILES)))
    c_down, c_gate, c_up = [_pair_sum(t, f, "pair_sum_" + nm) for t, f, nm in
                            ((t_down, f_down, "down"), (t_gate, f_gate, "gate"), (t_up, f_up, "up"))]
    t_out = _mm_tn(mixed, d_r, tk=512, nk=4, tm=512, tn=1024, out_dtype=BF16, name="grad_out")
    t_bh = _mm_tn(d_a, y_hg, tk=512, nk=4, tm=512, tn=512, out_dtype=BF16, name="grad_bh")
    t_ba = _mm_tn(d_b, y_at, tk=512, nk=4, tm=512, tn=512, out_dtype=BF16, name="grad_ba")
    t_bh, t_ba, t_out = [by_chip(t.reshape(N_DEV, D // N_DEV, t.shape[1])) for t in (t_bh, t_ba, t_out)]
    (dq, dk_rep, dv_rep, dsink), (f_bh, f_ba, f_out) = _attn_bwd(
        qn, k_rep, v_rep, attn_sinks, y_at, lse, dy_at, (_sibling_comm([t_bh, t_ba, t_out]), first_last(N_BLOCKS)))
    c_bh, c_ba, c_out = [_pair_sum(t, f, "pair_sum_" + nm) for t, f, nm in
                         ((t_bh, f_bh, "bh"), (t_ba, f_ba, "ba"), (t_out, f_out, "out"))]
    dp_b, d_o, acc_prep = _prep_bwd(p_b, o, cos, sin, hgrn_norm_w, qnw_t, knw_t,
                                    dy_hg, dq, dk_rep, dv_rep)
    (dp_a, dlb), (r_bh, r_ba, r_out, r_gate, r_up, r_down) = _hgrn_bwd(
        p_a, lbl, d_o, st, (_chip_comm([c_bh, c_ba, c_out, c_gate, c_up, c_down]), first_last(HG_HEADS)))
    grad_x, acc_in = _input_bwd(dp_a, dp_b, dp_c, w_a, w_b, w_c, ctx2, x2, dx1, norm_mix_w, sh_in, sc_in)
    t_a = _mm_tn(dp_a, h_all, tk=768, nk=3, tm=1024, tn=1024, out_dtype=BF16, name="grad_in_a")
    t_b = _mm_tn(dp_b, h_all, tk=768, nk=3, tm=640, tn=1024, out_dtype=BF16, name="grad_in_b")
    t_c = _mm_tn(dp_c, h_all, tk=256, nk=8, b_off=1, tm=1024, tn=1024, out_dtype=BF16, name="grad_in_c")
    t_in = by_chip(_take_rows(jnp.concatenate([t_a, t_b, t_c], axis=0), _INV_PERM).reshape(N_DEV, IN_BLK, D))
    (f_in,) = _run_comm(_sibling_comm([t_in]), "scatter_in_sibling")
    (r_in,) = _run_comm(_chip_comm([_pair_sum(t_in, f_in, "pair_sum_in")]), "scatter_in_chips")
    small = _pack_small(acc_in, acc_mg, acc_ffn, acc_prep, dsink, dlb)
    return grad_x, small, [r_in, r_bh, r_ba, r_out, r_gate, r_up, r_down]


def kernel(x, c, ctx, c_ctx, w_ada, b_ada, norm_mix_w, norm_ffn_w, w_in, hgrn_lb_logits, hgrn_norm_w, q_norm_w, k_norm_w, attn_sinks, w_branch_hgrn, w_branch_attn, w_out, w_ffn_gate, w_ffn_up, w_ffn_down, loss_target, m_c_ctx, m_w_ada, m_b_ada, m_norm_mix_w, m_norm_ffn_w, m_w_in, m_hgrn_lb_logits, m_hgrn_norm_w, m_q_norm_w, m_k_norm_w, m_attn_sinks, m_w_branch_hgrn, m_w_branch_attn, m_w_out, m_w_ffn_gate, m_w_ffn_up, m_w_ffn_down, v_c_ctx, v_w_ada, v_b_ada, v_norm_mix_w, v_norm_ffn_w, v_w_in, v_hgrn_lb_logits, v_hgrn_norm_w, v_q_norm_w, v_k_norm_w, v_attn_sinks, v_w_branch_hgrn, v_w_branch_attn, v_w_out, v_w_ffn_gate, v_w_ffn_up, v_w_ffn_down):
    me = 4 * lax.axis_index("x") + 2 * lax.axis_index("y") + lax.axis_index("c")
    x2, ctx2, tgt = x[0], ctx[0], loss_target[0]
    w_ada2, w_in2 = w_ada[0], w_in[0]

    blk = jnp.zeros((8, D), F32).at[0].set(c[0]).at[1, :256].set(hgrn_lb_logits.reshape(256))
    (g0,) = _all_gather([blk], "gather_cond", True)
    cc = jnp.zeros((16, D), F32).at[:8].set(g0[:, 0, :]).at[8].set(c_ctx)
    lbl = jnp.transpose(g0[:, 1, :256].reshape(N_DEV, 2, 2, 64), (1, 2, 0, 3)).reshape(2, 2, HGW)

    b_cols = lax.dynamic_slice(b_ada, (0, me * ADA_BLK), (1, ADA_BLK))
    (g1,) = _all_gather([_ada_rows(cc, w_ada2, b_cols)], "gather_mod", True)
    mod_all = _cols_from_blocks(g1)
    mod = lax.dynamic_slice(mod_all, (me, 0), (1, 6 * D)).reshape(6, D)
    mod_c = mod_all[8].reshape(6, D)
    sh1, sc1, gate1, sh2, sc2, gate2 = [mod[k:k + 1] for k in range(6)]
    sh_in = jnp.concatenate([mod_c[0:1], sh1], axis=0)
    sc_in = jnp.concatenate([mod_c[1:2], sc1], axis=0)

    shards = [w_branch_hgrn[0].T, w_branch_attn[0].T, w_out[0], w_ffn_gate[0].T, w_ffn_up[0].T, w_ffn_down[0]]
    (g_in,) = _all_gather([w_in2.T.astype(BF16)], "gather_w_in", False)
    w_in_t = _take_rows(g_in.reshape(IN_COLS, D), _PERM)
    w_a, w_b, w_c = w_in_t[:WA], w_in_t[WA:WA + WB], w_in_t[WA + WB:]

    grad_x, small, (r_in, r_bh, r_ba, r_out, r_gate, r_up, r_down) = _local_step(
        x2, ctx2, tgt, lbl, sh_in, sc_in, gate1, sh2, sc2, gate2, norm_mix_w, norm_ffn_w, hgrn_norm_w,
        q_norm_w, k_norm_w, attn_sinks, w_a, w_b, w_c, *[s.astype(BF16) for s in shards])

    big = {}
    for nm, rr, ww, mm, vv, tr, transposed in (
            ("w_in", r_in, w_in2, m_w_in[0], v_w_in[0], 336, True),
            ("w_branch_hgrn", r_bh, w_branch_hgrn[0], m_w_branch_hgrn[0], v_w_branch_hgrn[0], 128, True),
            ("w_branch_attn", r_ba, w_branch_attn[0], m_w_branch_attn[0], v_w_branch_attn[0], 128, True),
            ("w_out", r_out, w_out[0], m_w_out[0], v_w_out[0], 128, False),
            ("w_ffn_gate", r_gate, w_ffn_gate[0], m_w_ffn_gate[0], v_w_ffn_gate[0], 352, True),
            ("w_ffn_up", r_up, w_ffn_up[0], m_w_ffn_up[0], v_w_ffn_up[0], 352, True),
            ("w_ffn_down", r_down, w_ffn_down[0], m_w_ffn_down[0], v_w_ffn_down[0], 352, False)):
        if transposed:
            res = _adamw_sharded(rr, ww.T, mm.T, vv.T, "adamw_" + nm, tr)
            big[nm] = [t.T[None] for t in res]
        else:
            big[nm] = [t[None] for t in _adamw_sharded(rr, ww, mm, vv, "adamw_" + nm, tr)]

    (g2,) = _all_gather([small], "gather_small", True)
    tot = _sum_small(g2)
    dm = jnp.zeros((16, 6 * D), F32).at[:8].set(g2[:, R_DMOD:R_DMOD + 6, :].reshape(N_DEV, 6 * D))
    dm = dm.at[8, :2 * D].set(tot[R_DCTX:R_DCTX + 2].reshape(2 * D))
    dm_cols = lax.dynamic_slice(dm, (0, me * ADA_BLK), (16, ADA_BLK))
    g_w_ada, dsc_term = _ada_grads(cc, dm_cols, w_ada2)
    (g3,) = _all_gather([dsc_term], "gather_cctx", True)
    g_c_ctx = _c_ctx_grad(g3, c_ctx.reshape(1, D))
    g_lbl = _lb_grads(tot[R_DLB:R_DLB + 2, :HGW], lbl)
    g_lb_mine = lax.dynamic_slice(g_lbl, (0, 0, me * 64), (2, 2, 64))
    misc = tot[R_MISC]
    loss = misc[M_LOSS]

    rep_out = _adamw_replicated(
        tot, g_c_ctx,
        [b_ada, c_ctx.reshape(1, D), norm_mix_w, norm_ffn_w, hgrn_norm_w, q_norm_w, k_norm_w, attn_sinks],
        [m_b_ada, m_c_ctx.reshape(1, D), m_norm_mix_w, m_norm_ffn_w, m_hgrn_norm_w, m_q_norm_w, m_k_norm_w,
         m_attn_sinks],
        [v_b_ada, v_c_ctx.reshape(1, D), v_norm_mix_w, v_norm_ffn_w, v_hgrn_norm_w, v_q_norm_w, v_k_norm_w,
         v_attn_sinks])
    rep = []
    for kind in range(4):
        vals = dict(zip(_REP_NAMES, rep_out[kind * len(_REP_NAMES):(kind + 1) * len(_REP_NAMES)]))
        vals["c_ctx"] = vals["c_ctx"].reshape(D)
        rep.append(vals)

    d_ada, nm_ada, nv_ada = _adamw_plain(g_w_ada, w_ada2, m_w_ada[0], v_w_ada[0], "adamw_w_ada")
    ada = [t[None] for t in (g_w_ada, d_ada, nm_ada, nv_ada)]
    lb_w = hgrn_lb_logits.reshape(4, 64)
    d_lb, nm_lb, nv_lb = _adamw_plain(g_lb_mine.reshape(4, 64), lb_w, m_hgrn_lb_logits.reshape(4, 64),
                                      v_hgrn_lb_logits.reshape(4, 64), "adamw_lb")
    lbs = [t.reshape(2, 2, 64) for t in (g_lb_mine, d_lb, nm_lb, nv_lb)]

    names = ['c_ctx', 'w_ada', 'b_ada', 'norm_mix_w', 'norm_ffn_w', 'w_in', 'hgrn_lb_logits', 'hgrn_norm_w',
             'q_norm_w', 'k_norm_w', 'attn_sinks', 'w_branch_hgrn', 'w_branch_attn', 'w_out', 'w_ffn_gate',
             'w_ffn_up', 'w_ffn_down']
    outs = [loss, grad_x[None]]
    for kind in range(4):
        for nm in names:
            if nm == 'w_ada':
                outs.append(ada[kind])
            elif nm == 'hgrn_lb_logits':
                outs.append(lbs[kind])
            elif nm in big:
                outs.append(big[nm][kind])
            else:
                outs.append(rep[kind][nm])
    return tuple(outs)
```

```python
import functools
import math

import numpy as np
import jax
import jax.numpy as jnp
from jax import lax
from jax.experimental import pallas as pl
from jax.experimental.pallas import tpu as pltpu

F32 = jnp.float32
BF16 = jnp.bfloat16

N_DEV = 8
D = 1024
S = 2048
L = 256
T = L + S
TM = 256
N_TILES = T // TM
N_LAT_TILES = S // TM
HG_HEADS = 4
HG_DIM = 128
HGW = 512
CHUNK = 32
N_CHUNKS = T // CHUNK
N_CTX_CHUNKS = L // CHUNK
N_LAT_CHUNKS = S // CHUNK
ATT_HEADS = 8
KV_HEADS = 2
HEAD_DIM = 64
ATW = 512
KVW = 128
BLOCK = 128
N_BLOCKS = S // BLOCK
GRID_W = 64
ROPE_THETA = 10000.0
D_FF = 2816
FF_BLK = D_FF // N_DEV
FF_TILE = 256
N_FF_TILES = D_FF // FF_TILE
IN_COLS = 5376
IN_BLK = IN_COLS // N_DEV
ADA_BLK = 6 * D // N_DEV
EPS = 1e-6
WA, WB, WC = 2048, 1280, 2048

ADAM_LR = 0.001
ADAM_B1 = 0.9
ADAM_B2 = 0.999
ADAM_EPS = 1e-08
ADAM_WD = 0.01
ADAM_STEP = 10

VMEM_LIMIT = 56 * 1024 * 1024
MESH = pl.DeviceIdType.MESH


def _cp(sem=None, vmem=VMEM_LIMIT):
    return pltpu.CompilerParams(dimension_semantics=sem, vmem_limit_bytes=vmem)


def _full(shape):
    n = len(shape)
    return pl.BlockSpec(shape, lambda *_: (0,) * n)


_VMEM_WHOLE = pl.BlockSpec(memory_space=pltpu.VMEM)
_ANY = pl.BlockSpec(memory_space=pl.ANY)


def _sigmoid(v):
    return 1.0 / (1.0 + jnp.exp(-v))


def _dot(a, b):
    return jnp.dot(a, b, preferred_element_type=F32)


def _dot_nt(a, b):
    return lax.dot_general(a, b, (((1,), (1,)), ((), ())), preferred_element_type=F32)


def _dot_tn(a, b):
    return lax.dot_general(a, b, (((0,), (0,)), ((), ())), preferred_element_type=F32)


def _split3(v):
    hi = v.astype(BF16)
    r = v - hi.astype(F32)
    mid = r.astype(BF16)
    lo = (r - mid.astype(F32)).astype(BF16)
    return hi, mid, lo


def _dot_exact_rhs01(v, m01):
    hi, mid, lo = _split3(v)
    return _dot(hi, m01) + _dot(mid, m01) + _dot(lo, m01)


def _dot_exact_lhs01(m01, v):
    hi, mid, lo = _split3(v)
    return _dot(m01, hi) + _dot(m01, mid) + _dot(m01, lo)


def _dot_f32(a, b, dot=_dot):
    ah, am, al = _split3(a)
    bh, bm, bl = _split3(b)
    return (dot(ah, bh) + (dot(ah, bm) + dot(am, bh))
            + (dot(am, bm) + dot(ah, bl) + dot(al, bh)))


def _my_pos():
    return lax.axis_index("x"), lax.axis_index("y"), lax.axis_index("c")


class _Comm:
    def __init__(self, operands, out_shapes, sems, phases):
        self.operands, self.out_shapes, self.sems, self.phases = operands, out_shapes, sems, phases


def _gather_comm(blocks):
    n = len(blocks)

    def parts(ins, outs, sems):
        send_sems, recv_sems, local_sems = sems
        x, y, c = _my_pos()
        me, sibling = (x, y, c), (x, y, 1 - c)
        chips = [(1 - x, y), (x, 1 - y), (1 - x, 1 - y)]

        def slot(a, px, py, pc):
            return outs[a].at[4 * px + 2 * py + pc]

        def copy(a, k, block, to, src=None):
            return pltpu.make_async_remote_copy(
                src_ref=slot(a, *block) if src is None else src, dst_ref=slot(a, *block),
                send_sem=send_sems.at[a, k], recv_sem=recv_sems.at[a, k],
                device_id=to, device_id_type=MESH)

        mine = [pltpu.make_async_copy(ins[a], slot(a, *me), local_sems.at[a]) for a in range(n)]
        first = []
        for a in range(n):
            first.append(copy(a, 0, me, sibling, src=ins[a]))
            first += [copy(a, 1 + j, me, (*chip, c), src=ins[a]) for j, chip in enumerate(chips)]
        passed = [copy(a, 4 + j, (*chip, c), sibling) for j, chip in enumerate(chips) for a in range(n)]
        return c, me, sibling, chips, copy, mine, first, passed

    def start(ins, outs, sems):
        _, _, _, _, _, mine, first, _ = parts(ins, outs, sems)
        for cp in mine + first:
            cp.start()

    def forward(ins, outs, sems):
        c, me, _, chips, copy, _, _, passed = parts(ins, outs, sems)
        for j, chip in enumerate(chips):
            for a in range(n):
                copy(a, 1 + j, (*chip, c), me).wait_recv()
                passed[j * n + a].start()

    def finish(ins, outs, sems):
        c, me, sibling, chips, copy, mine, first, passed = parts(ins, outs, sems)
        for a in range(n):
            copy(a, 0, sibling, me).wait_recv()
            for j, chip in enumerate(chips):
                copy(a, 4 + j, (*chip, 1 - c), me).wait_recv()
        for cp in first + passed:
            cp.wait_send()
        for cp in mine:
            cp.wait()

    return _Comm(blocks, [jax.ShapeDtypeStruct((N_DEV,) + b.shape, b.dtype) for b in blocks],
                 [pltpu.SemaphoreType.DMA((n, 7)), pltpu.SemaphoreType.DMA((n, 7)), pltpu.SemaphoreType.DMA((n,))],
                 [start, forward, finish])


def _run_comm(comm, name, in_vmem=False):
    n_in, n_out = len(comm.operands), len(comm.out_shapes)

    def body(*refs):
        ins, outs, sems = refs[:n_in], refs[n_in:n_in + n_out], refs[n_in + n_out:]
        for phase in comm.phases:
            phase(ins, outs, sems)

    spec = _VMEM_WHOLE if in_vmem else _ANY
    return pl.pallas_call(
        body, name=name, out_shape=comm.out_shapes, in_specs=[spec] * n_in, out_specs=[spec] * n_out,
        scratch_shapes=comm.sems,
    )(*comm.operands)


def _carrier_call(body, comm, schedule, *, name, grid, in_specs, out_specs, out_shape, scratch_shapes, operands):
    n_in, n_out, n_scr = len(in_specs), len(out_specs), len(scratch_shapes)
    c_in, c_out = len(comm.operands), len(comm.out_shapes)

    def full_body(*refs):
        ins, refs = refs[:n_in], refs[n_in:]
        cins, refs = refs[:c_in], refs[c_in:]
        outs, refs = refs[:n_out], refs[n_out:]
        couts, refs = refs[:c_out], refs[c_out:]
        scr, csems = refs[:n_scr], refs[n_scr:]
        step = pl.program_id(0)

        def run(before):
            for (at, when_before), phase in zip(schedule, comm.phases):
                if when_before == before:
                    pl.when(step == at)(functools.partial(phase, cins, couts, csems))

        run(True)
        body(*ins, *outs, *scr)
        run(False)

    res = pl.pallas_call(
        full_body, name=name, grid=grid,
        in_specs=list(in_specs) + [_ANY] * c_in, out_specs=list(out_specs) + [_ANY] * c_out,
        out_shape=list(out_shape) + list(comm.out_shapes),
        scratch_shapes=list(scratch_shapes) + list(comm.sems),
        compiler_params=_cp(("arbitrary",)),
    )(*operands, *comm.operands)
    return res[:n_out], res[n_out:]


def _pcall(body, carried, *, name, grid, in_specs, out_specs, out_shape, scratch_shapes, operands):
    if carried is None:
        res = pl.pallas_call(body, name=name, grid=grid, in_specs=in_specs, out_specs=out_specs,
                             out_shape=out_shape, scratch_shapes=scratch_shapes,
                             compiler_params=_cp(("arbitrary",)))(*operands)
        return res, ()
    return _carrier_call(body, carried[0], carried[1], name=name, grid=grid, in_specs=in_specs,
                         out_specs=out_specs, out_shape=out_shape, scratch_shapes=scratch_shapes,
                         operands=operands)


def _all_gather(blocks, name, in_vmem):
    return _run_comm(_gather_comm(blocks), name, in_vmem)


N_CHIPS = 4


def _sibling_comm(contribs):
    n = len(contribs)

    def copies(ins, outs, sems):
        send_sems, recv_sems = sems
        x, y, c = _my_pos()
        return [pltpu.make_async_remote_copy(
            src_ref=ins[a].at[pl.ds(0, N_CHIPS), 1 - c], dst_ref=outs[a],
            send_sem=send_sems.at[a], recv_sem=recv_sems.at[a],
            device_id=(x, y, 1 - c), device_id_type=MESH) for a in range(n)]

    def start(ins, outs, sems):
        for cp in copies(ins, outs, sems):
            cp.start()

    def finish(ins, outs, sems):
        cps = copies(ins, outs, sems)
        for cp in cps:
            cp.wait_recv()
        for cp in cps:
            cp.wait_send()

    return _Comm(contribs, [jax.ShapeDtypeStruct((N_CHIPS,) + b.shape[2:], b.dtype) for b in contribs],
                 [pltpu.SemaphoreType.DMA((n,)), pltpu.SemaphoreType.DMA((n,))], [start, finish])


def _pair_sum(mine, theirs, name):
    _, _, rows, cols = mine.shape

    def body(m_ref, t_ref, o_ref):
        c = lax.axis_index("c")
        o_ref[...] = (m_ref[c].astype(F32) + t_ref[...].astype(F32)).astype(BF16)

    return pl.pallas_call(
        body, name=name, grid=(N_CHIPS,),
        in_specs=[pl.BlockSpec((None, 2, rows, cols), lambda q: (q, 0, 0, 0)),
                  pl.BlockSpec((None, rows, cols), lambda q: (q, 0, 0))],
        out_specs=pl.BlockSpec((None, rows, cols), lambda q: (q, 0, 0)),
        out_shape=jax.ShapeDtypeStruct((N_CHIPS, rows, cols), BF16),
        compiler_params=_cp(("parallel",)),
    )(mine, theirs)


def _chip_comm(sums):
    n = len(sums)

    def parts(ins, outs, sems):
        send_sems, recv_sems, local_sems = sems
        x, y, c = _my_pos()
        q_me = 2 * x + y
        chips = [(1 - x, y), (x, 1 - y), (1 - x, 1 - y)]
        mine = [pltpu.make_async_copy(ins[a].at[q_me], outs[a].at[q_me], local_sems.at[a]) for a in range(n)]
        sends, recvs = [], []
        for j, (px, py) in enumerate(chips):
            for a in range(n):
                q = 2 * px + py
                sends.append(pltpu.make_async_remote_copy(
                    src_ref=ins[a].at[q], dst_ref=outs[a].at[q_me],
                    send_sem=send_sems.at[a, j], recv_sem=recv_sems.at[a, j],
                    device_id=(px, py, c), device_id_type=MESH))
                recvs.append(pltpu.make_async_remote_copy(
                    src_ref=ins[a].at[q], dst_ref=outs[a].at[q],
                    send_sem=send_sems.at[a, j], recv_sem=recv_sems.at[a, j],
                    device_id=(x, y, c), device_id_type=MESH))
        return mine, sends, recvs

    def start(ins, outs, sems):
        mine, sends, _ = parts(ins, outs, sems)
        for cp in mine + sends:
            cp.start()

    def finish(ins, outs, sems):
        mine, sends, recvs = parts(ins, outs, sems)
        for cp in recvs:
            cp.wait_recv()
        for cp in sends:
            cp.wait_send()
        for cp in mine:
            cp.wait()

    return _Comm(sums, [jax.ShapeDtypeStruct(b.shape, b.dtype) for b in sums],
                 [pltpu.SemaphoreType.DMA((n, 3)), pltpu.SemaphoreType.DMA((n, 3)), pltpu.SemaphoreType.DMA((n,))],
                 [start, finish])


def _mm_nt(a, bt, *, tm, tn, out_dtype, name, row_off=0, rows=None):
    rows = a.shape[0] if rows is None else rows
    n, k = bt.shape

    def body(a_ref, b_ref, o_ref):
        o_ref[...] = _dot_nt(a_ref[...], b_ref[...]).astype(out_dtype)

    return pl.pallas_call(
        body, name=name, grid=(rows // tm, n // tn),
        in_specs=[pl.BlockSpec((tm, k), lambda i, j: (i + row_off, 0)),
                  pl.BlockSpec((tn, k), lambda i, j: (j, 0))],
        out_specs=pl.BlockSpec((tm, tn), lambda i, j: (i, j)),
        out_shape=jax.ShapeDtypeStruct((rows, n), out_dtype),
        compiler_params=_cp(("parallel", "parallel")),
    )(a, bt)


def _mm_tn(a, b, *, tk, nk, tm, tn, out_dtype, name, a_off=0, b_off=0):
    m, n = a.shape[1], b.shape[1]

    def body(a_ref, b_ref, o_ref, acc):
        kk = pl.program_id(2)

        @pl.when(kk == 0)
        def _():
            acc[...] = jnp.zeros_like(acc)

        acc[...] += _dot_tn(a_ref[...], b_ref[...])

        @pl.when(kk == nk - 1)
        def _():
            o_ref[...] = acc[...].astype(out_dtype)

    return pl.pallas_call(
        body, name=name, grid=(m // tm, n // tn, nk),
        in_specs=[pl.BlockSpec((tk, tm), lambda i, j, kk: (kk + a_off, i)),
                  pl.BlockSpec((tk, tn), lambda i, j, kk: (kk + b_off, j))],
        out_specs=pl.BlockSpec((tm, tn), lambda i, j, kk: (i, j)),
        out_shape=jax.ShapeDtypeStruct((m, n), out_dtype),
        scratch_shapes=[pltpu.VMEM((tm, tn), F32)],
        compiler_params=_cp(("parallel", "parallel", "arbitrary")),
    )(a, b)


def _mm_tn_blocked(a, b, name, carried=None):
    nb, _, w = a.shape
    n = b.shape[1]

    def body(a_ref, b_ref, o_ref):
        o_ref[...] = _dot_tn(a_ref[...], b_ref[...]).astype(BF16)

    (out,), extra = _pcall(
        body, carried, name=name, grid=(nb,),
        in_specs=[pl.BlockSpec((None, S, w), lambda j: (j, 0, 0)), _full((S, n))],
        out_specs=[pl.BlockSpec((None, w, n), lambda j: (j, 0, 0))],
        out_shape=[jax.ShapeDtypeStruct((nb, w, n), BF16)],
        scratch_shapes=[], operands=[a, b])
    return out, extra


def _ada_rows(cc, w_ada, b_cols):
    def body(c_ref, w_ref, b_ref, o_ref):
        cv = c_ref[...]
        o_ref[...] = _dot_f32(cv * _sigmoid(cv), w_ref[...]) + b_ref[...]

    return pl.pallas_call(
        body, name="ada_rows",
        in_specs=[_VMEM_WHOLE] * 3, out_specs=_VMEM_WHOLE,
        out_shape=jax.ShapeDtypeStruct((16, ADA_BLK), F32),
        compiler_params=_cp(),
    )(cc, w_ada, b_cols)


def _ada_grads(cc, dm_cols, w_ada):
    def body(c_ref, dm_ref, w_ref, gw_ref, dsc_ref):
        cv = c_ref[...]
        sc = cv * _sigmoid(cv)
        dm = dm_ref[...]
        gw_ref[...] = _dot_f32(sc, dm, dot=_dot_tn)
        dsc_ref[...] = _dot_f32(dm, w_ref[...], dot=_dot_nt)

    return pl.pallas_call(
        body, name="ada_grads",
        in_specs=[_VMEM_WHOLE] * 3, out_specs=[_VMEM_WHOLE] * 2,
        out_shape=[jax.ShapeDtypeStruct((D, ADA_BLK), F32), jax.ShapeDtypeStruct((16, D), F32)],
        compiler_params=_cp(),
    )(cc, dm_cols, w_ada)


def _lat(i):
    return jnp.maximum(i - 1, 0)


def _rms_mod(xv, nw, sh, sc):
    rstd = lax.rsqrt(jnp.mean(xv * xv, axis=-1, keepdims=True) + EPS)
    return (xv * rstd * nw) * (1.0 + sc) + sh


def _rms_mod_bwd(xv, nw, sc, dh):
    rstd = lax.rsqrt(jnp.mean(xv * xv, axis=-1, keepdims=True) + EPS)
    xhat = xv * rstd
    dn = dh * (1.0 + sc)
    dxhat = dn * nw
    dx = rstd * (dxhat - xhat * jnp.mean(dxhat * xhat, axis=-1, keepdims=True))
    return (dx, jnp.sum(dh, axis=0, keepdims=True), jnp.sum(dh * (xhat * nw), axis=0, keepdims=True),
            jnp.sum(dn * xhat, axis=0, keepdims=True))


def _norm_mod_all(ctx, x, nw, sh, sc):
    def body(ctx_ref, x_ref, nw_ref, sh_ref, sc_ref, o_ref):
        i = pl.program_id(0)
        sel = jnp.minimum(i, 1)
        xv = jnp.where(i == 0, ctx_ref[...], x_ref[...])
        o_ref[...] = _rms_mod(xv, nw_ref[...], sh_ref[pl.ds(sel, 1), :], sc_ref[pl.ds(sel, 1), :]).astype(BF16)

    return pl.pallas_call(
        body, name="norm_mod", grid=(N_TILES,),
        in_specs=[_full((TM, D)), pl.BlockSpec((TM, D), lambda i: (_lat(i), 0)),
                  _full((1, D)), _full((2, D)), _full((2, D))],
        out_specs=pl.BlockSpec((TM, D), lambda i: (i, 0)),
        out_shape=jax.ShapeDtypeStruct((T, D), BF16),
        compiler_params=_cp(("parallel",)),
    )(ctx, x, nw, sh, sc)


def _chunk_masks(reverse):
    row = lax.broadcasted_iota(jnp.int32, (TM, TM), 0)
    col = lax.broadcasted_iota(jnp.int32, (TM, TM), 1)
    same = (row // CHUNK) == (col // CHUNK)
    tri = same & ((col >= row) if reverse else (col <= row))
    return same, tri


def _chunk_order(i, reverse):
    if not reverse:
        return i
    return jnp.where(i < N_CTX_CHUNKS, N_CTX_CHUNKS - 1 - i, N_CHUNKS + N_CTX_CHUNKS - 1 - i)


def _decay_terms(z, lb, same01, tri01):
    f = lb + (1.0 - lb) * _sigmoid(z)
    g = jnp.log(f)
    hi, mid, lo = _split3(g)
    g3 = jnp.concatenate([hi, mid, lo], axis=1)
    b3 = _dot(tri01, g3)
    t3 = _dot(same01, g3)
    b = b3[:, :HG_DIM] + b3[:, HG_DIM:2 * HG_DIM] + b3[:, 2 * HG_DIM:]
    bt = t3[:, :HG_DIM] + t3[:, HG_DIM:2 * HG_DIM] + t3[:, 2 * HG_DIM:]
    return f, 1.0 - f, b, bt


def _chunk_outer(a, b):
    n = TM // CHUNK
    return jnp.einsum('ncv,nck->nvk', a.reshape(n, CHUNK, HG_DIM), b.reshape(n, CHUNK, HG_DIM),
                      preferred_element_type=F32)


def _hgrn_fwd(p_a, lbl, carried=None):
    cpt = TM // CHUNK

    def body(p_ref, lbl_ref, o_ref, st_ref, qd_s, kd_s, u_s, v_s, ebt_s):
        masks = [_chunk_masks(d == 1) for d in (0, 1)]
        same01 = jnp.where(masks[0][0], 1.0, 0.0).astype(BF16)
        tri = [m[1] for m in masks]
        tri01 = [jnp.where(t, 1.0, 0.0).astype(BF16) for t in tri]
        lb = [_sigmoid(lbl_ref[d][0:1, :] - lbl_ref[d][1:2, :]) for d in (0, 1)]

        def prep(r, carry):
            r0 = pl.multiple_of(r * TM, TM)
            vb = p_ref[pl.ds(r0, TM), 2 * HG_DIM:3 * HG_DIM].astype(BF16)
            v_s[pl.ds(r0, TM), :] = vb
            for d in (0, 1):
                z = p_ref[pl.ds(r0, TM), d * HG_DIM:(d + 1) * HG_DIM]
                _, k, b, bt = _decay_terms(z, lb[d], same01, tri01[d])
                u_s[d, pl.ds(r * cpt, cpt)] = _chunk_outer(vb, (k * jnp.exp(bt - b)).astype(BF16))
                ebt_s[d, pl.ds(r0, TM), :] = jnp.exp(bt)

                @pl.when(r >= 1)
                def _():
                    rl = pl.multiple_of(r0 - L, TM)
                    qr = p_ref[pl.ds(r0, TM), 3 * HG_DIM:4 * HG_DIM]
                    q = qr * _sigmoid(qr) * HG_DIM ** -0.5
                    qd_s[d, pl.ds(rl, TM), :] = (q * jnp.exp(b)).astype(BF16)
                    kd_s[d, pl.ds(rl, TM), :] = (k * jnp.exp(-b)).astype(BF16)

            return carry

        lax.fori_loop(0, N_TILES, prep, 0)

        def scan(i, sts):
            new = []
            for d in (0, 1):
                nn = _chunk_order(i, d == 1)
                c0 = pl.multiple_of(nn * CHUNK, CHUNK)
                st_ref[d, nn] = sts[d].astype(BF16)
                new.append(sts[d] * ebt_s[d, pl.ds(c0, 1), :] + u_s[d, nn])
            return tuple(new)

        zero = jnp.zeros((HG_DIM, HG_DIM), F32)
        lax.fori_loop(0, N_CHUNKS, scan, (zero, zero))

        def outp(r, carry):
            r0 = pl.multiple_of(r * TM, TM)
            vb = v_s[pl.ds(r0 + L, TM), :]
            o = jnp.zeros((TM, HG_DIM), F32)
            for d in (0, 1):
                qd = qd_s[d, pl.ds(r0, TM), :]
                a = jnp.where(tri[d], _dot_nt(qd, kd_s[d, pl.ds(r0, TM), :]), 0.0)
                stb = st_ref[d, pl.ds(N_CTX_CHUNKS + r * cpt, cpt)]
                inter = jnp.einsum('nck,nvk->ncv', qd.reshape(cpt, CHUNK, HG_DIM), stb,
                                   preferred_element_type=F32)
                o = o + _dot(a.astype(BF16), vb) + inter.reshape(TM, HG_DIM)
            o_ref[pl.ds(r0, TM), :] = o
            return carry

        lax.fori_loop(0, N_LAT_TILES, outp, 0)

    return _pcall(
        body, carried, name="hgrn_fwd", grid=(HG_HEADS,),
        in_specs=[pl.BlockSpec((T, 4 * HG_DIM), lambda h: (0, h)),
                  pl.BlockSpec((2, 2, HG_DIM), lambda h: (0, 0, h))],
        out_specs=[pl.BlockSpec((S, HG_DIM), lambda h: (0, h)),
                   pl.BlockSpec((2, None, N_CHUNKS, HG_DIM, HG_DIM), lambda h: (0, h, 0, 0, 0))],
        out_shape=[jax.ShapeDtypeStruct((S, HGW), F32),
                   jax.ShapeDtypeStruct((2, HG_HEADS, N_CHUNKS, HG_DIM, HG_DIM), BF16)],
        scratch_shapes=[pltpu.VMEM((2, S, HG_DIM), BF16), pltpu.VMEM((2, S, HG_DIM), BF16),
                        pltpu.VMEM((2, N_CHUNKS, HG_DIM, HG_DIM), F32), pltpu.VMEM((T, HG_DIM), BF16),
                        pltpu.VMEM((2, T, HG_DIM), F32)],
        operands=[p_a, lbl])


def _hgrn_bwd(p_a, lbl, d_o, st, carried=None):
    cpt = TM // CHUNK

    def rows(r):
        return r * TM if isinstance(r, int) else pl.multiple_of(r * TM, TM)

    def body(p_ref, lbl_ref, do_ref, st_ref, dp_ref, dlb_ref, b_s, bt_s, dbt_s, qd_s, dst_s, w_s):
        masks = [_chunk_masks(d == 1) for d in (0, 1)]
        same01 = jnp.where(masks[0][0], 1.0, 0.0).astype(BF16)
        tri = [m[1] for m in masks]
        tri01 = [jnp.where(t, 1.0, 0.0).astype(BF16) for t in tri]
        later01 = [tri01[1], tri01[0]]
        lb = [_sigmoid(lbl_ref[d][0:1, :] - lbl_ref[d][1:2, :]) for d in (0, 1)]
        dbt_s[...] = jnp.zeros_like(dbt_s)

        def prep_tile(r, latent):
            r0 = rows(r)
            for d in (0, 1):
                z = p_ref[pl.ds(r0, TM), d * HG_DIM:(d + 1) * HG_DIM]
                _, _, b, bt = _decay_terms(z, lb[d], same01, tri01[d])
                b_s[d, pl.ds(r0, TM), :] = b
                bt_s[d, pl.ds(r0, TM), :] = bt
                if latent:
                    rl = pl.multiple_of(r0 - L, TM)
                    qr = p_ref[pl.ds(r0, TM), 3 * HG_DIM:4 * HG_DIM]
                    qd = (qr * _sigmoid(qr) * HG_DIM ** -0.5 * jnp.exp(b)).astype(BF16)
                    qd_s[d, pl.ds(rl, TM), :] = qd
                    w_s[d, pl.ds(r * cpt, cpt)] = _chunk_outer(
                        do_ref[pl.ds(rl, TM), :].astype(BF16), qd).astype(BF16)

        prep_tile(0, False)
        w_s[:, pl.ds(0, N_CTX_CHUNKS)] = jnp.zeros((2, N_CTX_CHUNKS, HG_DIM, HG_DIM), BF16)

        def prep(r, carry):
            prep_tile(r, True)
            return carry

        lax.fori_loop(1, N_TILES, prep, 0)

        def rscan(j, dsts):
            i = N_CHUNKS - 1 - j
            new = []
            for d in (0, 1):
                nn = _chunk_order(i, d == 1)
                c0 = pl.multiple_of(nn * CHUNK, CHUNK)
                dst_s[d, nn] = dsts[d].astype(BF16)
                after = st_ref[d, _chunk_order(jnp.minimum(i + 1, N_CHUNKS - 1), d == 1)].astype(F32)
                dbt_s[d, pl.ds(c0, 1), :] = jnp.sum(after * dsts[d], axis=0, keepdims=True)
                new.append(dsts[d] * jnp.exp(bt_s[d, pl.ds(c0, 1), :]) + w_s[d, nn].astype(F32))
            return tuple(new)

        zero = jnp.zeros((HG_DIM, HG_DIM), F32)
        lax.fori_loop(0, N_CHUNKS, rscan, (zero, zero))

        def grad_tile(r, latent):
            r0 = rows(r)
            vb = p_ref[pl.ds(r0, TM), 2 * HG_DIM:3 * HG_DIM].astype(BF16)
            dv = jnp.zeros((TM, HG_DIM), F32)
            dq = jnp.zeros((TM, HG_DIM), F32)
            dlbs = []
            if latent:
                rl = pl.multiple_of(r0 - L, TM)
                qr = p_ref[pl.ds(r0, TM), 3 * HG_DIM:4 * HG_DIM]
                sq = _sigmoid(qr)
                do = do_ref[pl.ds(rl, TM), :].astype(BF16)
                da_full = _dot_nt(do, vb)
            for d in (0, 1):
                z = p_ref[pl.ds(r0, TM), d * HG_DIM:(d + 1) * HG_DIM]
                sz = _sigmoid(z)
                f = lb[d] + (1.0 - lb[d]) * sz
                k = 1.0 - f
                b = b_s[d, pl.ds(r0, TM), :]
                e2 = jnp.exp(bt_s[d, pl.ds(r0, TM), :] - b)
                dstb = dst_s[d, pl.ds(r * cpt, cpt)]
                kd2 = k * e2
                dkd2 = jnp.einsum('ncv,nvk->nck', vb.reshape(cpt, CHUNK, HG_DIM), dstb,
                                  preferred_element_type=F32).reshape(TM, HG_DIM)
                dv = dv + jnp.einsum('nck,nvk->ncv', kd2.astype(BF16).reshape(cpt, CHUNK, HG_DIM), dstb,
                                     preferred_element_type=F32).reshape(TM, HG_DIM)
                dk = dkd2 * e2
                db = -(kd2 * dkd2)
                if latent:
                    eb = jnp.exp(b)
                    enb = jnp.exp(-b)
                    qdf = qr * sq * HG_DIM ** -0.5 * eb
                    kdf = k * enb
                    qd = qd_s[d, pl.ds(rl, TM), :]
                    kd = kdf.astype(BF16)
                    a = jnp.where(tri[d], _dot_nt(qd, kd), 0.0).astype(BF16)
                    da = jnp.where(tri[d], da_full, 0.0).astype(BF16)
                    stb = st_ref[d, pl.ds(r * cpt, cpt)]
                    dqd = _dot(da, kd) + jnp.einsum(
                        'ncv,nvk->nck', do.reshape(cpt, CHUNK, HG_DIM), stb,
                        preferred_element_type=F32).reshape(TM, HG_DIM)
                    dkd = _dot_tn(da, qd)
                    dv = dv + _dot_tn(a, do)
                    dk = dk + dkd * enb
                    db = db + qdf * dqd - kdf * dkd
                    dq = dq + dqd * eb
                dg = (_dot_exact_lhs01(later01[d], db)
                      + _dot_exact_lhs01(same01, dbt_s[d, pl.ds(r0, TM), :]))
                df = dg / f - dk
                dp_ref[pl.ds(r0, TM), d * HG_DIM:(d + 1) * HG_DIM] = (
                    df * (1.0 - lb[d]) * sz * (1.0 - sz)).astype(BF16)
                dlbs.append(jnp.sum(df * (1.0 - sz), axis=0, keepdims=True))
            dp_ref[pl.ds(r0, TM), 2 * HG_DIM:3 * HG_DIM] = dv.astype(BF16)
            if latent:
                dq = dq * (HG_DIM ** -0.5) * (sq * (1.0 + qr * (1.0 - sq)))
            dp_ref[pl.ds(r0, TM), 3 * HG_DIM:4 * HG_DIM] = dq.astype(BF16)
            return dlbs

        dlb_ctx = grad_tile(0, False)

        def grads(r, acc):
            t = grad_tile(r, True)
            return (acc[0] + t[0], acc[1] + t[1])

        dlb = lax.fori_loop(1, N_TILES, grads, (dlb_ctx[0], dlb_ctx[1]))
        dlb_ref[0:1, :] = dlb[0]
        dlb_ref[1:2, :] = dlb[1]

    return _pcall(
        body, carried, name="hgrn_bwd", grid=(HG_HEADS,),
        in_specs=[pl.BlockSpec((T, 4 * HG_DIM), lambda h: (0, h)),
                  pl.BlockSpec((2, 2, HG_DIM), lambda h: (0, 0, h)),
                  pl.BlockSpec((S, HG_DIM), lambda h: (0, h)),
                  pl.BlockSpec((2, None, N_CHUNKS, HG_DIM, HG_DIM), lambda h: (0, h, 0, 0, 0))],
        out_specs=[pl.BlockSpec((T, 4 * HG_DIM), lambda h: (0, h)),
                   pl.BlockSpec((2, HG_DIM), lambda h: (0, h))],
        out_shape=[jax.ShapeDtypeStruct((T, WA), BF16), jax.ShapeDtypeStruct((2, HGW), F32)],
        scratch_shapes=[pltpu.VMEM((2, T, HG_DIM), F32), pltpu.VMEM((2, T, HG_DIM), F32),
                        pltpu.VMEM((2, T, HG_DIM), F32), pltpu.VMEM((2, S, HG_DIM), BF16),
                        pltpu.VMEM((2, N_CHUNKS, HG_DIM, HG_DIM), BF16),
                        pltpu.VMEM((2, N_CHUNKS, HG_DIM, HG_DIM), BF16)],
        operands=[p_a, lbl, d_o, st])


def _rope_tables():
    t = np.arange(S)
    inv = ROPE_THETA ** (-np.arange(0, 32, 2, dtype=np.float64) / 32)
    lane = np.arange(64)
    pos = np.where(lane[None, :] < 32, (t // GRID_W)[:, None], (t % GRID_W)[:, None]).astype(np.float64)
    ang = pos * inv[(lane % 32) % 16][None, :]
    sign = np.where((lane % 32) < 16, -1.0, 1.0)[None, :]
    cos = np.tile(np.cos(ang), (1, 2)).astype(np.float32)
    sin = np.tile(np.sin(ang) * sign, (1, 2)).astype(np.float32)
    return jnp.asarray(cos), jnp.asarray(sin)


def _rope_partner(v):
    lane = lax.broadcasted_iota(jnp.int32, (1, 128), 1)
    first = (lane % 32) < 16
    slabs = []
    for j in range(v.shape[1] // 128):
        s = v[:, 128 * j:128 * (j + 1)]
        slabs.append(jnp.where(first, pltpu.roll(s, 112, 1), pltpu.roll(s, 16, 1)))
    return slabs[0] if len(slabs) == 1 else jnp.concatenate(slabs, axis=1)


def _group_ones(width, group):
    r = lax.broadcasted_iota(jnp.int32, (width, width), 0)
    c = lax.broadcasted_iota(jnp.int32, (width, width), 1)
    return jnp.where((r // group) == (c // group), 1.0, 0.0).astype(BF16)


def _group_mean(v, ones01, group):
    hi = v.astype(BF16)
    lo = (v - hi.astype(F32)).astype(BF16)
    return (_dot(hi, ones01) + _dot(lo, ones01)) * (1.0 / group)


def _rep_matrix():
    r = lax.broadcasted_iota(jnp.int32, (KVW, ATW), 0)
    c = lax.broadcasted_iota(jnp.int32, (KVW, ATW), 1)
    return jnp.where(r == HEAD_DIM * (c // 256) + c % HEAD_DIM, 1.0, 0.0).astype(BF16)


def _tile_lanes(v, reps):
    return jnp.concatenate([v] * reps, axis=1)


def _prep_fwd(p_b, o, cos, sin, hnw, qnw, knw):
    def body(p_ref, o_ref, cos_ref, sin_ref, hnw_ref, qnw_ref, knw_ref, y_ref, q_ref, k_ref, v_ref):
        i = pl.program_id(0)
        rep = _rep_matrix()
        ones_k = _group_ones(KVW, HEAD_DIM)
        kr = p_ref[:, 1024:1152]
        krstd = lax.rsqrt(_group_mean(kr * kr, ones_k, HEAD_DIM) + EPS)
        kn = kr * krstd * knw_ref[...]
        v_ref[...] = _dot(p_ref[:, 1152:1280].astype(BF16), rep).astype(BF16)

        @pl.when(i == 0)
        def _():
            k_ref[...] = _dot(kn.astype(BF16), rep).astype(BF16)

        @pl.when(i > 0)
        def _():
            cs, sn = cos_ref[...], sin_ref[...]
            kro = kn * cs + _rope_partner(kn) * sn
            k_ref[...] = _dot(kro.astype(BF16), rep).astype(BF16)
            qr = p_ref[:, 512:1024]
            qrstd = lax.rsqrt(_group_mean(qr * qr, _group_ones(ATW, HEAD_DIM), HEAD_DIM) + EPS)
            qn = qr * qrstd * qnw_ref[...]
            qro = qn * _tile_lanes(cs, 4) + _rope_partner(qn) * _tile_lanes(sn, 4)
            q_ref[...] = (qro * HEAD_DIM ** -0.5).astype(BF16)
            ys = []
            for h in range(HG_HEADS):
                oh = o_ref[:, HG_DIM * h:HG_DIM * (h + 1)]
                gh = p_ref[:, HG_DIM * h:HG_DIM * (h + 1)]
                rstd = lax.rsqrt(jnp.mean(oh * oh, axis=-1, keepdims=True) + EPS)
                ys.append(oh * rstd * hnw_ref[...] * (gh * _sigmoid(gh)))
            y_ref[...] = jnp.concatenate(ys, axis=1).astype(BF16)

    return pl.pallas_call(
        body, name="prep_fwd", grid=(N_TILES,),
        in_specs=[pl.BlockSpec((TM, WB), lambda i: (i, 0)),
                  pl.BlockSpec((TM, HGW), lambda i: (_lat(i), 0)),
                  pl.BlockSpec((TM, 128), lambda i: (_lat(i), 0)),
                  pl.BlockSpec((TM, 128), lambda i: (_lat(i), 0)),
                  _full((1, HG_DIM)), _full((1, ATW)), _full((1, KVW))],
        out_specs=[pl.BlockSpec((TM, HGW), lambda i: (_lat(i), 0)),
                   pl.BlockSpec((TM, ATW), lambda i: (_lat(i), 0)),
                   pl.BlockSpec((TM, ATW), lambda i: (i, 0)),
                   pl.BlockSpec((TM, ATW), lambda i: (i, 0))],
        out_shape=[jax.ShapeDtypeStruct((S, HGW), BF16), jax.ShapeDtypeStruct((S, ATW), BF16),
                   jax.ShapeDtypeStruct((T, ATW), BF16), jax.ShapeDtypeStruct((T, ATW), BF16)],
        compiler_params=_cp(("arbitrary",)),
    )(p_b, o, cos, sin, hnw, qnw, knw)


def _prep_bwd(p_b, o, cos, sin, hnw, qnw, knw, dy_hg, dq, dk_rep, dv_rep):
    def body(p_ref, o_ref, cos_ref, sin_ref, hnw_ref, qnw_ref, knw_ref, dy_ref, dq_ref, dk_ref, dv_ref,
             dp_ref, do_ref, acc_ref):
        i = pl.program_id(0)

        @pl.when(i == 0)
        def _():
            acc_ref[...] = jnp.zeros_like(acc_ref)

        rep = _rep_matrix()
        ones_k = _group_ones(KVW, HEAD_DIM)

        def fold(v):
            hi = v.astype(BF16)
            lo = (v - hi.astype(F32)).astype(BF16)
            return _dot_nt(hi, rep) + _dot_nt(lo, rep)

        kr = p_ref[:, 1024:1152]
        krstd = lax.rsqrt(_group_mean(kr * kr, ones_k, HEAD_DIM) + EPS)
        khat = kr * krstd
        kw = knw_ref[...]
        dkro = fold(dk_ref[...])
        dv = fold(dv_ref[...])

        def k_back(dkn):
            dkhat = dkn * kw
            dkr = krstd * (dkhat - khat * _group_mean(dkhat * khat, ones_k, HEAD_DIM))
            acc_ref[2:3, 0:KVW] += jnp.sum(dkn * khat, axis=0, keepdims=True)
            dp_ref[:, 1024:1152] = dkr.astype(BF16)
            dp_ref[:, 1152:1280] = dv.astype(BF16)

        @pl.when(i == 0)
        def _():
            k_back(dkro)
            dp_ref[:, 0:1024] = jnp.zeros((TM, 1024), BF16)

        @pl.when(i > 0)
        def _():
            cs, sn = cos_ref[...], sin_ref[...]
            k_back(dkro * cs + _rope_partner(dkro * sn))
            ones_q = _group_ones(ATW, HEAD_DIM)
            qr = p_ref[:, 512:1024]
            qrstd = lax.rsqrt(_group_mean(qr * qr, ones_q, HEAD_DIM) + EPS)
            qhat = qr * qrstd
            dqro = dq_ref[...] * HEAD_DIM ** -0.5
            dqn = dqro * _tile_lanes(cs, 4) + _rope_partner(dqro * _tile_lanes(sn, 4))
            dqhat = dqn * qnw_ref[...]
            dqr = qrstd * (dqhat - qhat * _group_mean(dqhat * qhat, ones_q, HEAD_DIM))
            acc_ref[1:2, :] += jnp.sum(dqn * qhat, axis=0, keepdims=True)
            dp_ref[:, 512:1024] = dqr.astype(BF16)
            dws = jnp.zeros((1, HG_DIM), F32)
            for h in range(HG_HEADS):
                sl = slice(HG_DIM * h, HG_DIM * (h + 1))
                oh, gh, dy = o_ref[:, sl], p_ref[:, sl], dy_ref[:, sl]
                rstd = lax.rsqrt(jnp.mean(oh * oh, axis=-1, keepdims=True) + EPS)
                ohat = oh * rstd
                sg = _sigmoid(gh)
                dp_ref[:, sl] = (dy * (ohat * hnw_ref[...]) * (sg * (1.0 + gh * (1.0 - sg)))).astype(BF16)
                dn = dy * (gh * sg)
                dws = dws + jnp.sum(dn * ohat, axis=0, keepdims=True)
                dohat = dn * hnw_ref[...]
                do_ref[:, sl] = rstd * (dohat - ohat * jnp.mean(dohat * ohat, axis=-1, keepdims=True))
            acc_ref[0:1, 0:HG_DIM] += dws

    return pl.pallas_call(
        body, name="prep_bwd", grid=(N_TILES,),
        in_specs=[pl.BlockSpec((TM, WB), lambda i: (i, 0)),
                  pl.BlockSpec((TM, HGW), lambda i: (_lat(i), 0)),
                  pl.BlockSpec((TM, 128), lambda i: (_lat(i), 0)),
                  pl.BlockSpec((TM, 128), lambda i: (_lat(i), 0)),
                  _full((1, HG_DIM)), _full((1, ATW)), _full((1, KVW)),
                  pl.BlockSpec((TM, HGW), lambda i: (_lat(i), 0)),
                  pl.BlockSpec((TM, ATW), lambda i: (_lat(i), 0)),
                  pl.BlockSpec((TM, ATW), lambda i: (i, 0)),
                  pl.BlockSpec((TM, ATW), lambda i: (i, 0))],
        out_specs=[pl.BlockSpec((TM, WB), lambda i: (i, 0)),
                   pl.BlockSpec((TM, HGW), lambda i: (_lat(i), 0)),
                   _full((8, ATW))],
        out_shape=[jax.ShapeDtypeStruct((T, WB), BF16), jax.ShapeDtypeStruct((S, HGW), F32),
                   jax.ShapeDtypeStruct((8, ATW), F32)],
        compiler_params=_cp(("arbitrary",)),
    )(p_b, o, cos, sin, hnw, qnw, knw, dy_hg, dq, dk_rep, dv_rep)


NEG = -1e30
_CTX_BLOCKS = L // BLOCK


def _attn_window_specs():
    prev = pl.BlockSpec((BLOCK, ATW), lambda i: (jnp.maximum(i - 1, 0) + _CTX_BLOCKS, 0))
    own = pl.BlockSpec((BLOCK, ATW), lambda i: (i + _CTX_BLOCKS, 0))
    nxt = pl.BlockSpec((BLOCK, ATW), lambda i: (jnp.minimum(i + 1, N_BLOCKS - 1) + _CTX_BLOCKS, 0))
    return [prev, own, nxt, _full((L, ATW))]


def _attn_valid(i):
    qi = lax.broadcasted_iota(jnp.int32, (4 * BLOCK, 3 * BLOCK), 0) % BLOCK
    kj = lax.broadcasted_iota(jnp.int32, (4 * BLOCK, 3 * BLOCK), 1)
    return ((jnp.abs(kj - BLOCK - qi) <= BLOCK) & ((kj >= BLOCK) | (i > 0))
            & ((kj < 2 * BLOCK) | (i < N_BLOCKS - 1)))


def _stack_heads(qg):
    lane = lax.broadcasted_iota(jnp.int32, (1, 256), 1) // HEAD_DIM
    return jnp.concatenate([jnp.where(lane == g, qg, jnp.zeros_like(qg)) for g in range(4)], axis=0)


def _unstack_heads(v4):
    lane = lax.broadcasted_iota(jnp.int32, (1, 256), 1) // HEAD_DIM
    out = jnp.where(lane == 0, v4[0:BLOCK], 0.0)
    for g in range(1, 4):
        out = out + jnp.where(lane == g, v4[g * BLOCK:(g + 1) * BLOCK], 0.0)
    return out


def _sink_rows(sink_ref, hk):
    return jnp.concatenate(
        [jnp.broadcast_to(sink_ref[0:1, 4 * hk + g:4 * hk + g + 1], (BLOCK, 1)) for g in range(4)], axis=0)


def _attn_fwd(q, k_rep, v_rep, sinks, carried=None):
    def body(q_ref, kp, ko, kn, kc, vp, vo, vn, vc, sink_ref, y_ref, lse_ref):
        i = pl.program_id(0)
        valid = _attn_valid(i)
        lane8 = lax.broadcasted_iota(jnp.int32, (1, ATT_HEADS), 1)
        lse_out = jnp.zeros((BLOCK, ATT_HEADS), F32)
        for hk in range(KV_HEADS):
            sl = slice(256 * hk, 256 * (hk + 1))
            q4 = _stack_heads(q_ref[:, sl])
            kl = jnp.concatenate([kp[:, sl], ko[:, sl], kn[:, sl]], axis=0)
            vl = jnp.concatenate([vp[:, sl], vo[:, sl], vn[:, sl]], axis=0)
            s_loc = jnp.where(valid, _dot_nt(q4, kl), NEG)
            s_ctx = _dot_nt(q4, kc[:, sl])
            sink = _sink_rows(sink_ref, hk)
            m = jnp.maximum(jnp.maximum(jnp.max(s_loc, axis=1, keepdims=True),
                                        jnp.max(s_ctx, axis=1, keepdims=True)), sink)
            p_loc = jnp.exp(s_loc - m)
            p_ctx = jnp.exp(s_ctx - m)
            den = (jnp.sum(p_loc, axis=1, keepdims=True) + jnp.sum(p_ctx, axis=1, keepdims=True)
                   + jnp.exp(sink - m))
            o4 = (_dot(p_loc.astype(BF16), vl) + _dot(p_ctx.astype(BF16), vc[:, sl])) / den
            y_ref[:, sl] = _unstack_heads(o4).astype(BF16)
            lse4 = m + jnp.log(den)
            for g in range(4):
                lse_out = lse_out + jnp.where(lane8 == 4 * hk + g, lse4[g * BLOCK:(g + 1) * BLOCK], 0.0)
        lse_ref[...] = lse_out

    return _pcall(
        body, carried, name="attn_fwd", grid=(N_BLOCKS,),
        in_specs=[pl.BlockSpec((BLOCK, ATW), lambda i: (i, 0))] + _attn_window_specs()
        + _attn_window_specs() + [_full((1, ATT_HEADS))],
        out_specs=[pl.BlockSpec((BLOCK, ATW), lambda i: (i, 0)),
                   pl.BlockSpec((BLOCK, ATT_HEADS), lambda i: (i, 0))],
        out_shape=[jax.ShapeDtypeStruct((S, ATW), BF16), jax.ShapeDtypeStruct((S, ATT_HEADS), F32)],
        scratch_shapes=[],
        operands=[q, k_rep, k_rep, k_rep, k_rep, v_rep, v_rep, v_rep, v_rep, sinks])


def _attn_bwd(q, k_rep, v_rep, sinks, y_at, lse, dy, carried=None):
    def body(q_ref, kp, ko, kn, kc, vp, vo, vn, vc, sink_ref, y_ref, lse_ref, dy_ref,
             dq_ref, dk_ref, dv_ref, dsink_ref, dk_acc, dv_acc):
        i = pl.program_id(0)

        @pl.when(i == 0)
        def _():
            dk_acc[...] = jnp.zeros_like(dk_acc)
            dv_acc[...] = jnp.zeros_like(dv_acc)
            dk_ref[pl.ds(0, L), :] = jnp.zeros((L, ATW), F32)
            dv_ref[pl.ds(0, L), :] = jnp.zeros((L, ATW), F32)
            dsink_ref[...] = jnp.zeros_like(dsink_ref)

        valid = _attn_valid(i)
        lane8 = lax.broadcasted_iota(jnp.int32, (1, ATT_HEADS), 1)
        w0 = pl.multiple_of(i * BLOCK, BLOCK)
        dsink = jnp.zeros((1, ATT_HEADS), F32)
        for hk in range(KV_HEADS):
            sl = slice(256 * hk, 256 * (hk + 1))
            q4 = _stack_heads(q_ref[:, sl])
            do4f = _stack_heads(dy_ref[:, sl])
            o4 = _stack_heads(y_ref[:, sl]).astype(F32)
            do4 = do4f.astype(BF16)
            kl = jnp.concatenate([kp[:, sl], ko[:, sl], kn[:, sl]], axis=0)
            vl = jnp.concatenate([vp[:, sl], vo[:, sl], vn[:, sl]], axis=0)
            lse4 = jnp.concatenate(
                [jnp.sum(jnp.where(lane8 == 4 * hk + g, lse_ref[...], 0.0), axis=1, keepdims=True)
                 for g in range(4)], axis=0)
            p_loc = jnp.where(valid, jnp.exp(_dot_nt(q4, kl) - lse4), 0.0)
            p_ctx = jnp.exp(_dot_nt(q4, kc[:, sl]) - lse4)
            delta = jnp.sum(do4f * o4, axis=1, keepdims=True)
            ds_loc = (p_loc * (_dot_nt(do4, vl) - delta)).astype(BF16)
            ds_ctx = (p_ctx * (_dot_nt(do4, vc[:, sl]) - delta)).astype(BF16)
            dq_ref[:, sl] = _unstack_heads(_dot(ds_loc, kl) + _dot(ds_ctx, kc[:, sl]))
            dk_acc[pl.ds(w0, 3 * BLOCK), sl] += _dot_tn(ds_loc, q4)
            dv_acc[pl.ds(w0, 3 * BLOCK), sl] += _dot_tn(p_loc.astype(BF16), do4)
            dk_ref[pl.ds(0, L), sl] += _dot_tn(ds_ctx, q4)
            dv_ref[pl.ds(0, L), sl] += _dot_tn(p_ctx.astype(BF16), do4)
            p_sink = jnp.exp(_sink_rows(sink_ref, hk) - lse4)
            for g in range(4):
                rows = slice(g * BLOCK, (g + 1) * BLOCK)
                dsink = dsink + jnp.where(lane8 == 4 * hk + g,
                                          -jnp.sum(p_sink[rows] * delta[rows], axis=0, keepdims=True), 0.0)
        dsink_ref[...] += dsink

        @pl.when(i == N_BLOCKS - 1)
        def _():
            dk_ref[pl.ds(L, S), :] = dk_acc[pl.ds(BLOCK, S), :]
            dv_ref[pl.ds(L, S), :] = dv_acc[pl.ds(BLOCK, S), :]

    row_q = pl.BlockSpec((BLOCK, ATW), lambda i: (i, 0))
    return _pcall(
        body, carried, name="attn_bwd", grid=(N_BLOCKS,),
        in_specs=[row_q] + _attn_window_specs() + _attn_window_specs()
        + [_full((1, ATT_HEADS)), row_q, pl.BlockSpec((BLOCK, ATT_HEADS), lambda i: (i, 0)), row_q],
        out_specs=[row_q, _full((T, ATW)), _full((T, ATW)), _full((1, ATT_HEADS))],
        out_shape=[jax.ShapeDtypeStruct((S, ATW), F32), jax.ShapeDtypeStruct((T, ATW), F32),
                   jax.ShapeDtypeStruct((T, ATW), F32), jax.ShapeDtypeStruct((1, ATT_HEADS), F32)],
        scratch_shapes=[pltpu.VMEM((S + 2 * BLOCK, ATW), F32), pltpu.VMEM((S + 2 * BLOCK, ATW), F32)],
        operands=[q, k_rep, k_rep, k_rep, k_rep, v_rep, v_rep, v_rep, v_rep, sinks, y_at, lse, dy])


def _merge_fwd(y_hg, y_at, p_c, x, w_bh, w_ba, w_out, g1, nfw, sh2, sc2):
    def body(yh_ref, ya_ref, g_ref, x_ref, wbh_ref, wba_ref, wo_ref, g1_ref, nfw_ref, sh_ref, sc_ref,
             a_ref, b_ref, mx_ref, r_ref, x1_ref, h2_ref):
        a = _dot_nt(yh_ref[...], wbh_ref[...])
        b = _dot_nt(ya_ref[...], wba_ref[...])
        mixed = (_sigmoid(g_ref[:, :D]) * a + _sigmoid(g_ref[:, D:]) * b).astype(BF16)
        r = _dot(mixed, wo_ref[...])
        x1 = x_ref[...] + g1_ref[...] * r
        a_ref[...] = a
        b_ref[...] = b
        mx_ref[...] = mixed
        r_ref[...] = r
        x1_ref[...] = x1
        h2_ref[...] = _rms_mod(x1, nfw_ref[...], sh_ref[...], sc_ref[...]).astype(BF16)

    row = lambda w: pl.BlockSpec((TM, w), lambda i: (i, 0))
    vec = _full((1, D))
    return pl.pallas_call(
        body, name="merge_fwd", grid=(N_LAT_TILES,),
        in_specs=[row(HGW), row(ATW), row(WC), row(D), _VMEM_WHOLE, _VMEM_WHOLE, _VMEM_WHOLE,
                  vec, vec, vec, vec],
        out_specs=[row(D)] * 6,
        out_shape=[jax.ShapeDtypeStruct((S, D), dt) for dt in (F32, F32, BF16, F32, F32, BF16)],
        compiler_params=_cp(("parallel",)),
    )(y_hg, y_at, p_c, x, w_bh, w_ba, w_out, g1, nfw, sh2, sc2)


def _merge_bwd(dx1, r, a, b, p_c, w_bh, w_ba, w_out, g1, carried=None):
    def body(dx_ref, r_ref, a_ref, b_ref, g_ref, wbh_ref, wba_ref, wo_ref, g1_ref,
             dr_ref, da_ref, db_ref, dg_ref, dyh_ref, dya_ref, acc_ref):
        @pl.when(pl.program_id(0) == 0)
        def _():
            acc_ref[...] = jnp.zeros_like(acc_ref)

        dx1v = dx_ref[...]
        acc_ref[0:1, :] += jnp.sum(dx1v * r_ref[...], axis=0, keepdims=True)
        dr = (g1_ref[...] * dx1v).astype(BF16)
        dr_ref[...] = dr
        dmix = _dot_nt(dr, wo_ref[...])
        sh, sa = _sigmoid(g_ref[:, :D]), _sigmoid(g_ref[:, D:])
        da = (dmix * sh).astype(BF16)
        db = (dmix * sa).astype(BF16)
        da_ref[...] = da
        db_ref[...] = db
        dg_ref[:, :D] = (dmix * a_ref[...] * sh * (1.0 - sh)).astype(BF16)
        dg_ref[:, D:] = (dmix * b_ref[...] * sa * (1.0 - sa)).astype(BF16)
        dyh_ref[...] = _dot(da, wbh_ref[...])
        dya_ref[...] = _dot(db, wba_ref[...])

    row = lambda w: pl.BlockSpec((TM, w), lambda i: (i, 0))
    return _pcall(
        body, carried, name="merge_bwd", grid=(N_LAT_TILES,),
        in_specs=[row(D), row(D), row(D), row(D), row(WC), _VMEM_WHOLE, _VMEM_WHOLE, _VMEM_WHOLE,
                  _full((1, D))],
        out_specs=[row(D), row(D), row(D), row(WC), row(HGW), row(ATW), _full((8, D))],
        out_shape=[jax.ShapeDtypeStruct((S, D), BF16), jax.ShapeDtypeStruct((S, D), BF16),
                   jax.ShapeDtypeStruct((S, D), BF16), jax.ShapeDtypeStruct((S, WC), BF16),
                   jax.ShapeDtypeStruct((S, HGW), F32), jax.ShapeDtypeStruct((S, ATW), F32),
                   jax.ShapeDtypeStruct((8, D), F32)],
        scratch_shapes=[], operands=[dx1, r, a, b, p_c, w_bh, w_ba, w_out, g1])


def _ffn_fused(x1, h2, tgt, w_gate, w_up, w_down, g2, nfw, sc2):
    def body(x1_ref, h2_ref, t_ref, wg_ref, wu_ref, wd_ref, g2_ref, nfw_ref, sc_ref,
             act_ref, dgt_ref, dup_ref, df_ref, dx_ref, acc_ref, gs, us):
        @pl.when(pl.program_id(0) == 0)
        def _():
            acc_ref[...] = jnp.zeros_like(acc_ref)

        h2 = h2_ref[...]
        f = jnp.zeros((TM, D), F32)
        for j in range(N_FF_TILES):
            g = _dot_nt(h2, wg_ref[j])
            u = _dot_nt(h2, wu_ref[j])
            gs[j] = g
            us[j] = u
            act = (g * _sigmoid(g) * u).astype(BF16)
            act_ref[j] = act
            f = f + _dot(act, wd_ref[j])
        x1v = x1_ref[...]
        g2 = g2_ref[...]
        diff = x1v + g2 * f - t_ref[...]
        dy = diff * (1.0 / D)
        df = (g2 * dy).astype(BF16)
        df_ref[...] = df
        dh2 = jnp.zeros((TM, D), F32)
        for j in range(N_FF_TILES):
            g, u = gs[j], us[j]
            sg = _sigmoid(g)
            dact = _dot_nt(df, wd_ref[j])
            dgate = (dact * u * (sg * (1.0 + g * (1.0 - sg)))).astype(BF16)
            dup = (dact * (g * sg)).astype(BF16)
            dgt_ref[j] = dgate
            dup_ref[j] = dup
            dh2 = dh2 + _dot(dgate, wg_ref[j]) + _dot(dup, wu_ref[j])
        dx, dsh, dsc, dnw = _rms_mod_bwd(x1v, nfw_ref[...], sc_ref[...], dh2)
        dx_ref[...] = dy + dx
        acc_ref[0:1, :] += dsh
        acc_ref[1:2, :] += dsc
        acc_ref[2:3, :] += dnw
        acc_ref[3:4, :] += jnp.sum(dy * f, axis=0, keepdims=True)
        acc_ref[4:5, :] += 0.5 * jnp.sum(jnp.sum(diff * diff, axis=1, keepdims=True), axis=0,
                                         keepdims=True) * (1.0 / D)

    row = lambda dt_w: pl.BlockSpec((TM, dt_w), lambda i: (i, 0))
    blk = pl.BlockSpec((N_FF_TILES, TM, FF_TILE), lambda i: (0, i, 0))
    vec = _full((1, D))
    return pl.pallas_call(
        body, name="ffn_fused", grid=(N_LAT_TILES,),
        in_specs=[row(D), row(D), row(D), _VMEM_WHOLE, _VMEM_WHOLE, _VMEM_WHOLE, vec, vec, vec],
        out_specs=[blk, blk, blk, row(D), row(D), _full((8, D))],
        out_shape=[jax.ShapeDtypeStruct((N_FF_TILES, S, FF_TILE), BF16)] * 3
        + [jax.ShapeDtypeStruct((S, D), BF16), jax.ShapeDtypeStruct((S, D), F32),
           jax.ShapeDtypeStruct((8, D), F32)],
        scratch_shapes=[pltpu.VMEM((N_FF_TILES, TM, FF_TILE), F32), pltpu.VMEM((N_FF_TILES, TM, FF_TILE), F32)],
        compiler_params=_cp(("arbitrary",)),
    )(x1, h2, tgt, w_gate, w_up, w_down, g2, nfw, sc2)


def _input_bwd(dp_a, dp_b, dp_c, w_a, w_b, w_c, ctx, x, dx1, nw, sh, sc):
    def body(da_ref, db_ref, dc_ref, wa_ref, wb_ref, wc_ref, ctx_ref, x_ref, dx1_ref, nw_ref, sh_ref,
             sc_ref, gx_ref, acc_ref):
        i = pl.program_id(0)

        @pl.when(i == 0)
        def _():
            acc_ref[...] = jnp.zeros_like(acc_ref)

        dh = _dot(da_ref[...], wa_ref[...]) + _dot(db_ref[...], wb_ref[...])

        @pl.when(i == 0)
        def _():
            _, dsh, dsc, dnw = _rms_mod_bwd(ctx_ref[...], nw_ref[...], sc_ref[0:1, :], dh)
            acc_ref[3:4, :] += dsh
            acc_ref[4:5, :] += dsc
            acc_ref[2:3, :] += dnw

        @pl.when(i > 0)
        def _():
            dhl = dh + _dot(dc_ref[...], wc_ref[...])
            dx, dsh, dsc, dnw = _rms_mod_bwd(x_ref[...], nw_ref[...], sc_ref[1:2, :], dhl)
            gx_ref[...] = dx1_ref[...] + dx
            acc_ref[0:1, :] += dsh
            acc_ref[1:2, :] += dsc
            acc_ref[2:3, :] += dnw

    lat = lambda w: pl.BlockSpec((TM, w), lambda i: (_lat(i), 0))
    return pl.pallas_call(
        body, name="input_bwd", grid=(N_TILES,),
        in_specs=[pl.BlockSpec((TM, WA), lambda i: (i, 0)), pl.BlockSpec((TM, WB), lambda i: (i, 0)),
                  lat(WC), _VMEM_WHOLE, _VMEM_WHOLE, _VMEM_WHOLE, _full((TM, D)), lat(D), lat(D),
                  _full((1, D)), _full((2, D)), _full((2, D))],
        out_specs=[lat(D), _full((8, D))],
        out_shape=[jax.ShapeDtypeStruct((S, D), F32), jax.ShapeDtypeStruct((8, D), F32)],
        compiler_params=_cp(("arbitrary",)),
    )(dp_a, dp_b, dp_c, w_a, w_b, w_c, ctx, x, dx1, nw, sh, sc)


_C1 = 1.0 - ADAM_B1 ** ADAM_STEP
_C2 = 1.0 - ADAM_B2 ** ADAM_STEP


def _adamw_math(w, g, m, v):
    m = ADAM_B1 * m + (1.0 - ADAM_B1) * g
    v = ADAM_B2 * v + (1.0 - ADAM_B2) * (g * g)
    m_hat = m / _C1
    v_hat = v / _C2
    delta = -ADAM_LR * (m_hat / (jnp.sqrt(v_hat) + ADAM_EPS) + ADAM_WD * w)
    return delta, m, v


def _adamw_sharded(terms, w, m, v, name, tr):
    rows, cols = w.shape

    def body(t_ref, w_ref, m_ref, v_ref, g_ref, d_ref, nm_ref, nv_ref):
        g = t_ref[0].astype(F32)
        for s in range(1, N_CHIPS):
            g = g + t_ref[s].astype(F32)
        g_ref[...] = g
        d_ref[...], nm_ref[...], nv_ref[...] = _adamw_math(w_ref[...], g, m_ref[...], v_ref[...])

    blk = pl.BlockSpec((tr, cols), lambda i: (i, 0))
    return pl.pallas_call(
        body, name=name, grid=(rows // tr,),
        in_specs=[pl.BlockSpec((N_CHIPS, tr, cols), lambda i: (0, i, 0)), blk, blk, blk],
        out_specs=[blk] * 4,
        out_shape=[jax.ShapeDtypeStruct((rows, cols), F32)] * 4,
        compiler_params=_cp(("parallel",)),
    )(terms, w, m, v)


def _adamw_plain(g, w, m, v, name):
    def body(g_ref, w_ref, m_ref, v_ref, d_ref, nm_ref, nv_ref):
        d_ref[...], nm_ref[...], nv_ref[...] = _adamw_math(w_ref[...], g_ref[...], m_ref[...], v_ref[...])

    return pl.pallas_call(
        body, name=name, in_specs=[_VMEM_WHOLE] * 4, out_specs=[_VMEM_WHOLE] * 3,
        out_shape=[jax.ShapeDtypeStruct(w.shape, F32)] * 3,
        compiler_params=_cp(),
    )(g, w, m, v)


SMALL_ROWS = 16
R_DMOD, R_DCTX, R_NMIX, R_NFFN, R_MISC, R_DLB, R_BADA01 = 0, 6, 8, 9, 10, 11, 13
M_HNW, M_QNW, M_KNW, M_SINK, M_LOSS = 0, 128, 256, 384, 512


def _pack_small(acc_in, acc_mg, acc_ffn, acc_prep, dsink, dlb):
    def body(in_ref, mg_ref, ff_ref, pp_ref, ds_ref, dlb_ref, o_ref):
        o_ref[...] = jnp.zeros_like(o_ref)
        o_ref[0:2, :] = in_ref[0:2, :]
        o_ref[2:3, :] = mg_ref[0:1, :]
        o_ref[3:5, :] = ff_ref[0:2, :]
        o_ref[5:6, :] = ff_ref[3:4, :]
        o_ref[6:8, :] = in_ref[3:5, :]
        o_ref[8:9, :] = in_ref[2:3, :]
        o_ref[9:10, :] = ff_ref[2:3, :]
        o_ref[10:11, M_HNW:M_HNW + HG_DIM] = pp_ref[0:1, 0:HG_DIM]
        r = lax.broadcasted_iota(jnp.int32, (ATW, 128), 0)
        c = lax.broadcasted_iota(jnp.int32, (ATW, 128), 1)
        fold = jnp.where((r % HEAD_DIM == c) & (c < HEAD_DIM), 1.0, 0.0).astype(BF16)
        qk = jnp.concatenate([pp_ref[1:2, :], pp_ref[2:3, :], jnp.zeros((6, ATW), F32)], axis=0)
        folded = _dot_exact_rhs01(qk, fold)
        o_ref[10:11, M_QNW:M_QNW + 128] = folded[0:1, :]
        o_ref[10:11, M_KNW:M_KNW + 128] = folded[1:2, :]
        o_ref[10:11, M_SINK:M_SINK + ATT_HEADS] = ds_ref[...]
        o_ref[10:11, M_LOSS:M_LOSS + 128] = ff_ref[4:5, 0:128]
        o_ref[11:13, 0:HGW] = dlb_ref[...]

    return pl.pallas_call(
        body, name="pack_small", in_specs=[_VMEM_WHOLE] * 6, out_specs=_VMEM_WHOLE,
        out_shape=jax.ShapeDtypeStruct((SMALL_ROWS, D), F32), compiler_params=_cp(),
    )(acc_in, acc_mg, acc_ffn, acc_prep, dsink, dlb)


def _sum_small(gathered):
    def body(g_ref, o_ref):
        tot = g_ref[0]
        for s in range(1, N_DEV):
            tot = tot + g_ref[s]
        o_ref[...] = tot
        o_ref[R_BADA01:R_BADA01 + 2, :] = tot[0:2, :] + tot[R_DCTX:R_DCTX + 2, :]

    return pl.pallas_call(
        body, name="sum_small", in_specs=[_VMEM_WHOLE], out_specs=_VMEM_WHOLE,
        out_shape=jax.ShapeDtypeStruct((SMALL_ROWS, D), F32), compiler_params=_cp(),
    )(gathered)


_REP_NAMES = ("b_ada", "c_ctx", "norm_mix_w", "norm_ffn_w", "hgrn_norm_w", "q_norm_w", "k_norm_w", "attn_sinks")


def _adamw_replicated(tot, g_c_ctx, ws, ms, vs):
    n = len(_REP_NAMES)

    def body(*refs):
        tot_ref, gc_ref = refs[0], refs[1]
        w_refs, m_refs, v_refs = refs[2:2 + n], refs[2 + n:2 + 2 * n], refs[2 + 2 * n:2 + 3 * n]
        outs = refs[2 + 3 * n:]
        row = lambda r: tot_ref[r:r + 1, :]
        misc = row(R_MISC)
        grads = [jnp.concatenate([row(R_BADA01), row(R_BADA01 + 1)] + [row(k) for k in range(2, 6)], axis=1),
                 gc_ref[...], row(R_NMIX), row(R_NFFN),
                 misc[:, M_HNW:M_HNW + HG_DIM], misc[:, M_QNW:M_QNW + HEAD_DIM],
                 misc[:, M_KNW:M_KNW + HEAD_DIM], misc[:, M_SINK:M_SINK + ATT_HEADS]]
        for k in range(n):
            outs[k][...] = grads[k]
            outs[n + k][...], outs[2 * n + k][...], outs[3 * n + k][...] = _adamw_math(
                w_refs[k][...], grads[k], m_refs[k][...], v_refs[k][...])

    shapes = [jax.ShapeDtypeStruct(w.shape, F32) for w in ws]
    return pl.pallas_call(
        body, name="adamw_replicated", in_specs=[_VMEM_WHOLE] * (2 + 3 * n), out_specs=[_VMEM_WHOLE] * (4 * n),
        out_shape=shapes * 4, compiler_params=_cp(),
    )(tot, g_c_ctx, *ws, *ms, *vs)


def _lb_grads(dlb, lbl):
    def body(d_ref, l_ref, o_ref):
        for d in (0, 1):
            ll = l_ref[d]
            lb = _sigmoid(ll[0:1, :] - ll[1:2, :])
            t = d_ref[d:d + 1, :] * lb * (1.0 - lb)
            o_ref[d, 0:1, :] = t
            o_ref[d, 1:2, :] = -t

    return pl.pallas_call(
        body, name="lb_grads", in_specs=[_VMEM_WHOLE] * 2, out_specs=_VMEM_WHOLE,
        out_shape=jax.ShapeDtypeStruct((2, 2, HGW), F32), compiler_params=_cp(),
    )(dlb, lbl)


def _c_ctx_grad(terms, c_ctx):
    def body(t_ref, c_ref, o_ref):
        tot = t_ref[0, 8:9, :]
        for s in range(1, N_DEV):
            tot = tot + t_ref[s, 8:9, :]
        cv = c_ref[...]
        sg = _sigmoid(cv)
        o_ref[...] = tot * (sg * (1.0 + cv * (1.0 - sg)))

    return pl.pallas_call(
        body, name="c_ctx_grad", in_specs=[_VMEM_WHOLE] * 2, out_specs=_VMEM_WHOLE,
        out_shape=jax.ShapeDtypeStruct((1, D), F32), compiler_params=_cp(),
    )(terms, c_ctx)


def _in_perm():
    fz, bz, inp, kk, vv, qhg, ghg, qat, gates = 0, 512, 1024, 1536, 1664, 1792, 2304, 2816, 3328
    cols = []
    for h in range(HG_HEADS):
        for base in (fz, bz, inp, qhg):
            cols += list(range(base + 128 * h, base + 128 * (h + 1)))
    cols += list(range(ghg, ghg + 512)) + list(range(qat, qat + 512))
    cols += list(range(kk, kk + 128)) + list(range(vv, vv + 128))
    cols += list(range(gates, gates + 2048))
    return np.asarray(cols, np.int32)


_PERM = _in_perm()
_INV_PERM = np.argsort(_PERM).astype(np.int32)


def _take_rows(w, perm):
    cuts = [0] + [i for i in range(1, len(perm)) if perm[i] != perm[i - 1] + 1] + [len(perm)]
    return jnp.concatenate([w[int(perm[a]):int(perm[b - 1]) + 1] for a, b in zip(cuts[:-1], cuts[1:])],
                           axis=0)


def _cols_from_blocks(g):
    return jnp.transpose(g, (1, 0, 2)).reshape(g.shape[1], N_DEV * g.shape[2])


def _local_step(x2, ctx2, tgt, lbl, sh_in, sc_in, gate1, sh2, sc2, gate2, norm_mix_w, norm_ffn_w,
                hgrn_norm_w, q_norm_w, k_norm_w, attn_sinks, w_a, w_b, w_c, s_bh, s_ba, s_out,
                s_gate, s_up, s_down):
    first_last = lambda n: [(0, True), (n - 1, False)]
    h_all = _norm_mod_all(ctx2, x2, norm_mix_w, sh_in, sc_in)
    p_a = _mm_nt(h_all, w_a, tm=768, tn=1024, out_dtype=F32, name="proj_a")
    (o, st), (g_bh, g_ba, g_out, g_gate) = _hgrn_fwd(
        p_a, lbl, (_gather_comm([s_bh, s_ba, s_out, s_gate]), [(0, True), (HG_HEADS - 1, True), (HG_HEADS - 1, False)]))
    p_b = _mm_nt(h_all, w_b, tm=768, tn=1280, out_dtype=F32, name="proj_b")
    p_c = _mm_nt(h_all, w_c, tm=256, tn=2048, out_dtype=F32, name="proj_c", row_off=1, rows=S)
    cos, sin = _rope_tables()
    qnw_t, knw_t = jnp.tile(q_norm_w, (1, ATT_HEADS)), jnp.tile(k_norm_w, (1, KV_HEADS))
    y_hg, qn, k_rep, v_rep = _prep_fwd(p_b, o, cos, sin, hgrn_norm_w, qnw_t, knw_t)
    (y_at, lse), (g_up, g_down) = _attn_fwd(
        qn, k_rep, v_rep, attn_sinks,
        (_gather_comm([s_up, s_down]), [(0, True), (N_BLOCKS - 3, True), (N_BLOCKS - 1, False)]))
    w_bh, w_ba, w_o = g_bh.reshape(D, HGW), g_ba.reshape(D, ATW), g_out.reshape(D, D)
    g_gate, g_up, g_down = [g.reshape(N_FF_TILES, FF_TILE, D) for g in (g_gate, g_up, g_down)]
    a, b, mixed, r, x1, h2 = _merge_fwd(y_hg, y_at, p_c, x2, w_bh, w_ba, w_o, gate1, norm_ffn_w, sh2, sc2)

    act, d_gate, d_up, d_f, dx1, acc_ffn = _ffn_fused(x1, h2, tgt, g_gate, g_up, g_down, gate2,
                                                      norm_ffn_w, sc2)
    by_chip = lambda t: t.reshape((N_CHIPS, 2) + t.shape[1:])
    ff_by_chip = lambda t: t.reshape(N_CHIPS, 2, FF_BLK, D)
    t_down, _ = _mm_tn_blocked(act, d_f, "grad_down")
    t_down = ff_by_chip(t_down)
    t_gate, (f_down,) = _mm_tn_blocked(d_gate, h2, "grad_gate", (_sibling_comm([t_down]), first_last(N_FF_TILES)))
    t_gate = ff_by_chip(t_gate)
    t_up, (f_gate,) = _mm_tn_blocked(d_up, h2, "grad_up", (_sibling_comm([t_gate]), first_last(N_FF_TILES)))
    t_up = ff_by_chip(t_up)

    (d_r, d_a, d_b, dp_c, dy_hg, dy_at, acc_mg), (f_up,) = _merge_bwd(
        dx1, r, a, b, p_c, w_bh, w_ba, w_o, gate1, (_sibling_comm([t_up]), first_last(N_LAT_TILES)))
    c_down, c_gate, c_up = [_pair_sum(t, f, "pair_sum_" + nm) for t, f, nm in
                            ((t_down, f_down, "down"), (t_gate, f_gate, "gate"), (t_up, f_up, "up"))]
    t_out = _mm_tn(mixed, d_r, tk=512, nk=4, tm=512, tn=1024, out_dtype=BF16, name="grad_out")
    t_bh = _mm_tn(d_a, y_hg, tk=512, nk=4, tm=512, tn=512, out_dtype=BF16, name="grad_bh")
    t_ba = _mm_tn(d_b, y_at, tk=512, nk=4, tm=512, tn=512, out_dtype=BF16, name="grad_ba")
    t_bh, t_ba, t_out = [by_chip(t.reshape(N_DEV, D // N_DEV, t.shape[1])) for t in (t_bh, t_ba, t_out)]
    (dq, dk_rep, dv_rep, dsink), (f_bh, f_ba, f_out) = _attn_bwd(
        qn, k_rep, v_rep, attn_sinks, y_at, lse, dy_at, (_sibling_comm([t_bh, t_ba, t_out]), first_last(N_BLOCKS)))
    c_bh, c_ba, c_out = [_pair_sum(t, f, "pair_sum_" + nm) for t, f, nm in
                         ((t_bh, f_bh, "bh"), (t_ba, f_ba, "ba"), (t_out, f_out, "out"))]
    dp_b, d_o, acc_prep = _prep_bwd(p_b, o, cos, sin, hgrn_norm_w, qnw_t, knw_t,
                                    dy_hg, dq, dk_rep, dv_rep)
    (dp_a, dlb), (r_bh, r_ba, r_out, r_gate, r_up, r_down) = _hgrn_bwd(
        p_a, lbl, d_o, st, (_chip_comm([c_bh, c_ba, c_out, c_gate, c_up, c_down]), first_last(HG_HEADS)))
    grad_x, acc_in = _input_bwd(dp_a, dp_b, dp_c, w_a, w_b, w_c, ctx2, x2, dx1, norm_mix_w, sh_in, sc_in)
    t_a = _mm_tn(dp_a, h_all, tk=768, nk=3, tm=1024, tn=1024, out_dtype=BF16, name="grad_in_a")
    t_b = _mm_tn(dp_b, h_all, tk=768, nk=3, tm=640, tn=1024, out_dtype=BF16, name="grad_in_b")
    t_c = _mm_tn(dp_c, h_all, tk=256, nk=8, b_off=1, tm=1024, tn=1024, out_dtype=BF16, name="grad_in_c")
    t_in = by_chip(_take_rows(jnp.concatenate([t_a, t_b, t_c], axis=0), _INV_PERM).reshape(N_DEV, IN_BLK, D))
    (f_in,) = _run_comm(_sibling_comm([t_in]), "scatter_in_sibling")
    (r_in,) = _run_comm(_chip_comm([_pair_sum(t_in, f_in, "pair_sum_in")]), "scatter_in_chips")
    small = _pack_small(acc_in, acc_mg, acc_ffn, acc_prep, dsink, dlb)
    return grad_x, small, [r_in, r_bh, r_ba, r_out, r_gate, r_up, r_down]


def kernel(x, c, ctx, c_ctx, w_ada, b_ada, norm_mix_w, norm_ffn_w, w_in, hgrn_lb_logits, hgrn_norm_w, q_norm_w, k_norm_w, attn_sinks, w_branch_hgrn, w_branch_attn, w_out, w_ffn_gate, w_ffn_up, w_ffn_down, loss_target, m_c_ctx, m_w_ada, m_b_ada, m_norm_mix_w, m_norm_ffn_w, m_w_in, m_hgrn_lb_logits, m_hgrn_norm_w, m_q_norm_w, m_k_norm_w, m_attn_sinks, m_w_branch_hgrn, m_w_branch_attn, m_w_out, m_w_ffn_gate, m_w_ffn_up, m_w_ffn_down, v_c_ctx, v_w_ada, v_b_ada, v_norm_mix_w, v_norm_ffn_w, v_w_in, v_hgrn_lb_logits, v_hgrn_norm_w, v_q_norm_w, v_k_norm_w, v_attn_sinks, v_w_branch_hgrn, v_w_branch_attn, v_w_out, v_w_ffn_gate, v_w_ffn_up, v_w_ffn_down):
    me = 4 * lax.axis_index("x") + 2 * lax.axis_index("y") + lax.axis_index("c")
    x2, ctx2, tgt = x[0], ctx[0], loss_target[0]
    w_ada2, w_in2 = w_ada[0], w_in[0]

    blk = jnp.zeros((8, D), F32).at[0].set(c[0]).at[1, :256].set(hgrn_lb_logits.reshape(256))
    (g0,) = _all_gather([blk], "gather_cond", True)
    cc = jnp.zeros((16, D), F32).at[:8].set(g0[:, 0, :]).at[8].set(c_ctx)
    lbl = jnp.transpose(g0[:, 1, :256].reshape(N_DEV, 2, 2, 64), (1, 2, 0, 3)).reshape(2, 2, HGW)

    b_cols = lax.dynamic_slice(b_ada, (0, me * ADA_BLK), (1, ADA_BLK))
    (g1,) = _all_gather([_ada_rows(cc, w_ada2, b_cols)], "gather_mod", True)
    mod_all = _cols_from_blocks(g1)
    mod = lax.dynamic_slice(mod_all, (me, 0), (1, 6 * D)).reshape(6, D)
    mod_c = mod_all[8].reshape(6, D)
    sh1, sc1, gate1, sh2, sc2, gate2 = [mod[k:k + 1] for k in range(6)]
    sh_in = jnp.concatenate([mod_c[0:1], sh1], axis=0)
    sc_in = jnp.concatenate([mod_c[1:2], sc1], axis=0)

    shards = [w_branch_hgrn[0].T, w_branch_attn[0].T, w_out[0], w_ffn_gate[0].T, w_ffn_up[0].T, w_ffn_down[0]]
    (g_in,) = _all_gather([w_in2.T.astype(BF16)], "gather_w_in", False)
    w_in_t = _take_rows(g_in.reshape(IN_COLS, D), _PERM)
    w_a, w_b, w_c = w_in_t[:WA], w_in_t[WA:WA + WB], w_in_t[WA + WB:]

    grad_x, small, (r_in, r_bh, r_ba, r_out, r_gate, r_up, r_down) = _local_step(
        x2, ctx2, tgt, lbl, sh_in, sc_in, gate1, sh2, sc2, gate2, norm_mix_w, norm_ffn_w, hgrn_norm_w,
        q_norm_w, k_norm_w, attn_sinks, w_a, w_b, w_c, *[s.astype(BF16) for s in shards])

    big = {}
    for nm, rr, ww, mm, vv, tr, transposed in (
            ("w_in", r_in, w_in2, m_w_in[0], v_w_in[0], 336, True),
            ("w_branch_hgrn", r_bh, w_branch_hgrn[0], m_w_branch_hgrn[0], v_w_branch_hgrn[0], 128, True),
            ("w_branch_attn", r_ba, w_branch_attn[0], m_w_branch_attn[0], v_w_branch_attn[0], 128, True),
            ("w_out", r_out, w_out[0], m_w_out[0], v_w_out[0], 128, False),
            ("w_ffn_gate", r_gate, w_ffn_gate[0], m_w_ffn_gate[0], v_w_ffn_gate[0], 352, True),
            ("w_ffn_up", r_up, w_ffn_up[0], m_w_ffn_up[0], v_w_ffn_up[0], 352, True),
            ("w_ffn_down", r_down, w_ffn_down[0], m_w_ffn_down[0], v_w_ffn_down[0], 352, False)):
        if transposed:
            res = _adamw_sharded(rr, ww.T, mm.T, vv.T, "adamw_" + nm, tr)
            big[nm] = [t.T[None] for t in res]
        else:
            big[nm] = [t[None] for t in _adamw_sharded(rr, ww, mm, vv, "adamw_" + nm, tr)]

    (g2,) = _all_gather([small], "gather_small", True)
    tot = _sum_small(g2)
    dm = jnp.zeros((16, 6 * D), F32).at[:8].set(g2[:, R_DMOD:R_DMOD + 6, :].reshape(N_DEV, 6 * D))
    dm = dm.at[8, :2 * D].set(tot[R_DCTX:R_DCTX + 2].reshape(2 * D))
    dm_cols = lax.dynamic_slice(dm, (0, me * ADA_BLK), (16, ADA_BLK))
    g_w_ada, dsc_term = _ada_grads(cc, dm_cols, w_ada2)
    (g3,) = _all_gather([dsc_term], "gather_cctx", True)
    g_c_ctx = _c_ctx_grad(g3, c_ctx.reshape(1, D))
    g_lbl = _lb_grads(tot[R_DLB:R_DLB + 2, :HGW], lbl)
    g_lb_mine = lax.dynamic_slice(g_lbl, (0, 0, me * 64), (2, 2, 64))
    misc = tot[R_MISC]
    loss = misc[M_LOSS]

    rep_out = _adamw_replicated(
        tot, g_c_ctx,
        [b_ada, c_ctx.reshape(1, D), norm_mix_w, norm_ffn_w, hgrn_norm_w, q_norm_w, k_norm_w, attn_sinks],
        [m_b_ada, m_c_ctx.reshape(1, D), m_norm_mix_w, m_norm_ffn_w, m_hgrn_norm_w, m_q_norm_w, m_k_norm_w,
         m_attn_sinks],
        [v_b_ada, v_c_ctx.reshape(1, D), v_norm_mix_w, v_norm_ffn_w, v_hgrn_norm_w, v_q_norm_w, v_k_norm_w,
         v_attn_sinks])
    rep = []
    for kind in range(4):
        vals = dict(zip(_REP_NAMES, rep_out[kind * len(_REP_NAMES):(kind + 1) * len(_REP_NAMES)]))
        vals["c_ctx"] = vals["c_ctx"].reshape(D)
        rep.append(vals)

    d_ada, nm_ada, nv_ada = _adamw_plain(g_w_ada, w_ada2, m_w_ada[0], v_w_ada[0], "adamw_w_ada")
    ada = [t[None] for t in (g_w_ada, d_ada, nm_ada, nv_ada)]
    lb_w = hgrn_lb_logits.reshape(4, 64)
    d_lb, nm_lb, nv_lb = _adamw_plain(g_lb_mine.reshape(4, 64), lb_w, m_hgrn_lb_logits.reshape(4, 64),
                                      v_hgrn_lb_logits.reshape(4, 64), "adamw_lb")
    lbs = [t.reshape(2, 2, 64) for t in (g_lb_mine, d_lb, nm_lb, nv_lb)]

    names = ['c_ctx', 'w_ada', 'b_ada', 'norm_mix_w', 'norm_ffn_w', 'w_in', 'hgrn_lb_logits', 'hgrn_norm_w',
             'q_norm_w', 'k_norm_w', 'attn_sinks', 'w_branch_hgrn', 'w_branch_attn', 'w_out', 'w_ffn_gate',
             'w_ffn_up', 'w_ffn_down']
    outs = [loss, grad_x[None]]
    for kind in range(4):
        for nm in names:
            if nm == 'w_ada':
                outs.append(ada[kind])
            elif nm == 'hgrn_lb_logits':
                outs.append(lbs[kind])
            elif nm in big:
                outs.append(big[nm][kind])
            else:
                outs.append(rep[kind][nm])
    return tuple(outs)
```

```python
import functools
import math

import numpy as np
import jax
import jax.numpy as jnp
from jax import lax
from jax.experimental import pallas as pl
from jax.experimental.pallas import tpu as pltpu

F32 = jnp.float32
BF16 = jnp.bfloat16

N_DEV = 8
D = 1024
S = 2048
L = 256
T = L + S
TM = 256
N_TILES = T // TM
N_LAT_TILES = S // TM
HG_HEADS = 4
HG_DIM = 128
HGW = 512
CHUNK = 32
N_CHUNKS = T // CHUNK
N_CTX_CHUNKS = L // CHUNK
N_LAT_CHUNKS = S // CHUNK
ATT_HEADS = 8
KV_HEADS = 2
HEAD_DIM = 64
ATW = 512
KVW = 128
BLOCK = 128
N_BLOCKS = S // BLOCK
GRID_W = 64
ROPE_THETA = 10000.0
D_FF = 2816
FF_BLK = D_FF // N_DEV
FF_TILE = 256
N_FF_TILES = D_FF // FF_TILE
IN_COLS = 5376
IN_BLK = IN_COLS // N_DEV
ADA_BLK = 6 * D // N_DEV
EPS = 1e-6
WA, WB, WC = 2048, 1280, 2048

ADAM_LR = 0.001
ADAM_B1 = 0.9
ADAM_B2 = 0.999
ADAM_EPS = 1e-08
ADAM_WD = 0.01
ADAM_STEP = 10

VMEM_LIMIT = 56 * 1024 * 1024
MESH = pl.DeviceIdType.MESH


def _cp(sem=None, vmem=VMEM_LIMIT):
    return pltpu.CompilerParams(dimension_semantics=sem, vmem_limit_bytes=vmem)


def _full(shape):
    n = len(shape)
    return pl.BlockSpec(shape, lambda *_: (0,) * n)


_VMEM_WHOLE = pl.BlockSpec(memory_space=pltpu.VMEM)
_ANY = pl.BlockSpec(memory_space=pl.ANY)


def _sigmoid(v):
    return 1.0 / (1.0 + jnp.exp(-v))


def _dot(a, b):
    return jnp.dot(a, b, preferred_element_type=F32)


def _dot_nt(a, b):
    return lax.dot_general(a, b, (((1,), (1,)), ((), ())), preferred_element_type=F32)


def _dot_tn(a, b):
    return lax.dot_general(a, b, (((0,), (0,)), ((), ())), preferred_element_type=F32)


def _split3(v):
    hi = v.astype(BF16)
    r = v - hi.astype(F32)
    mid = r.astype(BF16)
    lo = (r - mid.astype(F32)).astype(BF16)
    return hi, mid, lo


def _dot_exact_rhs01(v, m01):
    hi, mid, lo = _split3(v)
    return _dot(hi, m01) + _dot(mid, m01) + _dot(lo, m01)


def _dot_exact_lhs01(m01, v):
    hi, mid, lo = _split3(v)
    return _dot(m01, hi) + _dot(m01, mid) + _dot(m01, lo)


def _dot_f32(a, b, dot=_dot):
    ah, am, al = _split3(a)
    bh, bm, bl = _split3(b)
    return (dot(ah, bh) + (dot(ah, bm) + dot(am, bh))
            + (dot(am, bm) + dot(ah, bl) + dot(al, bh)))


def _my_pos():
    return lax.axis_index("x"), lax.axis_index("y"), lax.axis_index("c")


class _Comm:
    def __init__(self, operands, out_shapes, sems, phases):
        self.operands, self.out_shapes, self.sems, self.phases = operands, out_shapes, sems, phases


def _gather_comm(blocks):
    n = len(blocks)

    def parts(ins, outs, sems):
        send_sems, recv_sems, local_sems = sems
        x, y, c = _my_pos()
        me, sibling = (x, y, c), (x, y, 1 - c)
        chips = [(1 - x, y), (x, 1 - y), (1 - x, 1 - y)]

        def slot(a, px, py, pc):
            return outs[a].at[4 * px + 2 * py + pc]

        def copy(a, k, block, to, src=None):
            return pltpu.make_async_remote_copy(
                src_ref=slot(a, *block) if src is None else src, dst_ref=slot(a, *block),
                send_sem=send_sems.at[a, k], recv_sem=recv_sems.at[a, k],
                device_id=to, device_id_type=MESH)

        mine = [pltpu.make_async_copy(ins[a], slot(a, *me), local_sems.at[a]) for a in range(n)]
        first = []
        for a in range(n):
            first.append(copy(a, 0, me, sibling, src=ins[a]))
            first += [copy(a, 1 + j, me, (*chip, c), src=ins[a]) for j, chip in enumerate(chips)]
        passed = [copy(a, 4 + j, (*chip, c), sibling) for j, chip in enumerate(chips) for a in range(n)]
        return c, me, sibling, chips, copy, mine, first, passed

    def start(ins, outs, sems):
        _, _, _, _, _, mine, first, _ = parts(ins, outs, sems)
        for cp in mine + first:
            cp.start()

    def forward(ins, outs, sems):
        c, me, _, chips, copy, _, _, passed = parts(ins, outs, sems)
        for j, chip in enumerate(chips):
            for a in range(n):
                copy(a, 1 + j, (*chip, c), me).wait_recv()
                passed[j * n + a].start()

    def finish(ins, outs, sems):
        c, me, sibling, chips, copy, mine, first, passed = parts(ins, outs, sems)
        for a in range(n):
            copy(a, 0, sibling, me).wait_recv()
            for j, chip in enumerate(chips):
                copy(a, 4 + j, (*chip, 1 - c), me).wait_recv()
        for cp in first + passed:
            cp.wait_send()
        for cp in mine:
            cp.wait()

    return _Comm(blocks, [jax.ShapeDtypeStruct((N_DEV,) + b.shape, b.dtype) for b in blocks],
                 [pltpu.SemaphoreType.DMA((n, 7)), pltpu.SemaphoreType.DMA((n, 7)), pltpu.SemaphoreType.DMA((n,))],
                 [start, forward, finish])


def _run_comm(comm, name, in_vmem=False):
    n_in, n_out = len(comm.operands), len(comm.out_shapes)

    def body(*refs):
        ins, outs, sems = refs[:n_in], refs[n_in:n_in + n_out], refs[n_in + n_out:]
        for phase in comm.phases:
            phase(ins, outs, sems)

    spec = _VMEM_WHOLE if in_vmem else _ANY
    return pl.pallas_call(
        body, name=name, out_shape=comm.out_shapes, in_specs=[spec] * n_in, out_specs=[spec] * n_out,
        scratch_shapes=comm.sems,
    )(*comm.operands)


def _carrier_call(body, comm, schedule, *, name, grid, in_specs, out_specs, out_shape, scratch_shapes, operands):
    n_in, n_out, n_scr = len(in_specs), len(out_specs), len(scratch_shapes)
    c_in, c_out = len(comm.operands), len(comm.out_shapes)

    def full_body(*refs):
        ins, refs = refs[:n_in], refs[n_in:]
        cins, refs = refs[:c_in], refs[c_in:]
        outs, refs = refs[:n_out], refs[n_out:]
        couts, refs = refs[:c_out], refs[c_out:]
        scr, csems = refs[:n_scr], refs[n_scr:]
        step = pl.program_id(0)

        def run(before):
            for (at, when_before), phase in zip(schedule, comm.phases):
                if when_before == before:
                    pl.when(step == at)(functools.partial(phase, cins, couts, csems))

        run(True)
        body(*ins, *outs, *scr)
        run(False)

    res = pl.pallas_call(
        full_body, name=name, grid=grid,
        in_specs=list(in_specs) + [_ANY] * c_in, out_specs=list(out_specs) + [_ANY] * c_out,
        out_shape=list(out_shape) + list(comm.out_shapes),
        scratch_shapes=list(scratch_shapes) + list(comm.sems),
        compiler_params=_cp(("arbitrary",)),
    )(*operands, *comm.operands)
    return res[:n_out], res[n_out:]


def _pcall(body, carried, *, name, grid, in_specs, out_specs, out_shape, scratch_shapes, operands):
    if carried is None:
        res = pl.pallas_call(body, name=name, grid=grid, in_specs=in_specs, out_specs=out_specs,
                             out_shape=out_shape, scratch_shapes=scratch_shapes,
                             compiler_params=_cp(("arbitrary",)))(*operands)
        return res, ()
    return _carrier_call(body, carried[0], carried[1], name=name, grid=grid, in_specs=in_specs,
                         out_specs=out_specs, out_shape=out_shape, scratch_shapes=scratch_shapes,
                         operands=operands)


def _all_gather(blocks, name, in_vmem):
    return _run_comm(_gather_comm(blocks), name, in_vmem)


N_CHIPS = 4


def _sibling_comm(contribs):
    n = len(contribs)

    def copies(ins, outs, sems):
        send_sems, recv_sems = sems
        x, y, c = _my_pos()
        return [pltpu.make_async_remote_copy(
            src_ref=ins[a].at[pl.ds(0, N_CHIPS), 1 - c], dst_ref=outs[a],
            send_sem=send_sems.at[a], recv_sem=recv_sems.at[a],
            device_id=(x, y, 1 - c), device_id_type=MESH) for a in range(n)]

    def start(ins, outs, sems):
        for cp in copies(ins, outs, sems):
            cp.start()

    def finish(ins, outs, sems):
        cps = copies(ins, outs, sems)
        for cp in cps:
            cp.wait_recv()
        for cp in cps:
            cp.wait_send()

    return _Comm(contribs, [jax.ShapeDtypeStruct((N_CHIPS,) + b.shape[2:], b.dtype) for b in contribs],
                 [pltpu.SemaphoreType.DMA((n,)), pltpu.SemaphoreType.DMA((n,))], [start, finish])


def _pair_sum(mine, theirs, name):
    _, _, rows, cols = mine.shape

    def body(m_ref, t_ref, o_ref):
        c = lax.axis_index("c")
        o_ref[...] = (m_ref[c].astype(F32) + t_ref[...].astype(F32)).astype(BF16)

    return pl.pallas_call(
        body, name=name, grid=(N_CHIPS,),
        in_specs=[pl.BlockSpec((None, 2, rows, cols), lambda q: (q, 0, 0, 0)),
                  pl.BlockSpec((None, rows, cols), lambda q: (q, 0, 0))],
        out_specs=pl.BlockSpec((None, rows, cols), lambda q: (q, 0, 0)),
        out_shape=jax.ShapeDtypeStruct((N_CHIPS, rows, cols), BF16),
        compiler_params=_cp(("parallel",)),
    )(mine, theirs)


def _chip_comm(sums):
    n = len(sums)

    def parts(ins, outs, sems):
        send_sems, recv_sems, local_sems = sems
        x, y, c = _my_pos()
        q_me = 2 * x + y
        chips = [(1 - x, y), (x, 1 - y), (1 - x, 1 - y)]
        mine = [pltpu.make_async_copy(ins[a].at[q_me], outs[a].at[q_me], local_sems.at[a]) for a in range(n)]
        sends, recvs = [], []
        for j, (px, py) in enumerate(chips):
            for a in range(n):
                q = 2 * px + py
                sends.append(pltpu.make_async_remote_copy(
                    src_ref=ins[a].at[q], dst_ref=outs[a].at[q_me],
                    send_sem=send_sems.at[a, j], recv_sem=recv_sems.at[a, j],
                    device_id=(px, py, c), device_id_type=MESH))
                recvs.append(pltpu.make_async_remote_copy(
                    src_ref=ins[a].at[q], dst_ref=outs[a].at[q],
                    send_sem=send_sems.at[a, j], recv_sem=recv_sems.at[a, j],
                    device_id=(x, y, c), device_id_type=MESH))
        return mine, sends, recvs

    def start(ins, outs, sems):
        mine, sends, _ = parts(ins, outs, sems)
        for cp in mine + sends:
            cp.start()

    def finish(ins, outs, sems):
        mine, sends, recvs = parts(ins, outs, sems)
        for cp in recvs:
            cp.wait_recv()
        for cp in sends:
            cp.wait_send()
        for cp in mine:
            cp.wait()

    return _Comm(sums, [jax.ShapeDtypeStruct(b.shape, b.dtype) for b in sums],
                 [pltpu.SemaphoreType.DMA((n, 3)), pltpu.SemaphoreType.DMA((n, 3)), pltpu.SemaphoreType.DMA((n,))],
                 [start, finish])


def _mm_nt(a, bt, *, tm, tn, out_dtype, name, row_off=0, rows=None):
    rows = a.shape[0] if rows is None else rows
    n, k = bt.shape

    def body(a_ref, b_ref, o_ref):
        o_ref[...] = _dot_nt(a_ref[...], b_ref[...]).astype(out_dtype)

    return pl.pallas_call(
        body, name=name, grid=(rows // tm, n // tn),
        in_specs=[pl.BlockSpec((tm, k), lambda i, j: (i + row_off, 0)),
                  pl.BlockSpec((tn, k), lambda i, j: (j, 0))],
        out_specs=pl.BlockSpec((tm, tn), lambda i, j: (i, j)),
        out_shape=jax.ShapeDtypeStruct((rows, n), out_dtype),
        compiler_params=_cp(("parallel", "parallel")),
    )(a, bt)


def _mm_tn(a, b, *, tk, nk, tm, tn, out_dtype, name, a_off=0, b_off=0):
    m, n = a.shape[1], b.shape[1]

    def body(a_ref, b_ref, o_ref, acc):
        kk = pl.program_id(2)

        @pl.when(kk == 0)
        def _():
            acc[...] = jnp.zeros_like(acc)

        acc[...] += _dot_tn(a_ref[...], b_ref[...])

        @pl.when(kk == nk - 1)
        def _():
            o_ref[...] = acc[...].astype(out_dtype)

    return pl.pallas_call(
        body, name=name, grid=(m // tm, n // tn, nk),
        in_specs=[pl.BlockSpec((tk, tm), lambda i, j, kk: (kk + a_off, i)),
                  pl.BlockSpec((tk, tn), lambda i, j, kk: (kk + b_off, j))],
        out_specs=pl.BlockSpec((tm, tn), lambda i, j, kk: (i, j)),
        out_shape=jax.ShapeDtypeStruct((m, n), out_dtype),
        scratch_shapes=[pltpu.VMEM((tm, tn), F32)],
        compiler_params=_cp(("parallel", "parallel", "arbitrary")),
    )(a, b)


def _mm_tn_blocked(a, b, name, carried=None):
    nb, _, w = a.shape
    n = b.shape[1]

    def body(a_ref, b_ref, o_ref):
        o_ref[...] = _dot_tn(a_ref[...], b_ref[...]).astype(BF16)

    (out,), extra = _pcall(
        body, carried, name=name, grid=(nb,),
        in_specs=[pl.BlockSpec((None, S, w), lambda j: (j, 0, 0)), _full((S, n))],
        out_specs=[pl.BlockSpec((None, w, n), lambda j: (j, 0, 0))],
        out_shape=[jax.ShapeDtypeStruct((nb, w, n), BF16)],
        scratch_shapes=[], operands=[a, b])
    return out, extra


def _ada_rows(cc, w_ada, b_cols):
    def body(c_ref, w_ref, b_ref, o_ref):
        cv = c_ref[...]
        o_ref[...] = _dot_f32(cv * _sigmoid(cv), w_ref[...]) + b_ref[...]

    return pl.pallas_call(
        body, name="ada_rows",
        in_specs=[_VMEM_WHOLE] * 3, out_specs=_VMEM_WHOLE,
        out_shape=jax.ShapeDtypeStruct((16, ADA_BLK), F32),
        compiler_params=_cp(),
    )(cc, w_ada, b_cols)


def _ada_grads(cc, dm_cols, w_ada):
    def body(c_ref, dm_ref, w_ref, gw_ref, dsc_ref):
        cv = c_ref[...]
        sc = cv * _sigmoid(cv)
        dm = dm_ref[...]
        gw_ref[...] = _dot_f32(sc, dm, dot=_dot_tn)
        dsc_ref[...] = _dot_f32(dm, w_ref[...], dot=_dot_nt)

    return pl.pallas_call(
        body, name="ada_grads",
        in_specs=[_VMEM_WHOLE] * 3, out_specs=[_VMEM_WHOLE] * 2,
        out_shape=[jax.ShapeDtypeStruct((D, ADA_BLK), F32), jax.ShapeDtypeStruct((16, D), F32)],
        compiler_params=_cp(),
    )(cc, dm_cols, w_ada)


def _lat(i):
    return jnp.maximum(i - 1, 0)


def _rms_mod(xv, nw, sh, sc):
    rstd = lax.rsqrt(jnp.mean(xv * xv, axis=-1, keepdims=True) + EPS)
    return (xv * rstd * nw) * (1.0 + sc) + sh


def _rms_mod_bwd(xv, nw, sc, dh):
    rstd = lax.rsqrt(jnp.mean(xv * xv, axis=-1, keepdims=True) + EPS)
    xhat = xv * rstd
    dn = dh * (1.0 + sc)
    dxhat = dn * nw
    dx = rstd * (dxhat - xhat * jnp.mean(dxhat * xhat, axis=-1, keepdims=True))
    return (dx, jnp.sum(dh, axis=0, keepdims=True), jnp.sum(dh * (xhat * nw), axis=0, keepdims=True),
            jnp.sum(dn * xhat, axis=0, keepdims=True))


def _norm_mod_all(ctx, x, nw, sh, sc):
    def body(ctx_ref, x_ref, nw_ref, sh_ref, sc_ref, o_ref):
        i = pl.program_id(0)
        sel = jnp.minimum(i, 1)
        xv = jnp.where(i == 0, ctx_ref[...], x_ref[...])
        o_ref[...] = _rms_mod(xv, nw_ref[...], sh_ref[pl.ds(sel, 1), :], sc_ref[pl.ds(sel, 1), :]).astype(BF16)

    return pl.pallas_call(
        body, name="norm_mod", grid=(N_TILES,),
        in_specs=[_full((TM, D)), pl.BlockSpec((TM, D), lambda i: (_lat(i), 0)),
                  _full((1, D)), _full((2, D)), _full((2, D))],
        out_specs=pl.BlockSpec((TM, D), lambda i: (i, 0)),
        out_shape=jax.ShapeDtypeStruct((T, D), BF16),
        compiler_params=_cp(("parallel",)),
    )(ctx, x, nw, sh, sc)


def _chunk_masks(reverse):
    row = lax.broadcasted_iota(jnp.int32, (TM, TM), 0)
    col = lax.broadcasted_iota(jnp.int32, (TM, TM), 1)
    same = (row // CHUNK) == (col // CHUNK)
    tri = same & ((col >= row) if reverse else (col <= row))
    return same, tri


def _chunk_order(i, reverse):
    if not reverse:
        return i
    return jnp.where(i < N_CTX_CHUNKS, N_CTX_CHUNKS - 1 - i, N_CHUNKS + N_CTX_CHUNKS - 1 - i)


def _decay_terms(z, lb, same01, tri01):
    f = lb + (1.0 - lb) * _sigmoid(z)
    g = jnp.log(f)
    hi, mid, lo = _split3(g)
    g3 = jnp.concatenate([hi, mid, lo], axis=1)
    b3 = _dot(tri01, g3)
    t3 = _dot(same01, g3)
    b = b3[:, :HG_DIM] + b3[:, HG_DIM:2 * HG_DIM] + b3[:, 2 * HG_DIM:]
    bt = t3[:, :HG_DIM] + t3[:, HG_DIM:2 * HG_DIM] + t3[:, 2 * HG_DIM:]
    return f, 1.0 - f, b, bt


def _chunk_outer(a, b):
    n = TM // CHUNK
    return jnp.einsum('ncv,nck->nvk', a.reshape(n, CHUNK, HG_DIM), b.reshape(n, CHUNK, HG_DIM),
                      preferred_element_type=F32)


def _hgrn_fwd(p_a, lbl, carried=None):
    cpt = TM // CHUNK

    def body(p_ref, lbl_ref, o_ref, st_ref, qd_s, kd_s, u_s, v_s, ebt_s):
        masks = [_chunk_masks(d == 1) for d in (0, 1)]
        same01 = jnp.where(masks[0][0], 1.0, 0.0).astype(BF16)
        tri = [m[1] for m in masks]
        tri01 = [jnp.where(t, 1.0, 0.0).astype(BF16) for t in tri]
        lb = [_sigmoid(lbl_ref[d][0:1, :] - lbl_ref[d][1:2, :]) for d in (0, 1)]

        def prep(r, carry):
            r0 = pl.multiple_of(r * TM, TM)
            vb = p_ref[pl.ds(r0, TM), 2 * HG_DIM:3 * HG_DIM].astype(BF16)
            v_s[pl.ds(r0, TM), :] = vb
            for d in (0, 1):
                z = p_ref[pl.ds(r0, TM), d * HG_DIM:(d + 1) * HG_DIM]
                _, k, b, bt = _decay_terms(z, lb[d], same01, tri01[d])
                u_s[d, pl.ds(r * cpt, cpt)] = _chunk_outer(vb, (k * jnp.exp(bt - b)).astype(BF16))
                ebt_s[d, pl.ds(r0, TM), :] = jnp.exp(bt)

                @pl.when(r >= 1)
                def _():
                    rl = pl.multiple_of(r0 - L, TM)
                    qr = p_ref[pl.ds(r0, TM), 3 * HG_DIM:4 * HG_DIM]
                    q = qr * _sigmoid(qr) * HG_DIM ** -0.5
                    qd_s[d, pl.ds(rl, TM), :] = (q * jnp.exp(b)).astype(BF16)
                    kd_s[d, pl.ds(rl, TM), :] = (k * jnp.exp(-b)).astype(BF16)

            return carry

        lax.fori_loop(0, N_TILES, prep, 0)

        def scan(i, sts):
            new = []
            for d in (0, 1):
                nn = _chunk_order(i, d == 1)
                c0 = pl.multiple_of(nn * CHUNK, CHUNK)
                st_ref[d, nn] = sts[d].astype(BF16)
                new.append(sts[d] * ebt_s[d, pl.ds(c0, 1), :] + u_s[d, nn])
            return tuple(new)

        zero = jnp.zeros((HG_DIM, HG_DIM), F32)
        lax.fori_loop(0, N_CHUNKS, scan, (zero, zero))

        def outp(r, carry):
            r0 = pl.multiple_of(r * TM, TM)
            vb = v_s[pl.ds(r0 + L, TM), :]
            o = jnp.zeros((TM, HG_DIM), F32)
            for d in (0, 1):
                qd = qd_s[d, pl.ds(r0, TM), :]
                a = jnp.where(tri[d], _dot_nt(qd, kd_s[d, pl.ds(r0, TM), :]), 0.0)
                stb = st_ref[d, pl.ds(N_CTX_CHUNKS + r * cpt, cpt)]
                inter = jnp.einsum('nck,nvk->ncv', qd.reshape(cpt, CHUNK, HG_DIM), stb,
                                   preferred_element_type=F32)
                o = o + _dot(a.astype(BF16), vb) + inter.reshape(TM, HG_DIM)
            o_ref[pl.ds(r0, TM), :] = o
            return carry

        lax.fori_loop(0, N_LAT_TILES, outp, 0)

    return _pcall(
        body, carried, name="hgrn_fwd", grid=(HG_HEADS,),
        in_specs=[pl.BlockSpec((T, 4 * HG_DIM), lambda h: (0, h)),
                  pl.BlockSpec((2, 2, HG_DIM), lambda h: (0, 0, h))],
        out_specs=[pl.BlockSpec((S, HG_DIM), lambda h: (0, h)),
                   pl.BlockSpec((2, None, N_CHUNKS, HG_DIM, HG_DIM), lambda h: (0, h, 0, 0, 0))],
        out_shape=[jax.ShapeDtypeStruct((S, HGW), F32),
                   jax.ShapeDtypeStruct((2, HG_HEADS, N_CHUNKS, HG_DIM, HG_DIM), BF16)],
        scratch_shapes=[pltpu.VMEM((2, S, HG_DIM), BF16), pltpu.VMEM((2, S, HG_DIM), BF16),
                        pltpu.VMEM((2, N_CHUNKS, HG_DIM, HG_DIM), F32), pltpu.VMEM((T, HG_DIM), BF16),
                        pltpu.VMEM((2, T, HG_DIM), F32)],
        operands=[p_a, lbl])


def _hgrn_bwd(p_a, lbl, d_o, st, carried=None):
    cpt = TM // CHUNK

    def rows(r):
        return r * TM if isinstance(r, int) else pl.multiple_of(r * TM, TM)

    def body(p_ref, lbl_ref, do_ref, st_ref, dp_ref, dlb_ref, b_s, bt_s, dbt_s, qd_s, dst_s, w_s):
        masks = [_chunk_masks(d == 1) for d in (0, 1)]
        same01 = jnp.where(masks[0][0], 1.0, 0.0).astype(BF16)
        tri = [m[1] for m in masks]
        tri01 = [jnp.where(t, 1.0, 0.0).astype(BF16) for t in tri]
        later01 = [tri01[1], tri01[0]]
        lb = [_sigmoid(lbl_ref[d][0:1, :] - lbl_ref[d][1:2, :]) for d in (0, 1)]
        dbt_s[...] = jnp.zeros_like(dbt_s)

        def prep_tile(r, latent):
            r0 = rows(r)
            for d in (0, 1):
                z = p_ref[pl.ds(r0, TM), d * HG_DIM:(d + 1) * HG_DIM]
                _, _, b, bt = _decay_terms(z, lb[d], same01, tri01[d])
                b_s[d, pl.ds(r0, TM), :] = b
                bt_s[d, pl.ds(r0, TM), :] = bt
                if latent:
                    rl = pl.multiple_of(r0 - L, TM)
                    qr = p_ref[pl.ds(r0, TM), 3 * HG_DIM:4 * HG_DIM]
                    qd = (qr * _sigmoid(qr) * HG_DIM ** -0.5 * jnp.exp(b)).astype(BF16)
                    qd_s[d, pl.ds(rl, TM), :] = qd
                    w_s[d, pl.ds(r * cpt, cpt)] = _chunk_outer(
                        do_ref[pl.ds(rl, TM), :].astype(BF16), qd).astype(BF16)

        prep_tile(0, False)
        w_s[:, pl.ds(0, N_CTX_CHUNKS)] = jnp.zeros((2, N_CTX_CHUNKS, HG_DIM, HG_DIM), BF16)

        def prep(r, carry):
            prep_tile(r, True)
            return carry

        lax.fori_loop(1, N_TILES, prep, 0)

        def rscan(j, dsts):
            i = N_CHUNKS - 1 - j
            new = []
            for d in (0, 1):
                nn = _chunk_order(i, d == 1)
                c0 = pl.multiple_of(nn * CHUNK, CHUNK)
                dst_s[d, nn] = dsts[d].astype(BF16)
                after = st_ref[d, _chunk_order(jnp.minimum(i + 1, N_CHUNKS - 1), d == 1)].astype(F32)
                dbt_s[d, pl.ds(c0, 1), :] = jnp.sum(after * dsts[d], axis=0, keepdims=True)
                new.append(dsts[d] * jnp.exp(bt_s[d, pl.ds(c0, 1), :]) + w_s[d, nn].astype(F32))
            return tuple(new)

        zero = jnp.zeros((HG_DIM, HG_DIM), F32)
        lax.fori_loop(0, N_CHUNKS, rscan, (zero, zero))

        def grad_tile(r, latent):
            r0 = rows(r)
            vb = p_ref[pl.ds(r0, TM), 2 * HG_DIM:3 * HG_DIM].astype(BF16)
            dv = jnp.zeros((TM, HG_DIM), F32)
            dq = jnp.zeros((TM, HG_DIM), F32)
            dlbs = []
            if latent:
                rl = pl.multiple_of(r0 - L, TM)
                qr = p_ref[pl.ds(r0, TM), 3 * HG_DIM:4 * HG_DIM]
                sq = _sigmoid(qr)
                do = do_ref[pl.ds(rl, TM), :].astype(BF16)
                da_full = _dot_nt(do, vb)
            for d in (0, 1):
                z = p_ref[pl.ds(r0, TM), d * HG_DIM:(d + 1) * HG_DIM]
                sz = _sigmoid(z)
                f = lb[d] + (1.0 - lb[d]) * sz
                k = 1.0 - f
                b = b_s[d, pl.ds(r0, TM), :]
                e2 = jnp.exp(bt_s[d, pl.ds(r0, TM), :] - b)
                dstb = dst_s[d, pl.ds(r * cpt, cpt)]
                kd2 = k * e2
                dkd2 = jnp.einsum('ncv,nvk->nck', vb.reshape(cpt, CHUNK, HG_DIM), dstb,
                                  preferred_element_type=F32).reshape(TM, HG_DIM)
                dv = dv + jnp.einsum('nck,nvk->ncv', kd2.astype(BF16).reshape(cpt, CHUNK, HG_DIM), dstb,
                                     preferred_element_type=F32).reshape(TM, HG_DIM)
                dk = dkd2 * e2
                db = -(kd2 * dkd2)
                if latent:
                    eb = jnp.exp(b)
                    enb = jnp.exp(-b)
                    qdf = qr * sq * HG_DIM ** -0.5 * eb
                    kdf = k * enb
                    qd = qd_s[d, pl.ds(rl, TM), :]
                    kd = kdf.astype(BF16)
                    a = jnp.where(tri[d], _dot_nt(qd, kd), 0.0).astype(BF16)
                    da = jnp.where(tri[d], da_full, 0.0).astype(BF16)
                    stb = st_ref[d, pl.ds(r * cpt, cpt)]
                    dqd = _dot(da, kd) + jnp.einsum(
                        'ncv,nvk->nck', do.reshape(cpt, CHUNK, HG_DIM), stb,
                        preferred_element_type=F32).reshape(TM, HG_DIM)
                    dkd = _dot_tn(da, qd)
                    dv = dv + _dot_tn(a, do)
                    dk = dk + dkd * enb
                    db = db + qdf * dqd - kdf * dkd
                    dq = dq + dqd * eb
                dg = (_dot_exact_lhs01(later01[d], db)
                      + _dot_exact_lhs01(same01, dbt_s[d, pl.ds(r0, TM), :]))
                df = dg / f - dk
                dp_ref[pl.ds(r0, TM), d * HG_DIM:(d + 1) * HG_DIM] = (
                    df * (1.0 - lb[d]) * sz * (1.0 - sz)).astype(BF16)
                dlbs.append(jnp.sum(df * (1.0 - sz), axis=0, keepdims=True))
            dp_ref[pl.ds(r0, TM), 2 * HG_DIM:3 * HG_DIM] = dv.astype(BF16)
            if latent:
                dq = dq * (HG_DIM ** -0.5) * (sq * (1.0 + qr * (1.0 - sq)))
            dp_ref[pl.ds(r0, TM), 3 * HG_DIM:4 * HG_DIM] = dq.astype(BF16)
            return dlbs

        dlb_ctx = grad_tile(0, False)

        def grads(r, acc):
            t = grad_tile(r, True)
            return (acc[0] + t[0], acc[1] + t[1])

        dlb = lax.fori_loop(1, N_TILES, grads, (dlb_ctx[0], dlb_ctx[1]))
        dlb_ref[0:1, :] = dlb[0]
        dlb_ref[1:2, :] = dlb[1]

    return _pcall(
        body, carried, name="hgrn_bwd", grid=(HG_HEADS,),
        in_specs=[pl.BlockSpec((T, 4 * HG_DIM), lambda h: (0, h)),
                  pl.BlockSpec((2, 2, HG_DIM), lambda h: (0, 0, h)),
                  pl.BlockSpec((S, HG_DIM), lambda h: (0, h)),
                  pl.BlockSpec((2, None, N_CHUNKS, HG_DIM, HG_DIM), lambda h: (0, h, 0, 0, 0))],
        out_specs=[pl.BlockSpec((T, 4 * HG_DIM), lambda h: (0, h)),
                   pl.BlockSpec((2, HG_DIM), lambda h: (0, h))],
        out_shape=[jax.ShapeDtypeStruct((T, WA), BF16), jax.ShapeDtypeStruct((2, HGW), F32)],
        scratch_shapes=[pltpu.VMEM((2, T, HG_DIM), F32), pltpu.VMEM((2, T, HG_DIM), F32),
                        pltpu.VMEM((2, T, HG_DIM), F32), pltpu.VMEM((2, S, HG_DIM), BF16),
                        pltpu.VMEM((2, N_CHUNKS, HG_DIM, HG_DIM), BF16),
                        pltpu.VMEM((2, N_CHUNKS, HG_DIM, HG_DIM), BF16)],
        operands=[p_a, lbl, d_o, st])


def _rope_tables():
    t = np.arange(S)
    inv = ROPE_THETA ** (-np.arange(0, 32, 2, dtype=np.float64) / 32)
    lane = np.arange(64)
    pos = np.where(lane[None, :] < 32, (t // GRID_W)[:, None], (t % GRID_W)[:, None]).astype(np.float64)
    ang = pos * inv[(lane % 32) % 16][None, :]
    sign = np.where((lane % 32) < 16, -1.0, 1.0)[None, :]
    cos = np.tile(np.cos(ang), (1, 2)).astype(np.float32)
    sin = np.tile(np.sin(ang) * sign, (1, 2)).astype(np.float32)
    return jnp.asarray(cos), jnp.asarray(sin)


def _rope_partner(v):
    lane = lax.broadcasted_iota(jnp.int32, (1, 128), 1)
    first = (lane % 32) < 16
    slabs = []
    for j in range(v.shape[1] // 128):
        s = v[:, 128 * j:128 * (j + 1)]
        slabs.append(jnp.where(first, pltpu.roll(s, 112, 1), pltpu.roll(s, 16, 1)))
    return slabs[0] if len(slabs) == 1 else jnp.concatenate(slabs, axis=1)


def _group_ones(width, group):
    r = lax.broadcasted_iota(jnp.int32, (width, width), 0)
    c = lax.broadcasted_iota(jnp.int32, (width, width), 1)
    return jnp.where((r // group) == (c // group), 1.0, 0.0).astype(BF16)


def _group_mean(v, ones01, group):
    hi = v.astype(BF16)
    lo = (v - hi.astype(F32)).astype(BF16)
    return (_dot(hi, ones01) + _dot(lo, ones01)) * (1.0 / group)


def _rep_matrix():
    r = lax.broadcasted_iota(jnp.int32, (KVW, ATW), 0)
    c = lax.broadcasted_iota(jnp.int32, (KVW, ATW), 1)
    return jnp.where(r == HEAD_DIM * (c // 256) + c % HEAD_DIM, 1.0, 0.0).astype(BF16)


def _tile_lanes(v, reps):
    return jnp.concatenate([v] * reps, axis=1)


def _prep_fwd(p_b, o, cos, sin, hnw, qnw, knw):
    def body(p_ref, o_ref, cos_ref, sin_ref, hnw_ref, qnw_ref, knw_ref, y_ref, q_ref, k_ref, v_ref):
        i = pl.program_id(0)
        rep = _rep_matrix()
        ones_k = _group_ones(KVW, HEAD_DIM)
        kr = p_ref[:, 1024:1152]
        krstd = lax.rsqrt(_group_mean(kr * kr, ones_k, HEAD_DIM) + EPS)
        kn = kr * krstd * knw_ref[...]
        v_ref[...] = _dot(p_ref[:, 1152:1280].astype(BF16), rep).astype(BF16)

        @pl.when(i == 0)
        def _():
            k_ref[...] = _dot(kn.astype(BF16), rep).astype(BF16)

        @pl.when(i > 0)
        def _():
            cs, sn = cos_ref[...], sin_ref[...]
            kro = kn * cs + _rope_partner(kn) * sn
            k_ref[...] = _dot(kro.astype(BF16), rep).astype(BF16)
            qr = p_ref[:, 512:1024]
            qrstd = lax.rsqrt(_group_mean(qr * qr, _group_ones(ATW, HEAD_DIM), HEAD_DIM) + EPS)
            qn = qr * qrstd * qnw_ref[...]
            qro = qn * _tile_lanes(cs, 4) + _rope_partner(qn) * _tile_lanes(sn, 4)
            q_ref[...] = (qro * HEAD_DIM ** -0.5).astype(BF16)
            ys = []
            for h in range(HG_HEADS):
                oh = o_ref[:, HG_DIM * h:HG_DIM * (h + 1)]
                gh = p_ref[:, HG_DIM * h:HG_DIM * (h + 1)]
                rstd = lax.rsqrt(jnp.mean(oh * oh, axis=-1, keepdims=True) + EPS)
                ys.append(oh * rstd * hnw_ref[...] * (gh * _sigmoid(gh)))
            y_ref[...] = jnp.concatenate(ys, axis=1).astype(BF16)

    return pl.pallas_call(
        body, name="prep_fwd", grid=(N_TILES,),
        in_specs=[pl.BlockSpec((TM, WB), lambda i: (i, 0)),
                  pl.BlockSpec((TM, HGW), lambda i: (_lat(i), 0)),
                  pl.BlockSpec((TM, 128), lambda i: (_lat(i), 0)),
                  pl.BlockSpec((TM, 128), lambda i: (_lat(i), 0)),
                  _full((1, HG_DIM)), _full((1, ATW)), _full((1, KVW))],
        out_specs=[pl.BlockSpec((TM, HGW), lambda i: (_lat(i), 0)),
                   pl.BlockSpec((TM, ATW), lambda i: (_lat(i), 0)),
                   pl.BlockSpec((TM, ATW), lambda i: (i, 0)),
                   pl.BlockSpec((TM, ATW), lambda i: (i, 0))],
        out_shape=[jax.ShapeDtypeStruct((S, HGW), BF16), jax.ShapeDtypeStruct((S, ATW), BF16),
                   jax.ShapeDtypeStruct((T, ATW), BF16), jax.ShapeDtypeStruct((T, ATW), BF16)],
        compiler_params=_cp(("arbitrary",)),
    )(p_b, o, cos, sin, hnw, qnw, knw)


def _prep_bwd(p_b, o, cos, sin, hnw, qnw, knw, dy_hg, dq, dk_rep, dv_rep, carried=None):
    def body(p_ref, o_ref, cos_ref, sin_ref, hnw_ref, qnw_ref, knw_ref, dy_ref, dq_ref, dk_ref, dv_ref,
             dp_ref, do_ref, acc_ref):
        i = pl.program_id(0)

        @pl.when(i == 0)
        def _():
            acc_ref[...] = jnp.zeros_like(acc_ref)

        rep = _rep_matrix()
        ones_k = _group_ones(KVW, HEAD_DIM)

        def fold(v):
            hi = v.astype(BF16)
            lo = (v - hi.astype(F32)).astype(BF16)
            return _dot_nt(hi, rep) + _dot_nt(lo, rep)

        kr = p_ref[:, 1024:1152]
        krstd = lax.rsqrt(_group_mean(kr * kr, ones_k, HEAD_DIM) + EPS)
        khat = kr * krstd
        kw = knw_ref[...]
        dkro = fold(dk_ref[...])
        dv = fold(dv_ref[...])

        def k_back(dkn):
            dkhat = dkn * kw
            dkr = krstd * (dkhat - khat * _group_mean(dkhat * khat, ones_k, HEAD_DIM))
            acc_ref[2:3, 0:KVW] += jnp.sum(dkn * khat, axis=0, keepdims=True)
            dp_ref[:, 1024:1152] = dkr.astype(BF16)
            dp_ref[:, 1152:1280] = dv.astype(BF16)

        @pl.when(i == 0)
        def _():
            k_back(dkro)
            dp_ref[:, 0:1024] = jnp.zeros((TM, 1024), BF16)

        @pl.when(i > 0)
        def _():
            cs, sn = cos_ref[...], sin_ref[...]
            k_back(dkro * cs + _rope_partner(dkro * sn))
            ones_q = _group_ones(ATW, HEAD_DIM)
            qr = p_ref[:, 512:1024]
            qrstd = lax.rsqrt(_group_mean(qr * qr, ones_q, HEAD_DIM) + EPS)
            qhat = qr * qrstd
            dqro = dq_ref[...] * HEAD_DIM ** -0.5
            dqn = dqro * _tile_lanes(cs, 4) + _rope_partner(dqro * _tile_lanes(sn, 4))
            dqhat = dqn * qnw_ref[...]
            dqr = qrstd * (dqhat - qhat * _group_mean(dqhat * qhat, ones_q, HEAD_DIM))
            acc_ref[1:2, :] += jnp.sum(dqn * qhat, axis=0, keepdims=True)
            dp_ref[:, 512:1024] = dqr.astype(BF16)
            dws = jnp.zeros((1, HG_DIM), F32)
            for h in range(HG_HEADS):
                sl = slice(HG_DIM * h, HG_DIM * (h + 1))
                oh, gh, dy = o_ref[:, sl], p_ref[:, sl], dy_ref[:, sl]
                rstd = lax.rsqrt(jnp.mean(oh * oh, axis=-1, keepdims=True) + EPS)
                ohat = oh * rstd
                sg = _sigmoid(gh)
                dp_ref[:, sl] = (dy * (ohat * hnw_ref[...]) * (sg * (1.0 + gh * (1.0 - sg)))).astype(BF16)
                dn = dy * (gh * sg)
                dws = dws + jnp.sum(dn * ohat, axis=0, keepdims=True)
                dohat = dn * hnw_ref[...]
                do_ref[:, sl] = rstd * (dohat - ohat * jnp.mean(dohat * ohat, axis=-1, keepdims=True))
            acc_ref[0:1, 0:HG_DIM] += dws

    return _pcall(
        body, carried, name="prep_bwd", grid=(N_TILES,),
        in_specs=[pl.BlockSpec((TM, WB), lambda i: (i, 0)),
                  pl.BlockSpec((TM, HGW), lambda i: (_lat(i), 0)),
                  pl.BlockSpec((TM, 128), lambda i: (_lat(i), 0)),
                  pl.BlockSpec((TM, 128), lambda i: (_lat(i), 0)),
                  _full((1, HG_DIM)), _full((1, ATW)), _full((1, KVW)),
                  pl.BlockSpec((TM, HGW), lambda i: (_lat(i), 0)),
                  pl.BlockSpec((TM, ATW), lambda i: (_lat(i), 0)),
                  pl.BlockSpec((TM, ATW), lambda i: (i, 0)),
                  pl.BlockSpec((TM, ATW), lambda i: (i, 0))],
        out_specs=[pl.BlockSpec((TM, WB), lambda i: (i, 0)),
                   pl.BlockSpec((TM, HGW), lambda i: (_lat(i), 0)),
                   _full((8, ATW))],
        out_shape=[jax.ShapeDtypeStruct((T, WB), BF16), jax.ShapeDtypeStruct((S, HGW), F32),
                   jax.ShapeDtypeStruct((8, ATW), F32)],
        scratch_shapes=[], operands=[p_b, o, cos, sin, hnw, qnw, knw, dy_hg, dq, dk_rep, dv_rep])


NEG = -1e30
_CTX_BLOCKS = L // BLOCK


def _attn_window_specs():
    prev = pl.BlockSpec((BLOCK, ATW), lambda i: (jnp.maximum(i - 1, 0) + _CTX_BLOCKS, 0))
    own = pl.BlockSpec((BLOCK, ATW), lambda i: (i + _CTX_BLOCKS, 0))
    nxt = pl.BlockSpec((BLOCK, ATW), lambda i: (jnp.minimum(i + 1, N_BLOCKS - 1) + _CTX_BLOCKS, 0))
    return [prev, own, nxt, _full((L, ATW))]


def _attn_valid(i):
    qi = lax.broadcasted_iota(jnp.int32, (4 * BLOCK, 3 * BLOCK), 0) % BLOCK
    kj = lax.broadcasted_iota(jnp.int32, (4 * BLOCK, 3 * BLOCK), 1)
    return ((jnp.abs(kj - BLOCK - qi) <= BLOCK) & ((kj >= BLOCK) | (i > 0))
            & ((kj < 2 * BLOCK) | (i < N_BLOCKS - 1)))


def _stack_heads(qg):
    lane = lax.broadcasted_iota(jnp.int32, (1, 256), 1) // HEAD_DIM
    return jnp.concatenate([jnp.where(lane == g, qg, jnp.zeros_like(qg)) for g in range(4)], axis=0)


def _unstack_heads(v4):
    lane = lax.broadcasted_iota(jnp.int32, (1, 256), 1) // HEAD_DIM
    out = jnp.where(lane == 0, v4[0:BLOCK], 0.0)
    for g in range(1, 4):
        out = out + jnp.where(lane == g, v4[g * BLOCK:(g + 1) * BLOCK], 0.0)
    return out


def _sink_rows(sink_ref, hk):
    return jnp.concatenate(
        [jnp.broadcast_to(sink_ref[0:1, 4 * hk + g:4 * hk + g + 1], (BLOCK, 1)) for g in range(4)], axis=0)


def _attn_fwd(q, k_rep, v_rep, sinks, carried=None):
    def body(q_ref, kp, ko, kn, kc, vp, vo, vn, vc, sink_ref, y_ref, lse_ref):
        i = pl.program_id(0)
        valid = _attn_valid(i)
        lane8 = lax.broadcasted_iota(jnp.int32, (1, ATT_HEADS), 1)
        lse_out = jnp.zeros((BLOCK, ATT_HEADS), F32)
        for hk in range(KV_HEADS):
            sl = slice(256 * hk, 256 * (hk + 1))
            q4 = _stack_heads(q_ref[:, sl])
            kl = jnp.concatenate([kp[:, sl], ko[:, sl], kn[:, sl]], axis=0)
            vl = jnp.concatenate([vp[:, sl], vo[:, sl], vn[:, sl]], axis=0)
            s_loc = jnp.where(valid, _dot_nt(q4, kl), NEG)
            s_ctx = _dot_nt(q4, kc[:, sl])
            sink = _sink_rows(sink_ref, hk)
            m = jnp.maximum(jnp.maximum(jnp.max(s_loc, axis=1, keepdims=True),
                                        jnp.max(s_ctx, axis=1, keepdims=True)), sink)
            p_loc = jnp.exp(s_loc - m)
            p_ctx = jnp.exp(s_ctx - m)
            den = (jnp.sum(p_loc, axis=1, keepdims=True) + jnp.sum(p_ctx, axis=1, keepdims=True)
                   + jnp.exp(sink - m))
            o4 = (_dot(p_loc.astype(BF16), vl) + _dot(p_ctx.astype(BF16), vc[:, sl])) / den
            y_ref[:, sl] = _unstack_heads(o4).astype(BF16)
            lse4 = m + jnp.log(den)
            for g in range(4):
                lse_out = lse_out + jnp.where(lane8 == 4 * hk + g, lse4[g * BLOCK:(g + 1) * BLOCK], 0.0)
        lse_ref[...] = lse_out

    return _pcall(
        body, carried, name="attn_fwd", grid=(N_BLOCKS,),
        in_specs=[pl.BlockSpec((BLOCK, ATW), lambda i: (i, 0))] + _attn_window_specs()
        + _attn_window_specs() + [_full((1, ATT_HEADS))],
        out_specs=[pl.BlockSpec((BLOCK, ATW), lambda i: (i, 0)),
                   pl.BlockSpec((BLOCK, ATT_HEADS), lambda i: (i, 0))],
        out_shape=[jax.ShapeDtypeStruct((S, ATW), BF16), jax.ShapeDtypeStruct((S, ATT_HEADS), F32)],
        scratch_shapes=[],
        operands=[q, k_rep, k_rep, k_rep, k_rep, v_rep, v_rep, v_rep, v_rep, sinks])


def _attn_bwd(q, k_rep, v_rep, sinks, y_at, lse, dy, carried=None):
    def body(q_ref, kp, ko, kn, kc, vp, vo, vn, vc, sink_ref, y_ref, lse_ref, dy_ref,
             dq_ref, dk_ref, dv_ref, dsink_ref, dk_acc, dv_acc):
        i = pl.program_id(0)

        @pl.when(i == 0)
        def _():
            dk_acc[...] = jnp.zeros_like(dk_acc)
            dv_acc[...] = jnp.zeros_like(dv_acc)
            dk_ref[pl.ds(0, L), :] = jnp.zeros((L, ATW), F32)
            dv_ref[pl.ds(0, L), :] = jnp.zeros((L, ATW), F32)
            dsink_ref[...] = jnp.zeros_like(dsink_ref)

        valid = _attn_valid(i)
        lane8 = lax.broadcasted_iota(jnp.int32, (1, ATT_HEADS), 1)
        w0 = pl.multiple_of(i * BLOCK, BLOCK)
        dsink = jnp.zeros((1, ATT_HEADS), F32)
        for hk in range(KV_HEADS):
            sl = slice(256 * hk, 256 * (hk + 1))
            q4 = _stack_heads(q_ref[:, sl])
            do4f = _stack_heads(dy_ref[:, sl])
            o4 = _stack_heads(y_ref[:, sl]).astype(F32)
            do4 = do4f.astype(BF16)
            kl = jnp.concatenate([kp[:, sl], ko[:, sl], kn[:, sl]], axis=0)
            vl = jnp.concatenate([vp[:, sl], vo[:, sl], vn[:, sl]], axis=0)
            lse4 = jnp.concatenate(
                [jnp.sum(jnp.where(lane8 == 4 * hk + g, lse_ref[...], 0.0), axis=1, keepdims=True)
                 for g in range(4)], axis=0)
            p_loc = jnp.where(valid, jnp.exp(_dot_nt(q4, kl) - lse4), 0.0)
            p_ctx = jnp.exp(_dot_nt(q4, kc[:, sl]) - lse4)
            delta = jnp.sum(do4f * o4, axis=1, keepdims=True)
            ds_loc = (p_loc * (_dot_nt(do4, vl) - delta)).astype(BF16)
            ds_ctx = (p_ctx * (_dot_nt(do4, vc[:, sl]) - delta)).astype(BF16)
            dq_ref[:, sl] = _unstack_heads(_dot(ds_loc, kl) + _dot(ds_ctx, kc[:, sl]))
            dk_acc[pl.ds(w0, 3 * BLOCK), sl] += _dot_tn(ds_loc, q4)
            dv_acc[pl.ds(w0, 3 * BLOCK), sl] += _dot_tn(p_loc.astype(BF16), do4)
            dk_ref[pl.ds(0, L), sl] += _dot_tn(ds_ctx, q4)
            dv_ref[pl.ds(0, L), sl] += _dot_tn(p_ctx.astype(BF16), do4)
            p_sink = jnp.exp(_sink_rows(sink_ref, hk) - lse4)
            for g in range(4):
                rows = slice(g * BLOCK, (g + 1) * BLOCK)
                dsink = dsink + jnp.where(lane8 == 4 * hk + g,
                                          -jnp.sum(p_sink[rows] * delta[rows], axis=0, keepdims=True), 0.0)
        dsink_ref[...] += dsink

        @pl.when(i == N_BLOCKS - 1)
        def _():
            dk_ref[pl.ds(L, S), :] = dk_acc[pl.ds(BLOCK, S), :]
            dv_ref[pl.ds(L, S), :] = dv_acc[pl.ds(BLOCK, S), :]

    row_q = pl.BlockSpec((BLOCK, ATW), lambda i: (i, 0))
    return _pcall(
        body, carried, name="attn_bwd", grid=(N_BLOCKS,),
        in_specs=[row_q] + _attn_window_specs() + _attn_window_specs()
        + [_full((1, ATT_HEADS)), row_q, pl.BlockSpec((BLOCK, ATT_HEADS), lambda i: (i, 0)), row_q],
        out_specs=[row_q, _full((T, ATW)), _full((T, ATW)), _full((1, ATT_HEADS))],
        out_shape=[jax.ShapeDtypeStruct((S, ATW), F32), jax.ShapeDtypeStruct((T, ATW), F32),
                   jax.ShapeDtypeStruct((T, ATW), F32), jax.ShapeDtypeStruct((1, ATT_HEADS), F32)],
        scratch_shapes=[pltpu.VMEM((S + 2 * BLOCK, ATW), F32), pltpu.VMEM((S + 2 * BLOCK, ATW), F32)],
        operands=[q, k_rep, k_rep, k_rep, k_rep, v_rep, v_rep, v_rep, v_rep, sinks, y_at, lse, dy])


def _merge_fwd(y_hg, y_at, p_c, x, w_bh, w_ba, w_out, g1, nfw, sh2, sc2):
    def body(yh_ref, ya_ref, g_ref, x_ref, wbh_ref, wba_ref, wo_ref, g1_ref, nfw_ref, sh_ref, sc_ref,
             a_ref, b_ref, mx_ref, r_ref, x1_ref, h2_ref):
        a = _dot_nt(yh_ref[...], wbh_ref[...])
        b = _dot_nt(ya_ref[...], wba_ref[...])
        mixed = (_sigmoid(g_ref[:, :D]) * a + _sigmoid(g_ref[:, D:]) * b).astype(BF16)
        r = _dot(mixed, wo_ref[...])
        x1 = x_ref[...] + g1_ref[...] * r
        a_ref[...] = a
        b_ref[...] = b
        mx_ref[...] = mixed
        r_ref[...] = r
        x1_ref[...] = x1
        h2_ref[...] = _rms_mod(x1, nfw_ref[...], sh_ref[...], sc_ref[...]).astype(BF16)

    row = lambda w: pl.BlockSpec((TM, w), lambda i: (i, 0))
    vec = _full((1, D))
    return pl.pallas_call(
        body, name="merge_fwd", grid=(N_LAT_TILES,),
        in_specs=[row(HGW), row(ATW), row(WC), row(D), _VMEM_WHOLE, _VMEM_WHOLE, _VMEM_WHOLE,
                  vec, vec, vec, vec],
        out_specs=[row(D)] * 6,
        out_shape=[jax.ShapeDtypeStruct((S, D), dt) for dt in (F32, F32, BF16, F32, F32, BF16)],
        compiler_params=_cp(("parallel",)),
    )(y_hg, y_at, p_c, x, w_bh, w_ba, w_out, g1, nfw, sh2, sc2)


def _merge_bwd(dx1, r, a, b, p_c, w_bh, w_ba, w_out, g1, carried=None):
    def body(dx_ref, r_ref, a_ref, b_ref, g_ref, wbh_ref, wba_ref, wo_ref, g1_ref,
             dr_ref, da_ref, db_ref, dg_ref, dyh_ref, dya_ref, acc_ref):
        @pl.when(pl.program_id(0) == 0)
        def _():
            acc_ref[...] = jnp.zeros_like(acc_ref)

        dx1v = dx_ref[...]
        acc_ref[0:1, :] += jnp.sum(dx1v * r_ref[...], axis=0, keepdims=True)
        dr = (g1_ref[...] * dx1v).astype(BF16)
        dr_ref[...] = dr
        dmix = _dot_nt(dr, wo_ref[...])
        sh, sa = _sigmoid(g_ref[:, :D]), _sigmoid(g_ref[:, D:])
        da = (dmix * sh).astype(BF16)
        db = (dmix * sa).astype(BF16)
        da_ref[...] = da
        db_ref[...] = db
        dg_ref[:, :D] = (dmix * a_ref[...] * sh * (1.0 - sh)).astype(BF16)
        dg_ref[:, D:] = (dmix * b_ref[...] * sa * (1.0 - sa)).astype(BF16)
        dyh_ref[...] = _dot(da, wbh_ref[...])
        dya_ref[...] = _dot(db, wba_ref[...])

    row = lambda w: pl.BlockSpec((TM, w), lambda i: (i, 0))
    return _pcall(
        body, carried, name="merge_bwd", grid=(N_LAT_TILES,),
        in_specs=[row(D), row(D), row(D), row(D), row(WC), _VMEM_WHOLE, _VMEM_WHOLE, _VMEM_WHOLE,
                  _full((1, D))],
        out_specs=[row(D), row(D), row(D), row(WC), row(HGW), row(ATW), _full((8, D))],
        out_shape=[jax.ShapeDtypeStruct((S, D), BF16), jax.ShapeDtypeStruct((S, D), BF16),
                   jax.ShapeDtypeStruct((S, D), BF16), jax.ShapeDtypeStruct((S, WC), BF16),
                   jax.ShapeDtypeStruct((S, HGW), F32), jax.ShapeDtypeStruct((S, ATW), F32),
                   jax.ShapeDtypeStruct((8, D), F32)],
        scratch_shapes=[], operands=[dx1, r, a, b, p_c, w_bh, w_ba, w_out, g1])


def _ffn_fused(x1, h2, tgt, w_gate, w_up, w_down, g2, nfw, sc2):
    def body(x1_ref, h2_ref, t_ref, wg_ref, wu_ref, wd_ref, g2_ref, nfw_ref, sc_ref,
             act_ref, dgt_ref, dup_ref, df_ref, dx_ref, acc_ref, gs, us):
        @pl.when(pl.program_id(0) == 0)
        def _():
            acc_ref[...] = jnp.zeros_like(acc_ref)

        h2 = h2_ref[...]
        f = jnp.zeros((TM, D), F32)
        for j in range(N_FF_TILES):
            g = _dot_nt(h2, wg_ref[j])
            u = _dot_nt(h2, wu_ref[j])
            gs[j] = g
            us[j] = u
            act = (g * _sigmoid(g) * u).astype(BF16)
            act_ref[j] = act
            f = f + _dot(act, wd_ref[j])
        x1v = x1_ref[...]
        g2 = g2_ref[...]
        diff = x1v + g2 * f - t_ref[...]
        dy = diff * (1.0 / D)
        df = (g2 * dy).astype(BF16)
        df_ref[...] = df
        dh2 = jnp.zeros((TM, D), F32)
        for j in range(N_FF_TILES):
            g, u = gs[j], us[j]
            sg = _sigmoid(g)
            dact = _dot_nt(df, wd_ref[j])
            dgate = (dact * u * (sg * (1.0 + g * (1.0 - sg)))).astype(BF16)
            dup = (dact * (g * sg)).astype(BF16)
            dgt_ref[j] = dgate
            dup_ref[j] = dup
            dh2 = dh2 + _dot(dgate, wg_ref[j]) + _dot(dup, wu_ref[j])
        dx, dsh, dsc, dnw = _rms_mod_bwd(x1v, nfw_ref[...], sc_ref[...], dh2)
        dx_ref[...] = dy + dx
        acc_ref[0:1, :] += dsh
        acc_ref[1:2, :] += dsc
        acc_ref[2:3, :] += dnw
        acc_ref[3:4, :] += jnp.sum(dy * f, axis=0, keepdims=True)
        acc_ref[4:5, :] += 0.5 * jnp.sum(jnp.sum(diff * diff, axis=1, keepdims=True), axis=0,
                                         keepdims=True) * (1.0 / D)

    row = lambda dt_w: pl.BlockSpec((TM, dt_w), lambda i: (i, 0))
    blk = pl.BlockSpec((N_FF_TILES, TM, FF_TILE), lambda i: (0, i, 0))
    vec = _full((1, D))
    return pl.pallas_call(
        body, name="ffn_fused", grid=(N_LAT_TILES,),
        in_specs=[row(D), row(D), row(D), _VMEM_WHOLE, _VMEM_WHOLE, _VMEM_WHOLE, vec, vec, vec],
        out_specs=[blk, blk, blk, row(D), row(D), _full((8, D))],
        out_shape=[jax.ShapeDtypeStruct((N_FF_TILES, S, FF_TILE), BF16)] * 3
        + [jax.ShapeDtypeStruct((S, D), BF16), jax.ShapeDtypeStruct((S, D), F32),
           jax.ShapeDtypeStruct((8, D), F32)],
        scratch_shapes=[pltpu.VMEM((N_FF_TILES, TM, FF_TILE), F32), pltpu.VMEM((N_FF_TILES, TM, FF_TILE), F32)],
        compiler_params=_cp(("arbitrary",)),
    )(x1, h2, tgt, w_gate, w_up, w_down, g2, nfw, sc2)


def _proj_bc(h_all, w_b, w_c, carried=None):
    def body(h_ref, wb_ref, wc_ref, pb_ref, pc_ref):
        h = h_ref[...]
        pb_ref[...] = _dot_nt(h, wb_ref[...])

        @pl.when(pl.program_id(0) > 0)
        def _():
            pc_ref[...] = _dot_nt(h, wc_ref[...])

    return _pcall(
        body, carried, name="proj_bc", grid=(N_TILES,),
        in_specs=[pl.BlockSpec((TM, D), lambda i: (i, 0)), _VMEM_WHOLE, _VMEM_WHOLE],
        out_specs=[pl.BlockSpec((TM, WB), lambda i: (i, 0)), pl.BlockSpec((TM, WC), lambda i: (_lat(i), 0))],
        out_shape=[jax.ShapeDtypeStruct((T, WB), F32), jax.ShapeDtypeStruct((S, WC), F32)],
        scratch_shapes=[], operands=[h_all, w_b, w_c])


def _input_bwd(dp_a, dp_b, dp_c, w_a, w_b, w_c, ctx, x, dx1, nw, sh, sc, carried=None):
    def body(da_ref, db_ref, dc_ref, wa_ref, wb_ref, wc_ref, ctx_ref, x_ref, dx1_ref, nw_ref, sh_ref,
             sc_ref, gx_ref, acc_ref):
        i = pl.program_id(0)

        @pl.when(i == 0)
        def _():
            acc_ref[...] = jnp.zeros_like(acc_ref)

        dh = _dot(da_ref[...], wa_ref[...]) + _dot(db_ref[...], wb_ref[...])

        @pl.when(i == 0)
        def _():
            _, dsh, dsc, dnw = _rms_mod_bwd(ctx_ref[...], nw_ref[...], sc_ref[0:1, :], dh)
            acc_ref[3:4, :] += dsh
            acc_ref[4:5, :] += dsc
            acc_ref[2:3, :] += dnw

        @pl.when(i > 0)
        def _():
            dhl = dh + _dot(dc_ref[...], wc_ref[...])
            dx, dsh, dsc, dnw = _rms_mod_bwd(x_ref[...], nw_ref[...], sc_ref[1:2, :], dhl)
            gx_ref[...] = dx1_ref[...] + dx
            acc_ref[0:1, :] += dsh
            acc_ref[1:2, :] += dsc
            acc_ref[2:3, :] += dnw

    lat = lambda w: pl.BlockSpec((TM, w), lambda i: (_lat(i), 0))
    return _pcall(
        body, carried, name="input_bwd", grid=(N_TILES,),
        in_specs=[pl.BlockSpec((TM, WA), lambda i: (i, 0)), pl.BlockSpec((TM, WB), lambda i: (i, 0)),
                  lat(WC), _VMEM_WHOLE, _VMEM_WHOLE, _VMEM_WHOLE, _full((TM, D)), lat(D), lat(D),
                  _full((1, D)), _full((2, D)), _full((2, D))],
        out_specs=[lat(D), _full((8, D))],
        out_shape=[jax.ShapeDtypeStruct((S, D), F32), jax.ShapeDtypeStruct((8, D), F32)],
        scratch_shapes=[], operands=[dp_a, dp_b, dp_c, w_a, w_b, w_c, ctx, x, dx1, nw, sh, sc])


_C1 = 1.0 - ADAM_B1 ** ADAM_STEP
_C2 = 1.0 - ADAM_B2 ** ADAM_STEP


def _adamw_math(w, g, m, v):
    m = ADAM_B1 * m + (1.0 - ADAM_B1) * g
    v = ADAM_B2 * v + (1.0 - ADAM_B2) * (g * g)
    m_hat = m / _C1
    v_hat = v / _C2
    delta = -ADAM_LR * (m_hat / (jnp.sqrt(v_hat) + ADAM_EPS) + ADAM_WD * w)
    return delta, m, v


def _adamw_sharded(terms, w, m, v, name, tr):
    rows, cols = w.shape

    def body(t_ref, w_ref, m_ref, v_ref, g_ref, d_ref, nm_ref, nv_ref):
        g = t_ref[0].astype(F32)
        for s in range(1, N_CHIPS):
            g = g + t_ref[s].astype(F32)
        g_ref[...] = g
        d_ref[...], nm_ref[...], nv_ref[...] = _adamw_math(w_ref[...], g, m_ref[...], v_ref[...])

    blk = pl.BlockSpec((tr, cols), lambda i: (i, 0))
    return pl.pallas_call(
        body, name=name, grid=(rows // tr,),
        in_specs=[pl.BlockSpec((N_CHIPS, tr, cols), lambda i: (0, i, 0)), blk, blk, blk],
        out_specs=[blk] * 4,
        out_shape=[jax.ShapeDtypeStruct((rows, cols), F32)] * 4,
        compiler_params=_cp(("parallel",)),
    )(terms, w, m, v)


def _adamw_plain(g, w, m, v, name):
    def body(g_ref, w_ref, m_ref, v_ref, d_ref, nm_ref, nv_ref):
        d_ref[...], nm_ref[...], nv_ref[...] = _adamw_math(w_ref[...], g_ref[...], m_ref[...], v_ref[...])

    return pl.pallas_call(
        body, name=name, in_specs=[_VMEM_WHOLE] * 4, out_specs=[_VMEM_WHOLE] * 3,
        out_shape=[jax.ShapeDtypeStruct(w.shape, F32)] * 3,
        compiler_params=_cp(),
    )(g, w, m, v)


SMALL_ROWS = 16
R_DMOD, R_DCTX, R_NMIX, R_NFFN, R_MISC, R_DLB, R_BADA01 = 0, 6, 8, 9, 10, 11, 13
M_HNW, M_QNW, M_KNW, M_SINK, M_LOSS = 0, 128, 256, 384, 512


def _pack_small(acc_in, acc_mg, acc_ffn, acc_prep, dsink, dlb):
    def body(in_ref, mg_ref, ff_ref, pp_ref, ds_ref, dlb_ref, o_ref):
        o_ref[...] = jnp.zeros_like(o_ref)
        o_ref[0:2, :] = in_ref[0:2, :]
        o_ref[2:3, :] = mg_ref[0:1, :]
        o_ref[3:5, :] = ff_ref[0:2, :]
        o_ref[5:6, :] = ff_ref[3:4, :]
        o_ref[6:8, :] = in_ref[3:5, :]
        o_ref[8:9, :] = in_ref[2:3, :]
        o_ref[9:10, :] = ff_ref[2:3, :]
        o_ref[10:11, M_HNW:M_HNW + HG_DIM] = pp_ref[0:1, 0:HG_DIM]
        r = lax.broadcasted_iota(jnp.int32, (ATW, 128), 0)
        c = lax.broadcasted_iota(jnp.int32, (ATW, 128), 1)
        fold = jnp.where((r % HEAD_DIM == c) & (c < HEAD_DIM), 1.0, 0.0).astype(BF16)
        qk = jnp.concatenate([pp_ref[1:2, :], pp_ref[2:3, :], jnp.zeros((6, ATW), F32)], axis=0)
        folded = _dot_exact_rhs01(qk, fold)
        o_ref[10:11, M_QNW:M_QNW + 128] = folded[0:1, :]
        o_ref[10:11, M_KNW:M_KNW + 128] = folded[1:2, :]
        o_ref[10:11, M_SINK:M_SINK + ATT_HEADS] = ds_ref[...]
        o_ref[10:11, M_LOSS:M_LOSS + 128] = ff_ref[4:5, 0:128]
        o_ref[11:13, 0:HGW] = dlb_ref[...]

    return pl.pallas_call(
        body, name="pack_small", in_specs=[_VMEM_WHOLE] * 6, out_specs=_VMEM_WHOLE,
        out_shape=jax.ShapeDtypeStruct((SMALL_ROWS, D), F32), compiler_params=_cp(),
    )(acc_in, acc_mg, acc_ffn, acc_prep, dsink, dlb)


def _sum_small(gathered):
    def body(g_ref, o_ref):
        tot = g_ref[0]
        for s in range(1, N_DEV):
            tot = tot + g_ref[s]
        o_ref[...] = tot
        o_ref[R_BADA01:R_BADA01 + 2, :] = tot[0:2, :] + tot[R_DCTX:R_DCTX + 2, :]

    return pl.pallas_call(
        body, name="sum_small", in_specs=[_VMEM_WHOLE], out_specs=_VMEM_WHOLE,
        out_shape=jax.ShapeDtypeStruct((SMALL_ROWS, D), F32), compiler_params=_cp(),
    )(gathered)


_REP_NAMES = ("b_ada", "c_ctx", "norm_mix_w", "norm_ffn_w", "hgrn_norm_w", "q_norm_w", "k_norm_w", "attn_sinks")


def _adamw_replicated(tot, g_c_ctx, ws, ms, vs):
    n = len(_REP_NAMES)

    def body(*refs):
        tot_ref, gc_ref = refs[0], refs[1]
        w_refs, m_refs, v_refs = refs[2:2 + n], refs[2 + n:2 + 2 * n], refs[2 + 2 * n:2 + 3 * n]
        outs = refs[2 + 3 * n:]
        row = lambda r: tot_ref[r:r + 1, :]
        misc = row(R_MISC)
        grads = [jnp.concatenate([row(R_BADA01), row(R_BADA01 + 1)] + [row(k) for k in range(2, 6)], axis=1),
                 gc_ref[...], row(R_NMIX), row(R_NFFN),
                 misc[:, M_HNW:M_HNW + HG_DIM], misc[:, M_QNW:M_QNW + HEAD_DIM],
                 misc[:, M_KNW:M_KNW + HEAD_DIM], misc[:, M_SINK:M_SINK + ATT_HEADS]]
        for k in range(n):
            outs[k][...] = grads[k]
            outs[n + k][...], outs[2 * n + k][...], outs[3 * n + k][...] = _adamw_math(
                w_refs[k][...], grads[k], m_refs[k][...], v_refs[k][...])

    shapes = [jax.ShapeDtypeStruct(w.shape, F32) for w in ws]
    return pl.pallas_call(
        body, name="adamw_replicated", in_specs=[_VMEM_WHOLE] * (2 + 3 * n), out_specs=[_VMEM_WHOLE] * (4 * n),
        out_shape=shapes * 4, compiler_params=_cp(),
    )(tot, g_c_ctx, *ws, *ms, *vs)


def _lb_grads(dlb, lbl):
    def body(d_ref, l_ref, o_ref):
        for d in (0, 1):
            ll = l_ref[d]
            lb = _sigmoid(ll[0:1, :] - ll[1:2, :])
            t = d_ref[d:d + 1, :] * lb * (1.0 - lb)
            o_ref[d, 0:1, :] = t
            o_ref[d, 1:2, :] = -t

    return pl.pallas_call(
        body, name="lb_grads", in_specs=[_VMEM_WHOLE] * 2, out_specs=_VMEM_WHOLE,
        out_shape=jax.ShapeDtypeStruct((2, 2, HGW), F32), compiler_params=_cp(),
    )(dlb, lbl)


def _c_ctx_grad(terms, c_ctx):
    def body(t_ref, c_ref, o_ref):
        tot = t_ref[0, 8:9, :]
        for s in range(1, N_DEV):
            tot = tot + t_ref[s, 8:9, :]
        cv = c_ref[...]
        sg = _sigmoid(cv)
        o_ref[...] = tot * (sg * (1.0 + cv * (1.0 - sg)))

    return pl.pallas_call(
        body, name="c_ctx_grad", in_specs=[_VMEM_WHOLE] * 2, out_specs=_VMEM_WHOLE,
        out_shape=jax.ShapeDtypeStruct((1, D), F32), compiler_params=_cp(),
    )(terms, c_ctx)


def _in_perm():
    fz, bz, inp, kk, vv, qhg, ghg, qat, gates = 0, 512, 1024, 1536, 1664, 1792, 2304, 2816, 3328
    cols = []
    for h in range(HG_HEADS):
        for base in (fz, bz, inp, qhg):
            cols += list(range(base + 128 * h, base + 128 * (h + 1)))
    cols += list(range(ghg, ghg + 512)) + list(range(qat, qat + 512))
    cols += list(range(kk, kk + 128)) + list(range(vv, vv + 128))
    cols += list(range(gates, gates + 2048))
    return np.asarray(cols, np.int32)


_PERM = _in_perm()
_INV_PERM = np.argsort(_PERM).astype(np.int32)


def _take_rows(w, perm):
    cuts = [0] + [i for i in range(1, len(perm)) if perm[i] != perm[i - 1] + 1] + [len(perm)]
    return jnp.concatenate([w[int(perm[a]):int(perm[b - 1]) + 1] for a, b in zip(cuts[:-1], cuts[1:])],
                           axis=0)


def _cols_from_blocks(g):
    return jnp.transpose(g, (1, 0, 2)).reshape(g.shape[1], N_DEV * g.shape[2])


def _local_step(x2, ctx2, tgt, lbl, sh_in, sc_in, gate1, sh2, sc2, gate2, norm_mix_w, norm_ffn_w,
                hgrn_norm_w, q_norm_w, k_norm_w, attn_sinks, w_a, w_b, w_c, s_bh, s_ba, s_out,
                s_gate, s_up, s_down):
    first_last = lambda n: [(0, True), (n - 1, False)]
    h_all = _norm_mod_all(ctx2, x2, norm_mix_w, sh_in, sc_in)
    p_a = _mm_nt(h_all, w_a, tm=768, tn=1024, out_dtype=F32, name="proj_a")
    (o, st), (g_bh, g_ba, g_out) = _hgrn_fwd(
        p_a, lbl, (_gather_comm([s_bh, s_ba, s_out]), [(0, True), (HG_HEADS - 2, True), (HG_HEADS - 1, False)]))
    (p_b, p_c), (g_gate,) = _proj_bc(
        h_all, w_b, w_c, (_gather_comm([s_gate]), [(0, True), (N_TILES - 2, True), (N_TILES - 1, False)]))
    cos, sin = _rope_tables()
    qnw_t, knw_t = jnp.tile(q_norm_w, (1, ATT_HEADS)), jnp.tile(k_norm_w, (1, KV_HEADS))
    y_hg, qn, k_rep, v_rep = _prep_fwd(p_b, o, cos, sin, hgrn_norm_w, qnw_t, knw_t)
    (y_at, lse), (g_up, g_down) = _attn_fwd(
        qn, k_rep, v_rep, attn_sinks,
        (_gather_comm([s_up, s_down]), [(0, True), (N_BLOCKS - 3, True), (N_BLOCKS - 1, False)]))
    w_bh, w_ba, w_o = g_bh.reshape(D, HGW), g_ba.reshape(D, ATW), g_out.reshape(D, D)
    g_gate, g_up, g_down = [g.reshape(N_FF_TILES, FF_TILE, D) for g in (g_gate, g_up, g_down)]
    a, b, mixed, r, x1, h2 = _merge_fwd(y_hg, y_at, p_c, x2, w_bh, w_ba, w_o, gate1, norm_ffn_w, sh2, sc2)

    act, d_gate, d_up, d_f, dx1, acc_ffn = _ffn_fused(x1, h2, tgt, g_gate, g_up, g_down, gate2,
                                                      norm_ffn_w, sc2)
    by_chip = lambda t: t.reshape((N_CHIPS, 2) + t.shape[1:])
    ff_by_chip = lambda t: t.reshape(N_CHIPS, 2, FF_BLK, D)
    t_down, _ = _mm_tn_blocked(act, d_f, "grad_down")
    t_down = ff_by_chip(t_down)
    t_gate, (f_down,) = _mm_tn_blocked(d_gate, h2, "grad_gate", (_sibling_comm([t_down]), first_last(N_FF_TILES)))
    t_gate = ff_by_chip(t_gate)
    t_up, (f_gate,) = _mm_tn_blocked(d_up, h2, "grad_up", (_sibling_comm([t_gate]), first_last(N_FF_TILES)))
    t_up = ff_by_chip(t_up)

    (d_r, d_a, d_b, dp_c, dy_hg, dy_at, acc_mg), (f_up,) = _merge_bwd(
        dx1, r, a, b, p_c, w_bh, w_ba, w_o, gate1, (_sibling_comm([t_up]), first_last(N_LAT_TILES)))
    c_down, c_gate, c_up = [_pair_sum(t, f, "pair_sum_" + nm) for t, f, nm in
                            ((t_down, f_down, "down"), (t_gate, f_gate, "gate"), (t_up, f_up, "up"))]
    t_out = _mm_tn(mixed, d_r, tk=512, nk=4, tm=512, tn=1024, out_dtype=BF16, name="grad_out")
    t_bh = _mm_tn(d_a, y_hg, tk=512, nk=4, tm=512, tn=512, out_dtype=BF16, name="grad_bh")
    t_ba = _mm_tn(d_b, y_at, tk=512, nk=4, tm=512, tn=512, out_dtype=BF16, name="grad_ba")
    t_bh, t_ba, t_out = [by_chip(t.reshape(N_DEV, D // N_DEV, t.shape[1])) for t in (t_bh, t_ba, t_out)]
    (dq, dk_rep, dv_rep, dsink), (r_down, r_gate) = _attn_bwd(
        qn, k_rep, v_rep, attn_sinks, y_at, lse, dy_at, (_chip_comm([c_down, c_gate]), first_last(N_BLOCKS)))
    (dp_b, d_o, acc_prep), (f_bh, f_ba, f_out) = _prep_bwd(
        p_b, o, cos, sin, hgrn_norm_w, qnw_t, knw_t, dy_hg, dq, dk_rep, dv_rep,
        (_sibling_comm([t_bh, t_ba, t_out]), first_last(N_TILES)))
    c_bh, c_ba, c_out = [_pair_sum(t, f, "pair_sum_" + nm) for t, f, nm in
                         ((t_bh, f_bh, "bh"), (t_ba, f_ba, "ba"), (t_out, f_out, "out"))]
    (dp_a, dlb), (r_bh, r_ba, r_out, r_up) = _hgrn_bwd(
        p_a, lbl, d_o, st, (_chip_comm([c_bh, c_ba, c_out, c_up]), first_last(HG_HEADS)))
    t_a = _mm_tn(dp_a, h_all, tk=768, nk=3, tm=1024, tn=1024, out_dtype=BF16, name="grad_in_a")
    t_b = _mm_tn(dp_b, h_all, tk=768, nk=3, tm=640, tn=1024, out_dtype=BF16, name="grad_in_b")
    t_c = _mm_tn(dp_c, h_all, tk=256, nk=8, b_off=1, tm=1024, tn=1024, out_dtype=BF16, name="grad_in_c")
    t_in = by_chip(_take_rows(jnp.concatenate([t_a, t_b, t_c], axis=0), _INV_PERM).reshape(N_DEV, IN_BLK, D))
    (f_in,) = _run_comm(_sibling_comm([t_in]), "scatter_in_sibling")
    (grad_x, acc_in), (r_in,) = _input_bwd(
        dp_a, dp_b, dp_c, w_a, w_b, w_c, ctx2, x2, dx1, norm_mix_w, sh_in, sc_in,
        (_chip_comm([_pair_sum(t_in, f_in, "pair_sum_in")]), first_last(N_TILES)))
    small = _pack_small(acc_in, acc_mg, acc_ffn, acc_prep, dsink, dlb)
    return grad_x, small, [r_in, r_bh, r_ba, r_out, r_gate, r_up, r_down]


def kernel(x, c, ctx, c_ctx, w_ada, b_ada, norm_mix_w, norm_ffn_w, w_in, hgrn_lb_logits, hgrn_norm_w, q_norm_w, k_norm_w, attn_sinks, w_branch_hgrn, w_branch_attn, w_out, w_ffn_gate, w_ffn_up, w_ffn_down, loss_target, m_c_ctx, m_w_ada, m_b_ada, m_norm_mix_w, m_norm_ffn_w, m_w_in, m_hgrn_lb_logits, m_hgrn_norm_w, m_q_norm_w, m_k_norm_w, m_attn_sinks, m_w_branch_hgrn, m_w_branch_attn, m_w_out, m_w_ffn_gate, m_w_ffn_up, m_w_ffn_down, v_c_ctx, v_w_ada, v_b_ada, v_norm_mix_w, v_norm_ffn_w, v_w_in, v_hgrn_lb_logits, v_hgrn_norm_w, v_q_norm_w, v_k_norm_w, v_attn_sinks, v_w_branch_hgrn, v_w_branch_attn, v_w_out, v_w_ffn_gate, v_w_ffn_up, v_w_ffn_down):
    me = 4 * lax.axis_index("x") + 2 * lax.axis_index("y") + lax.axis_index("c")
    x2, ctx2, tgt = x[0], ctx[0], loss_target[0]
    w_ada2, w_in2 = w_ada[0], w_in[0]

    blk = jnp.zeros((8, D), F32).at[0].set(c[0]).at[1, :256].set(hgrn_lb_logits.reshape(256))
    (g0,) = _all_gather([blk], "gather_cond", True)
    cc = jnp.zeros((16, D), F32).at[:8].set(g0[:, 0, :]).at[8].set(c_ctx)
    lbl = jnp.transpose(g0[:, 1, :256].reshape(N_DEV, 2, 2, 64), (1, 2, 0, 3)).reshape(2, 2, HGW)

    b_cols = lax.dynamic_slice(b_ada, (0, me * ADA_BLK), (1, ADA_BLK))
    (g1,) = _all_gather([_ada_rows(cc, w_ada2, b_cols)], "gather_mod", True)
    mod_all = _cols_from_blocks(g1)
    mod = lax.dynamic_slice(mod_all, (me, 0), (1, 6 * D)).reshape(6, D)
    mod_c = mod_all[8].reshape(6, D)
    sh1, sc1, gate1, sh2, sc2, gate2 = [mod[k:k + 1] for k in range(6)]
    sh_in = jnp.concatenate([mod_c[0:1], sh1], axis=0)
    sc_in = jnp.concatenate([mod_c[1:2], sc1], axis=0)

    shards = [w_branch_hgrn[0].T, w_branch_attn[0].T, w_out[0], w_ffn_gate[0].T, w_ffn_up[0].T, w_ffn_down[0]]
    (g_in,) = _all_gather([w_in2.T.astype(BF16)], "gather_w_in", False)
    w_in_t = _take_rows(g_in.reshape(IN_COLS, D), _PERM)
    w_a, w_b, w_c = w_in_t[:WA], w_in_t[WA:WA + WB], w_in_t[WA + WB:]

    grad_x, small, (r_in, r_bh, r_ba, r_out, r_gate, r_up, r_down) = _local_step(
        x2, ctx2, tgt, lbl, sh_in, sc_in, gate1, sh2, sc2, gate2, norm_mix_w, norm_ffn_w, hgrn_norm_w,
        q_norm_w, k_norm_w, attn_sinks, w_a, w_b, w_c, *[s.astype(BF16) for s in shards])

    big = {}
    for nm, rr, ww, mm, vv, tr, transposed in (
            ("w_in", r_in, w_in2, m_w_in[0], v_w_in[0], 336, True),
            ("w_branch_hgrn", r_bh, w_branch_hgrn[0], m_w_branch_hgrn[0], v_w_branch_hgrn[0], 128, True),
            ("w_branch_attn", r_ba, w_branch_attn[0], m_w_branch_attn[0], v_w_branch_attn[0], 128, True),
            ("w_out", r_out, w_out[0], m_w_out[0], v_w_out[0], 128, False),
            ("w_ffn_gate", r_gate, w_ffn_gate[0], m_w_ffn_gate[0], v_w_ffn_gate[0], 352, True),
            ("w_ffn_up", r_up, w_ffn_up[0], m_w_ffn_up[0], v_w_ffn_up[0], 352, True),
            ("w_ffn_down", r_down, w_ffn_down[0], m_w_ffn_down[0], v_w_ffn_down[0], 352, False)):
        if transposed:
            res = _adamw_sharded(rr, ww.T, mm.T, vv.T, "adamw_" + nm, tr)
            big[nm] = [t.T[None] for t in res]
        else:
            big[nm] = [t[None] for t in _adamw_sharded(rr, ww, mm, vv, "adamw_" + nm, tr)]

    (g2,) = _all_gather([small], "gather_small", True)
    tot = _sum_small(g2)
    dm = jnp.zeros((16, 6 * D), F32).at[:8].set(g2[:, R_DMOD:R_DMOD + 6, :].reshape(N_DEV, 6 * D))
    dm = dm.at[8, :2 * D].set(tot[R_DCTX:R_DCTX + 2].reshape(2 * D))
    dm_cols = lax.dynamic_slice(dm, (0, me * ADA_BLK), (16, ADA_BLK))
    g_w_ada, dsc_term = _ada_grads(cc, dm_cols, w_ada2)
    (g3,) = _all_gather([dsc_term], "gather_cctx", True)
    g_c_ctx = _c_ctx_grad(g3, c_ctx.reshape(1, D))
    g_lbl = _lb_grads(tot[R_DLB:R_DLB + 2, :HGW], lbl)
    g_lb_mine = lax.dynamic_slice(g_lbl, (0, 0, me * 64), (2, 2, 64))
    misc = tot[R_MISC]
    loss = misc[M_LOSS]

    rep_out = _adamw_replicated(
        tot, g_c_ctx,
        [b_ada, c_ctx.reshape(1, D), norm_mix_w, norm_ffn_w, hgrn_norm_w, q_norm_w, k_norm_w, attn_sinks],
        [m_b_ada, m_c_ctx.reshape(1, D), m_norm_mix_w, m_norm_ffn_w, m_hgrn_norm_w, m_q_norm_w, m_k_norm_w,
         m_attn_sinks],
        [v_b_ada, v_c_ctx.reshape(1, D), v_norm_mix_w, v_norm_ffn_w, v_hgrn_norm_w, v_q_norm_w, v_k_norm_w,
         v_attn_sinks])
    rep = []
    for kind in range(4):
        vals = dict(zip(_REP_NAMES, rep_out[kind * len(_REP_NAMES):(kind + 1) * len(_REP_NAMES)]))
        vals["c_ctx"] = vals["c_ctx"].reshape(D)
        rep.append(vals)

    d_ada, nm_ada, nv_ada = _adamw_plain(g_w_ada, w_ada2, m_w_ada[0], v_w_ada[0], "adamw_w_ada")
    ada = [t[None] for t in (g_w_ada, d_ada, nm_ada, nv_ada)]
    lb_w = hgrn_lb_logits.reshape(4, 64)
    d_lb, nm_lb, nv_lb = _adamw_plain(g_lb_mine.reshape(4, 64), lb_w, m_hgrn_lb_logits.reshape(4, 64),
                                      v_hgrn_lb_logits.reshape(4, 64), "adamw_lb")
    lbs = [t.reshape(2, 2, 64) for t in (g_lb_mine, d_lb, nm_lb, nv_lb)]

    names = ['c_ctx', 'w_ada', 'b_ada', 'norm_mix_w', 'norm_ffn_w', 'w_in', 'hgrn_lb_logits', 'hgrn_norm_w',
             'q_norm_w', 'k_norm_w', 'attn_sinks', 'w_branch_hgrn', 'w_branch_attn', 'w_out', 'w_ffn_gate',
             'w_ffn_up', 'w_ffn_down']
    outs = [loss, grad_x[None]]
    for kind in range(4):
        for nm in names:
            if nm == 'w_ada':
                outs.append(ada[kind])
            elif nm == 'hgrn_lb_logits':
                outs.append(lbs[kind])
            elif nm in big:
                outs.append(big[nm][kind])
            else:
                outs.append(rep[kind][nm])
    return tuple(outs)
```

```python
import functools
import math

import numpy as np
import jax
import jax.numpy as jnp
from jax import lax
from jax.experimental import pallas as pl
from jax.experimental.pallas import tpu as pltpu

F32 = jnp.float32
BF16 = jnp.bfloat16

N_DEV = 8
D = 1024
S = 2048
L = 256
T = L + S
TM = 256
N_TILES = T // TM
N_LAT_TILES = S // TM
HG_HEADS = 4
HG_DIM = 128
HGW = 512
CHUNK = 32
N_CHUNKS = T // CHUNK
N_CTX_CHUNKS = L // CHUNK
N_LAT_CHUNKS = S // CHUNK
ATT_HEADS = 8
KV_HEADS = 2
HEAD_DIM = 64
ATW = 512
KVW = 128
BLOCK = 128
N_BLOCKS = S // BLOCK
GRID_W = 64
ROPE_THETA = 10000.0
D_FF = 2816
FF_BLK = D_FF // N_DEV
FF_TILE = 256
N_FF_TILES = D_FF // FF_TILE
IN_COLS = 5376
IN_BLK = IN_COLS // N_DEV
ADA_BLK = 6 * D // N_DEV
EPS = 1e-6
WA, WB, WC = 2048, 1280, 2048

ADAM_LR = 0.001
ADAM_B1 = 0.9
ADAM_B2 = 0.999
ADAM_EPS = 1e-08
ADAM_WD = 0.01
ADAM_STEP = 10

VMEM_LIMIT = 56 * 1024 * 1024
MESH = pl.DeviceIdType.MESH


def _cp(sem=None, vmem=VMEM_LIMIT):
    return pltpu.CompilerParams(dimension_semantics=sem, vmem_limit_bytes=vmem)


def _full(shape):
    n = len(shape)
    return pl.BlockSpec(shape, lambda *_: (0,) * n)


_VMEM_WHOLE = pl.BlockSpec(memory_space=pltpu.VMEM)
_ANY = pl.BlockSpec(memory_space=pl.ANY)


def _sigmoid(v):
    return 1.0 / (1.0 + jnp.exp(-v))


def _dot(a, b):
    return jnp.dot(a, b, preferred_element_type=F32)


def _dot_nt(a, b):
    return lax.dot_general(a, b, (((1,), (1,)), ((), ())), preferred_element_type=F32)


def _dot_tn(a, b):
    return lax.dot_general(a, b, (((0,), (0,)), ((), ())), preferred_element_type=F32)


def _split3(v):
    hi = v.astype(BF16)
    r = v - hi.astype(F32)
    mid = r.astype(BF16)
    lo = (r - mid.astype(F32)).astype(BF16)
    return hi, mid, lo


def _dot_exact_rhs01(v, m01):
    hi, mid, lo = _split3(v)
    return _dot(hi, m01) + _dot(mid, m01) + _dot(lo, m01)


def _dot_exact_lhs01(m01, v):
    hi, mid, lo = _split3(v)
    return _dot(m01, hi) + _dot(m01, mid) + _dot(m01, lo)


def _dot_f32(a, b, dot=_dot):
    ah, am, al = _split3(a)
    bh, bm, bl = _split3(b)
    return (dot(ah, bh) + (dot(ah, bm) + dot(am, bh))
            + (dot(am, bm) + dot(ah, bl) + dot(al, bh)))


def _my_pos():
    return lax.axis_index("x"), lax.axis_index("y"), lax.axis_index("c")


class _Comm:
    def __init__(self, operands, out_shapes, sems, phases):
        self.operands, self.out_shapes, self.sems, self.phases = operands, out_shapes, sems, phases


def _gather_comm(blocks):
    n = len(blocks)

    def parts(ins, outs, sems):
        send_sems, recv_sems, local_sems = sems
        x, y, c = _my_pos()
        me, sibling = (x, y, c), (x, y, 1 - c)
        chips = [(1 - x, y), (x, 1 - y), (1 - x, 1 - y)]

        def slot(a, px, py, pc):
            return outs[a].at[4 * px + 2 * py + pc]

        def copy(a, k, block, to, src=None):
            return pltpu.make_async_remote_copy(
                src_ref=slot(a, *block) if src is None else src, dst_ref=slot(a, *block),
                send_sem=send_sems.at[a, k], recv_sem=recv_sems.at[a, k],
                device_id=to, device_id_type=MESH)

        mine = [pltpu.make_async_copy(ins[a], slot(a, *me), local_sems.at[a]) for a in range(n)]
        first = []
        for a in range(n):
            first.append(copy(a, 0, me, sibling, src=ins[a]))
            first += [copy(a, 1 + j, me, (*chip, c), src=ins[a]) for j, chip in enumerate(chips)]
        passed = [copy(a, 4 + j, (*chip, c), sibling) for j, chip in enumerate(chips) for a in range(n)]
        return c, me, sibling, chips, copy, mine, first, passed

    def start(ins, outs, sems):
        _, _, _, _, _, mine, first, _ = parts(ins, outs, sems)
        for cp in mine + first:
            cp.start()

    def forward(ins, outs, sems):
        c, me, _, chips, copy, _, _, passed = parts(ins, outs, sems)
        for j, chip in enumerate(chips):
            for a in range(n):
                copy(a, 1 + j, (*chip, c), me).wait_recv()
                passed[j * n + a].start()

    def finish(ins, outs, sems):
        c, me, sibling, chips, copy, mine, first, passed = parts(ins, outs, sems)
        for a in range(n):
            copy(a, 0, sibling, me).wait_recv()
            for j, chip in enumerate(chips):
                copy(a, 4 + j, (*chip, 1 - c), me).wait_recv()
        for cp in first + passed:
            cp.wait_send()
        for cp in mine:
            cp.wait()

    return _Comm(blocks, [jax.ShapeDtypeStruct((N_DEV,) + b.shape, b.dtype) for b in blocks],
                 [pltpu.SemaphoreType.DMA((n, 7)), pltpu.SemaphoreType.DMA((n, 7)), pltpu.SemaphoreType.DMA((n,))],
                 [start, forward, finish])


def _run_comm(comm, name, in_vmem=False):
    n_in, n_out = len(comm.operands), len(comm.out_shapes)

    def body(*refs):
        ins, outs, sems = refs[:n_in], refs[n_in:n_in + n_out], refs[n_in + n_out:]
        for phase in comm.phases:
            phase(ins, outs, sems)

    spec = _VMEM_WHOLE if in_vmem else _ANY
    return pl.pallas_call(
        body, name=name, out_shape=comm.out_shapes, in_specs=[spec] * n_in, out_specs=[spec] * n_out,
        scratch_shapes=comm.sems,
    )(*comm.operands)


def _carrier_call(body, comm, schedule, *, name, grid, in_specs, out_specs, out_shape, scratch_shapes, operands):
    n_in, n_out, n_scr = len(in_specs), len(out_specs), len(scratch_shapes)
    c_in, c_out = len(comm.operands), len(comm.out_shapes)

    def full_body(*refs):
        ins, refs = refs[:n_in], refs[n_in:]
        cins, refs = refs[:c_in], refs[c_in:]
        outs, refs = refs[:n_out], refs[n_out:]
        couts, refs = refs[:c_out], refs[c_out:]
        scr, csems = refs[:n_scr], refs[n_scr:]
        step = pl.program_id(0)

        def run(before):
            for (at, when_before), phase in zip(schedule, comm.phases):
                if when_before == before:
                    pl.when(step == at)(functools.partial(phase, cins, couts, csems))

        run(True)
        body(*ins, *outs, *scr)
        run(False)

    res = pl.pallas_call(
        full_body, name=name, grid=grid,
        in_specs=list(in_specs) + [_ANY] * c_in, out_specs=list(out_specs) + [_ANY] * c_out,
        out_shape=list(out_shape) + list(comm.out_shapes),
        scratch_shapes=list(scratch_shapes) + list(comm.sems),
        compiler_params=_cp(("arbitrary",)),
    )(*operands, *comm.operands)
    return res[:n_out], res[n_out:]


def _pcall(body, carried, *, name, grid, in_specs, out_specs, out_shape, scratch_shapes, operands):
    if carried is None:
        res = pl.pallas_call(body, name=name, grid=grid, in_specs=in_specs, out_specs=out_specs,
                             out_shape=out_shape, scratch_shapes=scratch_shapes,
                             compiler_params=_cp(("arbitrary",)))(*operands)
        return res, ()
    return _carrier_call(body, carried[0], carried[1], name=name, grid=grid, in_specs=in_specs,
                         out_specs=out_specs, out_shape=out_shape, scratch_shapes=scratch_shapes,
                         operands=operands)


def _all_gather(blocks, name, in_vmem):
    return _run_comm(_gather_comm(blocks), name, in_vmem)


N_CHIPS = 4


def _sibling_comm(contribs):
    n = len(contribs)

    def copies(ins, outs, sems):
        send_sems, recv_sems = sems
        x, y, c = _my_pos()
        return [pltpu.make_async_remote_copy(
            src_ref=ins[a].at[pl.ds(0, N_CHIPS), 1 - c], dst_ref=outs[a],
            send_sem=send_sems.at[a], recv_sem=recv_sems.at[a],
            device_id=(x, y, 1 - c), device_id_type=MESH) for a in range(n)]

    def start(ins, outs, sems):
        for cp in copies(ins, outs, sems):
            cp.start()

    def finish(ins, outs, sems):
        cps = copies(ins, outs, sems)
        for cp in cps:
            cp.wait_recv()
        for cp in cps:
            cp.wait_send()

    return _Comm(contribs, [jax.ShapeDtypeStruct((N_CHIPS,) + b.shape[2:], b.dtype) for b in contribs],
                 [pltpu.SemaphoreType.DMA((n,)), pltpu.SemaphoreType.DMA((n,))], [start, finish])


def _pair_sum(mine, theirs, name):
    _, _, rows, cols = mine.shape

    def body(m_ref, t_ref, o_ref):
        c = lax.axis_index("c")
        o_ref[...] = (m_ref[c].astype(F32) + t_ref[...].astype(F32)).astype(BF16)

    return pl.pallas_call(
        body, name=name, grid=(N_CHIPS,),
        in_specs=[pl.BlockSpec((None, 2, rows, cols), lambda q: (q, 0, 0, 0)),
                  pl.BlockSpec((None, rows, cols), lambda q: (q, 0, 0))],
        out_specs=pl.BlockSpec((None, rows, cols), lambda q: (q, 0, 0)),
        out_shape=jax.ShapeDtypeStruct((N_CHIPS, rows, cols), BF16),
        compiler_params=_cp(("parallel",)),
    )(mine, theirs)


def _chip_comm(sums):
    n = len(sums)

    def parts(ins, outs, sems):
        send_sems, recv_sems, local_sems = sems
        x, y, c = _my_pos()
        q_me = 2 * x + y
        chips = [(1 - x, y), (x, 1 - y), (1 - x, 1 - y)]
        mine = [pltpu.make_async_copy(ins[a].at[q_me], outs[a].at[q_me], local_sems.at[a]) for a in range(n)]
        sends, recvs = [], []
        for j, (px, py) in enumerate(chips):
            for a in range(n):
                q = 2 * px + py
                sends.append(pltpu.make_async_remote_copy(
                    src_ref=ins[a].at[q], dst_ref=outs[a].at[q_me],
                    send_sem=send_sems.at[a, j], recv_sem=recv_sems.at[a, j],
                    device_id=(px, py, c), device_id_type=MESH))
                recvs.append(pltpu.make_async_remote_copy(
                    src_ref=ins[a].at[q], dst_ref=outs[a].at[q],
                    send_sem=send_sems.at[a, j], recv_sem=recv_sems.at[a, j],
                    device_id=(x, y, c), device_id_type=MESH))
        return mine, sends, recvs

    def start(ins, outs, sems):
        mine, sends, _ = parts(ins, outs, sems)
        for cp in mine + sends:
            cp.start()

    def finish(ins, outs, sems):
        mine, sends, recvs = parts(ins, outs, sems)
        for cp in recvs:
            cp.wait_recv()
        for cp in sends:
            cp.wait_send()
        for cp in mine:
            cp.wait()

    return _Comm(sums, [jax.ShapeDtypeStruct(b.shape, b.dtype) for b in sums],
                 [pltpu.SemaphoreType.DMA((n, 3)), pltpu.SemaphoreType.DMA((n, 3)), pltpu.SemaphoreType.DMA((n,))],
                 [start, finish])


def _mm_nt(a, bt, *, tm, tn, out_dtype, name, row_off=0, rows=None):
    rows = a.shape[0] if rows is None else rows
    n, k = bt.shape

    def body(a_ref, b_ref, o_ref):
        o_ref[...] = _dot_nt(a_ref[...], b_ref[...]).astype(out_dtype)

    return pl.pallas_call(
        body, name=name, grid=(rows // tm, n // tn),
        in_specs=[pl.BlockSpec((tm, k), lambda i, j: (i + row_off, 0)),
                  pl.BlockSpec((tn, k), lambda i, j: (j, 0))],
        out_specs=pl.BlockSpec((tm, tn), lambda i, j: (i, j)),
        out_shape=jax.ShapeDtypeStruct((rows, n), out_dtype),
        compiler_params=_cp(("parallel", "parallel")),
    )(a, bt)


def _mm_tn(a, b, *, tk, nk, tm, tn, out_dtype, name, a_off=0, b_off=0):
    m, n = a.shape[1], b.shape[1]

    def body(a_ref, b_ref, o_ref, acc):
        kk = pl.program_id(2)

        @pl.when(kk == 0)
        def _():
            acc[...] = jnp.zeros_like(acc)

        acc[...] += _dot_tn(a_ref[...], b_ref[...])

        @pl.when(kk == nk - 1)
        def _():
            o_ref[...] = acc[...].astype(out_dtype)

    return pl.pallas_call(
        body, name=name, grid=(m // tm, n // tn, nk),
        in_specs=[pl.BlockSpec((tk, tm), lambda i, j, kk: (kk + a_off, i)),
                  pl.BlockSpec((tk, tn), lambda i, j, kk: (kk + b_off, j))],
        out_specs=pl.BlockSpec((tm, tn), lambda i, j, kk: (i, j)),
        out_shape=jax.ShapeDtypeStruct((m, n), out_dtype),
        scratch_shapes=[pltpu.VMEM((tm, tn), F32)],
        compiler_params=_cp(("parallel", "parallel", "arbitrary")),
    )(a, b)


def _mm_tn_blocked(a, b, name, carried=None):
    nb, _, w = a.shape
    n = b.shape[1]

    def body(a_ref, b_ref, o_ref):
        o_ref[...] = _dot_tn(a_ref[...], b_ref[...]).astype(BF16)

    (out,), extra = _pcall(
        body, carried, name=name, grid=(nb,),
        in_specs=[pl.BlockSpec((None, S, w), lambda j: (j, 0, 0)), _full((S, n))],
        out_specs=[pl.BlockSpec((None, w, n), lambda j: (j, 0, 0))],
        out_shape=[jax.ShapeDtypeStruct((nb, w, n), BF16)],
        scratch_shapes=[], operands=[a, b])
    return out, extra


def _ada_rows(cc, w_ada, b_cols):
    def body(c_ref, w_ref, b_ref, o_ref):
        cv = c_ref[...]
        o_ref[...] = _dot_f32(cv * _sigmoid(cv), w_ref[...]) + b_ref[...]

    return pl.pallas_call(
        body, name="ada_rows",
        in_specs=[_VMEM_WHOLE] * 3, out_specs=_VMEM_WHOLE,
        out_shape=jax.ShapeDtypeStruct((16, ADA_BLK), F32),
        compiler_params=_cp(),
    )(cc, w_ada, b_cols)


def _ada_grads(cc, dm_cols, w_ada):
    def body(c_ref, dm_ref, w_ref, gw_ref, dsc_ref):
        cv = c_ref[...]
        sc = cv * _sigmoid(cv)
        dm = dm_ref[...]
        gw_ref[...] = _dot_f32(sc, dm, dot=_dot_tn)
        dsc_ref[...] = _dot_f32(dm, w_ref[...], dot=_dot_nt)

    return pl.pallas_call(
        body, name="ada_grads",
        in_specs=[_VMEM_WHOLE] * 3, out_specs=[_VMEM_WHOLE] * 2,
        out_shape=[jax.ShapeDtypeStruct((D, ADA_BLK), F32), jax.ShapeDtypeStruct((16, D), F32)],
        compiler_params=_cp(),
    )(cc, dm_cols, w_ada)


def _lat(i):
    return jnp.maximum(i - 1, 0)


def _rms_mod(xv, nw, sh, sc):
    rstd = lax.rsqrt(jnp.mean(xv * xv, axis=-1, keepdims=True) + EPS)
    return (xv * rstd * nw) * (1.0 + sc) + sh


def _rms_mod_bwd(xv, nw, sc, dh):
    rstd = lax.rsqrt(jnp.mean(xv * xv, axis=-1, keepdims=True) + EPS)
    xhat = xv * rstd
    dn = dh * (1.0 + sc)
    dxhat = dn * nw
    dx = rstd * (dxhat - xhat * jnp.mean(dxhat * xhat, axis=-1, keepdims=True))
    return (dx, jnp.sum(dh, axis=0, keepdims=True), jnp.sum(dh * (xhat * nw), axis=0, keepdims=True),
            jnp.sum(dn * xhat, axis=0, keepdims=True))


def _norm_mod_all(ctx, x, nw, sh, sc):
    def body(ctx_ref, x_ref, nw_ref, sh_ref, sc_ref, o_ref):
        i = pl.program_id(0)
        sel = jnp.minimum(i, 1)
        xv = jnp.where(i == 0, ctx_ref[...], x_ref[...])
        o_ref[...] = _rms_mod(xv, nw_ref[...], sh_ref[pl.ds(sel, 1), :], sc_ref[pl.ds(sel, 1), :]).astype(BF16)

    return pl.pallas_call(
        body, name="norm_mod", grid=(N_TILES,),
        in_specs=[_full((TM, D)), pl.BlockSpec((TM, D), lambda i: (_lat(i), 0)),
                  _full((1, D)), _full((2, D)), _full((2, D))],
        out_specs=pl.BlockSpec((TM, D), lambda i: (i, 0)),
        out_shape=jax.ShapeDtypeStruct((T, D), BF16),
        compiler_params=_cp(("parallel",)),
    )(ctx, x, nw, sh, sc)


def _chunk_masks(reverse):
    row = lax.broadcasted_iota(jnp.int32, (TM, TM), 0)
    col = lax.broadcasted_iota(jnp.int32, (TM, TM), 1)
    same = (row // CHUNK) == (col // CHUNK)
    tri = same & ((col >= row) if reverse else (col <= row))
    return same, tri


def _chunk_order(i, reverse):
    if not reverse:
        return i
    return jnp.where(i < N_CTX_CHUNKS, N_CTX_CHUNKS - 1 - i, N_CHUNKS + N_CTX_CHUNKS - 1 - i)


def _decay_terms(z, lb, same01, tri01):
    f = lb + (1.0 - lb) * _sigmoid(z)
    g = jnp.log(f)
    hi, mid, lo = _split3(g)
    g3 = jnp.concatenate([hi, mid, lo], axis=1)
    b3 = _dot(tri01, g3)
    t3 = _dot(same01, g3)
    b = b3[:, :HG_DIM] + b3[:, HG_DIM:2 * HG_DIM] + b3[:, 2 * HG_DIM:]
    bt = t3[:, :HG_DIM] + t3[:, HG_DIM:2 * HG_DIM] + t3[:, 2 * HG_DIM:]
    return f, 1.0 - f, b, bt


def _chunk_outer(a, b):
    n = TM // CHUNK
    return jnp.einsum('ncv,nck->nvk', a.reshape(n, CHUNK, HG_DIM), b.reshape(n, CHUNK, HG_DIM),
                      preferred_element_type=F32)


def _hgrn_fwd(p_a, lbl, carried=None):
    cpt = TM // CHUNK

    def body(p_ref, lbl_ref, o_ref, st_ref, qd_s, kd_s, u_s, v_s, ebt_s):
        masks = [_chunk_masks(d == 1) for d in (0, 1)]
        same01 = jnp.where(masks[0][0], 1.0, 0.0).astype(BF16)
        tri = [m[1] for m in masks]
        tri01 = [jnp.where(t, 1.0, 0.0).astype(BF16) for t in tri]
        lb = [_sigmoid(lbl_ref[d][0:1, :] - lbl_ref[d][1:2, :]) for d in (0, 1)]

        def prep(r, carry):
            r0 = pl.multiple_of(r * TM, TM)
            vb = p_ref[pl.ds(r0, TM), 2 * HG_DIM:3 * HG_DIM].astype(BF16)
            v_s[pl.ds(r0, TM), :] = vb
            for d in (0, 1):
                z = p_ref[pl.ds(r0, TM), d * HG_DIM:(d + 1) * HG_DIM]
                _, k, b, bt = _decay_terms(z, lb[d], same01, tri01[d])
                u_s[d, pl.ds(r * cpt, cpt)] = _chunk_outer(vb, (k * jnp.exp(bt - b)).astype(BF16))
                ebt_s[d, pl.ds(r0, TM), :] = jnp.exp(bt)

                @pl.when(r >= 1)
                def _():
                    rl = pl.multiple_of(r0 - L, TM)
                    qr = p_ref[pl.ds(r0, TM), 3 * HG_DIM:4 * HG_DIM]
                    q = qr * _sigmoid(qr) * HG_DIM ** -0.5
                    qd_s[d, pl.ds(rl, TM), :] = (q * jnp.exp(b)).astype(BF16)
                    kd_s[d, pl.ds(rl, TM), :] = (k * jnp.exp(-b)).astype(BF16)

            return carry

        lax.fori_loop(0, N_TILES, prep, 0)

        def scan(i, sts):
            new = []
            for d in (0, 1):
                nn = _chunk_order(i, d == 1)
                c0 = pl.multiple_of(nn * CHUNK, CHUNK)
                st_ref[d, nn] = sts[d].astype(BF16)
                new.append(sts[d] * ebt_s[d, pl.ds(c0, 1), :] + u_s[d, nn])
            return tuple(new)

        zero = jnp.zeros((HG_DIM, HG_DIM), F32)
        lax.fori_loop(0, N_CHUNKS, scan, (zero, zero))

        def outp(r, carry):
            r0 = pl.multiple_of(r * TM, TM)
            vb = v_s[pl.ds(r0 + L, TM), :]
            o = jnp.zeros((TM, HG_DIM), F32)
            for d in (0, 1):
                qd = qd_s[d, pl.ds(r0, TM), :]
                a = jnp.where(tri[d], _dot_nt(qd, kd_s[d, pl.ds(r0, TM), :]), 0.0)
                stb = st_ref[d, pl.ds(N_CTX_CHUNKS + r * cpt, cpt)]
                inter = jnp.einsum('nck,nvk->ncv', qd.reshape(cpt, CHUNK, HG_DIM), stb,
                                   preferred_element_type=F32)
                o = o + _dot(a.astype(BF16), vb) + inter.reshape(TM, HG_DIM)
            o_ref[pl.ds(r0, TM), :] = o
            return carry

        lax.fori_loop(0, N_LAT_TILES, outp, 0)

    return _pcall(
        body, carried, name="hgrn_fwd", grid=(HG_HEADS,),
        in_specs=[pl.BlockSpec((T, 4 * HG_DIM), lambda h: (0, h)),
                  pl.BlockSpec((2, 2, HG_DIM), lambda h: (0, 0, h))],
        out_specs=[pl.BlockSpec((S, HG_DIM), lambda h: (0, h)),
                   pl.BlockSpec((2, None, N_CHUNKS, HG_DIM, HG_DIM), lambda h: (0, h, 0, 0, 0))],
        out_shape=[jax.ShapeDtypeStruct((S, HGW), F32),
                   jax.ShapeDtypeStruct((2, HG_HEADS, N_CHUNKS, HG_DIM, HG_DIM), BF16)],
        scratch_shapes=[pltpu.VMEM((2, S, HG_DIM), BF16), pltpu.VMEM((2, S, HG_DIM), BF16),
                        pltpu.VMEM((2, N_CHUNKS, HG_DIM, HG_DIM), F32), pltpu.VMEM((T, HG_DIM), BF16),
                        pltpu.VMEM((2, T, HG_DIM), F32)],
        operands=[p_a, lbl])


def _hgrn_bwd(p_a, lbl, d_o, st, carried=None):
    cpt = TM // CHUNK

    def rows(r):
        return r * TM if isinstance(r, int) else pl.multiple_of(r * TM, TM)

    def body(p_ref, lbl_ref, do_ref, st_ref, dp_ref, dlb_ref, b_s, bt_s, dbt_s, qd_s, dst_s, w_s):
        masks = [_chunk_masks(d == 1) for d in (0, 1)]
        same01 = jnp.where(masks[0][0], 1.0, 0.0).astype(BF16)
        tri = [m[1] for m in masks]
        tri01 = [jnp.where(t, 1.0, 0.0).astype(BF16) for t in tri]
        later01 = [tri01[1], tri01[0]]
        lb = [_sigmoid(lbl_ref[d][0:1, :] - lbl_ref[d][1:2, :]) for d in (0, 1)]
        dbt_s[...] = jnp.zeros_like(dbt_s)

        def prep_tile(r, latent):
            r0 = rows(r)
            for d in (0, 1):
                z = p_ref[pl.ds(r0, TM), d * HG_DIM:(d + 1) * HG_DIM]
                _, _, b, bt = _decay_terms(z, lb[d], same01, tri01[d])
                b_s[d, pl.ds(r0, TM), :] = b
                bt_s[d, pl.ds(r0, TM), :] = bt
                if latent:
                    rl = pl.multiple_of(r0 - L, TM)
                    qr = p_ref[pl.ds(r0, TM), 3 * HG_DIM:4 * HG_DIM]
                    qd = (qr * _sigmoid(qr) * HG_DIM ** -0.5 * jnp.exp(b)).astype(BF16)
                    qd_s[d, pl.ds(rl, TM), :] = qd
                    w_s[d, pl.ds(r * cpt, cpt)] = _chunk_outer(
                        do_ref[pl.ds(rl, TM), :].astype(BF16), qd).astype(BF16)

        prep_tile(0, False)
        w_s[:, pl.ds(0, N_CTX_CHUNKS)] = jnp.zeros((2, N_CTX_CHUNKS, HG_DIM, HG_DIM), BF16)

        def prep(r, carry):
            prep_tile(r, True)
            return carry

        lax.fori_loop(1, N_TILES, prep, 0)

        def rscan(j, dsts):
            i = N_CHUNKS - 1 - j
            new = []
            for d in (0, 1):
                nn = _chunk_order(i, d == 1)
                c0 = pl.multiple_of(nn * CHUNK, CHUNK)
                dst_s[d, nn] = dsts[d].astype(BF16)
                after = st_ref[d, _chunk_order(jnp.minimum(i + 1, N_CHUNKS - 1), d == 1)].astype(F32)
                dbt_s[d, pl.ds(c0, 1), :] = jnp.sum(after * dsts[d], axis=0, keepdims=True)
                new.append(dsts[d] * jnp.exp(bt_s[d, pl.ds(c0, 1), :]) + w_s[d, nn].astype(F32))
            return tuple(new)

        zero = jnp.zeros((HG_DIM, HG_DIM), F32)
        lax.fori_loop(0, N_CHUNKS, rscan, (zero, zero))

        def grad_tile(r, latent):
            r0 = rows(r)
            vb = p_ref[pl.ds(r0, TM), 2 * HG_DIM:3 * HG_DIM].astype(BF16)
            dv = jnp.zeros((TM, HG_DIM), F32)
            dq = jnp.zeros((TM, HG_DIM), F32)
            dlbs = []
            if latent:
                rl = pl.multiple_of(r0 - L, TM)
                qr = p_ref[pl.ds(r0, TM), 3 * HG_DIM:4 * HG_DIM]
                sq = _sigmoid(qr)
                do = do_ref[pl.ds(rl, TM), :].astype(BF16)
                da_full = _dot_nt(do, vb)
            for d in (0, 1):
                z = p_ref[pl.ds(r0, TM), d * HG_DIM:(d + 1) * HG_DIM]
                sz = _sigmoid(z)
                f = lb[d] + (1.0 - lb[d]) * sz
                k = 1.0 - f
                b = b_s[d, pl.ds(r0, TM), :]
                e2 = jnp.exp(bt_s[d, pl.ds(r0, TM), :] - b)
                dstb = dst_s[d, pl.ds(r * cpt, cpt)]
                kd2 = k * e2
                dkd2 = jnp.einsum('ncv,nvk->nck', vb.reshape(cpt, CHUNK, HG_DIM), dstb,
                                  preferred_element_type=F32).reshape(TM, HG_DIM)
                dv = dv + jnp.einsum('nck,nvk->ncv', kd2.astype(BF16).reshape(cpt, CHUNK, HG_DIM), dstb,
                                     preferred_element_type=F32).reshape(TM, HG_DIM)
                dk = dkd2 * e2
                db = -(kd2 * dkd2)
                if latent:
                    eb = jnp.exp(b)
                    enb = jnp.exp(-b)
                    qdf = qr * sq * HG_DIM ** -0.5 * eb
                    kdf = k * enb
                    qd = qd_s[d, pl.ds(rl, TM), :]
                    kd = kdf.astype(BF16)
                    a = jnp.where(tri[d], _dot_nt(qd, kd), 0.0).astype(BF16)
                    da = jnp.where(tri[d], da_full, 0.0).astype(BF16)
                    stb = st_ref[d, pl.ds(r * cpt, cpt)]
                    dqd = _dot(da, kd) + jnp.einsum(
                        'ncv,nvk->nck', do.reshape(cpt, CHUNK, HG_DIM), stb,
                        preferred_element_type=F32).reshape(TM, HG_DIM)
                    dkd = _dot_tn(da, qd)
                    dv = dv + _dot_tn(a, do)
                    dk = dk + dkd * enb
                    db = db + qdf * dqd - kdf * dkd
                    dq = dq + dqd * eb
                dg = (_dot_exact_lhs01(later01[d], db)
                      + _dot_exact_lhs01(same01, dbt_s[d, pl.ds(r0, TM), :]))
                df = dg / f - dk
                dp_ref[pl.ds(r0, TM), d * HG_DIM:(d + 1) * HG_DIM] = (
                    df * (1.0 - lb[d]) * sz * (1.0 - sz)).astype(BF16)
                dlbs.append(jnp.sum(df * (1.0 - sz), axis=0, keepdims=True))
            dp_ref[pl.ds(r0, TM), 2 * HG_DIM:3 * HG_DIM] = dv.astype(BF16)
            if latent:
                dq = dq * (HG_DIM ** -0.5) * (sq * (1.0 + qr * (1.0 - sq)))
            dp_ref[pl.ds(r0, TM), 3 * HG_DIM:4 * HG_DIM] = dq.astype(BF16)
            return dlbs

        dlb_ctx = grad_tile(0, False)

        def grads(r, acc):
            t = grad_tile(r, True)
            return (acc[0] + t[0], acc[1] + t[1])

        dlb = lax.fori_loop(1, N_TILES, grads, (dlb_ctx[0], dlb_ctx[1]))
        dlb_ref[0:1, :] = dlb[0]
        dlb_ref[1:2, :] = dlb[1]

    return _pcall(
        body, carried, name="hgrn_bwd", grid=(HG_HEADS,),
        in_specs=[pl.BlockSpec((T, 4 * HG_DIM), lambda h: (0, h)),
                  pl.BlockSpec((2, 2, HG_DIM), lambda h: (0, 0, h)),
                  pl.BlockSpec((S, HG_DIM), lambda h: (0, h)),
                  pl.BlockSpec((2, None, N_CHUNKS, HG_DIM, HG_DIM), lambda h: (0, h, 0, 0, 0))],
        out_specs=[pl.BlockSpec((T, 4 * HG_DIM), lambda h: (0, h)),
                   pl.BlockSpec((2, HG_DIM), lambda h: (0, h))],
        out_shape=[jax.ShapeDtypeStruct((T, WA), BF16), jax.ShapeDtypeStruct((2, HGW), F32)],
        scratch_shapes=[pltpu.VMEM((2, T, HG_DIM), F32), pltpu.VMEM((2, T, HG_DIM), F32),
                        pltpu.VMEM((2, T, HG_DIM), F32), pltpu.VMEM((2, S, HG_DIM), BF16),
                        pltpu.VMEM((2, N_CHUNKS, HG_DIM, HG_DIM), BF16),
                        pltpu.VMEM((2, N_CHUNKS, HG_DIM, HG_DIM), BF16)],
        operands=[p_a, lbl, d_o, st])


def _rope_tables():
    t = np.arange(S)
    inv = ROPE_THETA ** (-np.arange(0, 32, 2, dtype=np.float64) / 32)
    lane = np.arange(64)
    pos = np.where(lane[None, :] < 32, (t // GRID_W)[:, None], (t % GRID_W)[:, None]).astype(np.float64)
    ang = pos * inv[(lane % 32) % 16][None, :]
    sign = np.where((lane % 32) < 16, -1.0, 1.0)[None, :]
    cos = np.tile(np.cos(ang), (1, 2)).astype(np.float32)
    sin = np.tile(np.sin(ang) * sign, (1, 2)).astype(np.float32)
    return jnp.asarray(cos), jnp.asarray(sin)


def _rope_partner(v):
    lane = lax.broadcasted_iota(jnp.int32, (1, 128), 1)
    first = (lane % 32) < 16
    slabs = []
    for j in range(v.shape[1] // 128):
        s = v[:, 128 * j:128 * (j + 1)]
        slabs.append(jnp.where(first, pltpu.roll(s, 112, 1), pltpu.roll(s, 16, 1)))
    return slabs[0] if len(slabs) == 1 else jnp.concatenate(slabs, axis=1)


def _group_ones(width, group):
    r = lax.broadcasted_iota(jnp.int32, (width, width), 0)
    c = lax.broadcasted_iota(jnp.int32, (width, width), 1)
    return jnp.where((r // group) == (c // group), 1.0, 0.0).astype(BF16)


def _group_mean(v, ones01, group):
    hi = v.astype(BF16)
    lo = (v - hi.astype(F32)).astype(BF16)
    return (_dot(hi, ones01) + _dot(lo, ones01)) * (1.0 / group)


def _rep_matrix():
    r = lax.broadcasted_iota(jnp.int32, (KVW, ATW), 0)
    c = lax.broadcasted_iota(jnp.int32, (KVW, ATW), 1)
    return jnp.where(r == HEAD_DIM * (c // 256) + c % HEAD_DIM, 1.0, 0.0).astype(BF16)


def _tile_lanes(v, reps):
    return jnp.concatenate([v] * reps, axis=1)


def _prep_fwd(p_b, o, cos, sin, hnw, qnw, knw):
    def body(p_ref, o_ref, cos_ref, sin_ref, hnw_ref, qnw_ref, knw_ref, y_ref, q_ref, k_ref, v_ref):
        i = pl.program_id(0)
        rep = _rep_matrix()
        ones_k = _group_ones(KVW, HEAD_DIM)
        kr = p_ref[:, 1024:1152]
        krstd = lax.rsqrt(_group_mean(kr * kr, ones_k, HEAD_DIM) + EPS)
        kn = kr * krstd * knw_ref[...]
        v_ref[...] = _dot(p_ref[:, 1152:1280].astype(BF16), rep).astype(BF16)

        @pl.when(i == 0)
        def _():
            k_ref[...] = _dot(kn.astype(BF16), rep).astype(BF16)

        @pl.when(i > 0)
        def _():
            cs, sn = cos_ref[...], sin_ref[...]
            kro = kn * cs + _rope_partner(kn) * sn
            k_ref[...] = _dot(kro.astype(BF16), rep).astype(BF16)
            qr = p_ref[:, 512:1024]
            qrstd = lax.rsqrt(_group_mean(qr * qr, _group_ones(ATW, HEAD_DIM), HEAD_DIM) + EPS)
            qn = qr * qrstd * qnw_ref[...]
            qro = qn * _tile_lanes(cs, 4) + _rope_partner(qn) * _tile_lanes(sn, 4)
            q_ref[...] = (qro * HEAD_DIM ** -0.5).astype(BF16)
            ys = []
            for h in range(HG_HEADS):
                oh = o_ref[:, HG_DIM * h:HG_DIM * (h + 1)]
                gh = p_ref[:, HG_DIM * h:HG_DIM * (h + 1)]
                rstd = lax.rsqrt(jnp.mean(oh * oh, axis=-1, keepdims=True) + EPS)
                ys.append(oh * rstd * hnw_ref[...] * (gh * _sigmoid(gh)))
            y_ref[...] = jnp.concatenate(ys, axis=1).astype(BF16)

    return pl.pallas_call(
        body, name="prep_fwd", grid=(N_TILES,),
        in_specs=[pl.BlockSpec((TM, WB), lambda i: (i, 0)),
                  pl.BlockSpec((TM, HGW), lambda i: (_lat(i), 0)),
                  pl.BlockSpec((TM, 128), lambda i: (_lat(i), 0)),
                  pl.BlockSpec((TM, 128), lambda i: (_lat(i), 0)),
                  _full((1, HG_DIM)), _full((1, ATW)), _full((1, KVW))],
        out_specs=[pl.BlockSpec((TM, HGW), lambda i: (_lat(i), 0)),
                   pl.BlockSpec((TM, ATW), lambda i: (_lat(i), 0)),
                   pl.BlockSpec((TM, ATW), lambda i: (i, 0)),
                   pl.BlockSpec((TM, ATW), lambda i: (i, 0))],
        out_shape=[jax.ShapeDtypeStruct((S, HGW), BF16), jax.ShapeDtypeStruct((S, ATW), BF16),
                   jax.ShapeDtypeStruct((T, ATW), BF16), jax.ShapeDtypeStruct((T, ATW), BF16)],
        compiler_params=_cp(("arbitrary",)),
    )(p_b, o, cos, sin, hnw, qnw, knw)


def _prep_bwd(p_b, o, cos, sin, hnw, qnw, knw, dy_hg, dq, dk_rep, dv_rep, carried=None):
    def body(p_ref, o_ref, cos_ref, sin_ref, hnw_ref, qnw_ref, knw_ref, dy_ref, dq_ref, dk_ref, dv_ref,
             dp_ref, do_ref, acc_ref):
        i = pl.program_id(0)

        @pl.when(i == 0)
        def _():
            acc_ref[...] = jnp.zeros_like(acc_ref)

        rep = _rep_matrix()
        ones_k = _group_ones(KVW, HEAD_DIM)

        def fold(v):
            hi = v.astype(BF16)
            lo = (v - hi.astype(F32)).astype(BF16)
            return _dot_nt(hi, rep) + _dot_nt(lo, rep)

        kr = p_ref[:, 1024:1152]
        krstd = lax.rsqrt(_group_mean(kr * kr, ones_k, HEAD_DIM) + EPS)
        khat = kr * krstd
        kw = knw_ref[...]
        dkro = fold(dk_ref[...])
        dv = fold(dv_ref[...])

        def k_back(dkn):
            dkhat = dkn * kw
            dkr = krstd * (dkhat - khat * _group_mean(dkhat * khat, ones_k, HEAD_DIM))
            acc_ref[2:3, 0:KVW] += jnp.sum(dkn * khat, axis=0, keepdims=True)
            dp_ref[:, 1024:1152] = dkr.astype(BF16)
            dp_ref[:, 1152:1280] = dv.astype(BF16)

        @pl.when(i == 0)
        def _():
            k_back(dkro)
            dp_ref[:, 0:1024] = jnp.zeros((TM, 1024), BF16)

        @pl.when(i > 0)
        def _():
            cs, sn = cos_ref[...], sin_ref[...]
            k_back(dkro * cs + _rope_partner(dkro * sn))
            ones_q = _group_ones(ATW, HEAD_DIM)
            qr = p_ref[:, 512:1024]
            qrstd = lax.rsqrt(_group_mean(qr * qr, ones_q, HEAD_DIM) + EPS)
            qhat = qr * qrstd
            dqro = dq_ref[...] * HEAD_DIM ** -0.5
            dqn = dqro * _tile_lanes(cs, 4) + _rope_partner(dqro * _tile_lanes(sn, 4))
            dqhat = dqn * qnw_ref[...]
            dqr = qrstd * (dqhat - qhat * _group_mean(dqhat * qhat, ones_q, HEAD_DIM))
            acc_ref[1:2, :] += jnp.sum(dqn * qhat, axis=0, keepdims=True)
            dp_ref[:, 512:1024] = dqr.astype(BF16)
            dws = jnp.zeros((1, HG_DIM), F32)
            for h in range(HG_HEADS):
                sl = slice(HG_DIM * h, HG_DIM * (h + 1))
                oh, gh, dy = o_ref[:, sl], p_ref[:, sl], dy_ref[:, sl]
                rstd = lax.rsqrt(jnp.mean(oh * oh, axis=-1, keepdims=True) + EPS)
                ohat = oh * rstd
                sg = _sigmoid(gh)
                dp_ref[:, sl] = (dy * (ohat * hnw_ref[...]) * (sg * (1.0 + gh * (1.0 - sg)))).astype(BF16)
                dn = dy * (gh * sg)
                dws = dws + jnp.sum(dn * ohat, axis=0, keepdims=True)
                dohat = dn * hnw_ref[...]
                do_ref[:, sl] = rstd * (dohat - ohat * jnp.mean(dohat * ohat, axis=-1, keepdims=True))
            acc_ref[0:1, 0:HG_DIM] += dws

    return _pcall(
        body, carried, name="prep_bwd", grid=(N_TILES,),
        in_specs=[pl.BlockSpec((TM, WB), lambda i: (i, 0)),
                  pl.BlockSpec((TM, HGW), lambda i: (_lat(i), 0)),
                  pl.BlockSpec((TM, 128), lambda i: (_lat(i), 0)),
                  pl.BlockSpec((TM, 128), lambda i: (_lat(i), 0)),
                  _full((1, HG_DIM)), _full((1, ATW)), _full((1, KVW)),
                  pl.BlockSpec((TM, HGW), lambda i: (_lat(i), 0)),
                  pl.BlockSpec((TM, ATW), lambda i: (_lat(i), 0)),
                  pl.BlockSpec((TM, ATW), lambda i: (i, 0)),
                  pl.BlockSpec((TM, ATW), lambda i: (i, 0))],
        out_specs=[pl.BlockSpec((TM, WB), lambda i: (i, 0)),
                   pl.BlockSpec((TM, HGW), lambda i: (_lat(i), 0)),
                   _full((8, ATW))],
        out_shape=[jax.ShapeDtypeStruct((T, WB), BF16), jax.ShapeDtypeStruct((S, HGW), F32),
                   jax.ShapeDtypeStruct((8, ATW), F32)],
        scratch_shapes=[], operands=[p_b, o, cos, sin, hnw, qnw, knw, dy_hg, dq, dk_rep, dv_rep])


NEG = -1e30
_CTX_BLOCKS = L // BLOCK


def _attn_window_specs():
    prev = pl.BlockSpec((BLOCK, ATW), lambda i: (jnp.maximum(i - 1, 0) + _CTX_BLOCKS, 0))
    own = pl.BlockSpec((BLOCK, ATW), lambda i: (i + _CTX_BLOCKS, 0))
    nxt = pl.BlockSpec((BLOCK, ATW), lambda i: (jnp.minimum(i + 1, N_BLOCKS - 1) + _CTX_BLOCKS, 0))
    return [prev, own, nxt, _full((L, ATW))]


def _attn_valid(i, heads=4):
    qi = lax.broadcasted_iota(jnp.int32, (heads * BLOCK, 3 * BLOCK), 0) % BLOCK
    kj = lax.broadcasted_iota(jnp.int32, (heads * BLOCK, 3 * BLOCK), 1)
    return ((jnp.abs(kj - BLOCK - qi) <= BLOCK) & ((kj >= BLOCK) | (i > 0))
            & ((kj < 2 * BLOCK) | (i < N_BLOCKS - 1)))


def _attn_fwd(q, k_rep, v_rep, sinks, carried=None):
    def body(q_ref, kp, ko, kn, kc, vp, vo, vn, vc, sink_ref, y_ref, lse_ref):
        i = pl.program_id(0)
        valid = _attn_valid(i, 1)
        lane8 = lax.broadcasted_iota(jnp.int32, (1, ATT_HEADS), 1)
        head_of_lane = lax.broadcasted_iota(jnp.int32, (1, 256), 1) // HEAD_DIM
        lse_out = jnp.zeros((BLOCK, ATT_HEADS), F32)
        for hk in range(KV_HEADS):
            sl = slice(256 * hk, 256 * (hk + 1))
            qg = q_ref[:, sl]
            kl = jnp.concatenate([kp[:, sl], ko[:, sl], kn[:, sl]], axis=0)
            vl = jnp.concatenate([vp[:, sl], vo[:, sl], vn[:, sl]], axis=0)
            yg = jnp.zeros((BLOCK, 256), F32)
            for g in range(4):
                q1 = jnp.where(head_of_lane == g, qg, jnp.zeros_like(qg))
                s_loc = jnp.where(valid, _dot_nt(q1, kl), NEG)
                s_ctx = _dot_nt(q1, kc[:, sl])
                sink = sink_ref[0:1, 4 * hk + g:4 * hk + g + 1]
                m = jnp.maximum(jnp.maximum(jnp.max(s_loc, axis=1, keepdims=True),
                                            jnp.max(s_ctx, axis=1, keepdims=True)), sink)
                p_loc = jnp.exp(s_loc - m)
                p_ctx = jnp.exp(s_ctx - m)
                den = (jnp.sum(p_loc, axis=1, keepdims=True) + jnp.sum(p_ctx, axis=1, keepdims=True)
                       + jnp.exp(sink - m))
                o1 = (_dot(p_loc.astype(BF16), vl) + _dot(p_ctx.astype(BF16), vc[:, sl])) * (1.0 / den)
                yg = yg + jnp.where(head_of_lane == g, o1, 0.0)
                lse_out = lse_out + jnp.where(lane8 == 4 * hk + g, m + jnp.log(den), 0.0)
            y_ref[:, sl] = yg.astype(BF16)
        lse_ref[...] = lse_out

    return _pcall(
        body, carried, name="attn_fwd", grid=(N_BLOCKS,),
        in_specs=[pl.BlockSpec((BLOCK, ATW), lambda i: (i, 0))] + _attn_window_specs()
        + _attn_window_specs() + [_full((1, ATT_HEADS))],
        out_specs=[pl.BlockSpec((BLOCK, ATW), lambda i: (i, 0)),
                   pl.BlockSpec((BLOCK, ATT_HEADS), lambda i: (i, 0))],
        out_shape=[jax.ShapeDtypeStruct((S, ATW), BF16), jax.ShapeDtypeStruct((S, ATT_HEADS), F32)],
        scratch_shapes=[],
        operands=[q, k_rep, k_rep, k_rep, k_rep, v_rep, v_rep, v_rep, v_rep, sinks])


def _attn_bwd(q, k_rep, v_rep, sinks, y_at, lse, dy, carried=None):
    def body(q_ref, kp, ko, kn, kc, vp, vo, vn, vc, sink_ref, y_ref, lse_ref, dy_ref,
             dq_ref, dk_ref, dv_ref, dsink_ref, dk_acc, dv_acc):
        i = pl.program_id(0)

        @pl.when(i == 0)
        def _():
            dk_acc[...] = jnp.zeros_like(dk_acc)
            dv_acc[...] = jnp.zeros_like(dv_acc)
            dk_ref[pl.ds(0, L), :] = jnp.zeros((L, ATW), F32)
            dv_ref[pl.ds(0, L), :] = jnp.zeros((L, ATW), F32)
            dsink_ref[...] = jnp.zeros_like(dsink_ref)

        valid = _attn_valid(i, 1)
        lane8 = lax.broadcasted_iota(jnp.int32, (1, ATT_HEADS), 1)
        head_of_lane = lax.broadcasted_iota(jnp.int32, (1, 256), 1) // HEAD_DIM
        w0 = pl.multiple_of(i * BLOCK, BLOCK)
        dsink = jnp.zeros((1, ATT_HEADS), F32)
        for hk in range(KV_HEADS):
            sl = slice(256 * hk, 256 * (hk + 1))
            qg, dyg, yg = q_ref[:, sl], dy_ref[:, sl], y_ref[:, sl].astype(F32)
            kl = jnp.concatenate([kp[:, sl], ko[:, sl], kn[:, sl]], axis=0)
            vl = jnp.concatenate([vp[:, sl], vo[:, sl], vn[:, sl]], axis=0)
            dq = jnp.zeros((BLOCK, 256), F32)
            q4, do4, ds4_loc, ds4_ctx, p4_loc, p4_ctx = [], [], [], [], [], []
            for g in range(4):
                mine = head_of_lane == g
                q1 = jnp.where(mine, qg, jnp.zeros_like(qg))
                do1f = jnp.where(mine, dyg, 0.0)
                do1 = do1f.astype(BF16)
                lse1 = jnp.sum(jnp.where(lane8 == 4 * hk + g, lse_ref[...], 0.0), axis=1, keepdims=True)
                p_loc = jnp.where(valid, jnp.exp(_dot_nt(q1, kl) - lse1), 0.0)
                p_ctx = jnp.exp(_dot_nt(q1, kc[:, sl]) - lse1)
                delta = jnp.sum(do1f * yg, axis=1, keepdims=True)
                ds_loc = (p_loc * (_dot_nt(do1, vl) - delta)).astype(BF16)
                ds_ctx = (p_ctx * (_dot_nt(do1, vc[:, sl]) - delta)).astype(BF16)
                dq = dq + jnp.where(mine, _dot(ds_loc, kl) + _dot(ds_ctx, kc[:, sl]), 0.0)
                p_sink = jnp.exp(sink_ref[0:1, 4 * hk + g:4 * hk + g + 1] - lse1)
                dsink = dsink + jnp.where(lane8 == 4 * hk + g,
                                          -jnp.sum(p_sink * delta, axis=0, keepdims=True), 0.0)
                q4.append(q1)
                do4.append(do1)
                ds4_loc.append(ds_loc)
                ds4_ctx.append(ds_ctx)
                p4_loc.append(p_loc.astype(BF16))
                p4_ctx.append(p_ctx.astype(BF16))
            dq_ref[:, sl] = dq
            q4, do4, ds4_loc, ds4_ctx, p4_loc, p4_ctx = [
                jnp.concatenate(t, axis=0) for t in (q4, do4, ds4_loc, ds4_ctx, p4_loc, p4_ctx)]
            dk_acc[pl.ds(w0, 3 * BLOCK), sl] += _dot_tn(ds4_loc, q4)
            dv_acc[pl.ds(w0, 3 * BLOCK), sl] += _dot_tn(p4_loc, do4)
            dk_ref[pl.ds(0, L), sl] += _dot_tn(ds4_ctx, q4)
            dv_ref[pl.ds(0, L), sl] += _dot_tn(p4_ctx, do4)
        dsink_ref[...] += dsink

        @pl.when(i == N_BLOCKS - 1)
        def _():
            dk_ref[pl.ds(L, S), :] = dk_acc[pl.ds(BLOCK, S), :]
            dv_ref[pl.ds(L, S), :] = dv_acc[pl.ds(BLOCK, S), :]

    row_q = pl.BlockSpec((BLOCK, ATW), lambda i: (i, 0))
    return _pcall(
        body, carried, name="attn_bwd", grid=(N_BLOCKS,),
        in_specs=[row_q] + _attn_window_specs() + _attn_window_specs()
        + [_full((1, ATT_HEADS)), row_q, pl.BlockSpec((BLOCK, ATT_HEADS), lambda i: (i, 0)), row_q],
        out_specs=[row_q, _full((T, ATW)), _full((T, ATW)), _full((1, ATT_HEADS))],
        out_shape=[jax.ShapeDtypeStruct((S, ATW), F32), jax.ShapeDtypeStruct((T, ATW), F32),
                   jax.ShapeDtypeStruct((T, ATW), F32), jax.ShapeDtypeStruct((1, ATT_HEADS), F32)],
        scratch_shapes=[pltpu.VMEM((S + 2 * BLOCK, ATW), F32), pltpu.VMEM((S + 2 * BLOCK, ATW), F32)],
        operands=[q, k_rep, k_rep, k_rep, k_rep, v_rep, v_rep, v_rep, v_rep, sinks, y_at, lse, dy])


def _merge_fwd(y_hg, y_at, p_c, x, w_bh, w_ba, w_out, g1, nfw, sh2, sc2):
    def body(yh_ref, ya_ref, g_ref, x_ref, wbh_ref, wba_ref, wo_ref, g1_ref, nfw_ref, sh_ref, sc_ref,
             a_ref, b_ref, mx_ref, r_ref, x1_ref, h2_ref):
        a = _dot_nt(yh_ref[...], wbh_ref[...])
        b = _dot_nt(ya_ref[...], wba_ref[...])
        mixed = (_sigmoid(g_ref[:, :D]) * a + _sigmoid(g_ref[:, D:]) * b).astype(BF16)
        r = _dot(mixed, wo_ref[...])
        x1 = x_ref[...] + g1_ref[...] * r
        a_ref[...] = a
        b_ref[...] = b
        mx_ref[...] = mixed
        r_ref[...] = r
        x1_ref[...] = x1
        h2_ref[...] = _rms_mod(x1, nfw_ref[...], sh_ref[...], sc_ref[...]).astype(BF16)

    row = lambda w: pl.BlockSpec((TM, w), lambda i: (i, 0))
    vec = _full((1, D))
    return pl.pallas_call(
        body, name="merge_fwd", grid=(N_LAT_TILES,),
        in_specs=[row(HGW), row(ATW), row(WC), row(D), _VMEM_WHOLE, _VMEM_WHOLE, _VMEM_WHOLE,
                  vec, vec, vec, vec],
        out_specs=[row(D)] * 6,
        out_shape=[jax.ShapeDtypeStruct((S, D), dt) for dt in (F32, F32, BF16, F32, F32, BF16)],
        compiler_params=_cp(("parallel",)),
    )(y_hg, y_at, p_c, x, w_bh, w_ba, w_out, g1, nfw, sh2, sc2)


def _merge_bwd(dx1, r, a, b, p_c, w_bh, w_ba, w_out, g1, carried=None):
    def body(dx_ref, r_ref, a_ref, b_ref, g_ref, wbh_ref, wba_ref, wo_ref, g1_ref,
             dr_ref, da_ref, db_ref, dg_ref, dyh_ref, dya_ref, acc_ref):
        @pl.when(pl.program_id(0) == 0)
        def _():
            acc_ref[...] = jnp.zeros_like(acc_ref)

        dx1v = dx_ref[...]
        acc_ref[0:1, :] += jnp.sum(dx1v * r_ref[...], axis=0, keepdims=True)
        dr = (g1_ref[...] * dx1v).astype(BF16)
        dr_ref[...] = dr
        dmix = _dot_nt(dr, wo_ref[...])
        sh, sa = _sigmoid(g_ref[:, :D]), _sigmoid(g_ref[:, D:])
        da = (dmix * sh).astype(BF16)
        db = (dmix * sa).astype(BF16)
        da_ref[...] = da
        db_ref[...] = db
        dg_ref[:, :D] = (dmix * a_ref[...] * sh * (1.0 - sh)).astype(BF16)
        dg_ref[:, D:] = (dmix * b_ref[...] * sa * (1.0 - sa)).astype(BF16)
        dyh_ref[...] = _dot(da, wbh_ref[...])
        dya_ref[...] = _dot(db, wba_ref[...])

    row = lambda w: pl.BlockSpec((TM, w), lambda i: (i, 0))
    return _pcall(
        body, carried, name="merge_bwd", grid=(N_LAT_TILES,),
        in_specs=[row(D), row(D), row(D), row(D), row(WC), _VMEM_WHOLE, _VMEM_WHOLE, _VMEM_WHOLE,
                  _full((1, D))],
        out_specs=[row(D), row(D), row(D), row(WC), row(HGW), row(ATW), _full((8, D))],
        out_shape=[jax.ShapeDtypeStruct((S, D), BF16), jax.ShapeDtypeStruct((S, D), BF16),
                   jax.ShapeDtypeStruct((S, D), BF16), jax.ShapeDtypeStruct((S, WC), BF16),
                   jax.ShapeDtypeStruct((S, HGW), F32), jax.ShapeDtypeStruct((S, ATW), F32),
                   jax.ShapeDtypeStruct((8, D), F32)],
        scratch_shapes=[], operands=[dx1, r, a, b, p_c, w_bh, w_ba, w_out, g1])


def _ffn_fused(x1, h2, tgt, w_gate, w_up, w_down, g2, nfw, sc2):
    def body(x1_ref, h2_ref, t_ref, wg_ref, wu_ref, wd_ref, g2_ref, nfw_ref, sc_ref,
             act_ref, dgt_ref, dup_ref, df_ref, dx_ref, acc_ref, gs, us):
        @pl.when(pl.program_id(0) == 0)
        def _():
            acc_ref[...] = jnp.zeros_like(acc_ref)

        h2 = h2_ref[...]
        f = jnp.zeros((TM, D), F32)
        for j in range(N_FF_TILES):
            g = _dot_nt(h2, wg_ref[j])
            u = _dot_nt(h2, wu_ref[j])
            gs[j] = g
            us[j] = u
            act = (g * _sigmoid(g) * u).astype(BF16)
            act_ref[j] = act
            f = f + _dot(act, wd_ref[j])
        x1v = x1_ref[...]
        g2 = g2_ref[...]
        diff = x1v + g2 * f - t_ref[...]
        dy = diff * (1.0 / D)
        df = (g2 * dy).astype(BF16)
        df_ref[...] = df
        dh2 = jnp.zeros((TM, D), F32)
        for j in range(N_FF_TILES):
            g, u = gs[j], us[j]
            sg = _sigmoid(g)
            dact = _dot_nt(df, wd_ref[j])
            dgate = (dact * u * (sg * (1.0 + g * (1.0 - sg)))).astype(BF16)
            dup = (dact * (g * sg)).astype(BF16)
            dgt_ref[j] = dgate
            dup_ref[j] = dup
            dh2 = dh2 + _dot(dgate, wg_ref[j]) + _dot(dup, wu_ref[j])
        dx, dsh, dsc, dnw = _rms_mod_bwd(x1v, nfw_ref[...], sc_ref[...], dh2)
        dx_ref[...] = dy + dx
        acc_ref[0:1, :] += dsh
        acc_ref[1:2, :] += dsc
        acc_ref[2:3, :] += dnw
        acc_ref[3:4, :] += jnp.sum(dy * f, axis=0, keepdims=True)
        acc_ref[4:5, :] += 0.5 * jnp.sum(jnp.sum(diff * diff, axis=1, keepdims=True), axis=0,
                                         keepdims=True) * (1.0 / D)

    row = lambda dt_w: pl.BlockSpec((TM, dt_w), lambda i: (i, 0))
    blk = pl.BlockSpec((N_FF_TILES, TM, FF_TILE), lambda i: (0, i, 0))
    vec = _full((1, D))
    return pl.pallas_call(
        body, name="ffn_fused", grid=(N_LAT_TILES,),
        in_specs=[row(D), row(D), row(D), _VMEM_WHOLE, _VMEM_WHOLE, _VMEM_WHOLE, vec, vec, vec],
        out_specs=[blk, blk, blk, row(D), row(D), _full((8, D))],
        out_shape=[jax.ShapeDtypeStruct((N_FF_TILES, S, FF_TILE), BF16)] * 3
        + [jax.ShapeDtypeStruct((S, D), BF16), jax.ShapeDtypeStruct((S, D), F32),
           jax.ShapeDtypeStruct((8, D), F32)],
        scratch_shapes=[pltpu.VMEM((N_FF_TILES, TM, FF_TILE), F32), pltpu.VMEM((N_FF_TILES, TM, FF_TILE), F32)],
        compiler_params=_cp(("arbitrary",)),
    )(x1, h2, tgt, w_gate, w_up, w_down, g2, nfw, sc2)


def _proj_bc(h_all, w_b, w_c, carried=None):
    def body(h_ref, wb_ref, wc_ref, pb_ref, pc_ref):
        h = h_ref[...]
        pb_ref[...] = _dot_nt(h, wb_ref[...])

        @pl.when(pl.program_id(0) > 0)
        def _():
            pc_ref[...] = _dot_nt(h, wc_ref[...])

    return _pcall(
        body, carried, name="proj_bc", grid=(N_TILES,),
        in_specs=[pl.BlockSpec((TM, D), lambda i: (i, 0)), _VMEM_WHOLE, _VMEM_WHOLE],
        out_specs=[pl.BlockSpec((TM, WB), lambda i: (i, 0)), pl.BlockSpec((TM, WC), lambda i: (_lat(i), 0))],
        out_shape=[jax.ShapeDtypeStruct((T, WB), F32), jax.ShapeDtypeStruct((S, WC), F32)],
        scratch_shapes=[], operands=[h_all, w_b, w_c])


def _input_bwd(dp_a, dp_b, dp_c, w_a, w_b, w_c, ctx, x, dx1, nw, sh, sc, carried=None):
    def body(da_ref, db_ref, dc_ref, wa_ref, wb_ref, wc_ref, ctx_ref, x_ref, dx1_ref, nw_ref, sh_ref,
             sc_ref, gx_ref, acc_ref):
        i = pl.program_id(0)

        @pl.when(i == 0)
        def _():
            acc_ref[...] = jnp.zeros_like(acc_ref)

        dh = _dot(da_ref[...], wa_ref[...]) + _dot(db_ref[...], wb_ref[...])

        @pl.when(i == 0)
        def _():
            _, dsh, dsc, dnw = _rms_mod_bwd(ctx_ref[...], nw_ref[...], sc_ref[0:1, :], dh)
            acc_ref[3:4, :] += dsh
            acc_ref[4:5, :] += dsc
            acc_ref[2:3, :] += dnw

        @pl.when(i > 0)
        def _():
            dhl = dh + _dot(dc_ref[...], wc_ref[...])
            dx, dsh, dsc, dnw = _rms_mod_bwd(x_ref[...], nw_ref[...], sc_ref[1:2, :], dhl)
            gx_ref[...] = dx1_ref[...] + dx
            acc_ref[0:1, :] += dsh
            acc_ref[1:2, :] += dsc
            acc_ref[2:3, :] += dnw

    lat = lambda w: pl.BlockSpec((TM, w), lambda i: (_lat(i), 0))
    return _pcall(
        body, carried, name="input_bwd", grid=(N_TILES,),
        in_specs=[pl.BlockSpec((TM, WA), lambda i: (i, 0)), pl.BlockSpec((TM, WB), lambda i: (i, 0)),
                  lat(WC), _VMEM_WHOLE, _VMEM_WHOLE, _VMEM_WHOLE, _full((TM, D)), lat(D), lat(D),
                  _full((1, D)), _full((2, D)), _full((2, D))],
        out_specs=[lat(D), _full((8, D))],
        out_shape=[jax.ShapeDtypeStruct((S, D), F32), jax.ShapeDtypeStruct((8, D), F32)],
        scratch_shapes=[], operands=[dp_a, dp_b, dp_c, w_a, w_b, w_c, ctx, x, dx1, nw, sh, sc])


_C1 = 1.0 - ADAM_B1 ** ADAM_STEP
_C2 = 1.0 - ADAM_B2 ** ADAM_STEP


def _adamw_math(w, g, m, v):
    m = ADAM_B1 * m + (1.0 - ADAM_B1) * g
    v = ADAM_B2 * v + (1.0 - ADAM_B2) * (g * g)
    m_hat = m / _C1
    v_hat = v / _C2
    delta = -ADAM_LR * (m_hat / (jnp.sqrt(v_hat) + ADAM_EPS) + ADAM_WD * w)
    return delta, m, v


def _adamw_sharded(terms, w, m, v, name, tr):
    rows, cols = w.shape

    def body(t_ref, w_ref, m_ref, v_ref, g_ref, d_ref, nm_ref, nv_ref):
        g = t_ref[0].astype(F32)
        for s in range(1, N_CHIPS):
            g = g + t_ref[s].astype(F32)
        g_ref[...] = g
        d_ref[...], nm_ref[...], nv_ref[...] = _adamw_math(w_ref[...], g, m_ref[...], v_ref[...])

    blk = pl.BlockSpec((tr, cols), lambda i: (i, 0))
    return pl.pallas_call(
        body, name=name, grid=(rows // tr,),
        in_specs=[pl.BlockSpec((N_CHIPS, tr, cols), lambda i: (0, i, 0)), blk, blk, blk],
        out_specs=[blk] * 4,
        out_shape=[jax.ShapeDtypeStruct((rows, cols), F32)] * 4,
        compiler_params=_cp(("parallel",)),
    )(terms, w, m, v)


def _adamw_plain(g, w, m, v, name):
    def body(g_ref, w_ref, m_ref, v_ref, d_ref, nm_ref, nv_ref):
        d_ref[...], nm_ref[...], nv_ref[...] = _adamw_math(w_ref[...], g_ref[...], m_ref[...], v_ref[...])

    return pl.pallas_call(
        body, name=name, in_specs=[_VMEM_WHOLE] * 4, out_specs=[_VMEM_WHOLE] * 3,
        out_shape=[jax.ShapeDtypeStruct(w.shape, F32)] * 3,
        compiler_params=_cp(),
    )(g, w, m, v)


SMALL_ROWS = 16
R_DMOD, R_DCTX, R_NMIX, R_NFFN, R_MISC, R_DLB, R_BADA01 = 0, 6, 8, 9, 10, 11, 13
M_HNW, M_QNW, M_KNW, M_SINK, M_LOSS = 0, 128, 256, 384, 512


def _pack_small(acc_in, acc_mg, acc_ffn, acc_prep, dsink, dlb):
    def body(in_ref, mg_ref, ff_ref, pp_ref, ds_ref, dlb_ref, o_ref):
        o_ref[...] = jnp.zeros_like(o_ref)
        o_ref[0:2, :] = in_ref[0:2, :]
        o_ref[2:3, :] = mg_ref[0:1, :]
        o_ref[3:5, :] = ff_ref[0:2, :]
        o_ref[5:6, :] = ff_ref[3:4, :]
        o_ref[6:8, :] = in_ref[3:5, :]
        o_ref[8:9, :] = in_ref[2:3, :]
        o_ref[9:10, :] = ff_ref[2:3, :]
        o_ref[10:11, M_HNW:M_HNW + HG_DIM] = pp_ref[0:1, 0:HG_DIM]
        r = lax.broadcasted_iota(jnp.int32, (ATW, 128), 0)
        c = lax.broadcasted_iota(jnp.int32, (ATW, 128), 1)
        fold = jnp.where((r % HEAD_DIM == c) & (c < HEAD_DIM), 1.0, 0.0).astype(BF16)
        qk = jnp.concatenate([pp_ref[1:2, :], pp_ref[2:3, :], jnp.zeros((6, ATW), F32)], axis=0)
        folded = _dot_exact_rhs01(qk, fold)
        o_ref[10:11, M_QNW:M_QNW + 128] = folded[0:1, :]
        o_ref[10:11, M_KNW:M_KNW + 128] = folded[1:2, :]
        o_ref[10:11, M_SINK:M_SINK + ATT_HEADS] = ds_ref[...]
        o_ref[10:11, M_LOSS:M_LOSS + 128] = ff_ref[4:5, 0:128]
        o_ref[11:13, 0:HGW] = dlb_ref[...]

    return pl.pallas_call(
        body, name="pack_small", in_specs=[_VMEM_WHOLE] * 6, out_specs=_VMEM_WHOLE,
        out_shape=jax.ShapeDtypeStruct((SMALL_ROWS, D), F32), compiler_params=_cp(),
    )(acc_in, acc_mg, acc_ffn, acc_prep, dsink, dlb)


def _sum_small(gathered):
    def body(g_ref, o_ref):
        tot = g_ref[0]
        for s in range(1, N_DEV):
            tot = tot + g_ref[s]
        o_ref[...] = tot
        o_ref[R_BADA01:R_BADA01 + 2, :] = tot[0:2, :] + tot[R_DCTX:R_DCTX + 2, :]

    return pl.pallas_call(
        body, name="sum_small", in_specs=[_VMEM_WHOLE], out_specs=_VMEM_WHOLE,
        out_shape=jax.ShapeDtypeStruct((SMALL_ROWS, D), F32), compiler_params=_cp(),
    )(gathered)


_REP_NAMES = ("b_ada", "c_ctx", "norm_mix_w", "norm_ffn_w", "hgrn_norm_w", "q_norm_w", "k_norm_w", "attn_sinks")


def _adamw_replicated(tot, g_c_ctx, ws, ms, vs):
    n = len(_REP_NAMES)

    def body(*refs):
        tot_ref, gc_ref = refs[0], refs[1]
        w_refs, m_refs, v_refs = refs[2:2 + n], refs[2 + n:2 + 2 * n], refs[2 + 2 * n:2 + 3 * n]
        outs = refs[2 + 3 * n:]
        row = lambda r: tot_ref[r:r + 1, :]
        misc = row(R_MISC)
        grads = [jnp.concatenate([row(R_BADA01), row(R_BADA01 + 1)] + [row(k) for k in range(2, 6)], axis=1),
                 gc_ref[...], row(R_NMIX), row(R_NFFN),
                 misc[:, M_HNW:M_HNW + HG_DIM], misc[:, M_QNW:M_QNW + HEAD_DIM],
                 misc[:, M_KNW:M_KNW + HEAD_DIM], misc[:, M_SINK:M_SINK + ATT_HEADS]]
        for k in range(n):
            outs[k][...] = grads[k]
            outs[n + k][...], outs[2 * n + k][...], outs[3 * n + k][...] = _adamw_math(
                w_refs[k][...], grads[k], m_refs[k][...], v_refs[k][...])

    shapes = [jax.ShapeDtypeStruct(w.shape, F32) for w in ws]
    return pl.pallas_call(
        body, name="adamw_replicated", in_specs=[_VMEM_WHOLE] * (2 + 3 * n), out_specs=[_VMEM_WHOLE] * (4 * n),
        out_shape=shapes * 4, compiler_params=_cp(),
    )(tot, g_c_ctx, *ws, *ms, *vs)


def _lb_grads(dlb, lbl):
    def body(d_ref, l_ref, o_ref):
        for d in (0, 1):
            ll = l_ref[d]
            lb = _sigmoid(ll[0:1, :] - ll[1:2, :])
            t = d_ref[d:d + 1, :] * lb * (1.0 - lb)
            o_ref[d, 0:1, :] = t
            o_ref[d, 1:2, :] = -t

    return pl.pallas_call(
        body, name="lb_grads", in_specs=[_VMEM_WHOLE] * 2, out_specs=_VMEM_WHOLE,
        out_shape=jax.ShapeDtypeStruct((2, 2, HGW), F32), compiler_params=_cp(),
    )(dlb, lbl)


def _c_ctx_grad(terms, c_ctx):
    def body(t_ref, c_ref, o_ref):
        tot = t_ref[0, 8:9, :]
        for s in range(1, N_DEV):
            tot = tot + t_ref[s, 8:9, :]
        cv = c_ref[...]
        sg = _sigmoid(cv)
        o_ref[...] = tot * (sg * (1.0 + cv * (1.0 - sg)))

    return pl.pallas_call(
        body, name="c_ctx_grad", in_specs=[_VMEM_WHOLE] * 2, out_specs=_VMEM_WHOLE,
        out_shape=jax.ShapeDtypeStruct((1, D), F32), compiler_params=_cp(),
    )(terms, c_ctx)


def _in_perm():
    fz, bz, inp, kk, vv, qhg, ghg, qat, gates = 0, 512, 1024, 1536, 1664, 1792, 2304, 2816, 3328
    cols = []
    for h in range(HG_HEADS):
        for base in (fz, bz, inp, qhg):
            cols += list(range(base + 128 * h, base + 128 * (h + 1)))
    cols += list(range(ghg, ghg + 512)) + list(range(qat, qat + 512))
    cols += list(range(kk, kk + 128)) + list(range(vv, vv + 128))
    cols += list(range(gates, gates + 2048))
    return np.asarray(cols, np.int32)


_PERM = _in_perm()
_INV_PERM = np.argsort(_PERM).astype(np.int32)


def _take_rows(w, perm):
    cuts = [0] + [i for i in range(1, len(perm)) if perm[i] != perm[i - 1] + 1] + [len(perm)]
    return jnp.concatenate([w[int(perm[a]):int(perm[b - 1]) + 1] for a, b in zip(cuts[:-1], cuts[1:])],
                           axis=0)


def _cols_from_blocks(g):
    return jnp.transpose(g, (1, 0, 2)).reshape(g.shape[1], N_DEV * g.shape[2])


def _local_step(x2, ctx2, tgt, lbl, sh_in, sc_in, gate1, sh2, sc2, gate2, norm_mix_w, norm_ffn_w,
                hgrn_norm_w, q_norm_w, k_norm_w, attn_sinks, w_a, w_b, w_c, s_bh, s_ba, s_out,
                s_gate, s_up, s_down):
    first_last = lambda n: [(0, True), (n - 1, False)]
    h_all = _norm_mod_all(ctx2, x2, norm_mix_w, sh_in, sc_in)
    p_a = _mm_nt(h_all, w_a, tm=768, tn=1024, out_dtype=F32, name="proj_a")
    (o, st), (g_gate,) = _hgrn_fwd(
        p_a, lbl, (_gather_comm([s_gate]), [(0, True), (HG_HEADS - 2, True), (HG_HEADS - 1, False)]))
    (p_b, p_c), (g_bh, g_ba, g_out) = _proj_bc(
        h_all, w_b, w_c, (_gather_comm([s_bh, s_ba, s_out]), [(0, True), (N_TILES - 2, True), (N_TILES - 1, False)]))
    cos, sin = _rope_tables()
    qnw_t, knw_t = jnp.tile(q_norm_w, (1, ATT_HEADS)), jnp.tile(k_norm_w, (1, KV_HEADS))
    y_hg, qn, k_rep, v_rep = _prep_fwd(p_b, o, cos, sin, hgrn_norm_w, qnw_t, knw_t)
    (y_at, lse), (g_up, g_down) = _attn_fwd(
        qn, k_rep, v_rep, attn_sinks,
        (_gather_comm([s_up, s_down]), [(0, True), (N_BLOCKS - 3, True), (N_BLOCKS - 1, False)]))
    w_bh, w_ba, w_o = g_bh.reshape(D, HGW), g_ba.reshape(D, ATW), g_out.reshape(D, D)
    g_gate, g_up, g_down = [g.reshape(N_FF_TILES, FF_TILE, D) for g in (g_gate, g_up, g_down)]
    a, b, mixed, r, x1, h2 = _merge_fwd(y_hg, y_at, p_c, x2, w_bh, w_ba, w_o, gate1, norm_ffn_w, sh2, sc2)

    act, d_gate, d_up, d_f, dx1, acc_ffn = _ffn_fused(x1, h2, tgt, g_gate, g_up, g_down, gate2,
                                                      norm_ffn_w, sc2)
    by_chip = lambda t: t.reshape((N_CHIPS, 2) + t.shape[1:])
    ff_by_chip = lambda t: t.reshape(N_CHIPS, 2, FF_BLK, D)
    t_down, _ = _mm_tn_blocked(act, d_f, "grad_down")
    t_down = ff_by_chip(t_down)
    t_gate, (f_down,) = _mm_tn_blocked(d_gate, h2, "grad_gate", (_sibling_comm([t_down]), first_last(N_FF_TILES)))
    t_gate = ff_by_chip(t_gate)
    t_up, (f_gate,) = _mm_tn_blocked(d_up, h2, "grad_up", (_sibling_comm([t_gate]), first_last(N_FF_TILES)))
    t_up = ff_by_chip(t_up)

    (d_r, d_a, d_b, dp_c, dy_hg, dy_at, acc_mg), (f_up,) = _merge_bwd(
        dx1, r, a, b, p_c, w_bh, w_ba, w_o, gate1, (_sibling_comm([t_up]), first_last(N_LAT_TILES)))
    c_down, c_gate, c_up = [_pair_sum(t, f, "pair_sum_" + nm) for t, f, nm in
                            ((t_down, f_down, "down"), (t_gate, f_gate, "gate"), (t_up, f_up, "up"))]
    t_out = _mm_tn(mixed, d_r, tk=512, nk=4, tm=512, tn=1024, out_dtype=BF16, name="grad_out")
    t_bh = _mm_tn(d_a, y_hg, tk=512, nk=4, tm=512, tn=512, out_dtype=BF16, name="grad_bh")
    t_ba = _mm_tn(d_b, y_at, tk=512, nk=4, tm=512, tn=512, out_dtype=BF16, name="grad_ba")
    t_bh, t_ba, t_out = [by_chip(t.reshape(N_DEV, D // N_DEV, t.shape[1])) for t in (t_bh, t_ba, t_out)]
    (dq, dk_rep, dv_rep, dsink), (r_up,) = _attn_bwd(
        qn, k_rep, v_rep, attn_sinks, y_at, lse, dy_at, (_chip_comm([c_up]), first_last(N_BLOCKS)))
    (dp_b, d_o, acc_prep), (f_bh, f_ba, f_out) = _prep_bwd(
        p_b, o, cos, sin, hgrn_norm_w, qnw_t, knw_t, dy_hg, dq, dk_rep, dv_rep,
        (_sibling_comm([t_bh, t_ba, t_out]), first_last(N_TILES)))
    c_bh, c_ba, c_out = [_pair_sum(t, f, "pair_sum_" + nm) for t, f, nm in
                         ((t_bh, f_bh, "bh"), (t_ba, f_ba, "ba"), (t_out, f_out, "out"))]
    (dp_a, dlb), (r_bh, r_ba, r_out, r_down, r_gate) = _hgrn_bwd(
        p_a, lbl, d_o, st, (_chip_comm([c_bh, c_ba, c_out, c_down, c_gate]), first_last(HG_HEADS)))
    t_a = _mm_tn(dp_a, h_all, tk=768, nk=3, tm=1024, tn=1024, out_dtype=BF16, name="grad_in_a")
    t_b = _mm_tn(dp_b, h_all, tk=768, nk=3, tm=640, tn=1024, out_dtype=BF16, name="grad_in_b")
    t_c = _mm_tn(dp_c, h_all, tk=256, nk=8, b_off=1, tm=1024, tn=1024, out_dtype=BF16, name="grad_in_c")
    t_in = by_chip(_take_rows(jnp.concatenate([t_a, t_b, t_c], axis=0), _INV_PERM).reshape(N_DEV, IN_BLK, D))
    (f_in,) = _run_comm(_sibling_comm([t_in]), "scatter_in_sibling")
    (grad_x, acc_in), (r_in,) = _input_bwd(
        dp_a, dp_b, dp_c, w_a, w_b, w_c, ctx2, x2, dx1, norm_mix_w, sh_in, sc_in,
        (_chip_comm([_pair_sum(t_in, f_in, "pair_sum_in")]), first_last(N_TILES)))
    small = _pack_small(acc_in, acc_mg, acc_ffn, acc_prep, dsink, dlb)
    return grad_x, small, [r_in, r_bh, r_ba, r_out, r_gate, r_up, r_down]


def kernel(x, c, ctx, c_ctx, w_ada, b_ada, norm_mix_w, norm_ffn_w, w_in, hgrn_lb_logits, hgrn_norm_w, q_norm_w, k_norm_w, attn_sinks, w_branch_hgrn, w_branch_attn, w_out, w_ffn_gate, w_ffn_up, w_ffn_down, loss_target, m_c_ctx, m_w_ada, m_b_ada, m_norm_mix_w, m_norm_ffn_w, m_w_in, m_hgrn_lb_logits, m_hgrn_norm_w, m_q_norm_w, m_k_norm_w, m_attn_sinks, m_w_branch_hgrn, m_w_branch_attn, m_w_out, m_w_ffn_gate, m_w_ffn_up, m_w_ffn_down, v_c_ctx, v_w_ada, v_b_ada, v_norm_mix_w, v_norm_ffn_w, v_w_in, v_hgrn_lb_logits, v_hgrn_norm_w, v_q_norm_w, v_k_norm_w, v_attn_sinks, v_w_branch_hgrn, v_w_branch_attn, v_w_out, v_w_ffn_gate, v_w_ffn_up, v_w_ffn_down):
    me = 4 * lax.axis_index("x") + 2 * lax.axis_index("y") + lax.axis_index("c")
    x2, ctx2, tgt = x[0], ctx[0], loss_target[0]
    w_ada2, w_in2 = w_ada[0], w_in[0]

    blk = jnp.zeros((8, D), F32).at[0].set(c[0]).at[1, :256].set(hgrn_lb_logits.reshape(256))
    (g0,) = _all_gather([blk], "gather_cond", True)
    cc = jnp.zeros((16, D), F32).at[:8].set(g0[:, 0, :]).at[8].set(c_ctx)
    lbl = jnp.transpose(g0[:, 1, :256].reshape(N_DEV, 2, 2, 64), (1, 2, 0, 3)).reshape(2, 2, HGW)

    b_cols = lax.dynamic_slice(b_ada, (0, me * ADA_BLK), (1, ADA_BLK))
    (g1,) = _all_gather([_ada_rows(cc, w_ada2, b_cols)], "gather_mod", True)
    mod_all = _cols_from_blocks(g1)
    mod = lax.dynamic_slice(mod_all, (me, 0), (1, 6 * D)).reshape(6, D)
    mod_c = mod_all[8].reshape(6, D)
    sh1, sc1, gate1, sh2, sc2, gate2 = [mod[k:k + 1] for k in range(6)]
    sh_in = jnp.concatenate([mod_c[0:1], sh1], axis=0)
    sc_in = jnp.concatenate([mod_c[1:2], sc1], axis=0)

    shards = [w_branch_hgrn[0].T, w_branch_attn[0].T, w_out[0], w_ffn_gate[0].T, w_ffn_up[0].T, w_ffn_down[0]]
    (g_in,) = _all_gather([w_in2.T.astype(BF16)], "gather_w_in", False)
    w_in_t = _take_rows(g_in.reshape(IN_COLS, D), _PERM)
    w_a, w_b, w_c = w_in_t[:WA], w_in_t[WA:WA + WB], w_in_t[WA + WB:]

    grad_x, small, (r_in, r_bh, r_ba, r_out, r_gate, r_up, r_down) = _local_step(
        x2, ctx2, tgt, lbl, sh_in, sc_in, gate1, sh2, sc2, gate2, norm_mix_w, norm_ffn_w, hgrn_norm_w,
        q_norm_w, k_norm_w, attn_sinks, w_a, w_b, w_c, *[s.astype(BF16) for s in shards])

    big = {}
    for nm, rr, ww, mm, vv, tr, transposed in (
            ("w_in", r_in, w_in2, m_w_in[0], v_w_in[0], 336, True),
            ("w_branch_hgrn", r_bh, w_branch_hgrn[0], m_w_branch_hgrn[0], v_w_branch_hgrn[0], 128, True),
            ("w_branch_attn", r_ba, w_branch_attn[0], m_w_branch_attn[0], v_w_branch_attn[0], 128, True),
            ("w_out", r_out, w_out[0], m_w_out[0], v_w_out[0], 128, False),
            ("w_ffn_gate", r_gate, w_ffn_gate[0], m_w_ffn_gate[0], v_w_ffn_gate[0], 352, True),
            ("w_ffn_up", r_up, w_ffn_up[0], m_w_ffn_up[0], v_w_ffn_up[0], 352, True),
            ("w_ffn_down", r_down, w_ffn_down[0], m_w_ffn_down[0], v_w_ffn_down[0], 352, False)):
        if transposed:
            res = _adamw_sharded(rr, ww.T, mm.T, vv.T, "adamw_" + nm, tr)
            big[nm] = [t.T[None] for t in res]
        else:
            big[nm] = [t[None] for t in _adamw_sharded(rr, ww, mm, vv, "adamw_" + nm, tr)]

    (g2,) = _all_gather([small], "gather_small", True)
    tot = _sum_small(g2)
    dm = jnp.zeros((16, 6 * D), F32).at[:8].set(g2[:, R_DMOD:R_DMOD + 6, :].reshape(N_DEV, 6 * D))
    dm = dm.at[8, :2 * D].set(tot[R_DCTX:R_DCTX + 2].reshape(2 * D))
    dm_cols = lax.dynamic_slice(dm, (0, me * ADA_BLK), (16, ADA_BLK))
    g_w_ada, dsc_term = _ada_grads(cc, dm_cols, w_ada2)
    (g3,) = _all_gather([dsc_term], "gather_cctx", True)
    g_c_ctx = _c_ctx_grad(g3, c_ctx.reshape(1, D))
    g_lbl = _lb_grads(tot[R_DLB:R_DLB + 2, :HGW], lbl)
    g_lb_mine = lax.dynamic_slice(g_lbl, (0, 0, me * 64), (2, 2, 64))
    misc = tot[R_MISC]
    loss = misc[M_LOSS]

    rep_out = _adamw_replicated(
        tot, g_c_ctx,
        [b_ada, c_ctx.reshape(1, D), norm_mix_w, norm_ffn_w, hgrn_norm_w, q_norm_w, k_norm_w, attn_sinks],
        [m_b_ada, m_c_ctx.reshape(1, D), m_norm_mix_w, m_norm_ffn_w, m_hgrn_norm_w, m_q_norm_w, m_k_norm_w,
         m_attn_sinks],
        [v_b_ada, v_c_ctx.reshape(1, D), v_norm_mix_w, v_norm_ffn_w, v_hgrn_norm_w, v_q_norm_w, v_k_norm_w,
         v_attn_sinks])
    rep = []
    for kind in range(4):
        vals = dict(zip(_REP_NAMES, rep_out[kind * len(_REP_NAMES):(kind + 1) * len(_REP_NAMES)]))
        vals["c_ctx"] = vals["c_ctx"].reshape(D)
        rep.append(vals)

    d_ada, nm_ada, nv_ada = _adamw_plain(g_w_ada, w_ada2, m_w_ada[0], v_w_ada[0], "adamw_w_ada")
    ada = [t[None] for t in (g_w_ada, d_ada, nm_ada, nv_ada)]
    lb_w = hgrn_lb_logits.reshape(4, 64)
    d_lb, nm_lb, nv_lb = _adamw_plain(g_lb_mine.reshape(4, 64), lb_w, m_hgrn_lb_logits.reshape(4, 64),
                                      v_hgrn_lb_logits.reshape(4, 64), "adamw_lb")
    lbs = [t.reshape(2, 2, 64) for t in (g_lb_mine, d_lb, nm_lb, nv_lb)]

    names = ['c_ctx', 'w_ada', 'b_ada', 'norm_mix_w', 'norm_ffn_w', 'w_in', 'hgrn_lb_logits', 'hgrn_norm_w',
             'q_norm_w', 'k_norm_w', 'attn_sinks', 'w_branch_hgrn', 'w_branch_attn', 'w_out', 'w_ffn_gate',
             'w_ffn_up', 'w_ffn_down']
    outs = [loss, grad_x[None]]
    for kind in range(4):
        for nm in names:
            if nm == 'w_ada':
                outs.append(ada[kind])
            elif nm == 'hgrn_lb_logits':
                outs.append(lbs[kind])
            elif nm in big:
                outs.append(big[nm][kind])
            else:
                outs.append(rep[kind][nm])
    return tuple(outs)
```

```python
import functools
import math

import numpy as np
import jax
import jax.numpy as jnp
from jax import lax
from jax.experimental import pallas as pl
from jax.experimental.pallas import tpu as pltpu

F32 = jnp.float32
BF16 = jnp.bfloat16

N_DEV = 8
D = 1024
S = 2048
L = 256
T = L + S
TM = 256
N_TILES = T // TM
N_LAT_TILES = S // TM
HG_HEADS = 4
HG_DIM = 128
HGW = 512
CHUNK = 32
N_CHUNKS = T // CHUNK
N_CTX_CHUNKS = L // CHUNK
N_LAT_CHUNKS = S // CHUNK
ATT_HEADS = 8
KV_HEADS = 2
HEAD_DIM = 64
ATW = 512
KVW = 128
BLOCK = 128
N_BLOCKS = S // BLOCK
GRID_W = 64
ROPE_THETA = 10000.0
D_FF = 2816
FF_BLK = D_FF // N_DEV
FF_TILE = 256
N_FF_TILES = D_FF // FF_TILE
IN_COLS = 5376
IN_BLK = IN_COLS // N_DEV
ADA_BLK = 6 * D // N_DEV
EPS = 1e-6
WA, WB, WC = 2048, 1280, 2048

ADAM_LR = 0.001
ADAM_B1 = 0.9
ADAM_B2 = 0.999
ADAM_EPS = 1e-08
ADAM_WD = 0.01
ADAM_STEP = 10

VMEM_LIMIT = 56 * 1024 * 1024
MESH = pl.DeviceIdType.MESH


def _cp(sem=None, vmem=VMEM_LIMIT):
    return pltpu.CompilerParams(dimension_semantics=sem, vmem_limit_bytes=vmem)


def _full(shape):
    n = len(shape)
    return pl.BlockSpec(shape, lambda *_: (0,) * n)


_VMEM_WHOLE = pl.BlockSpec(memory_space=pltpu.VMEM)
_ANY = pl.BlockSpec(memory_space=pl.ANY)


def _sigmoid(v):
    return 1.0 / (1.0 + jnp.exp(-v))


def _dot(a, b):
    return jnp.dot(a, b, preferred_element_type=F32)


def _dot_nt(a, b):
    return lax.dot_general(a, b, (((1,), (1,)), ((), ())), preferred_element_type=F32)


def _dot_tn(a, b):
    return lax.dot_general(a, b, (((0,), (0,)), ((), ())), preferred_element_type=F32)


def _split3(v):
    hi = v.astype(BF16)
    r = v - hi.astype(F32)
    mid = r.astype(BF16)
    lo = (r - mid.astype(F32)).astype(BF16)
    return hi, mid, lo


def _dot_exact_rhs01(v, m01):
    hi, mid, lo = _split3(v)
    return _dot(hi, m01) + _dot(mid, m01) + _dot(lo, m01)


def _split2(v):
    hi = v.astype(BF16)
    return hi, (v - hi.astype(F32)).astype(BF16)


def _dot_lhs01(m01, v):
    hi, lo = _split2(v)
    return _dot(m01, hi) + _dot(m01, lo)


def _dot_f32(a, b, dot=_dot):
    ah, am, al = _split3(a)
    bh, bm, bl = _split3(b)
    return (dot(ah, bh) + (dot(ah, bm) + dot(am, bh))
            + (dot(am, bm) + dot(ah, bl) + dot(al, bh)))


def _my_pos():
    return lax.axis_index("x"), lax.axis_index("y"), lax.axis_index("c")


class _Comm:
    def __init__(self, operands, out_shapes, sems, phases):
        self.operands, self.out_shapes, self.sems, self.phases = operands, out_shapes, sems, phases


def _gather_comm(blocks):
    n = len(blocks)

    def parts(ins, outs, sems):
        send_sems, recv_sems, local_sems = sems
        x, y, c = _my_pos()
        me, sibling = (x, y, c), (x, y, 1 - c)
        chips = [(1 - x, y), (x, 1 - y), (1 - x, 1 - y)]

        def slot(a, px, py, pc):
            return outs[a].at[4 * px + 2 * py + pc]

        def copy(a, k, block, to, src=None):
            return pltpu.make_async_remote_copy(
                src_ref=slot(a, *block) if src is None else src, dst_ref=slot(a, *block),
                send_sem=send_sems.at[a, k], recv_sem=recv_sems.at[a, k],
                device_id=to, device_id_type=MESH)

        mine = [pltpu.make_async_copy(ins[a], slot(a, *me), local_sems.at[a]) for a in range(n)]
        first = []
        for a in range(n):
            first.append(copy(a, 0, me, sibling, src=ins[a]))
            first += [copy(a, 1 + j, me, (*chip, c), src=ins[a]) for j, chip in enumerate(chips)]
        passed = [copy(a, 4 + j, (*chip, c), sibling) for j, chip in enumerate(chips) for a in range(n)]
        return c, me, sibling, chips, copy, mine, first, passed

    def start(ins, outs, sems):
        _, _, _, _, _, mine, first, _ = parts(ins, outs, sems)
        for cp in mine + first:
            cp.start()

    def forward(ins, outs, sems):
        c, me, _, chips, copy, _, _, passed = parts(ins, outs, sems)
        for j, chip in enumerate(chips):
            for a in range(n):
                copy(a, 1 + j, (*chip, c), me).wait_recv()
                passed[j * n + a].start()

    def finish(ins, outs, sems):
        c, me, sibling, chips, copy, mine, first, passed = parts(ins, outs, sems)
        for a in range(n):
            copy(a, 0, sibling, me).wait_recv()
            for j, chip in enumerate(chips):
                copy(a, 4 + j, (*chip, 1 - c), me).wait_recv()
        for cp in first + passed:
            cp.wait_send()
        for cp in mine:
            cp.wait()

    return _Comm(blocks, [jax.ShapeDtypeStruct((N_DEV,) + b.shape, b.dtype) for b in blocks],
                 [pltpu.SemaphoreType.DMA((n, 7)), pltpu.SemaphoreType.DMA((n, 7)), pltpu.SemaphoreType.DMA((n,))],
                 [start, forward, finish])


def _run_comm(comm, name, in_vmem=False):
    n_in, n_out = len(comm.operands), len(comm.out_shapes)

    def body(*refs):
        ins, outs, sems = refs[:n_in], refs[n_in:n_in + n_out], refs[n_in + n_out:]
        for phase in comm.phases:
            phase(ins, outs, sems)

    spec = _VMEM_WHOLE if in_vmem else _ANY
    return pl.pallas_call(
        body, name=name, out_shape=comm.out_shapes, in_specs=[spec] * n_in, out_specs=[spec] * n_out,
        scratch_shapes=comm.sems,
    )(*comm.operands)


def _carrier_call(body, comm, schedule, *, name, grid, in_specs, out_specs, out_shape, scratch_shapes, operands):
    n_in, n_out, n_scr = len(in_specs), len(out_specs), len(scratch_shapes)
    c_in, c_out = len(comm.operands), len(comm.out_shapes)

    def full_body(*refs):
        ins, refs = refs[:n_in], refs[n_in:]
        cins, refs = refs[:c_in], refs[c_in:]
        outs, refs = refs[:n_out], refs[n_out:]
        couts, refs = refs[:c_out], refs[c_out:]
        scr, csems = refs[:n_scr], refs[n_scr:]
        step = pl.program_id(0)

        def run(before):
            for (at, when_before), phase in zip(schedule, comm.phases):
                if when_before == before:
                    pl.when(step == at)(functools.partial(phase, cins, couts, csems))

        run(True)
        body(*ins, *outs, *scr)
        run(False)

    res = pl.pallas_call(
        full_body, name=name, grid=grid,
        in_specs=list(in_specs) + [_ANY] * c_in, out_specs=list(out_specs) + [_ANY] * c_out,
        out_shape=list(out_shape) + list(comm.out_shapes),
        scratch_shapes=list(scratch_shapes) + list(comm.sems),
        compiler_params=_cp(("arbitrary",)),
    )(*operands, *comm.operands)
    return res[:n_out], res[n_out:]


def _pcall(body, carried, *, name, grid, in_specs, out_specs, out_shape, scratch_shapes, operands):
    if carried is None:
        res = pl.pallas_call(body, name=name, grid=grid, in_specs=in_specs, out_specs=out_specs,
                             out_shape=out_shape, scratch_shapes=scratch_shapes,
                             compiler_params=_cp(("arbitrary",)))(*operands)
        return res, ()
    return _carrier_call(body, carried[0], carried[1], name=name, grid=grid, in_specs=in_specs,
                         out_specs=out_specs, out_shape=out_shape, scratch_shapes=scratch_shapes,
                         operands=operands)


def _all_gather(blocks, name, in_vmem):
    return _run_comm(_gather_comm(blocks), name, in_vmem)


N_CHIPS = 4


def _sibling_comm(contribs):
    n = len(contribs)

    def copies(ins, outs, sems):
        send_sems, recv_sems = sems
        x, y, c = _my_pos()
        return [pltpu.make_async_remote_copy(
            src_ref=ins[a].at[pl.ds(0, N_CHIPS), 1 - c], dst_ref=outs[a],
            send_sem=send_sems.at[a], recv_sem=recv_sems.at[a],
            device_id=(x, y, 1 - c), device_id_type=MESH) for a in range(n)]

    def start(ins, outs, sems):
        for cp in copies(ins, outs, sems):
            cp.start()

    def finish(ins, outs, sems):
        cps = copies(ins, outs, sems)
        for cp in cps:
            cp.wait_recv()
        for cp in cps:
            cp.wait_send()

    return _Comm(contribs, [jax.ShapeDtypeStruct((N_CHIPS,) + b.shape[2:], b.dtype) for b in contribs],
                 [pltpu.SemaphoreType.DMA((n,)), pltpu.SemaphoreType.DMA((n,))], [start, finish])


def _pair_sum(mine, theirs, name):
    _, _, rows, cols = mine.shape
    core = lax.axis_index("c").astype(jnp.int32).reshape(1)

    def body(c_ref, m_ref, t_ref, o_ref):
        o_ref[...] = (m_ref[...].astype(F32) + t_ref[...].astype(F32)).astype(BF16)

    return pl.pallas_call(
        body, name=name,
        grid_spec=pltpu.PrefetchScalarGridSpec(
            num_scalar_prefetch=1, grid=(N_CHIPS,),
            in_specs=[pl.BlockSpec((None, None, rows, cols), lambda q, c: (q, c[0], 0, 0)),
                      pl.BlockSpec((None, rows, cols), lambda q, c: (q, 0, 0))],
            out_specs=pl.BlockSpec((None, rows, cols), lambda q, c: (q, 0, 0))),
        out_shape=jax.ShapeDtypeStruct((N_CHIPS, rows, cols), BF16),
        compiler_params=_cp(("parallel",)),
    )(core, mine, theirs)


def _chip_comm(sums):
    n = len(sums)

    def parts(ins, outs, sems):
        send_sems, recv_sems, local_sems = sems
        x, y, c = _my_pos()
        q_me = 2 * x + y
        chips = [(1 - x, y), (x, 1 - y), (1 - x, 1 - y)]
        mine = [pltpu.make_async_copy(ins[a].at[q_me], outs[a].at[q_me], local_sems.at[a]) for a in range(n)]
        sends, recvs = [], []
        for j, (px, py) in enumerate(chips):
            for a in range(n):
                q = 2 * px + py
                sends.append(pltpu.make_async_remote_copy(
                    src_ref=ins[a].at[q], dst_ref=outs[a].at[q_me],
                    send_sem=send_sems.at[a, j], recv_sem=recv_sems.at[a, j],
                    device_id=(px, py, c), device_id_type=MESH))
                recvs.append(pltpu.make_async_remote_copy(
                    src_ref=ins[a].at[q], dst_ref=outs[a].at[q],
                    send_sem=send_sems.at[a, j], recv_sem=recv_sems.at[a, j],
                    device_id=(x, y, c), device_id_type=MESH))
        return mine, sends, recvs

    def start(ins, outs, sems):
        mine, sends, _ = parts(ins, outs, sems)
        for cp in mine + sends:
            cp.start()

    def finish(ins, outs, sems):
        mine, sends, recvs = parts(ins, outs, sems)
        for cp in recvs:
            cp.wait_recv()
        for cp in sends:
            cp.wait_send()
        for cp in mine:
            cp.wait()

    return _Comm(sums, [jax.ShapeDtypeStruct(b.shape, b.dtype) for b in sums],
                 [pltpu.SemaphoreType.DMA((n, 3)), pltpu.SemaphoreType.DMA((n, 3)), pltpu.SemaphoreType.DMA((n,))],
                 [start, finish])


def _mm_nt(a, bt, *, tm, tn, out_dtype, name, row_off=0, rows=None):
    rows = a.shape[0] if rows is None else rows
    n, k = bt.shape

    def body(a_ref, b_ref, o_ref):
        o_ref[...] = _dot_nt(a_ref[...], b_ref[...]).astype(out_dtype)

    return pl.pallas_call(
        body, name=name, grid=(rows // tm, n // tn),
        in_specs=[pl.BlockSpec((tm, k), lambda i, j: (i + row_off, 0)),
                  pl.BlockSpec((tn, k), lambda i, j: (j, 0))],
        out_specs=pl.BlockSpec((tm, tn), lambda i, j: (i, j)),
        out_shape=jax.ShapeDtypeStruct((rows, n), out_dtype),
        compiler_params=_cp(("parallel", "parallel")),
    )(a, bt)


def _mm_tn(a, b, *, tk, nk, tm, tn, out_dtype, name, a_off=0, b_off=0):
    m, n = a.shape[1], b.shape[1]

    def body(a_ref, b_ref, o_ref, acc):
        kk = pl.program_id(2)

        @pl.when(kk == 0)
        def _():
            acc[...] = jnp.zeros_like(acc)

        acc[...] += _dot_tn(a_ref[...], b_ref[...])

        @pl.when(kk == nk - 1)
        def _():
            o_ref[...] = acc[...].astype(out_dtype)

    return pl.pallas_call(
        body, name=name, grid=(m // tm, n // tn, nk),
        in_specs=[pl.BlockSpec((tk, tm), lambda i, j, kk: (kk + a_off, i)),
                  pl.BlockSpec((tk, tn), lambda i, j, kk: (kk + b_off, j))],
        out_specs=pl.BlockSpec((tm, tn), lambda i, j, kk: (i, j)),
        out_shape=jax.ShapeDtypeStruct((m, n), out_dtype),
        scratch_shapes=[pltpu.VMEM((tm, tn), F32)],
        compiler_params=_cp(("parallel", "parallel", "arbitrary")),
    )(a, b)


def _mm_tn_blocked(a, b, name, carried=None):
    nb, _, w = a.shape
    n = b.shape[1]

    def body(a_ref, b_ref, o_ref):
        o_ref[...] = _dot_tn(a_ref[...], b_ref[...]).astype(BF16)

    (out,), extra = _pcall(
        body, carried, name=name, grid=(nb,),
        in_specs=[pl.BlockSpec((None, S, w), lambda j: (j, 0, 0)), _full((S, n))],
        out_specs=[pl.BlockSpec((None, w, n), lambda j: (j, 0, 0))],
        out_shape=[jax.ShapeDtypeStruct((nb, w, n), BF16)],
        scratch_shapes=[], operands=[a, b])
    return out, extra


def _ada_rows(cc, w_ada, b_cols):
    def body(c_ref, w_ref, b_ref, o_ref):
        cv = c_ref[...]
        o_ref[...] = _dot_f32(cv * _sigmoid(cv), w_ref[...]) + b_ref[...]

    return pl.pallas_call(
        body, name="ada_rows",
        in_specs=[_VMEM_WHOLE] * 3, out_specs=_VMEM_WHOLE,
        out_shape=jax.ShapeDtypeStruct((16, ADA_BLK), F32),
        compiler_params=_cp(),
    )(cc, w_ada, b_cols)


def _ada_grads(cc, dm_cols, w_ada):
    def body(c_ref, dm_ref, w_ref, gw_ref, dsc_ref):
        cv = c_ref[...]
        sc = cv * _sigmoid(cv)
        dm = dm_ref[...]
        gw_ref[...] = _dot_f32(sc, dm, dot=_dot_tn)
        dsc_ref[...] = _dot_f32(dm, w_ref[...], dot=_dot_nt)

    return pl.pallas_call(
        body, name="ada_grads",
        in_specs=[_VMEM_WHOLE] * 3, out_specs=[_VMEM_WHOLE] * 2,
        out_shape=[jax.ShapeDtypeStruct((D, ADA_BLK), F32), jax.ShapeDtypeStruct((16, D), F32)],
        compiler_params=_cp(),
    )(cc, dm_cols, w_ada)


def _lat(i):
    return jnp.maximum(i - 1, 0)


def _rms_mod(xv, nw, sh, sc):
    rstd = lax.rsqrt(jnp.mean(xv * xv, axis=-1, keepdims=True) + EPS)
    return (xv * rstd * nw) * (1.0 + sc) + sh


def _rms_mod_bwd(xv, nw, sc, dh):
    rstd = lax.rsqrt(jnp.mean(xv * xv, axis=-1, keepdims=True) + EPS)
    xhat = xv * rstd
    dn = dh * (1.0 + sc)
    dxhat = dn * nw
    dx = rstd * (dxhat - xhat * jnp.mean(dxhat * xhat, axis=-1, keepdims=True))
    return (dx, jnp.sum(dh, axis=0, keepdims=True), jnp.sum(dh * (xhat * nw), axis=0, keepdims=True),
            jnp.sum(dn * xhat, axis=0, keepdims=True))


def _norm_mod_all(ctx, x, nw, sh, sc):
    def body(ctx_ref, x_ref, nw_ref, sh_ref, sc_ref, o_ref):
        i = pl.program_id(0)
        sel = jnp.minimum(i, 1)
        xv = jnp.where(i == 0, ctx_ref[...], x_ref[...])
        o_ref[...] = _rms_mod(xv, nw_ref[...], sh_ref[pl.ds(sel, 1), :], sc_ref[pl.ds(sel, 1), :]).astype(BF16)

    return pl.pallas_call(
        body, name="norm_mod", grid=(N_TILES,),
        in_specs=[_full((TM, D)), pl.BlockSpec((TM, D), lambda i: (_lat(i), 0)),
                  _full((1, D)), _full((2, D)), _full((2, D))],
        out_specs=pl.BlockSpec((TM, D), lambda i: (i, 0)),
        out_shape=jax.ShapeDtypeStruct((T, D), BF16),
        compiler_params=_cp(("parallel",)),
    )(ctx, x, nw, sh, sc)


def _chunk_masks(reverse):
    row = lax.broadcasted_iota(jnp.int32, (TM, TM), 0)
    col = lax.broadcasted_iota(jnp.int32, (TM, TM), 1)
    same = (row // CHUNK) == (col // CHUNK)
    tri = same & ((col >= row) if reverse else (col <= row))
    return same, tri


def _chunk_order(i, reverse):
    if not reverse:
        return i
    return jnp.where(i < N_CTX_CHUNKS, N_CTX_CHUNKS - 1 - i, N_CHUNKS + N_CTX_CHUNKS - 1 - i)


def _decay_terms(z, lb, same01, tri01):
    f = lb + (1.0 - lb) * _sigmoid(z)
    g = jnp.log(f)
    g2 = jnp.concatenate(_split2(g), axis=1)
    b2 = _dot(tri01, g2)
    t2 = _dot(same01, g2)
    return f, 1.0 - f, b2[:, :HG_DIM] + b2[:, HG_DIM:], t2[:, :HG_DIM] + t2[:, HG_DIM:]


def _chunk_outer(a, b):
    n = TM // CHUNK
    return jnp.einsum('ncv,nck->nvk', a.reshape(n, CHUNK, HG_DIM), b.reshape(n, CHUNK, HG_DIM),
                      preferred_element_type=F32)


def _hgrn_fwd(p_a, lbl, carried=None):
    cpt = TM // CHUNK

    def body(p_ref, lbl_ref, o_ref, st_ref, qd_s, kd_s, u_s, v_s, ebt_s):
        masks = [_chunk_masks(d == 1) for d in (0, 1)]
        same01 = jnp.where(masks[0][0], 1.0, 0.0).astype(BF16)
        tri = [m[1] for m in masks]
        tri01 = [jnp.where(t, 1.0, 0.0).astype(BF16) for t in tri]
        lb = [_sigmoid(lbl_ref[d][0:1, :] - lbl_ref[d][1:2, :]) for d in (0, 1)]

        def prep(r, carry):
            r0 = pl.multiple_of(r * TM, TM)
            vb = p_ref[pl.ds(r0, TM), 2 * HG_DIM:3 * HG_DIM].astype(BF16)
            v_s[pl.ds(r0, TM), :] = vb
            for d in (0, 1):
                z = p_ref[pl.ds(r0, TM), d * HG_DIM:(d + 1) * HG_DIM]
                _, k, b, bt = _decay_terms(z, lb[d], same01, tri01[d])
                u_s[d, pl.ds(r * cpt, cpt)] = _chunk_outer(vb, (k * jnp.exp(bt - b)).astype(BF16))
                ebt_s[d, pl.ds(r0, TM), :] = jnp.exp(bt)

                @pl.when(r >= 1)
                def _():
                    rl = pl.multiple_of(r0 - L, TM)
                    qr = p_ref[pl.ds(r0, TM), 3 * HG_DIM:4 * HG_DIM]
                    q = qr * _sigmoid(qr) * HG_DIM ** -0.5
                    qd_s[d, pl.ds(rl, TM), :] = (q * jnp.exp(b)).astype(BF16)
                    kd_s[d, pl.ds(rl, TM), :] = (k * jnp.exp(-b)).astype(BF16)

            return carry

        lax.fori_loop(0, N_TILES, prep, 0)

        def scan(i, sts):
            new = []
            for d in (0, 1):
                nn = _chunk_order(i, d == 1)
                c0 = pl.multiple_of(nn * CHUNK, CHUNK)
                st_ref[d, nn] = sts[d].astype(BF16)
                new.append(sts[d] * ebt_s[d, pl.ds(c0, 1), :] + u_s[d, nn])
            return tuple(new)

        zero = jnp.zeros((HG_DIM, HG_DIM), F32)
        lax.fori_loop(0, N_CHUNKS, scan, (zero, zero))

        def outp(r, carry):
            r0 = pl.multiple_of(r * TM, TM)
            vb = v_s[pl.ds(r0 + L, TM), :]
            o = jnp.zeros((TM, HG_DIM), F32)
            for d in (0, 1):
                qd = qd_s[d, pl.ds(r0, TM), :]
                a = jnp.where(tri[d], _dot_nt(qd, kd_s[d, pl.ds(r0, TM), :]), 0.0)
                stb = st_ref[d, pl.ds(N_CTX_CHUNKS + r * cpt, cpt)]
                inter = jnp.einsum('nck,nvk->ncv', qd.reshape(cpt, CHUNK, HG_DIM), stb,
                                   preferred_element_type=F32)
                o = o + _dot(a.astype(BF16), vb) + inter.reshape(TM, HG_DIM)
            o_ref[pl.ds(r0, TM), :] = o
            return carry

        lax.fori_loop(0, N_LAT_TILES, outp, 0)

    return _pcall(
        body, carried, name="hgrn_fwd", grid=(HG_HEADS,),
        in_specs=[pl.BlockSpec((T, 4 * HG_DIM), lambda h: (0, h)),
                  pl.BlockSpec((2, 2, HG_DIM), lambda h: (0, 0, h))],
        out_specs=[pl.BlockSpec((S, HG_DIM), lambda h: (0, h)),
                   pl.BlockSpec((2, None, N_CHUNKS, HG_DIM, HG_DIM), lambda h: (0, h, 0, 0, 0))],
        out_shape=[jax.ShapeDtypeStruct((S, HGW), F32),
                   jax.ShapeDtypeStruct((2, HG_HEADS, N_CHUNKS, HG_DIM, HG_DIM), BF16)],
        scratch_shapes=[pltpu.VMEM((2, S, HG_DIM), BF16), pltpu.VMEM((2, S, HG_DIM), BF16),
                        pltpu.VMEM((2, N_CHUNKS, HG_DIM, HG_DIM), F32), pltpu.VMEM((T, HG_DIM), BF16),
                        pltpu.VMEM((2, T, HG_DIM), F32)],
        operands=[p_a, lbl])


def _hgrn_bwd(p_a, lbl, d_o, st, carried=None):
    cpt = TM // CHUNK

    def rows(r):
        return r * TM if isinstance(r, int) else pl.multiple_of(r * TM, TM)

    def body(p_ref, lbl_ref, do_ref, st_ref, dp_ref, dlb_ref, b_s, bt_s, dbt_s, qd_s, dst_s, w_s):
        masks = [_chunk_masks(d == 1) for d in (0, 1)]
        same01 = jnp.where(masks[0][0], 1.0, 0.0).astype(BF16)
        tri = [m[1] for m in masks]
        tri01 = [jnp.where(t, 1.0, 0.0).astype(BF16) for t in tri]
        later01 = [tri01[1], tri01[0]]
        lb = [_sigmoid(lbl_ref[d][0:1, :] - lbl_ref[d][1:2, :]) for d in (0, 1)]

        def prep_tile(r, latent):
            r0 = rows(r)
            for d in (0, 1):
                z = p_ref[pl.ds(r0, TM), d * HG_DIM:(d + 1) * HG_DIM]
                _, _, b, bt = _decay_terms(z, lb[d], same01, tri01[d])
                b_s[d, pl.ds(r0, TM), :] = b
                bt_s[d, pl.ds(r0, TM), :] = bt
                if latent:
                    rl = pl.multiple_of(r0 - L, TM)
                    qr = p_ref[pl.ds(r0, TM), 3 * HG_DIM:4 * HG_DIM]
                    qd = (qr * _sigmoid(qr) * HG_DIM ** -0.5 * jnp.exp(b)).astype(BF16)
                    qd_s[d, pl.ds(rl, TM), :] = qd
                    w_s[d, pl.ds(r * cpt, cpt)] = _chunk_outer(
                        do_ref[pl.ds(rl, TM), :].astype(BF16), qd).astype(BF16)

        prep_tile(0, False)
        w_s[:, pl.ds(0, N_CTX_CHUNKS)] = jnp.zeros((2, N_CTX_CHUNKS, HG_DIM, HG_DIM), BF16)

        def prep(r, carry):
            prep_tile(r, True)
            return carry

        lax.fori_loop(1, N_TILES, prep, 0)

        def rscan(j, dsts):
            i = N_CHUNKS - 1 - j
            new = []
            for d in (0, 1):
                nn = _chunk_order(i, d == 1)
                c0 = pl.multiple_of(nn * CHUNK, CHUNK)
                dst_s[d, nn] = dsts[d].astype(BF16)
                after = st_ref[d, _chunk_order(jnp.minimum(i + 1, N_CHUNKS - 1), d == 1)].astype(F32)
                dbt_s[d, pl.ds(c0, CHUNK), :] = jnp.broadcast_to(
                    jnp.sum(after * dsts[d], axis=0, keepdims=True), (CHUNK, HG_DIM))
                new.append(dsts[d] * jnp.exp(bt_s[d, pl.ds(c0, 1), :]) + w_s[d, nn].astype(F32))
            return tuple(new)

        zero = jnp.zeros((HG_DIM, HG_DIM), F32)
        lax.fori_loop(0, N_CHUNKS, rscan, (zero, zero))

        def grad_tile(r, latent):
            r0 = rows(r)
            vb = p_ref[pl.ds(r0, TM), 2 * HG_DIM:3 * HG_DIM].astype(BF16)
            dv = jnp.zeros((TM, HG_DIM), F32)
            dq = jnp.zeros((TM, HG_DIM), F32)
            dlbs = []
            if latent:
                rl = pl.multiple_of(r0 - L, TM)
                qr = p_ref[pl.ds(r0, TM), 3 * HG_DIM:4 * HG_DIM]
                sq = _sigmoid(qr)
                do = do_ref[pl.ds(rl, TM), :].astype(BF16)
                da_full = _dot_nt(do, vb)
            for d in (0, 1):
                z = p_ref[pl.ds(r0, TM), d * HG_DIM:(d + 1) * HG_DIM]
                sz = _sigmoid(z)
                f = lb[d] + (1.0 - lb[d]) * sz
                k = 1.0 - f
                b = b_s[d, pl.ds(r0, TM), :]
                e2 = jnp.exp(bt_s[d, pl.ds(r0, TM), :] - b)
                dstb = dst_s[d, pl.ds(r * cpt, cpt)]
                kd2 = k * e2
                dkd2 = jnp.einsum('ncv,nvk->nck', vb.reshape(cpt, CHUNK, HG_DIM), dstb,
                                  preferred_element_type=F32).reshape(TM, HG_DIM)
                dv = dv + jnp.einsum('nck,nvk->ncv', kd2.astype(BF16).reshape(cpt, CHUNK, HG_DIM), dstb,
                                     preferred_element_type=F32).reshape(TM, HG_DIM)
                dk = dkd2 * e2
                db = -(kd2 * dkd2)
                if latent:
                    eb = jnp.exp(b)
                    enb = jnp.exp(-b)
                    qdf = qr * sq * HG_DIM ** -0.5 * eb
                    kdf = k * enb
                    qd = qd_s[d, pl.ds(rl, TM), :]
                    kd = kdf.astype(BF16)
                    a = jnp.where(tri[d], _dot_nt(qd, kd), 0.0).astype(BF16)
                    da = jnp.where(tri[d], da_full, 0.0).astype(BF16)
                    stb = st_ref[d, pl.ds(r * cpt, cpt)]
                    dqd = _dot(da, kd) + jnp.einsum(
                        'ncv,nvk->nck', do.reshape(cpt, CHUNK, HG_DIM), stb,
                        preferred_element_type=F32).reshape(TM, HG_DIM)
                    dkd = _dot_tn(da, qd)
                    dv = dv + _dot_tn(a, do)
                    dk = dk + dkd * enb
                    db = db + qdf * dqd - kdf * dkd
                    dq = dq + dqd * eb
                dg = _dot_lhs01(later01[d], db) + dbt_s[d, pl.ds(r0, TM), :]
                df = dg / f - dk
                dp_ref[pl.ds(r0, TM), d * HG_DIM:(d + 1) * HG_DIM] = (
                    df * (1.0 - lb[d]) * sz * (1.0 - sz)).astype(BF16)
                dlbs.append(jnp.sum(df * (1.0 - sz), axis=0, keepdims=True))
            dp_ref[pl.ds(r0, TM), 2 * HG_DIM:3 * HG_DIM] = dv.astype(BF16)
            if latent:
                dq = dq * (HG_DIM ** -0.5) * (sq * (1.0 + qr * (1.0 - sq)))
            dp_ref[pl.ds(r0, TM), 3 * HG_DIM:4 * HG_DIM] = dq.astype(BF16)
            return dlbs

        dlb_ctx = grad_tile(0, False)

        def grads(r, acc):
            t = grad_tile(r, True)
            return (acc[0] + t[0], acc[1] + t[1])

        dlb = lax.fori_loop(1, N_TILES, grads, (dlb_ctx[0], dlb_ctx[1]))
        dlb_ref[0:1, :] = dlb[0]
        dlb_ref[1:2, :] = dlb[1]

    return _pcall(
        body, carried, name="hgrn_bwd", grid=(HG_HEADS,),
        in_specs=[pl.BlockSpec((T, 4 * HG_DIM), lambda h: (0, h)),
                  pl.BlockSpec((2, 2, HG_DIM), lambda h: (0, 0, h)),
                  pl.BlockSpec((S, HG_DIM), lambda h: (0, h)),
                  pl.BlockSpec((2, None, N_CHUNKS, HG_DIM, HG_DIM), lambda h: (0, h, 0, 0, 0))],
        out_specs=[pl.BlockSpec((T, 4 * HG_DIM), lambda h: (0, h)),
                   pl.BlockSpec((2, HG_DIM), lambda h: (0, h))],
        out_shape=[jax.ShapeDtypeStruct((T, WA), BF16), jax.ShapeDtypeStruct((2, HGW), F32)],
        scratch_shapes=[pltpu.VMEM((2, T, HG_DIM), F32), pltpu.VMEM((2, T, HG_DIM), F32),
                        pltpu.VMEM((2, T, HG_DIM), F32), pltpu.VMEM((2, S, HG_DIM), BF16),
                        pltpu.VMEM((2, N_CHUNKS, HG_DIM, HG_DIM), BF16),
                        pltpu.VMEM((2, N_CHUNKS, HG_DIM, HG_DIM), BF16)],
        operands=[p_a, lbl, d_o, st])


def _rope_tables():
    t = np.arange(S)
    inv = ROPE_THETA ** (-np.arange(0, 32, 2, dtype=np.float64) / 32)
    lane = np.arange(64)
    pos = np.where(lane[None, :] < 32, (t // GRID_W)[:, None], (t % GRID_W)[:, None]).astype(np.float64)
    ang = pos * inv[(lane % 32) % 16][None, :]
    sign = np.where((lane % 32) < 16, -1.0, 1.0)[None, :]
    cos = np.tile(np.cos(ang), (1, 2)).astype(np.float32)
    sin = np.tile(np.sin(ang) * sign, (1, 2)).astype(np.float32)
    return jnp.asarray(cos), jnp.asarray(sin)


def _rope_partner(v):
    lane = lax.broadcasted_iota(jnp.int32, (1, 128), 1)
    first = (lane % 32) < 16
    slabs = []
    for j in range(v.shape[1] // 128):
        s = v[:, 128 * j:128 * (j + 1)]
        slabs.append(jnp.where(first, pltpu.roll(s, 112, 1), pltpu.roll(s, 16, 1)))
    return slabs[0] if len(slabs) == 1 else jnp.concatenate(slabs, axis=1)


def _group_ones(width, group):
    r = lax.broadcasted_iota(jnp.int32, (width, width), 0)
    c = lax.broadcasted_iota(jnp.int32, (width, width), 1)
    return jnp.where((r // group) == (c // group), 1.0, 0.0).astype(BF16)


def _group_mean(v, ones01, group):
    hi = v.astype(BF16)
    lo = (v - hi.astype(F32)).astype(BF16)
    return (_dot(hi, ones01) + _dot(lo, ones01)) * (1.0 / group)


def _rep_matrix():
    r = lax.broadcasted_iota(jnp.int32, (KVW, ATW), 0)
    c = lax.broadcasted_iota(jnp.int32, (KVW, ATW), 1)
    return jnp.where(r == HEAD_DIM * (c // 256) + c % HEAD_DIM, 1.0, 0.0).astype(BF16)


def _tile_lanes(v, reps):
    return jnp.concatenate([v] * reps, axis=1)


def _prep_fwd(p_b, o, cos, sin, hnw, qnw, knw):
    def body(p_ref, o_ref, cos_ref, sin_ref, hnw_ref, qnw_ref, knw_ref, y_ref, q_ref, k_ref, v_ref):
        i = pl.program_id(0)
        rep = _rep_matrix()
        ones_k = _group_ones(KVW, HEAD_DIM)
        kr = p_ref[:, 1024:1152]
        krstd = lax.rsqrt(_group_mean(kr * kr, ones_k, HEAD_DIM) + EPS)
        kn = kr * krstd * knw_ref[...]
        v_ref[...] = _dot(p_ref[:, 1152:1280].astype(BF16), rep).astype(BF16)

        @pl.when(i == 0)
        def _():
            k_ref[...] = _dot(kn.astype(BF16), rep).astype(BF16)

        @pl.when(i > 0)
        def _():
            cs, sn = cos_ref[...], sin_ref[...]
            kro = kn * cs + _rope_partner(kn) * sn
            k_ref[...] = _dot(kro.astype(BF16), rep).astype(BF16)
            qr = p_ref[:, 512:1024]
            qrstd = lax.rsqrt(_group_mean(qr * qr, _group_ones(ATW, HEAD_DIM), HEAD_DIM) + EPS)
            qn = qr * qrstd * qnw_ref[...]
            qro = qn * _tile_lanes(cs, 4) + _rope_partner(qn) * _tile_lanes(sn, 4)
            q_ref[...] = (qro * HEAD_DIM ** -0.5).astype(BF16)
            ys = []
            for h in range(HG_HEADS):
                oh = o_ref[:, HG_DIM * h:HG_DIM * (h + 1)]
                gh = p_ref[:, HG_DIM * h:HG_DIM * (h + 1)]
                rstd = lax.rsqrt(jnp.mean(oh * oh, axis=-1, keepdims=True) + EPS)
                ys.append(oh * rstd * hnw_ref[...] * (gh * _sigmoid(gh)))
            y_ref[...] = jnp.concatenate(ys, axis=1).astype(BF16)

    return pl.pallas_call(
        body, name="prep_fwd", grid=(N_TILES,),
        in_specs=[pl.BlockSpec((TM, WB), lambda i: (i, 0)),
                  pl.BlockSpec((TM, HGW), lambda i: (_lat(i), 0)),
                  pl.BlockSpec((TM, 128), lambda i: (_lat(i), 0)),
                  pl.BlockSpec((TM, 128), lambda i: (_lat(i), 0)),
                  _full((1, HG_DIM)), _full((1, ATW)), _full((1, KVW))],
        out_specs=[pl.BlockSpec((TM, HGW), lambda i: (_lat(i), 0)),
                   pl.BlockSpec((TM, ATW), lambda i: (_lat(i), 0)),
                   pl.BlockSpec((TM, ATW), lambda i: (i, 0)),
                   pl.BlockSpec((TM, ATW), lambda i: (i, 0))],
        out_shape=[jax.ShapeDtypeStruct((S, HGW), BF16), jax.ShapeDtypeStruct((S, ATW), BF16),
                   jax.ShapeDtypeStruct((T, ATW), BF16), jax.ShapeDtypeStruct((T, ATW), BF16)],
        compiler_params=_cp(("arbitrary",)),
    )(p_b, o, cos, sin, hnw, qnw, knw)


def _prep_bwd(p_b, o, cos, sin, hnw, qnw, knw, dy_hg, dq, dk_rep, dv_rep, carried=None):
    def body(p_ref, o_ref, cos_ref, sin_ref, hnw_ref, qnw_ref, knw_ref, dy_ref, dq_ref, dk_ref, dv_ref,
             dp_ref, do_ref, acc_ref):
        i = pl.program_id(0)

        @pl.when(i == 0)
        def _():
            acc_ref[...] = jnp.zeros_like(acc_ref)

        rep = _rep_matrix()
        ones_k = _group_ones(KVW, HEAD_DIM)

        def fold(v):
            hi = v.astype(BF16)
            lo = (v - hi.astype(F32)).astype(BF16)
            return _dot_nt(hi, rep) + _dot_nt(lo, rep)

        kr = p_ref[:, 1024:1152]
        krstd = lax.rsqrt(_group_mean(kr * kr, ones_k, HEAD_DIM) + EPS)
        khat = kr * krstd
        kw = knw_ref[...]
        dkro = fold(dk_ref[...])
        dv = fold(dv_ref[...])

        def k_back(dkn):
            dkhat = dkn * kw
            dkr = krstd * (dkhat - khat * _group_mean(dkhat * khat, ones_k, HEAD_DIM))
            acc_ref[2:3, 0:KVW] += jnp.sum(dkn * khat, axis=0, keepdims=True)
            dp_ref[:, 1024:1152] = dkr.astype(BF16)
            dp_ref[:, 1152:1280] = dv.astype(BF16)

        @pl.when(i == 0)
        def _():
            k_back(dkro)
            dp_ref[:, 0:1024] = jnp.zeros((TM, 1024), BF16)

        @pl.when(i > 0)
        def _():
            cs, sn = cos_ref[...], sin_ref[...]
            k_back(dkro * cs + _rope_partner(dkro * sn))
            ones_q = _group_ones(ATW, HEAD_DIM)
            qr = p_ref[:, 512:1024]
            qrstd = lax.rsqrt(_group_mean(qr * qr, ones_q, HEAD_DIM) + EPS)
            qhat = qr * qrstd
            dqro = dq_ref[...] * HEAD_DIM ** -0.5
            dqn = dqro * _tile_lanes(cs, 4) + _rope_partner(dqro * _tile_lanes(sn, 4))
            dqhat = dqn * qnw_ref[...]
            dqr = qrstd * (dqhat - qhat * _group_mean(dqhat * qhat, ones_q, HEAD_DIM))
            acc_ref[1:2, :] += jnp.sum(dqn * qhat, axis=0, keepdims=True)
            dp_ref[:, 512:1024] = dqr.astype(BF16)
            dws = jnp.zeros((1, HG_DIM), F32)
            for h in range(HG_HEADS):
                sl = slice(HG_DIM * h, HG_DIM * (h + 1))
                oh, gh, dy = o_ref[:, sl], p_ref[:, sl], dy_ref[:, sl]
                rstd = lax.rsqrt(jnp.mean(oh * oh, axis=-1, keepdims=True) + EPS)
                ohat = oh * rstd
                sg = _sigmoid(gh)
                dp_ref[:, sl] = (dy * (ohat * hnw_ref[...]) * (sg * (1.0 + gh * (1.0 - sg)))).astype(BF16)
                dn = dy * (gh * sg)
                dws = dws + jnp.sum(dn * ohat, axis=0, keepdims=True)
                dohat = dn * hnw_ref[...]
                do_ref[:, sl] = rstd * (dohat - ohat * jnp.mean(dohat * ohat, axis=-1, keepdims=True))
            acc_ref[0:1, 0:HG_DIM] += dws

    return _pcall(
        body, carried, name="prep_bwd", grid=(N_TILES,),
        in_specs=[pl.BlockSpec((TM, WB), lambda i: (i, 0)),
                  pl.BlockSpec((TM, HGW), lambda i: (_lat(i), 0)),
                  pl.BlockSpec((TM, 128), lambda i: (_lat(i), 0)),
                  pl.BlockSpec((TM, 128), lambda i: (_lat(i), 0)),
                  _full((1, HG_DIM)), _full((1, ATW)), _full((1, KVW)),
                  pl.BlockSpec((TM, HGW), lambda i: (_lat(i), 0)),
                  pl.BlockSpec((TM, ATW), lambda i: (_lat(i), 0)),
                  pl.BlockSpec((TM, ATW), lambda i: (i, 0)),
                  pl.BlockSpec((TM, ATW), lambda i: (i, 0))],
        out_specs=[pl.BlockSpec((TM, WB), lambda i: (i, 0)),
                   pl.BlockSpec((TM, HGW), lambda i: (_lat(i), 0)),
                   _full((8, ATW))],
        out_shape=[jax.ShapeDtypeStruct((T, WB), BF16), jax.ShapeDtypeStruct((S, HGW), F32),
                   jax.ShapeDtypeStruct((8, ATW), F32)],
        scratch_shapes=[], operands=[p_b, o, cos, sin, hnw, qnw, knw, dy_hg, dq, dk_rep, dv_rep])


NEG = -1e30
_CTX_BLOCKS = L // BLOCK


def _attn_window_specs():
    prev = pl.BlockSpec((BLOCK, ATW), lambda i: (jnp.maximum(i - 1, 0) + _CTX_BLOCKS, 0))
    own = pl.BlockSpec((BLOCK, ATW), lambda i: (i + _CTX_BLOCKS, 0))
    nxt = pl.BlockSpec((BLOCK, ATW), lambda i: (jnp.minimum(i + 1, N_BLOCKS - 1) + _CTX_BLOCKS, 0))
    return [prev, own, nxt, _full((L, ATW))]


def _attn_valid(i, heads=4):
    qi = lax.broadcasted_iota(jnp.int32, (heads * BLOCK, 3 * BLOCK), 0) % BLOCK
    kj = lax.broadcasted_iota(jnp.int32, (heads * BLOCK, 3 * BLOCK), 1)
    return ((jnp.abs(kj - BLOCK - qi) <= BLOCK) & ((kj >= BLOCK) | (i > 0))
            & ((kj < 2 * BLOCK) | (i < N_BLOCKS - 1)))


def _stack_heads(qg):
    lane = lax.broadcasted_iota(jnp.int32, (1, 256), 1) // HEAD_DIM
    return jnp.concatenate([jnp.where(lane == g, qg, jnp.zeros_like(qg)) for g in range(4)], axis=0)


def _unstack_heads(v4):
    lane = lax.broadcasted_iota(jnp.int32, (1, 256), 1) // HEAD_DIM
    out = jnp.where(lane == 0, v4[0:BLOCK], 0.0)
    for g in range(1, 4):
        out = out + jnp.where(lane == g, v4[g * BLOCK:(g + 1) * BLOCK], 0.0)
    return out


def _sink_rows(sink_ref, hk):
    return jnp.concatenate(
        [jnp.broadcast_to(sink_ref[0:1, 4 * hk + g:4 * hk + g + 1], (BLOCK, 1)) for g in range(4)], axis=0)


def _attn_fwd(q, k_rep, v_rep, sinks, carried=None):
    def body(q_ref, kp, ko, kn, kc, vp, vo, vn, vc, sink_ref, y_ref, lse_ref):
        i = pl.program_id(0)
        valid = _attn_valid(i, 1)
        lane8 = lax.broadcasted_iota(jnp.int32, (1, ATT_HEADS), 1)
        head_of_lane = lax.broadcasted_iota(jnp.int32, (1, 256), 1) // HEAD_DIM
        lse_out = jnp.zeros((BLOCK, ATT_HEADS), F32)
        for hk in range(KV_HEADS):
            sl = slice(256 * hk, 256 * (hk + 1))
            qg = q_ref[:, sl]
            kl = jnp.concatenate([kp[:, sl], ko[:, sl], kn[:, sl]], axis=0)
            vl = jnp.concatenate([vp[:, sl], vo[:, sl], vn[:, sl]], axis=0)
            yg = jnp.zeros((BLOCK, 256), F32)
            for g in range(4):
                q1 = jnp.where(head_of_lane == g, qg, jnp.zeros_like(qg))
                s_loc = jnp.where(valid, _dot_nt(q1, kl), NEG)
                s_ctx = _dot_nt(q1, kc[:, sl])
                sink = sink_ref[0:1, 4 * hk + g:4 * hk + g + 1]
                m = jnp.maximum(jnp.maximum(jnp.max(s_loc, axis=1, keepdims=True),
                                            jnp.max(s_ctx, axis=1, keepdims=True)), sink)
                p_loc = jnp.exp(s_loc - m)
                p_ctx = jnp.exp(s_ctx - m)
                den = (jnp.sum(p_loc, axis=1, keepdims=True) + jnp.sum(p_ctx, axis=1, keepdims=True)
                       + jnp.exp(sink - m))
                o1 = (_dot(p_loc.astype(BF16), vl) + _dot(p_ctx.astype(BF16), vc[:, sl])) * (1.0 / den)
                yg = yg + jnp.where(head_of_lane == g, o1, 0.0)
                lse_out = lse_out + jnp.where(lane8 == 4 * hk + g, m + jnp.log(den), 0.0)
            y_ref[:, sl] = yg.astype(BF16)
        lse_ref[...] = lse_out

    return _pcall(
        body, carried, name="attn_fwd", grid=(N_BLOCKS,),
        in_specs=[pl.BlockSpec((BLOCK, ATW), lambda i: (i, 0))] + _attn_window_specs()
        + _attn_window_specs() + [_full((1, ATT_HEADS))],
        out_specs=[pl.BlockSpec((BLOCK, ATW), lambda i: (i, 0)),
                   pl.BlockSpec((BLOCK, ATT_HEADS), lambda i: (i, 0))],
        out_shape=[jax.ShapeDtypeStruct((S, ATW), BF16), jax.ShapeDtypeStruct((S, ATT_HEADS), F32)],
        scratch_shapes=[],
        operands=[q, k_rep, k_rep, k_rep, k_rep, v_rep, v_rep, v_rep, v_rep, sinks])


def _attn_bwd(q, k_rep, v_rep, sinks, y_at, lse, dy, carried=None):
    def body(q_ref, kp, ko, kn, kc, vp, vo, vn, vc, sink_ref, y_ref, lse_ref, dy_ref,
             dq_ref, dk_ref, dv_ref, dsink_ref, dk_acc, dv_acc):
        i = pl.program_id(0)

        @pl.when(i == 0)
        def _():
            dk_acc[...] = jnp.zeros_like(dk_acc)
            dv_acc[...] = jnp.zeros_like(dv_acc)
            dk_ref[pl.ds(0, L), :] = jnp.zeros((L, ATW), F32)
            dv_ref[pl.ds(0, L), :] = jnp.zeros((L, ATW), F32)
            dsink_ref[...] = jnp.zeros_like(dsink_ref)

        valid = _attn_valid(i)
        lane8 = lax.broadcasted_iota(jnp.int32, (1, ATT_HEADS), 1)
        w0 = pl.multiple_of(i * BLOCK, BLOCK)
        dsink = jnp.zeros((1, ATT_HEADS), F32)
        for hk in range(KV_HEADS):
            sl = slice(256 * hk, 256 * (hk + 1))
            q4 = _stack_heads(q_ref[:, sl])
            do4f = _stack_heads(dy_ref[:, sl])
            o4 = _stack_heads(y_ref[:, sl]).astype(F32)
            do4 = do4f.astype(BF16)
            kl = jnp.concatenate([kp[:, sl], ko[:, sl], kn[:, sl]], axis=0)
            vl = jnp.concatenate([vp[:, sl], vo[:, sl], vn[:, sl]], axis=0)
            lse4 = jnp.concatenate(
                [jnp.sum(jnp.where(lane8 == 4 * hk + g, lse_ref[...], 0.0), axis=1, keepdims=True)
                 for g in range(4)], axis=0)
            p_loc = jnp.where(valid, jnp.exp(_dot_nt(q4, kl) - lse4), 0.0)
            p_ctx = jnp.exp(_dot_nt(q4, kc[:, sl]) - lse4)
            delta = jnp.sum(do4f * o4, axis=1, keepdims=True)
            ds_loc = (p_loc * (_dot_nt(do4, vl) - delta)).astype(BF16)
            ds_ctx = (p_ctx * (_dot_nt(do4, vc[:, sl]) - delta)).astype(BF16)
            dq_ref[:, sl] = _unstack_heads(_dot(ds_loc, kl) + _dot(ds_ctx, kc[:, sl]))
            dk_acc[pl.ds(w0, 3 * BLOCK), sl] += _dot_tn(ds_loc, q4)
            dv_acc[pl.ds(w0, 3 * BLOCK), sl] += _dot_tn(p_loc.astype(BF16), do4)
            dk_ref[pl.ds(0, L), sl] += _dot_tn(ds_ctx, q4)
            dv_ref[pl.ds(0, L), sl] += _dot_tn(p_ctx.astype(BF16), do4)
            p_sink = jnp.exp(_sink_rows(sink_ref, hk) - lse4)
            for g in range(4):
                rows = slice(g * BLOCK, (g + 1) * BLOCK)
                dsink = dsink + jnp.where(lane8 == 4 * hk + g,
                                          -jnp.sum(p_sink[rows] * delta[rows], axis=0, keepdims=True), 0.0)
        dsink_ref[...] += dsink

        @pl.when(i == N_BLOCKS - 1)
        def _():
            dk_ref[pl.ds(L, S), :] = dk_acc[pl.ds(BLOCK, S), :]
            dv_ref[pl.ds(L, S), :] = dv_acc[pl.ds(BLOCK, S), :]

    row_q = pl.BlockSpec((BLOCK, ATW), lambda i: (i, 0))
    return _pcall(
        body, carried, name="attn_bwd", grid=(N_BLOCKS,),
        in_specs=[row_q] + _attn_window_specs() + _attn_window_specs()
        + [_full((1, ATT_HEADS)), row_q, pl.BlockSpec((BLOCK, ATT_HEADS), lambda i: (i, 0)), row_q],
        out_specs=[row_q, _full((T, ATW)), _full((T, ATW)), _full((1, ATT_HEADS))],
        out_shape=[jax.ShapeDtypeStruct((S, ATW), F32), jax.ShapeDtypeStruct((T, ATW), F32),
                   jax.ShapeDtypeStruct((T, ATW), F32), jax.ShapeDtypeStruct((1, ATT_HEADS), F32)],
        scratch_shapes=[pltpu.VMEM((S + 2 * BLOCK, ATW), F32), pltpu.VMEM((S + 2 * BLOCK, ATW), F32)],
        operands=[q, k_rep, k_rep, k_rep, k_rep, v_rep, v_rep, v_rep, v_rep, sinks, y_at, lse, dy])


def _merge_fwd(y_hg, y_at, p_c, x, w_bh, w_ba, w_out, g1, nfw, sh2, sc2):
    def body(yh_ref, ya_ref, g_ref, x_ref, wbh_ref, wba_ref, wo_ref, g1_ref, nfw_ref, sh_ref, sc_ref,
             a_ref, b_ref, mx_ref, r_ref, x1_ref, h2_ref):
        a = _dot_nt(yh_ref[...], wbh_ref[...])
        b = _dot_nt(ya_ref[...], wba_ref[...])
        mixed = (_sigmoid(g_ref[:, :D]) * a + _sigmoid(g_ref[:, D:]) * b).astype(BF16)
        r = _dot(mixed, wo_ref[...])
        x1 = x_ref[...] + g1_ref[...] * r
        a_ref[...] = a
        b_ref[...] = b
        mx_ref[...] = mixed
        r_ref[...] = r
        x1_ref[...] = x1
        h2_ref[...] = _rms_mod(x1, nfw_ref[...], sh_ref[...], sc_ref[...]).astype(BF16)

    row = lambda w: pl.BlockSpec((TM, w), lambda i: (i, 0))
    vec = _full((1, D))
    return pl.pallas_call(
        body, name="merge_fwd", grid=(N_LAT_TILES,),
        in_specs=[row(HGW), row(ATW), row(WC), row(D), _VMEM_WHOLE, _VMEM_WHOLE, _VMEM_WHOLE,
                  vec, vec, vec, vec],
        out_specs=[row(D)] * 6,
        out_shape=[jax.ShapeDtypeStruct((S, D), dt) for dt in (F32, F32, BF16, F32, F32, BF16)],
        compiler_params=_cp(("parallel",)),
    )(y_hg, y_at, p_c, x, w_bh, w_ba, w_out, g1, nfw, sh2, sc2)


def _merge_bwd(dx1, r, a, b, p_c, w_bh, w_ba, w_out, g1, carried=None):
    def body(dx_ref, r_ref, a_ref, b_ref, g_ref, wbh_ref, wba_ref, wo_ref, g1_ref,
             dr_ref, da_ref, db_ref, dg_ref, dyh_ref, dya_ref, acc_ref):
        @pl.when(pl.program_id(0) == 0)
        def _():
            acc_ref[...] = jnp.zeros_like(acc_ref)

        dx1v = dx_ref[...]
        acc_ref[0:1, :] += jnp.sum(dx1v * r_ref[...], axis=0, keepdims=True)
        dr = (g1_ref[...] * dx1v).astype(BF16)
        dr_ref[...] = dr
        dmix = _dot_nt(dr, wo_ref[...])
        sh, sa = _sigmoid(g_ref[:, :D]), _sigmoid(g_ref[:, D:])
        da = (dmix * sh).astype(BF16)
        db = (dmix * sa).astype(BF16)
        da_ref[...] = da
        db_ref[...] = db
        dg_ref[:, :D] = (dmix * a_ref[...] * sh * (1.0 - sh)).astype(BF16)
        dg_ref[:, D:] = (dmix * b_ref[...] * sa * (1.0 - sa)).astype(BF16)
        dyh_ref[...] = _dot(da, wbh_ref[...])
        dya_ref[...] = _dot(db, wba_ref[...])

    row = lambda w: pl.BlockSpec((TM, w), lambda i: (i, 0))
    return _pcall(
        body, carried, name="merge_bwd", grid=(N_LAT_TILES,),
        in_specs=[row(D), row(D), row(D), row(D), row(WC), _VMEM_WHOLE, _VMEM_WHOLE, _VMEM_WHOLE,
                  _full((1, D))],
        out_specs=[row(D), row(D), row(D), row(WC), row(HGW), row(ATW), _full((8, D))],
        out_shape=[jax.ShapeDtypeStruct((S, D), BF16), jax.ShapeDtypeStruct((S, D), BF16),
                   jax.ShapeDtypeStruct((S, D), BF16), jax.ShapeDtypeStruct((S, WC), BF16),
                   jax.ShapeDtypeStruct((S, HGW), F32), jax.ShapeDtypeStruct((S, ATW), F32),
                   jax.ShapeDtypeStruct((8, D), F32)],
        scratch_shapes=[], operands=[dx1, r, a, b, p_c, w_bh, w_ba, w_out, g1])


def _ffn_fused(x1, h2, tgt, w_gate, w_up, w_down, g2, nfw, sc2):
    def body(x1_ref, h2_ref, t_ref, wg_ref, wu_ref, wd_ref, g2_ref, nfw_ref, sc_ref,
             act_ref, dgt_ref, dup_ref, df_ref, dx_ref, acc_ref, gs, us):
        @pl.when(pl.program_id(0) == 0)
        def _():
            acc_ref[...] = jnp.zeros_like(acc_ref)

        h2 = h2_ref[...]
        f = jnp.zeros((TM, D), F32)
        for j in range(N_FF_TILES):
            g = _dot_nt(h2, wg_ref[j])
            u = _dot_nt(h2, wu_ref[j])
            gs[j] = g
            us[j] = u
            act = (g * _sigmoid(g) * u).astype(BF16)
            act_ref[j] = act
            f = f + _dot(act, wd_ref[j])
        x1v = x1_ref[...]
        g2 = g2_ref[...]
        diff = x1v + g2 * f - t_ref[...]
        dy = diff * (1.0 / D)
        df = (g2 * dy).astype(BF16)
        df_ref[...] = df
        dh2 = jnp.zeros((TM, D), F32)
        for j in range(N_FF_TILES):
            g, u = gs[j], us[j]
            sg = _sigmoid(g)
            dact = _dot_nt(df, wd_ref[j])
            dgate = (dact * u * (sg * (1.0 + g * (1.0 - sg)))).astype(BF16)
            dup = (dact * (g * sg)).astype(BF16)
            dgt_ref[j] = dgate
            dup_ref[j] = dup
            dh2 = dh2 + _dot(dgate, wg_ref[j]) + _dot(dup, wu_ref[j])
        dx, dsh, dsc, dnw = _rms_mod_bwd(x1v, nfw_ref[...], sc_ref[...], dh2)
        dx_ref[...] = dy + dx
        acc_ref[0:1, :] += dsh
        acc_ref[1:2, :] += dsc
        acc_ref[2:3, :] += dnw
        acc_ref[3:4, :] += jnp.sum(dy * f, axis=0, keepdims=True)
        acc_ref[4:5, :] += 0.5 * jnp.sum(jnp.sum(diff * diff, axis=1, keepdims=True), axis=0,
                                         keepdims=True) * (1.0 / D)

    row = lambda dt_w: pl.BlockSpec((TM, dt_w), lambda i: (i, 0))
    blk = pl.BlockSpec((N_FF_TILES, TM, FF_TILE), lambda i: (0, i, 0))
    vec = _full((1, D))
    return pl.pallas_call(
        body, name="ffn_fused", grid=(N_LAT_TILES,),
        in_specs=[row(D), row(D), row(D), _VMEM_WHOLE, _VMEM_WHOLE, _VMEM_WHOLE, vec, vec, vec],
        out_specs=[blk, blk, blk, row(D), row(D), _full((8, D))],
        out_shape=[jax.ShapeDtypeStruct((N_FF_TILES, S, FF_TILE), BF16)] * 3
        + [jax.ShapeDtypeStruct((S, D), BF16), jax.ShapeDtypeStruct((S, D), F32),
           jax.ShapeDtypeStruct((8, D), F32)],
        scratch_shapes=[pltpu.VMEM((N_FF_TILES, TM, FF_TILE), F32), pltpu.VMEM((N_FF_TILES, TM, FF_TILE), F32)],
        compiler_params=_cp(("arbitrary",)),
    )(x1, h2, tgt, w_gate, w_up, w_down, g2, nfw, sc2)


def _proj_bc(h_all, w_b, w_c, carried=None):
    def body(h_ref, wb_ref, wc_ref, pb_ref, pc_ref):
        h = h_ref[...]
        pb_ref[...] = _dot_nt(h, wb_ref[...])

        @pl.when(pl.program_id(0) > 0)
        def _():
            pc_ref[...] = _dot_nt(h, wc_ref[...])

    return _pcall(
        body, carried, name="proj_bc", grid=(N_TILES,),
        in_specs=[pl.BlockSpec((TM, D), lambda i: (i, 0)), _VMEM_WHOLE, _VMEM_WHOLE],
        out_specs=[pl.BlockSpec((TM, WB), lambda i: (i, 0)), pl.BlockSpec((TM, WC), lambda i: (_lat(i), 0))],
        out_shape=[jax.ShapeDtypeStruct((T, WB), F32), jax.ShapeDtypeStruct((S, WC), F32)],
        scratch_shapes=[], operands=[h_all, w_b, w_c])


def _input_bwd(dp_a, dp_b, dp_c, w_a, w_b, w_c, ctx, x, dx1, nw, sh, sc, carried=None):
    def body(da_ref, db_ref, dc_ref, wa_ref, wb_ref, wc_ref, ctx_ref, x_ref, dx1_ref, nw_ref, sh_ref,
             sc_ref, gx_ref, acc_ref):
        i = pl.program_id(0)

        @pl.when(i == 0)
        def _():
            acc_ref[...] = jnp.zeros_like(acc_ref)

        dh = _dot(da_ref[...], wa_ref[...]) + _dot(db_ref[...], wb_ref[...])

        @pl.when(i == 0)
        def _():
            _, dsh, dsc, dnw = _rms_mod_bwd(ctx_ref[...], nw_ref[...], sc_ref[0:1, :], dh)
            acc_ref[3:4, :] += dsh
            acc_ref[4:5, :] += dsc
            acc_ref[2:3, :] += dnw

        @pl.when(i > 0)
        def _():
            dhl = dh + _dot(dc_ref[...], wc_ref[...])
            dx, dsh, dsc, dnw = _rms_mod_bwd(x_ref[...], nw_ref[...], sc_ref[1:2, :], dhl)
            gx_ref[...] = dx1_ref[...] + dx
            acc_ref[0:1, :] += dsh
            acc_ref[1:2, :] += dsc
            acc_ref[2:3, :] += dnw

    lat = lambda w: pl.BlockSpec((TM, w), lambda i: (_lat(i), 0))
    return _pcall(
        body, carried, name="input_bwd", grid=(N_TILES,),
        in_specs=[pl.BlockSpec((TM, WA), lambda i: (i, 0)), pl.BlockSpec((TM, WB), lambda i: (i, 0)),
                  lat(WC), _VMEM_WHOLE, _VMEM_WHOLE, _VMEM_WHOLE, _full((TM, D)), lat(D), lat(D),
                  _full((1, D)), _full((2, D)), _full((2, D))],
        out_specs=[lat(D), _full((8, D))],
        out_shape=[jax.ShapeDtypeStruct((S, D), F32), jax.ShapeDtypeStruct((8, D), F32)],
        scratch_shapes=[], operands=[dp_a, dp_b, dp_c, w_a, w_b, w_c, ctx, x, dx1, nw, sh, sc])


_C1 = 1.0 - ADAM_B1 ** ADAM_STEP
_C2 = 1.0 - ADAM_B2 ** ADAM_STEP


def _adamw_math(w, g, m, v):
    m = ADAM_B1 * m + (1.0 - ADAM_B1) * g
    v = ADAM_B2 * v + (1.0 - ADAM_B2) * (g * g)
    m_hat = m / _C1
    v_hat = v / _C2
    delta = -ADAM_LR * (m_hat / (jnp.sqrt(v_hat) + ADAM_EPS) + ADAM_WD * w)
    return delta, m, v


def _adamw_sharded(terms, w, m, v, name, tr):
    rows, cols = w.shape

    def body(t_ref, w_ref, m_ref, v_ref, g_ref, d_ref, nm_ref, nv_ref):
        g = t_ref[0].astype(F32)
        for s in range(1, N_CHIPS):
            g = g + t_ref[s].astype(F32)
        g_ref[...] = g
        d_ref[...], nm_ref[...], nv_ref[...] = _adamw_math(w_ref[...], g, m_ref[...], v_ref[...])

    blk = pl.BlockSpec((tr, cols), lambda i: (i, 0))
    return pl.pallas_call(
        body, name=name, grid=(rows // tr,),
        in_specs=[pl.BlockSpec((N_CHIPS, tr, cols), lambda i: (0, i, 0)), blk, blk, blk],
        out_specs=[blk] * 4,
        out_shape=[jax.ShapeDtypeStruct((rows, cols), F32)] * 4,
        compiler_params=_cp(("parallel",)),
    )(terms, w, m, v)


def _adamw_plain(g, w, m, v, name):
    def body(g_ref, w_ref, m_ref, v_ref, d_ref, nm_ref, nv_ref):
        d_ref[...], nm_ref[...], nv_ref[...] = _adamw_math(w_ref[...], g_ref[...], m_ref[...], v_ref[...])

    return pl.pallas_call(
        body, name=name, in_specs=[_VMEM_WHOLE] * 4, out_specs=[_VMEM_WHOLE] * 3,
        out_shape=[jax.ShapeDtypeStruct(w.shape, F32)] * 3,
        compiler_params=_cp(),
    )(g, w, m, v)


SMALL_ROWS = 16
R_DMOD, R_DCTX, R_NMIX, R_NFFN, R_MISC, R_DLB, R_BADA01 = 0, 6, 8, 9, 10, 11, 13
M_HNW, M_QNW, M_KNW, M_SINK, M_LOSS = 0, 128, 256, 384, 512


def _pack_small(acc_in, acc_mg, acc_ffn, acc_prep, dsink, dlb):
    def body(in_ref, mg_ref, ff_ref, pp_ref, ds_ref, dlb_ref, o_ref):
        o_ref[...] = jnp.zeros_like(o_ref)
        o_ref[0:2, :] = in_ref[0:2, :]
        o_ref[2:3, :] = mg_ref[0:1, :]
        o_ref[3:5, :] = ff_ref[0:2, :]
        o_ref[5:6, :] = ff_ref[3:4, :]
        o_ref[6:8, :] = in_ref[3:5, :]
        o_ref[8:9, :] = in_ref[2:3, :]
        o_ref[9:10, :] = ff_ref[2:3, :]
        o_ref[10:11, M_HNW:M_HNW + HG_DIM] = pp_ref[0:1, 0:HG_DIM]
        r = lax.broadcasted_iota(jnp.int32, (ATW, 128), 0)
        c = lax.broadcasted_iota(jnp.int32, (ATW, 128), 1)
        fold = jnp.where((r % HEAD_DIM == c) & (c < HEAD_DIM), 1.0, 0.0).astype(BF16)
        qk = jnp.concatenate([pp_ref[1:2, :], pp_ref[2:3, :], jnp.zeros((6, ATW), F32)], axis=0)
        folded = _dot_exact_rhs01(qk, fold)
        o_ref[10:11, M_QNW:M_QNW + 128] = folded[0:1, :]
        o_ref[10:11, M_KNW:M_KNW + 128] = folded[1:2, :]
        o_ref[10:11, M_SINK:M_SINK + ATT_HEADS] = ds_ref[...]
        o_ref[10:11, M_LOSS:M_LOSS + 128] = ff_ref[4:5, 0:128]
        o_ref[11:13, 0:HGW] = dlb_ref[...]

    return pl.pallas_call(
        body, name="pack_small", in_specs=[_VMEM_WHOLE] * 6, out_specs=_VMEM_WHOLE,
        out_shape=jax.ShapeDtypeStruct((SMALL_ROWS, D), F32), compiler_params=_cp(),
    )(acc_in, acc_mg, acc_ffn, acc_prep, dsink, dlb)


def _sum_small(gathered):
    def body(g_ref, o_ref):
        tot = g_ref[0]
        for s in range(1, N_DEV):
            tot = tot + g_ref[s]
        o_ref[...] = tot
        o_ref[R_BADA01:R_BADA01 + 2, :] = tot[0:2, :] + tot[R_DCTX:R_DCTX + 2, :]

    return pl.pallas_call(
        body, name="sum_small", in_specs=[_VMEM_WHOLE], out_specs=_VMEM_WHOLE,
        out_shape=jax.ShapeDtypeStruct((SMALL_ROWS, D), F32), compiler_params=_cp(),
    )(gathered)


_REP_NAMES = ("b_ada", "c_ctx", "norm_mix_w", "norm_ffn_w", "hgrn_norm_w", "q_norm_w", "k_norm_w", "attn_sinks")


def _adamw_replicated(tot, g_c_ctx, ws, ms, vs):
    n = len(_REP_NAMES)

    def body(*refs):
        tot_ref, gc_ref = refs[0], refs[1]
        w_refs, m_refs, v_refs = refs[2:2 + n], refs[2 + n:2 + 2 * n], refs[2 + 2 * n:2 + 3 * n]
        outs = refs[2 + 3 * n:]
        row = lambda r: tot_ref[r:r + 1, :]
        misc = row(R_MISC)
        grads = [jnp.concatenate([row(R_BADA01), row(R_BADA01 + 1)] + [row(k) for k in range(2, 6)], axis=1),
                 gc_ref[...], row(R_NMIX), row(R_NFFN),
                 misc[:, M_HNW:M_HNW + HG_DIM], misc[:, M_QNW:M_QNW + HEAD_DIM],
                 misc[:, M_KNW:M_KNW + HEAD_DIM], misc[:, M_SINK:M_SINK + ATT_HEADS]]
        for k in range(n):
            outs[k][...] = grads[k]
            outs[n + k][...], outs[2 * n + k][...], outs[3 * n + k][...] = _adamw_math(
                w_refs[k][...], grads[k], m_refs[k][...], v_refs[k][...])

    shapes = [jax.ShapeDtypeStruct(w.shape, F32) for w in ws]
    return pl.pallas_call(
        body, name="adamw_replicated", in_specs=[_VMEM_WHOLE] * (2 + 3 * n), out_specs=[_VMEM_WHOLE] * (4 * n),
        out_shape=shapes * 4, compiler_params=_cp(),
    )(tot, g_c_ctx, *ws, *ms, *vs)


def _lb_grads(dlb, lbl):
    def body(d_ref, l_ref, o_ref):
        for d in (0, 1):
            ll = l_ref[d]
            lb = _sigmoid(ll[0:1, :] - ll[1:2, :])
            t = d_ref[d:d + 1, :] * lb * (1.0 - lb)
            o_ref[d, 0:1, :] = t
            o_ref[d, 1:2, :] = -t

    return pl.pallas_call(
        body, name="lb_grads", in_specs=[_VMEM_WHOLE] * 2, out_specs=_VMEM_WHOLE,
        out_shape=jax.ShapeDtypeStruct((2, 2, HGW), F32), compiler_params=_cp(),
    )(dlb, lbl)


def _c_ctx_grad(terms, c_ctx):
    def body(t_ref, c_ref, o_ref):
        tot = t_ref[0, 8:9, :]
        for s in range(1, N_DEV):
            tot = tot + t_ref[s, 8:9, :]
        cv = c_ref[...]
        sg = _sigmoid(cv)
        o_ref[...] = tot * (sg * (1.0 + cv * (1.0 - sg)))

    return pl.pallas_call(
        body, name="c_ctx_grad", in_specs=[_VMEM_WHOLE] * 2, out_specs=_VMEM_WHOLE,
        out_shape=jax.ShapeDtypeStruct((1, D), F32), compiler_params=_cp(),
    )(terms, c_ctx)


def _in_perm():
    fz, bz, inp, kk, vv, qhg, ghg, qat, gates = 0, 512, 1024, 1536, 1664, 1792, 2304, 2816, 3328
    cols = []
    for h in range(HG_HEADS):
        for base in (fz, bz, inp, qhg):
            cols += list(range(base + 128 * h, base + 128 * (h + 1)))
    cols += list(range(ghg, ghg + 512)) + list(range(qat, qat + 512))
    cols += list(range(kk, kk + 128)) + list(range(vv, vv + 128))
    cols += list(range(gates, gates + 2048))
    return np.asarray(cols, np.int32)


_PERM = _in_perm()
_INV_PERM = np.argsort(_PERM).astype(np.int32)


def _take_rows(w, perm):
    cuts = [0] + [i for i in range(1, len(perm)) if perm[i] != perm[i - 1] + 1] + [len(perm)]
    return jnp.concatenate([w[int(perm[a]):int(perm[b - 1]) + 1] for a, b in zip(cuts[:-1], cuts[1:])],
                           axis=0)


def _cols_from_blocks(g):
    return jnp.transpose(g, (1, 0, 2)).reshape(g.shape[1], N_DEV * g.shape[2])


def _local_step(x2, ctx2, tgt, lbl, sh_in, sc_in, gate1, sh2, sc2, gate2, norm_mix_w, norm_ffn_w,
                hgrn_norm_w, q_norm_w, k_norm_w, attn_sinks, w_a, w_b, w_c, s_bh, s_ba, s_out,
                s_gate, s_up, s_down):
    first_last = lambda n: [(0, True), (n - 1, False)]
    h_all = _norm_mod_all(ctx2, x2, norm_mix_w, sh_in, sc_in)
    p_a = _mm_nt(h_all, w_a, tm=768, tn=1024, out_dtype=F32, name="proj_a")
    (o, st), (g_gate,) = _hgrn_fwd(
        p_a, lbl, (_gather_comm([s_gate]), [(0, True), (HG_HEADS - 2, True), (HG_HEADS - 1, False)]))
    (p_b, p_c), (g_bh, g_ba, g_out) = _proj_bc(
        h_all, w_b, w_c, (_gather_comm([s_bh, s_ba, s_out]), [(0, True), (N_TILES - 2, True), (N_TILES - 1, False)]))
    cos, sin = _rope_tables()
    qnw_t, knw_t = jnp.tile(q_norm_w, (1, ATT_HEADS)), jnp.tile(k_norm_w, (1, KV_HEADS))
    y_hg, qn, k_rep, v_rep = _prep_fwd(p_b, o, cos, sin, hgrn_norm_w, qnw_t, knw_t)
    (y_at, lse), (g_up, g_down) = _attn_fwd(
        qn, k_rep, v_rep, attn_sinks,
        (_gather_comm([s_up, s_down]), [(0, True), (N_BLOCKS - 3, True), (N_BLOCKS - 1, False)]))
    w_bh, w_ba, w_o = g_bh.reshape(D, HGW), g_ba.reshape(D, ATW), g_out.reshape(D, D)
    g_gate, g_up, g_down = [g.reshape(N_FF_TILES, FF_TILE, D) for g in (g_gate, g_up, g_down)]
    a, b, mixed, r, x1, h2 = _merge_fwd(y_hg, y_at, p_c, x2, w_bh, w_ba, w_o, gate1, norm_ffn_w, sh2, sc2)

    act, d_gate, d_up, d_f, dx1, acc_ffn = _ffn_fused(x1, h2, tgt, g_gate, g_up, g_down, gate2,
                                                      norm_ffn_w, sc2)
    by_chip = lambda t: t.reshape((N_CHIPS, 2) + t.shape[1:])
    ff_by_chip = lambda t: t.reshape(N_CHIPS, 2, FF_BLK, D)
    t_down, _ = _mm_tn_blocked(act, d_f, "grad_down")
    t_down = ff_by_chip(t_down)
    t_gate, (f_down,) = _mm_tn_blocked(d_gate, h2, "grad_gate", (_sibling_comm([t_down]), first_last(N_FF_TILES)))
    t_gate = ff_by_chip(t_gate)
    t_up, (f_gate,) = _mm_tn_blocked(d_up, h2, "grad_up", (_sibling_comm([t_gate]), first_last(N_FF_TILES)))
    t_up = ff_by_chip(t_up)

    (d_r, d_a, d_b, dp_c, dy_hg, dy_at, acc_mg), (f_up,) = _merge_bwd(
        dx1, r, a, b, p_c, w_bh, w_ba, w_o, gate1, (_sibling_comm([t_up]), first_last(N_LAT_TILES)))
    c_down, c_gate, c_up = [_pair_sum(t, f, "pair_sum_" + nm) for t, f, nm in
                            ((t_down, f_down, "down"), (t_gate, f_gate, "gate"), (t_up, f_up, "up"))]
    t_out = _mm_tn(mixed, d_r, tk=512, nk=4, tm=512, tn=1024, out_dtype=BF16, name="grad_out")
    t_bh = _mm_tn(d_a, y_hg, tk=512, nk=4, tm=512, tn=512, out_dtype=BF16, name="grad_bh")
    t_ba = _mm_tn(d_b, y_at, tk=512, nk=4, tm=512, tn=512, out_dtype=BF16, name="grad_ba")
    t_bh, t_ba, t_out = [by_chip(t.reshape(N_DEV, D // N_DEV, t.shape[1])) for t in (t_bh, t_ba, t_out)]
    (dq, dk_rep, dv_rep, dsink), (r_up,) = _attn_bwd(
        qn, k_rep, v_rep, attn_sinks, y_at, lse, dy_at, (_chip_comm([c_up]), first_last(N_BLOCKS)))
    (dp_b, d_o, acc_prep), (f_bh, f_ba, f_out) = _prep_bwd(
        p_b, o, cos, sin, hgrn_norm_w, qnw_t, knw_t, dy_hg, dq, dk_rep, dv_rep,
        (_sibling_comm([t_bh, t_ba, t_out]), first_last(N_TILES)))
    c_bh, c_ba, c_out = [_pair_sum(t, f, "pair_sum_" + nm) for t, f, nm in
                         ((t_bh, f_bh, "bh"), (t_ba, f_ba, "ba"), (t_out, f_out, "out"))]
    (dp_a, dlb), (r_bh, r_ba, r_out, r_down, r_gate) = _hgrn_bwd(
        p_a, lbl, d_o, st, (_chip_comm([c_bh, c_ba, c_out, c_down, c_gate]), first_last(HG_HEADS)))
    t_a = _mm_tn(dp_a, h_all, tk=768, nk=3, tm=1024, tn=1024, out_dtype=BF16, name="grad_in_a")
    t_b = _mm_tn(dp_b, h_all, tk=768, nk=3, tm=640, tn=1024, out_dtype=BF16, name="grad_in_b")
    t_c = _mm_tn(dp_c, h_all, tk=256, nk=8, b_off=1, tm=1024, tn=1024, out_dtype=BF16, name="grad_in_c")
    t_in = by_chip(_take_rows(jnp.concatenate([t_a, t_b, t_c], axis=0), _INV_PERM).reshape(N_DEV, IN_BLK, D))
    (f_in,) = _run_comm(_sibling_comm([t_in]), "scatter_in_sibling")
    (grad_x, acc_in), (r_in,) = _input_bwd(
        dp_a, dp_b, dp_c, w_a, w_b, w_c, ctx2, x2, dx1, norm_mix_w, sh_in, sc_in,
        (_chip_comm([_pair_sum(t_in, f_in, "pair_sum_in")]), first_last(N_TILES)))
    small = _pack_small(acc_in, acc_mg, acc_ffn, acc_prep, dsink, dlb)
    return grad_x, small, [r_in, r_bh, r_ba, r_out, r_gate, r_up, r_down]


def kernel(x, c, ctx, c_ctx, w_ada, b_ada, norm_mix_w, norm_ffn_w, w_in, hgrn_lb_logits, hgrn_norm_w, q_norm_w, k_norm_w, attn_sinks, w_branch_hgrn, w_branch_attn, w_out, w_ffn_gate, w_ffn_up, w_ffn_down, loss_target, m_c_ctx, m_w_ada, m_b_ada, m_norm_mix_w, m_norm_ffn_w, m_w_in, m_hgrn_lb_logits, m_hgrn_norm_w, m_q_norm_w, m_k_norm_w, m_attn_sinks, m_w_branch_hgrn, m_w_branch_attn, m_w_out, m_w_ffn_gate, m_w_ffn_up, m_w_ffn_down, v_c_ctx, v_w_ada, v_b_ada, v_norm_mix_w, v_norm_ffn_w, v_w_in, v_hgrn_lb_logits, v_hgrn_norm_w, v_q_norm_w, v_k_norm_w, v_attn_sinks, v_w_branch_hgrn, v_w_branch_attn, v_w_out, v_w_ffn_gate, v_w_ffn_up, v_w_ffn_down):
    me = 4 * lax.axis_index("x") + 2 * lax.axis_index("y") + lax.axis_index("c")
    x2, ctx2, tgt = x[0], ctx[0], loss_target[0]
    w_ada2, w_in2 = w_ada[0], w_in[0]

    blk = jnp.zeros((8, D), F32).at[0].set(c[0]).at[1, :256].set(hgrn_lb_logits.reshape(256))
    (g0,) = _all_gather([blk], "gather_cond", True)
    cc = jnp.zeros((16, D), F32).at[:8].set(g0[:, 0, :]).at[8].set(c_ctx)
    lbl = jnp.transpose(g0[:, 1, :256].reshape(N_DEV, 2, 2, 64), (1, 2, 0, 3)).reshape(2, 2, HGW)

    b_cols = lax.dynamic_slice(b_ada, (0, me * ADA_BLK), (1, ADA_BLK))
    (g1,) = _all_gather([_ada_rows(cc, w_ada2, b_cols)], "gather_mod", True)
    mod_all = _cols_from_blocks(g1)
    mod = lax.dynamic_slice(mod_all, (me, 0), (1, 6 * D)).reshape(6, D)
    mod_c = mod_all[8].reshape(6, D)
    sh1, sc1, gate1, sh2, sc2, gate2 = [mod[k:k + 1] for k in range(6)]
    sh_in = jnp.concatenate([mod_c[0:1], sh1], axis=0)
    sc_in = jnp.concatenate([mod_c[1:2], sc1], axis=0)

    shards = [w_branch_hgrn[0].T, w_branch_attn[0].T, w_out[0], w_ffn_gate[0].T, w_ffn_up[0].T, w_ffn_down[0]]
    (g_in,) = _all_gather([w_in2.T.astype(BF16)], "gather_w_in", False)
    w_in_t = _take_rows(g_in.reshape(IN_COLS, D), _PERM)
    w_a, w_b, w_c = w_in_t[:WA], w_in_t[WA:WA + WB], w_in_t[WA + WB:]

    grad_x, small, (r_in, r_bh, r_ba, r_out, r_gate, r_up, r_down) = _local_step(
        x2, ctx2, tgt, lbl, sh_in, sc_in, gate1, sh2, sc2, gate2, norm_mix_w, norm_ffn_w, hgrn_norm_w,
        q_norm_w, k_norm_w, attn_sinks, w_a, w_b, w_c, *[s.astype(BF16) for s in shards])

    big = {}
    for nm, rr, ww, mm, vv, tr, transposed in (
            ("w_in", r_in, w_in2, m_w_in[0], v_w_in[0], 336, True),
            ("w_branch_hgrn", r_bh, w_branch_hgrn[0], m_w_branch_hgrn[0], v_w_branch_hgrn[0], 128, True),
            ("w_branch_attn", r_ba, w_branch_attn[0], m_w_branch_attn[0], v_w_branch_attn[0], 128, True),
            ("w_out", r_out, w_out[0], m_w_out[0], v_w_out[0], 128, False),
            ("w_ffn_gate", r_gate, w_ffn_gate[0], m_w_ffn_gate[0], v_w_ffn_gate[0], 352, True),
            ("w_ffn_up", r_up, w_ffn_up[0], m_w_ffn_up[0], v_w_ffn_up[0], 352, True),
            ("w_ffn_down", r_down, w_ffn_down[0], m_w_ffn_down[0], v_w_ffn_down[0], 352, False)):
        if transposed:
            res = _adamw_sharded(rr, ww.T, mm.T, vv.T, "adamw_" + nm, tr)
            big[nm] = [t.T[None] for t in res]
        else:
            big[nm] = [t[None] for t in _adamw_sharded(rr, ww, mm, vv, "adamw_" + nm, tr)]

    (g2,) = _all_gather([small], "gather_small", True)
    tot = _sum_small(g2)
    dm = jnp.zeros((16, 6 * D), F32).at[:8].set(g2[:, R_DMOD:R_DMOD + 6, :].reshape(N_DEV, 6 * D))
    dm = dm.at[8, :2 * D].set(tot[R_DCTX:R_DCTX + 2].reshape(2 * D))
    dm_cols = lax.dynamic_slice(dm, (0, me * ADA_BLK), (16, ADA_BLK))
    g_w_ada, dsc_term = _ada_grads(cc, dm_cols, w_ada2)
    (g3,) = _all_gather([dsc_term], "gather_cctx", True)
    g_c_ctx = _c_ctx_grad(g3, c_ctx.reshape(1, D))
    g_lbl = _lb_grads(tot[R_DLB:R_DLB + 2, :HGW], lbl)
    g_lb_mine = lax.dynamic_slice(g_lbl, (0, 0, me * 64), (2, 2, 64))
    misc = tot[R_MISC]
    loss = misc[M_LOSS]

    rep_out = _adamw_replicated(
        tot, g_c_ctx,
        [b_ada, c_ctx.reshape(1, D), norm_mix_w, norm_ffn_w, hgrn_norm_w, q_norm_w, k_norm_w, attn_sinks],
        [m_b_ada, m_c_ctx.reshape(1, D), m_norm_mix_w, m_norm_ffn_w, m_hgrn_norm_w, m_q_norm_w, m_k_norm_w,
         m_attn_sinks],
        [v_b_ada, v_c_ctx.reshape(1, D), v_norm_mix_w, v_norm_ffn_w, v_hgrn_norm_w, v_q_norm_w, v_k_norm_w,
         v_attn_sinks])
    rep = []
    for kind in range(4):
        vals = dict(zip(_REP_NAMES, rep_out[kind * len(_REP_NAMES):(kind + 1) * len(_REP_NAMES)]))
        vals["c_ctx"] = vals["c_ctx"].reshape(D)
        rep.append(vals)

    d_ada, nm_ada, nv_ada = _adamw_plain(g_w_ada, w_ada2, m_w_ada[0], v_w_ada[0], "adamw_w_ada")
    ada = [t[None] for t in (g_w_ada, d_ada, nm_ada, nv_ada)]
    lb_w = hgrn_lb_logits.reshape(4, 64)
    d_lb, nm_lb, nv_lb = _adamw_plain(g_lb_mine.reshape(4, 64), lb_w, m_hgrn_lb_logits.reshape(4, 64),
                                      v_hgrn_lb_logits.reshape(4, 64), "adamw_lb")
    lbs = [t.reshape(2, 2, 64) for t in (g_lb_mine, d_lb, nm_lb, nv_lb)]

    names = ['c_ctx', 'w_ada', 'b_ada', 'norm_mix_w', 'norm_ffn_w', 'w_in', 'hgrn_lb_logits', 'hgrn_norm_w',
             'q_norm_w', 'k_norm_w', 'attn_sinks', 'w_branch_hgrn', 'w_branch_attn', 'w_out', 'w_ffn_gate',
             'w_ffn_up', 'w_ffn_down']
    outs = [loss, grad_x[None]]
    for kind in range(4):
        for nm in names:
            if nm == 'w_ada':
                outs.append(ada[kind])
            elif nm == 'hgrn_lb_logits':
                outs.append(lbs[kind])
            elif nm in big:
                outs.append(big[nm][kind])
            else:
                outs.append(rep[kind][nm])
    return tuple(outs)
```

```python
import functools
import math

import numpy as np
import jax
import jax.numpy as jnp
from jax import lax
from jax.experimental import pallas as pl
from jax.experimental.pallas import tpu as pltpu

F32 = jnp.float32
BF16 = jnp.bfloat16

N_DEV = 8
D = 1024
S = 2048
L = 256
T = L + S
TM = 256
N_TILES = T // TM
N_LAT_TILES = S // TM
HG_HEADS = 4
HG_DIM = 128
HGW = 512
CHUNK = 32
N_CHUNKS = T // CHUNK
N_CTX_CHUNKS = L // CHUNK
N_LAT_CHUNKS = S // CHUNK
ATT_HEADS = 8
KV_HEADS = 2
HEAD_DIM = 64
ATW = 512
KVW = 128
BLOCK = 128
N_BLOCKS = S // BLOCK
GRID_W = 64
ROPE_THETA = 10000.0
D_FF = 2816
FF_BLK = D_FF // N_DEV
FF_TILE = 256
N_FF_TILES = D_FF // FF_TILE
IN_COLS = 5376
IN_BLK = IN_COLS // N_DEV
ADA_BLK = 6 * D // N_DEV
EPS = 1e-6
WA, WB, WC = 2048, 1280, 2048

ADAM_LR = 0.001
ADAM_B1 = 0.9
ADAM_B2 = 0.999
ADAM_EPS = 1e-08
ADAM_WD = 0.01
ADAM_STEP = 10

VMEM_LIMIT = 56 * 1024 * 1024
MESH = pl.DeviceIdType.MESH


def _cp(sem=None, vmem=VMEM_LIMIT):
    return pltpu.CompilerParams(dimension_semantics=sem, vmem_limit_bytes=vmem)


def _full(shape):
    n = len(shape)
    return pl.BlockSpec(shape, lambda *_: (0,) * n)


_VMEM_WHOLE = pl.BlockSpec(memory_space=pltpu.VMEM)
_ANY = pl.BlockSpec(memory_space=pl.ANY)


def _sigmoid(v):
    return 1.0 / (1.0 + jnp.exp(-v))


def _dot(a, b):
    return jnp.dot(a, b, preferred_element_type=F32)


def _dot_nt(a, b):
    return lax.dot_general(a, b, (((1,), (1,)), ((), ())), preferred_element_type=F32)


def _dot_tn(a, b):
    return lax.dot_general(a, b, (((0,), (0,)), ((), ())), preferred_element_type=F32)


def _split3(v):
    hi = v.astype(BF16)
    r = v - hi.astype(F32)
    mid = r.astype(BF16)
    lo = (r - mid.astype(F32)).astype(BF16)
    return hi, mid, lo


def _dot_exact_rhs01(v, m01):
    hi, mid, lo = _split3(v)
    return _dot(hi, m01) + _dot(mid, m01) + _dot(lo, m01)


def _split2(v):
    hi = v.astype(BF16)
    return hi, (v - hi.astype(F32)).astype(BF16)


def _dot_lhs01(m01, v):
    hi, lo = _split2(v)
    return _dot(m01, hi) + _dot(m01, lo)


def _dot_f32(a, b, dot=_dot):
    ah, am, al = _split3(a)
    bh, bm, bl = _split3(b)
    return (dot(ah, bh) + (dot(ah, bm) + dot(am, bh))
            + (dot(am, bm) + dot(ah, bl) + dot(al, bh)))


def _my_pos():
    return lax.axis_index("x"), lax.axis_index("y"), lax.axis_index("c")


class _Comm:
    def __init__(self, operands, out_shapes, sems, phases):
        self.operands, self.out_shapes, self.sems, self.phases = operands, out_shapes, sems, phases


def _gather_comm(blocks):
    n = len(blocks)

    def parts(ins, outs, sems):
        send_sems, recv_sems, local_sems = sems
        x, y, c = _my_pos()
        me, sibling = (x, y, c), (x, y, 1 - c)
        chips = [(1 - x, y), (x, 1 - y), (1 - x, 1 - y)]

        def slot(a, px, py, pc):
            return outs[a].at[4 * px + 2 * py + pc]

        def copy(a, k, block, to, src=None):
            return pltpu.make_async_remote_copy(
                src_ref=slot(a, *block) if src is None else src, dst_ref=slot(a, *block),
                send_sem=send_sems.at[a, k], recv_sem=recv_sems.at[a, k],
                device_id=to, device_id_type=MESH)

        mine = [pltpu.make_async_copy(ins[a], slot(a, *me), local_sems.at[a]) for a in range(n)]
        first = []
        for a in range(n):
            first.append(copy(a, 0, me, sibling, src=ins[a]))
            first += [copy(a, 1 + j, me, (*chip, c), src=ins[a]) for j, chip in enumerate(chips)]
        passed = [copy(a, 4 + j, (*chip, c), sibling) for j, chip in enumerate(chips) for a in range(n)]
        return c, me, sibling, chips, copy, mine, first, passed

    def start(ins, outs, sems):
        _, _, _, _, _, mine, first, _ = parts(ins, outs, sems)
        for cp in mine + first:
            cp.start()

    def forward(ins, outs, sems):
        c, me, _, chips, copy, _, _, passed = parts(ins, outs, sems)
        for j, chip in enumerate(chips):
            for a in range(n):
                copy(a, 1 + j, (*chip, c), me).wait_recv()
                passed[j * n + a].start()

    def finish(ins, outs, sems):
        c, me, sibling, chips, copy, mine, first, passed = parts(ins, outs, sems)
        for a in range(n):
            copy(a, 0, sibling, me).wait_recv()
            for j, chip in enumerate(chips):
                copy(a, 4 + j, (*chip, 1 - c), me).wait_recv()
        for cp in first + passed:
            cp.wait_send()
        for cp in mine:
            cp.wait()

    return _Comm(blocks, [jax.ShapeDtypeStruct((N_DEV,) + b.shape, b.dtype) for b in blocks],
                 [pltpu.SemaphoreType.DMA((n, 7)), pltpu.SemaphoreType.DMA((n, 7)), pltpu.SemaphoreType.DMA((n,))],
                 [start, forward, finish])


def _run_comm(comm, name, in_vmem=False):
    n_in, n_out = len(comm.operands), len(comm.out_shapes)

    def body(*refs):
        ins, outs, sems = refs[:n_in], refs[n_in:n_in + n_out], refs[n_in + n_out:]
        for phase in comm.phases:
            phase(ins, outs, sems)

    spec = _VMEM_WHOLE if in_vmem else _ANY
    return pl.pallas_call(
        body, name=name, out_shape=comm.out_shapes, in_specs=[spec] * n_in, out_specs=[spec] * n_out,
        scratch_shapes=comm.sems,
    )(*comm.operands)


def _carrier_call(body, comm, schedule, *, name, grid, in_specs, out_specs, out_shape, scratch_shapes, operands):
    n_in, n_out, n_scr = len(in_specs), len(out_specs), len(scratch_shapes)
    c_in, c_out = len(comm.operands), len(comm.out_shapes)

    def full_body(*refs):
        ins, refs = refs[:n_in], refs[n_in:]
        cins, refs = refs[:c_in], refs[c_in:]
        outs, refs = refs[:n_out], refs[n_out:]
        couts, refs = refs[:c_out], refs[c_out:]
        scr, csems = refs[:n_scr], refs[n_scr:]
        step = pl.program_id(0)

        def run(before):
            for (at, when_before), phase in zip(schedule, comm.phases):
                if when_before == before:
                    pl.when(step == at)(functools.partial(phase, cins, couts, csems))

        run(True)
        body(*ins, *outs, *scr)
        run(False)

    res = pl.pallas_call(
        full_body, name=name, grid=grid,
        in_specs=list(in_specs) + [_ANY] * c_in, out_specs=list(out_specs) + [_ANY] * c_out,
        out_shape=list(out_shape) + list(comm.out_shapes),
        scratch_shapes=list(scratch_shapes) + list(comm.sems),
        compiler_params=_cp(("arbitrary",)),
    )(*operands, *comm.operands)
    return res[:n_out], res[n_out:]


def _pcall(body, carried, *, name, grid, in_specs, out_specs, out_shape, scratch_shapes, operands):
    if carried is None:
        res = pl.pallas_call(body, name=name, grid=grid, in_specs=in_specs, out_specs=out_specs,
                             out_shape=out_shape, scratch_shapes=scratch_shapes,
                             compiler_params=_cp(("arbitrary",)))(*operands)
        return res, ()
    return _carrier_call(body, carried[0], carried[1], name=name, grid=grid, in_specs=in_specs,
                         out_specs=out_specs, out_shape=out_shape, scratch_shapes=scratch_shapes,
                         operands=operands)


def _all_gather(blocks, name, in_vmem):
    return _run_comm(_gather_comm(blocks), name, in_vmem)


N_CHIPS = 4


def _sibling_comm(contribs):
    n = len(contribs)

    def copies(ins, outs, sems):
        send_sems, recv_sems = sems
        x, y, c = _my_pos()
        return [pltpu.make_async_remote_copy(
            src_ref=ins[a].at[pl.ds(0, N_CHIPS), 1 - c], dst_ref=outs[a],
            send_sem=send_sems.at[a], recv_sem=recv_sems.at[a],
            device_id=(x, y, 1 - c), device_id_type=MESH) for a in range(n)]

    def start(ins, outs, sems):
        for cp in copies(ins, outs, sems):
            cp.start()

    def finish(ins, outs, sems):
        cps = copies(ins, outs, sems)
        for cp in cps:
            cp.wait_recv()
        for cp in cps:
            cp.wait_send()

    return _Comm(contribs, [jax.ShapeDtypeStruct((N_CHIPS,) + b.shape[2:], b.dtype) for b in contribs],
                 [pltpu.SemaphoreType.DMA((n,)), pltpu.SemaphoreType.DMA((n,))], [start, finish])


def _pair_sum(mine, theirs, name):
    _, _, rows, cols = mine.shape
    core = lax.axis_index("c").astype(jnp.int32).reshape(1)

    def body(c_ref, m_ref, t_ref, o_ref):
        o_ref[...] = (m_ref[...].astype(F32) + t_ref[...].astype(F32)).astype(BF16)

    return pl.pallas_call(
        body, name=name,
        grid_spec=pltpu.PrefetchScalarGridSpec(
            num_scalar_prefetch=1, grid=(N_CHIPS,),
            in_specs=[pl.BlockSpec((None, None, rows, cols), lambda q, c: (q, c[0], 0, 0)),
                      pl.BlockSpec((None, rows, cols), lambda q, c: (q, 0, 0))],
            out_specs=pl.BlockSpec((None, rows, cols), lambda q, c: (q, 0, 0))),
        out_shape=jax.ShapeDtypeStruct((N_CHIPS, rows, cols), BF16),
        compiler_params=_cp(("parallel",)),
    )(core, mine, theirs)


def _chip_comm(sums):
    n = len(sums)

    def parts(ins, outs, sems):
        send_sems, recv_sems, local_sems = sems
        x, y, c = _my_pos()
        q_me = 2 * x + y
        chips = [(1 - x, y), (x, 1 - y), (1 - x, 1 - y)]
        mine = [pltpu.make_async_copy(ins[a].at[q_me], outs[a].at[q_me], local_sems.at[a]) for a in range(n)]
        sends, recvs = [], []
        for j, (px, py) in enumerate(chips):
            for a in range(n):
                q = 2 * px + py
                sends.append(pltpu.make_async_remote_copy(
                    src_ref=ins[a].at[q], dst_ref=outs[a].at[q_me],
                    send_sem=send_sems.at[a, j], recv_sem=recv_sems.at[a, j],
                    device_id=(px, py, c), device_id_type=MESH))
                recvs.append(pltpu.make_async_remote_copy(
                    src_ref=ins[a].at[q], dst_ref=outs[a].at[q],
                    send_sem=send_sems.at[a, j], recv_sem=recv_sems.at[a, j],
                    device_id=(x, y, c), device_id_type=MESH))
        return mine, sends, recvs

    def start(ins, outs, sems):
        mine, sends, _ = parts(ins, outs, sems)
        for cp in mine + sends:
            cp.start()

    def finish(ins, outs, sems):
        mine, sends, recvs = parts(ins, outs, sems)
        for cp in recvs:
            cp.wait_recv()
        for cp in sends:
            cp.wait_send()
        for cp in mine:
            cp.wait()

    return _Comm(sums, [jax.ShapeDtypeStruct(b.shape, b.dtype) for b in sums],
                 [pltpu.SemaphoreType.DMA((n, 3)), pltpu.SemaphoreType.DMA((n, 3)), pltpu.SemaphoreType.DMA((n,))],
                 [start, finish])


def _mm_nt(a, bt, *, tm, tn, out_dtype, name, row_off=0, rows=None):
    rows = a.shape[0] if rows is None else rows
    n, k = bt.shape

    def body(a_ref, b_ref, o_ref):
        o_ref[...] = _dot_nt(a_ref[...], b_ref[...]).astype(out_dtype)

    return pl.pallas_call(
        body, name=name, grid=(rows // tm, n // tn),
        in_specs=[pl.BlockSpec((tm, k), lambda i, j: (i + row_off, 0)),
                  pl.BlockSpec((tn, k), lambda i, j: (j, 0))],
        out_specs=pl.BlockSpec((tm, tn), lambda i, j: (i, j)),
        out_shape=jax.ShapeDtypeStruct((rows, n), out_dtype),
        compiler_params=_cp(("parallel", "parallel")),
    )(a, bt)


def _mm_tn(a, b, *, tk, nk, tm, tn, out_dtype, name, a_off=0, b_off=0):
    m, n = a.shape[1], b.shape[1]

    def body(a_ref, b_ref, o_ref, acc):
        kk = pl.program_id(2)

        @pl.when(kk == 0)
        def _():
            acc[...] = jnp.zeros_like(acc)

        acc[...] += _dot_tn(a_ref[...], b_ref[...])

        @pl.when(kk == nk - 1)
        def _():
            o_ref[...] = acc[...].astype(out_dtype)

    return pl.pallas_call(
        body, name=name, grid=(m // tm, n // tn, nk),
        in_specs=[pl.BlockSpec((tk, tm), lambda i, j, kk: (kk + a_off, i)),
                  pl.BlockSpec((tk, tn), lambda i, j, kk: (kk + b_off, j))],
        out_specs=pl.BlockSpec((tm, tn), lambda i, j, kk: (i, j)),
        out_shape=jax.ShapeDtypeStruct((m, n), out_dtype),
        scratch_shapes=[pltpu.VMEM((tm, tn), F32)],
        compiler_params=_cp(("parallel", "parallel", "arbitrary")),
    )(a, b)


def _mm_tn_blocked(a, b, name, carried=None):
    nb, _, w = a.shape
    n = b.shape[1]

    def body(a_ref, b_ref, o_ref):
        o_ref[...] = _dot_tn(a_ref[...], b_ref[...]).astype(BF16)

    (out,), extra = _pcall(
        body, carried, name=name, grid=(nb,),
        in_specs=[pl.BlockSpec((None, S, w), lambda j: (j, 0, 0)), _full((S, n))],
        out_specs=[pl.BlockSpec((None, w, n), lambda j: (j, 0, 0))],
        out_shape=[jax.ShapeDtypeStruct((nb, w, n), BF16)],
        scratch_shapes=[], operands=[a, b])
    return out, extra


def _ada_rows(cc, w_ada, b_cols):
    def body(c_ref, w_ref, b_ref, o_ref):
        cv = c_ref[...]
        o_ref[...] = _dot_f32(cv * _sigmoid(cv), w_ref[...]) + b_ref[...]

    return pl.pallas_call(
        body, name="ada_rows",
        in_specs=[_VMEM_WHOLE] * 3, out_specs=_VMEM_WHOLE,
        out_shape=jax.ShapeDtypeStruct((16, ADA_BLK), F32),
        compiler_params=_cp(),
    )(cc, w_ada, b_cols)


def _ada_grads(cc, dm_cols, w_ada):
    def body(c_ref, dm_ref, w_ref, gw_ref, dsc_ref):
        cv = c_ref[...]
        sc = cv * _sigmoid(cv)
        dm = dm_ref[...]
        gw_ref[...] = _dot_f32(sc, dm, dot=_dot_tn)
        dsc_ref[...] = _dot_f32(dm, w_ref[...], dot=_dot_nt)

    return pl.pallas_call(
        body, name="ada_grads",
        in_specs=[_VMEM_WHOLE] * 3, out_specs=[_VMEM_WHOLE] * 2,
        out_shape=[jax.ShapeDtypeStruct((D, ADA_BLK), F32), jax.ShapeDtypeStruct((16, D), F32)],
        compiler_params=_cp(),
    )(cc, dm_cols, w_ada)


def _lat(i):
    return jnp.maximum(i - 1, 0)


def _rms_mod(xv, nw, sh, sc):
    rstd = lax.rsqrt(jnp.mean(xv * xv, axis=-1, keepdims=True) + EPS)
    return (xv * rstd * nw) * (1.0 + sc) + sh


def _rms_mod_bwd(xv, nw, sc, dh):
    rstd = lax.rsqrt(jnp.mean(xv * xv, axis=-1, keepdims=True) + EPS)
    xhat = xv * rstd
    dn = dh * (1.0 + sc)
    dxhat = dn * nw
    dx = rstd * (dxhat - xhat * jnp.mean(dxhat * xhat, axis=-1, keepdims=True))
    return (dx, jnp.sum(dh, axis=0, keepdims=True), jnp.sum(dh * (xhat * nw), axis=0, keepdims=True),
            jnp.sum(dn * xhat, axis=0, keepdims=True))


def _norm_mod_all(ctx, x, nw, sh, sc):
    def body(ctx_ref, x_ref, nw_ref, sh_ref, sc_ref, o_ref):
        i = pl.program_id(0)
        sel = jnp.minimum(i, 1)
        xv = jnp.where(i == 0, ctx_ref[...], x_ref[...])
        o_ref[...] = _rms_mod(xv, nw_ref[...], sh_ref[pl.ds(sel, 1), :], sc_ref[pl.ds(sel, 1), :]).astype(BF16)

    return pl.pallas_call(
        body, name="norm_mod", grid=(N_TILES,),
        in_specs=[_full((TM, D)), pl.BlockSpec((TM, D), lambda i: (_lat(i), 0)),
                  _full((1, D)), _full((2, D)), _full((2, D))],
        out_specs=pl.BlockSpec((TM, D), lambda i: (i, 0)),
        out_shape=jax.ShapeDtypeStruct((T, D), BF16),
        compiler_params=_cp(("parallel",)),
    )(ctx, x, nw, sh, sc)


def _chunk_masks(reverse):
    row = lax.broadcasted_iota(jnp.int32, (TM, TM), 0)
    col = lax.broadcasted_iota(jnp.int32, (TM, TM), 1)
    same = (row // CHUNK) == (col // CHUNK)
    tri = same & ((col >= row) if reverse else (col <= row))
    return same, tri


def _chunk_order(i, reverse):
    if not reverse:
        return i
    return jnp.where(i < N_CTX_CHUNKS, N_CTX_CHUNKS - 1 - i, N_CHUNKS + N_CTX_CHUNKS - 1 - i)


def _decay_terms(z, lb, same01, tri01):
    f = lb + (1.0 - lb) * _sigmoid(z)
    g = jnp.log(f)
    g2 = jnp.concatenate(_split2(g), axis=1)
    b2 = _dot(tri01, g2)
    t2 = _dot(same01, g2)
    return f, 1.0 - f, b2[:, :HG_DIM] + b2[:, HG_DIM:], t2[:, :HG_DIM] + t2[:, HG_DIM:]


def _chunk_outer(a, b):
    n = TM // CHUNK
    return jnp.einsum('ncv,nck->nvk', a.reshape(n, CHUNK, HG_DIM), b.reshape(n, CHUNK, HG_DIM),
                      preferred_element_type=F32)


def _hgrn_fwd(p_a, lbl, carried=None):
    cpt = TM // CHUNK

    def body(p_ref, lbl_ref, o_ref, st_ref, qd_s, kd_s, u_s, v_s, ebt_s):
        masks = [_chunk_masks(d == 1) for d in (0, 1)]
        same01 = jnp.where(masks[0][0], 1.0, 0.0).astype(BF16)
        tri = [m[1] for m in masks]
        tri01 = [jnp.where(t, 1.0, 0.0).astype(BF16) for t in tri]
        lb = [_sigmoid(lbl_ref[d][0:1, :] - lbl_ref[d][1:2, :]) for d in (0, 1)]

        def prep(r, carry):
            r0 = pl.multiple_of(r * TM, TM)
            vb = p_ref[pl.ds(r0, TM), 2 * HG_DIM:3 * HG_DIM].astype(BF16)
            v_s[pl.ds(r0, TM), :] = vb
            for d in (0, 1):
                z = p_ref[pl.ds(r0, TM), d * HG_DIM:(d + 1) * HG_DIM]
                _, k, b, bt = _decay_terms(z, lb[d], same01, tri01[d])
                u_s[d, pl.ds(r * cpt, cpt)] = _chunk_outer(vb, (k * jnp.exp(bt - b)).astype(BF16))
                ebt_s[d, pl.ds(r0, TM), :] = jnp.exp(bt)

                @pl.when(r >= 1)
                def _():
                    rl = pl.multiple_of(r0 - L, TM)
                    qr = p_ref[pl.ds(r0, TM), 3 * HG_DIM:4 * HG_DIM]
                    q = qr * _sigmoid(qr) * HG_DIM ** -0.5
                    qd_s[d, pl.ds(rl, TM), :] = (q * jnp.exp(b)).astype(BF16)
                    kd_s[d, pl.ds(rl, TM), :] = (k * jnp.exp(-b)).astype(BF16)

            return carry

        lax.fori_loop(0, N_TILES, prep, 0)

        def scan(i, sts):
            new = []
            for d in (0, 1):
                nn = _chunk_order(i, d == 1)
                c0 = pl.multiple_of(nn * CHUNK, CHUNK)
                st_ref[d, nn] = sts[d].astype(BF16)
                new.append(sts[d] * ebt_s[d, pl.ds(c0, 1), :] + u_s[d, nn])
            return tuple(new)

        zero = jnp.zeros((HG_DIM, HG_DIM), F32)
        lax.fori_loop(0, N_CHUNKS, scan, (zero, zero))

        def outp(r, carry):
            r0 = pl.multiple_of(r * TM, TM)
            vb = v_s[pl.ds(r0 + L, TM), :]
            o = jnp.zeros((TM, HG_DIM), F32)
            for d in (0, 1):
                qd = qd_s[d, pl.ds(r0, TM), :]
                a = jnp.where(tri[d], _dot_nt(qd, kd_s[d, pl.ds(r0, TM), :]), 0.0)
                stb = st_ref[d, pl.ds(N_CTX_CHUNKS + r * cpt, cpt)]
                inter = jnp.einsum('nck,nvk->ncv', qd.reshape(cpt, CHUNK, HG_DIM), stb,
                                   preferred_element_type=F32)
                o = o + _dot(a.astype(BF16), vb) + inter.reshape(TM, HG_DIM)
            o_ref[pl.ds(r0, TM), :] = o
            return carry

        lax.fori_loop(0, N_LAT_TILES, outp, 0)

    return _pcall(
        body, carried, name="hgrn_fwd", grid=(HG_HEADS,),
        in_specs=[pl.BlockSpec((T, 4 * HG_DIM), lambda h: (0, h)),
                  pl.BlockSpec((2, 2, HG_DIM), lambda h: (0, 0, h))],
        out_specs=[pl.BlockSpec((S, HG_DIM), lambda h: (0, h)),
                   pl.BlockSpec((2, None, N_CHUNKS, HG_DIM, HG_DIM), lambda h: (0, h, 0, 0, 0))],
        out_shape=[jax.ShapeDtypeStruct((S, HGW), F32),
                   jax.ShapeDtypeStruct((2, HG_HEADS, N_CHUNKS, HG_DIM, HG_DIM), BF16)],
        scratch_shapes=[pltpu.VMEM((2, S, HG_DIM), BF16), pltpu.VMEM((2, S, HG_DIM), BF16),
                        pltpu.VMEM((2, N_CHUNKS, HG_DIM, HG_DIM), F32), pltpu.VMEM((T, HG_DIM), BF16),
                        pltpu.VMEM((2, T, HG_DIM), F32)],
        operands=[p_a, lbl])


def _hgrn_bwd(p_a, lbl, d_o, st, carried=None):
    cpt = TM // CHUNK

    def rows(r):
        return r * TM if isinstance(r, int) else pl.multiple_of(r * TM, TM)

    def body(p_ref, lbl_ref, do_ref, st_ref, dp_ref, dlb_ref, b_s, bt_s, dbt_s, qd_s, dst_s, w_s):
        masks = [_chunk_masks(d == 1) for d in (0, 1)]
        same01 = jnp.where(masks[0][0], 1.0, 0.0).astype(BF16)
        tri = [m[1] for m in masks]
        tri01 = [jnp.where(t, 1.0, 0.0).astype(BF16) for t in tri]
        later01 = [tri01[1], tri01[0]]
        lb = [_sigmoid(lbl_ref[d][0:1, :] - lbl_ref[d][1:2, :]) for d in (0, 1)]

        def prep_tile(r, latent):
            r0 = rows(r)
            for d in (0, 1):
                z = p_ref[pl.ds(r0, TM), d * HG_DIM:(d + 1) * HG_DIM]
                _, _, b, bt = _decay_terms(z, lb[d], same01, tri01[d])
                b_s[d, pl.ds(r0, TM), :] = b
                bt_s[d, pl.ds(r0, TM), :] = bt
                if latent:
                    rl = pl.multiple_of(r0 - L, TM)
                    qr = p_ref[pl.ds(r0, TM), 3 * HG_DIM:4 * HG_DIM]
                    qd = (qr * _sigmoid(qr) * HG_DIM ** -0.5 * jnp.exp(b)).astype(BF16)
                    qd_s[d, pl.ds(rl, TM), :] = qd
                    w_s[d, pl.ds(r * cpt, cpt)] = _chunk_outer(
                        do_ref[pl.ds(rl, TM), :].astype(BF16), qd).astype(BF16)

        prep_tile(0, False)
        w_s[:, pl.ds(0, N_CTX_CHUNKS)] = jnp.zeros((2, N_CTX_CHUNKS, HG_DIM, HG_DIM), BF16)

        def prep(r, carry):
            prep_tile(r, True)
            return carry

        lax.fori_loop(1, N_TILES, prep, 0)

        def rscan(j, dsts):
            i = N_CHUNKS - 1 - j
            new = []
            for d in (0, 1):
                nn = _chunk_order(i, d == 1)
                c0 = pl.multiple_of(nn * CHUNK, CHUNK)
                dst_s[d, nn] = dsts[d].astype(BF16)
                after = st_ref[d, _chunk_order(jnp.minimum(i + 1, N_CHUNKS - 1), d == 1)].astype(F32)
                dbt_s[d, pl.ds(c0, CHUNK), :] = jnp.broadcast_to(
                    jnp.sum(after * dsts[d], axis=0, keepdims=True), (CHUNK, HG_DIM))
                new.append(dsts[d] * jnp.exp(bt_s[d, pl.ds(c0, 1), :]) + w_s[d, nn].astype(F32))
            return tuple(new)

        zero = jnp.zeros((HG_DIM, HG_DIM), F32)
        lax.fori_loop(0, N_CHUNKS, rscan, (zero, zero))

        def grad_tile(r, latent):
            r0 = rows(r)
            vb = p_ref[pl.ds(r0, TM), 2 * HG_DIM:3 * HG_DIM].astype(BF16)
            dv = jnp.zeros((TM, HG_DIM), F32)
            dq = jnp.zeros((TM, HG_DIM), F32)
            dlbs = []
            if latent:
                rl = pl.multiple_of(r0 - L, TM)
                qr = p_ref[pl.ds(r0, TM), 3 * HG_DIM:4 * HG_DIM]
                sq = _sigmoid(qr)
                do = do_ref[pl.ds(rl, TM), :].astype(BF16)
                da_full = _dot_nt(do, vb)
            for d in (0, 1):
                z = p_ref[pl.ds(r0, TM), d * HG_DIM:(d + 1) * HG_DIM]
                sz = _sigmoid(z)
                f = lb[d] + (1.0 - lb[d]) * sz
                k = 1.0 - f
                b = b_s[d, pl.ds(r0, TM), :]
                e2 = jnp.exp(bt_s[d, pl.ds(r0, TM), :] - b)
                dstb = dst_s[d, pl.ds(r * cpt, cpt)]
                kd2 = k * e2
                dkd2 = jnp.einsum('ncv,nvk->nck', vb.reshape(cpt, CHUNK, HG_DIM), dstb,
                                  preferred_element_type=F32).reshape(TM, HG_DIM)
                dv = dv + jnp.einsum('nck,nvk->ncv', kd2.astype(BF16).reshape(cpt, CHUNK, HG_DIM), dstb,
                                     preferred_element_type=F32).reshape(TM, HG_DIM)
                dk = dkd2 * e2
                db = -(kd2 * dkd2)
                if latent:
                    eb = jnp.exp(b)
                    enb = jnp.exp(-b)
                    qdf = qr * sq * HG_DIM ** -0.5 * eb
                    kdf = k * enb
                    qd = qd_s[d, pl.ds(rl, TM), :]
                    kd = kdf.astype(BF16)
                    a = jnp.where(tri[d], _dot_nt(qd, kd), 0.0).astype(BF16)
                    da = jnp.where(tri[d], da_full, 0.0).astype(BF16)
                    stb = st_ref[d, pl.ds(r * cpt, cpt)]
                    dqd = _dot(da, kd) + jnp.einsum(
                        'ncv,nvk->nck', do.reshape(cpt, CHUNK, HG_DIM), stb,
                        preferred_element_type=F32).reshape(TM, HG_DIM)
                    dkd = _dot_tn(da, qd)
                    dv = dv + _dot_tn(a, do)
                    dk = dk + dkd * enb
                    db = db + qdf * dqd - kdf * dkd
                    dq = dq + dqd * eb
                dg = _dot_lhs01(later01[d], db) + dbt_s[d, pl.ds(r0, TM), :]
                df = dg / f - dk
                dp_ref[pl.ds(r0, TM), d * HG_DIM:(d + 1) * HG_DIM] = (
                    df * (1.0 - lb[d]) * sz * (1.0 - sz)).astype(BF16)
                dlbs.append(jnp.sum(df * (1.0 - sz), axis=0, keepdims=True))
            dp_ref[pl.ds(r0, TM), 2 * HG_DIM:3 * HG_DIM] = dv.astype(BF16)
            if latent:
                dq = dq * (HG_DIM ** -0.5) * (sq * (1.0 + qr * (1.0 - sq)))
            dp_ref[pl.ds(r0, TM), 3 * HG_DIM:4 * HG_DIM] = dq.astype(BF16)
            return dlbs

        dlb_ctx = grad_tile(0, False)

        def grads(r, acc):
            t = grad_tile(r, True)
            return (acc[0] + t[0], acc[1] + t[1])

        dlb = lax.fori_loop(1, N_TILES, grads, (dlb_ctx[0], dlb_ctx[1]))
        dlb_ref[0:1, :] = dlb[0]
        dlb_ref[1:2, :] = dlb[1]

    return _pcall(
        body, carried, name="hgrn_bwd", grid=(HG_HEADS,),
        in_specs=[pl.BlockSpec((T, 4 * HG_DIM), lambda h: (0, h)),
                  pl.BlockSpec((2, 2, HG_DIM), lambda h: (0, 0, h)),
                  pl.BlockSpec((S, HG_DIM), lambda h: (0, h)),
                  pl.BlockSpec((2, None, N_CHUNKS, HG_DIM, HG_DIM), lambda h: (0, h, 0, 0, 0))],
        out_specs=[pl.BlockSpec((T, 4 * HG_DIM), lambda h: (0, h)),
                   pl.BlockSpec((2, HG_DIM), lambda h: (0, h))],
        out_shape=[jax.ShapeDtypeStruct((T, WA), BF16), jax.ShapeDtypeStruct((2, HGW), F32)],
        scratch_shapes=[pltpu.VMEM((2, T, HG_DIM), F32), pltpu.VMEM((2, T, HG_DIM), F32),
                        pltpu.VMEM((2, T, HG_DIM), F32), pltpu.VMEM((2, S, HG_DIM), BF16),
                        pltpu.VMEM((2, N_CHUNKS, HG_DIM, HG_DIM), BF16),
                        pltpu.VMEM((2, N_CHUNKS, HG_DIM, HG_DIM), BF16)],
        operands=[p_a, lbl, d_o, st])


def _rope_tables():
    t = np.arange(S)
    inv = ROPE_THETA ** (-np.arange(0, 32, 2, dtype=np.float64) / 32)
    lane = np.arange(64)
    pos = np.where(lane[None, :] < 32, (t // GRID_W)[:, None], (t % GRID_W)[:, None]).astype(np.float64)
    ang = pos * inv[(lane % 32) % 16][None, :]
    sign = np.where((lane % 32) < 16, -1.0, 1.0)[None, :]
    cos = np.tile(np.cos(ang), (1, 2)).astype(np.float32)
    sin = np.tile(np.sin(ang) * sign, (1, 2)).astype(np.float32)
    return jnp.asarray(cos), jnp.asarray(sin)


def _rope_partner(v):
    lane = lax.broadcasted_iota(jnp.int32, (1, 128), 1)
    first = (lane % 32) < 16
    slabs = []
    for j in range(v.shape[1] // 128):
        s = v[:, 128 * j:128 * (j + 1)]
        slabs.append(jnp.where(first, pltpu.roll(s, 112, 1), pltpu.roll(s, 16, 1)))
    return slabs[0] if len(slabs) == 1 else jnp.concatenate(slabs, axis=1)


def _group_ones(width, group):
    r = lax.broadcasted_iota(jnp.int32, (width, width), 0)
    c = lax.broadcasted_iota(jnp.int32, (width, width), 1)
    return jnp.where((r // group) == (c // group), 1.0, 0.0).astype(BF16)


def _group_mean(v, ones01, group):
    hi = v.astype(BF16)
    lo = (v - hi.astype(F32)).astype(BF16)
    return (_dot(hi, ones01) + _dot(lo, ones01)) * (1.0 / group)


def _rep_matrix():
    r = lax.broadcasted_iota(jnp.int32, (KVW, ATW), 0)
    c = lax.broadcasted_iota(jnp.int32, (KVW, ATW), 1)
    return jnp.where(r == HEAD_DIM * (c // 256) + c % HEAD_DIM, 1.0, 0.0).astype(BF16)


def _tile_lanes(v, reps):
    return jnp.concatenate([v] * reps, axis=1)


def _prep_fwd(p_b, o, cos, sin, hnw, qnw, knw):
    def body(p_ref, o_ref, cos_ref, sin_ref, hnw_ref, qnw_ref, knw_ref, y_ref, q_ref, k_ref, v_ref):
        i = pl.program_id(0)
        rep = _rep_matrix()
        ones_k = _group_ones(KVW, HEAD_DIM)
        kr = p_ref[:, 1024:1152]
        krstd = lax.rsqrt(_group_mean(kr * kr, ones_k, HEAD_DIM) + EPS)
        kn = kr * krstd * knw_ref[...]
        v_ref[...] = _dot(p_ref[:, 1152:1280].astype(BF16), rep).astype(BF16)

        @pl.when(i == 0)
        def _():
            k_ref[...] = _dot(kn.astype(BF16), rep).astype(BF16)

        @pl.when(i > 0)
        def _():
            cs, sn = cos_ref[...], sin_ref[...]
            kro = kn * cs + _rope_partner(kn) * sn
            k_ref[...] = _dot(kro.astype(BF16), rep).astype(BF16)
            qr = p_ref[:, 512:1024]
            qrstd = lax.rsqrt(_group_mean(qr * qr, _group_ones(ATW, HEAD_DIM), HEAD_DIM) + EPS)
            qn = qr * qrstd * qnw_ref[...]
            qro = qn * _tile_lanes(cs, 4) + _rope_partner(qn) * _tile_lanes(sn, 4)
            q_ref[...] = (qro * HEAD_DIM ** -0.5).astype(BF16)
            ys = []
            for h in range(HG_HEADS):
                oh = o_ref[:, HG_DIM * h:HG_DIM * (h + 1)]
                gh = p_ref[:, HG_DIM * h:HG_DIM * (h + 1)]
                rstd = lax.rsqrt(jnp.mean(oh * oh, axis=-1, keepdims=True) + EPS)
                ys.append(oh * rstd * hnw_ref[...] * (gh * _sigmoid(gh)))
            y_ref[...] = jnp.concatenate(ys, axis=1).astype(BF16)

    return pl.pallas_call(
        body, name="prep_fwd", grid=(N_TILES,),
        in_specs=[pl.BlockSpec((TM, WB), lambda i: (i, 0)),
                  pl.BlockSpec((TM, HGW), lambda i: (_lat(i), 0)),
                  pl.BlockSpec((TM, 128), lambda i: (_lat(i), 0)),
                  pl.BlockSpec((TM, 128), lambda i: (_lat(i), 0)),
                  _full((1, HG_DIM)), _full((1, ATW)), _full((1, KVW))],
        out_specs=[pl.BlockSpec((TM, HGW), lambda i: (_lat(i), 0)),
                   pl.BlockSpec((TM, ATW), lambda i: (_lat(i), 0)),
                   pl.BlockSpec((TM, ATW), lambda i: (i, 0)),
                   pl.BlockSpec((TM, ATW), lambda i: (i, 0))],
        out_shape=[jax.ShapeDtypeStruct((S, HGW), BF16), jax.ShapeDtypeStruct((S, ATW), BF16),
                   jax.ShapeDtypeStruct((T, ATW), BF16), jax.ShapeDtypeStruct((T, ATW), BF16)],
        compiler_params=_cp(("arbitrary",)),
    )(p_b, o, cos, sin, hnw, qnw, knw)


def _prep_bwd(p_b, o, cos, sin, hnw, qnw, knw, dy_hg, dq, dk_rep, dv_rep, carried=None):
    def body(p_ref, o_ref, cos_ref, sin_ref, hnw_ref, qnw_ref, knw_ref, dy_ref, dq_ref, dk_ref, dv_ref,
             dp_ref, do_ref, acc_ref):
        i = pl.program_id(0)

        @pl.when(i == 0)
        def _():
            acc_ref[...] = jnp.zeros_like(acc_ref)

        rep = _rep_matrix()
        ones_k = _group_ones(KVW, HEAD_DIM)

        def fold(v):
            hi = v.astype(BF16)
            lo = (v - hi.astype(F32)).astype(BF16)
            return _dot_nt(hi, rep) + _dot_nt(lo, rep)

        kr = p_ref[:, 1024:1152]
        krstd = lax.rsqrt(_group_mean(kr * kr, ones_k, HEAD_DIM) + EPS)
        khat = kr * krstd
        kw = knw_ref[...]
        dkro = fold(dk_ref[...])
        dv = fold(dv_ref[...])

        def k_back(dkn):
            dkhat = dkn * kw
            dkr = krstd * (dkhat - khat * _group_mean(dkhat * khat, ones_k, HEAD_DIM))
            acc_ref[2:3, 0:KVW] += jnp.sum(dkn * khat, axis=0, keepdims=True)
            dp_ref[:, 1024:1152] = dkr.astype(BF16)
            dp_ref[:, 1152:1280] = dv.astype(BF16)

        @pl.when(i == 0)
        def _():
            k_back(dkro)
            dp_ref[:, 0:1024] = jnp.zeros((TM, 1024), BF16)

        @pl.when(i > 0)
        def _():
            cs, sn = cos_ref[...], sin_ref[...]
            k_back(dkro * cs + _rope_partner(dkro * sn))
            ones_q = _group_ones(ATW, HEAD_DIM)
            qr = p_ref[:, 512:1024]
            qrstd = lax.rsqrt(_group_mean(qr * qr, ones_q, HEAD_DIM) + EPS)
            qhat = qr * qrstd
            dqro = dq_ref[...] * HEAD_DIM ** -0.5
            dqn = dqro * _tile_lanes(cs, 4) + _rope_partner(dqro * _tile_lanes(sn, 4))
            dqhat = dqn * qnw_ref[...]
            dqr = qrstd * (dqhat - qhat * _group_mean(dqhat * qhat, ones_q, HEAD_DIM))
            acc_ref[1:2, :] += jnp.sum(dqn * qhat, axis=0, keepdims=True)
            dp_ref[:, 512:1024] = dqr.astype(BF16)
            dws = jnp.zeros((1, HG_DIM), F32)
            for h in range(HG_HEADS):
                sl = slice(HG_DIM * h, HG_DIM * (h + 1))
                oh, gh, dy = o_ref[:, sl], p_ref[:, sl], dy_ref[:, sl]
                rstd = lax.rsqrt(jnp.mean(oh * oh, axis=-1, keepdims=True) + EPS)
                ohat = oh * rstd
                sg = _sigmoid(gh)
                dp_ref[:, sl] = (dy * (ohat * hnw_ref[...]) * (sg * (1.0 + gh * (1.0 - sg)))).astype(BF16)
                dn = dy * (gh * sg)
                dws = dws + jnp.sum(dn * ohat, axis=0, keepdims=True)
                dohat = dn * hnw_ref[...]
                do_ref[:, sl] = rstd * (dohat - ohat * jnp.mean(dohat * ohat, axis=-1, keepdims=True))
            acc_ref[0:1, 0:HG_DIM] += dws

    return _pcall(
        body, carried, name="prep_bwd", grid=(N_TILES,),
        in_specs=[pl.BlockSpec((TM, WB), lambda i: (i, 0)),
                  pl.BlockSpec((TM, HGW), lambda i: (_lat(i), 0)),
                  pl.BlockSpec((TM, 128), lambda i: (_lat(i), 0)),
                  pl.BlockSpec((TM, 128), lambda i: (_lat(i), 0)),
                  _full((1, HG_DIM)), _full((1, ATW)), _full((1, KVW)),
                  pl.BlockSpec((TM, HGW), lambda i: (_lat(i), 0)),
                  pl.BlockSpec((TM, ATW), lambda i: (_lat(i), 0)),
                  pl.BlockSpec((TM, ATW), lambda i: (i, 0)),
                  pl.BlockSpec((TM, ATW), lambda i: (i, 0))],
        out_specs=[pl.BlockSpec((TM, WB), lambda i: (i, 0)),
                   pl.BlockSpec((TM, HGW), lambda i: (_lat(i), 0)),
                   _full((8, ATW))],
        out_shape=[jax.ShapeDtypeStruct((T, WB), BF16), jax.ShapeDtypeStruct((S, HGW), F32),
                   jax.ShapeDtypeStruct((8, ATW), F32)],
        scratch_shapes=[], operands=[p_b, o, cos, sin, hnw, qnw, knw, dy_hg, dq, dk_rep, dv_rep])


NEG = -1e30
_CTX_BLOCKS = L // BLOCK


def _attn_window_specs():
    prev = pl.BlockSpec((BLOCK, ATW), lambda i: (jnp.maximum(i - 1, 0) + _CTX_BLOCKS, 0))
    own = pl.BlockSpec((BLOCK, ATW), lambda i: (i + _CTX_BLOCKS, 0))
    nxt = pl.BlockSpec((BLOCK, ATW), lambda i: (jnp.minimum(i + 1, N_BLOCKS - 1) + _CTX_BLOCKS, 0))
    return [prev, own, nxt, _full((L, ATW))]


def _attn_valid(i, heads=4):
    qi = lax.broadcasted_iota(jnp.int32, (heads * BLOCK, 3 * BLOCK), 0) % BLOCK
    kj = lax.broadcasted_iota(jnp.int32, (heads * BLOCK, 3 * BLOCK), 1)
    return ((jnp.abs(kj - BLOCK - qi) <= BLOCK) & ((kj >= BLOCK) | (i > 0))
            & ((kj < 2 * BLOCK) | (i < N_BLOCKS - 1)))


def _stack_heads(qg):
    lane = lax.broadcasted_iota(jnp.int32, (1, 256), 1) // HEAD_DIM
    return jnp.concatenate([jnp.where(lane == g, qg, jnp.zeros_like(qg)) for g in range(4)], axis=0)


def _unstack_heads(v4):
    lane = lax.broadcasted_iota(jnp.int32, (1, 256), 1) // HEAD_DIM
    out = jnp.where(lane == 0, v4[0:BLOCK], 0.0)
    for g in range(1, 4):
        out = out + jnp.where(lane == g, v4[g * BLOCK:(g + 1) * BLOCK], 0.0)
    return out


def _sink_rows(sink_ref, hk):
    return jnp.concatenate(
        [jnp.broadcast_to(sink_ref[0:1, 4 * hk + g:4 * hk + g + 1], (BLOCK, 1)) for g in range(4)], axis=0)


def _attn_fwd(q, k_rep, v_rep, sinks, carried=None):
    def body(q_ref, kp, ko, kn, kc, vp, vo, vn, vc, sink_ref, y_ref, lse_ref):
        i = pl.program_id(0)
        valid = _attn_valid(i, 1)
        lane8 = lax.broadcasted_iota(jnp.int32, (1, ATT_HEADS), 1)
        head_of_lane = lax.broadcasted_iota(jnp.int32, (1, 256), 1) // HEAD_DIM
        lse_out = jnp.zeros((BLOCK, ATT_HEADS), F32)
        for hk in range(KV_HEADS):
            sl = slice(256 * hk, 256 * (hk + 1))
            qg = q_ref[:, sl]
            kl = jnp.concatenate([kp[:, sl], ko[:, sl], kn[:, sl]], axis=0)
            vl = jnp.concatenate([vp[:, sl], vo[:, sl], vn[:, sl]], axis=0)
            yg = jnp.zeros((BLOCK, 256), F32)
            for g in range(4):
                q1 = jnp.where(head_of_lane == g, qg, jnp.zeros_like(qg))
                s_loc = jnp.where(valid, _dot_nt(q1, kl), NEG)
                s_ctx = _dot_nt(q1, kc[:, sl])
                sink = sink_ref[0:1, 4 * hk + g:4 * hk + g + 1]
                m = jnp.maximum(jnp.maximum(jnp.max(s_loc, axis=1, keepdims=True),
                                            jnp.max(s_ctx, axis=1, keepdims=True)), sink)
                p_loc = jnp.exp(s_loc - m)
                p_ctx = jnp.exp(s_ctx - m)
                den = (jnp.sum(p_loc, axis=1, keepdims=True) + jnp.sum(p_ctx, axis=1, keepdims=True)
                       + jnp.exp(sink - m))
                o1 = (_dot(p_loc.astype(BF16), vl) + _dot(p_ctx.astype(BF16), vc[:, sl])) * (1.0 / den)
                yg = yg + jnp.where(head_of_lane == g, o1, 0.0)
                lse_out = lse_out + jnp.where(lane8 == 4 * hk + g, m + jnp.log(den), 0.0)
            y_ref[:, sl] = yg.astype(BF16)
        lse_ref[...] = lse_out

    return _pcall(
        body, carried, name="attn_fwd", grid=(N_BLOCKS,),
        in_specs=[pl.BlockSpec((BLOCK, ATW), lambda i: (i, 0))] + _attn_window_specs()
        + _attn_window_specs() + [_full((1, ATT_HEADS))],
        out_specs=[pl.BlockSpec((BLOCK, ATW), lambda i: (i, 0)),
                   pl.BlockSpec((BLOCK, ATT_HEADS), lambda i: (i, 0))],
        out_shape=[jax.ShapeDtypeStruct((S, ATW), BF16), jax.ShapeDtypeStruct((S, ATT_HEADS), F32)],
        scratch_shapes=[],
        operands=[q, k_rep, k_rep, k_rep, k_rep, v_rep, v_rep, v_rep, v_rep, sinks])


def _attn_bwd(q, k_rep, v_rep, sinks, y_at, lse, dy, carried=None):
    def body(q_ref, kp, ko, kn, kc, vp, vo, vn, vc, sink_ref, y_ref, lse_ref, dy_ref,
             dq_ref, dk_ref, dv_ref, dsink_ref, dk_acc, dv_acc):
        i = pl.program_id(0)

        @pl.when(i == 0)
        def _():
            dk_acc[...] = jnp.zeros_like(dk_acc)
            dv_acc[...] = jnp.zeros_like(dv_acc)
            dk_ref[pl.ds(0, L), :] = jnp.zeros((L, ATW), F32)
            dv_ref[pl.ds(0, L), :] = jnp.zeros((L, ATW), F32)
            dsink_ref[...] = jnp.zeros_like(dsink_ref)

        valid = _attn_valid(i)
        lane8 = lax.broadcasted_iota(jnp.int32, (1, ATT_HEADS), 1)
        w0 = pl.multiple_of(i * BLOCK, BLOCK)
        dsink = jnp.zeros((1, ATT_HEADS), F32)
        for hk in range(KV_HEADS):
            sl = slice(256 * hk, 256 * (hk + 1))
            q4 = _stack_heads(q_ref[:, sl])
            do4f = _stack_heads(dy_ref[:, sl])
            o4 = _stack_heads(y_ref[:, sl]).astype(F32)
            do4 = do4f.astype(BF16)
            kl = jnp.concatenate([kp[:, sl], ko[:, sl], kn[:, sl]], axis=0)
            vl = jnp.concatenate([vp[:, sl], vo[:, sl], vn[:, sl]], axis=0)
            lse4 = jnp.concatenate(
                [jnp.sum(jnp.where(lane8 == 4 * hk + g, lse_ref[...], 0.0), axis=1, keepdims=True)
                 for g in range(4)], axis=0)
            p_loc = jnp.where(valid, jnp.exp(_dot_nt(q4, kl) - lse4), 0.0)
            p_ctx = jnp.exp(_dot_nt(q4, kc[:, sl]) - lse4)
            delta = jnp.sum(do4f * o4, axis=1, keepdims=True)
            ds_loc = (p_loc * (_dot_nt(do4, vl) - delta)).astype(BF16)
            ds_ctx = (p_ctx * (_dot_nt(do4, vc[:, sl]) - delta)).astype(BF16)
            dq_ref[:, sl] = _unstack_heads(_dot(ds_loc, kl) + _dot(ds_ctx, kc[:, sl]))
            dk_acc[pl.ds(w0, 3 * BLOCK), sl] += _dot_tn(ds_loc, q4)
            dv_acc[pl.ds(w0, 3 * BLOCK), sl] += _dot_tn(p_loc.astype(BF16), do4)
            dk_ref[pl.ds(0, L), sl] += _dot_tn(ds_ctx, q4)
            dv_ref[pl.ds(0, L), sl] += _dot_tn(p_ctx.astype(BF16), do4)
            p_sink = jnp.exp(_sink_rows(sink_ref, hk) - lse4)
            for g in range(4):
                rows = slice(g * BLOCK, (g + 1) * BLOCK)
                dsink = dsink + jnp.where(lane8 == 4 * hk + g,
                                          -jnp.sum(p_sink[rows] * delta[rows], axis=0, keepdims=True), 0.0)
        dsink_ref[...] += dsink

        @pl.when(i == N_BLOCKS - 1)
        def _():
            dk_ref[pl.ds(L, S), :] = dk_acc[pl.ds(BLOCK, S), :]
            dv_ref[pl.ds(L, S), :] = dv_acc[pl.ds(BLOCK, S), :]

    row_q = pl.BlockSpec((BLOCK, ATW), lambda i: (i, 0))
    return _pcall(
        body, carried, name="attn_bwd", grid=(N_BLOCKS,),
        in_specs=[row_q] + _attn_window_specs() + _attn_window_specs()
        + [_full((1, ATT_HEADS)), row_q, pl.BlockSpec((BLOCK, ATT_HEADS), lambda i: (i, 0)), row_q],
        out_specs=[row_q, _full((T, ATW)), _full((T, ATW)), _full((1, ATT_HEADS))],
        out_shape=[jax.ShapeDtypeStruct((S, ATW), F32), jax.ShapeDtypeStruct((T, ATW), F32),
                   jax.ShapeDtypeStruct((T, ATW), F32), jax.ShapeDtypeStruct((1, ATT_HEADS), F32)],
        scratch_shapes=[pltpu.VMEM((S + 2 * BLOCK, ATW), F32), pltpu.VMEM((S + 2 * BLOCK, ATW), F32)],
        operands=[q, k_rep, k_rep, k_rep, k_rep, v_rep, v_rep, v_rep, v_rep, sinks, y_at, lse, dy])


def _merge_fwd(y_hg, y_at, p_c, x, w_bh, w_ba, w_out, g1, nfw, sh2, sc2):
    def body(yh_ref, ya_ref, g_ref, x_ref, wbh_ref, wba_ref, wo_ref, g1_ref, nfw_ref, sh_ref, sc_ref,
             a_ref, b_ref, mx_ref, r_ref, x1_ref, h2_ref):
        a = _dot_nt(yh_ref[...], wbh_ref[...])
        b = _dot_nt(ya_ref[...], wba_ref[...])
        mixed = (_sigmoid(g_ref[:, :D]) * a + _sigmoid(g_ref[:, D:]) * b).astype(BF16)
        r = _dot(mixed, wo_ref[...])
        x1 = x_ref[...] + g1_ref[...] * r
        a_ref[...] = a
        b_ref[...] = b
        mx_ref[...] = mixed
        r_ref[...] = r
        x1_ref[...] = x1
        h2_ref[...] = _rms_mod(x1, nfw_ref[...], sh_ref[...], sc_ref[...]).astype(BF16)

    row = lambda w: pl.BlockSpec((TM, w), lambda i: (i, 0))
    vec = _full((1, D))
    return pl.pallas_call(
        body, name="merge_fwd", grid=(N_LAT_TILES,),
        in_specs=[row(HGW), row(ATW), row(WC), row(D), _VMEM_WHOLE, _VMEM_WHOLE, _VMEM_WHOLE,
                  vec, vec, vec, vec],
        out_specs=[row(D)] * 6,
        out_shape=[jax.ShapeDtypeStruct((S, D), dt) for dt in (F32, F32, BF16, F32, F32, BF16)],
        compiler_params=_cp(("parallel",)),
    )(y_hg, y_at, p_c, x, w_bh, w_ba, w_out, g1, nfw, sh2, sc2)


def _merge_bwd(dx1, r, a, b, p_c, w_bh, w_ba, w_out, g1, carried=None):
    def body(dx_ref, r_ref, a_ref, b_ref, g_ref, wbh_ref, wba_ref, wo_ref, g1_ref,
             dr_ref, da_ref, db_ref, dg_ref, dyh_ref, dya_ref, acc_ref):
        @pl.when(pl.program_id(0) == 0)
        def _():
            acc_ref[...] = jnp.zeros_like(acc_ref)

        dx1v = dx_ref[...]
        acc_ref[0:1, :] += jnp.sum(dx1v * r_ref[...], axis=0, keepdims=True)
        dr = (g1_ref[...] * dx1v).astype(BF16)
        dr_ref[...] = dr
        dmix = _dot_nt(dr, wo_ref[...])
        sh, sa = _sigmoid(g_ref[:, :D]), _sigmoid(g_ref[:, D:])
        da = (dmix * sh).astype(BF16)
        db = (dmix * sa).astype(BF16)
        da_ref[...] = da
        db_ref[...] = db
        dg_ref[:, :D] = (dmix * a_ref[...] * sh * (1.0 - sh)).astype(BF16)
        dg_ref[:, D:] = (dmix * b_ref[...] * sa * (1.0 - sa)).astype(BF16)
        dyh_ref[...] = _dot(da, wbh_ref[...])
        dya_ref[...] = _dot(db, wba_ref[...])

    row = lambda w: pl.BlockSpec((TM, w), lambda i: (i, 0))
    return _pcall(
        body, carried, name="merge_bwd", grid=(N_LAT_TILES,),
        in_specs=[row(D), row(D), row(D), row(D), row(WC), _VMEM_WHOLE, _VMEM_WHOLE, _VMEM_WHOLE,
                  _full((1, D))],
        out_specs=[row(D), row(D), row(D), row(WC), row(HGW), row(ATW), _full((8, D))],
        out_shape=[jax.ShapeDtypeStruct((S, D), BF16), jax.ShapeDtypeStruct((S, D), BF16),
                   jax.ShapeDtypeStruct((S, D), BF16), jax.ShapeDtypeStruct((S, WC), BF16),
                   jax.ShapeDtypeStruct((S, HGW), F32), jax.ShapeDtypeStruct((S, ATW), F32),
                   jax.ShapeDtypeStruct((8, D), F32)],
        scratch_shapes=[], operands=[dx1, r, a, b, p_c, w_bh, w_ba, w_out, g1])


def _ffn_fused(x1, h2, tgt, w_gate, w_up, w_down, g2, nfw, sc2):
    def body(x1_ref, h2_ref, t_ref, wg_ref, wu_ref, wd_ref, g2_ref, nfw_ref, sc_ref,
             act_ref, dgt_ref, dup_ref, df_ref, dx_ref, acc_ref, gs, us):
        @pl.when(pl.program_id(0) == 0)
        def _():
            acc_ref[...] = jnp.zeros_like(acc_ref)

        h2 = h2_ref[...]
        f = jnp.zeros((TM, D), F32)
        for j in range(N_FF_TILES):
            g = _dot_nt(h2, wg_ref[j])
            u = _dot_nt(h2, wu_ref[j])
            gs[j] = g
            us[j] = u
            act = (g * _sigmoid(g) * u).astype(BF16)
            act_ref[j] = act
            f = f + _dot(act, wd_ref[j])
        x1v = x1_ref[...]
        g2 = g2_ref[...]
        diff = x1v + g2 * f - t_ref[...]
        dy = diff * (1.0 / D)
        df = (g2 * dy).astype(BF16)
        df_ref[...] = df
        dh2 = jnp.zeros((TM, D), F32)
        for j in range(N_FF_TILES):
            g, u = gs[j], us[j]
            sg = _sigmoid(g)
            dact = _dot_nt(df, wd_ref[j])
            dgate = (dact * u * (sg * (1.0 + g * (1.0 - sg)))).astype(BF16)
            dup = (dact * (g * sg)).astype(BF16)
            dgt_ref[j] = dgate
            dup_ref[j] = dup
            dh2 = dh2 + _dot(dgate, wg_ref[j]) + _dot(dup, wu_ref[j])
        dx, dsh, dsc, dnw = _rms_mod_bwd(x1v, nfw_ref[...], sc_ref[...], dh2)
        dx_ref[...] = dy + dx
        acc_ref[0:1, :] += dsh
        acc_ref[1:2, :] += dsc
        acc_ref[2:3, :] += dnw
        acc_ref[3:4, :] += jnp.sum(dy * f, axis=0, keepdims=True)
        acc_ref[4:5, :] += 0.5 * jnp.sum(jnp.sum(diff * diff, axis=1, keepdims=True), axis=0,
                                         keepdims=True) * (1.0 / D)

    row = lambda dt_w: pl.BlockSpec((TM, dt_w), lambda i: (i, 0))
    blk = pl.BlockSpec((N_FF_TILES, TM, FF_TILE), lambda i: (0, i, 0))
    vec = _full((1, D))
    return pl.pallas_call(
        body, name="ffn_fused", grid=(N_LAT_TILES,),
        in_specs=[row(D), row(D), row(D), _VMEM_WHOLE, _VMEM_WHOLE, _VMEM_WHOLE, vec, vec, vec],
        out_specs=[blk, blk, blk, row(D), row(D), _full((8, D))],
        out_shape=[jax.ShapeDtypeStruct((N_FF_TILES, S, FF_TILE), BF16)] * 3
        + [jax.ShapeDtypeStruct((S, D), BF16), jax.ShapeDtypeStruct((S, D), F32),
           jax.ShapeDtypeStruct((8, D), F32)],
        scratch_shapes=[pltpu.VMEM((N_FF_TILES, TM, FF_TILE), F32), pltpu.VMEM((N_FF_TILES, TM, FF_TILE), F32)],
        compiler_params=_cp(("arbitrary",)),
    )(x1, h2, tgt, w_gate, w_up, w_down, g2, nfw, sc2)


def _proj_bc(h_all, w_b, w_c, carried=None):
    def body(h_ref, wb_ref, wc_ref, pb_ref, pc_ref):
        h = h_ref[...]
        pb_ref[...] = _dot_nt(h, wb_ref[...])

        @pl.when(pl.program_id(0) > 0)
        def _():
            pc_ref[...] = _dot_nt(h, wc_ref[...])

    return _pcall(
        body, carried, name="proj_bc", grid=(N_TILES,),
        in_specs=[pl.BlockSpec((TM, D), lambda i: (i, 0)), _VMEM_WHOLE, _VMEM_WHOLE],
        out_specs=[pl.BlockSpec((TM, WB), lambda i: (i, 0)), pl.BlockSpec((TM, WC), lambda i: (_lat(i), 0))],
        out_shape=[jax.ShapeDtypeStruct((T, WB), F32), jax.ShapeDtypeStruct((S, WC), F32)],
        scratch_shapes=[], operands=[h_all, w_b, w_c])


def _input_bwd(dp_a, dp_b, dp_c, w_a, w_b, w_c, ctx, x, dx1, nw, sh, sc, carried=None):
    def body(da_ref, db_ref, dc_ref, wa_ref, wb_ref, wc_ref, ctx_ref, x_ref, dx1_ref, nw_ref, sh_ref,
             sc_ref, gx_ref, acc_ref):
        i = pl.program_id(0)

        @pl.when(i == 0)
        def _():
            acc_ref[...] = jnp.zeros_like(acc_ref)

        dh = _dot(da_ref[...], wa_ref[...]) + _dot(db_ref[...], wb_ref[...])

        @pl.when(i == 0)
        def _():
            _, dsh, dsc, dnw = _rms_mod_bwd(ctx_ref[...], nw_ref[...], sc_ref[0:1, :], dh)
            acc_ref[3:4, :] += dsh
            acc_ref[4:5, :] += dsc
            acc_ref[2:3, :] += dnw

        @pl.when(i > 0)
        def _():
            dhl = dh + _dot(dc_ref[...], wc_ref[...])
            dx, dsh, dsc, dnw = _rms_mod_bwd(x_ref[...], nw_ref[...], sc_ref[1:2, :], dhl)
            gx_ref[...] = dx1_ref[...] + dx
            acc_ref[0:1, :] += dsh
            acc_ref[1:2, :] += dsc
            acc_ref[2:3, :] += dnw

    lat = lambda w: pl.BlockSpec((TM, w), lambda i: (_lat(i), 0))
    return _pcall(
        body, carried, name="input_bwd", grid=(N_TILES,),
        in_specs=[pl.BlockSpec((TM, WA), lambda i: (i, 0)), pl.BlockSpec((TM, WB), lambda i: (i, 0)),
                  lat(WC), _VMEM_WHOLE, _VMEM_WHOLE, _VMEM_WHOLE, _full((TM, D)), lat(D), lat(D),
                  _full((1, D)), _full((2, D)), _full((2, D))],
        out_specs=[lat(D), _full((8, D))],
        out_shape=[jax.ShapeDtypeStruct((S, D), F32), jax.ShapeDtypeStruct((8, D), F32)],
        scratch_shapes=[], operands=[dp_a, dp_b, dp_c, w_a, w_b, w_c, ctx, x, dx1, nw, sh, sc])


_C1 = 1.0 - ADAM_B1 ** ADAM_STEP
_C2 = 1.0 - ADAM_B2 ** ADAM_STEP


def _adamw_math(w, g, m, v):
    m = ADAM_B1 * m + (1.0 - ADAM_B1) * g
    v = ADAM_B2 * v + (1.0 - ADAM_B2) * (g * g)
    m_hat = m / _C1
    v_hat = v / _C2
    delta = -ADAM_LR * (m_hat / (jnp.sqrt(v_hat) + ADAM_EPS) + ADAM_WD * w)
    return delta, m, v


def _adamw_sharded(terms, w, m, v, name, tr):
    rows, cols = w.shape

    def body(t_ref, w_ref, m_ref, v_ref, g_ref, d_ref, nm_ref, nv_ref):
        g = t_ref[0].astype(F32)
        for s in range(1, N_CHIPS):
            g = g + t_ref[s].astype(F32)
        g_ref[...] = g
        d_ref[...], nm_ref[...], nv_ref[...] = _adamw_math(w_ref[...], g, m_ref[...], v_ref[...])

    blk = pl.BlockSpec((tr, cols), lambda i: (i, 0))
    return pl.pallas_call(
        body, name=name, grid=(rows // tr,),
        in_specs=[pl.BlockSpec((N_CHIPS, tr, cols), lambda i: (0, i, 0)), blk, blk, blk],
        out_specs=[blk] * 4,
        out_shape=[jax.ShapeDtypeStruct((rows, cols), F32)] * 4,
        compiler_params=_cp(("parallel",)),
    )(terms, w, m, v)


def _adamw_plain(g, w, m, v, name):
    def body(g_ref, w_ref, m_ref, v_ref, d_ref, nm_ref, nv_ref):
        d_ref[...], nm_ref[...], nv_ref[...] = _adamw_math(w_ref[...], g_ref[...], m_ref[...], v_ref[...])

    return pl.pallas_call(
        body, name=name, in_specs=[_VMEM_WHOLE] * 4, out_specs=[_VMEM_WHOLE] * 3,
        out_shape=[jax.ShapeDtypeStruct(w.shape, F32)] * 3,
        compiler_params=_cp(),
    )(g, w, m, v)


SMALL_ROWS = 16
R_DMOD, R_DCTX, R_NMIX, R_NFFN, R_MISC, R_DLB, R_BADA01 = 0, 6, 8, 9, 10, 11, 13
M_HNW, M_QNW, M_KNW, M_SINK, M_LOSS = 0, 128, 256, 384, 512


def _pack_small(acc_in, acc_mg, acc_ffn, acc_prep, dsink, dlb):
    def body(in_ref, mg_ref, ff_ref, pp_ref, ds_ref, dlb_ref, o_ref):
        o_ref[...] = jnp.zeros_like(o_ref)
        o_ref[0:2, :] = in_ref[0:2, :]
        o_ref[2:3, :] = mg_ref[0:1, :]
        o_ref[3:5, :] = ff_ref[0:2, :]
        o_ref[5:6, :] = ff_ref[3:4, :]
        o_ref[6:8, :] = in_ref[3:5, :]
        o_ref[8:9, :] = in_ref[2:3, :]
        o_ref[9:10, :] = ff_ref[2:3, :]
        o_ref[10:11, M_HNW:M_HNW + HG_DIM] = pp_ref[0:1, 0:HG_DIM]
        r = lax.broadcasted_iota(jnp.int32, (ATW, 128), 0)
        c = lax.broadcasted_iota(jnp.int32, (ATW, 128), 1)
        fold = jnp.where((r % HEAD_DIM == c) & (c < HEAD_DIM), 1.0, 0.0).astype(BF16)
        qk = jnp.concatenate([pp_ref[1:2, :], pp_ref[2:3, :], jnp.zeros((6, ATW), F32)], axis=0)
        folded = _dot_exact_rhs01(qk, fold)
        o_ref[10:11, M_QNW:M_QNW + 128] = folded[0:1, :]
        o_ref[10:11, M_KNW:M_KNW + 128] = folded[1:2, :]
        o_ref[10:11, M_SINK:M_SINK + ATT_HEADS] = ds_ref[...]
        o_ref[10:11, M_LOSS:M_LOSS + 128] = ff_ref[4:5, 0:128]
        o_ref[11:13, 0:HGW] = dlb_ref[...]

    return pl.pallas_call(
        body, name="pack_small", in_specs=[_VMEM_WHOLE] * 6, out_specs=_VMEM_WHOLE,
        out_shape=jax.ShapeDtypeStruct((SMALL_ROWS, D), F32), compiler_params=_cp(),
    )(acc_in, acc_mg, acc_ffn, acc_prep, dsink, dlb)


def _sum_small(gathered):
    def body(g_ref, o_ref):
        tot = g_ref[0]
        for s in range(1, N_DEV):
            tot = tot + g_ref[s]
        o_ref[...] = tot
        o_ref[R_BADA01:R_BADA01 + 2, :] = tot[0:2, :] + tot[R_DCTX:R_DCTX + 2, :]

    return pl.pallas_call(
        body, name="sum_small", in_specs=[_VMEM_WHOLE], out_specs=_VMEM_WHOLE,
        out_shape=jax.ShapeDtypeStruct((SMALL_ROWS, D), F32), compiler_params=_cp(),
    )(gathered)


_REP_NAMES = ("b_ada", "c_ctx", "norm_mix_w", "norm_ffn_w", "hgrn_norm_w", "q_norm_w", "k_norm_w", "attn_sinks")


def _adamw_replicated(tot, g_c_ctx, ws, ms, vs):
    n = len(_REP_NAMES)

    def body(*refs):
        tot_ref, gc_ref = refs[0], refs[1]
        w_refs, m_refs, v_refs = refs[2:2 + n], refs[2 + n:2 + 2 * n], refs[2 + 2 * n:2 + 3 * n]
        outs = refs[2 + 3 * n:]
        row = lambda r: tot_ref[r:r + 1, :]
        misc = row(R_MISC)
        grads = [jnp.concatenate([row(R_BADA01), row(R_BADA01 + 1)] + [row(k) for k in range(2, 6)], axis=1),
                 gc_ref[...], row(R_NMIX), row(R_NFFN),
                 misc[:, M_HNW:M_HNW + HG_DIM], misc[:, M_QNW:M_QNW + HEAD_DIM],
                 misc[:, M_KNW:M_KNW + HEAD_DIM], misc[:, M_SINK:M_SINK + ATT_HEADS]]
        for k in range(n):
            outs[k][...] = grads[k]
            outs[n + k][...], outs[2 * n + k][...], outs[3 * n + k][...] = _adamw_math(
                w_refs[k][...], grads[k], m_refs[k][...], v_refs[k][...])

    shapes = [jax.ShapeDtypeStruct(w.shape, F32) for w in ws]
    return pl.pallas_call(
        body, name="adamw_replicated", in_specs=[_VMEM_WHOLE] * (2 + 3 * n), out_specs=[_VMEM_WHOLE] * (4 * n),
        out_shape=shapes * 4, compiler_params=_cp(),
    )(tot, g_c_ctx, *ws, *ms, *vs)


def _lb_grads(dlb, lbl):
    def body(d_ref, l_ref, o_ref):
        for d in (0, 1):
            ll = l_ref[d]
            lb = _sigmoid(ll[0:1, :] - ll[1:2, :])
            t = d_ref[d:d + 1, :] * lb * (1.0 - lb)
            o_ref[d, 0:1, :] = t
            o_ref[d, 1:2, :] = -t

    return pl.pallas_call(
        body, name="lb_grads", in_specs=[_VMEM_WHOLE] * 2, out_specs=_VMEM_WHOLE,
        out_shape=jax.ShapeDtypeStruct((2, 2, HGW), F32), compiler_params=_cp(),
    )(dlb, lbl)


def _c_ctx_grad(terms, c_ctx):
    def body(t_ref, c_ref, o_ref):
        tot = t_ref[0, 8:9, :]
        for s in range(1, N_DEV):
            tot = tot + t_ref[s, 8:9, :]
        cv = c_ref[...]
        sg = _sigmoid(cv)
        o_ref[...] = tot * (sg * (1.0 + cv * (1.0 - sg)))

    return pl.pallas_call(
        body, name="c_ctx_grad", in_specs=[_VMEM_WHOLE] * 2, out_specs=_VMEM_WHOLE,
        out_shape=jax.ShapeDtypeStruct((1, D), F32), compiler_params=_cp(),
    )(terms, c_ctx)


def _in_perm():
    fz, bz, inp, kk, vv, qhg, ghg, qat, gates = 0, 512, 1024, 1536, 1664, 1792, 2304, 2816, 3328
    cols = []
    for h in range(HG_HEADS):
        for base in (fz, bz, inp, qhg):
            cols += list(range(base + 128 * h, base + 128 * (h + 1)))
    cols += list(range(ghg, ghg + 512)) + list(range(qat, qat + 512))
    cols += list(range(kk, kk + 128)) + list(range(vv, vv + 128))
    cols += list(range(gates, gates + 2048))
    return np.asarray(cols, np.int32)


_PERM = _in_perm()
_INV_PERM = np.argsort(_PERM).astype(np.int32)


def _row_runs(perm, src_edges=(), dst_edges=()):
    cuts = [0] + [i for i in range(1, len(perm))
                  if perm[i] != perm[i - 1] + 1 or perm[i] in src_edges or i in dst_edges] + [len(perm)]
    return [(int(perm[a]), a, b - a) for a, b in zip(cuts[:-1], cuts[1:])]


def _copy_row_runs(inputs, out_rows, runs, name):
    n_in, n_out = len(inputs), len(out_rows)

    def body(*refs):
        ins, outs, sem = refs[:n_in], refs[n_in:n_in + n_out], refs[n_in + n_out]
        copies = [pltpu.make_async_copy(ins[i].at[pl.ds(s, n)], outs[o].at[pl.ds(d, n)], sem.at[k])
                  for k, (i, s, o, d, n) in enumerate(runs)]
        for cp in copies:
            cp.start()
        for cp in copies:
            cp.wait()

    cols, dtype = inputs[0].shape[1], inputs[0].dtype
    return pl.pallas_call(
        body, name=name, in_specs=[_ANY] * n_in, out_specs=[_ANY] * n_out,
        out_shape=[jax.ShapeDtypeStruct((r, cols), dtype) for r in out_rows],
        scratch_shapes=[pltpu.SemaphoreType.DMA((len(runs),))],
    )(*inputs)


def _locate(pos, n, sizes):
    starts = np.concatenate([[0], np.cumsum(sizes)])
    a = int(np.searchsorted(starts, pos, side="right") - 1)
    assert pos + n <= starts[a + 1], "a run straddles two arrays"
    return a, pos - int(starts[a])


_ABC = (WA, WB, WC)
_EDGES = (WA, WA + WB)
_RUNS_TO_ABC = [(0, src) + _locate(dst, n, _ABC) + (n,) for src, dst, n in _row_runs(_PERM, dst_edges=_EDGES)]
_RUNS_FROM_ABC = [_locate(src, n, _ABC) + (0, dst, n)
                  for src, dst, n in _row_runs(_INV_PERM, src_edges=_EDGES)]


def _cols_from_blocks(g):
    return jnp.transpose(g, (1, 0, 2)).reshape(g.shape[1], N_DEV * g.shape[2])


def _local_step(x2, ctx2, tgt, lbl, sh_in, sc_in, gate1, sh2, sc2, gate2, norm_mix_w, norm_ffn_w,
                hgrn_norm_w, q_norm_w, k_norm_w, attn_sinks, w_a, w_b, w_c, s_bh, s_ba, s_out,
                s_gate, s_up, s_down):
    first_last = lambda n: [(0, True), (n - 1, False)]
    h_all = _norm_mod_all(ctx2, x2, norm_mix_w, sh_in, sc_in)
    p_a = _mm_nt(h_all, w_a, tm=768, tn=1024, out_dtype=F32, name="proj_a")
    (o, st), (g_gate,) = _hgrn_fwd(
        p_a, lbl, (_gather_comm([s_gate]), [(0, True), (HG_HEADS - 2, True), (HG_HEADS - 1, False)]))
    (p_b, p_c), (g_bh, g_ba, g_out) = _proj_bc(
        h_all, w_b, w_c, (_gather_comm([s_bh, s_ba, s_out]), [(0, True), (N_TILES - 2, True), (N_TILES - 1, False)]))
    cos, sin = _rope_tables()
    qnw_t, knw_t = jnp.tile(q_norm_w, (1, ATT_HEADS)), jnp.tile(k_norm_w, (1, KV_HEADS))
    y_hg, qn, k_rep, v_rep = _prep_fwd(p_b, o, cos, sin, hgrn_norm_w, qnw_t, knw_t)
    (y_at, lse), (g_up, g_down) = _attn_fwd(
        qn, k_rep, v_rep, attn_sinks,
        (_gather_comm([s_up, s_down]), [(0, True), (N_BLOCKS - 3, True), (N_BLOCKS - 1, False)]))
    w_bh, w_ba, w_o = g_bh.reshape(D, HGW), g_ba.reshape(D, ATW), g_out.reshape(D, D)
    g_gate, g_up, g_down = [g.reshape(N_FF_TILES, FF_TILE, D) for g in (g_gate, g_up, g_down)]
    a, b, mixed, r, x1, h2 = _merge_fwd(y_hg, y_at, p_c, x2, w_bh, w_ba, w_o, gate1, norm_ffn_w, sh2, sc2)

    act, d_gate, d_up, d_f, dx1, acc_ffn = _ffn_fused(x1, h2, tgt, g_gate, g_up, g_down, gate2,
                                                      norm_ffn_w, sc2)
    by_chip = lambda t: t.reshape((N_CHIPS, 2) + t.shape[1:])
    ff_by_chip = lambda t: t.reshape(N_CHIPS, 2, FF_BLK, D)
    t_down, _ = _mm_tn_blocked(act, d_f, "grad_down")
    t_down = ff_by_chip(t_down)
    t_gate, (f_down,) = _mm_tn_blocked(d_gate, h2, "grad_gate", (_sibling_comm([t_down]), first_last(N_FF_TILES)))
    t_gate = ff_by_chip(t_gate)
    t_up, (f_gate,) = _mm_tn_blocked(d_up, h2, "grad_up", (_sibling_comm([t_gate]), first_last(N_FF_TILES)))
    t_up = ff_by_chip(t_up)

    (d_r, d_a, d_b, dp_c, dy_hg, dy_at, acc_mg), (f_up,) = _merge_bwd(
        dx1, r, a, b, p_c, w_bh, w_ba, w_o, gate1, (_sibling_comm([t_up]), first_last(N_LAT_TILES)))
    c_down, c_gate, c_up = [_pair_sum(t, f, "pair_sum_" + nm) for t, f, nm in
                            ((t_down, f_down, "down"), (t_gate, f_gate, "gate"), (t_up, f_up, "up"))]
    t_out = _mm_tn(mixed, d_r, tk=512, nk=4, tm=512, tn=1024, out_dtype=BF16, name="grad_out")
    t_bh = _mm_tn(d_a, y_hg, tk=512, nk=4, tm=512, tn=512, out_dtype=BF16, name="grad_bh")
    t_ba = _mm_tn(d_b, y_at, tk=512, nk=4, tm=512, tn=512, out_dtype=BF16, name="grad_ba")
    t_bh, t_ba, t_out = [by_chip(t.reshape(N_DEV, D // N_DEV, t.shape[1])) for t in (t_bh, t_ba, t_out)]
    (dq, dk_rep, dv_rep, dsink), (r_up,) = _attn_bwd(
        qn, k_rep, v_rep, attn_sinks, y_at, lse, dy_at, (_chip_comm([c_up]), first_last(N_BLOCKS)))
    (dp_b, d_o, acc_prep), (f_bh, f_ba, f_out) = _prep_bwd(
        p_b, o, cos, sin, hgrn_norm_w, qnw_t, knw_t, dy_hg, dq, dk_rep, dv_rep,
        (_sibling_comm([t_bh, t_ba, t_out]), first_last(N_TILES)))
    c_bh, c_ba, c_out = [_pair_sum(t, f, "pair_sum_" + nm) for t, f, nm in
                         ((t_bh, f_bh, "bh"), (t_ba, f_ba, "ba"), (t_out, f_out, "out"))]
    (dp_a, dlb), (r_bh, r_ba, r_out, r_down, r_gate) = _hgrn_bwd(
        p_a, lbl, d_o, st, (_chip_comm([c_bh, c_ba, c_out, c_down, c_gate]), first_last(HG_HEADS)))
    t_a = _mm_tn(dp_a, h_all, tk=768, nk=3, tm=1024, tn=1024, out_dtype=BF16, name="grad_in_a")
    t_b = _mm_tn(dp_b, h_all, tk=768, nk=3, tm=640, tn=1024, out_dtype=BF16, name="grad_in_b")
    t_c = _mm_tn(dp_c, h_all, tk=256, nk=8, b_off=1, tm=1024, tn=1024, out_dtype=BF16, name="grad_in_c")
    (t_in,) = _copy_row_runs([t_a, t_b, t_c], [IN_COLS], _RUNS_FROM_ABC, "order_in_terms")
    t_in = by_chip(t_in.reshape(N_DEV, IN_BLK, D))
    (f_in,) = _run_comm(_sibling_comm([t_in]), "scatter_in_sibling")
    (grad_x, acc_in), (r_in,) = _input_bwd(
        dp_a, dp_b, dp_c, w_a, w_b, w_c, ctx2, x2, dx1, norm_mix_w, sh_in, sc_in,
        (_chip_comm([_pair_sum(t_in, f_in, "pair_sum_in")]), first_last(N_TILES)))
    small = _pack_small(acc_in, acc_mg, acc_ffn, acc_prep, dsink, dlb)
    return grad_x, small, [r_in, r_bh, r_ba, r_out, r_gate, r_up, r_down]


def kernel(x, c, ctx, c_ctx, w_ada, b_ada, norm_mix_w, norm_ffn_w, w_in, hgrn_lb_logits, hgrn_norm_w, q_norm_w, k_norm_w, attn_sinks, w_branch_hgrn, w_branch_attn, w_out, w_ffn_gate, w_ffn_up, w_ffn_down, loss_target, m_c_ctx, m_w_ada, m_b_ada, m_norm_mix_w, m_norm_ffn_w, m_w_in, m_hgrn_lb_logits, m_hgrn_norm_w, m_q_norm_w, m_k_norm_w, m_attn_sinks, m_w_branch_hgrn, m_w_branch_attn, m_w_out, m_w_ffn_gate, m_w_ffn_up, m_w_ffn_down, v_c_ctx, v_w_ada, v_b_ada, v_norm_mix_w, v_norm_ffn_w, v_w_in, v_hgrn_lb_logits, v_hgrn_norm_w, v_q_norm_w, v_k_norm_w, v_attn_sinks, v_w_branch_hgrn, v_w_branch_attn, v_w_out, v_w_ffn_gate, v_w_ffn_up, v_w_ffn_down):
    me = 4 * lax.axis_index("x") + 2 * lax.axis_index("y") + lax.axis_index("c")
    x2, ctx2, tgt = x[0], ctx[0], loss_target[0]
    w_ada2, w_in2 = w_ada[0], w_in[0]

    blk = jnp.zeros((8, D), F32).at[0].set(c[0]).at[1, :256].set(hgrn_lb_logits.reshape(256))
    (g0,) = _all_gather([blk], "gather_cond", True)
    cc = jnp.zeros((16, D), F32).at[:8].set(g0[:, 0, :]).at[8].set(c_ctx)
    lbl = jnp.transpose(g0[:, 1, :256].reshape(N_DEV, 2, 2, 64), (1, 2, 0, 3)).reshape(2, 2, HGW)

    b_cols = lax.dynamic_slice(b_ada, (0, me * ADA_BLK), (1, ADA_BLK))
    (g1,) = _all_gather([_ada_rows(cc, w_ada2, b_cols)], "gather_mod", True)
    mod_all = _cols_from_blocks(g1)
    mod = lax.dynamic_slice(mod_all, (me, 0), (1, 6 * D)).reshape(6, D)
    mod_c = mod_all[8].reshape(6, D)
    sh1, sc1, gate1, sh2, sc2, gate2 = [mod[k:k + 1] for k in range(6)]
    sh_in = jnp.concatenate([mod_c[0:1], sh1], axis=0)
    sc_in = jnp.concatenate([mod_c[1:2], sc1], axis=0)

    shards = [w_branch_hgrn[0].T, w_branch_attn[0].T, w_out[0], w_ffn_gate[0].T, w_ffn_up[0].T, w_ffn_down[0]]
    (g_in,) = _all_gather([w_in2.T.astype(BF16)], "gather_w_in", False)
    w_a, w_b, w_c = _copy_row_runs([g_in.reshape(IN_COLS, D)], _ABC, _RUNS_TO_ABC, "order_w_in")

    grad_x, small, (r_in, r_bh, r_ba, r_out, r_gate, r_up, r_down) = _local_step(
        x2, ctx2, tgt, lbl, sh_in, sc_in, gate1, sh2, sc2, gate2, norm_mix_w, norm_ffn_w, hgrn_norm_w,
        q_norm_w, k_norm_w, attn_sinks, w_a, w_b, w_c, *[s.astype(BF16) for s in shards])

    big = {}
    for nm, rr, ww, mm, vv, tr, transposed in (
            ("w_in", r_in, w_in2, m_w_in[0], v_w_in[0], 336, True),
            ("w_branch_hgrn", r_bh, w_branch_hgrn[0], m_w_branch_hgrn[0], v_w_branch_hgrn[0], 128, True),
            ("w_branch_attn", r_ba, w_branch_attn[0], m_w_branch_attn[0], v_w_branch_attn[0], 128, True),
            ("w_out", r_out, w_out[0], m_w_out[0], v_w_out[0], 128, False),
            ("w_ffn_gate", r_gate, w_ffn_gate[0], m_w_ffn_gate[0], v_w_ffn_gate[0], 352, True),
            ("w_ffn_up", r_up, w_ffn_up[0], m_w_ffn_up[0], v_w_ffn_up[0], 352, True),
            ("w_ffn_down", r_down, w_ffn_down[0], m_w_ffn_down[0], v_w_ffn_down[0], 352, False)):
        if transposed:
            res = _adamw_sharded(rr, ww.T, mm.T, vv.T, "adamw_" + nm, tr)
            big[nm] = [t.T[None] for t in res]
        else:
            big[nm] = [t[None] for t in _adamw_sharded(rr, ww, mm, vv, "adamw_" + nm, tr)]

    (g2,) = _all_gather([small], "gather_small", True)
    tot = _sum_small(g2)
    dm = jnp.zeros((16, 6 * D), F32).at[:8].set(g2[:, R_DMOD:R_DMOD + 6, :].reshape(N_DEV, 6 * D))
    dm = dm.at[8, :2 * D].set(tot[R_DCTX:R_DCTX + 2].reshape(2 * D))
    dm_cols = lax.dynamic_slice(dm, (0, me * ADA_BLK), (16, ADA_BLK))
    g_w_ada, dsc_term = _ada_grads(cc, dm_cols, w_ada2)
    (g3,) = _all_gather([dsc_term], "gather_cctx", True)
    g_c_ctx = _c_ctx_grad(g3, c_ctx.reshape(1, D))
    g_lbl = _lb_grads(tot[R_DLB:R_DLB + 2, :HGW], lbl)
    g_lb_mine = lax.dynamic_slice(g_lbl, (0, 0, me * 64), (2, 2, 64))
    misc = tot[R_MISC]
    loss = misc[M_LOSS]

    rep_out = _adamw_replicated(
        tot, g_c_ctx,
        [b_ada, c_ctx.reshape(1, D), norm_mix_w, norm_ffn_w, hgrn_norm_w, q_norm_w, k_norm_w, attn_sinks],
        [m_b_ada, m_c_ctx.reshape(1, D), m_norm_mix_w, m_norm_ffn_w, m_hgrn_norm_w, m_q_norm_w, m_k_norm_w,
         m_attn_sinks],
        [v_b_ada, v_c_ctx.reshape(1, D), v_norm_mix_w, v_norm_ffn_w, v_hgrn_norm_w, v_q_norm_w, v_k_norm_w,
         v_attn_sinks])
    rep = []
    for kind in range(4):
        vals = dict(zip(_REP_NAMES, rep_out[kind * len(_REP_NAMES):(kind + 1) * len(_REP_NAMES)]))
        vals["c_ctx"] = vals["c_ctx"].reshape(D)
        rep.append(vals)

    d_ada, nm_ada, nv_ada = _adamw_plain(g_w_ada, w_ada2, m_w_ada[0], v_w_ada[0], "adamw_w_ada")
    ada = [t[None] for t in (g_w_ada, d_ada, nm_ada, nv_ada)]
    lb_w = hgrn_lb_logits.reshape(4, 64)
    d_lb, nm_lb, nv_lb = _adamw_plain(g_lb_mine.reshape(4, 64), lb_w, m_hgrn_lb_logits.reshape(4, 64),
                                      v_hgrn_lb_logits.reshape(4, 64), "adamw_lb")
    lbs = [t.reshape(2, 2, 64) for t in (g_lb_mine, d_lb, nm_lb, nv_lb)]

    names = ['c_ctx', 'w_ada', 'b_ada', 'norm_mix_w', 'norm_ffn_w', 'w_in', 'hgrn_lb_logits', 'hgrn_norm_w',
             'q_norm_w', 'k_norm_w', 'attn_sinks', 'w_branch_hgrn', 'w_branch_attn', 'w_out', 'w_ffn_gate',
             'w_ffn_up', 'w_ffn_down']
    outs = [loss, grad_x[None]]
    for kind in range(4):
        for nm in names:
            if nm == 'w_ada':
                outs.append(ada[kind])
            elif nm == 'hgrn_lb_logits':
                outs.append(lbs[kind])
            elif nm in big:
                outs.append(big[nm][kind])
            else:
                outs.append(rep[kind][nm])
    return tuple(outs)
```

```python
import functools
import math

import numpy as np
import jax
import jax.numpy as jnp
from jax import lax
from jax.experimental import pallas as pl
from jax.experimental.pallas import tpu as pltpu

F32 = jnp.float32
BF16 = jnp.bfloat16

N_DEV = 8
D = 1024
S = 2048
L = 256
T = L + S
TM = 256
N_TILES = T // TM
N_LAT_TILES = S // TM
HG_HEADS = 4
HG_DIM = 128
HGW = 512
CHUNK = 32
N_CHUNKS = T // CHUNK
N_CTX_CHUNKS = L // CHUNK
N_LAT_CHUNKS = S // CHUNK
ATT_HEADS = 8
KV_HEADS = 2
HEAD_DIM = 64
ATW = 512
KVW = 128
BLOCK = 128
N_BLOCKS = S // BLOCK
GRID_W = 64
ROPE_THETA = 10000.0
D_FF = 2816
FF_BLK = D_FF // N_DEV
FF_TILE = 256
N_FF_TILES = D_FF // FF_TILE
IN_COLS = 5376
IN_BLK = IN_COLS // N_DEV
ADA_BLK = 6 * D // N_DEV
EPS = 1e-6
WA, WB, WC = 2048, 1280, 2048

ADAM_LR = 0.001
ADAM_B1 = 0.9
ADAM_B2 = 0.999
ADAM_EPS = 1e-08
ADAM_WD = 0.01
ADAM_STEP = 10

VMEM_LIMIT = 56 * 1024 * 1024
MESH = pl.DeviceIdType.MESH


def _cp(sem=None, vmem=VMEM_LIMIT):
    return pltpu.CompilerParams(dimension_semantics=sem, vmem_limit_bytes=vmem)


def _full(shape):
    n = len(shape)
    return pl.BlockSpec(shape, lambda *_: (0,) * n)


_VMEM_WHOLE = pl.BlockSpec(memory_space=pltpu.VMEM)
_ANY = pl.BlockSpec(memory_space=pl.ANY)


def _sigmoid(v):
    return 1.0 / (1.0 + jnp.exp(-v))


def _dot(a, b):
    return jnp.dot(a, b, preferred_element_type=F32)


def _dot_nt(a, b):
    return lax.dot_general(a, b, (((1,), (1,)), ((), ())), preferred_element_type=F32)


def _dot_tn(a, b):
    return lax.dot_general(a, b, (((0,), (0,)), ((), ())), preferred_element_type=F32)


def _split3(v):
    hi = v.astype(BF16)
    r = v - hi.astype(F32)
    mid = r.astype(BF16)
    lo = (r - mid.astype(F32)).astype(BF16)
    return hi, mid, lo


def _dot_exact_rhs01(v, m01):
    hi, mid, lo = _split3(v)
    return _dot(hi, m01) + _dot(mid, m01) + _dot(lo, m01)


def _split2(v):
    hi = v.astype(BF16)
    return hi, (v - hi.astype(F32)).astype(BF16)


def _dot_lhs01(m01, v):
    hi, lo = _split2(v)
    return _dot(m01, hi) + _dot(m01, lo)


def _dot_f32(a, b, dot=_dot):
    ah, am, al = _split3(a)
    bh, bm, bl = _split3(b)
    return (dot(ah, bh) + (dot(ah, bm) + dot(am, bh))
            + (dot(am, bm) + dot(ah, bl) + dot(al, bh)))


def _my_pos():
    return lax.axis_index("x"), lax.axis_index("y"), lax.axis_index("c")


class _Comm:
    def __init__(self, operands, out_shapes, sems, phases):
        self.operands, self.out_shapes, self.sems, self.phases = operands, out_shapes, sems, phases


def _gather_comm(blocks):
    n = len(blocks)

    def parts(ins, outs, sems):
        send_sems, recv_sems, local_sems = sems
        x, y, c = _my_pos()
        me, sibling = (x, y, c), (x, y, 1 - c)
        chips = [(1 - x, y), (x, 1 - y), (1 - x, 1 - y)]

        def slot(a, px, py, pc):
            return outs[a].at[4 * px + 2 * py + pc]

        def copy(a, k, block, to, src=None):
            return pltpu.make_async_remote_copy(
                src_ref=slot(a, *block) if src is None else src, dst_ref=slot(a, *block),
                send_sem=send_sems.at[a, k], recv_sem=recv_sems.at[a, k],
                device_id=to, device_id_type=MESH)

        mine = [pltpu.make_async_copy(ins[a], slot(a, *me), local_sems.at[a]) for a in range(n)]
        first = []
        for a in range(n):
            first.append(copy(a, 0, me, sibling, src=ins[a]))
            first += [copy(a, 1 + j, me, (*chip, c), src=ins[a]) for j, chip in enumerate(chips)]
        passed = [copy(a, 4 + j, (*chip, c), sibling) for j, chip in enumerate(chips) for a in range(n)]
        return c, me, sibling, chips, copy, mine, first, passed

    def start(ins, outs, sems):
        _, _, _, _, _, mine, first, _ = parts(ins, outs, sems)
        for cp in mine + first:
            cp.start()

    def forward(ins, outs, sems):
        c, me, _, chips, copy, _, _, passed = parts(ins, outs, sems)
        for j, chip in enumerate(chips):
            for a in range(n):
                copy(a, 1 + j, (*chip, c), me).wait_recv()
                passed[j * n + a].start()

    def finish(ins, outs, sems):
        c, me, sibling, chips, copy, mine, first, passed = parts(ins, outs, sems)
        for a in range(n):
            copy(a, 0, sibling, me).wait_recv()
            for j, chip in enumerate(chips):
                copy(a, 4 + j, (*chip, 1 - c), me).wait_recv()
        for cp in first + passed:
            cp.wait_send()
        for cp in mine:
            cp.wait()

    return _Comm(blocks, [jax.ShapeDtypeStruct((N_DEV,) + b.shape, b.dtype) for b in blocks],
                 [pltpu.SemaphoreType.DMA((n, 7)), pltpu.SemaphoreType.DMA((n, 7)), pltpu.SemaphoreType.DMA((n,))],
                 [start, forward, finish])


def _run_comm(comm, name, in_vmem=False):
    n_in, n_out = len(comm.operands), len(comm.out_shapes)

    def body(*refs):
        ins, outs, sems = refs[:n_in], refs[n_in:n_in + n_out], refs[n_in + n_out:]
        for phase in comm.phases:
            phase(ins, outs, sems)

    spec = _VMEM_WHOLE if in_vmem else _ANY
    return pl.pallas_call(
        body, name=name, out_shape=comm.out_shapes, in_specs=[spec] * n_in, out_specs=[spec] * n_out,
        scratch_shapes=comm.sems,
    )(*comm.operands)


def _carrier_call(body, comm, schedule, *, name, grid, in_specs, out_specs, out_shape, scratch_shapes, operands):
    n_in, n_out, n_scr = len(in_specs), len(out_specs), len(scratch_shapes)
    c_in, c_out = len(comm.operands), len(comm.out_shapes)

    def full_body(*refs):
        ins, refs = refs[:n_in], refs[n_in:]
        cins, refs = refs[:c_in], refs[c_in:]
        outs, refs = refs[:n_out], refs[n_out:]
        couts, refs = refs[:c_out], refs[c_out:]
        scr, csems = refs[:n_scr], refs[n_scr:]
        step = pl.program_id(0)

        def run(before):
            for (at, when_before), phase in zip(schedule, comm.phases):
                if when_before == before:
                    pl.when(step == at)(functools.partial(phase, cins, couts, csems))

        run(True)
        body(*ins, *outs, *scr)
        run(False)

    res = pl.pallas_call(
        full_body, name=name, grid=grid,
        in_specs=list(in_specs) + [_ANY] * c_in, out_specs=list(out_specs) + [_ANY] * c_out,
        out_shape=list(out_shape) + list(comm.out_shapes),
        scratch_shapes=list(scratch_shapes) + list(comm.sems),
        compiler_params=_cp(("arbitrary",)),
    )(*operands, *comm.operands)
    return res[:n_out], res[n_out:]


def _pcall(body, carried, *, name, grid, in_specs, out_specs, out_shape, scratch_shapes, operands):
    if carried is None:
        res = pl.pallas_call(body, name=name, grid=grid, in_specs=in_specs, out_specs=out_specs,
                             out_shape=out_shape, scratch_shapes=scratch_shapes,
                             compiler_params=_cp(("arbitrary",)))(*operands)
        return res, ()
    return _carrier_call(body, carried[0], carried[1], name=name, grid=grid, in_specs=in_specs,
                         out_specs=out_specs, out_shape=out_shape, scratch_shapes=scratch_shapes,
                         operands=operands)


def _all_gather(blocks, name, in_vmem):
    return _run_comm(_gather_comm(blocks), name, in_vmem)


N_CHIPS = 4


def _sibling_comm(contribs):
    n = len(contribs)

    def copies(ins, outs, sems):
        send_sems, recv_sems = sems
        x, y, c = _my_pos()
        return [pltpu.make_async_remote_copy(
            src_ref=ins[a].at[pl.ds(0, N_CHIPS), 1 - c], dst_ref=outs[a],
            send_sem=send_sems.at[a], recv_sem=recv_sems.at[a],
            device_id=(x, y, 1 - c), device_id_type=MESH) for a in range(n)]

    def start(ins, outs, sems):
        for cp in copies(ins, outs, sems):
            cp.start()

    def finish(ins, outs, sems):
        cps = copies(ins, outs, sems)
        for cp in cps:
            cp.wait_recv()
        for cp in cps:
            cp.wait_send()

    return _Comm(contribs, [jax.ShapeDtypeStruct((N_CHIPS,) + b.shape[2:], b.dtype) for b in contribs],
                 [pltpu.SemaphoreType.DMA((n,)), pltpu.SemaphoreType.DMA((n,))], [start, finish])


def _pair_sum(mine, theirs, name):
    _, _, rows, cols = mine.shape
    core = lax.axis_index("c").astype(jnp.int32).reshape(1)

    def body(c_ref, m_ref, t_ref, o_ref):
        o_ref[...] = (m_ref[...].astype(F32) + t_ref[...].astype(F32)).astype(BF16)

    return pl.pallas_call(
        body, name=name,
        grid_spec=pltpu.PrefetchScalarGridSpec(
            num_scalar_prefetch=1, grid=(N_CHIPS,),
            in_specs=[pl.BlockSpec((None, None, rows, cols), lambda q, c: (q, c[0], 0, 0)),
                      pl.BlockSpec((None, rows, cols), lambda q, c: (q, 0, 0))],
            out_specs=pl.BlockSpec((None, rows, cols), lambda q, c: (q, 0, 0))),
        out_shape=jax.ShapeDtypeStruct((N_CHIPS, rows, cols), BF16),
        compiler_params=_cp(("parallel",)),
    )(core, mine, theirs)


def _chip_comm(sums):
    n = len(sums)

    def parts(ins, outs, sems):
        send_sems, recv_sems, local_sems = sems
        x, y, c = _my_pos()
        q_me = 2 * x + y
        chips = [(1 - x, y), (x, 1 - y), (1 - x, 1 - y)]
        mine = [pltpu.make_async_copy(ins[a].at[q_me], outs[a].at[q_me], local_sems.at[a]) for a in range(n)]
        sends, recvs = [], []
        for j, (px, py) in enumerate(chips):
            for a in range(n):
                q = 2 * px + py
                sends.append(pltpu.make_async_remote_copy(
                    src_ref=ins[a].at[q], dst_ref=outs[a].at[q_me],
                    send_sem=send_sems.at[a, j], recv_sem=recv_sems.at[a, j],
                    device_id=(px, py, c), device_id_type=MESH))
                recvs.append(pltpu.make_async_remote_copy(
                    src_ref=ins[a].at[q], dst_ref=outs[a].at[q],
                    send_sem=send_sems.at[a, j], recv_sem=recv_sems.at[a, j],
                    device_id=(x, y, c), device_id_type=MESH))
        return mine, sends, recvs

    def start(ins, outs, sems):
        mine, sends, _ = parts(ins, outs, sems)
        for cp in mine + sends:
            cp.start()

    def finish(ins, outs, sems):
        mine, sends, recvs = parts(ins, outs, sems)
        for cp in recvs:
            cp.wait_recv()
        for cp in sends:
            cp.wait_send()
        for cp in mine:
            cp.wait()

    return _Comm(sums, [jax.ShapeDtypeStruct(b.shape, b.dtype) for b in sums],
                 [pltpu.SemaphoreType.DMA((n, 3)), pltpu.SemaphoreType.DMA((n, 3)), pltpu.SemaphoreType.DMA((n,))],
                 [start, finish])


def _mm_nt(a, bt, *, tm, tn, out_dtype, name, row_off=0, rows=None):
    rows = a.shape[0] if rows is None else rows
    n, k = bt.shape

    def body(a_ref, b_ref, o_ref):
        o_ref[...] = _dot_nt(a_ref[...], b_ref[...]).astype(out_dtype)

    return pl.pallas_call(
        body, name=name, grid=(rows // tm, n // tn),
        in_specs=[pl.BlockSpec((tm, k), lambda i, j: (i + row_off, 0)),
                  pl.BlockSpec((tn, k), lambda i, j: (j, 0))],
        out_specs=pl.BlockSpec((tm, tn), lambda i, j: (i, j)),
        out_shape=jax.ShapeDtypeStruct((rows, n), out_dtype),
        compiler_params=_cp(("parallel", "parallel")),
    )(a, bt)


def _mm_tn(a, b, *, tk, nk, tm, tn, out_dtype, name, a_off=0, b_off=0):
    m, n = a.shape[1], b.shape[1]

    def body(a_ref, b_ref, o_ref, acc):
        kk = pl.program_id(2)

        @pl.when(kk == 0)
        def _():
            acc[...] = jnp.zeros_like(acc)

        acc[...] += _dot_tn(a_ref[...], b_ref[...])

        @pl.when(kk == nk - 1)
        def _():
            o_ref[...] = acc[...].astype(out_dtype)

    return pl.pallas_call(
        body, name=name, grid=(m // tm, n // tn, nk),
        in_specs=[pl.BlockSpec((tk, tm), lambda i, j, kk: (kk + a_off, i)),
                  pl.BlockSpec((tk, tn), lambda i, j, kk: (kk + b_off, j))],
        out_specs=pl.BlockSpec((tm, tn), lambda i, j, kk: (i, j)),
        out_shape=jax.ShapeDtypeStruct((m, n), out_dtype),
        scratch_shapes=[pltpu.VMEM((tm, tn), F32)],
        compiler_params=_cp(("parallel", "parallel", "arbitrary")),
    )(a, b)


def _mm_tn_blocked(a, b, name, carried=None):
    nb, _, w = a.shape
    n = b.shape[1]

    def body(a_ref, b_ref, o_ref):
        o_ref[...] = _dot_tn(a_ref[...], b_ref[...]).astype(BF16)

    (out,), extra = _pcall(
        body, carried, name=name, grid=(nb,),
        in_specs=[pl.BlockSpec((None, S, w), lambda j: (j, 0, 0)), _full((S, n))],
        out_specs=[pl.BlockSpec((None, w, n), lambda j: (j, 0, 0))],
        out_shape=[jax.ShapeDtypeStruct((nb, w, n), BF16)],
        scratch_shapes=[], operands=[a, b])
    return out, extra


def _ada_rows(cc, w_ada, b_cols):
    def body(c_ref, w_ref, b_ref, o_ref):
        cv = c_ref[...]
        o_ref[...] = _dot_f32(cv * _sigmoid(cv), w_ref[...]) + b_ref[...]

    return pl.pallas_call(
        body, name="ada_rows",
        in_specs=[_VMEM_WHOLE] * 3, out_specs=_VMEM_WHOLE,
        out_shape=jax.ShapeDtypeStruct((16, ADA_BLK), F32),
        compiler_params=_cp(),
    )(cc, w_ada, b_cols)


def _ada_grads(cc, dm_cols, w_ada):
    def body(c_ref, dm_ref, w_ref, gw_ref, dsc_ref):
        cv = c_ref[...]
        sc = cv * _sigmoid(cv)
        dm = dm_ref[...]
        gw_ref[...] = _dot_f32(sc, dm, dot=_dot_tn)
        dsc_ref[...] = _dot_f32(dm, w_ref[...], dot=_dot_nt)

    return pl.pallas_call(
        body, name="ada_grads",
        in_specs=[_VMEM_WHOLE] * 3, out_specs=[_VMEM_WHOLE] * 2,
        out_shape=[jax.ShapeDtypeStruct((D, ADA_BLK), F32), jax.ShapeDtypeStruct((16, D), F32)],
        compiler_params=_cp(),
    )(cc, dm_cols, w_ada)


def _lat(i):
    return jnp.maximum(i - 1, 0)


def _rms_mod(xv, nw, sh, sc):
    rstd = lax.rsqrt(jnp.mean(xv * xv, axis=-1, keepdims=True) + EPS)
    return (xv * rstd * nw) * (1.0 + sc) + sh


def _rms_mod_bwd(xv, nw, sc, dh):
    rstd = lax.rsqrt(jnp.mean(xv * xv, axis=-1, keepdims=True) + EPS)
    xhat = xv * rstd
    dn = dh * (1.0 + sc)
    dxhat = dn * nw
    dx = rstd * (dxhat - xhat * jnp.mean(dxhat * xhat, axis=-1, keepdims=True))
    return (dx, jnp.sum(dh, axis=0, keepdims=True), jnp.sum(dh * (xhat * nw), axis=0, keepdims=True),
            jnp.sum(dn * xhat, axis=0, keepdims=True))


def _norm_mod_all(ctx, x, nw, sh, sc):
    def body(ctx_ref, x_ref, nw_ref, sh_ref, sc_ref, o_ref):
        i = pl.program_id(0)
        sel = jnp.minimum(i, 1)
        xv = jnp.where(i == 0, ctx_ref[...], x_ref[...])
        o_ref[...] = _rms_mod(xv, nw_ref[...], sh_ref[pl.ds(sel, 1), :], sc_ref[pl.ds(sel, 1), :]).astype(BF16)

    return pl.pallas_call(
        body, name="norm_mod", grid=(N_TILES,),
        in_specs=[_full((TM, D)), pl.BlockSpec((TM, D), lambda i: (_lat(i), 0)),
                  _full((1, D)), _full((2, D)), _full((2, D))],
        out_specs=pl.BlockSpec((TM, D), lambda i: (i, 0)),
        out_shape=jax.ShapeDtypeStruct((T, D), BF16),
        compiler_params=_cp(("parallel",)),
    )(ctx, x, nw, sh, sc)


def _chunk_masks(reverse):
    row = lax.broadcasted_iota(jnp.int32, (TM, TM), 0)
    col = lax.broadcasted_iota(jnp.int32, (TM, TM), 1)
    same = (row // CHUNK) == (col // CHUNK)
    tri = same & ((col >= row) if reverse else (col <= row))
    return same, tri


def _chunk_order(i, reverse):
    if not reverse:
        return i
    return jnp.where(i < N_CTX_CHUNKS, N_CTX_CHUNKS - 1 - i, N_CHUNKS + N_CTX_CHUNKS - 1 - i)


def _decay_terms(z, lb, same01, tri01):
    f = lb + (1.0 - lb) * _sigmoid(z)
    g = jnp.log(f)
    g2 = jnp.concatenate(_split2(g), axis=1)
    b2 = _dot(tri01, g2)
    t2 = _dot(same01, g2)
    return f, 1.0 - f, b2[:, :HG_DIM] + b2[:, HG_DIM:], t2[:, :HG_DIM] + t2[:, HG_DIM:]


def _chunk_outer(a, b):
    n = TM // CHUNK
    return jnp.einsum('ncv,nck->nvk', a.reshape(n, CHUNK, HG_DIM), b.reshape(n, CHUNK, HG_DIM),
                      preferred_element_type=F32)


def _hgrn_fwd(p_a, lbl, carried=None):
    cpt = TM // CHUNK

    def body(p_ref, lbl_ref, o_ref, st_ref, qd_s, kd_s, u_s, v_s, ebt_s):
        masks = [_chunk_masks(d == 1) for d in (0, 1)]
        same01 = jnp.where(masks[0][0], 1.0, 0.0).astype(BF16)
        tri = [m[1] for m in masks]
        tri01 = [jnp.where(t, 1.0, 0.0).astype(BF16) for t in tri]
        lb = [_sigmoid(lbl_ref[d][0:1, :] - lbl_ref[d][1:2, :]) for d in (0, 1)]

        def prep(r, carry):
            r0 = pl.multiple_of(r * TM, TM)
            vb = p_ref[pl.ds(r0, TM), 2 * HG_DIM:3 * HG_DIM].astype(BF16)
            v_s[pl.ds(r0, TM), :] = vb
            for d in (0, 1):
                z = p_ref[pl.ds(r0, TM), d * HG_DIM:(d + 1) * HG_DIM]
                _, k, b, bt = _decay_terms(z, lb[d], same01, tri01[d])
                u_s[d, pl.ds(r * cpt, cpt)] = _chunk_outer(vb, (k * jnp.exp(bt - b)).astype(BF16))
                ebt_s[d, pl.ds(r0, TM), :] = jnp.exp(bt)

                @pl.when(r >= 1)
                def _():
                    rl = pl.multiple_of(r0 - L, TM)
                    qr = p_ref[pl.ds(r0, TM), 3 * HG_DIM:4 * HG_DIM]
                    q = qr * _sigmoid(qr) * HG_DIM ** -0.5
                    qd_s[d, pl.ds(rl, TM), :] = (q * jnp.exp(b)).astype(BF16)
                    kd_s[d, pl.ds(rl, TM), :] = (k * jnp.exp(-b)).astype(BF16)

            return carry

        lax.fori_loop(0, N_TILES, prep, 0)

        def scan(i, sts):
            new = []
            for d in (0, 1):
                nn = _chunk_order(i, d == 1)
                c0 = pl.multiple_of(nn * CHUNK, CHUNK)
                st_ref[d, nn] = sts[d].astype(BF16)
                new.append(sts[d] * ebt_s[d, pl.ds(c0, 1), :] + u_s[d, nn])
            return tuple(new)

        zero = jnp.zeros((HG_DIM, HG_DIM), F32)
        lax.fori_loop(0, N_CHUNKS, scan, (zero, zero))

        def outp(r, carry):
            r0 = pl.multiple_of(r * TM, TM)
            vb = v_s[pl.ds(r0 + L, TM), :]
            o = jnp.zeros((TM, HG_DIM), F32)
            for d in (0, 1):
                qd = qd_s[d, pl.ds(r0, TM), :]
                a = jnp.where(tri[d], _dot_nt(qd, kd_s[d, pl.ds(r0, TM), :]), 0.0)
                stb = st_ref[d, pl.ds(N_CTX_CHUNKS + r * cpt, cpt)]
                inter = jnp.einsum('nck,nvk->ncv', qd.reshape(cpt, CHUNK, HG_DIM), stb,
                                   preferred_element_type=F32)
                o = o + _dot(a.astype(BF16), vb) + inter.reshape(TM, HG_DIM)
            o_ref[pl.ds(r0, TM), :] = o
            return carry

        lax.fori_loop(0, N_LAT_TILES, outp, 0)

    return _pcall(
        body, carried, name="hgrn_fwd", grid=(HG_HEADS,),
        in_specs=[pl.BlockSpec((T, 4 * HG_DIM), lambda h: (0, h)),
                  pl.BlockSpec((2, 2, HG_DIM), lambda h: (0, 0, h))],
        out_specs=[pl.BlockSpec((S, HG_DIM), lambda h: (0, h)),
                   pl.BlockSpec((2, None, N_CHUNKS, HG_DIM, HG_DIM), lambda h: (0, h, 0, 0, 0))],
        out_shape=[jax.ShapeDtypeStruct((S, HGW), F32),
                   jax.ShapeDtypeStruct((2, HG_HEADS, N_CHUNKS, HG_DIM, HG_DIM), BF16)],
        scratch_shapes=[pltpu.VMEM((2, S, HG_DIM), BF16), pltpu.VMEM((2, S, HG_DIM), BF16),
                        pltpu.VMEM((2, N_CHUNKS, HG_DIM, HG_DIM), F32), pltpu.VMEM((T, HG_DIM), BF16),
                        pltpu.VMEM((2, T, HG_DIM), F32)],
        operands=[p_a, lbl])


def _hgrn_bwd(p_a, lbl, d_o, st, carried=None):
    cpt = TM // CHUNK

    def rows(r):
        return r * TM if isinstance(r, int) else pl.multiple_of(r * TM, TM)

    def body(p_ref, lbl_ref, do_ref, st_ref, dp_ref, dlb_ref, b_s, bt_s, dbt_s, qd_s, dst_s, w_s):
        masks = [_chunk_masks(d == 1) for d in (0, 1)]
        same01 = jnp.where(masks[0][0], 1.0, 0.0).astype(BF16)
        tri = [m[1] for m in masks]
        tri01 = [jnp.where(t, 1.0, 0.0).astype(BF16) for t in tri]
        later01 = [tri01[1], tri01[0]]
        lb = [_sigmoid(lbl_ref[d][0:1, :] - lbl_ref[d][1:2, :]) for d in (0, 1)]

        def prep_tile(r, latent):
            r0 = rows(r)
            for d in (0, 1):
                z = p_ref[pl.ds(r0, TM), d * HG_DIM:(d + 1) * HG_DIM]
                _, _, b, bt = _decay_terms(z, lb[d], same01, tri01[d])
                b_s[d, pl.ds(r0, TM), :] = b
                bt_s[d, pl.ds(r0, TM), :] = bt
                if latent:
                    rl = pl.multiple_of(r0 - L, TM)
                    qr = p_ref[pl.ds(r0, TM), 3 * HG_DIM:4 * HG_DIM]
                    qd = (qr * _sigmoid(qr) * HG_DIM ** -0.5 * jnp.exp(b)).astype(BF16)
                    qd_s[d, pl.ds(rl, TM), :] = qd
                    w_s[d, pl.ds(r * cpt, cpt)] = _chunk_outer(
                        do_ref[pl.ds(rl, TM), :].astype(BF16), qd).astype(BF16)

        prep_tile(0, False)
        w_s[:, pl.ds(0, N_CTX_CHUNKS)] = jnp.zeros((2, N_CTX_CHUNKS, HG_DIM, HG_DIM), BF16)

        def prep(r, carry):
            prep_tile(r, True)
            return carry

        lax.fori_loop(1, N_TILES, prep, 0)

        def rscan(j, dsts):
            i = N_CHUNKS - 1 - j
            new = []
            for d in (0, 1):
                nn = _chunk_order(i, d == 1)
                c0 = pl.multiple_of(nn * CHUNK, CHUNK)
                dst_s[d, nn] = dsts[d].astype(BF16)
                after = st_ref[d, _chunk_order(jnp.minimum(i + 1, N_CHUNKS - 1), d == 1)].astype(F32)
                dbt_s[d, pl.ds(c0, CHUNK), :] = jnp.broadcast_to(
                    jnp.sum(after * dsts[d], axis=0, keepdims=True), (CHUNK, HG_DIM))
                new.append(dsts[d] * jnp.exp(bt_s[d, pl.ds(c0, 1), :]) + w_s[d, nn].astype(F32))
            return tuple(new)

        zero = jnp.zeros((HG_DIM, HG_DIM), F32)
        lax.fori_loop(0, N_CHUNKS, rscan, (zero, zero))

        def grad_tile(r, latent):
            r0 = rows(r)
            vb = p_ref[pl.ds(r0, TM), 2 * HG_DIM:3 * HG_DIM].astype(BF16)
            dv = jnp.zeros((TM, HG_DIM), F32)
            dq = jnp.zeros((TM, HG_DIM), F32)
            dlbs = []
            if latent:
                rl = pl.multiple_of(r0 - L, TM)
                qr = p_ref[pl.ds(r0, TM), 3 * HG_DIM:4 * HG_DIM]
                sq = _sigmoid(qr)
                do = do_ref[pl.ds(rl, TM), :].astype(BF16)
                da_full = _dot_nt(do, vb)
            for d in (0, 1):
                z = p_ref[pl.ds(r0, TM), d * HG_DIM:(d + 1) * HG_DIM]
                sz = _sigmoid(z)
                f = lb[d] + (1.0 - lb[d]) * sz
                k = 1.0 - f
                b = b_s[d, pl.ds(r0, TM), :]
                e2 = jnp.exp(bt_s[d, pl.ds(r0, TM), :] - b)
                dstb = dst_s[d, pl.ds(r * cpt, cpt)]
                kd2 = k * e2
                dkd2 = jnp.einsum('ncv,nvk->nck', vb.reshape(cpt, CHUNK, HG_DIM), dstb,
                                  preferred_element_type=F32).reshape(TM, HG_DIM)
                dv = dv + jnp.einsum('nck,nvk->ncv', kd2.astype(BF16).reshape(cpt, CHUNK, HG_DIM), dstb,
                                     preferred_element_type=F32).reshape(TM, HG_DIM)
                dk = dkd2 * e2
                db = -(kd2 * dkd2)
                if latent:
                    eb = jnp.exp(b)
                    enb = jnp.exp(-b)
                    qdf = qr * sq * HG_DIM ** -0.5 * eb
                    kdf = k * enb
                    qd = qd_s[d, pl.ds(rl, TM), :]
                    kd = kdf.astype(BF16)
                    a = jnp.where(tri[d], _dot_nt(qd, kd), 0.0).astype(BF16)
                    da = jnp.where(tri[d], da_full, 0.0).astype(BF16)
                    stb = st_ref[d, pl.ds(r * cpt, cpt)]
                    dqd = _dot(da, kd) + jnp.einsum(
                        'ncv,nvk->nck', do.reshape(cpt, CHUNK, HG_DIM), stb,
                        preferred_element_type=F32).reshape(TM, HG_DIM)
                    dkd = _dot_tn(da, qd)
                    dv = dv + _dot_tn(a, do)
                    dk = dk + dkd * enb
                    db = db + qdf * dqd - kdf * dkd
                    dq = dq + dqd * eb
                dg = _dot_lhs01(later01[d], db) + dbt_s[d, pl.ds(r0, TM), :]
                df = dg / f - dk
                dp_ref[pl.ds(r0, TM), d * HG_DIM:(d + 1) * HG_DIM] = (
                    df * (1.0 - lb[d]) * sz * (1.0 - sz)).astype(BF16)
                dlbs.append(jnp.sum(df * (1.0 - sz), axis=0, keepdims=True))
            dp_ref[pl.ds(r0, TM), 2 * HG_DIM:3 * HG_DIM] = dv.astype(BF16)
            if latent:
                dq = dq * (HG_DIM ** -0.5) * (sq * (1.0 + qr * (1.0 - sq)))
            dp_ref[pl.ds(r0, TM), 3 * HG_DIM:4 * HG_DIM] = dq.astype(BF16)
            return dlbs

        dlb_ctx = grad_tile(0, False)

        def grads(r, acc):
            t = grad_tile(r, True)
            return (acc[0] + t[0], acc[1] + t[1])

        dlb = lax.fori_loop(1, N_TILES, grads, (dlb_ctx[0], dlb_ctx[1]))
        dlb_ref[0:1, :] = dlb[0]
        dlb_ref[1:2, :] = dlb[1]

    return _pcall(
        body, carried, name="hgrn_bwd", grid=(HG_HEADS,),
        in_specs=[pl.BlockSpec((T, 4 * HG_DIM), lambda h: (0, h)),
                  pl.BlockSpec((2, 2, HG_DIM), lambda h: (0, 0, h)),
                  pl.BlockSpec((S, HG_DIM), lambda h: (0, h)),
                  pl.BlockSpec((2, None, N_CHUNKS, HG_DIM, HG_DIM), lambda h: (0, h, 0, 0, 0))],
        out_specs=[pl.BlockSpec((T, 4 * HG_DIM), lambda h: (0, h)),
                   pl.BlockSpec((2, HG_DIM), lambda h: (0, h))],
        out_shape=[jax.ShapeDtypeStruct((T, WA), BF16), jax.ShapeDtypeStruct((2, HGW), F32)],
        scratch_shapes=[pltpu.VMEM((2, T, HG_DIM), F32), pltpu.VMEM((2, T, HG_DIM), F32),
                        pltpu.VMEM((2, T, HG_DIM), F32), pltpu.VMEM((2, S, HG_DIM), BF16),
                        pltpu.VMEM((2, N_CHUNKS, HG_DIM, HG_DIM), BF16),
                        pltpu.VMEM((2, N_CHUNKS, HG_DIM, HG_DIM), BF16)],
        operands=[p_a, lbl, d_o, st])


def _rope_tables():
    t = np.arange(S)
    inv = ROPE_THETA ** (-np.arange(0, 32, 2, dtype=np.float64) / 32)
    lane = np.arange(64)
    pos = np.where(lane[None, :] < 32, (t // GRID_W)[:, None], (t % GRID_W)[:, None]).astype(np.float64)
    ang = pos * inv[(lane % 32) % 16][None, :]
    sign = np.where((lane % 32) < 16, -1.0, 1.0)[None, :]
    cos = np.tile(np.cos(ang), (1, 2)).astype(np.float32)
    sin = np.tile(np.sin(ang) * sign, (1, 2)).astype(np.float32)
    return jnp.asarray(cos), jnp.asarray(sin)


def _rope_partner(v):
    lane = lax.broadcasted_iota(jnp.int32, (1, 128), 1)
    first = (lane % 32) < 16
    slabs = []
    for j in range(v.shape[1] // 128):
        s = v[:, 128 * j:128 * (j + 1)]
        slabs.append(jnp.where(first, pltpu.roll(s, 112, 1), pltpu.roll(s, 16, 1)))
    return slabs[0] if len(slabs) == 1 else jnp.concatenate(slabs, axis=1)


def _group_ones(width, group):
    r = lax.broadcasted_iota(jnp.int32, (width, width), 0)
    c = lax.broadcasted_iota(jnp.int32, (width, width), 1)
    return jnp.where((r // group) == (c // group), 1.0, 0.0).astype(BF16)


def _group_mean(v, ones01, group):
    hi = v.astype(BF16)
    lo = (v - hi.astype(F32)).astype(BF16)
    return (_dot(hi, ones01) + _dot(lo, ones01)) * (1.0 / group)


def _rep_matrix():
    r = lax.broadcasted_iota(jnp.int32, (KVW, ATW), 0)
    c = lax.broadcasted_iota(jnp.int32, (KVW, ATW), 1)
    return jnp.where(r == HEAD_DIM * (c // 256) + c % HEAD_DIM, 1.0, 0.0).astype(BF16)


def _tile_lanes(v, reps):
    return jnp.concatenate([v] * reps, axis=1)


def _prep_fwd(p_b, o, cos, sin, hnw, qnw, knw):
    def body(p_ref, o_ref, cos_ref, sin_ref, hnw_ref, qnw_ref, knw_ref, y_ref, q_ref, k_ref, v_ref):
        i = pl.program_id(0)
        rep = _rep_matrix()
        ones_k = _group_ones(KVW, HEAD_DIM)
        kr = p_ref[:, 1024:1152]
        krstd = lax.rsqrt(_group_mean(kr * kr, ones_k, HEAD_DIM) + EPS)
        kn = kr * krstd * knw_ref[...]
        v_ref[...] = _dot(p_ref[:, 1152:1280].astype(BF16), rep).astype(BF16)

        @pl.when(i == 0)
        def _():
            k_ref[...] = _dot(kn.astype(BF16), rep).astype(BF16)

        @pl.when(i > 0)
        def _():
            cs, sn = cos_ref[...], sin_ref[...]
            kro = kn * cs + _rope_partner(kn) * sn
            k_ref[...] = _dot(kro.astype(BF16), rep).astype(BF16)
            qr = p_ref[:, 512:1024]
            qrstd = lax.rsqrt(_group_mean(qr * qr, _group_ones(ATW, HEAD_DIM), HEAD_DIM) + EPS)
            qn = qr * qrstd * qnw_ref[...]
            qro = qn * _tile_lanes(cs, 4) + _rope_partner(qn) * _tile_lanes(sn, 4)
            q_ref[...] = (qro * HEAD_DIM ** -0.5).astype(BF16)
            ys = []
            for h in range(HG_HEADS):
                oh = o_ref[:, HG_DIM * h:HG_DIM * (h + 1)]
                gh = p_ref[:, HG_DIM * h:HG_DIM * (h + 1)]
                rstd = lax.rsqrt(jnp.mean(oh * oh, axis=-1, keepdims=True) + EPS)
                ys.append(oh * rstd * hnw_ref[...] * (gh * _sigmoid(gh)))
            y_ref[...] = jnp.concatenate(ys, axis=1).astype(BF16)

    return pl.pallas_call(
        body, name="prep_fwd", grid=(N_TILES,),
        in_specs=[pl.BlockSpec((TM, WB), lambda i: (i, 0)),
                  pl.BlockSpec((TM, HGW), lambda i: (_lat(i), 0)),
                  pl.BlockSpec((TM, 128), lambda i: (_lat(i), 0)),
                  pl.BlockSpec((TM, 128), lambda i: (_lat(i), 0)),
                  _full((1, HG_DIM)), _full((1, ATW)), _full((1, KVW))],
        out_specs=[pl.BlockSpec((TM, HGW), lambda i: (_lat(i), 0)),
                   pl.BlockSpec((TM, ATW), lambda i: (_lat(i), 0)),
                   pl.BlockSpec((TM, ATW), lambda i: (i, 0)),
                   pl.BlockSpec((TM, ATW), lambda i: (i, 0))],
        out_shape=[jax.ShapeDtypeStruct((S, HGW), BF16), jax.ShapeDtypeStruct((S, ATW), BF16),
                   jax.ShapeDtypeStruct((T, ATW), BF16), jax.ShapeDtypeStruct((T, ATW), BF16)],
        compiler_params=_cp(("arbitrary",)),
    )(p_b, o, cos, sin, hnw, qnw, knw)


def _prep_bwd(p_b, o, cos, sin, hnw, qnw, knw, dy_hg, dq, dk_rep, dv_rep, carried=None):
    def body(p_ref, o_ref, cos_ref, sin_ref, hnw_ref, qnw_ref, knw_ref, dy_ref, dq_ref, dk_ref, dv_ref,
             dp_ref, do_ref, acc_ref):
        i = pl.program_id(0)

        @pl.when(i == 0)
        def _():
            acc_ref[...] = jnp.zeros_like(acc_ref)

        rep = _rep_matrix()
        ones_k = _group_ones(KVW, HEAD_DIM)

        def fold(v):
            hi = v.astype(BF16)
            lo = (v - hi.astype(F32)).astype(BF16)
            return _dot_nt(hi, rep) + _dot_nt(lo, rep)

        kr = p_ref[:, 1024:1152]
        krstd = lax.rsqrt(_group_mean(kr * kr, ones_k, HEAD_DIM) + EPS)
        khat = kr * krstd
        kw = knw_ref[...]
        dkro = fold(dk_ref[...])
        dv = fold(dv_ref[...])

        def k_back(dkn):
            dkhat = dkn * kw
            dkr = krstd * (dkhat - khat * _group_mean(dkhat * khat, ones_k, HEAD_DIM))
            acc_ref[2:3, 0:KVW] += jnp.sum(dkn * khat, axis=0, keepdims=True)
            dp_ref[:, 1024:1152] = dkr.astype(BF16)
            dp_ref[:, 1152:1280] = dv.astype(BF16)

        @pl.when(i == 0)
        def _():
            k_back(dkro)
            dp_ref[:, 0:1024] = jnp.zeros((TM, 1024), BF16)

        @pl.when(i > 0)
        def _():
            cs, sn = cos_ref[...], sin_ref[...]
            k_back(dkro * cs + _rope_partner(dkro * sn))
            ones_q = _group_ones(ATW, HEAD_DIM)
            qr = p_ref[:, 512:1024]
            qrstd = lax.rsqrt(_group_mean(qr * qr, ones_q, HEAD_DIM) + EPS)
            qhat = qr * qrstd
            dqro = dq_ref[...] * HEAD_DIM ** -0.5
            dqn = dqro * _tile_lanes(cs, 4) + _rope_partner(dqro * _tile_lanes(sn, 4))
            dqhat = dqn * qnw_ref[...]
            dqr = qrstd * (dqhat - qhat * _group_mean(dqhat * qhat, ones_q, HEAD_DIM))
            acc_ref[1:2, :] += jnp.sum(dqn * qhat, axis=0, keepdims=True)
            dp_ref[:, 512:1024] = dqr.astype(BF16)
            dws = jnp.zeros((1, HG_DIM), F32)
            for h in range(HG_HEADS):
                sl = slice(HG_DIM * h, HG_DIM * (h + 1))
                oh, gh, dy = o_ref[:, sl], p_ref[:, sl], dy_ref[:, sl]
                rstd = lax.rsqrt(jnp.mean(oh * oh, axis=-1, keepdims=True) + EPS)
                ohat = oh * rstd
                sg = _sigmoid(gh)
                dp_ref[:, sl] = (dy * (ohat * hnw_ref[...]) * (sg * (1.0 + gh * (1.0 - sg)))).astype(BF16)
                dn = dy * (gh * sg)
                dws = dws + jnp.sum(dn * ohat, axis=0, keepdims=True)
                dohat = dn * hnw_ref[...]
                do_ref[:, sl] = rstd * (dohat - ohat * jnp.mean(dohat * ohat, axis=-1, keepdims=True))
            acc_ref[0:1, 0:HG_DIM] += dws

    return _pcall(
        body, carried, name="prep_bwd", grid=(N_TILES,),
        in_specs=[pl.BlockSpec((TM, WB), lambda i: (i, 0)),
                  pl.BlockSpec((TM, HGW), lambda i: (_lat(i), 0)),
                  pl.BlockSpec((TM, 128), lambda i: (_lat(i), 0)),
                  pl.BlockSpec((TM, 128), lambda i: (_lat(i), 0)),
                  _full((1, HG_DIM)), _full((1, ATW)), _full((1, KVW)),
                  pl.BlockSpec((TM, HGW), lambda i: (_lat(i), 0)),
                  pl.BlockSpec((TM, ATW), lambda i: (_lat(i), 0)),
                  pl.BlockSpec((TM, ATW), lambda i: (i, 0)),
                  pl.BlockSpec((TM, ATW), lambda i: (i, 0))],
        out_specs=[pl.BlockSpec((TM, WB), lambda i: (i, 0)),
                   pl.BlockSpec((TM, HGW), lambda i: (_lat(i), 0)),
                   _full((8, ATW))],
        out_shape=[jax.ShapeDtypeStruct((T, WB), BF16), jax.ShapeDtypeStruct((S, HGW), F32),
                   jax.ShapeDtypeStruct((8, ATW), F32)],
        scratch_shapes=[], operands=[p_b, o, cos, sin, hnw, qnw, knw, dy_hg, dq, dk_rep, dv_rep])


NEG = -1e30
_CTX_BLOCKS = L // BLOCK


def _attn_window_specs():
    prev = pl.BlockSpec((BLOCK, ATW), lambda i: (jnp.maximum(i - 1, 0) + _CTX_BLOCKS, 0))
    own = pl.BlockSpec((BLOCK, ATW), lambda i: (i + _CTX_BLOCKS, 0))
    nxt = pl.BlockSpec((BLOCK, ATW), lambda i: (jnp.minimum(i + 1, N_BLOCKS - 1) + _CTX_BLOCKS, 0))
    return [prev, own, nxt, _full((L, ATW))]


def _attn_valid(i, heads=4):
    qi = lax.broadcasted_iota(jnp.int32, (heads * BLOCK, 3 * BLOCK), 0) % BLOCK
    kj = lax.broadcasted_iota(jnp.int32, (heads * BLOCK, 3 * BLOCK), 1)
    return ((jnp.abs(kj - BLOCK - qi) <= BLOCK) & ((kj >= BLOCK) | (i > 0))
            & ((kj < 2 * BLOCK) | (i < N_BLOCKS - 1)))


def _stack_heads(qg):
    lane = lax.broadcasted_iota(jnp.int32, (1, 256), 1) // HEAD_DIM
    return jnp.concatenate([jnp.where(lane == g, qg, jnp.zeros_like(qg)) for g in range(4)], axis=0)


def _unstack_heads(v4):
    lane = lax.broadcasted_iota(jnp.int32, (1, 256), 1) // HEAD_DIM
    out = jnp.where(lane == 0, v4[0:BLOCK], 0.0)
    for g in range(1, 4):
        out = out + jnp.where(lane == g, v4[g * BLOCK:(g + 1) * BLOCK], 0.0)
    return out


def _sink_rows(sink_ref, hk):
    return jnp.concatenate(
        [jnp.broadcast_to(sink_ref[0:1, 4 * hk + g:4 * hk + g + 1], (BLOCK, 1)) for g in range(4)], axis=0)


def _attn_fwd(q, k_rep, v_rep, sinks, carried=None):
    def body(q_ref, kp, ko, kn, kc, vp, vo, vn, vc, sink_ref, y_ref, lse_ref):
        i = pl.program_id(0)
        valid = _attn_valid(i, 1)
        lane8 = lax.broadcasted_iota(jnp.int32, (1, ATT_HEADS), 1)
        head_of_lane = lax.broadcasted_iota(jnp.int32, (1, 256), 1) // HEAD_DIM
        lse_out = jnp.zeros((BLOCK, ATT_HEADS), F32)
        for hk in range(KV_HEADS):
            sl = slice(256 * hk, 256 * (hk + 1))
            qg = q_ref[:, sl]
            kl = jnp.concatenate([kp[:, sl], ko[:, sl], kn[:, sl]], axis=0)
            vl = jnp.concatenate([vp[:, sl], vo[:, sl], vn[:, sl]], axis=0)
            yg = jnp.zeros((BLOCK, 256), F32)
            for g in range(4):
                q1 = jnp.where(head_of_lane == g, qg, jnp.zeros_like(qg))
                s_loc = jnp.where(valid, _dot_nt(q1, kl), NEG)
                s_ctx = _dot_nt(q1, kc[:, sl])
                sink = sink_ref[0:1, 4 * hk + g:4 * hk + g + 1]
                m = jnp.maximum(jnp.maximum(jnp.max(s_loc, axis=1, keepdims=True),
                                            jnp.max(s_ctx, axis=1, keepdims=True)), sink)
                p_loc = jnp.exp(s_loc - m)
                p_ctx = jnp.exp(s_ctx - m)
                den = (jnp.sum(p_loc, axis=1, keepdims=True) + jnp.sum(p_ctx, axis=1, keepdims=True)
                       + jnp.exp(sink - m))
                o1 = (_dot(p_loc.astype(BF16), vl) + _dot(p_ctx.astype(BF16), vc[:, sl])) * (1.0 / den)
                yg = yg + jnp.where(head_of_lane == g, o1, 0.0)
                lse_out = lse_out + jnp.where(lane8 == 4 * hk + g, m + jnp.log(den), 0.0)
            y_ref[:, sl] = yg.astype(BF16)
        lse_ref[...] = lse_out

    return _pcall(
        body, carried, name="attn_fwd", grid=(N_BLOCKS,),
        in_specs=[pl.BlockSpec((BLOCK, ATW), lambda i: (i, 0))] + _attn_window_specs()
        + _attn_window_specs() + [_full((1, ATT_HEADS))],
        out_specs=[pl.BlockSpec((BLOCK, ATW), lambda i: (i, 0)),
                   pl.BlockSpec((BLOCK, ATT_HEADS), lambda i: (i, 0))],
        out_shape=[jax.ShapeDtypeStruct((S, ATW), BF16), jax.ShapeDtypeStruct((S, ATT_HEADS), F32)],
        scratch_shapes=[],
        operands=[q, k_rep, k_rep, k_rep, k_rep, v_rep, v_rep, v_rep, v_rep, sinks])


def _attn_bwd(q, k_rep, v_rep, sinks, y_at, lse, dy, carried=None):
    def body(q_ref, kp, ko, kn, kc, vp, vo, vn, vc, sink_ref, y_ref, lse_ref, dy_ref,
             dq_ref, dk_ref, dv_ref, dsink_ref, dk_acc, dv_acc):
        i = pl.program_id(0)

        @pl.when(i == 0)
        def _():
            dk_acc[...] = jnp.zeros_like(dk_acc)
            dv_acc[...] = jnp.zeros_like(dv_acc)
            dk_ref[pl.ds(0, L), :] = jnp.zeros((L, ATW), F32)
            dv_ref[pl.ds(0, L), :] = jnp.zeros((L, ATW), F32)
            dsink_ref[...] = jnp.zeros_like(dsink_ref)

        valid = _attn_valid(i)
        lane8 = lax.broadcasted_iota(jnp.int32, (1, ATT_HEADS), 1)
        w0 = pl.multiple_of(i * BLOCK, BLOCK)
        dsink = jnp.zeros((1, ATT_HEADS), F32)
        for hk in range(KV_HEADS):
            sl = slice(256 * hk, 256 * (hk + 1))
            q4 = _stack_heads(q_ref[:, sl])
            do4f = _stack_heads(dy_ref[:, sl])
            o4 = _stack_heads(y_ref[:, sl]).astype(F32)
            do4 = do4f.astype(BF16)
            kl = jnp.concatenate([kp[:, sl], ko[:, sl], kn[:, sl]], axis=0)
            vl = jnp.concatenate([vp[:, sl], vo[:, sl], vn[:, sl]], axis=0)
            lse4 = jnp.concatenate(
                [jnp.sum(jnp.where(lane8 == 4 * hk + g, lse_ref[...], 0.0), axis=1, keepdims=True)
                 for g in range(4)], axis=0)
            p_loc = jnp.where(valid, jnp.exp(_dot_nt(q4, kl) - lse4), 0.0)
            p_ctx = jnp.exp(_dot_nt(q4, kc[:, sl]) - lse4)
            delta = jnp.sum(do4f * o4, axis=1, keepdims=True)
            ds_loc = (p_loc * (_dot_nt(do4, vl) - delta)).astype(BF16)
            ds_ctx = (p_ctx * (_dot_nt(do4, vc[:, sl]) - delta)).astype(BF16)
            dq_ref[:, sl] = _unstack_heads(_dot(ds_loc, kl) + _dot(ds_ctx, kc[:, sl]))
            dk_acc[pl.ds(w0, 3 * BLOCK), sl] += _dot_tn(ds_loc, q4)
            dv_acc[pl.ds(w0, 3 * BLOCK), sl] += _dot_tn(p_loc.astype(BF16), do4)
            dk_ref[pl.ds(0, L), sl] += _dot_tn(ds_ctx, q4)
            dv_ref[pl.ds(0, L), sl] += _dot_tn(p_ctx.astype(BF16), do4)
            p_sink = jnp.exp(_sink_rows(sink_ref, hk) - lse4)
            for g in range(4):
                rows = slice(g * BLOCK, (g + 1) * BLOCK)
                dsink = dsink + jnp.where(lane8 == 4 * hk + g,
                                          -jnp.sum(p_sink[rows] * delta[rows], axis=0, keepdims=True), 0.0)
        dsink_ref[...] += dsink

        @pl.when(i == N_BLOCKS - 1)
        def _():
            dk_ref[pl.ds(L, S), :] = dk_acc[pl.ds(BLOCK, S), :]
            dv_ref[pl.ds(L, S), :] = dv_acc[pl.ds(BLOCK, S), :]

    row_q = pl.BlockSpec((BLOCK, ATW), lambda i: (i, 0))
    return _pcall(
        body, carried, name="attn_bwd", grid=(N_BLOCKS,),
        in_specs=[row_q] + _attn_window_specs() + _attn_window_specs()
        + [_full((1, ATT_HEADS)), row_q, pl.BlockSpec((BLOCK, ATT_HEADS), lambda i: (i, 0)), row_q],
        out_specs=[row_q, _full((T, ATW)), _full((T, ATW)), _full((1, ATT_HEADS))],
        out_shape=[jax.ShapeDtypeStruct((S, ATW), F32), jax.ShapeDtypeStruct((T, ATW), F32),
                   jax.ShapeDtypeStruct((T, ATW), F32), jax.ShapeDtypeStruct((1, ATT_HEADS), F32)],
        scratch_shapes=[pltpu.VMEM((S + 2 * BLOCK, ATW), F32), pltpu.VMEM((S + 2 * BLOCK, ATW), F32)],
        operands=[q, k_rep, k_rep, k_rep, k_rep, v_rep, v_rep, v_rep, v_rep, sinks, y_at, lse, dy])


def _merge_fwd(y_hg, y_at, p_c, x, w_bh, w_ba, w_out, g1, nfw, sh2, sc2):
    def body(yh_ref, ya_ref, g_ref, x_ref, wbh_ref, wba_ref, wo_ref, g1_ref, nfw_ref, sh_ref, sc_ref,
             a_ref, b_ref, mx_ref, r_ref, x1_ref, h2_ref):
        a = _dot_nt(yh_ref[...], wbh_ref[...])
        b = _dot_nt(ya_ref[...], wba_ref[...])
        mixed = (_sigmoid(g_ref[:, :D]) * a + _sigmoid(g_ref[:, D:]) * b).astype(BF16)
        r = _dot(mixed, wo_ref[...])
        x1 = x_ref[...] + g1_ref[...] * r
        a_ref[...] = a
        b_ref[...] = b
        mx_ref[...] = mixed
        r_ref[...] = r
        x1_ref[...] = x1
        h2_ref[...] = _rms_mod(x1, nfw_ref[...], sh_ref[...], sc_ref[...]).astype(BF16)

    row = lambda w: pl.BlockSpec((TM, w), lambda i: (i, 0))
    vec = _full((1, D))
    return pl.pallas_call(
        body, name="merge_fwd", grid=(N_LAT_TILES,),
        in_specs=[row(HGW), row(ATW), row(WC), row(D), _VMEM_WHOLE, _VMEM_WHOLE, _VMEM_WHOLE,
                  vec, vec, vec, vec],
        out_specs=[row(D)] * 6,
        out_shape=[jax.ShapeDtypeStruct((S, D), dt) for dt in (F32, F32, BF16, F32, F32, BF16)],
        compiler_params=_cp(("parallel",)),
    )(y_hg, y_at, p_c, x, w_bh, w_ba, w_out, g1, nfw, sh2, sc2)


def _merge_bwd(dx1, r, a, b, p_c, w_bh, w_ba, w_out, g1, carried=None):
    def body(dx_ref, r_ref, a_ref, b_ref, g_ref, wbh_ref, wba_ref, wo_ref, g1_ref,
             dr_ref, da_ref, db_ref, dg_ref, dyh_ref, dya_ref, acc_ref):
        @pl.when(pl.program_id(0) == 0)
        def _():
            acc_ref[...] = jnp.zeros_like(acc_ref)

        dx1v = dx_ref[...]
        acc_ref[0:1, :] += jnp.sum(dx1v * r_ref[...], axis=0, keepdims=True)
        dr = (g1_ref[...] * dx1v).astype(BF16)
        dr_ref[...] = dr
        dmix = _dot_nt(dr, wo_ref[...])
        sh, sa = _sigmoid(g_ref[:, :D]), _sigmoid(g_ref[:, D:])
        da = (dmix * sh).astype(BF16)
        db = (dmix * sa).astype(BF16)
        da_ref[...] = da
        db_ref[...] = db
        dg_ref[:, :D] = (dmix * a_ref[...] * sh * (1.0 - sh)).astype(BF16)
        dg_ref[:, D:] = (dmix * b_ref[...] * sa * (1.0 - sa)).astype(BF16)
        dyh_ref[...] = _dot(da, wbh_ref[...])
        dya_ref[...] = _dot(db, wba_ref[...])

    row = lambda w: pl.BlockSpec((TM, w), lambda i: (i, 0))
    return _pcall(
        body, carried, name="merge_bwd", grid=(N_LAT_TILES,),
        in_specs=[row(D), row(D), row(D), row(D), row(WC), _VMEM_WHOLE, _VMEM_WHOLE, _VMEM_WHOLE,
                  _full((1, D))],
        out_specs=[row(D), row(D), row(D), row(WC), row(HGW), row(ATW), _full((8, D))],
        out_shape=[jax.ShapeDtypeStruct((S, D), BF16), jax.ShapeDtypeStruct((S, D), BF16),
                   jax.ShapeDtypeStruct((S, D), BF16), jax.ShapeDtypeStruct((S, WC), BF16),
                   jax.ShapeDtypeStruct((S, HGW), F32), jax.ShapeDtypeStruct((S, ATW), F32),
                   jax.ShapeDtypeStruct((8, D), F32)],
        scratch_shapes=[], operands=[dx1, r, a, b, p_c, w_bh, w_ba, w_out, g1])


def _ffn_fused(x1, h2, tgt, w_gate, w_up, w_down, g2, nfw, sc2):
    def body(x1_ref, h2_ref, t_ref, wg_ref, wu_ref, wd_ref, g2_ref, nfw_ref, sc_ref,
             act_ref, dgt_ref, dup_ref, df_ref, dx_ref, acc_ref, gs, us):
        @pl.when(pl.program_id(0) == 0)
        def _():
            acc_ref[...] = jnp.zeros_like(acc_ref)

        h2 = h2_ref[...]
        f = jnp.zeros((TM, D), F32)
        for j in range(N_FF_TILES):
            g = _dot_nt(h2, wg_ref[j])
            u = _dot_nt(h2, wu_ref[j])
            gs[j] = g
            us[j] = u
            act = (g * _sigmoid(g) * u).astype(BF16)
            act_ref[j] = act
            f = f + _dot(act, wd_ref[j])
        x1v = x1_ref[...]
        g2 = g2_ref[...]
        diff = x1v + g2 * f - t_ref[...]
        dy = diff * (1.0 / D)
        df = (g2 * dy).astype(BF16)
        df_ref[...] = df
        dh2 = jnp.zeros((TM, D), F32)
        for j in range(N_FF_TILES):
            g, u = gs[j], us[j]
            sg = _sigmoid(g)
            dact = _dot_nt(df, wd_ref[j])
            dgate = (dact * u * (sg * (1.0 + g * (1.0 - sg)))).astype(BF16)
            dup = (dact * (g * sg)).astype(BF16)
            dgt_ref[j] = dgate
            dup_ref[j] = dup
            dh2 = dh2 + _dot(dgate, wg_ref[j]) + _dot(dup, wu_ref[j])
        dx, dsh, dsc, dnw = _rms_mod_bwd(x1v, nfw_ref[...], sc_ref[...], dh2)
        dx_ref[...] = dy + dx
        acc_ref[0:1, :] += dsh
        acc_ref[1:2, :] += dsc
        acc_ref[2:3, :] += dnw
        acc_ref[3:4, :] += jnp.sum(dy * f, axis=0, keepdims=True)
        acc_ref[4:5, :] += 0.5 * jnp.sum(jnp.sum(diff * diff, axis=1, keepdims=True), axis=0,
                                         keepdims=True) * (1.0 / D)

    row = lambda dt_w: pl.BlockSpec((TM, dt_w), lambda i: (i, 0))
    blk = pl.BlockSpec((N_FF_TILES, TM, FF_TILE), lambda i: (0, i, 0))
    vec = _full((1, D))
    return pl.pallas_call(
        body, name="ffn_fused", grid=(N_LAT_TILES,),
        in_specs=[row(D), row(D), row(D), _VMEM_WHOLE, _VMEM_WHOLE, _VMEM_WHOLE, vec, vec, vec],
        out_specs=[blk, blk, blk, row(D), row(D), _full((8, D))],
        out_shape=[jax.ShapeDtypeStruct((N_FF_TILES, S, FF_TILE), BF16)] * 3
        + [jax.ShapeDtypeStruct((S, D), BF16), jax.ShapeDtypeStruct((S, D), F32),
           jax.ShapeDtypeStruct((8, D), F32)],
        scratch_shapes=[pltpu.VMEM((N_FF_TILES, TM, FF_TILE), F32), pltpu.VMEM((N_FF_TILES, TM, FF_TILE), F32)],
        compiler_params=_cp(("arbitrary",)),
    )(x1, h2, tgt, w_gate, w_up, w_down, g2, nfw, sc2)


def _proj_bc(h_all, w_b, w_c, carried=None):
    def body(h_ref, wb_ref, wc_ref, pb_ref, pc_ref):
        h = h_ref[...]
        pb_ref[...] = _dot_nt(h, wb_ref[...])

        @pl.when(pl.program_id(0) > 0)
        def _():
            pc_ref[...] = _dot_nt(h, wc_ref[...])

    return _pcall(
        body, carried, name="proj_bc", grid=(N_TILES,),
        in_specs=[pl.BlockSpec((TM, D), lambda i: (i, 0)), _VMEM_WHOLE, _VMEM_WHOLE],
        out_specs=[pl.BlockSpec((TM, WB), lambda i: (i, 0)), pl.BlockSpec((TM, WC), lambda i: (_lat(i), 0))],
        out_shape=[jax.ShapeDtypeStruct((T, WB), F32), jax.ShapeDtypeStruct((S, WC), F32)],
        scratch_shapes=[], operands=[h_all, w_b, w_c])


def _input_bwd(dp_a, dp_b, dp_c, w_a, w_b, w_c, ctx, x, dx1, nw, sh, sc, carried=None):
    def body(da_ref, db_ref, dc_ref, wa_ref, wb_ref, wc_ref, ctx_ref, x_ref, dx1_ref, nw_ref, sh_ref,
             sc_ref, gx_ref, acc_ref):
        i = pl.program_id(0)

        @pl.when(i == 0)
        def _():
            acc_ref[...] = jnp.zeros_like(acc_ref)

        dh = _dot(da_ref[...], wa_ref[...]) + _dot(db_ref[...], wb_ref[...])

        @pl.when(i == 0)
        def _():
            _, dsh, dsc, dnw = _rms_mod_bwd(ctx_ref[...], nw_ref[...], sc_ref[0:1, :], dh)
            acc_ref[3:4, :] += dsh
            acc_ref[4:5, :] += dsc
            acc_ref[2:3, :] += dnw

        @pl.when(i > 0)
        def _():
            dhl = dh + _dot(dc_ref[...], wc_ref[...])
            dx, dsh, dsc, dnw = _rms_mod_bwd(x_ref[...], nw_ref[...], sc_ref[1:2, :], dhl)
            gx_ref[...] = dx1_ref[...] + dx
            acc_ref[0:1, :] += dsh
            acc_ref[1:2, :] += dsc
            acc_ref[2:3, :] += dnw

    lat = lambda w: pl.BlockSpec((TM, w), lambda i: (_lat(i), 0))
    return _pcall(
        body, carried, name="input_bwd", grid=(N_TILES,),
        in_specs=[pl.BlockSpec((TM, WA), lambda i: (i, 0)), pl.BlockSpec((TM, WB), lambda i: (i, 0)),
                  lat(WC), _VMEM_WHOLE, _VMEM_WHOLE, _VMEM_WHOLE, _full((TM, D)), lat(D), lat(D),
                  _full((1, D)), _full((2, D)), _full((2, D))],
        out_specs=[lat(D), _full((8, D))],
        out_shape=[jax.ShapeDtypeStruct((S, D), F32), jax.ShapeDtypeStruct((8, D), F32)],
        scratch_shapes=[], operands=[dp_a, dp_b, dp_c, w_a, w_b, w_c, ctx, x, dx1, nw, sh, sc])


_C1 = 1.0 - ADAM_B1 ** ADAM_STEP
_C2 = 1.0 - ADAM_B2 ** ADAM_STEP


def _adamw_math(w, g, m, v):
    m = ADAM_B1 * m + (1.0 - ADAM_B1) * g
    v = ADAM_B2 * v + (1.0 - ADAM_B2) * (g * g)
    m_hat = m / _C1
    v_hat = v / _C2
    delta = -ADAM_LR * (m_hat / (jnp.sqrt(v_hat) + ADAM_EPS) + ADAM_WD * w)
    return delta, m, v


def _adamw_sharded(terms, w, m, v, name, tr):
    rows, cols = w.shape

    def body(t_ref, w_ref, m_ref, v_ref, g_ref, d_ref, nm_ref, nv_ref):
        g = t_ref[0].astype(F32)
        for s in range(1, N_CHIPS):
            g = g + t_ref[s].astype(F32)
        g_ref[...] = g
        d_ref[...], nm_ref[...], nv_ref[...] = _adamw_math(w_ref[...], g, m_ref[...], v_ref[...])

    blk = pl.BlockSpec((tr, cols), lambda i: (i, 0))
    return pl.pallas_call(
        body, name=name, grid=(rows // tr,),
        in_specs=[pl.BlockSpec((N_CHIPS, tr, cols), lambda i: (0, i, 0)), blk, blk, blk],
        out_specs=[blk] * 4,
        out_shape=[jax.ShapeDtypeStruct((rows, cols), F32)] * 4,
        compiler_params=_cp(("parallel",)),
    )(terms, w, m, v)


def _adamw_plain(g, w, m, v, name):
    def body(g_ref, w_ref, m_ref, v_ref, d_ref, nm_ref, nv_ref):
        d_ref[...], nm_ref[...], nv_ref[...] = _adamw_math(w_ref[...], g_ref[...], m_ref[...], v_ref[...])

    return pl.pallas_call(
        body, name=name, in_specs=[_VMEM_WHOLE] * 4, out_specs=[_VMEM_WHOLE] * 3,
        out_shape=[jax.ShapeDtypeStruct(w.shape, F32)] * 3,
        compiler_params=_cp(),
    )(g, w, m, v)


SMALL_ROWS = 16
R_DMOD, R_DCTX, R_NMIX, R_NFFN, R_MISC, R_DLB, R_BADA01 = 0, 6, 8, 9, 10, 11, 13
M_HNW, M_QNW, M_KNW, M_SINK, M_LOSS = 0, 128, 256, 384, 512


def _pack_small(acc_in, acc_mg, acc_ffn, acc_prep, dsink, dlb):
    def body(in_ref, mg_ref, ff_ref, pp_ref, ds_ref, dlb_ref, o_ref):
        o_ref[...] = jnp.zeros_like(o_ref)
        o_ref[0:2, :] = in_ref[0:2, :]
        o_ref[2:3, :] = mg_ref[0:1, :]
        o_ref[3:5, :] = ff_ref[0:2, :]
        o_ref[5:6, :] = ff_ref[3:4, :]
        o_ref[6:8, :] = in_ref[3:5, :]
        o_ref[8:9, :] = in_ref[2:3, :]
        o_ref[9:10, :] = ff_ref[2:3, :]
        o_ref[10:11, M_HNW:M_HNW + HG_DIM] = pp_ref[0:1, 0:HG_DIM]
        r = lax.broadcasted_iota(jnp.int32, (ATW, 128), 0)
        c = lax.broadcasted_iota(jnp.int32, (ATW, 128), 1)
        fold = jnp.where((r % HEAD_DIM == c) & (c < HEAD_DIM), 1.0, 0.0).astype(BF16)
        qk = jnp.concatenate([pp_ref[1:2, :], pp_ref[2:3, :], jnp.zeros((6, ATW), F32)], axis=0)
        folded = _dot_exact_rhs01(qk, fold)
        o_ref[10:11, M_QNW:M_QNW + 128] = folded[0:1, :]
        o_ref[10:11, M_KNW:M_KNW + 128] = folded[1:2, :]
        o_ref[10:11, M_SINK:M_SINK + ATT_HEADS] = ds_ref[...]
        o_ref[10:11, M_LOSS:M_LOSS + 128] = ff_ref[4:5, 0:128]
        o_ref[11:13, 0:HGW] = dlb_ref[...]

    return pl.pallas_call(
        body, name="pack_small", in_specs=[_VMEM_WHOLE] * 6, out_specs=_VMEM_WHOLE,
        out_shape=jax.ShapeDtypeStruct((SMALL_ROWS, D), F32), compiler_params=_cp(),
    )(acc_in, acc_mg, acc_ffn, acc_prep, dsink, dlb)


def _sum_small(gathered):
    def body(g_ref, o_ref):
        tot = g_ref[0]
        for s in range(1, N_DEV):
            tot = tot + g_ref[s]
        o_ref[...] = tot
        o_ref[R_BADA01:R_BADA01 + 2, :] = tot[0:2, :] + tot[R_DCTX:R_DCTX + 2, :]

    return pl.pallas_call(
        body, name="sum_small", in_specs=[_VMEM_WHOLE], out_specs=_VMEM_WHOLE,
        out_shape=jax.ShapeDtypeStruct((SMALL_ROWS, D), F32), compiler_params=_cp(),
    )(gathered)


_REP_NAMES = ("b_ada", "c_ctx", "norm_mix_w", "norm_ffn_w", "hgrn_norm_w", "q_norm_w", "k_norm_w", "attn_sinks")


def _adamw_replicated(tot, g_c_ctx, ws, ms, vs):
    n = len(_REP_NAMES)

    def body(*refs):
        tot_ref, gc_ref = refs[0], refs[1]
        w_refs, m_refs, v_refs = refs[2:2 + n], refs[2 + n:2 + 2 * n], refs[2 + 2 * n:2 + 3 * n]
        outs = refs[2 + 3 * n:]
        row = lambda r: tot_ref[r:r + 1, :]
        misc = row(R_MISC)
        grads = [jnp.concatenate([row(R_BADA01), row(R_BADA01 + 1)] + [row(k) for k in range(2, 6)], axis=1),
                 gc_ref[...], row(R_NMIX), row(R_NFFN),
                 misc[:, M_HNW:M_HNW + HG_DIM], misc[:, M_QNW:M_QNW + HEAD_DIM],
                 misc[:, M_KNW:M_KNW + HEAD_DIM], misc[:, M_SINK:M_SINK + ATT_HEADS]]
        for k in range(n):
            outs[k][...] = grads[k]
            outs[n + k][...], outs[2 * n + k][...], outs[3 * n + k][...] = _adamw_math(
                w_refs[k][...], grads[k], m_refs[k][...], v_refs[k][...])

    shapes = [jax.ShapeDtypeStruct(w.shape, F32) for w in ws]
    return pl.pallas_call(
        body, name="adamw_replicated", in_specs=[_VMEM_WHOLE] * (2 + 3 * n), out_specs=[_VMEM_WHOLE] * (4 * n),
        out_shape=shapes * 4, compiler_params=_cp(),
    )(tot, g_c_ctx, *ws, *ms, *vs)


def _lb_grads(dlb, lbl):
    def body(d_ref, l_ref, o_ref):
        for d in (0, 1):
            ll = l_ref[d]
            lb = _sigmoid(ll[0:1, :] - ll[1:2, :])
            t = d_ref[d:d + 1, :] * lb * (1.0 - lb)
            o_ref[d, 0:1, :] = t
            o_ref[d, 1:2, :] = -t

    return pl.pallas_call(
        body, name="lb_grads", in_specs=[_VMEM_WHOLE] * 2, out_specs=_VMEM_WHOLE,
        out_shape=jax.ShapeDtypeStruct((2, 2, HGW), F32), compiler_params=_cp(),
    )(dlb, lbl)


def _c_ctx_grad(terms, c_ctx):
    def body(t_ref, c_ref, o_ref):
        tot = t_ref[0, 8:9, :]
        for s in range(1, N_DEV):
            tot = tot + t_ref[s, 8:9, :]
        cv = c_ref[...]
        sg = _sigmoid(cv)
        o_ref[...] = tot * (sg * (1.0 + cv * (1.0 - sg)))

    return pl.pallas_call(
        body, name="c_ctx_grad", in_specs=[_VMEM_WHOLE] * 2, out_specs=_VMEM_WHOLE,
        out_shape=jax.ShapeDtypeStruct((1, D), F32), compiler_params=_cp(),
    )(terms, c_ctx)


def _in_perm():
    fz, bz, inp, kk, vv, qhg, ghg, qat, gates = 0, 512, 1024, 1536, 1664, 1792, 2304, 2816, 3328
    cols = []
    for h in range(HG_HEADS):
        for base in (fz, bz, inp, qhg):
            cols += list(range(base + 128 * h, base + 128 * (h + 1)))
    cols += list(range(ghg, ghg + 512)) + list(range(qat, qat + 512))
    cols += list(range(kk, kk + 128)) + list(range(vv, vv + 128))
    cols += list(range(gates, gates + 2048))
    return np.asarray(cols, np.int32)


_PERM = _in_perm()
_INV_PERM = np.argsort(_PERM).astype(np.int32)


ROW_BLK = 128


def _block_table(lo, hi):
    return jnp.asarray([_PERM[r] // ROW_BLK for r in range(lo, hi, ROW_BLK)], jnp.int32)


def _pick_row_blocks(x, table, name):
    cols = x.shape[1]

    def body(t_ref, x_ref, o_ref):
        o_ref[...] = x_ref[...]

    return pl.pallas_call(
        body, name=name,
        grid_spec=pltpu.PrefetchScalarGridSpec(
            num_scalar_prefetch=1, grid=(table.shape[0],),
            in_specs=[pl.BlockSpec((ROW_BLK, cols), lambda i, t: (t[i], 0))],
            out_specs=pl.BlockSpec((ROW_BLK, cols), lambda i, t: (i, 0))),
        out_shape=jax.ShapeDtypeStruct((table.shape[0] * ROW_BLK, cols), x.dtype),
        compiler_params=_cp(("arbitrary",)),
    )(table, x)


def _place_row_blocks(x, table, into, out_rows, name):
    cols = x.shape[1]

    def body(t_ref, x_ref, *rest):
        rest[-1][...] = x_ref[...]

    operands, in_specs, aliases = [table, x], [pl.BlockSpec((ROW_BLK, cols), lambda i, t: (i, 0))], {}
    if into is not None:
        operands.append(into)
        in_specs.append(_ANY)
        aliases = {2: 0}
    return pl.pallas_call(
        body, name=name,
        grid_spec=pltpu.PrefetchScalarGridSpec(
            num_scalar_prefetch=1, grid=(table.shape[0],), in_specs=in_specs,
            out_specs=pl.BlockSpec((ROW_BLK, cols), lambda i, t: (t[i], 0))),
        out_shape=jax.ShapeDtypeStruct((out_rows, cols), x.dtype),
        input_output_aliases=aliases,
        compiler_params=_cp(("arbitrary",)),
    )(*operands)


def _cols_from_blocks(g):
    return jnp.transpose(g, (1, 0, 2)).reshape(g.shape[1], N_DEV * g.shape[2])


def _local_step(x2, ctx2, tgt, lbl, sh_in, sc_in, gate1, sh2, sc2, gate2, norm_mix_w, norm_ffn_w,
                hgrn_norm_w, q_norm_w, k_norm_w, attn_sinks, w_a, w_b, w_c, s_bh, s_ba, s_out,
                s_gate, s_up, s_down):
    first_last = lambda n: [(0, True), (n - 1, False)]
    h_all = _norm_mod_all(ctx2, x2, norm_mix_w, sh_in, sc_in)
    p_a = _mm_nt(h_all, w_a, tm=768, tn=1024, out_dtype=F32, name="proj_a")
    (o, st), (g_gate,) = _hgrn_fwd(
        p_a, lbl, (_gather_comm([s_gate]), [(0, True), (HG_HEADS - 2, True), (HG_HEADS - 1, False)]))
    (p_b, p_c), (g_bh, g_ba, g_out) = _proj_bc(
        h_all, w_b, w_c, (_gather_comm([s_bh, s_ba, s_out]), [(0, True), (N_TILES - 2, True), (N_TILES - 1, False)]))
    cos, sin = _rope_tables()
    qnw_t, knw_t = jnp.tile(q_norm_w, (1, ATT_HEADS)), jnp.tile(k_norm_w, (1, KV_HEADS))
    y_hg, qn, k_rep, v_rep = _prep_fwd(p_b, o, cos, sin, hgrn_norm_w, qnw_t, knw_t)
    (y_at, lse), (g_up, g_down) = _attn_fwd(
        qn, k_rep, v_rep, attn_sinks,
        (_gather_comm([s_up, s_down]), [(0, True), (N_BLOCKS - 3, True), (N_BLOCKS - 1, False)]))
    w_bh, w_ba, w_o = g_bh.reshape(D, HGW), g_ba.reshape(D, ATW), g_out.reshape(D, D)
    g_gate, g_up, g_down = [g.reshape(N_FF_TILES, FF_TILE, D) for g in (g_gate, g_up, g_down)]
    a, b, mixed, r, x1, h2 = _merge_fwd(y_hg, y_at, p_c, x2, w_bh, w_ba, w_o, gate1, norm_ffn_w, sh2, sc2)

    act, d_gate, d_up, d_f, dx1, acc_ffn = _ffn_fused(x1, h2, tgt, g_gate, g_up, g_down, gate2,
                                                      norm_ffn_w, sc2)
    by_chip = lambda t: t.reshape((N_CHIPS, 2) + t.shape[1:])
    ff_by_chip = lambda t: t.reshape(N_CHIPS, 2, FF_BLK, D)
    t_down, _ = _mm_tn_blocked(act, d_f, "grad_down")
    t_down = ff_by_chip(t_down)
    t_gate, (f_down,) = _mm_tn_blocked(d_gate, h2, "grad_gate", (_sibling_comm([t_down]), first_last(N_FF_TILES)))
    t_gate = ff_by_chip(t_gate)
    t_up, (f_gate,) = _mm_tn_blocked(d_up, h2, "grad_up", (_sibling_comm([t_gate]), first_last(N_FF_TILES)))
    t_up = ff_by_chip(t_up)

    (d_r, d_a, d_b, dp_c, dy_hg, dy_at, acc_mg), (f_up,) = _merge_bwd(
        dx1, r, a, b, p_c, w_bh, w_ba, w_o, gate1, (_sibling_comm([t_up]), first_last(N_LAT_TILES)))
    c_down, c_gate, c_up = [_pair_sum(t, f, "pair_sum_" + nm) for t, f, nm in
                            ((t_down, f_down, "down"), (t_gate, f_gate, "gate"), (t_up, f_up, "up"))]
    t_out = _mm_tn(mixed, d_r, tk=512, nk=4, tm=512, tn=1024, out_dtype=BF16, name="grad_out")
    t_bh = _mm_tn(d_a, y_hg, tk=512, nk=4, tm=512, tn=512, out_dtype=BF16, name="grad_bh")
    t_ba = _mm_tn(d_b, y_at, tk=512, nk=4, tm=512, tn=512, out_dtype=BF16, name="grad_ba")
    t_bh, t_ba, t_out = [by_chip(t.reshape(N_DEV, D // N_DEV, t.shape[1])) for t in (t_bh, t_ba, t_out)]
    (dq, dk_rep, dv_rep, dsink), (r_up,) = _attn_bwd(
        qn, k_rep, v_rep, attn_sinks, y_at, lse, dy_at, (_chip_comm([c_up]), first_last(N_BLOCKS)))
    (dp_b, d_o, acc_prep), (f_bh, f_ba, f_out) = _prep_bwd(
        p_b, o, cos, sin, hgrn_norm_w, qnw_t, knw_t, dy_hg, dq, dk_rep, dv_rep,
        (_sibling_comm([t_bh, t_ba, t_out]), first_last(N_TILES)))
    c_bh, c_ba, c_out = [_pair_sum(t, f, "pair_sum_" + nm) for t, f, nm in
                         ((t_bh, f_bh, "bh"), (t_ba, f_ba, "ba"), (t_out, f_out, "out"))]
    (dp_a, dlb), (r_bh, r_ba, r_out, r_down, r_gate) = _hgrn_bwd(
        p_a, lbl, d_o, st, (_chip_comm([c_bh, c_ba, c_out, c_down, c_gate]), first_last(HG_HEADS)))
    t_a = _mm_tn(dp_a, h_all, tk=768, nk=3, tm=1024, tn=1024, out_dtype=BF16, name="grad_in_a")
    t_b = _mm_tn(dp_b, h_all, tk=768, nk=3, tm=640, tn=1024, out_dtype=BF16, name="grad_in_b")
    t_c = _mm_tn(dp_c, h_all, tk=256, nk=8, b_off=1, tm=1024, tn=1024, out_dtype=BF16, name="grad_in_c")
    t_in = None
    for piece, lo, nm in ((t_a, 0, "a"), (t_b, WA, "b"), (t_c, WA + WB, "c")):
        t_in = _place_row_blocks(piece, _block_table(lo, lo + piece.shape[0]), t_in, IN_COLS, "order_terms_" + nm)
    t_in = by_chip(t_in.reshape(N_DEV, IN_BLK, D))
    (f_in,) = _run_comm(_sibling_comm([t_in]), "scatter_in_sibling")
    (grad_x, acc_in), (r_in,) = _input_bwd(
        dp_a, dp_b, dp_c, w_a, w_b, w_c, ctx2, x2, dx1, norm_mix_w, sh_in, sc_in,
        (_chip_comm([_pair_sum(t_in, f_in, "pair_sum_in")]), first_last(N_TILES)))
    small = _pack_small(acc_in, acc_mg, acc_ffn, acc_prep, dsink, dlb)
    return grad_x, small, [r_in, r_bh, r_ba, r_out, r_gate, r_up, r_down]


def kernel(x, c, ctx, c_ctx, w_ada, b_ada, norm_mix_w, norm_ffn_w, w_in, hgrn_lb_logits, hgrn_norm_w, q_norm_w, k_norm_w, attn_sinks, w_branch_hgrn, w_branch_attn, w_out, w_ffn_gate, w_ffn_up, w_ffn_down, loss_target, m_c_ctx, m_w_ada, m_b_ada, m_norm_mix_w, m_norm_ffn_w, m_w_in, m_hgrn_lb_logits, m_hgrn_norm_w, m_q_norm_w, m_k_norm_w, m_attn_sinks, m_w_branch_hgrn, m_w_branch_attn, m_w_out, m_w_ffn_gate, m_w_ffn_up, m_w_ffn_down, v_c_ctx, v_w_ada, v_b_ada, v_norm_mix_w, v_norm_ffn_w, v_w_in, v_hgrn_lb_logits, v_hgrn_norm_w, v_q_norm_w, v_k_norm_w, v_attn_sinks, v_w_branch_hgrn, v_w_branch_attn, v_w_out, v_w_ffn_gate, v_w_ffn_up, v_w_ffn_down):
    me = 4 * lax.axis_index("x") + 2 * lax.axis_index("y") + lax.axis_index("c")
    x2, ctx2, tgt = x[0], ctx[0], loss_target[0]
    w_ada2, w_in2 = w_ada[0], w_in[0]

    blk = jnp.zeros((8, D), F32).at[0].set(c[0]).at[1, :256].set(hgrn_lb_logits.reshape(256))
    (g0,) = _all_gather([blk], "gather_cond", True)
    cc = jnp.zeros((16, D), F32).at[:8].set(g0[:, 0, :]).at[8].set(c_ctx)
    lbl = jnp.transpose(g0[:, 1, :256].reshape(N_DEV, 2, 2, 64), (1, 2, 0, 3)).reshape(2, 2, HGW)

    b_cols = lax.dynamic_slice(b_ada, (0, me * ADA_BLK), (1, ADA_BLK))
    (g1,) = _all_gather([_ada_rows(cc, w_ada2, b_cols)], "gather_mod", True)
    mod_all = _cols_from_blocks(g1)
    mod = lax.dynamic_slice(mod_all, (me, 0), (1, 6 * D)).reshape(6, D)
    mod_c = mod_all[8].reshape(6, D)
    sh1, sc1, gate1, sh2, sc2, gate2 = [mod[k:k + 1] for k in range(6)]
    sh_in = jnp.concatenate([mod_c[0:1], sh1], axis=0)
    sc_in = jnp.concatenate([mod_c[1:2], sc1], axis=0)

    shards = [w_branch_hgrn[0].T, w_branch_attn[0].T, w_out[0], w_ffn_gate[0].T, w_ffn_up[0].T, w_ffn_down[0]]
    (g_in,) = _all_gather([w_in2.T.astype(BF16)], "gather_w_in", False)
    w_in_t = g_in.reshape(IN_COLS, D)
    w_a, w_b, w_c = [_pick_row_blocks(w_in_t, _block_table(lo, hi), "order_w_" + nm)
                     for lo, hi, nm in ((0, WA, "a"), (WA, WA + WB, "b"), (WA + WB, IN_COLS, "c"))]

    grad_x, small, (r_in, r_bh, r_ba, r_out, r_gate, r_up, r_down) = _local_step(
        x2, ctx2, tgt, lbl, sh_in, sc_in, gate1, sh2, sc2, gate2, norm_mix_w, norm_ffn_w, hgrn_norm_w,
        q_norm_w, k_norm_w, attn_sinks, w_a, w_b, w_c, *[s.astype(BF16) for s in shards])

    big = {}
    for nm, rr, ww, mm, vv, tr, transposed in (
            ("w_in", r_in, w_in2, m_w_in[0], v_w_in[0], 336, True),
            ("w_branch_hgrn", r_bh, w_branch_hgrn[0], m_w_branch_hgrn[0], v_w_branch_hgrn[0], 128, True),
            ("w_branch_attn", r_ba, w_branch_attn[0], m_w_branch_attn[0], v_w_branch_attn[0], 128, True),
            ("w_out", r_out, w_out[0], m_w_out[0], v_w_out[0], 128, False),
            ("w_ffn_gate", r_gate, w_ffn_gate[0], m_w_ffn_gate[0], v_w_ffn_gate[0], 352, True),
            ("w_ffn_up", r_up, w_ffn_up[0], m_w_ffn_up[0], v_w_ffn_up[0], 352, True),
            ("w_ffn_down", r_down, w_ffn_down[0], m_w_ffn_down[0], v_w_ffn_down[0], 352, False)):
        if transposed:
            res = _adamw_sharded(rr, ww.T, mm.T, vv.T, "adamw_" + nm, tr)
            big[nm] = [t.T[None] for t in res]
        else:
            big[nm] = [t[None] for t in _adamw_sharded(rr, ww, mm, vv, "adamw_" + nm, tr)]

    (g2,) = _all_gather([small], "gather_small", True)
    tot = _sum_small(g2)
    dm = jnp.zeros((16, 6 * D), F32).at[:8].set(g2[:, R_DMOD:R_DMOD + 6, :].reshape(N_DEV, 6 * D))
    dm = dm.at[8, :2 * D].set(tot[R_DCTX:R_DCTX + 2].reshape(2 * D))
    dm_cols = lax.dynamic_slice(dm, (0, me * ADA_BLK), (16, ADA_BLK))
    g_w_ada, dsc_term = _ada_grads(cc, dm_cols, w_ada2)
    (g3,) = _all_gather([dsc_term], "gather_cctx", True)
    g_c_ctx = _c_ctx_grad(g3, c_ctx.reshape(1, D))
    g_lbl = _lb_grads(tot[R_DLB:R_DLB + 2, :HGW], lbl)
    g_lb_mine = lax.dynamic_slice(g_lbl, (0, 0, me * 64), (2, 2, 64))
    misc = tot[R_MISC]
    loss = misc[M_LOSS]

    rep_out = _adamw_replicated(
        tot, g_c_ctx,
        [b_ada, c_ctx.reshape(1, D), norm_mix_w, norm_ffn_w, hgrn_norm_w, q_norm_w, k_norm_w, attn_sinks],
        [m_b_ada, m_c_ctx.reshape(1, D), m_norm_mix_w, m_norm_ffn_w, m_hgrn_norm_w, m_q_norm_w, m_k_norm_w,
         m_attn_sinks],
        [v_b_ada, v_c_ctx.reshape(1, D), v_norm_mix_w, v_norm_ffn_w, v_hgrn_norm_w, v_q_norm_w, v_k_norm_w,
         v_attn_sinks])
    rep = []
    for kind in range(4):
        vals = dict(zip(_REP_NAMES, rep_out[kind * len(_REP_NAMES):(kind + 1) * len(_REP_NAMES)]))
        vals["c_ctx"] = vals["c_ctx"].reshape(D)
        rep.append(vals)

    d_ada, nm_ada, nv_ada = _adamw_plain(g_w_ada, w_ada2, m_w_ada[0], v_w_ada[0], "adamw_w_ada")
    ada = [t[None] for t in (g_w_ada, d_ada, nm_ada, nv_ada)]
    lb_w = hgrn_lb_logits.reshape(4, 64)
    d_lb, nm_lb, nv_lb = _adamw_plain(g_lb_mine.reshape(4, 64), lb_w, m_hgrn_lb_logits.reshape(4, 64),
                                      v_hgrn_lb_logits.reshape(4, 64), "adamw_lb")
    lbs = [t.reshape(2, 2, 64) for t in (g_lb_mine, d_lb, nm_lb, nv_lb)]

    names = ['c_ctx', 'w_ada', 'b_ada', 'norm_mix_w', 'norm_ffn_w', 'w_in', 'hgrn_lb_logits', 'hgrn_norm_w',
             'q_norm_w', 'k_norm_w', 'attn_sinks', 'w_branch_hgrn', 'w_branch_attn', 'w_out', 'w_ffn_gate',
             'w_ffn_up', 'w_ffn_down']
    outs = [loss, grad_x[None]]
    for kind in range(4):
        for nm in names:
            if nm == 'w_ada':
                outs.append(ada[kind])
            elif nm == 'hgrn_lb_logits':
                outs.append(lbs[kind])
            elif nm in big:
                outs.append(big[nm][kind])
            else:
                outs.append(rep[kind][nm])
    return tuple(outs)
```

```python
import functools
import math

import numpy as np
import jax
import jax.numpy as jnp
from jax import lax
from jax.experimental import pallas as pl
from jax.experimental.pallas import tpu as pltpu

F32 = jnp.float32
BF16 = jnp.bfloat16

N_DEV = 8
D = 1024
S = 2048
L = 256
T = L + S
TM = 256
N_TILES = T // TM
N_LAT_TILES = S // TM
HG_HEADS = 4
HG_DIM = 128
HGW = 512
CHUNK = 32
N_CHUNKS = T // CHUNK
N_CTX_CHUNKS = L // CHUNK
N_LAT_CHUNKS = S // CHUNK
ATT_HEADS = 8
KV_HEADS = 2
HEAD_DIM = 64
ATW = 512
KVW = 128
BLOCK = 128
N_BLOCKS = S // BLOCK
GRID_W = 64
ROPE_THETA = 10000.0
D_FF = 2816
FF_BLK = D_FF // N_DEV
FF_TILE = 256
N_FF_TILES = D_FF // FF_TILE
IN_COLS = 5376
IN_BLK = IN_COLS // N_DEV
ADA_BLK = 6 * D // N_DEV
EPS = 1e-6
WA, WB, WC = 2048, 1280, 2048

ADAM_LR = 0.001
ADAM_B1 = 0.9
ADAM_B2 = 0.999
ADAM_EPS = 1e-08
ADAM_WD = 0.01
ADAM_STEP = 10

VMEM_LIMIT = 56 * 1024 * 1024
MESH = pl.DeviceIdType.MESH


def _cp(sem=None, vmem=VMEM_LIMIT):
    return pltpu.CompilerParams(dimension_semantics=sem, vmem_limit_bytes=vmem)


def _full(shape):
    n = len(shape)
    return pl.BlockSpec(shape, lambda *_: (0,) * n)


_VMEM_WHOLE = pl.BlockSpec(memory_space=pltpu.VMEM)
_ANY = pl.BlockSpec(memory_space=pl.ANY)


def _sigmoid(v):
    return 1.0 / (1.0 + jnp.exp(-v))


def _dot(a, b):
    return jnp.dot(a, b, preferred_element_type=F32)


def _dot_nt(a, b):
    return lax.dot_general(a, b, (((1,), (1,)), ((), ())), preferred_element_type=F32)


def _dot_tn(a, b):
    return lax.dot_general(a, b, (((0,), (0,)), ((), ())), preferred_element_type=F32)


def _split3(v):
    hi = v.astype(BF16)
    r = v - hi.astype(F32)
    mid = r.astype(BF16)
    lo = (r - mid.astype(F32)).astype(BF16)
    return hi, mid, lo


def _dot_exact_rhs01(v, m01):
    hi, mid, lo = _split3(v)
    return _dot(hi, m01) + _dot(mid, m01) + _dot(lo, m01)


def _split2(v):
    hi = v.astype(BF16)
    return hi, (v - hi.astype(F32)).astype(BF16)


def _dot_lhs01(m01, v):
    hi, lo = _split2(v)
    return _dot(m01, hi) + _dot(m01, lo)


def _dot_f32(a, b, dot=_dot):
    ah, am, al = _split3(a)
    bh, bm, bl = _split3(b)
    return (dot(ah, bh) + (dot(ah, bm) + dot(am, bh))
            + (dot(am, bm) + dot(ah, bl) + dot(al, bh)))


def _my_pos():
    return lax.axis_index("x"), lax.axis_index("y"), lax.axis_index("c")


class _Comm:
    def __init__(self, operands, out_shapes, sems, phases):
        self.operands, self.out_shapes, self.sems, self.phases = operands, out_shapes, sems, phases


def _gather_comm(blocks):
    n = len(blocks)

    def parts(ins, outs, sems):
        send_sems, recv_sems, local_sems = sems
        x, y, c = _my_pos()
        me, sibling = (x, y, c), (x, y, 1 - c)
        chips = [(1 - x, y), (x, 1 - y), (1 - x, 1 - y)]

        def slot(a, px, py, pc):
            return outs[a].at[4 * px + 2 * py + pc]

        def copy(a, k, block, to, src=None):
            return pltpu.make_async_remote_copy(
                src_ref=slot(a, *block) if src is None else src, dst_ref=slot(a, *block),
                send_sem=send_sems.at[a, k], recv_sem=recv_sems.at[a, k],
                device_id=to, device_id_type=MESH)

        mine = [pltpu.make_async_copy(ins[a], slot(a, *me), local_sems.at[a]) for a in range(n)]
        first = []
        for a in range(n):
            first.append(copy(a, 0, me, sibling, src=ins[a]))
            first += [copy(a, 1 + j, me, (*chip, c), src=ins[a]) for j, chip in enumerate(chips)]
        passed = [copy(a, 4 + j, (*chip, c), sibling) for j, chip in enumerate(chips) for a in range(n)]
        return c, me, sibling, chips, copy, mine, first, passed

    def start(ins, outs, sems):
        _, _, _, _, _, mine, first, _ = parts(ins, outs, sems)
        for cp in mine + first:
            cp.start()

    def forward(ins, outs, sems):
        c, me, _, chips, copy, _, _, passed = parts(ins, outs, sems)
        for j, chip in enumerate(chips):
            for a in range(n):
                copy(a, 1 + j, (*chip, c), me).wait_recv()
                passed[j * n + a].start()

    def finish(ins, outs, sems):
        c, me, sibling, chips, copy, mine, first, passed = parts(ins, outs, sems)
        for a in range(n):
            copy(a, 0, sibling, me).wait_recv()
            for j, chip in enumerate(chips):
                copy(a, 4 + j, (*chip, 1 - c), me).wait_recv()
        for cp in first + passed:
            cp.wait_send()
        for cp in mine:
            cp.wait()

    return _Comm(blocks, [jax.ShapeDtypeStruct((N_DEV,) + b.shape, b.dtype) for b in blocks],
                 [pltpu.SemaphoreType.DMA((n, 7)), pltpu.SemaphoreType.DMA((n, 7)), pltpu.SemaphoreType.DMA((n,))],
                 [start, forward, finish])


def _run_comm(comm, name, in_vmem=False):
    n_in, n_out = len(comm.operands), len(comm.out_shapes)

    def body(*refs):
        ins, outs, sems = refs[:n_in], refs[n_in:n_in + n_out], refs[n_in + n_out:]
        for phase in comm.phases:
            phase(ins, outs, sems)

    spec = _VMEM_WHOLE if in_vmem else _ANY
    return pl.pallas_call(
        body, name=name, out_shape=comm.out_shapes, in_specs=[spec] * n_in, out_specs=[spec] * n_out,
        scratch_shapes=comm.sems,
    )(*comm.operands)


def _carrier_call(body, comm, schedule, *, name, grid, in_specs, out_specs, out_shape, scratch_shapes, operands):
    n_in, n_out, n_scr = len(in_specs), len(out_specs), len(scratch_shapes)
    c_in, c_out = len(comm.operands), len(comm.out_shapes)

    def full_body(*refs):
        ins, refs = refs[:n_in], refs[n_in:]
        cins, refs = refs[:c_in], refs[c_in:]
        outs, refs = refs[:n_out], refs[n_out:]
        couts, refs = refs[:c_out], refs[c_out:]
        scr, csems = refs[:n_scr], refs[n_scr:]
        step = pl.program_id(0)

        def run(before):
            for (at, when_before), phase in zip(schedule, comm.phases):
                if when_before == before:
                    pl.when(step == at)(functools.partial(phase, cins, couts, csems))

        run(True)
        body(*ins, *outs, *scr)
        run(False)

    res = pl.pallas_call(
        full_body, name=name, grid=grid,
        in_specs=list(in_specs) + [_ANY] * c_in, out_specs=list(out_specs) + [_ANY] * c_out,
        out_shape=list(out_shape) + list(comm.out_shapes),
        scratch_shapes=list(scratch_shapes) + list(comm.sems),
        compiler_params=_cp(("arbitrary",)),
    )(*operands, *comm.operands)
    return res[:n_out], res[n_out:]


def _pcall(body, carried, *, name, grid, in_specs, out_specs, out_shape, scratch_shapes, operands):
    if carried is None:
        res = pl.pallas_call(body, name=name, grid=grid, in_specs=in_specs, out_specs=out_specs,
                             out_shape=out_shape, scratch_shapes=scratch_shapes,
                             compiler_params=_cp(("arbitrary",)))(*operands)
        return res, ()
    return _carrier_call(body, carried[0], carried[1], name=name, grid=grid, in_specs=in_specs,
                         out_specs=out_specs, out_shape=out_shape, scratch_shapes=scratch_shapes,
                         operands=operands)


def _all_gather(blocks, name, in_vmem):
    return _run_comm(_gather_comm(blocks), name, in_vmem)


N_CHIPS = 4


def _sibling_comm(contribs):
    n = len(contribs)

    def copies(ins, outs, sems):
        send_sems, recv_sems = sems
        x, y, c = _my_pos()
        return [pltpu.make_async_remote_copy(
            src_ref=ins[a].at[pl.ds(0, N_CHIPS), 1 - c], dst_ref=outs[a],
            send_sem=send_sems.at[a], recv_sem=recv_sems.at[a],
            device_id=(x, y, 1 - c), device_id_type=MESH) for a in range(n)]

    def start(ins, outs, sems):
        for cp in copies(ins, outs, sems):
            cp.start()

    def finish(ins, outs, sems):
        cps = copies(ins, outs, sems)
        for cp in cps:
            cp.wait_recv()
        for cp in cps:
            cp.wait_send()

    return _Comm(contribs, [jax.ShapeDtypeStruct((N_CHIPS,) + b.shape[2:], b.dtype) for b in contribs],
                 [pltpu.SemaphoreType.DMA((n,)), pltpu.SemaphoreType.DMA((n,))], [start, finish])


def _pair_sum(mine, theirs, name):
    _, _, rows, cols = mine.shape
    core = lax.axis_index("c").astype(jnp.int32).reshape(1)

    def body(c_ref, m_ref, t_ref, o_ref):
        o_ref[...] = (m_ref[...].astype(F32) + t_ref[...].astype(F32)).astype(BF16)

    return pl.pallas_call(
        body, name=name,
        grid_spec=pltpu.PrefetchScalarGridSpec(
            num_scalar_prefetch=1, grid=(N_CHIPS,),
            in_specs=[pl.BlockSpec((None, None, rows, cols), lambda q, c: (q, c[0], 0, 0)),
                      pl.BlockSpec((None, rows, cols), lambda q, c: (q, 0, 0))],
            out_specs=pl.BlockSpec((None, rows, cols), lambda q, c: (q, 0, 0))),
        out_shape=jax.ShapeDtypeStruct((N_CHIPS, rows, cols), BF16),
        compiler_params=_cp(("parallel",)),
    )(core, mine, theirs)


def _chip_comm(sums):
    n = len(sums)

    def parts(ins, outs, sems):
        send_sems, recv_sems, local_sems = sems
        x, y, c = _my_pos()
        q_me = 2 * x + y
        chips = [(1 - x, y), (x, 1 - y), (1 - x, 1 - y)]
        mine = [pltpu.make_async_copy(ins[a].at[q_me], outs[a].at[q_me], local_sems.at[a]) for a in range(n)]
        sends, recvs = [], []
        for j, (px, py) in enumerate(chips):
            for a in range(n):
                q = 2 * px + py
                sends.append(pltpu.make_async_remote_copy(
                    src_ref=ins[a].at[q], dst_ref=outs[a].at[q_me],
                    send_sem=send_sems.at[a, j], recv_sem=recv_sems.at[a, j],
                    device_id=(px, py, c), device_id_type=MESH))
                recvs.append(pltpu.make_async_remote_copy(
                    src_ref=ins[a].at[q], dst_ref=outs[a].at[q],
                    send_sem=send_sems.at[a, j], recv_sem=recv_sems.at[a, j],
                    device_id=(x, y, c), device_id_type=MESH))
        return mine, sends, recvs

    def start(ins, outs, sems):
        mine, sends, _ = parts(ins, outs, sems)
        for cp in mine + sends:
            cp.start()

    def finish(ins, outs, sems):
        mine, sends, recvs = parts(ins, outs, sems)
        for cp in recvs:
            cp.wait_recv()
        for cp in sends:
            cp.wait_send()
        for cp in mine:
            cp.wait()

    return _Comm(sums, [jax.ShapeDtypeStruct(b.shape, b.dtype) for b in sums],
                 [pltpu.SemaphoreType.DMA((n, 3)), pltpu.SemaphoreType.DMA((n, 3)), pltpu.SemaphoreType.DMA((n,))],
                 [start, finish])


def _mm_nt(a, bt, *, tm, tn, out_dtype, name, row_off=0, rows=None):
    rows = a.shape[0] if rows is None else rows
    n, k = bt.shape

    def body(a_ref, b_ref, o_ref):
        o_ref[...] = _dot_nt(a_ref[...], b_ref[...]).astype(out_dtype)

    return pl.pallas_call(
        body, name=name, grid=(rows // tm, n // tn),
        in_specs=[pl.BlockSpec((tm, k), lambda i, j: (i + row_off, 0)),
                  pl.BlockSpec((tn, k), lambda i, j: (j, 0))],
        out_specs=pl.BlockSpec((tm, tn), lambda i, j: (i, j)),
        out_shape=jax.ShapeDtypeStruct((rows, n), out_dtype),
        compiler_params=_cp(("parallel", "parallel")),
    )(a, bt)


def _mm_tn(a, b, *, tk, nk, tm, tn, out_dtype, name, a_off=0, b_off=0):
    m, n = a.shape[1], b.shape[1]

    def body(a_ref, b_ref, o_ref, acc):
        kk = pl.program_id(2)

        @pl.when(kk == 0)
        def _():
            acc[...] = jnp.zeros_like(acc)

        acc[...] += _dot_tn(a_ref[...], b_ref[...])

        @pl.when(kk == nk - 1)
        def _():
            o_ref[...] = acc[...].astype(out_dtype)

    return pl.pallas_call(
        body, name=name, grid=(m // tm, n // tn, nk),
        in_specs=[pl.BlockSpec((tk, tm), lambda i, j, kk: (kk + a_off, i)),
                  pl.BlockSpec((tk, tn), lambda i, j, kk: (kk + b_off, j))],
        out_specs=pl.BlockSpec((tm, tn), lambda i, j, kk: (i, j)),
        out_shape=jax.ShapeDtypeStruct((m, n), out_dtype),
        scratch_shapes=[pltpu.VMEM((tm, tn), F32)],
        compiler_params=_cp(("parallel", "parallel", "arbitrary")),
    )(a, b)


def _mm_tn_blocked(a, b, name, carried=None):
    nb, _, w = a.shape
    n = b.shape[1]

    def body(a_ref, b_ref, o_ref):
        o_ref[...] = _dot_tn(a_ref[...], b_ref[...]).astype(BF16)

    (out,), extra = _pcall(
        body, carried, name=name, grid=(nb,),
        in_specs=[pl.BlockSpec((None, S, w), lambda j: (j, 0, 0)), _full((S, n))],
        out_specs=[pl.BlockSpec((None, w, n), lambda j: (j, 0, 0))],
        out_shape=[jax.ShapeDtypeStruct((nb, w, n), BF16)],
        scratch_shapes=[], operands=[a, b])
    return out, extra


def _prologue(cond, c_ctx, w_ada, b_cols, w_in_t):
    rows_shape = jax.ShapeDtypeStruct((16, ADA_BLK), F32)
    big, g_cond, g_mod = _gather_comm([w_in_t]), _gather_comm([cond]), _gather_comm([rows_shape])

    def body(cond_ref, cctx_ref, wada_ref, b_ref, win_ref, g0_ref, cc_ref, g1_ref, gin_ref, rows_ref, *sems):
        s_big, s_cond, s_mod = sems[0:3], sems[3:6], sems[6:9]
        big.phases[0]([win_ref], [gin_ref], s_big)
        for phase in g_cond.phases:
            phase([cond_ref], [g0_ref], s_cond)
        cc_ref[...] = jnp.zeros_like(cc_ref)
        for j in range(N_DEV):
            cc_ref[j:j + 1, :] = g0_ref[j, 0:1, :]
        cc_ref[N_DEV:N_DEV + 1, :] = cctx_ref[...]
        cv = cc_ref[...]
        rows_ref[...] = _dot_f32(cv * _sigmoid(cv), wada_ref[...]) + b_ref[...]
        for phase in g_mod.phases:
            phase([rows_ref], [g1_ref], s_mod)
        big.phases[1]([win_ref], [gin_ref], s_big)
        big.phases[2]([win_ref], [gin_ref], s_big)

    return pl.pallas_call(
        body, name="prologue",
        in_specs=[_VMEM_WHOLE] * 4 + [_ANY], out_specs=[_VMEM_WHOLE] * 3 + [_ANY],
        out_shape=[g_cond.out_shapes[0], jax.ShapeDtypeStruct((16, D), F32), g_mod.out_shapes[0],
                   big.out_shapes[0]],
        scratch_shapes=[pltpu.VMEM((16, ADA_BLK), F32)] + big.sems + g_cond.sems + g_mod.sems,
        compiler_params=_cp(),
    )(cond, c_ctx, w_ada, b_cols, w_in_t)


def _ada_grads(cc, dm_cols, w_ada):
    def body(c_ref, dm_ref, w_ref, gw_ref, dsc_ref):
        cv = c_ref[...]
        sc = cv * _sigmoid(cv)
        dm = dm_ref[...]
        gw_ref[...] = _dot_f32(sc, dm, dot=_dot_tn)
        dsc_ref[...] = _dot_f32(dm, w_ref[...], dot=_dot_nt)

    return pl.pallas_call(
        body, name="ada_grads",
        in_specs=[_VMEM_WHOLE] * 3, out_specs=[_VMEM_WHOLE] * 2,
        out_shape=[jax.ShapeDtypeStruct((D, ADA_BLK), F32), jax.ShapeDtypeStruct((16, D), F32)],
        compiler_params=_cp(),
    )(cc, dm_cols, w_ada)


def _lat(i):
    return jnp.maximum(i - 1, 0)


def _rms_mod(xv, nw, sh, sc):
    rstd = lax.rsqrt(jnp.mean(xv * xv, axis=-1, keepdims=True) + EPS)
    return (xv * rstd * nw) * (1.0 + sc) + sh


def _rms_mod_bwd(xv, nw, sc, dh):
    rstd = lax.rsqrt(jnp.mean(xv * xv, axis=-1, keepdims=True) + EPS)
    xhat = xv * rstd
    dn = dh * (1.0 + sc)
    dxhat = dn * nw
    dx = rstd * (dxhat - xhat * jnp.mean(dxhat * xhat, axis=-1, keepdims=True))
    return (dx, jnp.sum(dh, axis=0, keepdims=True), jnp.sum(dh * (xhat * nw), axis=0, keepdims=True),
            jnp.sum(dn * xhat, axis=0, keepdims=True))


def _norm_mod_all(ctx, x, nw, sh, sc):
    def body(ctx_ref, x_ref, nw_ref, sh_ref, sc_ref, o_ref):
        i = pl.program_id(0)
        sel = jnp.minimum(i, 1)
        xv = jnp.where(i == 0, ctx_ref[...], x_ref[...])
        o_ref[...] = _rms_mod(xv, nw_ref[...], sh_ref[pl.ds(sel, 1), :], sc_ref[pl.ds(sel, 1), :]).astype(BF16)

    return pl.pallas_call(
        body, name="norm_mod", grid=(N_TILES,),
        in_specs=[_full((TM, D)), pl.BlockSpec((TM, D), lambda i: (_lat(i), 0)),
                  _full((1, D)), _full((2, D)), _full((2, D))],
        out_specs=pl.BlockSpec((TM, D), lambda i: (i, 0)),
        out_shape=jax.ShapeDtypeStruct((T, D), BF16),
        compiler_params=_cp(("parallel",)),
    )(ctx, x, nw, sh, sc)


def _chunk_masks(reverse):
    row = lax.broadcasted_iota(jnp.int32, (TM, TM), 0)
    col = lax.broadcasted_iota(jnp.int32, (TM, TM), 1)
    same = (row // CHUNK) == (col // CHUNK)
    tri = same & ((col >= row) if reverse else (col <= row))
    return same, tri


def _chunk_order(i, reverse):
    if not reverse:
        return i
    return jnp.where(i < N_CTX_CHUNKS, N_CTX_CHUNKS - 1 - i, N_CHUNKS + N_CTX_CHUNKS - 1 - i)


def _decay_terms(z, lb, same01, tri01):
    f = lb + (1.0 - lb) * _sigmoid(z)
    g = jnp.log(f)
    g2 = jnp.concatenate(_split2(g), axis=1)
    b2 = _dot(tri01, g2)
    t2 = _dot(same01, g2)
    return f, 1.0 - f, b2[:, :HG_DIM] + b2[:, HG_DIM:], t2[:, :HG_DIM] + t2[:, HG_DIM:]


def _chunk_outer(a, b):
    n = TM // CHUNK
    return jnp.einsum('ncv,nck->nvk', a.reshape(n, CHUNK, HG_DIM), b.reshape(n, CHUNK, HG_DIM),
                      preferred_element_type=F32)


def _hgrn_fwd(p_a, lbl, carried=None):
    cpt = TM // CHUNK

    def body(p_ref, lbl_ref, o_ref, st_ref, qd_s, kd_s, u_s, v_s, ebt_s):
        masks = [_chunk_masks(d == 1) for d in (0, 1)]
        same01 = jnp.where(masks[0][0], 1.0, 0.0).astype(BF16)
        tri = [m[1] for m in masks]
        tri01 = [jnp.where(t, 1.0, 0.0).astype(BF16) for t in tri]
        lb = [_sigmoid(lbl_ref[d][0:1, :] - lbl_ref[d][1:2, :]) for d in (0, 1)]

        def prep(r, carry):
            r0 = pl.multiple_of(r * TM, TM)
            vb = p_ref[pl.ds(r0, TM), 2 * HG_DIM:3 * HG_DIM].astype(BF16)
            v_s[pl.ds(r0, TM), :] = vb
            for d in (0, 1):
                z = p_ref[pl.ds(r0, TM), d * HG_DIM:(d + 1) * HG_DIM]
                _, k, b, bt = _decay_terms(z, lb[d], same01, tri01[d])
                u_s[d, pl.ds(r * cpt, cpt)] = _chunk_outer(vb, (k * jnp.exp(bt - b)).astype(BF16))
                ebt_s[d, pl.ds(r0, TM), :] = jnp.exp(bt)

                @pl.when(r >= 1)
                def _():
                    rl = pl.multiple_of(r0 - L, TM)
                    qr = p_ref[pl.ds(r0, TM), 3 * HG_DIM:4 * HG_DIM]
                    q = qr * _sigmoid(qr) * HG_DIM ** -0.5
                    qd_s[d, pl.ds(rl, TM), :] = (q * jnp.exp(b)).astype(BF16)
                    kd_s[d, pl.ds(rl, TM), :] = (k * jnp.exp(-b)).astype(BF16)

            return carry

        lax.fori_loop(0, N_TILES, prep, 0)

        def scan(i, sts):
            new = []
            for d in (0, 1):
                nn = _chunk_order(i, d == 1)
                c0 = pl.multiple_of(nn * CHUNK, CHUNK)
                st_ref[d, nn] = sts[d].astype(BF16)
                new.append(sts[d] * ebt_s[d, pl.ds(c0, 1), :] + u_s[d, nn])
            return tuple(new)

        zero = jnp.zeros((HG_DIM, HG_DIM), F32)
        lax.fori_loop(0, N_CHUNKS, scan, (zero, zero))

        def outp(r, carry):
            r0 = pl.multiple_of(r * TM, TM)
            vb = v_s[pl.ds(r0 + L, TM), :]
            o = jnp.zeros((TM, HG_DIM), F32)
            for d in (0, 1):
                qd = qd_s[d, pl.ds(r0, TM), :]
                a = jnp.where(tri[d], _dot_nt(qd, kd_s[d, pl.ds(r0, TM), :]), 0.0)
                stb = st_ref[d, pl.ds(N_CTX_CHUNKS + r * cpt, cpt)]
                inter = jnp.einsum('nck,nvk->ncv', qd.reshape(cpt, CHUNK, HG_DIM), stb,
                                   preferred_element_type=F32)
                o = o + _dot(a.astype(BF16), vb) + inter.reshape(TM, HG_DIM)
            o_ref[pl.ds(r0, TM), :] = o
            return carry

        lax.fori_loop(0, N_LAT_TILES, outp, 0)

    return _pcall(
        body, carried, name="hgrn_fwd", grid=(HG_HEADS,),
        in_specs=[pl.BlockSpec((T, 4 * HG_DIM), lambda h: (0, h)),
                  pl.BlockSpec((2, 2, HG_DIM), lambda h: (0, 0, h))],
        out_specs=[pl.BlockSpec((S, HG_DIM), lambda h: (0, h)),
                   pl.BlockSpec((2, None, N_CHUNKS, HG_DIM, HG_DIM), lambda h: (0, h, 0, 0, 0))],
        out_shape=[jax.ShapeDtypeStruct((S, HGW), F32),
                   jax.ShapeDtypeStruct((2, HG_HEADS, N_CHUNKS, HG_DIM, HG_DIM), BF16)],
        scratch_shapes=[pltpu.VMEM((2, S, HG_DIM), BF16), pltpu.VMEM((2, S, HG_DIM), BF16),
                        pltpu.VMEM((2, N_CHUNKS, HG_DIM, HG_DIM), F32), pltpu.VMEM((T, HG_DIM), BF16),
                        pltpu.VMEM((2, T, HG_DIM), F32)],
        operands=[p_a, lbl])


def _hgrn_bwd(p_a, lbl, d_o, st, carried=None):
    cpt = TM // CHUNK

    def rows(r):
        return r * TM if isinstance(r, int) else pl.multiple_of(r * TM, TM)

    def body(p_ref, lbl_ref, do_ref, st_ref, dp_ref, dlb_ref, b_s, bt_s, dbt_s, qd_s, dst_s, w_s):
        masks = [_chunk_masks(d == 1) for d in (0, 1)]
        same01 = jnp.where(masks[0][0], 1.0, 0.0).astype(BF16)
        tri = [m[1] for m in masks]
        tri01 = [jnp.where(t, 1.0, 0.0).astype(BF16) for t in tri]
        later01 = [tri01[1], tri01[0]]
        lb = [_sigmoid(lbl_ref[d][0:1, :] - lbl_ref[d][1:2, :]) for d in (0, 1)]

        def prep_tile(r, latent):
            r0 = rows(r)
            for d in (0, 1):
                z = p_ref[pl.ds(r0, TM), d * HG_DIM:(d + 1) * HG_DIM]
                _, _, b, bt = _decay_terms(z, lb[d], same01, tri01[d])
                b_s[d, pl.ds(r0, TM), :] = b
                bt_s[d, pl.ds(r0, TM), :] = bt
                if latent:
                    rl = pl.multiple_of(r0 - L, TM)
                    qr = p_ref[pl.ds(r0, TM), 3 * HG_DIM:4 * HG_DIM]
                    qd = (qr * _sigmoid(qr) * HG_DIM ** -0.5 * jnp.exp(b)).astype(BF16)
                    qd_s[d, pl.ds(rl, TM), :] = qd
                    w_s[d, pl.ds(r * cpt, cpt)] = _chunk_outer(
                        do_ref[pl.ds(rl, TM), :].astype(BF16), qd).astype(BF16)

        prep_tile(0, False)
        w_s[:, pl.ds(0, N_CTX_CHUNKS)] = jnp.zeros((2, N_CTX_CHUNKS, HG_DIM, HG_DIM), BF16)

        def prep(r, carry):
            prep_tile(r, True)
            return carry

        lax.fori_loop(1, N_TILES, prep, 0)

        def rscan(j, dsts):
            i = N_CHUNKS - 1 - j
            new = []
            for d in (0, 1):
                nn = _chunk_order(i, d == 1)
                c0 = pl.multiple_of(nn * CHUNK, CHUNK)
                dst_s[d, nn] = dsts[d].astype(BF16)
                after = st_ref[d, _chunk_order(jnp.minimum(i + 1, N_CHUNKS - 1), d == 1)].astype(F32)
                dbt_s[d, pl.ds(c0, CHUNK), :] = jnp.broadcast_to(
                    jnp.sum(after * dsts[d], axis=0, keepdims=True), (CHUNK, HG_DIM))
                new.append(dsts[d] * jnp.exp(bt_s[d, pl.ds(c0, 1), :]) + w_s[d, nn].astype(F32))
            return tuple(new)

        zero = jnp.zeros((HG_DIM, HG_DIM), F32)
        lax.fori_loop(0, N_CHUNKS, rscan, (zero, zero))

        def grad_tile(r, latent):
            r0 = rows(r)
            vb = p_ref[pl.ds(r0, TM), 2 * HG_DIM:3 * HG_DIM].astype(BF16)
            dv = jnp.zeros((TM, HG_DIM), F32)
            dq = jnp.zeros((TM, HG_DIM), F32)
            dlbs = []
            if latent:
                rl = pl.multiple_of(r0 - L, TM)
                qr = p_ref[pl.ds(r0, TM), 3 * HG_DIM:4 * HG_DIM]
                sq = _sigmoid(qr)
                do = do_ref[pl.ds(rl, TM), :].astype(BF16)
                da_full = _dot_nt(do, vb)
            for d in (0, 1):
                z = p_ref[pl.ds(r0, TM), d * HG_DIM:(d + 1) * HG_DIM]
                sz = _sigmoid(z)
                f = lb[d] + (1.0 - lb[d]) * sz
                k = 1.0 - f
                b = b_s[d, pl.ds(r0, TM), :]
                e2 = jnp.exp(bt_s[d, pl.ds(r0, TM), :] - b)
                dstb = dst_s[d, pl.ds(r * cpt, cpt)]
                kd2 = k * e2
                dkd2 = jnp.einsum('ncv,nvk->nck', vb.reshape(cpt, CHUNK, HG_DIM), dstb,
                                  preferred_element_type=F32).reshape(TM, HG_DIM)
                dv = dv + jnp.einsum('nck,nvk->ncv', kd2.astype(BF16).reshape(cpt, CHUNK, HG_DIM), dstb,
                                     preferred_element_type=F32).reshape(TM, HG_DIM)
                dk = dkd2 * e2
                db = -(kd2 * dkd2)
                if latent:
                    eb = jnp.exp(b)
                    enb = jnp.exp(-b)
                    qdf = qr * sq * HG_DIM ** -0.5 * eb
                    kdf = k * enb
                    qd = qd_s[d, pl.ds(rl, TM), :]
                    kd = kdf.astype(BF16)
                    a = jnp.where(tri[d], _dot_nt(qd, kd), 0.0).astype(BF16)
                    da = jnp.where(tri[d], da_full, 0.0).astype(BF16)
                    stb = st_ref[d, pl.ds(r * cpt, cpt)]
                    dqd = _dot(da, kd) + jnp.einsum(
                        'ncv,nvk->nck', do.reshape(cpt, CHUNK, HG_DIM), stb,
                        preferred_element_type=F32).reshape(TM, HG_DIM)
                    dkd = _dot_tn(da, qd)
                    dv = dv + _dot_tn(a, do)
                    dk = dk + dkd * enb
                    db = db + qdf * dqd - kdf * dkd
                    dq = dq + dqd * eb
                dg = _dot_lhs01(later01[d], db) + dbt_s[d, pl.ds(r0, TM), :]
                df = dg / f - dk
                dp_ref[pl.ds(r0, TM), d * HG_DIM:(d + 1) * HG_DIM] = (
                    df * (1.0 - lb[d]) * sz * (1.0 - sz)).astype(BF16)
                dlbs.append(jnp.sum(df * (1.0 - sz), axis=0, keepdims=True))
            dp_ref[pl.ds(r0, TM), 2 * HG_DIM:3 * HG_DIM] = dv.astype(BF16)
            if latent:
                dq = dq * (HG_DIM ** -0.5) * (sq * (1.0 + qr * (1.0 - sq)))
            dp_ref[pl.ds(r0, TM), 3 * HG_DIM:4 * HG_DIM] = dq.astype(BF16)
            return dlbs

        dlb_ctx = grad_tile(0, False)

        def grads(r, acc):
            t = grad_tile(r, True)
            return (acc[0] + t[0], acc[1] + t[1])

        dlb = lax.fori_loop(1, N_TILES, grads, (dlb_ctx[0], dlb_ctx[1]))
        dlb_ref[0:1, :] = dlb[0]
        dlb_ref[1:2, :] = dlb[1]

    return _pcall(
        body, carried, name="hgrn_bwd", grid=(HG_HEADS,),
        in_specs=[pl.BlockSpec((T, 4 * HG_DIM), lambda h: (0, h)),
                  pl.BlockSpec((2, 2, HG_DIM), lambda h: (0, 0, h)),
                  pl.BlockSpec((S, HG_DIM), lambda h: (0, h)),
                  pl.BlockSpec((2, None, N_CHUNKS, HG_DIM, HG_DIM), lambda h: (0, h, 0, 0, 0))],
        out_specs=[pl.BlockSpec((T, 4 * HG_DIM), lambda h: (0, h)),
                   pl.BlockSpec((2, HG_DIM), lambda h: (0, h))],
        out_shape=[jax.ShapeDtypeStruct((T, WA), BF16), jax.ShapeDtypeStruct((2, HGW), F32)],
        scratch_shapes=[pltpu.VMEM((2, T, HG_DIM), F32), pltpu.VMEM((2, T, HG_DIM), F32),
                        pltpu.VMEM((2, T, HG_DIM), F32), pltpu.VMEM((2, S, HG_DIM), BF16),
                        pltpu.VMEM((2, N_CHUNKS, HG_DIM, HG_DIM), BF16),
                        pltpu.VMEM((2, N_CHUNKS, HG_DIM, HG_DIM), BF16)],
        operands=[p_a, lbl, d_o, st])


def _rope_tables():
    t = np.arange(S)
    inv = ROPE_THETA ** (-np.arange(0, 32, 2, dtype=np.float64) / 32)
    lane = np.arange(64)
    pos = np.where(lane[None, :] < 32, (t // GRID_W)[:, None], (t % GRID_W)[:, None]).astype(np.float64)
    ang = pos * inv[(lane % 32) % 16][None, :]
    sign = np.where((lane % 32) < 16, -1.0, 1.0)[None, :]
    cos = np.tile(np.cos(ang), (1, 2)).astype(np.float32)
    sin = np.tile(np.sin(ang) * sign, (1, 2)).astype(np.float32)
    return jnp.asarray(cos), jnp.asarray(sin)


def _rope_partner(v):
    lane = lax.broadcasted_iota(jnp.int32, (1, 128), 1)
    first = (lane % 32) < 16
    slabs = []
    for j in range(v.shape[1] // 128):
        s = v[:, 128 * j:128 * (j + 1)]
        slabs.append(jnp.where(first, pltpu.roll(s, 112, 1), pltpu.roll(s, 16, 1)))
    return slabs[0] if len(slabs) == 1 else jnp.concatenate(slabs, axis=1)


def _group_ones(width, group):
    r = lax.broadcasted_iota(jnp.int32, (width, width), 0)
    c = lax.broadcasted_iota(jnp.int32, (width, width), 1)
    return jnp.where((r // group) == (c // group), 1.0, 0.0).astype(BF16)


def _group_mean(v, ones01, group):
    hi = v.astype(BF16)
    lo = (v - hi.astype(F32)).astype(BF16)
    return (_dot(hi, ones01) + _dot(lo, ones01)) * (1.0 / group)


def _rep_matrix():
    r = lax.broadcasted_iota(jnp.int32, (KVW, ATW), 0)
    c = lax.broadcasted_iota(jnp.int32, (KVW, ATW), 1)
    return jnp.where(r == HEAD_DIM * (c // 256) + c % HEAD_DIM, 1.0, 0.0).astype(BF16)


def _tile_lanes(v, reps):
    return jnp.concatenate([v] * reps, axis=1)


def _prep_fwd(p_b, o, cos, sin, hnw, qnw, knw):
    def body(p_ref, o_ref, cos_ref, sin_ref, hnw_ref, qnw_ref, knw_ref, y_ref, q_ref, k_ref, v_ref):
        i = pl.program_id(0)
        rep = _rep_matrix()
        ones_k = _group_ones(KVW, HEAD_DIM)
        kr = p_ref[:, 1024:1152]
        krstd = lax.rsqrt(_group_mean(kr * kr, ones_k, HEAD_DIM) + EPS)
        kn = kr * krstd * knw_ref[...]
        v_ref[...] = _dot(p_ref[:, 1152:1280].astype(BF16), rep).astype(BF16)

        @pl.when(i == 0)
        def _():
            k_ref[...] = _dot(kn.astype(BF16), rep).astype(BF16)

        @pl.when(i > 0)
        def _():
            cs, sn = cos_ref[...], sin_ref[...]
            kro = kn * cs + _rope_partner(kn) * sn
            k_ref[...] = _dot(kro.astype(BF16), rep).astype(BF16)
            qr = p_ref[:, 512:1024]
            qrstd = lax.rsqrt(_group_mean(qr * qr, _group_ones(ATW, HEAD_DIM), HEAD_DIM) + EPS)
            qn = qr * qrstd * qnw_ref[...]
            qro = qn * _tile_lanes(cs, 4) + _rope_partner(qn) * _tile_lanes(sn, 4)
            q_ref[...] = (qro * HEAD_DIM ** -0.5).astype(BF16)
            ys = []
            for h in range(HG_HEADS):
                oh = o_ref[:, HG_DIM * h:HG_DIM * (h + 1)]
                gh = p_ref[:, HG_DIM * h:HG_DIM * (h + 1)]
                rstd = lax.rsqrt(jnp.mean(oh * oh, axis=-1, keepdims=True) + EPS)
                ys.append(oh * rstd * hnw_ref[...] * (gh * _sigmoid(gh)))
            y_ref[...] = jnp.concatenate(ys, axis=1).astype(BF16)

    return pl.pallas_call(
        body, name="prep_fwd", grid=(N_TILES,),
        in_specs=[pl.BlockSpec((TM, WB), lambda i: (i, 0)),
                  pl.BlockSpec((TM, HGW), lambda i: (_lat(i), 0)),
                  pl.BlockSpec((TM, 128), lambda i: (_lat(i), 0)),
                  pl.BlockSpec((TM, 128), lambda i: (_lat(i), 0)),
                  _full((1, HG_DIM)), _full((1, ATW)), _full((1, KVW))],
        out_specs=[pl.BlockSpec((TM, HGW), lambda i: (_lat(i), 0)),
                   pl.BlockSpec((TM, ATW), lambda i: (_lat(i), 0)),
                   pl.BlockSpec((TM, ATW), lambda i: (i, 0)),
                   pl.BlockSpec((TM, ATW), lambda i: (i, 0))],
        out_shape=[jax.ShapeDtypeStruct((S, HGW), BF16), jax.ShapeDtypeStruct((S, ATW), BF16),
                   jax.ShapeDtypeStruct((T, ATW), BF16), jax.ShapeDtypeStruct((T, ATW), BF16)],
        compiler_params=_cp(("arbitrary",)),
    )(p_b, o, cos, sin, hnw, qnw, knw)


def _prep_bwd(p_b, o, cos, sin, hnw, qnw, knw, dy_hg, dq, dk_rep, dv_rep, carried=None):
    def body(p_ref, o_ref, cos_ref, sin_ref, hnw_ref, qnw_ref, knw_ref, dy_ref, dq_ref, dk_ref, dv_ref,
             dp_ref, do_ref, acc_ref):
        i = pl.program_id(0)

        @pl.when(i == 0)
        def _():
            acc_ref[...] = jnp.zeros_like(acc_ref)

        rep = _rep_matrix()
        ones_k = _group_ones(KVW, HEAD_DIM)

        def fold(v):
            hi = v.astype(BF16)
            lo = (v - hi.astype(F32)).astype(BF16)
            return _dot_nt(hi, rep) + _dot_nt(lo, rep)

        kr = p_ref[:, 1024:1152]
        krstd = lax.rsqrt(_group_mean(kr * kr, ones_k, HEAD_DIM) + EPS)
        khat = kr * krstd
        kw = knw_ref[...]
        dkro = fold(dk_ref[...])
        dv = fold(dv_ref[...])

        def k_back(dkn):
            dkhat = dkn * kw
            dkr = krstd * (dkhat - khat * _group_mean(dkhat * khat, ones_k, HEAD_DIM))
            acc_ref[2:3, 0:KVW] += jnp.sum(dkn * khat, axis=0, keepdims=True)
            dp_ref[:, 1024:1152] = dkr.astype(BF16)
            dp_ref[:, 1152:1280] = dv.astype(BF16)

        @pl.when(i == 0)
        def _():
            k_back(dkro)
            dp_ref[:, 0:1024] = jnp.zeros((TM, 1024), BF16)

        @pl.when(i > 0)
        def _():
            cs, sn = cos_ref[...], sin_ref[...]
            k_back(dkro * cs + _rope_partner(dkro * sn))
            ones_q = _group_ones(ATW, HEAD_DIM)
            qr = p_ref[:, 512:1024]
            qrstd = lax.rsqrt(_group_mean(qr * qr, ones_q, HEAD_DIM) + EPS)
            qhat = qr * qrstd
            dqro = dq_ref[...] * HEAD_DIM ** -0.5
            dqn = dqro * _tile_lanes(cs, 4) + _rope_partner(dqro * _tile_lanes(sn, 4))
            dqhat = dqn * qnw_ref[...]
            dqr = qrstd * (dqhat - qhat * _group_mean(dqhat * qhat, ones_q, HEAD_DIM))
            acc_ref[1:2, :] += jnp.sum(dqn * qhat, axis=0, keepdims=True)
            dp_ref[:, 512:1024] = dqr.astype(BF16)
            dws = jnp.zeros((1, HG_DIM), F32)
            for h in range(HG_HEADS):
                sl = slice(HG_DIM * h, HG_DIM * (h + 1))
                oh, gh, dy = o_ref[:, sl], p_ref[:, sl], dy_ref[:, sl]
                rstd = lax.rsqrt(jnp.mean(oh * oh, axis=-1, keepdims=True) + EPS)
                ohat = oh * rstd
                sg = _sigmoid(gh)
                dp_ref[:, sl] = (dy * (ohat * hnw_ref[...]) * (sg * (1.0 + gh * (1.0 - sg)))).astype(BF16)
                dn = dy * (gh * sg)
                dws = dws + jnp.sum(dn * ohat, axis=0, keepdims=True)
                dohat = dn * hnw_ref[...]
                do_ref[:, sl] = rstd * (dohat - ohat * jnp.mean(dohat * ohat, axis=-1, keepdims=True))
            acc_ref[0:1, 0:HG_DIM] += dws

    return _pcall(
        body, carried, name="prep_bwd", grid=(N_TILES,),
        in_specs=[pl.BlockSpec((TM, WB), lambda i: (i, 0)),
                  pl.BlockSpec((TM, HGW), lambda i: (_lat(i), 0)),
                  pl.BlockSpec((TM, 128), lambda i: (_lat(i), 0)),
                  pl.BlockSpec((TM, 128), lambda i: (_lat(i), 0)),
                  _full((1, HG_DIM)), _full((1, ATW)), _full((1, KVW)),
                  pl.BlockSpec((TM, HGW), lambda i: (_lat(i), 0)),
                  pl.BlockSpec((TM, ATW), lambda i: (_lat(i), 0)),
                  pl.BlockSpec((TM, ATW), lambda i: (i, 0)),
                  pl.BlockSpec((TM, ATW), lambda i: (i, 0))],
        out_specs=[pl.BlockSpec((TM, WB), lambda i: (i, 0)),
                   pl.BlockSpec((TM, HGW), lambda i: (_lat(i), 0)),
                   _full((8, ATW))],
        out_shape=[jax.ShapeDtypeStruct((T, WB), BF16), jax.ShapeDtypeStruct((S, HGW), F32),
                   jax.ShapeDtypeStruct((8, ATW), F32)],
        scratch_shapes=[], operands=[p_b, o, cos, sin, hnw, qnw, knw, dy_hg, dq, dk_rep, dv_rep])


NEG = -1e30
_CTX_BLOCKS = L // BLOCK


def _attn_window_specs():
    prev = pl.BlockSpec((BLOCK, ATW), lambda i: (jnp.maximum(i - 1, 0) + _CTX_BLOCKS, 0))
    own = pl.BlockSpec((BLOCK, ATW), lambda i: (i + _CTX_BLOCKS, 0))
    nxt = pl.BlockSpec((BLOCK, ATW), lambda i: (jnp.minimum(i + 1, N_BLOCKS - 1) + _CTX_BLOCKS, 0))
    return [prev, own, nxt, _full((L, ATW))]


def _attn_valid(i, heads=4):
    qi = lax.broadcasted_iota(jnp.int32, (heads * BLOCK, 3 * BLOCK), 0) % BLOCK
    kj = lax.broadcasted_iota(jnp.int32, (heads * BLOCK, 3 * BLOCK), 1)
    return ((jnp.abs(kj - BLOCK - qi) <= BLOCK) & ((kj >= BLOCK) | (i > 0))
            & ((kj < 2 * BLOCK) | (i < N_BLOCKS - 1)))


def _stack_heads(qg):
    lane = lax.broadcasted_iota(jnp.int32, (1, 256), 1) // HEAD_DIM
    return jnp.concatenate([jnp.where(lane == g, qg, jnp.zeros_like(qg)) for g in range(4)], axis=0)


def _unstack_heads(v4):
    lane = lax.broadcasted_iota(jnp.int32, (1, 256), 1) // HEAD_DIM
    out = jnp.where(lane == 0, v4[0:BLOCK], 0.0)
    for g in range(1, 4):
        out = out + jnp.where(lane == g, v4[g * BLOCK:(g + 1) * BLOCK], 0.0)
    return out


def _sink_rows(sink_ref, hk):
    return jnp.concatenate(
        [jnp.broadcast_to(sink_ref[0:1, 4 * hk + g:4 * hk + g + 1], (BLOCK, 1)) for g in range(4)], axis=0)


def _attn_fwd(q, k_rep, v_rep, sinks, carried=None):
    def body(q_ref, kp, ko, kn, kc, vp, vo, vn, vc, sink_ref, y_ref, lse_ref):
        i = pl.program_id(0)
        valid = _attn_valid(i, 1)
        lane8 = lax.broadcasted_iota(jnp.int32, (1, ATT_HEADS), 1)
        head_of_lane = lax.broadcasted_iota(jnp.int32, (1, 256), 1) // HEAD_DIM
        lse_out = jnp.zeros((BLOCK, ATT_HEADS), F32)
        for hk in range(KV_HEADS):
            sl = slice(256 * hk, 256 * (hk + 1))
            qg = q_ref[:, sl]
            kl = jnp.concatenate([kp[:, sl], ko[:, sl], kn[:, sl]], axis=0)
            vl = jnp.concatenate([vp[:, sl], vo[:, sl], vn[:, sl]], axis=0)
            yg = jnp.zeros((BLOCK, 256), F32)
            for g in range(4):
                q1 = jnp.where(head_of_lane == g, qg, jnp.zeros_like(qg))
                s_loc = jnp.where(valid, _dot_nt(q1, kl), NEG)
                s_ctx = _dot_nt(q1, kc[:, sl])
                sink = sink_ref[0:1, 4 * hk + g:4 * hk + g + 1]
                m = jnp.maximum(jnp.maximum(jnp.max(s_loc, axis=1, keepdims=True),
                                            jnp.max(s_ctx, axis=1, keepdims=True)), sink)
                p_loc = jnp.exp(s_loc - m)
                p_ctx = jnp.exp(s_ctx - m)
                den = (jnp.sum(p_loc, axis=1, keepdims=True) + jnp.sum(p_ctx, axis=1, keepdims=True)
                       + jnp.exp(sink - m))
                o1 = (_dot(p_loc.astype(BF16), vl) + _dot(p_ctx.astype(BF16), vc[:, sl])) * (1.0 / den)
                yg = yg + jnp.where(head_of_lane == g, o1, 0.0)
                lse_out = lse_out + jnp.where(lane8 == 4 * hk + g, m + jnp.log(den), 0.0)
            y_ref[:, sl] = yg.astype(BF16)
        lse_ref[...] = lse_out

    return _pcall(
        body, carried, name="attn_fwd", grid=(N_BLOCKS,),
        in_specs=[pl.BlockSpec((BLOCK, ATW), lambda i: (i, 0))] + _attn_window_specs()
        + _attn_window_specs() + [_full((1, ATT_HEADS))],
        out_specs=[pl.BlockSpec((BLOCK, ATW), lambda i: (i, 0)),
                   pl.BlockSpec((BLOCK, ATT_HEADS), lambda i: (i, 0))],
        out_shape=[jax.ShapeDtypeStruct((S, ATW), BF16), jax.ShapeDtypeStruct((S, ATT_HEADS), F32)],
        scratch_shapes=[],
        operands=[q, k_rep, k_rep, k_rep, k_rep, v_rep, v_rep, v_rep, v_rep, sinks])


def _attn_bwd(q, k_rep, v_rep, sinks, y_at, lse, dy, carried=None):
    def body(q_ref, kp, ko, kn, kc, vp, vo, vn, vc, sink_ref, y_ref, lse_ref, dy_ref,
             dq_ref, dk_ref, dv_ref, dsink_ref, dk_acc, dv_acc):
        i = pl.program_id(0)

        @pl.when(i == 0)
        def _():
            dk_acc[...] = jnp.zeros_like(dk_acc)
            dv_acc[...] = jnp.zeros_like(dv_acc)
            dk_ref[pl.ds(0, L), :] = jnp.zeros((L, ATW), F32)
            dv_ref[pl.ds(0, L), :] = jnp.zeros((L, ATW), F32)
            dsink_ref[...] = jnp.zeros_like(dsink_ref)

        valid = _attn_valid(i)
        lane8 = lax.broadcasted_iota(jnp.int32, (1, ATT_HEADS), 1)
        w0 = pl.multiple_of(i * BLOCK, BLOCK)
        dsink = jnp.zeros((1, ATT_HEADS), F32)
        for hk in range(KV_HEADS):
            sl = slice(256 * hk, 256 * (hk + 1))
            q4 = _stack_heads(q_ref[:, sl])
            do4f = _stack_heads(dy_ref[:, sl])
            o4 = _stack_heads(y_ref[:, sl]).astype(F32)
            do4 = do4f.astype(BF16)
            kl = jnp.concatenate([kp[:, sl], ko[:, sl], kn[:, sl]], axis=0)
            vl = jnp.concatenate([vp[:, sl], vo[:, sl], vn[:, sl]], axis=0)
            lse4 = jnp.concatenate(
                [jnp.sum(jnp.where(lane8 == 4 * hk + g, lse_ref[...], 0.0), axis=1, keepdims=True)
                 for g in range(4)], axis=0)
            p_loc = jnp.where(valid, jnp.exp(_dot_nt(q4, kl) - lse4), 0.0)
            p_ctx = jnp.exp(_dot_nt(q4, kc[:, sl]) - lse4)
            delta = jnp.sum(do4f * o4, axis=1, keepdims=True)
            ds_loc = (p_loc * (_dot_nt(do4, vl) - delta)).astype(BF16)
            ds_ctx = (p_ctx * (_dot_nt(do4, vc[:, sl]) - delta)).astype(BF16)
            dq_ref[:, sl] = _unstack_heads(_dot(ds_loc, kl) + _dot(ds_ctx, kc[:, sl]))
            dk_acc[pl.ds(w0, 3 * BLOCK), sl] += _dot_tn(ds_loc, q4)
            dv_acc[pl.ds(w0, 3 * BLOCK), sl] += _dot_tn(p_loc.astype(BF16), do4)
            dk_ref[pl.ds(0, L), sl] += _dot_tn(ds_ctx, q4)
            dv_ref[pl.ds(0, L), sl] += _dot_tn(p_ctx.astype(BF16), do4)
            p_sink = jnp.exp(_sink_rows(sink_ref, hk) - lse4)
            for g in range(4):
                rows = slice(g * BLOCK, (g + 1) * BLOCK)
                dsink = dsink + jnp.where(lane8 == 4 * hk + g,
                                          -jnp.sum(p_sink[rows] * delta[rows], axis=0, keepdims=True), 0.0)
        dsink_ref[...] += dsink

        @pl.when(i == N_BLOCKS - 1)
        def _():
            dk_ref[pl.ds(L, S), :] = dk_acc[pl.ds(BLOCK, S), :]
            dv_ref[pl.ds(L, S), :] = dv_acc[pl.ds(BLOCK, S), :]

    row_q = pl.BlockSpec((BLOCK, ATW), lambda i: (i, 0))
    return _pcall(
        body, carried, name="attn_bwd", grid=(N_BLOCKS,),
        in_specs=[row_q] + _attn_window_specs() + _attn_window_specs()
        + [_full((1, ATT_HEADS)), row_q, pl.BlockSpec((BLOCK, ATT_HEADS), lambda i: (i, 0)), row_q],
        out_specs=[row_q, _full((T, ATW)), _full((T, ATW)), _full((1, ATT_HEADS))],
        out_shape=[jax.ShapeDtypeStruct((S, ATW), F32), jax.ShapeDtypeStruct((T, ATW), F32),
                   jax.ShapeDtypeStruct((T, ATW), F32), jax.ShapeDtypeStruct((1, ATT_HEADS), F32)],
        scratch_shapes=[pltpu.VMEM((S + 2 * BLOCK, ATW), F32), pltpu.VMEM((S + 2 * BLOCK, ATW), F32)],
        operands=[q, k_rep, k_rep, k_rep, k_rep, v_rep, v_rep, v_rep, v_rep, sinks, y_at, lse, dy])


def _merge_fwd(y_hg, y_at, p_c, x, w_bh, w_ba, w_out, g1, nfw, sh2, sc2, carried=None):
    def body(yh_ref, ya_ref, g_ref, x_ref, wbh_ref, wba_ref, wo_ref, g1_ref, nfw_ref, sh_ref, sc_ref,
             a_ref, b_ref, mx_ref, r_ref, x1_ref, h2_ref):
        a = _dot_nt(yh_ref[...], wbh_ref[...])
        b = _dot_nt(ya_ref[...], wba_ref[...])
        mixed = (_sigmoid(g_ref[:, :D]) * a + _sigmoid(g_ref[:, D:]) * b).astype(BF16)
        r = _dot(mixed, wo_ref[...])
        x1 = x_ref[...] + g1_ref[...] * r
        a_ref[...] = a
        b_ref[...] = b
        mx_ref[...] = mixed
        r_ref[...] = r
        x1_ref[...] = x1
        h2_ref[...] = _rms_mod(x1, nfw_ref[...], sh_ref[...], sc_ref[...]).astype(BF16)

    row = lambda w: pl.BlockSpec((TM, w), lambda i: (i, 0))
    vec = _full((1, D))
    return _pcall(
        body, carried, name="merge_fwd", grid=(N_LAT_TILES,),
        in_specs=[row(HGW), row(ATW), row(WC), row(D), _VMEM_WHOLE, _VMEM_WHOLE, _VMEM_WHOLE,
                  vec, vec, vec, vec],
        out_specs=[row(D)] * 6,
        out_shape=[jax.ShapeDtypeStruct((S, D), dt) for dt in (F32, F32, BF16, F32, F32, BF16)],
        scratch_shapes=[], operands=[y_hg, y_at, p_c, x, w_bh, w_ba, w_out, g1, nfw, sh2, sc2])


def _merge_bwd(dx1, r, a, b, p_c, w_bh, w_ba, w_out, g1, carried=None):
    def body(dx_ref, r_ref, a_ref, b_ref, g_ref, wbh_ref, wba_ref, wo_ref, g1_ref,
             dr_ref, da_ref, db_ref, dg_ref, dyh_ref, dya_ref, acc_ref):
        @pl.when(pl.program_id(0) == 0)
        def _():
            acc_ref[...] = jnp.zeros_like(acc_ref)

        dx1v = dx_ref[...]
        acc_ref[0:1, :] += jnp.sum(dx1v * r_ref[...], axis=0, keepdims=True)
        dr = (g1_ref[...] * dx1v).astype(BF16)
        dr_ref[...] = dr
        dmix = _dot_nt(dr, wo_ref[...])
        sh, sa = _sigmoid(g_ref[:, :D]), _sigmoid(g_ref[:, D:])
        da = (dmix * sh).astype(BF16)
        db = (dmix * sa).astype(BF16)
        da_ref[...] = da
        db_ref[...] = db
        dg_ref[:, :D] = (dmix * a_ref[...] * sh * (1.0 - sh)).astype(BF16)
        dg_ref[:, D:] = (dmix * b_ref[...] * sa * (1.0 - sa)).astype(BF16)
        dyh_ref[...] = _dot(da, wbh_ref[...])
        dya_ref[...] = _dot(db, wba_ref[...])

    row = lambda w: pl.BlockSpec((TM, w), lambda i: (i, 0))
    return _pcall(
        body, carried, name="merge_bwd", grid=(N_LAT_TILES,),
        in_specs=[row(D), row(D), row(D), row(D), row(WC), _VMEM_WHOLE, _VMEM_WHOLE, _VMEM_WHOLE,
                  _full((1, D))],
        out_specs=[row(D), row(D), row(D), row(WC), row(HGW), row(ATW), _full((8, D))],
        out_shape=[jax.ShapeDtypeStruct((S, D), BF16), jax.ShapeDtypeStruct((S, D), BF16),
                   jax.ShapeDtypeStruct((S, D), BF16), jax.ShapeDtypeStruct((S, WC), BF16),
                   jax.ShapeDtypeStruct((S, HGW), F32), jax.ShapeDtypeStruct((S, ATW), F32),
                   jax.ShapeDtypeStruct((8, D), F32)],
        scratch_shapes=[], operands=[dx1, r, a, b, p_c, w_bh, w_ba, w_out, g1])


def _ffn_fused(x1, h2, tgt, w_gate, w_up, w_down, g2, nfw, sc2):
    def body(x1_ref, h2_ref, t_ref, wg_ref, wu_ref, wd_ref, g2_ref, nfw_ref, sc_ref,
             act_ref, dgt_ref, dup_ref, df_ref, dx_ref, acc_ref, gs, us):
        @pl.when(pl.program_id(0) == 0)
        def _():
            acc_ref[...] = jnp.zeros_like(acc_ref)

        h2 = h2_ref[...]
        f = jnp.zeros((TM, D), F32)
        for j in range(N_FF_TILES):
            g = _dot_nt(h2, wg_ref[j])
            u = _dot_nt(h2, wu_ref[j])
            gs[j] = g
            us[j] = u
            act = (g * _sigmoid(g) * u).astype(BF16)
            act_ref[j] = act
            f = f + _dot(act, wd_ref[j])
        x1v = x1_ref[...]
        g2 = g2_ref[...]
        diff = x1v + g2 * f - t_ref[...]
        dy = diff * (1.0 / D)
        df = (g2 * dy).astype(BF16)
        df_ref[...] = df
        dh2 = jnp.zeros((TM, D), F32)
        for j in range(N_FF_TILES):
            g, u = gs[j], us[j]
            sg = _sigmoid(g)
            dact = _dot_nt(df, wd_ref[j])
            dgate = (dact * u * (sg * (1.0 + g * (1.0 - sg)))).astype(BF16)
            dup = (dact * (g * sg)).astype(BF16)
            dgt_ref[j] = dgate
            dup_ref[j] = dup
            dh2 = dh2 + _dot(dgate, wg_ref[j]) + _dot(dup, wu_ref[j])
        dx, dsh, dsc, dnw = _rms_mod_bwd(x1v, nfw_ref[...], sc_ref[...], dh2)
        dx_ref[...] = dy + dx
        acc_ref[0:1, :] += dsh
        acc_ref[1:2, :] += dsc
        acc_ref[2:3, :] += dnw
        acc_ref[3:4, :] += jnp.sum(dy * f, axis=0, keepdims=True)
        acc_ref[4:5, :] += 0.5 * jnp.sum(jnp.sum(diff * diff, axis=1, keepdims=True), axis=0,
                                         keepdims=True) * (1.0 / D)

    row = lambda dt_w: pl.BlockSpec((TM, dt_w), lambda i: (i, 0))
    blk = pl.BlockSpec((N_FF_TILES, TM, FF_TILE), lambda i: (0, i, 0))
    vec = _full((1, D))
    return pl.pallas_call(
        body, name="ffn_fused", grid=(N_LAT_TILES,),
        in_specs=[row(D), row(D), row(D), _VMEM_WHOLE, _VMEM_WHOLE, _VMEM_WHOLE, vec, vec, vec],
        out_specs=[blk, blk, blk, row(D), row(D), _full((8, D))],
        out_shape=[jax.ShapeDtypeStruct((N_FF_TILES, S, FF_TILE), BF16)] * 3
        + [jax.ShapeDtypeStruct((S, D), BF16), jax.ShapeDtypeStruct((S, D), F32),
           jax.ShapeDtypeStruct((8, D), F32)],
        scratch_shapes=[pltpu.VMEM((N_FF_TILES, TM, FF_TILE), F32), pltpu.VMEM((N_FF_TILES, TM, FF_TILE), F32)],
        compiler_params=_cp(("arbitrary",)),
    )(x1, h2, tgt, w_gate, w_up, w_down, g2, nfw, sc2)


def _proj_bc(h_all, w_b, w_c, carried=None):
    def body(h_ref, wb_ref, wc_ref, pb_ref, pc_ref):
        h = h_ref[...]
        pb_ref[...] = _dot_nt(h, wb_ref[...])

        @pl.when(pl.program_id(0) > 0)
        def _():
            pc_ref[...] = _dot_nt(h, wc_ref[...])

    return _pcall(
        body, carried, name="proj_bc", grid=(N_TILES,),
        in_specs=[pl.BlockSpec((TM, D), lambda i: (i, 0)), _VMEM_WHOLE, _VMEM_WHOLE],
        out_specs=[pl.BlockSpec((TM, WB), lambda i: (i, 0)), pl.BlockSpec((TM, WC), lambda i: (_lat(i), 0))],
        out_shape=[jax.ShapeDtypeStruct((T, WB), F32), jax.ShapeDtypeStruct((S, WC), F32)],
        scratch_shapes=[], operands=[h_all, w_b, w_c])


def _input_bwd(dp_a, dp_b, dp_c, w_a, w_b, w_c, ctx, x, dx1, nw, sh, sc, carried=None):
    def body(da_ref, db_ref, dc_ref, wa_ref, wb_ref, wc_ref, ctx_ref, x_ref, dx1_ref, nw_ref, sh_ref,
             sc_ref, gx_ref, acc_ref):
        i = pl.program_id(0)

        @pl.when(i == 0)
        def _():
            acc_ref[...] = jnp.zeros_like(acc_ref)

        dh = _dot(da_ref[...], wa_ref[...]) + _dot(db_ref[...], wb_ref[...])

        @pl.when(i == 0)
        def _():
            _, dsh, dsc, dnw = _rms_mod_bwd(ctx_ref[...], nw_ref[...], sc_ref[0:1, :], dh)
            acc_ref[3:4, :] += dsh
            acc_ref[4:5, :] += dsc
            acc_ref[2:3, :] += dnw

        @pl.when(i > 0)
        def _():
            dhl = dh + _dot(dc_ref[...], wc_ref[...])
            dx, dsh, dsc, dnw = _rms_mod_bwd(x_ref[...], nw_ref[...], sc_ref[1:2, :], dhl)
            gx_ref[...] = dx1_ref[...] + dx
            acc_ref[0:1, :] += dsh
            acc_ref[1:2, :] += dsc
            acc_ref[2:3, :] += dnw

    lat = lambda w: pl.BlockSpec((TM, w), lambda i: (_lat(i), 0))
    return _pcall(
        body, carried, name="input_bwd", grid=(N_TILES,),
        in_specs=[pl.BlockSpec((TM, WA), lambda i: (i, 0)), pl.BlockSpec((TM, WB), lambda i: (i, 0)),
                  lat(WC), _VMEM_WHOLE, _VMEM_WHOLE, _VMEM_WHOLE, _full((TM, D)), lat(D), lat(D),
                  _full((1, D)), _full((2, D)), _full((2, D))],
        out_specs=[lat(D), _full((8, D))],
        out_shape=[jax.ShapeDtypeStruct((S, D), F32), jax.ShapeDtypeStruct((8, D), F32)],
        scratch_shapes=[], operands=[dp_a, dp_b, dp_c, w_a, w_b, w_c, ctx, x, dx1, nw, sh, sc])


_C1 = 1.0 - ADAM_B1 ** ADAM_STEP
_C2 = 1.0 - ADAM_B2 ** ADAM_STEP


def _adamw_math(w, g, m, v):
    m = ADAM_B1 * m + (1.0 - ADAM_B1) * g
    v = ADAM_B2 * v + (1.0 - ADAM_B2) * (g * g)
    m_hat = m / _C1
    v_hat = v / _C2
    delta = -ADAM_LR * (m_hat / (jnp.sqrt(v_hat) + ADAM_EPS) + ADAM_WD * w)
    return delta, m, v


def _adamw_sharded(terms, w, m, v, name, tr):
    rows, cols = w.shape

    def body(t_ref, w_ref, m_ref, v_ref, g_ref, d_ref, nm_ref, nv_ref):
        g = t_ref[0].astype(F32)
        for s in range(1, N_CHIPS):
            g = g + t_ref[s].astype(F32)
        g_ref[...] = g
        d_ref[...], nm_ref[...], nv_ref[...] = _adamw_math(w_ref[...], g, m_ref[...], v_ref[...])

    blk = pl.BlockSpec((tr, cols), lambda i: (i, 0))
    return pl.pallas_call(
        body, name=name, grid=(rows // tr,),
        in_specs=[pl.BlockSpec((N_CHIPS, tr, cols), lambda i: (0, i, 0)), blk, blk, blk],
        out_specs=[blk] * 4,
        out_shape=[jax.ShapeDtypeStruct((rows, cols), F32)] * 4,
        compiler_params=_cp(("parallel",)),
    )(terms, w, m, v)


def _adamw_plain(g, w, m, v, name):
    def body(g_ref, w_ref, m_ref, v_ref, d_ref, nm_ref, nv_ref):
        d_ref[...], nm_ref[...], nv_ref[...] = _adamw_math(w_ref[...], g_ref[...], m_ref[...], v_ref[...])

    return pl.pallas_call(
        body, name=name, in_specs=[_VMEM_WHOLE] * 4, out_specs=[_VMEM_WHOLE] * 3,
        out_shape=[jax.ShapeDtypeStruct(w.shape, F32)] * 3,
        compiler_params=_cp(),
    )(g, w, m, v)


SMALL_ROWS = 16
R_DMOD, R_DCTX, R_NMIX, R_NFFN, R_MISC, R_DLB, R_BADA01 = 0, 6, 8, 9, 10, 11, 13
M_HNW, M_QNW, M_KNW, M_SINK, M_LOSS = 0, 128, 256, 384, 512


def _pack_small(acc_in, acc_mg, acc_ffn, acc_prep, dsink, dlb):
    def body(in_ref, mg_ref, ff_ref, pp_ref, ds_ref, dlb_ref, o_ref):
        o_ref[...] = jnp.zeros_like(o_ref)
        o_ref[0:2, :] = in_ref[0:2, :]
        o_ref[2:3, :] = mg_ref[0:1, :]
        o_ref[3:5, :] = ff_ref[0:2, :]
        o_ref[5:6, :] = ff_ref[3:4, :]
        o_ref[6:8, :] = in_ref[3:5, :]
        o_ref[8:9, :] = in_ref[2:3, :]
        o_ref[9:10, :] = ff_ref[2:3, :]
        o_ref[10:11, M_HNW:M_HNW + HG_DIM] = pp_ref[0:1, 0:HG_DIM]
        r = lax.broadcasted_iota(jnp.int32, (ATW, 128), 0)
        c = lax.broadcasted_iota(jnp.int32, (ATW, 128), 1)
        fold = jnp.where((r % HEAD_DIM == c) & (c < HEAD_DIM), 1.0, 0.0).astype(BF16)
        qk = jnp.concatenate([pp_ref[1:2, :], pp_ref[2:3, :], jnp.zeros((6, ATW), F32)], axis=0)
        folded = _dot_exact_rhs01(qk, fold)
        o_ref[10:11, M_QNW:M_QNW + 128] = folded[0:1, :]
        o_ref[10:11, M_KNW:M_KNW + 128] = folded[1:2, :]
        o_ref[10:11, M_SINK:M_SINK + ATT_HEADS] = ds_ref[...]
        o_ref[10:11, M_LOSS:M_LOSS + 128] = ff_ref[4:5, 0:128]
        o_ref[11:13, 0:HGW] = dlb_ref[...]

    return pl.pallas_call(
        body, name="pack_small", in_specs=[_VMEM_WHOLE] * 6, out_specs=_VMEM_WHOLE,
        out_shape=jax.ShapeDtypeStruct((SMALL_ROWS, D), F32), compiler_params=_cp(),
    )(acc_in, acc_mg, acc_ffn, acc_prep, dsink, dlb)


def _sum_small(gathered):
    def body(g_ref, o_ref):
        tot = g_ref[0]
        for s in range(1, N_DEV):
            tot = tot + g_ref[s]
        o_ref[...] = tot
        o_ref[R_BADA01:R_BADA01 + 2, :] = tot[0:2, :] + tot[R_DCTX:R_DCTX + 2, :]

    return pl.pallas_call(
        body, name="sum_small", in_specs=[_VMEM_WHOLE], out_specs=_VMEM_WHOLE,
        out_shape=jax.ShapeDtypeStruct((SMALL_ROWS, D), F32), compiler_params=_cp(),
    )(gathered)


_REP_NAMES = ("b_ada", "c_ctx", "norm_mix_w", "norm_ffn_w", "hgrn_norm_w", "q_norm_w", "k_norm_w", "attn_sinks")


def _adamw_replicated(tot, g_c_ctx, ws, ms, vs):
    n = len(_REP_NAMES)

    def body(*refs):
        tot_ref, gc_ref = refs[0], refs[1]
        w_refs, m_refs, v_refs = refs[2:2 + n], refs[2 + n:2 + 2 * n], refs[2 + 2 * n:2 + 3 * n]
        outs = refs[2 + 3 * n:]
        row = lambda r: tot_ref[r:r + 1, :]
        misc = row(R_MISC)
        grads = [jnp.concatenate([row(R_BADA01), row(R_BADA01 + 1)] + [row(k) for k in range(2, 6)], axis=1),
                 gc_ref[...], row(R_NMIX), row(R_NFFN),
                 misc[:, M_HNW:M_HNW + HG_DIM], misc[:, M_QNW:M_QNW + HEAD_DIM],
                 misc[:, M_KNW:M_KNW + HEAD_DIM], misc[:, M_SINK:M_SINK + ATT_HEADS]]
        for k in range(n):
            outs[k][...] = grads[k]
            outs[n + k][...], outs[2 * n + k][...], outs[3 * n + k][...] = _adamw_math(
                w_refs[k][...], grads[k], m_refs[k][...], v_refs[k][...])

    shapes = [jax.ShapeDtypeStruct(w.shape, F32) for w in ws]
    return pl.pallas_call(
        body, name="adamw_replicated", in_specs=[_VMEM_WHOLE] * (2 + 3 * n), out_specs=[_VMEM_WHOLE] * (4 * n),
        out_shape=shapes * 4, compiler_params=_cp(),
    )(tot, g_c_ctx, *ws, *ms, *vs)


def _lb_grads(dlb, lbl):
    def body(d_ref, l_ref, o_ref):
        for d in (0, 1):
            ll = l_ref[d]
            lb = _sigmoid(ll[0:1, :] - ll[1:2, :])
            t = d_ref[d:d + 1, :] * lb * (1.0 - lb)
            o_ref[d, 0:1, :] = t
            o_ref[d, 1:2, :] = -t

    return pl.pallas_call(
        body, name="lb_grads", in_specs=[_VMEM_WHOLE] * 2, out_specs=_VMEM_WHOLE,
        out_shape=jax.ShapeDtypeStruct((2, 2, HGW), F32), compiler_params=_cp(),
    )(dlb, lbl)


def _c_ctx_grad(terms, c_ctx):
    def body(t_ref, c_ref, o_ref):
        tot = t_ref[0, 8:9, :]
        for s in range(1, N_DEV):
            tot = tot + t_ref[s, 8:9, :]
        cv = c_ref[...]
        sg = _sigmoid(cv)
        o_ref[...] = tot * (sg * (1.0 + cv * (1.0 - sg)))

    return pl.pallas_call(
        body, name="c_ctx_grad", in_specs=[_VMEM_WHOLE] * 2, out_specs=_VMEM_WHOLE,
        out_shape=jax.ShapeDtypeStruct((1, D), F32), compiler_params=_cp(),
    )(terms, c_ctx)


def _in_perm():
    fz, bz, inp, kk, vv, qhg, ghg, qat, gates = 0, 512, 1024, 1536, 1664, 1792, 2304, 2816, 3328
    cols = []
    for h in range(HG_HEADS):
        for base in (fz, bz, inp, qhg):
            cols += list(range(base + 128 * h, base + 128 * (h + 1)))
    cols += list(range(ghg, ghg + 512)) + list(range(qat, qat + 512))
    cols += list(range(kk, kk + 128)) + list(range(vv, vv + 128))
    cols += list(range(gates, gates + 2048))
    return np.asarray(cols, np.int32)


_PERM = _in_perm()
_INV_PERM = np.argsort(_PERM).astype(np.int32)


_PIECES = {"a": (0, WA, 128), "b": (WA, WB, 256), "c": (WA + WB, WC, 256)}


def _block_table(piece):
    lo, n, blk = _PIECES[piece]
    starts = [int(_PERM[r]) for r in range(lo, lo + n, blk)]
    assert all(s % blk == 0 and np.array_equal(_PERM[r:r + blk], np.arange(s, s + blk))
               for s, r in zip(starts, range(lo, lo + n, blk)))
    return jnp.asarray([s // blk for s in starts], jnp.int32), blk


def _pick_row_blocks(x, table, blk, name):
    cols = x.shape[1]

    def body(t_ref, x_ref, o_ref):
        o_ref[...] = x_ref[...]

    return pl.pallas_call(
        body, name=name,
        grid_spec=pltpu.PrefetchScalarGridSpec(
            num_scalar_prefetch=1, grid=(table.shape[0],),
            in_specs=[pl.BlockSpec((blk, cols), lambda i, t: (t[i], 0))],
            out_specs=pl.BlockSpec((blk, cols), lambda i, t: (i, 0))),
        out_shape=jax.ShapeDtypeStruct((table.shape[0] * blk, cols), x.dtype),
        compiler_params=_cp(("arbitrary",)),
    )(table, x)


def _place_row_blocks(x, table, blk, into, out_rows, name):
    cols = x.shape[1]

    def body(t_ref, x_ref, *rest):
        rest[-1][...] = x_ref[...]

    operands, in_specs, aliases = [table, x], [pl.BlockSpec((blk, cols), lambda i, t: (i, 0))], {}
    if into is not None:
        operands.append(into)
        in_specs.append(_ANY)
        aliases = {2: 0}
    return pl.pallas_call(
        body, name=name,
        grid_spec=pltpu.PrefetchScalarGridSpec(
            num_scalar_prefetch=1, grid=(table.shape[0],), in_specs=in_specs,
            out_specs=pl.BlockSpec((blk, cols), lambda i, t: (t[i], 0))),
        out_shape=jax.ShapeDtypeStruct((out_rows, cols), x.dtype),
        input_output_aliases=aliases,
        compiler_params=_cp(("arbitrary",)),
    )(*operands)


def _cols_from_blocks(g):
    return jnp.transpose(g, (1, 0, 2)).reshape(g.shape[1], N_DEV * g.shape[2])


def _local_step(x2, ctx2, tgt, lbl, sh_in, sc_in, gate1, sh2, sc2, gate2, norm_mix_w, norm_ffn_w,
                hgrn_norm_w, q_norm_w, k_norm_w, attn_sinks, w_a, w_b, w_c, s_bh, s_ba, s_out,
                s_gate, s_up, s_down):
    first_last = lambda n: [(0, True), (n - 1, False)]
    h_all = _norm_mod_all(ctx2, x2, norm_mix_w, sh_in, sc_in)
    p_a = _mm_nt(h_all, w_a, tm=768, tn=1024, out_dtype=F32, name="proj_a")
    (o, st), (g_gate,) = _hgrn_fwd(
        p_a, lbl, (_gather_comm([s_gate]), [(0, True), (HG_HEADS - 2, True), (HG_HEADS - 1, False)]))
    (p_b, p_c), (g_bh, g_ba, g_out) = _proj_bc(
        h_all, w_b, w_c, (_gather_comm([s_bh, s_ba, s_out]), [(0, True), (N_TILES - 2, True), (N_TILES - 1, False)]))
    cos, sin = _rope_tables()
    qnw_t, knw_t = jnp.tile(q_norm_w, (1, ATT_HEADS)), jnp.tile(k_norm_w, (1, KV_HEADS))
    y_hg, qn, k_rep, v_rep = _prep_fwd(p_b, o, cos, sin, hgrn_norm_w, qnw_t, knw_t)
    (y_at, lse), (g_up,) = _attn_fwd(
        qn, k_rep, v_rep, attn_sinks,
        (_gather_comm([s_up]), [(0, True), (N_BLOCKS // 2, True), (N_BLOCKS - 1, False)]))
    w_bh, w_ba, w_o = g_bh.reshape(D, HGW), g_ba.reshape(D, ATW), g_out.reshape(D, D)
    (a, b, mixed, r, x1, h2), (g_down,) = _merge_fwd(
        y_hg, y_at, p_c, x2, w_bh, w_ba, w_o, gate1, norm_ffn_w, sh2, sc2,
        (_gather_comm([s_down]), [(0, True), (N_LAT_TILES - 2, True), (N_LAT_TILES - 1, False)]))
    g_gate, g_up, g_down = [g.reshape(N_FF_TILES, FF_TILE, D) for g in (g_gate, g_up, g_down)]

    act, d_gate, d_up, d_f, dx1, acc_ffn = _ffn_fused(x1, h2, tgt, g_gate, g_up, g_down, gate2,
                                                      norm_ffn_w, sc2)
    by_chip = lambda t: t.reshape((N_CHIPS, 2) + t.shape[1:])
    ff_by_chip = lambda t: t.reshape(N_CHIPS, 2, FF_BLK, D)
    t_down, _ = _mm_tn_blocked(act, d_f, "grad_down")
    t_down = ff_by_chip(t_down)
    t_gate, (f_down,) = _mm_tn_blocked(d_gate, h2, "grad_gate", (_sibling_comm([t_down]), first_last(N_FF_TILES)))
    t_gate = ff_by_chip(t_gate)
    t_up, (f_gate,) = _mm_tn_blocked(d_up, h2, "grad_up", (_sibling_comm([t_gate]), first_last(N_FF_TILES)))
    t_up = ff_by_chip(t_up)

    (d_r, d_a, d_b, dp_c, dy_hg, dy_at, acc_mg), (f_up,) = _merge_bwd(
        dx1, r, a, b, p_c, w_bh, w_ba, w_o, gate1, (_sibling_comm([t_up]), first_last(N_LAT_TILES)))
    c_down, c_gate, c_up = [_pair_sum(t, f, "pair_sum_" + nm) for t, f, nm in
                            ((t_down, f_down, "down"), (t_gate, f_gate, "gate"), (t_up, f_up, "up"))]
    t_out = _mm_tn(mixed, d_r, tk=512, nk=4, tm=512, tn=1024, out_dtype=BF16, name="grad_out")
    t_bh = _mm_tn(d_a, y_hg, tk=512, nk=4, tm=512, tn=512, out_dtype=BF16, name="grad_bh")
    t_ba = _mm_tn(d_b, y_at, tk=512, nk=4, tm=512, tn=512, out_dtype=BF16, name="grad_ba")
    t_bh, t_ba, t_out = [by_chip(t.reshape(N_DEV, D // N_DEV, t.shape[1])) for t in (t_bh, t_ba, t_out)]
    (dq, dk_rep, dv_rep, dsink), (r_up,) = _attn_bwd(
        qn, k_rep, v_rep, attn_sinks, y_at, lse, dy_at, (_chip_comm([c_up]), first_last(N_BLOCKS)))
    (dp_b, d_o, acc_prep), (f_bh, f_ba, f_out) = _prep_bwd(
        p_b, o, cos, sin, hgrn_norm_w, qnw_t, knw_t, dy_hg, dq, dk_rep, dv_rep,
        (_sibling_comm([t_bh, t_ba, t_out]), first_last(N_TILES)))
    c_bh, c_ba, c_out = [_pair_sum(t, f, "pair_sum_" + nm) for t, f, nm in
                         ((t_bh, f_bh, "bh"), (t_ba, f_ba, "ba"), (t_out, f_out, "out"))]
    (dp_a, dlb), (r_bh, r_ba, r_out, r_down, r_gate) = _hgrn_bwd(
        p_a, lbl, d_o, st, (_chip_comm([c_bh, c_ba, c_out, c_down, c_gate]), first_last(HG_HEADS)))
    t_a = _mm_tn(dp_a, h_all, tk=768, nk=3, tm=1024, tn=1024, out_dtype=BF16, name="grad_in_a")
    t_b = _mm_tn(dp_b, h_all, tk=768, nk=3, tm=640, tn=1024, out_dtype=BF16, name="grad_in_b")
    t_c = _mm_tn(dp_c, h_all, tk=256, nk=8, b_off=1, tm=1024, tn=1024, out_dtype=BF16, name="grad_in_c")
    t_in = None
    for piece, nm in ((t_a, "a"), (t_b, "b"), (t_c, "c")):
        t_in = _place_row_blocks(piece, *_block_table(nm), t_in, IN_COLS, "order_terms_" + nm)
    t_in = by_chip(t_in.reshape(N_DEV, IN_BLK, D))
    (f_in,) = _run_comm(_sibling_comm([t_in]), "scatter_in_sibling")
    (grad_x, acc_in), (r_in,) = _input_bwd(
        dp_a, dp_b, dp_c, w_a, w_b, w_c, ctx2, x2, dx1, norm_mix_w, sh_in, sc_in,
        (_chip_comm([_pair_sum(t_in, f_in, "pair_sum_in")]), first_last(N_TILES)))
    small = _pack_small(acc_in, acc_mg, acc_ffn, acc_prep, dsink, dlb)
    return grad_x, small, [r_in, r_bh, r_ba, r_out, r_gate, r_up, r_down]


def kernel(x, c, ctx, c_ctx, w_ada, b_ada, norm_mix_w, norm_ffn_w, w_in, hgrn_lb_logits, hgrn_norm_w, q_norm_w, k_norm_w, attn_sinks, w_branch_hgrn, w_branch_attn, w_out, w_ffn_gate, w_ffn_up, w_ffn_down, loss_target, m_c_ctx, m_w_ada, m_b_ada, m_norm_mix_w, m_norm_ffn_w, m_w_in, m_hgrn_lb_logits, m_hgrn_norm_w, m_q_norm_w, m_k_norm_w, m_attn_sinks, m_w_branch_hgrn, m_w_branch_attn, m_w_out, m_w_ffn_gate, m_w_ffn_up, m_w_ffn_down, v_c_ctx, v_w_ada, v_b_ada, v_norm_mix_w, v_norm_ffn_w, v_w_in, v_hgrn_lb_logits, v_hgrn_norm_w, v_q_norm_w, v_k_norm_w, v_attn_sinks, v_w_branch_hgrn, v_w_branch_attn, v_w_out, v_w_ffn_gate, v_w_ffn_up, v_w_ffn_down):
    me = 4 * lax.axis_index("x") + 2 * lax.axis_index("y") + lax.axis_index("c")
    x2, ctx2, tgt = x[0], ctx[0], loss_target[0]
    w_ada2, w_in2 = w_ada[0], w_in[0]

    cond = jnp.zeros((8, D), F32).at[0].set(c[0]).at[1, :256].set(hgrn_lb_logits.reshape(256))
    b_cols = lax.dynamic_slice(b_ada, (0, me * ADA_BLK), (1, ADA_BLK))
    g0, cc, g1, g_in = _prologue(cond, c_ctx.reshape(1, D), w_ada2, b_cols, w_in2.T.astype(BF16))
    lbl = jnp.transpose(g0[:, 1, :256].reshape(N_DEV, 2, 2, 64), (1, 2, 0, 3)).reshape(2, 2, HGW)
    mod_all = _cols_from_blocks(g1)
    mod = lax.dynamic_slice(mod_all, (me, 0), (1, 6 * D)).reshape(6, D)
    mod_c = mod_all[8].reshape(6, D)
    sh1, sc1, gate1, sh2, sc2, gate2 = [mod[k:k + 1] for k in range(6)]
    sh_in = jnp.concatenate([mod_c[0:1], sh1], axis=0)
    sc_in = jnp.concatenate([mod_c[1:2], sc1], axis=0)

    shards = [w_branch_hgrn[0].T, w_branch_attn[0].T, w_out[0], w_ffn_gate[0].T, w_ffn_up[0].T, w_ffn_down[0]]
    w_in_t = g_in.reshape(IN_COLS, D)
    w_a, w_b, w_c = [_pick_row_blocks(w_in_t, *_block_table(nm), "order_w_" + nm) for nm in "abc"]

    grad_x, small, (r_in, r_bh, r_ba, r_out, r_gate, r_up, r_down) = _local_step(
        x2, ctx2, tgt, lbl, sh_in, sc_in, gate1, sh2, sc2, gate2, norm_mix_w, norm_ffn_w, hgrn_norm_w,
        q_norm_w, k_norm_w, attn_sinks, w_a, w_b, w_c, *[s.astype(BF16) for s in shards])

    big = {}
    for nm, rr, ww, mm, vv, tr, transposed in (
            ("w_in", r_in, w_in2, m_w_in[0], v_w_in[0], 336, True),
            ("w_branch_hgrn", r_bh, w_branch_hgrn[0], m_w_branch_hgrn[0], v_w_branch_hgrn[0], 128, True),
            ("w_branch_attn", r_ba, w_branch_attn[0], m_w_branch_attn[0], v_w_branch_attn[0], 128, True),
            ("w_out", r_out, w_out[0], m_w_out[0], v_w_out[0], 128, False),
            ("w_ffn_gate", r_gate, w_ffn_gate[0], m_w_ffn_gate[0], v_w_ffn_gate[0], 352, True),
            ("w_ffn_up", r_up, w_ffn_up[0], m_w_ffn_up[0], v_w_ffn_up[0], 352, True),
            ("w_ffn_down", r_down, w_ffn_down[0], m_w_ffn_down[0], v_w_ffn_down[0], 352, False)):
        if transposed:
            res = _adamw_sharded(rr, ww.T, mm.T, vv.T, "adamw_" + nm, tr)
            big[nm] = [t.T[None] for t in res]
        else:
            big[nm] = [t[None] for t in _adamw_sharded(rr, ww, mm, vv, "adamw_" + nm, tr)]

    (g2,) = _all_gather([small], "gather_small", True)
    tot = _sum_small(g2)
    dm = jnp.zeros((16, 6 * D), F32).at[:8].set(g2[:, R_DMOD:R_DMOD + 6, :].reshape(N_DEV, 6 * D))
    dm = dm.at[8, :2 * D].set(tot[R_DCTX:R_DCTX + 2].reshape(2 * D))
    dm_cols = lax.dynamic_slice(dm, (0, me * ADA_BLK), (16, ADA_BLK))
    g_w_ada, dsc_term = _ada_grads(cc, dm_cols, w_ada2)
    (g3,) = _all_gather([dsc_term], "gather_cctx", True)
    g_c_ctx = _c_ctx_grad(g3, c_ctx.reshape(1, D))
    g_lbl = _lb_grads(tot[R_DLB:R_DLB + 2, :HGW], lbl)
    g_lb_mine = lax.dynamic_slice(g_lbl, (0, 0, me * 64), (2, 2, 64))
    misc = tot[R_MISC]
    loss = misc[M_LOSS]

    rep_out = _adamw_replicated(
        tot, g_c_ctx,
        [b_ada, c_ctx.reshape(1, D), norm_mix_w, norm_ffn_w, hgrn_norm_w, q_norm_w, k_norm_w, attn_sinks],
        [m_b_ada, m_c_ctx.reshape(1, D), m_norm_mix_w, m_norm_ffn_w, m_hgrn_norm_w, m_q_norm_w, m_k_norm_w,
         m_attn_sinks],
        [v_b_ada, v_c_ctx.reshape(1, D), v_norm_mix_w, v_norm_ffn_w, v_hgrn_norm_w, v_q_norm_w, v_k_norm_w,
         v_attn_sinks])
    rep = []
    for kind in range(4):
        vals = dict(zip(_REP_NAMES, rep_out[kind * len(_REP_NAMES):(kind + 1) * len(_REP_NAMES)]))
        vals["c_ctx"] = vals["c_ctx"].reshape(D)
        rep.append(vals)

    d_ada, nm_ada, nv_ada = _adamw_plain(g_w_ada, w_ada2, m_w_ada[0], v_w_ada[0], "adamw_w_ada")
    ada = [t[None] for t in (g_w_ada, d_ada, nm_ada, nv_ada)]
    lb_w = hgrn_lb_logits.reshape(4, 64)
    d_lb, nm_lb, nv_lb = _adamw_plain(g_lb_mine.reshape(4, 64), lb_w, m_hgrn_lb_logits.reshape(4, 64),
                                      v_hgrn_lb_logits.reshape(4, 64), "adamw_lb")
    lbs = [t.reshape(2, 2, 64) for t in (g_lb_mine, d_lb, nm_lb, nv_lb)]

    names = ['c_ctx', 'w_ada', 'b_ada', 'norm_mix_w', 'norm_ffn_w', 'w_in', 'hgrn_lb_logits', 'hgrn_norm_w',
             'q_norm_w', 'k_norm_w', 'attn_sinks', 'w_branch_hgrn', 'w_branch_attn', 'w_out', 'w_ffn_gate',
             'w_ffn_up', 'w_ffn_down']
    outs = [loss, grad_x[None]]
    for kind in range(4):
        for nm in names:
            if nm == 'w_ada':
                outs.append(ada[kind])
            elif nm == 'hgrn_lb_logits':
                outs.append(lbs[kind])
            elif nm in big:
                outs.append(big[nm][kind])
            else:
                outs.append(rep[kind][nm])
    return tuple(outs)
```

```python
import functools
import math

import numpy as np
import jax
import jax.numpy as jnp
from jax import lax
from jax.experimental import pallas as pl
from jax.experimental.pallas import tpu as pltpu

F32 = jnp.float32
BF16 = jnp.bfloat16

N_DEV = 8
D = 1024
S = 2048
L = 256
T = L + S
TM = 256
N_TILES = T // TM
N_LAT_TILES = S // TM
HG_HEADS = 4
HG_DIM = 128
HGW = 512
CHUNK = 32
N_CHUNKS = T // CHUNK
N_CTX_CHUNKS = L // CHUNK
N_LAT_CHUNKS = S // CHUNK
ATT_HEADS = 8
KV_HEADS = 2
HEAD_DIM = 64
ATW = 512
KVW = 128
BLOCK = 128
N_BLOCKS = S // BLOCK
GRID_W = 64
ROPE_THETA = 10000.0
D_FF = 2816
FF_BLK = D_FF // N_DEV
FF_TILE = 256
N_FF_TILES = D_FF // FF_TILE
IN_COLS = 5376
IN_BLK = IN_COLS // N_DEV
ADA_BLK = 6 * D // N_DEV
EPS = 1e-6
WA, WB, WC = 2048, 1280, 2048

ADAM_LR = 0.001
ADAM_B1 = 0.9
ADAM_B2 = 0.999
ADAM_EPS = 1e-08
ADAM_WD = 0.01
ADAM_STEP = 10

VMEM_LIMIT = 56 * 1024 * 1024
MESH = pl.DeviceIdType.MESH


def _cp(sem=None, vmem=VMEM_LIMIT):
    return pltpu.CompilerParams(dimension_semantics=sem, vmem_limit_bytes=vmem)


def _full(shape):
    n = len(shape)
    return pl.BlockSpec(shape, lambda *_: (0,) * n)


_VMEM_WHOLE = pl.BlockSpec(memory_space=pltpu.VMEM)
_ANY = pl.BlockSpec(memory_space=pl.ANY)


def _sigmoid(v):
    return 1.0 / (1.0 + jnp.exp(-v))


def _dot(a, b):
    return jnp.dot(a, b, preferred_element_type=F32)


def _dot_nt(a, b):
    return lax.dot_general(a, b, (((1,), (1,)), ((), ())), preferred_element_type=F32)


def _dot_tn(a, b):
    return lax.dot_general(a, b, (((0,), (0,)), ((), ())), preferred_element_type=F32)


def _split3(v):
    hi = v.astype(BF16)
    r = v - hi.astype(F32)
    mid = r.astype(BF16)
    lo = (r - mid.astype(F32)).astype(BF16)
    return hi, mid, lo


def _dot_exact_rhs01(v, m01):
    hi, mid, lo = _split3(v)
    return _dot(hi, m01) + _dot(mid, m01) + _dot(lo, m01)


def _split2(v):
    hi = v.astype(BF16)
    return hi, (v - hi.astype(F32)).astype(BF16)


def _dot_lhs01(m01, v):
    hi, lo = _split2(v)
    return _dot(m01, hi) + _dot(m01, lo)


def _dot_f32(a, b, dot=_dot):
    ah, am, al = _split3(a)
    bh, bm, bl = _split3(b)
    return (dot(ah, bh) + (dot(ah, bm) + dot(am, bh))
            + (dot(am, bm) + dot(ah, bl) + dot(al, bh)))


def _my_pos():
    return lax.axis_index("x"), lax.axis_index("y"), lax.axis_index("c")


class _Comm:
    def __init__(self, operands, out_shapes, sems, phases):
        self.operands, self.out_shapes, self.sems, self.phases = operands, out_shapes, sems, phases


def _gather_comm(blocks):
    n = len(blocks)

    def parts(ins, outs, sems):
        send_sems, recv_sems, local_sems = sems
        x, y, c = _my_pos()
        me, sibling = (x, y, c), (x, y, 1 - c)
        chips = [(1 - x, y), (x, 1 - y), (1 - x, 1 - y)]

        def slot(a, px, py, pc):
            return outs[a].at[4 * px + 2 * py + pc]

        def copy(a, k, block, to, src=None):
            return pltpu.make_async_remote_copy(
                src_ref=slot(a, *block) if src is None else src, dst_ref=slot(a, *block),
                send_sem=send_sems.at[a, k], recv_sem=recv_sems.at[a, k],
                device_id=to, device_id_type=MESH)

        mine = [pltpu.make_async_copy(ins[a], slot(a, *me), local_sems.at[a]) for a in range(n)]
        first = []
        for a in range(n):
            first.append(copy(a, 0, me, sibling, src=ins[a]))
            first += [copy(a, 1 + j, me, (*chip, c), src=ins[a]) for j, chip in enumerate(chips)]
        passed = [copy(a, 4 + j, (*chip, c), sibling) for j, chip in enumerate(chips) for a in range(n)]
        return c, me, sibling, chips, copy, mine, first, passed

    def start(ins, outs, sems):
        _, _, _, _, _, mine, first, _ = parts(ins, outs, sems)
        for cp in mine + first:
            cp.start()

    def forward(ins, outs, sems):
        c, me, _, chips, copy, _, _, passed = parts(ins, outs, sems)
        for j, chip in enumerate(chips):
            for a in range(n):
                copy(a, 1 + j, (*chip, c), me).wait_recv()
                passed[j * n + a].start()

    def finish(ins, outs, sems):
        c, me, sibling, chips, copy, mine, first, passed = parts(ins, outs, sems)
        for a in range(n):
            copy(a, 0, sibling, me).wait_recv()
            for j, chip in enumerate(chips):
                copy(a, 4 + j, (*chip, 1 - c), me).wait_recv()
        for cp in first + passed:
            cp.wait_send()
        for cp in mine:
            cp.wait()

    return _Comm(blocks, [jax.ShapeDtypeStruct((N_DEV,) + b.shape, b.dtype) for b in blocks],
                 [pltpu.SemaphoreType.DMA((n, 7)), pltpu.SemaphoreType.DMA((n, 7)), pltpu.SemaphoreType.DMA((n,))],
                 [start, forward, finish])


def _run_comm(comm, name, in_vmem=False):
    n_in, n_out = len(comm.operands), len(comm.out_shapes)

    def body(*refs):
        ins, outs, sems = refs[:n_in], refs[n_in:n_in + n_out], refs[n_in + n_out:]
        for phase in comm.phases:
            phase(ins, outs, sems)

    spec = _VMEM_WHOLE if in_vmem else _ANY
    return pl.pallas_call(
        body, name=name, out_shape=comm.out_shapes, in_specs=[spec] * n_in, out_specs=[spec] * n_out,
        scratch_shapes=comm.sems,
    )(*comm.operands)


def _carrier_call(body, comm, schedule, *, name, grid, in_specs, out_specs, out_shape, scratch_shapes, operands):
    n_in, n_out, n_scr = len(in_specs), len(out_specs), len(scratch_shapes)
    c_in, c_out = len(comm.operands), len(comm.out_shapes)

    def full_body(*refs):
        ins, refs = refs[:n_in], refs[n_in:]
        cins, refs = refs[:c_in], refs[c_in:]
        outs, refs = refs[:n_out], refs[n_out:]
        couts, refs = refs[:c_out], refs[c_out:]
        scr, csems = refs[:n_scr], refs[n_scr:]
        step = pl.program_id(0)

        def run(before):
            for (at, when_before), phase in zip(schedule, comm.phases):
                if when_before == before:
                    pl.when(step == at)(functools.partial(phase, cins, couts, csems))

        run(True)
        body(*ins, *outs, *scr)
        run(False)

    res = pl.pallas_call(
        full_body, name=name, grid=grid,
        in_specs=list(in_specs) + [_ANY] * c_in, out_specs=list(out_specs) + [_ANY] * c_out,
        out_shape=list(out_shape) + list(comm.out_shapes),
        scratch_shapes=list(scratch_shapes) + list(comm.sems),
        compiler_params=_cp(("arbitrary",)),
    )(*operands, *comm.operands)
    return res[:n_out], res[n_out:]


def _pcall(body, carried, *, name, grid, in_specs, out_specs, out_shape, scratch_shapes, operands):
    if carried is None:
        res = pl.pallas_call(body, name=name, grid=grid, in_specs=in_specs, out_specs=out_specs,
                             out_shape=out_shape, scratch_shapes=scratch_shapes,
                             compiler_params=_cp(("arbitrary",)))(*operands)
        return res, ()
    return _carrier_call(body, carried[0], carried[1], name=name, grid=grid, in_specs=in_specs,
                         out_specs=out_specs, out_shape=out_shape, scratch_shapes=scratch_shapes,
                         operands=operands)


def _all_gather(blocks, name, in_vmem):
    return _run_comm(_gather_comm(blocks), name, in_vmem)


N_CHIPS = 4


def _sibling_comm(contribs):
    n = len(contribs)

    def copies(ins, outs, sems):
        send_sems, recv_sems = sems
        x, y, c = _my_pos()
        return [pltpu.make_async_remote_copy(
            src_ref=ins[a].at[pl.ds(0, N_CHIPS), 1 - c], dst_ref=outs[a],
            send_sem=send_sems.at[a], recv_sem=recv_sems.at[a],
            device_id=(x, y, 1 - c), device_id_type=MESH) for a in range(n)]

    def start(ins, outs, sems):
        for cp in copies(ins, outs, sems):
            cp.start()

    def finish(ins, outs, sems):
        cps = copies(ins, outs, sems)
        for cp in cps:
            cp.wait_recv()
        for cp in cps:
            cp.wait_send()

    return _Comm(contribs, [jax.ShapeDtypeStruct((N_CHIPS,) + b.shape[2:], b.dtype) for b in contribs],
                 [pltpu.SemaphoreType.DMA((n,)), pltpu.SemaphoreType.DMA((n,))], [start, finish])


def _pair_sum(mine, theirs, name):
    _, _, rows, cols = mine.shape
    core = lax.axis_index("c").astype(jnp.int32).reshape(1)

    def body(c_ref, m_ref, t_ref, o_ref):
        o_ref[...] = (m_ref[...].astype(F32) + t_ref[...].astype(F32)).astype(BF16)

    return pl.pallas_call(
        body, name=name,
        grid_spec=pltpu.PrefetchScalarGridSpec(
            num_scalar_prefetch=1, grid=(N_CHIPS,),
            in_specs=[pl.BlockSpec((None, None, rows, cols), lambda q, c: (q, c[0], 0, 0)),
                      pl.BlockSpec((None, rows, cols), lambda q, c: (q, 0, 0))],
            out_specs=pl.BlockSpec((None, rows, cols), lambda q, c: (q, 0, 0))),
        out_shape=jax.ShapeDtypeStruct((N_CHIPS, rows, cols), BF16),
        compiler_params=_cp(("parallel",)),
    )(core, mine, theirs)


def _chip_comm(sums):
    n = len(sums)

    def parts(ins, outs, sems):
        send_sems, recv_sems, local_sems = sems
        x, y, c = _my_pos()
        q_me = 2 * x + y
        chips = [(1 - x, y), (x, 1 - y), (1 - x, 1 - y)]
        mine = [pltpu.make_async_copy(ins[a].at[q_me], outs[a].at[q_me], local_sems.at[a]) for a in range(n)]
        sends, recvs = [], []
        for j, (px, py) in enumerate(chips):
            for a in range(n):
                q = 2 * px + py
                sends.append(pltpu.make_async_remote_copy(
                    src_ref=ins[a].at[q], dst_ref=outs[a].at[q_me],
                    send_sem=send_sems.at[a, j], recv_sem=recv_sems.at[a, j],
                    device_id=(px, py, c), device_id_type=MESH))
                recvs.append(pltpu.make_async_remote_copy(
                    src_ref=ins[a].at[q], dst_ref=outs[a].at[q],
                    send_sem=send_sems.at[a, j], recv_sem=recv_sems.at[a, j],
                    device_id=(x, y, c), device_id_type=MESH))
        return mine, sends, recvs

    def start(ins, outs, sems):
        mine, sends, _ = parts(ins, outs, sems)
        for cp in mine + sends:
            cp.start()

    def finish(ins, outs, sems):
        mine, sends, recvs = parts(ins, outs, sems)
        for cp in recvs:
            cp.wait_recv()
        for cp in sends:
            cp.wait_send()
        for cp in mine:
            cp.wait()

    return _Comm(sums, [jax.ShapeDtypeStruct(b.shape, b.dtype) for b in sums],
                 [pltpu.SemaphoreType.DMA((n, 3)), pltpu.SemaphoreType.DMA((n, 3)), pltpu.SemaphoreType.DMA((n,))],
                 [start, finish])


def _mm_nt(a, bt, *, tm, tn, out_dtype, name, row_off=0, rows=None):
    rows = a.shape[0] if rows is None else rows
    n, k = bt.shape

    def body(a_ref, b_ref, o_ref):
        o_ref[...] = _dot_nt(a_ref[...], b_ref[...]).astype(out_dtype)

    return pl.pallas_call(
        body, name=name, grid=(rows // tm, n // tn),
        in_specs=[pl.BlockSpec((tm, k), lambda i, j: (i + row_off, 0)),
                  pl.BlockSpec((tn, k), lambda i, j: (j, 0))],
        out_specs=pl.BlockSpec((tm, tn), lambda i, j: (i, j)),
        out_shape=jax.ShapeDtypeStruct((rows, n), out_dtype),
        compiler_params=_cp(("parallel", "parallel")),
    )(a, bt)


def _mm_tn(a, b, *, tk, nk, tm, tn, out_dtype, name, a_off=0, b_off=0):
    m, n = a.shape[1], b.shape[1]

    def body(a_ref, b_ref, o_ref, acc):
        kk = pl.program_id(2)

        @pl.when(kk == 0)
        def _():
            acc[...] = jnp.zeros_like(acc)

        acc[...] += _dot_tn(a_ref[...], b_ref[...])

        @pl.when(kk == nk - 1)
        def _():
            o_ref[...] = acc[...].astype(out_dtype)

    return pl.pallas_call(
        body, name=name, grid=(m // tm, n // tn, nk),
        in_specs=[pl.BlockSpec((tk, tm), lambda i, j, kk: (kk + a_off, i)),
                  pl.BlockSpec((tk, tn), lambda i, j, kk: (kk + b_off, j))],
        out_specs=pl.BlockSpec((tm, tn), lambda i, j, kk: (i, j)),
        out_shape=jax.ShapeDtypeStruct((m, n), out_dtype),
        scratch_shapes=[pltpu.VMEM((tm, tn), F32)],
        compiler_params=_cp(("parallel", "parallel", "arbitrary")),
    )(a, b)


def _mm_tn_blocked(a, b, name, carried=None):
    nb, _, w = a.shape
    n = b.shape[1]

    def body(a_ref, b_ref, o_ref):
        o_ref[...] = _dot_tn(a_ref[...], b_ref[...]).astype(BF16)

    (out,), extra = _pcall(
        body, carried, name=name, grid=(nb,),
        in_specs=[pl.BlockSpec((None, S, w), lambda j: (j, 0, 0)), _full((S, n))],
        out_specs=[pl.BlockSpec((None, w, n), lambda j: (j, 0, 0))],
        out_shape=[jax.ShapeDtypeStruct((nb, w, n), BF16)],
        scratch_shapes=[], operands=[a, b])
    return out, extra


def _prologue(cond, c_ctx, w_ada, b_cols, w_in_t):
    rows_shape = jax.ShapeDtypeStruct((16, ADA_BLK), F32)
    big, g_cond, g_mod = _gather_comm([w_in_t]), _gather_comm([cond]), _gather_comm([rows_shape])

    def body(cond_ref, cctx_ref, wada_ref, b_ref, win_ref, g0_ref, cc_ref, g1_ref, gin_ref, rows_ref, *sems):
        s_big, s_cond, s_mod = sems[0:3], sems[3:6], sems[6:9]
        big.phases[0]([win_ref], [gin_ref], s_big)
        for phase in g_cond.phases:
            phase([cond_ref], [g0_ref], s_cond)
        cc_ref[...] = jnp.zeros_like(cc_ref)
        for j in range(N_DEV):
            cc_ref[j:j + 1, :] = g0_ref[j, 0:1, :]
        cc_ref[N_DEV:N_DEV + 1, :] = cctx_ref[...]
        cv = cc_ref[...]
        rows_ref[...] = _dot_f32(cv * _sigmoid(cv), wada_ref[...]) + b_ref[...]
        for phase in g_mod.phases:
            phase([rows_ref], [g1_ref], s_mod)
        big.phases[1]([win_ref], [gin_ref], s_big)
        big.phases[2]([win_ref], [gin_ref], s_big)

    return pl.pallas_call(
        body, name="prologue",
        in_specs=[_VMEM_WHOLE] * 4 + [_ANY], out_specs=[_VMEM_WHOLE] * 3 + [_ANY],
        out_shape=[g_cond.out_shapes[0], jax.ShapeDtypeStruct((16, D), F32), g_mod.out_shapes[0],
                   big.out_shapes[0]],
        scratch_shapes=[pltpu.VMEM((16, ADA_BLK), F32)] + big.sems + g_cond.sems + g_mod.sems,
        compiler_params=_cp(),
    )(cond, c_ctx, w_ada, b_cols, w_in_t)


def _ada_grads(cc, dm_cols, w_ada):
    def body(c_ref, dm_ref, w_ref, gw_ref, dsc_ref):
        cv = c_ref[...]
        sc = cv * _sigmoid(cv)
        dm = dm_ref[...]
        gw_ref[...] = _dot_f32(sc, dm, dot=_dot_tn)
        dsc_ref[...] = _dot_f32(dm, w_ref[...], dot=_dot_nt)

    return pl.pallas_call(
        body, name="ada_grads",
        in_specs=[_VMEM_WHOLE] * 3, out_specs=[_VMEM_WHOLE] * 2,
        out_shape=[jax.ShapeDtypeStruct((D, ADA_BLK), F32), jax.ShapeDtypeStruct((16, D), F32)],
        compiler_params=_cp(),
    )(cc, dm_cols, w_ada)


def _lat(i):
    return jnp.maximum(i - 1, 0)


def _rms_mod(xv, nw, sh, sc):
    rstd = lax.rsqrt(jnp.mean(xv * xv, axis=-1, keepdims=True) + EPS)
    return (xv * rstd * nw) * (1.0 + sc) + sh


def _rms_mod_bwd(xv, nw, sc, dh):
    rstd = lax.rsqrt(jnp.mean(xv * xv, axis=-1, keepdims=True) + EPS)
    xhat = xv * rstd
    dn = dh * (1.0 + sc)
    dxhat = dn * nw
    dx = rstd * (dxhat - xhat * jnp.mean(dxhat * xhat, axis=-1, keepdims=True))
    return (dx, jnp.sum(dh, axis=0, keepdims=True), jnp.sum(dh * (xhat * nw), axis=0, keepdims=True),
            jnp.sum(dn * xhat, axis=0, keepdims=True))


def _norm_mod_all(ctx, x, nw, sh, sc):
    def body(ctx_ref, x_ref, nw_ref, sh_ref, sc_ref, o_ref):
        i = pl.program_id(0)
        sel = jnp.minimum(i, 1)
        xv = jnp.where(i == 0, ctx_ref[...], x_ref[...])
        o_ref[...] = _rms_mod(xv, nw_ref[...], sh_ref[pl.ds(sel, 1), :], sc_ref[pl.ds(sel, 1), :]).astype(BF16)

    return pl.pallas_call(
        body, name="norm_mod", grid=(N_TILES,),
        in_specs=[_full((TM, D)), pl.BlockSpec((TM, D), lambda i: (_lat(i), 0)),
                  _full((1, D)), _full((2, D)), _full((2, D))],
        out_specs=pl.BlockSpec((TM, D), lambda i: (i, 0)),
        out_shape=jax.ShapeDtypeStruct((T, D), BF16),
        compiler_params=_cp(("parallel",)),
    )(ctx, x, nw, sh, sc)


def _chunk_masks(reverse):
    row = lax.broadcasted_iota(jnp.int32, (TM, TM), 0)
    col = lax.broadcasted_iota(jnp.int32, (TM, TM), 1)
    same = (row // CHUNK) == (col // CHUNK)
    tri = same & ((col >= row) if reverse else (col <= row))
    return same, tri


def _chunk_order(i, reverse):
    if not reverse:
        return i
    return jnp.where(i < N_CTX_CHUNKS, N_CTX_CHUNKS - 1 - i, N_CHUNKS + N_CTX_CHUNKS - 1 - i)


def _decay_terms(z, lb, same01, tri01):
    f = lb + (1.0 - lb) * _sigmoid(z)
    g = jnp.log(f)
    g2 = jnp.concatenate(_split2(g), axis=1)
    b2 = _dot(tri01, g2)
    t2 = _dot(same01, g2)
    return f, 1.0 - f, b2[:, :HG_DIM] + b2[:, HG_DIM:], t2[:, :HG_DIM] + t2[:, HG_DIM:]


def _chunk_outer(a, b):
    n = TM // CHUNK
    return jnp.einsum('ncv,nck->nvk', a.reshape(n, CHUNK, HG_DIM), b.reshape(n, CHUNK, HG_DIM),
                      preferred_element_type=F32)


def _hgrn_fwd(p_a, lbl, carried=None):
    cpt = TM // CHUNK

    def body(p_ref, lbl_ref, o_ref, st_ref, qd_s, kd_s, u_s, v_s, ebt_s):
        masks = [_chunk_masks(d == 1) for d in (0, 1)]
        same01 = jnp.where(masks[0][0], 1.0, 0.0).astype(BF16)
        tri = [m[1] for m in masks]
        tri01 = [jnp.where(t, 1.0, 0.0).astype(BF16) for t in tri]
        lb = [_sigmoid(lbl_ref[d][0:1, :] - lbl_ref[d][1:2, :]) for d in (0, 1)]

        def prep(r, carry):
            r0 = pl.multiple_of(r * TM, TM)
            vb = p_ref[pl.ds(r0, TM), 2 * HG_DIM:3 * HG_DIM].astype(BF16)
            v_s[pl.ds(r0, TM), :] = vb
            for d in (0, 1):
                z = p_ref[pl.ds(r0, TM), d * HG_DIM:(d + 1) * HG_DIM]
                _, k, b, bt = _decay_terms(z, lb[d], same01, tri01[d])
                u_s[d, pl.ds(r * cpt, cpt)] = _chunk_outer(vb, (k * jnp.exp(bt - b)).astype(BF16))
                ebt_s[d, pl.ds(r0, TM), :] = jnp.exp(bt)

                @pl.when(r >= 1)
                def _():
                    rl = pl.multiple_of(r0 - L, TM)
                    qr = p_ref[pl.ds(r0, TM), 3 * HG_DIM:4 * HG_DIM]
                    q = qr * _sigmoid(qr) * HG_DIM ** -0.5
                    qd_s[d, pl.ds(rl, TM), :] = (q * jnp.exp(b)).astype(BF16)
                    kd_s[d, pl.ds(rl, TM), :] = (k * jnp.exp(-b)).astype(BF16)

            return carry

        lax.fori_loop(0, N_TILES, prep, 0)

        def scan(i, sts):
            new = []
            for d in (0, 1):
                nn = _chunk_order(i, d == 1)
                c0 = pl.multiple_of(nn * CHUNK, CHUNK)
                st_ref[d, nn] = sts[d].astype(BF16)
                new.append(sts[d] * ebt_s[d, pl.ds(c0, 1), :] + u_s[d, nn])
            return tuple(new)

        zero = jnp.zeros((HG_DIM, HG_DIM), F32)
        lax.fori_loop(0, N_CHUNKS, scan, (zero, zero))

        def outp(r, carry):
            r0 = pl.multiple_of(r * TM, TM)
            vb = v_s[pl.ds(r0 + L, TM), :]
            o = jnp.zeros((TM, HG_DIM), F32)
            for d in (0, 1):
                qd = qd_s[d, pl.ds(r0, TM), :]
                a = jnp.where(tri[d], _dot_nt(qd, kd_s[d, pl.ds(r0, TM), :]), 0.0)
                stb = st_ref[d, pl.ds(N_CTX_CHUNKS + r * cpt, cpt)]
                inter = jnp.einsum('nck,nvk->ncv', qd.reshape(cpt, CHUNK, HG_DIM), stb,
                                   preferred_element_type=F32)
                o = o + _dot(a.astype(BF16), vb) + inter.reshape(TM, HG_DIM)
            o_ref[pl.ds(r0, TM), :] = o
            return carry

        lax.fori_loop(0, N_LAT_TILES, outp, 0)

    return _pcall(
        body, carried, name="hgrn_fwd", grid=(HG_HEADS,),
        in_specs=[pl.BlockSpec((T, 4 * HG_DIM), lambda h: (0, h)),
                  pl.BlockSpec((2, 2, HG_DIM), lambda h: (0, 0, h))],
        out_specs=[pl.BlockSpec((S, HG_DIM), lambda h: (0, h)),
                   pl.BlockSpec((2, None, N_CHUNKS, HG_DIM, HG_DIM), lambda h: (0, h, 0, 0, 0))],
        out_shape=[jax.ShapeDtypeStruct((S, HGW), F32),
                   jax.ShapeDtypeStruct((2, HG_HEADS, N_CHUNKS, HG_DIM, HG_DIM), BF16)],
        scratch_shapes=[pltpu.VMEM((2, S, HG_DIM), BF16), pltpu.VMEM((2, S, HG_DIM), BF16),
                        pltpu.VMEM((2, N_CHUNKS, HG_DIM, HG_DIM), F32), pltpu.VMEM((T, HG_DIM), BF16),
                        pltpu.VMEM((2, T, HG_DIM), F32)],
        operands=[p_a, lbl])


def _hgrn_bwd(p_a, lbl, d_o, st, carried=None):
    cpt = TM // CHUNK

    def rows(r):
        return r * TM if isinstance(r, int) else pl.multiple_of(r * TM, TM)

    def body(p_ref, lbl_ref, do_ref, st_ref, dp_ref, dlb_ref, b_s, bt_s, dbt_s, qd_s, dst_s, w_s):
        masks = [_chunk_masks(d == 1) for d in (0, 1)]
        same01 = jnp.where(masks[0][0], 1.0, 0.0).astype(BF16)
        tri = [m[1] for m in masks]
        tri01 = [jnp.where(t, 1.0, 0.0).astype(BF16) for t in tri]
        later01 = [tri01[1], tri01[0]]
        lb = [_sigmoid(lbl_ref[d][0:1, :] - lbl_ref[d][1:2, :]) for d in (0, 1)]

        def prep_tile(r, latent):
            r0 = rows(r)
            for d in (0, 1):
                z = p_ref[pl.ds(r0, TM), d * HG_DIM:(d + 1) * HG_DIM]
                _, _, b, bt = _decay_terms(z, lb[d], same01, tri01[d])
                b_s[d, pl.ds(r0, TM), :] = b
                bt_s[d, pl.ds(r0, TM), :] = bt
                if latent:
                    rl = pl.multiple_of(r0 - L, TM)
                    qr = p_ref[pl.ds(r0, TM), 3 * HG_DIM:4 * HG_DIM]
                    qd = (qr * _sigmoid(qr) * HG_DIM ** -0.5 * jnp.exp(b)).astype(BF16)
                    qd_s[d, pl.ds(rl, TM), :] = qd
                    w_s[d, pl.ds(r * cpt, cpt)] = _chunk_outer(
                        do_ref[pl.ds(rl, TM), :].astype(BF16), qd).astype(BF16)

        prep_tile(0, False)
        w_s[:, pl.ds(0, N_CTX_CHUNKS)] = jnp.zeros((2, N_CTX_CHUNKS, HG_DIM, HG_DIM), BF16)

        def prep(r, carry):
            prep_tile(r, True)
            return carry

        lax.fori_loop(1, N_TILES, prep, 0)

        def rscan(j, dsts):
            i = N_CHUNKS - 1 - j
            new = []
            for d in (0, 1):
                nn = _chunk_order(i, d == 1)
                c0 = pl.multiple_of(nn * CHUNK, CHUNK)
                dst_s[d, nn] = dsts[d].astype(BF16)
                after = st_ref[d, _chunk_order(jnp.minimum(i + 1, N_CHUNKS - 1), d == 1)].astype(F32)
                dbt_s[d, pl.ds(c0, CHUNK), :] = jnp.broadcast_to(
                    jnp.sum(after * dsts[d], axis=0, keepdims=True), (CHUNK, HG_DIM))
                new.append(dsts[d] * jnp.exp(bt_s[d, pl.ds(c0, 1), :]) + w_s[d, nn].astype(F32))
            return tuple(new)

        zero = jnp.zeros((HG_DIM, HG_DIM), F32)
        lax.fori_loop(0, N_CHUNKS, rscan, (zero, zero))

        def grad_tile(r, latent):
            r0 = rows(r)
            vb = p_ref[pl.ds(r0, TM), 2 * HG_DIM:3 * HG_DIM].astype(BF16)
            dv = jnp.zeros((TM, HG_DIM), F32)
            dq = jnp.zeros((TM, HG_DIM), F32)
            dlbs = []
            if latent:
                rl = pl.multiple_of(r0 - L, TM)
                qr = p_ref[pl.ds(r0, TM), 3 * HG_DIM:4 * HG_DIM]
                sq = _sigmoid(qr)
                do = do_ref[pl.ds(rl, TM), :].astype(BF16)
                da_full = _dot_nt(do, vb)
            for d in (0, 1):
                z = p_ref[pl.ds(r0, TM), d * HG_DIM:(d + 1) * HG_DIM]
                sz = _sigmoid(z)
                f = lb[d] + (1.0 - lb[d]) * sz
                k = 1.0 - f
                b = b_s[d, pl.ds(r0, TM), :]
                e2 = jnp.exp(bt_s[d, pl.ds(r0, TM), :] - b)
                dstb = dst_s[d, pl.ds(r * cpt, cpt)]
                kd2 = k * e2
                dkd2 = jnp.einsum('ncv,nvk->nck', vb.reshape(cpt, CHUNK, HG_DIM), dstb,
                                  preferred_element_type=F32).reshape(TM, HG_DIM)
                dv = dv + jnp.einsum('nck,nvk->ncv', kd2.astype(BF16).reshape(cpt, CHUNK, HG_DIM), dstb,
                                     preferred_element_type=F32).reshape(TM, HG_DIM)
                dk = dkd2 * e2
                db = -(kd2 * dkd2)
                if latent:
                    eb = jnp.exp(b)
                    enb = jnp.exp(-b)
                    qdf = qr * sq * HG_DIM ** -0.5 * eb
                    kdf = k * enb
                    qd = qd_s[d, pl.ds(rl, TM), :]
                    kd = kdf.astype(BF16)
                    a = jnp.where(tri[d], _dot_nt(qd, kd), 0.0).astype(BF16)
                    da = jnp.where(tri[d], da_full, 0.0).astype(BF16)
                    stb = st_ref[d, pl.ds(r * cpt, cpt)]
                    dqd = _dot(da, kd) + jnp.einsum(
                        'ncv,nvk->nck', do.reshape(cpt, CHUNK, HG_DIM), stb,
                        preferred_element_type=F32).reshape(TM, HG_DIM)
                    dkd = _dot_tn(da, qd)
                    dv = dv + _dot_tn(a, do)
                    dk = dk + dkd * enb
                    db = db + qdf * dqd - kdf * dkd
                    dq = dq + dqd * eb
                dg = _dot_lhs01(later01[d], db) + dbt_s[d, pl.ds(r0, TM), :]
                df = dg / f - dk
                dp_ref[pl.ds(r0, TM), d * HG_DIM:(d + 1) * HG_DIM] = (
                    df * (1.0 - lb[d]) * sz * (1.0 - sz)).astype(BF16)
                dlbs.append(jnp.sum(df * (1.0 - sz), axis=0, keepdims=True))
            dp_ref[pl.ds(r0, TM), 2 * HG_DIM:3 * HG_DIM] = dv.astype(BF16)
            if latent:
                dq = dq * (HG_DIM ** -0.5) * (sq * (1.0 + qr * (1.0 - sq)))
            dp_ref[pl.ds(r0, TM), 3 * HG_DIM:4 * HG_DIM] = dq.astype(BF16)
            return dlbs

        dlb_ctx = grad_tile(0, False)

        def grads(r, acc):
            t = grad_tile(r, True)
            return (acc[0] + t[0], acc[1] + t[1])

        dlb = lax.fori_loop(1, N_TILES, grads, (dlb_ctx[0], dlb_ctx[1]))
        dlb_ref[0:1, :] = dlb[0]
        dlb_ref[1:2, :] = dlb[1]

    return _pcall(
        body, carried, name="hgrn_bwd", grid=(HG_HEADS,),
        in_specs=[pl.BlockSpec((T, 4 * HG_DIM), lambda h: (0, h)),
                  pl.BlockSpec((2, 2, HG_DIM), lambda h: (0, 0, h)),
                  pl.BlockSpec((S, HG_DIM), lambda h: (0, h)),
                  pl.BlockSpec((2, None, N_CHUNKS, HG_DIM, HG_DIM), lambda h: (0, h, 0, 0, 0))],
        out_specs=[pl.BlockSpec((T, 4 * HG_DIM), lambda h: (0, h)),
                   pl.BlockSpec((2, HG_DIM), lambda h: (0, h))],
        out_shape=[jax.ShapeDtypeStruct((T, WA), BF16), jax.ShapeDtypeStruct((2, HGW), F32)],
        scratch_shapes=[pltpu.VMEM((2, T, HG_DIM), F32), pltpu.VMEM((2, T, HG_DIM), F32),
                        pltpu.VMEM((2, T, HG_DIM), F32), pltpu.VMEM((2, S, HG_DIM), BF16),
                        pltpu.VMEM((2, N_CHUNKS, HG_DIM, HG_DIM), BF16),
                        pltpu.VMEM((2, N_CHUNKS, HG_DIM, HG_DIM), BF16)],
        operands=[p_a, lbl, d_o, st])


def _rope_tables():
    t = np.arange(S)
    inv = ROPE_THETA ** (-np.arange(0, 32, 2, dtype=np.float64) / 32)
    lane = np.arange(64)
    pos = np.where(lane[None, :] < 32, (t // GRID_W)[:, None], (t % GRID_W)[:, None]).astype(np.float64)
    ang = pos * inv[(lane % 32) % 16][None, :]
    sign = np.where((lane % 32) < 16, -1.0, 1.0)[None, :]
    cos = np.tile(np.cos(ang), (1, 2)).astype(np.float32)
    sin = np.tile(np.sin(ang) * sign, (1, 2)).astype(np.float32)
    return jnp.asarray(cos), jnp.asarray(sin)


def _rope_partner(v):
    lane = lax.broadcasted_iota(jnp.int32, (1, 128), 1)
    first = (lane % 32) < 16
    slabs = []
    for j in range(v.shape[1] // 128):
        s = v[:, 128 * j:128 * (j + 1)]
        slabs.append(jnp.where(first, pltpu.roll(s, 112, 1), pltpu.roll(s, 16, 1)))
    return slabs[0] if len(slabs) == 1 else jnp.concatenate(slabs, axis=1)


def _group_ones(width, group):
    r = lax.broadcasted_iota(jnp.int32, (width, width), 0)
    c = lax.broadcasted_iota(jnp.int32, (width, width), 1)
    return jnp.where((r // group) == (c // group), 1.0, 0.0).astype(BF16)


def _group_mean(v, ones01, group):
    hi = v.astype(BF16)
    lo = (v - hi.astype(F32)).astype(BF16)
    return (_dot(hi, ones01) + _dot(lo, ones01)) * (1.0 / group)


def _rep_matrix():
    r = lax.broadcasted_iota(jnp.int32, (KVW, ATW), 0)
    c = lax.broadcasted_iota(jnp.int32, (KVW, ATW), 1)
    return jnp.where(r == HEAD_DIM * (c // 256) + c % HEAD_DIM, 1.0, 0.0).astype(BF16)


def _tile_lanes(v, reps):
    return jnp.concatenate([v] * reps, axis=1)


def _prep_fwd(p_b, o, cos, sin, hnw, qnw, knw):
    def body(p_ref, o_ref, cos_ref, sin_ref, hnw_ref, qnw_ref, knw_ref, y_ref, q_ref, k_ref, v_ref):
        i = pl.program_id(0)
        rep = _rep_matrix()
        ones_k = _group_ones(KVW, HEAD_DIM)
        kr = p_ref[:, 1024:1152]
        krstd = lax.rsqrt(_group_mean(kr * kr, ones_k, HEAD_DIM) + EPS)
        kn = kr * krstd * knw_ref[...]
        v_ref[...] = _dot(p_ref[:, 1152:1280].astype(BF16), rep).astype(BF16)

        @pl.when(i == 0)
        def _():
            k_ref[...] = _dot(kn.astype(BF16), rep).astype(BF16)

        @pl.when(i > 0)
        def _():
            cs, sn = cos_ref[...], sin_ref[...]
            kro = kn * cs + _rope_partner(kn) * sn
            k_ref[...] = _dot(kro.astype(BF16), rep).astype(BF16)
            qr = p_ref[:, 512:1024]
            qrstd = lax.rsqrt(_group_mean(qr * qr, _group_ones(ATW, HEAD_DIM), HEAD_DIM) + EPS)
            qn = qr * qrstd * qnw_ref[...]
            qro = qn * _tile_lanes(cs, 4) + _rope_partner(qn) * _tile_lanes(sn, 4)
            q_ref[...] = (qro * HEAD_DIM ** -0.5).astype(BF16)
            ys = []
            for h in range(HG_HEADS):
                oh = o_ref[:, HG_DIM * h:HG_DIM * (h + 1)]
                gh = p_ref[:, HG_DIM * h:HG_DIM * (h + 1)]
                rstd = lax.rsqrt(jnp.mean(oh * oh, axis=-1, keepdims=True) + EPS)
                ys.append(oh * rstd * hnw_ref[...] * (gh * _sigmoid(gh)))
            y_ref[...] = jnp.concatenate(ys, axis=1).astype(BF16)

    return pl.pallas_call(
        body, name="prep_fwd", grid=(N_TILES,),
        in_specs=[pl.BlockSpec((TM, WB), lambda i: (i, 0)),
                  pl.BlockSpec((TM, HGW), lambda i: (_lat(i), 0)),
                  pl.BlockSpec((TM, 128), lambda i: (_lat(i), 0)),
                  pl.BlockSpec((TM, 128), lambda i: (_lat(i), 0)),
                  _full((1, HG_DIM)), _full((1, ATW)), _full((1, KVW))],
        out_specs=[pl.BlockSpec((TM, HGW), lambda i: (_lat(i), 0)),
                   pl.BlockSpec((TM, ATW), lambda i: (_lat(i), 0)),
                   pl.BlockSpec((TM, ATW), lambda i: (i, 0)),
                   pl.BlockSpec((TM, ATW), lambda i: (i, 0))],
        out_shape=[jax.ShapeDtypeStruct((S, HGW), BF16), jax.ShapeDtypeStruct((S, ATW), BF16),
                   jax.ShapeDtypeStruct((T, ATW), BF16), jax.ShapeDtypeStruct((T, ATW), BF16)],
        compiler_params=_cp(("arbitrary",)),
    )(p_b, o, cos, sin, hnw, qnw, knw)


def _prep_bwd(p_b, o, cos, sin, hnw, qnw, knw, dy_hg, dq, dk_rep, dv_rep, carried=None):
    def body(p_ref, o_ref, cos_ref, sin_ref, hnw_ref, qnw_ref, knw_ref, dy_ref, dq_ref, dk_ref, dv_ref,
             dp_ref, do_ref, acc_ref):
        i = pl.program_id(0)

        @pl.when(i == 0)
        def _():
            acc_ref[...] = jnp.zeros_like(acc_ref)

        rep = _rep_matrix()
        ones_k = _group_ones(KVW, HEAD_DIM)

        def fold(v):
            hi = v.astype(BF16)
            lo = (v - hi.astype(F32)).astype(BF16)
            return _dot_nt(hi, rep) + _dot_nt(lo, rep)

        kr = p_ref[:, 1024:1152]
        krstd = lax.rsqrt(_group_mean(kr * kr, ones_k, HEAD_DIM) + EPS)
        khat = kr * krstd
        kw = knw_ref[...]
        dkro = fold(dk_ref[...])
        dv = fold(dv_ref[...])

        def k_back(dkn):
            dkhat = dkn * kw
            dkr = krstd * (dkhat - khat * _group_mean(dkhat * khat, ones_k, HEAD_DIM))
            acc_ref[2:3, 0:KVW] += jnp.sum(dkn * khat, axis=0, keepdims=True)
            dp_ref[:, 1024:1152] = dkr.astype(BF16)
            dp_ref[:, 1152:1280] = dv.astype(BF16)

        @pl.when(i == 0)
        def _():
            k_back(dkro)
            dp_ref[:, 0:1024] = jnp.zeros((TM, 1024), BF16)

        @pl.when(i > 0)
        def _():
            cs, sn = cos_ref[...], sin_ref[...]
            k_back(dkro * cs + _rope_partner(dkro * sn))
            ones_q = _group_ones(ATW, HEAD_DIM)
            qr = p_ref[:, 512:1024]
            qrstd = lax.rsqrt(_group_mean(qr * qr, ones_q, HEAD_DIM) + EPS)
            qhat = qr * qrstd
            dqro = dq_ref[...] * HEAD_DIM ** -0.5
            dqn = dqro * _tile_lanes(cs, 4) + _rope_partner(dqro * _tile_lanes(sn, 4))
            dqhat = dqn * qnw_ref[...]
            dqr = qrstd * (dqhat - qhat * _group_mean(dqhat * qhat, ones_q, HEAD_DIM))
            acc_ref[1:2, :] += jnp.sum(dqn * qhat, axis=0, keepdims=True)
            dp_ref[:, 512:1024] = dqr.astype(BF16)
            dws = jnp.zeros((1, HG_DIM), F32)
            for h in range(HG_HEADS):
                sl = slice(HG_DIM * h, HG_DIM * (h + 1))
                oh, gh, dy = o_ref[:, sl], p_ref[:, sl], dy_ref[:, sl]
                rstd = lax.rsqrt(jnp.mean(oh * oh, axis=-1, keepdims=True) + EPS)
                ohat = oh * rstd
                sg = _sigmoid(gh)
                dp_ref[:, sl] = (dy * (ohat * hnw_ref[...]) * (sg * (1.0 + gh * (1.0 - sg)))).astype(BF16)
                dn = dy * (gh * sg)
                dws = dws + jnp.sum(dn * ohat, axis=0, keepdims=True)
                dohat = dn * hnw_ref[...]
                do_ref[:, sl] = rstd * (dohat - ohat * jnp.mean(dohat * ohat, axis=-1, keepdims=True))
            acc_ref[0:1, 0:HG_DIM] += dws

    return _pcall(
        body, carried, name="prep_bwd", grid=(N_TILES,),
        in_specs=[pl.BlockSpec((TM, WB), lambda i: (i, 0)),
                  pl.BlockSpec((TM, HGW), lambda i: (_lat(i), 0)),
                  pl.BlockSpec((TM, 128), lambda i: (_lat(i), 0)),
                  pl.BlockSpec((TM, 128), lambda i: (_lat(i), 0)),
                  _full((1, HG_DIM)), _full((1, ATW)), _full((1, KVW)),
                  pl.BlockSpec((TM, HGW), lambda i: (_lat(i), 0)),
                  pl.BlockSpec((TM, ATW), lambda i: (_lat(i), 0)),
                  pl.BlockSpec((TM, ATW), lambda i: (i, 0)),
                  pl.BlockSpec((TM, ATW), lambda i: (i, 0))],
        out_specs=[pl.BlockSpec((TM, WB), lambda i: (i, 0)),
                   pl.BlockSpec((TM, HGW), lambda i: (_lat(i), 0)),
                   _full((8, ATW))],
        out_shape=[jax.ShapeDtypeStruct((T, WB), BF16), jax.ShapeDtypeStruct((S, HGW), F32),
                   jax.ShapeDtypeStruct((8, ATW), F32)],
        scratch_shapes=[], operands=[p_b, o, cos, sin, hnw, qnw, knw, dy_hg, dq, dk_rep, dv_rep])


NEG = -1e30
_CTX_BLOCKS = L // BLOCK


def _attn_window_specs():
    prev = pl.BlockSpec((BLOCK, ATW), lambda i: (jnp.maximum(i - 1, 0) + _CTX_BLOCKS, 0))
    own = pl.BlockSpec((BLOCK, ATW), lambda i: (i + _CTX_BLOCKS, 0))
    nxt = pl.BlockSpec((BLOCK, ATW), lambda i: (jnp.minimum(i + 1, N_BLOCKS - 1) + _CTX_BLOCKS, 0))
    return [prev, own, nxt, _full((L, ATW))]


def _attn_valid(i, heads=4):
    qi = lax.broadcasted_iota(jnp.int32, (heads * BLOCK, 3 * BLOCK + L), 0) % BLOCK
    kj = lax.broadcasted_iota(jnp.int32, (heads * BLOCK, 3 * BLOCK + L), 1)
    window = ((jnp.abs(kj - BLOCK - qi) <= BLOCK) & ((kj >= BLOCK) | (i > 0))
              & ((kj < 2 * BLOCK) | (i < N_BLOCKS - 1)))
    return window | (kj >= 3 * BLOCK)


def _stack_heads(qg):
    lane = lax.broadcasted_iota(jnp.int32, (1, 256), 1) // HEAD_DIM
    return jnp.concatenate([jnp.where(lane == g, qg, jnp.zeros_like(qg)) for g in range(4)], axis=0)


def _unstack_heads(v4):
    lane = lax.broadcasted_iota(jnp.int32, (1, 256), 1) // HEAD_DIM
    out = jnp.where(lane == 0, v4[0:BLOCK], 0.0)
    for g in range(1, 4):
        out = out + jnp.where(lane == g, v4[g * BLOCK:(g + 1) * BLOCK], 0.0)
    return out


def _sink_rows(sink_ref, hk):
    return jnp.concatenate(
        [jnp.broadcast_to(sink_ref[0:1, 4 * hk + g:4 * hk + g + 1], (BLOCK, 1)) for g in range(4)], axis=0)


def _attn_fwd(q, k_rep, v_rep, sinks, carried=None):
    def body(q_ref, kp, ko, kn, kc, vp, vo, vn, vc, sink_ref, y_ref, lse_ref):
        i = pl.program_id(0)
        valid = _attn_valid(i, 1)
        lane8 = lax.broadcasted_iota(jnp.int32, (1, ATT_HEADS), 1)
        head_of_lane = lax.broadcasted_iota(jnp.int32, (1, 256), 1) // HEAD_DIM
        lse_out = jnp.zeros((BLOCK, ATT_HEADS), F32)
        for hk in range(KV_HEADS):
            sl = slice(256 * hk, 256 * (hk + 1))
            qg = q_ref[:, sl]
            keys = jnp.concatenate([kp[:, sl], ko[:, sl], kn[:, sl], kc[:, sl]], axis=0)
            vals = jnp.concatenate([vp[:, sl], vo[:, sl], vn[:, sl], vc[:, sl]], axis=0)
            yg = jnp.zeros((BLOCK, 256), F32)
            for g in range(4):
                q1 = jnp.where(head_of_lane == g, qg, jnp.zeros_like(qg))
                s = jnp.where(valid, _dot_nt(q1, keys), NEG)
                sink = sink_ref[0:1, 4 * hk + g:4 * hk + g + 1]
                m = jnp.maximum(jnp.max(s, axis=1, keepdims=True), sink)
                p = jnp.exp(s - m)
                den = jnp.sum(p, axis=1, keepdims=True) + jnp.exp(sink - m)
                o1 = _dot(p.astype(BF16), vals) * (1.0 / den)
                yg = yg + jnp.where(head_of_lane == g, o1, 0.0)
                lse_out = lse_out + jnp.where(lane8 == 4 * hk + g, m + jnp.log(den), 0.0)
            y_ref[:, sl] = yg.astype(BF16)
        lse_ref[...] = lse_out

    return _pcall(
        body, carried, name="attn_fwd", grid=(N_BLOCKS,),
        in_specs=[pl.BlockSpec((BLOCK, ATW), lambda i: (i, 0))] + _attn_window_specs()
        + _attn_window_specs() + [_full((1, ATT_HEADS))],
        out_specs=[pl.BlockSpec((BLOCK, ATW), lambda i: (i, 0)),
                   pl.BlockSpec((BLOCK, ATT_HEADS), lambda i: (i, 0))],
        out_shape=[jax.ShapeDtypeStruct((S, ATW), BF16), jax.ShapeDtypeStruct((S, ATT_HEADS), F32)],
        scratch_shapes=[],
        operands=[q, k_rep, k_rep, k_rep, k_rep, v_rep, v_rep, v_rep, v_rep, sinks])


def _attn_bwd(q, k_rep, v_rep, sinks, y_at, lse, dy, carried=None):
    def body(q_ref, kp, ko, kn, kc, vp, vo, vn, vc, sink_ref, y_ref, lse_ref, dy_ref,
             dq_ref, dk_ref, dv_ref, dsink_ref, dk_acc, dv_acc):
        i = pl.program_id(0)

        @pl.when(i == 0)
        def _():
            dk_acc[...] = jnp.zeros_like(dk_acc)
            dv_acc[...] = jnp.zeros_like(dv_acc)
            dk_ref[pl.ds(0, L), :] = jnp.zeros((L, ATW), F32)
            dv_ref[pl.ds(0, L), :] = jnp.zeros((L, ATW), F32)
            dsink_ref[...] = jnp.zeros_like(dsink_ref)

        valid = _attn_valid(i)
        lane8 = lax.broadcasted_iota(jnp.int32, (1, ATT_HEADS), 1)
        w0 = pl.multiple_of(i * BLOCK, BLOCK)
        dsink = jnp.zeros((1, ATT_HEADS), F32)
        for hk in range(KV_HEADS):
            sl = slice(256 * hk, 256 * (hk + 1))
            q4 = _stack_heads(q_ref[:, sl])
            do4f = _stack_heads(dy_ref[:, sl])
            o4 = _stack_heads(y_ref[:, sl]).astype(F32)
            do4 = do4f.astype(BF16)
            keys = jnp.concatenate([kp[:, sl], ko[:, sl], kn[:, sl], kc[:, sl]], axis=0)
            vals = jnp.concatenate([vp[:, sl], vo[:, sl], vn[:, sl], vc[:, sl]], axis=0)
            lse4 = jnp.concatenate(
                [jnp.sum(jnp.where(lane8 == 4 * hk + g, lse_ref[...], 0.0), axis=1, keepdims=True)
                 for g in range(4)], axis=0)
            p = jnp.where(valid, jnp.exp(_dot_nt(q4, keys) - lse4), 0.0)
            delta = jnp.sum(do4f * o4, axis=1, keepdims=True)
            ds = (p * (_dot_nt(do4, vals) - delta)).astype(BF16)
            dq_ref[:, sl] = _unstack_heads(_dot(ds, keys))
            dk = _dot_tn(ds, q4)
            dv = _dot_tn(p.astype(BF16), do4)
            dk_acc[pl.ds(w0, 3 * BLOCK), sl] += dk[:3 * BLOCK]
            dv_acc[pl.ds(w0, 3 * BLOCK), sl] += dv[:3 * BLOCK]
            dk_ref[pl.ds(0, L), sl] += dk[3 * BLOCK:]
            dv_ref[pl.ds(0, L), sl] += dv[3 * BLOCK:]
            p_sink = jnp.exp(_sink_rows(sink_ref, hk) - lse4)
            for g in range(4):
                rows = slice(g * BLOCK, (g + 1) * BLOCK)
                dsink = dsink + jnp.where(lane8 == 4 * hk + g,
                                          -jnp.sum(p_sink[rows] * delta[rows], axis=0, keepdims=True), 0.0)
        dsink_ref[...] += dsink

        @pl.when(i == N_BLOCKS - 1)
        def _():
            dk_ref[pl.ds(L, S), :] = dk_acc[pl.ds(BLOCK, S), :]
            dv_ref[pl.ds(L, S), :] = dv_acc[pl.ds(BLOCK, S), :]

    row_q = pl.BlockSpec((BLOCK, ATW), lambda i: (i, 0))
    return _pcall(
        body, carried, name="attn_bwd", grid=(N_BLOCKS,),
        in_specs=[row_q] + _attn_window_specs() + _attn_window_specs()
        + [_full((1, ATT_HEADS)), row_q, pl.BlockSpec((BLOCK, ATT_HEADS), lambda i: (i, 0)), row_q],
        out_specs=[row_q, _full((T, ATW)), _full((T, ATW)), _full((1, ATT_HEADS))],
        out_shape=[jax.ShapeDtypeStruct((S, ATW), F32), jax.ShapeDtypeStruct((T, ATW), F32),
                   jax.ShapeDtypeStruct((T, ATW), F32), jax.ShapeDtypeStruct((1, ATT_HEADS), F32)],
        scratch_shapes=[pltpu.VMEM((S + 2 * BLOCK, ATW), F32), pltpu.VMEM((S + 2 * BLOCK, ATW), F32)],
        operands=[q, k_rep, k_rep, k_rep, k_rep, v_rep, v_rep, v_rep, v_rep, sinks, y_at, lse, dy])


def _merge_fwd(y_hg, y_at, p_c, x, w_bh, w_ba, w_out, g1, nfw, sh2, sc2, carried=None):
    def body(yh_ref, ya_ref, g_ref, x_ref, wbh_ref, wba_ref, wo_ref, g1_ref, nfw_ref, sh_ref, sc_ref,
             a_ref, b_ref, mx_ref, r_ref, x1_ref, h2_ref):
        a = _dot_nt(yh_ref[...], wbh_ref[...])
        b = _dot_nt(ya_ref[...], wba_ref[...])
        mixed = (_sigmoid(g_ref[:, :D]) * a + _sigmoid(g_ref[:, D:]) * b).astype(BF16)
        r = _dot(mixed, wo_ref[...])
        x1 = x_ref[...] + g1_ref[...] * r
        a_ref[...] = a
        b_ref[...] = b
        mx_ref[...] = mixed
        r_ref[...] = r
        x1_ref[...] = x1
        h2_ref[...] = _rms_mod(x1, nfw_ref[...], sh_ref[...], sc_ref[...]).astype(BF16)

    row = lambda w: pl.BlockSpec((TM, w), lambda i: (i, 0))
    vec = _full((1, D))
    return _pcall(
        body, carried, name="merge_fwd", grid=(N_LAT_TILES,),
        in_specs=[row(HGW), row(ATW), row(WC), row(D), _VMEM_WHOLE, _VMEM_WHOLE, _VMEM_WHOLE,
                  vec, vec, vec, vec],
        out_specs=[row(D)] * 6,
        out_shape=[jax.ShapeDtypeStruct((S, D), dt) for dt in (F32, F32, BF16, F32, F32, BF16)],
        scratch_shapes=[], operands=[y_hg, y_at, p_c, x, w_bh, w_ba, w_out, g1, nfw, sh2, sc2])


def _merge_bwd(dx1, r, a, b, p_c, w_bh, w_ba, w_out, g1, carried=None):
    def body(dx_ref, r_ref, a_ref, b_ref, g_ref, wbh_ref, wba_ref, wo_ref, g1_ref,
             dr_ref, da_ref, db_ref, dg_ref, dyh_ref, dya_ref, acc_ref):
        @pl.when(pl.program_id(0) == 0)
        def _():
            acc_ref[...] = jnp.zeros_like(acc_ref)

        dx1v = dx_ref[...]
        acc_ref[0:1, :] += jnp.sum(dx1v * r_ref[...], axis=0, keepdims=True)
        dr = (g1_ref[...] * dx1v).astype(BF16)
        dr_ref[...] = dr
        dmix = _dot_nt(dr, wo_ref[...])
        sh, sa = _sigmoid(g_ref[:, :D]), _sigmoid(g_ref[:, D:])
        da = (dmix * sh).astype(BF16)
        db = (dmix * sa).astype(BF16)
        da_ref[...] = da
        db_ref[...] = db
        dg_ref[:, :D] = (dmix * a_ref[...] * sh * (1.0 - sh)).astype(BF16)
        dg_ref[:, D:] = (dmix * b_ref[...] * sa * (1.0 - sa)).astype(BF16)
        dyh_ref[...] = _dot(da, wbh_ref[...])
        dya_ref[...] = _dot(db, wba_ref[...])

    row = lambda w: pl.BlockSpec((TM, w), lambda i: (i, 0))
    return _pcall(
        body, carried, name="merge_bwd", grid=(N_LAT_TILES,),
        in_specs=[row(D), row(D), row(D), row(D), row(WC), _VMEM_WHOLE, _VMEM_WHOLE, _VMEM_WHOLE,
                  _full((1, D))],
        out_specs=[row(D), row(D), row(D), row(WC), row(HGW), row(ATW), _full((8, D))],
        out_shape=[jax.ShapeDtypeStruct((S, D), BF16), jax.ShapeDtypeStruct((S, D), BF16),
                   jax.ShapeDtypeStruct((S, D), BF16), jax.ShapeDtypeStruct((S, WC), BF16),
                   jax.ShapeDtypeStruct((S, HGW), F32), jax.ShapeDtypeStruct((S, ATW), F32),
                   jax.ShapeDtypeStruct((8, D), F32)],
        scratch_shapes=[], operands=[dx1, r, a, b, p_c, w_bh, w_ba, w_out, g1])


def _ffn_fused(x1, h2, tgt, w_gate, w_up, w_down, g2, nfw, sc2):
    def body(x1_ref, h2_ref, t_ref, wg_ref, wu_ref, wd_ref, g2_ref, nfw_ref, sc_ref,
             act_ref, dgt_ref, dup_ref, df_ref, dx_ref, acc_ref, gs, us):
        @pl.when(pl.program_id(0) == 0)
        def _():
            acc_ref[...] = jnp.zeros_like(acc_ref)

        h2 = h2_ref[...]
        whole = lambda w_ref: w_ref[...].reshape(D_FF, D)
        wide = lambda t_ref: jnp.concatenate([t_ref[j] for j in range(N_FF_TILES)], axis=1)
        for j in range(N_FF_TILES):
            g = _dot_nt(h2, wg_ref[j])
            u = _dot_nt(h2, wu_ref[j])
            gs[j] = g
            us[j] = u
            act_ref[j] = (g * _sigmoid(g) * u).astype(BF16)
        f = _dot(wide(act_ref), whole(wd_ref))
        x1v = x1_ref[...]
        g2 = g2_ref[...]
        diff = x1v + g2 * f - t_ref[...]
        dy = diff * (1.0 / D)
        df = (g2 * dy).astype(BF16)
        df_ref[...] = df
        dact_all = _dot_nt(df, whole(wd_ref))
        for j in range(N_FF_TILES):
            g, u = gs[j], us[j]
            sg = _sigmoid(g)
            dact = dact_all[:, j * FF_TILE:(j + 1) * FF_TILE]
            dgt_ref[j] = (dact * u * (sg * (1.0 + g * (1.0 - sg)))).astype(BF16)
            dup_ref[j] = (dact * (g * sg)).astype(BF16)
        dh2 = _dot(wide(dgt_ref), whole(wg_ref)) + _dot(wide(dup_ref), whole(wu_ref))
        dx, dsh, dsc, dnw = _rms_mod_bwd(x1v, nfw_ref[...], sc_ref[...], dh2)
        dx_ref[...] = dy + dx
        acc_ref[0:1, :] += dsh
        acc_ref[1:2, :] += dsc
        acc_ref[2:3, :] += dnw
        acc_ref[3:4, :] += jnp.sum(dy * f, axis=0, keepdims=True)
        acc_ref[4:5, :] += 0.5 * jnp.sum(jnp.sum(diff * diff, axis=1, keepdims=True), axis=0,
                                         keepdims=True) * (1.0 / D)

    row = lambda dt_w: pl.BlockSpec((TM, dt_w), lambda i: (i, 0))
    blk = pl.BlockSpec((N_FF_TILES, TM, FF_TILE), lambda i: (0, i, 0))
    vec = _full((1, D))
    return pl.pallas_call(
        body, name="ffn_fused", grid=(N_LAT_TILES,),
        in_specs=[row(D), row(D), row(D), _VMEM_WHOLE, _VMEM_WHOLE, _VMEM_WHOLE, vec, vec, vec],
        out_specs=[blk, blk, blk, row(D), row(D), _full((8, D))],
        out_shape=[jax.ShapeDtypeStruct((N_FF_TILES, S, FF_TILE), BF16)] * 3
        + [jax.ShapeDtypeStruct((S, D), BF16), jax.ShapeDtypeStruct((S, D), F32),
           jax.ShapeDtypeStruct((8, D), F32)],
        scratch_shapes=[pltpu.VMEM((N_FF_TILES, TM, FF_TILE), F32), pltpu.VMEM((N_FF_TILES, TM, FF_TILE), F32)],
        compiler_params=_cp(("arbitrary",)),
    )(x1, h2, tgt, w_gate, w_up, w_down, g2, nfw, sc2)


def _proj_bc(h_all, w_b, w_c, carried=None):
    def body(h_ref, wb_ref, wc_ref, pb_ref, pc_ref):
        h = h_ref[...]
        pb_ref[...] = _dot_nt(h, wb_ref[...])

        @pl.when(pl.program_id(0) > 0)
        def _():
            pc_ref[...] = _dot_nt(h, wc_ref[...])

    return _pcall(
        body, carried, name="proj_bc", grid=(N_TILES,),
        in_specs=[pl.BlockSpec((TM, D), lambda i: (i, 0)), _VMEM_WHOLE, _VMEM_WHOLE],
        out_specs=[pl.BlockSpec((TM, WB), lambda i: (i, 0)), pl.BlockSpec((TM, WC), lambda i: (_lat(i), 0))],
        out_shape=[jax.ShapeDtypeStruct((T, WB), F32), jax.ShapeDtypeStruct((S, WC), F32)],
        scratch_shapes=[], operands=[h_all, w_b, w_c])


def _input_bwd(dp_a, dp_b, dp_c, w_a, w_b, w_c, ctx, x, dx1, nw, sh, sc, carried=None):
    def body(da_ref, db_ref, dc_ref, wa_ref, wb_ref, wc_ref, ctx_ref, x_ref, dx1_ref, nw_ref, sh_ref,
             sc_ref, gx_ref, acc_ref):
        i = pl.program_id(0)

        @pl.when(i == 0)
        def _():
            acc_ref[...] = jnp.zeros_like(acc_ref)

        dh = _dot(da_ref[...], wa_ref[...]) + _dot(db_ref[...], wb_ref[...])

        @pl.when(i == 0)
        def _():
            _, dsh, dsc, dnw = _rms_mod_bwd(ctx_ref[...], nw_ref[...], sc_ref[0:1, :], dh)
            acc_ref[3:4, :] += dsh
            acc_ref[4:5, :] += dsc
            acc_ref[2:3, :] += dnw

        @pl.when(i > 0)
        def _():
            dhl = dh + _dot(dc_ref[...], wc_ref[...])
            dx, dsh, dsc, dnw = _rms_mod_bwd(x_ref[...], nw_ref[...], sc_ref[1:2, :], dhl)
            gx_ref[...] = dx1_ref[...] + dx
            acc_ref[0:1, :] += dsh
            acc_ref[1:2, :] += dsc
            acc_ref[2:3, :] += dnw

    lat = lambda w: pl.BlockSpec((TM, w), lambda i: (_lat(i), 0))
    return _pcall(
        body, carried, name="input_bwd", grid=(N_TILES,),
        in_specs=[pl.BlockSpec((TM, WA), lambda i: (i, 0)), pl.BlockSpec((TM, WB), lambda i: (i, 0)),
                  lat(WC), _VMEM_WHOLE, _VMEM_WHOLE, _VMEM_WHOLE, _full((TM, D)), lat(D), lat(D),
                  _full((1, D)), _full((2, D)), _full((2, D))],
        out_specs=[lat(D), _full((8, D))],
        out_shape=[jax.ShapeDtypeStruct((S, D), F32), jax.ShapeDtypeStruct((8, D), F32)],
        scratch_shapes=[], operands=[dp_a, dp_b, dp_c, w_a, w_b, w_c, ctx, x, dx1, nw, sh, sc])


_C1 = 1.0 - ADAM_B1 ** ADAM_STEP
_C2 = 1.0 - ADAM_B2 ** ADAM_STEP


def _adamw_math(w, g, m, v):
    m = ADAM_B1 * m + (1.0 - ADAM_B1) * g
    v = ADAM_B2 * v + (1.0 - ADAM_B2) * (g * g)
    m_hat = m / _C1
    v_hat = v / _C2
    delta = -ADAM_LR * (m_hat / (jnp.sqrt(v_hat) + ADAM_EPS) + ADAM_WD * w)
    return delta, m, v


def _adamw_sharded(terms, w, m, v, name, tr):
    rows, cols = w.shape

    def body(t_ref, w_ref, m_ref, v_ref, g_ref, d_ref, nm_ref, nv_ref):
        g = t_ref[0].astype(F32)
        for s in range(1, N_CHIPS):
            g = g + t_ref[s].astype(F32)
        g_ref[...] = g
        d_ref[...], nm_ref[...], nv_ref[...] = _adamw_math(w_ref[...], g, m_ref[...], v_ref[...])

    blk = pl.BlockSpec((tr, cols), lambda i: (i, 0))
    return pl.pallas_call(
        body, name=name, grid=(rows // tr,),
        in_specs=[pl.BlockSpec((N_CHIPS, tr, cols), lambda i: (0, i, 0)), blk, blk, blk],
        out_specs=[blk] * 4,
        out_shape=[jax.ShapeDtypeStruct((rows, cols), F32)] * 4,
        compiler_params=_cp(("parallel",)),
    )(terms, w, m, v)


def _adamw_plain(g, w, m, v, name):
    def body(g_ref, w_ref, m_ref, v_ref, d_ref, nm_ref, nv_ref):
        d_ref[...], nm_ref[...], nv_ref[...] = _adamw_math(w_ref[...], g_ref[...], m_ref[...], v_ref[...])

    return pl.pallas_call(
        body, name=name, in_specs=[_VMEM_WHOLE] * 4, out_specs=[_VMEM_WHOLE] * 3,
        out_shape=[jax.ShapeDtypeStruct(w.shape, F32)] * 3,
        compiler_params=_cp(),
    )(g, w, m, v)


SMALL_ROWS = 16
R_DMOD, R_DCTX, R_NMIX, R_NFFN, R_MISC, R_DLB, R_BADA01 = 0, 6, 8, 9, 10, 11, 13
M_HNW, M_QNW, M_KNW, M_SINK, M_LOSS = 0, 128, 256, 384, 512


def _pack_small(acc_in, acc_mg, acc_ffn, acc_prep, dsink, dlb):
    def body(in_ref, mg_ref, ff_ref, pp_ref, ds_ref, dlb_ref, o_ref):
        o_ref[...] = jnp.zeros_like(o_ref)
        o_ref[0:2, :] = in_ref[0:2, :]
        o_ref[2:3, :] = mg_ref[0:1, :]
        o_ref[3:5, :] = ff_ref[0:2, :]
        o_ref[5:6, :] = ff_ref[3:4, :]
        o_ref[6:8, :] = in_ref[3:5, :]
        o_ref[8:9, :] = in_ref[2:3, :]
        o_ref[9:10, :] = ff_ref[2:3, :]
        o_ref[10:11, M_HNW:M_HNW + HG_DIM] = pp_ref[0:1, 0:HG_DIM]
        r = lax.broadcasted_iota(jnp.int32, (ATW, 128), 0)
        c = lax.broadcasted_iota(jnp.int32, (ATW, 128), 1)
        fold = jnp.where((r % HEAD_DIM == c) & (c < HEAD_DIM), 1.0, 0.0).astype(BF16)
        qk = jnp.concatenate([pp_ref[1:2, :], pp_ref[2:3, :], jnp.zeros((6, ATW), F32)], axis=0)
        folded = _dot_exact_rhs01(qk, fold)
        o_ref[10:11, M_QNW:M_QNW + 128] = folded[0:1, :]
        o_ref[10:11, M_KNW:M_KNW + 128] = folded[1:2, :]
        o_ref[10:11, M_SINK:M_SINK + ATT_HEADS] = ds_ref[...]
        o_ref[10:11, M_LOSS:M_LOSS + 128] = ff_ref[4:5, 0:128]
        o_ref[11:13, 0:HGW] = dlb_ref[...]

    return pl.pallas_call(
        body, name="pack_small", in_specs=[_VMEM_WHOLE] * 6, out_specs=_VMEM_WHOLE,
        out_shape=jax.ShapeDtypeStruct((SMALL_ROWS, D), F32), compiler_params=_cp(),
    )(acc_in, acc_mg, acc_ffn, acc_prep, dsink, dlb)


def _sum_small(gathered):
    def body(g_ref, o_ref):
        tot = g_ref[0]
        for s in range(1, N_DEV):
            tot = tot + g_ref[s]
        o_ref[...] = tot
        o_ref[R_BADA01:R_BADA01 + 2, :] = tot[0:2, :] + tot[R_DCTX:R_DCTX + 2, :]

    return pl.pallas_call(
        body, name="sum_small", in_specs=[_VMEM_WHOLE], out_specs=_VMEM_WHOLE,
        out_shape=jax.ShapeDtypeStruct((SMALL_ROWS, D), F32), compiler_params=_cp(),
    )(gathered)


_REP_NAMES = ("b_ada", "c_ctx", "norm_mix_w", "norm_ffn_w", "hgrn_norm_w", "q_norm_w", "k_norm_w", "attn_sinks")


def _adamw_replicated(tot, g_c_ctx, ws, ms, vs):
    n = len(_REP_NAMES)

    def body(*refs):
        tot_ref, gc_ref = refs[0], refs[1]
        w_refs, m_refs, v_refs = refs[2:2 + n], refs[2 + n:2 + 2 * n], refs[2 + 2 * n:2 + 3 * n]
        outs = refs[2 + 3 * n:]
        row = lambda r: tot_ref[r:r + 1, :]
        misc = row(R_MISC)
        grads = [jnp.concatenate([row(R_BADA01), row(R_BADA01 + 1)] + [row(k) for k in range(2, 6)], axis=1),
                 gc_ref[...], row(R_NMIX), row(R_NFFN),
                 misc[:, M_HNW:M_HNW + HG_DIM], misc[:, M_QNW:M_QNW + HEAD_DIM],
                 misc[:, M_KNW:M_KNW + HEAD_DIM], misc[:, M_SINK:M_SINK + ATT_HEADS]]
        for k in range(n):
            outs[k][...] = grads[k]
            outs[n + k][...], outs[2 * n + k][...], outs[3 * n + k][...] = _adamw_math(
                w_refs[k][...], grads[k], m_refs[k][...], v_refs[k][...])

    shapes = [jax.ShapeDtypeStruct(w.shape, F32) for w in ws]
    return pl.pallas_call(
        body, name="adamw_replicated", in_specs=[_VMEM_WHOLE] * (2 + 3 * n), out_specs=[_VMEM_WHOLE] * (4 * n),
        out_shape=shapes * 4, compiler_params=_cp(),
    )(tot, g_c_ctx, *ws, *ms, *vs)


def _lb_grads(dlb, lbl):
    def body(d_ref, l_ref, o_ref):
        for d in (0, 1):
            ll = l_ref[d]
            lb = _sigmoid(ll[0:1, :] - ll[1:2, :])
            t = d_ref[d:d + 1, :] * lb * (1.0 - lb)
            o_ref[d, 0:1, :] = t
            o_ref[d, 1:2, :] = -t

    return pl.pallas_call(
        body, name="lb_grads", in_specs=[_VMEM_WHOLE] * 2, out_specs=_VMEM_WHOLE,
        out_shape=jax.ShapeDtypeStruct((2, 2, HGW), F32), compiler_params=_cp(),
    )(dlb, lbl)


def _c_ctx_grad(terms, c_ctx):
    def body(t_ref, c_ref, o_ref):
        tot = t_ref[0, 8:9, :]
        for s in range(1, N_DEV):
            tot = tot + t_ref[s, 8:9, :]
        cv = c_ref[...]
        sg = _sigmoid(cv)
        o_ref[...] = tot * (sg * (1.0 + cv * (1.0 - sg)))

    return pl.pallas_call(
        body, name="c_ctx_grad", in_specs=[_VMEM_WHOLE] * 2, out_specs=_VMEM_WHOLE,
        out_shape=jax.ShapeDtypeStruct((1, D), F32), compiler_params=_cp(),
    )(terms, c_ctx)


def _in_perm():
    fz, bz, inp, kk, vv, qhg, ghg, qat, gates = 0, 512, 1024, 1536, 1664, 1792, 2304, 2816, 3328
    cols = []
    for h in range(HG_HEADS):
        for base in (fz, bz, inp, qhg):
            cols += list(range(base + 128 * h, base + 128 * (h + 1)))
    cols += list(range(ghg, ghg + 512)) + list(range(qat, qat + 512))
    cols += list(range(kk, kk + 128)) + list(range(vv, vv + 128))
    cols += list(range(gates, gates + 2048))
    return np.asarray(cols, np.int32)


_PERM = _in_perm()
_INV_PERM = np.argsort(_PERM).astype(np.int32)


_PIECES = {"a": (0, WA, 128), "b": (WA, WB, 256), "c": (WA + WB, WC, 256)}


def _block_table(piece):
    lo, n, blk = _PIECES[piece]
    starts = [int(_PERM[r]) for r in range(lo, lo + n, blk)]
    assert all(s % blk == 0 and np.array_equal(_PERM[r:r + blk], np.arange(s, s + blk))
               for s, r in zip(starts, range(lo, lo + n, blk)))
    return jnp.asarray([s // blk for s in starts], jnp.int32), blk


def _pick_row_blocks(x, table, blk, name):
    cols = x.shape[1]

    def body(t_ref, x_ref, o_ref):
        o_ref[...] = x_ref[...]

    return pl.pallas_call(
        body, name=name,
        grid_spec=pltpu.PrefetchScalarGridSpec(
            num_scalar_prefetch=1, grid=(table.shape[0],),
            in_specs=[pl.BlockSpec((blk, cols), lambda i, t: (t[i], 0))],
            out_specs=pl.BlockSpec((blk, cols), lambda i, t: (i, 0))),
        out_shape=jax.ShapeDtypeStruct((table.shape[0] * blk, cols), x.dtype),
        compiler_params=_cp(("arbitrary",)),
    )(table, x)


def _place_row_blocks(x, table, blk, into, out_rows, name):
    cols = x.shape[1]

    def body(t_ref, x_ref, *rest):
        rest[-1][...] = x_ref[...]

    operands, in_specs, aliases = [table, x], [pl.BlockSpec((blk, cols), lambda i, t: (i, 0))], {}
    if into is not None:
        operands.append(into)
        in_specs.append(_ANY)
        aliases = {2: 0}
    return pl.pallas_call(
        body, name=name,
        grid_spec=pltpu.PrefetchScalarGridSpec(
            num_scalar_prefetch=1, grid=(table.shape[0],), in_specs=in_specs,
            out_specs=pl.BlockSpec((blk, cols), lambda i, t: (t[i], 0))),
        out_shape=jax.ShapeDtypeStruct((out_rows, cols), x.dtype),
        input_output_aliases=aliases,
        compiler_params=_cp(("arbitrary",)),
    )(*operands)


def _cols_from_blocks(g):
    return jnp.transpose(g, (1, 0, 2)).reshape(g.shape[1], N_DEV * g.shape[2])


def _local_step(x2, ctx2, tgt, lbl, sh_in, sc_in, gate1, sh2, sc2, gate2, norm_mix_w, norm_ffn_w,
                hgrn_norm_w, q_norm_w, k_norm_w, attn_sinks, w_a, w_b, w_c, s_bh, s_ba, s_out,
                s_gate, s_up, s_down):
    first_last = lambda n: [(0, True), (n - 1, False)]
    h_all = _norm_mod_all(ctx2, x2, norm_mix_w, sh_in, sc_in)
    p_a = _mm_nt(h_all, w_a, tm=768, tn=1024, out_dtype=F32, name="proj_a")
    (o, st), (g_gate,) = _hgrn_fwd(
        p_a, lbl, (_gather_comm([s_gate]), [(0, True), (HG_HEADS - 2, True), (HG_HEADS - 1, False)]))
    (p_b, p_c), (g_bh, g_ba, g_out) = _proj_bc(
        h_all, w_b, w_c, (_gather_comm([s_bh, s_ba, s_out]), [(0, True), (N_TILES - 2, True), (N_TILES - 1, False)]))
    cos, sin = _rope_tables()
    qnw_t, knw_t = jnp.tile(q_norm_w, (1, ATT_HEADS)), jnp.tile(k_norm_w, (1, KV_HEADS))
    y_hg, qn, k_rep, v_rep = _prep_fwd(p_b, o, cos, sin, hgrn_norm_w, qnw_t, knw_t)
    (y_at, lse), (g_up, g_down) = _attn_fwd(
        qn, k_rep, v_rep, attn_sinks,
        (_gather_comm([s_up, s_down]), [(0, True), (N_BLOCKS - 3, True), (N_BLOCKS - 1, False)]))
    w_bh, w_ba, w_o = g_bh.reshape(D, HGW), g_ba.reshape(D, ATW), g_out.reshape(D, D)
    (a, b, mixed, r, x1, h2), _ = _merge_fwd(
        y_hg, y_at, p_c, x2, w_bh, w_ba, w_o, gate1, norm_ffn_w, sh2, sc2)
    g_gate, g_up, g_down = [g.reshape(N_FF_TILES, FF_TILE, D) for g in (g_gate, g_up, g_down)]

    act, d_gate, d_up, d_f, dx1, acc_ffn = _ffn_fused(x1, h2, tgt, g_gate, g_up, g_down, gate2,
                                                      norm_ffn_w, sc2)
    by_chip = lambda t: t.reshape((N_CHIPS, 2) + t.shape[1:])
    ff_by_chip = lambda t: t.reshape(N_CHIPS, 2, FF_BLK, D)
    t_down, _ = _mm_tn_blocked(act, d_f, "grad_down")
    t_down = ff_by_chip(t_down)
    t_gate, (f_down,) = _mm_tn_blocked(d_gate, h2, "grad_gate", (_sibling_comm([t_down]), first_last(N_FF_TILES)))
    t_gate = ff_by_chip(t_gate)
    t_up, (f_gate,) = _mm_tn_blocked(d_up, h2, "grad_up", (_sibling_comm([t_gate]), first_last(N_FF_TILES)))
    t_up = ff_by_chip(t_up)

    (d_r, d_a, d_b, dp_c, dy_hg, dy_at, acc_mg), (f_up,) = _merge_bwd(
        dx1, r, a, b, p_c, w_bh, w_ba, w_o, gate1, (_sibling_comm([t_up]), first_last(N_LAT_TILES)))
    c_down, c_gate, c_up = [_pair_sum(t, f, "pair_sum_" + nm) for t, f, nm in
                            ((t_down, f_down, "down"), (t_gate, f_gate, "gate"), (t_up, f_up, "up"))]
    t_out = _mm_tn(mixed, d_r, tk=512, nk=4, tm=512, tn=1024, out_dtype=BF16, name="grad_out")
    t_bh = _mm_tn(d_a, y_hg, tk=512, nk=4, tm=512, tn=512, out_dtype=BF16, name="grad_bh")
    t_ba = _mm_tn(d_b, y_at, tk=512, nk=4, tm=512, tn=512, out_dtype=BF16, name="grad_ba")
    t_bh, t_ba, t_out = [by_chip(t.reshape(N_DEV, D // N_DEV, t.shape[1])) for t in (t_bh, t_ba, t_out)]
    (dq, dk_rep, dv_rep, dsink), (r_up,) = _attn_bwd(
        qn, k_rep, v_rep, attn_sinks, y_at, lse, dy_at, (_chip_comm([c_up]), first_last(N_BLOCKS)))
    (dp_b, d_o, acc_prep), (f_bh, f_ba, f_out) = _prep_bwd(
        p_b, o, cos, sin, hgrn_norm_w, qnw_t, knw_t, dy_hg, dq, dk_rep, dv_rep,
        (_sibling_comm([t_bh, t_ba, t_out]), first_last(N_TILES)))
    c_bh, c_ba, c_out = [_pair_sum(t, f, "pair_sum_" + nm) for t, f, nm in
                         ((t_bh, f_bh, "bh"), (t_ba, f_ba, "ba"), (t_out, f_out, "out"))]
    (dp_a, dlb), (r_bh, r_ba, r_out, r_down, r_gate) = _hgrn_bwd(
        p_a, lbl, d_o, st, (_chip_comm([c_bh, c_ba, c_out, c_down, c_gate]), first_last(HG_HEADS)))
    t_a = _mm_tn(dp_a, h_all, tk=768, nk=3, tm=1024, tn=1024, out_dtype=BF16, name="grad_in_a")
    t_b = _mm_tn(dp_b, h_all, tk=768, nk=3, tm=640, tn=1024, out_dtype=BF16, name="grad_in_b")
    t_c = _mm_tn(dp_c, h_all, tk=256, nk=8, b_off=1, tm=1024, tn=1024, out_dtype=BF16, name="grad_in_c")
    t_in = None
    for piece, nm in ((t_a, "a"), (t_b, "b"), (t_c, "c")):
        t_in = _place_row_blocks(piece, *_block_table(nm), t_in, IN_COLS, "order_terms_" + nm)
    t_in = by_chip(t_in.reshape(N_DEV, IN_BLK, D))
    (f_in,) = _run_comm(_sibling_comm([t_in]), "scatter_in_sibling")
    (grad_x, acc_in), (r_in,) = _input_bwd(
        dp_a, dp_b, dp_c, w_a, w_b, w_c, ctx2, x2, dx1, norm_mix_w, sh_in, sc_in,
        (_chip_comm([_pair_sum(t_in, f_in, "pair_sum_in")]), first_last(N_TILES)))
    small = _pack_small(acc_in, acc_mg, acc_ffn, acc_prep, dsink, dlb)
    return grad_x, small, [r_in, r_bh, r_ba, r_out, r_gate, r_up, r_down]


def kernel(x, c, ctx, c_ctx, w_ada, b_ada, norm_mix_w, norm_ffn_w, w_in, hgrn_lb_logits, hgrn_norm_w, q_norm_w, k_norm_w, attn_sinks, w_branch_hgrn, w_branch_attn, w_out, w_ffn_gate, w_ffn_up, w_ffn_down, loss_target, m_c_ctx, m_w_ada, m_b_ada, m_norm_mix_w, m_norm_ffn_w, m_w_in, m_hgrn_lb_logits, m_hgrn_norm_w, m_q_norm_w, m_k_norm_w, m_attn_sinks, m_w_branch_hgrn, m_w_branch_attn, m_w_out, m_w_ffn_gate, m_w_ffn_up, m_w_ffn_down, v_c_ctx, v_w_ada, v_b_ada, v_norm_mix_w, v_norm_ffn_w, v_w_in, v_hgrn_lb_logits, v_hgrn_norm_w, v_q_norm_w, v_k_norm_w, v_attn_sinks, v_w_branch_hgrn, v_w_branch_attn, v_w_out, v_w_ffn_gate, v_w_ffn_up, v_w_ffn_down):
    me = 4 * lax.axis_index("x") + 2 * lax.axis_index("y") + lax.axis_index("c")
    x2, ctx2, tgt = x[0], ctx[0], loss_target[0]
    w_ada2, w_in2 = w_ada[0], w_in[0]

    cond = jnp.zeros((8, D), F32).at[0].set(c[0]).at[1, :256].set(hgrn_lb_logits.reshape(256))
    b_cols = lax.dynamic_slice(b_ada, (0, me * ADA_BLK), (1, ADA_BLK))
    g0, cc, g1, g_in = _prologue(cond, c_ctx.reshape(1, D), w_ada2, b_cols, w_in2.T.astype(BF16))
    lbl = jnp.transpose(g0[:, 1, :256].reshape(N_DEV, 2, 2, 64), (1, 2, 0, 3)).reshape(2, 2, HGW)
    mod_all = _cols_from_blocks(g1)
    mod = lax.dynamic_slice(mod_all, (me, 0), (1, 6 * D)).reshape(6, D)
    mod_c = mod_all[8].reshape(6, D)
    sh1, sc1, gate1, sh2, sc2, gate2 = [mod[k:k + 1] for k in range(6)]
    sh_in = jnp.concatenate([mod_c[0:1], sh1], axis=0)
    sc_in = jnp.concatenate([mod_c[1:2], sc1], axis=0)

    shards = [w_branch_hgrn[0].T, w_branch_attn[0].T, w_out[0], w_ffn_gate[0].T, w_ffn_up[0].T, w_ffn_down[0]]
    w_in_t = g_in.reshape(IN_COLS, D)
    w_a, w_b, w_c = [_pick_row_blocks(w_in_t, *_block_table(nm), "order_w_" + nm) for nm in "abc"]

    grad_x, small, (r_in, r_bh, r_ba, r_out, r_gate, r_up, r_down) = _local_step(
        x2, ctx2, tgt, lbl, sh_in, sc_in, gate1, sh2, sc2, gate2, norm_mix_w, norm_ffn_w, hgrn_norm_w,
        q_norm_w, k_norm_w, attn_sinks, w_a, w_b, w_c, *[s.astype(BF16) for s in shards])

    big = {}
    for nm, rr, ww, mm, vv, tr, transposed in (
            ("w_in", r_in, w_in2, m_w_in[0], v_w_in[0], 336, True),
            ("w_branch_hgrn", r_bh, w_branch_hgrn[0], m_w_branch_hgrn[0], v_w_branch_hgrn[0], 128, True),
            ("w_branch_attn", r_ba, w_branch_attn[0], m_w_branch_attn[0], v_w_branch_attn[0], 128, True),
            ("w_out", r_out, w_out[0], m_w_out[0], v_w_out[0], 128, False),
            ("w_ffn_gate", r_gate, w_ffn_gate[0], m_w_ffn_gate[0], v_w_ffn_gate[0], 352, True),
            ("w_ffn_up", r_up, w_ffn_up[0], m_w_ffn_up[0], v_w_ffn_up[0], 352, True),
            ("w_ffn_down", r_down, w_ffn_down[0], m_w_ffn_down[0], v_w_ffn_down[0], 352, False)):
        if transposed:
            res = _adamw_sharded(rr, ww.T, mm.T, vv.T, "adamw_" + nm, tr)
            big[nm] = [t.T[None] for t in res]
        else:
            big[nm] = [t[None] for t in _adamw_sharded(rr, ww, mm, vv, "adamw_" + nm, tr)]

    (g2,) = _all_gather([small], "gather_small", True)
    tot = _sum_small(g2)
    dm = jnp.zeros((16, 6 * D), F32).at[:8].set(g2[:, R_DMOD:R_DMOD + 6, :].reshape(N_DEV, 6 * D))
    dm = dm.at[8, :2 * D].set(tot[R_DCTX:R_DCTX + 2].reshape(2 * D))
    dm_cols = lax.dynamic_slice(dm, (0, me * ADA_BLK), (16, ADA_BLK))
    g_w_ada, dsc_term = _ada_grads(cc, dm_cols, w_ada2)
    (g3,) = _all_gather([dsc_term], "gather_cctx", True)
    g_c_ctx = _c_ctx_grad(g3, c_ctx.reshape(1, D))
    g_lbl = _lb_grads(tot[R_DLB:R_DLB + 2, :HGW], lbl)
    g_lb_mine = lax.dynamic_slice(g_lbl, (0, 0, me * 64), (2, 2, 64))
    misc = tot[R_MISC]
    loss = misc[M_LOSS]

    rep_out = _adamw_replicated(
        tot, g_c_ctx,
        [b_ada, c_ctx.reshape(1, D), norm_mix_w, norm_ffn_w, hgrn_norm_w, q_norm_w, k_norm_w, attn_sinks],
        [m_b_ada, m_c_ctx.reshape(1, D), m_norm_mix_w, m_norm_ffn_w, m_hgrn_norm_w, m_q_norm_w, m_k_norm_w,
         m_attn_sinks],
        [v_b_ada, v_c_ctx.reshape(1, D), v_norm_mix_w, v_norm_ffn_w, v_hgrn_norm_w, v_q_norm_w, v_k_norm_w,
         v_attn_sinks])
    rep = []
    for kind in range(4):
        vals = dict(zip(_REP_NAMES, rep_out[kind * len(_REP_NAMES):(kind + 1) * len(_REP_NAMES)]))
        vals["c_ctx"] = vals["c_ctx"].reshape(D)
        rep.append(vals)

    d_ada, nm_ada, nv_ada = _adamw_plain(g_w_ada, w_ada2, m_w_ada[0], v_w_ada[0], "adamw_w_ada")
    ada = [t[None] for t in (g_w_ada, d_ada, nm_ada, nv_ada)]
    lb_w = hgrn_lb_logits.reshape(4, 64)
    d_lb, nm_lb, nv_lb = _adamw_plain(g_lb_mine.reshape(4, 64), lb_w, m_hgrn_lb_logits.reshape(4, 64),
                                      v_hgrn_lb_logits.reshape(4, 64), "adamw_lb")
    lbs = [t.reshape(2, 2, 64) for t in (g_lb_mine, d_lb, nm_lb, nv_lb)]

    names = ['c_ctx', 'w_ada', 'b_ada', 'norm_mix_w', 'norm_ffn_w', 'w_in', 'hgrn_lb_logits', 'hgrn_norm_w',
             'q_norm_w', 'k_norm_w', 'attn_sinks', 'w_branch_hgrn', 'w_branch_attn', 'w_out', 'w_ffn_gate',
             'w_ffn_up', 'w_ffn_down']
    outs = [loss, grad_x[None]]
    for kind in range(4):
        for nm in names:
            if nm == 'w_ada':
                outs.append(ada[kind])
            elif nm == 'hgrn_lb_logits':
                outs.append(lbs[kind])
            elif nm in big:
                outs.append(big[nm][kind])
            else:
                outs.append(rep[kind][nm])
    return tuple(outs)
```

```python
import functools
import math

import numpy as np
import jax
import jax.numpy as jnp
from jax import lax
from jax.experimental import pallas as pl
from jax.experimental.pallas import tpu as pltpu

F32 = jnp.float32
BF16 = jnp.bfloat16

N_DEV = 8
D = 1024
S = 2048
L = 256
T = L + S
TM = 256
N_TILES = T // TM
N_LAT_TILES = S // TM
HG_HEADS = 4
HG_DIM = 128
HGW = 512
CHUNK = 32
N_CHUNKS = T // CHUNK
N_CTX_CHUNKS = L // CHUNK
N_LAT_CHUNKS = S // CHUNK
ATT_HEADS = 8
KV_HEADS = 2
HEAD_DIM = 64
ATW = 512
KVW = 128
BLOCK = 128
N_BLOCKS = S // BLOCK
GRID_W = 64
ROPE_THETA = 10000.0
D_FF = 2816
FF_BLK = D_FF // N_DEV
FF_TILE = 256
N_FF_TILES = D_FF // FF_TILE
IN_COLS = 5376
IN_BLK = IN_COLS // N_DEV
ADA_BLK = 6 * D // N_DEV
EPS = 1e-6
WA, WB, WC = 2048, 1280, 2048

ADAM_LR = 0.001
ADAM_B1 = 0.9
ADAM_B2 = 0.999
ADAM_EPS = 1e-08
ADAM_WD = 0.01
ADAM_STEP = 10

VMEM_LIMIT = 56 * 1024 * 1024
MESH = pl.DeviceIdType.MESH


def _cp(sem=None, vmem=VMEM_LIMIT):
    return pltpu.CompilerParams(dimension_semantics=sem, vmem_limit_bytes=vmem)


def _full(shape):
    n = len(shape)
    return pl.BlockSpec(shape, lambda *_: (0,) * n)


_VMEM_WHOLE = pl.BlockSpec(memory_space=pltpu.VMEM)
_ANY = pl.BlockSpec(memory_space=pl.ANY)


def _sigmoid(v):
    return 1.0 / (1.0 + jnp.exp(-v))


def _dot(a, b):
    return jnp.dot(a, b, preferred_element_type=F32)


def _dot_nt(a, b):
    return lax.dot_general(a, b, (((1,), (1,)), ((), ())), preferred_element_type=F32)


def _dot_tn(a, b):
    return lax.dot_general(a, b, (((0,), (0,)), ((), ())), preferred_element_type=F32)


def _split3(v):
    hi = v.astype(BF16)
    r = v - hi.astype(F32)
    mid = r.astype(BF16)
    lo = (r - mid.astype(F32)).astype(BF16)
    return hi, mid, lo


def _dot_exact_rhs01(v, m01):
    hi, mid, lo = _split3(v)
    return _dot(hi, m01) + _dot(mid, m01) + _dot(lo, m01)


def _split2(v):
    hi = v.astype(BF16)
    return hi, (v - hi.astype(F32)).astype(BF16)


def _dot_lhs01(m01, v):
    hi, lo = _split2(v)
    return _dot(m01, hi) + _dot(m01, lo)


def _dot_f32(a, b, dot=_dot):
    ah, am, al = _split3(a)
    bh, bm, bl = _split3(b)
    return (dot(ah, bh) + (dot(ah, bm) + dot(am, bh))
            + (dot(am, bm) + dot(ah, bl) + dot(al, bh)))


def _my_pos():
    return lax.axis_index("x"), lax.axis_index("y"), lax.axis_index("c")


class _Comm:
    def __init__(self, operands, out_shapes, sems, phases):
        self.operands, self.out_shapes, self.sems, self.phases = operands, out_shapes, sems, phases


def _gather_comm(blocks):
    n = len(blocks)

    def parts(ins, outs, sems):
        send_sems, recv_sems, local_sems = sems
        x, y, c = _my_pos()
        me, sibling = (x, y, c), (x, y, 1 - c)
        chips = [(1 - x, y), (x, 1 - y), (1 - x, 1 - y)]

        def slot(a, px, py, pc):
            return outs[a].at[4 * px + 2 * py + pc]

        def copy(a, k, block, to, src=None):
            return pltpu.make_async_remote_copy(
                src_ref=slot(a, *block) if src is None else src, dst_ref=slot(a, *block),
                send_sem=send_sems.at[a, k], recv_sem=recv_sems.at[a, k],
                device_id=to, device_id_type=MESH)

        mine = [pltpu.make_async_copy(ins[a], slot(a, *me), local_sems.at[a]) for a in range(n)]
        first = []
        for a in range(n):
            first.append(copy(a, 0, me, sibling, src=ins[a]))
            first += [copy(a, 1 + j, me, (*chip, c), src=ins[a]) for j, chip in enumerate(chips)]
        passed = [copy(a, 4 + j, (*chip, c), sibling) for j, chip in enumerate(chips) for a in range(n)]
        return c, me, sibling, chips, copy, mine, first, passed

    def start(ins, outs, sems):
        _, _, _, _, _, mine, first, _ = parts(ins, outs, sems)
        for cp in mine + first:
            cp.start()

    def forward(ins, outs, sems):
        c, me, _, chips, copy, _, _, passed = parts(ins, outs, sems)
        for j, chip in enumerate(chips):
            for a in range(n):
                copy(a, 1 + j, (*chip, c), me).wait_recv()
                passed[j * n + a].start()

    def finish(ins, outs, sems):
        c, me, sibling, chips, copy, mine, first, passed = parts(ins, outs, sems)
        for a in range(n):
            copy(a, 0, sibling, me).wait_recv()
            for j, chip in enumerate(chips):
                copy(a, 4 + j, (*chip, 1 - c), me).wait_recv()
        for cp in first + passed:
            cp.wait_send()
        for cp in mine:
            cp.wait()

    return _Comm(blocks, [jax.ShapeDtypeStruct((N_DEV,) + b.shape, b.dtype) for b in blocks],
                 [pltpu.SemaphoreType.DMA((n, 7)), pltpu.SemaphoreType.DMA((n, 7)), pltpu.SemaphoreType.DMA((n,))],
                 [start, forward, finish])


def _run_comm(comm, name, in_vmem=False):
    n_in, n_out = len(comm.operands), len(comm.out_shapes)

    def body(*refs):
        ins, outs, sems = refs[:n_in], refs[n_in:n_in + n_out], refs[n_in + n_out:]
        for phase in comm.phases:
            phase(ins, outs, sems)

    spec = _VMEM_WHOLE if in_vmem else _ANY
    return pl.pallas_call(
        body, name=name, out_shape=comm.out_shapes, in_specs=[spec] * n_in, out_specs=[spec] * n_out,
        scratch_shapes=comm.sems,
    )(*comm.operands)


def _carrier_call(body, comm, schedule, *, name, grid, in_specs, out_specs, out_shape, scratch_shapes, operands):
    n_in, n_out, n_scr = len(in_specs), len(out_specs), len(scratch_shapes)
    c_in, c_out = len(comm.operands), len(comm.out_shapes)

    def full_body(*refs):
        ins, refs = refs[:n_in], refs[n_in:]
        cins, refs = refs[:c_in], refs[c_in:]
        outs, refs = refs[:n_out], refs[n_out:]
        couts, refs = refs[:c_out], refs[c_out:]
        scr, csems = refs[:n_scr], refs[n_scr:]
        step = pl.program_id(0)

        def run(before):
            for (at, when_before), phase in zip(schedule, comm.phases):
                if when_before == before:
                    pl.when(step == at)(functools.partial(phase, cins, couts, csems))

        run(True)
        body(*ins, *outs, *scr)
        run(False)

    res = pl.pallas_call(
        full_body, name=name, grid=grid,
        in_specs=list(in_specs) + [_ANY] * c_in, out_specs=list(out_specs) + [_ANY] * c_out,
        out_shape=list(out_shape) + list(comm.out_shapes),
        scratch_shapes=list(scratch_shapes) + list(comm.sems),
        compiler_params=_cp(("arbitrary",)),
    )(*operands, *comm.operands)
    return res[:n_out], res[n_out:]


def _pcall(body, carried, *, name, grid, in_specs, out_specs, out_shape, scratch_shapes, operands):
    if carried is None:
        res = pl.pallas_call(body, name=name, grid=grid, in_specs=in_specs, out_specs=out_specs,
                             out_shape=out_shape, scratch_shapes=scratch_shapes,
                             compiler_params=_cp(("arbitrary",)))(*operands)
        return res, ()
    return _carrier_call(body, carried[0], carried[1], name=name, grid=grid, in_specs=in_specs,
                         out_specs=out_specs, out_shape=out_shape, scratch_shapes=scratch_shapes,
                         operands=operands)


def _all_gather(blocks, name, in_vmem):
    return _run_comm(_gather_comm(blocks), name, in_vmem)


N_CHIPS = 4


def _sibling_comm(contribs):
    n = len(contribs)

    def copies(ins, outs, sems):
        send_sems, recv_sems = sems
        x, y, c = _my_pos()
        return [pltpu.make_async_remote_copy(
            src_ref=ins[a].at[pl.ds(0, N_CHIPS), 1 - c], dst_ref=outs[a],
            send_sem=send_sems.at[a], recv_sem=recv_sems.at[a],
            device_id=(x, y, 1 - c), device_id_type=MESH) for a in range(n)]

    def start(ins, outs, sems):
        for cp in copies(ins, outs, sems):
            cp.start()

    def finish(ins, outs, sems):
        cps = copies(ins, outs, sems)
        for cp in cps:
            cp.wait_recv()
        for cp in cps:
            cp.wait_send()

    return _Comm(contribs, [jax.ShapeDtypeStruct((N_CHIPS,) + b.shape[2:], b.dtype) for b in contribs],
                 [pltpu.SemaphoreType.DMA((n,)), pltpu.SemaphoreType.DMA((n,))], [start, finish])


def _pair_sum(mine, theirs, name):
    _, _, rows, cols = mine.shape
    core = lax.axis_index("c").astype(jnp.int32).reshape(1)

    def body(c_ref, m_ref, t_ref, o_ref):
        o_ref[...] = (m_ref[...].astype(F32) + t_ref[...].astype(F32)).astype(BF16)

    return pl.pallas_call(
        body, name=name,
        grid_spec=pltpu.PrefetchScalarGridSpec(
            num_scalar_prefetch=1, grid=(N_CHIPS,),
            in_specs=[pl.BlockSpec((None, None, rows, cols), lambda q, c: (q, c[0], 0, 0)),
                      pl.BlockSpec((None, rows, cols), lambda q, c: (q, 0, 0))],
            out_specs=pl.BlockSpec((None, rows, cols), lambda q, c: (q, 0, 0))),
        out_shape=jax.ShapeDtypeStruct((N_CHIPS, rows, cols), BF16),
        compiler_params=_cp(("parallel",)),
    )(core, mine, theirs)


def _chip_comm(sums):
    n = len(sums)

    def parts(ins, outs, sems):
        send_sems, recv_sems, local_sems = sems
        x, y, c = _my_pos()
        q_me = 2 * x + y
        chips = [(1 - x, y), (x, 1 - y), (1 - x, 1 - y)]
        mine = [pltpu.make_async_copy(ins[a].at[q_me], outs[a].at[q_me], local_sems.at[a]) for a in range(n)]
        sends, recvs = [], []
        for j, (px, py) in enumerate(chips):
            for a in range(n):
                q = 2 * px + py
                sends.append(pltpu.make_async_remote_copy(
                    src_ref=ins[a].at[q], dst_ref=outs[a].at[q_me],
                    send_sem=send_sems.at[a, j], recv_sem=recv_sems.at[a, j],
                    device_id=(px, py, c), device_id_type=MESH))
                recvs.append(pltpu.make_async_remote_copy(
                    src_ref=ins[a].at[q], dst_ref=outs[a].at[q],
                    send_sem=send_sems.at[a, j], recv_sem=recv_sems.at[a, j],
                    device_id=(x, y, c), device_id_type=MESH))
        return mine, sends, recvs

    def start(ins, outs, sems):
        mine, sends, _ = parts(ins, outs, sems)
        for cp in mine + sends:
            cp.start()

    def finish(ins, outs, sems):
        mine, sends, recvs = parts(ins, outs, sems)
        for cp in recvs:
            cp.wait_recv()
        for cp in sends:
            cp.wait_send()
        for cp in mine:
            cp.wait()

    return _Comm(sums, [jax.ShapeDtypeStruct(b.shape, b.dtype) for b in sums],
                 [pltpu.SemaphoreType.DMA((n, 3)), pltpu.SemaphoreType.DMA((n, 3)), pltpu.SemaphoreType.DMA((n,))],
                 [start, finish])


def _mm_nt(a, bt, *, tm, tn, out_dtype, name, row_off=0, rows=None):
    rows = a.shape[0] if rows is None else rows
    n, k = bt.shape

    def body(a_ref, b_ref, o_ref):
        o_ref[...] = _dot_nt(a_ref[...], b_ref[...]).astype(out_dtype)

    return pl.pallas_call(
        body, name=name, grid=(rows // tm, n // tn),
        in_specs=[pl.BlockSpec((tm, k), lambda i, j: (i + row_off, 0)),
                  pl.BlockSpec((tn, k), lambda i, j: (j, 0))],
        out_specs=pl.BlockSpec((tm, tn), lambda i, j: (i, j)),
        out_shape=jax.ShapeDtypeStruct((rows, n), out_dtype),
        compiler_params=_cp(("parallel", "parallel")),
    )(a, bt)


def _mm_tn(a, b, *, tk, nk, tm, tn, out_dtype, name, a_off=0, b_off=0):
    m, n = a.shape[1], b.shape[1]

    def body(a_ref, b_ref, o_ref, acc):
        kk = pl.program_id(2)

        @pl.when(kk == 0)
        def _():
            acc[...] = jnp.zeros_like(acc)

        acc[...] += _dot_tn(a_ref[...], b_ref[...])

        @pl.when(kk == nk - 1)
        def _():
            o_ref[...] = acc[...].astype(out_dtype)

    return pl.pallas_call(
        body, name=name, grid=(m // tm, n // tn, nk),
        in_specs=[pl.BlockSpec((tk, tm), lambda i, j, kk: (kk + a_off, i)),
                  pl.BlockSpec((tk, tn), lambda i, j, kk: (kk + b_off, j))],
        out_specs=pl.BlockSpec((tm, tn), lambda i, j, kk: (i, j)),
        out_shape=jax.ShapeDtypeStruct((m, n), out_dtype),
        scratch_shapes=[pltpu.VMEM((tm, tn), F32)],
        compiler_params=_cp(("parallel", "parallel", "arbitrary")),
    )(a, b)


def _mm_tn_blocked(a, b, name, carried=None):
    nb, _, w = a.shape
    n = b.shape[1]

    def body(a_ref, b_ref, o_ref):
        o_ref[...] = _dot_tn(a_ref[...], b_ref[...]).astype(BF16)

    (out,), extra = _pcall(
        body, carried, name=name, grid=(nb,),
        in_specs=[pl.BlockSpec((None, S, w), lambda j: (j, 0, 0)), _full((S, n))],
        out_specs=[pl.BlockSpec((None, w, n), lambda j: (j, 0, 0))],
        out_shape=[jax.ShapeDtypeStruct((nb, w, n), BF16)],
        scratch_shapes=[], operands=[a, b])
    return out, extra


def _prologue(cond, c_ctx, w_ada, b_cols, w_in_t):
    rows_shape = jax.ShapeDtypeStruct((16, ADA_BLK), F32)
    big, g_cond, g_mod = _gather_comm([w_in_t]), _gather_comm([cond]), _gather_comm([rows_shape])

    def body(cond_ref, cctx_ref, wada_ref, b_ref, win_ref, g0_ref, cc_ref, g1_ref, gin_ref, rows_ref, *sems):
        s_big, s_cond, s_mod = sems[0:3], sems[3:6], sems[6:9]
        big.phases[0]([win_ref], [gin_ref], s_big)
        for phase in g_cond.phases:
            phase([cond_ref], [g0_ref], s_cond)
        cc_ref[...] = jnp.zeros_like(cc_ref)
        for j in range(N_DEV):
            cc_ref[j:j + 1, :] = g0_ref[j, 0:1, :]
        cc_ref[N_DEV:N_DEV + 1, :] = cctx_ref[...]
        cv = cc_ref[...]
        rows_ref[...] = _dot_f32(cv * _sigmoid(cv), wada_ref[...]) + b_ref[...]
        for phase in g_mod.phases:
            phase([rows_ref], [g1_ref], s_mod)
        big.phases[1]([win_ref], [gin_ref], s_big)
        big.phases[2]([win_ref], [gin_ref], s_big)

    return pl.pallas_call(
        body, name="prologue",
        in_specs=[_VMEM_WHOLE] * 4 + [_ANY], out_specs=[_VMEM_WHOLE] * 3 + [_ANY],
        out_shape=[g_cond.out_shapes[0], jax.ShapeDtypeStruct((16, D), F32), g_mod.out_shapes[0],
                   big.out_shapes[0]],
        scratch_shapes=[pltpu.VMEM((16, ADA_BLK), F32)] + big.sems + g_cond.sems + g_mod.sems,
        compiler_params=_cp(),
    )(cond, c_ctx, w_ada, b_cols, w_in_t)


def _ada_grads(cc, dm_cols, w_ada):
    def body(c_ref, dm_ref, w_ref, gw_ref, dsc_ref):
        cv = c_ref[...]
        sc = cv * _sigmoid(cv)
        dm = dm_ref[...]
        gw_ref[...] = _dot_f32(sc, dm, dot=_dot_tn)
        dsc_ref[...] = _dot_f32(dm, w_ref[...], dot=_dot_nt)

    return pl.pallas_call(
        body, name="ada_grads",
        in_specs=[_VMEM_WHOLE] * 3, out_specs=[_VMEM_WHOLE] * 2,
        out_shape=[jax.ShapeDtypeStruct((D, ADA_BLK), F32), jax.ShapeDtypeStruct((16, D), F32)],
        compiler_params=_cp(),
    )(cc, dm_cols, w_ada)


def _lat(i):
    return jnp.maximum(i - 1, 0)


def _rms_mod(xv, nw, sh, sc):
    rstd = lax.rsqrt(jnp.mean(xv * xv, axis=-1, keepdims=True) + EPS)
    return (xv * rstd * nw) * (1.0 + sc) + sh


def _rms_mod_bwd(xv, nw, sc, dh):
    rstd = lax.rsqrt(jnp.mean(xv * xv, axis=-1, keepdims=True) + EPS)
    xhat = xv * rstd
    dn = dh * (1.0 + sc)
    dxhat = dn * nw
    dx = rstd * (dxhat - xhat * jnp.mean(dxhat * xhat, axis=-1, keepdims=True))
    return (dx, jnp.sum(dh, axis=0, keepdims=True), jnp.sum(dh * (xhat * nw), axis=0, keepdims=True),
            jnp.sum(dn * xhat, axis=0, keepdims=True))


def _norm_mod_all(ctx, x, nw, sh, sc):
    def body(ctx_ref, x_ref, nw_ref, sh_ref, sc_ref, o_ref):
        i = pl.program_id(0)
        sel = jnp.minimum(i, 1)
        xv = jnp.where(i == 0, ctx_ref[...], x_ref[...])
        o_ref[...] = _rms_mod(xv, nw_ref[...], sh_ref[pl.ds(sel, 1), :], sc_ref[pl.ds(sel, 1), :]).astype(BF16)

    return pl.pallas_call(
        body, name="norm_mod", grid=(N_TILES,),
        in_specs=[_full((TM, D)), pl.BlockSpec((TM, D), lambda i: (_lat(i), 0)),
                  _full((1, D)), _full((2, D)), _full((2, D))],
        out_specs=pl.BlockSpec((TM, D), lambda i: (i, 0)),
        out_shape=jax.ShapeDtypeStruct((T, D), BF16),
        compiler_params=_cp(("parallel",)),
    )(ctx, x, nw, sh, sc)


def _chunk_masks(reverse):
    row = lax.broadcasted_iota(jnp.int32, (TM, TM), 0)
    col = lax.broadcasted_iota(jnp.int32, (TM, TM), 1)
    same = (row // CHUNK) == (col // CHUNK)
    tri = same & ((col >= row) if reverse else (col <= row))
    return same, tri


def _chunk_order(i, reverse):
    if not reverse:
        return i
    return jnp.where(i < N_CTX_CHUNKS, N_CTX_CHUNKS - 1 - i, N_CHUNKS + N_CTX_CHUNKS - 1 - i)


def _decay_terms(z, lb, same01, tri01):
    f = lb + (1.0 - lb) * _sigmoid(z)
    g = jnp.log(f)
    g2 = jnp.concatenate(_split2(g), axis=1)
    b2 = _dot(tri01, g2)
    t2 = _dot(same01, g2)
    return f, 1.0 - f, b2[:, :HG_DIM] + b2[:, HG_DIM:], t2[:, :HG_DIM] + t2[:, HG_DIM:]


def _chunk_outer(a, b):
    n = TM // CHUNK
    return jnp.einsum('ncv,nck->nvk', a.reshape(n, CHUNK, HG_DIM), b.reshape(n, CHUNK, HG_DIM),
                      preferred_element_type=F32)


def _hgrn_fwd(p_a, lbl, carried=None):
    cpt = TM // CHUNK

    def body(p_ref, lbl_ref, o_ref, st_ref, qd_s, kd_s, u_s, v_s, ebt_s):
        masks = [_chunk_masks(d == 1) for d in (0, 1)]
        same01 = jnp.where(masks[0][0], 1.0, 0.0).astype(BF16)
        tri = [m[1] for m in masks]
        tri01 = [jnp.where(t, 1.0, 0.0).astype(BF16) for t in tri]
        lb = [_sigmoid(lbl_ref[d][0:1, :] - lbl_ref[d][1:2, :]) for d in (0, 1)]

        def prep(r, carry):
            r0 = pl.multiple_of(r * TM, TM)
            vb = p_ref[pl.ds(r0, TM), 2 * HG_DIM:3 * HG_DIM].astype(BF16)
            v_s[pl.ds(r0, TM), :] = vb
            for d in (0, 1):
                z = p_ref[pl.ds(r0, TM), d * HG_DIM:(d + 1) * HG_DIM]
                _, k, b, bt = _decay_terms(z, lb[d], same01, tri01[d])
                u_s[d, pl.ds(r * cpt, cpt)] = _chunk_outer(vb, (k * jnp.exp(bt - b)).astype(BF16))
                ebt_s[d, pl.ds(r0, TM), :] = jnp.exp(bt)

                @pl.when(r >= 1)
                def _():
                    rl = pl.multiple_of(r0 - L, TM)
                    qr = p_ref[pl.ds(r0, TM), 3 * HG_DIM:4 * HG_DIM]
                    q = qr * _sigmoid(qr) * HG_DIM ** -0.5
                    qd_s[d, pl.ds(rl, TM), :] = (q * jnp.exp(b)).astype(BF16)
                    kd_s[d, pl.ds(rl, TM), :] = (k * jnp.exp(-b)).astype(BF16)

            return carry

        lax.fori_loop(0, N_TILES, prep, 0)

        def scan(i, sts):
            new = []
            for d in (0, 1):
                nn = _chunk_order(i, d == 1)
                c0 = pl.multiple_of(nn * CHUNK, CHUNK)
                st_ref[d, nn] = sts[d].astype(BF16)
                new.append(sts[d] * ebt_s[d, pl.ds(c0, 1), :] + u_s[d, nn])
            return tuple(new)

        zero = jnp.zeros((HG_DIM, HG_DIM), F32)
        lax.fori_loop(0, N_CHUNKS, scan, (zero, zero))

        def outp(r, carry):
            r0 = pl.multiple_of(r * TM, TM)
            vb = v_s[pl.ds(r0 + L, TM), :]
            o = jnp.zeros((TM, HG_DIM), F32)
            for d in (0, 1):
                qd = qd_s[d, pl.ds(r0, TM), :]
                a = jnp.where(tri[d], _dot_nt(qd, kd_s[d, pl.ds(r0, TM), :]), 0.0)
                stb = st_ref[d, pl.ds(N_CTX_CHUNKS + r * cpt, cpt)]
                inter = jnp.einsum('nck,nvk->ncv', qd.reshape(cpt, CHUNK, HG_DIM), stb,
                                   preferred_element_type=F32)
                o = o + _dot(a.astype(BF16), vb) + inter.reshape(TM, HG_DIM)
            o_ref[pl.ds(r0, TM), :] = o
            return carry

        lax.fori_loop(0, N_LAT_TILES, outp, 0)

    return _pcall(
        body, carried, name="hgrn_fwd", grid=(HG_HEADS,),
        in_specs=[pl.BlockSpec((T, 4 * HG_DIM), lambda h: (0, h)),
                  pl.BlockSpec((2, 2, HG_DIM), lambda h: (0, 0, h))],
        out_specs=[pl.BlockSpec((S, HG_DIM), lambda h: (0, h)),
                   pl.BlockSpec((2, None, N_CHUNKS, HG_DIM, HG_DIM), lambda h: (0, h, 0, 0, 0))],
        out_shape=[jax.ShapeDtypeStruct((S, HGW), F32),
                   jax.ShapeDtypeStruct((2, HG_HEADS, N_CHUNKS, HG_DIM, HG_DIM), BF16)],
        scratch_shapes=[pltpu.VMEM((2, S, HG_DIM), BF16), pltpu.VMEM((2, S, HG_DIM), BF16),
                        pltpu.VMEM((2, N_CHUNKS, HG_DIM, HG_DIM), F32), pltpu.VMEM((T, HG_DIM), BF16),
                        pltpu.VMEM((2, T, HG_DIM), F32)],
        operands=[p_a, lbl])


def _hgrn_bwd(p_a, lbl, d_o, st, carried=None):
    cpt = TM // CHUNK

    def rows(r):
        return r * TM if isinstance(r, int) else pl.multiple_of(r * TM, TM)

    def body(p_ref, lbl_ref, do_ref, st_ref, dp_ref, dlb_ref, b_s, bt_s, dbt_s, qd_s, dst_s, w_s):
        masks = [_chunk_masks(d == 1) for d in (0, 1)]
        same01 = jnp.where(masks[0][0], 1.0, 0.0).astype(BF16)
        tri = [m[1] for m in masks]
        tri01 = [jnp.where(t, 1.0, 0.0).astype(BF16) for t in tri]
        later01 = [tri01[1], tri01[0]]
        lb = [_sigmoid(lbl_ref[d][0:1, :] - lbl_ref[d][1:2, :]) for d in (0, 1)]

        def prep_tile(r, latent):
            r0 = rows(r)
            for d in (0, 1):
                z = p_ref[pl.ds(r0, TM), d * HG_DIM:(d + 1) * HG_DIM]
                _, _, b, bt = _decay_terms(z, lb[d], same01, tri01[d])
                b_s[d, pl.ds(r0, TM), :] = b
                bt_s[d, pl.ds(r0, TM), :] = bt
                if latent:
                    rl = pl.multiple_of(r0 - L, TM)
                    qr = p_ref[pl.ds(r0, TM), 3 * HG_DIM:4 * HG_DIM]
                    qd = (qr * _sigmoid(qr) * HG_DIM ** -0.5 * jnp.exp(b)).astype(BF16)
                    qd_s[d, pl.ds(rl, TM), :] = qd
                    w_s[d, pl.ds(r * cpt, cpt)] = _chunk_outer(
                        do_ref[pl.ds(rl, TM), :].astype(BF16), qd).astype(BF16)

        prep_tile(0, False)
        w_s[:, pl.ds(0, N_CTX_CHUNKS)] = jnp.zeros((2, N_CTX_CHUNKS, HG_DIM, HG_DIM), BF16)

        def prep(r, carry):
            prep_tile(r, True)
            return carry

        lax.fori_loop(1, N_TILES, prep, 0)

        def rscan(j, dsts):
            i = N_CHUNKS - 1 - j
            new = []
            for d in (0, 1):
                nn = _chunk_order(i, d == 1)
                c0 = pl.multiple_of(nn * CHUNK, CHUNK)
                dst_s[d, nn] = dsts[d].astype(BF16)
                after = st_ref[d, _chunk_order(jnp.minimum(i + 1, N_CHUNKS - 1), d == 1)].astype(F32)
                dbt_s[d, pl.ds(c0, CHUNK), :] = jnp.broadcast_to(
                    jnp.sum(after * dsts[d], axis=0, keepdims=True), (CHUNK, HG_DIM))
                new.append(dsts[d] * jnp.exp(bt_s[d, pl.ds(c0, 1), :]) + w_s[d, nn].astype(F32))
            return tuple(new)

        zero = jnp.zeros((HG_DIM, HG_DIM), F32)
        lax.fori_loop(0, N_CHUNKS, rscan, (zero, zero))

        def grad_tile(r, latent):
            r0 = rows(r)
            vb = p_ref[pl.ds(r0, TM), 2 * HG_DIM:3 * HG_DIM].astype(BF16)
            dv = jnp.zeros((TM, HG_DIM), F32)
            dq = jnp.zeros((TM, HG_DIM), F32)
            dlbs = []
            if latent:
                rl = pl.multiple_of(r0 - L, TM)
                qr = p_ref[pl.ds(r0, TM), 3 * HG_DIM:4 * HG_DIM]
                sq = _sigmoid(qr)
                do = do_ref[pl.ds(rl, TM), :].astype(BF16)
                da_full = _dot_nt(do, vb)
            for d in (0, 1):
                z = p_ref[pl.ds(r0, TM), d * HG_DIM:(d + 1) * HG_DIM]
                sz = _sigmoid(z)
                f = lb[d] + (1.0 - lb[d]) * sz
                k = 1.0 - f
                b = b_s[d, pl.ds(r0, TM), :]
                e2 = jnp.exp(bt_s[d, pl.ds(r0, TM), :] - b)
                dstb = dst_s[d, pl.ds(r * cpt, cpt)]
                kd2 = k * e2
                dkd2 = jnp.einsum('ncv,nvk->nck', vb.reshape(cpt, CHUNK, HG_DIM), dstb,
                                  preferred_element_type=F32).reshape(TM, HG_DIM)
                dv = dv + jnp.einsum('nck,nvk->ncv', kd2.astype(BF16).reshape(cpt, CHUNK, HG_DIM), dstb,
                                     preferred_element_type=F32).reshape(TM, HG_DIM)
                dk = dkd2 * e2
                db = -(kd2 * dkd2)
                if latent:
                    eb = jnp.exp(b)
                    enb = jnp.exp(-b)
                    qdf = qr * sq * HG_DIM ** -0.5 * eb
                    kdf = k * enb
                    qd = qd_s[d, pl.ds(rl, TM), :]
                    kd = kdf.astype(BF16)
                    a = jnp.where(tri[d], _dot_nt(qd, kd), 0.0).astype(BF16)
                    da = jnp.where(tri[d], da_full, 0.0).astype(BF16)
                    stb = st_ref[d, pl.ds(r * cpt, cpt)]
                    dqd = _dot(da, kd) + jnp.einsum(
                        'ncv,nvk->nck', do.reshape(cpt, CHUNK, HG_DIM), stb,
                        preferred_element_type=F32).reshape(TM, HG_DIM)
                    dkd = _dot_tn(da, qd)
                    dv = dv + _dot_tn(a, do)
                    dk = dk + dkd * enb
                    db = db + qdf * dqd - kdf * dkd
                    dq = dq + dqd * eb
                dg = _dot_lhs01(later01[d], db) + dbt_s[d, pl.ds(r0, TM), :]
                df = dg / f - dk
                dp_ref[pl.ds(r0, TM), d * HG_DIM:(d + 1) * HG_DIM] = (
                    df * (1.0 - lb[d]) * sz * (1.0 - sz)).astype(BF16)
                dlbs.append(jnp.sum(df * (1.0 - sz), axis=0, keepdims=True))
            dp_ref[pl.ds(r0, TM), 2 * HG_DIM:3 * HG_DIM] = dv.astype(BF16)
            if latent:
                dq = dq * (HG_DIM ** -0.5) * (sq * (1.0 + qr * (1.0 - sq)))
            dp_ref[pl.ds(r0, TM), 3 * HG_DIM:4 * HG_DIM] = dq.astype(BF16)
            return dlbs

        dlb_ctx = grad_tile(0, False)

        def grads(r, acc):
            t = grad_tile(r, True)
            return (acc[0] + t[0], acc[1] + t[1])

        dlb = lax.fori_loop(1, N_TILES, grads, (dlb_ctx[0], dlb_ctx[1]))
        dlb_ref[0:1, :] = dlb[0]
        dlb_ref[1:2, :] = dlb[1]

    return _pcall(
        body, carried, name="hgrn_bwd", grid=(HG_HEADS,),
        in_specs=[pl.BlockSpec((T, 4 * HG_DIM), lambda h: (0, h)),
                  pl.BlockSpec((2, 2, HG_DIM), lambda h: (0, 0, h)),
                  pl.BlockSpec((S, HG_DIM), lambda h: (0, h)),
                  pl.BlockSpec((2, None, N_CHUNKS, HG_DIM, HG_DIM), lambda h: (0, h, 0, 0, 0))],
        out_specs=[pl.BlockSpec((T, 4 * HG_DIM), lambda h: (0, h)),
                   pl.BlockSpec((2, HG_DIM), lambda h: (0, h))],
        out_shape=[jax.ShapeDtypeStruct((T, WA), BF16), jax.ShapeDtypeStruct((2, HGW), F32)],
        scratch_shapes=[pltpu.VMEM((2, T, HG_DIM), F32), pltpu.VMEM((2, T, HG_DIM), F32),
                        pltpu.VMEM((2, T, HG_DIM), F32), pltpu.VMEM((2, S, HG_DIM), BF16),
                        pltpu.VMEM((2, N_CHUNKS, HG_DIM, HG_DIM), BF16),
                        pltpu.VMEM((2, N_CHUNKS, HG_DIM, HG_DIM), BF16)],
        operands=[p_a, lbl, d_o, st])


def _rope_tables():
    t = np.arange(S)
    inv = ROPE_THETA ** (-np.arange(0, 32, 2, dtype=np.float64) / 32)
    lane = np.arange(64)
    pos = np.where(lane[None, :] < 32, (t // GRID_W)[:, None], (t % GRID_W)[:, None]).astype(np.float64)
    ang = pos * inv[(lane % 32) % 16][None, :]
    sign = np.where((lane % 32) < 16, -1.0, 1.0)[None, :]
    cos = np.tile(np.cos(ang), (1, 2)).astype(np.float32)
    sin = np.tile(np.sin(ang) * sign, (1, 2)).astype(np.float32)
    return jnp.asarray(cos), jnp.asarray(sin)


def _rope_partner(v):
    lane = lax.broadcasted_iota(jnp.int32, (1, 128), 1)
    first = (lane % 32) < 16
    slabs = []
    for j in range(v.shape[1] // 128):
        s = v[:, 128 * j:128 * (j + 1)]
        slabs.append(jnp.where(first, pltpu.roll(s, 112, 1), pltpu.roll(s, 16, 1)))
    return slabs[0] if len(slabs) == 1 else jnp.concatenate(slabs, axis=1)


def _group_ones(width, group):
    r = lax.broadcasted_iota(jnp.int32, (width, width), 0)
    c = lax.broadcasted_iota(jnp.int32, (width, width), 1)
    return jnp.where((r // group) == (c // group), 1.0, 0.0).astype(BF16)


def _group_mean(v, ones01, group):
    hi = v.astype(BF16)
    lo = (v - hi.astype(F32)).astype(BF16)
    return (_dot(hi, ones01) + _dot(lo, ones01)) * (1.0 / group)


def _rep_matrix():
    r = lax.broadcasted_iota(jnp.int32, (KVW, ATW), 0)
    c = lax.broadcasted_iota(jnp.int32, (KVW, ATW), 1)
    return jnp.where(r == HEAD_DIM * (c // 256) + c % HEAD_DIM, 1.0, 0.0).astype(BF16)


def _tile_lanes(v, reps):
    return jnp.concatenate([v] * reps, axis=1)


def _prep_fwd(p_b, o, cos, sin, hnw, qnw, knw):
    def body(p_ref, o_ref, cos_ref, sin_ref, hnw_ref, qnw_ref, knw_ref, y_ref, q_ref, k_ref, v_ref):
        i = pl.program_id(0)
        rep = _rep_matrix()
        ones_k = _group_ones(KVW, HEAD_DIM)
        kr = p_ref[:, 1024:1152]
        krstd = lax.rsqrt(_group_mean(kr * kr, ones_k, HEAD_DIM) + EPS)
        kn = kr * krstd * knw_ref[...]
        v_ref[...] = _dot(p_ref[:, 1152:1280].astype(BF16), rep).astype(BF16)

        @pl.when(i == 0)
        def _():
            k_ref[...] = _dot(kn.astype(BF16), rep).astype(BF16)

        @pl.when(i > 0)
        def _():
            cs, sn = cos_ref[...], sin_ref[...]
            kro = kn * cs + _rope_partner(kn) * sn
            k_ref[...] = _dot(kro.astype(BF16), rep).astype(BF16)
            qr = p_ref[:, 512:1024]
            qrstd = lax.rsqrt(_group_mean(qr * qr, _group_ones(ATW, HEAD_DIM), HEAD_DIM) + EPS)
            qn = qr * qrstd * qnw_ref[...]
            qro = qn * _tile_lanes(cs, 4) + _rope_partner(qn) * _tile_lanes(sn, 4)
            q_ref[...] = (qro * HEAD_DIM ** -0.5).astype(BF16)
            ys = []
            for h in range(HG_HEADS):
                oh = o_ref[:, HG_DIM * h:HG_DIM * (h + 1)]
                gh = p_ref[:, HG_DIM * h:HG_DIM * (h + 1)]
                rstd = lax.rsqrt(jnp.mean(oh * oh, axis=-1, keepdims=True) + EPS)
                ys.append(oh * rstd * hnw_ref[...] * (gh * _sigmoid(gh)))
            y_ref[...] = jnp.concatenate(ys, axis=1).astype(BF16)

    return pl.pallas_call(
        body, name="prep_fwd", grid=(N_TILES,),
        in_specs=[pl.BlockSpec((TM, WB), lambda i: (i, 0)),
                  pl.BlockSpec((TM, HGW), lambda i: (_lat(i), 0)),
                  pl.BlockSpec((TM, 128), lambda i: (_lat(i), 0)),
                  pl.BlockSpec((TM, 128), lambda i: (_lat(i), 0)),
                  _full((1, HG_DIM)), _full((1, ATW)), _full((1, KVW))],
        out_specs=[pl.BlockSpec((TM, HGW), lambda i: (_lat(i), 0)),
                   pl.BlockSpec((TM, ATW), lambda i: (_lat(i), 0)),
                   pl.BlockSpec((TM, ATW), lambda i: (i, 0)),
                   pl.BlockSpec((TM, ATW), lambda i: (i, 0))],
        out_shape=[jax.ShapeDtypeStruct((S, HGW), BF16), jax.ShapeDtypeStruct((S, ATW), BF16),
                   jax.ShapeDtypeStruct((T, ATW), BF16), jax.ShapeDtypeStruct((T, ATW), BF16)],
        compiler_params=_cp(("arbitrary",)),
    )(p_b, o, cos, sin, hnw, qnw, knw)


def _prep_bwd(p_b, o, cos, sin, hnw, qnw, knw, dy_hg, dq, dk_rep, dv_rep, carried=None):
    def body(p_ref, o_ref, cos_ref, sin_ref, hnw_ref, qnw_ref, knw_ref, dy_ref, dq_ref, dk_ref, dv_ref,
             dp_ref, do_ref, acc_ref):
        i = pl.program_id(0)

        @pl.when(i == 0)
        def _():
            acc_ref[...] = jnp.zeros_like(acc_ref)

        rep = _rep_matrix()
        ones_k = _group_ones(KVW, HEAD_DIM)

        def fold(v):
            hi = v.astype(BF16)
            lo = (v - hi.astype(F32)).astype(BF16)
            return _dot_nt(hi, rep) + _dot_nt(lo, rep)

        kr = p_ref[:, 1024:1152]
        krstd = lax.rsqrt(_group_mean(kr * kr, ones_k, HEAD_DIM) + EPS)
        khat = kr * krstd
        kw = knw_ref[...]
        dkro = fold(dk_ref[...])
        dv = fold(dv_ref[...])

        def k_back(dkn):
            dkhat = dkn * kw
            dkr = krstd * (dkhat - khat * _group_mean(dkhat * khat, ones_k, HEAD_DIM))
            acc_ref[2:3, 0:KVW] += jnp.sum(dkn * khat, axis=0, keepdims=True)
            dp_ref[:, 1024:1152] = dkr.astype(BF16)
            dp_ref[:, 1152:1280] = dv.astype(BF16)

        @pl.when(i == 0)
        def _():
            k_back(dkro)
            dp_ref[:, 0:1024] = jnp.zeros((TM, 1024), BF16)

        @pl.when(i > 0)
        def _():
            cs, sn = cos_ref[...], sin_ref[...]
            k_back(dkro * cs + _rope_partner(dkro * sn))
            ones_q = _group_ones(ATW, HEAD_DIM)
            qr = p_ref[:, 512:1024]
            qrstd = lax.rsqrt(_group_mean(qr * qr, ones_q, HEAD_DIM) + EPS)
            qhat = qr * qrstd
            dqro = dq_ref[...] * HEAD_DIM ** -0.5
            dqn = dqro * _tile_lanes(cs, 4) + _rope_partner(dqro * _tile_lanes(sn, 4))
            dqhat = dqn * qnw_ref[...]
            dqr = qrstd * (dqhat - qhat * _group_mean(dqhat * qhat, ones_q, HEAD_DIM))
            acc_ref[1:2, :] += jnp.sum(dqn * qhat, axis=0, keepdims=True)
            dp_ref[:, 512:1024] = dqr.astype(BF16)
            dws = jnp.zeros((1, HG_DIM), F32)
            for h in range(HG_HEADS):
                sl = slice(HG_DIM * h, HG_DIM * (h + 1))
                oh, gh, dy = o_ref[:, sl], p_ref[:, sl], dy_ref[:, sl]
                rstd = lax.rsqrt(jnp.mean(oh * oh, axis=-1, keepdims=True) + EPS)
                ohat = oh * rstd
                sg = _sigmoid(gh)
                dp_ref[:, sl] = (dy * (ohat * hnw_ref[...]) * (sg * (1.0 + gh * (1.0 - sg)))).astype(BF16)
                dn = dy * (gh * sg)
                dws = dws + jnp.sum(dn * ohat, axis=0, keepdims=True)
                dohat = dn * hnw_ref[...]
                do_ref[:, sl] = rstd * (dohat - ohat * jnp.mean(dohat * ohat, axis=-1, keepdims=True))
            acc_ref[0:1, 0:HG_DIM] += dws

    return _pcall(
        body, carried, name="prep_bwd", grid=(N_TILES,),
        in_specs=[pl.BlockSpec((TM, WB), lambda i: (i, 0)),
                  pl.BlockSpec((TM, HGW), lambda i: (_lat(i), 0)),
                  pl.BlockSpec((TM, 128), lambda i: (_lat(i), 0)),
                  pl.BlockSpec((TM, 128), lambda i: (_lat(i), 0)),
                  _full((1, HG_DIM)), _full((1, ATW)), _full((1, KVW)),
                  pl.BlockSpec((TM, HGW), lambda i: (_lat(i), 0)),
                  pl.BlockSpec((TM, ATW), lambda i: (_lat(i), 0)),
                  pl.BlockSpec((TM, ATW), lambda i: (i, 0)),
                  pl.BlockSpec((TM, ATW), lambda i: (i, 0))],
        out_specs=[pl.BlockSpec((TM, WB), lambda i: (i, 0)),
                   pl.BlockSpec((TM, HGW), lambda i: (_lat(i), 0)),
                   _full((8, ATW))],
        out_shape=[jax.ShapeDtypeStruct((T, WB), BF16), jax.ShapeDtypeStruct((S, HGW), F32),
                   jax.ShapeDtypeStruct((8, ATW), F32)],
        scratch_shapes=[], operands=[p_b, o, cos, sin, hnw, qnw, knw, dy_hg, dq, dk_rep, dv_rep])


NEG = -1e30
_CTX_BLOCKS = L // BLOCK


def _attn_window_specs():
    prev = pl.BlockSpec((BLOCK, ATW), lambda i: (jnp.maximum(i - 1, 0) + _CTX_BLOCKS, 0))
    own = pl.BlockSpec((BLOCK, ATW), lambda i: (i + _CTX_BLOCKS, 0))
    nxt = pl.BlockSpec((BLOCK, ATW), lambda i: (jnp.minimum(i + 1, N_BLOCKS - 1) + _CTX_BLOCKS, 0))
    return [prev, own, nxt, _full((L, ATW))]


def _attn_valid(i, heads, context):
    n_keys = 3 * BLOCK + (L if context else 0)
    qi = lax.broadcasted_iota(jnp.int32, (heads * BLOCK, n_keys), 0) % BLOCK
    kj = lax.broadcasted_iota(jnp.int32, (heads * BLOCK, n_keys), 1)
    window = ((jnp.abs(kj - BLOCK - qi) <= BLOCK) & ((kj >= BLOCK) | (i > 0))
              & ((kj < 2 * BLOCK) | (i < N_BLOCKS - 1)))
    return window | (kj >= 3 * BLOCK)


def _stack_heads(qg):
    lane = lax.broadcasted_iota(jnp.int32, (1, 256), 1) // HEAD_DIM
    return jnp.concatenate([jnp.where(lane == g, qg, jnp.zeros_like(qg)) for g in range(4)], axis=0)


def _unstack_heads(v4):
    lane = lax.broadcasted_iota(jnp.int32, (1, 256), 1) // HEAD_DIM
    out = jnp.where(lane == 0, v4[0:BLOCK], 0.0)
    for g in range(1, 4):
        out = out + jnp.where(lane == g, v4[g * BLOCK:(g + 1) * BLOCK], 0.0)
    return out


def _sink_rows(sink_ref, hk):
    return jnp.concatenate(
        [jnp.broadcast_to(sink_ref[0:1, 4 * hk + g:4 * hk + g + 1], (BLOCK, 1)) for g in range(4)], axis=0)


def _attn_fwd(q, k_rep, v_rep, sinks, carried=None):
    def body(q_ref, kp, ko, kn, kc, vp, vo, vn, vc, sink_ref, y_ref, lse_ref):
        i = pl.program_id(0)
        valid = _attn_valid(i, 1, True)
        lane8 = lax.broadcasted_iota(jnp.int32, (1, ATT_HEADS), 1)
        head_of_lane = lax.broadcasted_iota(jnp.int32, (1, 256), 1) // HEAD_DIM
        lse_out = jnp.zeros((BLOCK, ATT_HEADS), F32)
        for hk in range(KV_HEADS):
            sl = slice(256 * hk, 256 * (hk + 1))
            qg = q_ref[:, sl]
            keys = jnp.concatenate([kp[:, sl], ko[:, sl], kn[:, sl], kc[:, sl]], axis=0)
            vals = jnp.concatenate([vp[:, sl], vo[:, sl], vn[:, sl], vc[:, sl]], axis=0)
            yg = jnp.zeros((BLOCK, 256), F32)
            for g in range(4):
                q1 = jnp.where(head_of_lane == g, qg, jnp.zeros_like(qg))
                s = jnp.where(valid, _dot_nt(q1, keys), NEG)
                sink = sink_ref[0:1, 4 * hk + g:4 * hk + g + 1]
                m = jnp.maximum(jnp.max(s, axis=1, keepdims=True), sink)
                p = jnp.exp(s - m)
                den = jnp.sum(p, axis=1, keepdims=True) + jnp.exp(sink - m)
                o1 = _dot(p.astype(BF16), vals) * (1.0 / den)
                yg = yg + jnp.where(head_of_lane == g, o1, 0.0)
                lse_out = lse_out + jnp.where(lane8 == 4 * hk + g, m + jnp.log(den), 0.0)
            y_ref[:, sl] = yg.astype(BF16)
        lse_ref[...] = lse_out

    return _pcall(
        body, carried, name="attn_fwd", grid=(N_BLOCKS,),
        in_specs=[pl.BlockSpec((BLOCK, ATW), lambda i: (i, 0))] + _attn_window_specs()
        + _attn_window_specs() + [_full((1, ATT_HEADS))],
        out_specs=[pl.BlockSpec((BLOCK, ATW), lambda i: (i, 0)),
                   pl.BlockSpec((BLOCK, ATT_HEADS), lambda i: (i, 0))],
        out_shape=[jax.ShapeDtypeStruct((S, ATW), BF16), jax.ShapeDtypeStruct((S, ATT_HEADS), F32)],
        scratch_shapes=[],
        operands=[q, k_rep, k_rep, k_rep, k_rep, v_rep, v_rep, v_rep, v_rep, sinks])


def _attn_bwd(q, k_rep, v_rep, sinks, y_at, lse, dy, carried=None):
    def body(q_ref, kp, ko, kn, kc, vp, vo, vn, vc, sink_ref, y_ref, lse_ref, dy_ref,
             dq_ref, dk_ref, dv_ref, dsink_ref, dk_acc, dv_acc):
        i = pl.program_id(0)

        @pl.when(i == 0)
        def _():
            dk_acc[...] = jnp.zeros_like(dk_acc)
            dv_acc[...] = jnp.zeros_like(dv_acc)
            dk_ref[pl.ds(0, L), :] = jnp.zeros((L, ATW), F32)
            dv_ref[pl.ds(0, L), :] = jnp.zeros((L, ATW), F32)
            dsink_ref[...] = jnp.zeros_like(dsink_ref)

        valid = _attn_valid(i, 4, False)
        lane8 = lax.broadcasted_iota(jnp.int32, (1, ATT_HEADS), 1)
        w0 = pl.multiple_of(i * BLOCK, BLOCK)
        dsink = jnp.zeros((1, ATT_HEADS), F32)
        for hk in range(KV_HEADS):
            sl = slice(256 * hk, 256 * (hk + 1))
            q4 = _stack_heads(q_ref[:, sl])
            do4f = _stack_heads(dy_ref[:, sl])
            o4 = _stack_heads(y_ref[:, sl]).astype(F32)
            do4 = do4f.astype(BF16)
            kl = jnp.concatenate([kp[:, sl], ko[:, sl], kn[:, sl]], axis=0)
            vl = jnp.concatenate([vp[:, sl], vo[:, sl], vn[:, sl]], axis=0)
            lse4 = jnp.concatenate(
                [jnp.sum(jnp.where(lane8 == 4 * hk + g, lse_ref[...], 0.0), axis=1, keepdims=True)
                 for g in range(4)], axis=0)
            p_loc = jnp.where(valid, jnp.exp(_dot_nt(q4, kl) - lse4), 0.0)
            p_ctx = jnp.exp(_dot_nt(q4, kc[:, sl]) - lse4)
            delta = jnp.sum(do4f * o4, axis=1, keepdims=True)
            ds_loc = (p_loc * (_dot_nt(do4, vl) - delta)).astype(BF16)
            ds_ctx = (p_ctx * (_dot_nt(do4, vc[:, sl]) - delta)).astype(BF16)
            dq_ref[:, sl] = _unstack_heads(_dot(ds_loc, kl) + _dot(ds_ctx, kc[:, sl]))
            dk_acc[pl.ds(w0, 3 * BLOCK), sl] += _dot_tn(ds_loc, q4)
            dv_acc[pl.ds(w0, 3 * BLOCK), sl] += _dot_tn(p_loc.astype(BF16), do4)
            dk_ref[pl.ds(0, L), sl] += _dot_tn(ds_ctx, q4)
            dv_ref[pl.ds(0, L), sl] += _dot_tn(p_ctx.astype(BF16), do4)
            p_sink = jnp.exp(_sink_rows(sink_ref, hk) - lse4)
            for g in range(4):
                rows = slice(g * BLOCK, (g + 1) * BLOCK)
                dsink = dsink + jnp.where(lane8 == 4 * hk + g,
                                          -jnp.sum(p_sink[rows] * delta[rows], axis=0, keepdims=True), 0.0)
        dsink_ref[...] += dsink

        @pl.when(i == N_BLOCKS - 1)
        def _():
            dk_ref[pl.ds(L, S), :] = dk_acc[pl.ds(BLOCK, S), :]
            dv_ref[pl.ds(L, S), :] = dv_acc[pl.ds(BLOCK, S), :]

    row_q = pl.BlockSpec((BLOCK, ATW), lambda i: (i, 0))
    return _pcall(
        body, carried, name="attn_bwd", grid=(N_BLOCKS,),
        in_specs=[row_q] + _attn_window_specs() + _attn_window_specs()
        + [_full((1, ATT_HEADS)), row_q, pl.BlockSpec((BLOCK, ATT_HEADS), lambda i: (i, 0)), row_q],
        out_specs=[row_q, _full((T, ATW)), _full((T, ATW)), _full((1, ATT_HEADS))],
        out_shape=[jax.ShapeDtypeStruct((S, ATW), F32), jax.ShapeDtypeStruct((T, ATW), F32),
                   jax.ShapeDtypeStruct((T, ATW), F32), jax.ShapeDtypeStruct((1, ATT_HEADS), F32)],
        scratch_shapes=[pltpu.VMEM((S + 2 * BLOCK, ATW), F32), pltpu.VMEM((S + 2 * BLOCK, ATW), F32)],
        operands=[q, k_rep, k_rep, k_rep, k_rep, v_rep, v_rep, v_rep, v_rep, sinks, y_at, lse, dy])


def _merge_fwd(y_hg, y_at, p_c, x, w_bh, w_ba, w_out, g1, nfw, sh2, sc2, carried=None):
    def body(yh_ref, ya_ref, g_ref, x_ref, wbh_ref, wba_ref, wo_ref, g1_ref, nfw_ref, sh_ref, sc_ref,
             a_ref, b_ref, mx_ref, r_ref, x1_ref, h2_ref):
        a = _dot_nt(yh_ref[...], wbh_ref[...])
        b = _dot_nt(ya_ref[...], wba_ref[...])
        mixed = (_sigmoid(g_ref[:, :D]) * a + _sigmoid(g_ref[:, D:]) * b).astype(BF16)
        r = _dot(mixed, wo_ref[...])
        x1 = x_ref[...] + g1_ref[...] * r
        a_ref[...] = a
        b_ref[...] = b
        mx_ref[...] = mixed
        r_ref[...] = r
        x1_ref[...] = x1
        h2_ref[...] = _rms_mod(x1, nfw_ref[...], sh_ref[...], sc_ref[...]).astype(BF16)

    row = lambda w: pl.BlockSpec((TM, w), lambda i: (i, 0))
    vec = _full((1, D))
    return _pcall(
        body, carried, name="merge_fwd", grid=(N_LAT_TILES,),
        in_specs=[row(HGW), row(ATW), row(WC), row(D), _VMEM_WHOLE, _VMEM_WHOLE, _VMEM_WHOLE,
                  vec, vec, vec, vec],
        out_specs=[row(D)] * 6,
        out_shape=[jax.ShapeDtypeStruct((S, D), dt) for dt in (F32, F32, BF16, F32, F32, BF16)],
        scratch_shapes=[], operands=[y_hg, y_at, p_c, x, w_bh, w_ba, w_out, g1, nfw, sh2, sc2])


def _merge_bwd(dx1, r, a, b, p_c, w_bh, w_ba, w_out, g1, carried=None):
    def body(dx_ref, r_ref, a_ref, b_ref, g_ref, wbh_ref, wba_ref, wo_ref, g1_ref,
             dr_ref, da_ref, db_ref, dg_ref, dyh_ref, dya_ref, acc_ref):
        @pl.when(pl.program_id(0) == 0)
        def _():
            acc_ref[...] = jnp.zeros_like(acc_ref)

        dx1v = dx_ref[...]
        acc_ref[0:1, :] += jnp.sum(dx1v * r_ref[...], axis=0, keepdims=True)
        dr = (g1_ref[...] * dx1v).astype(BF16)
        dr_ref[...] = dr
        dmix = _dot_nt(dr, wo_ref[...])
        sh, sa = _sigmoid(g_ref[:, :D]), _sigmoid(g_ref[:, D:])
        da = (dmix * sh).astype(BF16)
        db = (dmix * sa).astype(BF16)
        da_ref[...] = da
        db_ref[...] = db
        dg_ref[:, :D] = (dmix * a_ref[...] * sh * (1.0 - sh)).astype(BF16)
        dg_ref[:, D:] = (dmix * b_ref[...] * sa * (1.0 - sa)).astype(BF16)
        dyh_ref[...] = _dot(da, wbh_ref[...])
        dya_ref[...] = _dot(db, wba_ref[...])

    row = lambda w: pl.BlockSpec((TM, w), lambda i: (i, 0))
    return _pcall(
        body, carried, name="merge_bwd", grid=(N_LAT_TILES,),
        in_specs=[row(D), row(D), row(D), row(D), row(WC), _VMEM_WHOLE, _VMEM_WHOLE, _VMEM_WHOLE,
                  _full((1, D))],
        out_specs=[row(D), row(D), row(D), row(WC), row(HGW), row(ATW), _full((8, D))],
        out_shape=[jax.ShapeDtypeStruct((S, D), BF16), jax.ShapeDtypeStruct((S, D), BF16),
                   jax.ShapeDtypeStruct((S, D), BF16), jax.ShapeDtypeStruct((S, WC), BF16),
                   jax.ShapeDtypeStruct((S, HGW), F32), jax.ShapeDtypeStruct((S, ATW), F32),
                   jax.ShapeDtypeStruct((8, D), F32)],
        scratch_shapes=[], operands=[dx1, r, a, b, p_c, w_bh, w_ba, w_out, g1])


def _ffn_fused(x1, h2, tgt, w_gate, w_up, w_down, g2, nfw, sc2):
    def body(x1_ref, h2_ref, t_ref, wg_ref, wu_ref, wd_ref, g2_ref, nfw_ref, sc_ref,
             act_ref, dgt_ref, dup_ref, df_ref, dx_ref, acc_ref, gs, us):
        @pl.when(pl.program_id(0) == 0)
        def _():
            acc_ref[...] = jnp.zeros_like(acc_ref)

        h2 = h2_ref[...]
        whole = lambda w_ref: w_ref[...].reshape(D_FF, D)
        wide = lambda t_ref: jnp.concatenate([t_ref[j] for j in range(N_FF_TILES)], axis=1)
        for j in range(N_FF_TILES):
            g = _dot_nt(h2, wg_ref[j])
            u = _dot_nt(h2, wu_ref[j])
            gs[j] = g
            us[j] = u
            act_ref[j] = (g * _sigmoid(g) * u).astype(BF16)
        f = _dot(wide(act_ref), whole(wd_ref))
        x1v = x1_ref[...]
        g2 = g2_ref[...]
        diff = x1v + g2 * f - t_ref[...]
        dy = diff * (1.0 / D)
        df = (g2 * dy).astype(BF16)
        df_ref[...] = df
        dact_all = _dot_nt(df, whole(wd_ref))
        for j in range(N_FF_TILES):
            g, u = gs[j], us[j]
            sg = _sigmoid(g)
            dact = dact_all[:, j * FF_TILE:(j + 1) * FF_TILE]
            dgt_ref[j] = (dact * u * (sg * (1.0 + g * (1.0 - sg)))).astype(BF16)
            dup_ref[j] = (dact * (g * sg)).astype(BF16)
        dh2 = _dot(wide(dgt_ref), whole(wg_ref)) + _dot(wide(dup_ref), whole(wu_ref))
        dx, dsh, dsc, dnw = _rms_mod_bwd(x1v, nfw_ref[...], sc_ref[...], dh2)
        dx_ref[...] = dy + dx
        acc_ref[0:1, :] += dsh
        acc_ref[1:2, :] += dsc
        acc_ref[2:3, :] += dnw
        acc_ref[3:4, :] += jnp.sum(dy * f, axis=0, keepdims=True)
        acc_ref[4:5, :] += 0.5 * jnp.sum(jnp.sum(diff * diff, axis=1, keepdims=True), axis=0,
                                         keepdims=True) * (1.0 / D)

    row = lambda dt_w: pl.BlockSpec((TM, dt_w), lambda i: (i, 0))
    blk = pl.BlockSpec((N_FF_TILES, TM, FF_TILE), lambda i: (0, i, 0))
    vec = _full((1, D))
    return pl.pallas_call(
        body, name="ffn_fused", grid=(N_LAT_TILES,),
        in_specs=[row(D), row(D), row(D), _VMEM_WHOLE, _VMEM_WHOLE, _VMEM_WHOLE, vec, vec, vec],
        out_specs=[blk, blk, blk, row(D), row(D), _full((8, D))],
        out_shape=[jax.ShapeDtypeStruct((N_FF_TILES, S, FF_TILE), BF16)] * 3
        + [jax.ShapeDtypeStruct((S, D), BF16), jax.ShapeDtypeStruct((S, D), F32),
           jax.ShapeDtypeStruct((8, D), F32)],
        scratch_shapes=[pltpu.VMEM((N_FF_TILES, TM, FF_TILE), F32), pltpu.VMEM((N_FF_TILES, TM, FF_TILE), F32)],
        compiler_params=_cp(("arbitrary",)),
    )(x1, h2, tgt, w_gate, w_up, w_down, g2, nfw, sc2)


def _proj_bc(h_all, w_b, w_c, carried=None):
    def body(h_ref, wb_ref, wc_ref, pb_ref, pc_ref):
        h = h_ref[...]
        pb_ref[...] = _dot_nt(h, wb_ref[...])

        @pl.when(pl.program_id(0) > 0)
        def _():
            pc_ref[...] = _dot_nt(h, wc_ref[...])

    return _pcall(
        body, carried, name="proj_bc", grid=(N_TILES,),
        in_specs=[pl.BlockSpec((TM, D), lambda i: (i, 0)), _VMEM_WHOLE, _VMEM_WHOLE],
        out_specs=[pl.BlockSpec((TM, WB), lambda i: (i, 0)), pl.BlockSpec((TM, WC), lambda i: (_lat(i), 0))],
        out_shape=[jax.ShapeDtypeStruct((T, WB), F32), jax.ShapeDtypeStruct((S, WC), F32)],
        scratch_shapes=[], operands=[h_all, w_b, w_c])


def _input_bwd(dp_a, dp_b, dp_c, w_a, w_b, w_c, ctx, x, dx1, nw, sh, sc, carried=None):
    def body(da_ref, db_ref, dc_ref, wa_ref, wb_ref, wc_ref, ctx_ref, x_ref, dx1_ref, nw_ref, sh_ref,
             sc_ref, gx_ref, acc_ref):
        i = pl.program_id(0)

        @pl.when(i == 0)
        def _():
            acc_ref[...] = jnp.zeros_like(acc_ref)

        dh = _dot(da_ref[...], wa_ref[...]) + _dot(db_ref[...], wb_ref[...])

        @pl.when(i == 0)
        def _():
            _, dsh, dsc, dnw = _rms_mod_bwd(ctx_ref[...], nw_ref[...], sc_ref[0:1, :], dh)
            acc_ref[3:4, :] += dsh
            acc_ref[4:5, :] += dsc
            acc_ref[2:3, :] += dnw

        @pl.when(i > 0)
        def _():
            dhl = dh + _dot(dc_ref[...], wc_ref[...])
            dx, dsh, dsc, dnw = _rms_mod_bwd(x_ref[...], nw_ref[...], sc_ref[1:2, :], dhl)
            gx_ref[...] = dx1_ref[...] + dx
            acc_ref[0:1, :] += dsh
            acc_ref[1:2, :] += dsc
            acc_ref[2:3, :] += dnw

    lat = lambda w: pl.BlockSpec((TM, w), lambda i: (_lat(i), 0))
    return _pcall(
        body, carried, name="input_bwd", grid=(N_TILES,),
        in_specs=[pl.BlockSpec((TM, WA), lambda i: (i, 0)), pl.BlockSpec((TM, WB), lambda i: (i, 0)),
                  lat(WC), _VMEM_WHOLE, _VMEM_WHOLE, _VMEM_WHOLE, _full((TM, D)), lat(D), lat(D),
                  _full((1, D)), _full((2, D)), _full((2, D))],
        out_specs=[lat(D), _full((8, D))],
        out_shape=[jax.ShapeDtypeStruct((S, D), F32), jax.ShapeDtypeStruct((8, D), F32)],
        scratch_shapes=[], operands=[dp_a, dp_b, dp_c, w_a, w_b, w_c, ctx, x, dx1, nw, sh, sc])


_C1 = 1.0 - ADAM_B1 ** ADAM_STEP
_C2 = 1.0 - ADAM_B2 ** ADAM_STEP


def _adamw_math(w, g, m, v):
    m = ADAM_B1 * m + (1.0 - ADAM_B1) * g
    v = ADAM_B2 * v + (1.0 - ADAM_B2) * (g * g)
    m_hat = m / _C1
    v_hat = v / _C2
    delta = -ADAM_LR * (m_hat / (jnp.sqrt(v_hat) + ADAM_EPS) + ADAM_WD * w)
    return delta, m, v


def _adamw_sharded(terms, w, m, v, name, tr):
    rows, cols = w.shape

    def body(t_ref, w_ref, m_ref, v_ref, g_ref, d_ref, nm_ref, nv_ref):
        g = t_ref[0].astype(F32)
        for s in range(1, N_CHIPS):
            g = g + t_ref[s].astype(F32)
        g_ref[...] = g
        d_ref[...], nm_ref[...], nv_ref[...] = _adamw_math(w_ref[...], g, m_ref[...], v_ref[...])

    blk = pl.BlockSpec((tr, cols), lambda i: (i, 0))
    return pl.pallas_call(
        body, name=name, grid=(rows // tr,),
        in_specs=[pl.BlockSpec((N_CHIPS, tr, cols), lambda i: (0, i, 0)), blk, blk, blk],
        out_specs=[blk] * 4,
        out_shape=[jax.ShapeDtypeStruct((rows, cols), F32)] * 4,
        compiler_params=_cp(("parallel",)),
    )(terms, w, m, v)


def _adamw_plain(g, w, m, v, name):
    def body(g_ref, w_ref, m_ref, v_ref, d_ref, nm_ref, nv_ref):
        d_ref[...], nm_ref[...], nv_ref[...] = _adamw_math(w_ref[...], g_ref[...], m_ref[...], v_ref[...])

    return pl.pallas_call(
        body, name=name, in_specs=[_VMEM_WHOLE] * 4, out_specs=[_VMEM_WHOLE] * 3,
        out_shape=[jax.ShapeDtypeStruct(w.shape, F32)] * 3,
        compiler_params=_cp(),
    )(g, w, m, v)


SMALL_ROWS = 16
R_DMOD, R_DCTX, R_NMIX, R_NFFN, R_MISC, R_DLB, R_BADA01 = 0, 6, 8, 9, 10, 11, 13
M_HNW, M_QNW, M_KNW, M_SINK, M_LOSS = 0, 128, 256, 384, 512


def _pack_small(acc_in, acc_mg, acc_ffn, acc_prep, dsink, dlb):
    def body(in_ref, mg_ref, ff_ref, pp_ref, ds_ref, dlb_ref, o_ref):
        o_ref[...] = jnp.zeros_like(o_ref)
        o_ref[0:2, :] = in_ref[0:2, :]
        o_ref[2:3, :] = mg_ref[0:1, :]
        o_ref[3:5, :] = ff_ref[0:2, :]
        o_ref[5:6, :] = ff_ref[3:4, :]
        o_ref[6:8, :] = in_ref[3:5, :]
        o_ref[8:9, :] = in_ref[2:3, :]
        o_ref[9:10, :] = ff_ref[2:3, :]
        o_ref[10:11, M_HNW:M_HNW + HG_DIM] = pp_ref[0:1, 0:HG_DIM]
        r = lax.broadcasted_iota(jnp.int32, (ATW, 128), 0)
        c = lax.broadcasted_iota(jnp.int32, (ATW, 128), 1)
        fold = jnp.where((r % HEAD_DIM == c) & (c < HEAD_DIM), 1.0, 0.0).astype(BF16)
        qk = jnp.concatenate([pp_ref[1:2, :], pp_ref[2:3, :], jnp.zeros((6, ATW), F32)], axis=0)
        folded = _dot_exact_rhs01(qk, fold)
        o_ref[10:11, M_QNW:M_QNW + 128] = folded[0:1, :]
        o_ref[10:11, M_KNW:M_KNW + 128] = folded[1:2, :]
        o_ref[10:11, M_SINK:M_SINK + ATT_HEADS] = ds_ref[...]
        o_ref[10:11, M_LOSS:M_LOSS + 128] = ff_ref[4:5, 0:128]
        o_ref[11:13, 0:HGW] = dlb_ref[...]

    return pl.pallas_call(
        body, name="pack_small", in_specs=[_VMEM_WHOLE] * 6, out_specs=_VMEM_WHOLE,
        out_shape=jax.ShapeDtypeStruct((SMALL_ROWS, D), F32), compiler_params=_cp(),
    )(acc_in, acc_mg, acc_ffn, acc_prep, dsink, dlb)


def _sum_small(gathered):
    def body(g_ref, o_ref):
        tot = g_ref[0]
        for s in range(1, N_DEV):
            tot = tot + g_ref[s]
        o_ref[...] = tot
        o_ref[R_BADA01:R_BADA01 + 2, :] = tot[0:2, :] + tot[R_DCTX:R_DCTX + 2, :]

    return pl.pallas_call(
        body, name="sum_small", in_specs=[_VMEM_WHOLE], out_specs=_VMEM_WHOLE,
        out_shape=jax.ShapeDtypeStruct((SMALL_ROWS, D), F32), compiler_params=_cp(),
    )(gathered)


_REP_NAMES = ("b_ada", "c_ctx", "norm_mix_w", "norm_ffn_w", "hgrn_norm_w", "q_norm_w", "k_norm_w", "attn_sinks")


def _adamw_replicated(tot, g_c_ctx, ws, ms, vs):
    n = len(_REP_NAMES)

    def body(*refs):
        tot_ref, gc_ref = refs[0], refs[1]
        w_refs, m_refs, v_refs = refs[2:2 + n], refs[2 + n:2 + 2 * n], refs[2 + 2 * n:2 + 3 * n]
        outs = refs[2 + 3 * n:]
        row = lambda r: tot_ref[r:r + 1, :]
        misc = row(R_MISC)
        grads = [jnp.concatenate([row(R_BADA01), row(R_BADA01 + 1)] + [row(k) for k in range(2, 6)], axis=1),
                 gc_ref[...], row(R_NMIX), row(R_NFFN),
                 misc[:, M_HNW:M_HNW + HG_DIM], misc[:, M_QNW:M_QNW + HEAD_DIM],
                 misc[:, M_KNW:M_KNW + HEAD_DIM], misc[:, M_SINK:M_SINK + ATT_HEADS]]
        for k in range(n):
            outs[k][...] = grads[k]
            outs[n + k][...], outs[2 * n + k][...], outs[3 * n + k][...] = _adamw_math(
                w_refs[k][...], grads[k], m_refs[k][...], v_refs[k][...])

    shapes = [jax.ShapeDtypeStruct(w.shape, F32) for w in ws]
    return pl.pallas_call(
        body, name="adamw_replicated", in_specs=[_VMEM_WHOLE] * (2 + 3 * n), out_specs=[_VMEM_WHOLE] * (4 * n),
        out_shape=shapes * 4, compiler_params=_cp(),
    )(tot, g_c_ctx, *ws, *ms, *vs)


def _lb_grads(dlb, lbl):
    def body(d_ref, l_ref, o_ref):
        for d in (0, 1):
            ll = l_ref[d]
            lb = _sigmoid(ll[0:1, :] - ll[1:2, :])
            t = d_ref[d:d + 1, :] * lb * (1.0 - lb)
            o_ref[d, 0:1, :] = t
            o_ref[d, 1:2, :] = -t

    return pl.pallas_call(
        body, name="lb_grads", in_specs=[_VMEM_WHOLE] * 2, out_specs=_VMEM_WHOLE,
        out_shape=jax.ShapeDtypeStruct((2, 2, HGW), F32), compiler_params=_cp(),
    )(dlb, lbl)


def _c_ctx_grad(terms, c_ctx):
    def body(t_ref, c_ref, o_ref):
        tot = t_ref[0, 8:9, :]
        for s in range(1, N_DEV):
            tot = tot + t_ref[s, 8:9, :]
        cv = c_ref[...]
        sg = _sigmoid(cv)
        o_ref[...] = tot * (sg * (1.0 + cv * (1.0 - sg)))

    return pl.pallas_call(
        body, name="c_ctx_grad", in_specs=[_VMEM_WHOLE] * 2, out_specs=_VMEM_WHOLE,
        out_shape=jax.ShapeDtypeStruct((1, D), F32), compiler_params=_cp(),
    )(terms, c_ctx)


def _in_perm():
    fz, bz, inp, kk, vv, qhg, ghg, qat, gates = 0, 512, 1024, 1536, 1664, 1792, 2304, 2816, 3328
    cols = []
    for h in range(HG_HEADS):
        for base in (fz, bz, inp, qhg):
            cols += list(range(base + 128 * h, base + 128 * (h + 1)))
    cols += list(range(ghg, ghg + 512)) + list(range(qat, qat + 512))
    cols += list(range(kk, kk + 128)) + list(range(vv, vv + 128))
    cols += list(range(gates, gates + 2048))
    return np.asarray(cols, np.int32)


_PERM = _in_perm()
_INV_PERM = np.argsort(_PERM).astype(np.int32)


_PIECES = {"a": (0, WA, 128), "b": (WA, WB, 256), "c": (WA + WB, WC, 256)}


def _block_table(piece):
    lo, n, blk = _PIECES[piece]
    starts = [int(_PERM[r]) for r in range(lo, lo + n, blk)]
    assert all(s % blk == 0 and np.array_equal(_PERM[r:r + blk], np.arange(s, s + blk))
               for s, r in zip(starts, range(lo, lo + n, blk)))
    return jnp.asarray([s // blk for s in starts], jnp.int32), blk


def _pick_row_blocks(x, table, blk, name):
    cols = x.shape[1]

    def body(t_ref, x_ref, o_ref):
        o_ref[...] = x_ref[...]

    return pl.pallas_call(
        body, name=name,
        grid_spec=pltpu.PrefetchScalarGridSpec(
            num_scalar_prefetch=1, grid=(table.shape[0],),
            in_specs=[pl.BlockSpec((blk, cols), lambda i, t: (t[i], 0))],
            out_specs=pl.BlockSpec((blk, cols), lambda i, t: (i, 0))),
        out_shape=jax.ShapeDtypeStruct((table.shape[0] * blk, cols), x.dtype),
        compiler_params=_cp(("arbitrary",)),
    )(table, x)


def _place_row_blocks(x, table, blk, into, out_rows, name):
    cols = x.shape[1]

    def body(t_ref, x_ref, *rest):
        rest[-1][...] = x_ref[...]

    operands, in_specs, aliases = [table, x], [pl.BlockSpec((blk, cols), lambda i, t: (i, 0))], {}
    if into is not None:
        operands.append(into)
        in_specs.append(_ANY)
        aliases = {2: 0}
    return pl.pallas_call(
        body, name=name,
        grid_spec=pltpu.PrefetchScalarGridSpec(
            num_scalar_prefetch=1, grid=(table.shape[0],), in_specs=in_specs,
            out_specs=pl.BlockSpec((blk, cols), lambda i, t: (t[i], 0))),
        out_shape=jax.ShapeDtypeStruct((out_rows, cols), x.dtype),
        input_output_aliases=aliases,
        compiler_params=_cp(("arbitrary",)),
    )(*operands)


def _cols_from_blocks(g):
    return jnp.transpose(g, (1, 0, 2)).reshape(g.shape[1], N_DEV * g.shape[2])


def _local_step(x2, ctx2, tgt, lbl, sh_in, sc_in, gate1, sh2, sc2, gate2, norm_mix_w, norm_ffn_w,
                hgrn_norm_w, q_norm_w, k_norm_w, attn_sinks, w_a, w_b, w_c, s_bh, s_ba, s_out,
                s_gate, s_up, s_down):
    first_last = lambda n: [(0, True), (n - 1, False)]
    h_all = _norm_mod_all(ctx2, x2, norm_mix_w, sh_in, sc_in)
    p_a = _mm_nt(h_all, w_a, tm=768, tn=1024, out_dtype=F32, name="proj_a")
    (o, st), (g_gate,) = _hgrn_fwd(
        p_a, lbl, (_gather_comm([s_gate]), [(0, True), (HG_HEADS - 2, True), (HG_HEADS - 1, False)]))
    (p_b, p_c), (g_bh, g_ba, g_out) = _proj_bc(
        h_all, w_b, w_c, (_gather_comm([s_bh, s_ba, s_out]), [(0, True), (N_TILES - 2, True), (N_TILES - 1, False)]))
    cos, sin = _rope_tables()
    qnw_t, knw_t = jnp.tile(q_norm_w, (1, ATT_HEADS)), jnp.tile(k_norm_w, (1, KV_HEADS))
    y_hg, qn, k_rep, v_rep = _prep_fwd(p_b, o, cos, sin, hgrn_norm_w, qnw_t, knw_t)
    (y_at, lse), (g_up, g_down) = _attn_fwd(
        qn, k_rep, v_rep, attn_sinks,
        (_gather_comm([s_up, s_down]), [(0, True), (N_BLOCKS - 3, True), (N_BLOCKS - 1, False)]))
    w_bh, w_ba, w_o = g_bh.reshape(D, HGW), g_ba.reshape(D, ATW), g_out.reshape(D, D)
    (a, b, mixed, r, x1, h2), _ = _merge_fwd(
        y_hg, y_at, p_c, x2, w_bh, w_ba, w_o, gate1, norm_ffn_w, sh2, sc2)
    g_gate, g_up, g_down = [g.reshape(N_FF_TILES, FF_TILE, D) for g in (g_gate, g_up, g_down)]

    act, d_gate, d_up, d_f, dx1, acc_ffn = _ffn_fused(x1, h2, tgt, g_gate, g_up, g_down, gate2,
                                                      norm_ffn_w, sc2)
    by_chip = lambda t: t.reshape((N_CHIPS, 2) + t.shape[1:])
    ff_by_chip = lambda t: t.reshape(N_CHIPS, 2, FF_BLK, D)
    t_down, _ = _mm_tn_blocked(act, d_f, "grad_down")
    t_down = ff_by_chip(t_down)
    t_gate, (f_down,) = _mm_tn_blocked(d_gate, h2, "grad_gate", (_sibling_comm([t_down]), first_last(N_FF_TILES)))
    t_gate = ff_by_chip(t_gate)
    t_up, (f_gate,) = _mm_tn_blocked(d_up, h2, "grad_up", (_sibling_comm([t_gate]), first_last(N_FF_TILES)))
    t_up = ff_by_chip(t_up)

    (d_r, d_a, d_b, dp_c, dy_hg, dy_at, acc_mg), (f_up,) = _merge_bwd(
        dx1, r, a, b, p_c, w_bh, w_ba, w_o, gate1, (_sibling_comm([t_up]), first_last(N_LAT_TILES)))
    c_down, c_gate, c_up = [_pair_sum(t, f, "pair_sum_" + nm) for t, f, nm in
                            ((t_down, f_down, "down"), (t_gate, f_gate, "gate"), (t_up, f_up, "up"))]
    t_out = _mm_tn(mixed, d_r, tk=512, nk=4, tm=512, tn=1024, out_dtype=BF16, name="grad_out")
    t_bh = _mm_tn(d_a, y_hg, tk=512, nk=4, tm=512, tn=512, out_dtype=BF16, name="grad_bh")
    t_ba = _mm_tn(d_b, y_at, tk=512, nk=4, tm=512, tn=512, out_dtype=BF16, name="grad_ba")
    t_bh, t_ba, t_out = [by_chip(t.reshape(N_DEV, D // N_DEV, t.shape[1])) for t in (t_bh, t_ba, t_out)]
    (dq, dk_rep, dv_rep, dsink), (r_up,) = _attn_bwd(
        qn, k_rep, v_rep, attn_sinks, y_at, lse, dy_at, (_chip_comm([c_up]), first_last(N_BLOCKS)))
    (dp_b, d_o, acc_prep), (f_bh, f_ba, f_out) = _prep_bwd(
        p_b, o, cos, sin, hgrn_norm_w, qnw_t, knw_t, dy_hg, dq, dk_rep, dv_rep,
        (_sibling_comm([t_bh, t_ba, t_out]), first_last(N_TILES)))
    c_bh, c_ba, c_out = [_pair_sum(t, f, "pair_sum_" + nm) for t, f, nm in
                         ((t_bh, f_bh, "bh"), (t_ba, f_ba, "ba"), (t_out, f_out, "out"))]
    (dp_a, dlb), (r_bh, r_ba, r_out, r_down, r_gate) = _hgrn_bwd(
        p_a, lbl, d_o, st, (_chip_comm([c_bh, c_ba, c_out, c_down, c_gate]), first_last(HG_HEADS)))
    t_a = _mm_tn(dp_a, h_all, tk=768, nk=3, tm=1024, tn=1024, out_dtype=BF16, name="grad_in_a")
    t_b = _mm_tn(dp_b, h_all, tk=768, nk=3, tm=640, tn=1024, out_dtype=BF16, name="grad_in_b")
    t_c = _mm_tn(dp_c, h_all, tk=256, nk=8, b_off=1, tm=1024, tn=1024, out_dtype=BF16, name="grad_in_c")
    t_in = None
    for piece, nm in ((t_a, "a"), (t_b, "b"), (t_c, "c")):
        t_in = _place_row_blocks(piece, *_block_table(nm), t_in, IN_COLS, "order_terms_" + nm)
    t_in = by_chip(t_in.reshape(N_DEV, IN_BLK, D))
    (f_in,) = _run_comm(_sibling_comm([t_in]), "scatter_in_sibling")
    (grad_x, acc_in), (r_in,) = _input_bwd(
        dp_a, dp_b, dp_c, w_a, w_b, w_c, ctx2, x2, dx1, norm_mix_w, sh_in, sc_in,
        (_chip_comm([_pair_sum(t_in, f_in, "pair_sum_in")]), first_last(N_TILES)))
    small = _pack_small(acc_in, acc_mg, acc_ffn, acc_prep, dsink, dlb)
    return grad_x, small, [r_in, r_bh, r_ba, r_out, r_gate, r_up, r_down]


def kernel(x, c, ctx, c_ctx, w_ada, b_ada, norm_mix_w, norm_ffn_w, w_in, hgrn_lb_logits, hgrn_norm_w, q_norm_w, k_norm_w, attn_sinks, w_branch_hgrn, w_branch_attn, w_out, w_ffn_gate, w_ffn_up, w_ffn_down, loss_target, m_c_ctx, m_w_ada, m_b_ada, m_norm_mix_w, m_norm_ffn_w, m_w_in, m_hgrn_lb_logits, m_hgrn_norm_w, m_q_norm_w, m_k_norm_w, m_attn_sinks, m_w_branch_hgrn, m_w_branch_attn, m_w_out, m_w_ffn_gate, m_w_ffn_up, m_w_ffn_down, v_c_ctx, v_w_ada, v_b_ada, v_norm_mix_w, v_norm_ffn_w, v_w_in, v_hgrn_lb_logits, v_hgrn_norm_w, v_q_norm_w, v_k_norm_w, v_attn_sinks, v_w_branch_hgrn, v_w_branch_attn, v_w_out, v_w_ffn_gate, v_w_ffn_up, v_w_ffn_down):
    me = 4 * lax.axis_index("x") + 2 * lax.axis_index("y") + lax.axis_index("c")
    x2, ctx2, tgt = x[0], ctx[0], loss_target[0]
    w_ada2, w_in2 = w_ada[0], w_in[0]

    cond = jnp.zeros((8, D), F32).at[0].set(c[0]).at[1, :256].set(hgrn_lb_logits.reshape(256))
    b_cols = lax.dynamic_slice(b_ada, (0, me * ADA_BLK), (1, ADA_BLK))
    g0, cc, g1, g_in = _prologue(cond, c_ctx.reshape(1, D), w_ada2, b_cols, w_in2.T.astype(BF16))
    lbl = jnp.transpose(g0[:, 1, :256].reshape(N_DEV, 2, 2, 64), (1, 2, 0, 3)).reshape(2, 2, HGW)
    mod_all = _cols_from_blocks(g1)
    mod = lax.dynamic_slice(mod_all, (me, 0), (1, 6 * D)).reshape(6, D)
    mod_c = mod_all[8].reshape(6, D)
    sh1, sc1, gate1, sh2, sc2, gate2 = [mod[k:k + 1] for k in range(6)]
    sh_in = jnp.concatenate([mod_c[0:1], sh1], axis=0)
    sc_in = jnp.concatenate([mod_c[1:2], sc1], axis=0)

    shards = [w_branch_hgrn[0].T, w_branch_attn[0].T, w_out[0], w_ffn_gate[0].T, w_ffn_up[0].T, w_ffn_down[0]]
    w_in_t = g_in.reshape(IN_COLS, D)
    w_a, w_b, w_c = [_pick_row_blocks(w_in_t, *_block_table(nm), "order_w_" + nm) for nm in "abc"]

    grad_x, small, (r_in, r_bh, r_ba, r_out, r_gate, r_up, r_down) = _local_step(
        x2, ctx2, tgt, lbl, sh_in, sc_in, gate1, sh2, sc2, gate2, norm_mix_w, norm_ffn_w, hgrn_norm_w,
        q_norm_w, k_norm_w, attn_sinks, w_a, w_b, w_c, *[s.astype(BF16) for s in shards])

    big = {}
    for nm, rr, ww, mm, vv, tr, transposed in (
            ("w_in", r_in, w_in2, m_w_in[0], v_w_in[0], 336, True),
            ("w_branch_hgrn", r_bh, w_branch_hgrn[0], m_w_branch_hgrn[0], v_w_branch_hgrn[0], 128, True),
            ("w_branch_attn", r_ba, w_branch_attn[0], m_w_branch_attn[0], v_w_branch_attn[0], 128, True),
            ("w_out", r_out, w_out[0], m_w_out[0], v_w_out[0], 128, False),
            ("w_ffn_gate", r_gate, w_ffn_gate[0], m_w_ffn_gate[0], v_w_ffn_gate[0], 352, True),
            ("w_ffn_up", r_up, w_ffn_up[0], m_w_ffn_up[0], v_w_ffn_up[0], 352, True),
            ("w_ffn_down", r_down, w_ffn_down[0], m_w_ffn_down[0], v_w_ffn_down[0], 352, False)):
        if transposed:
            res = _adamw_sharded(rr, ww.T, mm.T, vv.T, "adamw_" + nm, tr)
            big[nm] = [t.T[None] for t in res]
        else:
            big[nm] = [t[None] for t in _adamw_sharded(rr, ww, mm, vv, "adamw_" + nm, tr)]

    (g2,) = _all_gather([small], "gather_small", True)
    tot = _sum_small(g2)
    dm = jnp.zeros((16, 6 * D), F32).at[:8].set(g2[:, R_DMOD:R_DMOD + 6, :].reshape(N_DEV, 6 * D))
    dm = dm.at[8, :2 * D].set(tot[R_DCTX:R_DCTX + 2].reshape(2 * D))
    dm_cols = lax.dynamic_slice(dm, (0, me * ADA_BLK), (16, ADA_BLK))
    g_w_ada, dsc_term = _ada_grads(cc, dm_cols, w_ada2)
    (g3,) = _all_gather([dsc_term], "gather_cctx", True)
    g_c_ctx = _c_ctx_grad(g3, c_ctx.reshape(1, D))
    g_lbl = _lb_grads(tot[R_DLB:R_DLB + 2, :HGW], lbl)
    g_lb_mine = lax.dynamic_slice(g_lbl, (0, 0, me * 64), (2, 2, 64))
    misc = tot[R_MISC]
    loss = misc[M_LOSS]

    rep_out = _adamw_replicated(
        tot, g_c_ctx,
        [b_ada, c_ctx.reshape(1, D), norm_mix_w, norm_ffn_w, hgrn_norm_w, q_norm_w, k_norm_w, attn_sinks],
        [m_b_ada, m_c_ctx.reshape(1, D), m_norm_mix_w, m_norm_ffn_w, m_hgrn_norm_w, m_q_norm_w, m_k_norm_w,
         m_attn_sinks],
        [v_b_ada, v_c_ctx.reshape(1, D), v_norm_mix_w, v_norm_ffn_w, v_hgrn_norm_w, v_q_norm_w, v_k_norm_w,
         v_attn_sinks])
    rep = []
    for kind in range(4):
        vals = dict(zip(_REP_NAMES, rep_out[kind * len(_REP_NAMES):(kind + 1) * len(_REP_NAMES)]))
        vals["c_ctx"] = vals["c_ctx"].reshape(D)
        rep.append(vals)

    d_ada, nm_ada, nv_ada = _adamw_plain(g_w_ada, w_ada2, m_w_ada[0], v_w_ada[0], "adamw_w_ada")
    ada = [t[None] for t in (g_w_ada, d_ada, nm_ada, nv_ada)]
    lb_w = hgrn_lb_logits.reshape(4, 64)
    d_lb, nm_lb, nv_lb = _adamw_plain(g_lb_mine.reshape(4, 64), lb_w, m_hgrn_lb_logits.reshape(4, 64),
                                      v_hgrn_lb_logits.reshape(4, 64), "adamw_lb")
    lbs = [t.reshape(2, 2, 64) for t in (g_lb_mine, d_lb, nm_lb, nv_lb)]

    names = ['c_ctx', 'w_ada', 'b_ada', 'norm_mix_w', 'norm_ffn_w', 'w_in', 'hgrn_lb_logits', 'hgrn_norm_w',
             'q_norm_w', 'k_norm_w', 'attn_sinks', 'w_branch_hgrn', 'w_branch_attn', 'w_out', 'w_ffn_gate',
             'w_ffn_up', 'w_ffn_down']
    outs = [loss, grad_x[None]]
    for kind in range(4):
        for nm in names:
            if nm == 'w_ada':
                outs.append(ada[kind])
            elif nm == 'hgrn_lb_logits':
                outs.append(lbs[kind])
            elif nm in big:
                outs.append(big[nm][kind])
            else:
                outs.append(rep[kind][nm])
    return tuple(outs)
```

```python
import functools
import math

import numpy as np
import jax
import jax.numpy as jnp
from jax import lax
from jax.experimental import pallas as pl
from jax.experimental.pallas import tpu as pltpu

F32 = jnp.float32
BF16 = jnp.bfloat16

N_DEV = 8
D = 1024
S = 2048
L = 256
T = L + S
TM = 256
N_TILES = T // TM
N_LAT_TILES = S // TM
HG_HEADS = 4
HG_DIM = 128
HGW = 512
CHUNK = 32
N_CHUNKS = T // CHUNK
N_CTX_CHUNKS = L // CHUNK
N_LAT_CHUNKS = S // CHUNK
ATT_HEADS = 8
KV_HEADS = 2
HEAD_DIM = 64
ATW = 512
KVW = 128
BLOCK = 128
N_BLOCKS = S // BLOCK
GRID_W = 64
ROPE_THETA = 10000.0
D_FF = 2816
FF_BLK = D_FF // N_DEV
FF_TILE = 256
N_FF_TILES = D_FF // FF_TILE
IN_COLS = 5376
IN_BLK = IN_COLS // N_DEV
ADA_BLK = 6 * D // N_DEV
EPS = 1e-6
WA, WB, WC = 2048, 1280, 2048

ADAM_LR = 0.001
ADAM_B1 = 0.9
ADAM_B2 = 0.999
ADAM_EPS = 1e-08
ADAM_WD = 0.01
ADAM_STEP = 10

VMEM_LIMIT = 56 * 1024 * 1024
MESH = pl.DeviceIdType.MESH


def _cp(sem=None, vmem=VMEM_LIMIT):
    return pltpu.CompilerParams(dimension_semantics=sem, vmem_limit_bytes=vmem)


def _full(shape):
    n = len(shape)
    return pl.BlockSpec(shape, lambda *_: (0,) * n)


_VMEM_WHOLE = pl.BlockSpec(memory_space=pltpu.VMEM)
_ANY = pl.BlockSpec(memory_space=pl.ANY)


def _sigmoid(v):
    return 1.0 / (1.0 + jnp.exp(-v))


def _dot(a, b):
    return jnp.dot(a, b, preferred_element_type=F32)


def _dot_nt(a, b):
    return lax.dot_general(a, b, (((1,), (1,)), ((), ())), preferred_element_type=F32)


def _dot_tn(a, b):
    return lax.dot_general(a, b, (((0,), (0,)), ((), ())), preferred_element_type=F32)


def _split3(v):
    hi = v.astype(BF16)
    r = v - hi.astype(F32)
    mid = r.astype(BF16)
    lo = (r - mid.astype(F32)).astype(BF16)
    return hi, mid, lo


def _dot_exact_rhs01(v, m01):
    hi, mid, lo = _split3(v)
    return _dot(hi, m01) + _dot(mid, m01) + _dot(lo, m01)


def _split2(v):
    hi = v.astype(BF16)
    return hi, (v - hi.astype(F32)).astype(BF16)


def _dot_lhs01(m01, v):
    hi, lo = _split2(v)
    return _dot(m01, hi) + _dot(m01, lo)


def _dot_f32(a, b, dot=_dot):
    ah, am, al = _split3(a)
    bh, bm, bl = _split3(b)
    return (dot(ah, bh) + (dot(ah, bm) + dot(am, bh))
            + (dot(am, bm) + dot(ah, bl) + dot(al, bh)))


def _my_pos():
    return lax.axis_index("x"), lax.axis_index("y"), lax.axis_index("c")


class _Comm:
    def __init__(self, operands, out_shapes, sems, phases):
        self.operands, self.out_shapes, self.sems, self.phases = operands, out_shapes, sems, phases


def _gather_comm(blocks):
    n = len(blocks)

    def parts(ins, outs, sems):
        send_sems, recv_sems, local_sems = sems
        x, y, c = _my_pos()
        me, sibling = (x, y, c), (x, y, 1 - c)
        chips = [(1 - x, y), (x, 1 - y), (1 - x, 1 - y)]

        def slot(a, px, py, pc):
            return outs[a].at[4 * px + 2 * py + pc]

        def copy(a, k, block, to, src=None):
            return pltpu.make_async_remote_copy(
                src_ref=slot(a, *block) if src is None else src, dst_ref=slot(a, *block),
                send_sem=send_sems.at[a, k], recv_sem=recv_sems.at[a, k],
                device_id=to, device_id_type=MESH)

        mine = [pltpu.make_async_copy(ins[a], slot(a, *me), local_sems.at[a]) for a in range(n)]
        first = []
        for a in range(n):
            first.append(copy(a, 0, me, sibling, src=ins[a]))
            first += [copy(a, 1 + j, me, (*chip, c), src=ins[a]) for j, chip in enumerate(chips)]
        passed = [copy(a, 4 + j, (*chip, c), sibling) for j, chip in enumerate(chips) for a in range(n)]
        return c, me, sibling, chips, copy, mine, first, passed

    def start(ins, outs, sems):
        _, _, _, _, _, mine, first, _ = parts(ins, outs, sems)
        for cp in mine + first:
            cp.start()

    def forward(ins, outs, sems):
        c, me, _, chips, copy, _, _, passed = parts(ins, outs, sems)
        for j, chip in enumerate(chips):
            for a in range(n):
                copy(a, 1 + j, (*chip, c), me).wait_recv()
                passed[j * n + a].start()

    def finish(ins, outs, sems):
        c, me, sibling, chips, copy, mine, first, passed = parts(ins, outs, sems)
        for a in range(n):
            copy(a, 0, sibling, me).wait_recv()
            for j, chip in enumerate(chips):
                copy(a, 4 + j, (*chip, 1 - c), me).wait_recv()
        for cp in first + passed:
            cp.wait_send()
        for cp in mine:
            cp.wait()

    return _Comm(blocks, [jax.ShapeDtypeStruct((N_DEV,) + b.shape, b.dtype) for b in blocks],
                 [pltpu.SemaphoreType.DMA((n, 7)), pltpu.SemaphoreType.DMA((n, 7)), pltpu.SemaphoreType.DMA((n,))],
                 [start, forward, finish])


def _gather_comm_relayed(blocks):
    n = len(blocks)

    def parts(ins, outs, sems):
        send_sems, recv_sems, local_sems = sems
        x, y, c = _my_pos()
        me, sibling = (x, y, c), (x, y, 1 - c)
        x_nbr, y_nbr, diag = (1 - x, y, c), (x, 1 - y, c), (1 - x, 1 - y, c)

        def slot(a, dev, half=None):
            ref = outs[a].at[4 * dev[0] + 2 * dev[1] + dev[2]]
            if half is None:
                return ref
            rows = blocks[a].shape[0] // 2
            return ref.at[pl.ds(half * rows, rows)]

        def copy(a, k, block, to, half=None, src=None):
            return pltpu.make_async_remote_copy(
                src_ref=slot(a, block, half) if src is None else src, dst_ref=slot(a, block, half),
                send_sem=send_sems.at[a, k], recv_sem=recv_sems.at[a, k],
                device_id=to, device_id_type=MESH)

        mine = [pltpu.make_async_copy(ins[a], slot(a, me), local_sems.at[a]) for a in range(n)]
        return me, sibling, x_nbr, y_nbr, diag, copy, mine

    def start(ins, outs, sems):
        me, sibling, x_nbr, y_nbr, _, copy, mine = parts(ins, outs, sems)
        for cp in mine:
            cp.start()
        for a in range(n):
            for k, to in ((1, x_nbr), (2, y_nbr), (0, sibling)):
                copy(a, k, me, to, src=ins[a]).start()

    def forward(ins, outs, sems):
        me, sibling, x_nbr, y_nbr, _, copy, _ = parts(ins, outs, sems)
        for a in range(n):
            copy(a, 1, x_nbr, me).wait_recv()
            copy(a, 3, x_nbr, y_nbr, half=0).start()
            copy(a, 5, x_nbr, sibling).start()
        for a in range(n):
            copy(a, 2, y_nbr, me).wait_recv()
            copy(a, 4, y_nbr, x_nbr, half=1).start()
            copy(a, 6, y_nbr, sibling).start()

    def finish(ins, outs, sems):
        me, sibling, x_nbr, y_nbr, diag, copy, mine = parts(ins, outs, sems)
        sib = lambda dev: (dev[0], dev[1], sibling[2])
        for a in range(n):
            copy(a, 3, diag, me, half=0).wait_recv()
            copy(a, 4, diag, me, half=1).wait_recv()
            copy(a, 7, diag, sibling).start()
        for a in range(n):
            copy(a, 0, sibling, me).wait_recv()
            for k, dev in ((5, x_nbr), (6, y_nbr), (7, diag)):
                copy(a, k, sib(dev), me).wait_recv()
        for a in range(n):
            for k, block, to, half in ((0, me, sibling, None), (1, me, x_nbr, None), (2, me, y_nbr, None),
                                       (3, x_nbr, y_nbr, 0), (4, y_nbr, x_nbr, 1), (5, x_nbr, sibling, None),
                                       (6, y_nbr, sibling, None), (7, diag, sibling, None)):
                copy(a, k, block, to, half=half, src=ins[a] if block is me else None).wait_send()
        for cp in mine:
            cp.wait()

    return _Comm(blocks, [jax.ShapeDtypeStruct((N_DEV,) + b.shape, b.dtype) for b in blocks],
                 [pltpu.SemaphoreType.DMA((n, 8)), pltpu.SemaphoreType.DMA((n, 8)), pltpu.SemaphoreType.DMA((n,))],
                 [start, forward, finish])


def _run_comm(comm, name, in_vmem=False):
    n_in, n_out = len(comm.operands), len(comm.out_shapes)

    def body(*refs):
        ins, outs, sems = refs[:n_in], refs[n_in:n_in + n_out], refs[n_in + n_out:]
        for phase in comm.phases:
            phase(ins, outs, sems)

    spec = _VMEM_WHOLE if in_vmem else _ANY
    return pl.pallas_call(
        body, name=name, out_shape=comm.out_shapes, in_specs=[spec] * n_in, out_specs=[spec] * n_out,
        scratch_shapes=comm.sems,
    )(*comm.operands)


def _carrier_call(body, comm, schedule, *, name, grid, in_specs, out_specs, out_shape, scratch_shapes, operands):
    n_in, n_out, n_scr = len(in_specs), len(out_specs), len(scratch_shapes)
    c_in, c_out = len(comm.operands), len(comm.out_shapes)

    def full_body(*refs):
        ins, refs = refs[:n_in], refs[n_in:]
        cins, refs = refs[:c_in], refs[c_in:]
        outs, refs = refs[:n_out], refs[n_out:]
        couts, refs = refs[:c_out], refs[c_out:]
        scr, csems = refs[:n_scr], refs[n_scr:]
        step = pl.program_id(0)

        def run(before):
            for (at, when_before), phase in zip(schedule, comm.phases):
                if when_before == before:
                    pl.when(step == at)(functools.partial(phase, cins, couts, csems))

        run(True)
        body(*ins, *outs, *scr)
        run(False)

    res = pl.pallas_call(
        full_body, name=name, grid=grid,
        in_specs=list(in_specs) + [_ANY] * c_in, out_specs=list(out_specs) + [_ANY] * c_out,
        out_shape=list(out_shape) + list(comm.out_shapes),
        scratch_shapes=list(scratch_shapes) + list(comm.sems),
        compiler_params=_cp(("arbitrary",)),
    )(*operands, *comm.operands)
    return res[:n_out], res[n_out:]


def _pcall(body, carried, *, name, grid, in_specs, out_specs, out_shape, scratch_shapes, operands):
    if carried is None:
        res = pl.pallas_call(body, name=name, grid=grid, in_specs=in_specs, out_specs=out_specs,
                             out_shape=out_shape, scratch_shapes=scratch_shapes,
                             compiler_params=_cp(("arbitrary",)))(*operands)
        return res, ()
    return _carrier_call(body, carried[0], carried[1], name=name, grid=grid, in_specs=in_specs,
                         out_specs=out_specs, out_shape=out_shape, scratch_shapes=scratch_shapes,
                         operands=operands)


def _all_gather(blocks, name, in_vmem):
    return _run_comm(_gather_comm(blocks), name, in_vmem)


N_CHIPS = 4


def _sibling_comm(contribs):
    n = len(contribs)

    def copies(ins, outs, sems):
        send_sems, recv_sems = sems
        x, y, c = _my_pos()
        return [pltpu.make_async_remote_copy(
            src_ref=ins[a].at[pl.ds(0, N_CHIPS), 1 - c], dst_ref=outs[a],
            send_sem=send_sems.at[a], recv_sem=recv_sems.at[a],
            device_id=(x, y, 1 - c), device_id_type=MESH) for a in range(n)]

    def start(ins, outs, sems):
        for cp in copies(ins, outs, sems):
            cp.start()

    def finish(ins, outs, sems):
        cps = copies(ins, outs, sems)
        for cp in cps:
            cp.wait_recv()
        for cp in cps:
            cp.wait_send()

    return _Comm(contribs, [jax.ShapeDtypeStruct((N_CHIPS,) + b.shape[2:], b.dtype) for b in contribs],
                 [pltpu.SemaphoreType.DMA((n,)), pltpu.SemaphoreType.DMA((n,))], [start, finish])


def _pair_sum(mine, theirs, name):
    _, _, rows, cols = mine.shape
    core = lax.axis_index("c").astype(jnp.int32).reshape(1)

    def body(c_ref, m_ref, t_ref, o_ref):
        o_ref[...] = (m_ref[...].astype(F32) + t_ref[...].astype(F32)).astype(BF16)

    return pl.pallas_call(
        body, name=name,
        grid_spec=pltpu.PrefetchScalarGridSpec(
            num_scalar_prefetch=1, grid=(N_CHIPS,),
            in_specs=[pl.BlockSpec((None, None, rows, cols), lambda q, c: (q, c[0], 0, 0)),
                      pl.BlockSpec((None, rows, cols), lambda q, c: (q, 0, 0))],
            out_specs=pl.BlockSpec((None, rows, cols), lambda q, c: (q, 0, 0))),
        out_shape=jax.ShapeDtypeStruct((N_CHIPS, rows, cols), BF16),
        compiler_params=_cp(("parallel",)),
    )(core, mine, theirs)


def _chip_comm(sums):
    n = len(sums)

    def parts(ins, outs, sems):
        send_sems, recv_sems, local_sems = sems
        x, y, c = _my_pos()
        q_me = 2 * x + y
        chips = [(1 - x, y), (x, 1 - y), (1 - x, 1 - y)]
        mine = [pltpu.make_async_copy(ins[a].at[q_me], outs[a].at[q_me], local_sems.at[a]) for a in range(n)]
        sends, recvs = [], []
        for j, (px, py) in enumerate(chips):
            for a in range(n):
                q = 2 * px + py
                sends.append(pltpu.make_async_remote_copy(
                    src_ref=ins[a].at[q], dst_ref=outs[a].at[q_me],
                    send_sem=send_sems.at[a, j], recv_sem=recv_sems.at[a, j],
                    device_id=(px, py, c), device_id_type=MESH))
                recvs.append(pltpu.make_async_remote_copy(
                    src_ref=ins[a].at[q], dst_ref=outs[a].at[q],
                    send_sem=send_sems.at[a, j], recv_sem=recv_sems.at[a, j],
                    device_id=(x, y, c), device_id_type=MESH))
        return mine, sends, recvs

    def start(ins, outs, sems):
        mine, sends, _ = parts(ins, outs, sems)
        for cp in mine + sends:
            cp.start()

    def finish(ins, outs, sems):
        mine, sends, recvs = parts(ins, outs, sems)
        for cp in recvs:
            cp.wait_recv()
        for cp in sends:
            cp.wait_send()
        for cp in mine:
            cp.wait()

    return _Comm(sums, [jax.ShapeDtypeStruct(b.shape, b.dtype) for b in sums],
                 [pltpu.SemaphoreType.DMA((n, 3)), pltpu.SemaphoreType.DMA((n, 3)), pltpu.SemaphoreType.DMA((n,))],
                 [start, finish])


def _mm_nt(a, bt, *, tm, tn, out_dtype, name, row_off=0, rows=None):
    rows = a.shape[0] if rows is None else rows
    n, k = bt.shape

    def body(a_ref, b_ref, o_ref):
        o_ref[...] = _dot_nt(a_ref[...], b_ref[...]).astype(out_dtype)

    return pl.pallas_call(
        body, name=name, grid=(rows // tm, n // tn),
        in_specs=[pl.BlockSpec((tm, k), lambda i, j: (i + row_off, 0)),
                  pl.BlockSpec((tn, k), lambda i, j: (j, 0))],
        out_specs=pl.BlockSpec((tm, tn), lambda i, j: (i, j)),
        out_shape=jax.ShapeDtypeStruct((rows, n), out_dtype),
        compiler_params=_cp(("parallel", "parallel")),
    )(a, bt)


def _mm_tn(a, b, *, tk, nk, tm, tn, out_dtype, name, a_off=0, b_off=0):
    m, n = a.shape[1], b.shape[1]

    def body(a_ref, b_ref, o_ref, acc):
        kk = pl.program_id(2)

        @pl.when(kk == 0)
        def _():
            acc[...] = jnp.zeros_like(acc)

        acc[...] += _dot_tn(a_ref[...], b_ref[...])

        @pl.when(kk == nk - 1)
        def _():
            o_ref[...] = acc[...].astype(out_dtype)

    return pl.pallas_call(
        body, name=name, grid=(m // tm, n // tn, nk),
        in_specs=[pl.BlockSpec((tk, tm), lambda i, j, kk: (kk + a_off, i)),
                  pl.BlockSpec((tk, tn), lambda i, j, kk: (kk + b_off, j))],
        out_specs=pl.BlockSpec((tm, tn), lambda i, j, kk: (i, j)),
        out_shape=jax.ShapeDtypeStruct((m, n), out_dtype),
        scratch_shapes=[pltpu.VMEM((tm, tn), F32)],
        compiler_params=_cp(("parallel", "parallel", "arbitrary")),
    )(a, b)


def _mm_tn_blocked(a, b, name, carried=None):
    nb, _, w = a.shape
    n = b.shape[1]

    def body(a_ref, b_ref, o_ref):
        o_ref[...] = _dot_tn(a_ref[...], b_ref[...]).astype(BF16)

    (out,), extra = _pcall(
        body, carried, name=name, grid=(nb,),
        in_specs=[pl.BlockSpec((None, S, w), lambda j: (j, 0, 0)), _full((S, n))],
        out_specs=[pl.BlockSpec((None, w, n), lambda j: (j, 0, 0))],
        out_shape=[jax.ShapeDtypeStruct((nb, w, n), BF16)],
        scratch_shapes=[], operands=[a, b])
    return out, extra


def _prologue(cond, c_ctx, w_ada, b_cols, w_in_t):
    rows_shape = jax.ShapeDtypeStruct((16, ADA_BLK), F32)
    big, g_cond, g_mod = _gather_comm_relayed([w_in_t]), _gather_comm([cond]), _gather_comm([rows_shape])

    def body(cond_ref, cctx_ref, wada_ref, b_ref, win_ref, g0_ref, cc_ref, g1_ref, gin_ref, rows_ref, *sems):
        s_big, s_cond, s_mod = sems[0:3], sems[3:6], sems[6:9]
        big.phases[0]([win_ref], [gin_ref], s_big)
        for phase in g_cond.phases:
            phase([cond_ref], [g0_ref], s_cond)
        cc_ref[...] = jnp.zeros_like(cc_ref)
        for j in range(N_DEV):
            cc_ref[j:j + 1, :] = g0_ref[j, 0:1, :]
        cc_ref[N_DEV:N_DEV + 1, :] = cctx_ref[...]
        cv = cc_ref[...]
        rows_ref[...] = _dot_f32(cv * _sigmoid(cv), wada_ref[...]) + b_ref[...]
        for phase in g_mod.phases:
            phase([rows_ref], [g1_ref], s_mod)
        big.phases[1]([win_ref], [gin_ref], s_big)
        big.phases[2]([win_ref], [gin_ref], s_big)

    return pl.pallas_call(
        body, name="prologue",
        in_specs=[_VMEM_WHOLE] * 4 + [_ANY], out_specs=[_VMEM_WHOLE] * 3 + [_ANY],
        out_shape=[g_cond.out_shapes[0], jax.ShapeDtypeStruct((16, D), F32), g_mod.out_shapes[0],
                   big.out_shapes[0]],
        scratch_shapes=[pltpu.VMEM((16, ADA_BLK), F32)] + big.sems + g_cond.sems + g_mod.sems,
        compiler_params=_cp(),
    )(cond, c_ctx, w_ada, b_cols, w_in_t)


def _ada_grads(cc, dm_cols, w_ada):
    def body(c_ref, dm_ref, w_ref, gw_ref, dsc_ref):
        cv = c_ref[...]
        sc = cv * _sigmoid(cv)
        dm = dm_ref[...]
        gw_ref[...] = _dot_f32(sc, dm, dot=_dot_tn)
        dsc_ref[...] = _dot_f32(dm, w_ref[...], dot=_dot_nt)

    return pl.pallas_call(
        body, name="ada_grads",
        in_specs=[_VMEM_WHOLE] * 3, out_specs=[_VMEM_WHOLE] * 2,
        out_shape=[jax.ShapeDtypeStruct((D, ADA_BLK), F32), jax.ShapeDtypeStruct((16, D), F32)],
        compiler_params=_cp(),
    )(cc, dm_cols, w_ada)


def _lat(i):
    return jnp.maximum(i - 1, 0)


def _rms_mod(xv, nw, sh, sc):
    rstd = lax.rsqrt(jnp.mean(xv * xv, axis=-1, keepdims=True) + EPS)
    return (xv * rstd * nw) * (1.0 + sc) + sh


def _rms_mod_bwd(xv, nw, sc, dh):
    rstd = lax.rsqrt(jnp.mean(xv * xv, axis=-1, keepdims=True) + EPS)
    xhat = xv * rstd
    dn = dh * (1.0 + sc)
    dxhat = dn * nw
    dx = rstd * (dxhat - xhat * jnp.mean(dxhat * xhat, axis=-1, keepdims=True))
    return (dx, jnp.sum(dh, axis=0, keepdims=True), jnp.sum(dh * (xhat * nw), axis=0, keepdims=True),
            jnp.sum(dn * xhat, axis=0, keepdims=True))


def _norm_mod_all(ctx, x, nw, sh, sc):
    def body(ctx_ref, x_ref, nw_ref, sh_ref, sc_ref, o_ref):
        i = pl.program_id(0)
        sel = jnp.minimum(i, 1)
        xv = jnp.where(i == 0, ctx_ref[...], x_ref[...])
        o_ref[...] = _rms_mod(xv, nw_ref[...], sh_ref[pl.ds(sel, 1), :], sc_ref[pl.ds(sel, 1), :]).astype(BF16)

    return pl.pallas_call(
        body, name="norm_mod", grid=(N_TILES,),
        in_specs=[_full((TM, D)), pl.BlockSpec((TM, D), lambda i: (_lat(i), 0)),
                  _full((1, D)), _full((2, D)), _full((2, D))],
        out_specs=pl.BlockSpec((TM, D), lambda i: (i, 0)),
        out_shape=jax.ShapeDtypeStruct((T, D), BF16),
        compiler_params=_cp(("parallel",)),
    )(ctx, x, nw, sh, sc)


def _chunk_masks(reverse):
    row = lax.broadcasted_iota(jnp.int32, (TM, TM), 0)
    col = lax.broadcasted_iota(jnp.int32, (TM, TM), 1)
    same = (row // CHUNK) == (col // CHUNK)
    tri = same & ((col >= row) if reverse else (col <= row))
    return same, tri


def _chunk_order(i, reverse):
    if not reverse:
        return i
    return jnp.where(i < N_CTX_CHUNKS, N_CTX_CHUNKS - 1 - i, N_CHUNKS + N_CTX_CHUNKS - 1 - i)


def _decay_terms(z, lb, same01, tri01):
    f = lb + (1.0 - lb) * _sigmoid(z)
    g = jnp.log(f)
    g2 = jnp.concatenate(_split2(g), axis=1)
    b2 = _dot(tri01, g2)
    t2 = _dot(same01, g2)
    return f, 1.0 - f, b2[:, :HG_DIM] + b2[:, HG_DIM:], t2[:, :HG_DIM] + t2[:, HG_DIM:]


def _chunk_outer(a, b):
    n = TM // CHUNK
    return jnp.einsum('ncv,nck->nvk', a.reshape(n, CHUNK, HG_DIM), b.reshape(n, CHUNK, HG_DIM),
                      preferred_element_type=F32)


def _hgrn_fwd(p_a, lbl, carried=None):
    cpt = TM // CHUNK

    def body(p_ref, lbl_ref, o_ref, st_ref, qd_s, kd_s, u_s, v_s, ebt_s):
        masks = [_chunk_masks(d == 1) for d in (0, 1)]
        same01 = jnp.where(masks[0][0], 1.0, 0.0).astype(BF16)
        tri = [m[1] for m in masks]
        tri01 = [jnp.where(t, 1.0, 0.0).astype(BF16) for t in tri]
        lb = [_sigmoid(lbl_ref[d][0:1, :] - lbl_ref[d][1:2, :]) for d in (0, 1)]

        def prep(r, carry):
            r0 = pl.multiple_of(r * TM, TM)
            vb = p_ref[pl.ds(r0, TM), 2 * HG_DIM:3 * HG_DIM].astype(BF16)
            v_s[pl.ds(r0, TM), :] = vb
            for d in (0, 1):
                z = p_ref[pl.ds(r0, TM), d * HG_DIM:(d + 1) * HG_DIM]
                _, k, b, bt = _decay_terms(z, lb[d], same01, tri01[d])
                u_s[d, pl.ds(r * cpt, cpt)] = _chunk_outer(vb, (k * jnp.exp(bt - b)).astype(BF16))
                ebt_s[d, pl.ds(r0, TM), :] = jnp.exp(bt)

                @pl.when(r >= 1)
                def _():
                    rl = pl.multiple_of(r0 - L, TM)
                    qr = p_ref[pl.ds(r0, TM), 3 * HG_DIM:4 * HG_DIM]
                    q = qr * _sigmoid(qr) * HG_DIM ** -0.5
                    qd_s[d, pl.ds(rl, TM), :] = (q * jnp.exp(b)).astype(BF16)
                    kd_s[d, pl.ds(rl, TM), :] = (k * jnp.exp(-b)).astype(BF16)

            return carry

        lax.fori_loop(0, N_TILES, prep, 0)

        def scan(i, sts):
            new = []
            for d in (0, 1):
                nn = _chunk_order(i, d == 1)
                c0 = pl.multiple_of(nn * CHUNK, CHUNK)
                st_ref[d, nn] = sts[d].astype(BF16)
                new.append(sts[d] * ebt_s[d, pl.ds(c0, 1), :] + u_s[d, nn])
            return tuple(new)

        zero = jnp.zeros((HG_DIM, HG_DIM), F32)
        lax.fori_loop(0, N_CHUNKS, scan, (zero, zero))

        def outp(r, carry):
            r0 = pl.multiple_of(r * TM, TM)
            vb = v_s[pl.ds(r0 + L, TM), :]
            o = jnp.zeros((TM, HG_DIM), F32)
            for d in (0, 1):
                qd = qd_s[d, pl.ds(r0, TM), :]
                a = jnp.where(tri[d], _dot_nt(qd, kd_s[d, pl.ds(r0, TM), :]), 0.0)
                stb = st_ref[d, pl.ds(N_CTX_CHUNKS + r * cpt, cpt)]
                inter = jnp.einsum('nck,nvk->ncv', qd.reshape(cpt, CHUNK, HG_DIM), stb,
                                   preferred_element_type=F32)
                o = o + _dot(a.astype(BF16), vb) + inter.reshape(TM, HG_DIM)
            o_ref[pl.ds(r0, TM), :] = o
            return carry

        lax.fori_loop(0, N_LAT_TILES, outp, 0)

    return _pcall(
        body, carried, name="hgrn_fwd", grid=(HG_HEADS,),
        in_specs=[pl.BlockSpec((T, 4 * HG_DIM), lambda h: (0, h)),
                  pl.BlockSpec((2, 2, HG_DIM), lambda h: (0, 0, h))],
        out_specs=[pl.BlockSpec((S, HG_DIM), lambda h: (0, h)),
                   pl.BlockSpec((2, None, N_CHUNKS, HG_DIM, HG_DIM), lambda h: (0, h, 0, 0, 0))],
        out_shape=[jax.ShapeDtypeStruct((S, HGW), F32),
                   jax.ShapeDtypeStruct((2, HG_HEADS, N_CHUNKS, HG_DIM, HG_DIM), BF16)],
        scratch_shapes=[pltpu.VMEM((2, S, HG_DIM), BF16), pltpu.VMEM((2, S, HG_DIM), BF16),
                        pltpu.VMEM((2, N_CHUNKS, HG_DIM, HG_DIM), F32), pltpu.VMEM((T, HG_DIM), BF16),
                        pltpu.VMEM((2, T, HG_DIM), F32)],
        operands=[p_a, lbl])


def _hgrn_bwd(p_a, lbl, d_o, st, carried=None):
    cpt = TM // CHUNK

    def rows(r):
        return r * TM if isinstance(r, int) else pl.multiple_of(r * TM, TM)

    def body(p_ref, lbl_ref, do_ref, st_ref, dp_ref, dlb_ref, b_s, bt_s, dbt_s, qd_s, dst_s, w_s):
        masks = [_chunk_masks(d == 1) for d in (0, 1)]
        same01 = jnp.where(masks[0][0], 1.0, 0.0).astype(BF16)
        tri = [m[1] for m in masks]
        tri01 = [jnp.where(t, 1.0, 0.0).astype(BF16) for t in tri]
        later01 = [tri01[1], tri01[0]]
        lb = [_sigmoid(lbl_ref[d][0:1, :] - lbl_ref[d][1:2, :]) for d in (0, 1)]

        def prep_tile(r, latent):
            r0 = rows(r)
            for d in (0, 1):
                z = p_ref[pl.ds(r0, TM), d * HG_DIM:(d + 1) * HG_DIM]
                _, _, b, bt = _decay_terms(z, lb[d], same01, tri01[d])
                b_s[d, pl.ds(r0, TM), :] = b
                bt_s[d, pl.ds(r0, TM), :] = bt
                if latent:
                    rl = pl.multiple_of(r0 - L, TM)
                    qr = p_ref[pl.ds(r0, TM), 3 * HG_DIM:4 * HG_DIM]
                    qd = (qr * _sigmoid(qr) * HG_DIM ** -0.5 * jnp.exp(b)).astype(BF16)
                    qd_s[d, pl.ds(rl, TM), :] = qd
                    w_s[d, pl.ds(r * cpt, cpt)] = _chunk_outer(
                        do_ref[pl.ds(rl, TM), :].astype(BF16), qd).astype(BF16)

        prep_tile(0, False)
        w_s[:, pl.ds(0, N_CTX_CHUNKS)] = jnp.zeros((2, N_CTX_CHUNKS, HG_DIM, HG_DIM), BF16)

        def prep(r, carry):
            prep_tile(r, True)
            return carry

        lax.fori_loop(1, N_TILES, prep, 0)

        def rscan(j, dsts):
            i = N_CHUNKS - 1 - j
            new = []
            for d in (0, 1):
                nn = _chunk_order(i, d == 1)
                c0 = pl.multiple_of(nn * CHUNK, CHUNK)
                dst_s[d, nn] = dsts[d].astype(BF16)
                after = st_ref[d, _chunk_order(jnp.minimum(i + 1, N_CHUNKS - 1), d == 1)].astype(F32)
                dbt_s[d, pl.ds(c0, CHUNK), :] = jnp.broadcast_to(
                    jnp.sum(after * dsts[d], axis=0, keepdims=True), (CHUNK, HG_DIM))
                new.append(dsts[d] * jnp.exp(bt_s[d, pl.ds(c0, 1), :]) + w_s[d, nn].astype(F32))
            return tuple(new)

        zero = jnp.zeros((HG_DIM, HG_DIM), F32)
        lax.fori_loop(0, N_CHUNKS, rscan, (zero, zero))

        def grad_tile(r, latent):
            r0 = rows(r)
            vb = p_ref[pl.ds(r0, TM), 2 * HG_DIM:3 * HG_DIM].astype(BF16)
            dv = jnp.zeros((TM, HG_DIM), F32)
            dq = jnp.zeros((TM, HG_DIM), F32)
            dlbs = []
            if latent:
                rl = pl.multiple_of(r0 - L, TM)
                qr = p_ref[pl.ds(r0, TM), 3 * HG_DIM:4 * HG_DIM]
                sq = _sigmoid(qr)
                do = do_ref[pl.ds(rl, TM), :].astype(BF16)
                da_full = _dot_nt(do, vb)
            for d in (0, 1):
                z = p_ref[pl.ds(r0, TM), d * HG_DIM:(d + 1) * HG_DIM]
                sz = _sigmoid(z)
                f = lb[d] + (1.0 - lb[d]) * sz
                k = 1.0 - f
                b = b_s[d, pl.ds(r0, TM), :]
                e2 = jnp.exp(bt_s[d, pl.ds(r0, TM), :] - b)
                dstb = dst_s[d, pl.ds(r * cpt, cpt)]
                kd2 = k * e2
                dkd2 = jnp.einsum('ncv,nvk->nck', vb.reshape(cpt, CHUNK, HG_DIM), dstb,
                                  preferred_element_type=F32).reshape(TM, HG_DIM)
                dv = dv + jnp.einsum('nck,nvk->ncv', kd2.astype(BF16).reshape(cpt, CHUNK, HG_DIM), dstb,
                                     preferred_element_type=F32).reshape(TM, HG_DIM)
                dk = dkd2 * e2
                db = -(kd2 * dkd2)
                if latent:
                    eb = jnp.exp(b)
                    enb = jnp.exp(-b)
                    qdf = qr * sq * HG_DIM ** -0.5 * eb
                    kdf = k * enb
                    qd = qd_s[d, pl.ds(rl, TM), :]
                    kd = kdf.astype(BF16)
                    a = jnp.where(tri[d], _dot_nt(qd, kd), 0.0).astype(BF16)
                    da = jnp.where(tri[d], da_full, 0.0).astype(BF16)
                    stb = st_ref[d, pl.ds(r * cpt, cpt)]
                    dqd = _dot(da, kd) + jnp.einsum(
                        'ncv,nvk->nck', do.reshape(cpt, CHUNK, HG_DIM), stb,
                        preferred_element_type=F32).reshape(TM, HG_DIM)
                    dkd = _dot_tn(da, qd)
                    dv = dv + _dot_tn(a, do)
                    dk = dk + dkd * enb
                    db = db + qdf * dqd - kdf * dkd
                    dq = dq + dqd * eb
                dg = _dot_lhs01(later01[d], db) + dbt_s[d, pl.ds(r0, TM), :]
                df = dg / f - dk
                dp_ref[pl.ds(r0, TM), d * HG_DIM:(d + 1) * HG_DIM] = (
                    df * (1.0 - lb[d]) * sz * (1.0 - sz)).astype(BF16)
                dlbs.append(jnp.sum(df * (1.0 - sz), axis=0, keepdims=True))
            dp_ref[pl.ds(r0, TM), 2 * HG_DIM:3 * HG_DIM] = dv.astype(BF16)
            if latent:
                dq = dq * (HG_DIM ** -0.5) * (sq * (1.0 + qr * (1.0 - sq)))
            dp_ref[pl.ds(r0, TM), 3 * HG_DIM:4 * HG_DIM] = dq.astype(BF16)
            return dlbs

        dlb_ctx = grad_tile(0, False)

        def grads(r, acc):
            t = grad_tile(r, True)
            return (acc[0] + t[0], acc[1] + t[1])

        dlb = lax.fori_loop(1, N_TILES, grads, (dlb_ctx[0], dlb_ctx[1]))
        dlb_ref[0:1, :] = dlb[0]
        dlb_ref[1:2, :] = dlb[1]

    return _pcall(
        body, carried, name="hgrn_bwd", grid=(HG_HEADS,),
        in_specs=[pl.BlockSpec((T, 4 * HG_DIM), lambda h: (0, h)),
                  pl.BlockSpec((2, 2, HG_DIM), lambda h: (0, 0, h)),
                  pl.BlockSpec((S, HG_DIM), lambda h: (0, h)),
                  pl.BlockSpec((2, None, N_CHUNKS, HG_DIM, HG_DIM), lambda h: (0, h, 0, 0, 0))],
        out_specs=[pl.BlockSpec((T, 4 * HG_DIM), lambda h: (0, h)),
                   pl.BlockSpec((2, HG_DIM), lambda h: (0, h))],
        out_shape=[jax.ShapeDtypeStruct((T, WA), BF16), jax.ShapeDtypeStruct((2, HGW), F32)],
        scratch_shapes=[pltpu.VMEM((2, T, HG_DIM), F32), pltpu.VMEM((2, T, HG_DIM), F32),
                        pltpu.VMEM((2, T, HG_DIM), F32), pltpu.VMEM((2, S, HG_DIM), BF16),
                        pltpu.VMEM((2, N_CHUNKS, HG_DIM, HG_DIM), BF16),
                        pltpu.VMEM((2, N_CHUNKS, HG_DIM, HG_DIM), BF16)],
        operands=[p_a, lbl, d_o, st])


def _rope_tables():
    t = np.arange(S)
    inv = ROPE_THETA ** (-np.arange(0, 32, 2, dtype=np.float64) / 32)
    lane = np.arange(64)
    pos = np.where(lane[None, :] < 32, (t // GRID_W)[:, None], (t % GRID_W)[:, None]).astype(np.float64)
    ang = pos * inv[(lane % 32) % 16][None, :]
    sign = np.where((lane % 32) < 16, -1.0, 1.0)[None, :]
    cos = np.tile(np.cos(ang), (1, 2)).astype(np.float32)
    sin = np.tile(np.sin(ang) * sign, (1, 2)).astype(np.float32)
    return jnp.asarray(cos), jnp.asarray(sin)


def _rope_partner(v):
    lane = lax.broadcasted_iota(jnp.int32, (1, 128), 1)
    first = (lane % 32) < 16
    slabs = []
    for j in range(v.shape[1] // 128):
        s = v[:, 128 * j:128 * (j + 1)]
        slabs.append(jnp.where(first, pltpu.roll(s, 112, 1), pltpu.roll(s, 16, 1)))
    return slabs[0] if len(slabs) == 1 else jnp.concatenate(slabs, axis=1)


def _group_ones(width, group):
    r = lax.broadcasted_iota(jnp.int32, (width, width), 0)
    c = lax.broadcasted_iota(jnp.int32, (width, width), 1)
    return jnp.where((r // group) == (c // group), 1.0, 0.0).astype(BF16)


def _group_mean(v, ones01, group):
    hi = v.astype(BF16)
    lo = (v - hi.astype(F32)).astype(BF16)
    return (_dot(hi, ones01) + _dot(lo, ones01)) * (1.0 / group)


def _rep_matrix():
    r = lax.broadcasted_iota(jnp.int32, (KVW, ATW), 0)
    c = lax.broadcasted_iota(jnp.int32, (KVW, ATW), 1)
    return jnp.where(r == HEAD_DIM * (c // 256) + c % HEAD_DIM, 1.0, 0.0).astype(BF16)


def _tile_lanes(v, reps):
    return jnp.concatenate([v] * reps, axis=1)


def _prep_fwd(p_b, o, cos, sin, hnw, qnw, knw):
    def body(p_ref, o_ref, cos_ref, sin_ref, hnw_ref, qnw_ref, knw_ref, y_ref, q_ref, k_ref, v_ref):
        i = pl.program_id(0)
        rep = _rep_matrix()
        ones_k = _group_ones(KVW, HEAD_DIM)
        kr = p_ref[:, 1024:1152]
        krstd = lax.rsqrt(_group_mean(kr * kr, ones_k, HEAD_DIM) + EPS)
        kn = kr * krstd * knw_ref[...]
        v_ref[...] = _dot(p_ref[:, 1152:1280].astype(BF16), rep).astype(BF16)

        @pl.when(i == 0)
        def _():
            k_ref[...] = _dot(kn.astype(BF16), rep).astype(BF16)

        @pl.when(i > 0)
        def _():
            cs, sn = cos_ref[...], sin_ref[...]
            kro = kn * cs + _rope_partner(kn) * sn
            k_ref[...] = _dot(kro.astype(BF16), rep).astype(BF16)
            qr = p_ref[:, 512:1024]
            qrstd = lax.rsqrt(_group_mean(qr * qr, _group_ones(ATW, HEAD_DIM), HEAD_DIM) + EPS)
            qn = qr * qrstd * qnw_ref[...]
            qro = qn * _tile_lanes(cs, 4) + _rope_partner(qn) * _tile_lanes(sn, 4)
            q_ref[...] = (qro * HEAD_DIM ** -0.5).astype(BF16)
            ys = []
            for h in range(HG_HEADS):
                oh = o_ref[:, HG_DIM * h:HG_DIM * (h + 1)]
                gh = p_ref[:, HG_DIM * h:HG_DIM * (h + 1)]
                rstd = lax.rsqrt(jnp.mean(oh * oh, axis=-1, keepdims=True) + EPS)
                ys.append(oh * rstd * hnw_ref[...] * (gh * _sigmoid(gh)))
            y_ref[...] = jnp.concatenate(ys, axis=1).astype(BF16)

    return pl.pallas_call(
        body, name="prep_fwd", grid=(N_TILES,),
        in_specs=[pl.BlockSpec((TM, WB), lambda i: (i, 0)),
                  pl.BlockSpec((TM, HGW), lambda i: (_lat(i), 0)),
                  pl.BlockSpec((TM, 128), lambda i: (_lat(i), 0)),
                  pl.BlockSpec((TM, 128), lambda i: (_lat(i), 0)),
                  _full((1, HG_DIM)), _full((1, ATW)), _full((1, KVW))],
        out_specs=[pl.BlockSpec((TM, HGW), lambda i: (_lat(i), 0)),
                   pl.BlockSpec((TM, ATW), lambda i: (_lat(i), 0)),
                   pl.BlockSpec((TM, ATW), lambda i: (i, 0)),
                   pl.BlockSpec((TM, ATW), lambda i: (i, 0))],
        out_shape=[jax.ShapeDtypeStruct((S, HGW), BF16), jax.ShapeDtypeStruct((S, ATW), BF16),
                   jax.ShapeDtypeStruct((T, ATW), BF16), jax.ShapeDtypeStruct((T, ATW), BF16)],
        compiler_params=_cp(("arbitrary",)),
    )(p_b, o, cos, sin, hnw, qnw, knw)


def _prep_bwd(p_b, o, cos, sin, hnw, qnw, knw, dy_hg, dq, dk_rep, dv_rep, carried=None):
    def body(p_ref, o_ref, cos_ref, sin_ref, hnw_ref, qnw_ref, knw_ref, dy_ref, dq_ref, dk_ref, dv_ref,
             dp_ref, do_ref, acc_ref):
        i = pl.program_id(0)

        @pl.when(i == 0)
        def _():
            acc_ref[...] = jnp.zeros_like(acc_ref)

        rep = _rep_matrix()
        ones_k = _group_ones(KVW, HEAD_DIM)

        def fold(v):
            hi = v.astype(BF16)
            lo = (v - hi.astype(F32)).astype(BF16)
            return _dot_nt(hi, rep) + _dot_nt(lo, rep)

        kr = p_ref[:, 1024:1152]
        krstd = lax.rsqrt(_group_mean(kr * kr, ones_k, HEAD_DIM) + EPS)
        khat = kr * krstd
        kw = knw_ref[...]
        dkro = fold(dk_ref[...])
        dv = fold(dv_ref[...])

        def k_back(dkn):
            dkhat = dkn * kw
            dkr = krstd * (dkhat - khat * _group_mean(dkhat * khat, ones_k, HEAD_DIM))
            acc_ref[2:3, 0:KVW] += jnp.sum(dkn * khat, axis=0, keepdims=True)
            dp_ref[:, 1024:1152] = dkr.astype(BF16)
            dp_ref[:, 1152:1280] = dv.astype(BF16)

        @pl.when(i == 0)
        def _():
            k_back(dkro)
            dp_ref[:, 0:1024] = jnp.zeros((TM, 1024), BF16)

        @pl.when(i > 0)
        def _():
            cs, sn = cos_ref[...], sin_ref[...]
            k_back(dkro * cs + _rope_partner(dkro * sn))
            ones_q = _group_ones(ATW, HEAD_DIM)
            qr = p_ref[:, 512:1024]
            qrstd = lax.rsqrt(_group_mean(qr * qr, ones_q, HEAD_DIM) + EPS)
            qhat = qr * qrstd
            dqro = dq_ref[...] * HEAD_DIM ** -0.5
            dqn = dqro * _tile_lanes(cs, 4) + _rope_partner(dqro * _tile_lanes(sn, 4))
            dqhat = dqn * qnw_ref[...]
            dqr = qrstd * (dqhat - qhat * _group_mean(dqhat * qhat, ones_q, HEAD_DIM))
            acc_ref[1:2, :] += jnp.sum(dqn * qhat, axis=0, keepdims=True)
            dp_ref[:, 512:1024] = dqr.astype(BF16)
            dws = jnp.zeros((1, HG_DIM), F32)
            for h in range(HG_HEADS):
                sl = slice(HG_DIM * h, HG_DIM * (h + 1))
                oh, gh, dy = o_ref[:, sl], p_ref[:, sl], dy_ref[:, sl]
                rstd = lax.rsqrt(jnp.mean(oh * oh, axis=-1, keepdims=True) + EPS)
                ohat = oh * rstd
                sg = _sigmoid(gh)
                dp_ref[:, sl] = (dy * (ohat * hnw_ref[...]) * (sg * (1.0 + gh * (1.0 - sg)))).astype(BF16)
                dn = dy * (gh * sg)
                dws = dws + jnp.sum(dn * ohat, axis=0, keepdims=True)
                dohat = dn * hnw_ref[...]
                do_ref[:, sl] = rstd * (dohat - ohat * jnp.mean(dohat * ohat, axis=-1, keepdims=True))
            acc_ref[0:1, 0:HG_DIM] += dws

    return _pcall(
        body, carried, name="prep_bwd", grid=(N_TILES,),
        in_specs=[pl.BlockSpec((TM, WB), lambda i: (i, 0)),
                  pl.BlockSpec((TM, HGW), lambda i: (_lat(i), 0)),
                  pl.BlockSpec((TM, 128), lambda i: (_lat(i), 0)),
                  pl.BlockSpec((TM, 128), lambda i: (_lat(i), 0)),
                  _full((1, HG_DIM)), _full((1, ATW)), _full((1, KVW)),
                  pl.BlockSpec((TM, HGW), lambda i: (_lat(i), 0)),
                  pl.BlockSpec((TM, ATW), lambda i: (_lat(i), 0)),
                  pl.BlockSpec((TM, ATW), lambda i: (i, 0)),
                  pl.BlockSpec((TM, ATW), lambda i: (i, 0))],
        out_specs=[pl.BlockSpec((TM, WB), lambda i: (i, 0)),
                   pl.BlockSpec((TM, HGW), lambda i: (_lat(i), 0)),
                   _full((8, ATW))],
        out_shape=[jax.ShapeDtypeStruct((T, WB), BF16), jax.ShapeDtypeStruct((S, HGW), F32),
                   jax.ShapeDtypeStruct((8, ATW), F32)],
        scratch_shapes=[], operands=[p_b, o, cos, sin, hnw, qnw, knw, dy_hg, dq, dk_rep, dv_rep])


NEG = -1e30
_CTX_BLOCKS = L // BLOCK


def _attn_window_specs():
    prev = pl.BlockSpec((BLOCK, ATW), lambda i: (jnp.maximum(i - 1, 0) + _CTX_BLOCKS, 0))
    own = pl.BlockSpec((BLOCK, ATW), lambda i: (i + _CTX_BLOCKS, 0))
    nxt = pl.BlockSpec((BLOCK, ATW), lambda i: (jnp.minimum(i + 1, N_BLOCKS - 1) + _CTX_BLOCKS, 0))
    return [prev, own, nxt, _full((L, ATW))]


def _attn_valid(i, heads, context):
    n_keys = 3 * BLOCK + (L if context else 0)
    qi = lax.broadcasted_iota(jnp.int32, (heads * BLOCK, n_keys), 0) % BLOCK
    kj = lax.broadcasted_iota(jnp.int32, (heads * BLOCK, n_keys), 1)
    window = ((jnp.abs(kj - BLOCK - qi) <= BLOCK) & ((kj >= BLOCK) | (i > 0))
              & ((kj < 2 * BLOCK) | (i < N_BLOCKS - 1)))
    return window | (kj >= 3 * BLOCK)


def _stack_heads(qg):
    lane = lax.broadcasted_iota(jnp.int32, (1, 256), 1) // HEAD_DIM
    return jnp.concatenate([jnp.where(lane == g, qg, jnp.zeros_like(qg)) for g in range(4)], axis=0)


def _unstack_heads(v4):
    lane = lax.broadcasted_iota(jnp.int32, (1, 256), 1) // HEAD_DIM
    out = jnp.where(lane == 0, v4[0:BLOCK], 0.0)
    for g in range(1, 4):
        out = out + jnp.where(lane == g, v4[g * BLOCK:(g + 1) * BLOCK], 0.0)
    return out


def _sink_rows(sink_ref, hk):
    return jnp.concatenate(
        [jnp.broadcast_to(sink_ref[0:1, 4 * hk + g:4 * hk + g + 1], (BLOCK, 1)) for g in range(4)], axis=0)


def _attn_fwd(q, k_rep, v_rep, sinks, carried=None):
    def body(q_ref, kp, ko, kn, kc, vp, vo, vn, vc, sink_ref, y_ref, lse_ref):
        i = pl.program_id(0)
        valid = _attn_valid(i, 1, True)
        lane8 = lax.broadcasted_iota(jnp.int32, (1, ATT_HEADS), 1)
        head_of_lane = lax.broadcasted_iota(jnp.int32, (1, 256), 1) // HEAD_DIM
        lse_out = jnp.zeros((BLOCK, ATT_HEADS), F32)
        for hk in range(KV_HEADS):
            sl = slice(256 * hk, 256 * (hk + 1))
            qg = q_ref[:, sl]
            keys = jnp.concatenate([kp[:, sl], ko[:, sl], kn[:, sl], kc[:, sl]], axis=0)
            vals = jnp.concatenate([vp[:, sl], vo[:, sl], vn[:, sl], vc[:, sl]], axis=0)
            yg = jnp.zeros((BLOCK, 256), F32)
            for g in range(4):
                q1 = jnp.where(head_of_lane == g, qg, jnp.zeros_like(qg))
                s = jnp.where(valid, _dot_nt(q1, keys), NEG)
                sink = sink_ref[0:1, 4 * hk + g:4 * hk + g + 1]
                m = jnp.maximum(jnp.max(s, axis=1, keepdims=True), sink)
                p = jnp.exp(s - m)
                den = jnp.sum(p, axis=1, keepdims=True) + jnp.exp(sink - m)
                o1 = _dot(p.astype(BF16), vals) * (1.0 / den)
                yg = yg + jnp.where(head_of_lane == g, o1, 0.0)
                lse_out = lse_out + jnp.where(lane8 == 4 * hk + g, m + jnp.log(den), 0.0)
            y_ref[:, sl] = yg.astype(BF16)
        lse_ref[...] = lse_out

    return _pcall(
        body, carried, name="attn_fwd", grid=(N_BLOCKS,),
        in_specs=[pl.BlockSpec((BLOCK, ATW), lambda i: (i, 0))] + _attn_window_specs()
        + _attn_window_specs() + [_full((1, ATT_HEADS))],
        out_specs=[pl.BlockSpec((BLOCK, ATW), lambda i: (i, 0)),
                   pl.BlockSpec((BLOCK, ATT_HEADS), lambda i: (i, 0))],
        out_shape=[jax.ShapeDtypeStruct((S, ATW), BF16), jax.ShapeDtypeStruct((S, ATT_HEADS), F32)],
        scratch_shapes=[],
        operands=[q, k_rep, k_rep, k_rep, k_rep, v_rep, v_rep, v_rep, v_rep, sinks])


def _attn_bwd(q, k_rep, v_rep, sinks, y_at, lse, dy, carried=None):
    def body(q_ref, kp, ko, kn, kc, vp, vo, vn, vc, sink_ref, y_ref, lse_ref, dy_ref,
             dq_ref, dk_ref, dv_ref, dsink_ref, dk_acc, dv_acc):
        i = pl.program_id(0)

        @pl.when(i == 0)
        def _():
            dk_acc[...] = jnp.zeros_like(dk_acc)
            dv_acc[...] = jnp.zeros_like(dv_acc)
            dk_ref[pl.ds(0, L), :] = jnp.zeros((L, ATW), F32)
            dv_ref[pl.ds(0, L), :] = jnp.zeros((L, ATW), F32)
            dsink_ref[...] = jnp.zeros_like(dsink_ref)

        valid = _attn_valid(i, 4, False)
        lane8 = lax.broadcasted_iota(jnp.int32, (1, ATT_HEADS), 1)
        w0 = pl.multiple_of(i * BLOCK, BLOCK)
        dsink = jnp.zeros((1, ATT_HEADS), F32)
        for hk in range(KV_HEADS):
            sl = slice(256 * hk, 256 * (hk + 1))
            q4 = _stack_heads(q_ref[:, sl])
            do4f = _stack_heads(dy_ref[:, sl])
            o4 = _stack_heads(y_ref[:, sl]).astype(F32)
            do4 = do4f.astype(BF16)
            kl = jnp.concatenate([kp[:, sl], ko[:, sl], kn[:, sl]], axis=0)
            vl = jnp.concatenate([vp[:, sl], vo[:, sl], vn[:, sl]], axis=0)
            lse4 = jnp.concatenate(
                [jnp.sum(jnp.where(lane8 == 4 * hk + g, lse_ref[...], 0.0), axis=1, keepdims=True)
                 for g in range(4)], axis=0)
            p_loc = jnp.where(valid, jnp.exp(_dot_nt(q4, kl) - lse4), 0.0)
            p_ctx = jnp.exp(_dot_nt(q4, kc[:, sl]) - lse4)
            delta = jnp.sum(do4f * o4, axis=1, keepdims=True)
            ds_loc = (p_loc * (_dot_nt(do4, vl) - delta)).astype(BF16)
            ds_ctx = (p_ctx * (_dot_nt(do4, vc[:, sl]) - delta)).astype(BF16)
            dq_ref[:, sl] = _unstack_heads(_dot(ds_loc, kl) + _dot(ds_ctx, kc[:, sl]))
            dk_acc[pl.ds(w0, 3 * BLOCK), sl] += _dot_tn(ds_loc, q4)
            dv_acc[pl.ds(w0, 3 * BLOCK), sl] += _dot_tn(p_loc.astype(BF16), do4)
            dk_ref[pl.ds(0, L), sl] += _dot_tn(ds_ctx, q4)
            dv_ref[pl.ds(0, L), sl] += _dot_tn(p_ctx.astype(BF16), do4)
            p_sink = jnp.exp(_sink_rows(sink_ref, hk) - lse4)
            for g in range(4):
                rows = slice(g * BLOCK, (g + 1) * BLOCK)
                dsink = dsink + jnp.where(lane8 == 4 * hk + g,
                                          -jnp.sum(p_sink[rows] * delta[rows], axis=0, keepdims=True), 0.0)
        dsink_ref[...] += dsink

        @pl.when(i == N_BLOCKS - 1)
        def _():
            dk_ref[pl.ds(L, S), :] = dk_acc[pl.ds(BLOCK, S), :]
            dv_ref[pl.ds(L, S), :] = dv_acc[pl.ds(BLOCK, S), :]

    row_q = pl.BlockSpec((BLOCK, ATW), lambda i: (i, 0))
    return _pcall(
        body, carried, name="attn_bwd", grid=(N_BLOCKS,),
        in_specs=[row_q] + _attn_window_specs() + _attn_window_specs()
        + [_full((1, ATT_HEADS)), row_q, pl.BlockSpec((BLOCK, ATT_HEADS), lambda i: (i, 0)), row_q],
        out_specs=[row_q, _full((T, ATW)), _full((T, ATW)), _full((1, ATT_HEADS))],
        out_shape=[jax.ShapeDtypeStruct((S, ATW), F32), jax.ShapeDtypeStruct((T, ATW), F32),
                   jax.ShapeDtypeStruct((T, ATW), F32), jax.ShapeDtypeStruct((1, ATT_HEADS), F32)],
        scratch_shapes=[pltpu.VMEM((S + 2 * BLOCK, ATW), F32), pltpu.VMEM((S + 2 * BLOCK, ATW), F32)],
        operands=[q, k_rep, k_rep, k_rep, k_rep, v_rep, v_rep, v_rep, v_rep, sinks, y_at, lse, dy])


def _merge_fwd(y_hg, y_at, p_c, x, w_bh, w_ba, w_out, g1, nfw, sh2, sc2, carried=None):
    def body(yh_ref, ya_ref, g_ref, x_ref, wbh_ref, wba_ref, wo_ref, g1_ref, nfw_ref, sh_ref, sc_ref,
             a_ref, b_ref, mx_ref, r_ref, x1_ref, h2_ref):
        a = _dot_nt(yh_ref[...], wbh_ref[...])
        b = _dot_nt(ya_ref[...], wba_ref[...])
        mixed = (_sigmoid(g_ref[:, :D]) * a + _sigmoid(g_ref[:, D:]) * b).astype(BF16)
        r = _dot(mixed, wo_ref[...])
        x1 = x_ref[...] + g1_ref[...] * r
        a_ref[...] = a
        b_ref[...] = b
        mx_ref[...] = mixed
        r_ref[...] = r
        x1_ref[...] = x1
        h2_ref[...] = _rms_mod(x1, nfw_ref[...], sh_ref[...], sc_ref[...]).astype(BF16)

    row = lambda w: pl.BlockSpec((TM, w), lambda i: (i, 0))
    vec = _full((1, D))
    return _pcall(
        body, carried, name="merge_fwd", grid=(N_LAT_TILES,),
        in_specs=[row(HGW), row(ATW), row(WC), row(D), _VMEM_WHOLE, _VMEM_WHOLE, _VMEM_WHOLE,
                  vec, vec, vec, vec],
        out_specs=[row(D)] * 6,
        out_shape=[jax.ShapeDtypeStruct((S, D), dt) for dt in (F32, F32, BF16, F32, F32, BF16)],
        scratch_shapes=[], operands=[y_hg, y_at, p_c, x, w_bh, w_ba, w_out, g1, nfw, sh2, sc2])


def _merge_bwd(dx1, r, a, b, p_c, w_bh, w_ba, w_out, g1, carried=None):
    def body(dx_ref, r_ref, a_ref, b_ref, g_ref, wbh_ref, wba_ref, wo_ref, g1_ref,
             dr_ref, da_ref, db_ref, dg_ref, dyh_ref, dya_ref, acc_ref):
        @pl.when(pl.program_id(0) == 0)
        def _():
            acc_ref[...] = jnp.zeros_like(acc_ref)

        dx1v = dx_ref[...]
        acc_ref[0:1, :] += jnp.sum(dx1v * r_ref[...], axis=0, keepdims=True)
        dr = (g1_ref[...] * dx1v).astype(BF16)
        dr_ref[...] = dr
        dmix = _dot_nt(dr, wo_ref[...])
        sh, sa = _sigmoid(g_ref[:, :D]), _sigmoid(g_ref[:, D:])
        da = (dmix * sh).astype(BF16)
        db = (dmix * sa).astype(BF16)
        da_ref[...] = da
        db_ref[...] = db
        dg_ref[:, :D] = (dmix * a_ref[...] * sh * (1.0 - sh)).astype(BF16)
        dg_ref[:, D:] = (dmix * b_ref[...] * sa * (1.0 - sa)).astype(BF16)
        dyh_ref[...] = _dot(da, wbh_ref[...])
        dya_ref[...] = _dot(db, wba_ref[...])

    row = lambda w: pl.BlockSpec((TM, w), lambda i: (i, 0))
    return _pcall(
        body, carried, name="merge_bwd", grid=(N_LAT_TILES,),
        in_specs=[row(D), row(D), row(D), row(D), row(WC), _VMEM_WHOLE, _VMEM_WHOLE, _VMEM_WHOLE,
                  _full((1, D))],
        out_specs=[row(D), row(D), row(D), row(WC), row(HGW), row(ATW), _full((8, D))],
        out_shape=[jax.ShapeDtypeStruct((S, D), BF16), jax.ShapeDtypeStruct((S, D), BF16),
                   jax.ShapeDtypeStruct((S, D), BF16), jax.ShapeDtypeStruct((S, WC), BF16),
                   jax.ShapeDtypeStruct((S, HGW), F32), jax.ShapeDtypeStruct((S, ATW), F32),
                   jax.ShapeDtypeStruct((8, D), F32)],
        scratch_shapes=[], operands=[dx1, r, a, b, p_c, w_bh, w_ba, w_out, g1])


def _ffn_fused(x1, h2, tgt, w_gate, w_up, w_down, g2, nfw, sc2):
    def body(x1_ref, h2_ref, t_ref, wg_ref, wu_ref, wd_ref, g2_ref, nfw_ref, sc_ref,
             act_ref, dgt_ref, dup_ref, df_ref, dx_ref, acc_ref, gs, us):
        @pl.when(pl.program_id(0) == 0)
        def _():
            acc_ref[...] = jnp.zeros_like(acc_ref)

        h2 = h2_ref[...]
        whole = lambda w_ref: w_ref[...].reshape(D_FF, D)
        wide = lambda t_ref: jnp.concatenate([t_ref[j] for j in range(N_FF_TILES)], axis=1)
        for j in range(N_FF_TILES):
            g = _dot_nt(h2, wg_ref[j])
            u = _dot_nt(h2, wu_ref[j])
            gs[j] = g
            us[j] = u
            act_ref[j] = (g * _sigmoid(g) * u).astype(BF16)
        f = _dot(wide(act_ref), whole(wd_ref))
        x1v = x1_ref[...]
        g2 = g2_ref[...]
        diff = x1v + g2 * f - t_ref[...]
        dy = diff * (1.0 / D)
        df = (g2 * dy).astype(BF16)
        df_ref[...] = df
        dact_all = _dot_nt(df, whole(wd_ref))
        for j in range(N_FF_TILES):
            g, u = gs[j], us[j]
            sg = _sigmoid(g)
            dact = dact_all[:, j * FF_TILE:(j + 1) * FF_TILE]
            dgt_ref[j] = (dact * u * (sg * (1.0 + g * (1.0 - sg)))).astype(BF16)
            dup_ref[j] = (dact * (g * sg)).astype(BF16)
        dh2 = _dot(wide(dgt_ref), whole(wg_ref)) + _dot(wide(dup_ref), whole(wu_ref))
        dx, dsh, dsc, dnw = _rms_mod_bwd(x1v, nfw_ref[...], sc_ref[...], dh2)
        dx_ref[...] = dy + dx
        acc_ref[0:1, :] += dsh
        acc_ref[1:2, :] += dsc
        acc_ref[2:3, :] += dnw
        acc_ref[3:4, :] += jnp.sum(dy * f, axis=0, keepdims=True)
        acc_ref[4:5, :] += 0.5 * jnp.sum(jnp.sum(diff * diff, axis=1, keepdims=True), axis=0,
                                         keepdims=True) * (1.0 / D)

    row = lambda dt_w: pl.BlockSpec((TM, dt_w), lambda i: (i, 0))
    blk = pl.BlockSpec((N_FF_TILES, TM, FF_TILE), lambda i: (0, i, 0))
    vec = _full((1, D))
    return pl.pallas_call(
        body, name="ffn_fused", grid=(N_LAT_TILES,),
        in_specs=[row(D), row(D), row(D), _VMEM_WHOLE, _VMEM_WHOLE, _VMEM_WHOLE, vec, vec, vec],
        out_specs=[blk, blk, blk, row(D), row(D), _full((8, D))],
        out_shape=[jax.ShapeDtypeStruct((N_FF_TILES, S, FF_TILE), BF16)] * 3
        + [jax.ShapeDtypeStruct((S, D), BF16), jax.ShapeDtypeStruct((S, D), F32),
           jax.ShapeDtypeStruct((8, D), F32)],
        scratch_shapes=[pltpu.VMEM((N_FF_TILES, TM, FF_TILE), F32), pltpu.VMEM((N_FF_TILES, TM, FF_TILE), F32)],
        compiler_params=_cp(("arbitrary",)),
    )(x1, h2, tgt, w_gate, w_up, w_down, g2, nfw, sc2)


def _proj_bc(h_all, w_b, w_c, carried=None):
    def body(h_ref, wb_ref, wc_ref, pb_ref, pc_ref):
        h = h_ref[...]
        pb_ref[...] = _dot_nt(h, wb_ref[...])

        @pl.when(pl.program_id(0) > 0)
        def _():
            pc_ref[...] = _dot_nt(h, wc_ref[...])

    return _pcall(
        body, carried, name="proj_bc", grid=(N_TILES,),
        in_specs=[pl.BlockSpec((TM, D), lambda i: (i, 0)), _VMEM_WHOLE, _VMEM_WHOLE],
        out_specs=[pl.BlockSpec((TM, WB), lambda i: (i, 0)), pl.BlockSpec((TM, WC), lambda i: (_lat(i), 0))],
        out_shape=[jax.ShapeDtypeStruct((T, WB), F32), jax.ShapeDtypeStruct((S, WC), F32)],
        scratch_shapes=[], operands=[h_all, w_b, w_c])


def _input_bwd(dp_a, dp_b, dp_c, w_a, w_b, w_c, ctx, x, dx1, nw, sh, sc, carried=None):
    def body(da_ref, db_ref, dc_ref, wa_ref, wb_ref, wc_ref, ctx_ref, x_ref, dx1_ref, nw_ref, sh_ref,
             sc_ref, gx_ref, acc_ref):
        i = pl.program_id(0)

        @pl.when(i == 0)
        def _():
            acc_ref[...] = jnp.zeros_like(acc_ref)

        dh = _dot(da_ref[...], wa_ref[...]) + _dot(db_ref[...], wb_ref[...])

        @pl.when(i == 0)
        def _():
            _, dsh, dsc, dnw = _rms_mod_bwd(ctx_ref[...], nw_ref[...], sc_ref[0:1, :], dh)
            acc_ref[3:4, :] += dsh
            acc_ref[4:5, :] += dsc
            acc_ref[2:3, :] += dnw

        @pl.when(i > 0)
        def _():
            dhl = dh + _dot(dc_ref[...], wc_ref[...])
            dx, dsh, dsc, dnw = _rms_mod_bwd(x_ref[...], nw_ref[...], sc_ref[1:2, :], dhl)
            gx_ref[...] = dx1_ref[...] + dx
            acc_ref[0:1, :] += dsh
            acc_ref[1:2, :] += dsc
            acc_ref[2:3, :] += dnw

    lat = lambda w: pl.BlockSpec((TM, w), lambda i: (_lat(i), 0))
    return _pcall(
        body, carried, name="input_bwd", grid=(N_TILES,),
        in_specs=[pl.BlockSpec((TM, WA), lambda i: (i, 0)), pl.BlockSpec((TM, WB), lambda i: (i, 0)),
                  lat(WC), _VMEM_WHOLE, _VMEM_WHOLE, _VMEM_WHOLE, _full((TM, D)), lat(D), lat(D),
                  _full((1, D)), _full((2, D)), _full((2, D))],
        out_specs=[lat(D), _full((8, D))],
        out_shape=[jax.ShapeDtypeStruct((S, D), F32), jax.ShapeDtypeStruct((8, D), F32)],
        scratch_shapes=[], operands=[dp_a, dp_b, dp_c, w_a, w_b, w_c, ctx, x, dx1, nw, sh, sc])


_C1 = 1.0 - ADAM_B1 ** ADAM_STEP
_C2 = 1.0 - ADAM_B2 ** ADAM_STEP


def _adamw_math(w, g, m, v):
    m = ADAM_B1 * m + (1.0 - ADAM_B1) * g
    v = ADAM_B2 * v + (1.0 - ADAM_B2) * (g * g)
    m_hat = m / _C1
    v_hat = v / _C2
    delta = -ADAM_LR * (m_hat / (jnp.sqrt(v_hat) + ADAM_EPS) + ADAM_WD * w)
    return delta, m, v


def _adamw_sharded(terms, w, m, v, name, tr):
    rows, cols = w.shape

    def body(t_ref, w_ref, m_ref, v_ref, g_ref, d_ref, nm_ref, nv_ref):
        g = t_ref[0].astype(F32)
        for s in range(1, N_CHIPS):
            g = g + t_ref[s].astype(F32)
        g_ref[...] = g
        d_ref[...], nm_ref[...], nv_ref[...] = _adamw_math(w_ref[...], g, m_ref[...], v_ref[...])

    blk = pl.BlockSpec((tr, cols), lambda i: (i, 0))
    return pl.pallas_call(
        body, name=name, grid=(rows // tr,),
        in_specs=[pl.BlockSpec((N_CHIPS, tr, cols), lambda i: (0, i, 0)), blk, blk, blk],
        out_specs=[blk] * 4,
        out_shape=[jax.ShapeDtypeStruct((rows, cols), F32)] * 4,
        compiler_params=_cp(("parallel",)),
    )(terms, w, m, v)


def _adamw_plain(g, w, m, v, name):
    def body(g_ref, w_ref, m_ref, v_ref, d_ref, nm_ref, nv_ref):
        d_ref[...], nm_ref[...], nv_ref[...] = _adamw_math(w_ref[...], g_ref[...], m_ref[...], v_ref[...])

    return pl.pallas_call(
        body, name=name, in_specs=[_VMEM_WHOLE] * 4, out_specs=[_VMEM_WHOLE] * 3,
        out_shape=[jax.ShapeDtypeStruct(w.shape, F32)] * 3,
        compiler_params=_cp(),
    )(g, w, m, v)


SMALL_ROWS = 16
R_DMOD, R_DCTX, R_NMIX, R_NFFN, R_MISC, R_DLB, R_BADA01 = 0, 6, 8, 9, 10, 11, 13
M_HNW, M_QNW, M_KNW, M_SINK, M_LOSS = 0, 128, 256, 384, 512


def _pack_small(acc_in, acc_mg, acc_ffn, acc_prep, dsink, dlb):
    def body(in_ref, mg_ref, ff_ref, pp_ref, ds_ref, dlb_ref, o_ref):
        o_ref[...] = jnp.zeros_like(o_ref)
        o_ref[0:2, :] = in_ref[0:2, :]
        o_ref[2:3, :] = mg_ref[0:1, :]
        o_ref[3:5, :] = ff_ref[0:2, :]
        o_ref[5:6, :] = ff_ref[3:4, :]
        o_ref[6:8, :] = in_ref[3:5, :]
        o_ref[8:9, :] = in_ref[2:3, :]
        o_ref[9:10, :] = ff_ref[2:3, :]
        o_ref[10:11, M_HNW:M_HNW + HG_DIM] = pp_ref[0:1, 0:HG_DIM]
        r = lax.broadcasted_iota(jnp.int32, (ATW, 128), 0)
        c = lax.broadcasted_iota(jnp.int32, (ATW, 128), 1)
        fold = jnp.where((r % HEAD_DIM == c) & (c < HEAD_DIM), 1.0, 0.0).astype(BF16)
        qk = jnp.concatenate([pp_ref[1:2, :], pp_ref[2:3, :], jnp.zeros((6, ATW), F32)], axis=0)
        folded = _dot_exact_rhs01(qk, fold)
        o_ref[10:11, M_QNW:M_QNW + 128] = folded[0:1, :]
        o_ref[10:11, M_KNW:M_KNW + 128] = folded[1:2, :]
        o_ref[10:11, M_SINK:M_SINK + ATT_HEADS] = ds_ref[...]
        o_ref[10:11, M_LOSS:M_LOSS + 128] = ff_ref[4:5, 0:128]
        o_ref[11:13, 0:HGW] = dlb_ref[...]

    return pl.pallas_call(
        body, name="pack_small", in_specs=[_VMEM_WHOLE] * 6, out_specs=_VMEM_WHOLE,
        out_shape=jax.ShapeDtypeStruct((SMALL_ROWS, D), F32), compiler_params=_cp(),
    )(acc_in, acc_mg, acc_ffn, acc_prep, dsink, dlb)


def _sum_small(gathered):
    def body(g_ref, o_ref):
        tot = g_ref[0]
        for s in range(1, N_DEV):
            tot = tot + g_ref[s]
        o_ref[...] = tot
        o_ref[R_BADA01:R_BADA01 + 2, :] = tot[0:2, :] + tot[R_DCTX:R_DCTX + 2, :]

    return pl.pallas_call(
        body, name="sum_small", in_specs=[_VMEM_WHOLE], out_specs=_VMEM_WHOLE,
        out_shape=jax.ShapeDtypeStruct((SMALL_ROWS, D), F32), compiler_params=_cp(),
    )(gathered)


_REP_NAMES = ("b_ada", "c_ctx", "norm_mix_w", "norm_ffn_w", "hgrn_norm_w", "q_norm_w", "k_norm_w", "attn_sinks")


def _adamw_replicated(tot, g_c_ctx, ws, ms, vs):
    n = len(_REP_NAMES)

    def body(*refs):
        tot_ref, gc_ref = refs[0], refs[1]
        w_refs, m_refs, v_refs = refs[2:2 + n], refs[2 + n:2 + 2 * n], refs[2 + 2 * n:2 + 3 * n]
        outs = refs[2 + 3 * n:]
        row = lambda r: tot_ref[r:r + 1, :]
        misc = row(R_MISC)
        grads = [jnp.concatenate([row(R_BADA01), row(R_BADA01 + 1)] + [row(k) for k in range(2, 6)], axis=1),
                 gc_ref[...], row(R_NMIX), row(R_NFFN),
                 misc[:, M_HNW:M_HNW + HG_DIM], misc[:, M_QNW:M_QNW + HEAD_DIM],
                 misc[:, M_KNW:M_KNW + HEAD_DIM], misc[:, M_SINK:M_SINK + ATT_HEADS]]
        for k in range(n):
            outs[k][...] = grads[k]
            outs[n + k][...], outs[2 * n + k][...], outs[3 * n + k][...] = _adamw_math(
                w_refs[k][...], grads[k], m_refs[k][...], v_refs[k][...])

    shapes = [jax.ShapeDtypeStruct(w.shape, F32) for w in ws]
    return pl.pallas_call(
        body, name="adamw_replicated", in_specs=[_VMEM_WHOLE] * (2 + 3 * n), out_specs=[_VMEM_WHOLE] * (4 * n),
        out_shape=shapes * 4, compiler_params=_cp(),
    )(tot, g_c_ctx, *ws, *ms, *vs)


def _lb_grads(dlb, lbl):
    def body(d_ref, l_ref, o_ref):
        for d in (0, 1):
            ll = l_ref[d]
            lb = _sigmoid(ll[0:1, :] - ll[1:2, :])
            t = d_ref[d:d + 1, :] * lb * (1.0 - lb)
            o_ref[d, 0:1, :] = t
            o_ref[d, 1:2, :] = -t

    return pl.pallas_call(
        body, name="lb_grads", in_specs=[_VMEM_WHOLE] * 2, out_specs=_VMEM_WHOLE,
        out_shape=jax.ShapeDtypeStruct((2, 2, HGW), F32), compiler_params=_cp(),
    )(dlb, lbl)


def _c_ctx_grad(terms, c_ctx):
    def body(t_ref, c_ref, o_ref):
        tot = t_ref[0, 8:9, :]
        for s in range(1, N_DEV):
            tot = tot + t_ref[s, 8:9, :]
        cv = c_ref[...]
        sg = _sigmoid(cv)
        o_ref[...] = tot * (sg * (1.0 + cv * (1.0 - sg)))

    return pl.pallas_call(
        body, name="c_ctx_grad", in_specs=[_VMEM_WHOLE] * 2, out_specs=_VMEM_WHOLE,
        out_shape=jax.ShapeDtypeStruct((1, D), F32), compiler_params=_cp(),
    )(terms, c_ctx)


def _in_perm():
    fz, bz, inp, kk, vv, qhg, ghg, qat, gates = 0, 512, 1024, 1536, 1664, 1792, 2304, 2816, 3328
    cols = []
    for h in range(HG_HEADS):
        for base in (fz, bz, inp, qhg):
            cols += list(range(base + 128 * h, base + 128 * (h + 1)))
    cols += list(range(ghg, ghg + 512)) + list(range(qat, qat + 512))
    cols += list(range(kk, kk + 128)) + list(range(vv, vv + 128))
    cols += list(range(gates, gates + 2048))
    return np.asarray(cols, np.int32)


_PERM = _in_perm()
_INV_PERM = np.argsort(_PERM).astype(np.int32)


_PIECES = {"a": (0, WA, 128), "b": (WA, WB, 256), "c": (WA + WB, WC, 256)}


def _block_table(piece):
    lo, n, blk = _PIECES[piece]
    starts = [int(_PERM[r]) for r in range(lo, lo + n, blk)]
    assert all(s % blk == 0 and np.array_equal(_PERM[r:r + blk], np.arange(s, s + blk))
               for s, r in zip(starts, range(lo, lo + n, blk)))
    return jnp.asarray([s // blk for s in starts], jnp.int32), blk


def _pick_row_blocks(x, table, blk, name):
    cols = x.shape[1]

    def body(t_ref, x_ref, o_ref):
        o_ref[...] = x_ref[...]

    return pl.pallas_call(
        body, name=name,
        grid_spec=pltpu.PrefetchScalarGridSpec(
            num_scalar_prefetch=1, grid=(table.shape[0],),
            in_specs=[pl.BlockSpec((blk, cols), lambda i, t: (t[i], 0))],
            out_specs=pl.BlockSpec((blk, cols), lambda i, t: (i, 0))),
        out_shape=jax.ShapeDtypeStruct((table.shape[0] * blk, cols), x.dtype),
        compiler_params=_cp(("arbitrary",)),
    )(table, x)


def _place_row_blocks(x, table, blk, into, out_rows, name):
    cols = x.shape[1]

    def body(t_ref, x_ref, *rest):
        rest[-1][...] = x_ref[...]

    operands, in_specs, aliases = [table, x], [pl.BlockSpec((blk, cols), lambda i, t: (i, 0))], {}
    if into is not None:
        operands.append(into)
        in_specs.append(_ANY)
        aliases = {2: 0}
    return pl.pallas_call(
        body, name=name,
        grid_spec=pltpu.PrefetchScalarGridSpec(
            num_scalar_prefetch=1, grid=(table.shape[0],), in_specs=in_specs,
            out_specs=pl.BlockSpec((blk, cols), lambda i, t: (t[i], 0))),
        out_shape=jax.ShapeDtypeStruct((out_rows, cols), x.dtype),
        input_output_aliases=aliases,
        compiler_params=_cp(("arbitrary",)),
    )(*operands)


def _cols_from_blocks(g):
    return jnp.transpose(g, (1, 0, 2)).reshape(g.shape[1], N_DEV * g.shape[2])


def _local_step(x2, ctx2, tgt, lbl, sh_in, sc_in, gate1, sh2, sc2, gate2, norm_mix_w, norm_ffn_w,
                hgrn_norm_w, q_norm_w, k_norm_w, attn_sinks, w_a, w_b, w_c, s_bh, s_ba, s_out,
                s_gate, s_up, s_down):
    first_last = lambda n: [(0, True), (n - 1, False)]
    h_all = _norm_mod_all(ctx2, x2, norm_mix_w, sh_in, sc_in)
    p_a = _mm_nt(h_all, w_a, tm=768, tn=1024, out_dtype=F32, name="proj_a")
    (o, st), (g_gate, g_bh, g_ba) = _hgrn_fwd(
        p_a, lbl, (_gather_comm_relayed([s_gate, s_bh, s_ba]),
                   [(0, True), (HG_HEADS - 2, True), (HG_HEADS - 1, False)]))
    (p_b, p_c), (g_out,) = _proj_bc(
        h_all, w_b, w_c, (_gather_comm_relayed([s_out]), [(0, True), (N_TILES - 4, True), (N_TILES - 1, False)]))
    cos, sin = _rope_tables()
    qnw_t, knw_t = jnp.tile(q_norm_w, (1, ATT_HEADS)), jnp.tile(k_norm_w, (1, KV_HEADS))
    y_hg, qn, k_rep, v_rep = _prep_fwd(p_b, o, cos, sin, hgrn_norm_w, qnw_t, knw_t)
    (y_at, lse), (g_up, g_down) = _attn_fwd(
        qn, k_rep, v_rep, attn_sinks,
        (_gather_comm_relayed([s_up, s_down]), [(0, True), (N_BLOCKS - 6, True), (N_BLOCKS - 1, False)]))
    w_bh, w_ba, w_o = g_bh.reshape(D, HGW), g_ba.reshape(D, ATW), g_out.reshape(D, D)
    (a, b, mixed, r, x1, h2), _ = _merge_fwd(
        y_hg, y_at, p_c, x2, w_bh, w_ba, w_o, gate1, norm_ffn_w, sh2, sc2)
    g_gate, g_up, g_down = [g.reshape(N_FF_TILES, FF_TILE, D) for g in (g_gate, g_up, g_down)]

    act, d_gate, d_up, d_f, dx1, acc_ffn = _ffn_fused(x1, h2, tgt, g_gate, g_up, g_down, gate2,
                                                      norm_ffn_w, sc2)
    by_chip = lambda t: t.reshape((N_CHIPS, 2) + t.shape[1:])
    ff_by_chip = lambda t: t.reshape(N_CHIPS, 2, FF_BLK, D)
    t_down, _ = _mm_tn_blocked(act, d_f, "grad_down")
    t_down = ff_by_chip(t_down)
    t_gate, (f_down,) = _mm_tn_blocked(d_gate, h2, "grad_gate", (_sibling_comm([t_down]), first_last(N_FF_TILES)))
    t_gate = ff_by_chip(t_gate)
    t_up, (f_gate,) = _mm_tn_blocked(d_up, h2, "grad_up", (_sibling_comm([t_gate]), first_last(N_FF_TILES)))
    t_up = ff_by_chip(t_up)

    (d_r, d_a, d_b, dp_c, dy_hg, dy_at, acc_mg), (f_up,) = _merge_bwd(
        dx1, r, a, b, p_c, w_bh, w_ba, w_o, gate1, (_sibling_comm([t_up]), first_last(N_LAT_TILES)))
    c_down, c_gate, c_up = [_pair_sum(t, f, "pair_sum_" + nm) for t, f, nm in
                            ((t_down, f_down, "down"), (t_gate, f_gate, "gate"), (t_up, f_up, "up"))]
    t_out = _mm_tn(mixed, d_r, tk=512, nk=4, tm=512, tn=1024, out_dtype=BF16, name="grad_out")
    t_bh = _mm_tn(d_a, y_hg, tk=512, nk=4, tm=512, tn=512, out_dtype=BF16, name="grad_bh")
    t_ba = _mm_tn(d_b, y_at, tk=512, nk=4, tm=512, tn=512, out_dtype=BF16, name="grad_ba")
    t_bh, t_ba, t_out = [by_chip(t.reshape(N_DEV, D // N_DEV, t.shape[1])) for t in (t_bh, t_ba, t_out)]
    (dq, dk_rep, dv_rep, dsink), (r_up,) = _attn_bwd(
        qn, k_rep, v_rep, attn_sinks, y_at, lse, dy_at, (_chip_comm([c_up]), first_last(N_BLOCKS)))
    (dp_b, d_o, acc_prep), (f_bh, f_ba, f_out) = _prep_bwd(
        p_b, o, cos, sin, hgrn_norm_w, qnw_t, knw_t, dy_hg, dq, dk_rep, dv_rep,
        (_sibling_comm([t_bh, t_ba, t_out]), first_last(N_TILES)))
    c_bh, c_ba, c_out = [_pair_sum(t, f, "pair_sum_" + nm) for t, f, nm in
                         ((t_bh, f_bh, "bh"), (t_ba, f_ba, "ba"), (t_out, f_out, "out"))]
    (dp_a, dlb), (r_bh, r_ba, r_out, r_down, r_gate) = _hgrn_bwd(
        p_a, lbl, d_o, st, (_chip_comm([c_bh, c_ba, c_out, c_down, c_gate]), first_last(HG_HEADS)))
    t_a = _mm_tn(dp_a, h_all, tk=768, nk=3, tm=1024, tn=1024, out_dtype=BF16, name="grad_in_a")
    t_b = _mm_tn(dp_b, h_all, tk=768, nk=3, tm=640, tn=1024, out_dtype=BF16, name="grad_in_b")
    t_c = _mm_tn(dp_c, h_all, tk=256, nk=8, b_off=1, tm=1024, tn=1024, out_dtype=BF16, name="grad_in_c")
    t_in = None
    for piece, nm in ((t_a, "a"), (t_b, "b"), (t_c, "c")):
        t_in = _place_row_blocks(piece, *_block_table(nm), t_in, IN_COLS, "order_terms_" + nm)
    t_in = by_chip(t_in.reshape(N_DEV, IN_BLK, D))
    (f_in,) = _run_comm(_sibling_comm([t_in]), "scatter_in_sibling")
    (grad_x, acc_in), (r_in,) = _input_bwd(
        dp_a, dp_b, dp_c, w_a, w_b, w_c, ctx2, x2, dx1, norm_mix_w, sh_in, sc_in,
        (_chip_comm([_pair_sum(t_in, f_in, "pair_sum_in")]), first_last(N_TILES)))
    small = _pack_small(acc_in, acc_mg, acc_ffn, acc_prep, dsink, dlb)
    return grad_x, small, [r_in, r_bh, r_ba, r_out, r_gate, r_up, r_down]


def kernel(x, c, ctx, c_ctx, w_ada, b_ada, norm_mix_w, norm_ffn_w, w_in, hgrn_lb_logits, hgrn_norm_w, q_norm_w, k_norm_w, attn_sinks, w_branch_hgrn, w_branch_attn, w_out, w_ffn_gate, w_ffn_up, w_ffn_down, loss_target, m_c_ctx, m_w_ada, m_b_ada, m_norm_mix_w, m_norm_ffn_w, m_w_in, m_hgrn_lb_logits, m_hgrn_norm_w, m_q_norm_w, m_k_norm_w, m_attn_sinks, m_w_branch_hgrn, m_w_branch_attn, m_w_out, m_w_ffn_gate, m_w_ffn_up, m_w_ffn_down, v_c_ctx, v_w_ada, v_b_ada, v_norm_mix_w, v_norm_ffn_w, v_w_in, v_hgrn_lb_logits, v_hgrn_norm_w, v_q_norm_w, v_k_norm_w, v_attn_sinks, v_w_branch_hgrn, v_w_branch_attn, v_w_out, v_w_ffn_gate, v_w_ffn_up, v_w_ffn_down):
    me = 4 * lax.axis_index("x") + 2 * lax.axis_index("y") + lax.axis_index("c")
    x2, ctx2, tgt = x[0], ctx[0], loss_target[0]
    w_ada2, w_in2 = w_ada[0], w_in[0]

    cond = jnp.zeros((8, D), F32).at[0].set(c[0]).at[1, :256].set(hgrn_lb_logits.reshape(256))
    b_cols = lax.dynamic_slice(b_ada, (0, me * ADA_BLK), (1, ADA_BLK))
    g0, cc, g1, g_in = _prologue(cond, c_ctx.reshape(1, D), w_ada2, b_cols, w_in2.T.astype(BF16))
    lbl = jnp.transpose(g0[:, 1, :256].reshape(N_DEV, 2, 2, 64), (1, 2, 0, 3)).reshape(2, 2, HGW)
    mod_all = _cols_from_blocks(g1)
    mod = lax.dynamic_slice(mod_all, (me, 0), (1, 6 * D)).reshape(6, D)
    mod_c = mod_all[8].reshape(6, D)
    sh1, sc1, gate1, sh2, sc2, gate2 = [mod[k:k + 1] for k in range(6)]
    sh_in = jnp.concatenate([mod_c[0:1], sh1], axis=0)
    sc_in = jnp.concatenate([mod_c[1:2], sc1], axis=0)

    shards = [w_branch_hgrn[0].T, w_branch_attn[0].T, w_out[0], w_ffn_gate[0].T, w_ffn_up[0].T, w_ffn_down[0]]
    w_in_t = g_in.reshape(IN_COLS, D)
    w_a, w_b, w_c = [_pick_row_blocks(w_in_t, *_block_table(nm), "order_w_" + nm) for nm in "abc"]

    grad_x, small, (r_in, r_bh, r_ba, r_out, r_gate, r_up, r_down) = _local_step(
        x2, ctx2, tgt, lbl, sh_in, sc_in, gate1, sh2, sc2, gate2, norm_mix_w, norm_ffn_w, hgrn_norm_w,
        q_norm_w, k_norm_w, attn_sinks, w_a, w_b, w_c, *[s.astype(BF16) for s in shards])

    big = {}
    for nm, rr, ww, mm, vv, tr, transposed in (
            ("w_in", r_in, w_in2, m_w_in[0], v_w_in[0], 336, True),
            ("w_branch_hgrn", r_bh, w_branch_hgrn[0], m_w_branch_hgrn[0], v_w_branch_hgrn[0], 128, True),
            ("w_branch_attn", r_ba, w_branch_attn[0], m_w_branch_attn[0], v_w_branch_attn[0], 128, True),
            ("w_out", r_out, w_out[0], m_w_out[0], v_w_out[0], 128, False),
            ("w_ffn_gate", r_gate, w_ffn_gate[0], m_w_ffn_gate[0], v_w_ffn_gate[0], 352, True),
            ("w_ffn_up", r_up, w_ffn_up[0], m_w_ffn_up[0], v_w_ffn_up[0], 352, True),
            ("w_ffn_down", r_down, w_ffn_down[0], m_w_ffn_down[0], v_w_ffn_down[0], 352, False)):
        if transposed:
            res = _adamw_sharded(rr, ww.T, mm.T, vv.T, "adamw_" + nm, tr)
            big[nm] = [t.T[None] for t in res]
        else:
            big[nm] = [t[None] for t in _adamw_sharded(rr, ww, mm, vv, "adamw_" + nm, tr)]

    (g2,) = _all_gather([small], "gather_small", True)
    tot = _sum_small(g2)
    dm = jnp.zeros((16, 6 * D), F32).at[:8].set(g2[:, R_DMOD:R_DMOD + 6, :].reshape(N_DEV, 6 * D))
    dm = dm.at[8, :2 * D].set(tot[R_DCTX:R_DCTX + 2].reshape(2 * D))
    dm_cols = lax.dynamic_slice(dm, (0, me * ADA_BLK), (16, ADA_BLK))
    g_w_ada, dsc_term = _ada_grads(cc, dm_cols, w_ada2)
    (g3,) = _all_gather([dsc_term], "gather_cctx", True)
    g_c_ctx = _c_ctx_grad(g3, c_ctx.reshape(1, D))
    g_lbl = _lb_grads(tot[R_DLB:R_DLB + 2, :HGW], lbl)
    g_lb_mine = lax.dynamic_slice(g_lbl, (0, 0, me * 64), (2, 2, 64))
    misc = tot[R_MISC]
    loss = misc[M_LOSS]

    rep_out = _adamw_replicated(
        tot, g_c_ctx,
        [b_ada, c_ctx.reshape(1, D), norm_mix_w, norm_ffn_w, hgrn_norm_w, q_norm_w, k_norm_w, attn_sinks],
        [m_b_ada, m_c_ctx.reshape(1, D), m_norm_mix_w, m_norm_ffn_w, m_hgrn_norm_w, m_q_norm_w, m_k_norm_w,
         m_attn_sinks],
        [v_b_ada, v_c_ctx.reshape(1, D), v_norm_mix_w, v_norm_ffn_w, v_hgrn_norm_w, v_q_norm_w, v_k_norm_w,
         v_attn_sinks])
    rep = []
    for kind in range(4):
        vals = dict(zip(_REP_NAMES, rep_out[kind * len(_REP_NAMES):(kind + 1) * len(_REP_NAMES)]))
        vals["c_ctx"] = vals["c_ctx"].reshape(D)
        rep.append(vals)

    d_ada, nm_ada, nv_ada = _adamw_plain(g_w_ada, w_ada2, m_w_ada[0], v_w_ada[0], "adamw_w_ada")
    ada = [t[None] for t in (g_w_ada, d_ada, nm_ada, nv_ada)]
    lb_w = hgrn_lb_logits.reshape(4, 64)
    d_lb, nm_lb, nv_lb = _adamw_plain(g_lb_mine.reshape(4, 64), lb_w, m_hgrn_lb_logits.reshape(4, 64),
                                      v_hgrn_lb_logits.reshape(4, 64), "adamw_lb")
    lbs = [t.reshape(2, 2, 64) for t in (g_lb_mine, d_lb, nm_lb, nv_lb)]

    names = ['c_ctx', 'w_ada', 'b_ada', 'norm_mix_w', 'norm_ffn_w', 'w_in', 'hgrn_lb_logits', 'hgrn_norm_w',
             'q_norm_w', 'k_norm_w', 'attn_sinks', 'w_branch_hgrn', 'w_branch_attn', 'w_out', 'w_ffn_gate',
             'w_ffn_up', 'w_ffn_down']
    outs = [loss, grad_x[None]]
    for kind in range(4):
        for nm in names:
            if nm == 'w_ada':
                outs.append(ada[kind])
            elif nm == 'hgrn_lb_logits':
                outs.append(lbs[kind])
            elif nm in big:
                outs.append(big[nm][kind])
            else:
                outs.append(rep[kind][nm])
    return tuple(outs)
```

```python
import functools
import math

import numpy as np
import jax
import jax.numpy as jnp
from jax import lax
from jax.experimental import pallas as pl
from jax.experimental.pallas import tpu as pltpu

F32 = jnp.float32
BF16 = jnp.bfloat16

N_DEV = 8
D = 1024
S = 2048
L = 256
T = L + S
TM = 256
N_TILES = T // TM
N_LAT_TILES = S // TM
HG_HEADS = 4
HG_DIM = 128
HGW = 512
CHUNK = 32
N_CHUNKS = T // CHUNK
N_CTX_CHUNKS = L // CHUNK
N_LAT_CHUNKS = S // CHUNK
ATT_HEADS = 8
KV_HEADS = 2
HEAD_DIM = 64
ATW = 512
KVW = 128
BLOCK = 128
N_BLOCKS = S // BLOCK
GRID_W = 64
ROPE_THETA = 10000.0
D_FF = 2816
FF_BLK = D_FF // N_DEV
FF_TILE = 256
N_FF_TILES = D_FF // FF_TILE
IN_COLS = 5376
IN_BLK = IN_COLS // N_DEV
ADA_BLK = 6 * D // N_DEV
EPS = 1e-6
WA, WB, WC = 2048, 1280, 2048

ADAM_LR = 0.001
ADAM_B1 = 0.9
ADAM_B2 = 0.999
ADAM_EPS = 1e-08
ADAM_WD = 0.01
ADAM_STEP = 10

VMEM_LIMIT = 56 * 1024 * 1024
MESH = pl.DeviceIdType.MESH


def _cp(sem=None, vmem=VMEM_LIMIT):
    return pltpu.CompilerParams(dimension_semantics=sem, vmem_limit_bytes=vmem)


def _full(shape):
    n = len(shape)
    return pl.BlockSpec(shape, lambda *_: (0,) * n)


_VMEM_WHOLE = pl.BlockSpec(memory_space=pltpu.VMEM)
_ANY = pl.BlockSpec(memory_space=pl.ANY)


def _sigmoid(v):
    return 1.0 / (1.0 + jnp.exp(-v))


def _dot(a, b):
    return jnp.dot(a, b, preferred_element_type=F32)


def _dot_nt(a, b):
    return lax.dot_general(a, b, (((1,), (1,)), ((), ())), preferred_element_type=F32)


def _dot_tn(a, b):
    return lax.dot_general(a, b, (((0,), (0,)), ((), ())), preferred_element_type=F32)


def _split3(v):
    hi = v.astype(BF16)
    r = v - hi.astype(F32)
    mid = r.astype(BF16)
    lo = (r - mid.astype(F32)).astype(BF16)
    return hi, mid, lo


def _dot_exact_rhs01(v, m01):
    hi, mid, lo = _split3(v)
    return _dot(hi, m01) + _dot(mid, m01) + _dot(lo, m01)


def _split2(v):
    hi = v.astype(BF16)
    return hi, (v - hi.astype(F32)).astype(BF16)


def _dot_lhs01(m01, v):
    hi, lo = _split2(v)
    return _dot(m01, hi) + _dot(m01, lo)


def _dot_f32(a, b, dot=_dot):
    ah, am, al = _split3(a)
    bh, bm, bl = _split3(b)
    return (dot(ah, bh) + (dot(ah, bm) + dot(am, bh))
            + (dot(am, bm) + dot(ah, bl) + dot(al, bh)))


def _my_pos():
    return lax.axis_index("x"), lax.axis_index("y"), lax.axis_index("c")


class _Comm:
    def __init__(self, operands, out_shapes, sems, phases):
        self.operands, self.out_shapes, self.sems, self.phases = operands, out_shapes, sems, phases


def _gather_comm(blocks):
    n = len(blocks)

    def parts(ins, outs, sems):
        send_sems, recv_sems, local_sems = sems
        x, y, c = _my_pos()
        me, sibling = (x, y, c), (x, y, 1 - c)
        chips = [(1 - x, y), (x, 1 - y), (1 - x, 1 - y)]

        def slot(a, px, py, pc):
            return outs[a].at[4 * px + 2 * py + pc]

        def copy(a, k, block, to, src=None):
            return pltpu.make_async_remote_copy(
                src_ref=slot(a, *block) if src is None else src, dst_ref=slot(a, *block),
                send_sem=send_sems.at[a, k], recv_sem=recv_sems.at[a, k],
                device_id=to, device_id_type=MESH)

        mine = [pltpu.make_async_copy(ins[a], slot(a, *me), local_sems.at[a]) for a in range(n)]
        first = []
        for a in range(n):
            first.append(copy(a, 0, me, sibling, src=ins[a]))
            first += [copy(a, 1 + j, me, (*chip, c), src=ins[a]) for j, chip in enumerate(chips)]
        passed = [copy(a, 4 + j, (*chip, c), sibling) for j, chip in enumerate(chips) for a in range(n)]
        return c, me, sibling, chips, copy, mine, first, passed

    def start(ins, outs, sems):
        _, _, _, _, _, mine, first, _ = parts(ins, outs, sems)
        for cp in mine + first:
            cp.start()

    def forward(ins, outs, sems):
        c, me, _, chips, copy, _, _, passed = parts(ins, outs, sems)
        for j, chip in enumerate(chips):
            for a in range(n):
                copy(a, 1 + j, (*chip, c), me).wait_recv()
                passed[j * n + a].start()

    def finish(ins, outs, sems):
        c, me, sibling, chips, copy, mine, first, passed = parts(ins, outs, sems)
        for a in range(n):
            copy(a, 0, sibling, me).wait_recv()
            for j, chip in enumerate(chips):
                copy(a, 4 + j, (*chip, 1 - c), me).wait_recv()
        for cp in first + passed:
            cp.wait_send()
        for cp in mine:
            cp.wait()

    return _Comm(blocks, [jax.ShapeDtypeStruct((N_DEV,) + b.shape, b.dtype) for b in blocks],
                 [pltpu.SemaphoreType.DMA((n, 7)), pltpu.SemaphoreType.DMA((n, 7)), pltpu.SemaphoreType.DMA((n,))],
                 [start, forward, finish])


def _gather_comm_relayed(blocks):
    n = len(blocks)

    def parts(ins, outs, sems):
        send_sems, recv_sems, local_sems = sems
        x, y, c = _my_pos()
        me, sibling = (x, y, c), (x, y, 1 - c)
        x_nbr, y_nbr, diag = (1 - x, y, c), (x, 1 - y, c), (1 - x, 1 - y, c)

        def slot(a, dev, half=None):
            ref = outs[a].at[4 * dev[0] + 2 * dev[1] + dev[2]]
            if half is None:
                return ref
            rows = blocks[a].shape[0] // 2
            return ref.at[pl.ds(half * rows, rows)]

        def copy(a, k, block, to, half=None, src=None):
            return pltpu.make_async_remote_copy(
                src_ref=slot(a, block, half) if src is None else src, dst_ref=slot(a, block, half),
                send_sem=send_sems.at[a, k], recv_sem=recv_sems.at[a, k],
                device_id=to, device_id_type=MESH)

        mine = [pltpu.make_async_copy(ins[a], slot(a, me), local_sems.at[a]) for a in range(n)]
        return me, sibling, x_nbr, y_nbr, diag, copy, mine

    def start(ins, outs, sems):
        me, sibling, x_nbr, y_nbr, _, copy, mine = parts(ins, outs, sems)
        for cp in mine:
            cp.start()
        for a in range(n):
            for k, to in ((1, x_nbr), (2, y_nbr), (0, sibling)):
                copy(a, k, me, to, src=ins[a]).start()

    def forward(ins, outs, sems):
        me, sibling, x_nbr, y_nbr, _, copy, _ = parts(ins, outs, sems)
        for a in range(n):
            copy(a, 1, x_nbr, me).wait_recv()
            copy(a, 3, x_nbr, y_nbr, half=0).start()
            copy(a, 5, x_nbr, sibling).start()
        for a in range(n):
            copy(a, 2, y_nbr, me).wait_recv()
            copy(a, 4, y_nbr, x_nbr, half=1).start()
            copy(a, 6, y_nbr, sibling).start()

    def finish(ins, outs, sems):
        me, sibling, x_nbr, y_nbr, diag, copy, mine = parts(ins, outs, sems)
        sib = lambda dev: (dev[0], dev[1], sibling[2])
        for a in range(n):
            copy(a, 3, diag, me, half=0).wait_recv()
            copy(a, 4, diag, me, half=1).wait_recv()
            copy(a, 7, diag, sibling).start()
        for a in range(n):
            copy(a, 0, sibling, me).wait_recv()
            for k, dev in ((5, x_nbr), (6, y_nbr), (7, diag)):
                copy(a, k, sib(dev), me).wait_recv()
        for a in range(n):
            for k, block, to, half in ((0, me, sibling, None), (1, me, x_nbr, None), (2, me, y_nbr, None),
                                       (3, x_nbr, y_nbr, 0), (4, y_nbr, x_nbr, 1), (5, x_nbr, sibling, None),
                                       (6, y_nbr, sibling, None), (7, diag, sibling, None)):
                copy(a, k, block, to, half=half, src=ins[a] if block is me else None).wait_send()
        for cp in mine:
            cp.wait()

    return _Comm(blocks, [jax.ShapeDtypeStruct((N_DEV,) + b.shape, b.dtype) for b in blocks],
                 [pltpu.SemaphoreType.DMA((n, 8)), pltpu.SemaphoreType.DMA((n, 8)), pltpu.SemaphoreType.DMA((n,))],
                 [start, forward, finish])


_HBM = pl.BlockSpec(memory_space=pltpu.HBM)
_SEM = pl.BlockSpec(memory_space=pltpu.SEMAPHORE)
_SPLIT_COPY = pltpu.CompilerParams(has_side_effects=pltpu.SideEffectType.DATAFLOW_SIDE_EFFECTING)


def _chip_exchange_copies(src_ref, land_ref, sems):
    x, y, c = _my_pos()
    q_me = 2 * x + y
    pairs = []
    for j, (px, py) in enumerate([(1 - x, y), (x, 1 - y), (1 - x, 1 - y)]):
        q = 2 * px + py
        send = pltpu.make_async_remote_copy(
            src_ref=src_ref.at[q], dst_ref=land_ref.at[q_me], send_sem=sems[j], recv_sem=sems[3 + j],
            device_id=(px, py, c), device_id_type=MESH)
        recv = pltpu.make_async_remote_copy(
            src_ref=src_ref.at[q], dst_ref=land_ref.at[q], send_sem=sems[j], recv_sem=sems[3 + j],
            device_id=(x, y, c), device_id_type=MESH)
        pairs.append((send, recv))
    return pairs


def _chip_exchange_start(src, land):
    def body(src_ref, land_ref, *outs):
        sems, token = outs[:6], outs[8]
        for send, _ in _chip_exchange_copies(src_ref, land_ref, sems):
            send.start()
        token[...] = jnp.zeros_like(token)

    res = pl.pallas_call(
        body, name="scatter_in_start",
        out_shape=(pltpu.SemaphoreType.DMA(()),) * 6 + (
            pltpu.HBM(src.shape, src.dtype), pltpu.HBM(land.shape, land.dtype),
            jax.ShapeDtypeStruct((8, 128), F32)),
        in_specs=(_HBM, _HBM), out_specs=(_SEM,) * 6 + (_HBM, _HBM, pl.BlockSpec(memory_space=pltpu.VMEM)),
        input_output_aliases={0: 6, 1: 7}, compiler_params=_SPLIT_COPY,
    )(pltpu.with_memory_space_constraint(src, pltpu.HBM), pltpu.with_memory_space_constraint(land, pltpu.HBM))
    return res[:6], res[6], res[7], res[8]


def _chip_exchange_wait(sems, src_thru, land_thru, after):
    def body(src_ref, land_ref, *rest):
        for send, recv in _chip_exchange_copies(src_ref, land_ref, rest[:6]):
            send.wait_send()
            recv.wait_recv()

    return pl.pallas_call(
        body, name="scatter_in_wait",
        out_shape=(pltpu.HBM(src_thru.shape, src_thru.dtype), pltpu.HBM(land_thru.shape, land_thru.dtype)),
        in_specs=(_HBM, _HBM) + (_SEM,) * 6 + (_ANY,), out_specs=(_HBM, _HBM),
        input_output_aliases={0: 0, 1: 1}, compiler_params=_SPLIT_COPY,
    )(src_thru, land_thru, *sems, after)[1]


def _run_comm(comm, name, in_vmem=False):
    n_in, n_out = len(comm.operands), len(comm.out_shapes)

    def body(*refs):
        ins, outs, sems = refs[:n_in], refs[n_in:n_in + n_out], refs[n_in + n_out:]
        for phase in comm.phases:
            phase(ins, outs, sems)

    spec = _VMEM_WHOLE if in_vmem else _ANY
    return pl.pallas_call(
        body, name=name, out_shape=comm.out_shapes, in_specs=[spec] * n_in, out_specs=[spec] * n_out,
        scratch_shapes=comm.sems,
    )(*comm.operands)


def _carrier_call(body, comm, schedule, *, name, grid, in_specs, out_specs, out_shape, scratch_shapes, operands):
    n_in, n_out, n_scr = len(in_specs), len(out_specs), len(scratch_shapes)
    c_in, c_out = len(comm.operands), len(comm.out_shapes)

    def full_body(*refs):
        ins, refs = refs[:n_in], refs[n_in:]
        cins, refs = refs[:c_in], refs[c_in:]
        outs, refs = refs[:n_out], refs[n_out:]
        couts, refs = refs[:c_out], refs[c_out:]
        scr, csems = refs[:n_scr], refs[n_scr:]
        step = pl.program_id(0)

        def run(before):
            for (at, when_before), phase in zip(schedule, comm.phases):
                if when_before == before:
                    pl.when(step == at)(functools.partial(phase, cins, couts, csems))

        run(True)
        body(*ins, *outs, *scr)
        run(False)

    res = pl.pallas_call(
        full_body, name=name, grid=grid,
        in_specs=list(in_specs) + [_ANY] * c_in, out_specs=list(out_specs) + [_ANY] * c_out,
        out_shape=list(out_shape) + list(comm.out_shapes),
        scratch_shapes=list(scratch_shapes) + list(comm.sems),
        compiler_params=_cp(("arbitrary",)),
    )(*operands, *comm.operands)
    return res[:n_out], res[n_out:]


def _pcall(body, carried, *, name, grid, in_specs, out_specs, out_shape, scratch_shapes, operands):
    if carried is None:
        res = pl.pallas_call(body, name=name, grid=grid, in_specs=in_specs, out_specs=out_specs,
                             out_shape=out_shape, scratch_shapes=scratch_shapes,
                             compiler_params=_cp(("arbitrary",)))(*operands)
        return res, ()
    return _carrier_call(body, carried[0], carried[1], name=name, grid=grid, in_specs=in_specs,
                         out_specs=out_specs, out_shape=out_shape, scratch_shapes=scratch_shapes,
                         operands=operands)


def _all_gather(blocks, name, in_vmem):
    return _run_comm(_gather_comm(blocks), name, in_vmem)


N_CHIPS = 4


def _sibling_comm(contribs):
    n = len(contribs)

    def copies(ins, outs, sems):
        send_sems, recv_sems = sems
        x, y, c = _my_pos()
        return [pltpu.make_async_remote_copy(
            src_ref=ins[a].at[pl.ds(0, N_CHIPS), 1 - c], dst_ref=outs[a],
            send_sem=send_sems.at[a], recv_sem=recv_sems.at[a],
            device_id=(x, y, 1 - c), device_id_type=MESH) for a in range(n)]

    def start(ins, outs, sems):
        for cp in copies(ins, outs, sems):
            cp.start()

    def finish(ins, outs, sems):
        cps = copies(ins, outs, sems)
        for cp in cps:
            cp.wait_recv()
        for cp in cps:
            cp.wait_send()

    return _Comm(contribs, [jax.ShapeDtypeStruct((N_CHIPS,) + b.shape[2:], b.dtype) for b in contribs],
                 [pltpu.SemaphoreType.DMA((n,)), pltpu.SemaphoreType.DMA((n,))], [start, finish])


def _pair_sum(mine, theirs, name):
    _, _, rows, cols = mine.shape
    core = lax.axis_index("c").astype(jnp.int32).reshape(1)

    def body(c_ref, m_ref, t_ref, o_ref):
        o_ref[...] = (m_ref[...].astype(F32) + t_ref[...].astype(F32)).astype(BF16)

    return pl.pallas_call(
        body, name=name,
        grid_spec=pltpu.PrefetchScalarGridSpec(
            num_scalar_prefetch=1, grid=(N_CHIPS,),
            in_specs=[pl.BlockSpec((None, None, rows, cols), lambda q, c: (q, c[0], 0, 0)),
                      pl.BlockSpec((None, rows, cols), lambda q, c: (q, 0, 0))],
            out_specs=pl.BlockSpec((None, rows, cols), lambda q, c: (q, 0, 0))),
        out_shape=jax.ShapeDtypeStruct((N_CHIPS, rows, cols), BF16),
        compiler_params=_cp(("parallel",)),
    )(core, mine, theirs)


def _chip_comm(sums):
    n = len(sums)

    def parts(ins, outs, sems):
        send_sems, recv_sems, local_sems = sems
        x, y, c = _my_pos()
        q_me = 2 * x + y
        chips = [(1 - x, y), (x, 1 - y), (1 - x, 1 - y)]
        mine = [pltpu.make_async_copy(ins[a].at[q_me], outs[a].at[q_me], local_sems.at[a]) for a in range(n)]
        sends, recvs = [], []
        for j, (px, py) in enumerate(chips):
            for a in range(n):
                q = 2 * px + py
                sends.append(pltpu.make_async_remote_copy(
                    src_ref=ins[a].at[q], dst_ref=outs[a].at[q_me],
                    send_sem=send_sems.at[a, j], recv_sem=recv_sems.at[a, j],
                    device_id=(px, py, c), device_id_type=MESH))
                recvs.append(pltpu.make_async_remote_copy(
                    src_ref=ins[a].at[q], dst_ref=outs[a].at[q],
                    send_sem=send_sems.at[a, j], recv_sem=recv_sems.at[a, j],
                    device_id=(x, y, c), device_id_type=MESH))
        return mine, sends, recvs

    def start(ins, outs, sems):
        mine, sends, _ = parts(ins, outs, sems)
        for cp in mine + sends:
            cp.start()

    def finish(ins, outs, sems):
        mine, sends, recvs = parts(ins, outs, sems)
        for cp in recvs:
            cp.wait_recv()
        for cp in sends:
            cp.wait_send()
        for cp in mine:
            cp.wait()

    return _Comm(sums, [jax.ShapeDtypeStruct(b.shape, b.dtype) for b in sums],
                 [pltpu.SemaphoreType.DMA((n, 3)), pltpu.SemaphoreType.DMA((n, 3)), pltpu.SemaphoreType.DMA((n,))],
                 [start, finish])


def _mm_nt(a, bt, *, tm, tn, out_dtype, name, row_off=0, rows=None):
    rows = a.shape[0] if rows is None else rows
    n, k = bt.shape

    def body(a_ref, b_ref, o_ref):
        o_ref[...] = _dot_nt(a_ref[...], b_ref[...]).astype(out_dtype)

    return pl.pallas_call(
        body, name=name, grid=(rows // tm, n // tn),
        in_specs=[pl.BlockSpec((tm, k), lambda i, j: (i + row_off, 0)),
                  pl.BlockSpec((tn, k), lambda i, j: (j, 0))],
        out_specs=pl.BlockSpec((tm, tn), lambda i, j: (i, j)),
        out_shape=jax.ShapeDtypeStruct((rows, n), out_dtype),
        compiler_params=_cp(("parallel", "parallel")),
    )(a, bt)


def _mm_tn(a, b, *, tk, nk, tm, tn, out_dtype, name, a_off=0, b_off=0):
    m, n = a.shape[1], b.shape[1]

    def body(a_ref, b_ref, o_ref, acc):
        kk = pl.program_id(2)

        @pl.when(kk == 0)
        def _():
            acc[...] = jnp.zeros_like(acc)

        acc[...] += _dot_tn(a_ref[...], b_ref[...])

        @pl.when(kk == nk - 1)
        def _():
            o_ref[...] = acc[...].astype(out_dtype)

    return pl.pallas_call(
        body, name=name, grid=(m // tm, n // tn, nk),
        in_specs=[pl.BlockSpec((tk, tm), lambda i, j, kk: (kk + a_off, i)),
                  pl.BlockSpec((tk, tn), lambda i, j, kk: (kk + b_off, j))],
        out_specs=pl.BlockSpec((tm, tn), lambda i, j, kk: (i, j)),
        out_shape=jax.ShapeDtypeStruct((m, n), out_dtype),
        scratch_shapes=[pltpu.VMEM((tm, tn), F32)],
        compiler_params=_cp(("parallel", "parallel", "arbitrary")),
    )(a, b)


def _mm_tn_blocked(a, b, name, carried=None):
    nb, _, w = a.shape
    n = b.shape[1]

    def body(a_ref, b_ref, o_ref):
        o_ref[...] = _dot_tn(a_ref[...], b_ref[...]).astype(BF16)

    (out,), extra = _pcall(
        body, carried, name=name, grid=(nb,),
        in_specs=[pl.BlockSpec((None, S, w), lambda j: (j, 0, 0)), _full((S, n))],
        out_specs=[pl.BlockSpec((None, w, n), lambda j: (j, 0, 0))],
        out_shape=[jax.ShapeDtypeStruct((nb, w, n), BF16)],
        scratch_shapes=[], operands=[a, b])
    return out, extra


def _prologue(cond, c_ctx, w_ada, b_cols, w_in_t):
    rows_shape = jax.ShapeDtypeStruct((16, ADA_BLK), F32)
    big, g_cond, g_mod = _gather_comm_relayed([w_in_t]), _gather_comm([cond]), _gather_comm([rows_shape])

    def body(cond_ref, cctx_ref, wada_ref, b_ref, win_ref, g0_ref, cc_ref, g1_ref, gin_ref, rows_ref, *sems):
        s_big, s_cond, s_mod = sems[0:3], sems[3:6], sems[6:9]
        big.phases[0]([win_ref], [gin_ref], s_big)
        for phase in g_cond.phases:
            phase([cond_ref], [g0_ref], s_cond)
        cc_ref[...] = jnp.zeros_like(cc_ref)
        for j in range(N_DEV):
            cc_ref[j:j + 1, :] = g0_ref[j, 0:1, :]
        cc_ref[N_DEV:N_DEV + 1, :] = cctx_ref[...]
        cv = cc_ref[...]
        rows_ref[...] = _dot_f32(cv * _sigmoid(cv), wada_ref[...]) + b_ref[...]
        for phase in g_mod.phases:
            phase([rows_ref], [g1_ref], s_mod)
        big.phases[1]([win_ref], [gin_ref], s_big)
        big.phases[2]([win_ref], [gin_ref], s_big)

    return pl.pallas_call(
        body, name="prologue",
        in_specs=[_VMEM_WHOLE] * 4 + [_ANY], out_specs=[_VMEM_WHOLE] * 3 + [_ANY],
        out_shape=[g_cond.out_shapes[0], jax.ShapeDtypeStruct((16, D), F32), g_mod.out_shapes[0],
                   big.out_shapes[0]],
        scratch_shapes=[pltpu.VMEM((16, ADA_BLK), F32)] + big.sems + g_cond.sems + g_mod.sems,
        compiler_params=_cp(),
    )(cond, c_ctx, w_ada, b_cols, w_in_t)


def _ada_grads(cc, dm_cols, w_ada):
    def body(c_ref, dm_ref, w_ref, gw_ref, dsc_ref):
        cv = c_ref[...]
        sc = cv * _sigmoid(cv)
        dm = dm_ref[...]
        gw_ref[...] = _dot_f32(sc, dm, dot=_dot_tn)
        dsc_ref[...] = _dot_f32(dm, w_ref[...], dot=_dot_nt)

    return pl.pallas_call(
        body, name="ada_grads",
        in_specs=[_VMEM_WHOLE] * 3, out_specs=[_VMEM_WHOLE] * 2,
        out_shape=[jax.ShapeDtypeStruct((D, ADA_BLK), F32), jax.ShapeDtypeStruct((16, D), F32)],
        compiler_params=_cp(),
    )(cc, dm_cols, w_ada)


def _lat(i):
    return jnp.maximum(i - 1, 0)


def _rms_mod(xv, nw, sh, sc):
    rstd = lax.rsqrt(jnp.mean(xv * xv, axis=-1, keepdims=True) + EPS)
    return (xv * rstd * nw) * (1.0 + sc) + sh


def _rms_mod_bwd(xv, nw, sc, dh):
    rstd = lax.rsqrt(jnp.mean(xv * xv, axis=-1, keepdims=True) + EPS)
    xhat = xv * rstd
    dn = dh * (1.0 + sc)
    dxhat = dn * nw
    dx = rstd * (dxhat - xhat * jnp.mean(dxhat * xhat, axis=-1, keepdims=True))
    return (dx, jnp.sum(dh, axis=0, keepdims=True), jnp.sum(dh * (xhat * nw), axis=0, keepdims=True),
            jnp.sum(dn * xhat, axis=0, keepdims=True))


def _norm_mod_all(ctx, x, nw, sh, sc):
    def body(ctx_ref, x_ref, nw_ref, sh_ref, sc_ref, o_ref):
        i = pl.program_id(0)
        sel = jnp.minimum(i, 1)
        xv = jnp.where(i == 0, ctx_ref[...], x_ref[...])
        o_ref[...] = _rms_mod(xv, nw_ref[...], sh_ref[pl.ds(sel, 1), :], sc_ref[pl.ds(sel, 1), :]).astype(BF16)

    return pl.pallas_call(
        body, name="norm_mod", grid=(N_TILES,),
        in_specs=[_full((TM, D)), pl.BlockSpec((TM, D), lambda i: (_lat(i), 0)),
                  _full((1, D)), _full((2, D)), _full((2, D))],
        out_specs=pl.BlockSpec((TM, D), lambda i: (i, 0)),
        out_shape=jax.ShapeDtypeStruct((T, D), BF16),
        compiler_params=_cp(("parallel",)),
    )(ctx, x, nw, sh, sc)


def _chunk_masks(reverse):
    row = lax.broadcasted_iota(jnp.int32, (TM, TM), 0)
    col = lax.broadcasted_iota(jnp.int32, (TM, TM), 1)
    same = (row // CHUNK) == (col // CHUNK)
    tri = same & ((col >= row) if reverse else (col <= row))
    return same, tri


def _chunk_order(i, reverse):
    if not reverse:
        return i
    return jnp.where(i < N_CTX_CHUNKS, N_CTX_CHUNKS - 1 - i, N_CHUNKS + N_CTX_CHUNKS - 1 - i)


def _decay_terms(z, lb, same01, tri01):
    f = lb + (1.0 - lb) * _sigmoid(z)
    g = jnp.log(f)
    g2 = jnp.concatenate(_split2(g), axis=1)
    b2 = _dot(tri01, g2)
    t2 = _dot(same01, g2)
    return f, 1.0 - f, b2[:, :HG_DIM] + b2[:, HG_DIM:], t2[:, :HG_DIM] + t2[:, HG_DIM:]


def _chunk_outer(a, b):
    n = TM // CHUNK
    return jnp.einsum('ncv,nck->nvk', a.reshape(n, CHUNK, HG_DIM), b.reshape(n, CHUNK, HG_DIM),
                      preferred_element_type=F32)


def _hgrn_fwd(p_a, lbl, carried=None):
    cpt = TM // CHUNK

    def body(p_ref, lbl_ref, o_ref, st_ref, qd_s, kd_s, u_s, v_s, ebt_s):
        masks = [_chunk_masks(d == 1) for d in (0, 1)]
        same01 = jnp.where(masks[0][0], 1.0, 0.0).astype(BF16)
        tri = [m[1] for m in masks]
        tri01 = [jnp.where(t, 1.0, 0.0).astype(BF16) for t in tri]
        lb = [_sigmoid(lbl_ref[d][0:1, :] - lbl_ref[d][1:2, :]) for d in (0, 1)]

        def prep(r, carry):
            r0 = pl.multiple_of(r * TM, TM)
            vb = p_ref[pl.ds(r0, TM), 2 * HG_DIM:3 * HG_DIM].astype(BF16)
            v_s[pl.ds(r0, TM), :] = vb
            for d in (0, 1):
                z = p_ref[pl.ds(r0, TM), d * HG_DIM:(d + 1) * HG_DIM]
                _, k, b, bt = _decay_terms(z, lb[d], same01, tri01[d])
                u_s[d, pl.ds(r * cpt, cpt)] = _chunk_outer(vb, (k * jnp.exp(bt - b)).astype(BF16))
                ebt_s[d, pl.ds(r0, TM), :] = jnp.exp(bt)

                @pl.when(r >= 1)
                def _():
                    rl = pl.multiple_of(r0 - L, TM)
                    qr = p_ref[pl.ds(r0, TM), 3 * HG_DIM:4 * HG_DIM]
                    q = qr * _sigmoid(qr) * HG_DIM ** -0.5
                    qd_s[d, pl.ds(rl, TM), :] = (q * jnp.exp(b)).astype(BF16)
                    kd_s[d, pl.ds(rl, TM), :] = (k * jnp.exp(-b)).astype(BF16)

            return carry

        lax.fori_loop(0, N_TILES, prep, 0)

        def scan(i, sts):
            new = []
            for d in (0, 1):
                nn = _chunk_order(i, d == 1)
                c0 = pl.multiple_of(nn * CHUNK, CHUNK)
                st_ref[d, nn] = sts[d].astype(BF16)
                new.append(sts[d] * ebt_s[d, pl.ds(c0, 1), :] + u_s[d, nn])
            return tuple(new)

        zero = jnp.zeros((HG_DIM, HG_DIM), F32)
        lax.fori_loop(0, N_CHUNKS, scan, (zero, zero))

        def outp(r, carry):
            r0 = pl.multiple_of(r * TM, TM)
            vb = v_s[pl.ds(r0 + L, TM), :]
            o = jnp.zeros((TM, HG_DIM), F32)
            for d in (0, 1):
                qd = qd_s[d, pl.ds(r0, TM), :]
                a = jnp.where(tri[d], _dot_nt(qd, kd_s[d, pl.ds(r0, TM), :]), 0.0)
                stb = st_ref[d, pl.ds(N_CTX_CHUNKS + r * cpt, cpt)]
                inter = jnp.einsum('nck,nvk->ncv', qd.reshape(cpt, CHUNK, HG_DIM), stb,
                                   preferred_element_type=F32)
                o = o + _dot(a.astype(BF16), vb) + inter.reshape(TM, HG_DIM)
            o_ref[pl.ds(r0, TM), :] = o
            return carry

        lax.fori_loop(0, N_LAT_TILES, outp, 0)

    return _pcall(
        body, carried, name="hgrn_fwd", grid=(HG_HEADS,),
        in_specs=[pl.BlockSpec((T, 4 * HG_DIM), lambda h: (0, h)),
                  pl.BlockSpec((2, 2, HG_DIM), lambda h: (0, 0, h))],
        out_specs=[pl.BlockSpec((S, HG_DIM), lambda h: (0, h)),
                   pl.BlockSpec((2, None, N_CHUNKS, HG_DIM, HG_DIM), lambda h: (0, h, 0, 0, 0))],
        out_shape=[jax.ShapeDtypeStruct((S, HGW), F32),
                   jax.ShapeDtypeStruct((2, HG_HEADS, N_CHUNKS, HG_DIM, HG_DIM), BF16)],
        scratch_shapes=[pltpu.VMEM((2, S, HG_DIM), BF16), pltpu.VMEM((2, S, HG_DIM), BF16),
                        pltpu.VMEM((2, N_CHUNKS, HG_DIM, HG_DIM), F32), pltpu.VMEM((T, HG_DIM), BF16),
                        pltpu.VMEM((2, T, HG_DIM), F32)],
        operands=[p_a, lbl])


def _hgrn_bwd(p_a, lbl, d_o, st, carried=None):
    cpt = TM // CHUNK

    def rows(r):
        return r * TM if isinstance(r, int) else pl.multiple_of(r * TM, TM)

    def body(p_ref, lbl_ref, do_ref, st_ref, dp_ref, dlb_ref, b_s, bt_s, dbt_s, qd_s, dst_s, w_s):
        masks = [_chunk_masks(d == 1) for d in (0, 1)]
        same01 = jnp.where(masks[0][0], 1.0, 0.0).astype(BF16)
        tri = [m[1] for m in masks]
        tri01 = [jnp.where(t, 1.0, 0.0).astype(BF16) for t in tri]
        later01 = [tri01[1], tri01[0]]
        lb = [_sigmoid(lbl_ref[d][0:1, :] - lbl_ref[d][1:2, :]) for d in (0, 1)]

        def prep_tile(r, latent):
            r0 = rows(r)
            for d in (0, 1):
                z = p_ref[pl.ds(r0, TM), d * HG_DIM:(d + 1) * HG_DIM]
                _, _, b, bt = _decay_terms(z, lb[d], same01, tri01[d])
                b_s[d, pl.ds(r0, TM), :] = b
                bt_s[d, pl.ds(r0, TM), :] = bt
                if latent:
                    rl = pl.multiple_of(r0 - L, TM)
                    qr = p_ref[pl.ds(r0, TM), 3 * HG_DIM:4 * HG_DIM]
                    qd = (qr * _sigmoid(qr) * HG_DIM ** -0.5 * jnp.exp(b)).astype(BF16)
                    qd_s[d, pl.ds(rl, TM), :] = qd
                    w_s[d, pl.ds(r * cpt, cpt)] = _chunk_outer(
                        do_ref[pl.ds(rl, TM), :].astype(BF16), qd).astype(BF16)

        prep_tile(0, False)
        w_s[:, pl.ds(0, N_CTX_CHUNKS)] = jnp.zeros((2, N_CTX_CHUNKS, HG_DIM, HG_DIM), BF16)

        def prep(r, carry):
            prep_tile(r, True)
            return carry

        lax.fori_loop(1, N_TILES, prep, 0)

        def rscan(j, dsts):
            i = N_CHUNKS - 1 - j
            new = []
            for d in (0, 1):
                nn = _chunk_order(i, d == 1)
                c0 = pl.multiple_of(nn * CHUNK, CHUNK)
                dst_s[d, nn] = dsts[d].astype(BF16)
                after = st_ref[d, _chunk_order(jnp.minimum(i + 1, N_CHUNKS - 1), d == 1)].astype(F32)
                dbt_s[d, pl.ds(c0, CHUNK), :] = jnp.broadcast_to(
                    jnp.sum(after * dsts[d], axis=0, keepdims=True), (CHUNK, HG_DIM))
                new.append(dsts[d] * jnp.exp(bt_s[d, pl.ds(c0, 1), :]) + w_s[d, nn].astype(F32))
            return tuple(new)

        zero = jnp.zeros((HG_DIM, HG_DIM), F32)
        lax.fori_loop(0, N_CHUNKS, rscan, (zero, zero))

        def grad_tile(r, latent):
            r0 = rows(r)
            vb = p_ref[pl.ds(r0, TM), 2 * HG_DIM:3 * HG_DIM].astype(BF16)
            dv = jnp.zeros((TM, HG_DIM), F32)
            dq = jnp.zeros((TM, HG_DIM), F32)
            dlbs = []
            if latent:
                rl = pl.multiple_of(r0 - L, TM)
                qr = p_ref[pl.ds(r0, TM), 3 * HG_DIM:4 * HG_DIM]
                sq = _sigmoid(qr)
                do = do_ref[pl.ds(rl, TM), :].astype(BF16)
                da_full = _dot_nt(do, vb)
            for d in (0, 1):
                z = p_ref[pl.ds(r0, TM), d * HG_DIM:(d + 1) * HG_DIM]
                sz = _sigmoid(z)
                f = lb[d] + (1.0 - lb[d]) * sz
                k = 1.0 - f
                b = b_s[d, pl.ds(r0, TM), :]
                e2 = jnp.exp(bt_s[d, pl.ds(r0, TM), :] - b)
                dstb = dst_s[d, pl.ds(r * cpt, cpt)]
                kd2 = k * e2
                dkd2 = jnp.einsum('ncv,nvk->nck', vb.reshape(cpt, CHUNK, HG_DIM), dstb,
                                  preferred_element_type=F32).reshape(TM, HG_DIM)
                dv = dv + jnp.einsum('nck,nvk->ncv', kd2.astype(BF16).reshape(cpt, CHUNK, HG_DIM), dstb,
                                     preferred_element_type=F32).reshape(TM, HG_DIM)
                dk = dkd2 * e2
                db = -(kd2 * dkd2)
                if latent:
                    eb = jnp.exp(b)
                    enb = jnp.exp(-b)
                    qdf = qr * sq * HG_DIM ** -0.5 * eb
                    kdf = k * enb
                    qd = qd_s[d, pl.ds(rl, TM), :]
                    kd = kdf.astype(BF16)
                    a = jnp.where(tri[d], _dot_nt(qd, kd), 0.0).astype(BF16)
                    da = jnp.where(tri[d], da_full, 0.0).astype(BF16)
                    stb = st_ref[d, pl.ds(r * cpt, cpt)]
                    dqd = _dot(da, kd) + jnp.einsum(
                        'ncv,nvk->nck', do.reshape(cpt, CHUNK, HG_DIM), stb,
                        preferred_element_type=F32).reshape(TM, HG_DIM)
                    dkd = _dot_tn(da, qd)
                    dv = dv + _dot_tn(a, do)
                    dk = dk + dkd * enb
                    db = db + qdf * dqd - kdf * dkd
                    dq = dq + dqd * eb
                dg = _dot_lhs01(later01[d], db) + dbt_s[d, pl.ds(r0, TM), :]
                df = dg / f - dk
                dp_ref[pl.ds(r0, TM), d * HG_DIM:(d + 1) * HG_DIM] = (
                    df * (1.0 - lb[d]) * sz * (1.0 - sz)).astype(BF16)
                dlbs.append(jnp.sum(df * (1.0 - sz), axis=0, keepdims=True))
            dp_ref[pl.ds(r0, TM), 2 * HG_DIM:3 * HG_DIM] = dv.astype(BF16)
            if latent:
                dq = dq * (HG_DIM ** -0.5) * (sq * (1.0 + qr * (1.0 - sq)))
            dp_ref[pl.ds(r0, TM), 3 * HG_DIM:4 * HG_DIM] = dq.astype(BF16)
            return dlbs

        dlb_ctx = grad_tile(0, False)

        def grads(r, acc):
            t = grad_tile(r, True)
            return (acc[0] + t[0], acc[1] + t[1])

        dlb = lax.fori_loop(1, N_TILES, grads, (dlb_ctx[0], dlb_ctx[1]))
        dlb_ref[0:1, :] = dlb[0]
        dlb_ref[1:2, :] = dlb[1]

    return _pcall(
        body, carried, name="hgrn_bwd", grid=(HG_HEADS,),
        in_specs=[pl.BlockSpec((T, 4 * HG_DIM), lambda h: (0, h)),
                  pl.BlockSpec((2, 2, HG_DIM), lambda h: (0, 0, h)),
                  pl.BlockSpec((S, HG_DIM), lambda h: (0, h)),
                  pl.BlockSpec((2, None, N_CHUNKS, HG_DIM, HG_DIM), lambda h: (0, h, 0, 0, 0))],
        out_specs=[pl.BlockSpec((T, 4 * HG_DIM), lambda h: (0, h)),
                   pl.BlockSpec((2, HG_DIM), lambda h: (0, h))],
        out_shape=[jax.ShapeDtypeStruct((T, WA), BF16), jax.ShapeDtypeStruct((2, HGW), F32)],
        scratch_shapes=[pltpu.VMEM((2, T, HG_DIM), F32), pltpu.VMEM((2, T, HG_DIM), F32),
                        pltpu.VMEM((2, T, HG_DIM), F32), pltpu.VMEM((2, S, HG_DIM), BF16),
                        pltpu.VMEM((2, N_CHUNKS, HG_DIM, HG_DIM), BF16),
                        pltpu.VMEM((2, N_CHUNKS, HG_DIM, HG_DIM), BF16)],
        operands=[p_a, lbl, d_o, st])


def _rope_tables():
    t = np.arange(S)
    inv = ROPE_THETA ** (-np.arange(0, 32, 2, dtype=np.float64) / 32)
    lane = np.arange(64)
    pos = np.where(lane[None, :] < 32, (t // GRID_W)[:, None], (t % GRID_W)[:, None]).astype(np.float64)
    ang = pos * inv[(lane % 32) % 16][None, :]
    sign = np.where((lane % 32) < 16, -1.0, 1.0)[None, :]
    cos = np.tile(np.cos(ang), (1, 2)).astype(np.float32)
    sin = np.tile(np.sin(ang) * sign, (1, 2)).astype(np.float32)
    return jnp.asarray(cos), jnp.asarray(sin)


def _rope_partner(v):
    lane = lax.broadcasted_iota(jnp.int32, (1, 128), 1)
    first = (lane % 32) < 16
    slabs = []
    for j in range(v.shape[1] // 128):
        s = v[:, 128 * j:128 * (j + 1)]
        slabs.append(jnp.where(first, pltpu.roll(s, 112, 1), pltpu.roll(s, 16, 1)))
    return slabs[0] if len(slabs) == 1 else jnp.concatenate(slabs, axis=1)


def _group_ones(width, group):
    r = lax.broadcasted_iota(jnp.int32, (width, width), 0)
    c = lax.broadcasted_iota(jnp.int32, (width, width), 1)
    return jnp.where((r // group) == (c // group), 1.0, 0.0).astype(BF16)


def _group_mean(v, ones01, group):
    hi = v.astype(BF16)
    lo = (v - hi.astype(F32)).astype(BF16)
    return (_dot(hi, ones01) + _dot(lo, ones01)) * (1.0 / group)


def _rep_matrix():
    r = lax.broadcasted_iota(jnp.int32, (KVW, ATW), 0)
    c = lax.broadcasted_iota(jnp.int32, (KVW, ATW), 1)
    return jnp.where(r == HEAD_DIM * (c // 256) + c % HEAD_DIM, 1.0, 0.0).astype(BF16)


def _tile_lanes(v, reps):
    return jnp.concatenate([v] * reps, axis=1)


def _prep_fwd(p_b, o, cos, sin, hnw, qnw, knw):
    def body(p_ref, o_ref, cos_ref, sin_ref, hnw_ref, qnw_ref, knw_ref, y_ref, q_ref, k_ref, v_ref):
        i = pl.program_id(0)
        rep = _rep_matrix()
        ones_k = _group_ones(KVW, HEAD_DIM)
        kr = p_ref[:, 1024:1152]
        krstd = lax.rsqrt(_group_mean(kr * kr, ones_k, HEAD_DIM) + EPS)
        kn = kr * krstd * knw_ref[...]
        v_ref[...] = _dot(p_ref[:, 1152:1280].astype(BF16), rep).astype(BF16)

        @pl.when(i == 0)
        def _():
            k_ref[...] = _dot(kn.astype(BF16), rep).astype(BF16)

        @pl.when(i > 0)
        def _():
            cs, sn = cos_ref[...], sin_ref[...]
            kro = kn * cs + _rope_partner(kn) * sn
            k_ref[...] = _dot(kro.astype(BF16), rep).astype(BF16)
            qr = p_ref[:, 512:1024]
            qrstd = lax.rsqrt(_group_mean(qr * qr, _group_ones(ATW, HEAD_DIM), HEAD_DIM) + EPS)
            qn = qr * qrstd * qnw_ref[...]
            qro = qn * _tile_lanes(cs, 4) + _rope_partner(qn) * _tile_lanes(sn, 4)
            q_ref[...] = (qro * HEAD_DIM ** -0.5).astype(BF16)
            ys = []
            for h in range(HG_HEADS):
                oh = o_ref[:, HG_DIM * h:HG_DIM * (h + 1)]
                gh = p_ref[:, HG_DIM * h:HG_DIM * (h + 1)]
                rstd = lax.rsqrt(jnp.mean(oh * oh, axis=-1, keepdims=True) + EPS)
                ys.append(oh * rstd * hnw_ref[...] * (gh * _sigmoid(gh)))
            y_ref[...] = jnp.concatenate(ys, axis=1).astype(BF16)

    return pl.pallas_call(
        body, name="prep_fwd", grid=(N_TILES,),
        in_specs=[pl.BlockSpec((TM, WB), lambda i: (i, 0)),
                  pl.BlockSpec((TM, HGW), lambda i: (_lat(i), 0)),
                  pl.BlockSpec((TM, 128), lambda i: (_lat(i), 0)),
                  pl.BlockSpec((TM, 128), lambda i: (_lat(i), 0)),
                  _full((1, HG_DIM)), _full((1, ATW)), _full((1, KVW))],
        out_specs=[pl.BlockSpec((TM, HGW), lambda i: (_lat(i), 0)),
                   pl.BlockSpec((TM, ATW), lambda i: (_lat(i), 0)),
                   pl.BlockSpec((TM, ATW), lambda i: (i, 0)),
                   pl.BlockSpec((TM, ATW), lambda i: (i, 0))],
        out_shape=[jax.ShapeDtypeStruct((S, HGW), BF16), jax.ShapeDtypeStruct((S, ATW), BF16),
                   jax.ShapeDtypeStruct((T, ATW), BF16), jax.ShapeDtypeStruct((T, ATW), BF16)],
        compiler_params=_cp(("arbitrary",)),
    )(p_b, o, cos, sin, hnw, qnw, knw)


def _prep_bwd(p_b, o, cos, sin, hnw, qnw, knw, dy_hg, dq, dk_rep, dv_rep, carried=None):
    def body(p_ref, o_ref, cos_ref, sin_ref, hnw_ref, qnw_ref, knw_ref, dy_ref, dq_ref, dk_ref, dv_ref,
             dp_ref, do_ref, acc_ref):
        i = pl.program_id(0)

        @pl.when(i == 0)
        def _():
            acc_ref[...] = jnp.zeros_like(acc_ref)

        rep = _rep_matrix()
        ones_k = _group_ones(KVW, HEAD_DIM)

        def fold(v):
            hi = v.astype(BF16)
            lo = (v - hi.astype(F32)).astype(BF16)
            return _dot_nt(hi, rep) + _dot_nt(lo, rep)

        kr = p_ref[:, 1024:1152]
        krstd = lax.rsqrt(_group_mean(kr * kr, ones_k, HEAD_DIM) + EPS)
        khat = kr * krstd
        kw = knw_ref[...]
        dkro = fold(dk_ref[...])
        dv = fold(dv_ref[...])

        def k_back(dkn):
            dkhat = dkn * kw
            dkr = krstd * (dkhat - khat * _group_mean(dkhat * khat, ones_k, HEAD_DIM))
            acc_ref[2:3, 0:KVW] += jnp.sum(dkn * khat, axis=0, keepdims=True)
            dp_ref[:, 1024:1152] = dkr.astype(BF16)
            dp_ref[:, 1152:1280] = dv.astype(BF16)

        @pl.when(i == 0)
        def _():
            k_back(dkro)
            dp_ref[:, 0:1024] = jnp.zeros((TM, 1024), BF16)

        @pl.when(i > 0)
        def _():
            cs, sn = cos_ref[...], sin_ref[...]
            k_back(dkro * cs + _rope_partner(dkro * sn))
            ones_q = _group_ones(ATW, HEAD_DIM)
            qr = p_ref[:, 512:1024]
            qrstd = lax.rsqrt(_group_mean(qr * qr, ones_q, HEAD_DIM) + EPS)
            qhat = qr * qrstd
            dqro = dq_ref[...] * HEAD_DIM ** -0.5
            dqn = dqro * _tile_lanes(cs, 4) + _rope_partner(dqro * _tile_lanes(sn, 4))
            dqhat = dqn * qnw_ref[...]
            dqr = qrstd * (dqhat - qhat * _group_mean(dqhat * qhat, ones_q, HEAD_DIM))
            acc_ref[1:2, :] += jnp.sum(dqn * qhat, axis=0, keepdims=True)
            dp_ref[:, 512:1024] = dqr.astype(BF16)
            dws = jnp.zeros((1, HG_DIM), F32)
            for h in range(HG_HEADS):
                sl = slice(HG_DIM * h, HG_DIM * (h + 1))
                oh, gh, dy = o_ref[:, sl], p_ref[:, sl], dy_ref[:, sl]
                rstd = lax.rsqrt(jnp.mean(oh * oh, axis=-1, keepdims=True) + EPS)
                ohat = oh * rstd
                sg = _sigmoid(gh)
                dp_ref[:, sl] = (dy * (ohat * hnw_ref[...]) * (sg * (1.0 + gh * (1.0 - sg)))).astype(BF16)
                dn = dy * (gh * sg)
                dws = dws + jnp.sum(dn * ohat, axis=0, keepdims=True)
                dohat = dn * hnw_ref[...]
                do_ref[:, sl] = rstd * (dohat - ohat * jnp.mean(dohat * ohat, axis=-1, keepdims=True))
            acc_ref[0:1, 0:HG_DIM] += dws

    return _pcall(
        body, carried, name="prep_bwd", grid=(N_TILES,),
        in_specs=[pl.BlockSpec((TM, WB), lambda i: (i, 0)),
                  pl.BlockSpec((TM, HGW), lambda i: (_lat(i), 0)),
                  pl.BlockSpec((TM, 128), lambda i: (_lat(i), 0)),
                  pl.BlockSpec((TM, 128), lambda i: (_lat(i), 0)),
                  _full((1, HG_DIM)), _full((1, ATW)), _full((1, KVW)),
                  pl.BlockSpec((TM, HGW), lambda i: (_lat(i), 0)),
                  pl.BlockSpec((TM, ATW), lambda i: (_lat(i), 0)),
                  pl.BlockSpec((TM, ATW), lambda i: (i, 0)),
                  pl.BlockSpec((TM, ATW), lambda i: (i, 0))],
        out_specs=[pl.BlockSpec((TM, WB), lambda i: (i, 0)),
                   pl.BlockSpec((TM, HGW), lambda i: (_lat(i), 0)),
                   _full((8, ATW))],
        out_shape=[jax.ShapeDtypeStruct((T, WB), BF16), jax.ShapeDtypeStruct((S, HGW), F32),
                   jax.ShapeDtypeStruct((8, ATW), F32)],
        scratch_shapes=[], operands=[p_b, o, cos, sin, hnw, qnw, knw, dy_hg, dq, dk_rep, dv_rep])


NEG = -1e30
_CTX_BLOCKS = L // BLOCK


def _attn_window_specs():
    prev = pl.BlockSpec((BLOCK, ATW), lambda i: (jnp.maximum(i - 1, 0) + _CTX_BLOCKS, 0))
    own = pl.BlockSpec((BLOCK, ATW), lambda i: (i + _CTX_BLOCKS, 0))
    nxt = pl.BlockSpec((BLOCK, ATW), lambda i: (jnp.minimum(i + 1, N_BLOCKS - 1) + _CTX_BLOCKS, 0))
    return [prev, own, nxt, _full((L, ATW))]


def _attn_valid(i, heads, context):
    n_keys = 3 * BLOCK + (L if context else 0)
    qi = lax.broadcasted_iota(jnp.int32, (heads * BLOCK, n_keys), 0) % BLOCK
    kj = lax.broadcasted_iota(jnp.int32, (heads * BLOCK, n_keys), 1)
    window = ((jnp.abs(kj - BLOCK - qi) <= BLOCK) & ((kj >= BLOCK) | (i > 0))
              & ((kj < 2 * BLOCK) | (i < N_BLOCKS - 1)))
    return window | (kj >= 3 * BLOCK)


def _stack_heads(qg):
    lane = lax.broadcasted_iota(jnp.int32, (1, 256), 1) // HEAD_DIM
    return jnp.concatenate([jnp.where(lane == g, qg, jnp.zeros_like(qg)) for g in range(4)], axis=0)


def _unstack_heads(v4):
    lane = lax.broadcasted_iota(jnp.int32, (1, 256), 1) // HEAD_DIM
    out = jnp.where(lane == 0, v4[0:BLOCK], 0.0)
    for g in range(1, 4):
        out = out + jnp.where(lane == g, v4[g * BLOCK:(g + 1) * BLOCK], 0.0)
    return out


def _sink_rows(sink_ref, hk):
    return jnp.concatenate(
        [jnp.broadcast_to(sink_ref[0:1, 4 * hk + g:4 * hk + g + 1], (BLOCK, 1)) for g in range(4)], axis=0)


def _attn_fwd(q, k_rep, v_rep, sinks, carried=None):
    def body(q_ref, kp, ko, kn, kc, vp, vo, vn, vc, sink_ref, y_ref, lse_ref):
        i = pl.program_id(0)
        valid = _attn_valid(i, 1, True)
        lane8 = lax.broadcasted_iota(jnp.int32, (1, ATT_HEADS), 1)
        head_of_lane = lax.broadcasted_iota(jnp.int32, (1, 256), 1) // HEAD_DIM
        lse_out = jnp.zeros((BLOCK, ATT_HEADS), F32)
        for hk in range(KV_HEADS):
            sl = slice(256 * hk, 256 * (hk + 1))
            qg = q_ref[:, sl]
            keys = jnp.concatenate([kp[:, sl], ko[:, sl], kn[:, sl], kc[:, sl]], axis=0)
            vals = jnp.concatenate([vp[:, sl], vo[:, sl], vn[:, sl], vc[:, sl]], axis=0)
            yg = jnp.zeros((BLOCK, 256), F32)
            for g in range(4):
                q1 = jnp.where(head_of_lane == g, qg, jnp.zeros_like(qg))
                s = jnp.where(valid, _dot_nt(q1, keys), NEG)
                sink = sink_ref[0:1, 4 * hk + g:4 * hk + g + 1]
                m = jnp.maximum(jnp.max(s, axis=1, keepdims=True), sink)
                p = jnp.exp(s - m)
                den = jnp.sum(p, axis=1, keepdims=True) + jnp.exp(sink - m)
                o1 = _dot(p.astype(BF16), vals) * (1.0 / den)
                yg = yg + jnp.where(head_of_lane == g, o1, 0.0)
                lse_out = lse_out + jnp.where(lane8 == 4 * hk + g, m + jnp.log(den), 0.0)
            y_ref[:, sl] = yg.astype(BF16)
        lse_ref[...] = lse_out

    return _pcall(
        body, carried, name="attn_fwd", grid=(N_BLOCKS,),
        in_specs=[pl.BlockSpec((BLOCK, ATW), lambda i: (i, 0))] + _attn_window_specs()
        + _attn_window_specs() + [_full((1, ATT_HEADS))],
        out_specs=[pl.BlockSpec((BLOCK, ATW), lambda i: (i, 0)),
                   pl.BlockSpec((BLOCK, ATT_HEADS), lambda i: (i, 0))],
        out_shape=[jax.ShapeDtypeStruct((S, ATW), BF16), jax.ShapeDtypeStruct((S, ATT_HEADS), F32)],
        scratch_shapes=[],
        operands=[q, k_rep, k_rep, k_rep, k_rep, v_rep, v_rep, v_rep, v_rep, sinks])


def _attn_bwd(q, k_rep, v_rep, sinks, y_at, lse, dy, carried=None):
    def body(q_ref, kp, ko, kn, kc, vp, vo, vn, vc, sink_ref, y_ref, lse_ref, dy_ref,
             dq_ref, dk_ref, dv_ref, dsink_ref, dk_acc, dv_acc):
        i = pl.program_id(0)

        @pl.when(i == 0)
        def _():
            dk_acc[...] = jnp.zeros_like(dk_acc)
            dv_acc[...] = jnp.zeros_like(dv_acc)
            dk_ref[pl.ds(0, L), :] = jnp.zeros((L, ATW), F32)
            dv_ref[pl.ds(0, L), :] = jnp.zeros((L, ATW), F32)
            dsink_ref[...] = jnp.zeros_like(dsink_ref)

        valid = _attn_valid(i, 4, False)
        lane8 = lax.broadcasted_iota(jnp.int32, (1, ATT_HEADS), 1)
        w0 = pl.multiple_of(i * BLOCK, BLOCK)
        dsink = jnp.zeros((1, ATT_HEADS), F32)
        for hk in range(KV_HEADS):
            sl = slice(256 * hk, 256 * (hk + 1))
            q4 = _stack_heads(q_ref[:, sl])
            do4f = _stack_heads(dy_ref[:, sl])
            o4 = _stack_heads(y_ref[:, sl]).astype(F32)
            do4 = do4f.astype(BF16)
            kl = jnp.concatenate([kp[:, sl], ko[:, sl], kn[:, sl]], axis=0)
            vl = jnp.concatenate([vp[:, sl], vo[:, sl], vn[:, sl]], axis=0)
            lse4 = jnp.concatenate(
                [jnp.sum(jnp.where(lane8 == 4 * hk + g, lse_ref[...], 0.0), axis=1, keepdims=True)
                 for g in range(4)], axis=0)
            p_loc = jnp.where(valid, jnp.exp(_dot_nt(q4, kl) - lse4), 0.0)
            p_ctx = jnp.exp(_dot_nt(q4, kc[:, sl]) - lse4)
            delta = jnp.sum(do4f * o4, axis=1, keepdims=True)
            ds_loc = (p_loc * (_dot_nt(do4, vl) - delta)).astype(BF16)
            ds_ctx = (p_ctx * (_dot_nt(do4, vc[:, sl]) - delta)).astype(BF16)
            dq_ref[:, sl] = _unstack_heads(_dot(ds_loc, kl) + _dot(ds_ctx, kc[:, sl]))
            dk_acc[pl.ds(w0, 3 * BLOCK), sl] += _dot_tn(ds_loc, q4)
            dv_acc[pl.ds(w0, 3 * BLOCK), sl] += _dot_tn(p_loc.astype(BF16), do4)
            dk_ref[pl.ds(0, L), sl] += _dot_tn(ds_ctx, q4)
            dv_ref[pl.ds(0, L), sl] += _dot_tn(p_ctx.astype(BF16), do4)
            p_sink = jnp.exp(_sink_rows(sink_ref, hk) - lse4)
            for g in range(4):
                rows = slice(g * BLOCK, (g + 1) * BLOCK)
                dsink = dsink + jnp.where(lane8 == 4 * hk + g,
                                          -jnp.sum(p_sink[rows] * delta[rows], axis=0, keepdims=True), 0.0)
        dsink_ref[...] += dsink

        @pl.when(i == N_BLOCKS - 1)
        def _():
            dk_ref[pl.ds(L, S), :] = dk_acc[pl.ds(BLOCK, S), :]
            dv_ref[pl.ds(L, S), :] = dv_acc[pl.ds(BLOCK, S), :]

    row_q = pl.BlockSpec((BLOCK, ATW), lambda i: (i, 0))
    return _pcall(
        body, carried, name="attn_bwd", grid=(N_BLOCKS,),
        in_specs=[row_q] + _attn_window_specs() + _attn_window_specs()
        + [_full((1, ATT_HEADS)), row_q, pl.BlockSpec((BLOCK, ATT_HEADS), lambda i: (i, 0)), row_q],
        out_specs=[row_q, _full((T, ATW)), _full((T, ATW)), _full((1, ATT_HEADS))],
        out_shape=[jax.ShapeDtypeStruct((S, ATW), F32), jax.ShapeDtypeStruct((T, ATW), F32),
                   jax.ShapeDtypeStruct((T, ATW), F32), jax.ShapeDtypeStruct((1, ATT_HEADS), F32)],
        scratch_shapes=[pltpu.VMEM((S + 2 * BLOCK, ATW), F32), pltpu.VMEM((S + 2 * BLOCK, ATW), F32)],
        operands=[q, k_rep, k_rep, k_rep, k_rep, v_rep, v_rep, v_rep, v_rep, sinks, y_at, lse, dy])


def _merge_fwd(y_hg, y_at, p_c, x, w_bh, w_ba, w_out, g1, nfw, sh2, sc2, carried=None):
    def body(yh_ref, ya_ref, g_ref, x_ref, wbh_ref, wba_ref, wo_ref, g1_ref, nfw_ref, sh_ref, sc_ref,
             a_ref, b_ref, mx_ref, r_ref, x1_ref, h2_ref):
        a = _dot_nt(yh_ref[...], wbh_ref[...])
        b = _dot_nt(ya_ref[...], wba_ref[...])
        mixed = (_sigmoid(g_ref[:, :D]) * a + _sigmoid(g_ref[:, D:]) * b).astype(BF16)
        r = _dot(mixed, wo_ref[...])
        x1 = x_ref[...] + g1_ref[...] * r
        a_ref[...] = a
        b_ref[...] = b
        mx_ref[...] = mixed
        r_ref[...] = r
        x1_ref[...] = x1
        h2_ref[...] = _rms_mod(x1, nfw_ref[...], sh_ref[...], sc_ref[...]).astype(BF16)

    row = lambda w: pl.BlockSpec((TM, w), lambda i: (i, 0))
    vec = _full((1, D))
    return _pcall(
        body, carried, name="merge_fwd", grid=(N_LAT_TILES,),
        in_specs=[row(HGW), row(ATW), row(WC), row(D), _VMEM_WHOLE, _VMEM_WHOLE, _VMEM_WHOLE,
                  vec, vec, vec, vec],
        out_specs=[row(D)] * 6,
        out_shape=[jax.ShapeDtypeStruct((S, D), dt) for dt in (F32, F32, BF16, F32, F32, BF16)],
        scratch_shapes=[], operands=[y_hg, y_at, p_c, x, w_bh, w_ba, w_out, g1, nfw, sh2, sc2])


def _merge_bwd(dx1, r, a, b, p_c, w_bh, w_ba, w_out, g1, carried=None):
    def body(dx_ref, r_ref, a_ref, b_ref, g_ref, wbh_ref, wba_ref, wo_ref, g1_ref,
             dr_ref, da_ref, db_ref, dg_ref, dyh_ref, dya_ref, acc_ref):
        @pl.when(pl.program_id(0) == 0)
        def _():
            acc_ref[...] = jnp.zeros_like(acc_ref)

        dx1v = dx_ref[...]
        acc_ref[0:1, :] += jnp.sum(dx1v * r_ref[...], axis=0, keepdims=True)
        dr = (g1_ref[...] * dx1v).astype(BF16)
        dr_ref[...] = dr
        dmix = _dot_nt(dr, wo_ref[...])
        sh, sa = _sigmoid(g_ref[:, :D]), _sigmoid(g_ref[:, D:])
        da = (dmix * sh).astype(BF16)
        db = (dmix * sa).astype(BF16)
        da_ref[...] = da
        db_ref[...] = db
        dg_ref[:, :D] = (dmix * a_ref[...] * sh * (1.0 - sh)).astype(BF16)
        dg_ref[:, D:] = (dmix * b_ref[...] * sa * (1.0 - sa)).astype(BF16)
        dyh_ref[...] = _dot(da, wbh_ref[...])
        dya_ref[...] = _dot(db, wba_ref[...])

    row = lambda w: pl.BlockSpec((TM, w), lambda i: (i, 0))
    return _pcall(
        body, carried, name="merge_bwd", grid=(N_LAT_TILES,),
        in_specs=[row(D), row(D), row(D), row(D), row(WC), _VMEM_WHOLE, _VMEM_WHOLE, _VMEM_WHOLE,
                  _full((1, D))],
        out_specs=[row(D), row(D), row(D), row(WC), row(HGW), row(ATW), _full((8, D))],
        out_shape=[jax.ShapeDtypeStruct((S, D), BF16), jax.ShapeDtypeStruct((S, D), BF16),
                   jax.ShapeDtypeStruct((S, D), BF16), jax.ShapeDtypeStruct((S, WC), BF16),
                   jax.ShapeDtypeStruct((S, HGW), F32), jax.ShapeDtypeStruct((S, ATW), F32),
                   jax.ShapeDtypeStruct((8, D), F32)],
        scratch_shapes=[], operands=[dx1, r, a, b, p_c, w_bh, w_ba, w_out, g1])


def _ffn_fused(x1, h2, tgt, w_gate, w_up, w_down, g2, nfw, sc2):
    def body(x1_ref, h2_ref, t_ref, wg_ref, wu_ref, wd_ref, g2_ref, nfw_ref, sc_ref,
             act_ref, dgt_ref, dup_ref, df_ref, dx_ref, acc_ref, gs, us):
        @pl.when(pl.program_id(0) == 0)
        def _():
            acc_ref[...] = jnp.zeros_like(acc_ref)

        h2 = h2_ref[...]
        whole = lambda w_ref: w_ref[...].reshape(D_FF, D)
        wide = lambda t_ref: jnp.concatenate([t_ref[j] for j in range(N_FF_TILES)], axis=1)
        for j in range(N_FF_TILES):
            g = _dot_nt(h2, wg_ref[j])
            u = _dot_nt(h2, wu_ref[j])
            gs[j] = g
            us[j] = u
            act_ref[j] = (g * _sigmoid(g) * u).astype(BF16)
        f = _dot(wide(act_ref), whole(wd_ref))
        x1v = x1_ref[...]
        g2 = g2_ref[...]
        diff = x1v + g2 * f - t_ref[...]
        dy = diff * (1.0 / D)
        df = (g2 * dy).astype(BF16)
        df_ref[...] = df
        dact_all = _dot_nt(df, whole(wd_ref))
        for j in range(N_FF_TILES):
            g, u = gs[j], us[j]
            sg = _sigmoid(g)
            dact = dact_all[:, j * FF_TILE:(j + 1) * FF_TILE]
            dgt_ref[j] = (dact * u * (sg * (1.0 + g * (1.0 - sg)))).astype(BF16)
            dup_ref[j] = (dact * (g * sg)).astype(BF16)
        dh2 = _dot(wide(dgt_ref), whole(wg_ref)) + _dot(wide(dup_ref), whole(wu_ref))
        dx, dsh, dsc, dnw = _rms_mod_bwd(x1v, nfw_ref[...], sc_ref[...], dh2)
        dx_ref[...] = dy + dx
        acc_ref[0:1, :] += dsh
        acc_ref[1:2, :] += dsc
        acc_ref[2:3, :] += dnw
        acc_ref[3:4, :] += jnp.sum(dy * f, axis=0, keepdims=True)
        acc_ref[4:5, :] += 0.5 * jnp.sum(jnp.sum(diff * diff, axis=1, keepdims=True), axis=0,
                                         keepdims=True) * (1.0 / D)

    row = lambda dt_w: pl.BlockSpec((TM, dt_w), lambda i: (i, 0))
    blk = pl.BlockSpec((N_FF_TILES, TM, FF_TILE), lambda i: (0, i, 0))
    vec = _full((1, D))
    return pl.pallas_call(
        body, name="ffn_fused", grid=(N_LAT_TILES,),
        in_specs=[row(D), row(D), row(D), _VMEM_WHOLE, _VMEM_WHOLE, _VMEM_WHOLE, vec, vec, vec],
        out_specs=[blk, blk, blk, row(D), row(D), _full((8, D))],
        out_shape=[jax.ShapeDtypeStruct((N_FF_TILES, S, FF_TILE), BF16)] * 3
        + [jax.ShapeDtypeStruct((S, D), BF16), jax.ShapeDtypeStruct((S, D), F32),
           jax.ShapeDtypeStruct((8, D), F32)],
        scratch_shapes=[pltpu.VMEM((N_FF_TILES, TM, FF_TILE), F32), pltpu.VMEM((N_FF_TILES, TM, FF_TILE), F32)],
        compiler_params=_cp(("arbitrary",)),
    )(x1, h2, tgt, w_gate, w_up, w_down, g2, nfw, sc2)


def _proj_bc(h_all, w_b, w_c, carried=None):
    def body(h_ref, wb_ref, wc_ref, pb_ref, pc_ref):
        h = h_ref[...]
        pb_ref[...] = _dot_nt(h, wb_ref[...])

        @pl.when(pl.program_id(0) > 0)
        def _():
            pc_ref[...] = _dot_nt(h, wc_ref[...])

    return _pcall(
        body, carried, name="proj_bc", grid=(N_TILES,),
        in_specs=[pl.BlockSpec((TM, D), lambda i: (i, 0)), _VMEM_WHOLE, _VMEM_WHOLE],
        out_specs=[pl.BlockSpec((TM, WB), lambda i: (i, 0)), pl.BlockSpec((TM, WC), lambda i: (_lat(i), 0))],
        out_shape=[jax.ShapeDtypeStruct((T, WB), F32), jax.ShapeDtypeStruct((S, WC), F32)],
        scratch_shapes=[], operands=[h_all, w_b, w_c])


def _input_bwd(dp_a, dp_b, dp_c, w_a, w_b, w_c, ctx, x, dx1, nw, sh, sc, carried=None):
    def body(da_ref, db_ref, dc_ref, wa_ref, wb_ref, wc_ref, ctx_ref, x_ref, dx1_ref, nw_ref, sh_ref,
             sc_ref, gx_ref, acc_ref):
        i = pl.program_id(0)

        @pl.when(i == 0)
        def _():
            acc_ref[...] = jnp.zeros_like(acc_ref)

        dh = _dot(da_ref[...], wa_ref[...]) + _dot(db_ref[...], wb_ref[...])

        @pl.when(i == 0)
        def _():
            _, dsh, dsc, dnw = _rms_mod_bwd(ctx_ref[...], nw_ref[...], sc_ref[0:1, :], dh)
            acc_ref[3:4, :] += dsh
            acc_ref[4:5, :] += dsc
            acc_ref[2:3, :] += dnw

        @pl.when(i > 0)
        def _():
            dhl = dh + _dot(dc_ref[...], wc_ref[...])
            dx, dsh, dsc, dnw = _rms_mod_bwd(x_ref[...], nw_ref[...], sc_ref[1:2, :], dhl)
            gx_ref[...] = dx1_ref[...] + dx
            acc_ref[0:1, :] += dsh
            acc_ref[1:2, :] += dsc
            acc_ref[2:3, :] += dnw

    lat = lambda w: pl.BlockSpec((TM, w), lambda i: (_lat(i), 0))
    return _pcall(
        body, carried, name="input_bwd", grid=(N_TILES,),
        in_specs=[pl.BlockSpec((TM, WA), lambda i: (i, 0)), pl.BlockSpec((TM, WB), lambda i: (i, 0)),
                  lat(WC), _VMEM_WHOLE, _VMEM_WHOLE, _VMEM_WHOLE, _full((TM, D)), lat(D), lat(D),
                  _full((1, D)), _full((2, D)), _full((2, D))],
        out_specs=[lat(D), _full((8, D))],
        out_shape=[jax.ShapeDtypeStruct((S, D), F32), jax.ShapeDtypeStruct((8, D), F32)],
        scratch_shapes=[], operands=[dp_a, dp_b, dp_c, w_a, w_b, w_c, ctx, x, dx1, nw, sh, sc])


_C1 = 1.0 - ADAM_B1 ** ADAM_STEP
_C2 = 1.0 - ADAM_B2 ** ADAM_STEP


def _adamw_math(w, g, m, v):
    m = ADAM_B1 * m + (1.0 - ADAM_B1) * g
    v = ADAM_B2 * v + (1.0 - ADAM_B2) * (g * g)
    m_hat = m / _C1
    v_hat = v / _C2
    delta = -ADAM_LR * (m_hat / (jnp.sqrt(v_hat) + ADAM_EPS) + ADAM_WD * w)
    return delta, m, v


def _adamw_sharded(terms, w, m, v, name, tr, extra=None):
    rows, cols = w.shape

    def body(*refs):
        t_ref, w_ref, m_ref, v_ref = refs[:4]
        g_ref, d_ref, nm_ref, nv_ref = refs[-4:]
        g = t_ref[0].astype(F32)
        for s in range(1, N_CHIPS):
            g = g + t_ref[s].astype(F32)
        if extra is not None:
            g = g + refs[4][...].astype(F32)
        g_ref[...] = g
        d_ref[...], nm_ref[...], nv_ref[...] = _adamw_math(w_ref[...], g, m_ref[...], v_ref[...])

    blk = pl.BlockSpec((tr, cols), lambda i: (i, 0))
    return pl.pallas_call(
        body, name=name, grid=(rows // tr,),
        in_specs=[pl.BlockSpec((N_CHIPS, tr, cols), lambda i: (0, i, 0)), blk, blk, blk]
        + ([blk] if extra is not None else []),
        out_specs=[blk] * 4,
        out_shape=[jax.ShapeDtypeStruct((rows, cols), F32)] * 4,
        compiler_params=_cp(("parallel",)),
    )(terms, w, m, v, *([extra] if extra is not None else []))


def _adamw_plain(g, w, m, v, name):
    def body(g_ref, w_ref, m_ref, v_ref, d_ref, nm_ref, nv_ref):
        d_ref[...], nm_ref[...], nv_ref[...] = _adamw_math(w_ref[...], g_ref[...], m_ref[...], v_ref[...])

    return pl.pallas_call(
        body, name=name, in_specs=[_VMEM_WHOLE] * 4, out_specs=[_VMEM_WHOLE] * 3,
        out_shape=[jax.ShapeDtypeStruct(w.shape, F32)] * 3,
        compiler_params=_cp(),
    )(g, w, m, v)


SMALL_ROWS = 16
R_DMOD, R_DCTX, R_NMIX, R_NFFN, R_MISC, R_DLB, R_BADA01 = 0, 6, 8, 9, 10, 11, 13
M_HNW, M_QNW, M_KNW, M_SINK, M_LOSS = 0, 128, 256, 384, 512


def _pack_small(acc_in, acc_mg, acc_ffn, acc_prep, dsink, dlb):
    def body(in_ref, mg_ref, ff_ref, pp_ref, ds_ref, dlb_ref, o_ref):
        o_ref[...] = jnp.zeros_like(o_ref)
        o_ref[0:2, :] = in_ref[0:2, :]
        o_ref[2:3, :] = mg_ref[0:1, :]
        o_ref[3:5, :] = ff_ref[0:2, :]
        o_ref[5:6, :] = ff_ref[3:4, :]
        o_ref[6:8, :] = in_ref[3:5, :]
        o_ref[8:9, :] = in_ref[2:3, :]
        o_ref[9:10, :] = ff_ref[2:3, :]
        o_ref[10:11, M_HNW:M_HNW + HG_DIM] = pp_ref[0:1, 0:HG_DIM]
        r = lax.broadcasted_iota(jnp.int32, (ATW, 128), 0)
        c = lax.broadcasted_iota(jnp.int32, (ATW, 128), 1)
        fold = jnp.where((r % HEAD_DIM == c) & (c < HEAD_DIM), 1.0, 0.0).astype(BF16)
        qk = jnp.concatenate([pp_ref[1:2, :], pp_ref[2:3, :], jnp.zeros((6, ATW), F32)], axis=0)
        folded = _dot_exact_rhs01(qk, fold)
        o_ref[10:11, M_QNW:M_QNW + 128] = folded[0:1, :]
        o_ref[10:11, M_KNW:M_KNW + 128] = folded[1:2, :]
        o_ref[10:11, M_SINK:M_SINK + ATT_HEADS] = ds_ref[...]
        o_ref[10:11, M_LOSS:M_LOSS + 128] = ff_ref[4:5, 0:128]
        o_ref[11:13, 0:HGW] = dlb_ref[...]

    return pl.pallas_call(
        body, name="pack_small", in_specs=[_VMEM_WHOLE] * 6, out_specs=_VMEM_WHOLE,
        out_shape=jax.ShapeDtypeStruct((SMALL_ROWS, D), F32), compiler_params=_cp(),
    )(acc_in, acc_mg, acc_ffn, acc_prep, dsink, dlb)


def _sum_small(gathered):
    def body(g_ref, o_ref):
        tot = g_ref[0]
        for s in range(1, N_DEV):
            tot = tot + g_ref[s]
        o_ref[...] = tot
        o_ref[R_BADA01:R_BADA01 + 2, :] = tot[0:2, :] + tot[R_DCTX:R_DCTX + 2, :]

    return pl.pallas_call(
        body, name="sum_small", in_specs=[_VMEM_WHOLE], out_specs=_VMEM_WHOLE,
        out_shape=jax.ShapeDtypeStruct((SMALL_ROWS, D), F32), compiler_params=_cp(),
    )(gathered)


_REP_NAMES = ("b_ada", "c_ctx", "norm_mix_w", "norm_ffn_w", "hgrn_norm_w", "q_norm_w", "k_norm_w", "attn_sinks")


def _adamw_replicated(tot, g_c_ctx, ws, ms, vs):
    n = len(_REP_NAMES)

    def body(*refs):
        tot_ref, gc_ref = refs[0], refs[1]
        w_refs, m_refs, v_refs = refs[2:2 + n], refs[2 + n:2 + 2 * n], refs[2 + 2 * n:2 + 3 * n]
        outs = refs[2 + 3 * n:]
        row = lambda r: tot_ref[r:r + 1, :]
        misc = row(R_MISC)
        grads = [jnp.concatenate([row(R_BADA01), row(R_BADA01 + 1)] + [row(k) for k in range(2, 6)], axis=1),
                 gc_ref[...], row(R_NMIX), row(R_NFFN),
                 misc[:, M_HNW:M_HNW + HG_DIM], misc[:, M_QNW:M_QNW + HEAD_DIM],
                 misc[:, M_KNW:M_KNW + HEAD_DIM], misc[:, M_SINK:M_SINK + ATT_HEADS]]
        for k in range(n):
            outs[k][...] = grads[k]
            outs[n + k][...], outs[2 * n + k][...], outs[3 * n + k][...] = _adamw_math(
                w_refs[k][...], grads[k], m_refs[k][...], v_refs[k][...])

    shapes = [jax.ShapeDtypeStruct(w.shape, F32) for w in ws]
    return pl.pallas_call(
        body, name="adamw_replicated", in_specs=[_VMEM_WHOLE] * (2 + 3 * n), out_specs=[_VMEM_WHOLE] * (4 * n),
        out_shape=shapes * 4, compiler_params=_cp(),
    )(tot, g_c_ctx, *ws, *ms, *vs)


def _lb_grads(dlb, lbl):
    def body(d_ref, l_ref, o_ref):
        for d in (0, 1):
            ll = l_ref[d]
            lb = _sigmoid(ll[0:1, :] - ll[1:2, :])
            t = d_ref[d:d + 1, :] * lb * (1.0 - lb)
            o_ref[d, 0:1, :] = t
            o_ref[d, 1:2, :] = -t

    return pl.pallas_call(
        body, name="lb_grads", in_specs=[_VMEM_WHOLE] * 2, out_specs=_VMEM_WHOLE,
        out_shape=jax.ShapeDtypeStruct((2, 2, HGW), F32), compiler_params=_cp(),
    )(dlb, lbl)


def _c_ctx_grad(terms, c_ctx):
    def body(t_ref, c_ref, o_ref):
        tot = t_ref[0, 8:9, :]
        for s in range(1, N_DEV):
            tot = tot + t_ref[s, 8:9, :]
        cv = c_ref[...]
        sg = _sigmoid(cv)
        o_ref[...] = tot * (sg * (1.0 + cv * (1.0 - sg)))

    return pl.pallas_call(
        body, name="c_ctx_grad", in_specs=[_VMEM_WHOLE] * 2, out_specs=_VMEM_WHOLE,
        out_shape=jax.ShapeDtypeStruct((1, D), F32), compiler_params=_cp(),
    )(terms, c_ctx)


def _in_perm():
    fz, bz, inp, kk, vv, qhg, ghg, qat, gates = 0, 512, 1024, 1536, 1664, 1792, 2304, 2816, 3328
    cols = []
    for h in range(HG_HEADS):
        for base in (fz, bz, inp, qhg):
            cols += list(range(base + 128 * h, base + 128 * (h + 1)))
    cols += list(range(ghg, ghg + 512)) + list(range(qat, qat + 512))
    cols += list(range(kk, kk + 128)) + list(range(vv, vv + 128))
    cols += list(range(gates, gates + 2048))
    return np.asarray(cols, np.int32)


_PERM = _in_perm()
_INV_PERM = np.argsort(_PERM).astype(np.int32)


_PIECES = {"a": (0, WA, 128), "b": (WA, WB, 256), "c": (WA + WB, WC, 256)}


def _block_table(piece):
    lo, n, blk = _PIECES[piece]
    starts = [int(_PERM[r]) for r in range(lo, lo + n, blk)]
    assert all(s % blk == 0 and np.array_equal(_PERM[r:r + blk], np.arange(s, s + blk))
               for s, r in zip(starts, range(lo, lo + n, blk)))
    return jnp.asarray([s // blk for s in starts], jnp.int32), blk


def _pick_row_blocks(x, table, blk, name):
    cols = x.shape[1]

    def body(t_ref, x_ref, o_ref):
        o_ref[...] = x_ref[...]

    return pl.pallas_call(
        body, name=name,
        grid_spec=pltpu.PrefetchScalarGridSpec(
            num_scalar_prefetch=1, grid=(table.shape[0],),
            in_specs=[pl.BlockSpec((blk, cols), lambda i, t: (t[i], 0))],
            out_specs=pl.BlockSpec((blk, cols), lambda i, t: (i, 0))),
        out_shape=jax.ShapeDtypeStruct((table.shape[0] * blk, cols), x.dtype),
        compiler_params=_cp(("arbitrary",)),
    )(table, x)


def _place_row_blocks(x, table, blk, into, out_rows, name):
    cols = x.shape[1]

    def body(t_ref, x_ref, *rest):
        rest[-1][...] = x_ref[...]

    operands, in_specs, aliases = [table, x], [pl.BlockSpec((blk, cols), lambda i, t: (i, 0))], {}
    if into is not None:
        operands.append(into)
        in_specs.append(_ANY)
        aliases = {2: 0}
    return pl.pallas_call(
        body, name=name,
        grid_spec=pltpu.PrefetchScalarGridSpec(
            num_scalar_prefetch=1, grid=(table.shape[0],), in_specs=in_specs,
            out_specs=pl.BlockSpec((blk, cols), lambda i, t: (t[i], 0))),
        out_shape=jax.ShapeDtypeStruct((out_rows, cols), x.dtype),
        input_output_aliases=aliases,
        compiler_params=_cp(("arbitrary",)),
    )(*operands)


def _cols_from_blocks(g):
    return jnp.transpose(g, (1, 0, 2)).reshape(g.shape[1], N_DEV * g.shape[2])


def _local_step(x2, ctx2, tgt, lbl, sh_in, sc_in, gate1, sh2, sc2, gate2, norm_mix_w, norm_ffn_w,
                hgrn_norm_w, q_norm_w, k_norm_w, attn_sinks, w_a, w_b, w_c, s_bh, s_ba, s_out,
                s_gate, s_up, s_down):
    first_last = lambda n: [(0, True), (n - 1, False)]
    h_all = _norm_mod_all(ctx2, x2, norm_mix_w, sh_in, sc_in)
    p_a = _mm_nt(h_all, w_a, tm=768, tn=1024, out_dtype=F32, name="proj_a")
    (o, st), (g_gate, g_bh, g_ba) = _hgrn_fwd(
        p_a, lbl, (_gather_comm_relayed([s_gate, s_bh, s_ba]),
                   [(0, True), (HG_HEADS - 2, True), (HG_HEADS - 1, False)]))
    (p_b, p_c), (g_out,) = _proj_bc(
        h_all, w_b, w_c, (_gather_comm_relayed([s_out]), [(0, True), (N_TILES - 4, True), (N_TILES - 1, False)]))
    cos, sin = _rope_tables()
    qnw_t, knw_t = jnp.tile(q_norm_w, (1, ATT_HEADS)), jnp.tile(k_norm_w, (1, KV_HEADS))
    y_hg, qn, k_rep, v_rep = _prep_fwd(p_b, o, cos, sin, hgrn_norm_w, qnw_t, knw_t)
    (y_at, lse), (g_up, g_down) = _attn_fwd(
        qn, k_rep, v_rep, attn_sinks,
        (_gather_comm_relayed([s_up, s_down]), [(0, True), (N_BLOCKS - 6, True), (N_BLOCKS - 1, False)]))
    w_bh, w_ba, w_o = g_bh.reshape(D, HGW), g_ba.reshape(D, ATW), g_out.reshape(D, D)
    (a, b, mixed, r, x1, h2), _ = _merge_fwd(
        y_hg, y_at, p_c, x2, w_bh, w_ba, w_o, gate1, norm_ffn_w, sh2, sc2)
    g_gate, g_up, g_down = [g.reshape(N_FF_TILES, FF_TILE, D) for g in (g_gate, g_up, g_down)]

    act, d_gate, d_up, d_f, dx1, acc_ffn = _ffn_fused(x1, h2, tgt, g_gate, g_up, g_down, gate2,
                                                      norm_ffn_w, sc2)
    by_chip = lambda t: t.reshape((N_CHIPS, 2) + t.shape[1:])
    ff_by_chip = lambda t: t.reshape(N_CHIPS, 2, FF_BLK, D)
    t_down, _ = _mm_tn_blocked(act, d_f, "grad_down")
    t_down = ff_by_chip(t_down)
    t_gate, (f_down,) = _mm_tn_blocked(d_gate, h2, "grad_gate", (_sibling_comm([t_down]), first_last(N_FF_TILES)))
    t_gate = ff_by_chip(t_gate)
    t_up, (f_gate,) = _mm_tn_blocked(d_up, h2, "grad_up", (_sibling_comm([t_gate]), first_last(N_FF_TILES)))
    t_up = ff_by_chip(t_up)

    (d_r, d_a, d_b, dp_c, dy_hg, dy_at, acc_mg), (f_up,) = _merge_bwd(
        dx1, r, a, b, p_c, w_bh, w_ba, w_o, gate1, (_sibling_comm([t_up]), first_last(N_LAT_TILES)))
    c_down, c_gate, c_up = [_pair_sum(t, f, "pair_sum_" + nm) for t, f, nm in
                            ((t_down, f_down, "down"), (t_gate, f_gate, "gate"), (t_up, f_up, "up"))]
    t_out = _mm_tn(mixed, d_r, tk=512, nk=4, tm=512, tn=1024, out_dtype=BF16, name="grad_out")
    t_bh = _mm_tn(d_a, y_hg, tk=512, nk=4, tm=512, tn=512, out_dtype=BF16, name="grad_bh")
    t_ba = _mm_tn(d_b, y_at, tk=512, nk=4, tm=512, tn=512, out_dtype=BF16, name="grad_ba")
    t_bh, t_ba, t_out = [by_chip(t.reshape(N_DEV, D // N_DEV, t.shape[1])) for t in (t_bh, t_ba, t_out)]
    (dq, dk_rep, dv_rep, dsink), (r_up,) = _attn_bwd(
        qn, k_rep, v_rep, attn_sinks, y_at, lse, dy_at, (_chip_comm([c_up]), first_last(N_BLOCKS)))
    (dp_b, d_o, acc_prep), (f_bh, f_ba, f_out) = _prep_bwd(
        p_b, o, cos, sin, hgrn_norm_w, qnw_t, knw_t, dy_hg, dq, dk_rep, dv_rep,
        (_sibling_comm([t_bh, t_ba, t_out]), first_last(N_TILES)))
    c_bh, c_ba, c_out = [_pair_sum(t, f, "pair_sum_" + nm) for t, f, nm in
                         ((t_bh, f_bh, "bh"), (t_ba, f_ba, "ba"), (t_out, f_out, "out"))]
    (dp_a, dlb), (r_bh, r_ba, r_out, r_down, r_gate) = _hgrn_bwd(
        p_a, lbl, d_o, st, (_chip_comm([c_bh, c_ba, c_out, c_down, c_gate]), first_last(HG_HEADS)))
    t_a = _mm_tn(dp_a, h_all, tk=768, nk=3, tm=1024, tn=1024, out_dtype=BF16, name="grad_in_a")
    t_b = _mm_tn(dp_b, h_all, tk=768, nk=3, tm=640, tn=1024, out_dtype=BF16, name="grad_in_b")
    t_c = _mm_tn(dp_c, h_all, tk=256, nk=8, b_off=1, tm=1024, tn=1024, out_dtype=BF16, name="grad_in_c")
    t_in = None
    for piece, nm in ((t_a, "a"), (t_b, "b"), (t_c, "c")):
        t_in = _place_row_blocks(piece, *_block_table(nm), t_in, IN_COLS, "order_terms_" + nm)
    t_in = by_chip(t_in.reshape(N_DEV, IN_BLK, D))
    (f_in,) = _run_comm(_sibling_comm([t_in]), "scatter_in_sibling")
    c_in = _pair_sum(t_in, f_in, "pair_sum_in")
    sems, c_in, land, token = _chip_exchange_start(c_in, jnp.zeros(c_in.shape, c_in.dtype))
    (grad_x, acc_in), _ = _input_bwd(dp_a, dp_b, dp_c, w_a, w_b, w_c, ctx2, x2, dx1,
                                     norm_mix_w + token[0, 0], sh_in, sc_in)
    small = _pack_small(acc_in, acc_mg, acc_ffn, acc_prep, dsink, dlb)
    return grad_x, small, [r_bh, r_ba, r_out, r_gate, r_up, r_down], (sems, c_in, land)


def kernel(x, c, ctx, c_ctx, w_ada, b_ada, norm_mix_w, norm_ffn_w, w_in, hgrn_lb_logits, hgrn_norm_w, q_norm_w, k_norm_w, attn_sinks, w_branch_hgrn, w_branch_attn, w_out, w_ffn_gate, w_ffn_up, w_ffn_down, loss_target, m_c_ctx, m_w_ada, m_b_ada, m_norm_mix_w, m_norm_ffn_w, m_w_in, m_hgrn_lb_logits, m_hgrn_norm_w, m_q_norm_w, m_k_norm_w, m_attn_sinks, m_w_branch_hgrn, m_w_branch_attn, m_w_out, m_w_ffn_gate, m_w_ffn_up, m_w_ffn_down, v_c_ctx, v_w_ada, v_b_ada, v_norm_mix_w, v_norm_ffn_w, v_w_in, v_hgrn_lb_logits, v_hgrn_norm_w, v_q_norm_w, v_k_norm_w, v_attn_sinks, v_w_branch_hgrn, v_w_branch_attn, v_w_out, v_w_ffn_gate, v_w_ffn_up, v_w_ffn_down):
    me = 4 * lax.axis_index("x") + 2 * lax.axis_index("y") + lax.axis_index("c")
    x2, ctx2, tgt = x[0], ctx[0], loss_target[0]
    w_ada2, w_in2 = w_ada[0], w_in[0]

    cond = jnp.zeros((8, D), F32).at[0].set(c[0]).at[1, :256].set(hgrn_lb_logits.reshape(256))
    b_cols = lax.dynamic_slice(b_ada, (0, me * ADA_BLK), (1, ADA_BLK))
    g0, cc, g1, g_in = _prologue(cond, c_ctx.reshape(1, D), w_ada2, b_cols, w_in2.T.astype(BF16))
    lbl = jnp.transpose(g0[:, 1, :256].reshape(N_DEV, 2, 2, 64), (1, 2, 0, 3)).reshape(2, 2, HGW)
    mod_all = _cols_from_blocks(g1)
    mod = lax.dynamic_slice(mod_all, (me, 0), (1, 6 * D)).reshape(6, D)
    mod_c = mod_all[8].reshape(6, D)
    sh1, sc1, gate1, sh2, sc2, gate2 = [mod[k:k + 1] for k in range(6)]
    sh_in = jnp.concatenate([mod_c[0:1], sh1], axis=0)
    sc_in = jnp.concatenate([mod_c[1:2], sc1], axis=0)

    shards = [w_branch_hgrn[0].T, w_branch_attn[0].T, w_out[0], w_ffn_gate[0].T, w_ffn_up[0].T, w_ffn_down[0]]
    w_in_t = g_in.reshape(IN_COLS, D)
    w_a, w_b, w_c = [_pick_row_blocks(w_in_t, *_block_table(nm), "order_w_" + nm) for nm in "abc"]

    grad_x, small, (r_bh, r_ba, r_out, r_gate, r_up, r_down), pending_in = _local_step(
        x2, ctx2, tgt, lbl, sh_in, sc_in, gate1, sh2, sc2, gate2, norm_mix_w, norm_ffn_w, hgrn_norm_w,
        q_norm_w, k_norm_w, attn_sinks, w_a, w_b, w_c, *[s.astype(BF16) for s in shards])

    big = {}
    for nm, rr, ww, mm, vv, tr, transposed in (
            ("w_branch_hgrn", r_bh, w_branch_hgrn[0], m_w_branch_hgrn[0], v_w_branch_hgrn[0], 128, True),
            ("w_branch_attn", r_ba, w_branch_attn[0], m_w_branch_attn[0], v_w_branch_attn[0], 128, True),
            ("w_out", r_out, w_out[0], m_w_out[0], v_w_out[0], 128, False),
            ("w_ffn_gate", r_gate, w_ffn_gate[0], m_w_ffn_gate[0], v_w_ffn_gate[0], 352, True),
            ("w_ffn_up", r_up, w_ffn_up[0], m_w_ffn_up[0], v_w_ffn_up[0], 352, True),
            ("w_ffn_down", r_down, w_ffn_down[0], m_w_ffn_down[0], v_w_ffn_down[0], 352, False)):
        if transposed:
            res = _adamw_sharded(rr, ww.T, mm.T, vv.T, "adamw_" + nm, tr)
            big[nm] = [t.T[None] for t in res]
        else:
            big[nm] = [t[None] for t in _adamw_sharded(rr, ww, mm, vv, "adamw_" + nm, tr)]

    (g2,) = _all_gather([small], "gather_small", True)
    tot = _sum_small(g2)
    dm = jnp.zeros((16, 6 * D), F32).at[:8].set(g2[:, R_DMOD:R_DMOD + 6, :].reshape(N_DEV, 6 * D))
    dm = dm.at[8, :2 * D].set(tot[R_DCTX:R_DCTX + 2].reshape(2 * D))
    dm_cols = lax.dynamic_slice(dm, (0, me * ADA_BLK), (16, ADA_BLK))
    g_w_ada, dsc_term = _ada_grads(cc, dm_cols, w_ada2)
    (g3,) = _all_gather([dsc_term], "gather_cctx", True)
    g_c_ctx = _c_ctx_grad(g3, c_ctx.reshape(1, D))
    g_lbl = _lb_grads(tot[R_DLB:R_DLB + 2, :HGW], lbl)
    g_lb_mine = lax.dynamic_slice(g_lbl, (0, 0, me * 64), (2, 2, 64))
    misc = tot[R_MISC]
    loss = misc[M_LOSS]

    rep_out = _adamw_replicated(
        tot, g_c_ctx,
        [b_ada, c_ctx.reshape(1, D), norm_mix_w, norm_ffn_w, hgrn_norm_w, q_norm_w, k_norm_w, attn_sinks],
        [m_b_ada, m_c_ctx.reshape(1, D), m_norm_mix_w, m_norm_ffn_w, m_hgrn_norm_w, m_q_norm_w, m_k_norm_w,
         m_attn_sinks],
        [v_b_ada, v_c_ctx.reshape(1, D), v_norm_mix_w, v_norm_ffn_w, v_hgrn_norm_w, v_q_norm_w, v_k_norm_w,
         v_attn_sinks])
    rep = []
    for kind in range(4):
        vals = dict(zip(_REP_NAMES, rep_out[kind * len(_REP_NAMES):(kind + 1) * len(_REP_NAMES)]))
        vals["c_ctx"] = vals["c_ctx"].reshape(D)
        rep.append(vals)

    sems, c_in, land = pending_in
    d_ada, nm_ada, nv_ada = _adamw_plain(g_w_ada, w_ada2, m_w_ada[0], v_w_ada[0], "adamw_w_ada")
    land = _chip_exchange_wait(sems, c_in, land, d_ada)
    own = lax.dynamic_index_in_dim(c_in, 2 * lax.axis_index("x") + lax.axis_index("y"), 0, keepdims=False)
    big["w_in"] = [t.T[None] for t in _adamw_sharded(land, w_in2.T, m_w_in[0].T, v_w_in[0].T, "adamw_w_in", 336,
                                                     extra=own)]
    ada = [t[None] for t in (g_w_ada, d_ada, nm_ada, nv_ada)]
    lb_w = hgrn_lb_logits.reshape(4, 64)
    d_lb, nm_lb, nv_lb = _adamw_plain(g_lb_mine.reshape(4, 64), lb_w, m_hgrn_lb_logits.reshape(4, 64),
                                      v_hgrn_lb_logits.reshape(4, 64), "adamw_lb")
    lbs = [t.reshape(2, 2, 64) for t in (g_lb_mine, d_lb, nm_lb, nv_lb)]

    names = ['c_ctx', 'w_ada', 'b_ada', 'norm_mix_w', 'norm_ffn_w', 'w_in', 'hgrn_lb_logits', 'hgrn_norm_w',
             'q_norm_w', 'k_norm_w', 'attn_sinks', 'w_branch_hgrn', 'w_branch_attn', 'w_out', 'w_ffn_gate',
             'w_ffn_up', 'w_ffn_down']
    outs = [loss, grad_x[None]]
    for kind in range(4):
        for nm in names:
            if nm == 'w_ada':
                outs.append(ada[kind])
            elif nm == 'hgrn_lb_logits':
                outs.append(lbs[kind])
            elif nm in big:
                outs.append(big[nm][kind])
            else:
                outs.append(rep[kind][nm])
    return tuple(outs)
```

```python
import functools
import math

import numpy as np
import jax
import jax.numpy as jnp
from jax import lax
from jax.experimental import pallas as pl
from jax.experimental.pallas import tpu as pltpu

F32 = jnp.float32
BF16 = jnp.bfloat16

N_DEV = 8
D = 1024
S = 2048
L = 256
T = L + S
TM = 256
N_TILES = T // TM
N_LAT_TILES = S // TM
HG_HEADS = 4
HG_DIM = 128
HGW = 512
CHUNK = 32
N_CHUNKS = T // CHUNK
N_CTX_CHUNKS = L // CHUNK
N_LAT_CHUNKS = S // CHUNK
ATT_HEADS = 8
KV_HEADS = 2
HEAD_DIM = 64
ATW = 512
KVW = 128
BLOCK = 128
N_BLOCKS = S // BLOCK
GRID_W = 64
ROPE_THETA = 10000.0
D_FF = 2816
FF_BLK = D_FF // N_DEV
FF_TILE = 256
N_FF_TILES = D_FF // FF_TILE
IN_COLS = 5376
IN_BLK = IN_COLS // N_DEV
ADA_BLK = 6 * D // N_DEV
EPS = 1e-6
WA, WB, WC = 2048, 1280, 2048

ADAM_LR = 0.001
ADAM_B1 = 0.9
ADAM_B2 = 0.999
ADAM_EPS = 1e-08
ADAM_WD = 0.01
ADAM_STEP = 10

VMEM_LIMIT = 56 * 1024 * 1024
MESH = pl.DeviceIdType.MESH


def _cp(sem=None, vmem=VMEM_LIMIT):
    return pltpu.CompilerParams(dimension_semantics=sem, vmem_limit_bytes=vmem)


def _full(shape):
    n = len(shape)
    return pl.BlockSpec(shape, lambda *_: (0,) * n)


_VMEM_WHOLE = pl.BlockSpec(memory_space=pltpu.VMEM)
_ANY = pl.BlockSpec(memory_space=pl.ANY)


def _sigmoid(v):
    return 1.0 / (1.0 + jnp.exp(-v))


def _dot(a, b):
    return jnp.dot(a, b, preferred_element_type=F32)


def _dot_nt(a, b):
    return lax.dot_general(a, b, (((1,), (1,)), ((), ())), preferred_element_type=F32)


def _dot_tn(a, b):
    return lax.dot_general(a, b, (((0,), (0,)), ((), ())), preferred_element_type=F32)


def _split3(v):
    hi = v.astype(BF16)
    r = v - hi.astype(F32)
    mid = r.astype(BF16)
    lo = (r - mid.astype(F32)).astype(BF16)
    return hi, mid, lo


def _dot_exact_rhs01(v, m01):
    hi, mid, lo = _split3(v)
    return _dot(hi, m01) + _dot(mid, m01) + _dot(lo, m01)


def _split2(v):
    hi = v.astype(BF16)
    return hi, (v - hi.astype(F32)).astype(BF16)


def _dot_lhs01(m01, v):
    hi, lo = _split2(v)
    return _dot(m01, hi) + _dot(m01, lo)


def _dot_f32(a, b, dot=_dot):
    ah, am, al = _split3(a)
    bh, bm, bl = _split3(b)
    return (dot(ah, bh) + (dot(ah, bm) + dot(am, bh))
            + (dot(am, bm) + dot(ah, bl) + dot(al, bh)))


def _my_pos():
    return lax.axis_index("x"), lax.axis_index("y"), lax.axis_index("c")


class _Comm:
    def __init__(self, operands, out_shapes, sems, phases):
        self.operands, self.out_shapes, self.sems, self.phases = operands, out_shapes, sems, phases


def _gather_comm(blocks):
    n = len(blocks)

    def parts(ins, outs, sems):
        send_sems, recv_sems, local_sems = sems
        x, y, c = _my_pos()
        me, sibling = (x, y, c), (x, y, 1 - c)
        chips = [(1 - x, y), (x, 1 - y), (1 - x, 1 - y)]

        def slot(a, px, py, pc):
            return outs[a].at[4 * px + 2 * py + pc]

        def copy(a, k, block, to, src=None):
            return pltpu.make_async_remote_copy(
                src_ref=slot(a, *block) if src is None else src, dst_ref=slot(a, *block),
                send_sem=send_sems.at[a, k], recv_sem=recv_sems.at[a, k],
                device_id=to, device_id_type=MESH)

        mine = [pltpu.make_async_copy(ins[a], slot(a, *me), local_sems.at[a]) for a in range(n)]
        first = []
        for a in range(n):
            first.append(copy(a, 0, me, sibling, src=ins[a]))
            first += [copy(a, 1 + j, me, (*chip, c), src=ins[a]) for j, chip in enumerate(chips)]
        passed = [copy(a, 4 + j, (*chip, c), sibling) for j, chip in enumerate(chips) for a in range(n)]
        return c, me, sibling, chips, copy, mine, first, passed

    def start(ins, outs, sems):
        _, _, _, _, _, mine, first, _ = parts(ins, outs, sems)
        for cp in mine + first:
            cp.start()

    def forward(ins, outs, sems):
        c, me, _, chips, copy, _, _, passed = parts(ins, outs, sems)
        for j, chip in enumerate(chips):
            for a in range(n):
                copy(a, 1 + j, (*chip, c), me).wait_recv()
                passed[j * n + a].start()

    def finish(ins, outs, sems):
        c, me, sibling, chips, copy, mine, first, passed = parts(ins, outs, sems)
        for a in range(n):
            copy(a, 0, sibling, me).wait_recv()
            for j, chip in enumerate(chips):
                copy(a, 4 + j, (*chip, 1 - c), me).wait_recv()
        for cp in first + passed:
            cp.wait_send()
        for cp in mine:
            cp.wait()

    return _Comm(blocks, [jax.ShapeDtypeStruct((N_DEV,) + b.shape, b.dtype) for b in blocks],
                 [pltpu.SemaphoreType.DMA((n, 7)), pltpu.SemaphoreType.DMA((n, 7)), pltpu.SemaphoreType.DMA((n,))],
                 [start, forward, finish])


def _gather_comm_relayed(blocks):
    n = len(blocks)

    def parts(ins, outs, sems):
        send_sems, recv_sems, local_sems = sems
        x, y, c = _my_pos()
        me, sibling = (x, y, c), (x, y, 1 - c)
        x_nbr, y_nbr, diag = (1 - x, y, c), (x, 1 - y, c), (1 - x, 1 - y, c)

        def slot(a, dev, half=None):
            ref = outs[a].at[4 * dev[0] + 2 * dev[1] + dev[2]]
            if half is None:
                return ref
            rows = blocks[a].shape[0] // 2
            return ref.at[pl.ds(half * rows, rows)]

        def copy(a, k, block, to, half=None, src=None):
            return pltpu.make_async_remote_copy(
                src_ref=slot(a, block, half) if src is None else src, dst_ref=slot(a, block, half),
                send_sem=send_sems.at[a, k], recv_sem=recv_sems.at[a, k],
                device_id=to, device_id_type=MESH)

        mine = [pltpu.make_async_copy(ins[a], slot(a, me), local_sems.at[a]) for a in range(n)]
        return me, sibling, x_nbr, y_nbr, diag, copy, mine

    def start(ins, outs, sems):
        me, sibling, x_nbr, y_nbr, _, copy, mine = parts(ins, outs, sems)
        for cp in mine:
            cp.start()
        for a in range(n):
            for k, to in ((1, x_nbr), (2, y_nbr), (0, sibling)):
                copy(a, k, me, to, src=ins[a]).start()

    def forward(ins, outs, sems):
        me, sibling, x_nbr, y_nbr, _, copy, _ = parts(ins, outs, sems)
        for a in range(n):
            copy(a, 1, x_nbr, me).wait_recv()
            copy(a, 3, x_nbr, y_nbr, half=0).start()
            copy(a, 5, x_nbr, sibling).start()
        for a in range(n):
            copy(a, 2, y_nbr, me).wait_recv()
            copy(a, 4, y_nbr, x_nbr, half=1).start()
            copy(a, 6, y_nbr, sibling).start()

    def finish(ins, outs, sems):
        me, sibling, x_nbr, y_nbr, diag, copy, mine = parts(ins, outs, sems)
        sib = lambda dev: (dev[0], dev[1], sibling[2])
        for a in range(n):
            copy(a, 3, diag, me, half=0).wait_recv()
            copy(a, 4, diag, me, half=1).wait_recv()
            copy(a, 7, diag, sibling).start()
        for a in range(n):
            copy(a, 0, sibling, me).wait_recv()
            for k, dev in ((5, x_nbr), (6, y_nbr), (7, diag)):
                copy(a, k, sib(dev), me).wait_recv()
        for a in range(n):
            for k, block, to, half in ((0, me, sibling, None), (1, me, x_nbr, None), (2, me, y_nbr, None),
                                       (3, x_nbr, y_nbr, 0), (4, y_nbr, x_nbr, 1), (5, x_nbr, sibling, None),
                                       (6, y_nbr, sibling, None), (7, diag, sibling, None)):
                copy(a, k, block, to, half=half, src=ins[a] if block is me else None).wait_send()
        for cp in mine:
            cp.wait()

    return _Comm(blocks, [jax.ShapeDtypeStruct((N_DEV,) + b.shape, b.dtype) for b in blocks],
                 [pltpu.SemaphoreType.DMA((n, 8)), pltpu.SemaphoreType.DMA((n, 8)), pltpu.SemaphoreType.DMA((n,))],
                 [start, forward, finish])


_HBM = pl.BlockSpec(memory_space=pltpu.HBM)
_SEM = pl.BlockSpec(memory_space=pltpu.SEMAPHORE)
_SPLIT_COPY = pltpu.CompilerParams(has_side_effects=pltpu.SideEffectType.DATAFLOW_SIDE_EFFECTING)


def _chip_exchange_copies(src_ref, land_ref, sems):
    x, y, c = _my_pos()
    q_me = 2 * x + y
    pairs = []
    for j, (px, py) in enumerate([(1 - x, y), (x, 1 - y), (1 - x, 1 - y)]):
        q = 2 * px + py
        send = pltpu.make_async_remote_copy(
            src_ref=src_ref.at[q], dst_ref=land_ref.at[q_me], send_sem=sems[j], recv_sem=sems[3 + j],
            device_id=(px, py, c), device_id_type=MESH)
        recv = pltpu.make_async_remote_copy(
            src_ref=src_ref.at[q], dst_ref=land_ref.at[q], send_sem=sems[j], recv_sem=sems[3 + j],
            device_id=(x, y, c), device_id_type=MESH)
        pairs.append((send, recv))
    return pairs


def _chip_exchange_start(src, land):
    def body(src_ref, land_ref, *outs):
        sems, token = outs[:6], outs[8]
        for send, _ in _chip_exchange_copies(src_ref, land_ref, sems):
            send.start()
        token[...] = jnp.zeros_like(token)

    res = pl.pallas_call(
        body, name="scatter_in_start",
        out_shape=(pltpu.SemaphoreType.DMA(()),) * 6 + (
            pltpu.HBM(src.shape, src.dtype), pltpu.HBM(land.shape, land.dtype),
            jax.ShapeDtypeStruct((8, 128), F32)),
        in_specs=(_HBM, _HBM), out_specs=(_SEM,) * 6 + (_HBM, _HBM, pl.BlockSpec(memory_space=pltpu.VMEM)),
        input_output_aliases={0: 6, 1: 7}, compiler_params=_SPLIT_COPY,
    )(pltpu.with_memory_space_constraint(src, pltpu.HBM), pltpu.with_memory_space_constraint(land, pltpu.HBM))
    return res[:6], res[6], res[7], res[8]


def _chip_exchange_wait(sems, src_thru, land_thru, after):
    def body(src_ref, land_ref, *rest):
        for send, recv in _chip_exchange_copies(src_ref, land_ref, rest[:6]):
            send.wait_send()
            recv.wait_recv()

    return pl.pallas_call(
        body, name="scatter_in_wait",
        out_shape=(pltpu.HBM(src_thru.shape, src_thru.dtype), pltpu.HBM(land_thru.shape, land_thru.dtype)),
        in_specs=(_HBM, _HBM) + (_SEM,) * 6 + (_ANY,), out_specs=(_HBM, _HBM),
        input_output_aliases={0: 0, 1: 1}, compiler_params=_SPLIT_COPY,
    )(src_thru, land_thru, *sems, after)[1]


def _run_comm(comm, name, in_vmem=False):
    n_in, n_out = len(comm.operands), len(comm.out_shapes)

    def body(*refs):
        ins, outs, sems = refs[:n_in], refs[n_in:n_in + n_out], refs[n_in + n_out:]
        for phase in comm.phases:
            phase(ins, outs, sems)

    spec = _VMEM_WHOLE if in_vmem else _ANY
    return pl.pallas_call(
        body, name=name, out_shape=comm.out_shapes, in_specs=[spec] * n_in, out_specs=[spec] * n_out,
        scratch_shapes=comm.sems,
    )(*comm.operands)


def _carrier_call(body, comm, schedule, *, name, grid, in_specs, out_specs, out_shape, scratch_shapes, operands):
    n_in, n_out, n_scr = len(in_specs), len(out_specs), len(scratch_shapes)
    c_in, c_out = len(comm.operands), len(comm.out_shapes)

    def full_body(*refs):
        ins, refs = refs[:n_in], refs[n_in:]
        cins, refs = refs[:c_in], refs[c_in:]
        outs, refs = refs[:n_out], refs[n_out:]
        couts, refs = refs[:c_out], refs[c_out:]
        scr, csems = refs[:n_scr], refs[n_scr:]
        step = pl.program_id(0)

        def run(before):
            for (at, when_before), phase in zip(schedule, comm.phases):
                if when_before == before:
                    pl.when(step == at)(functools.partial(phase, cins, couts, csems))

        run(True)
        body(*ins, *outs, *scr)
        run(False)

    res = pl.pallas_call(
        full_body, name=name, grid=grid,
        in_specs=list(in_specs) + [_ANY] * c_in, out_specs=list(out_specs) + [_ANY] * c_out,
        out_shape=list(out_shape) + list(comm.out_shapes),
        scratch_shapes=list(scratch_shapes) + list(comm.sems),
        compiler_params=_cp(("arbitrary",)),
    )(*operands, *comm.operands)
    return res[:n_out], res[n_out:]


def _pcall(body, carried, *, name, grid, in_specs, out_specs, out_shape, scratch_shapes, operands):
    if carried is None:
        res = pl.pallas_call(body, name=name, grid=grid, in_specs=in_specs, out_specs=out_specs,
                             out_shape=out_shape, scratch_shapes=scratch_shapes,
                             compiler_params=_cp(("arbitrary",)))(*operands)
        return res, ()
    return _carrier_call(body, carried[0], carried[1], name=name, grid=grid, in_specs=in_specs,
                         out_specs=out_specs, out_shape=out_shape, scratch_shapes=scratch_shapes,
                         operands=operands)


def _all_gather(blocks, name, in_vmem):
    return _run_comm(_gather_comm(blocks), name, in_vmem)


N_CHIPS = 4


def _sibling_comm(contribs):
    n = len(contribs)

    def copies(ins, outs, sems):
        send_sems, recv_sems = sems
        x, y, c = _my_pos()
        return [pltpu.make_async_remote_copy(
            src_ref=ins[a].at[pl.ds(0, N_CHIPS), 1 - c], dst_ref=outs[a],
            send_sem=send_sems.at[a], recv_sem=recv_sems.at[a],
            device_id=(x, y, 1 - c), device_id_type=MESH) for a in range(n)]

    def start(ins, outs, sems):
        for cp in copies(ins, outs, sems):
            cp.start()

    def finish(ins, outs, sems):
        cps = copies(ins, outs, sems)
        for cp in cps:
            cp.wait_recv()
        for cp in cps:
            cp.wait_send()

    return _Comm(contribs, [jax.ShapeDtypeStruct((N_CHIPS,) + b.shape[2:], b.dtype) for b in contribs],
                 [pltpu.SemaphoreType.DMA((n,)), pltpu.SemaphoreType.DMA((n,))], [start, finish])


def _pair_sum(mine, theirs, name):
    _, _, rows, cols = mine.shape
    core = lax.axis_index("c").astype(jnp.int32).reshape(1)

    def body(c_ref, m_ref, t_ref, o_ref):
        o_ref[...] = (m_ref[...].astype(F32) + t_ref[...].astype(F32)).astype(BF16)

    return pl.pallas_call(
        body, name=name,
        grid_spec=pltpu.PrefetchScalarGridSpec(
            num_scalar_prefetch=1, grid=(N_CHIPS,),
            in_specs=[pl.BlockSpec((None, None, rows, cols), lambda q, c: (q, c[0], 0, 0)),
                      pl.BlockSpec((None, rows, cols), lambda q, c: (q, 0, 0))],
            out_specs=pl.BlockSpec((None, rows, cols), lambda q, c: (q, 0, 0))),
        out_shape=jax.ShapeDtypeStruct((N_CHIPS, rows, cols), BF16),
        compiler_params=_cp(("parallel",)),
    )(core, mine, theirs)


def _chip_comm(sums):
    n = len(sums)

    def parts(ins, outs, sems):
        send_sems, recv_sems, local_sems = sems
        x, y, c = _my_pos()
        q_me = 2 * x + y
        chips = [(1 - x, y), (x, 1 - y), (1 - x, 1 - y)]
        mine = [pltpu.make_async_copy(ins[a].at[q_me], outs[a].at[q_me], local_sems.at[a]) for a in range(n)]
        sends, recvs = [], []
        for j, (px, py) in enumerate(chips):
            for a in range(n):
                q = 2 * px + py
                sends.append(pltpu.make_async_remote_copy(
                    src_ref=ins[a].at[q], dst_ref=outs[a].at[q_me],
                    send_sem=send_sems.at[a, j], recv_sem=recv_sems.at[a, j],
                    device_id=(px, py, c), device_id_type=MESH))
                recvs.append(pltpu.make_async_remote_copy(
                    src_ref=ins[a].at[q], dst_ref=outs[a].at[q],
                    send_sem=send_sems.at[a, j], recv_sem=recv_sems.at[a, j],
                    device_id=(x, y, c), device_id_type=MESH))
        return mine, sends, recvs

    def start(ins, outs, sems):
        mine, sends, _ = parts(ins, outs, sems)
        for cp in mine + sends:
            cp.start()

    def finish(ins, outs, sems):
        mine, sends, recvs = parts(ins, outs, sems)
        for cp in recvs:
            cp.wait_recv()
        for cp in sends:
            cp.wait_send()
        for cp in mine:
            cp.wait()

    return _Comm(sums, [jax.ShapeDtypeStruct(b.shape, b.dtype) for b in sums],
                 [pltpu.SemaphoreType.DMA((n, 3)), pltpu.SemaphoreType.DMA((n, 3)), pltpu.SemaphoreType.DMA((n,))],
                 [start, finish])


def _mm_nt(a, bt, *, tm, tn, out_dtype, name, row_off=0, rows=None):
    rows = a.shape[0] if rows is None else rows
    n, k = bt.shape

    def body(a_ref, b_ref, o_ref):
        o_ref[...] = _dot_nt(a_ref[...], b_ref[...]).astype(out_dtype)

    return pl.pallas_call(
        body, name=name, grid=(rows // tm, n // tn),
        in_specs=[pl.BlockSpec((tm, k), lambda i, j: (i + row_off, 0)),
                  pl.BlockSpec((tn, k), lambda i, j: (j, 0))],
        out_specs=pl.BlockSpec((tm, tn), lambda i, j: (i, j)),
        out_shape=jax.ShapeDtypeStruct((rows, n), out_dtype),
        compiler_params=_cp(("parallel", "parallel")),
    )(a, bt)


def _mm_tn(a, b, *, tk, nk, tm, tn, out_dtype, name, a_off=0, b_off=0):
    m, n = a.shape[1], b.shape[1]

    def body(a_ref, b_ref, o_ref, acc):
        kk = pl.program_id(2)

        @pl.when(kk == 0)
        def _():
            acc[...] = jnp.zeros_like(acc)

        acc[...] += _dot_tn(a_ref[...], b_ref[...])

        @pl.when(kk == nk - 1)
        def _():
            o_ref[...] = acc[...].astype(out_dtype)

    return pl.pallas_call(
        body, name=name, grid=(m // tm, n // tn, nk),
        in_specs=[pl.BlockSpec((tk, tm), lambda i, j, kk: (kk + a_off, i)),
                  pl.BlockSpec((tk, tn), lambda i, j, kk: (kk + b_off, j))],
        out_specs=pl.BlockSpec((tm, tn), lambda i, j, kk: (i, j)),
        out_shape=jax.ShapeDtypeStruct((m, n), out_dtype),
        scratch_shapes=[pltpu.VMEM((tm, tn), F32)],
        compiler_params=_cp(("parallel", "parallel", "arbitrary")),
    )(a, b)


def _mm_tn_blocked(a, b, name, carried=None):
    nb, _, w = a.shape
    n = b.shape[1]

    def body(a_ref, b_ref, o_ref):
        o_ref[...] = _dot_tn(a_ref[...], b_ref[...]).astype(BF16)

    (out,), extra = _pcall(
        body, carried, name=name, grid=(nb,),
        in_specs=[pl.BlockSpec((None, S, w), lambda j: (j, 0, 0)), _full((S, n))],
        out_specs=[pl.BlockSpec((None, w, n), lambda j: (j, 0, 0))],
        out_shape=[jax.ShapeDtypeStruct((nb, w, n), BF16)],
        scratch_shapes=[], operands=[a, b])
    return out, extra


def _prologue(cond, c_ctx, w_ada, b_cols, w_in_t):
    rows_shape = jax.ShapeDtypeStruct((16, ADA_BLK), F32)
    big, g_cond, g_mod = _gather_comm_relayed([w_in_t]), _gather_comm([cond]), _gather_comm([rows_shape])

    def body(cond_ref, cctx_ref, wada_ref, b_ref, win_ref, g0_ref, cc_ref, g1_ref, gin_ref, rows_ref, *sems):
        s_big, s_cond, s_mod = sems[0:3], sems[3:6], sems[6:9]
        big.phases[0]([win_ref], [gin_ref], s_big)
        for phase in g_cond.phases:
            phase([cond_ref], [g0_ref], s_cond)
        cc_ref[...] = jnp.zeros_like(cc_ref)
        for j in range(N_DEV):
            cc_ref[j:j + 1, :] = g0_ref[j, 0:1, :]
        cc_ref[N_DEV:N_DEV + 1, :] = cctx_ref[...]
        cv = cc_ref[...]
        rows_ref[...] = _dot_f32(cv * _sigmoid(cv), wada_ref[...]) + b_ref[...]
        for phase in g_mod.phases:
            phase([rows_ref], [g1_ref], s_mod)
        big.phases[1]([win_ref], [gin_ref], s_big)
        big.phases[2]([win_ref], [gin_ref], s_big)

    return pl.pallas_call(
        body, name="prologue",
        in_specs=[_VMEM_WHOLE] * 4 + [_ANY], out_specs=[_VMEM_WHOLE] * 3 + [_ANY],
        out_shape=[g_cond.out_shapes[0], jax.ShapeDtypeStruct((16, D), F32), g_mod.out_shapes[0],
                   big.out_shapes[0]],
        scratch_shapes=[pltpu.VMEM((16, ADA_BLK), F32)] + big.sems + g_cond.sems + g_mod.sems,
        compiler_params=_cp(),
    )(cond, c_ctx, w_ada, b_cols, w_in_t)


def _ada_grads(cc, dm_cols, w_ada):
    def body(c_ref, dm_ref, w_ref, gw_ref, dsc_ref):
        cv = c_ref[...]
        sc = cv * _sigmoid(cv)
        dm = dm_ref[...]
        gw_ref[...] = _dot_f32(sc, dm, dot=_dot_tn)
        dsc_ref[...] = _dot_f32(dm, w_ref[...], dot=_dot_nt)

    return pl.pallas_call(
        body, name="ada_grads",
        in_specs=[_VMEM_WHOLE] * 3, out_specs=[_VMEM_WHOLE] * 2,
        out_shape=[jax.ShapeDtypeStruct((D, ADA_BLK), F32), jax.ShapeDtypeStruct((16, D), F32)],
        compiler_params=_cp(),
    )(cc, dm_cols, w_ada)


def _lat(i):
    return jnp.maximum(i - 1, 0)


def _rms_mod(xv, nw, sh, sc):
    rstd = lax.rsqrt(jnp.mean(xv * xv, axis=-1, keepdims=True) + EPS)
    return (xv * rstd * nw) * (1.0 + sc) + sh


def _rms_mod_bwd(xv, nw, sc, dh):
    rstd = lax.rsqrt(jnp.mean(xv * xv, axis=-1, keepdims=True) + EPS)
    xhat = xv * rstd
    dn = dh * (1.0 + sc)
    dxhat = dn * nw
    dx = rstd * (dxhat - xhat * jnp.mean(dxhat * xhat, axis=-1, keepdims=True))
    return (dx, jnp.sum(dh, axis=0, keepdims=True), jnp.sum(dh * (xhat * nw), axis=0, keepdims=True),
            jnp.sum(dn * xhat, axis=0, keepdims=True))


def _norm_mod_all(ctx, x, nw, sh, sc):
    def body(ctx_ref, x_ref, nw_ref, sh_ref, sc_ref, o_ref):
        i = pl.program_id(0)
        sel = jnp.minimum(i, 1)
        xv = jnp.where(i == 0, ctx_ref[...], x_ref[...])
        o_ref[...] = _rms_mod(xv, nw_ref[...], sh_ref[pl.ds(sel, 1), :], sc_ref[pl.ds(sel, 1), :]).astype(BF16)

    return pl.pallas_call(
        body, name="norm_mod", grid=(N_TILES,),
        in_specs=[_full((TM, D)), pl.BlockSpec((TM, D), lambda i: (_lat(i), 0)),
                  _full((1, D)), _full((2, D)), _full((2, D))],
        out_specs=pl.BlockSpec((TM, D), lambda i: (i, 0)),
        out_shape=jax.ShapeDtypeStruct((T, D), BF16),
        compiler_params=_cp(("parallel",)),
    )(ctx, x, nw, sh, sc)


def _chunk_masks(reverse):
    row = lax.broadcasted_iota(jnp.int32, (TM, TM), 0)
    col = lax.broadcasted_iota(jnp.int32, (TM, TM), 1)
    same = (row // CHUNK) == (col // CHUNK)
    tri = same & ((col >= row) if reverse else (col <= row))
    return same, tri


def _chunk_order(i, reverse):
    if not reverse:
        return i
    return jnp.where(i < N_CTX_CHUNKS, N_CTX_CHUNKS - 1 - i, N_CHUNKS + N_CTX_CHUNKS - 1 - i)


def _decay_terms(z, lb, same01, tri01):
    f = lb + (1.0 - lb) * _sigmoid(z)
    g = jnp.log(f)
    g2 = jnp.concatenate(_split2(g), axis=1)
    b2 = _dot(tri01, g2)
    t2 = _dot(same01, g2)
    return f, 1.0 - f, b2[:, :HG_DIM] + b2[:, HG_DIM:], t2[:, :HG_DIM] + t2[:, HG_DIM:]


def _chunk_outer(a, b):
    n = TM // CHUNK
    return jnp.einsum('ncv,nck->nvk', a.reshape(n, CHUNK, HG_DIM), b.reshape(n, CHUNK, HG_DIM),
                      preferred_element_type=F32)


def _hgrn_fwd(p_a, lbl, carried=None):
    cpt = TM // CHUNK

    def body(p_ref, lbl_ref, o_ref, st_ref, qd_s, kd_s, u_s, v_s, ebt_s):
        masks = [_chunk_masks(d == 1) for d in (0, 1)]
        same01 = jnp.where(masks[0][0], 1.0, 0.0).astype(BF16)
        tri = [m[1] for m in masks]
        tri01 = [jnp.where(t, 1.0, 0.0).astype(BF16) for t in tri]
        lb = [_sigmoid(lbl_ref[d][0:1, :] - lbl_ref[d][1:2, :]) for d in (0, 1)]

        def prep(r, carry):
            r0 = pl.multiple_of(r * TM, TM)
            vb = p_ref[pl.ds(r0, TM), 2 * HG_DIM:3 * HG_DIM].astype(BF16)
            v_s[pl.ds(r0, TM), :] = vb
            for d in (0, 1):
                z = p_ref[pl.ds(r0, TM), d * HG_DIM:(d + 1) * HG_DIM]
                _, k, b, bt = _decay_terms(z, lb[d], same01, tri01[d])
                u_s[d, pl.ds(r * cpt, cpt)] = _chunk_outer(vb, (k * jnp.exp(bt - b)).astype(BF16))
                ebt_s[d, pl.ds(r0, TM), :] = jnp.exp(bt)

                @pl.when(r >= 1)
                def _():
                    rl = pl.multiple_of(r0 - L, TM)
                    qr = p_ref[pl.ds(r0, TM), 3 * HG_DIM:4 * HG_DIM]
                    q = qr * _sigmoid(qr) * HG_DIM ** -0.5
                    qd_s[d, pl.ds(rl, TM), :] = (q * jnp.exp(b)).astype(BF16)
                    kd_s[d, pl.ds(rl, TM), :] = (k * jnp.exp(-b)).astype(BF16)

            return carry

        lax.fori_loop(0, N_TILES, prep, 0)

        def scan(i, sts):
            new = []
            for d in (0, 1):
                nn = _chunk_order(i, d == 1)
                c0 = pl.multiple_of(nn * CHUNK, CHUNK)
                st_ref[d, nn] = sts[d].astype(BF16)
                new.append(sts[d] * ebt_s[d, pl.ds(c0, 1), :] + u_s[d, nn])
            return tuple(new)

        zero = jnp.zeros((HG_DIM, HG_DIM), F32)
        lax.fori_loop(0, N_CHUNKS, scan, (zero, zero))

        def outp(r, carry):
            r0 = pl.multiple_of(r * TM, TM)
            vb = v_s[pl.ds(r0 + L, TM), :]
            o = jnp.zeros((TM, HG_DIM), F32)
            for d in (0, 1):
                qd = qd_s[d, pl.ds(r0, TM), :]
                a = jnp.where(tri[d], _dot_nt(qd, kd_s[d, pl.ds(r0, TM), :]), 0.0)
                stb = st_ref[d, pl.ds(N_CTX_CHUNKS + r * cpt, cpt)]
                inter = jnp.einsum('nck,nvk->ncv', qd.reshape(cpt, CHUNK, HG_DIM), stb,
                                   preferred_element_type=F32)
                o = o + _dot(a.astype(BF16), vb) + inter.reshape(TM, HG_DIM)
            o_ref[pl.ds(r0, TM), :] = o
            return carry

        lax.fori_loop(0, N_LAT_TILES, outp, 0)

    return _pcall(
        body, carried, name="hgrn_fwd", grid=(HG_HEADS,),
        in_specs=[pl.BlockSpec((T, 4 * HG_DIM), lambda h: (0, h)),
                  pl.BlockSpec((2, 2, HG_DIM), lambda h: (0, 0, h))],
        out_specs=[pl.BlockSpec((S, HG_DIM), lambda h: (0, h)),
                   pl.BlockSpec((2, None, N_CHUNKS, HG_DIM, HG_DIM), lambda h: (0, h, 0, 0, 0))],
        out_shape=[jax.ShapeDtypeStruct((S, HGW), F32),
                   jax.ShapeDtypeStruct((2, HG_HEADS, N_CHUNKS, HG_DIM, HG_DIM), BF16)],
        scratch_shapes=[pltpu.VMEM((2, S, HG_DIM), BF16), pltpu.VMEM((2, S, HG_DIM), BF16),
                        pltpu.VMEM((2, N_CHUNKS, HG_DIM, HG_DIM), F32), pltpu.VMEM((T, HG_DIM), BF16),
                        pltpu.VMEM((2, T, HG_DIM), F32)],
        operands=[p_a, lbl])


def _hgrn_bwd(p_a, lbl, d_o, st, carried=None):
    cpt = TM // CHUNK

    def rows(r):
        return r * TM if isinstance(r, int) else pl.multiple_of(r * TM, TM)

    def body(p_ref, lbl_ref, do_ref, st_ref, dp_ref, dlb_ref, b_s, bt_s, dbt_s, qd_s, dst_s, w_s):
        masks = [_chunk_masks(d == 1) for d in (0, 1)]
        same01 = jnp.where(masks[0][0], 1.0, 0.0).astype(BF16)
        tri = [m[1] for m in masks]
        tri01 = [jnp.where(t, 1.0, 0.0).astype(BF16) for t in tri]
        later01 = [tri01[1], tri01[0]]
        lb = [_sigmoid(lbl_ref[d][0:1, :] - lbl_ref[d][1:2, :]) for d in (0, 1)]

        def prep_tile(r, latent):
            r0 = rows(r)
            for d in (0, 1):
                z = p_ref[pl.ds(r0, TM), d * HG_DIM:(d + 1) * HG_DIM]
                _, _, b, bt = _decay_terms(z, lb[d], same01, tri01[d])
                b_s[d, pl.ds(r0, TM), :] = b
                bt_s[d, pl.ds(r0, TM), :] = bt
                if latent:
                    rl = pl.multiple_of(r0 - L, TM)
                    qr = p_ref[pl.ds(r0, TM), 3 * HG_DIM:4 * HG_DIM]
                    qd = (qr * _sigmoid(qr) * HG_DIM ** -0.5 * jnp.exp(b)).astype(BF16)
                    qd_s[d, pl.ds(rl, TM), :] = qd
                    w_s[d, pl.ds(r * cpt, cpt)] = _chunk_outer(
                        do_ref[pl.ds(rl, TM), :].astype(BF16), qd).astype(BF16)

        prep_tile(0, False)
        w_s[:, pl.ds(0, N_CTX_CHUNKS)] = jnp.zeros((2, N_CTX_CHUNKS, HG_DIM, HG_DIM), BF16)

        def prep(r, carry):
            prep_tile(r, True)
            return carry

        lax.fori_loop(1, N_TILES, prep, 0)

        def rscan(j, dsts):
            i = N_CHUNKS - 1 - j
            new = []
            for d in (0, 1):
                nn = _chunk_order(i, d == 1)
                c0 = pl.multiple_of(nn * CHUNK, CHUNK)
                dst_s[d, nn] = dsts[d].astype(BF16)
                after = st_ref[d, _chunk_order(jnp.minimum(i + 1, N_CHUNKS - 1), d == 1)].astype(F32)
                dbt_s[d, pl.ds(c0, CHUNK), :] = jnp.broadcast_to(
                    jnp.sum(after * dsts[d], axis=0, keepdims=True), (CHUNK, HG_DIM))
                new.append(dsts[d] * jnp.exp(bt_s[d, pl.ds(c0, 1), :]) + w_s[d, nn].astype(F32))
            return tuple(new)

        zero = jnp.zeros((HG_DIM, HG_DIM), F32)
        lax.fori_loop(0, N_CHUNKS, rscan, (zero, zero))

        def grad_tile(r, latent):
            r0 = rows(r)
            vb = p_ref[pl.ds(r0, TM), 2 * HG_DIM:3 * HG_DIM].astype(BF16)
            dv = jnp.zeros((TM, HG_DIM), F32)
            dq = jnp.zeros((TM, HG_DIM), F32)
            dlbs = []
            if latent:
                rl = pl.multiple_of(r0 - L, TM)
                qr = p_ref[pl.ds(r0, TM), 3 * HG_DIM:4 * HG_DIM]
                sq = _sigmoid(qr)
                do = do_ref[pl.ds(rl, TM), :].astype(BF16)
                da_full = _dot_nt(do, vb)
            for d in (0, 1):
                z = p_ref[pl.ds(r0, TM), d * HG_DIM:(d + 1) * HG_DIM]
                sz = _sigmoid(z)
                f = lb[d] + (1.0 - lb[d]) * sz
                k = 1.0 - f
                b = b_s[d, pl.ds(r0, TM), :]
                e2 = jnp.exp(bt_s[d, pl.ds(r0, TM), :] - b)
                dstb = dst_s[d, pl.ds(r * cpt, cpt)]
                kd2 = k * e2
                dkd2 = jnp.einsum('ncv,nvk->nck', vb.reshape(cpt, CHUNK, HG_DIM), dstb,
                                  preferred_element_type=F32).reshape(TM, HG_DIM)
                dv = dv + jnp.einsum('nck,nvk->ncv', kd2.astype(BF16).reshape(cpt, CHUNK, HG_DIM), dstb,
                                     preferred_element_type=F32).reshape(TM, HG_DIM)
                dk = dkd2 * e2
                db = -(kd2 * dkd2)
                if latent:
                    eb = jnp.exp(b)
                    enb = jnp.exp(-b)
                    qdf = qr * sq * HG_DIM ** -0.5 * eb
                    kdf = k * enb
                    qd = qd_s[d, pl.ds(rl, TM), :]
                    kd = kdf.astype(BF16)
                    a = jnp.where(tri[d], _dot_nt(qd, kd), 0.0).astype(BF16)
                    da = jnp.where(tri[d], da_full, 0.0).astype(BF16)
                    stb = st_ref[d, pl.ds(r * cpt, cpt)]
                    dqd = _dot(da, kd) + jnp.einsum(
                        'ncv,nvk->nck', do.reshape(cpt, CHUNK, HG_DIM), stb,
                        preferred_element_type=F32).reshape(TM, HG_DIM)
                    dkd = _dot_tn(da, qd)
                    dv = dv + _dot_tn(a, do)
                    dk = dk + dkd * enb
                    db = db + qdf * dqd - kdf * dkd
                    dq = dq + dqd * eb
                dg = _dot_lhs01(later01[d], db) + dbt_s[d, pl.ds(r0, TM), :]
                df = dg / f - dk
                dp_ref[pl.ds(r0, TM), d * HG_DIM:(d + 1) * HG_DIM] = (
                    df * (1.0 - lb[d]) * sz * (1.0 - sz)).astype(BF16)
                dlbs.append(jnp.sum(df * (1.0 - sz), axis=0, keepdims=True))
            dp_ref[pl.ds(r0, TM), 2 * HG_DIM:3 * HG_DIM] = dv.astype(BF16)
            if latent:
                dq = dq * (HG_DIM ** -0.5) * (sq * (1.0 + qr * (1.0 - sq)))
            dp_ref[pl.ds(r0, TM), 3 * HG_DIM:4 * HG_DIM] = dq.astype(BF16)
            return dlbs

        dlb_ctx = grad_tile(0, False)

        def grads(r, acc):
            t = grad_tile(r, True)
            return (acc[0] + t[0], acc[1] + t[1])

        dlb = lax.fori_loop(1, N_TILES, grads, (dlb_ctx[0], dlb_ctx[1]))
        dlb_ref[0:1, :] = dlb[0]
        dlb_ref[1:2, :] = dlb[1]

    return _pcall(
        body, carried, name="hgrn_bwd", grid=(HG_HEADS,),
        in_specs=[pl.BlockSpec((T, 4 * HG_DIM), lambda h: (0, h)),
                  pl.BlockSpec((2, 2, HG_DIM), lambda h: (0, 0, h)),
                  pl.BlockSpec((S, HG_DIM), lambda h: (0, h)),
                  pl.BlockSpec((2, None, N_CHUNKS, HG_DIM, HG_DIM), lambda h: (0, h, 0, 0, 0))],
        out_specs=[pl.BlockSpec((T, 4 * HG_DIM), lambda h: (0, h)),
                   pl.BlockSpec((2, HG_DIM), lambda h: (0, h))],
        out_shape=[jax.ShapeDtypeStruct((T, WA), BF16), jax.ShapeDtypeStruct((2, HGW), F32)],
        scratch_shapes=[pltpu.VMEM((2, T, HG_DIM), F32), pltpu.VMEM((2, T, HG_DIM), F32),
                        pltpu.VMEM((2, T, HG_DIM), F32), pltpu.VMEM((2, S, HG_DIM), BF16),
                        pltpu.VMEM((2, N_CHUNKS, HG_DIM, HG_DIM), BF16),
                        pltpu.VMEM((2, N_CHUNKS, HG_DIM, HG_DIM), BF16)],
        operands=[p_a, lbl, d_o, st])


def _rope_tables():
    t = np.arange(S)
    inv = ROPE_THETA ** (-np.arange(0, 32, 2, dtype=np.float64) / 32)
    lane = np.arange(64)
    pos = np.where(lane[None, :] < 32, (t // GRID_W)[:, None], (t % GRID_W)[:, None]).astype(np.float64)
    ang = pos * inv[(lane % 32) % 16][None, :]
    sign = np.where((lane % 32) < 16, -1.0, 1.0)[None, :]
    cos = np.tile(np.cos(ang), (1, 2)).astype(np.float32)
    sin = np.tile(np.sin(ang) * sign, (1, 2)).astype(np.float32)
    return jnp.asarray(cos), jnp.asarray(sin)


def _rope_partner(v):
    lane = lax.broadcasted_iota(jnp.int32, (1, 128), 1)
    first = (lane % 32) < 16
    slabs = []
    for j in range(v.shape[1] // 128):
        s = v[:, 128 * j:128 * (j + 1)]
        slabs.append(jnp.where(first, pltpu.roll(s, 112, 1), pltpu.roll(s, 16, 1)))
    return slabs[0] if len(slabs) == 1 else jnp.concatenate(slabs, axis=1)


def _group_ones(width, group):
    r = lax.broadcasted_iota(jnp.int32, (width, width), 0)
    c = lax.broadcasted_iota(jnp.int32, (width, width), 1)
    return jnp.where((r // group) == (c // group), 1.0, 0.0).astype(BF16)


def _group_mean(v, ones01, group):
    hi = v.astype(BF16)
    lo = (v - hi.astype(F32)).astype(BF16)
    return (_dot(hi, ones01) + _dot(lo, ones01)) * (1.0 / group)


def _rep_matrix():
    r = lax.broadcasted_iota(jnp.int32, (KVW, ATW), 0)
    c = lax.broadcasted_iota(jnp.int32, (KVW, ATW), 1)
    return jnp.where(r == HEAD_DIM * (c // 256) + c % HEAD_DIM, 1.0, 0.0).astype(BF16)


def _tile_lanes(v, reps):
    return jnp.concatenate([v] * reps, axis=1)


def _prep_fwd(p_b, o, cos, sin, hnw, qnw, knw):
    def body(p_ref, o_ref, cos_ref, sin_ref, hnw_ref, qnw_ref, knw_ref, y_ref, q_ref, k_ref, v_ref):
        i = pl.program_id(0)
        rep = _rep_matrix()
        ones_k = _group_ones(KVW, HEAD_DIM)
        kr = p_ref[:, 1024:1152]
        krstd = lax.rsqrt(_group_mean(kr * kr, ones_k, HEAD_DIM) + EPS)
        kn = kr * krstd * knw_ref[...]
        v_ref[...] = _dot(p_ref[:, 1152:1280].astype(BF16), rep).astype(BF16)

        @pl.when(i == 0)
        def _():
            k_ref[...] = _dot(kn.astype(BF16), rep).astype(BF16)

        @pl.when(i > 0)
        def _():
            cs, sn = cos_ref[...], sin_ref[...]
            kro = kn * cs + _rope_partner(kn) * sn
            k_ref[...] = _dot(kro.astype(BF16), rep).astype(BF16)
            qr = p_ref[:, 512:1024]
            qrstd = lax.rsqrt(_group_mean(qr * qr, _group_ones(ATW, HEAD_DIM), HEAD_DIM) + EPS)
            qn = qr * qrstd * qnw_ref[...]
            qro = qn * _tile_lanes(cs, 4) + _rope_partner(qn) * _tile_lanes(sn, 4)
            q_ref[...] = (qro * HEAD_DIM ** -0.5).astype(BF16)
            ys = []
            for h in range(HG_HEADS):
                oh = o_ref[:, HG_DIM * h:HG_DIM * (h + 1)]
                gh = p_ref[:, HG_DIM * h:HG_DIM * (h + 1)]
                rstd = lax.rsqrt(jnp.mean(oh * oh, axis=-1, keepdims=True) + EPS)
                ys.append(oh * rstd * hnw_ref[...] * (gh * _sigmoid(gh)))
            y_ref[...] = jnp.concatenate(ys, axis=1).astype(BF16)

    return pl.pallas_call(
        body, name="prep_fwd", grid=(N_TILES,),
        in_specs=[pl.BlockSpec((TM, WB), lambda i: (i, 0)),
                  pl.BlockSpec((TM, HGW), lambda i: (_lat(i), 0)),
                  pl.BlockSpec((TM, 128), lambda i: (_lat(i), 0)),
                  pl.BlockSpec((TM, 128), lambda i: (_lat(i), 0)),
                  _full((1, HG_DIM)), _full((1, ATW)), _full((1, KVW))],
        out_specs=[pl.BlockSpec((TM, HGW), lambda i: (_lat(i), 0)),
                   pl.BlockSpec((TM, ATW), lambda i: (_lat(i), 0)),
                   pl.BlockSpec((TM, ATW), lambda i: (i, 0)),
                   pl.BlockSpec((TM, ATW), lambda i: (i, 0))],
        out_shape=[jax.ShapeDtypeStruct((S, HGW), BF16), jax.ShapeDtypeStruct((S, ATW), BF16),
                   jax.ShapeDtypeStruct((T, ATW), BF16), jax.ShapeDtypeStruct((T, ATW), BF16)],
        compiler_params=_cp(("arbitrary",)),
    )(p_b, o, cos, sin, hnw, qnw, knw)


def _prep_bwd(p_b, o, cos, sin, hnw, qnw, knw, dy_hg, dq, dk_rep, dv_rep, carried=None):
    def body(p_ref, o_ref, cos_ref, sin_ref, hnw_ref, qnw_ref, knw_ref, dy_ref, dq_ref, dk_ref, dv_ref,
             dp_ref, do_ref, acc_ref):
        i = pl.program_id(0)

        @pl.when(i == 0)
        def _():
            acc_ref[...] = jnp.zeros_like(acc_ref)

        rep = _rep_matrix()
        ones_k = _group_ones(KVW, HEAD_DIM)

        def fold(v):
            hi = v.astype(BF16)
            lo = (v - hi.astype(F32)).astype(BF16)
            return _dot_nt(hi, rep) + _dot_nt(lo, rep)

        kr = p_ref[:, 1024:1152]
        krstd = lax.rsqrt(_group_mean(kr * kr, ones_k, HEAD_DIM) + EPS)
        khat = kr * krstd
        kw = knw_ref[...]
        dkro = fold(dk_ref[...])
        dv = fold(dv_ref[...])

        def k_back(dkn):
            dkhat = dkn * kw
            dkr = krstd * (dkhat - khat * _group_mean(dkhat * khat, ones_k, HEAD_DIM))
            acc_ref[2:3, 0:KVW] += jnp.sum(dkn * khat, axis=0, keepdims=True)
            dp_ref[:, 1024:1152] = dkr.astype(BF16)
            dp_ref[:, 1152:1280] = dv.astype(BF16)

        @pl.when(i == 0)
        def _():
            k_back(dkro)
            dp_ref[:, 0:1024] = jnp.zeros((TM, 1024), BF16)

        @pl.when(i > 0)
        def _():
            cs, sn = cos_ref[...], sin_ref[...]
            k_back(dkro * cs + _rope_partner(dkro * sn))
            ones_q = _group_ones(ATW, HEAD_DIM)
            qr = p_ref[:, 512:1024]
            qrstd = lax.rsqrt(_group_mean(qr * qr, ones_q, HEAD_DIM) + EPS)
            qhat = qr * qrstd
            dqro = dq_ref[...] * HEAD_DIM ** -0.5
            dqn = dqro * _tile_lanes(cs, 4) + _rope_partner(dqro * _tile_lanes(sn, 4))
            dqhat = dqn * qnw_ref[...]
            dqr = qrstd * (dqhat - qhat * _group_mean(dqhat * qhat, ones_q, HEAD_DIM))
            acc_ref[1:2, :] += jnp.sum(dqn * qhat, axis=0, keepdims=True)
            dp_ref[:, 512:1024] = dqr.astype(BF16)
            dws = jnp.zeros((1, HG_DIM), F32)
            for h in range(HG_HEADS):
                sl = slice(HG_DIM * h, HG_DIM * (h + 1))
                oh, gh, dy = o_ref[:, sl], p_ref[:, sl], dy_ref[:, sl]
                rstd = lax.rsqrt(jnp.mean(oh * oh, axis=-1, keepdims=True) + EPS)
                ohat = oh * rstd
                sg = _sigmoid(gh)
                dp_ref[:, sl] = (dy * (ohat * hnw_ref[...]) * (sg * (1.0 + gh * (1.0 - sg)))).astype(BF16)
                dn = dy * (gh * sg)
                dws = dws + jnp.sum(dn * ohat, axis=0, keepdims=True)
                dohat = dn * hnw_ref[...]
                do_ref[:, sl] = rstd * (dohat - ohat * jnp.mean(dohat * ohat, axis=-1, keepdims=True))
            acc_ref[0:1, 0:HG_DIM] += dws

    return _pcall(
        body, carried, name="prep_bwd", grid=(N_TILES,),
        in_specs=[pl.BlockSpec((TM, WB), lambda i: (i, 0)),
                  pl.BlockSpec((TM, HGW), lambda i: (_lat(i), 0)),
                  pl.BlockSpec((TM, 128), lambda i: (_lat(i), 0)),
                  pl.BlockSpec((TM, 128), lambda i: (_lat(i), 0)),
                  _full((1, HG_DIM)), _full((1, ATW)), _full((1, KVW)),
                  pl.BlockSpec((TM, HGW), lambda i: (_lat(i), 0)),
                  pl.BlockSpec((TM, ATW), lambda i: (_lat(i), 0)),
                  pl.BlockSpec((TM, ATW), lambda i: (i, 0)),
                  pl.BlockSpec((TM, ATW), lambda i: (i, 0))],
        out_specs=[pl.BlockSpec((TM, WB), lambda i: (i, 0)),
                   pl.BlockSpec((TM, HGW), lambda i: (_lat(i), 0)),
                   _full((8, ATW))],
        out_shape=[jax.ShapeDtypeStruct((T, WB), BF16), jax.ShapeDtypeStruct((S, HGW), F32),
                   jax.ShapeDtypeStruct((8, ATW), F32)],
        scratch_shapes=[], operands=[p_b, o, cos, sin, hnw, qnw, knw, dy_hg, dq, dk_rep, dv_rep])


NEG = -1e30
_CTX_BLOCKS = L // BLOCK


def _attn_window_specs():
    prev = pl.BlockSpec((BLOCK, ATW), lambda i: (jnp.maximum(i - 1, 0) + _CTX_BLOCKS, 0))
    own = pl.BlockSpec((BLOCK, ATW), lambda i: (i + _CTX_BLOCKS, 0))
    nxt = pl.BlockSpec((BLOCK, ATW), lambda i: (jnp.minimum(i + 1, N_BLOCKS - 1) + _CTX_BLOCKS, 0))
    return [prev, own, nxt, _full((L, ATW))]


def _attn_valid(i, heads, context):
    n_keys = 3 * BLOCK + (L if context else 0)
    qi = lax.broadcasted_iota(jnp.int32, (heads * BLOCK, n_keys), 0) % BLOCK
    kj = lax.broadcasted_iota(jnp.int32, (heads * BLOCK, n_keys), 1)
    window = ((jnp.abs(kj - BLOCK - qi) <= BLOCK) & ((kj >= BLOCK) | (i > 0))
              & ((kj < 2 * BLOCK) | (i < N_BLOCKS - 1)))
    return window | (kj >= 3 * BLOCK)


def _stack_heads(qg):
    lane = lax.broadcasted_iota(jnp.int32, (1, 256), 1) // HEAD_DIM
    return jnp.concatenate([jnp.where(lane == g, qg, jnp.zeros_like(qg)) for g in range(4)], axis=0)


def _unstack_heads(v4):
    lane = lax.broadcasted_iota(jnp.int32, (1, 256), 1) // HEAD_DIM
    out = jnp.where(lane == 0, v4[0:BLOCK], 0.0)
    for g in range(1, 4):
        out = out + jnp.where(lane == g, v4[g * BLOCK:(g + 1) * BLOCK], 0.0)
    return out


def _sink_rows(sink_ref, hk):
    return jnp.concatenate(
        [jnp.broadcast_to(sink_ref[0:1, 4 * hk + g:4 * hk + g + 1], (BLOCK, 1)) for g in range(4)], axis=0)


def _attn_fwd(q, k_rep, v_rep, sinks, carried=None):
    def body(q_ref, kp, ko, kn, kc, vp, vo, vn, vc, sink_ref, y_ref, lse_ref):
        i = pl.program_id(0)
        valid = _attn_valid(i, 1, True)
        lane8 = lax.broadcasted_iota(jnp.int32, (1, ATT_HEADS), 1)
        head_of_lane = lax.broadcasted_iota(jnp.int32, (1, 256), 1) // HEAD_DIM
        lse_out = jnp.zeros((BLOCK, ATT_HEADS), F32)
        for hk in range(KV_HEADS):
            sl = slice(256 * hk, 256 * (hk + 1))
            qg = q_ref[:, sl]
            keys = jnp.concatenate([kp[:, sl], ko[:, sl], kn[:, sl], kc[:, sl]], axis=0)
            vals = jnp.concatenate([vp[:, sl], vo[:, sl], vn[:, sl], vc[:, sl]], axis=0)
            yg = jnp.zeros((BLOCK, 256), F32)
            for g in range(4):
                q1 = jnp.where(head_of_lane == g, qg, jnp.zeros_like(qg))
                s = jnp.where(valid, _dot_nt(q1, keys), NEG)
                sink = sink_ref[0:1, 4 * hk + g:4 * hk + g + 1]
                m = jnp.maximum(jnp.max(s, axis=1, keepdims=True), sink)
                p = jnp.exp(s - m)
                den = jnp.sum(p, axis=1, keepdims=True) + jnp.exp(sink - m)
                o1 = _dot(p.astype(BF16), vals) * (1.0 / den)
                yg = yg + jnp.where(head_of_lane == g, o1, 0.0)
                lse_out = lse_out + jnp.where(lane8 == 4 * hk + g, m + jnp.log(den), 0.0)
            y_ref[:, sl] = yg.astype(BF16)
        lse_ref[...] = lse_out

    return _pcall(
        body, carried, name="attn_fwd", grid=(N_BLOCKS,),
        in_specs=[pl.BlockSpec((BLOCK, ATW), lambda i: (i, 0))] + _attn_window_specs()
        + _attn_window_specs() + [_full((1, ATT_HEADS))],
        out_specs=[pl.BlockSpec((BLOCK, ATW), lambda i: (i, 0)),
                   pl.BlockSpec((BLOCK, ATT_HEADS), lambda i: (i, 0))],
        out_shape=[jax.ShapeDtypeStruct((S, ATW), BF16), jax.ShapeDtypeStruct((S, ATT_HEADS), F32)],
        scratch_shapes=[],
        operands=[q, k_rep, k_rep, k_rep, k_rep, v_rep, v_rep, v_rep, v_rep, sinks])


def _attn_bwd(q, k_rep, v_rep, sinks, y_at, lse, dy, carried=None):
    def body(q_ref, kp, ko, kn, kc, vp, vo, vn, vc, sink_ref, y_ref, lse_ref, dy_ref,
             dq_ref, dk_ref, dv_ref, dsink_ref, dk_acc, dv_acc):
        i = pl.program_id(0)

        @pl.when(i == 0)
        def _():
            dk_acc[...] = jnp.zeros_like(dk_acc)
            dv_acc[...] = jnp.zeros_like(dv_acc)
            dk_ref[pl.ds(0, L), :] = jnp.zeros((L, ATW), F32)
            dv_ref[pl.ds(0, L), :] = jnp.zeros((L, ATW), F32)
            dsink_ref[...] = jnp.zeros_like(dsink_ref)

        valid = _attn_valid(i, 4, False)
        lane8 = lax.broadcasted_iota(jnp.int32, (1, ATT_HEADS), 1)
        w0 = pl.multiple_of(i * BLOCK, BLOCK)
        dsink = jnp.zeros((1, ATT_HEADS), F32)
        for hk in range(KV_HEADS):
            sl = slice(256 * hk, 256 * (hk + 1))
            q4 = _stack_heads(q_ref[:, sl])
            do4f = _stack_heads(dy_ref[:, sl])
            o4 = _stack_heads(y_ref[:, sl]).astype(F32)
            do4 = do4f.astype(BF16)
            kl = jnp.concatenate([kp[:, sl], ko[:, sl], kn[:, sl]], axis=0)
            vl = jnp.concatenate([vp[:, sl], vo[:, sl], vn[:, sl]], axis=0)
            lse4 = jnp.concatenate(
                [jnp.sum(jnp.where(lane8 == 4 * hk + g, lse_ref[...], 0.0), axis=1, keepdims=True)
                 for g in range(4)], axis=0)
            p_loc = jnp.where(valid, jnp.exp(_dot_nt(q4, kl) - lse4), 0.0)
            p_ctx = jnp.exp(_dot_nt(q4, kc[:, sl]) - lse4)
            delta = jnp.sum(do4f * o4, axis=1, keepdims=True)
            ds_loc = (p_loc * (_dot_nt(do4, vl) - delta)).astype(BF16)
            ds_ctx = (p_ctx * (_dot_nt(do4, vc[:, sl]) - delta)).astype(BF16)
            dq_ref[:, sl] = _unstack_heads(_dot(ds_loc, kl) + _dot(ds_ctx, kc[:, sl]))
            dk_acc[pl.ds(w0, 3 * BLOCK), sl] += _dot_tn(ds_loc, q4)
            dv_acc[pl.ds(w0, 3 * BLOCK), sl] += _dot_tn(p_loc.astype(BF16), do4)
            dk_ref[pl.ds(0, L), sl] += _dot_tn(ds_ctx, q4)
            dv_ref[pl.ds(0, L), sl] += _dot_tn(p_ctx.astype(BF16), do4)
            p_sink = jnp.exp(_sink_rows(sink_ref, hk) - lse4)
            for g in range(4):
                rows = slice(g * BLOCK, (g + 1) * BLOCK)
                dsink = dsink + jnp.where(lane8 == 4 * hk + g,
                                          -jnp.sum(p_sink[rows] * delta[rows], axis=0, keepdims=True), 0.0)
        dsink_ref[...] += dsink

        @pl.when(i == N_BLOCKS - 1)
        def _():
            dk_ref[pl.ds(L, S), :] = dk_acc[pl.ds(BLOCK, S), :]
            dv_ref[pl.ds(L, S), :] = dv_acc[pl.ds(BLOCK, S), :]

    row_q = pl.BlockSpec((BLOCK, ATW), lambda i: (i, 0))
    return _pcall(
        body, carried, name="attn_bwd", grid=(N_BLOCKS,),
        in_specs=[row_q] + _attn_window_specs() + _attn_window_specs()
        + [_full((1, ATT_HEADS)), row_q, pl.BlockSpec((BLOCK, ATT_HEADS), lambda i: (i, 0)), row_q],
        out_specs=[row_q, _full((T, ATW)), _full((T, ATW)), _full((1, ATT_HEADS))],
        out_shape=[jax.ShapeDtypeStruct((S, ATW), F32), jax.ShapeDtypeStruct((T, ATW), F32),
                   jax.ShapeDtypeStruct((T, ATW), F32), jax.ShapeDtypeStruct((1, ATT_HEADS), F32)],
        scratch_shapes=[pltpu.VMEM((S + 2 * BLOCK, ATW), F32), pltpu.VMEM((S + 2 * BLOCK, ATW), F32)],
        operands=[q, k_rep, k_rep, k_rep, k_rep, v_rep, v_rep, v_rep, v_rep, sinks, y_at, lse, dy])


def _merge_fwd(y_hg, y_at, p_c, x, w_bh, w_ba, w_out, g1, nfw, sh2, sc2, carried=None):
    def body(yh_ref, ya_ref, g_ref, x_ref, wbh_ref, wba_ref, wo_ref, g1_ref, nfw_ref, sh_ref, sc_ref,
             a_ref, b_ref, mx_ref, r_ref, x1_ref, h2_ref):
        a = _dot_nt(yh_ref[...], wbh_ref[...])
        b = _dot_nt(ya_ref[...], wba_ref[...])
        mixed = (_sigmoid(g_ref[:, :D]) * a + _sigmoid(g_ref[:, D:]) * b).astype(BF16)
        r = _dot(mixed, wo_ref[...])
        x1 = x_ref[...] + g1_ref[...] * r
        a_ref[...] = a
        b_ref[...] = b
        mx_ref[...] = mixed
        r_ref[...] = r
        x1_ref[...] = x1
        h2_ref[...] = _rms_mod(x1, nfw_ref[...], sh_ref[...], sc_ref[...]).astype(BF16)

    row = lambda w: pl.BlockSpec((TM, w), lambda i: (i, 0))
    vec = _full((1, D))
    return _pcall(
        body, carried, name="merge_fwd", grid=(N_LAT_TILES,),
        in_specs=[row(HGW), row(ATW), row(WC), row(D), _VMEM_WHOLE, _VMEM_WHOLE, _VMEM_WHOLE,
                  vec, vec, vec, vec],
        out_specs=[row(D)] * 6,
        out_shape=[jax.ShapeDtypeStruct((S, D), dt) for dt in (F32, F32, BF16, F32, F32, BF16)],
        scratch_shapes=[], operands=[y_hg, y_at, p_c, x, w_bh, w_ba, w_out, g1, nfw, sh2, sc2])


def _merge_bwd(dx1, r, a, b, p_c, w_bh, w_ba, w_out, g1, carried=None):
    def body(dx_ref, r_ref, a_ref, b_ref, g_ref, wbh_ref, wba_ref, wo_ref, g1_ref,
             dr_ref, da_ref, db_ref, dg_ref, dyh_ref, dya_ref, acc_ref):
        @pl.when(pl.program_id(0) == 0)
        def _():
            acc_ref[...] = jnp.zeros_like(acc_ref)

        dx1v = dx_ref[...]
        acc_ref[0:1, :] += jnp.sum(dx1v * r_ref[...], axis=0, keepdims=True)
        dr = (g1_ref[...] * dx1v).astype(BF16)
        dr_ref[...] = dr
        dmix = _dot_nt(dr, wo_ref[...])
        sh, sa = _sigmoid(g_ref[:, :D]), _sigmoid(g_ref[:, D:])
        da = (dmix * sh).astype(BF16)
        db = (dmix * sa).astype(BF16)
        da_ref[...] = da
        db_ref[...] = db
        dg_ref[:, :D] = (dmix * a_ref[...] * sh * (1.0 - sh)).astype(BF16)
        dg_ref[:, D:] = (dmix * b_ref[...] * sa * (1.0 - sa)).astype(BF16)
        dyh_ref[...] = _dot(da, wbh_ref[...])
        dya_ref[...] = _dot(db, wba_ref[...])

    row = lambda w: pl.BlockSpec((TM, w), lambda i: (i, 0))
    return _pcall(
        body, carried, name="merge_bwd", grid=(N_LAT_TILES,),
        in_specs=[row(D), row(D), row(D), row(D), row(WC), _VMEM_WHOLE, _VMEM_WHOLE, _VMEM_WHOLE,
                  _full((1, D))],
        out_specs=[row(D), row(D), row(D), row(WC), row(HGW), row(ATW), _full((8, D))],
        out_shape=[jax.ShapeDtypeStruct((S, D), BF16), jax.ShapeDtypeStruct((S, D), BF16),
                   jax.ShapeDtypeStruct((S, D), BF16), jax.ShapeDtypeStruct((S, WC), BF16),
                   jax.ShapeDtypeStruct((S, HGW), F32), jax.ShapeDtypeStruct((S, ATW), F32),
                   jax.ShapeDtypeStruct((8, D), F32)],
        scratch_shapes=[], operands=[dx1, r, a, b, p_c, w_bh, w_ba, w_out, g1])


def _ffn_fused(x1, h2, tgt, w_gate, w_up, w_down, g2, nfw, sc2):
    def body(x1_ref, h2_ref, t_ref, wg_ref, wu_ref, wd_ref, g2_ref, nfw_ref, sc_ref,
             act_ref, dgt_ref, dup_ref, df_ref, dx_ref, acc_ref, gs, us):
        @pl.when(pl.program_id(0) == 0)
        def _():
            acc_ref[...] = jnp.zeros_like(acc_ref)

        h2 = h2_ref[...]
        whole = lambda w_ref: w_ref[...].reshape(D_FF, D)
        wide = lambda t_ref: jnp.concatenate([t_ref[j] for j in range(N_FF_TILES)], axis=1)
        for j in range(N_FF_TILES):
            g = _dot_nt(h2, wg_ref[j])
            u = _dot_nt(h2, wu_ref[j])
            gs[j] = g
            us[j] = u
            act_ref[j] = (g * _sigmoid(g) * u).astype(BF16)
        f = _dot(wide(act_ref), whole(wd_ref))
        x1v = x1_ref[...]
        g2 = g2_ref[...]
        diff = x1v + g2 * f - t_ref[...]
        dy = diff * (1.0 / D)
        df = (g2 * dy).astype(BF16)
        df_ref[...] = df
        dact_all = _dot_nt(df, whole(wd_ref))
        for j in range(N_FF_TILES):
            g, u = gs[j], us[j]
            sg = _sigmoid(g)
            dact = dact_all[:, j * FF_TILE:(j + 1) * FF_TILE]
            dgt_ref[j] = (dact * u * (sg * (1.0 + g * (1.0 - sg)))).astype(BF16)
            dup_ref[j] = (dact * (g * sg)).astype(BF16)
        dh2 = _dot(wide(dgt_ref), whole(wg_ref)) + _dot(wide(dup_ref), whole(wu_ref))
        dx, dsh, dsc, dnw = _rms_mod_bwd(x1v, nfw_ref[...], sc_ref[...], dh2)
        dx_ref[...] = dy + dx
        acc_ref[0:1, :] += dsh
        acc_ref[1:2, :] += dsc
        acc_ref[2:3, :] += dnw
        acc_ref[3:4, :] += jnp.sum(dy * f, axis=0, keepdims=True)
        acc_ref[4:5, :] += 0.5 * jnp.sum(jnp.sum(diff * diff, axis=1, keepdims=True), axis=0,
                                         keepdims=True) * (1.0 / D)

    row = lambda dt_w: pl.BlockSpec((TM, dt_w), lambda i: (i, 0))
    blk = pl.BlockSpec((N_FF_TILES, TM, FF_TILE), lambda i: (0, i, 0))
    vec = _full((1, D))
    return pl.pallas_call(
        body, name="ffn_fused", grid=(N_LAT_TILES,),
        in_specs=[row(D), row(D), row(D), _VMEM_WHOLE, _VMEM_WHOLE, _VMEM_WHOLE, vec, vec, vec],
        out_specs=[blk, blk, blk, row(D), row(D), _full((8, D))],
        out_shape=[jax.ShapeDtypeStruct((N_FF_TILES, S, FF_TILE), BF16)] * 3
        + [jax.ShapeDtypeStruct((S, D), BF16), jax.ShapeDtypeStruct((S, D), F32),
           jax.ShapeDtypeStruct((8, D), F32)],
        scratch_shapes=[pltpu.VMEM((N_FF_TILES, TM, FF_TILE), F32), pltpu.VMEM((N_FF_TILES, TM, FF_TILE), F32)],
        compiler_params=_cp(("arbitrary",)),
    )(x1, h2, tgt, w_gate, w_up, w_down, g2, nfw, sc2)


def _proj_bc(h_all, w_b, w_c, carried=None):
    def body(h_ref, wb_ref, wc_ref, pb_ref, pc_ref):
        h = h_ref[...]
        pb_ref[...] = _dot_nt(h, wb_ref[...])

        @pl.when(pl.program_id(0) > 0)
        def _():
            pc_ref[...] = _dot_nt(h, wc_ref[...])

    return _pcall(
        body, carried, name="proj_bc", grid=(N_TILES,),
        in_specs=[pl.BlockSpec((TM, D), lambda i: (i, 0)), _VMEM_WHOLE, _VMEM_WHOLE],
        out_specs=[pl.BlockSpec((TM, WB), lambda i: (i, 0)), pl.BlockSpec((TM, WC), lambda i: (_lat(i), 0))],
        out_shape=[jax.ShapeDtypeStruct((T, WB), F32), jax.ShapeDtypeStruct((S, WC), F32)],
        scratch_shapes=[], operands=[h_all, w_b, w_c])


def _input_bwd(dp_a, dp_b, dp_c, w_a, w_b, w_c, ctx, x, dx1, nw, sh, sc, carried=None):
    def body(da_ref, db_ref, dc_ref, wa_ref, wb_ref, wc_ref, ctx_ref, x_ref, dx1_ref, nw_ref, sh_ref,
             sc_ref, gx_ref, acc_ref):
        i = pl.program_id(0)

        @pl.when(i == 0)
        def _():
            acc_ref[...] = jnp.zeros_like(acc_ref)

        dh = _dot(da_ref[...], wa_ref[...]) + _dot(db_ref[...], wb_ref[...])

        @pl.when(i == 0)
        def _():
            _, dsh, dsc, dnw = _rms_mod_bwd(ctx_ref[...], nw_ref[...], sc_ref[0:1, :], dh)
            acc_ref[3:4, :] += dsh
            acc_ref[4:5, :] += dsc
            acc_ref[2:3, :] += dnw

        @pl.when(i > 0)
        def _():
            dhl = dh + _dot(dc_ref[...], wc_ref[...])
            dx, dsh, dsc, dnw = _rms_mod_bwd(x_ref[...], nw_ref[...], sc_ref[1:2, :], dhl)
            gx_ref[...] = dx1_ref[...] + dx
            acc_ref[0:1, :] += dsh
            acc_ref[1:2, :] += dsc
            acc_ref[2:3, :] += dnw

    lat = lambda w: pl.BlockSpec((TM, w), lambda i: (_lat(i), 0))
    return _pcall(
        body, carried, name="input_bwd", grid=(N_TILES,),
        in_specs=[pl.BlockSpec((TM, WA), lambda i: (i, 0)), pl.BlockSpec((TM, WB), lambda i: (i, 0)),
                  lat(WC), _VMEM_WHOLE, _VMEM_WHOLE, _VMEM_WHOLE, _full((TM, D)), lat(D), lat(D),
                  _full((1, D)), _full((2, D)), _full((2, D))],
        out_specs=[lat(D), _full((8, D))],
        out_shape=[jax.ShapeDtypeStruct((S, D), F32), jax.ShapeDtypeStruct((8, D), F32)],
        scratch_shapes=[], operands=[dp_a, dp_b, dp_c, w_a, w_b, w_c, ctx, x, dx1, nw, sh, sc])


_C1 = 1.0 - ADAM_B1 ** ADAM_STEP
_C2 = 1.0 - ADAM_B2 ** ADAM_STEP


def _adamw_math(w, g, m, v):
    m = ADAM_B1 * m + (1.0 - ADAM_B1) * g
    v = ADAM_B2 * v + (1.0 - ADAM_B2) * (g * g)
    m_hat = m / _C1
    v_hat = v / _C2
    delta = -ADAM_LR * (m_hat / (jnp.sqrt(v_hat) + ADAM_EPS) + ADAM_WD * w)
    return delta, m, v


def _adamw_sharded(terms, w, m, v, name, tr, extra=None):
    rows, cols = w.shape

    def body(*refs):
        t_ref, w_ref, m_ref, v_ref = refs[:4]
        g_ref, d_ref, nm_ref, nv_ref, token = refs[-5:]
        g = t_ref[0].astype(F32)
        for s in range(1, N_CHIPS):
            g = g + t_ref[s].astype(F32)
        if extra is not None:
            g = g + refs[4][...].astype(F32)
        g_ref[...] = g
        d_ref[...], nm_ref[...], nv_ref[...] = _adamw_math(w_ref[...], g, m_ref[...], v_ref[...])
        token[...] = jnp.zeros_like(token)

    blk = pl.BlockSpec((tr, cols), lambda i: (i, 0))
    return pl.pallas_call(
        body, name=name, grid=(rows // tr,),
        in_specs=[pl.BlockSpec((N_CHIPS, tr, cols), lambda i: (0, i, 0)), blk, blk, blk]
        + ([blk] if extra is not None else []),
        out_specs=[blk] * 4 + [_full((8, 128))],
        out_shape=[jax.ShapeDtypeStruct((rows, cols), F32)] * 4 + [jax.ShapeDtypeStruct((8, 128), F32)],
        compiler_params=_cp(("arbitrary",)),
    )(terms, w, m, v, *([extra] if extra is not None else []))


def _adamw_plain(g, w, m, v, name):
    def body(g_ref, w_ref, m_ref, v_ref, d_ref, nm_ref, nv_ref):
        d_ref[...], nm_ref[...], nv_ref[...] = _adamw_math(w_ref[...], g_ref[...], m_ref[...], v_ref[...])

    return pl.pallas_call(
        body, name=name, in_specs=[_VMEM_WHOLE] * 4, out_specs=[_VMEM_WHOLE] * 3,
        out_shape=[jax.ShapeDtypeStruct(w.shape, F32)] * 3,
        compiler_params=_cp(),
    )(g, w, m, v)


SMALL_ROWS = 16
R_DMOD, R_DCTX, R_NMIX, R_NFFN, R_MISC, R_DLB, R_BADA01 = 0, 6, 8, 9, 10, 11, 13
M_HNW, M_QNW, M_KNW, M_SINK, M_LOSS = 0, 128, 256, 384, 512


def _pack_small(acc_in, acc_mg, acc_ffn, acc_prep, dsink, dlb):
    def body(in_ref, mg_ref, ff_ref, pp_ref, ds_ref, dlb_ref, o_ref):
        o_ref[...] = jnp.zeros_like(o_ref)
        o_ref[0:2, :] = in_ref[0:2, :]
        o_ref[2:3, :] = mg_ref[0:1, :]
        o_ref[3:5, :] = ff_ref[0:2, :]
        o_ref[5:6, :] = ff_ref[3:4, :]
        o_ref[6:8, :] = in_ref[3:5, :]
        o_ref[8:9, :] = in_ref[2:3, :]
        o_ref[9:10, :] = ff_ref[2:3, :]
        o_ref[10:11, M_HNW:M_HNW + HG_DIM] = pp_ref[0:1, 0:HG_DIM]
        r = lax.broadcasted_iota(jnp.int32, (ATW, 128), 0)
        c = lax.broadcasted_iota(jnp.int32, (ATW, 128), 1)
        fold = jnp.where((r % HEAD_DIM == c) & (c < HEAD_DIM), 1.0, 0.0).astype(BF16)
        qk = jnp.concatenate([pp_ref[1:2, :], pp_ref[2:3, :], jnp.zeros((6, ATW), F32)], axis=0)
        folded = _dot_exact_rhs01(qk, fold)
        o_ref[10:11, M_QNW:M_QNW + 128] = folded[0:1, :]
        o_ref[10:11, M_KNW:M_KNW + 128] = folded[1:2, :]
        o_ref[10:11, M_SINK:M_SINK + ATT_HEADS] = ds_ref[...]
        o_ref[10:11, M_LOSS:M_LOSS + 128] = ff_ref[4:5, 0:128]
        o_ref[11:13, 0:HGW] = dlb_ref[...]

    return pl.pallas_call(
        body, name="pack_small", in_specs=[_VMEM_WHOLE] * 6, out_specs=_VMEM_WHOLE,
        out_shape=jax.ShapeDtypeStruct((SMALL_ROWS, D), F32), compiler_params=_cp(),
    )(acc_in, acc_mg, acc_ffn, acc_prep, dsink, dlb)


def _sum_small(gathered):
    def body(g_ref, o_ref):
        tot = g_ref[0]
        for s in range(1, N_DEV):
            tot = tot + g_ref[s]
        o_ref[...] = tot
        o_ref[R_BADA01:R_BADA01 + 2, :] = tot[0:2, :] + tot[R_DCTX:R_DCTX + 2, :]

    return pl.pallas_call(
        body, name="sum_small", in_specs=[_VMEM_WHOLE], out_specs=_VMEM_WHOLE,
        out_shape=jax.ShapeDtypeStruct((SMALL_ROWS, D), F32), compiler_params=_cp(),
    )(gathered)


_REP_NAMES = ("b_ada", "c_ctx", "norm_mix_w", "norm_ffn_w", "hgrn_norm_w", "q_norm_w", "k_norm_w", "attn_sinks")


def _adamw_replicated(tot, g_c_ctx, ws, ms, vs):
    n = len(_REP_NAMES)

    def body(*refs):
        tot_ref, gc_ref = refs[0], refs[1]
        w_refs, m_refs, v_refs = refs[2:2 + n], refs[2 + n:2 + 2 * n], refs[2 + 2 * n:2 + 3 * n]
        outs = refs[2 + 3 * n:]
        row = lambda r: tot_ref[r:r + 1, :]
        misc = row(R_MISC)
        grads = [jnp.concatenate([row(R_BADA01), row(R_BADA01 + 1)] + [row(k) for k in range(2, 6)], axis=1),
                 gc_ref[...], row(R_NMIX), row(R_NFFN),
                 misc[:, M_HNW:M_HNW + HG_DIM], misc[:, M_QNW:M_QNW + HEAD_DIM],
                 misc[:, M_KNW:M_KNW + HEAD_DIM], misc[:, M_SINK:M_SINK + ATT_HEADS]]
        for k in range(n):
            outs[k][...] = grads[k]
            outs[n + k][...], outs[2 * n + k][...], outs[3 * n + k][...] = _adamw_math(
                w_refs[k][...], grads[k], m_refs[k][...], v_refs[k][...])

    shapes = [jax.ShapeDtypeStruct(w.shape, F32) for w in ws]
    return pl.pallas_call(
        body, name="adamw_replicated", in_specs=[_VMEM_WHOLE] * (2 + 3 * n), out_specs=[_VMEM_WHOLE] * (4 * n),
        out_shape=shapes * 4, compiler_params=_cp(),
    )(tot, g_c_ctx, *ws, *ms, *vs)


def _lb_grads(dlb, lbl):
    def body(d_ref, l_ref, o_ref):
        for d in (0, 1):
            ll = l_ref[d]
            lb = _sigmoid(ll[0:1, :] - ll[1:2, :])
            t = d_ref[d:d + 1, :] * lb * (1.0 - lb)
            o_ref[d, 0:1, :] = t
            o_ref[d, 1:2, :] = -t

    return pl.pallas_call(
        body, name="lb_grads", in_specs=[_VMEM_WHOLE] * 2, out_specs=_VMEM_WHOLE,
        out_shape=jax.ShapeDtypeStruct((2, 2, HGW), F32), compiler_params=_cp(),
    )(dlb, lbl)


def _c_ctx_grad(terms, c_ctx):
    def body(t_ref, c_ref, o_ref):
        tot = t_ref[0, 8:9, :]
        for s in range(1, N_DEV):
            tot = tot + t_ref[s, 8:9, :]
        cv = c_ref[...]
        sg = _sigmoid(cv)
        o_ref[...] = tot * (sg * (1.0 + cv * (1.0 - sg)))

    return pl.pallas_call(
        body, name="c_ctx_grad", in_specs=[_VMEM_WHOLE] * 2, out_specs=_VMEM_WHOLE,
        out_shape=jax.ShapeDtypeStruct((1, D), F32), compiler_params=_cp(),
    )(terms, c_ctx)


def _in_perm():
    fz, bz, inp, kk, vv, qhg, ghg, qat, gates = 0, 512, 1024, 1536, 1664, 1792, 2304, 2816, 3328
    cols = []
    for h in range(HG_HEADS):
        for base in (fz, bz, inp, qhg):
            cols += list(range(base + 128 * h, base + 128 * (h + 1)))
    cols += list(range(ghg, ghg + 512)) + list(range(qat, qat + 512))
    cols += list(range(kk, kk + 128)) + list(range(vv, vv + 128))
    cols += list(range(gates, gates + 2048))
    return np.asarray(cols, np.int32)


_PERM = _in_perm()
_INV_PERM = np.argsort(_PERM).astype(np.int32)


_PIECES = {"a": (0, WA, 128), "b": (WA, WB, 256), "c": (WA + WB, WC, 256)}


def _block_table(piece):
    lo, n, blk = _PIECES[piece]
    starts = [int(_PERM[r]) for r in range(lo, lo + n, blk)]
    assert all(s % blk == 0 and np.array_equal(_PERM[r:r + blk], np.arange(s, s + blk))
               for s, r in zip(starts, range(lo, lo + n, blk)))
    return jnp.asarray([s // blk for s in starts], jnp.int32), blk


def _pick_row_blocks(x, table, blk, name):
    cols = x.shape[1]

    def body(t_ref, x_ref, o_ref):
        o_ref[...] = x_ref[...]

    return pl.pallas_call(
        body, name=name,
        grid_spec=pltpu.PrefetchScalarGridSpec(
            num_scalar_prefetch=1, grid=(table.shape[0],),
            in_specs=[pl.BlockSpec((blk, cols), lambda i, t: (t[i], 0))],
            out_specs=pl.BlockSpec((blk, cols), lambda i, t: (i, 0))),
        out_shape=jax.ShapeDtypeStruct((table.shape[0] * blk, cols), x.dtype),
        compiler_params=_cp(("arbitrary",)),
    )(table, x)


def _place_row_blocks(x, table, blk, into, out_rows, name):
    cols = x.shape[1]

    def body(t_ref, x_ref, *rest):
        rest[-1][...] = x_ref[...]

    operands, in_specs, aliases = [table, x], [pl.BlockSpec((blk, cols), lambda i, t: (i, 0))], {}
    if into is not None:
        operands.append(into)
        in_specs.append(_ANY)
        aliases = {2: 0}
    return pl.pallas_call(
        body, name=name,
        grid_spec=pltpu.PrefetchScalarGridSpec(
            num_scalar_prefetch=1, grid=(table.shape[0],), in_specs=in_specs,
            out_specs=pl.BlockSpec((blk, cols), lambda i, t: (t[i], 0))),
        out_shape=jax.ShapeDtypeStruct((out_rows, cols), x.dtype),
        input_output_aliases=aliases,
        compiler_params=_cp(("arbitrary",)),
    )(*operands)


def _cols_from_blocks(g):
    return jnp.transpose(g, (1, 0, 2)).reshape(g.shape[1], N_DEV * g.shape[2])


def _local_step(x2, ctx2, tgt, lbl, sh_in, sc_in, gate1, sh2, sc2, gate2, norm_mix_w, norm_ffn_w,
                hgrn_norm_w, q_norm_w, k_norm_w, attn_sinks, w_a, w_b, w_c, s_bh, s_ba, s_out,
                s_gate, s_up, s_down):
    first_last = lambda n: [(0, True), (n - 1, False)]
    h_all = _norm_mod_all(ctx2, x2, norm_mix_w, sh_in, sc_in)
    p_a = _mm_nt(h_all, w_a, tm=768, tn=1024, out_dtype=F32, name="proj_a")
    (o, st), (g_gate, g_bh, g_ba) = _hgrn_fwd(
        p_a, lbl, (_gather_comm_relayed([s_gate, s_bh, s_ba]),
                   [(0, True), (HG_HEADS - 2, True), (HG_HEADS - 1, False)]))
    (p_b, p_c), (g_out,) = _proj_bc(
        h_all, w_b, w_c, (_gather_comm_relayed([s_out]), [(0, True), (N_TILES - 4, True), (N_TILES - 1, False)]))
    cos, sin = _rope_tables()
    qnw_t, knw_t = jnp.tile(q_norm_w, (1, ATT_HEADS)), jnp.tile(k_norm_w, (1, KV_HEADS))
    y_hg, qn, k_rep, v_rep = _prep_fwd(p_b, o, cos, sin, hgrn_norm_w, qnw_t, knw_t)
    (y_at, lse), (g_up, g_down) = _attn_fwd(
        qn, k_rep, v_rep, attn_sinks,
        (_gather_comm_relayed([s_up, s_down]), [(0, True), (N_BLOCKS - 6, True), (N_BLOCKS - 1, False)]))
    w_bh, w_ba, w_o = g_bh.reshape(D, HGW), g_ba.reshape(D, ATW), g_out.reshape(D, D)
    (a, b, mixed, r, x1, h2), _ = _merge_fwd(
        y_hg, y_at, p_c, x2, w_bh, w_ba, w_o, gate1, norm_ffn_w, sh2, sc2)
    g_gate, g_up, g_down = [g.reshape(N_FF_TILES, FF_TILE, D) for g in (g_gate, g_up, g_down)]

    act, d_gate, d_up, d_f, dx1, acc_ffn = _ffn_fused(x1, h2, tgt, g_gate, g_up, g_down, gate2,
                                                      norm_ffn_w, sc2)
    by_chip = lambda t: t.reshape((N_CHIPS, 2) + t.shape[1:])
    ff_by_chip = lambda t: t.reshape(N_CHIPS, 2, FF_BLK, D)
    t_down, _ = _mm_tn_blocked(act, d_f, "grad_down")
    t_down = ff_by_chip(t_down)
    t_gate, (f_down,) = _mm_tn_blocked(d_gate, h2, "grad_gate", (_sibling_comm([t_down]), first_last(N_FF_TILES)))
    t_gate = ff_by_chip(t_gate)
    t_up, (f_gate,) = _mm_tn_blocked(d_up, h2, "grad_up", (_sibling_comm([t_gate]), first_last(N_FF_TILES)))
    t_up = ff_by_chip(t_up)

    (d_r, d_a, d_b, dp_c, dy_hg, dy_at, acc_mg), (f_up,) = _merge_bwd(
        dx1, r, a, b, p_c, w_bh, w_ba, w_o, gate1, (_sibling_comm([t_up]), first_last(N_LAT_TILES)))
    c_down, c_gate, c_up = [_pair_sum(t, f, "pair_sum_" + nm) for t, f, nm in
                            ((t_down, f_down, "down"), (t_gate, f_gate, "gate"), (t_up, f_up, "up"))]
    t_out = _mm_tn(mixed, d_r, tk=512, nk=4, tm=512, tn=1024, out_dtype=BF16, name="grad_out")
    t_bh = _mm_tn(d_a, y_hg, tk=512, nk=4, tm=512, tn=512, out_dtype=BF16, name="grad_bh")
    t_ba = _mm_tn(d_b, y_at, tk=512, nk=4, tm=512, tn=512, out_dtype=BF16, name="grad_ba")
    t_bh, t_ba, t_out = [by_chip(t.reshape(N_DEV, D // N_DEV, t.shape[1])) for t in (t_bh, t_ba, t_out)]
    (dq, dk_rep, dv_rep, dsink), (r_up,) = _attn_bwd(
        qn, k_rep, v_rep, attn_sinks, y_at, lse, dy_at, (_chip_comm([c_up]), first_last(N_BLOCKS)))
    (dp_b, d_o, acc_prep), (f_bh, f_ba, f_out) = _prep_bwd(
        p_b, o, cos, sin, hgrn_norm_w, qnw_t, knw_t, dy_hg, dq, dk_rep, dv_rep,
        (_sibling_comm([t_bh, t_ba, t_out]), first_last(N_TILES)))
    c_bh, c_ba, c_out = [_pair_sum(t, f, "pair_sum_" + nm) for t, f, nm in
                         ((t_bh, f_bh, "bh"), (t_ba, f_ba, "ba"), (t_out, f_out, "out"))]
    (dp_a, dlb), (r_bh, r_ba, r_out, r_down, r_gate) = _hgrn_bwd(
        p_a, lbl, d_o, st, (_chip_comm([c_bh, c_ba, c_out, c_down, c_gate]), first_last(HG_HEADS)))
    t_a = _mm_tn(dp_a, h_all, tk=768, nk=3, tm=1024, tn=1024, out_dtype=BF16, name="grad_in_a")
    t_b = _mm_tn(dp_b, h_all, tk=768, nk=3, tm=640, tn=1024, out_dtype=BF16, name="grad_in_b")
    t_c = _mm_tn(dp_c, h_all, tk=256, nk=8, b_off=1, tm=1024, tn=1024, out_dtype=BF16, name="grad_in_c")
    t_in = None
    for piece, nm in ((t_a, "a"), (t_b, "b"), (t_c, "c")):
        t_in = _place_row_blocks(piece, *_block_table(nm), t_in, IN_COLS, "order_terms_" + nm)
    t_in = by_chip(t_in.reshape(N_DEV, IN_BLK, D))
    (f_in,) = _run_comm(_sibling_comm([t_in]), "scatter_in_sibling")
    c_in = _pair_sum(t_in, f_in, "pair_sum_in")
    sems, c_in, land, token = _chip_exchange_start(c_in, jnp.zeros(c_in.shape, c_in.dtype))
    (grad_x, acc_in), _ = _input_bwd(dp_a, dp_b, dp_c, w_a, w_b, w_c, ctx2, x2, dx1,
                                     norm_mix_w + token[0, 0], sh_in, sc_in)
    small = _pack_small(acc_in, acc_mg, acc_ffn, acc_prep, dsink, dlb)
    return grad_x, small, [r_bh, r_ba, r_out, r_gate, r_up, r_down], (sems, c_in, land)


def kernel(x, c, ctx, c_ctx, w_ada, b_ada, norm_mix_w, norm_ffn_w, w_in, hgrn_lb_logits, hgrn_norm_w, q_norm_w, k_norm_w, attn_sinks, w_branch_hgrn, w_branch_attn, w_out, w_ffn_gate, w_ffn_up, w_ffn_down, loss_target, m_c_ctx, m_w_ada, m_b_ada, m_norm_mix_w, m_norm_ffn_w, m_w_in, m_hgrn_lb_logits, m_hgrn_norm_w, m_q_norm_w, m_k_norm_w, m_attn_sinks, m_w_branch_hgrn, m_w_branch_attn, m_w_out, m_w_ffn_gate, m_w_ffn_up, m_w_ffn_down, v_c_ctx, v_w_ada, v_b_ada, v_norm_mix_w, v_norm_ffn_w, v_w_in, v_hgrn_lb_logits, v_hgrn_norm_w, v_q_norm_w, v_k_norm_w, v_attn_sinks, v_w_branch_hgrn, v_w_branch_attn, v_w_out, v_w_ffn_gate, v_w_ffn_up, v_w_ffn_down):
    me = 4 * lax.axis_index("x") + 2 * lax.axis_index("y") + lax.axis_index("c")
    x2, ctx2, tgt = x[0], ctx[0], loss_target[0]
    w_ada2, w_in2 = w_ada[0], w_in[0]

    cond = jnp.zeros((8, D), F32).at[0].set(c[0]).at[1, :256].set(hgrn_lb_logits.reshape(256))
    b_cols = lax.dynamic_slice(b_ada, (0, me * ADA_BLK), (1, ADA_BLK))
    g0, cc, g1, g_in = _prologue(cond, c_ctx.reshape(1, D), w_ada2, b_cols, w_in2.T.astype(BF16))
    lbl = jnp.transpose(g0[:, 1, :256].reshape(N_DEV, 2, 2, 64), (1, 2, 0, 3)).reshape(2, 2, HGW)
    mod_all = _cols_from_blocks(g1)
    mod = lax.dynamic_slice(mod_all, (me, 0), (1, 6 * D)).reshape(6, D)
    mod_c = mod_all[8].reshape(6, D)
    sh1, sc1, gate1, sh2, sc2, gate2 = [mod[k:k + 1] for k in range(6)]
    sh_in = jnp.concatenate([mod_c[0:1], sh1], axis=0)
    sc_in = jnp.concatenate([mod_c[1:2], sc1], axis=0)

    shards = [w_branch_hgrn[0].T, w_branch_attn[0].T, w_out[0], w_ffn_gate[0].T, w_ffn_up[0].T, w_ffn_down[0]]
    w_in_t = g_in.reshape(IN_COLS, D)
    w_a, w_b, w_c = [_pick_row_blocks(w_in_t, *_block_table(nm), "order_w_" + nm) for nm in "abc"]

    grad_x, small, (r_bh, r_ba, r_out, r_gate, r_up, r_down), pending_in = _local_step(
        x2, ctx2, tgt, lbl, sh_in, sc_in, gate1, sh2, sc2, gate2, norm_mix_w, norm_ffn_w, hgrn_norm_w,
        q_norm_w, k_norm_w, attn_sinks, w_a, w_b, w_c, *[s.astype(BF16) for s in shards])

    big, tokens = {}, []
    for nm, rr, ww, mm, vv, tr, transposed in (
            ("w_branch_hgrn", r_bh, w_branch_hgrn[0], m_w_branch_hgrn[0], v_w_branch_hgrn[0], 128, True),
            ("w_branch_attn", r_ba, w_branch_attn[0], m_w_branch_attn[0], v_w_branch_attn[0], 128, True),
            ("w_out", r_out, w_out[0], m_w_out[0], v_w_out[0], 128, False),
            ("w_ffn_gate", r_gate, w_ffn_gate[0], m_w_ffn_gate[0], v_w_ffn_gate[0], 352, True),
            ("w_ffn_up", r_up, w_ffn_up[0], m_w_ffn_up[0], v_w_ffn_up[0], 352, True),
            ("w_ffn_down", r_down, w_ffn_down[0], m_w_ffn_down[0], v_w_ffn_down[0], 352, False)):
        if transposed:
            *res, token = _adamw_sharded(rr, ww.T, mm.T, vv.T, "adamw_" + nm, tr)
            big[nm] = [t.T[None] for t in res]
        else:
            *res, token = _adamw_sharded(rr, ww, mm, vv, "adamw_" + nm, tr)
            big[nm] = [t[None] for t in res]
        tokens.append(token[0, 0])
    small = small + sum(tokens)

    (g2,) = _all_gather([small], "gather_small", True)
    tot = _sum_small(g2)
    dm = jnp.zeros((16, 6 * D), F32).at[:8].set(g2[:, R_DMOD:R_DMOD + 6, :].reshape(N_DEV, 6 * D))
    dm = dm.at[8, :2 * D].set(tot[R_DCTX:R_DCTX + 2].reshape(2 * D))
    dm_cols = lax.dynamic_slice(dm, (0, me * ADA_BLK), (16, ADA_BLK))
    g_w_ada, dsc_term = _ada_grads(cc, dm_cols, w_ada2)
    (g3,) = _all_gather([dsc_term], "gather_cctx", True)
    g_c_ctx = _c_ctx_grad(g3, c_ctx.reshape(1, D))
    g_lbl = _lb_grads(tot[R_DLB:R_DLB + 2, :HGW], lbl)
    g_lb_mine = lax.dynamic_slice(g_lbl, (0, 0, me * 64), (2, 2, 64))
    misc = tot[R_MISC]
    loss = misc[M_LOSS]

    rep_out = _adamw_replicated(
        tot, g_c_ctx,
        [b_ada, c_ctx.reshape(1, D), norm_mix_w, norm_ffn_w, hgrn_norm_w, q_norm_w, k_norm_w, attn_sinks],
        [m_b_ada, m_c_ctx.reshape(1, D), m_norm_mix_w, m_norm_ffn_w, m_hgrn_norm_w, m_q_norm_w, m_k_norm_w,
         m_attn_sinks],
        [v_b_ada, v_c_ctx.reshape(1, D), v_norm_mix_w, v_norm_ffn_w, v_hgrn_norm_w, v_q_norm_w, v_k_norm_w,
         v_attn_sinks])
    rep = []
    for kind in range(4):
        vals = dict(zip(_REP_NAMES, rep_out[kind * len(_REP_NAMES):(kind + 1) * len(_REP_NAMES)]))
        vals["c_ctx"] = vals["c_ctx"].reshape(D)
        rep.append(vals)

    sems, c_in, land = pending_in
    d_ada, nm_ada, nv_ada = _adamw_plain(g_w_ada, w_ada2, m_w_ada[0], v_w_ada[0], "adamw_w_ada")
    land = _chip_exchange_wait(sems, c_in, land, d_ada)
    own = lax.dynamic_index_in_dim(c_in, 2 * lax.axis_index("x") + lax.axis_index("y"), 0, keepdims=False)
    big["w_in"] = [t.T[None] for t in _adamw_sharded(land, w_in2.T, m_w_in[0].T, v_w_in[0].T, "adamw_w_in", 336,
                                                     extra=own)[:4]]
    ada = [t[None] for t in (g_w_ada, d_ada, nm_ada, nv_ada)]
    lb_w = hgrn_lb_logits.reshape(4, 64)
    d_lb, nm_lb, nv_lb = _adamw_plain(g_lb_mine.reshape(4, 64), lb_w, m_hgrn_lb_logits.reshape(4, 64),
                                      v_hgrn_lb_logits.reshape(4, 64), "adamw_lb")
    lbs = [t.reshape(2, 2, 64) for t in (g_lb_mine, d_lb, nm_lb, nv_lb)]

    names = ['c_ctx', 'w_ada', 'b_ada', 'norm_mix_w', 'norm_ffn_w', 'w_in', 'hgrn_lb_logits', 'hgrn_norm_w',
             'q_norm_w', 'k_norm_w', 'attn_sinks', 'w_branch_hgrn', 'w_branch_attn', 'w_out', 'w_ffn_gate',
             'w_ffn_up', 'w_ffn_down']
    outs = [loss, grad_x[None]]
    for kind in range(4):
        for nm in names:
            if nm == 'w_ada':
                outs.append(ada[kind])
            elif nm == 'hgrn_lb_logits':
                outs.append(lbs[kind])
            elif nm in big:
                outs.append(big[nm][kind])
            else:
                outs.append(rep[kind][nm])
    return tuple(outs)
```

```python
import functools
import math

import numpy as np
import jax
import jax.numpy as jnp
from jax import lax
from jax.experimental import pallas as pl
from jax.experimental.pallas import tpu as pltpu

F32 = jnp.float32
BF16 = jnp.bfloat16

N_DEV = 8
D = 1024
S = 2048
L = 256
T = L + S
TM = 256
N_TILES = T // TM
N_LAT_TILES = S // TM
HG_HEADS = 4
HG_DIM = 128
HGW = 512
CHUNK = 32
N_CHUNKS = T // CHUNK
N_CTX_CHUNKS = L // CHUNK
ATT_HEADS = 8
KV_HEADS = 2
HEAD_DIM = 64
ATW = 512
KVW = 128
BLOCK = 128
N_BLOCKS = S // BLOCK
GRID_W = 64
ROPE_THETA = 10000.0
D_FF = 2816
FF_BLK = D_FF // N_DEV
FF_TILE = 256
N_FF_TILES = D_FF // FF_TILE
IN_COLS = 5376
IN_BLK = IN_COLS // N_DEV
ADA_BLK = 6 * D // N_DEV
EPS = 1e-6
WA, WB, WC = 2048, 1280, 2048

ADAM_LR = 0.001
ADAM_B1 = 0.9
ADAM_B2 = 0.999
ADAM_EPS = 1e-08
ADAM_WD = 0.01
ADAM_STEP = 10

VMEM_LIMIT = 56 * 1024 * 1024
MESH = pl.DeviceIdType.MESH


def _cp(sem=None, vmem=VMEM_LIMIT):
    return pltpu.CompilerParams(dimension_semantics=sem, vmem_limit_bytes=vmem)


def _full(shape):
    n = len(shape)
    return pl.BlockSpec(shape, lambda *_: (0,) * n)


_VMEM_WHOLE = pl.BlockSpec(memory_space=pltpu.VMEM)
_ANY = pl.BlockSpec(memory_space=pl.ANY)


def _sigmoid(v):
    return 1.0 / (1.0 + jnp.exp(-v))


def _dot(a, b):
    return jnp.dot(a, b, preferred_element_type=F32)


def _dot_nt(a, b):
    return lax.dot_general(a, b, (((1,), (1,)), ((), ())), preferred_element_type=F32)


def _dot_tn(a, b):
    return lax.dot_general(a, b, (((0,), (0,)), ((), ())), preferred_element_type=F32)


def _split3(v):
    hi = v.astype(BF16)
    r = v - hi.astype(F32)
    mid = r.astype(BF16)
    lo = (r - mid.astype(F32)).astype(BF16)
    return hi, mid, lo


def _dot_exact_rhs01(v, m01):
    hi, mid, lo = _split3(v)
    return _dot(hi, m01) + _dot(mid, m01) + _dot(lo, m01)


def _split2(v):
    hi = v.astype(BF16)
    return hi, (v - hi.astype(F32)).astype(BF16)


def _dot_lhs01(m01, v):
    hi, lo = _split2(v)
    return _dot(m01, hi) + _dot(m01, lo)


def _dot_f32(a, b, dot=_dot):
    ah, am, al = _split3(a)
    bh, bm, bl = _split3(b)
    return (dot(ah, bh) + (dot(ah, bm) + dot(am, bh))
            + (dot(am, bm) + dot(ah, bl) + dot(al, bh)))


def _my_pos():
    return lax.axis_index("x"), lax.axis_index("y"), lax.axis_index("c")


class _Comm:
    def __init__(self, operands, out_shapes, sems, phases):
        self.operands, self.out_shapes, self.sems, self.phases = operands, out_shapes, sems, phases


def _gather_comm(blocks):
    n = len(blocks)

    def parts(ins, outs, sems):
        send_sems, recv_sems, local_sems = sems
        x, y, c = _my_pos()
        me, sibling = (x, y, c), (x, y, 1 - c)
        chips = [(1 - x, y), (x, 1 - y), (1 - x, 1 - y)]

        def slot(a, px, py, pc):
            return outs[a].at[4 * px + 2 * py + pc]

        def copy(a, k, block, to, src=None):
            return pltpu.make_async_remote_copy(
                src_ref=slot(a, *block) if src is None else src, dst_ref=slot(a, *block),
                send_sem=send_sems.at[a, k], recv_sem=recv_sems.at[a, k],
                device_id=to, device_id_type=MESH)

        mine = [pltpu.make_async_copy(ins[a], slot(a, *me), local_sems.at[a]) for a in range(n)]
        first = []
        for a in range(n):
            first.append(copy(a, 0, me, sibling, src=ins[a]))
            first += [copy(a, 1 + j, me, (*chip, c), src=ins[a]) for j, chip in enumerate(chips)]
        passed = [copy(a, 4 + j, (*chip, c), sibling) for j, chip in enumerate(chips) for a in range(n)]
        return c, me, sibling, chips, copy, mine, first, passed

    def start(ins, outs, sems):
        _, _, _, _, _, mine, first, _ = parts(ins, outs, sems)
        for cp in mine + first:
            cp.start()

    def forward(ins, outs, sems):
        c, me, _, chips, copy, _, _, passed = parts(ins, outs, sems)
        for j, chip in enumerate(chips):
            for a in range(n):
                copy(a, 1 + j, (*chip, c), me).wait_recv()
                passed[j * n + a].start()

    def finish(ins, outs, sems):
        c, me, sibling, chips, copy, mine, first, passed = parts(ins, outs, sems)
        for a in range(n):
            copy(a, 0, sibling, me).wait_recv()
            for j, chip in enumerate(chips):
                copy(a, 4 + j, (*chip, 1 - c), me).wait_recv()
        for cp in first + passed:
            cp.wait_send()
        for cp in mine:
            cp.wait()

    return _Comm(blocks, [jax.ShapeDtypeStruct((N_DEV,) + b.shape, b.dtype) for b in blocks],
                 [pltpu.SemaphoreType.DMA((n, 7)), pltpu.SemaphoreType.DMA((n, 7)), pltpu.SemaphoreType.DMA((n,))],
                 [start, forward, finish])


def _gather_comm_relayed(blocks):
    n = len(blocks)

    def parts(ins, outs, sems):
        send_sems, recv_sems, local_sems = sems
        x, y, c = _my_pos()
        me, sibling = (x, y, c), (x, y, 1 - c)
        x_nbr, y_nbr, diag = (1 - x, y, c), (x, 1 - y, c), (1 - x, 1 - y, c)

        def slot(a, dev, half=None):
            ref = outs[a].at[4 * dev[0] + 2 * dev[1] + dev[2]]
            if half is None:
                return ref
            rows = blocks[a].shape[0] // 2
            return ref.at[pl.ds(half * rows, rows)]

        def copy(a, k, block, to, half=None, src=None):
            return pltpu.make_async_remote_copy(
                src_ref=slot(a, block, half) if src is None else src, dst_ref=slot(a, block, half),
                send_sem=send_sems.at[a, k], recv_sem=recv_sems.at[a, k],
                device_id=to, device_id_type=MESH)

        mine = [pltpu.make_async_copy(ins[a], slot(a, me), local_sems.at[a]) for a in range(n)]
        return me, sibling, x_nbr, y_nbr, diag, copy, mine

    def start(ins, outs, sems):
        me, sibling, x_nbr, y_nbr, _, copy, mine = parts(ins, outs, sems)
        for cp in mine:
            cp.start()
        for a in range(n):
            for k, to in ((1, x_nbr), (2, y_nbr), (0, sibling)):
                copy(a, k, me, to, src=ins[a]).start()

    def forward(ins, outs, sems):
        me, sibling, x_nbr, y_nbr, _, copy, _ = parts(ins, outs, sems)
        for a in range(n):
            copy(a, 1, x_nbr, me).wait_recv()
            copy(a, 3, x_nbr, y_nbr, half=0).start()
            copy(a, 5, x_nbr, sibling).start()
        for a in range(n):
            copy(a, 2, y_nbr, me).wait_recv()
            copy(a, 4, y_nbr, x_nbr, half=1).start()
            copy(a, 6, y_nbr, sibling).start()

    def finish(ins, outs, sems):
        me, sibling, x_nbr, y_nbr, diag, copy, mine = parts(ins, outs, sems)
        sib = lambda dev: (dev[0], dev[1], sibling[2])
        for a in range(n):
            copy(a, 3, diag, me, half=0).wait_recv()
            copy(a, 4, diag, me, half=1).wait_recv()
            copy(a, 7, diag, sibling).start()
        for a in range(n):
            copy(a, 0, sibling, me).wait_recv()
            for k, dev in ((5, x_nbr), (6, y_nbr), (7, diag)):
                copy(a, k, sib(dev), me).wait_recv()
        for a in range(n):
            for k, block, to, half in ((0, me, sibling, None), (1, me, x_nbr, None), (2, me, y_nbr, None),
                                       (3, x_nbr, y_nbr, 0), (4, y_nbr, x_nbr, 1), (5, x_nbr, sibling, None),
                                       (6, y_nbr, sibling, None), (7, diag, sibling, None)):
                copy(a, k, block, to, half=half, src=ins[a] if block is me else None).wait_send()
        for cp in mine:
            cp.wait()

    return _Comm(blocks, [jax.ShapeDtypeStruct((N_DEV,) + b.shape, b.dtype) for b in blocks],
                 [pltpu.SemaphoreType.DMA((n, 8)), pltpu.SemaphoreType.DMA((n, 8)), pltpu.SemaphoreType.DMA((n,))],
                 [start, forward, finish])


_HBM = pl.BlockSpec(memory_space=pltpu.HBM)
_SEM = pl.BlockSpec(memory_space=pltpu.SEMAPHORE)
_SPLIT_COPY = pltpu.CompilerParams(has_side_effects=pltpu.SideEffectType.DATAFLOW_SIDE_EFFECTING)


def _chip_exchange_copies(src_ref, land_ref, sems):
    x, y, c = _my_pos()
    q_me = 2 * x + y
    pairs = []
    for j, (px, py) in enumerate([(1 - x, y), (x, 1 - y), (1 - x, 1 - y)]):
        q = 2 * px + py
        send = pltpu.make_async_remote_copy(
            src_ref=src_ref.at[q], dst_ref=land_ref.at[q_me], send_sem=sems[j], recv_sem=sems[3 + j],
            device_id=(px, py, c), device_id_type=MESH)
        recv = pltpu.make_async_remote_copy(
            src_ref=src_ref.at[q], dst_ref=land_ref.at[q], send_sem=sems[j], recv_sem=sems[3 + j],
            device_id=(x, y, c), device_id_type=MESH)
        pairs.append((send, recv))
    return pairs


def _chip_exchange_start(src, land):
    def body(src_ref, land_ref, *outs):
        sems, token = outs[:6], outs[8]
        for send, _ in _chip_exchange_copies(src_ref, land_ref, sems):
            send.start()
        token[...] = jnp.zeros_like(token)

    res = pl.pallas_call(
        body, name="scatter_in_start",
        out_shape=(pltpu.SemaphoreType.DMA(()),) * 6 + (
            pltpu.HBM(src.shape, src.dtype), pltpu.HBM(land.shape, land.dtype),
            jax.ShapeDtypeStruct((8, 128), F32)),
        in_specs=(_HBM, _HBM), out_specs=(_SEM,) * 6 + (_HBM, _HBM, pl.BlockSpec(memory_space=pltpu.VMEM)),
        input_output_aliases={0: 6, 1: 7}, compiler_params=_SPLIT_COPY,
    )(pltpu.with_memory_space_constraint(src, pltpu.HBM), pltpu.with_memory_space_constraint(land, pltpu.HBM))
    return res[:6], res[6], res[7], res[8]


def _chip_exchange_wait(sems, src_thru, land_thru, after):
    def body(src_ref, land_ref, *rest):
        for send, recv in _chip_exchange_copies(src_ref, land_ref, rest[:6]):
            send.wait_send()
            recv.wait_recv()

    return pl.pallas_call(
        body, name="scatter_in_wait",
        out_shape=(pltpu.HBM(src_thru.shape, src_thru.dtype), pltpu.HBM(land_thru.shape, land_thru.dtype)),
        in_specs=(_HBM, _HBM) + (_SEM,) * 6 + (_ANY,), out_specs=(_HBM, _HBM),
        input_output_aliases={0: 0, 1: 1}, compiler_params=_SPLIT_COPY,
    )(src_thru, land_thru, *sems, after)[1]


def _run_comm(comm, name, in_vmem=False):
    n_in, n_out = len(comm.operands), len(comm.out_shapes)

    def body(*refs):
        ins, outs, sems = refs[:n_in], refs[n_in:n_in + n_out], refs[n_in + n_out:]
        for phase in comm.phases:
            phase(ins, outs, sems)

    spec = _VMEM_WHOLE if in_vmem else _ANY
    return pl.pallas_call(
        body, name=name, out_shape=comm.out_shapes, in_specs=[spec] * n_in, out_specs=[spec] * n_out,
        scratch_shapes=comm.sems,
    )(*comm.operands)


def _carrier_call(body, comm, schedule, *, name, grid, in_specs, out_specs, out_shape, scratch_shapes, operands):
    n_in, n_out, n_scr = len(in_specs), len(out_specs), len(scratch_shapes)
    c_in, c_out = len(comm.operands), len(comm.out_shapes)

    def full_body(*refs):
        ins, refs = refs[:n_in], refs[n_in:]
        cins, refs = refs[:c_in], refs[c_in:]
        outs, refs = refs[:n_out], refs[n_out:]
        couts, refs = refs[:c_out], refs[c_out:]
        scr, csems = refs[:n_scr], refs[n_scr:]
        step = pl.program_id(0)

        def run(before):
            for (at, when_before), phase in zip(schedule, comm.phases):
                if when_before == before:
                    pl.when(step == at)(functools.partial(phase, cins, couts, csems))

        run(True)
        body(*ins, *outs, *scr)
        run(False)

    res = pl.pallas_call(
        full_body, name=name, grid=grid,
        in_specs=list(in_specs) + [_ANY] * c_in, out_specs=list(out_specs) + [_ANY] * c_out,
        out_shape=list(out_shape) + list(comm.out_shapes),
        scratch_shapes=list(scratch_shapes) + list(comm.sems),
        compiler_params=_cp(("arbitrary",)),
    )(*operands, *comm.operands)
    return res[:n_out], res[n_out:]


def _pcall(body, carried, *, name, grid, in_specs, out_specs, out_shape, scratch_shapes, operands):
    if carried is None:
        res = pl.pallas_call(body, name=name, grid=grid, in_specs=in_specs, out_specs=out_specs,
                             out_shape=out_shape, scratch_shapes=scratch_shapes,
                             compiler_params=_cp(("arbitrary",)))(*operands)
        return res, ()
    return _carrier_call(body, carried[0], carried[1], name=name, grid=grid, in_specs=in_specs,
                         out_specs=out_specs, out_shape=out_shape, scratch_shapes=scratch_shapes,
                         operands=operands)


def _all_gather(blocks, name, in_vmem):
    return _run_comm(_gather_comm(blocks), name, in_vmem)


N_CHIPS = 4


def _sibling_comm(contribs):
    n = len(contribs)

    def copies(ins, outs, sems):
        send_sems, recv_sems = sems
        x, y, c = _my_pos()
        return [pltpu.make_async_remote_copy(
            src_ref=ins[a].at[pl.ds(0, N_CHIPS), 1 - c], dst_ref=outs[a],
            send_sem=send_sems.at[a], recv_sem=recv_sems.at[a],
            device_id=(x, y, 1 - c), device_id_type=MESH) for a in range(n)]

    def start(ins, outs, sems):
        for cp in copies(ins, outs, sems):
            cp.start()

    def finish(ins, outs, sems):
        cps = copies(ins, outs, sems)
        for cp in cps:
            cp.wait_recv()
        for cp in cps:
            cp.wait_send()

    return _Comm(contribs, [jax.ShapeDtypeStruct((N_CHIPS,) + b.shape[2:], b.dtype) for b in contribs],
                 [pltpu.SemaphoreType.DMA((n,)), pltpu.SemaphoreType.DMA((n,))], [start, finish])


def _pair_sum(mine, theirs, name):
    _, _, rows, cols = mine.shape
    core = lax.axis_index("c").astype(jnp.int32).reshape(1)

    def body(c_ref, m_ref, t_ref, o_ref):
        o_ref[...] = (m_ref[...].astype(F32) + t_ref[...].astype(F32)).astype(BF16)

    return pl.pallas_call(
        body, name=name,
        grid_spec=pltpu.PrefetchScalarGridSpec(
            num_scalar_prefetch=1, grid=(N_CHIPS,),
            in_specs=[pl.BlockSpec((None, None, rows, cols), lambda q, c: (q, c[0], 0, 0)),
                      pl.BlockSpec((None, rows, cols), lambda q, c: (q, 0, 0))],
            out_specs=pl.BlockSpec((None, rows, cols), lambda q, c: (q, 0, 0))),
        out_shape=jax.ShapeDtypeStruct((N_CHIPS, rows, cols), BF16),
        compiler_params=_cp(("parallel",)),
    )(core, mine, theirs)


def _chip_comm(sums):
    n = len(sums)

    def parts(ins, outs, sems):
        send_sems, recv_sems, local_sems = sems
        x, y, c = _my_pos()
        q_me = 2 * x + y
        chips = [(1 - x, y), (x, 1 - y), (1 - x, 1 - y)]
        mine = [pltpu.make_async_copy(ins[a].at[q_me], outs[a].at[q_me], local_sems.at[a]) for a in range(n)]
        sends, recvs = [], []
        for j, (px, py) in enumerate(chips):
            for a in range(n):
                q = 2 * px + py
                sends.append(pltpu.make_async_remote_copy(
                    src_ref=ins[a].at[q], dst_ref=outs[a].at[q_me],
                    send_sem=send_sems.at[a, j], recv_sem=recv_sems.at[a, j],
                    device_id=(px, py, c), device_id_type=MESH))
                recvs.append(pltpu.make_async_remote_copy(
                    src_ref=ins[a].at[q], dst_ref=outs[a].at[q],
                    send_sem=send_sems.at[a, j], recv_sem=recv_sems.at[a, j],
                    device_id=(x, y, c), device_id_type=MESH))
        return mine, sends, recvs

    def start(ins, outs, sems):
        mine, sends, _ = parts(ins, outs, sems)
        for cp in mine + sends:
            cp.start()

    def finish(ins, outs, sems):
        mine, sends, recvs = parts(ins, outs, sems)
        for cp in recvs:
            cp.wait_recv()
        for cp in sends:
            cp.wait_send()
        for cp in mine:
            cp.wait()

    return _Comm(sums, [jax.ShapeDtypeStruct(b.shape, b.dtype) for b in sums],
                 [pltpu.SemaphoreType.DMA((n, 3)), pltpu.SemaphoreType.DMA((n, 3)), pltpu.SemaphoreType.DMA((n,))],
                 [start, finish])


def _mm_nt(a, bt, *, tm, tn, out_dtype, name, row_off=0, rows=None):
    rows = a.shape[0] if rows is None else rows
    n, k = bt.shape

    def body(a_ref, b_ref, o_ref):
        o_ref[...] = _dot_nt(a_ref[...], b_ref[...]).astype(out_dtype)

    return pl.pallas_call(
        body, name=name, grid=(rows // tm, n // tn),
        in_specs=[pl.BlockSpec((tm, k), lambda i, j: (i + row_off, 0)),
                  pl.BlockSpec((tn, k), lambda i, j: (j, 0))],
        out_specs=pl.BlockSpec((tm, tn), lambda i, j: (i, j)),
        out_shape=jax.ShapeDtypeStruct((rows, n), out_dtype),
        compiler_params=_cp(("parallel", "parallel")),
    )(a, bt)


def _mm_tn(a, b, *, tk, nk, tm, tn, out_dtype, name, a_off=0, b_off=0):
    m, n = a.shape[1], b.shape[1]

    def body(a_ref, b_ref, o_ref, acc):
        kk = pl.program_id(2)

        @pl.when(kk == 0)
        def _():
            acc[...] = jnp.zeros_like(acc)

        acc[...] += _dot_tn(a_ref[...], b_ref[...])

        @pl.when(kk == nk - 1)
        def _():
            o_ref[...] = acc[...].astype(out_dtype)

    return pl.pallas_call(
        body, name=name, grid=(m // tm, n // tn, nk),
        in_specs=[pl.BlockSpec((tk, tm), lambda i, j, kk: (kk + a_off, i)),
                  pl.BlockSpec((tk, tn), lambda i, j, kk: (kk + b_off, j))],
        out_specs=pl.BlockSpec((tm, tn), lambda i, j, kk: (i, j)),
        out_shape=jax.ShapeDtypeStruct((m, n), out_dtype),
        scratch_shapes=[pltpu.VMEM((tm, tn), F32)],
        compiler_params=_cp(("parallel", "parallel", "arbitrary")),
    )(a, b)


def _mm_tn_blocked(a, b, name, carried=None):
    nb, _, w = a.shape
    n = b.shape[1]

    def body(a_ref, b_ref, o_ref):
        o_ref[...] = _dot_tn(a_ref[...], b_ref[...]).astype(BF16)

    (out,), extra = _pcall(
        body, carried, name=name, grid=(nb,),
        in_specs=[pl.BlockSpec((None, S, w), lambda j: (j, 0, 0)), _full((S, n))],
        out_specs=[pl.BlockSpec((None, w, n), lambda j: (j, 0, 0))],
        out_shape=[jax.ShapeDtypeStruct((nb, w, n), BF16)],
        scratch_shapes=[], operands=[a, b])
    return out, extra


def _prologue(cond, c_ctx, w_ada, b_cols, w_in_t):
    rows_shape = jax.ShapeDtypeStruct((16, ADA_BLK), F32)
    big, g_cond, g_mod = _gather_comm_relayed([w_in_t]), _gather_comm([cond]), _gather_comm([rows_shape])

    def body(cond_ref, cctx_ref, wada_ref, b_ref, win_ref, g0_ref, cc_ref, g1_ref, gin_ref, rows_ref, *sems):
        s_big, s_cond, s_mod = sems[0:3], sems[3:6], sems[6:9]
        big.phases[0]([win_ref], [gin_ref], s_big)
        for phase in g_cond.phases:
            phase([cond_ref], [g0_ref], s_cond)
        cc_ref[...] = jnp.zeros_like(cc_ref)
        for j in range(N_DEV):
            cc_ref[j:j + 1, :] = g0_ref[j, 0:1, :]
        cc_ref[N_DEV:N_DEV + 1, :] = cctx_ref[...]
        cv = cc_ref[...]
        rows_ref[...] = _dot_f32(cv * _sigmoid(cv), wada_ref[...]) + b_ref[...]
        for phase in g_mod.phases:
            phase([rows_ref], [g1_ref], s_mod)
        big.phases[1]([win_ref], [gin_ref], s_big)
        big.phases[2]([win_ref], [gin_ref], s_big)

    return pl.pallas_call(
        body, name="prologue",
        in_specs=[_VMEM_WHOLE] * 4 + [_ANY], out_specs=[_VMEM_WHOLE] * 3 + [_ANY],
        out_shape=[g_cond.out_shapes[0], jax.ShapeDtypeStruct((16, D), F32), g_mod.out_shapes[0],
                   big.out_shapes[0]],
        scratch_shapes=[pltpu.VMEM((16, ADA_BLK), F32)] + big.sems + g_cond.sems + g_mod.sems,
        compiler_params=_cp(),
    )(cond, c_ctx, w_ada, b_cols, w_in_t)


def _ada_grads(cc, dm_cols, w_ada):
    def body(c_ref, dm_ref, w_ref, gw_ref, dsc_ref):
        cv = c_ref[...]
        sc = cv * _sigmoid(cv)
        dm = dm_ref[...]
        gw_ref[...] = _dot_f32(sc, dm, dot=_dot_tn)
        dsc_ref[...] = _dot_f32(dm, w_ref[...], dot=_dot_nt)

    return pl.pallas_call(
        body, name="ada_grads",
        in_specs=[_VMEM_WHOLE] * 3, out_specs=[_VMEM_WHOLE] * 2,
        out_shape=[jax.ShapeDtypeStruct((D, ADA_BLK), F32), jax.ShapeDtypeStruct((16, D), F32)],
        compiler_params=_cp(),
    )(cc, dm_cols, w_ada)


def _lat(i):
    return jnp.maximum(i - 1, 0)


def _rms_mod(xv, nw, sh, sc):
    rstd = lax.rsqrt(jnp.mean(xv * xv, axis=-1, keepdims=True) + EPS)
    return (xv * rstd * nw) * (1.0 + sc) + sh


def _rms_mod_bwd(xv, nw, sc, dh):
    rstd = lax.rsqrt(jnp.mean(xv * xv, axis=-1, keepdims=True) + EPS)
    xhat = xv * rstd
    dn = dh * (1.0 + sc)
    dxhat = dn * nw
    dx = rstd * (dxhat - xhat * jnp.mean(dxhat * xhat, axis=-1, keepdims=True))
    return (dx, jnp.sum(dh, axis=0, keepdims=True), jnp.sum(dh * (xhat * nw), axis=0, keepdims=True),
            jnp.sum(dn * xhat, axis=0, keepdims=True))


def _norm_mod_all(ctx, x, nw, sh, sc):
    def body(ctx_ref, x_ref, nw_ref, sh_ref, sc_ref, o_ref):
        i = pl.program_id(0)
        sel = jnp.minimum(i, 1)
        xv = jnp.where(i == 0, ctx_ref[...], x_ref[...])
        o_ref[...] = _rms_mod(xv, nw_ref[...], sh_ref[pl.ds(sel, 1), :], sc_ref[pl.ds(sel, 1), :]).astype(BF16)

    return pl.pallas_call(
        body, name="norm_mod", grid=(N_TILES,),
        in_specs=[_full((TM, D)), pl.BlockSpec((TM, D), lambda i: (_lat(i), 0)),
                  _full((1, D)), _full((2, D)), _full((2, D))],
        out_specs=pl.BlockSpec((TM, D), lambda i: (i, 0)),
        out_shape=jax.ShapeDtypeStruct((T, D), BF16),
        compiler_params=_cp(("parallel",)),
    )(ctx, x, nw, sh, sc)


def _chunk_masks(reverse):
    row = lax.broadcasted_iota(jnp.int32, (TM, TM), 0)
    col = lax.broadcasted_iota(jnp.int32, (TM, TM), 1)
    same = (row // CHUNK) == (col // CHUNK)
    tri = same & ((col >= row) if reverse else (col <= row))
    return same, tri


def _chunk_order(i, reverse):
    if not reverse:
        return i
    return jnp.where(i < N_CTX_CHUNKS, N_CTX_CHUNKS - 1 - i, N_CHUNKS + N_CTX_CHUNKS - 1 - i)


def _decay_terms(z, lb, same01, tri01):
    f = lb + (1.0 - lb) * _sigmoid(z)
    g = jnp.log(f)
    g2 = jnp.concatenate(_split2(g), axis=1)
    b2 = _dot(tri01, g2)
    t2 = _dot(same01, g2)
    return f, 1.0 - f, b2[:, :HG_DIM] + b2[:, HG_DIM:], t2[:, :HG_DIM] + t2[:, HG_DIM:]


def _chunk_outer(a, b):
    n = TM // CHUNK
    return jnp.einsum('ncv,nck->nvk', a.reshape(n, CHUNK, HG_DIM), b.reshape(n, CHUNK, HG_DIM),
                      preferred_element_type=F32)


def _hgrn_fwd(p_a, lbl, carried=None):
    cpt = TM // CHUNK

    def body(p_ref, lbl_ref, o_ref, st_ref, qd_s, kd_s, u_s, v_s, ebt_s):
        masks = [_chunk_masks(d == 1) for d in (0, 1)]
        same01 = jnp.where(masks[0][0], 1.0, 0.0).astype(BF16)
        tri = [m[1] for m in masks]
        tri01 = [jnp.where(t, 1.0, 0.0).astype(BF16) for t in tri]
        lb = [_sigmoid(lbl_ref[d][0:1, :] - lbl_ref[d][1:2, :]) for d in (0, 1)]

        def prep(r, carry):
            r0 = pl.multiple_of(r * TM, TM)
            vb = p_ref[pl.ds(r0, TM), 2 * HG_DIM:3 * HG_DIM].astype(BF16)
            v_s[pl.ds(r0, TM), :] = vb
            for d in (0, 1):
                z = p_ref[pl.ds(r0, TM), d * HG_DIM:(d + 1) * HG_DIM]
                _, k, b, bt = _decay_terms(z, lb[d], same01, tri01[d])
                u_s[d, pl.ds(r * cpt, cpt)] = _chunk_outer(vb, (k * jnp.exp(bt - b)).astype(BF16))
                ebt_s[d, pl.ds(r0, TM), :] = jnp.exp(bt)

                @pl.when(r >= 1)
                def _():
                    rl = pl.multiple_of(r0 - L, TM)
                    qr = p_ref[pl.ds(r0, TM), 3 * HG_DIM:4 * HG_DIM]
                    q = qr * _sigmoid(qr) * HG_DIM ** -0.5
                    qd_s[d, pl.ds(rl, TM), :] = (q * jnp.exp(b)).astype(BF16)
                    kd_s[d, pl.ds(rl, TM), :] = (k * jnp.exp(-b)).astype(BF16)

            return carry

        lax.fori_loop(0, N_TILES, prep, 0)

        def scan(i, sts):
            new = []
            for d in (0, 1):
                nn = _chunk_order(i, d == 1)
                c0 = pl.multiple_of(nn * CHUNK, CHUNK)
                st_ref[d, nn] = sts[d].astype(BF16)
                new.append(sts[d] * ebt_s[d, pl.ds(c0, 1), :] + u_s[d, nn])
            return tuple(new)

        zero = jnp.zeros((HG_DIM, HG_DIM), F32)
        lax.fori_loop(0, N_CHUNKS, scan, (zero, zero))

        def outp(r, carry):
            r0 = pl.multiple_of(r * TM, TM)
            vb = v_s[pl.ds(r0 + L, TM), :]
            o = jnp.zeros((TM, HG_DIM), F32)
            for d in (0, 1):
                qd = qd_s[d, pl.ds(r0, TM), :]
                a = jnp.where(tri[d], _dot_nt(qd, kd_s[d, pl.ds(r0, TM), :]), 0.0)
                stb = st_ref[d, pl.ds(N_CTX_CHUNKS + r * cpt, cpt)]
                inter = jnp.einsum('nck,nvk->ncv', qd.reshape(cpt, CHUNK, HG_DIM), stb,
                                   preferred_element_type=F32)
                o = o + _dot(a.astype(BF16), vb) + inter.reshape(TM, HG_DIM)
            o_ref[pl.ds(r0, TM), :] = o
            return carry

        lax.fori_loop(0, N_LAT_TILES, outp, 0)

    return _pcall(
        body, carried, name="hgrn_fwd", grid=(HG_HEADS,),
        in_specs=[pl.BlockSpec((T, 4 * HG_DIM), lambda h: (0, h)),
                  pl.BlockSpec((2, 2, HG_DIM), lambda h: (0, 0, h))],
        out_specs=[pl.BlockSpec((S, HG_DIM), lambda h: (0, h)),
                   pl.BlockSpec((2, None, N_CHUNKS, HG_DIM, HG_DIM), lambda h: (0, h, 0, 0, 0))],
        out_shape=[jax.ShapeDtypeStruct((S, HGW), F32),
                   jax.ShapeDtypeStruct((2, HG_HEADS, N_CHUNKS, HG_DIM, HG_DIM), BF16)],
        scratch_shapes=[pltpu.VMEM((2, S, HG_DIM), BF16), pltpu.VMEM((2, S, HG_DIM), BF16),
                        pltpu.VMEM((2, N_CHUNKS, HG_DIM, HG_DIM), F32), pltpu.VMEM((T, HG_DIM), BF16),
                        pltpu.VMEM((2, T, HG_DIM), F32)],
        operands=[p_a, lbl])


def _hgrn_bwd(p_a, lbl, d_o, st, carried=None):
    cpt = TM // CHUNK

    def rows(r):
        return r * TM if isinstance(r, int) else pl.multiple_of(r * TM, TM)

    def body(p_ref, lbl_ref, do_ref, st_ref, dp_ref, dlb_ref, b_s, bt_s, dbt_s, qd_s, dst_s, w_s):
        masks = [_chunk_masks(d == 1) for d in (0, 1)]
        same01 = jnp.where(masks[0][0], 1.0, 0.0).astype(BF16)
        tri = [m[1] for m in masks]
        tri01 = [jnp.where(t, 1.0, 0.0).astype(BF16) for t in tri]
        later01 = [tri01[1], tri01[0]]
        lb = [_sigmoid(lbl_ref[d][0:1, :] - lbl_ref[d][1:2, :]) for d in (0, 1)]

        def prep_tile(r, latent):
            r0 = rows(r)
            for d in (0, 1):
                z = p_ref[pl.ds(r0, TM), d * HG_DIM:(d + 1) * HG_DIM]
                _, _, b, bt = _decay_terms(z, lb[d], same01, tri01[d])
                b_s[d, pl.ds(r0, TM), :] = b
                bt_s[d, pl.ds(r0, TM), :] = bt
                if latent:
                    rl = pl.multiple_of(r0 - L, TM)
                    qr = p_ref[pl.ds(r0, TM), 3 * HG_DIM:4 * HG_DIM]
                    qd = (qr * _sigmoid(qr) * HG_DIM ** -0.5 * jnp.exp(b)).astype(BF16)
                    qd_s[d, pl.ds(rl, TM), :] = qd
                    w_s[d, pl.ds(r * cpt, cpt)] = _chunk_outer(
                        do_ref[pl.ds(rl, TM), :].astype(BF16), qd).astype(BF16)

        prep_tile(0, False)
        w_s[:, pl.ds(0, N_CTX_CHUNKS)] = jnp.zeros((2, N_CTX_CHUNKS, HG_DIM, HG_DIM), BF16)

        def prep(r, carry):
            prep_tile(r, True)
            return carry

        lax.fori_loop(1, N_TILES, prep, 0)

        def rscan(j, dsts):
            i = N_CHUNKS - 1 - j
            new = []
            for d in (0, 1):
                nn = _chunk_order(i, d == 1)
                c0 = pl.multiple_of(nn * CHUNK, CHUNK)
                dst_s[d, nn] = dsts[d].astype(BF16)
                after = st_ref[d, _chunk_order(jnp.minimum(i + 1, N_CHUNKS - 1), d == 1)].astype(F32)
                dbt_s[d, pl.ds(c0, CHUNK), :] = jnp.broadcast_to(
                    jnp.sum(after * dsts[d], axis=0, keepdims=True), (CHUNK, HG_DIM))
                new.append(dsts[d] * jnp.exp(bt_s[d, pl.ds(c0, 1), :]) + w_s[d, nn].astype(F32))
            return tuple(new)

        zero = jnp.zeros((HG_DIM, HG_DIM), F32)
        lax.fori_loop(0, N_CHUNKS, rscan, (zero, zero))

        def grad_tile(r, latent):
            r0 = rows(r)
            vb = p_ref[pl.ds(r0, TM), 2 * HG_DIM:3 * HG_DIM].astype(BF16)
            dv = jnp.zeros((TM, HG_DIM), F32)
            dq = jnp.zeros((TM, HG_DIM), F32)
            dlbs = []
            if latent:
                rl = pl.multiple_of(r0 - L, TM)
                qr = p_ref[pl.ds(r0, TM), 3 * HG_DIM:4 * HG_DIM]
                sq = _sigmoid(qr)
                do = do_ref[pl.ds(rl, TM), :].astype(BF16)
                da_full = _dot_nt(do, vb)
            for d in (0, 1):
                z = p_ref[pl.ds(r0, TM), d * HG_DIM:(d + 1) * HG_DIM]
                sz = _sigmoid(z)
                f = lb[d] + (1.0 - lb[d]) * sz
                k = 1.0 - f
                b = b_s[d, pl.ds(r0, TM), :]
                e2 = jnp.exp(bt_s[d, pl.ds(r0, TM), :] - b)
                dstb = dst_s[d, pl.ds(r * cpt, cpt)]
                kd2 = k * e2
                dkd2 = jnp.einsum('ncv,nvk->nck', vb.reshape(cpt, CHUNK, HG_DIM), dstb,
                                  preferred_element_type=F32).reshape(TM, HG_DIM)
                dv = dv + jnp.einsum('nck,nvk->ncv', kd2.astype(BF16).reshape(cpt, CHUNK, HG_DIM), dstb,
                                     preferred_element_type=F32).reshape(TM, HG_DIM)
                dk = dkd2 * e2
                db = -(kd2 * dkd2)
                if latent:
                    eb = jnp.exp(b)
                    enb = jnp.exp(-b)
                    qdf = qr * sq * HG_DIM ** -0.5 * eb
                    kdf = k * enb
                    qd = qd_s[d, pl.ds(rl, TM), :]
                    kd = kdf.astype(BF16)
                    a = jnp.where(tri[d], _dot_nt(qd, kd), 0.0).astype(BF16)
                    da = jnp.where(tri[d], da_full, 0.0).astype(BF16)
                    stb = st_ref[d, pl.ds(r * cpt, cpt)]
                    dqd = _dot(da, kd) + jnp.einsum(
                        'ncv,nvk->nck', do.reshape(cpt, CHUNK, HG_DIM), stb,
                        preferred_element_type=F32).reshape(TM, HG_DIM)
                    dkd = _dot_tn(da, qd)
                    dv = dv + _dot_tn(a, do)
                    dk = dk + dkd * enb
                    db = db + qdf * dqd - kdf * dkd
                    dq = dq + dqd * eb
                dg = _dot_lhs01(later01[d], db) + dbt_s[d, pl.ds(r0, TM), :]
                df = dg / f - dk
                dp_ref[pl.ds(r0, TM), d * HG_DIM:(d + 1) * HG_DIM] = (
                    df * (1.0 - lb[d]) * sz * (1.0 - sz)).astype(BF16)
                dlbs.append(jnp.sum(df * (1.0 - sz), axis=0, keepdims=True))
            dp_ref[pl.ds(r0, TM), 2 * HG_DIM:3 * HG_DIM] = dv.astype(BF16)
            if latent:
                dq = dq * (HG_DIM ** -0.5) * (sq * (1.0 + qr * (1.0 - sq)))
            dp_ref[pl.ds(r0, TM), 3 * HG_DIM:4 * HG_DIM] = dq.astype(BF16)
            return dlbs

        dlb_ctx = grad_tile(0, False)

        def grads(r, acc):
            t = grad_tile(r, True)
            return (acc[0] + t[0], acc[1] + t[1])

        dlb = lax.fori_loop(1, N_TILES, grads, (dlb_ctx[0], dlb_ctx[1]))
        dlb_ref[0:1, :] = dlb[0]
        dlb_ref[1:2, :] = dlb[1]

    return _pcall(
        body, carried, name="hgrn_bwd", grid=(HG_HEADS,),
        in_specs=[pl.BlockSpec((T, 4 * HG_DIM), lambda h: (0, h)),
                  pl.BlockSpec((2, 2, HG_DIM), lambda h: (0, 0, h)),
                  pl.BlockSpec((S, HG_DIM), lambda h: (0, h)),
                  pl.BlockSpec((2, None, N_CHUNKS, HG_DIM, HG_DIM), lambda h: (0, h, 0, 0, 0))],
        out_specs=[pl.BlockSpec((T, 4 * HG_DIM), lambda h: (0, h)),
                   pl.BlockSpec((2, HG_DIM), lambda h: (0, h))],
        out_shape=[jax.ShapeDtypeStruct((T, WA), BF16), jax.ShapeDtypeStruct((2, HGW), F32)],
        scratch_shapes=[pltpu.VMEM((2, T, HG_DIM), F32), pltpu.VMEM((2, T, HG_DIM), F32),
                        pltpu.VMEM((2, T, HG_DIM), F32), pltpu.VMEM((2, S, HG_DIM), BF16),
                        pltpu.VMEM((2, N_CHUNKS, HG_DIM, HG_DIM), BF16),
                        pltpu.VMEM((2, N_CHUNKS, HG_DIM, HG_DIM), BF16)],
        operands=[p_a, lbl, d_o, st])


def _rope_tables():
    t = np.arange(S)
    inv = ROPE_THETA ** (-np.arange(0, 32, 2, dtype=np.float64) / 32)
    lane = np.arange(64)
    pos = np.where(lane[None, :] < 32, (t // GRID_W)[:, None], (t % GRID_W)[:, None]).astype(np.float64)
    ang = pos * inv[(lane % 32) % 16][None, :]
    sign = np.where((lane % 32) < 16, -1.0, 1.0)[None, :]
    cos = np.tile(np.cos(ang), (1, 2)).astype(np.float32)
    sin = np.tile(np.sin(ang) * sign, (1, 2)).astype(np.float32)
    return jnp.asarray(cos), jnp.asarray(sin)


def _rope_partner(v):
    lane = lax.broadcasted_iota(jnp.int32, (1, 128), 1)
    first = (lane % 32) < 16
    slabs = []
    for j in range(v.shape[1] // 128):
        s = v[:, 128 * j:128 * (j + 1)]
        slabs.append(jnp.where(first, pltpu.roll(s, 112, 1), pltpu.roll(s, 16, 1)))
    return slabs[0] if len(slabs) == 1 else jnp.concatenate(slabs, axis=1)


def _group_ones(width, group):
    r = lax.broadcasted_iota(jnp.int32, (width, width), 0)
    c = lax.broadcasted_iota(jnp.int32, (width, width), 1)
    return jnp.where((r // group) == (c // group), 1.0, 0.0).astype(BF16)


def _group_mean(v, ones01, group):
    hi = v.astype(BF16)
    lo = (v - hi.astype(F32)).astype(BF16)
    return (_dot(hi, ones01) + _dot(lo, ones01)) * (1.0 / group)


def _rep_matrix():
    r = lax.broadcasted_iota(jnp.int32, (KVW, ATW), 0)
    c = lax.broadcasted_iota(jnp.int32, (KVW, ATW), 1)
    return jnp.where(r == HEAD_DIM * (c // 256) + c % HEAD_DIM, 1.0, 0.0).astype(BF16)


def _tile_lanes(v, reps):
    return jnp.concatenate([v] * reps, axis=1)


def _prep_fwd(p_b, o, cos, sin, hnw, qnw, knw):
    def body(p_ref, o_ref, cos_ref, sin_ref, hnw_ref, qnw_ref, knw_ref, y_ref, q_ref, k_ref, v_ref):
        i = pl.program_id(0)
        rep = _rep_matrix()
        ones_k = _group_ones(KVW, HEAD_DIM)
        kr = p_ref[:, 1024:1152]
        krstd = lax.rsqrt(_group_mean(kr * kr, ones_k, HEAD_DIM) + EPS)
        kn = kr * krstd * knw_ref[...]
        v_ref[...] = _dot(p_ref[:, 1152:1280].astype(BF16), rep).astype(BF16)

        @pl.when(i == 0)
        def _():
            k_ref[...] = _dot(kn.astype(BF16), rep).astype(BF16)

        @pl.when(i > 0)
        def _():
            cs, sn = cos_ref[...], sin_ref[...]
            kro = kn * cs + _rope_partner(kn) * sn
            k_ref[...] = _dot(kro.astype(BF16), rep).astype(BF16)
            qr = p_ref[:, 512:1024]
            qrstd = lax.rsqrt(_group_mean(qr * qr, _group_ones(ATW, HEAD_DIM), HEAD_DIM) + EPS)
            qn = qr * qrstd * qnw_ref[...]
            qro = qn * _tile_lanes(cs, 4) + _rope_partner(qn) * _tile_lanes(sn, 4)
            q_ref[...] = (qro * HEAD_DIM ** -0.5).astype(BF16)
            ys = []
            for h in range(HG_HEADS):
                oh = o_ref[:, HG_DIM * h:HG_DIM * (h + 1)]
                gh = p_ref[:, HG_DIM * h:HG_DIM * (h + 1)]
                rstd = lax.rsqrt(jnp.mean(oh * oh, axis=-1, keepdims=True) + EPS)
                ys.append(oh * rstd * hnw_ref[...] * (gh * _sigmoid(gh)))
            y_ref[...] = jnp.concatenate(ys, axis=1).astype(BF16)

    return pl.pallas_call(
        body, name="prep_fwd", grid=(N_TILES,),
        in_specs=[pl.BlockSpec((TM, WB), lambda i: (i, 0)),
                  pl.BlockSpec((TM, HGW), lambda i: (_lat(i), 0)),
                  pl.BlockSpec((TM, 128), lambda i: (_lat(i), 0)),
                  pl.BlockSpec((TM, 128), lambda i: (_lat(i), 0)),
                  _full((1, HG_DIM)), _full((1, ATW)), _full((1, KVW))],
        out_specs=[pl.BlockSpec((TM, HGW), lambda i: (_lat(i), 0)),
                   pl.BlockSpec((TM, ATW), lambda i: (_lat(i), 0)),
                   pl.BlockSpec((TM, ATW), lambda i: (i, 0)),
                   pl.BlockSpec((TM, ATW), lambda i: (i, 0))],
        out_shape=[jax.ShapeDtypeStruct((S, HGW), BF16), jax.ShapeDtypeStruct((S, ATW), BF16),
                   jax.ShapeDtypeStruct((T, ATW), BF16), jax.ShapeDtypeStruct((T, ATW), BF16)],
        compiler_params=_cp(("arbitrary",)),
    )(p_b, o, cos, sin, hnw, qnw, knw)


def _prep_bwd(p_b, o, cos, sin, hnw, qnw, knw, dy_hg, dq, dk_rep, dv_rep, carried=None):
    def body(p_ref, o_ref, cos_ref, sin_ref, hnw_ref, qnw_ref, knw_ref, dy_ref, dq_ref, dk_ref, dv_ref,
             dp_ref, do_ref, acc_ref):
        i = pl.program_id(0)

        @pl.when(i == 0)
        def _():
            acc_ref[...] = jnp.zeros_like(acc_ref)

        rep = _rep_matrix()
        ones_k = _group_ones(KVW, HEAD_DIM)

        def fold(v):
            hi = v.astype(BF16)
            lo = (v - hi.astype(F32)).astype(BF16)
            return _dot_nt(hi, rep) + _dot_nt(lo, rep)

        kr = p_ref[:, 1024:1152]
        krstd = lax.rsqrt(_group_mean(kr * kr, ones_k, HEAD_DIM) + EPS)
        khat = kr * krstd
        kw = knw_ref[...]
        dkro = fold(dk_ref[...])
        dv = fold(dv_ref[...])

        def k_back(dkn):
            dkhat = dkn * kw
            dkr = krstd * (dkhat - khat * _group_mean(dkhat * khat, ones_k, HEAD_DIM))
            acc_ref[2:3, 0:KVW] += jnp.sum(dkn * khat, axis=0, keepdims=True)
            dp_ref[:, 1024:1152] = dkr.astype(BF16)
            dp_ref[:, 1152:1280] = dv.astype(BF16)

        @pl.when(i == 0)
        def _():
            k_back(dkro)
            dp_ref[:, 0:1024] = jnp.zeros((TM, 1024), BF16)

        @pl.when(i > 0)
        def _():
            cs, sn = cos_ref[...], sin_ref[...]
            k_back(dkro * cs + _rope_partner(dkro * sn))
            ones_q = _group_ones(ATW, HEAD_DIM)
            qr = p_ref[:, 512:1024]
            qrstd = lax.rsqrt(_group_mean(qr * qr, ones_q, HEAD_DIM) + EPS)
            qhat = qr * qrstd
            dqro = dq_ref[...] * HEAD_DIM ** -0.5
            dqn = dqro * _tile_lanes(cs, 4) + _rope_partner(dqro * _tile_lanes(sn, 4))
            dqhat = dqn * qnw_ref[...]
            dqr = qrstd * (dqhat - qhat * _group_mean(dqhat * qhat, ones_q, HEAD_DIM))
            acc_ref[1:2, :] += jnp.sum(dqn * qhat, axis=0, keepdims=True)
            dp_ref[:, 512:1024] = dqr.astype(BF16)
            dws = jnp.zeros((1, HG_DIM), F32)
            for h in range(HG_HEADS):
                sl = slice(HG_DIM * h, HG_DIM * (h + 1))
                oh, gh, dy = o_ref[:, sl], p_ref[:, sl], dy_ref[:, sl]
                rstd = lax.rsqrt(jnp.mean(oh * oh, axis=-1, keepdims=True) + EPS)
                ohat = oh * rstd
                sg = _sigmoid(gh)
                dp_ref[:, sl] = (dy * (ohat * hnw_ref[...]) * (sg * (1.0 + gh * (1.0 - sg)))).astype(BF16)
                dn = dy * (gh * sg)
                dws = dws + jnp.sum(dn * ohat, axis=0, keepdims=True)
                dohat = dn * hnw_ref[...]
                do_ref[:, sl] = rstd * (dohat - ohat * jnp.mean(dohat * ohat, axis=-1, keepdims=True))
            acc_ref[0:1, 0:HG_DIM] += dws

    return _pcall(
        body, carried, name="prep_bwd", grid=(N_TILES,),
        in_specs=[pl.BlockSpec((TM, WB), lambda i: (i, 0)),
                  pl.BlockSpec((TM, HGW), lambda i: (_lat(i), 0)),
                  pl.BlockSpec((TM, 128), lambda i: (_lat(i), 0)),
                  pl.BlockSpec((TM, 128), lambda i: (_lat(i), 0)),
                  _full((1, HG_DIM)), _full((1, ATW)), _full((1, KVW)),
                  pl.BlockSpec((TM, HGW), lambda i: (_lat(i), 0)),
                  pl.BlockSpec((TM, ATW), lambda i: (_lat(i), 0)),
                  pl.BlockSpec((TM, ATW), lambda i: (i, 0)),
                  pl.BlockSpec((TM, ATW), lambda i: (i, 0))],
        out_specs=[pl.BlockSpec((TM, WB), lambda i: (i, 0)),
                   pl.BlockSpec((TM, HGW), lambda i: (_lat(i), 0)),
                   _full((8, ATW))],
        out_shape=[jax.ShapeDtypeStruct((T, WB), BF16), jax.ShapeDtypeStruct((S, HGW), F32),
                   jax.ShapeDtypeStruct((8, ATW), F32)],
        scratch_shapes=[], operands=[p_b, o, cos, sin, hnw, qnw, knw, dy_hg, dq, dk_rep, dv_rep])


NEG = -1e30
_CTX_BLOCKS = L // BLOCK


def _attn_window_specs():
    prev = pl.BlockSpec((BLOCK, ATW), lambda i: (jnp.maximum(i - 1, 0) + _CTX_BLOCKS, 0))
    own = pl.BlockSpec((BLOCK, ATW), lambda i: (i + _CTX_BLOCKS, 0))
    nxt = pl.BlockSpec((BLOCK, ATW), lambda i: (jnp.minimum(i + 1, N_BLOCKS - 1) + _CTX_BLOCKS, 0))
    return [prev, own, nxt, _full((L, ATW))]


def _attn_valid(i, heads, context):
    n_keys = 3 * BLOCK + (L if context else 0)
    qi = lax.broadcasted_iota(jnp.int32, (heads * BLOCK, n_keys), 0) % BLOCK
    kj = lax.broadcasted_iota(jnp.int32, (heads * BLOCK, n_keys), 1)
    window = ((jnp.abs(kj - BLOCK - qi) <= BLOCK) & ((kj >= BLOCK) | (i > 0))
              & ((kj < 2 * BLOCK) | (i < N_BLOCKS - 1)))
    return window | (kj >= 3 * BLOCK)


def _stack_heads(qg):
    lane = lax.broadcasted_iota(jnp.int32, (1, 256), 1) // HEAD_DIM
    return jnp.concatenate([jnp.where(lane == g, qg, jnp.zeros_like(qg)) for g in range(4)], axis=0)


def _unstack_heads(v4):
    lane = lax.broadcasted_iota(jnp.int32, (1, 256), 1) // HEAD_DIM
    out = jnp.where(lane == 0, v4[0:BLOCK], 0.0)
    for g in range(1, 4):
        out = out + jnp.where(lane == g, v4[g * BLOCK:(g + 1) * BLOCK], 0.0)
    return out


def _sink_rows(sink_ref, hk):
    return jnp.concatenate(
        [jnp.broadcast_to(sink_ref[0:1, 4 * hk + g:4 * hk + g + 1], (BLOCK, 1)) for g in range(4)], axis=0)


def _attn_fwd(q, k_rep, v_rep, sinks, carried=None):
    def body(q_ref, kp, ko, kn, kc, vp, vo, vn, vc, sink_ref, y_ref, lse_ref):
        i = pl.program_id(0)
        valid = _attn_valid(i, 1, True)
        lane8 = lax.broadcasted_iota(jnp.int32, (1, ATT_HEADS), 1)
        head_of_lane = lax.broadcasted_iota(jnp.int32, (1, 256), 1) // HEAD_DIM
        lse_out = jnp.zeros((BLOCK, ATT_HEADS), F32)
        for hk in range(KV_HEADS):
            sl = slice(256 * hk, 256 * (hk + 1))
            qg = q_ref[:, sl]
            keys = jnp.concatenate([kp[:, sl], ko[:, sl], kn[:, sl], kc[:, sl]], axis=0)
            vals = jnp.concatenate([vp[:, sl], vo[:, sl], vn[:, sl], vc[:, sl]], axis=0)
            yg = jnp.zeros((BLOCK, 256), F32)
            for g in range(4):
                q1 = jnp.where(head_of_lane == g, qg, jnp.zeros_like(qg))
                s = jnp.where(valid, _dot_nt(q1, keys), NEG)
                sink = sink_ref[0:1, 4 * hk + g:4 * hk + g + 1]
                m = jnp.maximum(jnp.max(s, axis=1, keepdims=True), sink)
                p = jnp.exp(s - m)
                den = jnp.sum(p, axis=1, keepdims=True) + jnp.exp(sink - m)
                o1 = _dot(p.astype(BF16), vals) * (1.0 / den)
                yg = yg + jnp.where(head_of_lane == g, o1, 0.0)
                lse_out = lse_out + jnp.where(lane8 == 4 * hk + g, m + jnp.log(den), 0.0)
            y_ref[:, sl] = yg.astype(BF16)
        lse_ref[...] = lse_out

    return _pcall(
        body, carried, name="attn_fwd", grid=(N_BLOCKS,),
        in_specs=[pl.BlockSpec((BLOCK, ATW), lambda i: (i, 0))] + _attn_window_specs()
        + _attn_window_specs() + [_full((1, ATT_HEADS))],
        out_specs=[pl.BlockSpec((BLOCK, ATW), lambda i: (i, 0)),
                   pl.BlockSpec((BLOCK, ATT_HEADS), lambda i: (i, 0))],
        out_shape=[jax.ShapeDtypeStruct((S, ATW), BF16), jax.ShapeDtypeStruct((S, ATT_HEADS), F32)],
        scratch_shapes=[],
        operands=[q, k_rep, k_rep, k_rep, k_rep, v_rep, v_rep, v_rep, v_rep, sinks])


def _attn_bwd(q, k_rep, v_rep, sinks, y_at, lse, dy, carried=None):
    def body(q_ref, kp, ko, kn, kc, vp, vo, vn, vc, sink_ref, y_ref, lse_ref, dy_ref,
             dq_ref, dk_ref, dv_ref, dsink_ref, dk_acc, dv_acc):
        i = pl.program_id(0)

        @pl.when(i == 0)
        def _():
            dk_acc[...] = jnp.zeros_like(dk_acc)
            dv_acc[...] = jnp.zeros_like(dv_acc)
            dk_ref[pl.ds(0, L), :] = jnp.zeros((L, ATW), F32)
            dv_ref[pl.ds(0, L), :] = jnp.zeros((L, ATW), F32)
            dsink_ref[...] = jnp.zeros_like(dsink_ref)

        valid = _attn_valid(i, 4, False)
        lane8 = lax.broadcasted_iota(jnp.int32, (1, ATT_HEADS), 1)
        w0 = pl.multiple_of(i * BLOCK, BLOCK)
        dsink = jnp.zeros((1, ATT_HEADS), F32)
        for hk in range(KV_HEADS):
            sl = slice(256 * hk, 256 * (hk + 1))
            q4 = _stack_heads(q_ref[:, sl])
            do4f = _stack_heads(dy_ref[:, sl])
            o4 = _stack_heads(y_ref[:, sl]).astype(F32)
            do4 = do4f.astype(BF16)
            kl = jnp.concatenate([kp[:, sl], ko[:, sl], kn[:, sl]], axis=0)
            vl = jnp.concatenate([vp[:, sl], vo[:, sl], vn[:, sl]], axis=0)
            lse4 = jnp.concatenate(
                [jnp.sum(jnp.where(lane8 == 4 * hk + g, lse_ref[...], 0.0), axis=1, keepdims=True)
                 for g in range(4)], axis=0)
            p_loc = jnp.where(valid, jnp.exp(_dot_nt(q4, kl) - lse4), 0.0)
            p_ctx = jnp.exp(_dot_nt(q4, kc[:, sl]) - lse4)
            delta = jnp.sum(do4f * o4, axis=1, keepdims=True)
            ds_loc = (p_loc * (_dot_nt(do4, vl) - delta)).astype(BF16)
            ds_ctx = (p_ctx * (_dot_nt(do4, vc[:, sl]) - delta)).astype(BF16)
            dq_ref[:, sl] = _unstack_heads(_dot(ds_loc, kl) + _dot(ds_ctx, kc[:, sl]))
            dk_acc[pl.ds(w0, 3 * BLOCK), sl] += _dot_tn(ds_loc, q4)
            dv_acc[pl.ds(w0, 3 * BLOCK), sl] += _dot_tn(p_loc.astype(BF16), do4)
            dk_ref[pl.ds(0, L), sl] += _dot_tn(ds_ctx, q4)
            dv_ref[pl.ds(0, L), sl] += _dot_tn(p_ctx.astype(BF16), do4)
            p_sink = jnp.exp(_sink_rows(sink_ref, hk) - lse4)
            for g in range(4):
                rows = slice(g * BLOCK, (g + 1) * BLOCK)
                dsink = dsink + jnp.where(lane8 == 4 * hk + g,
                                          -jnp.sum(p_sink[rows] * delta[rows], axis=0, keepdims=True), 0.0)
        dsink_ref[...] += dsink

        @pl.when(i == N_BLOCKS - 1)
        def _():
            dk_ref[pl.ds(L, S), :] = dk_acc[pl.ds(BLOCK, S), :]
            dv_ref[pl.ds(L, S), :] = dv_acc[pl.ds(BLOCK, S), :]

    row_q = pl.BlockSpec((BLOCK, ATW), lambda i: (i, 0))
    return _pcall(
        body, carried, name="attn_bwd", grid=(N_BLOCKS,),
        in_specs=[row_q] + _attn_window_specs() + _attn_window_specs()
        + [_full((1, ATT_HEADS)), row_q, pl.BlockSpec((BLOCK, ATT_HEADS), lambda i: (i, 0)), row_q],
        out_specs=[row_q, _full((T, ATW)), _full((T, ATW)), _full((1, ATT_HEADS))],
        out_shape=[jax.ShapeDtypeStruct((S, ATW), F32), jax.ShapeDtypeStruct((T, ATW), F32),
                   jax.ShapeDtypeStruct((T, ATW), F32), jax.ShapeDtypeStruct((1, ATT_HEADS), F32)],
        scratch_shapes=[pltpu.VMEM((S + 2 * BLOCK, ATW), F32), pltpu.VMEM((S + 2 * BLOCK, ATW), F32)],
        operands=[q, k_rep, k_rep, k_rep, k_rep, v_rep, v_rep, v_rep, v_rep, sinks, y_at, lse, dy])


def _merge_fwd(y_hg, y_at, p_c, x, w_bh, w_ba, w_out, g1, nfw, sh2, sc2, carried=None):
    def body(yh_ref, ya_ref, g_ref, x_ref, wbh_ref, wba_ref, wo_ref, g1_ref, nfw_ref, sh_ref, sc_ref,
             mx_ref, r_ref, x1_ref, h2_ref):
        a = _dot_nt(yh_ref[...], wbh_ref[...])
        b = _dot_nt(ya_ref[...], wba_ref[...])
        mixed = (_sigmoid(g_ref[:, :D]) * a + _sigmoid(g_ref[:, D:]) * b).astype(BF16)
        r = _dot(mixed, wo_ref[...])
        x1 = x_ref[...] + g1_ref[...] * r
        mx_ref[...] = mixed
        r_ref[...] = r
        x1_ref[...] = x1
        h2_ref[...] = _rms_mod(x1, nfw_ref[...], sh_ref[...], sc_ref[...]).astype(BF16)

    row = lambda w: pl.BlockSpec((TM, w), lambda i: (i, 0))
    vec = _full((1, D))
    return _pcall(
        body, carried, name="merge_fwd", grid=(N_LAT_TILES,),
        in_specs=[row(HGW), row(ATW), row(WC), row(D), _VMEM_WHOLE, _VMEM_WHOLE, _VMEM_WHOLE,
                  vec, vec, vec, vec],
        out_specs=[row(D)] * 4,
        out_shape=[jax.ShapeDtypeStruct((S, D), dt) for dt in (BF16, F32, F32, BF16)],
        scratch_shapes=[], operands=[y_hg, y_at, p_c, x, w_bh, w_ba, w_out, g1, nfw, sh2, sc2])


def _merge_bwd(dx1, r, y_hg, y_at, p_c, w_bh, w_ba, w_out, g1, carried=None):
    def body(dx_ref, r_ref, yh_ref, ya_ref, g_ref, wbh_ref, wba_ref, wo_ref, g1_ref,
             dr_ref, da_ref, db_ref, dg_ref, dyh_ref, dya_ref, acc_ref):
        @pl.when(pl.program_id(0) == 0)
        def _():
            acc_ref[...] = jnp.zeros_like(acc_ref)

        dx1v = dx_ref[...]
        acc_ref[0:1, :] += jnp.sum(dx1v * r_ref[...], axis=0, keepdims=True)
        dr = (g1_ref[...] * dx1v).astype(BF16)
        dr_ref[...] = dr
        dmix = _dot_nt(dr, wo_ref[...])
        sh, sa = _sigmoid(g_ref[:, :D]), _sigmoid(g_ref[:, D:])
        da = (dmix * sh).astype(BF16)
        db = (dmix * sa).astype(BF16)
        da_ref[...] = da
        db_ref[...] = db
        dg_ref[:, :D] = (dmix * _dot_nt(yh_ref[...], wbh_ref[...]) * sh * (1.0 - sh)).astype(BF16)
        dg_ref[:, D:] = (dmix * _dot_nt(ya_ref[...], wba_ref[...]) * sa * (1.0 - sa)).astype(BF16)
        dyh_ref[...] = _dot(da, wbh_ref[...])
        dya_ref[...] = _dot(db, wba_ref[...])

    row = lambda w: pl.BlockSpec((TM, w), lambda i: (i, 0))
    return _pcall(
        body, carried, name="merge_bwd", grid=(N_LAT_TILES,),
        in_specs=[row(D), row(D), row(HGW), row(ATW), row(WC), _VMEM_WHOLE, _VMEM_WHOLE, _VMEM_WHOLE,
                  _full((1, D))],
        out_specs=[row(D), row(D), row(D), row(WC), row(HGW), row(ATW), _full((8, D))],
        out_shape=[jax.ShapeDtypeStruct((S, D), BF16), jax.ShapeDtypeStruct((S, D), BF16),
                   jax.ShapeDtypeStruct((S, D), BF16), jax.ShapeDtypeStruct((S, WC), BF16),
                   jax.ShapeDtypeStruct((S, HGW), F32), jax.ShapeDtypeStruct((S, ATW), F32),
                   jax.ShapeDtypeStruct((8, D), F32)],
        scratch_shapes=[], operands=[dx1, r, y_hg, y_at, p_c, w_bh, w_ba, w_out, g1])


def _ffn_fused(x1, h2, tgt, w_gate, w_up, w_down, g2, nfw, sc2):
    def body(x1_ref, h2_ref, t_ref, wg_ref, wu_ref, wd_ref, g2_ref, nfw_ref, sc_ref,
             act_ref, dgt_ref, dup_ref, df_ref, dx_ref, acc_ref, gs, us):
        @pl.when(pl.program_id(0) == 0)
        def _():
            acc_ref[...] = jnp.zeros_like(acc_ref)

        h2 = h2_ref[...]
        whole = lambda w_ref: w_ref[...].reshape(D_FF, D)
        wide = lambda t_ref: jnp.concatenate([t_ref[j] for j in range(N_FF_TILES)], axis=1)
        for j in range(N_FF_TILES):
            g = _dot_nt(h2, wg_ref[j])
            u = _dot_nt(h2, wu_ref[j])
            gs[j] = g
            us[j] = u
            act_ref[j] = (g * _sigmoid(g) * u).astype(BF16)
        f = _dot(wide(act_ref), whole(wd_ref))
        x1v = x1_ref[...]
        g2 = g2_ref[...]
        diff = x1v + g2 * f - t_ref[...]
        dy = diff * (1.0 / D)
        df = (g2 * dy).astype(BF16)
        df_ref[...] = df
        dact_all = _dot_nt(df, whole(wd_ref))
        for j in range(N_FF_TILES):
            g, u = gs[j], us[j]
            sg = _sigmoid(g)
            dact = dact_all[:, j * FF_TILE:(j + 1) * FF_TILE]
            dgt_ref[j] = (dact * u * (sg * (1.0 + g * (1.0 - sg)))).astype(BF16)
            dup_ref[j] = (dact * (g * sg)).astype(BF16)
        dh2 = _dot(wide(dgt_ref), whole(wg_ref)) + _dot(wide(dup_ref), whole(wu_ref))
        dx, dsh, dsc, dnw = _rms_mod_bwd(x1v, nfw_ref[...], sc_ref[...], dh2)
        dx_ref[...] = dy + dx
        acc_ref[0:1, :] += dsh
        acc_ref[1:2, :] += dsc
        acc_ref[2:3, :] += dnw
        acc_ref[3:4, :] += jnp.sum(dy * f, axis=0, keepdims=True)
        acc_ref[4:5, :] += 0.5 * jnp.sum(jnp.sum(diff * diff, axis=1, keepdims=True), axis=0,
                                         keepdims=True) * (1.0 / D)

    row = lambda dt_w: pl.BlockSpec((TM, dt_w), lambda i: (i, 0))
    blk = pl.BlockSpec((N_FF_TILES, TM, FF_TILE), lambda i: (0, i, 0))
    vec = _full((1, D))
    return pl.pallas_call(
        body, name="ffn_fused", grid=(N_LAT_TILES,),
        in_specs=[row(D), row(D), row(D), _VMEM_WHOLE, _VMEM_WHOLE, _VMEM_WHOLE, vec, vec, vec],
        out_specs=[blk, blk, blk, row(D), row(D), _full((8, D))],
        out_shape=[jax.ShapeDtypeStruct((N_FF_TILES, S, FF_TILE), BF16)] * 3
        + [jax.ShapeDtypeStruct((S, D), BF16), jax.ShapeDtypeStruct((S, D), F32),
           jax.ShapeDtypeStruct((8, D), F32)],
        scratch_shapes=[pltpu.VMEM((N_FF_TILES, TM, FF_TILE), F32), pltpu.VMEM((N_FF_TILES, TM, FF_TILE), F32)],
        compiler_params=_cp(("arbitrary",)),
    )(x1, h2, tgt, w_gate, w_up, w_down, g2, nfw, sc2)


def _proj_bc(h_all, w_b, w_c, carried=None):
    def body(h_ref, wb_ref, wc_ref, pb_ref, pc_ref):
        h = h_ref[...]
        pb_ref[...] = _dot_nt(h, wb_ref[...])

        @pl.when(pl.program_id(0) > 0)
        def _():
            pc_ref[...] = _dot_nt(h, wc_ref[...])

    return _pcall(
        body, carried, name="proj_bc", grid=(N_TILES,),
        in_specs=[pl.BlockSpec((TM, D), lambda i: (i, 0)), _VMEM_WHOLE, _VMEM_WHOLE],
        out_specs=[pl.BlockSpec((TM, WB), lambda i: (i, 0)), pl.BlockSpec((TM, WC), lambda i: (_lat(i), 0))],
        out_shape=[jax.ShapeDtypeStruct((T, WB), F32), jax.ShapeDtypeStruct((S, WC), F32)],
        scratch_shapes=[], operands=[h_all, w_b, w_c])


def _input_bwd(dp_a, dp_b, dp_c, w_a, w_b, w_c, ctx, x, dx1, nw, sh, sc, carried=None):
    def body(da_ref, db_ref, dc_ref, wa_ref, wb_ref, wc_ref, ctx_ref, x_ref, dx1_ref, nw_ref, sh_ref,
             sc_ref, gx_ref, acc_ref):
        i = pl.program_id(0)

        @pl.when(i == 0)
        def _():
            acc_ref[...] = jnp.zeros_like(acc_ref)

        dh = _dot(da_ref[...], wa_ref[...]) + _dot(db_ref[...], wb_ref[...])

        @pl.when(i == 0)
        def _():
            _, dsh, dsc, dnw = _rms_mod_bwd(ctx_ref[...], nw_ref[...], sc_ref[0:1, :], dh)
            acc_ref[3:4, :] += dsh
            acc_ref[4:5, :] += dsc
            acc_ref[2:3, :] += dnw

        @pl.when(i > 0)
        def _():
            dhl = dh + _dot(dc_ref[...], wc_ref[...])
            dx, dsh, dsc, dnw = _rms_mod_bwd(x_ref[...], nw_ref[...], sc_ref[1:2, :], dhl)
            gx_ref[...] = dx1_ref[...] + dx
            acc_ref[0:1, :] += dsh
            acc_ref[1:2, :] += dsc
            acc_ref[2:3, :] += dnw

    lat = lambda w: pl.BlockSpec((TM, w), lambda i: (_lat(i), 0))
    return _pcall(
        body, carried, name="input_bwd", grid=(N_TILES,),
        in_specs=[pl.BlockSpec((TM, WA), lambda i: (i, 0)), pl.BlockSpec((TM, WB), lambda i: (i, 0)),
                  lat(WC), _VMEM_WHOLE, _VMEM_WHOLE, _VMEM_WHOLE, _full((TM, D)), lat(D), lat(D),
                  _full((1, D)), _full((2, D)), _full((2, D))],
        out_specs=[lat(D), _full((8, D))],
        out_shape=[jax.ShapeDtypeStruct((S, D), F32), jax.ShapeDtypeStruct((8, D), F32)],
        scratch_shapes=[], operands=[dp_a, dp_b, dp_c, w_a, w_b, w_c, ctx, x, dx1, nw, sh, sc])


_C1 = 1.0 - ADAM_B1 ** ADAM_STEP
_C2 = 1.0 - ADAM_B2 ** ADAM_STEP


def _adamw_math(w, g, m, v):
    m = ADAM_B1 * m + (1.0 - ADAM_B1) * g
    v = ADAM_B2 * v + (1.0 - ADAM_B2) * (g * g)
    m_hat = m / _C1
    v_hat = v / _C2
    delta = -ADAM_LR * (m_hat / (jnp.sqrt(v_hat) + ADAM_EPS) + ADAM_WD * w)
    return delta, m, v


def _adamw_sharded(terms, w, m, v, name, tr, extra=None):
    rows, cols = w.shape

    def body(*refs):
        t_ref, w_ref, m_ref, v_ref = refs[:4]
        g_ref, d_ref, nm_ref, nv_ref = refs[-4:]
        g = t_ref[0].astype(F32)
        for s in range(1, N_CHIPS):
            g = g + t_ref[s].astype(F32)
        if extra is not None:
            g = g + refs[4][...].astype(F32)
        g_ref[...] = g
        d_ref[...], nm_ref[...], nv_ref[...] = _adamw_math(w_ref[...], g, m_ref[...], v_ref[...])

    blk = pl.BlockSpec((tr, cols), lambda i: (i, 0))
    return pl.pallas_call(
        body, name=name, grid=(rows // tr,),
        in_specs=[pl.BlockSpec((N_CHIPS, tr, cols), lambda i: (0, i, 0)), blk, blk, blk]
        + ([blk] if extra is not None else []),
        out_specs=[blk] * 4,
        out_shape=[jax.ShapeDtypeStruct((rows, cols), F32)] * 4,
        compiler_params=_cp(("parallel",)),
    )(terms, w, m, v, *([extra] if extra is not None else []))


def _adamw_plain(g, w, m, v, name):
    def body(g_ref, w_ref, m_ref, v_ref, d_ref, nm_ref, nv_ref):
        d_ref[...], nm_ref[...], nv_ref[...] = _adamw_math(w_ref[...], g_ref[...], m_ref[...], v_ref[...])

    return pl.pallas_call(
        body, name=name, in_specs=[_VMEM_WHOLE] * 4, out_specs=[_VMEM_WHOLE] * 3,
        out_shape=[jax.ShapeDtypeStruct(w.shape, F32)] * 3,
        compiler_params=_cp(),
    )(g, w, m, v)


SMALL_ROWS = 16
R_DMOD, R_DCTX, R_NMIX, R_NFFN, R_MISC, R_DLB, R_BADA01 = 0, 6, 8, 9, 10, 11, 13
M_HNW, M_QNW, M_KNW, M_SINK, M_LOSS = 0, 128, 256, 384, 512


def _pack_small(acc_in, acc_mg, acc_ffn, acc_prep, dsink, dlb):
    def body(in_ref, mg_ref, ff_ref, pp_ref, ds_ref, dlb_ref, o_ref):
        o_ref[...] = jnp.zeros_like(o_ref)
        o_ref[0:2, :] = in_ref[0:2, :]
        o_ref[2:3, :] = mg_ref[0:1, :]
        o_ref[3:5, :] = ff_ref[0:2, :]
        o_ref[5:6, :] = ff_ref[3:4, :]
        o_ref[6:8, :] = in_ref[3:5, :]
        o_ref[8:9, :] = in_ref[2:3, :]
        o_ref[9:10, :] = ff_ref[2:3, :]
        o_ref[10:11, M_HNW:M_HNW + HG_DIM] = pp_ref[0:1, 0:HG_DIM]
        r = lax.broadcasted_iota(jnp.int32, (ATW, 128), 0)
        c = lax.broadcasted_iota(jnp.int32, (ATW, 128), 1)
        fold = jnp.where((r % HEAD_DIM == c) & (c < HEAD_DIM), 1.0, 0.0).astype(BF16)
        qk = jnp.concatenate([pp_ref[1:2, :], pp_ref[2:3, :], jnp.zeros((6, ATW), F32)], axis=0)
        folded = _dot_exact_rhs01(qk, fold)
        o_ref[10:11, M_QNW:M_QNW + 128] = folded[0:1, :]
        o_ref[10:11, M_KNW:M_KNW + 128] = folded[1:2, :]
        o_ref[10:11, M_SINK:M_SINK + ATT_HEADS] = ds_ref[...]
        o_ref[10:11, M_LOSS:M_LOSS + 128] = ff_ref[4:5, 0:128]
        o_ref[11:13, 0:HGW] = dlb_ref[...]

    return pl.pallas_call(
        body, name="pack_small", in_specs=[_VMEM_WHOLE] * 6, out_specs=_VMEM_WHOLE,
        out_shape=jax.ShapeDtypeStruct((SMALL_ROWS, D), F32), compiler_params=_cp(),
    )(acc_in, acc_mg, acc_ffn, acc_prep, dsink, dlb)


def _sum_small(gathered):
    def body(g_ref, o_ref):
        tot = g_ref[0]
        for s in range(1, N_DEV):
            tot = tot + g_ref[s]
        o_ref[...] = tot
        o_ref[R_BADA01:R_BADA01 + 2, :] = tot[0:2, :] + tot[R_DCTX:R_DCTX + 2, :]

    return pl.pallas_call(
        body, name="sum_small", in_specs=[_VMEM_WHOLE], out_specs=_VMEM_WHOLE,
        out_shape=jax.ShapeDtypeStruct((SMALL_ROWS, D), F32), compiler_params=_cp(),
    )(gathered)


_REP_NAMES = ("b_ada", "c_ctx", "norm_mix_w", "norm_ffn_w", "hgrn_norm_w", "q_norm_w", "k_norm_w", "attn_sinks")


def _adamw_replicated(tot, g_c_ctx, ws, ms, vs):
    n = len(_REP_NAMES)

    def body(*refs):
        tot_ref, gc_ref = refs[0], refs[1]
        w_refs, m_refs, v_refs = refs[2:2 + n], refs[2 + n:2 + 2 * n], refs[2 + 2 * n:2 + 3 * n]
        outs = refs[2 + 3 * n:]
        row = lambda r: tot_ref[r:r + 1, :]
        misc = row(R_MISC)
        grads = [jnp.concatenate([row(R_BADA01), row(R_BADA01 + 1)] + [row(k) for k in range(2, 6)], axis=1),
                 gc_ref[...], row(R_NMIX), row(R_NFFN),
                 misc[:, M_HNW:M_HNW + HG_DIM], misc[:, M_QNW:M_QNW + HEAD_DIM],
                 misc[:, M_KNW:M_KNW + HEAD_DIM], misc[:, M_SINK:M_SINK + ATT_HEADS]]
        for k in range(n):
            outs[k][...] = grads[k]
            outs[n + k][...], outs[2 * n + k][...], outs[3 * n + k][...] = _adamw_math(
                w_refs[k][...], grads[k], m_refs[k][...], v_refs[k][...])

    shapes = [jax.ShapeDtypeStruct(w.shape, F32) for w in ws]
    return pl.pallas_call(
        body, name="adamw_replicated", in_specs=[_VMEM_WHOLE] * (2 + 3 * n), out_specs=[_VMEM_WHOLE] * (4 * n),
        out_shape=shapes * 4, compiler_params=_cp(),
    )(tot, g_c_ctx, *ws, *ms, *vs)


def _lb_grads(dlb, lbl):
    def body(d_ref, l_ref, o_ref):
        for d in (0, 1):
            ll = l_ref[d]
            lb = _sigmoid(ll[0:1, :] - ll[1:2, :])
            t = d_ref[d:d + 1, :] * lb * (1.0 - lb)
            o_ref[d, 0:1, :] = t
            o_ref[d, 1:2, :] = -t

    return pl.pallas_call(
        body, name="lb_grads", in_specs=[_VMEM_WHOLE] * 2, out_specs=_VMEM_WHOLE,
        out_shape=jax.ShapeDtypeStruct((2, 2, HGW), F32), compiler_params=_cp(),
    )(dlb, lbl)


def _c_ctx_grad(terms, c_ctx):
    def body(t_ref, c_ref, o_ref):
        tot = t_ref[0, 8:9, :]
        for s in range(1, N_DEV):
            tot = tot + t_ref[s, 8:9, :]
        cv = c_ref[...]
        sg = _sigmoid(cv)
        o_ref[...] = tot * (sg * (1.0 + cv * (1.0 - sg)))

    return pl.pallas_call(
        body, name="c_ctx_grad", in_specs=[_VMEM_WHOLE] * 2, out_specs=_VMEM_WHOLE,
        out_shape=jax.ShapeDtypeStruct((1, D), F32), compiler_params=_cp(),
    )(terms, c_ctx)


def _in_perm():
    fz, bz, inp, kk, vv, qhg, ghg, qat, gates = 0, 512, 1024, 1536, 1664, 1792, 2304, 2816, 3328
    cols = []
    for h in range(HG_HEADS):
        for base in (fz, bz, inp, qhg):
            cols += list(range(base + 128 * h, base + 128 * (h + 1)))
    cols += list(range(ghg, ghg + 512)) + list(range(qat, qat + 512))
    cols += list(range(kk, kk + 128)) + list(range(vv, vv + 128))
    cols += list(range(gates, gates + 2048))
    return np.asarray(cols, np.int32)


_PERM = _in_perm()


_PIECES = {"a": (0, WA, 128), "b": (WA, WB, 256), "c": (WA + WB, WC, 256)}


def _block_table(piece):
    lo, n, blk = _PIECES[piece]
    starts = [int(_PERM[r]) for r in range(lo, lo + n, blk)]
    assert all(s % blk == 0 and np.array_equal(_PERM[r:r + blk], np.arange(s, s + blk))
               for s, r in zip(starts, range(lo, lo + n, blk)))
    return jnp.asarray([s // blk for s in starts], jnp.int32), blk


def _pick_row_blocks(x, table, blk, name):
    cols = x.shape[1]

    def body(t_ref, x_ref, o_ref):
        o_ref[...] = x_ref[...]

    return pl.pallas_call(
        body, name=name,
        grid_spec=pltpu.PrefetchScalarGridSpec(
            num_scalar_prefetch=1, grid=(table.shape[0],),
            in_specs=[pl.BlockSpec((blk, cols), lambda i, t: (t[i], 0))],
            out_specs=pl.BlockSpec((blk, cols), lambda i, t: (i, 0))),
        out_shape=jax.ShapeDtypeStruct((table.shape[0] * blk, cols), x.dtype),
        compiler_params=_cp(("arbitrary",)),
    )(table, x)


def _place_row_blocks(x, table, blk, into, out_rows, name):
    cols = x.shape[1]

    def body(t_ref, x_ref, *rest):
        rest[-1][...] = x_ref[...]

    operands, in_specs, aliases = [table, x], [pl.BlockSpec((blk, cols), lambda i, t: (i, 0))], {}
    if into is not None:
        operands.append(into)
        in_specs.append(_ANY)
        aliases = {2: 0}
    return pl.pallas_call(
        body, name=name,
        grid_spec=pltpu.PrefetchScalarGridSpec(
            num_scalar_prefetch=1, grid=(table.shape[0],), in_specs=in_specs,
            out_specs=pl.BlockSpec((blk, cols), lambda i, t: (t[i], 0))),
        out_shape=jax.ShapeDtypeStruct((out_rows, cols), x.dtype),
        input_output_aliases=aliases,
        compiler_params=_cp(("arbitrary",)),
    )(*operands)


def _cols_from_blocks(g):
    return jnp.transpose(g, (1, 0, 2)).reshape(g.shape[1], N_DEV * g.shape[2])


def _local_step(x2, ctx2, tgt, lbl, sh_in, sc_in, gate1, sh2, sc2, gate2, norm_mix_w, norm_ffn_w,
                hgrn_norm_w, q_norm_w, k_norm_w, attn_sinks, w_a, w_b, w_c, s_bh, s_ba, s_out,
                s_gate, s_up, s_down):
    first_last = lambda n: [(0, True), (n - 1, False)]
    h_all = _norm_mod_all(ctx2, x2, norm_mix_w, sh_in, sc_in)
    p_a = _mm_nt(h_all, w_a, tm=768, tn=1024, out_dtype=F32, name="proj_a")
    (o, st), (g_gate, g_bh, g_ba) = _hgrn_fwd(
        p_a, lbl, (_gather_comm_relayed([s_gate, s_bh, s_ba]),
                   [(0, True), (HG_HEADS - 2, True), (HG_HEADS - 1, False)]))
    (p_b, p_c), (g_out,) = _proj_bc(
        h_all, w_b, w_c, (_gather_comm_relayed([s_out]), [(0, True), (N_TILES - 4, True), (N_TILES - 1, False)]))
    cos, sin = _rope_tables()
    qnw_t, knw_t = jnp.tile(q_norm_w, (1, ATT_HEADS)), jnp.tile(k_norm_w, (1, KV_HEADS))
    y_hg, qn, k_rep, v_rep = _prep_fwd(p_b, o, cos, sin, hgrn_norm_w, qnw_t, knw_t)
    (y_at, lse), (g_up, g_down) = _attn_fwd(
        qn, k_rep, v_rep, attn_sinks,
        (_gather_comm_relayed([s_up, s_down]), [(0, True), (N_BLOCKS - 6, True), (N_BLOCKS - 1, False)]))
    w_bh, w_ba, w_o = g_bh.reshape(D, HGW), g_ba.reshape(D, ATW), g_out.reshape(D, D)
    (mixed, r, x1, h2), _ = _merge_fwd(
        y_hg, y_at, p_c, x2, w_bh, w_ba, w_o, gate1, norm_ffn_w, sh2, sc2)
    g_gate, g_up, g_down = [g.reshape(N_FF_TILES, FF_TILE, D) for g in (g_gate, g_up, g_down)]

    act, d_gate, d_up, d_f, dx1, acc_ffn = _ffn_fused(x1, h2, tgt, g_gate, g_up, g_down, gate2,
                                                      norm_ffn_w, sc2)
    by_chip = lambda t: t.reshape((N_CHIPS, 2) + t.shape[1:])
    ff_by_chip = lambda t: t.reshape(N_CHIPS, 2, FF_BLK, D)
    t_down, _ = _mm_tn_blocked(act, d_f, "grad_down")
    t_down = ff_by_chip(t_down)
    t_gate, (f_down,) = _mm_tn_blocked(d_gate, h2, "grad_gate", (_sibling_comm([t_down]), first_last(N_FF_TILES)))
    t_gate = ff_by_chip(t_gate)
    t_up, (f_gate,) = _mm_tn_blocked(d_up, h2, "grad_up", (_sibling_comm([t_gate]), first_last(N_FF_TILES)))
    t_up = ff_by_chip(t_up)

    (d_r, d_a, d_b, dp_c, dy_hg, dy_at, acc_mg), (f_up,) = _merge_bwd(
        dx1, r, y_hg, y_at, p_c, w_bh, w_ba, w_o, gate1, (_sibling_comm([t_up]), first_last(N_LAT_TILES)))
    c_down, c_gate, c_up = [_pair_sum(t, f, "pair_sum_" + nm) for t, f, nm in
                            ((t_down, f_down, "down"), (t_gate, f_gate, "gate"), (t_up, f_up, "up"))]
    t_out = _mm_tn(mixed, d_r, tk=512, nk=4, tm=512, tn=1024, out_dtype=BF16, name="grad_out")
    t_bh = _mm_tn(d_a, y_hg, tk=512, nk=4, tm=512, tn=512, out_dtype=BF16, name="grad_bh")
    t_ba = _mm_tn(d_b, y_at, tk=512, nk=4, tm=512, tn=512, out_dtype=BF16, name="grad_ba")
    t_bh, t_ba, t_out = [by_chip(t.reshape(N_DEV, D // N_DEV, t.shape[1])) for t in (t_bh, t_ba, t_out)]
    (dq, dk_rep, dv_rep, dsink), (r_up,) = _attn_bwd(
        qn, k_rep, v_rep, attn_sinks, y_at, lse, dy_at, (_chip_comm([c_up]), first_last(N_BLOCKS)))
    (dp_b, d_o, acc_prep), (f_bh, f_ba, f_out) = _prep_bwd(
        p_b, o, cos, sin, hgrn_norm_w, qnw_t, knw_t, dy_hg, dq, dk_rep, dv_rep,
        (_sibling_comm([t_bh, t_ba, t_out]), first_last(N_TILES)))
    c_bh, c_ba, c_out = [_pair_sum(t, f, "pair_sum_" + nm) for t, f, nm in
                         ((t_bh, f_bh, "bh"), (t_ba, f_ba, "ba"), (t_out, f_out, "out"))]
    (dp_a, dlb), (r_bh, r_ba, r_out, r_down, r_gate) = _hgrn_bwd(
        p_a, lbl, d_o, st, (_chip_comm([c_bh, c_ba, c_out, c_down, c_gate]), first_last(HG_HEADS)))
    t_a = _mm_tn(dp_a, h_all, tk=768, nk=3, tm=1024, tn=1024, out_dtype=BF16, name="grad_in_a")
    t_b = _mm_tn(dp_b, h_all, tk=768, nk=3, tm=640, tn=1024, out_dtype=BF16, name="grad_in_b")
    t_c = _mm_tn(dp_c, h_all, tk=256, nk=8, b_off=1, tm=1024, tn=1024, out_dtype=BF16, name="grad_in_c")
    t_in = None
    for piece, nm in ((t_a, "a"), (t_b, "b"), (t_c, "c")):
        t_in = _place_row_blocks(piece, *_block_table(nm), t_in, IN_COLS, "order_terms_" + nm)
    t_in = by_chip(t_in.reshape(N_DEV, IN_BLK, D))
    (f_in,) = _run_comm(_sibling_comm([t_in]), "scatter_in_sibling")
    c_in = _pair_sum(t_in, f_in, "pair_sum_in")
    sems, c_in, land, token = _chip_exchange_start(c_in, jnp.zeros(c_in.shape, c_in.dtype))
    (grad_x, acc_in), _ = _input_bwd(dp_a, dp_b, dp_c, w_a, w_b, w_c, ctx2, x2, dx1,
                                     norm_mix_w + token[0, 0], sh_in, sc_in)
    small = _pack_small(acc_in, acc_mg, acc_ffn, acc_prep, dsink, dlb)
    return grad_x, small, [r_bh, r_ba, r_out, r_gate, r_up, r_down], (sems, c_in, land)


def kernel(x, c, ctx, c_ctx, w_ada, b_ada, norm_mix_w, norm_ffn_w, w_in, hgrn_lb_logits, hgrn_norm_w, q_norm_w, k_norm_w, attn_sinks, w_branch_hgrn, w_branch_attn, w_out, w_ffn_gate, w_ffn_up, w_ffn_down, loss_target, m_c_ctx, m_w_ada, m_b_ada, m_norm_mix_w, m_norm_ffn_w, m_w_in, m_hgrn_lb_logits, m_hgrn_norm_w, m_q_norm_w, m_k_norm_w, m_attn_sinks, m_w_branch_hgrn, m_w_branch_attn, m_w_out, m_w_ffn_gate, m_w_ffn_up, m_w_ffn_down, v_c_ctx, v_w_ada, v_b_ada, v_norm_mix_w, v_norm_ffn_w, v_w_in, v_hgrn_lb_logits, v_hgrn_norm_w, v_q_norm_w, v_k_norm_w, v_attn_sinks, v_w_branch_hgrn, v_w_branch_attn, v_w_out, v_w_ffn_gate, v_w_ffn_up, v_w_ffn_down):
    me = 4 * lax.axis_index("x") + 2 * lax.axis_index("y") + lax.axis_index("c")
    x2, ctx2, tgt = x[0], ctx[0], loss_target[0]
    w_ada2, w_in2 = w_ada[0], w_in[0]

    cond = jnp.zeros((8, D), F32).at[0].set(c[0]).at[1, :256].set(hgrn_lb_logits.reshape(256))
    b_cols = lax.dynamic_slice(b_ada, (0, me * ADA_BLK), (1, ADA_BLK))
    g0, cc, g1, g_in = _prologue(cond, c_ctx.reshape(1, D), w_ada2, b_cols, w_in2.T.astype(BF16))
    lbl = jnp.transpose(g0[:, 1, :256].reshape(N_DEV, 2, 2, 64), (1, 2, 0, 3)).reshape(2, 2, HGW)
    mod_all = _cols_from_blocks(g1)
    mod = lax.dynamic_slice(mod_all, (me, 0), (1, 6 * D)).reshape(6, D)
    mod_c = mod_all[8].reshape(6, D)
    sh1, sc1, gate1, sh2, sc2, gate2 = [mod[k:k + 1] for k in range(6)]
    sh_in = jnp.concatenate([mod_c[0:1], sh1], axis=0)
    sc_in = jnp.concatenate([mod_c[1:2], sc1], axis=0)

    shards = [w_branch_hgrn[0].T, w_branch_attn[0].T, w_out[0], w_ffn_gate[0].T, w_ffn_up[0].T, w_ffn_down[0]]
    w_in_t = g_in.reshape(IN_COLS, D)
    w_a, w_b, w_c = [_pick_row_blocks(w_in_t, *_block_table(nm), "order_w_" + nm) for nm in "abc"]

    grad_x, small, (r_bh, r_ba, r_out, r_gate, r_up, r_down), pending_in = _local_step(
        x2, ctx2, tgt, lbl, sh_in, sc_in, gate1, sh2, sc2, gate2, norm_mix_w, norm_ffn_w, hgrn_norm_w,
        q_norm_w, k_norm_w, attn_sinks, w_a, w_b, w_c, *[s.astype(BF16) for s in shards])

    big = {}
    for nm, rr, ww, mm, vv, tr, transposed in (
            ("w_branch_hgrn", r_bh, w_branch_hgrn[0], m_w_branch_hgrn[0], v_w_branch_hgrn[0], 128, True),
            ("w_branch_attn", r_ba, w_branch_attn[0], m_w_branch_attn[0], v_w_branch_attn[0], 128, True),
            ("w_out", r_out, w_out[0], m_w_out[0], v_w_out[0], 128, False),
            ("w_ffn_gate", r_gate, w_ffn_gate[0], m_w_ffn_gate[0], v_w_ffn_gate[0], 352, True),
            ("w_ffn_up", r_up, w_ffn_up[0], m_w_ffn_up[0], v_w_ffn_up[0], 352, True),
            ("w_ffn_down", r_down, w_ffn_down[0], m_w_ffn_down[0], v_w_ffn_down[0], 352, False)):
        if transposed:
            res = _adamw_sharded(rr, ww.T, mm.T, vv.T, "adamw_" + nm, tr)
            big[nm] = [t.T[None] for t in res]
        else:
            big[nm] = [t[None] for t in _adamw_sharded(rr, ww, mm, vv, "adamw_" + nm, tr)]

    (g2,) = _all_gather([small], "gather_small", True)
    tot = _sum_small(g2)
    dm = jnp.zeros((16, 6 * D), F32).at[:8].set(g2[:, R_DMOD:R_DMOD + 6, :].reshape(N_DEV, 6 * D))
    dm = dm.at[8, :2 * D].set(tot[R_DCTX:R_DCTX + 2].reshape(2 * D))
    dm_cols = lax.dynamic_slice(dm, (0, me * ADA_BLK), (16, ADA_BLK))
    g_w_ada, dsc_term = _ada_grads(cc, dm_cols, w_ada2)
    (g3,) = _all_gather([dsc_term], "gather_cctx", True)
    g_c_ctx = _c_ctx_grad(g3, c_ctx.reshape(1, D))
    g_lbl = _lb_grads(tot[R_DLB:R_DLB + 2, :HGW], lbl)
    g_lb_mine = lax.dynamic_slice(g_lbl, (0, 0, me * 64), (2, 2, 64))
    misc = tot[R_MISC]
    loss = misc[M_LOSS]

    rep_out = _adamw_replicated(
        tot, g_c_ctx,
        [b_ada, c_ctx.reshape(1, D), norm_mix_w, norm_ffn_w, hgrn_norm_w, q_norm_w, k_norm_w, attn_sinks],
        [m_b_ada, m_c_ctx.reshape(1, D), m_norm_mix_w, m_norm_ffn_w, m_hgrn_norm_w, m_q_norm_w, m_k_norm_w,
         m_attn_sinks],
        [v_b_ada, v_c_ctx.reshape(1, D), v_norm_mix_w, v_norm_ffn_w, v_hgrn_norm_w, v_q_norm_w, v_k_norm_w,
         v_attn_sinks])
    rep = []
    for kind in range(4):
        vals = dict(zip(_REP_NAMES, rep_out[kind * len(_REP_NAMES):(kind + 1) * len(_REP_NAMES)]))
        vals["c_ctx"] = vals["c_ctx"].reshape(D)
        rep.append(vals)

    sems, c_in, land = pending_in
    d_ada, nm_ada, nv_ada = _adamw_plain(g_w_ada, w_ada2, m_w_ada[0], v_w_ada[0], "adamw_w_ada")
    land = _chip_exchange_wait(sems, c_in, land, d_ada)
    own = lax.dynamic_index_in_dim(c_in, 2 * lax.axis_index("x") + lax.axis_index("y"), 0, keepdims=False)
    big["w_in"] = [t.T[None] for t in _adamw_sharded(land, w_in2.T, m_w_in[0].T, v_w_in[0].T, "adamw_w_in", 336,
                                                     extra=own)]
    ada = [t[None] for t in (g_w_ada, d_ada, nm_ada, nv_ada)]
    lb_w = hgrn_lb_logits.reshape(4, 64)
    d_lb, nm_lb, nv_lb = _adamw_plain(g_lb_mine.reshape(4, 64), lb_w, m_hgrn_lb_logits.reshape(4, 64),
                                      v_hgrn_lb_logits.reshape(4, 64), "adamw_lb")
    lbs = [t.reshape(2, 2, 64) for t in (g_lb_mine, d_lb, nm_lb, nv_lb)]

    names = ['c_ctx', 'w_ada', 'b_ada', 'norm_mix_w', 'norm_ffn_w', 'w_in', 'hgrn_lb_logits', 'hgrn_norm_w',
             'q_norm_w', 'k_norm_w', 'attn_sinks', 'w_branch_hgrn', 'w_branch_attn', 'w_out', 'w_ffn_gate',
             'w_ffn_up', 'w_ffn_down']
    outs = [loss, grad_x[None]]
    for kind in range(4):
        for nm in names:
            if nm == 'w_ada':
                outs.append(ada[kind])
            elif nm == 'hgrn_lb_logits':
                outs.append(lbs[kind])
            elif nm in big:
                outs.append(big[nm][kind])
            else:
                outs.append(rep[kind][nm])
    return tuple(outs)
```

```python
import functools
import math

import numpy as np
import jax
import jax.numpy as jnp
from jax import lax
from jax.experimental import pallas as pl
from jax.experimental.pallas import tpu as pltpu

F32 = jnp.float32
BF16 = jnp.bfloat16

N_DEV = 8
D = 1024
S = 2048
L = 256
T = L + S
TM = 256
N_TILES = T // TM
N_LAT_TILES = S // TM
HG_HEADS = 4
HG_DIM = 128
HGW = 512
CHUNK = 32
N_CHUNKS = T // CHUNK
N_CTX_CHUNKS = L // CHUNK
ATT_HEADS = 8
KV_HEADS = 2
HEAD_DIM = 64
ATW = 512
KVW = 128
BLOCK = 128
N_BLOCKS = S // BLOCK
GRID_W = 64
ROPE_THETA = 10000.0
D_FF = 2816
FF_BLK = D_FF // N_DEV
FF_TILE = 256
N_FF_TILES = D_FF // FF_TILE
IN_COLS = 5376
IN_BLK = IN_COLS // N_DEV
ADA_BLK = 6 * D // N_DEV
EPS = 1e-6
WA, WB, WC = 2048, 1280, 2048

ADAM_LR = 0.001
ADAM_B1 = 0.9
ADAM_B2 = 0.999
ADAM_EPS = 1e-08
ADAM_WD = 0.01
ADAM_STEP = 10

VMEM_LIMIT = 56 * 1024 * 1024
MESH = pl.DeviceIdType.MESH


def _cp(sem=None, vmem=VMEM_LIMIT):
    return pltpu.CompilerParams(dimension_semantics=sem, vmem_limit_bytes=vmem)


def _full(shape):
    n = len(shape)
    return pl.BlockSpec(shape, lambda *_: (0,) * n)


_VMEM_WHOLE = pl.BlockSpec(memory_space=pltpu.VMEM)
_ANY = pl.BlockSpec(memory_space=pl.ANY)


def _sigmoid(v):
    return 1.0 / (1.0 + jnp.exp(-v))


def _dot(a, b):
    return jnp.dot(a, b, preferred_element_type=F32)


def _dot_nt(a, b):
    return lax.dot_general(a, b, (((1,), (1,)), ((), ())), preferred_element_type=F32)


def _dot_tn(a, b):
    return lax.dot_general(a, b, (((0,), (0,)), ((), ())), preferred_element_type=F32)


def _split3(v):
    hi = v.astype(BF16)
    r = v - hi.astype(F32)
    mid = r.astype(BF16)
    lo = (r - mid.astype(F32)).astype(BF16)
    return hi, mid, lo


def _dot_exact_rhs01(v, m01):
    hi, mid, lo = _split3(v)
    return _dot(hi, m01) + _dot(mid, m01) + _dot(lo, m01)


def _split2(v):
    hi = v.astype(BF16)
    return hi, (v - hi.astype(F32)).astype(BF16)


def _dot_lhs01(m01, v):
    hi, lo = _split2(v)
    return _dot(m01, hi) + _dot(m01, lo)


def _dot_f32(a, b, dot=_dot):
    ah, am, al = _split3(a)
    bh, bm, bl = _split3(b)
    return (dot(ah, bh) + (dot(ah, bm) + dot(am, bh))
            + (dot(am, bm) + dot(ah, bl) + dot(al, bh)))


def _my_pos():
    return lax.axis_index("x"), lax.axis_index("y"), lax.axis_index("c")


class _Comm:
    def __init__(self, operands, out_shapes, sems, phases):
        self.operands, self.out_shapes, self.sems, self.phases = operands, out_shapes, sems, phases


def _gather_comm(blocks):
    n = len(blocks)

    def parts(ins, outs, sems):
        send_sems, recv_sems, local_sems = sems
        x, y, c = _my_pos()
        me, sibling = (x, y, c), (x, y, 1 - c)
        chips = [(1 - x, y), (x, 1 - y), (1 - x, 1 - y)]

        def slot(a, px, py, pc):
            return outs[a].at[4 * px + 2 * py + pc]

        def copy(a, k, block, to, src=None):
            return pltpu.make_async_remote_copy(
                src_ref=slot(a, *block) if src is None else src, dst_ref=slot(a, *block),
                send_sem=send_sems.at[a, k], recv_sem=recv_sems.at[a, k],
                device_id=to, device_id_type=MESH)

        mine = [pltpu.make_async_copy(ins[a], slot(a, *me), local_sems.at[a]) for a in range(n)]
        first = []
        for a in range(n):
            first.append(copy(a, 0, me, sibling, src=ins[a]))
            first += [copy(a, 1 + j, me, (*chip, c), src=ins[a]) for j, chip in enumerate(chips)]
        passed = [copy(a, 4 + j, (*chip, c), sibling) for j, chip in enumerate(chips) for a in range(n)]
        return c, me, sibling, chips, copy, mine, first, passed

    def start(ins, outs, sems):
        _, _, _, _, _, mine, first, _ = parts(ins, outs, sems)
        for cp in mine + first:
            cp.start()

    def forward(ins, outs, sems):
        c, me, _, chips, copy, _, _, passed = parts(ins, outs, sems)
        for j, chip in enumerate(chips):
            for a in range(n):
                copy(a, 1 + j, (*chip, c), me).wait_recv()
                passed[j * n + a].start()

    def finish(ins, outs, sems):
        c, me, sibling, chips, copy, mine, first, passed = parts(ins, outs, sems)
        for a in range(n):
            copy(a, 0, sibling, me).wait_recv()
            for j, chip in enumerate(chips):
                copy(a, 4 + j, (*chip, 1 - c), me).wait_recv()
        for cp in first + passed:
            cp.wait_send()
        for cp in mine:
            cp.wait()

    return _Comm(blocks, [jax.ShapeDtypeStruct((N_DEV,) + b.shape, b.dtype) for b in blocks],
                 [pltpu.SemaphoreType.DMA((n, 7)), pltpu.SemaphoreType.DMA((n, 7)), pltpu.SemaphoreType.DMA((n,))],
                 [start, forward, finish])


def _gather_comm_relayed(blocks):
    n = len(blocks)

    def parts(ins, outs, sems):
        send_sems, recv_sems, local_sems = sems
        x, y, c = _my_pos()
        me, sibling = (x, y, c), (x, y, 1 - c)
        x_nbr, y_nbr, diag = (1 - x, y, c), (x, 1 - y, c), (1 - x, 1 - y, c)

        def slot(a, dev, half=None):
            ref = outs[a].at[4 * dev[0] + 2 * dev[1] + dev[2]]
            if half is None:
                return ref
            rows = blocks[a].shape[0] // 2
            return ref.at[pl.ds(half * rows, rows)]

        def copy(a, k, block, to, half=None, src=None):
            return pltpu.make_async_remote_copy(
                src_ref=slot(a, block, half) if src is None else src, dst_ref=slot(a, block, half),
                send_sem=send_sems.at[a, k], recv_sem=recv_sems.at[a, k],
                device_id=to, device_id_type=MESH)

        mine = [pltpu.make_async_copy(ins[a], slot(a, me), local_sems.at[a]) for a in range(n)]
        return me, sibling, x_nbr, y_nbr, diag, copy, mine

    def start(ins, outs, sems):
        me, sibling, x_nbr, y_nbr, _, copy, mine = parts(ins, outs, sems)
        for cp in mine:
            cp.start()
        for a in range(n):
            for k, to in ((1, x_nbr), (2, y_nbr), (0, sibling)):
                copy(a, k, me, to, src=ins[a]).start()

    def forward(ins, outs, sems):
        me, sibling, x_nbr, y_nbr, _, copy, _ = parts(ins, outs, sems)
        for a in range(n):
            copy(a, 1, x_nbr, me).wait_recv()
            copy(a, 3, x_nbr, y_nbr, half=0).start()
            copy(a, 5, x_nbr, sibling).start()
        for a in range(n):
            copy(a, 2, y_nbr, me).wait_recv()
            copy(a, 4, y_nbr, x_nbr, half=1).start()
            copy(a, 6, y_nbr, sibling).start()

    def finish(ins, outs, sems):
        me, sibling, x_nbr, y_nbr, diag, copy, mine = parts(ins, outs, sems)
        sib = lambda dev: (dev[0], dev[1], sibling[2])
        for a in range(n):
            copy(a, 3, diag, me, half=0).wait_recv()
            copy(a, 4, diag, me, half=1).wait_recv()
            copy(a, 7, diag, sibling).start()
        for a in range(n):
            copy(a, 0, sibling, me).wait_recv()
            for k, dev in ((5, x_nbr), (6, y_nbr), (7, diag)):
                copy(a, k, sib(dev), me).wait_recv()
        for a in range(n):
            for k, block, to, half in ((0, me, sibling, None), (1, me, x_nbr, None), (2, me, y_nbr, None),
                                       (3, x_nbr, y_nbr, 0), (4, y_nbr, x_nbr, 1), (5, x_nbr, sibling, None),
                                       (6, y_nbr, sibling, None), (7, diag, sibling, None)):
                copy(a, k, block, to, half=half, src=ins[a] if block is me else None).wait_send()
        for cp in mine:
            cp.wait()

    return _Comm(blocks, [jax.ShapeDtypeStruct((N_DEV,) + b.shape, b.dtype) for b in blocks],
                 [pltpu.SemaphoreType.DMA((n, 8)), pltpu.SemaphoreType.DMA((n, 8)), pltpu.SemaphoreType.DMA((n,))],
                 [start, forward, finish])


_HBM = pl.BlockSpec(memory_space=pltpu.HBM)
_SEM = pl.BlockSpec(memory_space=pltpu.SEMAPHORE)
_SPLIT_COPY = pltpu.CompilerParams(has_side_effects=pltpu.SideEffectType.DATAFLOW_SIDE_EFFECTING)


def _chip_exchange_copies(src_ref, land_ref, sems):
    x, y, c = _my_pos()
    q_me = 2 * x + y
    pairs = []
    for j, (px, py) in enumerate([(1 - x, y), (x, 1 - y), (1 - x, 1 - y)]):
        q = 2 * px + py
        send = pltpu.make_async_remote_copy(
            src_ref=src_ref.at[q], dst_ref=land_ref.at[q_me], send_sem=sems[j], recv_sem=sems[3 + j],
            device_id=(px, py, c), device_id_type=MESH)
        recv = pltpu.make_async_remote_copy(
            src_ref=src_ref.at[q], dst_ref=land_ref.at[q], send_sem=sems[j], recv_sem=sems[3 + j],
            device_id=(x, y, c), device_id_type=MESH)
        pairs.append((send, recv))
    return pairs


def _chip_exchange_start(src, land):
    def body(src_ref, land_ref, *outs):
        sems, token = outs[:6], outs[8]
        for send, _ in _chip_exchange_copies(src_ref, land_ref, sems):
            send.start()
        token[...] = jnp.zeros_like(token)

    res = pl.pallas_call(
        body, name="scatter_in_start",
        out_shape=(pltpu.SemaphoreType.DMA(()),) * 6 + (
            pltpu.HBM(src.shape, src.dtype), pltpu.HBM(land.shape, land.dtype),
            jax.ShapeDtypeStruct((8, 128), F32)),
        in_specs=(_HBM, _HBM), out_specs=(_SEM,) * 6 + (_HBM, _HBM, pl.BlockSpec(memory_space=pltpu.VMEM)),
        input_output_aliases={0: 6, 1: 7}, compiler_params=_SPLIT_COPY,
    )(pltpu.with_memory_space_constraint(src, pltpu.HBM), pltpu.with_memory_space_constraint(land, pltpu.HBM))
    return res[:6], res[6], res[7], res[8]


def _chip_exchange_wait(sems, src_thru, land_thru, after):
    def body(src_ref, land_ref, *rest):
        for send, recv in _chip_exchange_copies(src_ref, land_ref, rest[:6]):
            send.wait_send()
            recv.wait_recv()

    return pl.pallas_call(
        body, name="scatter_in_wait",
        out_shape=(pltpu.HBM(src_thru.shape, src_thru.dtype), pltpu.HBM(land_thru.shape, land_thru.dtype)),
        in_specs=(_HBM, _HBM) + (_SEM,) * 6 + (_ANY,), out_specs=(_HBM, _HBM),
        input_output_aliases={0: 0, 1: 1}, compiler_params=_SPLIT_COPY,
    )(src_thru, land_thru, *sems, after)[1]


def _run_comm(comm, name, in_vmem=False):
    n_in, n_out = len(comm.operands), len(comm.out_shapes)

    def body(*refs):
        ins, outs, sems = refs[:n_in], refs[n_in:n_in + n_out], refs[n_in + n_out:]
        for phase in comm.phases:
            phase(ins, outs, sems)

    spec = _VMEM_WHOLE if in_vmem else _ANY
    return pl.pallas_call(
        body, name=name, out_shape=comm.out_shapes, in_specs=[spec] * n_in, out_specs=[spec] * n_out,
        scratch_shapes=comm.sems,
    )(*comm.operands)


def _carrier_call(body, comm, schedule, *, name, grid, in_specs, out_specs, out_shape, scratch_shapes, operands):
    n_in, n_out, n_scr = len(in_specs), len(out_specs), len(scratch_shapes)
    c_in, c_out = len(comm.operands), len(comm.out_shapes)

    def full_body(*refs):
        ins, refs = refs[:n_in], refs[n_in:]
        cins, refs = refs[:c_in], refs[c_in:]
        outs, refs = refs[:n_out], refs[n_out:]
        couts, refs = refs[:c_out], refs[c_out:]
        scr, csems = refs[:n_scr], refs[n_scr:]
        step = pl.program_id(0)

        def run(before):
            for (at, when_before), phase in zip(schedule, comm.phases):
                if when_before == before:
                    pl.when(step == at)(functools.partial(phase, cins, couts, csems))

        run(True)
        body(*ins, *outs, *scr)
        run(False)

    res = pl.pallas_call(
        full_body, name=name, grid=grid,
        in_specs=list(in_specs) + [_ANY] * c_in, out_specs=list(out_specs) + [_ANY] * c_out,
        out_shape=list(out_shape) + list(comm.out_shapes),
        scratch_shapes=list(scratch_shapes) + list(comm.sems),
        compiler_params=_cp(("arbitrary",)),
    )(*operands, *comm.operands)
    return res[:n_out], res[n_out:]


def _pcall(body, carried, *, name, grid, in_specs, out_specs, out_shape, scratch_shapes, operands):
    if carried is None:
        res = pl.pallas_call(body, name=name, grid=grid, in_specs=in_specs, out_specs=out_specs,
                             out_shape=out_shape, scratch_shapes=scratch_shapes,
                             compiler_params=_cp(("arbitrary",)))(*operands)
        return res, ()
    return _carrier_call(body, carried[0], carried[1], name=name, grid=grid, in_specs=in_specs,
                         out_specs=out_specs, out_shape=out_shape, scratch_shapes=scratch_shapes,
                         operands=operands)


def _all_gather(blocks, name, in_vmem):
    return _run_comm(_gather_comm(blocks), name, in_vmem)


N_CHIPS = 4


def _sibling_comm(contribs):
    n = len(contribs)

    def copies(ins, outs, sems):
        send_sems, recv_sems = sems
        x, y, c = _my_pos()
        return [pltpu.make_async_remote_copy(
            src_ref=ins[a].at[pl.ds(0, N_CHIPS), 1 - c], dst_ref=outs[a],
            send_sem=send_sems.at[a], recv_sem=recv_sems.at[a],
            device_id=(x, y, 1 - c), device_id_type=MESH) for a in range(n)]

    def start(ins, outs, sems):
        for cp in copies(ins, outs, sems):
            cp.start()

    def finish(ins, outs, sems):
        cps = copies(ins, outs, sems)
        for cp in cps:
            cp.wait_recv()
        for cp in cps:
            cp.wait_send()

    return _Comm(contribs, [jax.ShapeDtypeStruct((N_CHIPS,) + b.shape[2:], b.dtype) for b in contribs],
                 [pltpu.SemaphoreType.DMA((n,)), pltpu.SemaphoreType.DMA((n,))], [start, finish])


def _pair_sum(mine, theirs, name):
    _, _, rows, cols = mine.shape
    core = lax.axis_index("c").astype(jnp.int32).reshape(1)

    def body(c_ref, m_ref, t_ref, o_ref):
        o_ref[...] = (m_ref[...].astype(F32) + t_ref[...].astype(F32)).astype(BF16)

    return pl.pallas_call(
        body, name=name,
        grid_spec=pltpu.PrefetchScalarGridSpec(
            num_scalar_prefetch=1, grid=(N_CHIPS,),
            in_specs=[pl.BlockSpec((None, None, rows, cols), lambda q, c: (q, c[0], 0, 0)),
                      pl.BlockSpec((None, rows, cols), lambda q, c: (q, 0, 0))],
            out_specs=pl.BlockSpec((None, rows, cols), lambda q, c: (q, 0, 0))),
        out_shape=jax.ShapeDtypeStruct((N_CHIPS, rows, cols), BF16),
        compiler_params=_cp(("parallel",)),
    )(core, mine, theirs)


def _chip_comm(sums):
    n = len(sums)

    def parts(ins, outs, sems):
        send_sems, recv_sems, local_sems = sems
        x, y, c = _my_pos()
        q_me = 2 * x + y
        chips = [(1 - x, y), (x, 1 - y), (1 - x, 1 - y)]
        mine = [pltpu.make_async_copy(ins[a].at[q_me], outs[a].at[q_me], local_sems.at[a]) for a in range(n)]
        sends, recvs = [], []
        for j, (px, py) in enumerate(chips):
            for a in range(n):
                q = 2 * px + py
                sends.append(pltpu.make_async_remote_copy(
                    src_ref=ins[a].at[q], dst_ref=outs[a].at[q_me],
                    send_sem=send_sems.at[a, j], recv_sem=recv_sems.at[a, j],
                    device_id=(px, py, c), device_id_type=MESH))
                recvs.append(pltpu.make_async_remote_copy(
                    src_ref=ins[a].at[q], dst_ref=outs[a].at[q],
                    send_sem=send_sems.at[a, j], recv_sem=recv_sems.at[a, j],
                    device_id=(x, y, c), device_id_type=MESH))
        return mine, sends, recvs

    def start(ins, outs, sems):
        mine, sends, _ = parts(ins, outs, sems)
        for cp in mine + sends:
            cp.start()

    def finish(ins, outs, sems):
        mine, sends, recvs = parts(ins, outs, sems)
        for cp in recvs:
            cp.wait_recv()
        for cp in sends:
            cp.wait_send()
        for cp in mine:
            cp.wait()

    return _Comm(sums, [jax.ShapeDtypeStruct(b.shape, b.dtype) for b in sums],
                 [pltpu.SemaphoreType.DMA((n, 3)), pltpu.SemaphoreType.DMA((n, 3)), pltpu.SemaphoreType.DMA((n,))],
                 [start, finish])


def _mm_nt(a, bt, *, tm, tn, out_dtype, name, row_off=0, rows=None):
    rows = a.shape[0] if rows is None else rows
    n, k = bt.shape

    def body(a_ref, b_ref, o_ref):
        o_ref[...] = _dot_nt(a_ref[...], b_ref[...]).astype(out_dtype)

    return pl.pallas_call(
        body, name=name, grid=(rows // tm, n // tn),
        in_specs=[pl.BlockSpec((tm, k), lambda i, j: (i + row_off, 0)),
                  pl.BlockSpec((tn, k), lambda i, j: (j, 0))],
        out_specs=pl.BlockSpec((tm, tn), lambda i, j: (i, j)),
        out_shape=jax.ShapeDtypeStruct((rows, n), out_dtype),
        compiler_params=_cp(("parallel", "parallel")),
    )(a, bt)


def _mm_tn(a, b, *, tk, nk, tm, tn, out_dtype, name, a_off=0, b_off=0):
    m, n = a.shape[1], b.shape[1]

    def body(a_ref, b_ref, o_ref, acc):
        kk = pl.program_id(2)

        @pl.when(kk == 0)
        def _():
            acc[...] = jnp.zeros_like(acc)

        acc[...] += _dot_tn(a_ref[...], b_ref[...])

        @pl.when(kk == nk - 1)
        def _():
            o_ref[...] = acc[...].astype(out_dtype)

    return pl.pallas_call(
        body, name=name, grid=(m // tm, n // tn, nk),
        in_specs=[pl.BlockSpec((tk, tm), lambda i, j, kk: (kk + a_off, i)),
                  pl.BlockSpec((tk, tn), lambda i, j, kk: (kk + b_off, j))],
        out_specs=pl.BlockSpec((tm, tn), lambda i, j, kk: (i, j)),
        out_shape=jax.ShapeDtypeStruct((m, n), out_dtype),
        scratch_shapes=[pltpu.VMEM((tm, tn), F32)],
        compiler_params=_cp(("parallel", "parallel", "arbitrary")),
    )(a, b)


def _mm_tn_blocked(a, b, name, carried=None):
    nb, _, w = a.shape
    n = b.shape[1]

    def body(a_ref, b_ref, o_ref):
        o_ref[...] = _dot_tn(a_ref[...], b_ref[...]).astype(BF16)

    (out,), extra = _pcall(
        body, carried, name=name, grid=(nb,),
        in_specs=[pl.BlockSpec((None, S, w), lambda j: (j, 0, 0)), _full((S, n))],
        out_specs=[pl.BlockSpec((None, w, n), lambda j: (j, 0, 0))],
        out_shape=[jax.ShapeDtypeStruct((nb, w, n), BF16)],
        scratch_shapes=[], operands=[a, b])
    return out, extra


def _prologue(cond, c_ctx, w_ada, b_cols, w_in_t, x, ctx, nw):
    rows_shape = jax.ShapeDtypeStruct((16, ADA_BLK), F32)
    big, g_cond, g_mod = _gather_comm_relayed([w_in_t]), _gather_comm([cond]), _gather_comm([rows_shape])

    def body(cond_ref, cctx_ref, wada_ref, b_ref, nw_ref, win_ref, x_ref, ctx_ref,
             g0_ref, cc_ref, mod_ref, gin_ref, h_ref, rows_ref, g1_ref, x_s, ctx_s, h_s, io_sems, *sems):
        s_big, s_cond, s_mod = sems[0:3], sems[3:6], sems[6:9]
        big.phases[0]([win_ref], [gin_ref], s_big)
        load_x = pltpu.make_async_copy(x_ref, x_s, io_sems.at[0])
        load_ctx = pltpu.make_async_copy(ctx_ref, ctx_s, io_sems.at[1])
        load_x.start()
        load_ctx.start()
        for phase in g_cond.phases:
            phase([cond_ref], [g0_ref], s_cond)
        cc_ref[...] = jnp.zeros_like(cc_ref)
        for j in range(N_DEV):
            cc_ref[j:j + 1, :] = g0_ref[j, 0:1, :]
        cc_ref[N_DEV:N_DEV + 1, :] = cctx_ref[...]
        cv = cc_ref[...]
        rows_ref[...] = _dot_f32(cv * _sigmoid(cv), wada_ref[...]) + b_ref[...]
        for phase in g_mod.phases:
            phase([rows_ref], [g1_ref], s_mod)
        x_pos, y_pos, c_pos = _my_pos()
        me = 4 * x_pos + 2 * y_pos + c_pos
        mine = jnp.concatenate([g1_ref[j, pl.ds(me, 1), :] for j in range(N_DEV)], axis=1)
        shared = jnp.concatenate([g1_ref[j, N_DEV:N_DEV + 1, :] for j in range(N_DEV)], axis=1)
        for k in range(6):
            mod_ref[k:k + 1, :] = mine[:, k * D:(k + 1) * D]
        mod_ref[6:7, :] = shared[:, 0:D]
        mod_ref[7:8, :] = shared[:, D:2 * D]
        load_ctx.wait()
        load_x.wait()
        h_s[pl.ds(0, L), :] = _rms_mod(ctx_s[...], nw_ref[...], mod_ref[6:7, :], mod_ref[7:8, :]).astype(BF16)

        def norm_tile(i, carry):
            r0 = pl.multiple_of(i * TM, TM)
            h_s[pl.ds(L + r0, TM), :] = _rms_mod(
                x_s[pl.ds(r0, TM), :], nw_ref[...], mod_ref[0:1, :], mod_ref[1:2, :]).astype(BF16)
            return carry

        lax.fori_loop(0, N_LAT_TILES, norm_tile, 0)
        store_h = pltpu.make_async_copy(h_s, h_ref, io_sems.at[2])
        store_h.start()
        big.phases[1]([win_ref], [gin_ref], s_big)
        big.phases[2]([win_ref], [gin_ref], s_big)
        store_h.wait()

    return pl.pallas_call(
        body, name="prologue",
        in_specs=[_VMEM_WHOLE] * 5 + [_ANY] * 3, out_specs=[_VMEM_WHOLE] * 3 + [_ANY] * 2,
        out_shape=[g_cond.out_shapes[0], jax.ShapeDtypeStruct((16, D), F32), jax.ShapeDtypeStruct((8, D), F32),
                   big.out_shapes[0], jax.ShapeDtypeStruct((T, D), BF16)],
        scratch_shapes=[pltpu.VMEM((16, ADA_BLK), F32), pltpu.VMEM((N_DEV, 16, ADA_BLK), F32),
                        pltpu.VMEM((S, D), F32), pltpu.VMEM((L, D), F32), pltpu.VMEM((T, D), BF16),
                        pltpu.SemaphoreType.DMA((3,))] + big.sems + g_cond.sems + g_mod.sems,
        compiler_params=_cp(),
    )(cond, c_ctx, w_ada, b_cols, nw, w_in_t, x, ctx)


def _ada_grads(cc, dm_cols, w_ada):
    def body(c_ref, dm_ref, w_ref, gw_ref, dsc_ref):
        cv = c_ref[...]
        sc = cv * _sigmoid(cv)
        dm = dm_ref[...]
        gw_ref[...] = _dot_f32(sc, dm, dot=_dot_tn)
        dsc_ref[...] = _dot_f32(dm, w_ref[...], dot=_dot_nt)

    return pl.pallas_call(
        body, name="ada_grads",
        in_specs=[_VMEM_WHOLE] * 3, out_specs=[_VMEM_WHOLE] * 2,
        out_shape=[jax.ShapeDtypeStruct((D, ADA_BLK), F32), jax.ShapeDtypeStruct((16, D), F32)],
        compiler_params=_cp(),
    )(cc, dm_cols, w_ada)


def _lat(i):
    return jnp.maximum(i - 1, 0)


def _rms_mod(xv, nw, sh, sc):
    rstd = lax.rsqrt(jnp.mean(xv * xv, axis=-1, keepdims=True) + EPS)
    return (xv * rstd * nw) * (1.0 + sc) + sh


def _rms_mod_bwd(xv, nw, sc, dh):
    rstd = lax.rsqrt(jnp.mean(xv * xv, axis=-1, keepdims=True) + EPS)
    xhat = xv * rstd
    dn = dh * (1.0 + sc)
    dxhat = dn * nw
    dx = rstd * (dxhat - xhat * jnp.mean(dxhat * xhat, axis=-1, keepdims=True))
    return (dx, jnp.sum(dh, axis=0, keepdims=True), jnp.sum(dh * (xhat * nw), axis=0, keepdims=True),
            jnp.sum(dn * xhat, axis=0, keepdims=True))


def _chunk_masks(reverse):
    row = lax.broadcasted_iota(jnp.int32, (TM, TM), 0)
    col = lax.broadcasted_iota(jnp.int32, (TM, TM), 1)
    same = (row // CHUNK) == (col // CHUNK)
    tri = same & ((col >= row) if reverse else (col <= row))
    return same, tri


def _chunk_order(i, reverse):
    if not reverse:
        return i
    return jnp.where(i < N_CTX_CHUNKS, N_CTX_CHUNKS - 1 - i, N_CHUNKS + N_CTX_CHUNKS - 1 - i)


def _decay_terms(z, lb, same01, tri01):
    f = lb + (1.0 - lb) * _sigmoid(z)
    g = jnp.log(f)
    g2 = jnp.concatenate(_split2(g), axis=1)
    b2 = _dot(tri01, g2)
    t2 = _dot(same01, g2)
    return f, 1.0 - f, b2[:, :HG_DIM] + b2[:, HG_DIM:], t2[:, :HG_DIM] + t2[:, HG_DIM:]


def _chunk_outer(a, b):
    n = TM // CHUNK
    return jnp.einsum('ncv,nck->nvk', a.reshape(n, CHUNK, HG_DIM), b.reshape(n, CHUNK, HG_DIM),
                      preferred_element_type=F32)


def _hgrn_fwd(p_a, lbl, carried=None):
    cpt = TM // CHUNK

    def body(p_ref, lbl_ref, o_ref, st_ref, qd_s, kd_s, u_s, v_s, ebt_s):
        masks = [_chunk_masks(d == 1) for d in (0, 1)]
        same01 = jnp.where(masks[0][0], 1.0, 0.0).astype(BF16)
        tri = [m[1] for m in masks]
        tri01 = [jnp.where(t, 1.0, 0.0).astype(BF16) for t in tri]
        lb = [_sigmoid(lbl_ref[d][0:1, :] - lbl_ref[d][1:2, :]) for d in (0, 1)]

        def prep(r, carry):
            r0 = pl.multiple_of(r * TM, TM)
            vb = p_ref[pl.ds(r0, TM), 2 * HG_DIM:3 * HG_DIM].astype(BF16)
            v_s[pl.ds(r0, TM), :] = vb
            for d in (0, 1):
                z = p_ref[pl.ds(r0, TM), d * HG_DIM:(d + 1) * HG_DIM]
                _, k, b, bt = _decay_terms(z, lb[d], same01, tri01[d])
                u_s[d, pl.ds(r * cpt, cpt)] = _chunk_outer(vb, (k * jnp.exp(bt - b)).astype(BF16))
                ebt_s[d, pl.ds(r0, TM), :] = jnp.exp(bt)

                @pl.when(r >= 1)
                def _():
                    rl = pl.multiple_of(r0 - L, TM)
                    qr = p_ref[pl.ds(r0, TM), 3 * HG_DIM:4 * HG_DIM]
                    q = qr * _sigmoid(qr) * HG_DIM ** -0.5
                    qd_s[d, pl.ds(rl, TM), :] = (q * jnp.exp(b)).astype(BF16)
                    kd_s[d, pl.ds(rl, TM), :] = (k * jnp.exp(-b)).astype(BF16)

            return carry

        lax.fori_loop(0, N_TILES, prep, 0)

        def scan(i, sts):
            new = []
            for d in (0, 1):
                nn = _chunk_order(i, d == 1)
                c0 = pl.multiple_of(nn * CHUNK, CHUNK)
                st_ref[d, nn] = sts[d].astype(BF16)
                new.append(sts[d] * ebt_s[d, pl.ds(c0, 1), :] + u_s[d, nn])
            return tuple(new)

        zero = jnp.zeros((HG_DIM, HG_DIM), F32)
        lax.fori_loop(0, N_CHUNKS, scan, (zero, zero))

        def outp(r, carry):
            r0 = pl.multiple_of(r * TM, TM)
            vb = v_s[pl.ds(r0 + L, TM), :]
            o = jnp.zeros((TM, HG_DIM), F32)
            for d in (0, 1):
                qd = qd_s[d, pl.ds(r0, TM), :]
                a = jnp.where(tri[d], _dot_nt(qd, kd_s[d, pl.ds(r0, TM), :]), 0.0)
                stb = st_ref[d, pl.ds(N_CTX_CHUNKS + r * cpt, cpt)]
                inter = jnp.einsum('nck,nvk->ncv', qd.reshape(cpt, CHUNK, HG_DIM), stb,
                                   preferred_element_type=F32)
                o = o + _dot(a.astype(BF16), vb) + inter.reshape(TM, HG_DIM)
            o_ref[pl.ds(r0, TM), :] = o
            return carry

        lax.fori_loop(0, N_LAT_TILES, outp, 0)

    return _pcall(
        body, carried, name="hgrn_fwd", grid=(HG_HEADS,),
        in_specs=[pl.BlockSpec((T, 4 * HG_DIM), lambda h: (0, h)),
                  pl.BlockSpec((2, 2, HG_DIM), lambda h: (0, 0, h))],
        out_specs=[pl.BlockSpec((S, HG_DIM), lambda h: (0, h)),
                   pl.BlockSpec((2, None, N_CHUNKS, HG_DIM, HG_DIM), lambda h: (0, h, 0, 0, 0))],
        out_shape=[jax.ShapeDtypeStruct((S, HGW), F32),
                   jax.ShapeDtypeStruct((2, HG_HEADS, N_CHUNKS, HG_DIM, HG_DIM), BF16)],
        scratch_shapes=[pltpu.VMEM((2, S, HG_DIM), BF16), pltpu.VMEM((2, S, HG_DIM), BF16),
                        pltpu.VMEM((2, N_CHUNKS, HG_DIM, HG_DIM), F32), pltpu.VMEM((T, HG_DIM), BF16),
                        pltpu.VMEM((2, T, HG_DIM), F32)],
        operands=[p_a, lbl])


def _hgrn_bwd(p_a, lbl, d_o, st, carried=None):
    cpt = TM // CHUNK

    def rows(r):
        return r * TM if isinstance(r, int) else pl.multiple_of(r * TM, TM)

    def body(p_ref, lbl_ref, do_ref, st_ref, dp_ref, dlb_ref, b_s, bt_s, dbt_s, qd_s, dst_s, w_s):
        masks = [_chunk_masks(d == 1) for d in (0, 1)]
        same01 = jnp.where(masks[0][0], 1.0, 0.0).astype(BF16)
        tri = [m[1] for m in masks]
        tri01 = [jnp.where(t, 1.0, 0.0).astype(BF16) for t in tri]
        later01 = [tri01[1], tri01[0]]
        lb = [_sigmoid(lbl_ref[d][0:1, :] - lbl_ref[d][1:2, :]) for d in (0, 1)]

        def prep_tile(r, latent):
            r0 = rows(r)
            for d in (0, 1):
                z = p_ref[pl.ds(r0, TM), d * HG_DIM:(d + 1) * HG_DIM]
                _, _, b, bt = _decay_terms(z, lb[d], same01, tri01[d])
                b_s[d, pl.ds(r0, TM), :] = b
                bt_s[d, pl.ds(r0, TM), :] = bt
                if latent:
                    rl = pl.multiple_of(r0 - L, TM)
                    qr = p_ref[pl.ds(r0, TM), 3 * HG_DIM:4 * HG_DIM]
                    qd = (qr * _sigmoid(qr) * HG_DIM ** -0.5 * jnp.exp(b)).astype(BF16)
                    qd_s[d, pl.ds(rl, TM), :] = qd
                    w_s[d, pl.ds(r * cpt, cpt)] = _chunk_outer(
                        do_ref[pl.ds(rl, TM), :].astype(BF16), qd).astype(BF16)

        prep_tile(0, False)
        w_s[:, pl.ds(0, N_CTX_CHUNKS)] = jnp.zeros((2, N_CTX_CHUNKS, HG_DIM, HG_DIM), BF16)

        def prep(r, carry):
            prep_tile(r, True)
            return carry

        lax.fori_loop(1, N_TILES, prep, 0)

        def rscan(j, dsts):
            i = N_CHUNKS - 1 - j
            new = []
            for d in (0, 1):
                nn = _chunk_order(i, d == 1)
                c0 = pl.multiple_of(nn * CHUNK, CHUNK)
                dst_s[d, nn] = dsts[d].astype(BF16)
                after = st_ref[d, _chunk_order(jnp.minimum(i + 1, N_CHUNKS - 1), d == 1)].astype(F32)
                dbt_s[d, pl.ds(c0, CHUNK), :] = jnp.broadcast_to(
                    jnp.sum(after * dsts[d], axis=0, keepdims=True), (CHUNK, HG_DIM))
                new.append(dsts[d] * jnp.exp(bt_s[d, pl.ds(c0, 1), :]) + w_s[d, nn].astype(F32))
            return tuple(new)

        zero = jnp.zeros((HG_DIM, HG_DIM), F32)
        lax.fori_loop(0, N_CHUNKS, rscan, (zero, zero))

        def grad_tile(r, latent):
            r0 = rows(r)
            vb = p_ref[pl.ds(r0, TM), 2 * HG_DIM:3 * HG_DIM].astype(BF16)
            dv = jnp.zeros((TM, HG_DIM), F32)
            dq = jnp.zeros((TM, HG_DIM), F32)
            dlbs = []
            if latent:
                rl = pl.multiple_of(r0 - L, TM)
                qr = p_ref[pl.ds(r0, TM), 3 * HG_DIM:4 * HG_DIM]
                sq = _sigmoid(qr)
                do = do_ref[pl.ds(rl, TM), :].astype(BF16)
                da_full = _dot_nt(do, vb)
            for d in (0, 1):
                z = p_ref[pl.ds(r0, TM), d * HG_DIM:(d + 1) * HG_DIM]
                sz = _sigmoid(z)
                f = lb[d] + (1.0 - lb[d]) * sz
                k = 1.0 - f
                b = b_s[d, pl.ds(r0, TM), :]
                e2 = jnp.exp(bt_s[d, pl.ds(r0, TM), :] - b)
                dstb = dst_s[d, pl.ds(r * cpt, cpt)]
                kd2 = k * e2
                dkd2 = jnp.einsum('ncv,nvk->nck', vb.reshape(cpt, CHUNK, HG_DIM), dstb,
                                  preferred_element_type=F32).reshape(TM, HG_DIM)
                dv = dv + jnp.einsum('nck,nvk->ncv', kd2.astype(BF16).reshape(cpt, CHUNK, HG_DIM), dstb,
                                     preferred_element_type=F32).reshape(TM, HG_DIM)
                dk = dkd2 * e2
                db = -(kd2 * dkd2)
                if latent:
                    eb = jnp.exp(b)
                    enb = jnp.exp(-b)
                    qdf = qr * sq * HG_DIM ** -0.5 * eb
                    kdf = k * enb
                    qd = qd_s[d, pl.ds(rl, TM), :]
                    kd = kdf.astype(BF16)
                    a = jnp.where(tri[d], _dot_nt(qd, kd), 0.0).astype(BF16)
                    da = jnp.where(tri[d], da_full, 0.0).astype(BF16)
                    stb = st_ref[d, pl.ds(r * cpt, cpt)]
                    dqd = _dot(da, kd) + jnp.einsum(
                        'ncv,nvk->nck', do.reshape(cpt, CHUNK, HG_DIM), stb,
                        preferred_element_type=F32).reshape(TM, HG_DIM)
                    dkd = _dot_tn(da, qd)
                    dv = dv + _dot_tn(a, do)
                    dk = dk + dkd * enb
                    db = db + qdf * dqd - kdf * dkd
                    dq = dq + dqd * eb
                dg = _dot_lhs01(later01[d], db) + dbt_s[d, pl.ds(r0, TM), :]
                df = dg / f - dk
                dp_ref[pl.ds(r0, TM), d * HG_DIM:(d + 1) * HG_DIM] = (
                    df * (1.0 - lb[d]) * sz * (1.0 - sz)).astype(BF16)
                dlbs.append(jnp.sum(df * (1.0 - sz), axis=0, keepdims=True))
            dp_ref[pl.ds(r0, TM), 2 * HG_DIM:3 * HG_DIM] = dv.astype(BF16)
            if latent:
                dq = dq * (HG_DIM ** -0.5) * (sq * (1.0 + qr * (1.0 - sq)))
            dp_ref[pl.ds(r0, TM), 3 * HG_DIM:4 * HG_DIM] = dq.astype(BF16)
            return dlbs

        dlb_ctx = grad_tile(0, False)

        def grads(r, acc):
            t = grad_tile(r, True)
            return (acc[0] + t[0], acc[1] + t[1])

        dlb = lax.fori_loop(1, N_TILES, grads, (dlb_ctx[0], dlb_ctx[1]))
        dlb_ref[0:1, :] = dlb[0]
        dlb_ref[1:2, :] = dlb[1]

    return _pcall(
        body, carried, name="hgrn_bwd", grid=(HG_HEADS,),
        in_specs=[pl.BlockSpec((T, 4 * HG_DIM), lambda h: (0, h)),
                  pl.BlockSpec((2, 2, HG_DIM), lambda h: (0, 0, h)),
                  pl.BlockSpec((S, HG_DIM), lambda h: (0, h)),
                  pl.BlockSpec((2, None, N_CHUNKS, HG_DIM, HG_DIM), lambda h: (0, h, 0, 0, 0))],
        out_specs=[pl.BlockSpec((T, 4 * HG_DIM), lambda h: (0, h)),
                   pl.BlockSpec((2, HG_DIM), lambda h: (0, h))],
        out_shape=[jax.ShapeDtypeStruct((T, WA), BF16), jax.ShapeDtypeStruct((2, HGW), F32)],
        scratch_shapes=[pltpu.VMEM((2, T, HG_DIM), F32), pltpu.VMEM((2, T, HG_DIM), F32),
                        pltpu.VMEM((2, T, HG_DIM), F32), pltpu.VMEM((2, S, HG_DIM), BF16),
                        pltpu.VMEM((2, N_CHUNKS, HG_DIM, HG_DIM), BF16),
                        pltpu.VMEM((2, N_CHUNKS, HG_DIM, HG_DIM), BF16)],
        operands=[p_a, lbl, d_o, st])


def _rope_tables():
    t = np.arange(S)
    inv = ROPE_THETA ** (-np.arange(0, 32, 2, dtype=np.float64) / 32)
    lane = np.arange(64)
    pos = np.where(lane[None, :] < 32, (t // GRID_W)[:, None], (t % GRID_W)[:, None]).astype(np.float64)
    ang = pos * inv[(lane % 32) % 16][None, :]
    sign = np.where((lane % 32) < 16, -1.0, 1.0)[None, :]
    cos = np.tile(np.cos(ang), (1, 2)).astype(np.float32)
    sin = np.tile(np.sin(ang) * sign, (1, 2)).astype(np.float32)
    return jnp.asarray(cos), jnp.asarray(sin)


def _rope_partner(v):
    lane = lax.broadcasted_iota(jnp.int32, (1, 128), 1)
    first = (lane % 32) < 16
    slabs = []
    for j in range(v.shape[1] // 128):
        s = v[:, 128 * j:128 * (j + 1)]
        slabs.append(jnp.where(first, pltpu.roll(s, 112, 1), pltpu.roll(s, 16, 1)))
    return slabs[0] if len(slabs) == 1 else jnp.concatenate(slabs, axis=1)


def _group_ones(width, group):
    r = lax.broadcasted_iota(jnp.int32, (width, width), 0)
    c = lax.broadcasted_iota(jnp.int32, (width, width), 1)
    return jnp.where((r // group) == (c // group), 1.0, 0.0).astype(BF16)


def _group_mean(v, ones01, group):
    hi = v.astype(BF16)
    lo = (v - hi.astype(F32)).astype(BF16)
    return (_dot(hi, ones01) + _dot(lo, ones01)) * (1.0 / group)


def _rep_matrix():
    r = lax.broadcasted_iota(jnp.int32, (KVW, ATW), 0)
    c = lax.broadcasted_iota(jnp.int32, (KVW, ATW), 1)
    return jnp.where(r == HEAD_DIM * (c // 256) + c % HEAD_DIM, 1.0, 0.0).astype(BF16)


def _tile_lanes(v, reps):
    return jnp.concatenate([v] * reps, axis=1)


def _prep_fwd(p_b, o, cos, sin, hnw, qnw, knw):
    def body(p_ref, o_ref, cos_ref, sin_ref, hnw_ref, qnw_ref, knw_ref, y_ref, q_ref, k_ref, v_ref):
        i = pl.program_id(0)
        rep = _rep_matrix()
        ones_k = _group_ones(KVW, HEAD_DIM)
        kr = p_ref[:, 1024:1152]
        krstd = lax.rsqrt(_group_mean(kr * kr, ones_k, HEAD_DIM) + EPS)
        kn = kr * krstd * knw_ref[...]
        v_ref[...] = _dot(p_ref[:, 1152:1280].astype(BF16), rep).astype(BF16)

        @pl.when(i == 0)
        def _():
            k_ref[...] = _dot(kn.astype(BF16), rep).astype(BF16)

        @pl.when(i > 0)
        def _():
            cs, sn = cos_ref[...], sin_ref[...]
            kro = kn * cs + _rope_partner(kn) * sn
            k_ref[...] = _dot(kro.astype(BF16), rep).astype(BF16)
            qr = p_ref[:, 512:1024]
            qrstd = lax.rsqrt(_group_mean(qr * qr, _group_ones(ATW, HEAD_DIM), HEAD_DIM) + EPS)
            qn = qr * qrstd * qnw_ref[...]
            qro = qn * _tile_lanes(cs, 4) + _rope_partner(qn) * _tile_lanes(sn, 4)
            q_ref[...] = (qro * HEAD_DIM ** -0.5).astype(BF16)
            ys = []
            for h in range(HG_HEADS):
                oh = o_ref[:, HG_DIM * h:HG_DIM * (h + 1)]
                gh = p_ref[:, HG_DIM * h:HG_DIM * (h + 1)]
                rstd = lax.rsqrt(jnp.mean(oh * oh, axis=-1, keepdims=True) + EPS)
                ys.append(oh * rstd * hnw_ref[...] * (gh * _sigmoid(gh)))
            y_ref[...] = jnp.concatenate(ys, axis=1).astype(BF16)

    return pl.pallas_call(
        body, name="prep_fwd", grid=(N_TILES,),
        in_specs=[pl.BlockSpec((TM, WB), lambda i: (i, 0)),
                  pl.BlockSpec((TM, HGW), lambda i: (_lat(i), 0)),
                  pl.BlockSpec((TM, 128), lambda i: (_lat(i), 0)),
                  pl.BlockSpec((TM, 128), lambda i: (_lat(i), 0)),
                  _full((1, HG_DIM)), _full((1, ATW)), _full((1, KVW))],
        out_specs=[pl.BlockSpec((TM, HGW), lambda i: (_lat(i), 0)),
                   pl.BlockSpec((TM, ATW), lambda i: (_lat(i), 0)),
                   pl.BlockSpec((TM, ATW), lambda i: (i, 0)),
                   pl.BlockSpec((TM, ATW), lambda i: (i, 0))],
        out_shape=[jax.ShapeDtypeStruct((S, HGW), BF16), jax.ShapeDtypeStruct((S, ATW), BF16),
                   jax.ShapeDtypeStruct((T, ATW), BF16), jax.ShapeDtypeStruct((T, ATW), BF16)],
        compiler_params=_cp(("arbitrary",)),
    )(p_b, o, cos, sin, hnw, qnw, knw)


def _prep_bwd(p_b, o, cos, sin, hnw, qnw, knw, dy_hg, dq, dk_rep, dv_rep, carried=None):
    def body(p_ref, o_ref, cos_ref, sin_ref, hnw_ref, qnw_ref, knw_ref, dy_ref, dq_ref, dk_ref, dv_ref,
             dp_ref, do_ref, acc_ref):
        i = pl.program_id(0)

        @pl.when(i == 0)
        def _():
            acc_ref[...] = jnp.zeros_like(acc_ref)

        rep = _rep_matrix()
        ones_k = _group_ones(KVW, HEAD_DIM)

        def fold(v):
            hi = v.astype(BF16)
            lo = (v - hi.astype(F32)).astype(BF16)
            return _dot_nt(hi, rep) + _dot_nt(lo, rep)

        kr = p_ref[:, 1024:1152]
        krstd = lax.rsqrt(_group_mean(kr * kr, ones_k, HEAD_DIM) + EPS)
        khat = kr * krstd
        kw = knw_ref[...]
        dkro = fold(dk_ref[...])
        dv = fold(dv_ref[...])

        def k_back(dkn):
            dkhat = dkn * kw
            dkr = krstd * (dkhat - khat * _group_mean(dkhat * khat, ones_k, HEAD_DIM))
            acc_ref[2:3, 0:KVW] += jnp.sum(dkn * khat, axis=0, keepdims=True)
            dp_ref[:, 1024:1152] = dkr.astype(BF16)
            dp_ref[:, 1152:1280] = dv.astype(BF16)

        @pl.when(i == 0)
        def _():
            k_back(dkro)
            dp_ref[:, 0:1024] = jnp.zeros((TM, 1024), BF16)

        @pl.when(i > 0)
        def _():
            cs, sn = cos_ref[...], sin_ref[...]
            k_back(dkro * cs + _rope_partner(dkro * sn))
            ones_q = _group_ones(ATW, HEAD_DIM)
            qr = p_ref[:, 512:1024]
            qrstd = lax.rsqrt(_group_mean(qr * qr, ones_q, HEAD_DIM) + EPS)
            qhat = qr * qrstd
            dqro = dq_ref[...] * HEAD_DIM ** -0.5
            dqn = dqro * _tile_lanes(cs, 4) + _rope_partner(dqro * _tile_lanes(sn, 4))
            dqhat = dqn * qnw_ref[...]
            dqr = qrstd * (dqhat - qhat * _group_mean(dqhat * qhat, ones_q, HEAD_DIM))
            acc_ref[1:2, :] += jnp.sum(dqn * qhat, axis=0, keepdims=True)
            dp_ref[:, 512:1024] = dqr.astype(BF16)
            dws = jnp.zeros((1, HG_DIM), F32)
            for h in range(HG_HEADS):
                sl = slice(HG_DIM * h, HG_DIM * (h + 1))
                oh, gh, dy = o_ref[:, sl], p_ref[:, sl], dy_ref[:, sl]
                rstd = lax.rsqrt(jnp.mean(oh * oh, axis=-1, keepdims=True) + EPS)
                ohat = oh * rstd
                sg = _sigmoid(gh)
                dp_ref[:, sl] = (dy * (ohat * hnw_ref[...]) * (sg * (1.0 + gh * (1.0 - sg)))).astype(BF16)
                dn = dy * (gh * sg)
                dws = dws + jnp.sum(dn * ohat, axis=0, keepdims=True)
                dohat = dn * hnw_ref[...]
                do_ref[:, sl] = rstd * (dohat - ohat * jnp.mean(dohat * ohat, axis=-1, keepdims=True))
            acc_ref[0:1, 0:HG_DIM] += dws

    return _pcall(
        body, carried, name="prep_bwd", grid=(N_TILES,),
        in_specs=[pl.BlockSpec((TM, WB), lambda i: (i, 0)),
                  pl.BlockSpec((TM, HGW), lambda i: (_lat(i), 0)),
                  pl.BlockSpec((TM, 128), lambda i: (_lat(i), 0)),
                  pl.BlockSpec((TM, 128), lambda i: (_lat(i), 0)),
                  _full((1, HG_DIM)), _full((1, ATW)), _full((1, KVW)),
                  pl.BlockSpec((TM, HGW), lambda i: (_lat(i), 0)),
                  pl.BlockSpec((TM, ATW), lambda i: (_lat(i), 0)),
                  pl.BlockSpec((TM, ATW), lambda i: (i, 0)),
                  pl.BlockSpec((TM, ATW), lambda i: (i, 0))],
        out_specs=[pl.BlockSpec((TM, WB), lambda i: (i, 0)),
                   pl.BlockSpec((TM, HGW), lambda i: (_lat(i), 0)),
                   _full((8, ATW))],
        out_shape=[jax.ShapeDtypeStruct((T, WB), BF16), jax.ShapeDtypeStruct((S, HGW), F32),
                   jax.ShapeDtypeStruct((8, ATW), F32)],
        scratch_shapes=[], operands=[p_b, o, cos, sin, hnw, qnw, knw, dy_hg, dq, dk_rep, dv_rep])


NEG = -1e30
_CTX_BLOCKS = L // BLOCK


def _attn_window_specs():
    prev = pl.BlockSpec((BLOCK, ATW), lambda i: (jnp.maximum(i - 1, 0) + _CTX_BLOCKS, 0))
    own = pl.BlockSpec((BLOCK, ATW), lambda i: (i + _CTX_BLOCKS, 0))
    nxt = pl.BlockSpec((BLOCK, ATW), lambda i: (jnp.minimum(i + 1, N_BLOCKS - 1) + _CTX_BLOCKS, 0))
    return [prev, own, nxt, _full((L, ATW))]


def _attn_valid(i, heads, context):
    n_keys = 3 * BLOCK + (L if context else 0)
    qi = lax.broadcasted_iota(jnp.int32, (heads * BLOCK, n_keys), 0) % BLOCK
    kj = lax.broadcasted_iota(jnp.int32, (heads * BLOCK, n_keys), 1)
    window = ((jnp.abs(kj - BLOCK - qi) <= BLOCK) & ((kj >= BLOCK) | (i > 0))
              & ((kj < 2 * BLOCK) | (i < N_BLOCKS - 1)))
    return window | (kj >= 3 * BLOCK)


def _stack_heads(qg):
    lane = lax.broadcasted_iota(jnp.int32, (1, 256), 1) // HEAD_DIM
    return jnp.concatenate([jnp.where(lane == g, qg, jnp.zeros_like(qg)) for g in range(4)], axis=0)


def _unstack_heads(v4):
    lane = lax.broadcasted_iota(jnp.int32, (1, 256), 1) // HEAD_DIM
    out = jnp.where(lane == 0, v4[0:BLOCK], 0.0)
    for g in range(1, 4):
        out = out + jnp.where(lane == g, v4[g * BLOCK:(g + 1) * BLOCK], 0.0)
    return out


def _sink_rows(sink_ref, hk):
    return jnp.concatenate(
        [jnp.broadcast_to(sink_ref[0:1, 4 * hk + g:4 * hk + g + 1], (BLOCK, 1)) for g in range(4)], axis=0)


def _attn_fwd(q, k_rep, v_rep, sinks, carried=None):
    def body(q_ref, kp, ko, kn, kc, vp, vo, vn, vc, sink_ref, y_ref, lse_ref):
        i = pl.program_id(0)
        valid = _attn_valid(i, 1, True)
        lane8 = lax.broadcasted_iota(jnp.int32, (1, ATT_HEADS), 1)
        head_of_lane = lax.broadcasted_iota(jnp.int32, (1, 256), 1) // HEAD_DIM
        lse_out = jnp.zeros((BLOCK, ATT_HEADS), F32)
        for hk in range(KV_HEADS):
            sl = slice(256 * hk, 256 * (hk + 1))
            qg = q_ref[:, sl]
            keys = jnp.concatenate([kp[:, sl], ko[:, sl], kn[:, sl], kc[:, sl]], axis=0)
            vals = jnp.concatenate([vp[:, sl], vo[:, sl], vn[:, sl], vc[:, sl]], axis=0)
            yg = jnp.zeros((BLOCK, 256), F32)
            for g in range(4):
                q1 = jnp.where(head_of_lane == g, qg, jnp.zeros_like(qg))
                s = jnp.where(valid, _dot_nt(q1, keys), NEG)
                sink = sink_ref[0:1, 4 * hk + g:4 * hk + g + 1]
                m = jnp.maximum(jnp.max(s, axis=1, keepdims=True), sink)
                p = jnp.exp(s - m)
                den = jnp.sum(p, axis=1, keepdims=True) + jnp.exp(sink - m)
                o1 = _dot(p.astype(BF16), vals) * (1.0 / den)
                yg = yg + jnp.where(head_of_lane == g, o1, 0.0)
                lse_out = lse_out + jnp.where(lane8 == 4 * hk + g, m + jnp.log(den), 0.0)
            y_ref[:, sl] = yg.astype(BF16)
        lse_ref[...] = lse_out

    return _pcall(
        body, carried, name="attn_fwd", grid=(N_BLOCKS,),
        in_specs=[pl.BlockSpec((BLOCK, ATW), lambda i: (i, 0))] + _attn_window_specs()
        + _attn_window_specs() + [_full((1, ATT_HEADS))],
        out_specs=[pl.BlockSpec((BLOCK, ATW), lambda i: (i, 0)),
                   pl.BlockSpec((BLOCK, ATT_HEADS), lambda i: (i, 0))],
        out_shape=[jax.ShapeDtypeStruct((S, ATW), BF16), jax.ShapeDtypeStruct((S, ATT_HEADS), F32)],
        scratch_shapes=[],
        operands=[q, k_rep, k_rep, k_rep, k_rep, v_rep, v_rep, v_rep, v_rep, sinks])


def _attn_bwd(q, k_rep, v_rep, sinks, y_at, lse, dy, carried=None):
    def body(q_ref, kp, ko, kn, kc, vp, vo, vn, vc, sink_ref, y_ref, lse_ref, dy_ref,
             dq_ref, dk_ref, dv_ref, dsink_ref, dk_acc, dv_acc):
        i = pl.program_id(0)

        @pl.when(i == 0)
        def _():
            dk_acc[...] = jnp.zeros_like(dk_acc)
            dv_acc[...] = jnp.zeros_like(dv_acc)
            dk_ref[pl.ds(0, L), :] = jnp.zeros((L, ATW), F32)
            dv_ref[pl.ds(0, L), :] = jnp.zeros((L, ATW), F32)
            dsink_ref[...] = jnp.zeros_like(dsink_ref)

        valid = _attn_valid(i, 4, False)
        lane8 = lax.broadcasted_iota(jnp.int32, (1, ATT_HEADS), 1)
        w0 = pl.multiple_of(i * BLOCK, BLOCK)
        dsink = jnp.zeros((1, ATT_HEADS), F32)
        for hk in range(KV_HEADS):
            sl = slice(256 * hk, 256 * (hk + 1))
            q4 = _stack_heads(q_ref[:, sl])
            do4f = _stack_heads(dy_ref[:, sl])
            o4 = _stack_heads(y_ref[:, sl]).astype(F32)
            do4 = do4f.astype(BF16)
            kl = jnp.concatenate([kp[:, sl], ko[:, sl], kn[:, sl]], axis=0)
            vl = jnp.concatenate([vp[:, sl], vo[:, sl], vn[:, sl]], axis=0)
            lse4 = jnp.concatenate(
                [jnp.sum(jnp.where(lane8 == 4 * hk + g, lse_ref[...], 0.0), axis=1, keepdims=True)
                 for g in range(4)], axis=0)
            p_loc = jnp.where(valid, jnp.exp(_dot_nt(q4, kl) - lse4), 0.0)
            p_ctx = jnp.exp(_dot_nt(q4, kc[:, sl]) - lse4)
            delta = jnp.sum(do4f * o4, axis=1, keepdims=True)
            ds_loc = (p_loc * (_dot_nt(do4, vl) - delta)).astype(BF16)
            ds_ctx = (p_ctx * (_dot_nt(do4, vc[:, sl]) - delta)).astype(BF16)
            dq_ref[:, sl] = _unstack_heads(_dot(ds_loc, kl) + _dot(ds_ctx, kc[:, sl]))
            dk_acc[pl.ds(w0, 3 * BLOCK), sl] += _dot_tn(ds_loc, q4)
            dv_acc[pl.ds(w0, 3 * BLOCK), sl] += _dot_tn(p_loc.astype(BF16), do4)
            dk_ref[pl.ds(0, L), sl] += _dot_tn(ds_ctx, q4)
            dv_ref[pl.ds(0, L), sl] += _dot_tn(p_ctx.astype(BF16), do4)
            p_sink = jnp.exp(_sink_rows(sink_ref, hk) - lse4)
            for g in range(4):
                rows = slice(g * BLOCK, (g + 1) * BLOCK)
                dsink = dsink + jnp.where(lane8 == 4 * hk + g,
                                          -jnp.sum(p_sink[rows] * delta[rows], axis=0, keepdims=True), 0.0)
        dsink_ref[...] += dsink

        @pl.when(i == N_BLOCKS - 1)
        def _():
            dk_ref[pl.ds(L, S), :] = dk_acc[pl.ds(BLOCK, S), :]
            dv_ref[pl.ds(L, S), :] = dv_acc[pl.ds(BLOCK, S), :]

    row_q = pl.BlockSpec((BLOCK, ATW), lambda i: (i, 0))
    return _pcall(
        body, carried, name="attn_bwd", grid=(N_BLOCKS,),
        in_specs=[row_q] + _attn_window_specs() + _attn_window_specs()
        + [_full((1, ATT_HEADS)), row_q, pl.BlockSpec((BLOCK, ATT_HEADS), lambda i: (i, 0)), row_q],
        out_specs=[row_q, _full((T, ATW)), _full((T, ATW)), _full((1, ATT_HEADS))],
        out_shape=[jax.ShapeDtypeStruct((S, ATW), F32), jax.ShapeDtypeStruct((T, ATW), F32),
                   jax.ShapeDtypeStruct((T, ATW), F32), jax.ShapeDtypeStruct((1, ATT_HEADS), F32)],
        scratch_shapes=[pltpu.VMEM((S + 2 * BLOCK, ATW), F32), pltpu.VMEM((S + 2 * BLOCK, ATW), F32)],
        operands=[q, k_rep, k_rep, k_rep, k_rep, v_rep, v_rep, v_rep, v_rep, sinks, y_at, lse, dy])


def _merge_fwd(y_hg, y_at, p_c, x, w_bh, w_ba, w_out, g1, nfw, sh2, sc2, carried=None):
    def body(yh_ref, ya_ref, g_ref, x_ref, wbh_ref, wba_ref, wo_ref, g1_ref, nfw_ref, sh_ref, sc_ref,
             mx_ref, r_ref, x1_ref, h2_ref):
        a = _dot_nt(yh_ref[...], wbh_ref[...])
        b = _dot_nt(ya_ref[...], wba_ref[...])
        mixed = (_sigmoid(g_ref[:, :D]) * a + _sigmoid(g_ref[:, D:]) * b).astype(BF16)
        r = _dot(mixed, wo_ref[...])
        x1 = x_ref[...] + g1_ref[...] * r
        mx_ref[...] = mixed
        r_ref[...] = r
        x1_ref[...] = x1
        h2_ref[...] = _rms_mod(x1, nfw_ref[...], sh_ref[...], sc_ref[...]).astype(BF16)

    row = lambda w: pl.BlockSpec((TM, w), lambda i: (i, 0))
    vec = _full((1, D))
    return _pcall(
        body, carried, name="merge_fwd", grid=(N_LAT_TILES,),
        in_specs=[row(HGW), row(ATW), row(WC), row(D), _VMEM_WHOLE, _VMEM_WHOLE, _VMEM_WHOLE,
                  vec, vec, vec, vec],
        out_specs=[row(D)] * 4,
        out_shape=[jax.ShapeDtypeStruct((S, D), dt) for dt in (BF16, F32, F32, BF16)],
        scratch_shapes=[], operands=[y_hg, y_at, p_c, x, w_bh, w_ba, w_out, g1, nfw, sh2, sc2])


def _merge_bwd(dx1, r, y_hg, y_at, p_c, w_bh, w_ba, w_out, g1, carried=None):
    def body(dx_ref, r_ref, yh_ref, ya_ref, g_ref, wbh_ref, wba_ref, wo_ref, g1_ref,
             dr_ref, da_ref, db_ref, dg_ref, dyh_ref, dya_ref, acc_ref):
        @pl.when(pl.program_id(0) == 0)
        def _():
            acc_ref[...] = jnp.zeros_like(acc_ref)

        dx1v = dx_ref[...]
        acc_ref[0:1, :] += jnp.sum(dx1v * r_ref[...], axis=0, keepdims=True)
        dr = (g1_ref[...] * dx1v).astype(BF16)
        dr_ref[...] = dr
        dmix = _dot_nt(dr, wo_ref[...])
        sh, sa = _sigmoid(g_ref[:, :D]), _sigmoid(g_ref[:, D:])
        da = (dmix * sh).astype(BF16)
        db = (dmix * sa).astype(BF16)
        da_ref[...] = da
        db_ref[...] = db
        dg_ref[:, :D] = (dmix * _dot_nt(yh_ref[...], wbh_ref[...]) * sh * (1.0 - sh)).astype(BF16)
        dg_ref[:, D:] = (dmix * _dot_nt(ya_ref[...], wba_ref[...]) * sa * (1.0 - sa)).astype(BF16)
        dyh_ref[...] = _dot(da, wbh_ref[...])
        dya_ref[...] = _dot(db, wba_ref[...])

    row = lambda w: pl.BlockSpec((TM, w), lambda i: (i, 0))
    return _pcall(
        body, carried, name="merge_bwd", grid=(N_LAT_TILES,),
        in_specs=[row(D), row(D), row(HGW), row(ATW), row(WC), _VMEM_WHOLE, _VMEM_WHOLE, _VMEM_WHOLE,
                  _full((1, D))],
        out_specs=[row(D), row(D), row(D), row(WC), row(HGW), row(ATW), _full((8, D))],
        out_shape=[jax.ShapeDtypeStruct((S, D), BF16), jax.ShapeDtypeStruct((S, D), BF16),
                   jax.ShapeDtypeStruct((S, D), BF16), jax.ShapeDtypeStruct((S, WC), BF16),
                   jax.ShapeDtypeStruct((S, HGW), F32), jax.ShapeDtypeStruct((S, ATW), F32),
                   jax.ShapeDtypeStruct((8, D), F32)],
        scratch_shapes=[], operands=[dx1, r, y_hg, y_at, p_c, w_bh, w_ba, w_out, g1])


def _ffn_fused(x1, h2, tgt, w_gate, w_up, w_down, g2, nfw, sc2):
    def body(x1_ref, h2_ref, t_ref, wg_ref, wu_ref, wd_ref, g2_ref, nfw_ref, sc_ref,
             act_ref, dgt_ref, dup_ref, df_ref, dx_ref, acc_ref, gs, us):
        @pl.when(pl.program_id(0) == 0)
        def _():
            acc_ref[...] = jnp.zeros_like(acc_ref)

        h2 = h2_ref[...]
        whole = lambda w_ref: w_ref[...].reshape(D_FF, D)
        wide = lambda t_ref: jnp.concatenate([t_ref[j] for j in range(N_FF_TILES)], axis=1)
        for j in range(N_FF_TILES):
            g = _dot_nt(h2, wg_ref[j])
            u = _dot_nt(h2, wu_ref[j])
            gs[j] = g
            us[j] = u
            act_ref[j] = (g * _sigmoid(g) * u).astype(BF16)
        f = _dot(wide(act_ref), whole(wd_ref))
        x1v = x1_ref[...]
        g2 = g2_ref[...]
        diff = x1v + g2 * f - t_ref[...]
        dy = diff * (1.0 / D)
        df = (g2 * dy).astype(BF16)
        df_ref[...] = df
        dact_all = _dot_nt(df, whole(wd_ref))
        for j in range(N_FF_TILES):
            g, u = gs[j], us[j]
            sg = _sigmoid(g)
            dact = dact_all[:, j * FF_TILE:(j + 1) * FF_TILE]
            dgt_ref[j] = (dact * u * (sg * (1.0 + g * (1.0 - sg)))).astype(BF16)
            dup_ref[j] = (dact * (g * sg)).astype(BF16)
        dh2 = _dot(wide(dgt_ref), whole(wg_ref)) + _dot(wide(dup_ref), whole(wu_ref))
        dx, dsh, dsc, dnw = _rms_mod_bwd(x1v, nfw_ref[...], sc_ref[...], dh2)
        dx_ref[...] = dy + dx
        acc_ref[0:1, :] += dsh
        acc_ref[1:2, :] += dsc
        acc_ref[2:3, :] += dnw
        acc_ref[3:4, :] += jnp.sum(dy * f, axis=0, keepdims=True)
        acc_ref[4:5, :] += 0.5 * jnp.sum(jnp.sum(diff * diff, axis=1, keepdims=True), axis=0,
                                         keepdims=True) * (1.0 / D)

    row = lambda dt_w: pl.BlockSpec((TM, dt_w), lambda i: (i, 0))
    blk = pl.BlockSpec((N_FF_TILES, TM, FF_TILE), lambda i: (0, i, 0))
    vec = _full((1, D))
    return pl.pallas_call(
        body, name="ffn_fused", grid=(N_LAT_TILES,),
        in_specs=[row(D), row(D), row(D), _VMEM_WHOLE, _VMEM_WHOLE, _VMEM_WHOLE, vec, vec, vec],
        out_specs=[blk, blk, blk, row(D), row(D), _full((8, D))],
        out_shape=[jax.ShapeDtypeStruct((N_FF_TILES, S, FF_TILE), BF16)] * 3
        + [jax.ShapeDtypeStruct((S, D), BF16), jax.ShapeDtypeStruct((S, D), F32),
           jax.ShapeDtypeStruct((8, D), F32)],
        scratch_shapes=[pltpu.VMEM((N_FF_TILES, TM, FF_TILE), F32), pltpu.VMEM((N_FF_TILES, TM, FF_TILE), F32)],
        compiler_params=_cp(("arbitrary",)),
    )(x1, h2, tgt, w_gate, w_up, w_down, g2, nfw, sc2)


def _proj_bc(h_all, w_b, w_c, carried=None):
    def body(h_ref, wb_ref, wc_ref, pb_ref, pc_ref):
        h = h_ref[...]
        pb_ref[...] = _dot_nt(h, wb_ref[...])

        @pl.when(pl.program_id(0) > 0)
        def _():
            pc_ref[...] = _dot_nt(h, wc_ref[...])

    return _pcall(
        body, carried, name="proj_bc", grid=(N_TILES,),
        in_specs=[pl.BlockSpec((TM, D), lambda i: (i, 0)), _VMEM_WHOLE, _VMEM_WHOLE],
        out_specs=[pl.BlockSpec((TM, WB), lambda i: (i, 0)), pl.BlockSpec((TM, WC), lambda i: (_lat(i), 0))],
        out_shape=[jax.ShapeDtypeStruct((T, WB), F32), jax.ShapeDtypeStruct((S, WC), F32)],
        scratch_shapes=[], operands=[h_all, w_b, w_c])


def _input_bwd(dp_a, dp_b, dp_c, w_a, w_b, w_c, ctx, x, dx1, nw, sh, sc, carried=None):
    def body(da_ref, db_ref, dc_ref, wa_ref, wb_ref, wc_ref, ctx_ref, x_ref, dx1_ref, nw_ref, sh_ref,
             sc_ref, gx_ref, acc_ref):
        i = pl.program_id(0)

        @pl.when(i == 0)
        def _():
            acc_ref[...] = jnp.zeros_like(acc_ref)

        dh = _dot(da_ref[...], wa_ref[...]) + _dot(db_ref[...], wb_ref[...])

        @pl.when(i == 0)
        def _():
            _, dsh, dsc, dnw = _rms_mod_bwd(ctx_ref[...], nw_ref[...], sc_ref[0:1, :], dh)
            acc_ref[3:4, :] += dsh
            acc_ref[4:5, :] += dsc
            acc_ref[2:3, :] += dnw

        @pl.when(i > 0)
        def _():
            dhl = dh + _dot(dc_ref[...], wc_ref[...])
            dx, dsh, dsc, dnw = _rms_mod_bwd(x_ref[...], nw_ref[...], sc_ref[1:2, :], dhl)
            gx_ref[...] = dx1_ref[...] + dx
            acc_ref[0:1, :] += dsh
            acc_ref[1:2, :] += dsc
            acc_ref[2:3, :] += dnw

    lat = lambda w: pl.BlockSpec((TM, w), lambda i: (_lat(i), 0))
    return _pcall(
        body, carried, name="input_bwd", grid=(N_TILES,),
        in_specs=[pl.BlockSpec((TM, WA), lambda i: (i, 0)), pl.BlockSpec((TM, WB), lambda i: (i, 0)),
                  lat(WC), _VMEM_WHOLE, _VMEM_WHOLE, _VMEM_WHOLE, _full((TM, D)), lat(D), lat(D),
                  _full((1, D)), _full((2, D)), _full((2, D))],
        out_specs=[lat(D), _full((8, D))],
        out_shape=[jax.ShapeDtypeStruct((S, D), F32), jax.ShapeDtypeStruct((8, D), F32)],
        scratch_shapes=[], operands=[dp_a, dp_b, dp_c, w_a, w_b, w_c, ctx, x, dx1, nw, sh, sc])


_C1 = 1.0 - ADAM_B1 ** ADAM_STEP
_C2 = 1.0 - ADAM_B2 ** ADAM_STEP


def _adamw_math(w, g, m, v):
    m = ADAM_B1 * m + (1.0 - ADAM_B1) * g
    v = ADAM_B2 * v + (1.0 - ADAM_B2) * (g * g)
    m_hat = m / _C1
    v_hat = v / _C2
    delta = -ADAM_LR * (m_hat / (jnp.sqrt(v_hat) + ADAM_EPS) + ADAM_WD * w)
    return delta, m, v


def _adamw_sharded(terms, w, m, v, name, tr, extra=None):
    rows, cols = w.shape

    def body(*refs):
        t_ref, w_ref, m_ref, v_ref = refs[:4]
        g_ref, d_ref, nm_ref, nv_ref = refs[-4:]
        g = t_ref[0].astype(F32)
        for s in range(1, N_CHIPS):
            g = g + t_ref[s].astype(F32)
        if extra is not None:
            g = g + refs[4][...].astype(F32)
        g_ref[...] = g
        d_ref[...], nm_ref[...], nv_ref[...] = _adamw_math(w_ref[...], g, m_ref[...], v_ref[...])

    blk = pl.BlockSpec((tr, cols), lambda i: (i, 0))
    return pl.pallas_call(
        body, name=name, grid=(rows // tr,),
        in_specs=[pl.BlockSpec((N_CHIPS, tr, cols), lambda i: (0, i, 0)), blk, blk, blk]
        + ([blk] if extra is not None else []),
        out_specs=[blk] * 4,
        out_shape=[jax.ShapeDtypeStruct((rows, cols), F32)] * 4,
        compiler_params=_cp(("parallel",)),
    )(terms, w, m, v, *([extra] if extra is not None else []))


def _adamw_plain(g, w, m, v, name):
    def body(g_ref, w_ref, m_ref, v_ref, d_ref, nm_ref, nv_ref):
        d_ref[...], nm_ref[...], nv_ref[...] = _adamw_math(w_ref[...], g_ref[...], m_ref[...], v_ref[...])

    return pl.pallas_call(
        body, name=name, in_specs=[_VMEM_WHOLE] * 4, out_specs=[_VMEM_WHOLE] * 3,
        out_shape=[jax.ShapeDtypeStruct(w.shape, F32)] * 3,
        compiler_params=_cp(),
    )(g, w, m, v)


SMALL_ROWS = 16
R_DMOD, R_DCTX, R_NMIX, R_NFFN, R_MISC, R_DLB, R_BADA01 = 0, 6, 8, 9, 10, 11, 13
M_HNW, M_QNW, M_KNW, M_SINK, M_LOSS = 0, 128, 256, 384, 512


def _pack_small(acc_in, acc_mg, acc_ffn, acc_prep, dsink, dlb):
    def body(in_ref, mg_ref, ff_ref, pp_ref, ds_ref, dlb_ref, o_ref):
        o_ref[...] = jnp.zeros_like(o_ref)
        o_ref[0:2, :] = in_ref[0:2, :]
        o_ref[2:3, :] = mg_ref[0:1, :]
        o_ref[3:5, :] = ff_ref[0:2, :]
        o_ref[5:6, :] = ff_ref[3:4, :]
        o_ref[6:8, :] = in_ref[3:5, :]
        o_ref[8:9, :] = in_ref[2:3, :]
        o_ref[9:10, :] = ff_ref[2:3, :]
        o_ref[10:11, M_HNW:M_HNW + HG_DIM] = pp_ref[0:1, 0:HG_DIM]
        r = lax.broadcasted_iota(jnp.int32, (ATW, 128), 0)
        c = lax.broadcasted_iota(jnp.int32, (ATW, 128), 1)
        fold = jnp.where((r % HEAD_DIM == c) & (c < HEAD_DIM), 1.0, 0.0).astype(BF16)
        qk = jnp.concatenate([pp_ref[1:2, :], pp_ref[2:3, :], jnp.zeros((6, ATW), F32)], axis=0)
        folded = _dot_exact_rhs01(qk, fold)
        o_ref[10:11, M_QNW:M_QNW + 128] = folded[0:1, :]
        o_ref[10:11, M_KNW:M_KNW + 128] = folded[1:2, :]
        o_ref[10:11, M_SINK:M_SINK + ATT_HEADS] = ds_ref[...]
        o_ref[10:11, M_LOSS:M_LOSS + 128] = ff_ref[4:5, 0:128]
        o_ref[11:13, 0:HGW] = dlb_ref[...]

    return pl.pallas_call(
        body, name="pack_small", in_specs=[_VMEM_WHOLE] * 6, out_specs=_VMEM_WHOLE,
        out_shape=jax.ShapeDtypeStruct((SMALL_ROWS, D), F32), compiler_params=_cp(),
    )(acc_in, acc_mg, acc_ffn, acc_prep, dsink, dlb)


def _sum_small(gathered):
    def body(g_ref, o_ref):
        tot = g_ref[0]
        for s in range(1, N_DEV):
            tot = tot + g_ref[s]
        o_ref[...] = tot
        o_ref[R_BADA01:R_BADA01 + 2, :] = tot[0:2, :] + tot[R_DCTX:R_DCTX + 2, :]

    return pl.pallas_call(
        body, name="sum_small", in_specs=[_VMEM_WHOLE], out_specs=_VMEM_WHOLE,
        out_shape=jax.ShapeDtypeStruct((SMALL_ROWS, D), F32), compiler_params=_cp(),
    )(gathered)


_REP_NAMES = ("b_ada", "c_ctx", "norm_mix_w", "norm_ffn_w", "hgrn_norm_w", "q_norm_w", "k_norm_w", "attn_sinks")


def _adamw_replicated(tot, g_c_ctx, ws, ms, vs):
    n = len(_REP_NAMES)

    def body(*refs):
        tot_ref, gc_ref = refs[0], refs[1]
        w_refs, m_refs, v_refs = refs[2:2 + n], refs[2 + n:2 + 2 * n], refs[2 + 2 * n:2 + 3 * n]
        outs = refs[2 + 3 * n:]
        row = lambda r: tot_ref[r:r + 1, :]
        misc = row(R_MISC)
        grads = [jnp.concatenate([row(R_BADA01), row(R_BADA01 + 1)] + [row(k) for k in range(2, 6)], axis=1),
                 gc_ref[...], row(R_NMIX), row(R_NFFN),
                 misc[:, M_HNW:M_HNW + HG_DIM], misc[:, M_QNW:M_QNW + HEAD_DIM],
                 misc[:, M_KNW:M_KNW + HEAD_DIM], misc[:, M_SINK:M_SINK + ATT_HEADS]]
        for k in range(n):
            outs[k][...] = grads[k]
            outs[n + k][...], outs[2 * n + k][...], outs[3 * n + k][...] = _adamw_math(
                w_refs[k][...], grads[k], m_refs[k][...], v_refs[k][...])

    shapes = [jax.ShapeDtypeStruct(w.shape, F32) for w in ws]
    return pl.pallas_call(
        body, name="adamw_replicated", in_specs=[_VMEM_WHOLE] * (2 + 3 * n), out_specs=[_VMEM_WHOLE] * (4 * n),
        out_shape=shapes * 4, compiler_params=_cp(),
    )(tot, g_c_ctx, *ws, *ms, *vs)


def _lb_grads(dlb, lbl):
    def body(d_ref, l_ref, o_ref):
        for d in (0, 1):
            ll = l_ref[d]
            lb = _sigmoid(ll[0:1, :] - ll[1:2, :])
            t = d_ref[d:d + 1, :] * lb * (1.0 - lb)
            o_ref[d, 0:1, :] = t
            o_ref[d, 1:2, :] = -t

    return pl.pallas_call(
        body, name="lb_grads", in_specs=[_VMEM_WHOLE] * 2, out_specs=_VMEM_WHOLE,
        out_shape=jax.ShapeDtypeStruct((2, 2, HGW), F32), compiler_params=_cp(),
    )(dlb, lbl)


def _c_ctx_grad(terms, c_ctx):
    def body(t_ref, c_ref, o_ref):
        tot = t_ref[0, 8:9, :]
        for s in range(1, N_DEV):
            tot = tot + t_ref[s, 8:9, :]
        cv = c_ref[...]
        sg = _sigmoid(cv)
        o_ref[...] = tot * (sg * (1.0 + cv * (1.0 - sg)))

    return pl.pallas_call(
        body, name="c_ctx_grad", in_specs=[_VMEM_WHOLE] * 2, out_specs=_VMEM_WHOLE,
        out_shape=jax.ShapeDtypeStruct((1, D), F32), compiler_params=_cp(),
    )(terms, c_ctx)


def _in_perm():
    fz, bz, inp, kk, vv, qhg, ghg, qat, gates = 0, 512, 1024, 1536, 1664, 1792, 2304, 2816, 3328
    cols = []
    for h in range(HG_HEADS):
        for base in (fz, bz, inp, qhg):
            cols += list(range(base + 128 * h, base + 128 * (h + 1)))
    cols += list(range(ghg, ghg + 512)) + list(range(qat, qat + 512))
    cols += list(range(kk, kk + 128)) + list(range(vv, vv + 128))
    cols += list(range(gates, gates + 2048))
    return np.asarray(cols, np.int32)


_PERM = _in_perm()


_PIECES = {"a": (0, WA, 128), "b": (WA, WB, 256), "c": (WA + WB, WC, 256)}


def _block_table(piece):
    lo, n, blk = _PIECES[piece]
    starts = [int(_PERM[r]) for r in range(lo, lo + n, blk)]
    assert all(s % blk == 0 and np.array_equal(_PERM[r:r + blk], np.arange(s, s + blk))
               for s, r in zip(starts, range(lo, lo + n, blk)))
    return jnp.asarray([s // blk for s in starts], jnp.int32), blk


def _pick_row_blocks(x, table, blk, name):
    cols = x.shape[1]

    def body(t_ref, x_ref, o_ref):
        o_ref[...] = x_ref[...]

    return pl.pallas_call(
        body, name=name,
        grid_spec=pltpu.PrefetchScalarGridSpec(
            num_scalar_prefetch=1, grid=(table.shape[0],),
            in_specs=[pl.BlockSpec((blk, cols), lambda i, t: (t[i], 0))],
            out_specs=pl.BlockSpec((blk, cols), lambda i, t: (i, 0))),
        out_shape=jax.ShapeDtypeStruct((table.shape[0] * blk, cols), x.dtype),
        compiler_params=_cp(("arbitrary",)),
    )(table, x)


def _place_row_blocks(x, table, blk, into, out_rows, name):
    cols = x.shape[1]

    def body(t_ref, x_ref, *rest):
        rest[-1][...] = x_ref[...]

    operands, in_specs, aliases = [table, x], [pl.BlockSpec((blk, cols), lambda i, t: (i, 0))], {}
    if into is not None:
        operands.append(into)
        in_specs.append(_ANY)
        aliases = {2: 0}
    return pl.pallas_call(
        body, name=name,
        grid_spec=pltpu.PrefetchScalarGridSpec(
            num_scalar_prefetch=1, grid=(table.shape[0],), in_specs=in_specs,
            out_specs=pl.BlockSpec((blk, cols), lambda i, t: (t[i], 0))),
        out_shape=jax.ShapeDtypeStruct((out_rows, cols), x.dtype),
        input_output_aliases=aliases,
        compiler_params=_cp(("arbitrary",)),
    )(*operands)


def _local_step(x2, ctx2, h_all, tgt, lbl, sh_in, sc_in, gate1, sh2, sc2, gate2, norm_mix_w, norm_ffn_w,
                hgrn_norm_w, q_norm_w, k_norm_w, attn_sinks, w_a, w_b, w_c, s_bh, s_ba, s_out,
                s_gate, s_up, s_down):
    first_last = lambda n: [(0, True), (n - 1, False)]
    p_a = _mm_nt(h_all, w_a, tm=768, tn=1024, out_dtype=F32, name="proj_a")
    (o, st), (g_gate, g_bh, g_ba) = _hgrn_fwd(
        p_a, lbl, (_gather_comm_relayed([s_gate, s_bh, s_ba]),
                   [(0, True), (HG_HEADS - 2, True), (HG_HEADS - 1, False)]))
    (p_b, p_c), (g_out,) = _proj_bc(
        h_all, w_b, w_c, (_gather_comm_relayed([s_out]), [(0, True), (N_TILES - 4, True), (N_TILES - 1, False)]))
    cos, sin = _rope_tables()
    qnw_t, knw_t = jnp.tile(q_norm_w, (1, ATT_HEADS)), jnp.tile(k_norm_w, (1, KV_HEADS))
    y_hg, qn, k_rep, v_rep = _prep_fwd(p_b, o, cos, sin, hgrn_norm_w, qnw_t, knw_t)
    (y_at, lse), (g_up, g_down) = _attn_fwd(
        qn, k_rep, v_rep, attn_sinks,
        (_gather_comm_relayed([s_up, s_down]), [(0, True), (N_BLOCKS - 6, True), (N_BLOCKS - 1, False)]))
    w_bh, w_ba, w_o = g_bh.reshape(D, HGW), g_ba.reshape(D, ATW), g_out.reshape(D, D)
    (mixed, r, x1, h2), _ = _merge_fwd(
        y_hg, y_at, p_c, x2, w_bh, w_ba, w_o, gate1, norm_ffn_w, sh2, sc2)
    g_gate, g_up, g_down = [g.reshape(N_FF_TILES, FF_TILE, D) for g in (g_gate, g_up, g_down)]

    act, d_gate, d_up, d_f, dx1, acc_ffn = _ffn_fused(x1, h2, tgt, g_gate, g_up, g_down, gate2,
                                                      norm_ffn_w, sc2)
    by_chip = lambda t: t.reshape((N_CHIPS, 2) + t.shape[1:])
    ff_by_chip = lambda t: t.reshape(N_CHIPS, 2, FF_BLK, D)
    t_down, _ = _mm_tn_blocked(act, d_f, "grad_down")
    t_down = ff_by_chip(t_down)
    t_gate, (f_down,) = _mm_tn_blocked(d_gate, h2, "grad_gate", (_sibling_comm([t_down]), first_last(N_FF_TILES)))
    t_gate = ff_by_chip(t_gate)
    t_up, (f_gate,) = _mm_tn_blocked(d_up, h2, "grad_up", (_sibling_comm([t_gate]), first_last(N_FF_TILES)))
    t_up = ff_by_chip(t_up)

    (d_r, d_a, d_b, dp_c, dy_hg, dy_at, acc_mg), (f_up,) = _merge_bwd(
        dx1, r, y_hg, y_at, p_c, w_bh, w_ba, w_o, gate1, (_sibling_comm([t_up]), first_last(N_LAT_TILES)))
    c_down, c_gate, c_up = [_pair_sum(t, f, "pair_sum_" + nm) for t, f, nm in
                            ((t_down, f_down, "down"), (t_gate, f_gate, "gate"), (t_up, f_up, "up"))]
    t_out = _mm_tn(mixed, d_r, tk=512, nk=4, tm=512, tn=1024, out_dtype=BF16, name="grad_out")
    t_bh = _mm_tn(d_a, y_hg, tk=512, nk=4, tm=512, tn=512, out_dtype=BF16, name="grad_bh")
    t_ba = _mm_tn(d_b, y_at, tk=512, nk=4, tm=512, tn=512, out_dtype=BF16, name="grad_ba")
    t_bh, t_ba, t_out = [by_chip(t.reshape(N_DEV, D // N_DEV, t.shape[1])) for t in (t_bh, t_ba, t_out)]
    (dq, dk_rep, dv_rep, dsink), (r_up,) = _attn_bwd(
        qn, k_rep, v_rep, attn_sinks, y_at, lse, dy_at, (_chip_comm([c_up]), first_last(N_BLOCKS)))
    (dp_b, d_o, acc_prep), (f_bh, f_ba, f_out) = _prep_bwd(
        p_b, o, cos, sin, hgrn_norm_w, qnw_t, knw_t, dy_hg, dq, dk_rep, dv_rep,
        (_sibling_comm([t_bh, t_ba, t_out]), first_last(N_TILES)))
    c_bh, c_ba, c_out = [_pair_sum(t, f, "pair_sum_" + nm) for t, f, nm in
                         ((t_bh, f_bh, "bh"), (t_ba, f_ba, "ba"), (t_out, f_out, "out"))]
    (dp_a, dlb), (r_bh, r_ba, r_out, r_down, r_gate) = _hgrn_bwd(
        p_a, lbl, d_o, st, (_chip_comm([c_bh, c_ba, c_out, c_down, c_gate]), first_last(HG_HEADS)))
    t_a = _mm_tn(dp_a, h_all, tk=768, nk=3, tm=1024, tn=1024, out_dtype=BF16, name="grad_in_a")
    t_b = _mm_tn(dp_b, h_all, tk=768, nk=3, tm=640, tn=1024, out_dtype=BF16, name="grad_in_b")
    t_c = _mm_tn(dp_c, h_all, tk=256, nk=8, b_off=1, tm=1024, tn=1024, out_dtype=BF16, name="grad_in_c")
    t_in = None
    for piece, nm in ((t_a, "a"), (t_b, "b"), (t_c, "c")):
        t_in = _place_row_blocks(piece, *_block_table(nm), t_in, IN_COLS, "order_terms_" + nm)
    t_in = by_chip(t_in.reshape(N_DEV, IN_BLK, D))
    (f_in,) = _run_comm(_sibling_comm([t_in]), "scatter_in_sibling")
    c_in = _pair_sum(t_in, f_in, "pair_sum_in")
    sems, c_in, land, token = _chip_exchange_start(c_in, jnp.zeros(c_in.shape, c_in.dtype))
    (grad_x, acc_in), _ = _input_bwd(dp_a, dp_b, dp_c, w_a, w_b, w_c, ctx2, x2, dx1,
                                     norm_mix_w + token[0, 0], sh_in, sc_in)
    small = _pack_small(acc_in, acc_mg, acc_ffn, acc_prep, dsink, dlb)
    return grad_x, small, [r_bh, r_ba, r_out, r_gate, r_up, r_down], (sems, c_in, land)


def kernel(x, c, ctx, c_ctx, w_ada, b_ada, norm_mix_w, norm_ffn_w, w_in, hgrn_lb_logits, hgrn_norm_w, q_norm_w, k_norm_w, attn_sinks, w_branch_hgrn, w_branch_attn, w_out, w_ffn_gate, w_ffn_up, w_ffn_down, loss_target, m_c_ctx, m_w_ada, m_b_ada, m_norm_mix_w, m_norm_ffn_w, m_w_in, m_hgrn_lb_logits, m_hgrn_norm_w, m_q_norm_w, m_k_norm_w, m_attn_sinks, m_w_branch_hgrn, m_w_branch_attn, m_w_out, m_w_ffn_gate, m_w_ffn_up, m_w_ffn_down, v_c_ctx, v_w_ada, v_b_ada, v_norm_mix_w, v_norm_ffn_w, v_w_in, v_hgrn_lb_logits, v_hgrn_norm_w, v_q_norm_w, v_k_norm_w, v_attn_sinks, v_w_branch_hgrn, v_w_branch_attn, v_w_out, v_w_ffn_gate, v_w_ffn_up, v_w_ffn_down):
    me = 4 * lax.axis_index("x") + 2 * lax.axis_index("y") + lax.axis_index("c")
    x2, ctx2, tgt = x[0], ctx[0], loss_target[0]
    w_ada2, w_in2 = w_ada[0], w_in[0]

    cond = jnp.zeros((8, D), F32).at[0].set(c[0]).at[1, :256].set(hgrn_lb_logits.reshape(256))
    b_cols = lax.dynamic_slice(b_ada, (0, me * ADA_BLK), (1, ADA_BLK))
    g0, cc, mod, g_in, h_all = _prologue(cond, c_ctx.reshape(1, D), w_ada2, b_cols, w_in2.T.astype(BF16),
                                         x2, ctx2, norm_mix_w)
    lbl = jnp.transpose(g0[:, 1, :256].reshape(N_DEV, 2, 2, 64), (1, 2, 0, 3)).reshape(2, 2, HGW)
    sh1, sc1, gate1, sh2, sc2, gate2 = [mod[k:k + 1] for k in range(6)]
    sh_in = jnp.concatenate([mod[6:7], sh1], axis=0)
    sc_in = jnp.concatenate([mod[7:8], sc1], axis=0)

    shards = [w_branch_hgrn[0].T, w_branch_attn[0].T, w_out[0], w_ffn_gate[0].T, w_ffn_up[0].T, w_ffn_down[0]]
    w_in_t = g_in.reshape(IN_COLS, D)
    w_a, w_b, w_c = [_pick_row_blocks(w_in_t, *_block_table(nm), "order_w_" + nm) for nm in "abc"]

    grad_x, small, (r_bh, r_ba, r_out, r_gate, r_up, r_down), pending_in = _local_step(
        x2, ctx2, h_all, tgt, lbl, sh_in, sc_in, gate1, sh2, sc2, gate2, norm_mix_w, norm_ffn_w, hgrn_norm_w,
        q_norm_w, k_norm_w, attn_sinks, w_a, w_b, w_c, *[s.astype(BF16) for s in shards])

    big = {}
    for nm, rr, ww, mm, vv, tr, transposed in (
            ("w_branch_hgrn", r_bh, w_branch_hgrn[0], m_w_branch_hgrn[0], v_w_branch_hgrn[0], 128, True),
            ("w_branch_attn", r_ba, w_branch_attn[0], m_w_branch_attn[0], v_w_branch_attn[0], 128, True),
            ("w_out", r_out, w_out[0], m_w_out[0], v_w_out[0], 128, False),
            ("w_ffn_gate", r_gate, w_ffn_gate[0], m_w_ffn_gate[0], v_w_ffn_gate[0], 352, True),
            ("w_ffn_up", r_up, w_ffn_up[0], m_w_ffn_up[0], v_w_ffn_up[0], 352, True),
            ("w_ffn_down", r_down, w_ffn_down[0], m_w_ffn_down[0], v_w_ffn_down[0], 352, False)):
        if transposed:
            res = _adamw_sharded(rr, ww.T, mm.T, vv.T, "adamw_" + nm, tr)
            big[nm] = [t.T[None] for t in res]
        else:
            big[nm] = [t[None] for t in _adamw_sharded(rr, ww, mm, vv, "adamw_" + nm, tr)]

    (g2,) = _all_gather([small], "gather_small", True)
    tot = _sum_small(g2)
    dm = jnp.zeros((16, 6 * D), F32).at[:8].set(g2[:, R_DMOD:R_DMOD + 6, :].reshape(N_DEV, 6 * D))
    dm = dm.at[8, :2 * D].set(tot[R_DCTX:R_DCTX + 2].reshape(2 * D))
    dm_cols = lax.dynamic_slice(dm, (0, me * ADA_BLK), (16, ADA_BLK))
    g_w_ada, dsc_term = _ada_grads(cc, dm_cols, w_ada2)
    (g3,) = _all_gather([dsc_term], "gather_cctx", True)
    g_c_ctx = _c_ctx_grad(g3, c_ctx.reshape(1, D))
    g_lbl = _lb_grads(tot[R_DLB:R_DLB + 2, :HGW], lbl)
    g_lb_mine = lax.dynamic_slice(g_lbl, (0, 0, me * 64), (2, 2, 64))
    misc = tot[R_MISC]
    loss = misc[M_LOSS]

    rep_out = _adamw_replicated(
        tot, g_c_ctx,
        [b_ada, c_ctx.reshape(1, D), norm_mix_w, norm_ffn_w, hgrn_norm_w, q_norm_w, k_norm_w, attn_sinks],
        [m_b_ada, m_c_ctx.reshape(1, D), m_norm_mix_w, m_norm_ffn_w, m_hgrn_norm_w, m_q_norm_w, m_k_norm_w,
         m_attn_sinks],
        [v_b_ada, v_c_ctx.reshape(1, D), v_norm_mix_w, v_norm_ffn_w, v_hgrn_norm_w, v_q_norm_w, v_k_norm_w,
         v_attn_sinks])
    rep = []
    for kind in range(4):
        vals = dict(zip(_REP_NAMES, rep_out[kind * len(_REP_NAMES):(kind + 1) * len(_REP_NAMES)]))
        vals["c_ctx"] = vals["c_ctx"].reshape(D)
        rep.append(vals)

    sems, c_in, land = pending_in
    d_ada, nm_ada, nv_ada = _adamw_plain(g_w_ada, w_ada2, m_w_ada[0], v_w_ada[0], "adamw_w_ada")
    land = _chip_exchange_wait(sems, c_in, land, d_ada)
    own = lax.dynamic_index_in_dim(c_in, 2 * lax.axis_index("x") + lax.axis_index("y"), 0, keepdims=False)
    big["w_in"] = [t.T[None] for t in _adamw_sharded(land, w_in2.T, m_w_in[0].T, v_w_in[0].T, "adamw_w_in", 336,
                                                     extra=own)]
    ada = [t[None] for t in (g_w_ada, d_ada, nm_ada, nv_ada)]
    lb_w = hgrn_lb_logits.reshape(4, 64)
    d_lb, nm_lb, nv_lb = _adamw_plain(g_lb_mine.reshape(4, 64), lb_w, m_hgrn_lb_logits.reshape(4, 64),
                                      v_hgrn_lb_logits.reshape(4, 64), "adamw_lb")
    lbs = [t.reshape(2, 2, 64) for t in (g_lb_mine, d_lb, nm_lb, nv_lb)]

    names = ['c_ctx', 'w_ada', 'b_ada', 'norm_mix_w', 'norm_ffn_w', 'w_in', 'hgrn_lb_logits', 'hgrn_norm_w',
             'q_norm_w', 'k_norm_w', 'attn_sinks', 'w_branch_hgrn', 'w_branch_attn', 'w_out', 'w_ffn_gate',
             'w_ffn_up', 'w_ffn_down']
    outs = [loss, grad_x[None]]
    for kind in range(4):
        for nm in names:
            if nm == 'w_ada':
                outs.append(ada[kind])
            elif nm == 'hgrn_lb_logits':
                outs.append(lbs[kind])
            elif nm in big:
                outs.append(big[nm][kind])
            else:
                outs.append(rep[kind][nm])
    return tuple(outs)
```

```python
import functools
import math

import numpy as np
import jax
import jax.numpy as jnp
from jax import lax
from jax.experimental import pallas as pl
from jax.experimental.pallas import tpu as pltpu

F32 = jnp.float32
BF16 = jnp.bfloat16

N_DEV = 8
D = 1024
S = 2048
L = 256
T = L + S
TM = 256
N_TILES = T // TM
N_LAT_TILES = S // TM
HG_HEADS = 4
HG_DIM = 128
HGW = 512
CHUNK = 32
N_CHUNKS = T // CHUNK
N_CTX_CHUNKS = L // CHUNK
ATT_HEADS = 8
KV_HEADS = 2
HEAD_DIM = 64
ATW = 512
KVW = 128
BLOCK = 128
N_BLOCKS = S // BLOCK
GRID_W = 64
ROPE_THETA = 10000.0
D_FF = 2816
FF_BLK = D_FF // N_DEV
FF_TILE = 256
N_FF_TILES = D_FF // FF_TILE
IN_COLS = 5376
IN_BLK = IN_COLS // N_DEV
ADA_BLK = 6 * D // N_DEV
EPS = 1e-6
WA, WB, WC = 2048, 1280, 2048

ADAM_LR = 0.001
ADAM_B1 = 0.9
ADAM_B2 = 0.999
ADAM_EPS = 1e-08
ADAM_WD = 0.01
ADAM_STEP = 10

VMEM_LIMIT = 56 * 1024 * 1024
MESH = pl.DeviceIdType.MESH


def _cp(sem=None, vmem=VMEM_LIMIT):
    return pltpu.CompilerParams(dimension_semantics=sem, vmem_limit_bytes=vmem)


def _full(shape):
    n = len(shape)
    return pl.BlockSpec(shape, lambda *_: (0,) * n)


_VMEM_WHOLE = pl.BlockSpec(memory_space=pltpu.VMEM)
_ANY = pl.BlockSpec(memory_space=pl.ANY)


def _sigmoid(v):
    return 1.0 / (1.0 + jnp.exp(-v))


def _dot(a, b):
    return jnp.dot(a, b, preferred_element_type=F32)


def _dot_nt(a, b):
    return lax.dot_general(a, b, (((1,), (1,)), ((), ())), preferred_element_type=F32)


def _dot_tn(a, b):
    return lax.dot_general(a, b, (((0,), (0,)), ((), ())), preferred_element_type=F32)


def _split3(v):
    hi = v.astype(BF16)
    r = v - hi.astype(F32)
    mid = r.astype(BF16)
    lo = (r - mid.astype(F32)).astype(BF16)
    return hi, mid, lo


def _dot_exact_rhs01(v, m01):
    hi, mid, lo = _split3(v)
    return _dot(hi, m01) + _dot(mid, m01) + _dot(lo, m01)


def _split2(v):
    hi = v.astype(BF16)
    return hi, (v - hi.astype(F32)).astype(BF16)


def _dot_lhs01(m01, v):
    hi, lo = _split2(v)
    return _dot(m01, hi) + _dot(m01, lo)


def _dot_f32(a, b, dot=_dot):
    ah, am, al = _split3(a)
    bh, bm, bl = _split3(b)
    return (dot(ah, bh) + (dot(ah, bm) + dot(am, bh))
            + (dot(am, bm) + dot(ah, bl) + dot(al, bh)))


def _my_pos():
    return lax.axis_index("x"), lax.axis_index("y"), lax.axis_index("c")


class _Comm:
    def __init__(self, operands, out_shapes, sems, phases):
        self.operands, self.out_shapes, self.sems, self.phases = operands, out_shapes, sems, phases


def _gather_comm(blocks):
    n = len(blocks)

    def parts(ins, outs, sems):
        send_sems, recv_sems, local_sems = sems
        x, y, c = _my_pos()
        me, sibling = (x, y, c), (x, y, 1 - c)
        chips = [(1 - x, y), (x, 1 - y), (1 - x, 1 - y)]

        def slot(a, px, py, pc):
            return outs[a].at[4 * px + 2 * py + pc]

        def copy(a, k, block, to, src=None):
            return pltpu.make_async_remote_copy(
                src_ref=slot(a, *block) if src is None else src, dst_ref=slot(a, *block),
                send_sem=send_sems.at[a, k], recv_sem=recv_sems.at[a, k],
                device_id=to, device_id_type=MESH)

        mine = [pltpu.make_async_copy(ins[a], slot(a, *me), local_sems.at[a]) for a in range(n)]
        first = []
        for a in range(n):
            first.append(copy(a, 0, me, sibling, src=ins[a]))
            first += [copy(a, 1 + j, me, (*chip, c), src=ins[a]) for j, chip in enumerate(chips)]
        passed = [copy(a, 4 + j, (*chip, c), sibling) for j, chip in enumerate(chips) for a in range(n)]
        return c, me, sibling, chips, copy, mine, first, passed

    def start(ins, outs, sems):
        _, _, _, _, _, mine, first, _ = parts(ins, outs, sems)
        for cp in mine + first:
            cp.start()

    def forward(ins, outs, sems):
        c, me, _, chips, copy, _, _, passed = parts(ins, outs, sems)
        for j, chip in enumerate(chips):
            for a in range(n):
                copy(a, 1 + j, (*chip, c), me).wait_recv()
                passed[j * n + a].start()

    def finish(ins, outs, sems):
        c, me, sibling, chips, copy, mine, first, passed = parts(ins, outs, sems)
        for a in range(n):
            copy(a, 0, sibling, me).wait_recv()
            for j, chip in enumerate(chips):
                copy(a, 4 + j, (*chip, 1 - c), me).wait_recv()
        for cp in first + passed:
            cp.wait_send()
        for cp in mine:
            cp.wait()

    return _Comm(blocks, [jax.ShapeDtypeStruct((N_DEV,) + b.shape, b.dtype) for b in blocks],
                 [pltpu.SemaphoreType.DMA((n, 7)), pltpu.SemaphoreType.DMA((n, 7)), pltpu.SemaphoreType.DMA((n,))],
                 [start, forward, finish])


def _gather_comm_relayed(blocks):
    n = len(blocks)

    def parts(ins, outs, sems):
        send_sems, recv_sems, local_sems = sems
        x, y, c = _my_pos()
        me, sibling = (x, y, c), (x, y, 1 - c)
        x_nbr, y_nbr, diag = (1 - x, y, c), (x, 1 - y, c), (1 - x, 1 - y, c)

        def slot(a, dev, half=None):
            ref = outs[a].at[4 * dev[0] + 2 * dev[1] + dev[2]]
            if half is None:
                return ref
            rows = blocks[a].shape[0] // 2
            return ref.at[pl.ds(half * rows, rows)]

        def copy(a, k, block, to, half=None, src=None):
            return pltpu.make_async_remote_copy(
                src_ref=slot(a, block, half) if src is None else src, dst_ref=slot(a, block, half),
                send_sem=send_sems.at[a, k], recv_sem=recv_sems.at[a, k],
                device_id=to, device_id_type=MESH)

        mine = [pltpu.make_async_copy(ins[a], slot(a, me), local_sems.at[a]) for a in range(n)]
        return me, sibling, x_nbr, y_nbr, diag, copy, mine

    def start(ins, outs, sems):
        me, sibling, x_nbr, y_nbr, _, copy, mine = parts(ins, outs, sems)
        for cp in mine:
            cp.start()
        for a in range(n):
            for k, to in ((1, x_nbr), (2, y_nbr), (0, sibling)):
                copy(a, k, me, to, src=ins[a]).start()

    def forward(ins, outs, sems):
        me, sibling, x_nbr, y_nbr, _, copy, _ = parts(ins, outs, sems)
        for a in range(n):
            copy(a, 1, x_nbr, me).wait_recv()
            copy(a, 3, x_nbr, y_nbr, half=0).start()
            copy(a, 5, x_nbr, sibling).start()
        for a in range(n):
            copy(a, 2, y_nbr, me).wait_recv()
            copy(a, 4, y_nbr, x_nbr, half=1).start()
            copy(a, 6, y_nbr, sibling).start()

    def finish(ins, outs, sems):
        me, sibling, x_nbr, y_nbr, diag, copy, mine = parts(ins, outs, sems)
        sib = lambda dev: (dev[0], dev[1], sibling[2])
        for a in range(n):
            copy(a, 3, diag, me, half=0).wait_recv()
            copy(a, 4, diag, me, half=1).wait_recv()
            copy(a, 7, diag, sibling).start()
        for a in range(n):
            copy(a, 0, sibling, me).wait_recv()
            for k, dev in ((5, x_nbr), (6, y_nbr), (7, diag)):
                copy(a, k, sib(dev), me).wait_recv()
        for a in range(n):
            for k, block, to, half in ((0, me, sibling, None), (1, me, x_nbr, None), (2, me, y_nbr, None),
                                       (3, x_nbr, y_nbr, 0), (4, y_nbr, x_nbr, 1), (5, x_nbr, sibling, None),
                                       (6, y_nbr, sibling, None), (7, diag, sibling, None)):
                copy(a, k, block, to, half=half, src=ins[a] if block is me else None).wait_send()
        for cp in mine:
            cp.wait()

    return _Comm(blocks, [jax.ShapeDtypeStruct((N_DEV,) + b.shape, b.dtype) for b in blocks],
                 [pltpu.SemaphoreType.DMA((n, 8)), pltpu.SemaphoreType.DMA((n, 8)), pltpu.SemaphoreType.DMA((n,))],
                 [start, forward, finish])


_HBM = pl.BlockSpec(memory_space=pltpu.HBM)
_SEM = pl.BlockSpec(memory_space=pltpu.SEMAPHORE)
_SPLIT_COPY = pltpu.CompilerParams(has_side_effects=pltpu.SideEffectType.DATAFLOW_SIDE_EFFECTING)


def _chip_exchange_copies(src_ref, land_ref, sems):
    x, y, c = _my_pos()
    q_me = 2 * x + y
    pairs = []
    for j, (px, py) in enumerate([(1 - x, y), (x, 1 - y), (1 - x, 1 - y)]):
        q = 2 * px + py
        send = pltpu.make_async_remote_copy(
            src_ref=src_ref.at[q], dst_ref=land_ref.at[q_me], send_sem=sems[j], recv_sem=sems[3 + j],
            device_id=(px, py, c), device_id_type=MESH)
        recv = pltpu.make_async_remote_copy(
            src_ref=src_ref.at[q], dst_ref=land_ref.at[q], send_sem=sems[j], recv_sem=sems[3 + j],
            device_id=(x, y, c), device_id_type=MESH)
        pairs.append((send, recv))
    return pairs


def _chip_exchange_start(src, land):
    def body(src_ref, land_ref, *outs):
        sems, token = outs[:6], outs[8]
        for send, _ in _chip_exchange_copies(src_ref, land_ref, sems):
            send.start()
        token[...] = jnp.zeros_like(token)

    res = pl.pallas_call(
        body, name="scatter_in_start",
        out_shape=(pltpu.SemaphoreType.DMA(()),) * 6 + (
            pltpu.HBM(src.shape, src.dtype), pltpu.HBM(land.shape, land.dtype),
            jax.ShapeDtypeStruct((8, 128), F32)),
        in_specs=(_HBM, _HBM), out_specs=(_SEM,) * 6 + (_HBM, _HBM, pl.BlockSpec(memory_space=pltpu.VMEM)),
        input_output_aliases={0: 6, 1: 7}, compiler_params=_SPLIT_COPY,
    )(pltpu.with_memory_space_constraint(src, pltpu.HBM), pltpu.with_memory_space_constraint(land, pltpu.HBM))
    return res[:6], res[6], res[7], res[8]


def _chip_exchange_wait(sems, src_thru, land_thru, after):
    def body(src_ref, land_ref, *rest):
        for send, recv in _chip_exchange_copies(src_ref, land_ref, rest[:6]):
            send.wait_send()
            recv.wait_recv()

    return pl.pallas_call(
        body, name="scatter_in_wait",
        out_shape=(pltpu.HBM(src_thru.shape, src_thru.dtype), pltpu.HBM(land_thru.shape, land_thru.dtype)),
        in_specs=(_HBM, _HBM) + (_SEM,) * 6 + (_ANY,), out_specs=(_HBM, _HBM),
        input_output_aliases={0: 0, 1: 1}, compiler_params=_SPLIT_COPY,
    )(src_thru, land_thru, *sems, after)[1]


def _run_comm(comm, name, in_vmem=False):
    n_in, n_out = len(comm.operands), len(comm.out_shapes)

    def body(*refs):
        ins, outs, sems = refs[:n_in], refs[n_in:n_in + n_out], refs[n_in + n_out:]
        for phase in comm.phases:
            phase(ins, outs, sems)

    spec = _VMEM_WHOLE if in_vmem else _ANY
    return pl.pallas_call(
        body, name=name, out_shape=comm.out_shapes, in_specs=[spec] * n_in, out_specs=[spec] * n_out,
        scratch_shapes=comm.sems,
    )(*comm.operands)


def _carrier_call(body, comm, schedule, *, name, grid, in_specs, out_specs, out_shape, scratch_shapes, operands):
    n_in, n_out, n_scr = len(in_specs), len(out_specs), len(scratch_shapes)
    c_in, c_out = len(comm.operands), len(comm.out_shapes)

    def full_body(*refs):
        ins, refs = refs[:n_in], refs[n_in:]
        cins, refs = refs[:c_in], refs[c_in:]
        outs, refs = refs[:n_out], refs[n_out:]
        couts, refs = refs[:c_out], refs[c_out:]
        scr, csems = refs[:n_scr], refs[n_scr:]
        step = pl.program_id(0)

        def run(before):
            for (at, when_before), phase in zip(schedule, comm.phases):
                if when_before == before:
                    pl.when(step == at)(functools.partial(phase, cins, couts, csems))

        run(True)
        body(*ins, *outs, *scr)
        run(False)

    res = pl.pallas_call(
        full_body, name=name, grid=grid,
        in_specs=list(in_specs) + [_ANY] * c_in, out_specs=list(out_specs) + [_ANY] * c_out,
        out_shape=list(out_shape) + list(comm.out_shapes),
        scratch_shapes=list(scratch_shapes) + list(comm.sems),
        compiler_params=_cp(("arbitrary",)),
    )(*operands, *comm.operands)
    return res[:n_out], res[n_out:]


def _pcall(body, carried, *, name, grid, in_specs, out_specs, out_shape, scratch_shapes, operands):
    if carried is None:
        res = pl.pallas_call(body, name=name, grid=grid, in_specs=in_specs, out_specs=out_specs,
                             out_shape=out_shape, scratch_shapes=scratch_shapes,
                             compiler_params=_cp(("arbitrary",)))(*operands)
        return res, ()
    return _carrier_call(body, carried[0], carried[1], name=name, grid=grid, in_specs=in_specs,
                         out_specs=out_specs, out_shape=out_shape, scratch_shapes=scratch_shapes,
                         operands=operands)


def _all_gather(blocks, name, in_vmem):
    return _run_comm(_gather_comm(blocks), name, in_vmem)


N_CHIPS = 4


def _sibling_comm(contribs):
    n = len(contribs)

    def copies(ins, outs, sems):
        send_sems, recv_sems = sems
        x, y, c = _my_pos()
        return [pltpu.make_async_remote_copy(
            src_ref=ins[a].at[pl.ds(0, N_CHIPS), 1 - c], dst_ref=outs[a],
            send_sem=send_sems.at[a], recv_sem=recv_sems.at[a],
            device_id=(x, y, 1 - c), device_id_type=MESH) for a in range(n)]

    def start(ins, outs, sems):
        for cp in copies(ins, outs, sems):
            cp.start()

    def finish(ins, outs, sems):
        cps = copies(ins, outs, sems)
        for cp in cps:
            cp.wait_recv()
        for cp in cps:
            cp.wait_send()

    return _Comm(contribs, [jax.ShapeDtypeStruct((N_CHIPS,) + b.shape[2:], b.dtype) for b in contribs],
                 [pltpu.SemaphoreType.DMA((n,)), pltpu.SemaphoreType.DMA((n,))], [start, finish])


def _pair_sum(mine, theirs, name):
    _, _, rows, cols = mine.shape
    core = lax.axis_index("c").astype(jnp.int32).reshape(1)

    def body(c_ref, m_ref, t_ref, o_ref):
        o_ref[...] = (m_ref[...].astype(F32) + t_ref[...].astype(F32)).astype(BF16)

    return pl.pallas_call(
        body, name=name,
        grid_spec=pltpu.PrefetchScalarGridSpec(
            num_scalar_prefetch=1, grid=(N_CHIPS,),
            in_specs=[pl.BlockSpec((None, None, rows, cols), lambda q, c: (q, c[0], 0, 0)),
                      pl.BlockSpec((None, rows, cols), lambda q, c: (q, 0, 0))],
            out_specs=pl.BlockSpec((None, rows, cols), lambda q, c: (q, 0, 0))),
        out_shape=jax.ShapeDtypeStruct((N_CHIPS, rows, cols), BF16),
        compiler_params=_cp(("parallel",)),
    )(core, mine, theirs)


def _chip_comm(sums):
    n = len(sums)

    def parts(ins, outs, sems):
        send_sems, recv_sems, local_sems = sems
        x, y, c = _my_pos()
        q_me = 2 * x + y
        chips = [(1 - x, y), (x, 1 - y), (1 - x, 1 - y)]
        mine = [pltpu.make_async_copy(ins[a].at[q_me], outs[a].at[q_me], local_sems.at[a]) for a in range(n)]
        sends, recvs = [], []
        for j, (px, py) in enumerate(chips):
            for a in range(n):
                q = 2 * px + py
                sends.append(pltpu.make_async_remote_copy(
                    src_ref=ins[a].at[q], dst_ref=outs[a].at[q_me],
                    send_sem=send_sems.at[a, j], recv_sem=recv_sems.at[a, j],
                    device_id=(px, py, c), device_id_type=MESH))
                recvs.append(pltpu.make_async_remote_copy(
                    src_ref=ins[a].at[q], dst_ref=outs[a].at[q],
                    send_sem=send_sems.at[a, j], recv_sem=recv_sems.at[a, j],
                    device_id=(x, y, c), device_id_type=MESH))
        return mine, sends, recvs

    def start(ins, outs, sems):
        mine, sends, _ = parts(ins, outs, sems)
        for cp in mine + sends:
            cp.start()

    def finish(ins, outs, sems):
        mine, sends, recvs = parts(ins, outs, sems)
        for cp in recvs:
            cp.wait_recv()
        for cp in sends:
            cp.wait_send()
        for cp in mine:
            cp.wait()

    return _Comm(sums, [jax.ShapeDtypeStruct(b.shape, b.dtype) for b in sums],
                 [pltpu.SemaphoreType.DMA((n, 3)), pltpu.SemaphoreType.DMA((n, 3)), pltpu.SemaphoreType.DMA((n,))],
                 [start, finish])


def _mm_nt(a, bt, *, tm, tn, out_dtype, name, row_off=0, rows=None):
    rows = a.shape[0] if rows is None else rows
    n, k = bt.shape

    def body(a_ref, b_ref, o_ref):
        o_ref[...] = _dot_nt(a_ref[...], b_ref[...]).astype(out_dtype)

    return pl.pallas_call(
        body, name=name, grid=(rows // tm, n // tn),
        in_specs=[pl.BlockSpec((tm, k), lambda i, j: (i + row_off, 0)),
                  pl.BlockSpec((tn, k), lambda i, j: (j, 0))],
        out_specs=pl.BlockSpec((tm, tn), lambda i, j: (i, j)),
        out_shape=jax.ShapeDtypeStruct((rows, n), out_dtype),
        compiler_params=_cp(("parallel", "parallel")),
    )(a, bt)


def _mm_tn(a, b, *, tk, nk, tm, tn, out_dtype, name, a_off=0, b_off=0):
    m, n = a.shape[1], b.shape[1]

    def body(a_ref, b_ref, o_ref, acc):
        kk = pl.program_id(2)

        @pl.when(kk == 0)
        def _():
            acc[...] = jnp.zeros_like(acc)

        acc[...] += _dot_tn(a_ref[...], b_ref[...])

        @pl.when(kk == nk - 1)
        def _():
            o_ref[...] = acc[...].astype(out_dtype)

    return pl.pallas_call(
        body, name=name, grid=(m // tm, n // tn, nk),
        in_specs=[pl.BlockSpec((tk, tm), lambda i, j, kk: (kk + a_off, i)),
                  pl.BlockSpec((tk, tn), lambda i, j, kk: (kk + b_off, j))],
        out_specs=pl.BlockSpec((tm, tn), lambda i, j, kk: (i, j)),
        out_shape=jax.ShapeDtypeStruct((m, n), out_dtype),
        scratch_shapes=[pltpu.VMEM((tm, tn), F32)],
        compiler_params=_cp(("parallel", "parallel", "arbitrary")),
    )(a, b)


def _mm_tn_blocked(a, b, name, carried=None):
    nb, _, w = a.shape
    n = b.shape[1]

    def body(a_ref, b_ref, o_ref):
        o_ref[...] = _dot_tn(a_ref[...], b_ref[...]).astype(BF16)

    (out,), extra = _pcall(
        body, carried, name=name, grid=(nb,),
        in_specs=[pl.BlockSpec((None, S, w), lambda j: (j, 0, 0)), _full((S, n))],
        out_specs=[pl.BlockSpec((None, w, n), lambda j: (j, 0, 0))],
        out_shape=[jax.ShapeDtypeStruct((nb, w, n), BF16)],
        scratch_shapes=[], operands=[a, b])
    return out, extra


def _prologue(cond, c_ctx, w_ada, b_cols, w_in_t, x, ctx, nw):
    rows_shape = jax.ShapeDtypeStruct((16, ADA_BLK), F32)
    big, g_cond, g_mod = _gather_comm_relayed([w_in_t]), _gather_comm([cond]), _gather_comm([rows_shape])

    def body(cond_ref, cctx_ref, wada_ref, b_ref, nw_ref, win_ref, x_ref, ctx_ref,
             g0_ref, cc_ref, mod_ref, gin_ref, h_ref, rows_ref, g1_ref, x_s, ctx_s, h_s, io_sems, *sems):
        s_big, s_cond, s_mod = sems[0:3], sems[3:6], sems[6:9]
        big.phases[0]([win_ref], [gin_ref], s_big)
        load_x = pltpu.make_async_copy(x_ref, x_s, io_sems.at[0])
        load_ctx = pltpu.make_async_copy(ctx_ref, ctx_s, io_sems.at[1])
        load_x.start()
        load_ctx.start()
        for phase in g_cond.phases:
            phase([cond_ref], [g0_ref], s_cond)
        cc_ref[...] = jnp.zeros_like(cc_ref)
        for j in range(N_DEV):
            cc_ref[j:j + 1, :] = g0_ref[j, 0:1, :]
        cc_ref[N_DEV:N_DEV + 1, :] = cctx_ref[...]
        cv = cc_ref[...]
        rows_ref[...] = _dot_f32(cv * _sigmoid(cv), wada_ref[...]) + b_ref[...]
        for phase in g_mod.phases:
            phase([rows_ref], [g1_ref], s_mod)
        x_pos, y_pos, c_pos = _my_pos()
        me = 4 * x_pos + 2 * y_pos + c_pos
        mine = jnp.concatenate([g1_ref[j, pl.ds(me, 1), :] for j in range(N_DEV)], axis=1)
        shared = jnp.concatenate([g1_ref[j, N_DEV:N_DEV + 1, :] for j in range(N_DEV)], axis=1)
        for k in range(6):
            mod_ref[k:k + 1, :] = mine[:, k * D:(k + 1) * D]
        mod_ref[6:7, :] = shared[:, 0:D]
        mod_ref[7:8, :] = shared[:, D:2 * D]
        load_ctx.wait()
        load_x.wait()
        h_s[pl.ds(0, L), :] = _rms_mod(ctx_s[...], nw_ref[...], mod_ref[6:7, :], mod_ref[7:8, :]).astype(BF16)

        def norm_tile(i, carry):
            r0 = pl.multiple_of(i * TM, TM)
            h_s[pl.ds(L + r0, TM), :] = _rms_mod(
                x_s[pl.ds(r0, TM), :], nw_ref[...], mod_ref[0:1, :], mod_ref[1:2, :]).astype(BF16)
            return carry

        lax.fori_loop(0, N_LAT_TILES, norm_tile, 0)
        store_h = pltpu.make_async_copy(h_s, h_ref, io_sems.at[2])
        store_h.start()
        big.phases[1]([win_ref], [gin_ref], s_big)
        big.phases[2]([win_ref], [gin_ref], s_big)
        store_h.wait()

    return pl.pallas_call(
        body, name="prologue",
        in_specs=[_VMEM_WHOLE] * 5 + [_ANY] * 3, out_specs=[_VMEM_WHOLE] * 3 + [_ANY] * 2,
        out_shape=[g_cond.out_shapes[0], jax.ShapeDtypeStruct((16, D), F32), jax.ShapeDtypeStruct((8, D), F32),
                   big.out_shapes[0], jax.ShapeDtypeStruct((T, D), BF16)],
        scratch_shapes=[pltpu.VMEM((16, ADA_BLK), F32), pltpu.VMEM((N_DEV, 16, ADA_BLK), F32),
                        pltpu.VMEM((S, D), F32), pltpu.VMEM((L, D), F32), pltpu.VMEM((T, D), BF16),
                        pltpu.SemaphoreType.DMA((3,))] + big.sems + g_cond.sems + g_mod.sems,
        compiler_params=_cp(),
    )(cond, c_ctx, w_ada, b_cols, nw, w_in_t, x, ctx)


def _ada_grads(cc, dm_cols, w_ada):
    def body(c_ref, dm_ref, w_ref, gw_ref, dsc_ref):
        cv = c_ref[...]
        sc = cv * _sigmoid(cv)
        dm = dm_ref[...]
        gw_ref[...] = _dot_f32(sc, dm, dot=_dot_tn)
        dsc_ref[...] = _dot_f32(dm, w_ref[...], dot=_dot_nt)

    return pl.pallas_call(
        body, name="ada_grads",
        in_specs=[_VMEM_WHOLE] * 3, out_specs=[_VMEM_WHOLE] * 2,
        out_shape=[jax.ShapeDtypeStruct((D, ADA_BLK), F32), jax.ShapeDtypeStruct((16, D), F32)],
        compiler_params=_cp(),
    )(cc, dm_cols, w_ada)


def _lat(i):
    return jnp.maximum(i - 1, 0)


def _rms_mod(xv, nw, sh, sc):
    rstd = lax.rsqrt(jnp.mean(xv * xv, axis=-1, keepdims=True) + EPS)
    return (xv * rstd * nw) * (1.0 + sc) + sh


def _rms_mod_bwd(xv, nw, sc, dh):
    rstd = lax.rsqrt(jnp.mean(xv * xv, axis=-1, keepdims=True) + EPS)
    xhat = xv * rstd
    dn = dh * (1.0 + sc)
    dxhat = dn * nw
    dx = rstd * (dxhat - xhat * jnp.mean(dxhat * xhat, axis=-1, keepdims=True))
    return (dx, jnp.sum(dh, axis=0, keepdims=True), jnp.sum(dh * (xhat * nw), axis=0, keepdims=True),
            jnp.sum(dn * xhat, axis=0, keepdims=True))


def _chunk_masks(reverse):
    row = lax.broadcasted_iota(jnp.int32, (TM, TM), 0)
    col = lax.broadcasted_iota(jnp.int32, (TM, TM), 1)
    same = (row // CHUNK) == (col // CHUNK)
    tri = same & ((col >= row) if reverse else (col <= row))
    return same, tri


def _chunk_order(i, reverse):
    if not reverse:
        return i
    return jnp.where(i < N_CTX_CHUNKS, N_CTX_CHUNKS - 1 - i, N_CHUNKS + N_CTX_CHUNKS - 1 - i)


def _decay_terms(z, lb, same01, tri01):
    f = lb + (1.0 - lb) * _sigmoid(z)
    g = jnp.log(f)
    g2 = jnp.concatenate(_split2(g), axis=1)
    b2 = _dot(tri01, g2)
    t2 = _dot(same01, g2)
    return f, 1.0 - f, b2[:, :HG_DIM] + b2[:, HG_DIM:], t2[:, :HG_DIM] + t2[:, HG_DIM:]


def _chunk_outer(a, b):
    n = TM // CHUNK
    return jnp.einsum('ncv,nck->nvk', a.reshape(n, CHUNK, HG_DIM), b.reshape(n, CHUNK, HG_DIM),
                      preferred_element_type=F32)


def _hgrn_fwd(p_a, lbl, carried=None):
    cpt = TM // CHUNK

    def body(p_ref, lbl_ref, o_ref, st_ref, qd_s, kd_s, u_s, v_s, ebt_s):
        masks = [_chunk_masks(d == 1) for d in (0, 1)]
        same01 = jnp.where(masks[0][0], 1.0, 0.0).astype(BF16)
        tri = [m[1] for m in masks]
        tri01 = [jnp.where(t, 1.0, 0.0).astype(BF16) for t in tri]
        lb = [_sigmoid(lbl_ref[d][0:1, :] - lbl_ref[d][1:2, :]) for d in (0, 1)]

        def prep(r, carry):
            r0 = pl.multiple_of(r * TM, TM)
            vb = p_ref[pl.ds(r0, TM), 2 * HG_DIM:3 * HG_DIM].astype(BF16)
            v_s[pl.ds(r0, TM), :] = vb
            for d in (0, 1):
                z = p_ref[pl.ds(r0, TM), d * HG_DIM:(d + 1) * HG_DIM]
                _, k, b, bt = _decay_terms(z, lb[d], same01, tri01[d])
                u_s[d, pl.ds(r * cpt, cpt)] = _chunk_outer(vb, (k * jnp.exp(bt - b)).astype(BF16))
                ebt_s[d, pl.ds(r0, TM), :] = jnp.exp(bt)

                @pl.when(r >= 1)
                def _():
                    rl = pl.multiple_of(r0 - L, TM)
                    qr = p_ref[pl.ds(r0, TM), 3 * HG_DIM:4 * HG_DIM]
                    q = qr * _sigmoid(qr) * HG_DIM ** -0.5
                    qd_s[d, pl.ds(rl, TM), :] = (q * jnp.exp(b)).astype(BF16)
                    kd_s[d, pl.ds(rl, TM), :] = (k * jnp.exp(-b)).astype(BF16)

            return carry

        lax.fori_loop(0, N_TILES, prep, 0)

        def scan(i, sts):
            new = []
            for d in (0, 1):
                nn = _chunk_order(i, d == 1)
                c0 = pl.multiple_of(nn * CHUNK, CHUNK)
                st_ref[d, nn] = sts[d].astype(BF16)
                new.append(sts[d] * ebt_s[d, pl.ds(c0, 1), :] + u_s[d, nn])
            return tuple(new)

        zero = jnp.zeros((HG_DIM, HG_DIM), F32)
        lax.fori_loop(0, N_CHUNKS, scan, (zero, zero))

        def outp(r, carry):
            r0 = pl.multiple_of(r * TM, TM)
            vb = v_s[pl.ds(r0 + L, TM), :]
            o = jnp.zeros((TM, HG_DIM), F32)
            for d in (0, 1):
                qd = qd_s[d, pl.ds(r0, TM), :]
                a = jnp.where(tri[d], _dot_nt(qd, kd_s[d, pl.ds(r0, TM), :]), 0.0)
                stb = st_ref[d, pl.ds(N_CTX_CHUNKS + r * cpt, cpt)]
                inter = jnp.einsum('nck,nvk->ncv', qd.reshape(cpt, CHUNK, HG_DIM), stb,
                                   preferred_element_type=F32)
                o = o + _dot(a.astype(BF16), vb) + inter.reshape(TM, HG_DIM)
            o_ref[pl.ds(r0, TM), :] = o
            return carry

        lax.fori_loop(0, N_LAT_TILES, outp, 0, unroll=2)

    return _pcall(
        body, carried, name="hgrn_fwd", grid=(HG_HEADS,),
        in_specs=[pl.BlockSpec((T, 4 * HG_DIM), lambda h: (0, h)),
                  pl.BlockSpec((2, 2, HG_DIM), lambda h: (0, 0, h))],
        out_specs=[pl.BlockSpec((S, HG_DIM), lambda h: (0, h)),
                   pl.BlockSpec((2, None, N_CHUNKS, HG_DIM, HG_DIM), lambda h: (0, h, 0, 0, 0))],
        out_shape=[jax.ShapeDtypeStruct((S, HGW), F32),
                   jax.ShapeDtypeStruct((2, HG_HEADS, N_CHUNKS, HG_DIM, HG_DIM), BF16)],
        scratch_shapes=[pltpu.VMEM((2, S, HG_DIM), BF16), pltpu.VMEM((2, S, HG_DIM), BF16),
                        pltpu.VMEM((2, N_CHUNKS, HG_DIM, HG_DIM), F32), pltpu.VMEM((T, HG_DIM), BF16),
                        pltpu.VMEM((2, T, HG_DIM), F32)],
        operands=[p_a, lbl])


def _hgrn_bwd(p_a, lbl, d_o, st, carried=None):
    cpt = TM // CHUNK

    def rows(r):
        return r * TM if isinstance(r, int) else pl.multiple_of(r * TM, TM)

    def body(p_ref, lbl_ref, do_ref, st_ref, dp_ref, dlb_ref, b_s, bt_s, dbt_s, qd_s, dst_s, w_s):
        masks = [_chunk_masks(d == 1) for d in (0, 1)]
        same01 = jnp.where(masks[0][0], 1.0, 0.0).astype(BF16)
        tri = [m[1] for m in masks]
        tri01 = [jnp.where(t, 1.0, 0.0).astype(BF16) for t in tri]
        later01 = [tri01[1], tri01[0]]
        lb = [_sigmoid(lbl_ref[d][0:1, :] - lbl_ref[d][1:2, :]) for d in (0, 1)]

        def prep_tile(r, latent):
            r0 = rows(r)
            for d in (0, 1):
                z = p_ref[pl.ds(r0, TM), d * HG_DIM:(d + 1) * HG_DIM]
                _, _, b, bt = _decay_terms(z, lb[d], same01, tri01[d])
                b_s[d, pl.ds(r0, TM), :] = b
                bt_s[d, pl.ds(r0, TM), :] = bt
                if latent:
                    rl = pl.multiple_of(r0 - L, TM)
                    qr = p_ref[pl.ds(r0, TM), 3 * HG_DIM:4 * HG_DIM]
                    qd = (qr * _sigmoid(qr) * HG_DIM ** -0.5 * jnp.exp(b)).astype(BF16)
                    qd_s[d, pl.ds(rl, TM), :] = qd
                    w_s[d, pl.ds(r * cpt, cpt)] = _chunk_outer(
                        do_ref[pl.ds(rl, TM), :].astype(BF16), qd).astype(BF16)

        prep_tile(0, False)
        w_s[:, pl.ds(0, N_CTX_CHUNKS)] = jnp.zeros((2, N_CTX_CHUNKS, HG_DIM, HG_DIM), BF16)

        def prep(r, carry):
            prep_tile(r, True)
            return carry

        lax.fori_loop(1, N_TILES, prep, 0, unroll=2)

        def rscan(j, dsts):
            i = N_CHUNKS - 1 - j
            new = []
            for d in (0, 1):
                nn = _chunk_order(i, d == 1)
                c0 = pl.multiple_of(nn * CHUNK, CHUNK)
                dst_s[d, nn] = dsts[d].astype(BF16)
                after = st_ref[d, _chunk_order(jnp.minimum(i + 1, N_CHUNKS - 1), d == 1)].astype(F32)
                dbt_s[d, pl.ds(c0, CHUNK), :] = jnp.broadcast_to(
                    jnp.sum(after * dsts[d], axis=0, keepdims=True), (CHUNK, HG_DIM))
                new.append(dsts[d] * jnp.exp(bt_s[d, pl.ds(c0, 1), :]) + w_s[d, nn].astype(F32))
            return tuple(new)

        zero = jnp.zeros((HG_DIM, HG_DIM), F32)
        lax.fori_loop(0, N_CHUNKS, rscan, (zero, zero))

        def grad_tile(r, latent):
            r0 = rows(r)
            vb = p_ref[pl.ds(r0, TM), 2 * HG_DIM:3 * HG_DIM].astype(BF16)
            dv = jnp.zeros((TM, HG_DIM), F32)
            dq = jnp.zeros((TM, HG_DIM), F32)
            dlbs = []
            if latent:
                rl = pl.multiple_of(r0 - L, TM)
                qr = p_ref[pl.ds(r0, TM), 3 * HG_DIM:4 * HG_DIM]
                sq = _sigmoid(qr)
                do = do_ref[pl.ds(rl, TM), :].astype(BF16)
                da_full = _dot_nt(do, vb)
            for d in (0, 1):
                z = p_ref[pl.ds(r0, TM), d * HG_DIM:(d + 1) * HG_DIM]
                sz = _sigmoid(z)
                f = lb[d] + (1.0 - lb[d]) * sz
                k = 1.0 - f
                b = b_s[d, pl.ds(r0, TM), :]
                e2 = jnp.exp(bt_s[d, pl.ds(r0, TM), :] - b)
                dstb = dst_s[d, pl.ds(r * cpt, cpt)]
                kd2 = k * e2
                dkd2 = jnp.einsum('ncv,nvk->nck', vb.reshape(cpt, CHUNK, HG_DIM), dstb,
                                  preferred_element_type=F32).reshape(TM, HG_DIM)
                dv = dv + jnp.einsum('nck,nvk->ncv', kd2.astype(BF16).reshape(cpt, CHUNK, HG_DIM), dstb,
                                     preferred_element_type=F32).reshape(TM, HG_DIM)
                dk = dkd2 * e2
                db = -(kd2 * dkd2)
                if latent:
                    eb = jnp.exp(b)
                    enb = jnp.exp(-b)
                    qdf = qr * sq * HG_DIM ** -0.5 * eb
                    kdf = k * enb
                    qd = qd_s[d, pl.ds(rl, TM), :]
                    kd = kdf.astype(BF16)
                    a = jnp.where(tri[d], _dot_nt(qd, kd), 0.0).astype(BF16)
                    da = jnp.where(tri[d], da_full, 0.0).astype(BF16)
                    stb = st_ref[d, pl.ds(r * cpt, cpt)]
                    dqd = _dot(da, kd) + jnp.einsum(
                        'ncv,nvk->nck', do.reshape(cpt, CHUNK, HG_DIM), stb,
                        preferred_element_type=F32).reshape(TM, HG_DIM)
                    dkd = _dot_tn(da, qd)
                    dv = dv + _dot_tn(a, do)
                    dk = dk + dkd * enb
                    db = db + qdf * dqd - kdf * dkd
                    dq = dq + dqd * eb
                dg = _dot_lhs01(later01[d], db) + dbt_s[d, pl.ds(r0, TM), :]
                df = dg / f - dk
                dp_ref[pl.ds(r0, TM), d * HG_DIM:(d + 1) * HG_DIM] = (
                    df * (1.0 - lb[d]) * sz * (1.0 - sz)).astype(BF16)
                dlbs.append(jnp.sum(df * (1.0 - sz), axis=0, keepdims=True))
            dp_ref[pl.ds(r0, TM), 2 * HG_DIM:3 * HG_DIM] = dv.astype(BF16)
            if latent:
                dq = dq * (HG_DIM ** -0.5) * (sq * (1.0 + qr * (1.0 - sq)))
            dp_ref[pl.ds(r0, TM), 3 * HG_DIM:4 * HG_DIM] = dq.astype(BF16)
            return dlbs

        dlb_ctx = grad_tile(0, False)

        def grads(r, acc):
            t = grad_tile(r, True)
            return (acc[0] + t[0], acc[1] + t[1])

        dlb = lax.fori_loop(1, N_TILES, grads, (dlb_ctx[0], dlb_ctx[1]))
        dlb_ref[0:1, :] = dlb[0]
        dlb_ref[1:2, :] = dlb[1]

    return _pcall(
        body, carried, name="hgrn_bwd", grid=(HG_HEADS,),
        in_specs=[pl.BlockSpec((T, 4 * HG_DIM), lambda h: (0, h)),
                  pl.BlockSpec((2, 2, HG_DIM), lambda h: (0, 0, h)),
                  pl.BlockSpec((S, HG_DIM), lambda h: (0, h)),
                  pl.BlockSpec((2, None, N_CHUNKS, HG_DIM, HG_DIM), lambda h: (0, h, 0, 0, 0))],
        out_specs=[pl.BlockSpec((T, 4 * HG_DIM), lambda h: (0, h)),
                   pl.BlockSpec((2, HG_DIM), lambda h: (0, h))],
        out_shape=[jax.ShapeDtypeStruct((T, WA), BF16), jax.ShapeDtypeStruct((2, HGW), F32)],
        scratch_shapes=[pltpu.VMEM((2, T, HG_DIM), F32), pltpu.VMEM((2, T, HG_DIM), F32),
                        pltpu.VMEM((2, T, HG_DIM), F32), pltpu.VMEM((2, S, HG_DIM), BF16),
                        pltpu.VMEM((2, N_CHUNKS, HG_DIM, HG_DIM), BF16),
                        pltpu.VMEM((2, N_CHUNKS, HG_DIM, HG_DIM), BF16)],
        operands=[p_a, lbl, d_o, st])


def _rope_tables():
    t = np.arange(S)
    inv = ROPE_THETA ** (-np.arange(0, 32, 2, dtype=np.float64) / 32)
    lane = np.arange(64)
    pos = np.where(lane[None, :] < 32, (t // GRID_W)[:, None], (t % GRID_W)[:, None]).astype(np.float64)
    ang = pos * inv[(lane % 32) % 16][None, :]
    sign = np.where((lane % 32) < 16, -1.0, 1.0)[None, :]
    cos = np.tile(np.cos(ang), (1, 2)).astype(np.float32)
    sin = np.tile(np.sin(ang) * sign, (1, 2)).astype(np.float32)
    return jnp.asarray(cos), jnp.asarray(sin)


def _rope_partner(v):
    lane = lax.broadcasted_iota(jnp.int32, (1, 128), 1)
    first = (lane % 32) < 16
    slabs = []
    for j in range(v.shape[1] // 128):
        s = v[:, 128 * j:128 * (j + 1)]
        slabs.append(jnp.where(first, pltpu.roll(s, 112, 1), pltpu.roll(s, 16, 1)))
    return slabs[0] if len(slabs) == 1 else jnp.concatenate(slabs, axis=1)


def _group_ones(width, group):
    r = lax.broadcasted_iota(jnp.int32, (width, width), 0)
    c = lax.broadcasted_iota(jnp.int32, (width, width), 1)
    return jnp.where((r // group) == (c // group), 1.0, 0.0).astype(BF16)


def _group_mean(v, ones01, group):
    hi = v.astype(BF16)
    lo = (v - hi.astype(F32)).astype(BF16)
    return (_dot(hi, ones01) + _dot(lo, ones01)) * (1.0 / group)


def _rep_matrix():
    r = lax.broadcasted_iota(jnp.int32, (KVW, ATW), 0)
    c = lax.broadcasted_iota(jnp.int32, (KVW, ATW), 1)
    return jnp.where(r == HEAD_DIM * (c // 256) + c % HEAD_DIM, 1.0, 0.0).astype(BF16)


def _tile_lanes(v, reps):
    return jnp.concatenate([v] * reps, axis=1)


def _prep_fwd(p_b, o, cos, sin, hnw, qnw, knw):
    def body(p_ref, o_ref, cos_ref, sin_ref, hnw_ref, qnw_ref, knw_ref, y_ref, q_ref, k_ref, v_ref):
        i = pl.program_id(0)
        rep = _rep_matrix()
        ones_k = _group_ones(KVW, HEAD_DIM)
        kr = p_ref[:, 1024:1152]
        krstd = lax.rsqrt(_group_mean(kr * kr, ones_k, HEAD_DIM) + EPS)
        kn = kr * krstd * knw_ref[...]
        v_ref[...] = _dot(p_ref[:, 1152:1280].astype(BF16), rep).astype(BF16)

        @pl.when(i == 0)
        def _():
            k_ref[...] = _dot(kn.astype(BF16), rep).astype(BF16)

        @pl.when(i > 0)
        def _():
            cs, sn = cos_ref[...], sin_ref[...]
            kro = kn * cs + _rope_partner(kn) * sn
            k_ref[...] = _dot(kro.astype(BF16), rep).astype(BF16)
            qr = p_ref[:, 512:1024]
            qrstd = lax.rsqrt(_group_mean(qr * qr, _group_ones(ATW, HEAD_DIM), HEAD_DIM) + EPS)
            qn = qr * qrstd * qnw_ref[...]
            qro = qn * _tile_lanes(cs, 4) + _rope_partner(qn) * _tile_lanes(sn, 4)
            q_ref[...] = (qro * HEAD_DIM ** -0.5).astype(BF16)
            ys = []
            for h in range(HG_HEADS):
                oh = o_ref[:, HG_DIM * h:HG_DIM * (h + 1)]
                gh = p_ref[:, HG_DIM * h:HG_DIM * (h + 1)]
                rstd = lax.rsqrt(jnp.mean(oh * oh, axis=-1, keepdims=True) + EPS)
                ys.append(oh * rstd * hnw_ref[...] * (gh * _sigmoid(gh)))
            y_ref[...] = jnp.concatenate(ys, axis=1).astype(BF16)

    return pl.pallas_call(
        body, name="prep_fwd", grid=(N_TILES,),
        in_specs=[pl.BlockSpec((TM, WB), lambda i: (i, 0)),
                  pl.BlockSpec((TM, HGW), lambda i: (_lat(i), 0)),
                  pl.BlockSpec((TM, 128), lambda i: (_lat(i), 0)),
                  pl.BlockSpec((TM, 128), lambda i: (_lat(i), 0)),
                  _full((1, HG_DIM)), _full((1, ATW)), _full((1, KVW))],
        out_specs=[pl.BlockSpec((TM, HGW), lambda i: (_lat(i), 0)),
                   pl.BlockSpec((TM, ATW), lambda i: (_lat(i), 0)),
                   pl.BlockSpec((TM, ATW), lambda i: (i, 0)),
                   pl.BlockSpec((TM, ATW), lambda i: (i, 0))],
        out_shape=[jax.ShapeDtypeStruct((S, HGW), BF16), jax.ShapeDtypeStruct((S, ATW), BF16),
                   jax.ShapeDtypeStruct((T, ATW), BF16), jax.ShapeDtypeStruct((T, ATW), BF16)],
        compiler_params=_cp(("arbitrary",)),
    )(p_b, o, cos, sin, hnw, qnw, knw)


def _prep_bwd(p_b, o, cos, sin, hnw, qnw, knw, dy_hg, dq, dk_rep, dv_rep, carried=None):
    def body(p_ref, o_ref, cos_ref, sin_ref, hnw_ref, qnw_ref, knw_ref, dy_ref, dq_ref, dk_ref, dv_ref,
             dp_ref, do_ref, acc_ref):
        i = pl.program_id(0)

        @pl.when(i == 0)
        def _():
            acc_ref[...] = jnp.zeros_like(acc_ref)

        rep = _rep_matrix()
        ones_k = _group_ones(KVW, HEAD_DIM)

        def fold(v):
            hi = v.astype(BF16)
            lo = (v - hi.astype(F32)).astype(BF16)
            return _dot_nt(hi, rep) + _dot_nt(lo, rep)

        kr = p_ref[:, 1024:1152]
        krstd = lax.rsqrt(_group_mean(kr * kr, ones_k, HEAD_DIM) + EPS)
        khat = kr * krstd
        kw = knw_ref[...]
        dkro = fold(dk_ref[...])
        dv = fold(dv_ref[...])

        def k_back(dkn):
            dkhat = dkn * kw
            dkr = krstd * (dkhat - khat * _group_mean(dkhat * khat, ones_k, HEAD_DIM))
            acc_ref[2:3, 0:KVW] += jnp.sum(dkn * khat, axis=0, keepdims=True)
            dp_ref[:, 1024:1152] = dkr.astype(BF16)
            dp_ref[:, 1152:1280] = dv.astype(BF16)

        @pl.when(i == 0)
        def _():
            k_back(dkro)
            dp_ref[:, 0:1024] = jnp.zeros((TM, 1024), BF16)

        @pl.when(i > 0)
        def _():
            cs, sn = cos_ref[...], sin_ref[...]
            k_back(dkro * cs + _rope_partner(dkro * sn))
            ones_q = _group_ones(ATW, HEAD_DIM)
            qr = p_ref[:, 512:1024]
            qrstd = lax.rsqrt(_group_mean(qr * qr, ones_q, HEAD_DIM) + EPS)
            qhat = qr * qrstd
            dqro = dq_ref[...] * HEAD_DIM ** -0.5
            dqn = dqro * _tile_lanes(cs, 4) + _rope_partner(dqro * _tile_lanes(sn, 4))
            dqhat = dqn * qnw_ref[...]
            dqr = qrstd * (dqhat - qhat * _group_mean(dqhat * qhat, ones_q, HEAD_DIM))
            acc_ref[1:2, :] += jnp.sum(dqn * qhat, axis=0, keepdims=True)
            dp_ref[:, 512:1024] = dqr.astype(BF16)
            dws = jnp.zeros((1, HG_DIM), F32)
            for h in range(HG_HEADS):
                sl = slice(HG_DIM * h, HG_DIM * (h + 1))
                oh, gh, dy = o_ref[:, sl], p_ref[:, sl], dy_ref[:, sl]
                rstd = lax.rsqrt(jnp.mean(oh * oh, axis=-1, keepdims=True) + EPS)
                ohat = oh * rstd
                sg = _sigmoid(gh)
                dp_ref[:, sl] = (dy * (ohat * hnw_ref[...]) * (sg * (1.0 + gh * (1.0 - sg)))).astype(BF16)
                dn = dy * (gh * sg)
                dws = dws + jnp.sum(dn * ohat, axis=0, keepdims=True)
                dohat = dn * hnw_ref[...]
                do_ref[:, sl] = rstd * (dohat - ohat * jnp.mean(dohat * ohat, axis=-1, keepdims=True))
            acc_ref[0:1, 0:HG_DIM] += dws

    return _pcall(
        body, carried, name="prep_bwd", grid=(N_TILES,),
        in_specs=[pl.BlockSpec((TM, WB), lambda i: (i, 0)),
                  pl.BlockSpec((TM, HGW), lambda i: (_lat(i), 0)),
                  pl.BlockSpec((TM, 128), lambda i: (_lat(i), 0)),
                  pl.BlockSpec((TM, 128), lambda i: (_lat(i), 0)),
                  _full((1, HG_DIM)), _full((1, ATW)), _full((1, KVW)),
                  pl.BlockSpec((TM, HGW), lambda i: (_lat(i), 0)),
                  pl.BlockSpec((TM, ATW), lambda i: (_lat(i), 0)),
                  pl.BlockSpec((TM, ATW), lambda i: (i, 0)),
                  pl.BlockSpec((TM, ATW), lambda i: (i, 0))],
        out_specs=[pl.BlockSpec((TM, WB), lambda i: (i, 0)),
                   pl.BlockSpec((TM, HGW), lambda i: (_lat(i), 0)),
                   _full((8, ATW))],
        out_shape=[jax.ShapeDtypeStruct((T, WB), BF16), jax.ShapeDtypeStruct((S, HGW), F32),
                   jax.ShapeDtypeStruct((8, ATW), F32)],
        scratch_shapes=[], operands=[p_b, o, cos, sin, hnw, qnw, knw, dy_hg, dq, dk_rep, dv_rep])


NEG = -1e30
_CTX_BLOCKS = L // BLOCK


def _attn_window_specs():
    prev = pl.BlockSpec((BLOCK, ATW), lambda i: (jnp.maximum(i - 1, 0) + _CTX_BLOCKS, 0))
    own = pl.BlockSpec((BLOCK, ATW), lambda i: (i + _CTX_BLOCKS, 0))
    nxt = pl.BlockSpec((BLOCK, ATW), lambda i: (jnp.minimum(i + 1, N_BLOCKS - 1) + _CTX_BLOCKS, 0))
    return [prev, own, nxt, _full((L, ATW))]


def _attn_valid(i, heads, context):
    n_keys = 3 * BLOCK + (L if context else 0)
    qi = lax.broadcasted_iota(jnp.int32, (heads * BLOCK, n_keys), 0) % BLOCK
    kj = lax.broadcasted_iota(jnp.int32, (heads * BLOCK, n_keys), 1)
    window = ((jnp.abs(kj - BLOCK - qi) <= BLOCK) & ((kj >= BLOCK) | (i > 0))
              & ((kj < 2 * BLOCK) | (i < N_BLOCKS - 1)))
    return window | (kj >= 3 * BLOCK)


def _stack_heads(qg):
    lane = lax.broadcasted_iota(jnp.int32, (1, 256), 1) // HEAD_DIM
    return jnp.concatenate([jnp.where(lane == g, qg, jnp.zeros_like(qg)) for g in range(4)], axis=0)


def _unstack_heads(v4):
    lane = lax.broadcasted_iota(jnp.int32, (1, 256), 1) // HEAD_DIM
    out = jnp.where(lane == 0, v4[0:BLOCK], 0.0)
    for g in range(1, 4):
        out = out + jnp.where(lane == g, v4[g * BLOCK:(g + 1) * BLOCK], 0.0)
    return out


def _sink_rows(sink_ref, hk):
    return jnp.concatenate(
        [jnp.broadcast_to(sink_ref[0:1, 4 * hk + g:4 * hk + g + 1], (BLOCK, 1)) for g in range(4)], axis=0)


def _attn_fwd(q, k_rep, v_rep, sinks, carried=None):
    def body(q_ref, kp, ko, kn, kc, vp, vo, vn, vc, sink_ref, y_ref, lse_ref):
        i = pl.program_id(0)
        valid = _attn_valid(i, 1, True)
        lane8 = lax.broadcasted_iota(jnp.int32, (1, ATT_HEADS), 1)
        head_of_lane = lax.broadcasted_iota(jnp.int32, (1, 256), 1) // HEAD_DIM
        lse_out = jnp.zeros((BLOCK, ATT_HEADS), F32)
        for hk in range(KV_HEADS):
            sl = slice(256 * hk, 256 * (hk + 1))
            qg = q_ref[:, sl]
            keys = jnp.concatenate([kp[:, sl], ko[:, sl], kn[:, sl], kc[:, sl]], axis=0)
            vals = jnp.concatenate([vp[:, sl], vo[:, sl], vn[:, sl], vc[:, sl]], axis=0)
            yg = jnp.zeros((BLOCK, 256), F32)
            for g in range(4):
                q1 = jnp.where(head_of_lane == g, qg, jnp.zeros_like(qg))
                s = jnp.where(valid, _dot_nt(q1, keys), NEG)
                sink = sink_ref[0:1, 4 * hk + g:4 * hk + g + 1]
                m = jnp.maximum(jnp.max(s, axis=1, keepdims=True), sink)
                p = jnp.exp(s - m)
                den = jnp.sum(p, axis=1, keepdims=True) + jnp.exp(sink - m)
                o1 = _dot(p.astype(BF16), vals) * (1.0 / den)
                yg = yg + jnp.where(head_of_lane == g, o1, 0.0)
                lse_out = lse_out + jnp.where(lane8 == 4 * hk + g, m + jnp.log(den), 0.0)
            y_ref[:, sl] = yg.astype(BF16)
        lse_ref[...] = lse_out

    return _pcall(
        body, carried, name="attn_fwd", grid=(N_BLOCKS,),
        in_specs=[pl.BlockSpec((BLOCK, ATW), lambda i: (i, 0))] + _attn_window_specs()
        + _attn_window_specs() + [_full((1, ATT_HEADS))],
        out_specs=[pl.BlockSpec((BLOCK, ATW), lambda i: (i, 0)),
                   pl.BlockSpec((BLOCK, ATT_HEADS), lambda i: (i, 0))],
        out_shape=[jax.ShapeDtypeStruct((S, ATW), BF16), jax.ShapeDtypeStruct((S, ATT_HEADS), F32)],
        scratch_shapes=[],
        operands=[q, k_rep, k_rep, k_rep, k_rep, v_rep, v_rep, v_rep, v_rep, sinks])


def _attn_bwd(q, k_rep, v_rep, sinks, y_at, lse, dy, carried=None):
    def body(q_ref, kp, ko, kn, kc, vp, vo, vn, vc, sink_ref, y_ref, lse_ref, dy_ref,
             dq_ref, dk_ref, dv_ref, dsink_ref, dk_acc, dv_acc):
        i = pl.program_id(0)

        @pl.when(i == 0)
        def _():
            dk_acc[...] = jnp.zeros_like(dk_acc)
            dv_acc[...] = jnp.zeros_like(dv_acc)
            dk_ref[pl.ds(0, L), :] = jnp.zeros((L, ATW), F32)
            dv_ref[pl.ds(0, L), :] = jnp.zeros((L, ATW), F32)
            dsink_ref[...] = jnp.zeros_like(dsink_ref)

        valid = _attn_valid(i, 4, False)
        lane8 = lax.broadcasted_iota(jnp.int32, (1, ATT_HEADS), 1)
        w0 = pl.multiple_of(i * BLOCK, BLOCK)
        dsink = jnp.zeros((1, ATT_HEADS), F32)
        for hk in range(KV_HEADS):
            sl = slice(256 * hk, 256 * (hk + 1))
            q4 = _stack_heads(q_ref[:, sl])
            do4f = _stack_heads(dy_ref[:, sl])
            o4 = _stack_heads(y_ref[:, sl]).astype(F32)
            do4 = do4f.astype(BF16)
            kl = jnp.concatenate([kp[:, sl], ko[:, sl], kn[:, sl]], axis=0)
            vl = jnp.concatenate([vp[:, sl], vo[:, sl], vn[:, sl]], axis=0)
            lse4 = jnp.concatenate(
                [jnp.sum(jnp.where(lane8 == 4 * hk + g, lse_ref[...], 0.0), axis=1, keepdims=True)
                 for g in range(4)], axis=0)
            p_loc = jnp.where(valid, jnp.exp(_dot_nt(q4, kl) - lse4), 0.0)
            p_ctx = jnp.exp(_dot_nt(q4, kc[:, sl]) - lse4)
            delta = jnp.sum(do4f * o4, axis=1, keepdims=True)
            ds_loc = (p_loc * (_dot_nt(do4, vl) - delta)).astype(BF16)
            ds_ctx = (p_ctx * (_dot_nt(do4, vc[:, sl]) - delta)).astype(BF16)
            dq_ref[:, sl] = _unstack_heads(_dot(ds_loc, kl) + _dot(ds_ctx, kc[:, sl]))
            dk_acc[pl.ds(w0, 3 * BLOCK), sl] += _dot_tn(ds_loc, q4)
            dv_acc[pl.ds(w0, 3 * BLOCK), sl] += _dot_tn(p_loc.astype(BF16), do4)
            dk_ref[pl.ds(0, L), sl] += _dot_tn(ds_ctx, q4)
            dv_ref[pl.ds(0, L), sl] += _dot_tn(p_ctx.astype(BF16), do4)
            p_sink = jnp.exp(_sink_rows(sink_ref, hk) - lse4)
            for g in range(4):
                rows = slice(g * BLOCK, (g + 1) * BLOCK)
                dsink = dsink + jnp.where(lane8 == 4 * hk + g,
                                          -jnp.sum(p_sink[rows] * delta[rows], axis=0, keepdims=True), 0.0)
        dsink_ref[...] += dsink

        @pl.when(i == N_BLOCKS - 1)
        def _():
            dk_ref[pl.ds(L, S), :] = dk_acc[pl.ds(BLOCK, S), :]
            dv_ref[pl.ds(L, S), :] = dv_acc[pl.ds(BLOCK, S), :]

    row_q = pl.BlockSpec((BLOCK, ATW), lambda i: (i, 0))
    return _pcall(
        body, carried, name="attn_bwd", grid=(N_BLOCKS,),
        in_specs=[row_q] + _attn_window_specs() + _attn_window_specs()
        + [_full((1, ATT_HEADS)), row_q, pl.BlockSpec((BLOCK, ATT_HEADS), lambda i: (i, 0)), row_q],
        out_specs=[row_q, _full((T, ATW)), _full((T, ATW)), _full((1, ATT_HEADS))],
        out_shape=[jax.ShapeDtypeStruct((S, ATW), F32), jax.ShapeDtypeStruct((T, ATW), F32),
                   jax.ShapeDtypeStruct((T, ATW), F32), jax.ShapeDtypeStruct((1, ATT_HEADS), F32)],
        scratch_shapes=[pltpu.VMEM((S + 2 * BLOCK, ATW), F32), pltpu.VMEM((S + 2 * BLOCK, ATW), F32)],
        operands=[q, k_rep, k_rep, k_rep, k_rep, v_rep, v_rep, v_rep, v_rep, sinks, y_at, lse, dy])


def _merge_fwd(y_hg, y_at, p_c, x, w_bh, w_ba, w_out, g1, nfw, sh2, sc2, carried=None):
    def body(yh_ref, ya_ref, g_ref, x_ref, wbh_ref, wba_ref, wo_ref, g1_ref, nfw_ref, sh_ref, sc_ref,
             mx_ref, r_ref, x1_ref, h2_ref):
        a = _dot_nt(yh_ref[...], wbh_ref[...])
        b = _dot_nt(ya_ref[...], wba_ref[...])
        mixed = (_sigmoid(g_ref[:, :D]) * a + _sigmoid(g_ref[:, D:]) * b).astype(BF16)
        r = _dot(mixed, wo_ref[...])
        x1 = x_ref[...] + g1_ref[...] * r
        mx_ref[...] = mixed
        r_ref[...] = r
        x1_ref[...] = x1
        h2_ref[...] = _rms_mod(x1, nfw_ref[...], sh_ref[...], sc_ref[...]).astype(BF16)

    row = lambda w: pl.BlockSpec((TM, w), lambda i: (i, 0))
    vec = _full((1, D))
    return _pcall(
        body, carried, name="merge_fwd", grid=(N_LAT_TILES,),
        in_specs=[row(HGW), row(ATW), row(WC), row(D), _VMEM_WHOLE, _VMEM_WHOLE, _VMEM_WHOLE,
                  vec, vec, vec, vec],
        out_specs=[row(D)] * 4,
        out_shape=[jax.ShapeDtypeStruct((S, D), dt) for dt in (BF16, F32, F32, BF16)],
        scratch_shapes=[], operands=[y_hg, y_at, p_c, x, w_bh, w_ba, w_out, g1, nfw, sh2, sc2])


def _merge_bwd(dx1, r, y_hg, y_at, p_c, w_bh, w_ba, w_out, g1, carried=None):
    def body(dx_ref, r_ref, yh_ref, ya_ref, g_ref, wbh_ref, wba_ref, wo_ref, g1_ref,
             dr_ref, da_ref, db_ref, dg_ref, dyh_ref, dya_ref, acc_ref):
        @pl.when(pl.program_id(0) == 0)
        def _():
            acc_ref[...] = jnp.zeros_like(acc_ref)

        dx1v = dx_ref[...]
        acc_ref[0:1, :] += jnp.sum(dx1v * r_ref[...], axis=0, keepdims=True)
        dr = (g1_ref[...] * dx1v).astype(BF16)
        dr_ref[...] = dr
        dmix = _dot_nt(dr, wo_ref[...])
        sh, sa = _sigmoid(g_ref[:, :D]), _sigmoid(g_ref[:, D:])
        da = (dmix * sh).astype(BF16)
        db = (dmix * sa).astype(BF16)
        da_ref[...] = da
        db_ref[...] = db
        dg_ref[:, :D] = (dmix * _dot_nt(yh_ref[...], wbh_ref[...]) * sh * (1.0 - sh)).astype(BF16)
        dg_ref[:, D:] = (dmix * _dot_nt(ya_ref[...], wba_ref[...]) * sa * (1.0 - sa)).astype(BF16)
        dyh_ref[...] = _dot(da, wbh_ref[...])
        dya_ref[...] = _dot(db, wba_ref[...])

    row = lambda w: pl.BlockSpec((TM, w), lambda i: (i, 0))
    return _pcall(
        body, carried, name="merge_bwd", grid=(N_LAT_TILES,),
        in_specs=[row(D), row(D), row(HGW), row(ATW), row(WC), _VMEM_WHOLE, _VMEM_WHOLE, _VMEM_WHOLE,
                  _full((1, D))],
        out_specs=[row(D), row(D), row(D), row(WC), row(HGW), row(ATW), _full((8, D))],
        out_shape=[jax.ShapeDtypeStruct((S, D), BF16), jax.ShapeDtypeStruct((S, D), BF16),
                   jax.ShapeDtypeStruct((S, D), BF16), jax.ShapeDtypeStruct((S, WC), BF16),
                   jax.ShapeDtypeStruct((S, HGW), F32), jax.ShapeDtypeStruct((S, ATW), F32),
                   jax.ShapeDtypeStruct((8, D), F32)],
        scratch_shapes=[], operands=[dx1, r, y_hg, y_at, p_c, w_bh, w_ba, w_out, g1])


def _ffn_fused(x1, h2, tgt, w_gate, w_up, w_down, g2, nfw, sc2):
    def body(x1_ref, h2_ref, t_ref, wg_ref, wu_ref, wd_ref, g2_ref, nfw_ref, sc_ref,
             act_ref, dgt_ref, dup_ref, df_ref, dx_ref, acc_ref, gs, us):
        @pl.when(pl.program_id(0) == 0)
        def _():
            acc_ref[...] = jnp.zeros_like(acc_ref)

        h2 = h2_ref[...]
        whole = lambda w_ref: w_ref[...].reshape(D_FF, D)
        wide = lambda t_ref: jnp.concatenate([t_ref[j] for j in range(N_FF_TILES)], axis=1)
        for j in range(N_FF_TILES):
            g = _dot_nt(h2, wg_ref[j])
            u = _dot_nt(h2, wu_ref[j])
            gs[j] = g
            us[j] = u
            act_ref[j] = (g * _sigmoid(g) * u).astype(BF16)
        f = _dot(wide(act_ref), whole(wd_ref))
        x1v = x1_ref[...]
        g2 = g2_ref[...]
        diff = x1v + g2 * f - t_ref[...]
        dy = diff * (1.0 / D)
        df = (g2 * dy).astype(BF16)
        df_ref[...] = df
        dact_all = _dot_nt(df, whole(wd_ref))
        for j in range(N_FF_TILES):
            g, u = gs[j], us[j]
            sg = _sigmoid(g)
            dact = dact_all[:, j * FF_TILE:(j + 1) * FF_TILE]
            dgt_ref[j] = (dact * u * (sg * (1.0 + g * (1.0 - sg)))).astype(BF16)
            dup_ref[j] = (dact * (g * sg)).astype(BF16)
        dh2 = _dot(wide(dgt_ref), whole(wg_ref)) + _dot(wide(dup_ref), whole(wu_ref))
        dx, dsh, dsc, dnw = _rms_mod_bwd(x1v, nfw_ref[...], sc_ref[...], dh2)
        dx_ref[...] = dy + dx
        acc_ref[0:1, :] += dsh
        acc_ref[1:2, :] += dsc
        acc_ref[2:3, :] += dnw
        acc_ref[3:4, :] += jnp.sum(dy * f, axis=0, keepdims=True)
        acc_ref[4:5, :] += 0.5 * jnp.sum(jnp.sum(diff * diff, axis=1, keepdims=True), axis=0,
                                         keepdims=True) * (1.0 / D)

    row = lambda dt_w: pl.BlockSpec((TM, dt_w), lambda i: (i, 0))
    blk = pl.BlockSpec((N_FF_TILES, TM, FF_TILE), lambda i: (0, i, 0))
    vec = _full((1, D))
    return pl.pallas_call(
        body, name="ffn_fused", grid=(N_LAT_TILES,),
        in_specs=[row(D), row(D), row(D), _VMEM_WHOLE, _VMEM_WHOLE, _VMEM_WHOLE, vec, vec, vec],
        out_specs=[blk, blk, blk, row(D), row(D), _full((8, D))],
        out_shape=[jax.ShapeDtypeStruct((N_FF_TILES, S, FF_TILE), BF16)] * 3
        + [jax.ShapeDtypeStruct((S, D), BF16), jax.ShapeDtypeStruct((S, D), F32),
           jax.ShapeDtypeStruct((8, D), F32)],
        scratch_shapes=[pltpu.VMEM((N_FF_TILES, TM, FF_TILE), F32), pltpu.VMEM((N_FF_TILES, TM, FF_TILE), F32)],
        compiler_params=_cp(("arbitrary",)),
    )(x1, h2, tgt, w_gate, w_up, w_down, g2, nfw, sc2)


def _proj_bc(h_all, w_b, w_c, carried=None):
    def body(h_ref, wb_ref, wc_ref, pb_ref, pc_ref):
        h = h_ref[...]
        pb_ref[...] = _dot_nt(h, wb_ref[...])

        @pl.when(pl.program_id(0) > 0)
        def _():
            pc_ref[...] = _dot_nt(h, wc_ref[...])

    return _pcall(
        body, carried, name="proj_bc", grid=(N_TILES,),
        in_specs=[pl.BlockSpec((TM, D), lambda i: (i, 0)), _VMEM_WHOLE, _VMEM_WHOLE],
        out_specs=[pl.BlockSpec((TM, WB), lambda i: (i, 0)), pl.BlockSpec((TM, WC), lambda i: (_lat(i), 0))],
        out_shape=[jax.ShapeDtypeStruct((T, WB), F32), jax.ShapeDtypeStruct((S, WC), F32)],
        scratch_shapes=[], operands=[h_all, w_b, w_c])


def _input_bwd(dp_a, dp_b, dp_c, w_a, w_b, w_c, ctx, x, dx1, nw, sh, sc, carried=None):
    def body(da_ref, db_ref, dc_ref, wa_ref, wb_ref, wc_ref, ctx_ref, x_ref, dx1_ref, nw_ref, sh_ref,
             sc_ref, gx_ref, acc_ref):
        i = pl.program_id(0)

        @pl.when(i == 0)
        def _():
            acc_ref[...] = jnp.zeros_like(acc_ref)

        dh = _dot(da_ref[...], wa_ref[...]) + _dot(db_ref[...], wb_ref[...])

        @pl.when(i == 0)
        def _():
            _, dsh, dsc, dnw = _rms_mod_bwd(ctx_ref[...], nw_ref[...], sc_ref[0:1, :], dh)
            acc_ref[3:4, :] += dsh
            acc_ref[4:5, :] += dsc
            acc_ref[2:3, :] += dnw

        @pl.when(i > 0)
        def _():
            dhl = dh + _dot(dc_ref[...], wc_ref[...])
            dx, dsh, dsc, dnw = _rms_mod_bwd(x_ref[...], nw_ref[...], sc_ref[1:2, :], dhl)
            gx_ref[...] = dx1_ref[...] + dx
            acc_ref[0:1, :] += dsh
            acc_ref[1:2, :] += dsc
            acc_ref[2:3, :] += dnw

    lat = lambda w: pl.BlockSpec((TM, w), lambda i: (_lat(i), 0))
    return _pcall(
        body, carried, name="input_bwd", grid=(N_TILES,),
        in_specs=[pl.BlockSpec((TM, WA), lambda i: (i, 0)), pl.BlockSpec((TM, WB), lambda i: (i, 0)),
                  lat(WC), _VMEM_WHOLE, _VMEM_WHOLE, _VMEM_WHOLE, _full((TM, D)), lat(D), lat(D),
                  _full((1, D)), _full((2, D)), _full((2, D))],
        out_specs=[lat(D), _full((8, D))],
        out_shape=[jax.ShapeDtypeStruct((S, D), F32), jax.ShapeDtypeStruct((8, D), F32)],
        scratch_shapes=[], operands=[dp_a, dp_b, dp_c, w_a, w_b, w_c, ctx, x, dx1, nw, sh, sc])


_C1 = 1.0 - ADAM_B1 ** ADAM_STEP
_C2 = 1.0 - ADAM_B2 ** ADAM_STEP


def _adamw_math(w, g, m, v):
    m = ADAM_B1 * m + (1.0 - ADAM_B1) * g
    v = ADAM_B2 * v + (1.0 - ADAM_B2) * (g * g)
    m_hat = m / _C1
    v_hat = v / _C2
    delta = -ADAM_LR * (m_hat / (jnp.sqrt(v_hat) + ADAM_EPS) + ADAM_WD * w)
    return delta, m, v


def _adamw_sharded(terms, w, m, v, name, tr, extra=None):
    rows, cols = w.shape

    def body(*refs):
        t_ref, w_ref, m_ref, v_ref = refs[:4]
        g_ref, d_ref, nm_ref, nv_ref = refs[-4:]
        g = t_ref[0].astype(F32)
        for s in range(1, N_CHIPS):
            g = g + t_ref[s].astype(F32)
        if extra is not None:
            g = g + refs[4][...].astype(F32)
        g_ref[...] = g
        d_ref[...], nm_ref[...], nv_ref[...] = _adamw_math(w_ref[...], g, m_ref[...], v_ref[...])

    blk = pl.BlockSpec((tr, cols), lambda i: (i, 0))
    return pl.pallas_call(
        body, name=name, grid=(rows // tr,),
        in_specs=[pl.BlockSpec((N_CHIPS, tr, cols), lambda i: (0, i, 0)), blk, blk, blk]
        + ([blk] if extra is not None else []),
        out_specs=[blk] * 4,
        out_shape=[jax.ShapeDtypeStruct((rows, cols), F32)] * 4,
        compiler_params=_cp(("parallel",)),
    )(terms, w, m, v, *([extra] if extra is not None else []))


def _adamw_plain(g, w, m, v, name):
    def body(g_ref, w_ref, m_ref, v_ref, d_ref, nm_ref, nv_ref):
        d_ref[...], nm_ref[...], nv_ref[...] = _adamw_math(w_ref[...], g_ref[...], m_ref[...], v_ref[...])

    return pl.pallas_call(
        body, name=name, in_specs=[_VMEM_WHOLE] * 4, out_specs=[_VMEM_WHOLE] * 3,
        out_shape=[jax.ShapeDtypeStruct(w.shape, F32)] * 3,
        compiler_params=_cp(),
    )(g, w, m, v)


SMALL_ROWS = 16
R_DMOD, R_DCTX, R_NMIX, R_NFFN, R_MISC, R_DLB, R_BADA01 = 0, 6, 8, 9, 10, 11, 13
M_HNW, M_QNW, M_KNW, M_SINK, M_LOSS = 0, 128, 256, 384, 512


def _pack_small(acc_in, acc_mg, acc_ffn, acc_prep, dsink, dlb):
    def body(in_ref, mg_ref, ff_ref, pp_ref, ds_ref, dlb_ref, o_ref):
        o_ref[...] = jnp.zeros_like(o_ref)
        o_ref[0:2, :] = in_ref[0:2, :]
        o_ref[2:3, :] = mg_ref[0:1, :]
        o_ref[3:5, :] = ff_ref[0:2, :]
        o_ref[5:6, :] = ff_ref[3:4, :]
        o_ref[6:8, :] = in_ref[3:5, :]
        o_ref[8:9, :] = in_ref[2:3, :]
        o_ref[9:10, :] = ff_ref[2:3, :]
        o_ref[10:11, M_HNW:M_HNW + HG_DIM] = pp_ref[0:1, 0:HG_DIM]
        r = lax.broadcasted_iota(jnp.int32, (ATW, 128), 0)
        c = lax.broadcasted_iota(jnp.int32, (ATW, 128), 1)
        fold = jnp.where((r % HEAD_DIM == c) & (c < HEAD_DIM), 1.0, 0.0).astype(BF16)
        qk = jnp.concatenate([pp_ref[1:2, :], pp_ref[2:3, :], jnp.zeros((6, ATW), F32)], axis=0)
        folded = _dot_exact_rhs01(qk, fold)
        o_ref[10:11, M_QNW:M_QNW + 128] = folded[0:1, :]
        o_ref[10:11, M_KNW:M_KNW + 128] = folded[1:2, :]
        o_ref[10:11, M_SINK:M_SINK + ATT_HEADS] = ds_ref[...]
        o_ref[10:11, M_LOSS:M_LOSS + 128] = ff_ref[4:5, 0:128]
        o_ref[11:13, 0:HGW] = dlb_ref[...]

    return pl.pallas_call(
        body, name="pack_small", in_specs=[_VMEM_WHOLE] * 6, out_specs=_VMEM_WHOLE,
        out_shape=jax.ShapeDtypeStruct((SMALL_ROWS, D), F32), compiler_params=_cp(),
    )(acc_in, acc_mg, acc_ffn, acc_prep, dsink, dlb)


def _sum_small(gathered):
    def body(g_ref, o_ref):
        tot = g_ref[0]
        for s in range(1, N_DEV):
            tot = tot + g_ref[s]
        o_ref[...] = tot
        o_ref[R_BADA01:R_BADA01 + 2, :] = tot[0:2, :] + tot[R_DCTX:R_DCTX + 2, :]

    return pl.pallas_call(
        body, name="sum_small", in_specs=[_VMEM_WHOLE], out_specs=_VMEM_WHOLE,
        out_shape=jax.ShapeDtypeStruct((SMALL_ROWS, D), F32), compiler_params=_cp(),
    )(gathered)


_REP_NAMES = ("b_ada", "c_ctx", "norm_mix_w", "norm_ffn_w", "hgrn_norm_w", "q_norm_w", "k_norm_w", "attn_sinks")


def _adamw_replicated(tot, g_c_ctx, ws, ms, vs):
    n = len(_REP_NAMES)

    def body(*refs):
        tot_ref, gc_ref = refs[0], refs[1]
        w_refs, m_refs, v_refs = refs[2:2 + n], refs[2 + n:2 + 2 * n], refs[2 + 2 * n:2 + 3 * n]
        outs = refs[2 + 3 * n:]
        row = lambda r: tot_ref[r:r + 1, :]
        misc = row(R_MISC)
        grads = [jnp.concatenate([row(R_BADA01), row(R_BADA01 + 1)] + [row(k) for k in range(2, 6)], axis=1),
                 gc_ref[...], row(R_NMIX), row(R_NFFN),
                 misc[:, M_HNW:M_HNW + HG_DIM], misc[:, M_QNW:M_QNW + HEAD_DIM],
                 misc[:, M_KNW:M_KNW + HEAD_DIM], misc[:, M_SINK:M_SINK + ATT_HEADS]]
        for k in range(n):
            outs[k][...] = grads[k]
            outs[n + k][...], outs[2 * n + k][...], outs[3 * n + k][...] = _adamw_math(
                w_refs[k][...], grads[k], m_refs[k][...], v_refs[k][...])

    shapes = [jax.ShapeDtypeStruct(w.shape, F32) for w in ws]
    return pl.pallas_call(
        body, name="adamw_replicated", in_specs=[_VMEM_WHOLE] * (2 + 3 * n), out_specs=[_VMEM_WHOLE] * (4 * n),
        out_shape=shapes * 4, compiler_params=_cp(),
    )(tot, g_c_ctx, *ws, *ms, *vs)


def _lb_grads(dlb, lbl):
    def body(d_ref, l_ref, o_ref):
        for d in (0, 1):
            ll = l_ref[d]
            lb = _sigmoid(ll[0:1, :] - ll[1:2, :])
            t = d_ref[d:d + 1, :] * lb * (1.0 - lb)
            o_ref[d, 0:1, :] = t
            o_ref[d, 1:2, :] = -t

    return pl.pallas_call(
        body, name="lb_grads", in_specs=[_VMEM_WHOLE] * 2, out_specs=_VMEM_WHOLE,
        out_shape=jax.ShapeDtypeStruct((2, 2, HGW), F32), compiler_params=_cp(),
    )(dlb, lbl)


def _c_ctx_grad(terms, c_ctx):
    def body(t_ref, c_ref, o_ref):
        tot = t_ref[0, 8:9, :]
        for s in range(1, N_DEV):
            tot = tot + t_ref[s, 8:9, :]
        cv = c_ref[...]
        sg = _sigmoid(cv)
        o_ref[...] = tot * (sg * (1.0 + cv * (1.0 - sg)))

    return pl.pallas_call(
        body, name="c_ctx_grad", in_specs=[_VMEM_WHOLE] * 2, out_specs=_VMEM_WHOLE,
        out_shape=jax.ShapeDtypeStruct((1, D), F32), compiler_params=_cp(),
    )(terms, c_ctx)


def _in_perm():
    fz, bz, inp, kk, vv, qhg, ghg, qat, gates = 0, 512, 1024, 1536, 1664, 1792, 2304, 2816, 3328
    cols = []
    for h in range(HG_HEADS):
        for base in (fz, bz, inp, qhg):
            cols += list(range(base + 128 * h, base + 128 * (h + 1)))
    cols += list(range(ghg, ghg + 512)) + list(range(qat, qat + 512))
    cols += list(range(kk, kk + 128)) + list(range(vv, vv + 128))
    cols += list(range(gates, gates + 2048))
    return np.asarray(cols, np.int32)


_PERM = _in_perm()


_PIECES = {"a": (0, WA, 128), "b": (WA, WB, 256), "c": (WA + WB, WC, 256)}


def _block_table(piece):
    lo, n, blk = _PIECES[piece]
    starts = [int(_PERM[r]) for r in range(lo, lo + n, blk)]
    assert all(s % blk == 0 and np.array_equal(_PERM[r:r + blk], np.arange(s, s + blk))
               for s, r in zip(starts, range(lo, lo + n, blk)))
    return jnp.asarray([s // blk for s in starts], jnp.int32), blk


def _pick_row_blocks(x, table, blk, name):
    cols = x.shape[1]

    def body(t_ref, x_ref, o_ref):
        o_ref[...] = x_ref[...]

    return pl.pallas_call(
        body, name=name,
        grid_spec=pltpu.PrefetchScalarGridSpec(
            num_scalar_prefetch=1, grid=(table.shape[0],),
            in_specs=[pl.BlockSpec((blk, cols), lambda i, t: (t[i], 0))],
            out_specs=pl.BlockSpec((blk, cols), lambda i, t: (i, 0))),
        out_shape=jax.ShapeDtypeStruct((table.shape[0] * blk, cols), x.dtype),
        compiler_params=_cp(("arbitrary",)),
    )(table, x)


def _place_row_blocks(x, table, blk, into, out_rows, name):
    cols = x.shape[1]

    def body(t_ref, x_ref, *rest):
        rest[-1][...] = x_ref[...]

    operands, in_specs, aliases = [table, x], [pl.BlockSpec((blk, cols), lambda i, t: (i, 0))], {}
    if into is not None:
        operands.append(into)
        in_specs.append(_ANY)
        aliases = {2: 0}
    return pl.pallas_call(
        body, name=name,
        grid_spec=pltpu.PrefetchScalarGridSpec(
            num_scalar_prefetch=1, grid=(table.shape[0],), in_specs=in_specs,
            out_specs=pl.BlockSpec((blk, cols), lambda i, t: (t[i], 0))),
        out_shape=jax.ShapeDtypeStruct((out_rows, cols), x.dtype),
        input_output_aliases=aliases,
        compiler_params=_cp(("arbitrary",)),
    )(*operands)


def _local_step(x2, ctx2, h_all, tgt, lbl, sh_in, sc_in, gate1, sh2, sc2, gate2, norm_mix_w, norm_ffn_w,
                hgrn_norm_w, q_norm_w, k_norm_w, attn_sinks, w_a, w_b, w_c, s_bh, s_ba, s_out,
                s_gate, s_up, s_down):
    first_last = lambda n: [(0, True), (n - 1, False)]
    p_a = _mm_nt(h_all, w_a, tm=768, tn=1024, out_dtype=F32, name="proj_a")
    (o, st), (g_gate, g_bh, g_ba) = _hgrn_fwd(
        p_a, lbl, (_gather_comm_relayed([s_gate, s_bh, s_ba]),
                   [(0, True), (HG_HEADS - 2, True), (HG_HEADS - 1, False)]))
    (p_b, p_c), (g_out,) = _proj_bc(
        h_all, w_b, w_c, (_gather_comm_relayed([s_out]), [(0, True), (N_TILES - 4, True), (N_TILES - 1, False)]))
    cos, sin = _rope_tables()
    qnw_t, knw_t = jnp.tile(q_norm_w, (1, ATT_HEADS)), jnp.tile(k_norm_w, (1, KV_HEADS))
    y_hg, qn, k_rep, v_rep = _prep_fwd(p_b, o, cos, sin, hgrn_norm_w, qnw_t, knw_t)
    (y_at, lse), (g_up, g_down) = _attn_fwd(
        qn, k_rep, v_rep, attn_sinks,
        (_gather_comm_relayed([s_up, s_down]), [(0, True), (N_BLOCKS - 6, True), (N_BLOCKS - 1, False)]))
    w_bh, w_ba, w_o = g_bh.reshape(D, HGW), g_ba.reshape(D, ATW), g_out.reshape(D, D)
    (mixed, r, x1, h2), _ = _merge_fwd(
        y_hg, y_at, p_c, x2, w_bh, w_ba, w_o, gate1, norm_ffn_w, sh2, sc2)
    g_gate, g_up, g_down = [g.reshape(N_FF_TILES, FF_TILE, D) for g in (g_gate, g_up, g_down)]

    act, d_gate, d_up, d_f, dx1, acc_ffn = _ffn_fused(x1, h2, tgt, g_gate, g_up, g_down, gate2,
                                                      norm_ffn_w, sc2)
    by_chip = lambda t: t.reshape((N_CHIPS, 2) + t.shape[1:])
    ff_by_chip = lambda t: t.reshape(N_CHIPS, 2, FF_BLK, D)
    t_down, _ = _mm_tn_blocked(act, d_f, "grad_down")
    t_down = ff_by_chip(t_down)
    t_gate, (f_down,) = _mm_tn_blocked(d_gate, h2, "grad_gate", (_sibling_comm([t_down]), first_last(N_FF_TILES)))
    t_gate = ff_by_chip(t_gate)
    t_up, (f_gate,) = _mm_tn_blocked(d_up, h2, "grad_up", (_sibling_comm([t_gate]), first_last(N_FF_TILES)))
    t_up = ff_by_chip(t_up)

    (d_r, d_a, d_b, dp_c, dy_hg, dy_at, acc_mg), (f_up,) = _merge_bwd(
        dx1, r, y_hg, y_at, p_c, w_bh, w_ba, w_o, gate1, (_sibling_comm([t_up]), first_last(N_LAT_TILES)))
    c_down, c_gate, c_up = [_pair_sum(t, f, "pair_sum_" + nm) for t, f, nm in
                            ((t_down, f_down, "down"), (t_gate, f_gate, "gate"), (t_up, f_up, "up"))]
    t_out = _mm_tn(mixed, d_r, tk=512, nk=4, tm=512, tn=1024, out_dtype=BF16, name="grad_out")
    t_bh = _mm_tn(d_a, y_hg, tk=512, nk=4, tm=512, tn=512, out_dtype=BF16, name="grad_bh")
    t_ba = _mm_tn(d_b, y_at, tk=512, nk=4, tm=512, tn=512, out_dtype=BF16, name="grad_ba")
    t_bh, t_ba, t_out = [by_chip(t.reshape(N_DEV, D // N_DEV, t.shape[1])) for t in (t_bh, t_ba, t_out)]
    (dq, dk_rep, dv_rep, dsink), (r_up,) = _attn_bwd(
        qn, k_rep, v_rep, attn_sinks, y_at, lse, dy_at, (_chip_comm([c_up]), first_last(N_BLOCKS)))
    (dp_b, d_o, acc_prep), (f_bh, f_ba, f_out) = _prep_bwd(
        p_b, o, cos, sin, hgrn_norm_w, qnw_t, knw_t, dy_hg, dq, dk_rep, dv_rep,
        (_sibling_comm([t_bh, t_ba, t_out]), first_last(N_TILES)))
    c_bh, c_ba, c_out = [_pair_sum(t, f, "pair_sum_" + nm) for t, f, nm in
                         ((t_bh, f_bh, "bh"), (t_ba, f_ba, "ba"), (t_out, f_out, "out"))]
    (dp_a, dlb), (r_bh, r_ba, r_out, r_down, r_gate) = _hgrn_bwd(
        p_a, lbl, d_o, st, (_chip_comm([c_bh, c_ba, c_out, c_down, c_gate]), first_last(HG_HEADS)))
    t_a = _mm_tn(dp_a, h_all, tk=768, nk=3, tm=1024, tn=1024, out_dtype=BF16, name="grad_in_a")
    t_b = _mm_tn(dp_b, h_all, tk=768, nk=3, tm=640, tn=1024, out_dtype=BF16, name="grad_in_b")
    t_c = _mm_tn(dp_c, h_all, tk=256, nk=8, b_off=1, tm=1024, tn=1024, out_dtype=BF16, name="grad_in_c")
    t_in = None
    for piece, nm in ((t_a, "a"), (t_b, "b"), (t_c, "c")):
        t_in = _place_row_blocks(piece, *_block_table(nm), t_in, IN_COLS, "order_terms_" + nm)
    t_in = by_chip(t_in.reshape(N_DEV, IN_BLK, D))
    (f_in,) = _run_comm(_sibling_comm([t_in]), "scatter_in_sibling")
    c_in = _pair_sum(t_in, f_in, "pair_sum_in")
    sems, c_in, land, token = _chip_exchange_start(c_in, jnp.zeros(c_in.shape, c_in.dtype))
    (grad_x, acc_in), _ = _input_bwd(dp_a, dp_b, dp_c, w_a, w_b, w_c, ctx2, x2, dx1,
                                     norm_mix_w + token[0, 0], sh_in, sc_in)
    small = _pack_small(acc_in, acc_mg, acc_ffn, acc_prep, dsink, dlb)
    return grad_x, small, [r_bh, r_ba, r_out, r_gate, r_up, r_down], (sems, c_in, land)


def kernel(x, c, ctx, c_ctx, w_ada, b_ada, norm_mix_w, norm_ffn_w, w_in, hgrn_lb_logits, hgrn_norm_w, q_norm_w, k_norm_w, attn_sinks, w_branch_hgrn, w_branch_attn, w_out, w_ffn_gate, w_ffn_up, w_ffn_down, loss_target, m_c_ctx, m_w_ada, m_b_ada, m_norm_mix_w, m_norm_ffn_w, m_w_in, m_hgrn_lb_logits, m_hgrn_norm_w, m_q_norm_w, m_k_norm_w, m_attn_sinks, m_w_branch_hgrn, m_w_branch_attn, m_w_out, m_w_ffn_gate, m_w_ffn_up, m_w_ffn_down, v_c_ctx, v_w_ada, v_b_ada, v_norm_mix_w, v_norm_ffn_w, v_w_in, v_hgrn_lb_logits, v_hgrn_norm_w, v_q_norm_w, v_k_norm_w, v_attn_sinks, v_w_branch_hgrn, v_w_branch_attn, v_w_out, v_w_ffn_gate, v_w_ffn_up, v_w_ffn_down):
    me = 4 * lax.axis_index("x") + 2 * lax.axis_index("y") + lax.axis_index("c")
    x2, ctx2, tgt = x[0], ctx[0], loss_target[0]
    w_ada2, w_in2 = w_ada[0], w_in[0]

    cond = jnp.zeros((8, D), F32).at[0].set(c[0]).at[1, :256].set(hgrn_lb_logits.reshape(256))
    b_cols = lax.dynamic_slice(b_ada, (0, me * ADA_BLK), (1, ADA_BLK))
    g0, cc, mod, g_in, h_all = _prologue(cond, c_ctx.reshape(1, D), w_ada2, b_cols, w_in2.T.astype(BF16),
                                         x2, ctx2, norm_mix_w)
    lbl = jnp.transpose(g0[:, 1, :256].reshape(N_DEV, 2, 2, 64), (1, 2, 0, 3)).reshape(2, 2, HGW)
    sh1, sc1, gate1, sh2, sc2, gate2 = [mod[k:k + 1] for k in range(6)]
    sh_in = jnp.concatenate([mod[6:7], sh1], axis=0)
    sc_in = jnp.concatenate([mod[7:8], sc1], axis=0)

    shards = [w_branch_hgrn[0].T, w_branch_attn[0].T, w_out[0], w_ffn_gate[0].T, w_ffn_up[0].T, w_ffn_down[0]]
    w_in_t = g_in.reshape(IN_COLS, D)
    w_a, w_b, w_c = [_pick_row_blocks(w_in_t, *_block_table(nm), "order_w_" + nm) for nm in "abc"]

    grad_x, small, (r_bh, r_ba, r_out, r_gate, r_up, r_down), pending_in = _local_step(
        x2, ctx2, h_all, tgt, lbl, sh_in, sc_in, gate1, sh2, sc2, gate2, norm_mix_w, norm_ffn_w, hgrn_norm_w,
        q_norm_w, k_norm_w, attn_sinks, w_a, w_b, w_c, *[s.astype(BF16) for s in shards])

    big = {}
    for nm, rr, ww, mm, vv, tr, transposed in (
            ("w_branch_hgrn", r_bh, w_branch_hgrn[0], m_w_branch_hgrn[0], v_w_branch_hgrn[0], 128, True),
            ("w_branch_attn", r_ba, w_branch_attn[0], m_w_branch_attn[0], v_w_branch_attn[0], 128, True),
            ("w_out", r_out, w_out[0], m_w_out[0], v_w_out[0], 128, False),
            ("w_ffn_gate", r_gate, w_ffn_gate[0], m_w_ffn_gate[0], v_w_ffn_gate[0], 352, True),
            ("w_ffn_up", r_up, w_ffn_up[0], m_w_ffn_up[0], v_w_ffn_up[0], 352, True),
            ("w_ffn_down", r_down, w_ffn_down[0], m_w_ffn_down[0], v_w_ffn_down[0], 352, False)):
        if transposed:
            res = _adamw_sharded(rr, ww.T, mm.T, vv.T, "adamw_" + nm, tr)
            big[nm] = [t.T[None] for t in res]
        else:
            big[nm] = [t[None] for t in _adamw_sharded(rr, ww, mm, vv, "adamw_" + nm, tr)]

    (g2,) = _all_gather([small], "gather_small", True)
    tot = _sum_small(g2)
    dm = jnp.zeros((16, 6 * D), F32).at[:8].set(g2[:, R_DMOD:R_DMOD + 6, :].reshape(N_DEV, 6 * D))
    dm = dm.at[8, :2 * D].set(tot[R_DCTX:R_DCTX + 2].reshape(2 * D))
    dm_cols = lax.dynamic_slice(dm, (0, me * ADA_BLK), (16, ADA_BLK))
    g_w_ada, dsc_term = _ada_grads(cc, dm_cols, w_ada2)
    (g3,) = _all_gather([dsc_term], "gather_cctx", True)
    g_c_ctx = _c_ctx_grad(g3, c_ctx.reshape(1, D))
    g_lbl = _lb_grads(tot[R_DLB:R_DLB + 2, :HGW], lbl)
    g_lb_mine = lax.dynamic_slice(g_lbl, (0, 0, me * 64), (2, 2, 64))
    misc = tot[R_MISC]
    loss = misc[M_LOSS]

    rep_out = _adamw_replicated(
        tot, g_c_ctx,
        [b_ada, c_ctx.reshape(1, D), norm_mix_w, norm_ffn_w, hgrn_norm_w, q_norm_w, k_norm_w, attn_sinks],
        [m_b_ada, m_c_ctx.reshape(1, D), m_norm_mix_w, m_norm_ffn_w, m_hgrn_norm_w, m_q_norm_w, m_k_norm_w,
         m_attn_sinks],
        [v_b_ada, v_c_ctx.reshape(1, D), v_norm_mix_w, v_norm_ffn_w, v_hgrn_norm_w, v_q_norm_w, v_k_norm_w,
         v_attn_sinks])
    rep = []
    for kind in range(4):
        vals = dict(zip(_REP_NAMES, rep_out[kind * len(_REP_NAMES):(kind + 1) * len(_REP_NAMES)]))
        vals["c_ctx"] = vals["c_ctx"].reshape(D)
        rep.append(vals)

    sems, c_in, land = pending_in
    d_ada, nm_ada, nv_ada = _adamw_plain(g_w_ada, w_ada2, m_w_ada[0], v_w_ada[0], "adamw_w_ada")
    land = _chip_exchange_wait(sems, c_in, land, d_ada)
    own = lax.dynamic_index_in_dim(c_in, 2 * lax.axis_index("x") + lax.axis_index("y"), 0, keepdims=False)
    big["w_in"] = [t.T[None] for t in _adamw_sharded(land, w_in2.T, m_w_in[0].T, v_w_in[0].T, "adamw_w_in", 336,
                                                     extra=own)]
    ada = [t[None] for t in (g_w_ada, d_ada, nm_ada, nv_ada)]
    lb_w = hgrn_lb_logits.reshape(4, 64)
    d_lb, nm_lb, nv_lb = _adamw_plain(g_lb_mine.reshape(4, 64), lb_w, m_hgrn_lb_logits.reshape(4, 64),
                                      v_hgrn_lb_logits.reshape(4, 64), "adamw_lb")
    lbs = [t.reshape(2, 2, 64) for t in (g_lb_mine, d_lb, nm_lb, nv_lb)]

    names = ['c_ctx', 'w_ada', 'b_ada', 'norm_mix_w', 'norm_ffn_w', 'w_in', 'hgrn_lb_logits', 'hgrn_norm_w',
             'q_norm_w', 'k_norm_w', 'attn_sinks', 'w_branch_hgrn', 'w_branch_attn', 'w_out', 'w_ffn_gate',
             'w_ffn_up', 'w_ffn_down']
    outs = [loss, grad_x[None]]
    for kind in range(4):
        for nm in names:
            if nm == 'w_ada':
                outs.append(ada[kind])
            elif nm == 'hgrn_lb_logits':
                outs.append(lbs[kind])
            elif nm in big:
                outs.append(big[nm][kind])
            else:
                outs.append(rep[kind][nm])
    return tuple(outs)
```

```python
import functools
import math

import numpy as np
import jax
import jax.numpy as jnp
from jax import lax
from jax.experimental import pallas as pl
from jax.experimental.pallas import tpu as pltpu

F32 = jnp.float32
BF16 = jnp.bfloat16

N_DEV = 8
D = 1024
S = 2048
L = 256
T = L + S
TM = 256
N_TILES = T // TM
N_LAT_TILES = S // TM
HG_HEADS = 4
HG_DIM = 128
HGW = 512
CHUNK = 32
N_CHUNKS = T // CHUNK
N_CTX_CHUNKS = L // CHUNK
ATT_HEADS = 8
KV_HEADS = 2
HEAD_DIM = 64
ATW = 512
KVW = 128
BLOCK = 128
N_BLOCKS = S // BLOCK
GRID_W = 64
ROPE_THETA = 10000.0
D_FF = 2816
FF_BLK = D_FF // N_DEV
FF_TILE = 256
N_FF_TILES = D_FF // FF_TILE
IN_COLS = 5376
IN_BLK = IN_COLS // N_DEV
ADA_BLK = 6 * D // N_DEV
EPS = 1e-6
WA, WB, WC = 2048, 1280, 2048

ADAM_LR = 0.001
ADAM_B1 = 0.9
ADAM_B2 = 0.999
ADAM_EPS = 1e-08
ADAM_WD = 0.01
ADAM_STEP = 10

VMEM_LIMIT = 56 * 1024 * 1024
MESH = pl.DeviceIdType.MESH


def _cp(sem=None, vmem=VMEM_LIMIT):
    return pltpu.CompilerParams(dimension_semantics=sem, vmem_limit_bytes=vmem)


def _full(shape):
    n = len(shape)
    return pl.BlockSpec(shape, lambda *_: (0,) * n)


_VMEM_WHOLE = pl.BlockSpec(memory_space=pltpu.VMEM)
_ANY = pl.BlockSpec(memory_space=pl.ANY)


def _sigmoid(v):
    return 1.0 / (1.0 + jnp.exp(-v))


def _dot(a, b):
    return jnp.dot(a, b, preferred_element_type=F32)


def _dot_nt(a, b):
    return lax.dot_general(a, b, (((1,), (1,)), ((), ())), preferred_element_type=F32)


def _dot_tn(a, b):
    return lax.dot_general(a, b, (((0,), (0,)), ((), ())), preferred_element_type=F32)


def _split3(v):
    hi = v.astype(BF16)
    r = v - hi.astype(F32)
    mid = r.astype(BF16)
    lo = (r - mid.astype(F32)).astype(BF16)
    return hi, mid, lo


def _dot_exact_rhs01(v, m01):
    hi, mid, lo = _split3(v)
    return _dot(hi, m01) + _dot(mid, m01) + _dot(lo, m01)


def _split2(v):
    hi = v.astype(BF16)
    return hi, (v - hi.astype(F32)).astype(BF16)


def _dot_lhs01(m01, v):
    hi, lo = _split2(v)
    return _dot(m01, hi) + _dot(m01, lo)


def _dot_f32(a, b, dot=_dot):
    ah, am, al = _split3(a)
    bh, bm, bl = _split3(b)
    return (dot(ah, bh) + (dot(ah, bm) + dot(am, bh))
            + (dot(am, bm) + dot(ah, bl) + dot(al, bh)))


def _my_pos():
    return lax.axis_index("x"), lax.axis_index("y"), lax.axis_index("c")


class _Comm:
    def __init__(self, operands, out_shapes, sems, phases):
        self.operands, self.out_shapes, self.sems, self.phases = operands, out_shapes, sems, phases


def _gather_comm(blocks):
    n = len(blocks)

    def parts(ins, outs, sems):
        send_sems, recv_sems, local_sems = sems
        x, y, c = _my_pos()
        me, sibling = (x, y, c), (x, y, 1 - c)
        chips = [(1 - x, y), (x, 1 - y), (1 - x, 1 - y)]

        def slot(a, px, py, pc):
            return outs[a].at[4 * px + 2 * py + pc]

        def copy(a, k, block, to, src=None):
            return pltpu.make_async_remote_copy(
                src_ref=slot(a, *block) if src is None else src, dst_ref=slot(a, *block),
                send_sem=send_sems.at[a, k], recv_sem=recv_sems.at[a, k],
                device_id=to, device_id_type=MESH)

        mine = [pltpu.make_async_copy(ins[a], slot(a, *me), local_sems.at[a]) for a in range(n)]
        first = []
        for a in range(n):
            first.append(copy(a, 0, me, sibling, src=ins[a]))
            first += [copy(a, 1 + j, me, (*chip, c), src=ins[a]) for j, chip in enumerate(chips)]
        passed = [copy(a, 4 + j, (*chip, c), sibling) for j, chip in enumerate(chips) for a in range(n)]
        return c, me, sibling, chips, copy, mine, first, passed

    def start(ins, outs, sems):
        _, _, _, _, _, mine, first, _ = parts(ins, outs, sems)
        for cp in mine + first:
            cp.start()

    def forward(ins, outs, sems):
        c, me, _, chips, copy, _, _, passed = parts(ins, outs, sems)
        for j, chip in enumerate(chips):
            for a in range(n):
                copy(a, 1 + j, (*chip, c), me).wait_recv()
                passed[j * n + a].start()

    def finish(ins, outs, sems):
        c, me, sibling, chips, copy, mine, first, passed = parts(ins, outs, sems)
        for a in range(n):
            copy(a, 0, sibling, me).wait_recv()
            for j, chip in enumerate(chips):
                copy(a, 4 + j, (*chip, 1 - c), me).wait_recv()
        for cp in first + passed:
            cp.wait_send()
        for cp in mine:
            cp.wait()

    return _Comm(blocks, [jax.ShapeDtypeStruct((N_DEV,) + b.shape, b.dtype) for b in blocks],
                 [pltpu.SemaphoreType.DMA((n, 7)), pltpu.SemaphoreType.DMA((n, 7)), pltpu.SemaphoreType.DMA((n,))],
                 [start, forward, finish])


def _gather_comm_relayed(blocks):
    n = len(blocks)

    def parts(ins, outs, sems):
        send_sems, recv_sems, local_sems = sems
        x, y, c = _my_pos()
        me, sibling = (x, y, c), (x, y, 1 - c)
        x_nbr, y_nbr, diag = (1 - x, y, c), (x, 1 - y, c), (1 - x, 1 - y, c)

        def slot(a, dev, half=None):
            ref = outs[a].at[4 * dev[0] + 2 * dev[1] + dev[2]]
            if half is None:
                return ref
            rows = blocks[a].shape[0] // 2
            return ref.at[pl.ds(half * rows, rows)]

        def copy(a, k, block, to, half=None, src=None):
            return pltpu.make_async_remote_copy(
                src_ref=slot(a, block, half) if src is None else src, dst_ref=slot(a, block, half),
                send_sem=send_sems.at[a, k], recv_sem=recv_sems.at[a, k],
                device_id=to, device_id_type=MESH)

        mine = [pltpu.make_async_copy(ins[a], slot(a, me), local_sems.at[a]) for a in range(n)]
        return me, sibling, x_nbr, y_nbr, diag, copy, mine

    def start(ins, outs, sems):
        me, sibling, x_nbr, y_nbr, _, copy, mine = parts(ins, outs, sems)
        for cp in mine:
            cp.start()
        for a in range(n):
            for k, to in ((1, x_nbr), (2, y_nbr), (0, sibling)):
                copy(a, k, me, to, src=ins[a]).start()

    def forward(ins, outs, sems):
        me, sibling, x_nbr, y_nbr, _, copy, _ = parts(ins, outs, sems)
        for a in range(n):
            copy(a, 1, x_nbr, me).wait_recv()
            copy(a, 3, x_nbr, y_nbr, half=0).start()
            copy(a, 5, x_nbr, sibling).start()
        for a in range(n):
            copy(a, 2, y_nbr, me).wait_recv()
            copy(a, 4, y_nbr, x_nbr, half=1).start()
            copy(a, 6, y_nbr, sibling).start()

    def finish(ins, outs, sems):
        me, sibling, x_nbr, y_nbr, diag, copy, mine = parts(ins, outs, sems)
        sib = lambda dev: (dev[0], dev[1], sibling[2])
        for a in range(n):
            copy(a, 3, diag, me, half=0).wait_recv()
            copy(a, 4, diag, me, half=1).wait_recv()
            copy(a, 7, diag, sibling).start()
        for a in range(n):
            copy(a, 0, sibling, me).wait_recv()
            for k, dev in ((5, x_nbr), (6, y_nbr), (7, diag)):
                copy(a, k, sib(dev), me).wait_recv()
        for a in range(n):
            for k, block, to, half in ((0, me, sibling, None), (1, me, x_nbr, None), (2, me, y_nbr, None),
                                       (3, x_nbr, y_nbr, 0), (4, y_nbr, x_nbr, 1), (5, x_nbr, sibling, None),
                                       (6, y_nbr, sibling, None), (7, diag, sibling, None)):
                copy(a, k, block, to, half=half, src=ins[a] if block is me else None).wait_send()
        for cp in mine:
            cp.wait()

    return _Comm(blocks, [jax.ShapeDtypeStruct((N_DEV,) + b.shape, b.dtype) for b in blocks],
                 [pltpu.SemaphoreType.DMA((n, 8)), pltpu.SemaphoreType.DMA((n, 8)), pltpu.SemaphoreType.DMA((n,))],
                 [start, forward, finish])


_HBM = pl.BlockSpec(memory_space=pltpu.HBM)
_SEM = pl.BlockSpec(memory_space=pltpu.SEMAPHORE)
_SPLIT_COPY = pltpu.CompilerParams(has_side_effects=pltpu.SideEffectType.DATAFLOW_SIDE_EFFECTING)


def _chip_exchange_copies(src_ref, land_ref, sems):
    x, y, c = _my_pos()
    q_me = 2 * x + y
    pairs = []
    for j, (px, py) in enumerate([(1 - x, y), (x, 1 - y), (1 - x, 1 - y)]):
        q = 2 * px + py
        send = pltpu.make_async_remote_copy(
            src_ref=src_ref.at[q], dst_ref=land_ref.at[q_me], send_sem=sems[j], recv_sem=sems[3 + j],
            device_id=(px, py, c), device_id_type=MESH)
        recv = pltpu.make_async_remote_copy(
            src_ref=src_ref.at[q], dst_ref=land_ref.at[q], send_sem=sems[j], recv_sem=sems[3 + j],
            device_id=(x, y, c), device_id_type=MESH)
        pairs.append((send, recv))
    return pairs


def _chip_exchange_start(src, land):
    def body(src_ref, land_ref, *outs):
        sems, token = outs[:6], outs[8]
        for send, _ in _chip_exchange_copies(src_ref, land_ref, sems):
            send.start()
        token[...] = jnp.zeros_like(token)

    res = pl.pallas_call(
        body, name="scatter_in_start",
        out_shape=(pltpu.SemaphoreType.DMA(()),) * 6 + (
            pltpu.HBM(src.shape, src.dtype), pltpu.HBM(land.shape, land.dtype),
            jax.ShapeDtypeStruct((8, 128), F32)),
        in_specs=(_HBM, _HBM), out_specs=(_SEM,) * 6 + (_HBM, _HBM, pl.BlockSpec(memory_space=pltpu.VMEM)),
        input_output_aliases={0: 6, 1: 7}, compiler_params=_SPLIT_COPY,
    )(pltpu.with_memory_space_constraint(src, pltpu.HBM), pltpu.with_memory_space_constraint(land, pltpu.HBM))
    return res[:6], res[6], res[7], res[8]


def _chip_exchange_wait(sems, src_thru, land_thru, after):
    def body(src_ref, land_ref, *rest):
        for send, recv in _chip_exchange_copies(src_ref, land_ref, rest[:6]):
            send.wait_send()
            recv.wait_recv()

    return pl.pallas_call(
        body, name="scatter_in_wait",
        out_shape=(pltpu.HBM(src_thru.shape, src_thru.dtype), pltpu.HBM(land_thru.shape, land_thru.dtype)),
        in_specs=(_HBM, _HBM) + (_SEM,) * 6 + (_ANY,), out_specs=(_HBM, _HBM),
        input_output_aliases={0: 0, 1: 1}, compiler_params=_SPLIT_COPY,
    )(src_thru, land_thru, *sems, after)[1]


def _run_comm(comm, name, in_vmem=False):
    n_in, n_out = len(comm.operands), len(comm.out_shapes)

    def body(*refs):
        ins, outs, sems = refs[:n_in], refs[n_in:n_in + n_out], refs[n_in + n_out:]
        for phase in comm.phases:
            phase(ins, outs, sems)

    spec = _VMEM_WHOLE if in_vmem else _ANY
    return pl.pallas_call(
        body, name=name, out_shape=comm.out_shapes, in_specs=[spec] * n_in, out_specs=[spec] * n_out,
        scratch_shapes=comm.sems,
    )(*comm.operands)


def _carrier_call(body, comm, schedule, *, name, grid, in_specs, out_specs, out_shape, scratch_shapes, operands):
    n_in, n_out, n_scr = len(in_specs), len(out_specs), len(scratch_shapes)
    c_in, c_out = len(comm.operands), len(comm.out_shapes)

    def full_body(*refs):
        ins, refs = refs[:n_in], refs[n_in:]
        cins, refs = refs[:c_in], refs[c_in:]
        outs, refs = refs[:n_out], refs[n_out:]
        couts, refs = refs[:c_out], refs[c_out:]
        scr, csems = refs[:n_scr], refs[n_scr:]
        step = pl.program_id(0)

        def run(before):
            for (at, when_before), phase in zip(schedule, comm.phases):
                if when_before == before:
                    pl.when(step == at)(functools.partial(phase, cins, couts, csems))

        run(True)
        body(*ins, *outs, *scr)
        run(False)

    res = pl.pallas_call(
        full_body, name=name, grid=grid,
        in_specs=list(in_specs) + [_ANY] * c_in, out_specs=list(out_specs) + [_ANY] * c_out,
        out_shape=list(out_shape) + list(comm.out_shapes),
        scratch_shapes=list(scratch_shapes) + list(comm.sems),
        compiler_params=_cp(("arbitrary",)),
    )(*operands, *comm.operands)
    return res[:n_out], res[n_out:]


def _pcall(body, carried, *, name, grid, in_specs, out_specs, out_shape, scratch_shapes, operands):
    if carried is None:
        res = pl.pallas_call(body, name=name, grid=grid, in_specs=in_specs, out_specs=out_specs,
                             out_shape=out_shape, scratch_shapes=scratch_shapes,
                             compiler_params=_cp(("arbitrary",)))(*operands)
        return res, ()
    return _carrier_call(body, carried[0], carried[1], name=name, grid=grid, in_specs=in_specs,
                         out_specs=out_specs, out_shape=out_shape, scratch_shapes=scratch_shapes,
                         operands=operands)


def _all_gather(blocks, name, in_vmem):
    return _run_comm(_gather_comm(blocks), name, in_vmem)


N_CHIPS = 4


def _sibling_comm(contribs):
    n = len(contribs)

    def copies(ins, outs, sems):
        send_sems, recv_sems = sems
        x, y, c = _my_pos()
        return [pltpu.make_async_remote_copy(
            src_ref=ins[a].at[pl.ds(0, N_CHIPS), 1 - c], dst_ref=outs[a],
            send_sem=send_sems.at[a], recv_sem=recv_sems.at[a],
            device_id=(x, y, 1 - c), device_id_type=MESH) for a in range(n)]

    def start(ins, outs, sems):
        for cp in copies(ins, outs, sems):
            cp.start()

    def finish(ins, outs, sems):
        cps = copies(ins, outs, sems)
        for cp in cps:
            cp.wait_recv()
        for cp in cps:
            cp.wait_send()

    return _Comm(contribs, [jax.ShapeDtypeStruct((N_CHIPS,) + b.shape[2:], b.dtype) for b in contribs],
                 [pltpu.SemaphoreType.DMA((n,)), pltpu.SemaphoreType.DMA((n,))], [start, finish])


def _pair_sum(mine, theirs, name):
    _, _, rows, cols = mine.shape
    core = lax.axis_index("c").astype(jnp.int32).reshape(1)

    def body(c_ref, m_ref, t_ref, o_ref):
        o_ref[...] = (m_ref[...].astype(F32) + t_ref[...].astype(F32)).astype(BF16)

    return pl.pallas_call(
        body, name=name,
        grid_spec=pltpu.PrefetchScalarGridSpec(
            num_scalar_prefetch=1, grid=(N_CHIPS,),
            in_specs=[pl.BlockSpec((None, None, rows, cols), lambda q, c: (q, c[0], 0, 0)),
                      pl.BlockSpec((None, rows, cols), lambda q, c: (q, 0, 0))],
            out_specs=pl.BlockSpec((None, rows, cols), lambda q, c: (q, 0, 0))),
        out_shape=jax.ShapeDtypeStruct((N_CHIPS, rows, cols), BF16),
        compiler_params=_cp(("parallel",)),
    )(core, mine, theirs)


def _chip_comm(sums):
    n = len(sums)

    def parts(ins, outs, sems):
        send_sems, recv_sems, local_sems = sems
        x, y, c = _my_pos()
        q_me = 2 * x + y
        chips = [(1 - x, y), (x, 1 - y), (1 - x, 1 - y)]
        mine = [pltpu.make_async_copy(ins[a].at[q_me], outs[a].at[q_me], local_sems.at[a]) for a in range(n)]
        sends, recvs = [], []
        for j, (px, py) in enumerate(chips):
            for a in range(n):
                q = 2 * px + py
                sends.append(pltpu.make_async_remote_copy(
                    src_ref=ins[a].at[q], dst_ref=outs[a].at[q_me],
                    send_sem=send_sems.at[a, j], recv_sem=recv_sems.at[a, j],
                    device_id=(px, py, c), device_id_type=MESH))
                recvs.append(pltpu.make_async_remote_copy(
                    src_ref=ins[a].at[q], dst_ref=outs[a].at[q],
                    send_sem=send_sems.at[a, j], recv_sem=recv_sems.at[a, j],
                    device_id=(x, y, c), device_id_type=MESH))
        return mine, sends, recvs

    def start(ins, outs, sems):
        mine, sends, _ = parts(ins, outs, sems)
        for cp in mine + sends:
            cp.start()

    def finish(ins, outs, sems):
        mine, sends, recvs = parts(ins, outs, sems)
        for cp in recvs:
            cp.wait_recv()
        for cp in sends:
            cp.wait_send()
        for cp in mine:
            cp.wait()

    return _Comm(sums, [jax.ShapeDtypeStruct(b.shape, b.dtype) for b in sums],
                 [pltpu.SemaphoreType.DMA((n, 3)), pltpu.SemaphoreType.DMA((n, 3)), pltpu.SemaphoreType.DMA((n,))],
                 [start, finish])


def _mm_nt(a, bt, *, tm, tn, out_dtype, name, row_off=0, rows=None):
    rows = a.shape[0] if rows is None else rows
    n, k = bt.shape

    def body(a_ref, b_ref, o_ref):
        o_ref[...] = _dot_nt(a_ref[...], b_ref[...]).astype(out_dtype)

    return pl.pallas_call(
        body, name=name, grid=(rows // tm, n // tn),
        in_specs=[pl.BlockSpec((tm, k), lambda i, j: (i + row_off, 0)),
                  pl.BlockSpec((tn, k), lambda i, j: (j, 0))],
        out_specs=pl.BlockSpec((tm, tn), lambda i, j: (i, j)),
        out_shape=jax.ShapeDtypeStruct((rows, n), out_dtype),
        compiler_params=_cp(("parallel", "parallel")),
    )(a, bt)


def _mm_tn(a, b, *, tk, nk, tm, tn, out_dtype, name, a_off=0, b_off=0):
    m, n = a.shape[1], b.shape[1]

    def body(a_ref, b_ref, o_ref, acc):
        kk = pl.program_id(2)

        @pl.when(kk == 0)
        def _():
            acc[...] = jnp.zeros_like(acc)

        acc[...] += _dot_tn(a_ref[...], b_ref[...])

        @pl.when(kk == nk - 1)
        def _():
            o_ref[...] = acc[...].astype(out_dtype)

    return pl.pallas_call(
        body, name=name, grid=(m // tm, n // tn, nk),
        in_specs=[pl.BlockSpec((tk, tm), lambda i, j, kk: (kk + a_off, i)),
                  pl.BlockSpec((tk, tn), lambda i, j, kk: (kk + b_off, j))],
        out_specs=pl.BlockSpec((tm, tn), lambda i, j, kk: (i, j)),
        out_shape=jax.ShapeDtypeStruct((m, n), out_dtype),
        scratch_shapes=[pltpu.VMEM((tm, tn), F32)],
        compiler_params=_cp(("parallel", "parallel", "arbitrary")),
    )(a, b)


def _mm_tn_blocked(a, b, name, carried=None):
    nb, _, w = a.shape
    n = b.shape[1]

    def body(a_ref, b_ref, o_ref):
        o_ref[...] = _dot_tn(a_ref[...], b_ref[...]).astype(BF16)

    (out,), extra = _pcall(
        body, carried, name=name, grid=(nb,),
        in_specs=[pl.BlockSpec((None, S, w), lambda j: (j, 0, 0)), _full((S, n))],
        out_specs=[pl.BlockSpec((None, w, n), lambda j: (j, 0, 0))],
        out_shape=[jax.ShapeDtypeStruct((nb, w, n), BF16)],
        scratch_shapes=[], operands=[a, b])
    return out, extra


def _prologue(cond, c_ctx, w_ada, b_cols, w_in_t, x, ctx, nw):
    rows_shape = jax.ShapeDtypeStruct((16, ADA_BLK), F32)
    big, g_cond, g_mod = _gather_comm_relayed([w_in_t]), _gather_comm([cond]), _gather_comm([rows_shape])

    def body(cond_ref, cctx_ref, wada_ref, b_ref, nw_ref, win_ref, x_ref, ctx_ref,
             g0_ref, cc_ref, mod_ref, gin_ref, h_ref, hl_ref, rows_ref, g1_ref, x_s, ctx_s, h_s, io_sems, *sems):
        s_big, s_cond, s_mod = sems[0:3], sems[3:6], sems[6:9]
        big.phases[0]([win_ref], [gin_ref], s_big)
        load_x = pltpu.make_async_copy(x_ref, x_s, io_sems.at[0])
        load_ctx = pltpu.make_async_copy(ctx_ref, ctx_s, io_sems.at[1])
        load_x.start()
        load_ctx.start()
        for phase in g_cond.phases:
            phase([cond_ref], [g0_ref], s_cond)
        cc_ref[...] = jnp.zeros_like(cc_ref)
        for j in range(N_DEV):
            cc_ref[j:j + 1, :] = g0_ref[j, 0:1, :]
        cc_ref[N_DEV:N_DEV + 1, :] = cctx_ref[...]
        cv = cc_ref[...]
        rows_ref[...] = _dot_f32(cv * _sigmoid(cv), wada_ref[...]) + b_ref[...]
        for phase in g_mod.phases:
            phase([rows_ref], [g1_ref], s_mod)
        x_pos, y_pos, c_pos = _my_pos()
        me = 4 * x_pos + 2 * y_pos + c_pos
        mine = jnp.concatenate([g1_ref[j, pl.ds(me, 1), :] for j in range(N_DEV)], axis=1)
        shared = jnp.concatenate([g1_ref[j, N_DEV:N_DEV + 1, :] for j in range(N_DEV)], axis=1)
        for k in range(6):
            mod_ref[k:k + 1, :] = mine[:, k * D:(k + 1) * D]
        mod_ref[6:7, :] = shared[:, 0:D]
        mod_ref[7:8, :] = shared[:, D:2 * D]
        load_ctx.wait()
        load_x.wait()
        h_s[pl.ds(0, L), :] = _rms_mod(ctx_s[...], nw_ref[...], mod_ref[6:7, :], mod_ref[7:8, :]).astype(BF16)

        def norm_tile(i, carry):
            r0 = pl.multiple_of(i * TM, TM)
            h_s[pl.ds(L + r0, TM), :] = _rms_mod(
                x_s[pl.ds(r0, TM), :], nw_ref[...], mod_ref[0:1, :], mod_ref[1:2, :]).astype(BF16)
            return carry

        lax.fori_loop(0, N_LAT_TILES, norm_tile, 0)
        stores = [pltpu.make_async_copy(h_s, h_ref, io_sems.at[2]),
                  pltpu.make_async_copy(h_s.at[pl.ds(L, S)], hl_ref, io_sems.at[3])]
        for cp in stores:
            cp.start()
        big.phases[1]([win_ref], [gin_ref], s_big)
        big.phases[2]([win_ref], [gin_ref], s_big)
        for cp in stores:
            cp.wait()

    return pl.pallas_call(
        body, name="prologue",
        in_specs=[_VMEM_WHOLE] * 5 + [_ANY] * 3, out_specs=[_VMEM_WHOLE] * 3 + [_ANY] * 3,
        out_shape=[g_cond.out_shapes[0], jax.ShapeDtypeStruct((16, D), F32), jax.ShapeDtypeStruct((8, D), F32),
                   big.out_shapes[0], jax.ShapeDtypeStruct((T, D), BF16), jax.ShapeDtypeStruct((S, D), BF16)],
        scratch_shapes=[pltpu.VMEM((16, ADA_BLK), F32), pltpu.VMEM((N_DEV, 16, ADA_BLK), F32),
                        pltpu.VMEM((S, D), F32), pltpu.VMEM((L, D), F32), pltpu.VMEM((T, D), BF16),
                        pltpu.SemaphoreType.DMA((4,))] + big.sems + g_cond.sems + g_mod.sems,
        compiler_params=_cp(),
    )(cond, c_ctx, w_ada, b_cols, nw, w_in_t, x, ctx)


def _ada_grads(cc, dm_cols, w_ada):
    def body(c_ref, dm_ref, w_ref, gw_ref, dsc_ref):
        cv = c_ref[...]
        sc = cv * _sigmoid(cv)
        dm = dm_ref[...]
        gw_ref[...] = _dot_f32(sc, dm, dot=_dot_tn)
        dsc_ref[...] = _dot_f32(dm, w_ref[...], dot=_dot_nt)

    return pl.pallas_call(
        body, name="ada_grads",
        in_specs=[_VMEM_WHOLE] * 3, out_specs=[_VMEM_WHOLE] * 2,
        out_shape=[jax.ShapeDtypeStruct((D, ADA_BLK), F32), jax.ShapeDtypeStruct((16, D), F32)],
        compiler_params=_cp(),
    )(cc, dm_cols, w_ada)


def _lat(i):
    return jnp.maximum(i - 1, 0)


def _rms_mod(xv, nw, sh, sc):
    rstd = lax.rsqrt(jnp.mean(xv * xv, axis=-1, keepdims=True) + EPS)
    return (xv * rstd * nw) * (1.0 + sc) + sh


def _rms_mod_bwd(xv, nw, sc, dh):
    rstd = lax.rsqrt(jnp.mean(xv * xv, axis=-1, keepdims=True) + EPS)
    xhat = xv * rstd
    dn = dh * (1.0 + sc)
    dxhat = dn * nw
    dx = rstd * (dxhat - xhat * jnp.mean(dxhat * xhat, axis=-1, keepdims=True))
    return (dx, jnp.sum(dh, axis=0, keepdims=True), jnp.sum(dh * (xhat * nw), axis=0, keepdims=True),
            jnp.sum(dn * xhat, axis=0, keepdims=True))


def _chunk_masks(reverse):
    row = lax.broadcasted_iota(jnp.int32, (TM, TM), 0)
    col = lax.broadcasted_iota(jnp.int32, (TM, TM), 1)
    same = (row // CHUNK) == (col // CHUNK)
    tri = same & ((col >= row) if reverse else (col <= row))
    return same, tri


def _chunk_order(i, reverse):
    if not reverse:
        return i
    return jnp.where(i < N_CTX_CHUNKS, N_CTX_CHUNKS - 1 - i, N_CHUNKS + N_CTX_CHUNKS - 1 - i)


def _decay_terms(z, lb, same01, tri01):
    f = lb + (1.0 - lb) * _sigmoid(z)
    g = jnp.log(f)
    g2 = jnp.concatenate(_split2(g), axis=1)
    b2 = _dot(tri01, g2)
    t2 = _dot(same01, g2)
    return f, 1.0 - f, b2[:, :HG_DIM] + b2[:, HG_DIM:], t2[:, :HG_DIM] + t2[:, HG_DIM:]


def _chunk_outer(a, b):
    n = TM // CHUNK
    return jnp.einsum('ncv,nck->nvk', a.reshape(n, CHUNK, HG_DIM), b.reshape(n, CHUNK, HG_DIM),
                      preferred_element_type=F32)


def _hgrn_fwd(p_a, lbl, carried=None):
    cpt = TM // CHUNK

    def body(p_ref, lbl_ref, o_ref, st_ref, qd_s, kd_s, u_s, v_s, ebt_s):
        masks = [_chunk_masks(d == 1) for d in (0, 1)]
        same01 = jnp.where(masks[0][0], 1.0, 0.0).astype(BF16)
        tri = [m[1] for m in masks]
        tri01 = [jnp.where(t, 1.0, 0.0).astype(BF16) for t in tri]
        lb = [_sigmoid(lbl_ref[d][0:1, :] - lbl_ref[d][1:2, :]) for d in (0, 1)]

        def prep(r, carry):
            r0 = pl.multiple_of(r * TM, TM)
            vb = p_ref[pl.ds(r0, TM), 2 * HG_DIM:3 * HG_DIM].astype(BF16)
            v_s[pl.ds(r0, TM), :] = vb
            for d in (0, 1):
                z = p_ref[pl.ds(r0, TM), d * HG_DIM:(d + 1) * HG_DIM]
                _, k, b, bt = _decay_terms(z, lb[d], same01, tri01[d])
                u_s[d, pl.ds(r * cpt, cpt)] = _chunk_outer(vb, (k * jnp.exp(bt - b)).astype(BF16))
                ebt_s[d, pl.ds(r0, TM), :] = jnp.exp(bt)

                @pl.when(r >= 1)
                def _():
                    rl = pl.multiple_of(r0 - L, TM)
                    qr = p_ref[pl.ds(r0, TM), 3 * HG_DIM:4 * HG_DIM]
                    q = qr * _sigmoid(qr) * HG_DIM ** -0.5
                    qd_s[d, pl.ds(rl, TM), :] = (q * jnp.exp(b)).astype(BF16)
                    kd_s[d, pl.ds(rl, TM), :] = (k * jnp.exp(-b)).astype(BF16)

            return carry

        lax.fori_loop(0, N_TILES, prep, 0)

        def scan(i, sts):
            new = []
            for d in (0, 1):
                nn = _chunk_order(i, d == 1)
                c0 = pl.multiple_of(nn * CHUNK, CHUNK)
                st_ref[d, nn] = sts[d].astype(BF16)
                new.append(sts[d] * ebt_s[d, pl.ds(c0, 1), :] + u_s[d, nn])
            return tuple(new)

        zero = jnp.zeros((HG_DIM, HG_DIM), F32)
        lax.fori_loop(0, N_CHUNKS, scan, (zero, zero))

        def outp(r, carry):
            r0 = pl.multiple_of(r * TM, TM)
            vb = v_s[pl.ds(r0 + L, TM), :]
            o = jnp.zeros((TM, HG_DIM), F32)
            for d in (0, 1):
                qd = qd_s[d, pl.ds(r0, TM), :]
                a = jnp.where(tri[d], _dot_nt(qd, kd_s[d, pl.ds(r0, TM), :]), 0.0)
                stb = st_ref[d, pl.ds(N_CTX_CHUNKS + r * cpt, cpt)]
                inter = jnp.einsum('nck,nvk->ncv', qd.reshape(cpt, CHUNK, HG_DIM), stb,
                                   preferred_element_type=F32)
                o = o + _dot(a.astype(BF16), vb) + inter.reshape(TM, HG_DIM)
            o_ref[pl.ds(r0, TM), :] = o
            return carry

        lax.fori_loop(0, N_LAT_TILES, outp, 0, unroll=2)

    return _pcall(
        body, carried, name="hgrn_fwd", grid=(HG_HEADS,),
        in_specs=[pl.BlockSpec((T, 4 * HG_DIM), lambda h: (0, h)),
                  pl.BlockSpec((2, 2, HG_DIM), lambda h: (0, 0, h))],
        out_specs=[pl.BlockSpec((S, HG_DIM), lambda h: (0, h)),
                   pl.BlockSpec((2, None, N_CHUNKS, HG_DIM, HG_DIM), lambda h: (0, h, 0, 0, 0))],
        out_shape=[jax.ShapeDtypeStruct((S, HGW), F32),
                   jax.ShapeDtypeStruct((2, HG_HEADS, N_CHUNKS, HG_DIM, HG_DIM), BF16)],
        scratch_shapes=[pltpu.VMEM((2, S, HG_DIM), BF16), pltpu.VMEM((2, S, HG_DIM), BF16),
                        pltpu.VMEM((2, N_CHUNKS, HG_DIM, HG_DIM), F32), pltpu.VMEM((T, HG_DIM), BF16),
                        pltpu.VMEM((2, T, HG_DIM), F32)],
        operands=[p_a, lbl])


def _hgrn_bwd(p_a, lbl, d_o, st, carried=None):
    cpt = TM // CHUNK

    def rows(r):
        return r * TM if isinstance(r, int) else pl.multiple_of(r * TM, TM)

    def body(p_ref, lbl_ref, do_ref, st_ref, dp_ref, dlb_ref, b_s, bt_s, dbt_s, qd_s, dst_s, w_s):
        masks = [_chunk_masks(d == 1) for d in (0, 1)]
        same01 = jnp.where(masks[0][0], 1.0, 0.0).astype(BF16)
        tri = [m[1] for m in masks]
        tri01 = [jnp.where(t, 1.0, 0.0).astype(BF16) for t in tri]
        later01 = [tri01[1], tri01[0]]
        lb = [_sigmoid(lbl_ref[d][0:1, :] - lbl_ref[d][1:2, :]) for d in (0, 1)]

        def prep_tile(r, latent):
            r0 = rows(r)
            for d in (0, 1):
                z = p_ref[pl.ds(r0, TM), d * HG_DIM:(d + 1) * HG_DIM]
                _, _, b, bt = _decay_terms(z, lb[d], same01, tri01[d])
                b_s[d, pl.ds(r0, TM), :] = b
                bt_s[d, pl.ds(r0, TM), :] = bt
                if latent:
                    rl = pl.multiple_of(r0 - L, TM)
                    qr = p_ref[pl.ds(r0, TM), 3 * HG_DIM:4 * HG_DIM]
                    qd = (qr * _sigmoid(qr) * HG_DIM ** -0.5 * jnp.exp(b)).astype(BF16)
                    qd_s[d, pl.ds(rl, TM), :] = qd
                    w_s[d, pl.ds(r * cpt, cpt)] = _chunk_outer(
                        do_ref[pl.ds(rl, TM), :].astype(BF16), qd).astype(BF16)

        prep_tile(0, False)
        w_s[:, pl.ds(0, N_CTX_CHUNKS)] = jnp.zeros((2, N_CTX_CHUNKS, HG_DIM, HG_DIM), BF16)

        def prep(r, carry):
            prep_tile(r, True)
            return carry

        lax.fori_loop(1, N_TILES, prep, 0, unroll=2)

        def rscan(j, dsts):
            i = N_CHUNKS - 1 - j
            new = []
            for d in (0, 1):
                nn = _chunk_order(i, d == 1)
                c0 = pl.multiple_of(nn * CHUNK, CHUNK)
                dst_s[d, nn] = dsts[d].astype(BF16)
                after = st_ref[d, _chunk_order(jnp.minimum(i + 1, N_CHUNKS - 1), d == 1)].astype(F32)
                dbt_s[d, pl.ds(c0, CHUNK), :] = jnp.broadcast_to(
                    jnp.sum(after * dsts[d], axis=0, keepdims=True), (CHUNK, HG_DIM))
                new.append(dsts[d] * jnp.exp(bt_s[d, pl.ds(c0, 1), :]) + w_s[d, nn].astype(F32))
            return tuple(new)

        zero = jnp.zeros((HG_DIM, HG_DIM), F32)
        lax.fori_loop(0, N_CHUNKS, rscan, (zero, zero))

        def grad_tile(r, latent):
            r0 = rows(r)
            vb = p_ref[pl.ds(r0, TM), 2 * HG_DIM:3 * HG_DIM].astype(BF16)
            dv = jnp.zeros((TM, HG_DIM), F32)
            dq = jnp.zeros((TM, HG_DIM), F32)
            dlbs = []
            if latent:
                rl = pl.multiple_of(r0 - L, TM)
                qr = p_ref[pl.ds(r0, TM), 3 * HG_DIM:4 * HG_DIM]
                sq = _sigmoid(qr)
                do = do_ref[pl.ds(rl, TM), :].astype(BF16)
                da_full = _dot_nt(do, vb)
            for d in (0, 1):
                z = p_ref[pl.ds(r0, TM), d * HG_DIM:(d + 1) * HG_DIM]
                sz = _sigmoid(z)
                f = lb[d] + (1.0 - lb[d]) * sz
                k = 1.0 - f
                b = b_s[d, pl.ds(r0, TM), :]
                e2 = jnp.exp(bt_s[d, pl.ds(r0, TM), :] - b)
                dstb = dst_s[d, pl.ds(r * cpt, cpt)]
                kd2 = k * e2
                dkd2 = jnp.einsum('ncv,nvk->nck', vb.reshape(cpt, CHUNK, HG_DIM), dstb,
                                  preferred_element_type=F32).reshape(TM, HG_DIM)
                dv = dv + jnp.einsum('nck,nvk->ncv', kd2.astype(BF16).reshape(cpt, CHUNK, HG_DIM), dstb,
                                     preferred_element_type=F32).reshape(TM, HG_DIM)
                dk = dkd2 * e2
                db = -(kd2 * dkd2)
                if latent:
                    eb = jnp.exp(b)
                    enb = jnp.exp(-b)
                    qdf = qr * sq * HG_DIM ** -0.5 * eb
                    kdf = k * enb
                    qd = qd_s[d, pl.ds(rl, TM), :]
                    kd = kdf.astype(BF16)
                    a = jnp.where(tri[d], _dot_nt(qd, kd), 0.0).astype(BF16)
                    da = jnp.where(tri[d], da_full, 0.0).astype(BF16)
                    stb = st_ref[d, pl.ds(r * cpt, cpt)]
                    dqd = _dot(da, kd) + jnp.einsum(
                        'ncv,nvk->nck', do.reshape(cpt, CHUNK, HG_DIM), stb,
                        preferred_element_type=F32).reshape(TM, HG_DIM)
                    dkd = _dot_tn(da, qd)
                    dv = dv + _dot_tn(a, do)
                    dk = dk + dkd * enb
                    db = db + qdf * dqd - kdf * dkd
                    dq = dq + dqd * eb
                dg = _dot_lhs01(later01[d], db) + dbt_s[d, pl.ds(r0, TM), :]
                df = dg / f - dk
                dp_ref[pl.ds(r0, TM), d * HG_DIM:(d + 1) * HG_DIM] = (
                    df * (1.0 - lb[d]) * sz * (1.0 - sz)).astype(BF16)
                dlbs.append(jnp.sum(df * (1.0 - sz), axis=0, keepdims=True))
            dp_ref[pl.ds(r0, TM), 2 * HG_DIM:3 * HG_DIM] = dv.astype(BF16)
            if latent:
                dq = dq * (HG_DIM ** -0.5) * (sq * (1.0 + qr * (1.0 - sq)))
            dp_ref[pl.ds(r0, TM), 3 * HG_DIM:4 * HG_DIM] = dq.astype(BF16)
            return dlbs

        dlb_ctx = grad_tile(0, False)

        def grads(r, acc):
            t = grad_tile(r, True)
            return (acc[0] + t[0], acc[1] + t[1])

        dlb = lax.fori_loop(1, N_TILES, grads, (dlb_ctx[0], dlb_ctx[1]))
        dlb_ref[0:1, :] = dlb[0]
        dlb_ref[1:2, :] = dlb[1]

    return _pcall(
        body, carried, name="hgrn_bwd", grid=(HG_HEADS,),
        in_specs=[pl.BlockSpec((T, 4 * HG_DIM), lambda h: (0, h)),
                  pl.BlockSpec((2, 2, HG_DIM), lambda h: (0, 0, h)),
                  pl.BlockSpec((S, HG_DIM), lambda h: (0, h)),
                  pl.BlockSpec((2, None, N_CHUNKS, HG_DIM, HG_DIM), lambda h: (0, h, 0, 0, 0))],
        out_specs=[pl.BlockSpec((T, 4 * HG_DIM), lambda h: (0, h)),
                   pl.BlockSpec((2, HG_DIM), lambda h: (0, h))],
        out_shape=[jax.ShapeDtypeStruct((T, WA), BF16), jax.ShapeDtypeStruct((2, HGW), F32)],
        scratch_shapes=[pltpu.VMEM((2, T, HG_DIM), F32), pltpu.VMEM((2, T, HG_DIM), F32),
                        pltpu.VMEM((2, T, HG_DIM), F32), pltpu.VMEM((2, S, HG_DIM), BF16),
                        pltpu.VMEM((2, N_CHUNKS, HG_DIM, HG_DIM), BF16),
                        pltpu.VMEM((2, N_CHUNKS, HG_DIM, HG_DIM), BF16)],
        operands=[p_a, lbl, d_o, st])


def _rope_tables():
    t = np.arange(S)
    inv = ROPE_THETA ** (-np.arange(0, 32, 2, dtype=np.float64) / 32)
    lane = np.arange(64)
    pos = np.where(lane[None, :] < 32, (t // GRID_W)[:, None], (t % GRID_W)[:, None]).astype(np.float64)
    ang = pos * inv[(lane % 32) % 16][None, :]
    sign = np.where((lane % 32) < 16, -1.0, 1.0)[None, :]
    cos = np.tile(np.cos(ang), (1, 2)).astype(np.float32)
    sin = np.tile(np.sin(ang) * sign, (1, 2)).astype(np.float32)
    return jnp.asarray(cos), jnp.asarray(sin)


def _rope_partner(v):
    lane = lax.broadcasted_iota(jnp.int32, (1, 128), 1)
    first = (lane % 32) < 16
    slabs = []
    for j in range(v.shape[1] // 128):
        s = v[:, 128 * j:128 * (j + 1)]
        slabs.append(jnp.where(first, pltpu.roll(s, 112, 1), pltpu.roll(s, 16, 1)))
    return slabs[0] if len(slabs) == 1 else jnp.concatenate(slabs, axis=1)


def _group_ones(width, group):
    r = lax.broadcasted_iota(jnp.int32, (width, width), 0)
    c = lax.broadcasted_iota(jnp.int32, (width, width), 1)
    return jnp.where((r // group) == (c // group), 1.0, 0.0).astype(BF16)


def _group_mean(v, ones01, group):
    hi = v.astype(BF16)
    lo = (v - hi.astype(F32)).astype(BF16)
    return (_dot(hi, ones01) + _dot(lo, ones01)) * (1.0 / group)


def _rep_matrix():
    r = lax.broadcasted_iota(jnp.int32, (KVW, ATW), 0)
    c = lax.broadcasted_iota(jnp.int32, (KVW, ATW), 1)
    return jnp.where(r == HEAD_DIM * (c // 256) + c % HEAD_DIM, 1.0, 0.0).astype(BF16)


def _tile_lanes(v, reps):
    return jnp.concatenate([v] * reps, axis=1)


def _prep_fwd(p_b, o, cos, sin, hnw, qnw, knw):
    def body(p_ref, o_ref, cos_ref, sin_ref, hnw_ref, qnw_ref, knw_ref, y_ref, q_ref, k_ref, v_ref):
        i = pl.program_id(0)
        rep = _rep_matrix()
        ones_k = _group_ones(KVW, HEAD_DIM)
        kr = p_ref[:, 1024:1152]
        krstd = lax.rsqrt(_group_mean(kr * kr, ones_k, HEAD_DIM) + EPS)
        kn = kr * krstd * knw_ref[...]
        v_ref[...] = _dot(p_ref[:, 1152:1280].astype(BF16), rep).astype(BF16)

        @pl.when(i == 0)
        def _():
            k_ref[...] = _dot(kn.astype(BF16), rep).astype(BF16)

        @pl.when(i > 0)
        def _():
            cs, sn = cos_ref[...], sin_ref[...]
            kro = kn * cs + _rope_partner(kn) * sn
            k_ref[...] = _dot(kro.astype(BF16), rep).astype(BF16)
            qr = p_ref[:, 512:1024]
            qrstd = lax.rsqrt(_group_mean(qr * qr, _group_ones(ATW, HEAD_DIM), HEAD_DIM) + EPS)
            qn = qr * qrstd * qnw_ref[...]
            qro = qn * _tile_lanes(cs, 4) + _rope_partner(qn) * _tile_lanes(sn, 4)
            q_ref[...] = (qro * HEAD_DIM ** -0.5).astype(BF16)
            ys = []
            for h in range(HG_HEADS):
                oh = o_ref[:, HG_DIM * h:HG_DIM * (h + 1)]
                gh = p_ref[:, HG_DIM * h:HG_DIM * (h + 1)]
                rstd = lax.rsqrt(jnp.mean(oh * oh, axis=-1, keepdims=True) + EPS)
                ys.append(oh * rstd * hnw_ref[...] * (gh * _sigmoid(gh)))
            y_ref[...] = jnp.concatenate(ys, axis=1).astype(BF16)

    return pl.pallas_call(
        body, name="prep_fwd", grid=(N_TILES,),
        in_specs=[pl.BlockSpec((TM, WB), lambda i: (i, 0)),
                  pl.BlockSpec((TM, HGW), lambda i: (_lat(i), 0)),
                  pl.BlockSpec((TM, 128), lambda i: (_lat(i), 0)),
                  pl.BlockSpec((TM, 128), lambda i: (_lat(i), 0)),
                  _full((1, HG_DIM)), _full((1, ATW)), _full((1, KVW))],
        out_specs=[pl.BlockSpec((TM, HGW), lambda i: (_lat(i), 0)),
                   pl.BlockSpec((TM, ATW), lambda i: (_lat(i), 0)),
                   pl.BlockSpec((TM, ATW), lambda i: (i, 0)),
                   pl.BlockSpec((TM, ATW), lambda i: (i, 0))],
        out_shape=[jax.ShapeDtypeStruct((S, HGW), BF16), jax.ShapeDtypeStruct((S, ATW), BF16),
                   jax.ShapeDtypeStruct((T, ATW), BF16), jax.ShapeDtypeStruct((T, ATW), BF16)],
        compiler_params=_cp(("arbitrary",)),
    )(p_b, o, cos, sin, hnw, qnw, knw)


def _prep_bwd(p_b, o, cos, sin, hnw, qnw, knw, dy_hg, dq, dk_rep, dv_rep, carried=None):
    def body(p_ref, o_ref, cos_ref, sin_ref, hnw_ref, qnw_ref, knw_ref, dy_ref, dq_ref, dk_ref, dv_ref,
             dp_ref, do_ref, acc_ref):
        i = pl.program_id(0)

        @pl.when(i == 0)
        def _():
            acc_ref[...] = jnp.zeros_like(acc_ref)

        rep = _rep_matrix()
        ones_k = _group_ones(KVW, HEAD_DIM)

        def fold(v):
            hi = v.astype(BF16)
            lo = (v - hi.astype(F32)).astype(BF16)
            return _dot_nt(hi, rep) + _dot_nt(lo, rep)

        kr = p_ref[:, 1024:1152]
        krstd = lax.rsqrt(_group_mean(kr * kr, ones_k, HEAD_DIM) + EPS)
        khat = kr * krstd
        kw = knw_ref[...]
        dkro = fold(dk_ref[...])
        dv = fold(dv_ref[...])

        def k_back(dkn):
            dkhat = dkn * kw
            dkr = krstd * (dkhat - khat * _group_mean(dkhat * khat, ones_k, HEAD_DIM))
            acc_ref[2:3, 0:KVW] += jnp.sum(dkn * khat, axis=0, keepdims=True)
            dp_ref[:, 1024:1152] = dkr.astype(BF16)
            dp_ref[:, 1152:1280] = dv.astype(BF16)

        @pl.when(i == 0)
        def _():
            k_back(dkro)
            dp_ref[:, 0:1024] = jnp.zeros((TM, 1024), BF16)

        @pl.when(i > 0)
        def _():
            cs, sn = cos_ref[...], sin_ref[...]
            k_back(dkro * cs + _rope_partner(dkro * sn))
            ones_q = _group_ones(ATW, HEAD_DIM)
            qr = p_ref[:, 512:1024]
            qrstd = lax.rsqrt(_group_mean(qr * qr, ones_q, HEAD_DIM) + EPS)
            qhat = qr * qrstd
            dqro = dq_ref[...] * HEAD_DIM ** -0.5
            dqn = dqro * _tile_lanes(cs, 4) + _rope_partner(dqro * _tile_lanes(sn, 4))
            dqhat = dqn * qnw_ref[...]
            dqr = qrstd * (dqhat - qhat * _group_mean(dqhat * qhat, ones_q, HEAD_DIM))
            acc_ref[1:2, :] += jnp.sum(dqn * qhat, axis=0, keepdims=True)
            dp_ref[:, 512:1024] = dqr.astype(BF16)
            dws = jnp.zeros((1, HG_DIM), F32)
            for h in range(HG_HEADS):
                sl = slice(HG_DIM * h, HG_DIM * (h + 1))
                oh, gh, dy = o_ref[:, sl], p_ref[:, sl], dy_ref[:, sl]
                rstd = lax.rsqrt(jnp.mean(oh * oh, axis=-1, keepdims=True) + EPS)
                ohat = oh * rstd
                sg = _sigmoid(gh)
                dp_ref[:, sl] = (dy * (ohat * hnw_ref[...]) * (sg * (1.0 + gh * (1.0 - sg)))).astype(BF16)
                dn = dy * (gh * sg)
                dws = dws + jnp.sum(dn * ohat, axis=0, keepdims=True)
                dohat = dn * hnw_ref[...]
                do_ref[:, sl] = rstd * (dohat - ohat * jnp.mean(dohat * ohat, axis=-1, keepdims=True))
            acc_ref[0:1, 0:HG_DIM] += dws

    return _pcall(
        body, carried, name="prep_bwd", grid=(N_TILES,),
        in_specs=[pl.BlockSpec((TM, WB), lambda i: (i, 0)),
                  pl.BlockSpec((TM, HGW), lambda i: (_lat(i), 0)),
                  pl.BlockSpec((TM, 128), lambda i: (_lat(i), 0)),
                  pl.BlockSpec((TM, 128), lambda i: (_lat(i), 0)),
                  _full((1, HG_DIM)), _full((1, ATW)), _full((1, KVW)),
                  pl.BlockSpec((TM, HGW), lambda i: (_lat(i), 0)),
                  pl.BlockSpec((TM, ATW), lambda i: (_lat(i), 0)),
                  pl.BlockSpec((TM, ATW), lambda i: (i, 0)),
                  pl.BlockSpec((TM, ATW), lambda i: (i, 0))],
        out_specs=[pl.BlockSpec((TM, WB), lambda i: (i, 0)),
                   pl.BlockSpec((TM, HGW), lambda i: (_lat(i), 0)),
                   _full((8, ATW))],
        out_shape=[jax.ShapeDtypeStruct((T, WB), BF16), jax.ShapeDtypeStruct((S, HGW), F32),
                   jax.ShapeDtypeStruct((8, ATW), F32)],
        scratch_shapes=[], operands=[p_b, o, cos, sin, hnw, qnw, knw, dy_hg, dq, dk_rep, dv_rep])


NEG = -1e30
_CTX_BLOCKS = L // BLOCK


def _attn_window_specs():
    prev = pl.BlockSpec((BLOCK, ATW), lambda i: (jnp.maximum(i - 1, 0) + _CTX_BLOCKS, 0))
    own = pl.BlockSpec((BLOCK, ATW), lambda i: (i + _CTX_BLOCKS, 0))
    nxt = pl.BlockSpec((BLOCK, ATW), lambda i: (jnp.minimum(i + 1, N_BLOCKS - 1) + _CTX_BLOCKS, 0))
    return [prev, own, nxt, _full((L, ATW))]


def _attn_valid(i, heads, context):
    n_keys = 3 * BLOCK + (L if context else 0)
    qi = lax.broadcasted_iota(jnp.int32, (heads * BLOCK, n_keys), 0) % BLOCK
    kj = lax.broadcasted_iota(jnp.int32, (heads * BLOCK, n_keys), 1)
    window = ((jnp.abs(kj - BLOCK - qi) <= BLOCK) & ((kj >= BLOCK) | (i > 0))
              & ((kj < 2 * BLOCK) | (i < N_BLOCKS - 1)))
    return window | (kj >= 3 * BLOCK)


def _stack_heads(qg):
    lane = lax.broadcasted_iota(jnp.int32, (1, 256), 1) // HEAD_DIM
    return jnp.concatenate([jnp.where(lane == g, qg, jnp.zeros_like(qg)) for g in range(4)], axis=0)


def _unstack_heads(v4):
    lane = lax.broadcasted_iota(jnp.int32, (1, 256), 1) // HEAD_DIM
    out = jnp.where(lane == 0, v4[0:BLOCK], 0.0)
    for g in range(1, 4):
        out = out + jnp.where(lane == g, v4[g * BLOCK:(g + 1) * BLOCK], 0.0)
    return out


def _sink_rows(sink_ref, hk):
    return jnp.concatenate(
        [jnp.broadcast_to(sink_ref[0:1, 4 * hk + g:4 * hk + g + 1], (BLOCK, 1)) for g in range(4)], axis=0)


def _attn_fwd(q, k_rep, v_rep, sinks, carried=None):
    def body(q_ref, kp, ko, kn, kc, vp, vo, vn, vc, sink_ref, y_ref, lse_ref):
        i = pl.program_id(0)
        valid = _attn_valid(i, 1, True)
        lane8 = lax.broadcasted_iota(jnp.int32, (1, ATT_HEADS), 1)
        head_of_lane = lax.broadcasted_iota(jnp.int32, (1, 256), 1) // HEAD_DIM
        lse_out = jnp.zeros((BLOCK, ATT_HEADS), F32)
        for hk in range(KV_HEADS):
            sl = slice(256 * hk, 256 * (hk + 1))
            qg = q_ref[:, sl]
            keys = jnp.concatenate([kp[:, sl], ko[:, sl], kn[:, sl], kc[:, sl]], axis=0)
            vals = jnp.concatenate([vp[:, sl], vo[:, sl], vn[:, sl], vc[:, sl]], axis=0)
            yg = jnp.zeros((BLOCK, 256), F32)
            for g in range(4):
                q1 = jnp.where(head_of_lane == g, qg, jnp.zeros_like(qg))
                s = jnp.where(valid, _dot_nt(q1, keys), NEG)
                sink = sink_ref[0:1, 4 * hk + g:4 * hk + g + 1]
                m = jnp.maximum(jnp.max(s, axis=1, keepdims=True), sink)
                p = jnp.exp(s - m)
                den = jnp.sum(p, axis=1, keepdims=True) + jnp.exp(sink - m)
                o1 = _dot(p.astype(BF16), vals) * (1.0 / den)
                yg = yg + jnp.where(head_of_lane == g, o1, 0.0)
                lse_out = lse_out + jnp.where(lane8 == 4 * hk + g, m + jnp.log(den), 0.0)
            y_ref[:, sl] = yg.astype(BF16)
        lse_ref[...] = lse_out

    return _pcall(
        body, carried, name="attn_fwd", grid=(N_BLOCKS,),
        in_specs=[pl.BlockSpec((BLOCK, ATW), lambda i: (i, 0))] + _attn_window_specs()
        + _attn_window_specs() + [_full((1, ATT_HEADS))],
        out_specs=[pl.BlockSpec((BLOCK, ATW), lambda i: (i, 0)),
                   pl.BlockSpec((BLOCK, ATT_HEADS), lambda i: (i, 0))],
        out_shape=[jax.ShapeDtypeStruct((S, ATW), BF16), jax.ShapeDtypeStruct((S, ATT_HEADS), F32)],
        scratch_shapes=[],
        operands=[q, k_rep, k_rep, k_rep, k_rep, v_rep, v_rep, v_rep, v_rep, sinks])


def _attn_bwd(q, k_rep, v_rep, sinks, y_at, lse, dy, carried=None):
    def body(q_ref, kp, ko, kn, kc, vp, vo, vn, vc, sink_ref, y_ref, lse_ref, dy_ref,
             dq_ref, dk_ref, dv_ref, dsink_ref, dk_acc, dv_acc):
        i = pl.program_id(0)

        @pl.when(i == 0)
        def _():
            dk_acc[...] = jnp.zeros_like(dk_acc)
            dv_acc[...] = jnp.zeros_like(dv_acc)
            dk_ref[pl.ds(0, L), :] = jnp.zeros((L, ATW), F32)
            dv_ref[pl.ds(0, L), :] = jnp.zeros((L, ATW), F32)
            dsink_ref[...] = jnp.zeros_like(dsink_ref)

        valid = _attn_valid(i, 4, False)
        lane8 = lax.broadcasted_iota(jnp.int32, (1, ATT_HEADS), 1)
        w0 = pl.multiple_of(i * BLOCK, BLOCK)
        dsink = jnp.zeros((1, ATT_HEADS), F32)
        for hk in range(KV_HEADS):
            sl = slice(256 * hk, 256 * (hk + 1))
            q4 = _stack_heads(q_ref[:, sl])
            do4f = _stack_heads(dy_ref[:, sl])
            o4 = _stack_heads(y_ref[:, sl]).astype(F32)
            do4 = do4f.astype(BF16)
            kl = jnp.concatenate([kp[:, sl], ko[:, sl], kn[:, sl]], axis=0)
            vl = jnp.concatenate([vp[:, sl], vo[:, sl], vn[:, sl]], axis=0)
            lse4 = jnp.concatenate(
                [jnp.sum(jnp.where(lane8 == 4 * hk + g, lse_ref[...], 0.0), axis=1, keepdims=True)
                 for g in range(4)], axis=0)
            p_loc = jnp.where(valid, jnp.exp(_dot_nt(q4, kl) - lse4), 0.0)
            p_ctx = jnp.exp(_dot_nt(q4, kc[:, sl]) - lse4)
            delta = jnp.sum(do4f * o4, axis=1, keepdims=True)
            ds_loc = (p_loc * (_dot_nt(do4, vl) - delta)).astype(BF16)
            ds_ctx = (p_ctx * (_dot_nt(do4, vc[:, sl]) - delta)).astype(BF16)
            dq_ref[:, sl] = _unstack_heads(_dot(ds_loc, kl) + _dot(ds_ctx, kc[:, sl]))
            dk_acc[pl.ds(w0, 3 * BLOCK), sl] += _dot_tn(ds_loc, q4)
            dv_acc[pl.ds(w0, 3 * BLOCK), sl] += _dot_tn(p_loc.astype(BF16), do4)
            dk_ref[pl.ds(0, L), sl] += _dot_tn(ds_ctx, q4)
            dv_ref[pl.ds(0, L), sl] += _dot_tn(p_ctx.astype(BF16), do4)
            p_sink = jnp.exp(_sink_rows(sink_ref, hk) - lse4)
            for g in range(4):
                rows = slice(g * BLOCK, (g + 1) * BLOCK)
                dsink = dsink + jnp.where(lane8 == 4 * hk + g,
                                          -jnp.sum(p_sink[rows] * delta[rows], axis=0, keepdims=True), 0.0)
        dsink_ref[...] += dsink

        @pl.when(i == N_BLOCKS - 1)
        def _():
            dk_ref[pl.ds(L, S), :] = dk_acc[pl.ds(BLOCK, S), :]
            dv_ref[pl.ds(L, S), :] = dv_acc[pl.ds(BLOCK, S), :]

    row_q = pl.BlockSpec((BLOCK, ATW), lambda i: (i, 0))
    return _pcall(
        body, carried, name="attn_bwd", grid=(N_BLOCKS,),
        in_specs=[row_q] + _attn_window_specs() + _attn_window_specs()
        + [_full((1, ATT_HEADS)), row_q, pl.BlockSpec((BLOCK, ATT_HEADS), lambda i: (i, 0)), row_q],
        out_specs=[row_q, _full((T, ATW)), _full((T, ATW)), _full((1, ATT_HEADS))],
        out_shape=[jax.ShapeDtypeStruct((S, ATW), F32), jax.ShapeDtypeStruct((T, ATW), F32),
                   jax.ShapeDtypeStruct((T, ATW), F32), jax.ShapeDtypeStruct((1, ATT_HEADS), F32)],
        scratch_shapes=[pltpu.VMEM((S + 2 * BLOCK, ATW), F32), pltpu.VMEM((S + 2 * BLOCK, ATW), F32)],
        operands=[q, k_rep, k_rep, k_rep, k_rep, v_rep, v_rep, v_rep, v_rep, sinks, y_at, lse, dy])


def _merge_fwd(y_hg, y_at, p_c, x, w_bh, w_ba, w_out, g1, nfw, sh2, sc2, carried=None):
    def body(yh_ref, ya_ref, g_ref, x_ref, wbh_ref, wba_ref, wo_ref, g1_ref, nfw_ref, sh_ref, sc_ref,
             mx_ref, r_ref, x1_ref, h2_ref):
        a = _dot_nt(yh_ref[...], wbh_ref[...])
        b = _dot_nt(ya_ref[...], wba_ref[...])
        mixed = (_sigmoid(g_ref[:, :D]) * a + _sigmoid(g_ref[:, D:]) * b).astype(BF16)
        r = _dot(mixed, wo_ref[...])
        x1 = x_ref[...] + g1_ref[...] * r
        mx_ref[...] = mixed
        r_ref[...] = r
        x1_ref[...] = x1
        h2_ref[...] = _rms_mod(x1, nfw_ref[...], sh_ref[...], sc_ref[...]).astype(BF16)

    row = lambda w: pl.BlockSpec((TM, w), lambda i: (i, 0))
    vec = _full((1, D))
    return _pcall(
        body, carried, name="merge_fwd", grid=(N_LAT_TILES,),
        in_specs=[row(HGW), row(ATW), row(WC), row(D), _VMEM_WHOLE, _VMEM_WHOLE, _VMEM_WHOLE,
                  vec, vec, vec, vec],
        out_specs=[row(D)] * 4,
        out_shape=[jax.ShapeDtypeStruct((S, D), dt) for dt in (BF16, F32, F32, BF16)],
        scratch_shapes=[], operands=[y_hg, y_at, p_c, x, w_bh, w_ba, w_out, g1, nfw, sh2, sc2])


def _merge_bwd(dx1, r, y_hg, y_at, p_c, w_bh, w_ba, w_out, g1, carried=None):
    def body(dx_ref, r_ref, yh_ref, ya_ref, g_ref, wbh_ref, wba_ref, wo_ref, g1_ref,
             dr_ref, da_ref, db_ref, dg_ref, dyh_ref, dya_ref, acc_ref):
        @pl.when(pl.program_id(0) == 0)
        def _():
            acc_ref[...] = jnp.zeros_like(acc_ref)

        dx1v = dx_ref[...]
        acc_ref[0:1, :] += jnp.sum(dx1v * r_ref[...], axis=0, keepdims=True)
        dr = (g1_ref[...] * dx1v).astype(BF16)
        dr_ref[...] = dr
        dmix = _dot_nt(dr, wo_ref[...])
        sh, sa = _sigmoid(g_ref[:, :D]), _sigmoid(g_ref[:, D:])
        da = (dmix * sh).astype(BF16)
        db = (dmix * sa).astype(BF16)
        da_ref[...] = da
        db_ref[...] = db
        dg_ref[:, :D] = (dmix * _dot_nt(yh_ref[...], wbh_ref[...]) * sh * (1.0 - sh)).astype(BF16)
        dg_ref[:, D:] = (dmix * _dot_nt(ya_ref[...], wba_ref[...]) * sa * (1.0 - sa)).astype(BF16)
        dyh_ref[...] = _dot(da, wbh_ref[...])
        dya_ref[...] = _dot(db, wba_ref[...])

    row = lambda w: pl.BlockSpec((TM, w), lambda i: (i, 0))
    return _pcall(
        body, carried, name="merge_bwd", grid=(N_LAT_TILES,),
        in_specs=[row(D), row(D), row(HGW), row(ATW), row(WC), _VMEM_WHOLE, _VMEM_WHOLE, _VMEM_WHOLE,
                  _full((1, D))],
        out_specs=[row(D), row(D), row(D), row(WC), row(HGW), row(ATW), _full((8, D))],
        out_shape=[jax.ShapeDtypeStruct((S, D), BF16), jax.ShapeDtypeStruct((S, D), BF16),
                   jax.ShapeDtypeStruct((S, D), BF16), jax.ShapeDtypeStruct((S, WC), BF16),
                   jax.ShapeDtypeStruct((S, HGW), F32), jax.ShapeDtypeStruct((S, ATW), F32),
                   jax.ShapeDtypeStruct((8, D), F32)],
        scratch_shapes=[], operands=[dx1, r, y_hg, y_at, p_c, w_bh, w_ba, w_out, g1])


def _ffn_fused(x1, h2, tgt, w_gate, w_up, w_down, g2, nfw, sc2):
    def body(x1_ref, h2_ref, t_ref, wg_ref, wu_ref, wd_ref, g2_ref, nfw_ref, sc_ref,
             act_ref, dgt_ref, dup_ref, df_ref, dx_ref, acc_ref, gs, us):
        @pl.when(pl.program_id(0) == 0)
        def _():
            acc_ref[...] = jnp.zeros_like(acc_ref)

        h2 = h2_ref[...]
        whole = lambda w_ref: w_ref[...].reshape(D_FF, D)
        wide = lambda t_ref: jnp.concatenate([t_ref[j] for j in range(N_FF_TILES)], axis=1)
        for j in range(N_FF_TILES):
            g = _dot_nt(h2, wg_ref[j])
            u = _dot_nt(h2, wu_ref[j])
            gs[j] = g
            us[j] = u
            act_ref[j] = (g * _sigmoid(g) * u).astype(BF16)
        f = _dot(wide(act_ref), whole(wd_ref))
        x1v = x1_ref[...]
        g2 = g2_ref[...]
        diff = x1v + g2 * f - t_ref[...]
        dy = diff * (1.0 / D)
        df = (g2 * dy).astype(BF16)
        df_ref[...] = df
        dact_all = _dot_nt(df, whole(wd_ref))
        for j in range(N_FF_TILES):
            g, u = gs[j], us[j]
            sg = _sigmoid(g)
            dact = dact_all[:, j * FF_TILE:(j + 1) * FF_TILE]
            dgt_ref[j] = (dact * u * (sg * (1.0 + g * (1.0 - sg)))).astype(BF16)
            dup_ref[j] = (dact * (g * sg)).astype(BF16)
        dh2 = _dot(wide(dgt_ref), whole(wg_ref)) + _dot(wide(dup_ref), whole(wu_ref))
        dx, dsh, dsc, dnw = _rms_mod_bwd(x1v, nfw_ref[...], sc_ref[...], dh2)
        dx_ref[...] = dy + dx
        acc_ref[0:1, :] += dsh
        acc_ref[1:2, :] += dsc
        acc_ref[2:3, :] += dnw
        acc_ref[3:4, :] += jnp.sum(dy * f, axis=0, keepdims=True)
        acc_ref[4:5, :] += 0.5 * jnp.sum(jnp.sum(diff * diff, axis=1, keepdims=True), axis=0,
                                         keepdims=True) * (1.0 / D)

    row = lambda dt_w: pl.BlockSpec((TM, dt_w), lambda i: (i, 0))
    blk = pl.BlockSpec((N_FF_TILES, TM, FF_TILE), lambda i: (0, i, 0))
    vec = _full((1, D))
    return pl.pallas_call(
        body, name="ffn_fused", grid=(N_LAT_TILES,),
        in_specs=[row(D), row(D), row(D), _VMEM_WHOLE, _VMEM_WHOLE, _VMEM_WHOLE, vec, vec, vec],
        out_specs=[blk, blk, blk, row(D), row(D), _full((8, D))],
        out_shape=[jax.ShapeDtypeStruct((N_FF_TILES, S, FF_TILE), BF16)] * 3
        + [jax.ShapeDtypeStruct((S, D), BF16), jax.ShapeDtypeStruct((S, D), F32),
           jax.ShapeDtypeStruct((8, D), F32)],
        scratch_shapes=[pltpu.VMEM((N_FF_TILES, TM, FF_TILE), F32), pltpu.VMEM((N_FF_TILES, TM, FF_TILE), F32)],
        compiler_params=_cp(("arbitrary",)),
    )(x1, h2, tgt, w_gate, w_up, w_down, g2, nfw, sc2)


def _proj_bc(h_all, w_b, w_c, carried=None):
    def body(h_ref, wb_ref, wc_ref, pb_ref, pc_ref):
        h = h_ref[...]
        pb_ref[...] = _dot_nt(h, wb_ref[...])

        @pl.when(pl.program_id(0) > 0)
        def _():
            pc_ref[...] = _dot_nt(h, wc_ref[...])

    return _pcall(
        body, carried, name="proj_bc", grid=(N_TILES,),
        in_specs=[pl.BlockSpec((TM, D), lambda i: (i, 0)), _VMEM_WHOLE, _VMEM_WHOLE],
        out_specs=[pl.BlockSpec((TM, WB), lambda i: (i, 0)), pl.BlockSpec((TM, WC), lambda i: (_lat(i), 0))],
        out_shape=[jax.ShapeDtypeStruct((T, WB), F32), jax.ShapeDtypeStruct((S, WC), F32)],
        scratch_shapes=[], operands=[h_all, w_b, w_c])


def _input_bwd(dp_a, dp_b, dp_c, w_a, w_b, w_c, ctx, x, dx1, nw, sh, sc, carried=None):
    def body(da_ref, db_ref, dc_ref, wa_ref, wb_ref, wc_ref, ctx_ref, x_ref, dx1_ref, nw_ref, sh_ref,
             sc_ref, gx_ref, acc_ref):
        i = pl.program_id(0)

        @pl.when(i == 0)
        def _():
            acc_ref[...] = jnp.zeros_like(acc_ref)

        dh = _dot(da_ref[...], wa_ref[...]) + _dot(db_ref[...], wb_ref[...])

        @pl.when(i == 0)
        def _():
            _, dsh, dsc, dnw = _rms_mod_bwd(ctx_ref[...], nw_ref[...], sc_ref[0:1, :], dh)
            acc_ref[3:4, :] += dsh
            acc_ref[4:5, :] += dsc
            acc_ref[2:3, :] += dnw

        @pl.when(i > 0)
        def _():
            dhl = dh + _dot(dc_ref[...], wc_ref[...])
            dx, dsh, dsc, dnw = _rms_mod_bwd(x_ref[...], nw_ref[...], sc_ref[1:2, :], dhl)
            gx_ref[...] = dx1_ref[...] + dx
            acc_ref[0:1, :] += dsh
            acc_ref[1:2, :] += dsc
            acc_ref[2:3, :] += dnw

    lat = lambda w: pl.BlockSpec((TM, w), lambda i: (_lat(i), 0))
    return _pcall(
        body, carried, name="input_bwd", grid=(N_TILES,),
        in_specs=[pl.BlockSpec((TM, WA), lambda i: (i, 0)), pl.BlockSpec((TM, WB), lambda i: (i, 0)),
                  lat(WC), _VMEM_WHOLE, _VMEM_WHOLE, _VMEM_WHOLE, _full((TM, D)), lat(D), lat(D),
                  _full((1, D)), _full((2, D)), _full((2, D))],
        out_specs=[lat(D), _full((8, D))],
        out_shape=[jax.ShapeDtypeStruct((S, D), F32), jax.ShapeDtypeStruct((8, D), F32)],
        scratch_shapes=[], operands=[dp_a, dp_b, dp_c, w_a, w_b, w_c, ctx, x, dx1, nw, sh, sc])


_C1 = 1.0 - ADAM_B1 ** ADAM_STEP
_C2 = 1.0 - ADAM_B2 ** ADAM_STEP


def _adamw_math(w, g, m, v):
    m = ADAM_B1 * m + (1.0 - ADAM_B1) * g
    v = ADAM_B2 * v + (1.0 - ADAM_B2) * (g * g)
    m_hat = m / _C1
    v_hat = v / _C2
    delta = -ADAM_LR * (m_hat / (jnp.sqrt(v_hat) + ADAM_EPS) + ADAM_WD * w)
    return delta, m, v


def _adamw_sharded(terms, w, m, v, name, tr, extra=None):
    rows, cols = w.shape

    def body(*refs):
        t_ref, w_ref, m_ref, v_ref = refs[:4]
        g_ref, d_ref, nm_ref, nv_ref = refs[-4:]
        g = t_ref[0].astype(F32)
        for s in range(1, N_CHIPS):
            g = g + t_ref[s].astype(F32)
        if extra is not None:
            g = g + refs[4][...].astype(F32)
        g_ref[...] = g
        d_ref[...], nm_ref[...], nv_ref[...] = _adamw_math(w_ref[...], g, m_ref[...], v_ref[...])

    blk = pl.BlockSpec((tr, cols), lambda i: (i, 0))
    return pl.pallas_call(
        body, name=name, grid=(rows // tr,),
        in_specs=[pl.BlockSpec((N_CHIPS, tr, cols), lambda i: (0, i, 0)), blk, blk, blk]
        + ([blk] if extra is not None else []),
        out_specs=[blk] * 4,
        out_shape=[jax.ShapeDtypeStruct((rows, cols), F32)] * 4,
        compiler_params=_cp(("parallel",)),
    )(terms, w, m, v, *([extra] if extra is not None else []))


def _adamw_plain(g, w, m, v, name):
    def body(g_ref, w_ref, m_ref, v_ref, d_ref, nm_ref, nv_ref):
        d_ref[...], nm_ref[...], nv_ref[...] = _adamw_math(w_ref[...], g_ref[...], m_ref[...], v_ref[...])

    return pl.pallas_call(
        body, name=name, in_specs=[_VMEM_WHOLE] * 4, out_specs=[_VMEM_WHOLE] * 3,
        out_shape=[jax.ShapeDtypeStruct(w.shape, F32)] * 3,
        compiler_params=_cp(),
    )(g, w, m, v)


SMALL_ROWS = 16
R_DMOD, R_DCTX, R_NMIX, R_NFFN, R_MISC, R_DLB, R_BADA01 = 0, 6, 8, 9, 10, 11, 13
M_HNW, M_QNW, M_KNW, M_SINK, M_LOSS = 0, 128, 256, 384, 512


def _pack_small(acc_in, acc_mg, acc_ffn, acc_prep, dsink, dlb):
    def body(in_ref, mg_ref, ff_ref, pp_ref, ds_ref, dlb_ref, o_ref):
        o_ref[...] = jnp.zeros_like(o_ref)
        o_ref[0:2, :] = in_ref[0:2, :]
        o_ref[2:3, :] = mg_ref[0:1, :]
        o_ref[3:5, :] = ff_ref[0:2, :]
        o_ref[5:6, :] = ff_ref[3:4, :]
        o_ref[6:8, :] = in_ref[3:5, :]
        o_ref[8:9, :] = in_ref[2:3, :]
        o_ref[9:10, :] = ff_ref[2:3, :]
        o_ref[10:11, M_HNW:M_HNW + HG_DIM] = pp_ref[0:1, 0:HG_DIM]
        r = lax.broadcasted_iota(jnp.int32, (ATW, 128), 0)
        c = lax.broadcasted_iota(jnp.int32, (ATW, 128), 1)
        fold = jnp.where((r % HEAD_DIM == c) & (c < HEAD_DIM), 1.0, 0.0).astype(BF16)
        qk = jnp.concatenate([pp_ref[1:2, :], pp_ref[2:3, :], jnp.zeros((6, ATW), F32)], axis=0)
        folded = _dot_exact_rhs01(qk, fold)
        o_ref[10:11, M_QNW:M_QNW + 128] = folded[0:1, :]
        o_ref[10:11, M_KNW:M_KNW + 128] = folded[1:2, :]
        o_ref[10:11, M_SINK:M_SINK + ATT_HEADS] = ds_ref[...]
        o_ref[10:11, M_LOSS:M_LOSS + 128] = ff_ref[4:5, 0:128]
        o_ref[11:13, 0:HGW] = dlb_ref[...]

    return pl.pallas_call(
        body, name="pack_small", in_specs=[_VMEM_WHOLE] * 6, out_specs=_VMEM_WHOLE,
        out_shape=jax.ShapeDtypeStruct((SMALL_ROWS, D), F32), compiler_params=_cp(),
    )(acc_in, acc_mg, acc_ffn, acc_prep, dsink, dlb)


def _sum_small(gathered):
    def body(g_ref, o_ref):
        tot = g_ref[0]
        for s in range(1, N_DEV):
            tot = tot + g_ref[s]
        o_ref[...] = tot
        o_ref[R_BADA01:R_BADA01 + 2, :] = tot[0:2, :] + tot[R_DCTX:R_DCTX + 2, :]

    return pl.pallas_call(
        body, name="sum_small", in_specs=[_VMEM_WHOLE], out_specs=_VMEM_WHOLE,
        out_shape=jax.ShapeDtypeStruct((SMALL_ROWS, D), F32), compiler_params=_cp(),
    )(gathered)


_REP_NAMES = ("b_ada", "c_ctx", "norm_mix_w", "norm_ffn_w", "hgrn_norm_w", "q_norm_w", "k_norm_w", "attn_sinks")


def _adamw_replicated(tot, g_c_ctx, ws, ms, vs):
    n = len(_REP_NAMES)

    def body(*refs):
        tot_ref, gc_ref = refs[0], refs[1]
        w_refs, m_refs, v_refs = refs[2:2 + n], refs[2 + n:2 + 2 * n], refs[2 + 2 * n:2 + 3 * n]
        outs = refs[2 + 3 * n:]
        row = lambda r: tot_ref[r:r + 1, :]
        misc = row(R_MISC)
        grads = [jnp.concatenate([row(R_BADA01), row(R_BADA01 + 1)] + [row(k) for k in range(2, 6)], axis=1),
                 gc_ref[...], row(R_NMIX), row(R_NFFN),
                 misc[:, M_HNW:M_HNW + HG_DIM], misc[:, M_QNW:M_QNW + HEAD_DIM],
                 misc[:, M_KNW:M_KNW + HEAD_DIM], misc[:, M_SINK:M_SINK + ATT_HEADS]]
        for k in range(n):
            outs[k][...] = grads[k]
            outs[n + k][...], outs[2 * n + k][...], outs[3 * n + k][...] = _adamw_math(
                w_refs[k][...], grads[k], m_refs[k][...], v_refs[k][...])

    shapes = [jax.ShapeDtypeStruct(w.shape, F32) for w in ws]
    return pl.pallas_call(
        body, name="adamw_replicated", in_specs=[_VMEM_WHOLE] * (2 + 3 * n), out_specs=[_VMEM_WHOLE] * (4 * n),
        out_shape=shapes * 4, compiler_params=_cp(),
    )(tot, g_c_ctx, *ws, *ms, *vs)


def _lb_grads(dlb, lbl):
    def body(d_ref, l_ref, o_ref):
        for d in (0, 1):
            ll = l_ref[d]
            lb = _sigmoid(ll[0:1, :] - ll[1:2, :])
            t = d_ref[d:d + 1, :] * lb * (1.0 - lb)
            o_ref[d, 0:1, :] = t
            o_ref[d, 1:2, :] = -t

    return pl.pallas_call(
        body, name="lb_grads", in_specs=[_VMEM_WHOLE] * 2, out_specs=_VMEM_WHOLE,
        out_shape=jax.ShapeDtypeStruct((2, 2, HGW), F32), compiler_params=_cp(),
    )(dlb, lbl)


def _c_ctx_grad(terms, c_ctx):
    def body(t_ref, c_ref, o_ref):
        tot = t_ref[0, 8:9, :]
        for s in range(1, N_DEV):
            tot = tot + t_ref[s, 8:9, :]
        cv = c_ref[...]
        sg = _sigmoid(cv)
        o_ref[...] = tot * (sg * (1.0 + cv * (1.0 - sg)))

    return pl.pallas_call(
        body, name="c_ctx_grad", in_specs=[_VMEM_WHOLE] * 2, out_specs=_VMEM_WHOLE,
        out_shape=jax.ShapeDtypeStruct((1, D), F32), compiler_params=_cp(),
    )(terms, c_ctx)


def _in_perm():
    fz, bz, inp, kk, vv, qhg, ghg, qat, gates = 0, 512, 1024, 1536, 1664, 1792, 2304, 2816, 3328
    cols = []
    for h in range(HG_HEADS):
        for base in (fz, bz, inp, qhg):
            cols += list(range(base + 128 * h, base + 128 * (h + 1)))
    cols += list(range(ghg, ghg + 512)) + list(range(qat, qat + 512))
    cols += list(range(kk, kk + 128)) + list(range(vv, vv + 128))
    cols += list(range(gates, gates + 2048))
    return np.asarray(cols, np.int32)


_PERM = _in_perm()


_PIECES = {"a": (0, WA, 128), "b": (WA, WB, 256), "c": (WA + WB, WC, 256)}


def _block_table(piece):
    lo, n, blk = _PIECES[piece]
    starts = [int(_PERM[r]) for r in range(lo, lo + n, blk)]
    assert all(s % blk == 0 and np.array_equal(_PERM[r:r + blk], np.arange(s, s + blk))
               for s, r in zip(starts, range(lo, lo + n, blk)))
    return jnp.asarray([s // blk for s in starts], jnp.int32), blk


def _pick_row_blocks(x, table, blk, name):
    cols = x.shape[1]

    def body(t_ref, x_ref, o_ref):
        o_ref[...] = x_ref[...]

    return pl.pallas_call(
        body, name=name,
        grid_spec=pltpu.PrefetchScalarGridSpec(
            num_scalar_prefetch=1, grid=(table.shape[0],),
            in_specs=[pl.BlockSpec((blk, cols), lambda i, t: (t[i], 0))],
            out_specs=pl.BlockSpec((blk, cols), lambda i, t: (i, 0))),
        out_shape=jax.ShapeDtypeStruct((table.shape[0] * blk, cols), x.dtype),
        compiler_params=_cp(("arbitrary",)),
    )(table, x)


def _place_row_blocks(x, table, blk, into, out_rows, name):
    cols = x.shape[1]

    def body(t_ref, x_ref, *rest):
        rest[-1][...] = x_ref[...]

    operands, in_specs, aliases = [table, x], [pl.BlockSpec((blk, cols), lambda i, t: (i, 0))], {}
    if into is not None:
        operands.append(into)
        in_specs.append(_ANY)
        aliases = {2: 0}
    return pl.pallas_call(
        body, name=name,
        grid_spec=pltpu.PrefetchScalarGridSpec(
            num_scalar_prefetch=1, grid=(table.shape[0],), in_specs=in_specs,
            out_specs=pl.BlockSpec((blk, cols), lambda i, t: (t[i], 0))),
        out_shape=jax.ShapeDtypeStruct((out_rows, cols), x.dtype),
        input_output_aliases=aliases,
        compiler_params=_cp(("arbitrary",)),
    )(*operands)


def _local_step(x2, ctx2, h_all, h_lat, tgt, lbl, sh_in, sc_in, gate1, sh2, sc2, gate2, norm_mix_w, norm_ffn_w,
                hgrn_norm_w, q_norm_w, k_norm_w, attn_sinks, w_a, w_b, w_c, s_bh, s_ba, s_out,
                s_gate, s_up, s_down):
    first_last = lambda n: [(0, True), (n - 1, False)]
    p_a = _mm_nt(h_all, w_a, tm=768, tn=1024, out_dtype=F32, name="proj_a")
    (o, st), (g_gate, g_bh, g_ba) = _hgrn_fwd(
        p_a, lbl, (_gather_comm_relayed([s_gate, s_bh, s_ba]),
                   [(0, True), (HG_HEADS - 2, True), (HG_HEADS - 1, False)]))
    (p_b, p_c), (g_out,) = _proj_bc(
        h_all, w_b, w_c, (_gather_comm_relayed([s_out]), [(0, True), (N_TILES - 4, True), (N_TILES - 1, False)]))
    cos, sin = _rope_tables()
    qnw_t, knw_t = jnp.tile(q_norm_w, (1, ATT_HEADS)), jnp.tile(k_norm_w, (1, KV_HEADS))
    y_hg, qn, k_rep, v_rep = _prep_fwd(p_b, o, cos, sin, hgrn_norm_w, qnw_t, knw_t)
    (y_at, lse), (g_up, g_down) = _attn_fwd(
        qn, k_rep, v_rep, attn_sinks,
        (_gather_comm_relayed([s_up, s_down]), [(0, True), (N_BLOCKS - 6, True), (N_BLOCKS - 1, False)]))
    w_bh, w_ba, w_o = g_bh.reshape(D, HGW), g_ba.reshape(D, ATW), g_out.reshape(D, D)
    (mixed, r, x1, h2), _ = _merge_fwd(
        y_hg, y_at, p_c, x2, w_bh, w_ba, w_o, gate1, norm_ffn_w, sh2, sc2)
    g_gate, g_up, g_down = [g.reshape(N_FF_TILES, FF_TILE, D) for g in (g_gate, g_up, g_down)]

    act, d_gate, d_up, d_f, dx1, acc_ffn = _ffn_fused(x1, h2, tgt, g_gate, g_up, g_down, gate2,
                                                      norm_ffn_w, sc2)
    by_chip = lambda t: t.reshape((N_CHIPS, 2) + t.shape[1:])
    ff_by_chip = lambda t: t.reshape(N_CHIPS, 2, FF_BLK, D)
    t_down, _ = _mm_tn_blocked(act, d_f, "grad_down")
    t_down = ff_by_chip(t_down)
    t_gate, (f_down,) = _mm_tn_blocked(d_gate, h2, "grad_gate", (_sibling_comm([t_down]), first_last(N_FF_TILES)))
    t_gate = ff_by_chip(t_gate)
    t_up, (f_gate,) = _mm_tn_blocked(d_up, h2, "grad_up", (_sibling_comm([t_gate]), first_last(N_FF_TILES)))
    t_up = ff_by_chip(t_up)

    (d_r, d_a, d_b, dp_c, dy_hg, dy_at, acc_mg), (f_up,) = _merge_bwd(
        dx1, r, y_hg, y_at, p_c, w_bh, w_ba, w_o, gate1, (_sibling_comm([t_up]), first_last(N_LAT_TILES)))
    c_down, c_gate, c_up = [_pair_sum(t, f, "pair_sum_" + nm) for t, f, nm in
                            ((t_down, f_down, "down"), (t_gate, f_gate, "gate"), (t_up, f_up, "up"))]
    t_out = _mm_tn(mixed, d_r, tk=1024, nk=2, tm=1024, tn=1024, out_dtype=BF16, name="grad_out")
    t_bh = _mm_tn(d_a, y_hg, tk=2048, nk=1, tm=1024, tn=512, out_dtype=BF16, name="grad_bh")
    t_ba = _mm_tn(d_b, y_at, tk=2048, nk=1, tm=1024, tn=512, out_dtype=BF16, name="grad_ba")
    t_bh, t_ba, t_out = [by_chip(t.reshape(N_DEV, D // N_DEV, t.shape[1])) for t in (t_bh, t_ba, t_out)]
    (dq, dk_rep, dv_rep, dsink), (r_up,) = _attn_bwd(
        qn, k_rep, v_rep, attn_sinks, y_at, lse, dy_at, (_chip_comm([c_up]), first_last(N_BLOCKS)))
    (dp_b, d_o, acc_prep), (f_bh, f_ba, f_out) = _prep_bwd(
        p_b, o, cos, sin, hgrn_norm_w, qnw_t, knw_t, dy_hg, dq, dk_rep, dv_rep,
        (_sibling_comm([t_bh, t_ba, t_out]), first_last(N_TILES)))
    c_bh, c_ba, c_out = [_pair_sum(t, f, "pair_sum_" + nm) for t, f, nm in
                         ((t_bh, f_bh, "bh"), (t_ba, f_ba, "ba"), (t_out, f_out, "out"))]
    (dp_a, dlb), (r_bh, r_ba, r_out, r_down, r_gate) = _hgrn_bwd(
        p_a, lbl, d_o, st, (_chip_comm([c_bh, c_ba, c_out, c_down, c_gate]), first_last(HG_HEADS)))
    t_a = _mm_tn(dp_a, h_all, tk=768, nk=3, tm=1024, tn=1024, out_dtype=BF16, name="grad_in_a")
    t_b = _mm_tn(dp_b, h_all, tk=768, nk=3, tm=640, tn=1024, out_dtype=BF16, name="grad_in_b")
    t_c = _mm_tn(dp_c, h_lat, tk=1024, nk=2, tm=1024, tn=1024, out_dtype=BF16, name="grad_in_c")
    t_in = None
    for piece, nm in ((t_a, "a"), (t_b, "b"), (t_c, "c")):
        t_in = _place_row_blocks(piece, *_block_table(nm), t_in, IN_COLS, "order_terms_" + nm)
    t_in = by_chip(t_in.reshape(N_DEV, IN_BLK, D))
    (f_in,) = _run_comm(_sibling_comm([t_in]), "scatter_in_sibling")
    c_in = _pair_sum(t_in, f_in, "pair_sum_in")
    sems, c_in, land, token = _chip_exchange_start(c_in, jnp.zeros(c_in.shape, c_in.dtype))
    (grad_x, acc_in), _ = _input_bwd(dp_a, dp_b, dp_c, w_a, w_b, w_c, ctx2, x2, dx1,
                                     norm_mix_w + token[0, 0], sh_in, sc_in)
    small = _pack_small(acc_in, acc_mg, acc_ffn, acc_prep, dsink, dlb)
    return grad_x, small, [r_bh, r_ba, r_out, r_gate, r_up, r_down], (sems, c_in, land)


def kernel(x, c, ctx, c_ctx, w_ada, b_ada, norm_mix_w, norm_ffn_w, w_in, hgrn_lb_logits, hgrn_norm_w, q_norm_w, k_norm_w, attn_sinks, w_branch_hgrn, w_branch_attn, w_out, w_ffn_gate, w_ffn_up, w_ffn_down, loss_target, m_c_ctx, m_w_ada, m_b_ada, m_norm_mix_w, m_norm_ffn_w, m_w_in, m_hgrn_lb_logits, m_hgrn_norm_w, m_q_norm_w, m_k_norm_w, m_attn_sinks, m_w_branch_hgrn, m_w_branch_attn, m_w_out, m_w_ffn_gate, m_w_ffn_up, m_w_ffn_down, v_c_ctx, v_w_ada, v_b_ada, v_norm_mix_w, v_norm_ffn_w, v_w_in, v_hgrn_lb_logits, v_hgrn_norm_w, v_q_norm_w, v_k_norm_w, v_attn_sinks, v_w_branch_hgrn, v_w_branch_attn, v_w_out, v_w_ffn_gate, v_w_ffn_up, v_w_ffn_down):
    me = 4 * lax.axis_index("x") + 2 * lax.axis_index("y") + lax.axis_index("c")
    x2, ctx2, tgt = x[0], ctx[0], loss_target[0]
    w_ada2, w_in2 = w_ada[0], w_in[0]

    cond = jnp.zeros((8, D), F32).at[0].set(c[0]).at[1, :256].set(hgrn_lb_logits.reshape(256))
    b_cols = lax.dynamic_slice(b_ada, (0, me * ADA_BLK), (1, ADA_BLK))
    g0, cc, mod, g_in, h_all, h_lat = _prologue(cond, c_ctx.reshape(1, D), w_ada2, b_cols, w_in2.T.astype(BF16),
                                         x2, ctx2, norm_mix_w)
    lbl = jnp.transpose(g0[:, 1, :256].reshape(N_DEV, 2, 2, 64), (1, 2, 0, 3)).reshape(2, 2, HGW)
    sh1, sc1, gate1, sh2, sc2, gate2 = [mod[k:k + 1] for k in range(6)]
    sh_in = jnp.concatenate([mod[6:7], sh1], axis=0)
    sc_in = jnp.concatenate([mod[7:8], sc1], axis=0)

    shards = [w_branch_hgrn[0].T, w_branch_attn[0].T, w_out[0], w_ffn_gate[0].T, w_ffn_up[0].T, w_ffn_down[0]]
    w_in_t = g_in.reshape(IN_COLS, D)
    w_a, w_b, w_c = [_pick_row_blocks(w_in_t, *_block_table(nm), "order_w_" + nm) for nm in "abc"]

    grad_x, small, (r_bh, r_ba, r_out, r_gate, r_up, r_down), pending_in = _local_step(
        x2, ctx2, h_all, h_lat, tgt, lbl, sh_in, sc_in, gate1, sh2, sc2, gate2, norm_mix_w, norm_ffn_w, hgrn_norm_w,
        q_norm_w, k_norm_w, attn_sinks, w_a, w_b, w_c, *[s.astype(BF16) for s in shards])

    big = {}
    for nm, rr, ww, mm, vv, tr, transposed in (
            ("w_branch_hgrn", r_bh, w_branch_hgrn[0], m_w_branch_hgrn[0], v_w_branch_hgrn[0], 128, True),
            ("w_branch_attn", r_ba, w_branch_attn[0], m_w_branch_attn[0], v_w_branch_attn[0], 128, True),
            ("w_out", r_out, w_out[0], m_w_out[0], v_w_out[0], 128, False),
            ("w_ffn_gate", r_gate, w_ffn_gate[0], m_w_ffn_gate[0], v_w_ffn_gate[0], 352, True),
            ("w_ffn_up", r_up, w_ffn_up[0], m_w_ffn_up[0], v_w_ffn_up[0], 352, True),
            ("w_ffn_down", r_down, w_ffn_down[0], m_w_ffn_down[0], v_w_ffn_down[0], 352, False)):
        if transposed:
            res = _adamw_sharded(rr, ww.T, mm.T, vv.T, "adamw_" + nm, tr)
            big[nm] = [t.T[None] for t in res]
        else:
            big[nm] = [t[None] for t in _adamw_sharded(rr, ww, mm, vv, "adamw_" + nm, tr)]

    (g2,) = _all_gather([small], "gather_small", True)
    tot = _sum_small(g2)
    dm = jnp.zeros((16, 6 * D), F32).at[:8].set(g2[:, R_DMOD:R_DMOD + 6, :].reshape(N_DEV, 6 * D))
    dm = dm.at[8, :2 * D].set(tot[R_DCTX:R_DCTX + 2].reshape(2 * D))
    dm_cols = lax.dynamic_slice(dm, (0, me * ADA_BLK), (16, ADA_BLK))
    g_w_ada, dsc_term = _ada_grads(cc, dm_cols, w_ada2)
    (g3,) = _all_gather([dsc_term], "gather_cctx", True)
    g_c_ctx = _c_ctx_grad(g3, c_ctx.reshape(1, D))
    g_lbl = _lb_grads(tot[R_DLB:R_DLB + 2, :HGW], lbl)
    g_lb_mine = lax.dynamic_slice(g_lbl, (0, 0, me * 64), (2, 2, 64))
    misc = tot[R_MISC]
    loss = misc[M_LOSS]

    rep_out = _adamw_replicated(
        tot, g_c_ctx,
        [b_ada, c_ctx.reshape(1, D), norm_mix_w, norm_ffn_w, hgrn_norm_w, q_norm_w, k_norm_w, attn_sinks],
        [m_b_ada, m_c_ctx.reshape(1, D), m_norm_mix_w, m_norm_ffn_w, m_hgrn_norm_w, m_q_norm_w, m_k_norm_w,
         m_attn_sinks],
        [v_b_ada, v_c_ctx.reshape(1, D), v_norm_mix_w, v_norm_ffn_w, v_hgrn_norm_w, v_q_norm_w, v_k_norm_w,
         v_attn_sinks])
    rep = []
    for kind in range(4):
        vals = dict(zip(_REP_NAMES, rep_out[kind * len(_REP_NAMES):(kind + 1) * len(_REP_NAMES)]))
        vals["c_ctx"] = vals["c_ctx"].reshape(D)
        rep.append(vals)

    sems, c_in, land = pending_in
    d_ada, nm_ada, nv_ada = _adamw_plain(g_w_ada, w_ada2, m_w_ada[0], v_w_ada[0], "adamw_w_ada")
    land = _chip_exchange_wait(sems, c_in, land, d_ada)
    own = lax.dynamic_index_in_dim(c_in, 2 * lax.axis_index("x") + lax.axis_index("y"), 0, keepdims=False)
    big["w_in"] = [t.T[None] for t in _adamw_sharded(land, w_in2.T, m_w_in[0].T, v_w_in[0].T, "adamw_w_in", 336,
                                                     extra=own)]
    ada = [t[None] for t in (g_w_ada, d_ada, nm_ada, nv_ada)]
    lb_w = hgrn_lb_logits.reshape(4, 64)
    d_lb, nm_lb, nv_lb = _adamw_plain(g_lb_mine.reshape(4, 64), lb_w, m_hgrn_lb_logits.reshape(4, 64),
                                      v_hgrn_lb_logits.reshape(4, 64), "adamw_lb")
    lbs = [t.reshape(2, 2, 64) for t in (g_lb_mine, d_lb, nm_lb, nv_lb)]

    names = ['c_ctx', 'w_ada', 'b_ada', 'norm_mix_w', 'norm_ffn_w', 'w_in', 'hgrn_lb_logits', 'hgrn_norm_w',
             'q_norm_w', 'k_norm_w', 'attn_sinks', 'w_branch_hgrn', 'w_branch_attn', 'w_out', 'w_ffn_gate',
             'w_ffn_up', 'w_ffn_down']
    outs = [loss, grad_x[None]]
    for kind in range(4):
        for nm in names:
            if nm == 'w_ada':
                outs.append(ada[kind])
            elif nm == 'hgrn_lb_logits':
                outs.append(lbs[kind])
            elif nm in big:
                outs.append(big[nm][kind])
            else:
                outs.append(rep[kind][nm])
    return tuple(outs)
```

```python
import functools
import math

import numpy as np
import jax
import jax.numpy as jnp
from jax import lax
from jax.experimental import pallas as pl
from jax.experimental.pallas import tpu as pltpu

F32 = jnp.float32
BF16 = jnp.bfloat16

N_DEV = 8
D = 1024
S = 2048
L = 256
T = L + S
TM = 256
N_TILES = T // TM
N_LAT_TILES = S // TM
HG_HEADS = 4
HG_DIM = 128
HGW = 512
CHUNK = 32
N_CHUNKS = T // CHUNK
N_CTX_CHUNKS = L // CHUNK
ATT_HEADS = 8
KV_HEADS = 2
HEAD_DIM = 64
ATW = 512
KVW = 128
BLOCK = 128
N_BLOCKS = S // BLOCK
GRID_W = 64
ROPE_THETA = 10000.0
D_FF = 2816
FF_BLK = D_FF // N_DEV
FF_TILE = 256
N_FF_TILES = D_FF // FF_TILE
IN_COLS = 5376
IN_BLK = IN_COLS // N_DEV
ADA_BLK = 6 * D // N_DEV
EPS = 1e-6
WA, WB, WC = 2048, 1280, 2048

ADAM_LR = 0.001
ADAM_B1 = 0.9
ADAM_B2 = 0.999
ADAM_EPS = 1e-08
ADAM_WD = 0.01
ADAM_STEP = 10

VMEM_LIMIT = 56 * 1024 * 1024
MESH = pl.DeviceIdType.MESH


def _cp(sem=None, vmem=VMEM_LIMIT):
    return pltpu.CompilerParams(dimension_semantics=sem, vmem_limit_bytes=vmem)


def _full(shape):
    n = len(shape)
    return pl.BlockSpec(shape, lambda *_: (0,) * n)


_VMEM_WHOLE = pl.BlockSpec(memory_space=pltpu.VMEM)
_ANY = pl.BlockSpec(memory_space=pl.ANY)


def _sigmoid(v):
    return 1.0 / (1.0 + jnp.exp(-v))


def _dot(a, b):
    return jnp.dot(a, b, preferred_element_type=F32)


def _dot_nt(a, b):
    return lax.dot_general(a, b, (((1,), (1,)), ((), ())), preferred_element_type=F32)


def _dot_tn(a, b):
    return lax.dot_general(a, b, (((0,), (0,)), ((), ())), preferred_element_type=F32)


def _split3(v):
    hi = v.astype(BF16)
    r = v - hi.astype(F32)
    mid = r.astype(BF16)
    lo = (r - mid.astype(F32)).astype(BF16)
    return hi, mid, lo


def _dot_exact_rhs01(v, m01):
    hi, mid, lo = _split3(v)
    return _dot(hi, m01) + _dot(mid, m01) + _dot(lo, m01)


def _split2(v):
    hi = v.astype(BF16)
    return hi, (v - hi.astype(F32)).astype(BF16)


def _dot_lhs01(m01, v):
    hi, lo = _split2(v)
    return _dot(m01, hi) + _dot(m01, lo)


def _dot_f32(a, b, dot=_dot):
    ah, am, al = _split3(a)
    bh, bm, bl = _split3(b)
    return (dot(ah, bh) + (dot(ah, bm) + dot(am, bh))
            + (dot(am, bm) + dot(ah, bl) + dot(al, bh)))


def _my_pos():
    return lax.axis_index("x"), lax.axis_index("y"), lax.axis_index("c")


class _Comm:
    def __init__(self, operands, out_shapes, sems, phases):
        self.operands, self.out_shapes, self.sems, self.phases = operands, out_shapes, sems, phases


def _gather_comm(blocks):
    n = len(blocks)

    def parts(ins, outs, sems):
        send_sems, recv_sems, local_sems = sems
        x, y, c = _my_pos()
        me, sibling = (x, y, c), (x, y, 1 - c)
        chips = [(1 - x, y), (x, 1 - y), (1 - x, 1 - y)]

        def slot(a, px, py, pc):
            return outs[a].at[4 * px + 2 * py + pc]

        def copy(a, k, block, to, src=None):
            return pltpu.make_async_remote_copy(
                src_ref=slot(a, *block) if src is None else src, dst_ref=slot(a, *block),
                send_sem=send_sems.at[a, k], recv_sem=recv_sems.at[a, k],
                device_id=to, device_id_type=MESH)

        mine = [pltpu.make_async_copy(ins[a], slot(a, *me), local_sems.at[a]) for a in range(n)]
        first = []
        for a in range(n):
            first.append(copy(a, 0, me, sibling, src=ins[a]))
            first += [copy(a, 1 + j, me, (*chip, c), src=ins[a]) for j, chip in enumerate(chips)]
        passed = [copy(a, 4 + j, (*chip, c), sibling) for j, chip in enumerate(chips) for a in range(n)]
        return c, me, sibling, chips, copy, mine, first, passed

    def start(ins, outs, sems):
        _, _, _, _, _, mine, first, _ = parts(ins, outs, sems)
        for cp in mine + first:
            cp.start()

    def forward(ins, outs, sems):
        c, me, _, chips, copy, _, _, passed = parts(ins, outs, sems)
        for j, chip in enumerate(chips):
            for a in range(n):
                copy(a, 1 + j, (*chip, c), me).wait_recv()
                passed[j * n + a].start()

    def finish(ins, outs, sems):
        c, me, sibling, chips, copy, mine, first, passed = parts(ins, outs, sems)
        for a in range(n):
            copy(a, 0, sibling, me).wait_recv()
            for j, chip in enumerate(chips):
                copy(a, 4 + j, (*chip, 1 - c), me).wait_recv()
        for cp in first + passed:
            cp.wait_send()
        for cp in mine:
            cp.wait()

    return _Comm(blocks, [jax.ShapeDtypeStruct((N_DEV,) + b.shape, b.dtype) for b in blocks],
                 [pltpu.SemaphoreType.DMA((n, 7)), pltpu.SemaphoreType.DMA((n, 7)), pltpu.SemaphoreType.DMA((n,))],
                 [start, forward, finish])


def _gather_comm_relayed(blocks):
    n = len(blocks)

    def parts(ins, outs, sems):
        send_sems, recv_sems, local_sems = sems
        x, y, c = _my_pos()
        me, sibling = (x, y, c), (x, y, 1 - c)
        x_nbr, y_nbr, diag = (1 - x, y, c), (x, 1 - y, c), (1 - x, 1 - y, c)

        def slot(a, dev, half=None):
            ref = outs[a].at[4 * dev[0] + 2 * dev[1] + dev[2]]
            if half is None:
                return ref
            rows = blocks[a].shape[0] // 2
            return ref.at[pl.ds(half * rows, rows)]

        def copy(a, k, block, to, half=None, src=None):
            return pltpu.make_async_remote_copy(
                src_ref=slot(a, block, half) if src is None else src, dst_ref=slot(a, block, half),
                send_sem=send_sems.at[a, k], recv_sem=recv_sems.at[a, k],
                device_id=to, device_id_type=MESH)

        mine = [pltpu.make_async_copy(ins[a], slot(a, me), local_sems.at[a]) for a in range(n)]
        return me, sibling, x_nbr, y_nbr, diag, copy, mine

    def start(ins, outs, sems):
        me, sibling, x_nbr, y_nbr, _, copy, mine = parts(ins, outs, sems)
        for cp in mine:
            cp.start()
        for a in range(n):
            for k, to in ((1, x_nbr), (2, y_nbr), (0, sibling)):
                copy(a, k, me, to, src=ins[a]).start()

    def forward(ins, outs, sems):
        me, sibling, x_nbr, y_nbr, _, copy, _ = parts(ins, outs, sems)
        for a in range(n):
            copy(a, 1, x_nbr, me).wait_recv()
            copy(a, 3, x_nbr, y_nbr, half=0).start()
            copy(a, 5, x_nbr, sibling).start()
        for a in range(n):
            copy(a, 2, y_nbr, me).wait_recv()
            copy(a, 4, y_nbr, x_nbr, half=1).start()
            copy(a, 6, y_nbr, sibling).start()

    def finish(ins, outs, sems):
        me, sibling, x_nbr, y_nbr, diag, copy, mine = parts(ins, outs, sems)
        sib = lambda dev: (dev[0], dev[1], sibling[2])
        for a in range(n):
            copy(a, 3, diag, me, half=0).wait_recv()
            copy(a, 4, diag, me, half=1).wait_recv()
            copy(a, 7, diag, sibling).start()
        for a in range(n):
            copy(a, 0, sibling, me).wait_recv()
            for k, dev in ((5, x_nbr), (6, y_nbr), (7, diag)):
                copy(a, k, sib(dev), me).wait_recv()
        for a in range(n):
            for k, block, to, half in ((0, me, sibling, None), (1, me, x_nbr, None), (2, me, y_nbr, None),
                                       (3, x_nbr, y_nbr, 0), (4, y_nbr, x_nbr, 1), (5, x_nbr, sibling, None),
                                       (6, y_nbr, sibling, None), (7, diag, sibling, None)):
                copy(a, k, block, to, half=half, src=ins[a] if block is me else None).wait_send()
        for cp in mine:
            cp.wait()

    return _Comm(blocks, [jax.ShapeDtypeStruct((N_DEV,) + b.shape, b.dtype) for b in blocks],
                 [pltpu.SemaphoreType.DMA((n, 8)), pltpu.SemaphoreType.DMA((n, 8)), pltpu.SemaphoreType.DMA((n,))],
                 [start, forward, finish])


_HBM = pl.BlockSpec(memory_space=pltpu.HBM)
_SEM = pl.BlockSpec(memory_space=pltpu.SEMAPHORE)
_SPLIT_COPY = pltpu.CompilerParams(has_side_effects=pltpu.SideEffectType.DATAFLOW_SIDE_EFFECTING)


def _chip_exchange_copies(src_ref, land_ref, sems):
    x, y, c = _my_pos()
    q_me = 2 * x + y
    pairs = []
    for j, (px, py) in enumerate([(1 - x, y), (x, 1 - y), (1 - x, 1 - y)]):
        q = 2 * px + py
        send = pltpu.make_async_remote_copy(
            src_ref=src_ref.at[q], dst_ref=land_ref.at[q_me], send_sem=sems[j], recv_sem=sems[3 + j],
            device_id=(px, py, c), device_id_type=MESH)
        recv = pltpu.make_async_remote_copy(
            src_ref=src_ref.at[q], dst_ref=land_ref.at[q], send_sem=sems[j], recv_sem=sems[3 + j],
            device_id=(x, y, c), device_id_type=MESH)
        pairs.append((send, recv))
    return pairs


def _chip_exchange_start(src, land):
    def body(src_ref, land_ref, *outs):
        sems, token = outs[:6], outs[8]
        for send, _ in _chip_exchange_copies(src_ref, land_ref, sems):
            send.start()
        token[...] = jnp.zeros_like(token)

    res = pl.pallas_call(
        body, name="scatter_in_start",
        out_shape=(pltpu.SemaphoreType.DMA(()),) * 6 + (
            pltpu.HBM(src.shape, src.dtype), pltpu.HBM(land.shape, land.dtype),
            jax.ShapeDtypeStruct((8, 128), F32)),
        in_specs=(_HBM, _HBM), out_specs=(_SEM,) * 6 + (_HBM, _HBM, pl.BlockSpec(memory_space=pltpu.VMEM)),
        input_output_aliases={0: 6, 1: 7}, compiler_params=_SPLIT_COPY,
    )(pltpu.with_memory_space_constraint(src, pltpu.HBM), pltpu.with_memory_space_constraint(land, pltpu.HBM))
    return res[:6], res[6], res[7], res[8]


def _chip_exchange_wait(sems, src_thru, land_thru, after):
    def body(src_ref, land_ref, *rest):
        for send, recv in _chip_exchange_copies(src_ref, land_ref, rest[:6]):
            send.wait_send()
            recv.wait_recv()

    return pl.pallas_call(
        body, name="scatter_in_wait",
        out_shape=(pltpu.HBM(src_thru.shape, src_thru.dtype), pltpu.HBM(land_thru.shape, land_thru.dtype)),
        in_specs=(_HBM, _HBM) + (_SEM,) * 6 + (_ANY,), out_specs=(_HBM, _HBM),
        input_output_aliases={0: 0, 1: 1}, compiler_params=_SPLIT_COPY,
    )(src_thru, land_thru, *sems, after)[1]


def _run_comm(comm, name, in_vmem=False):
    n_in, n_out = len(comm.operands), len(comm.out_shapes)

    def body(*refs):
        ins, outs, sems = refs[:n_in], refs[n_in:n_in + n_out], refs[n_in + n_out:]
        for phase in comm.phases:
            phase(ins, outs, sems)

    spec = _VMEM_WHOLE if in_vmem else _ANY
    return pl.pallas_call(
        body, name=name, out_shape=comm.out_shapes, in_specs=[spec] * n_in, out_specs=[spec] * n_out,
        scratch_shapes=comm.sems,
    )(*comm.operands)


def _carrier_call(body, comm, schedule, *, name, grid, in_specs, out_specs, out_shape, scratch_shapes, operands):
    n_in, n_out, n_scr = len(in_specs), len(out_specs), len(scratch_shapes)
    c_in, c_out = len(comm.operands), len(comm.out_shapes)

    def full_body(*refs):
        ins, refs = refs[:n_in], refs[n_in:]
        cins, refs = refs[:c_in], refs[c_in:]
        outs, refs = refs[:n_out], refs[n_out:]
        couts, refs = refs[:c_out], refs[c_out:]
        scr, csems = refs[:n_scr], refs[n_scr:]
        step = pl.program_id(0)

        def run(before):
            for (at, when_before), phase in zip(schedule, comm.phases):
                if when_before == before:
                    pl.when(step == at)(functools.partial(phase, cins, couts, csems))

        run(True)
        body(*ins, *outs, *scr)
        run(False)

    res = pl.pallas_call(
        full_body, name=name, grid=grid,
        in_specs=list(in_specs) + [_ANY] * c_in, out_specs=list(out_specs) + [_ANY] * c_out,
        out_shape=list(out_shape) + list(comm.out_shapes),
        scratch_shapes=list(scratch_shapes) + list(comm.sems),
        compiler_params=_cp(("arbitrary",)),
    )(*operands, *comm.operands)
    return res[:n_out], res[n_out:]


def _pcall(body, carried, *, name, grid, in_specs, out_specs, out_shape, scratch_shapes, operands):
    if carried is None:
        res = pl.pallas_call(body, name=name, grid=grid, in_specs=in_specs, out_specs=out_specs,
                             out_shape=out_shape, scratch_shapes=scratch_shapes,
                             compiler_params=_cp(("arbitrary",)))(*operands)
        return res, ()
    return _carrier_call(body, carried[0], carried[1], name=name, grid=grid, in_specs=in_specs,
                         out_specs=out_specs, out_shape=out_shape, scratch_shapes=scratch_shapes,
                         operands=operands)


def _all_gather(blocks, name, in_vmem):
    return _run_comm(_gather_comm(blocks), name, in_vmem)


N_CHIPS = 4


def _sibling_comm(contribs):
    n = len(contribs)

    def copies(ins, outs, sems):
        send_sems, recv_sems = sems
        x, y, c = _my_pos()
        return [pltpu.make_async_remote_copy(
            src_ref=ins[a].at[pl.ds(0, N_CHIPS), 1 - c], dst_ref=outs[a],
            send_sem=send_sems.at[a], recv_sem=recv_sems.at[a],
            device_id=(x, y, 1 - c), device_id_type=MESH) for a in range(n)]

    def start(ins, outs, sems):
        for cp in copies(ins, outs, sems):
            cp.start()

    def finish(ins, outs, sems):
        cps = copies(ins, outs, sems)
        for cp in cps:
            cp.wait_recv()
        for cp in cps:
            cp.wait_send()

    return _Comm(contribs, [jax.ShapeDtypeStruct((N_CHIPS,) + b.shape[2:], b.dtype) for b in contribs],
                 [pltpu.SemaphoreType.DMA((n,)), pltpu.SemaphoreType.DMA((n,))], [start, finish])


def _pair_sum(mine, theirs, name):
    _, _, rows, cols = mine.shape
    core = lax.axis_index("c").astype(jnp.int32).reshape(1)

    def body(c_ref, m_ref, t_ref, o_ref):
        o_ref[...] = (m_ref[...].astype(F32) + t_ref[...].astype(F32)).astype(BF16)

    return pl.pallas_call(
        body, name=name,
        grid_spec=pltpu.PrefetchScalarGridSpec(
            num_scalar_prefetch=1, grid=(N_CHIPS,),
            in_specs=[pl.BlockSpec((None, None, rows, cols), lambda q, c: (q, c[0], 0, 0)),
                      pl.BlockSpec((None, rows, cols), lambda q, c: (q, 0, 0))],
            out_specs=pl.BlockSpec((None, rows, cols), lambda q, c: (q, 0, 0))),
        out_shape=jax.ShapeDtypeStruct((N_CHIPS, rows, cols), BF16),
        compiler_params=_cp(("parallel",)),
    )(core, mine, theirs)


def _chip_comm(sums):
    n = len(sums)

    def parts(ins, outs, sems):
        send_sems, recv_sems, local_sems = sems
        x, y, c = _my_pos()
        q_me = 2 * x + y
        chips = [(1 - x, y), (x, 1 - y), (1 - x, 1 - y)]
        mine = [pltpu.make_async_copy(ins[a].at[q_me], outs[a].at[q_me], local_sems.at[a]) for a in range(n)]
        sends, recvs = [], []
        for j, (px, py) in enumerate(chips):
            for a in range(n):
                q = 2 * px + py
                sends.append(pltpu.make_async_remote_copy(
                    src_ref=ins[a].at[q], dst_ref=outs[a].at[q_me],
                    send_sem=send_sems.at[a, j], recv_sem=recv_sems.at[a, j],
                    device_id=(px, py, c), device_id_type=MESH))
                recvs.append(pltpu.make_async_remote_copy(
                    src_ref=ins[a].at[q], dst_ref=outs[a].at[q],
                    send_sem=send_sems.at[a, j], recv_sem=recv_sems.at[a, j],
                    device_id=(x, y, c), device_id_type=MESH))
        return mine, sends, recvs

    def start(ins, outs, sems):
        mine, sends, _ = parts(ins, outs, sems)
        for cp in mine + sends:
            cp.start()

    def finish(ins, outs, sems):
        mine, sends, recvs = parts(ins, outs, sems)
        for cp in recvs:
            cp.wait_recv()
        for cp in sends:
            cp.wait_send()
        for cp in mine:
            cp.wait()

    return _Comm(sums, [jax.ShapeDtypeStruct(b.shape, b.dtype) for b in sums],
                 [pltpu.SemaphoreType.DMA((n, 3)), pltpu.SemaphoreType.DMA((n, 3)), pltpu.SemaphoreType.DMA((n,))],
                 [start, finish])


def _mm_nt(a, bt, *, tm, tn, out_dtype, name, row_off=0, rows=None):
    rows = a.shape[0] if rows is None else rows
    n, k = bt.shape

    def body(a_ref, b_ref, o_ref):
        o_ref[...] = _dot_nt(a_ref[...], b_ref[...]).astype(out_dtype)

    return pl.pallas_call(
        body, name=name, grid=(rows // tm, n // tn),
        in_specs=[pl.BlockSpec((tm, k), lambda i, j: (i + row_off, 0)),
                  pl.BlockSpec((tn, k), lambda i, j: (j, 0))],
        out_specs=pl.BlockSpec((tm, tn), lambda i, j: (i, j)),
        out_shape=jax.ShapeDtypeStruct((rows, n), out_dtype),
        compiler_params=_cp(("parallel", "parallel")),
    )(a, bt)


def _mm_tn(a, b, *, tk, nk, tm, tn, out_dtype, name, a_off=0, b_off=0):
    m, n = a.shape[1], b.shape[1]

    def body(a_ref, b_ref, o_ref, acc):
        kk = pl.program_id(2)

        @pl.when(kk == 0)
        def _():
            acc[...] = jnp.zeros_like(acc)

        acc[...] += _dot_tn(a_ref[...], b_ref[...])

        @pl.when(kk == nk - 1)
        def _():
            o_ref[...] = acc[...].astype(out_dtype)

    return pl.pallas_call(
        body, name=name, grid=(m // tm, n // tn, nk),
        in_specs=[pl.BlockSpec((tk, tm), lambda i, j, kk: (kk + a_off, i)),
                  pl.BlockSpec((tk, tn), lambda i, j, kk: (kk + b_off, j))],
        out_specs=pl.BlockSpec((tm, tn), lambda i, j, kk: (i, j)),
        out_shape=jax.ShapeDtypeStruct((m, n), out_dtype),
        scratch_shapes=[pltpu.VMEM((tm, tn), F32)],
        compiler_params=_cp(("parallel", "parallel", "arbitrary")),
    )(a, b)


def _mm_tn_blocked(a, b, name, carried=None):
    nb, _, w = a.shape
    n = b.shape[1]

    def body(a_ref, b_ref, o_ref):
        o_ref[...] = _dot_tn(a_ref[...], b_ref[...]).astype(BF16)

    (out,), extra = _pcall(
        body, carried, name=name, grid=(nb,),
        in_specs=[pl.BlockSpec((None, S, w), lambda j: (j, 0, 0)), _full((S, n))],
        out_specs=[pl.BlockSpec((None, w, n), lambda j: (j, 0, 0))],
        out_shape=[jax.ShapeDtypeStruct((nb, w, n), BF16)],
        scratch_shapes=[], operands=[a, b])
    return out, extra


def _prologue(cond, c_ctx, w_ada, b_cols, w_in_t, x, ctx, nw):
    rows_shape = jax.ShapeDtypeStruct((16, ADA_BLK), F32)
    big, g_cond, g_mod = _gather_comm_relayed([w_in_t]), _gather_comm([cond]), _gather_comm([rows_shape])

    def body(cond_ref, cctx_ref, wada_ref, b_ref, nw_ref, win_ref, x_ref, ctx_ref,
             g0_ref, cc_ref, mod_ref, gin_ref, h_ref, hl_ref, rows_ref, g1_ref, x_s, ctx_s, h_s, io_sems, *sems):
        s_big, s_cond, s_mod = sems[0:3], sems[3:6], sems[6:9]
        big.phases[0]([win_ref], [gin_ref], s_big)
        load_x = pltpu.make_async_copy(x_ref, x_s, io_sems.at[0])
        load_ctx = pltpu.make_async_copy(ctx_ref, ctx_s, io_sems.at[1])
        load_x.start()
        load_ctx.start()
        for phase in g_cond.phases:
            phase([cond_ref], [g0_ref], s_cond)
        cc_ref[...] = jnp.zeros_like(cc_ref)
        for j in range(N_DEV):
            cc_ref[j:j + 1, :] = g0_ref[j, 0:1, :]
        cc_ref[N_DEV:N_DEV + 1, :] = cctx_ref[...]
        cv = cc_ref[...]
        rows_ref[...] = _dot_f32(cv * _sigmoid(cv), wada_ref[...]) + b_ref[...]
        for phase in g_mod.phases:
            phase([rows_ref], [g1_ref], s_mod)
        x_pos, y_pos, c_pos = _my_pos()
        me = 4 * x_pos + 2 * y_pos + c_pos
        mine = jnp.concatenate([g1_ref[j, pl.ds(me, 1), :] for j in range(N_DEV)], axis=1)
        shared = jnp.concatenate([g1_ref[j, N_DEV:N_DEV + 1, :] for j in range(N_DEV)], axis=1)
        for k in range(6):
            mod_ref[k:k + 1, :] = mine[:, k * D:(k + 1) * D]
        mod_ref[6:7, :] = shared[:, 0:D]
        mod_ref[7:8, :] = shared[:, D:2 * D]
        load_ctx.wait()
        load_x.wait()
        h_s[pl.ds(0, L), :] = _rms_mod(ctx_s[...], nw_ref[...], mod_ref[6:7, :], mod_ref[7:8, :]).astype(BF16)

        def norm_tile(i, carry):
            r0 = pl.multiple_of(i * TM, TM)
            h_s[pl.ds(L + r0, TM), :] = _rms_mod(
                x_s[pl.ds(r0, TM), :], nw_ref[...], mod_ref[0:1, :], mod_ref[1:2, :]).astype(BF16)
            return carry

        lax.fori_loop(0, N_LAT_TILES, norm_tile, 0)
        stores = [pltpu.make_async_copy(h_s, h_ref, io_sems.at[2]),
                  pltpu.make_async_copy(h_s.at[pl.ds(L, S)], hl_ref, io_sems.at[3])]
        for cp in stores:
            cp.start()
        big.phases[1]([win_ref], [gin_ref], s_big)
        big.phases[2]([win_ref], [gin_ref], s_big)
        for cp in stores:
            cp.wait()

    return pl.pallas_call(
        body, name="prologue",
        in_specs=[_VMEM_WHOLE] * 5 + [_ANY] * 3, out_specs=[_VMEM_WHOLE] * 3 + [_ANY] * 3,
        out_shape=[g_cond.out_shapes[0], jax.ShapeDtypeStruct((16, D), F32), jax.ShapeDtypeStruct((8, D), F32),
                   big.out_shapes[0], jax.ShapeDtypeStruct((T, D), BF16), jax.ShapeDtypeStruct((S, D), BF16)],
        scratch_shapes=[pltpu.VMEM((16, ADA_BLK), F32), pltpu.VMEM((N_DEV, 16, ADA_BLK), F32),
                        pltpu.VMEM((S, D), F32), pltpu.VMEM((L, D), F32), pltpu.VMEM((T, D), BF16),
                        pltpu.SemaphoreType.DMA((4,))] + big.sems + g_cond.sems + g_mod.sems,
        compiler_params=_cp(),
    )(cond, c_ctx, w_ada, b_cols, nw, w_in_t, x, ctx)


def _ada_grads(cc, dm_cols, w_ada):
    def body(c_ref, dm_ref, w_ref, gw_ref, dsc_ref):
        cv = c_ref[...]
        sc = cv * _sigmoid(cv)
        dm = dm_ref[...]
        gw_ref[...] = _dot_f32(sc, dm, dot=_dot_tn)
        dsc_ref[...] = _dot_f32(dm, w_ref[...], dot=_dot_nt)

    return pl.pallas_call(
        body, name="ada_grads",
        in_specs=[_VMEM_WHOLE] * 3, out_specs=[_VMEM_WHOLE] * 2,
        out_shape=[jax.ShapeDtypeStruct((D, ADA_BLK), F32), jax.ShapeDtypeStruct((16, D), F32)],
        compiler_params=_cp(),
    )(cc, dm_cols, w_ada)


def _lat(i):
    return jnp.maximum(i - 1, 0)


def _rms_mod(xv, nw, sh, sc):
    rstd = lax.rsqrt(jnp.mean(xv * xv, axis=-1, keepdims=True) + EPS)
    return (xv * rstd * nw) * (1.0 + sc) + sh


def _rms_mod_bwd(xv, nw, sc, dh):
    rstd = lax.rsqrt(jnp.mean(xv * xv, axis=-1, keepdims=True) + EPS)
    xhat = xv * rstd
    dn = dh * (1.0 + sc)
    dxhat = dn * nw
    dx = rstd * (dxhat - xhat * jnp.mean(dxhat * xhat, axis=-1, keepdims=True))
    return (dx, jnp.sum(dh, axis=0, keepdims=True), jnp.sum(dh * (xhat * nw), axis=0, keepdims=True),
            jnp.sum(dn * xhat, axis=0, keepdims=True))


def _chunk_masks(reverse):
    row = lax.broadcasted_iota(jnp.int32, (TM, TM), 0)
    col = lax.broadcasted_iota(jnp.int32, (TM, TM), 1)
    same = (row // CHUNK) == (col // CHUNK)
    tri = same & ((col >= row) if reverse else (col <= row))
    return same, tri


def _chunk_order(i, reverse):
    if not reverse:
        return i
    return jnp.where(i < N_CTX_CHUNKS, N_CTX_CHUNKS - 1 - i, N_CHUNKS + N_CTX_CHUNKS - 1 - i)


def _decay_terms(z, lb, same01, tri01):
    f = lb + (1.0 - lb) * _sigmoid(z)
    g = jnp.log(f)
    g2 = jnp.concatenate(_split2(g), axis=1)
    b2 = _dot(tri01, g2)
    t2 = _dot(same01, g2)
    return f, 1.0 - f, b2[:, :HG_DIM] + b2[:, HG_DIM:], t2[:, :HG_DIM] + t2[:, HG_DIM:]


def _chunk_outer(a, b):
    n = TM // CHUNK
    return jnp.einsum('ncv,nck->nvk', a.reshape(n, CHUNK, HG_DIM), b.reshape(n, CHUNK, HG_DIM),
                      preferred_element_type=F32)


def _hgrn_fwd(p_a, lbl, carried=None):
    cpt = TM // CHUNK

    def body(p_ref, lbl_ref, o_ref, st_ref, qd_s, kd_s, u_s, v_s, ebt_s):
        masks = [_chunk_masks(d == 1) for d in (0, 1)]
        same01 = jnp.where(masks[0][0], 1.0, 0.0).astype(BF16)
        tri = [m[1] for m in masks]
        tri01 = [jnp.where(t, 1.0, 0.0).astype(BF16) for t in tri]
        lb = [_sigmoid(lbl_ref[d][0:1, :] - lbl_ref[d][1:2, :]) for d in (0, 1)]

        def prep(r, carry):
            r0 = pl.multiple_of(r * TM, TM)
            vb = p_ref[pl.ds(r0, TM), 2 * HG_DIM:3 * HG_DIM].astype(BF16)
            v_s[pl.ds(r0, TM), :] = vb
            for d in (0, 1):
                z = p_ref[pl.ds(r0, TM), d * HG_DIM:(d + 1) * HG_DIM]
                _, k, b, bt = _decay_terms(z, lb[d], same01, tri01[d])
                u_s[d, pl.ds(r * cpt, cpt)] = _chunk_outer(vb, (k * jnp.exp(bt - b)).astype(BF16))
                ebt_s[d, pl.ds(r0, TM), :] = jnp.exp(bt)

                @pl.when(r >= 1)
                def _():
                    rl = pl.multiple_of(r0 - L, TM)
                    qr = p_ref[pl.ds(r0, TM), 3 * HG_DIM:4 * HG_DIM]
                    q = qr * _sigmoid(qr) * HG_DIM ** -0.5
                    qd_s[d, pl.ds(rl, TM), :] = (q * jnp.exp(b)).astype(BF16)
                    kd_s[d, pl.ds(rl, TM), :] = (k * jnp.exp(-b)).astype(BF16)

            return carry

        lax.fori_loop(0, N_TILES, prep, 0)

        def scan(i, sts):
            new = []
            for d in (0, 1):
                nn = _chunk_order(i, d == 1)
                c0 = pl.multiple_of(nn * CHUNK, CHUNK)
                st_ref[d, nn] = sts[d].astype(BF16)
                new.append(sts[d] * ebt_s[d, pl.ds(c0, 1), :] + u_s[d, nn])
            return tuple(new)

        zero = jnp.zeros((HG_DIM, HG_DIM), F32)
        lax.fori_loop(0, N_CHUNKS, scan, (zero, zero))

        def outp(r, carry):
            r0 = pl.multiple_of(r * TM, TM)
            vb = v_s[pl.ds(r0 + L, TM), :]
            o = jnp.zeros((TM, HG_DIM), F32)
            for d in (0, 1):
                qd = qd_s[d, pl.ds(r0, TM), :]
                a = jnp.where(tri[d], _dot_nt(qd, kd_s[d, pl.ds(r0, TM), :]), 0.0)
                stb = st_ref[d, pl.ds(N_CTX_CHUNKS + r * cpt, cpt)]
                inter = jnp.einsum('nck,nvk->ncv', qd.reshape(cpt, CHUNK, HG_DIM), stb,
                                   preferred_element_type=F32)
                o = o + _dot(a.astype(BF16), vb) + inter.reshape(TM, HG_DIM)
            o_ref[pl.ds(r0, TM), :] = o
            return carry

        lax.fori_loop(0, N_LAT_TILES, outp, 0, unroll=2)

    return _pcall(
        body, carried, name="hgrn_fwd", grid=(HG_HEADS,),
        in_specs=[pl.BlockSpec((T, 4 * HG_DIM), lambda h: (0, h)),
                  pl.BlockSpec((2, 2, HG_DIM), lambda h: (0, 0, h))],
        out_specs=[pl.BlockSpec((S, HG_DIM), lambda h: (0, h)),
                   pl.BlockSpec((2, None, N_CHUNKS, HG_DIM, HG_DIM), lambda h: (0, h, 0, 0, 0))],
        out_shape=[jax.ShapeDtypeStruct((S, HGW), F32),
                   jax.ShapeDtypeStruct((2, HG_HEADS, N_CHUNKS, HG_DIM, HG_DIM), BF16)],
        scratch_shapes=[pltpu.VMEM((2, S, HG_DIM), BF16), pltpu.VMEM((2, S, HG_DIM), BF16),
                        pltpu.VMEM((2, N_CHUNKS, HG_DIM, HG_DIM), F32), pltpu.VMEM((T, HG_DIM), BF16),
                        pltpu.VMEM((2, T, HG_DIM), F32)],
        operands=[p_a, lbl])


def _hgrn_bwd(p_a, lbl, d_o, st, carried=None):
    cpt = TM // CHUNK

    def rows(r):
        return r * TM if isinstance(r, int) else pl.multiple_of(r * TM, TM)

    def body(p_ref, lbl_ref, do_ref, st_ref, dp_ref, dlb_ref, b_s, bt_s, dbt_s, qd_s, dst_s, w_s):
        masks = [_chunk_masks(d == 1) for d in (0, 1)]
        same01 = jnp.where(masks[0][0], 1.0, 0.0).astype(BF16)
        tri = [m[1] for m in masks]
        tri01 = [jnp.where(t, 1.0, 0.0).astype(BF16) for t in tri]
        later01 = [tri01[1], tri01[0]]
        lb = [_sigmoid(lbl_ref[d][0:1, :] - lbl_ref[d][1:2, :]) for d in (0, 1)]

        def prep_tile(r, latent):
            r0 = rows(r)
            for d in (0, 1):
                z = p_ref[pl.ds(r0, TM), d * HG_DIM:(d + 1) * HG_DIM]
                _, _, b, bt = _decay_terms(z, lb[d], same01, tri01[d])
                b_s[d, pl.ds(r0, TM), :] = b
                bt_s[d, pl.ds(r0, TM), :] = bt
                if latent:
                    rl = pl.multiple_of(r0 - L, TM)
                    qr = p_ref[pl.ds(r0, TM), 3 * HG_DIM:4 * HG_DIM]
                    qd = (qr * _sigmoid(qr) * HG_DIM ** -0.5 * jnp.exp(b)).astype(BF16)
                    qd_s[d, pl.ds(rl, TM), :] = qd
                    w_s[d, pl.ds(r * cpt, cpt)] = _chunk_outer(
                        do_ref[pl.ds(rl, TM), :].astype(BF16), qd).astype(BF16)

        prep_tile(0, False)
        w_s[:, pl.ds(0, N_CTX_CHUNKS)] = jnp.zeros((2, N_CTX_CHUNKS, HG_DIM, HG_DIM), BF16)

        def prep(r, carry):
            prep_tile(r, True)
            return carry

        lax.fori_loop(1, N_TILES, prep, 0, unroll=2)

        def rscan(j, dsts):
            i = N_CHUNKS - 1 - j
            new = []
            for d in (0, 1):
                nn = _chunk_order(i, d == 1)
                c0 = pl.multiple_of(nn * CHUNK, CHUNK)
                dst_s[d, nn] = dsts[d].astype(BF16)
                after = st_ref[d, _chunk_order(jnp.minimum(i + 1, N_CHUNKS - 1), d == 1)].astype(F32)
                dbt_s[d, pl.ds(c0, CHUNK), :] = jnp.broadcast_to(
                    jnp.sum(after * dsts[d], axis=0, keepdims=True), (CHUNK, HG_DIM))
                new.append(dsts[d] * jnp.exp(bt_s[d, pl.ds(c0, 1), :]) + w_s[d, nn].astype(F32))
            return tuple(new)

        zero = jnp.zeros((HG_DIM, HG_DIM), F32)
        lax.fori_loop(0, N_CHUNKS, rscan, (zero, zero))

        def grad_tile(r, latent):
            r0 = rows(r)
            vb = p_ref[pl.ds(r0, TM), 2 * HG_DIM:3 * HG_DIM].astype(BF16)
            dv = jnp.zeros((TM, HG_DIM), F32)
            dq = jnp.zeros((TM, HG_DIM), F32)
            dlbs = []
            if latent:
                rl = pl.multiple_of(r0 - L, TM)
                qr = p_ref[pl.ds(r0, TM), 3 * HG_DIM:4 * HG_DIM]
                sq = _sigmoid(qr)
                do = do_ref[pl.ds(rl, TM), :].astype(BF16)
                da_full = _dot_nt(do, vb)
            for d in (0, 1):
                z = p_ref[pl.ds(r0, TM), d * HG_DIM:(d + 1) * HG_DIM]
                sz = _sigmoid(z)
                f = lb[d] + (1.0 - lb[d]) * sz
                k = 1.0 - f
                b = b_s[d, pl.ds(r0, TM), :]
                e2 = jnp.exp(bt_s[d, pl.ds(r0, TM), :] - b)
                dstb = dst_s[d, pl.ds(r * cpt, cpt)]
                kd2 = k * e2
                dkd2 = jnp.einsum('ncv,nvk->nck', vb.reshape(cpt, CHUNK, HG_DIM), dstb,
                                  preferred_element_type=F32).reshape(TM, HG_DIM)
                dv = dv + jnp.einsum('nck,nvk->ncv', kd2.astype(BF16).reshape(cpt, CHUNK, HG_DIM), dstb,
                                     preferred_element_type=F32).reshape(TM, HG_DIM)
                dk = dkd2 * e2
                db = -(kd2 * dkd2)
                if latent:
                    eb = jnp.exp(b)
                    enb = jnp.exp(-b)
                    qdf = qr * sq * HG_DIM ** -0.5 * eb
                    kdf = k * enb
                    qd = qd_s[d, pl.ds(rl, TM), :]
                    kd = kdf.astype(BF16)
                    a = jnp.where(tri[d], _dot_nt(qd, kd), 0.0).astype(BF16)
                    da = jnp.where(tri[d], da_full, 0.0).astype(BF16)
                    stb = st_ref[d, pl.ds(r * cpt, cpt)]
                    dqd = _dot(da, kd) + jnp.einsum(
                        'ncv,nvk->nck', do.reshape(cpt, CHUNK, HG_DIM), stb,
                        preferred_element_type=F32).reshape(TM, HG_DIM)
                    dkd = _dot_tn(da, qd)
                    dv = dv + _dot_tn(a, do)
                    dk = dk + dkd * enb
                    db = db + qdf * dqd - kdf * dkd
                    dq = dq + dqd * eb
                dg = _dot_lhs01(later01[d], db) + dbt_s[d, pl.ds(r0, TM), :]
                df = dg / f - dk
                dp_ref[pl.ds(r0, TM), d * HG_DIM:(d + 1) * HG_DIM] = (
                    df * (1.0 - lb[d]) * sz * (1.0 - sz)).astype(BF16)
                dlbs.append(jnp.sum(df * (1.0 - sz), axis=0, keepdims=True))
            dp_ref[pl.ds(r0, TM), 2 * HG_DIM:3 * HG_DIM] = dv.astype(BF16)
            if latent:
                dq = dq * (HG_DIM ** -0.5) * (sq * (1.0 + qr * (1.0 - sq)))
            dp_ref[pl.ds(r0, TM), 3 * HG_DIM:4 * HG_DIM] = dq.astype(BF16)
            return dlbs

        dlb_ctx = grad_tile(0, False)

        def grads(r, acc):
            t = grad_tile(r, True)
            return (acc[0] + t[0], acc[1] + t[1])

        dlb = lax.fori_loop(1, N_TILES, grads, (dlb_ctx[0], dlb_ctx[1]))
        dlb_ref[0:1, :] = dlb[0]
        dlb_ref[1:2, :] = dlb[1]

    return _pcall(
        body, carried, name="hgrn_bwd", grid=(HG_HEADS,),
        in_specs=[pl.BlockSpec((T, 4 * HG_DIM), lambda h: (0, h)),
                  pl.BlockSpec((2, 2, HG_DIM), lambda h: (0, 0, h)),
                  pl.BlockSpec((S, HG_DIM), lambda h: (0, h)),
                  pl.BlockSpec((2, None, N_CHUNKS, HG_DIM, HG_DIM), lambda h: (0, h, 0, 0, 0))],
        out_specs=[pl.BlockSpec((T, 4 * HG_DIM), lambda h: (0, h)),
                   pl.BlockSpec((2, HG_DIM), lambda h: (0, h))],
        out_shape=[jax.ShapeDtypeStruct((T, WA), BF16), jax.ShapeDtypeStruct((2, HGW), F32)],
        scratch_shapes=[pltpu.VMEM((2, T, HG_DIM), F32), pltpu.VMEM((2, T, HG_DIM), F32),
                        pltpu.VMEM((2, T, HG_DIM), F32), pltpu.VMEM((2, S, HG_DIM), BF16),
                        pltpu.VMEM((2, N_CHUNKS, HG_DIM, HG_DIM), BF16),
                        pltpu.VMEM((2, N_CHUNKS, HG_DIM, HG_DIM), BF16)],
        operands=[p_a, lbl, d_o, st])


def _rope_tables():
    t = np.arange(S)
    inv = ROPE_THETA ** (-np.arange(0, 32, 2, dtype=np.float64) / 32)
    lane = np.arange(64)
    pos = np.where(lane[None, :] < 32, (t // GRID_W)[:, None], (t % GRID_W)[:, None]).astype(np.float64)
    ang = pos * inv[(lane % 32) % 16][None, :]
    sign = np.where((lane % 32) < 16, -1.0, 1.0)[None, :]
    cos = np.tile(np.cos(ang), (1, 2)).astype(np.float32)
    sin = np.tile(np.sin(ang) * sign, (1, 2)).astype(np.float32)
    return jnp.asarray(cos), jnp.asarray(sin)


def _rope_partner(v):
    lane = lax.broadcasted_iota(jnp.int32, (1, 128), 1)
    first = (lane % 32) < 16
    slabs = []
    for j in range(v.shape[1] // 128):
        s = v[:, 128 * j:128 * (j + 1)]
        slabs.append(jnp.where(first, pltpu.roll(s, 112, 1), pltpu.roll(s, 16, 1)))
    return slabs[0] if len(slabs) == 1 else jnp.concatenate(slabs, axis=1)


def _group_ones(width, group):
    r = lax.broadcasted_iota(jnp.int32, (width, width), 0)
    c = lax.broadcasted_iota(jnp.int32, (width, width), 1)
    return jnp.where((r // group) == (c // group), 1.0, 0.0).astype(BF16)


def _group_mean(v, ones01, group):
    hi = v.astype(BF16)
    lo = (v - hi.astype(F32)).astype(BF16)
    return (_dot(hi, ones01) + _dot(lo, ones01)) * (1.0 / group)


def _rep_matrix():
    r = lax.broadcasted_iota(jnp.int32, (KVW, ATW), 0)
    c = lax.broadcasted_iota(jnp.int32, (KVW, ATW), 1)
    return jnp.where(r == HEAD_DIM * (c // 256) + c % HEAD_DIM, 1.0, 0.0).astype(BF16)


def _tile_lanes(v, reps):
    return jnp.concatenate([v] * reps, axis=1)


def _prep_fwd(p_b, o, cos, sin, hnw, qnw, knw):
    def body(p_ref, o_ref, cos_ref, sin_ref, hnw_ref, qnw_ref, knw_ref, y_ref, q_ref, k_ref, v_ref):
        i = pl.program_id(0)
        rep = _rep_matrix()
        ones_k = _group_ones(KVW, HEAD_DIM)
        kr = p_ref[:, 1024:1152]
        krstd = lax.rsqrt(_group_mean(kr * kr, ones_k, HEAD_DIM) + EPS)
        kn = kr * krstd * knw_ref[...]
        v_ref[...] = _dot(p_ref[:, 1152:1280].astype(BF16), rep).astype(BF16)

        @pl.when(i == 0)
        def _():
            k_ref[...] = _dot(kn.astype(BF16), rep).astype(BF16)

        @pl.when(i > 0)
        def _():
            cs, sn = cos_ref[...], sin_ref[...]
            kro = kn * cs + _rope_partner(kn) * sn
            k_ref[...] = _dot(kro.astype(BF16), rep).astype(BF16)
            qr = p_ref[:, 512:1024]
            qrstd = lax.rsqrt(_group_mean(qr * qr, _group_ones(ATW, HEAD_DIM), HEAD_DIM) + EPS)
            qn = qr * qrstd * qnw_ref[...]
            qro = qn * _tile_lanes(cs, 4) + _rope_partner(qn) * _tile_lanes(sn, 4)
            q_ref[...] = (qro * HEAD_DIM ** -0.5).astype(BF16)
            ys = []
            for h in range(HG_HEADS):
                oh = o_ref[:, HG_DIM * h:HG_DIM * (h + 1)]
                gh = p_ref[:, HG_DIM * h:HG_DIM * (h + 1)]
                rstd = lax.rsqrt(jnp.mean(oh * oh, axis=-1, keepdims=True) + EPS)
                ys.append(oh * rstd * hnw_ref[...] * (gh * _sigmoid(gh)))
            y_ref[...] = jnp.concatenate(ys, axis=1).astype(BF16)

    return pl.pallas_call(
        body, name="prep_fwd", grid=(N_TILES,),
        in_specs=[pl.BlockSpec((TM, WB), lambda i: (i, 0)),
                  pl.BlockSpec((TM, HGW), lambda i: (_lat(i), 0)),
                  pl.BlockSpec((TM, 128), lambda i: (_lat(i), 0)),
                  pl.BlockSpec((TM, 128), lambda i: (_lat(i), 0)),
                  _full((1, HG_DIM)), _full((1, ATW)), _full((1, KVW))],
        out_specs=[pl.BlockSpec((TM, HGW), lambda i: (_lat(i), 0)),
                   pl.BlockSpec((TM, ATW), lambda i: (_lat(i), 0)),
                   pl.BlockSpec((TM, ATW), lambda i: (i, 0)),
                   pl.BlockSpec((TM, ATW), lambda i: (i, 0))],
        out_shape=[jax.ShapeDtypeStruct((S, HGW), BF16), jax.ShapeDtypeStruct((S, ATW), BF16),
                   jax.ShapeDtypeStruct((T, ATW), BF16), jax.ShapeDtypeStruct((T, ATW), BF16)],
        compiler_params=_cp(("arbitrary",)),
    )(p_b, o, cos, sin, hnw, qnw, knw)


def _prep_bwd(p_b, o, cos, sin, hnw, qnw, knw, dy_hg, dq, dk_rep, dv_rep, carried=None):
    def body(p_ref, o_ref, cos_ref, sin_ref, hnw_ref, qnw_ref, knw_ref, dy_ref, dq_ref, dk_ref, dv_ref,
             dp_ref, do_ref, acc_ref):
        i = pl.program_id(0)

        @pl.when(i == 0)
        def _():
            acc_ref[...] = jnp.zeros_like(acc_ref)

        rep = _rep_matrix()
        ones_k = _group_ones(KVW, HEAD_DIM)

        def fold(v):
            hi = v.astype(BF16)
            lo = (v - hi.astype(F32)).astype(BF16)
            return _dot_nt(hi, rep) + _dot_nt(lo, rep)

        kr = p_ref[:, 1024:1152]
        krstd = lax.rsqrt(_group_mean(kr * kr, ones_k, HEAD_DIM) + EPS)
        khat = kr * krstd
        kw = knw_ref[...]
        dkro = fold(dk_ref[...])
        dv = fold(dv_ref[...])

        def k_back(dkn):
            dkhat = dkn * kw
            dkr = krstd * (dkhat - khat * _group_mean(dkhat * khat, ones_k, HEAD_DIM))
            acc_ref[2:3, 0:KVW] += jnp.sum(dkn * khat, axis=0, keepdims=True)
            dp_ref[:, 1024:1152] = dkr.astype(BF16)
            dp_ref[:, 1152:1280] = dv.astype(BF16)

        @pl.when(i == 0)
        def _():
            k_back(dkro)
            dp_ref[:, 0:1024] = jnp.zeros((TM, 1024), BF16)

        @pl.when(i > 0)
        def _():
            cs, sn = cos_ref[...], sin_ref[...]
            k_back(dkro * cs + _rope_partner(dkro * sn))
            ones_q = _group_ones(ATW, HEAD_DIM)
            qr = p_ref[:, 512:1024]
            qrstd = lax.rsqrt(_group_mean(qr * qr, ones_q, HEAD_DIM) + EPS)
            qhat = qr * qrstd
            dqro = dq_ref[...] * HEAD_DIM ** -0.5
            dqn = dqro * _tile_lanes(cs, 4) + _rope_partner(dqro * _tile_lanes(sn, 4))
            dqhat = dqn * qnw_ref[...]
            dqr = qrstd * (dqhat - qhat * _group_mean(dqhat * qhat, ones_q, HEAD_DIM))
            acc_ref[1:2, :] += jnp.sum(dqn * qhat, axis=0, keepdims=True)
            dp_ref[:, 512:1024] = dqr.astype(BF16)
            dws = jnp.zeros((1, HG_DIM), F32)
            for h in range(HG_HEADS):
                sl = slice(HG_DIM * h, HG_DIM * (h + 1))
                oh, gh, dy = o_ref[:, sl], p_ref[:, sl], dy_ref[:, sl]
                rstd = lax.rsqrt(jnp.mean(oh * oh, axis=-1, keepdims=True) + EPS)
                ohat = oh * rstd
                sg = _sigmoid(gh)
                dp_ref[:, sl] = (dy * (ohat * hnw_ref[...]) * (sg * (1.0 + gh * (1.0 - sg)))).astype(BF16)
                dn = dy * (gh * sg)
                dws = dws + jnp.sum(dn * ohat, axis=0, keepdims=True)
                dohat = dn * hnw_ref[...]
                do_ref[:, sl] = rstd * (dohat - ohat * jnp.mean(dohat * ohat, axis=-1, keepdims=True))
            acc_ref[0:1, 0:HG_DIM] += dws

    return _pcall(
        body, carried, name="prep_bwd", grid=(N_TILES,),
        in_specs=[pl.BlockSpec((TM, WB), lambda i: (i, 0)),
                  pl.BlockSpec((TM, HGW), lambda i: (_lat(i), 0)),
                  pl.BlockSpec((TM, 128), lambda i: (_lat(i), 0)),
                  pl.BlockSpec((TM, 128), lambda i: (_lat(i), 0)),
                  _full((1, HG_DIM)), _full((1, ATW)), _full((1, KVW)),
                  pl.BlockSpec((TM, HGW), lambda i: (_lat(i), 0)),
                  pl.BlockSpec((TM, ATW), lambda i: (_lat(i), 0)),
                  pl.BlockSpec((TM, ATW), lambda i: (i, 0)),
                  pl.BlockSpec((TM, ATW), lambda i: (i, 0))],
        out_specs=[pl.BlockSpec((TM, WB), lambda i: (i, 0)),
                   pl.BlockSpec((TM, HGW), lambda i: (_lat(i), 0)),
                   _full((8, ATW))],
        out_shape=[jax.ShapeDtypeStruct((T, WB), BF16), jax.ShapeDtypeStruct((S, HGW), F32),
                   jax.ShapeDtypeStruct((8, ATW), F32)],
        scratch_shapes=[], operands=[p_b, o, cos, sin, hnw, qnw, knw, dy_hg, dq, dk_rep, dv_rep])


NEG = -1e30
_CTX_BLOCKS = L // BLOCK


def _attn_window_specs():
    prev = pl.BlockSpec((BLOCK, ATW), lambda i: (jnp.maximum(i - 1, 0) + _CTX_BLOCKS, 0))
    own = pl.BlockSpec((BLOCK, ATW), lambda i: (i + _CTX_BLOCKS, 0))
    nxt = pl.BlockSpec((BLOCK, ATW), lambda i: (jnp.minimum(i + 1, N_BLOCKS - 1) + _CTX_BLOCKS, 0))
    return [prev, own, nxt, _full((L, ATW))]


def _attn_valid(i, heads, context):
    n_keys = 3 * BLOCK + (L if context else 0)
    qi = lax.broadcasted_iota(jnp.int32, (heads * BLOCK, n_keys), 0) % BLOCK
    kj = lax.broadcasted_iota(jnp.int32, (heads * BLOCK, n_keys), 1)
    window = ((jnp.abs(kj - BLOCK - qi) <= BLOCK) & ((kj >= BLOCK) | (i > 0))
              & ((kj < 2 * BLOCK) | (i < N_BLOCKS - 1)))
    return window | (kj >= 3 * BLOCK)


def _stack_heads(qg):
    lane = lax.broadcasted_iota(jnp.int32, (1, 256), 1) // HEAD_DIM
    return jnp.concatenate([jnp.where(lane == g, qg, jnp.zeros_like(qg)) for g in range(4)], axis=0)


def _unstack_heads(v4):
    lane = lax.broadcasted_iota(jnp.int32, (1, 256), 1) // HEAD_DIM
    out = jnp.where(lane == 0, v4[0:BLOCK], 0.0)
    for g in range(1, 4):
        out = out + jnp.where(lane == g, v4[g * BLOCK:(g + 1) * BLOCK], 0.0)
    return out


def _sink_rows(sink_ref, hk):
    return jnp.concatenate(
        [jnp.broadcast_to(sink_ref[0:1, 4 * hk + g:4 * hk + g + 1], (BLOCK, 1)) for g in range(4)], axis=0)


def _attn_fwd(q, k_rep, v_rep, sinks, carried=None):
    def body(q_ref, kp, ko, kn, kc, vp, vo, vn, vc, sink_ref, y_ref, lse_ref):
        i = pl.program_id(0)
        valid = _attn_valid(i, 1, True)
        lane8 = lax.broadcasted_iota(jnp.int32, (1, ATT_HEADS), 1)
        head_of_lane = lax.broadcasted_iota(jnp.int32, (1, 256), 1) // HEAD_DIM
        lse_out = jnp.zeros((BLOCK, ATT_HEADS), F32)
        for hk in range(KV_HEADS):
            sl = slice(256 * hk, 256 * (hk + 1))
            qg = q_ref[:, sl]
            keys = jnp.concatenate([kp[:, sl], ko[:, sl], kn[:, sl], kc[:, sl]], axis=0)
            vals = jnp.concatenate([vp[:, sl], vo[:, sl], vn[:, sl], vc[:, sl]], axis=0)
            yg = jnp.zeros((BLOCK, 256), F32)
            for g in range(4):
                q1 = jnp.where(head_of_lane == g, qg, jnp.zeros_like(qg))
                s = jnp.where(valid, _dot_nt(q1, keys), NEG)
                sink = sink_ref[0:1, 4 * hk + g:4 * hk + g + 1]
                m = jnp.maximum(jnp.max(s, axis=1, keepdims=True), sink)
                p = jnp.exp(s - m)
                den = jnp.sum(p, axis=1, keepdims=True) + jnp.exp(sink - m)
                o1 = _dot(p.astype(BF16), vals) * (1.0 / den)
                yg = yg + jnp.where(head_of_lane == g, o1, 0.0)
                lse_out = lse_out + jnp.where(lane8 == 4 * hk + g, m + jnp.log(den), 0.0)
            y_ref[:, sl] = yg.astype(BF16)
        lse_ref[...] = lse_out

    return _pcall(
        body, carried, name="attn_fwd", grid=(N_BLOCKS,),
        in_specs=[pl.BlockSpec((BLOCK, ATW), lambda i: (i, 0))] + _attn_window_specs()
        + _attn_window_specs() + [_full((1, ATT_HEADS))],
        out_specs=[pl.BlockSpec((BLOCK, ATW), lambda i: (i, 0)),
                   pl.BlockSpec((BLOCK, ATT_HEADS), lambda i: (i, 0))],
        out_shape=[jax.ShapeDtypeStruct((S, ATW), BF16), jax.ShapeDtypeStruct((S, ATT_HEADS), F32)],
        scratch_shapes=[],
        operands=[q, k_rep, k_rep, k_rep, k_rep, v_rep, v_rep, v_rep, v_rep, sinks])


def _attn_bwd(q, k_rep, v_rep, sinks, y_at, lse, dy, carried=None):
    def body(q_ref, kp, ko, kn, kc, vp, vo, vn, vc, sink_ref, y_ref, lse_ref, dy_ref,
             dq_ref, dk_ref, dv_ref, dsink_ref, dk_acc, dv_acc):
        i = pl.program_id(0)

        @pl.when(i == 0)
        def _():
            dk_acc[...] = jnp.zeros_like(dk_acc)
            dv_acc[...] = jnp.zeros_like(dv_acc)
            dk_ref[pl.ds(0, L), :] = jnp.zeros((L, ATW), F32)
            dv_ref[pl.ds(0, L), :] = jnp.zeros((L, ATW), F32)
            dsink_ref[...] = jnp.zeros_like(dsink_ref)

        valid = _attn_valid(i, 4, False)
        lane8 = lax.broadcasted_iota(jnp.int32, (1, ATT_HEADS), 1)
        w0 = pl.multiple_of(i * BLOCK, BLOCK)
        dsink = jnp.zeros((1, ATT_HEADS), F32)
        for hk in range(KV_HEADS):
            sl = slice(256 * hk, 256 * (hk + 1))
            q4 = _stack_heads(q_ref[:, sl])
            do4f = _stack_heads(dy_ref[:, sl])
            o4 = _stack_heads(y_ref[:, sl]).astype(F32)
            do4 = do4f.astype(BF16)
            kl = jnp.concatenate([kp[:, sl], ko[:, sl], kn[:, sl]], axis=0)
            vl = jnp.concatenate([vp[:, sl], vo[:, sl], vn[:, sl]], axis=0)
            lse4 = jnp.concatenate(
                [jnp.sum(jnp.where(lane8 == 4 * hk + g, lse_ref[...], 0.0), axis=1, keepdims=True)
                 for g in range(4)], axis=0)
            p_loc = jnp.where(valid, jnp.exp(_dot_nt(q4, kl) - lse4), 0.0)
            p_ctx = jnp.exp(_dot_nt(q4, kc[:, sl]) - lse4)
            delta = jnp.sum(do4f * o4, axis=1, keepdims=True)
            ds_loc = (p_loc * (_dot_nt(do4, vl) - delta)).astype(BF16)
            ds_ctx = (p_ctx * (_dot_nt(do4, vc[:, sl]) - delta)).astype(BF16)
            dq_ref[:, sl] = _unstack_heads(_dot(ds_loc, kl) + _dot(ds_ctx, kc[:, sl]))
            dk_acc[pl.ds(w0, 3 * BLOCK), sl] += _dot_tn(ds_loc, q4)
            dv_acc[pl.ds(w0, 3 * BLOCK), sl] += _dot_tn(p_loc.astype(BF16), do4)
            dk_ref[pl.ds(0, L), sl] += _dot_tn(ds_ctx, q4)
            dv_ref[pl.ds(0, L), sl] += _dot_tn(p_ctx.astype(BF16), do4)
            p_sink = jnp.exp(_sink_rows(sink_ref, hk) - lse4)
            for g in range(4):
                rows = slice(g * BLOCK, (g + 1) * BLOCK)
                dsink = dsink + jnp.where(lane8 == 4 * hk + g,
                                          -jnp.sum(p_sink[rows] * delta[rows], axis=0, keepdims=True), 0.0)
        dsink_ref[...] += dsink

        @pl.when(i == N_BLOCKS - 1)
        def _():
            dk_ref[pl.ds(L, S), :] = dk_acc[pl.ds(BLOCK, S), :]
            dv_ref[pl.ds(L, S), :] = dv_acc[pl.ds(BLOCK, S), :]

    row_q = pl.BlockSpec((BLOCK, ATW), lambda i: (i, 0))
    return _pcall(
        body, carried, name="attn_bwd", grid=(N_BLOCKS,),
        in_specs=[row_q] + _attn_window_specs() + _attn_window_specs()
        + [_full((1, ATT_HEADS)), row_q, pl.BlockSpec((BLOCK, ATT_HEADS), lambda i: (i, 0)), row_q],
        out_specs=[row_q, _full((T, ATW)), _full((T, ATW)), _full((1, ATT_HEADS))],
        out_shape=[jax.ShapeDtypeStruct((S, ATW), F32), jax.ShapeDtypeStruct((T, ATW), F32),
                   jax.ShapeDtypeStruct((T, ATW), F32), jax.ShapeDtypeStruct((1, ATT_HEADS), F32)],
        scratch_shapes=[pltpu.VMEM((S + 2 * BLOCK, ATW), F32), pltpu.VMEM((S + 2 * BLOCK, ATW), F32)],
        operands=[q, k_rep, k_rep, k_rep, k_rep, v_rep, v_rep, v_rep, v_rep, sinks, y_at, lse, dy])


def _merge_fwd(y_hg, y_at, p_c, x, w_bh, w_ba, w_out, g1, nfw, sh2, sc2, carried=None):
    def body(yh_ref, ya_ref, g_ref, x_ref, wbh_ref, wba_ref, wo_ref, g1_ref, nfw_ref, sh_ref, sc_ref,
             mx_ref, r_ref, x1_ref, h2_ref):
        a = _dot_nt(yh_ref[...], wbh_ref[...])
        b = _dot_nt(ya_ref[...], wba_ref[...])
        mixed = (_sigmoid(g_ref[:, :D]) * a + _sigmoid(g_ref[:, D:]) * b).astype(BF16)
        r = _dot(mixed, wo_ref[...])
        x1 = x_ref[...] + g1_ref[...] * r
        mx_ref[...] = mixed
        r_ref[...] = r
        x1_ref[...] = x1
        h2_ref[...] = _rms_mod(x1, nfw_ref[...], sh_ref[...], sc_ref[...]).astype(BF16)

    row = lambda w: pl.BlockSpec((TM, w), lambda i: (i, 0))
    vec = _full((1, D))
    return _pcall(
        body, carried, name="merge_fwd", grid=(N_LAT_TILES,),
        in_specs=[row(HGW), row(ATW), row(WC), row(D), _VMEM_WHOLE, _VMEM_WHOLE, _VMEM_WHOLE,
                  vec, vec, vec, vec],
        out_specs=[row(D)] * 4,
        out_shape=[jax.ShapeDtypeStruct((S, D), dt) for dt in (BF16, F32, F32, BF16)],
        scratch_shapes=[], operands=[y_hg, y_at, p_c, x, w_bh, w_ba, w_out, g1, nfw, sh2, sc2])


def _merge_bwd(dx1, r, y_hg, y_at, p_c, w_bh, w_ba, w_out, g1, carried=None):
    def body(dx_ref, r_ref, yh_ref, ya_ref, g_ref, wbh_ref, wba_ref, wo_ref, g1_ref,
             dr_ref, da_ref, db_ref, dg_ref, dyh_ref, dya_ref, acc_ref):
        @pl.when(pl.program_id(0) == 0)
        def _():
            acc_ref[...] = jnp.zeros_like(acc_ref)

        dx1v = dx_ref[...]
        acc_ref[0:1, :] += jnp.sum(dx1v * r_ref[...], axis=0, keepdims=True)
        dr = (g1_ref[...] * dx1v).astype(BF16)
        dr_ref[...] = dr
        dmix = _dot_nt(dr, wo_ref[...])
        sh, sa = _sigmoid(g_ref[:, :D]), _sigmoid(g_ref[:, D:])
        da = (dmix * sh).astype(BF16)
        db = (dmix * sa).astype(BF16)
        da_ref[...] = da
        db_ref[...] = db
        dg_ref[:, :D] = (dmix * _dot_nt(yh_ref[...], wbh_ref[...]) * sh * (1.0 - sh)).astype(BF16)
        dg_ref[:, D:] = (dmix * _dot_nt(ya_ref[...], wba_ref[...]) * sa * (1.0 - sa)).astype(BF16)
        dyh_ref[...] = _dot(da, wbh_ref[...])
        dya_ref[...] = _dot(db, wba_ref[...])

    row = lambda w: pl.BlockSpec((TM, w), lambda i: (i, 0))
    return _pcall(
        body, carried, name="merge_bwd", grid=(N_LAT_TILES,),
        in_specs=[row(D), row(D), row(HGW), row(ATW), row(WC), _VMEM_WHOLE, _VMEM_WHOLE, _VMEM_WHOLE,
                  _full((1, D))],
        out_specs=[row(D), row(D), row(D), row(WC), row(HGW), row(ATW), _full((8, D))],
        out_shape=[jax.ShapeDtypeStruct((S, D), BF16), jax.ShapeDtypeStruct((S, D), BF16),
                   jax.ShapeDtypeStruct((S, D), BF16), jax.ShapeDtypeStruct((S, WC), BF16),
                   jax.ShapeDtypeStruct((S, HGW), F32), jax.ShapeDtypeStruct((S, ATW), F32),
                   jax.ShapeDtypeStruct((8, D), F32)],
        scratch_shapes=[], operands=[dx1, r, y_hg, y_at, p_c, w_bh, w_ba, w_out, g1])


def _ffn_fused(x1, h2, tgt, w_gate, w_up, w_down, g2, nfw, sc2):
    def body(x1_ref, h2_ref, t_ref, wg_ref, wu_ref, wd_ref, g2_ref, nfw_ref, sc_ref,
             act_ref, dgt_ref, dup_ref, df_ref, dx_ref, acc_ref, gs, us):
        @pl.when(pl.program_id(0) == 0)
        def _():
            acc_ref[...] = jnp.zeros_like(acc_ref)

        h2 = h2_ref[...]
        whole = lambda w_ref: w_ref[...].reshape(D_FF, D)
        wide = lambda t_ref: jnp.concatenate([t_ref[j] for j in range(N_FF_TILES)], axis=1)
        for j in range(N_FF_TILES):
            g = _dot_nt(h2, wg_ref[j])
            u = _dot_nt(h2, wu_ref[j])
            gs[j] = g
            us[j] = u
            act_ref[j] = (g * _sigmoid(g) * u).astype(BF16)
        f = _dot(wide(act_ref), whole(wd_ref))
        x1v = x1_ref[...]
        g2 = g2_ref[...]
        diff = x1v + g2 * f - t_ref[...]
        dy = diff * (1.0 / D)
        df = (g2 * dy).astype(BF16)
        df_ref[...] = df
        dact_all = _dot_nt(df, whole(wd_ref))
        for j in range(N_FF_TILES):
            g, u = gs[j], us[j]
            sg = _sigmoid(g)
            dact = dact_all[:, j * FF_TILE:(j + 1) * FF_TILE]
            dgt_ref[j] = (dact * u * (sg * (1.0 + g * (1.0 - sg)))).astype(BF16)
            dup_ref[j] = (dact * (g * sg)).astype(BF16)
        dh2 = _dot(wide(dgt_ref), whole(wg_ref)) + _dot(wide(dup_ref), whole(wu_ref))
        dx, dsh, dsc, dnw = _rms_mod_bwd(x1v, nfw_ref[...], sc_ref[...], dh2)
        dx_ref[...] = dy + dx
        acc_ref[0:1, :] += dsh
        acc_ref[1:2, :] += dsc
        acc_ref[2:3, :] += dnw
        acc_ref[3:4, :] += jnp.sum(dy * f, axis=0, keepdims=True)
        acc_ref[4:5, :] += 0.5 * jnp.sum(jnp.sum(diff * diff, axis=1, keepdims=True), axis=0,
                                         keepdims=True) * (1.0 / D)

    row = lambda dt_w: pl.BlockSpec((TM, dt_w), lambda i: (i, 0))
    blk = pl.BlockSpec((N_FF_TILES, TM, FF_TILE), lambda i: (0, i, 0))
    vec = _full((1, D))
    return pl.pallas_call(
        body, name="ffn_fused", grid=(N_LAT_TILES,),
        in_specs=[row(D), row(D), row(D), _VMEM_WHOLE, _VMEM_WHOLE, _VMEM_WHOLE, vec, vec, vec],
        out_specs=[blk, blk, blk, row(D), row(D), _full((8, D))],
        out_shape=[jax.ShapeDtypeStruct((N_FF_TILES, S, FF_TILE), BF16)] * 3
        + [jax.ShapeDtypeStruct((S, D), BF16), jax.ShapeDtypeStruct((S, D), F32),
           jax.ShapeDtypeStruct((8, D), F32)],
        scratch_shapes=[pltpu.VMEM((N_FF_TILES, TM, FF_TILE), F32), pltpu.VMEM((N_FF_TILES, TM, FF_TILE), F32)],
        compiler_params=_cp(("arbitrary",)),
    )(x1, h2, tgt, w_gate, w_up, w_down, g2, nfw, sc2)


def _proj_bc(h_all, w_b, w_c, carried=None):
    def body(h_ref, wb_ref, wc_ref, pb_ref, pc_ref):
        h = h_ref[...]
        pb_ref[...] = _dot_nt(h, wb_ref[...])

        @pl.when(pl.program_id(0) > 0)
        def _():
            pc_ref[...] = _dot_nt(h, wc_ref[...])

    return _pcall(
        body, carried, name="proj_bc", grid=(N_TILES,),
        in_specs=[pl.BlockSpec((TM, D), lambda i: (i, 0)), _VMEM_WHOLE, _VMEM_WHOLE],
        out_specs=[pl.BlockSpec((TM, WB), lambda i: (i, 0)), pl.BlockSpec((TM, WC), lambda i: (_lat(i), 0))],
        out_shape=[jax.ShapeDtypeStruct((T, WB), F32), jax.ShapeDtypeStruct((S, WC), F32)],
        scratch_shapes=[], operands=[h_all, w_b, w_c])


def _input_bwd(dp_a, dp_b, dp_c, w_a, w_b, w_c, ctx, x, dx1, nw, sh, sc, carried=None):
    def body(da_ref, db_ref, dc_ref, wa_ref, wb_ref, wc_ref, ctx_ref, x_ref, dx1_ref, nw_ref, sh_ref,
             sc_ref, gx_ref, acc_ref):
        i = pl.program_id(0)

        @pl.when(i == 0)
        def _():
            acc_ref[...] = jnp.zeros_like(acc_ref)

        dh = _dot(da_ref[...], wa_ref[...]) + _dot(db_ref[...], wb_ref[...])

        @pl.when(i == 0)
        def _():
            _, dsh, dsc, dnw = _rms_mod_bwd(ctx_ref[...], nw_ref[...], sc_ref[0:1, :], dh)
            acc_ref[3:4, :] += dsh
            acc_ref[4:5, :] += dsc
            acc_ref[2:3, :] += dnw

        @pl.when(i > 0)
        def _():
            dhl = dh + _dot(dc_ref[...], wc_ref[...])
            dx, dsh, dsc, dnw = _rms_mod_bwd(x_ref[...], nw_ref[...], sc_ref[1:2, :], dhl)
            gx_ref[...] = dx1_ref[...] + dx
            acc_ref[0:1, :] += dsh
            acc_ref[1:2, :] += dsc
            acc_ref[2:3, :] += dnw

    lat = lambda w: pl.BlockSpec((TM, w), lambda i: (_lat(i), 0))
    return _pcall(
        body, carried, name="input_bwd", grid=(N_TILES,),
        in_specs=[pl.BlockSpec((TM, WA), lambda i: (i, 0)), pl.BlockSpec((TM, WB), lambda i: (i, 0)),
                  lat(WC), _VMEM_WHOLE, _VMEM_WHOLE, _VMEM_WHOLE, _full((TM, D)), lat(D), lat(D),
                  _full((1, D)), _full((2, D)), _full((2, D))],
        out_specs=[lat(D), _full((8, D))],
        out_shape=[jax.ShapeDtypeStruct((S, D), F32), jax.ShapeDtypeStruct((8, D), F32)],
        scratch_shapes=[], operands=[dp_a, dp_b, dp_c, w_a, w_b, w_c, ctx, x, dx1, nw, sh, sc])


_C1 = 1.0 - ADAM_B1 ** ADAM_STEP
_C2 = 1.0 - ADAM_B2 ** ADAM_STEP


def _adamw_math(w, g, m, v):
    m = ADAM_B1 * m + (1.0 - ADAM_B1) * g
    v = ADAM_B2 * v + (1.0 - ADAM_B2) * (g * g)
    m_hat = m / _C1
    v_hat = v / _C2
    delta = -ADAM_LR * (m_hat / (jnp.sqrt(v_hat) + ADAM_EPS) + ADAM_WD * w)
    return delta, m, v


def _adamw_sharded(terms, w, m, v, name, tr, extra=None):
    rows, cols = w.shape

    def body(*refs):
        t_ref, w_ref, m_ref, v_ref = refs[:4]
        g_ref, d_ref, nm_ref, nv_ref = refs[-4:]
        g = t_ref[0].astype(F32)
        for s in range(1, N_CHIPS):
            g = g + t_ref[s].astype(F32)
        if extra is not None:
            g = g + refs[4][...].astype(F32)
        g_ref[...] = g
        d_ref[...], nm_ref[...], nv_ref[...] = _adamw_math(w_ref[...], g, m_ref[...], v_ref[...])

    blk = pl.BlockSpec((tr, cols), lambda i: (i, 0))
    return pl.pallas_call(
        body, name=name, grid=(rows // tr,),
        in_specs=[pl.BlockSpec((N_CHIPS, tr, cols), lambda i: (0, i, 0)), blk, blk, blk]
        + ([blk] if extra is not None else []),
        out_specs=[blk] * 4,
        out_shape=[jax.ShapeDtypeStruct((rows, cols), F32)] * 4,
        compiler_params=_cp(("parallel",)),
    )(terms, w, m, v, *([extra] if extra is not None else []))


def _adamw_plain(g, w, m, v, name):
    def body(g_ref, w_ref, m_ref, v_ref, d_ref, nm_ref, nv_ref):
        d_ref[...], nm_ref[...], nv_ref[...] = _adamw_math(w_ref[...], g_ref[...], m_ref[...], v_ref[...])

    return pl.pallas_call(
        body, name=name, in_specs=[_VMEM_WHOLE] * 4, out_specs=[_VMEM_WHOLE] * 3,
        out_shape=[jax.ShapeDtypeStruct(w.shape, F32)] * 3,
        compiler_params=_cp(),
    )(g, w, m, v)


SMALL_ROWS = 16
R_DMOD, R_DCTX, R_NMIX, R_NFFN, R_MISC, R_DLB, R_BADA01 = 0, 6, 8, 9, 10, 11, 13
M_HNW, M_QNW, M_KNW, M_SINK, M_LOSS = 0, 128, 256, 384, 512


def _pack_small(acc_in, acc_mg, acc_ffn, acc_prep, dsink, dlb):
    def body(in_ref, mg_ref, ff_ref, pp_ref, ds_ref, dlb_ref, o_ref):
        o_ref[...] = jnp.zeros_like(o_ref)
        o_ref[0:2, :] = in_ref[0:2, :]
        o_ref[2:3, :] = mg_ref[0:1, :]
        o_ref[3:5, :] = ff_ref[0:2, :]
        o_ref[5:6, :] = ff_ref[3:4, :]
        o_ref[6:8, :] = in_ref[3:5, :]
        o_ref[8:9, :] = in_ref[2:3, :]
        o_ref[9:10, :] = ff_ref[2:3, :]
        o_ref[10:11, M_HNW:M_HNW + HG_DIM] = pp_ref[0:1, 0:HG_DIM]
        r = lax.broadcasted_iota(jnp.int32, (ATW, 128), 0)
        c = lax.broadcasted_iota(jnp.int32, (ATW, 128), 1)
        fold = jnp.where((r % HEAD_DIM == c) & (c < HEAD_DIM), 1.0, 0.0).astype(BF16)
        qk = jnp.concatenate([pp_ref[1:2, :], pp_ref[2:3, :], jnp.zeros((6, ATW), F32)], axis=0)
        folded = _dot_exact_rhs01(qk, fold)
        o_ref[10:11, M_QNW:M_QNW + 128] = folded[0:1, :]
        o_ref[10:11, M_KNW:M_KNW + 128] = folded[1:2, :]
        o_ref[10:11, M_SINK:M_SINK + ATT_HEADS] = ds_ref[...]
        o_ref[10:11, M_LOSS:M_LOSS + 128] = ff_ref[4:5, 0:128]
        o_ref[11:13, 0:HGW] = dlb_ref[...]

    return pl.pallas_call(
        body, name="pack_small", in_specs=[_VMEM_WHOLE] * 6, out_specs=_VMEM_WHOLE,
        out_shape=jax.ShapeDtypeStruct((SMALL_ROWS, D), F32), compiler_params=_cp(),
    )(acc_in, acc_mg, acc_ffn, acc_prep, dsink, dlb)


def _sum_small(gathered):
    def body(g_ref, o_ref):
        tot = g_ref[0]
        for s in range(1, N_DEV):
            tot = tot + g_ref[s]
        o_ref[...] = tot
        o_ref[R_BADA01:R_BADA01 + 2, :] = tot[0:2, :] + tot[R_DCTX:R_DCTX + 2, :]

    return pl.pallas_call(
        body, name="sum_small", in_specs=[_VMEM_WHOLE], out_specs=_VMEM_WHOLE,
        out_shape=jax.ShapeDtypeStruct((SMALL_ROWS, D), F32), compiler_params=_cp(),
    )(gathered)


_REP_NAMES = ("b_ada", "c_ctx", "norm_mix_w", "norm_ffn_w", "hgrn_norm_w", "q_norm_w", "k_norm_w", "attn_sinks")


def _adamw_replicated(tot, g_c_ctx, ws, ms, vs):
    n = len(_REP_NAMES)

    def body(*refs):
        tot_ref, gc_ref = refs[0], refs[1]
        w_refs, m_refs, v_refs = refs[2:2 + n], refs[2 + n:2 + 2 * n], refs[2 + 2 * n:2 + 3 * n]
        outs = refs[2 + 3 * n:]
        row = lambda r: tot_ref[r:r + 1, :]
        misc = row(R_MISC)
        grads = [jnp.concatenate([row(R_BADA01), row(R_BADA01 + 1)] + [row(k) for k in range(2, 6)], axis=1),
                 gc_ref[...], row(R_NMIX), row(R_NFFN),
                 misc[:, M_HNW:M_HNW + HG_DIM], misc[:, M_QNW:M_QNW + HEAD_DIM],
                 misc[:, M_KNW:M_KNW + HEAD_DIM], misc[:, M_SINK:M_SINK + ATT_HEADS]]
        for k in range(n):
            outs[k][...] = grads[k]
            outs[n + k][...], outs[2 * n + k][...], outs[3 * n + k][...] = _adamw_math(
                w_refs[k][...], grads[k], m_refs[k][...], v_refs[k][...])

    shapes = [jax.ShapeDtypeStruct(w.shape, F32) for w in ws]
    return pl.pallas_call(
        body, name="adamw_replicated", in_specs=[_VMEM_WHOLE] * (2 + 3 * n), out_specs=[_VMEM_WHOLE] * (4 * n),
        out_shape=shapes * 4, compiler_params=_cp(),
    )(tot, g_c_ctx, *ws, *ms, *vs)


def _lb_grads(dlb, lbl):
    def body(d_ref, l_ref, o_ref):
        for d in (0, 1):
            ll = l_ref[d]
            lb = _sigmoid(ll[0:1, :] - ll[1:2, :])
            t = d_ref[d:d + 1, :] * lb * (1.0 - lb)
            o_ref[d, 0:1, :] = t
            o_ref[d, 1:2, :] = -t

    return pl.pallas_call(
        body, name="lb_grads", in_specs=[_VMEM_WHOLE] * 2, out_specs=_VMEM_WHOLE,
        out_shape=jax.ShapeDtypeStruct((2, 2, HGW), F32), compiler_params=_cp(),
    )(dlb, lbl)


def _c_ctx_grad(terms, c_ctx):
    def body(t_ref, c_ref, o_ref):
        tot = t_ref[0, 8:9, :]
        for s in range(1, N_DEV):
            tot = tot + t_ref[s, 8:9, :]
        cv = c_ref[...]
        sg = _sigmoid(cv)
        o_ref[...] = tot * (sg * (1.0 + cv * (1.0 - sg)))

    return pl.pallas_call(
        body, name="c_ctx_grad", in_specs=[_VMEM_WHOLE] * 2, out_specs=_VMEM_WHOLE,
        out_shape=jax.ShapeDtypeStruct((1, D), F32), compiler_params=_cp(),
    )(terms, c_ctx)


def _in_perm():
    fz, bz, inp, kk, vv, qhg, ghg, qat, gates = 0, 512, 1024, 1536, 1664, 1792, 2304, 2816, 3328
    cols = []
    for h in range(HG_HEADS):
        for base in (fz, bz, inp, qhg):
            cols += list(range(base + 128 * h, base + 128 * (h + 1)))
    cols += list(range(ghg, ghg + 512)) + list(range(qat, qat + 512))
    cols += list(range(kk, kk + 128)) + list(range(vv, vv + 128))
    cols += list(range(gates, gates + 2048))
    return np.asarray(cols, np.int32)


_PERM = _in_perm()


_PIECES = {"a": (0, WA, 128), "b": (WA, WB, 256), "c": (WA + WB, WC, 256)}


def _block_table(piece):
    lo, n, blk = _PIECES[piece]
    starts = [int(_PERM[r]) for r in range(lo, lo + n, blk)]
    assert all(s % blk == 0 and np.array_equal(_PERM[r:r + blk], np.arange(s, s + blk))
               for s, r in zip(starts, range(lo, lo + n, blk)))
    return jnp.asarray([s // blk for s in starts], jnp.int32), blk


def _pick_row_blocks(x, table, blk, name):
    cols = x.shape[1]

    def body(t_ref, x_ref, o_ref):
        o_ref[...] = x_ref[...]

    return pl.pallas_call(
        body, name=name,
        grid_spec=pltpu.PrefetchScalarGridSpec(
            num_scalar_prefetch=1, grid=(table.shape[0],),
            in_specs=[pl.BlockSpec((blk, cols), lambda i, t: (t[i], 0))],
            out_specs=pl.BlockSpec((blk, cols), lambda i, t: (i, 0))),
        out_shape=jax.ShapeDtypeStruct((table.shape[0] * blk, cols), x.dtype),
        compiler_params=_cp(("arbitrary",)),
    )(table, x)


def _place_row_blocks(x, table, blk, into, out_rows, name):
    cols = x.shape[1]

    def body(t_ref, x_ref, *rest):
        rest[-1][...] = x_ref[...]

    operands, in_specs, aliases = [table, x], [pl.BlockSpec((blk, cols), lambda i, t: (i, 0))], {}
    if into is not None:
        operands.append(into)
        in_specs.append(_ANY)
        aliases = {2: 0}
    return pl.pallas_call(
        body, name=name,
        grid_spec=pltpu.PrefetchScalarGridSpec(
            num_scalar_prefetch=1, grid=(table.shape[0],), in_specs=in_specs,
            out_specs=pl.BlockSpec((blk, cols), lambda i, t: (t[i], 0))),
        out_shape=jax.ShapeDtypeStruct((out_rows, cols), x.dtype),
        input_output_aliases=aliases,
        compiler_params=_cp(("arbitrary",)),
    )(*operands)


def _local_step(x2, ctx2, h_all, h_lat, tgt, lbl, sh_in, sc_in, gate1, sh2, sc2, gate2, norm_mix_w, norm_ffn_w,
                hgrn_norm_w, q_norm_w, k_norm_w, attn_sinks, w_a, w_b, w_c, s_bh, s_ba, s_out,
                s_gate, s_up, s_down):
    first_last = lambda n: [(0, True), (n - 1, False)]
    p_a = _mm_nt(h_all, w_a, tm=T, tn=512, out_dtype=F32, name="proj_a")
    (o, st), (g_gate, g_bh, g_ba) = _hgrn_fwd(
        p_a, lbl, (_gather_comm_relayed([s_gate, s_bh, s_ba]),
                   [(0, True), (HG_HEADS - 2, True), (HG_HEADS - 1, False)]))
    (p_b, p_c), (g_out,) = _proj_bc(
        h_all, w_b, w_c, (_gather_comm_relayed([s_out]), [(0, True), (N_TILES - 4, True), (N_TILES - 1, False)]))
    cos, sin = _rope_tables()
    qnw_t, knw_t = jnp.tile(q_norm_w, (1, ATT_HEADS)), jnp.tile(k_norm_w, (1, KV_HEADS))
    y_hg, qn, k_rep, v_rep = _prep_fwd(p_b, o, cos, sin, hgrn_norm_w, qnw_t, knw_t)
    (y_at, lse), (g_up, g_down) = _attn_fwd(
        qn, k_rep, v_rep, attn_sinks,
        (_gather_comm_relayed([s_up, s_down]), [(0, True), (N_BLOCKS - 6, True), (N_BLOCKS - 1, False)]))
    w_bh, w_ba, w_o = g_bh.reshape(D, HGW), g_ba.reshape(D, ATW), g_out.reshape(D, D)
    (mixed, r, x1, h2), _ = _merge_fwd(
        y_hg, y_at, p_c, x2, w_bh, w_ba, w_o, gate1, norm_ffn_w, sh2, sc2)
    g_gate, g_up, g_down = [g.reshape(N_FF_TILES, FF_TILE, D) for g in (g_gate, g_up, g_down)]

    act, d_gate, d_up, d_f, dx1, acc_ffn = _ffn_fused(x1, h2, tgt, g_gate, g_up, g_down, gate2,
                                                      norm_ffn_w, sc2)
    by_chip = lambda t: t.reshape((N_CHIPS, 2) + t.shape[1:])
    ff_by_chip = lambda t: t.reshape(N_CHIPS, 2, FF_BLK, D)
    t_down, _ = _mm_tn_blocked(act, d_f, "grad_down")
    t_down = ff_by_chip(t_down)
    t_gate, (f_down,) = _mm_tn_blocked(d_gate, h2, "grad_gate", (_sibling_comm([t_down]), first_last(N_FF_TILES)))
    t_gate = ff_by_chip(t_gate)
    t_up, (f_gate,) = _mm_tn_blocked(d_up, h2, "grad_up", (_sibling_comm([t_gate]), first_last(N_FF_TILES)))
    t_up = ff_by_chip(t_up)

    (d_r, d_a, d_b, dp_c, dy_hg, dy_at, acc_mg), (f_up,) = _merge_bwd(
        dx1, r, y_hg, y_at, p_c, w_bh, w_ba, w_o, gate1, (_sibling_comm([t_up]), first_last(N_LAT_TILES)))
    c_down, c_gate, c_up = [_pair_sum(t, f, "pair_sum_" + nm) for t, f, nm in
                            ((t_down, f_down, "down"), (t_gate, f_gate, "gate"), (t_up, f_up, "up"))]
    t_out = _mm_tn(mixed, d_r, tk=1024, nk=2, tm=1024, tn=1024, out_dtype=BF16, name="grad_out")
    t_bh = _mm_tn(d_a, y_hg, tk=2048, nk=1, tm=1024, tn=512, out_dtype=BF16, name="grad_bh")
    t_ba = _mm_tn(d_b, y_at, tk=2048, nk=1, tm=1024, tn=512, out_dtype=BF16, name="grad_ba")
    t_bh, t_ba, t_out = [by_chip(t.reshape(N_DEV, D // N_DEV, t.shape[1])) for t in (t_bh, t_ba, t_out)]
    (dq, dk_rep, dv_rep, dsink), (r_up,) = _attn_bwd(
        qn, k_rep, v_rep, attn_sinks, y_at, lse, dy_at, (_chip_comm([c_up]), first_last(N_BLOCKS)))
    (dp_b, d_o, acc_prep), (f_bh, f_ba, f_out) = _prep_bwd(
        p_b, o, cos, sin, hgrn_norm_w, qnw_t, knw_t, dy_hg, dq, dk_rep, dv_rep,
        (_sibling_comm([t_bh, t_ba, t_out]), first_last(N_TILES)))
    c_bh, c_ba, c_out = [_pair_sum(t, f, "pair_sum_" + nm) for t, f, nm in
                         ((t_bh, f_bh, "bh"), (t_ba, f_ba, "ba"), (t_out, f_out, "out"))]
    (dp_a, dlb), (r_bh, r_ba, r_out, r_down, r_gate) = _hgrn_bwd(
        p_a, lbl, d_o, st, (_chip_comm([c_bh, c_ba, c_out, c_down, c_gate]), first_last(HG_HEADS)))
    t_a = _mm_tn(dp_a, h_all, tk=T, nk=1, tm=1024, tn=1024, out_dtype=BF16, name="grad_in_a")
    t_b = _mm_tn(dp_b, h_all, tk=T, nk=1, tm=640, tn=1024, out_dtype=BF16, name="grad_in_b")
    t_c = _mm_tn(dp_c, h_lat, tk=1024, nk=2, tm=1024, tn=1024, out_dtype=BF16, name="grad_in_c")
    t_in = None
    for piece, nm in ((t_a, "a"), (t_b, "b"), (t_c, "c")):
        t_in = _place_row_blocks(piece, *_block_table(nm), t_in, IN_COLS, "order_terms_" + nm)
    t_in = by_chip(t_in.reshape(N_DEV, IN_BLK, D))
    (f_in,) = _run_comm(_sibling_comm([t_in]), "scatter_in_sibling")
    c_in = _pair_sum(t_in, f_in, "pair_sum_in")
    sems, c_in, land, token = _chip_exchange_start(c_in, jnp.zeros(c_in.shape, c_in.dtype))
    (grad_x, acc_in), _ = _input_bwd(dp_a, dp_b, dp_c, w_a, w_b, w_c, ctx2, x2, dx1,
                                     norm_mix_w + token[0, 0], sh_in, sc_in)
    small = _pack_small(acc_in, acc_mg, acc_ffn, acc_prep, dsink, dlb)
    return grad_x, small, [r_bh, r_ba, r_out, r_gate, r_up, r_down], (sems, c_in, land)


def kernel(x, c, ctx, c_ctx, w_ada, b_ada, norm_mix_w, norm_ffn_w, w_in, hgrn_lb_logits, hgrn_norm_w, q_norm_w, k_norm_w, attn_sinks, w_branch_hgrn, w_branch_attn, w_out, w_ffn_gate, w_ffn_up, w_ffn_down, loss_target, m_c_ctx, m_w_ada, m_b_ada, m_norm_mix_w, m_norm_ffn_w, m_w_in, m_hgrn_lb_logits, m_hgrn_norm_w, m_q_norm_w, m_k_norm_w, m_attn_sinks, m_w_branch_hgrn, m_w_branch_attn, m_w_out, m_w_ffn_gate, m_w_ffn_up, m_w_ffn_down, v_c_ctx, v_w_ada, v_b_ada, v_norm_mix_w, v_norm_ffn_w, v_w_in, v_hgrn_lb_logits, v_hgrn_norm_w, v_q_norm_w, v_k_norm_w, v_attn_sinks, v_w_branch_hgrn, v_w_branch_attn, v_w_out, v_w_ffn_gate, v_w_ffn_up, v_w_ffn_down):
    me = 4 * lax.axis_index("x") + 2 * lax.axis_index("y") + lax.axis_index("c")
    x2, ctx2, tgt = x[0], ctx[0], loss_target[0]
    w_ada2, w_in2 = w_ada[0], w_in[0]

    cond = jnp.zeros((8, D), F32).at[0].set(c[0]).at[1, :256].set(hgrn_lb_logits.reshape(256))
    b_cols = lax.dynamic_slice(b_ada, (0, me * ADA_BLK), (1, ADA_BLK))
    g0, cc, mod, g_in, h_all, h_lat = _prologue(cond, c_ctx.reshape(1, D), w_ada2, b_cols, w_in2.T.astype(BF16),
                                         x2, ctx2, norm_mix_w)
    lbl = jnp.transpose(g0[:, 1, :256].reshape(N_DEV, 2, 2, 64), (1, 2, 0, 3)).reshape(2, 2, HGW)
    sh1, sc1, gate1, sh2, sc2, gate2 = [mod[k:k + 1] for k in range(6)]
    sh_in = jnp.concatenate([mod[6:7], sh1], axis=0)
    sc_in = jnp.concatenate([mod[7:8], sc1], axis=0)

    shards = [w_branch_hgrn[0].T, w_branch_attn[0].T, w_out[0], w_ffn_gate[0].T, w_ffn_up[0].T, w_ffn_down[0]]
    w_in_t = g_in.reshape(IN_COLS, D)
    w_a, w_b, w_c = [_pick_row_blocks(w_in_t, *_block_table(nm), "order_w_" + nm) for nm in "abc"]

    grad_x, small, (r_bh, r_ba, r_out, r_gate, r_up, r_down), pending_in = _local_step(
        x2, ctx2, h_all, h_lat, tgt, lbl, sh_in, sc_in, gate1, sh2, sc2, gate2, norm_mix_w, norm_ffn_w, hgrn_norm_w,
        q_norm_w, k_norm_w, attn_sinks, w_a, w_b, w_c, *[s.astype(BF16) for s in shards])

    big = {}
    for nm, rr, ww, mm, vv, tr, transposed in (
            ("w_branch_hgrn", r_bh, w_branch_hgrn[0], m_w_branch_hgrn[0], v_w_branch_hgrn[0], 128, True),
            ("w_branch_attn", r_ba, w_branch_attn[0], m_w_branch_attn[0], v_w_branch_attn[0], 128, True),
            ("w_out", r_out, w_out[0], m_w_out[0], v_w_out[0], 128, False),
            ("w_ffn_gate", r_gate, w_ffn_gate[0], m_w_ffn_gate[0], v_w_ffn_gate[0], 352, True),
            ("w_ffn_up", r_up, w_ffn_up[0], m_w_ffn_up[0], v_w_ffn_up[0], 352, True),
            ("w_ffn_down", r_down, w_ffn_down[0], m_w_ffn_down[0], v_w_ffn_down[0], 352, False)):
        if transposed:
            res = _adamw_sharded(rr, ww.T, mm.T, vv.T, "adamw_" + nm, tr)
            big[nm] = [t.T[None] for t in res]
        else:
            big[nm] = [t[None] for t in _adamw_sharded(rr, ww, mm, vv, "adamw_" + nm, tr)]

    (g2,) = _all_gather([small], "gather_small", True)
    tot = _sum_small(g2)
    dm = jnp.zeros((16, 6 * D), F32).at[:8].set(g2[:, R_DMOD:R_DMOD + 6, :].reshape(N_DEV, 6 * D))
    dm = dm.at[8, :2 * D].set(tot[R_DCTX:R_DCTX + 2].reshape(2 * D))
    dm_cols = lax.dynamic_slice(dm, (0, me * ADA_BLK), (16, ADA_BLK))
    g_w_ada, dsc_term = _ada_grads(cc, dm_cols, w_ada2)
    (g3,) = _all_gather([dsc_term], "gather_cctx", True)
    g_c_ctx = _c_ctx_grad(g3, c_ctx.reshape(1, D))
    g_lbl = _lb_grads(tot[R_DLB:R_DLB + 2, :HGW], lbl)
    g_lb_mine = lax.dynamic_slice(g_lbl, (0, 0, me * 64), (2, 2, 64))
    misc = tot[R_MISC]
    loss = misc[M_LOSS]

    rep_out = _adamw_replicated(
        tot, g_c_ctx,
        [b_ada, c_ctx.reshape(1, D), norm_mix_w, norm_ffn_w, hgrn_norm_w, q_norm_w, k_norm_w, attn_sinks],
        [m_b_ada, m_c_ctx.reshape(1, D), m_norm_mix_w, m_norm_ffn_w, m_hgrn_norm_w, m_q_norm_w, m_k_norm_w,
         m_attn_sinks],
        [v_b_ada, v_c_ctx.reshape(1, D), v_norm_mix_w, v_norm_ffn_w, v_hgrn_norm_w, v_q_norm_w, v_k_norm_w,
         v_attn_sinks])
    rep = []
    for kind in range(4):
        vals = dict(zip(_REP_NAMES, rep_out[kind * len(_REP_NAMES):(kind + 1) * len(_REP_NAMES)]))
        vals["c_ctx"] = vals["c_ctx"].reshape(D)
        rep.append(vals)

    sems, c_in, land = pending_in
    d_ada, nm_ada, nv_ada = _adamw_plain(g_w_ada, w_ada2, m_w_ada[0], v_w_ada[0], "adamw_w_ada")
    land = _chip_exchange_wait(sems, c_in, land, d_ada)
    own = lax.dynamic_index_in_dim(c_in, 2 * lax.axis_index("x") + lax.axis_index("y"), 0, keepdims=False)
    big["w_in"] = [t.T[None] for t in _adamw_sharded(land, w_in2.T, m_w_in[0].T, v_w_in[0].T, "adamw_w_in", 336,
                                                     extra=own)]
    ada = [t[None] for t in (g_w_ada, d_ada, nm_ada, nv_ada)]
    lb_w = hgrn_lb_logits.reshape(4, 64)
    d_lb, nm_lb, nv_lb = _adamw_plain(g_lb_mine.reshape(4, 64), lb_w, m_hgrn_lb_logits.reshape(4, 64),
                                      v_hgrn_lb_logits.reshape(4, 64), "adamw_lb")
    lbs = [t.reshape(2, 2, 64) for t in (g_lb_mine, d_lb, nm_lb, nv_lb)]

    names = ['c_ctx', 'w_ada', 'b_ada', 'norm_mix_w', 'norm_ffn_w', 'w_in', 'hgrn_lb_logits', 'hgrn_norm_w',
             'q_norm_w', 'k_norm_w', 'attn_sinks', 'w_branch_hgrn', 'w_branch_attn', 'w_out', 'w_ffn_gate',
             'w_ffn_up', 'w_ffn_down']
    outs = [loss, grad_x[None]]
    for kind in range(4):
        for nm in names:
            if nm == 'w_ada':
                outs.append(ada[kind])
            elif nm == 'hgrn_lb_logits':
                outs.append(lbs[kind])
            elif nm in big:
                outs.append(big[nm][kind])
            else:
                outs.append(rep[kind][nm])
    return tuple(outs)
```

```python
import functools
import math

import numpy as np
import jax
import jax.numpy as jnp
from jax import lax
from jax.experimental import pallas as pl
from jax.experimental.pallas import tpu as pltpu

F32 = jnp.float32
BF16 = jnp.bfloat16

N_DEV = 8
D = 1024
S = 2048
L = 256
T = L + S
TM = 256
N_TILES = T // TM
N_LAT_TILES = S // TM
HG_HEADS = 4
HG_DIM = 128
HGW = 512
CHUNK = 32
N_CHUNKS = T // CHUNK
N_CTX_CHUNKS = L // CHUNK
ATT_HEADS = 8
KV_HEADS = 2
HEAD_DIM = 64
ATW = 512
KVW = 128
BLOCK = 128
N_BLOCKS = S // BLOCK
GRID_W = 64
ROPE_THETA = 10000.0
D_FF = 2816
FF_BLK = D_FF // N_DEV
FF_TILE = 256
N_FF_TILES = D_FF // FF_TILE
IN_COLS = 5376
IN_BLK = IN_COLS // N_DEV
ADA_BLK = 6 * D // N_DEV
EPS = 1e-6
WA, WB, WC = 2048, 1280, 2048

ADAM_LR = 0.001
ADAM_B1 = 0.9
ADAM_B2 = 0.999
ADAM_EPS = 1e-08
ADAM_WD = 0.01
ADAM_STEP = 10

VMEM_LIMIT = 56 * 1024 * 1024
MESH = pl.DeviceIdType.MESH


def _cp(sem=None, vmem=VMEM_LIMIT):
    return pltpu.CompilerParams(dimension_semantics=sem, vmem_limit_bytes=vmem)


def _full(shape):
    n = len(shape)
    return pl.BlockSpec(shape, lambda *_: (0,) * n)


_VMEM_WHOLE = pl.BlockSpec(memory_space=pltpu.VMEM)
_ANY = pl.BlockSpec(memory_space=pl.ANY)


def _sigmoid(v):
    return 1.0 / (1.0 + jnp.exp(-v))


def _dot(a, b):
    return jnp.dot(a, b, preferred_element_type=F32)


def _dot_nt(a, b):
    return lax.dot_general(a, b, (((1,), (1,)), ((), ())), preferred_element_type=F32)


def _dot_tn(a, b):
    return lax.dot_general(a, b, (((0,), (0,)), ((), ())), preferred_element_type=F32)


def _split3(v):
    hi = v.astype(BF16)
    r = v - hi.astype(F32)
    mid = r.astype(BF16)
    lo = (r - mid.astype(F32)).astype(BF16)
    return hi, mid, lo


def _dot_exact_rhs01(v, m01):
    hi, mid, lo = _split3(v)
    return _dot(hi, m01) + _dot(mid, m01) + _dot(lo, m01)


def _split2(v):
    hi = v.astype(BF16)
    return hi, (v - hi.astype(F32)).astype(BF16)


def _dot_lhs01(m01, v):
    hi, lo = _split2(v)
    return _dot(m01, hi) + _dot(m01, lo)


def _dot_f32(a, b, dot=_dot):
    ah, am, al = _split3(a)
    bh, bm, bl = _split3(b)
    return (dot(ah, bh) + (dot(ah, bm) + dot(am, bh))
            + (dot(am, bm) + dot(ah, bl) + dot(al, bh)))


def _my_pos():
    return lax.axis_index("x"), lax.axis_index("y"), lax.axis_index("c")


class _Comm:
    def __init__(self, operands, out_shapes, sems, phases):
        self.operands, self.out_shapes, self.sems, self.phases = operands, out_shapes, sems, phases


def _gather_comm(blocks):
    n = len(blocks)

    def parts(ins, outs, sems):
        send_sems, recv_sems, local_sems = sems
        x, y, c = _my_pos()
        me, sibling = (x, y, c), (x, y, 1 - c)
        chips = [(1 - x, y), (x, 1 - y), (1 - x, 1 - y)]

        def slot(a, px, py, pc):
            return outs[a].at[4 * px + 2 * py + pc]

        def copy(a, k, block, to, src=None):
            return pltpu.make_async_remote_copy(
                src_ref=slot(a, *block) if src is None else src, dst_ref=slot(a, *block),
                send_sem=send_sems.at[a, k], recv_sem=recv_sems.at[a, k],
                device_id=to, device_id_type=MESH)

        mine = [pltpu.make_async_copy(ins[a], slot(a, *me), local_sems.at[a]) for a in range(n)]
        first = []
        for a in range(n):
            first.append(copy(a, 0, me, sibling, src=ins[a]))
            first += [copy(a, 1 + j, me, (*chip, c), src=ins[a]) for j, chip in enumerate(chips)]
        passed = [copy(a, 4 + j, (*chip, c), sibling) for j, chip in enumerate(chips) for a in range(n)]
        return c, me, sibling, chips, copy, mine, first, passed

    def start(ins, outs, sems):
        _, _, _, _, _, mine, first, _ = parts(ins, outs, sems)
        for cp in mine + first:
            cp.start()

    def forward(ins, outs, sems):
        c, me, _, chips, copy, _, _, passed = parts(ins, outs, sems)
        for j, chip in enumerate(chips):
            for a in range(n):
                copy(a, 1 + j, (*chip, c), me).wait_recv()
                passed[j * n + a].start()

    def finish(ins, outs, sems):
        c, me, sibling, chips, copy, mine, first, passed = parts(ins, outs, sems)
        for a in range(n):
            copy(a, 0, sibling, me).wait_recv()
            for j, chip in enumerate(chips):
                copy(a, 4 + j, (*chip, 1 - c), me).wait_recv()
        for cp in first + passed:
            cp.wait_send()
        for cp in mine:
            cp.wait()

    return _Comm(blocks, [jax.ShapeDtypeStruct((N_DEV,) + b.shape, b.dtype) for b in blocks],
                 [pltpu.SemaphoreType.DMA((n, 7)), pltpu.SemaphoreType.DMA((n, 7)), pltpu.SemaphoreType.DMA((n,))],
                 [start, forward, finish])


def _gather_comm_relayed(blocks):
    n = len(blocks)

    def parts(ins, outs, sems):
        send_sems, recv_sems, local_sems = sems
        x, y, c = _my_pos()
        me, sibling = (x, y, c), (x, y, 1 - c)
        x_nbr, y_nbr, diag = (1 - x, y, c), (x, 1 - y, c), (1 - x, 1 - y, c)

        def slot(a, dev, half=None):
            ref = outs[a].at[4 * dev[0] + 2 * dev[1] + dev[2]]
            if half is None:
                return ref
            rows = blocks[a].shape[0] // 2
            return ref.at[pl.ds(half * rows, rows)]

        def copy(a, k, block, to, half=None, src=None):
            return pltpu.make_async_remote_copy(
                src_ref=slot(a, block, half) if src is None else src, dst_ref=slot(a, block, half),
                send_sem=send_sems.at[a, k], recv_sem=recv_sems.at[a, k],
                device_id=to, device_id_type=MESH)

        mine = [pltpu.make_async_copy(ins[a], slot(a, me), local_sems.at[a]) for a in range(n)]
        return me, sibling, x_nbr, y_nbr, diag, copy, mine

    def start(ins, outs, sems):
        me, sibling, x_nbr, y_nbr, _, copy, mine = parts(ins, outs, sems)
        for cp in mine:
            cp.start()
        for a in range(n):
            for k, to in ((1, x_nbr), (2, y_nbr), (0, sibling)):
                copy(a, k, me, to, src=ins[a]).start()

    def forward(ins, outs, sems):
        me, sibling, x_nbr, y_nbr, _, copy, _ = parts(ins, outs, sems)
        for a in range(n):
            copy(a, 1, x_nbr, me).wait_recv()
            copy(a, 3, x_nbr, y_nbr, half=0).start()
            copy(a, 5, x_nbr, sibling).start()
        for a in range(n):
            copy(a, 2, y_nbr, me).wait_recv()
            copy(a, 4, y_nbr, x_nbr, half=1).start()
            copy(a, 6, y_nbr, sibling).start()

    def finish(ins, outs, sems):
        me, sibling, x_nbr, y_nbr, diag, copy, mine = parts(ins, outs, sems)
        sib = lambda dev: (dev[0], dev[1], sibling[2])
        for a in range(n):
            copy(a, 3, diag, me, half=0).wait_recv()
            copy(a, 4, diag, me, half=1).wait_recv()
            copy(a, 7, diag, sibling).start()
        for a in range(n):
            copy(a, 0, sibling, me).wait_recv()
            for k, dev in ((5, x_nbr), (6, y_nbr), (7, diag)):
                copy(a, k, sib(dev), me).wait_recv()
        for a in range(n):
            for k, block, to, half in ((0, me, sibling, None), (1, me, x_nbr, None), (2, me, y_nbr, None),
                                       (3, x_nbr, y_nbr, 0), (4, y_nbr, x_nbr, 1), (5, x_nbr, sibling, None),
                                       (6, y_nbr, sibling, None), (7, diag, sibling, None)):
                copy(a, k, block, to, half=half, src=ins[a] if block is me else None).wait_send()
        for cp in mine:
            cp.wait()

    return _Comm(blocks, [jax.ShapeDtypeStruct((N_DEV,) + b.shape, b.dtype) for b in blocks],
                 [pltpu.SemaphoreType.DMA((n, 8)), pltpu.SemaphoreType.DMA((n, 8)), pltpu.SemaphoreType.DMA((n,))],
                 [start, forward, finish])


_HBM = pl.BlockSpec(memory_space=pltpu.HBM)
_SEM = pl.BlockSpec(memory_space=pltpu.SEMAPHORE)
_SPLIT_COPY = pltpu.CompilerParams(has_side_effects=pltpu.SideEffectType.DATAFLOW_SIDE_EFFECTING)


def _chip_exchange_copies(src_ref, land_ref, sems):
    x, y, c = _my_pos()
    q_me = 2 * x + y
    pairs = []
    for j, (px, py) in enumerate([(1 - x, y), (x, 1 - y), (1 - x, 1 - y)]):
        q = 2 * px + py
        send = pltpu.make_async_remote_copy(
            src_ref=src_ref.at[q], dst_ref=land_ref.at[q_me], send_sem=sems[j], recv_sem=sems[3 + j],
            device_id=(px, py, c), device_id_type=MESH)
        recv = pltpu.make_async_remote_copy(
            src_ref=src_ref.at[q], dst_ref=land_ref.at[q], send_sem=sems[j], recv_sem=sems[3 + j],
            device_id=(x, y, c), device_id_type=MESH)
        pairs.append((send, recv))
    return pairs


def _chip_exchange_start(src, land):
    def body(src_ref, land_ref, *outs):
        sems, token = outs[:6], outs[8]
        for send, _ in _chip_exchange_copies(src_ref, land_ref, sems):
            send.start()
        token[...] = jnp.zeros_like(token)

    res = pl.pallas_call(
        body, name="scatter_in_start",
        out_shape=(pltpu.SemaphoreType.DMA(()),) * 6 + (
            pltpu.HBM(src.shape, src.dtype), pltpu.HBM(land.shape, land.dtype),
            jax.ShapeDtypeStruct((8, 128), F32)),
        in_specs=(_HBM, _HBM), out_specs=(_SEM,) * 6 + (_HBM, _HBM, pl.BlockSpec(memory_space=pltpu.VMEM)),
        input_output_aliases={0: 6, 1: 7}, compiler_params=_SPLIT_COPY,
    )(pltpu.with_memory_space_constraint(src, pltpu.HBM), pltpu.with_memory_space_constraint(land, pltpu.HBM))
    return res[:6], res[6], res[7], res[8]


def _chip_exchange_wait(sems, src_thru, land_thru, after):
    def body(src_ref, land_ref, *rest):
        for send, recv in _chip_exchange_copies(src_ref, land_ref, rest[:6]):
            send.wait_send()
            recv.wait_recv()

    return pl.pallas_call(
        body, name="scatter_in_wait",
        out_shape=(pltpu.HBM(src_thru.shape, src_thru.dtype), pltpu.HBM(land_thru.shape, land_thru.dtype)),
        in_specs=(_HBM, _HBM) + (_SEM,) * 6 + (_ANY,), out_specs=(_HBM, _HBM),
        input_output_aliases={0: 0, 1: 1}, compiler_params=_SPLIT_COPY,
    )(src_thru, land_thru, *sems, after)[1]


def _run_comm(comm, name, in_vmem=False):
    n_in, n_out = len(comm.operands), len(comm.out_shapes)

    def body(*refs):
        ins, outs, sems = refs[:n_in], refs[n_in:n_in + n_out], refs[n_in + n_out:]
        for phase in comm.phases:
            phase(ins, outs, sems)

    spec = _VMEM_WHOLE if in_vmem else _ANY
    return pl.pallas_call(
        body, name=name, out_shape=comm.out_shapes, in_specs=[spec] * n_in, out_specs=[spec] * n_out,
        scratch_shapes=comm.sems,
    )(*comm.operands)


def _carrier_call(body, comm, schedule, *, name, grid, in_specs, out_specs, out_shape, scratch_shapes, operands):
    n_in, n_out, n_scr = len(in_specs), len(out_specs), len(scratch_shapes)
    c_in, c_out = len(comm.operands), len(comm.out_shapes)

    def full_body(*refs):
        ins, refs = refs[:n_in], refs[n_in:]
        cins, refs = refs[:c_in], refs[c_in:]
        outs, refs = refs[:n_out], refs[n_out:]
        couts, refs = refs[:c_out], refs[c_out:]
        scr, csems = refs[:n_scr], refs[n_scr:]
        step = pl.program_id(0)

        def run(before):
            for (at, when_before), phase in zip(schedule, comm.phases):
                if when_before == before:
                    pl.when(step == at)(functools.partial(phase, cins, couts, csems))

        run(True)
        body(*ins, *outs, *scr)
        run(False)

    res = pl.pallas_call(
        full_body, name=name, grid=grid,
        in_specs=list(in_specs) + [_ANY] * c_in, out_specs=list(out_specs) + [_ANY] * c_out,
        out_shape=list(out_shape) + list(comm.out_shapes),
        scratch_shapes=list(scratch_shapes) + list(comm.sems),
        compiler_params=_cp(("arbitrary",)),
    )(*operands, *comm.operands)
    return res[:n_out], res[n_out:]


def _pcall(body, carried, *, name, grid, in_specs, out_specs, out_shape, scratch_shapes, operands):
    if carried is None:
        res = pl.pallas_call(body, name=name, grid=grid, in_specs=in_specs, out_specs=out_specs,
                             out_shape=out_shape, scratch_shapes=scratch_shapes,
                             compiler_params=_cp(("arbitrary",)))(*operands)
        return res, ()
    return _carrier_call(body, carried[0], carried[1], name=name, grid=grid, in_specs=in_specs,
                         out_specs=out_specs, out_shape=out_shape, scratch_shapes=scratch_shapes,
                         operands=operands)


def _all_gather(blocks, name, in_vmem):
    return _run_comm(_gather_comm(blocks), name, in_vmem)


N_CHIPS = 4


def _sibling_comm(contribs):
    n = len(contribs)

    def copies(ins, outs, sems):
        send_sems, recv_sems = sems
        x, y, c = _my_pos()
        return [pltpu.make_async_remote_copy(
            src_ref=ins[a].at[pl.ds(0, N_CHIPS), 1 - c], dst_ref=outs[a],
            send_sem=send_sems.at[a], recv_sem=recv_sems.at[a],
            device_id=(x, y, 1 - c), device_id_type=MESH) for a in range(n)]

    def start(ins, outs, sems):
        for cp in copies(ins, outs, sems):
            cp.start()

    def finish(ins, outs, sems):
        cps = copies(ins, outs, sems)
        for cp in cps:
            cp.wait_recv()
        for cp in cps:
            cp.wait_send()

    return _Comm(contribs, [jax.ShapeDtypeStruct((N_CHIPS,) + b.shape[2:], b.dtype) for b in contribs],
                 [pltpu.SemaphoreType.DMA((n,)), pltpu.SemaphoreType.DMA((n,))], [start, finish])


def _pair_sum(mine, theirs, name):
    _, _, rows, cols = mine.shape
    core = lax.axis_index("c").astype(jnp.int32).reshape(1)

    def body(c_ref, m_ref, t_ref, o_ref):
        o_ref[...] = (m_ref[...].astype(F32) + t_ref[...].astype(F32)).astype(BF16)

    return pl.pallas_call(
        body, name=name,
        grid_spec=pltpu.PrefetchScalarGridSpec(
            num_scalar_prefetch=1, grid=(N_CHIPS,),
            in_specs=[pl.BlockSpec((None, None, rows, cols), lambda q, c: (q, c[0], 0, 0)),
                      pl.BlockSpec((None, rows, cols), lambda q, c: (q, 0, 0))],
            out_specs=pl.BlockSpec((None, rows, cols), lambda q, c: (q, 0, 0))),
        out_shape=jax.ShapeDtypeStruct((N_CHIPS, rows, cols), BF16),
        compiler_params=_cp(("parallel",)),
    )(core, mine, theirs)


def _chip_comm(sums):
    n = len(sums)

    def parts(ins, outs, sems):
        send_sems, recv_sems, local_sems = sems
        x, y, c = _my_pos()
        q_me = 2 * x + y
        chips = [(1 - x, y), (x, 1 - y), (1 - x, 1 - y)]
        mine = [pltpu.make_async_copy(ins[a].at[q_me], outs[a].at[q_me], local_sems.at[a]) for a in range(n)]
        sends, recvs = [], []
        for j, (px, py) in enumerate(chips):
            for a in range(n):
                q = 2 * px + py
                sends.append(pltpu.make_async_remote_copy(
                    src_ref=ins[a].at[q], dst_ref=outs[a].at[q_me],
                    send_sem=send_sems.at[a, j], recv_sem=recv_sems.at[a, j],
                    device_id=(px, py, c), device_id_type=MESH))
                recvs.append(pltpu.make_async_remote_copy(
                    src_ref=ins[a].at[q], dst_ref=outs[a].at[q],
                    send_sem=send_sems.at[a, j], recv_sem=recv_sems.at[a, j],
                    device_id=(x, y, c), device_id_type=MESH))
        return mine, sends, recvs

    def start(ins, outs, sems):
        mine, sends, _ = parts(ins, outs, sems)
        for cp in mine + sends:
            cp.start()

    def finish(ins, outs, sems):
        mine, sends, recvs = parts(ins, outs, sems)
        for cp in recvs:
            cp.wait_recv()
        for cp in sends:
            cp.wait_send()
        for cp in mine:
            cp.wait()

    return _Comm(sums, [jax.ShapeDtypeStruct(b.shape, b.dtype) for b in sums],
                 [pltpu.SemaphoreType.DMA((n, 3)), pltpu.SemaphoreType.DMA((n, 3)), pltpu.SemaphoreType.DMA((n,))],
                 [start, finish])


def _mm_nt(a, bt, *, tm, tn, out_dtype, name, row_off=0, rows=None):
    rows = a.shape[0] if rows is None else rows
    n, k = bt.shape

    def body(a_ref, b_ref, o_ref):
        o_ref[...] = _dot_nt(a_ref[...], b_ref[...]).astype(out_dtype)

    return pl.pallas_call(
        body, name=name, grid=(rows // tm, n // tn),
        in_specs=[pl.BlockSpec((tm, k), lambda i, j: (i + row_off, 0)),
                  pl.BlockSpec((tn, k), lambda i, j: (j, 0))],
        out_specs=pl.BlockSpec((tm, tn), lambda i, j: (i, j)),
        out_shape=jax.ShapeDtypeStruct((rows, n), out_dtype),
        compiler_params=_cp(("parallel", "parallel")),
    )(a, bt)


def _mm_tn(a, b, *, tk, nk, tm, tn, out_dtype, name, a_off=0, b_off=0):
    m, n = a.shape[1], b.shape[1]

    def body(a_ref, b_ref, o_ref, acc):
        kk = pl.program_id(2)

        @pl.when(kk == 0)
        def _():
            acc[...] = jnp.zeros_like(acc)

        acc[...] += _dot_tn(a_ref[...], b_ref[...])

        @pl.when(kk == nk - 1)
        def _():
            o_ref[...] = acc[...].astype(out_dtype)

    return pl.pallas_call(
        body, name=name, grid=(m // tm, n // tn, nk),
        in_specs=[pl.BlockSpec((tk, tm), lambda i, j, kk: (kk + a_off, i)),
                  pl.BlockSpec((tk, tn), lambda i, j, kk: (kk + b_off, j))],
        out_specs=pl.BlockSpec((tm, tn), lambda i, j, kk: (i, j)),
        out_shape=jax.ShapeDtypeStruct((m, n), out_dtype),
        scratch_shapes=[pltpu.VMEM((tm, tn), F32)],
        compiler_params=_cp(("parallel", "parallel", "arbitrary")),
    )(a, b)


def _mm_tn_blocked(a, b, name, carried=None):
    nb, _, w = a.shape
    n = b.shape[1]

    def body(a_ref, b_ref, o_ref):
        o_ref[...] = _dot_tn(a_ref[...], b_ref[...]).astype(BF16)

    (out,), extra = _pcall(
        body, carried, name=name, grid=(nb,),
        in_specs=[pl.BlockSpec((None, S, w), lambda j: (j, 0, 0)), _full((S, n))],
        out_specs=[pl.BlockSpec((None, w, n), lambda j: (j, 0, 0))],
        out_shape=[jax.ShapeDtypeStruct((nb, w, n), BF16)],
        scratch_shapes=[], operands=[a, b])
    return out, extra


def _prologue(cond, c_ctx, w_ada, b_cols, w_in_t, x, ctx, nw):
    rows_shape = jax.ShapeDtypeStruct((16, ADA_BLK), F32)
    big, g_cond, g_mod = _gather_comm_relayed([w_in_t]), _gather_comm([cond]), _gather_comm([rows_shape])

    def body(cond_ref, cctx_ref, wada_ref, b_ref, nw_ref, win_ref, x_ref, ctx_ref,
             g0_ref, cc_ref, mod_ref, gin_ref, h_ref, hl_ref, rows_ref, g1_ref, x_s, ctx_s, h_s, io_sems, *sems):
        s_big, s_cond, s_mod = sems[0:3], sems[3:6], sems[6:9]
        big.phases[0]([win_ref], [gin_ref], s_big)
        load_x = pltpu.make_async_copy(x_ref, x_s, io_sems.at[0])
        load_ctx = pltpu.make_async_copy(ctx_ref, ctx_s, io_sems.at[1])
        load_x.start()
        load_ctx.start()
        for phase in g_cond.phases:
            phase([cond_ref], [g0_ref], s_cond)
        cc_ref[...] = jnp.zeros_like(cc_ref)
        for j in range(N_DEV):
            cc_ref[j:j + 1, :] = g0_ref[j, 0:1, :]
        cc_ref[N_DEV:N_DEV + 1, :] = cctx_ref[...]
        cv = cc_ref[...]
        rows_ref[...] = _dot_f32(cv * _sigmoid(cv), wada_ref[...]) + b_ref[...]
        for phase in g_mod.phases:
            phase([rows_ref], [g1_ref], s_mod)
        x_pos, y_pos, c_pos = _my_pos()
        me = 4 * x_pos + 2 * y_pos + c_pos
        mine = jnp.concatenate([g1_ref[j, pl.ds(me, 1), :] for j in range(N_DEV)], axis=1)
        shared = jnp.concatenate([g1_ref[j, N_DEV:N_DEV + 1, :] for j in range(N_DEV)], axis=1)
        for k in range(6):
            mod_ref[k:k + 1, :] = mine[:, k * D:(k + 1) * D]
        mod_ref[6:7, :] = shared[:, 0:D]
        mod_ref[7:8, :] = shared[:, D:2 * D]
        load_ctx.wait()
        load_x.wait()
        h_s[pl.ds(0, L), :] = _rms_mod(ctx_s[...], nw_ref[...], mod_ref[6:7, :], mod_ref[7:8, :]).astype(BF16)

        def norm_tile(i, carry):
            r0 = pl.multiple_of(i * TM, TM)
            h_s[pl.ds(L + r0, TM), :] = _rms_mod(
                x_s[pl.ds(r0, TM), :], nw_ref[...], mod_ref[0:1, :], mod_ref[1:2, :]).astype(BF16)
            return carry

        lax.fori_loop(0, N_LAT_TILES, norm_tile, 0)
        stores = [pltpu.make_async_copy(h_s, h_ref, io_sems.at[2]),
                  pltpu.make_async_copy(h_s.at[pl.ds(L, S)], hl_ref, io_sems.at[3])]
        for cp in stores:
            cp.start()
        big.phases[1]([win_ref], [gin_ref], s_big)
        big.phases[2]([win_ref], [gin_ref], s_big)
        for cp in stores:
            cp.wait()

    return pl.pallas_call(
        body, name="prologue",
        in_specs=[_VMEM_WHOLE] * 5 + [_ANY] * 3, out_specs=[_VMEM_WHOLE] * 3 + [_ANY] * 3,
        out_shape=[g_cond.out_shapes[0], jax.ShapeDtypeStruct((16, D), F32), jax.ShapeDtypeStruct((8, D), F32),
                   big.out_shapes[0], jax.ShapeDtypeStruct((T, D), BF16), jax.ShapeDtypeStruct((S, D), BF16)],
        scratch_shapes=[pltpu.VMEM((16, ADA_BLK), F32), pltpu.VMEM((N_DEV, 16, ADA_BLK), F32),
                        pltpu.VMEM((S, D), F32), pltpu.VMEM((L, D), F32), pltpu.VMEM((T, D), BF16),
                        pltpu.SemaphoreType.DMA((4,))] + big.sems + g_cond.sems + g_mod.sems,
        compiler_params=_cp(),
    )(cond, c_ctx, w_ada, b_cols, nw, w_in_t, x, ctx)


def _ada_grads(cc, dm_cols, w_ada):
    def body(c_ref, dm_ref, w_ref, gw_ref, dsc_ref):
        cv = c_ref[...]
        sc = cv * _sigmoid(cv)
        dm = dm_ref[...]
        gw_ref[...] = _dot_f32(sc, dm, dot=_dot_tn)
        dsc_ref[...] = _dot_f32(dm, w_ref[...], dot=_dot_nt)

    return pl.pallas_call(
        body, name="ada_grads",
        in_specs=[_VMEM_WHOLE] * 3, out_specs=[_VMEM_WHOLE] * 2,
        out_shape=[jax.ShapeDtypeStruct((D, ADA_BLK), F32), jax.ShapeDtypeStruct((16, D), F32)],
        compiler_params=_cp(),
    )(cc, dm_cols, w_ada)


def _lat(i):
    return jnp.maximum(i - 1, 0)


def _rms_mod(xv, nw, sh, sc):
    rstd = lax.rsqrt(jnp.mean(xv * xv, axis=-1, keepdims=True) + EPS)
    return (xv * rstd * nw) * (1.0 + sc) + sh


def _rms_mod_bwd(xv, nw, sc, dh):
    rstd = lax.rsqrt(jnp.mean(xv * xv, axis=-1, keepdims=True) + EPS)
    xhat = xv * rstd
    dn = dh * (1.0 + sc)
    dxhat = dn * nw
    dx = rstd * (dxhat - xhat * jnp.mean(dxhat * xhat, axis=-1, keepdims=True))
    return (dx, jnp.sum(dh, axis=0, keepdims=True), jnp.sum(dh * (xhat * nw), axis=0, keepdims=True),
            jnp.sum(dn * xhat, axis=0, keepdims=True))


def _chunk_masks(reverse):
    row = lax.broadcasted_iota(jnp.int32, (TM, TM), 0)
    col = lax.broadcasted_iota(jnp.int32, (TM, TM), 1)
    same = (row // CHUNK) == (col // CHUNK)
    tri = same & ((col >= row) if reverse else (col <= row))
    return same, tri


def _chunk_order(i, reverse):
    if not reverse:
        return i
    return jnp.where(i < N_CTX_CHUNKS, N_CTX_CHUNKS - 1 - i, N_CHUNKS + N_CTX_CHUNKS - 1 - i)


def _decay_terms(z, lb, same01, tri01):
    f = lb + (1.0 - lb) * _sigmoid(z)
    g = jnp.log(f)
    g2 = jnp.concatenate(_split2(g), axis=1)
    b2 = _dot(tri01, g2)
    t2 = _dot(same01, g2)
    return f, 1.0 - f, b2[:, :HG_DIM] + b2[:, HG_DIM:], t2[:, :HG_DIM] + t2[:, HG_DIM:]


def _chunk_outer(a, b):
    n = TM // CHUNK
    return jnp.einsum('ncv,nck->nvk', a.reshape(n, CHUNK, HG_DIM), b.reshape(n, CHUNK, HG_DIM),
                      preferred_element_type=F32)


def _hgrn_fwd(p_a, lbl, carried=None):
    cpt = TM // CHUNK

    def body(p_ref, lbl_ref, o_ref, st_ref, qd_s, kd_s, u_s, v_s, ebt_s):
        masks = [_chunk_masks(d == 1) for d in (0, 1)]
        same01 = jnp.where(masks[0][0], 1.0, 0.0).astype(BF16)
        tri = [m[1] for m in masks]
        tri01 = [jnp.where(t, 1.0, 0.0).astype(BF16) for t in tri]
        lb = [_sigmoid(lbl_ref[d][0:1, :] - lbl_ref[d][1:2, :]) for d in (0, 1)]

        def prep(r, carry):
            r0 = pl.multiple_of(r * TM, TM)
            vb = p_ref[pl.ds(r0, TM), 2 * HG_DIM:3 * HG_DIM].astype(BF16)
            v_s[pl.ds(r0, TM), :] = vb
            for d in (0, 1):
                z = p_ref[pl.ds(r0, TM), d * HG_DIM:(d + 1) * HG_DIM]
                _, k, b, bt = _decay_terms(z, lb[d], same01, tri01[d])
                u_s[d, pl.ds(r * cpt, cpt)] = _chunk_outer(vb, (k * jnp.exp(bt - b)).astype(BF16))
                ebt_s[d, pl.ds(r0, TM), :] = jnp.exp(bt)

                @pl.when(r >= 1)
                def _():
                    rl = pl.multiple_of(r0 - L, TM)
                    qr = p_ref[pl.ds(r0, TM), 3 * HG_DIM:4 * HG_DIM]
                    q = qr * _sigmoid(qr) * HG_DIM ** -0.5
                    qd_s[d, pl.ds(rl, TM), :] = (q * jnp.exp(b)).astype(BF16)
                    kd_s[d, pl.ds(rl, TM), :] = (k * jnp.exp(-b)).astype(BF16)

            return carry

        lax.fori_loop(0, N_TILES, prep, 0)

        def scan(i, sts):
            new = []
            for d in (0, 1):
                nn = _chunk_order(i, d == 1)
                c0 = pl.multiple_of(nn * CHUNK, CHUNK)
                st_ref[d, nn] = sts[d].astype(BF16)
                new.append(sts[d] * ebt_s[d, pl.ds(c0, 1), :] + u_s[d, nn])
            return tuple(new)

        zero = jnp.zeros((HG_DIM, HG_DIM), F32)
        lax.fori_loop(0, N_CHUNKS, scan, (zero, zero))

        def outp(r, carry):
            r0 = pl.multiple_of(r * TM, TM)
            vb = v_s[pl.ds(r0 + L, TM), :]
            o = jnp.zeros((TM, HG_DIM), F32)
            for d in (0, 1):
                qd = qd_s[d, pl.ds(r0, TM), :]
                a = jnp.where(tri[d], _dot_nt(qd, kd_s[d, pl.ds(r0, TM), :]), 0.0)
                stb = st_ref[d, pl.ds(N_CTX_CHUNKS + r * cpt, cpt)]
                inter = jnp.einsum('nck,nvk->ncv', qd.reshape(cpt, CHUNK, HG_DIM), stb,
                                   preferred_element_type=F32)
                o = o + _dot(a.astype(BF16), vb) + inter.reshape(TM, HG_DIM)
            o_ref[pl.ds(r0, TM), :] = o
            return carry

        lax.fori_loop(0, N_LAT_TILES, outp, 0, unroll=2)

    return _pcall(
        body, carried, name="hgrn_fwd", grid=(HG_HEADS,),
        in_specs=[pl.BlockSpec((T, 4 * HG_DIM), lambda h: (0, h)),
                  pl.BlockSpec((2, 2, HG_DIM), lambda h: (0, 0, h))],
        out_specs=[pl.BlockSpec((S, HG_DIM), lambda h: (0, h)),
                   pl.BlockSpec((2, None, N_CHUNKS, HG_DIM, HG_DIM), lambda h: (0, h, 0, 0, 0))],
        out_shape=[jax.ShapeDtypeStruct((S, HGW), F32),
                   jax.ShapeDtypeStruct((2, HG_HEADS, N_CHUNKS, HG_DIM, HG_DIM), BF16)],
        scratch_shapes=[pltpu.VMEM((2, S, HG_DIM), BF16), pltpu.VMEM((2, S, HG_DIM), BF16),
                        pltpu.VMEM((2, N_CHUNKS, HG_DIM, HG_DIM), F32), pltpu.VMEM((T, HG_DIM), BF16),
                        pltpu.VMEM((2, T, HG_DIM), F32)],
        operands=[p_a, lbl])


def _hgrn_bwd(p_a, lbl, d_o, st, carried=None):
    cpt = TM // CHUNK

    def rows(r):
        return r * TM if isinstance(r, int) else pl.multiple_of(r * TM, TM)

    def body(p_ref, lbl_ref, do_ref, st_ref, dp_ref, dlb_ref, b_s, bt_s, dbt_s, qd_s, dst_s, w_s):
        masks = [_chunk_masks(d == 1) for d in (0, 1)]
        same01 = jnp.where(masks[0][0], 1.0, 0.0).astype(BF16)
        tri = [m[1] for m in masks]
        tri01 = [jnp.where(t, 1.0, 0.0).astype(BF16) for t in tri]
        later01 = [tri01[1], tri01[0]]
        lb = [_sigmoid(lbl_ref[d][0:1, :] - lbl_ref[d][1:2, :]) for d in (0, 1)]

        def prep_tile(r, latent):
            r0 = rows(r)
            for d in (0, 1):
                z = p_ref[pl.ds(r0, TM), d * HG_DIM:(d + 1) * HG_DIM]
                _, _, b, bt = _decay_terms(z, lb[d], same01, tri01[d])
                b_s[d, pl.ds(r0, TM), :] = b
                bt_s[d, pl.ds(r0, TM), :] = bt
                if latent:
                    rl = pl.multiple_of(r0 - L, TM)
                    qr = p_ref[pl.ds(r0, TM), 3 * HG_DIM:4 * HG_DIM]
                    qd = (qr * _sigmoid(qr) * HG_DIM ** -0.5 * jnp.exp(b)).astype(BF16)
                    qd_s[d, pl.ds(rl, TM), :] = qd
                    w_s[d, pl.ds(r * cpt, cpt)] = _chunk_outer(
                        do_ref[pl.ds(rl, TM), :].astype(BF16), qd).astype(BF16)

        prep_tile(0, False)
        w_s[:, pl.ds(0, N_CTX_CHUNKS)] = jnp.zeros((2, N_CTX_CHUNKS, HG_DIM, HG_DIM), BF16)

        def prep(r, carry):
            prep_tile(r, True)
            return carry

        lax.fori_loop(1, N_TILES, prep, 0, unroll=2)

        def rscan(j, dsts):
            i = N_CHUNKS - 1 - j
            new = []
            for d in (0, 1):
                nn = _chunk_order(i, d == 1)
                c0 = pl.multiple_of(nn * CHUNK, CHUNK)
                dst_s[d, nn] = dsts[d].astype(BF16)
                after = st_ref[d, _chunk_order(jnp.minimum(i + 1, N_CHUNKS - 1), d == 1)].astype(F32)
                dbt_s[d, pl.ds(c0, CHUNK), :] = jnp.broadcast_to(
                    jnp.sum(after * dsts[d], axis=0, keepdims=True), (CHUNK, HG_DIM))
                new.append(dsts[d] * jnp.exp(bt_s[d, pl.ds(c0, 1), :]) + w_s[d, nn].astype(F32))
            return tuple(new)

        zero = jnp.zeros((HG_DIM, HG_DIM), F32)
        lax.fori_loop(0, N_CHUNKS, rscan, (zero, zero))

        def grad_tile(r, latent):
            r0 = rows(r)
            vb = p_ref[pl.ds(r0, TM), 2 * HG_DIM:3 * HG_DIM].astype(BF16)
            dv = jnp.zeros((TM, HG_DIM), F32)
            dq = jnp.zeros((TM, HG_DIM), F32)
            dlbs = []
            if latent:
                rl = pl.multiple_of(r0 - L, TM)
                qr = p_ref[pl.ds(r0, TM), 3 * HG_DIM:4 * HG_DIM]
                sq = _sigmoid(qr)
                do = do_ref[pl.ds(rl, TM), :].astype(BF16)
                da_full = _dot_nt(do, vb)
            for d in (0, 1):
                z = p_ref[pl.ds(r0, TM), d * HG_DIM:(d + 1) * HG_DIM]
                sz = _sigmoid(z)
                f = lb[d] + (1.0 - lb[d]) * sz
                k = 1.0 - f
                b = b_s[d, pl.ds(r0, TM), :]
                e2 = jnp.exp(bt_s[d, pl.ds(r0, TM), :] - b)
                dstb = dst_s[d, pl.ds(r * cpt, cpt)]
                kd2 = k * e2
                dkd2 = jnp.einsum('ncv,nvk->nck', vb.reshape(cpt, CHUNK, HG_DIM), dstb,
                                  preferred_element_type=F32).reshape(TM, HG_DIM)
                dv = dv + jnp.einsum('nck,nvk->ncv', kd2.astype(BF16).reshape(cpt, CHUNK, HG_DIM), dstb,
                                     preferred_element_type=F32).reshape(TM, HG_DIM)
                dk = dkd2 * e2
                db = -(kd2 * dkd2)
                if latent:
                    eb = jnp.exp(b)
                    enb = jnp.exp(-b)
                    qdf = qr * sq * HG_DIM ** -0.5 * eb
                    kdf = k * enb
                    qd = qd_s[d, pl.ds(rl, TM), :]
                    kd = kdf.astype(BF16)
                    a = jnp.where(tri[d], _dot_nt(qd, kd), 0.0).astype(BF16)
                    da = jnp.where(tri[d], da_full, 0.0).astype(BF16)
                    stb = st_ref[d, pl.ds(r * cpt, cpt)]
                    dqd = _dot(da, kd) + jnp.einsum(
                        'ncv,nvk->nck', do.reshape(cpt, CHUNK, HG_DIM), stb,
                        preferred_element_type=F32).reshape(TM, HG_DIM)
                    dkd = _dot_tn(da, qd)
                    dv = dv + _dot_tn(a, do)
                    dk = dk + dkd * enb
                    db = db + qdf * dqd - kdf * dkd
                    dq = dq + dqd * eb
                dg = _dot_lhs01(later01[d], db) + dbt_s[d, pl.ds(r0, TM), :]
                df = dg / f - dk
                dp_ref[pl.ds(r0, TM), d * HG_DIM:(d + 1) * HG_DIM] = (
                    df * (1.0 - lb[d]) * sz * (1.0 - sz)).astype(BF16)
                dlbs.append(jnp.sum(df * (1.0 - sz), axis=0, keepdims=True))
            dp_ref[pl.ds(r0, TM), 2 * HG_DIM:3 * HG_DIM] = dv.astype(BF16)
            if latent:
                dq = dq * (HG_DIM ** -0.5) * (sq * (1.0 + qr * (1.0 - sq)))
            dp_ref[pl.ds(r0, TM), 3 * HG_DIM:4 * HG_DIM] = dq.astype(BF16)
            return dlbs

        dlb_ctx = grad_tile(0, False)

        def grads(r, acc):
            t = grad_tile(r, True)
            return (acc[0] + t[0], acc[1] + t[1])

        dlb = lax.fori_loop(1, N_TILES, grads, (dlb_ctx[0], dlb_ctx[1]))
        dlb_ref[0:1, :] = dlb[0]
        dlb_ref[1:2, :] = dlb[1]

    return _pcall(
        body, carried, name="hgrn_bwd", grid=(HG_HEADS,),
        in_specs=[pl.BlockSpec((T, 4 * HG_DIM), lambda h: (0, h)),
                  pl.BlockSpec((2, 2, HG_DIM), lambda h: (0, 0, h)),
                  pl.BlockSpec((S, HG_DIM), lambda h: (0, h)),
                  pl.BlockSpec((2, None, N_CHUNKS, HG_DIM, HG_DIM), lambda h: (0, h, 0, 0, 0))],
        out_specs=[pl.BlockSpec((T, 4 * HG_DIM), lambda h: (0, h)),
                   pl.BlockSpec((2, HG_DIM), lambda h: (0, h))],
        out_shape=[jax.ShapeDtypeStruct((T, WA), BF16), jax.ShapeDtypeStruct((2, HGW), F32)],
        scratch_shapes=[pltpu.VMEM((2, T, HG_DIM), F32), pltpu.VMEM((2, T, HG_DIM), F32),
                        pltpu.VMEM((2, T, HG_DIM), F32), pltpu.VMEM((2, S, HG_DIM), BF16),
                        pltpu.VMEM((2, N_CHUNKS, HG_DIM, HG_DIM), BF16),
                        pltpu.VMEM((2, N_CHUNKS, HG_DIM, HG_DIM), BF16)],
        operands=[p_a, lbl, d_o, st])


def _rope_tables():
    t = np.arange(S)
    inv = ROPE_THETA ** (-np.arange(0, 32, 2, dtype=np.float64) / 32)
    lane = np.arange(64)
    pos = np.where(lane[None, :] < 32, (t // GRID_W)[:, None], (t % GRID_W)[:, None]).astype(np.float64)
    ang = pos * inv[(lane % 32) % 16][None, :]
    sign = np.where((lane % 32) < 16, -1.0, 1.0)[None, :]
    cos = np.tile(np.cos(ang), (1, 2)).astype(np.float32)
    sin = np.tile(np.sin(ang) * sign, (1, 2)).astype(np.float32)
    return jnp.asarray(cos), jnp.asarray(sin)


def _rope_partner(v):
    lane = lax.broadcasted_iota(jnp.int32, (1, 128), 1)
    first = (lane % 32) < 16
    slabs = []
    for j in range(v.shape[1] // 128):
        s = v[:, 128 * j:128 * (j + 1)]
        slabs.append(jnp.where(first, pltpu.roll(s, 112, 1), pltpu.roll(s, 16, 1)))
    return slabs[0] if len(slabs) == 1 else jnp.concatenate(slabs, axis=1)


def _group_ones(width, group):
    r = lax.broadcasted_iota(jnp.int32, (width, width), 0)
    c = lax.broadcasted_iota(jnp.int32, (width, width), 1)
    return jnp.where((r // group) == (c // group), 1.0, 0.0).astype(BF16)


def _group_mean(v, ones01, group):
    hi = v.astype(BF16)
    lo = (v - hi.astype(F32)).astype(BF16)
    return (_dot(hi, ones01) + _dot(lo, ones01)) * (1.0 / group)


def _rep_matrix():
    r = lax.broadcasted_iota(jnp.int32, (KVW, ATW), 0)
    c = lax.broadcasted_iota(jnp.int32, (KVW, ATW), 1)
    return jnp.where(r == HEAD_DIM * (c // 256) + c % HEAD_DIM, 1.0, 0.0).astype(BF16)


def _tile_lanes(v, reps):
    return jnp.concatenate([v] * reps, axis=1)


def _prep_fwd(p_b, o, cos, sin, hnw, qnw, knw):
    def body(p_ref, o_ref, cos_ref, sin_ref, hnw_ref, qnw_ref, knw_ref, y_ref, q_ref, k_ref, v_ref):
        i = pl.program_id(0)
        rep = _rep_matrix()
        ones_k = _group_ones(KVW, HEAD_DIM)
        kr = p_ref[:, 1024:1152]
        krstd = lax.rsqrt(_group_mean(kr * kr, ones_k, HEAD_DIM) + EPS)
        kn = kr * krstd * knw_ref[...]
        v_ref[...] = _dot(p_ref[:, 1152:1280].astype(BF16), rep).astype(BF16)

        @pl.when(i == 0)
        def _():
            k_ref[...] = _dot(kn.astype(BF16), rep).astype(BF16)

        @pl.when(i > 0)
        def _():
            cs, sn = cos_ref[...], sin_ref[...]
            kro = kn * cs + _rope_partner(kn) * sn
            k_ref[...] = _dot(kro.astype(BF16), rep).astype(BF16)
            qr = p_ref[:, 512:1024]
            qrstd = lax.rsqrt(_group_mean(qr * qr, _group_ones(ATW, HEAD_DIM), HEAD_DIM) + EPS)
            qn = qr * qrstd * qnw_ref[...]
            qro = qn * _tile_lanes(cs, 4) + _rope_partner(qn) * _tile_lanes(sn, 4)
            q_ref[...] = (qro * HEAD_DIM ** -0.5).astype(BF16)
            ys = []
            for h in range(HG_HEADS):
                oh = o_ref[:, HG_DIM * h:HG_DIM * (h + 1)]
                gh = p_ref[:, HG_DIM * h:HG_DIM * (h + 1)]
                rstd = lax.rsqrt(jnp.mean(oh * oh, axis=-1, keepdims=True) + EPS)
                ys.append(oh * rstd * hnw_ref[...] * (gh * _sigmoid(gh)))
            y_ref[...] = jnp.concatenate(ys, axis=1).astype(BF16)

    return pl.pallas_call(
        body, name="prep_fwd", grid=(N_TILES,),
        in_specs=[pl.BlockSpec((TM, WB), lambda i: (i, 0)),
                  pl.BlockSpec((TM, HGW), lambda i: (_lat(i), 0)),
                  pl.BlockSpec((TM, 128), lambda i: (_lat(i), 0)),
                  pl.BlockSpec((TM, 128), lambda i: (_lat(i), 0)),
                  _full((1, HG_DIM)), _full((1, ATW)), _full((1, KVW))],
        out_specs=[pl.BlockSpec((TM, HGW), lambda i: (_lat(i), 0)),
                   pl.BlockSpec((TM, ATW), lambda i: (_lat(i), 0)),
                   pl.BlockSpec((TM, ATW), lambda i: (i, 0)),
                   pl.BlockSpec((TM, ATW), lambda i: (i, 0))],
        out_shape=[jax.ShapeDtypeStruct((S, HGW), BF16), jax.ShapeDtypeStruct((S, ATW), BF16),
                   jax.ShapeDtypeStruct((T, ATW), BF16), jax.ShapeDtypeStruct((T, ATW), BF16)],
        compiler_params=_cp(("arbitrary",)),
    )(p_b, o, cos, sin, hnw, qnw, knw)


def _prep_bwd(p_b, o, cos, sin, hnw, qnw, knw, dy_hg, dq, dk_rep, dv_rep, carried=None):
    def body(p_ref, o_ref, cos_ref, sin_ref, hnw_ref, qnw_ref, knw_ref, dy_ref, dq_ref, dk_ref, dv_ref,
             dp_ref, do_ref, acc_ref):
        i = pl.program_id(0)

        @pl.when(i == 0)
        def _():
            acc_ref[...] = jnp.zeros_like(acc_ref)

        rep = _rep_matrix()
        ones_k = _group_ones(KVW, HEAD_DIM)

        def fold(v):
            hi = v.astype(BF16)
            lo = (v - hi.astype(F32)).astype(BF16)
            return _dot_nt(hi, rep) + _dot_nt(lo, rep)

        kr = p_ref[:, 1024:1152]
        krstd = lax.rsqrt(_group_mean(kr * kr, ones_k, HEAD_DIM) + EPS)
        khat = kr * krstd
        kw = knw_ref[...]
        dkro = fold(dk_ref[...])
        dv = fold(dv_ref[...])

        def k_back(dkn):
            dkhat = dkn * kw
            dkr = krstd * (dkhat - khat * _group_mean(dkhat * khat, ones_k, HEAD_DIM))
            acc_ref[2:3, 0:KVW] += jnp.sum(dkn * khat, axis=0, keepdims=True)
            dp_ref[:, 1024:1152] = dkr.astype(BF16)
            dp_ref[:, 1152:1280] = dv.astype(BF16)

        @pl.when(i == 0)
        def _():
            k_back(dkro)
            dp_ref[:, 0:1024] = jnp.zeros((TM, 1024), BF16)

        @pl.when(i > 0)
        def _():
            cs, sn = cos_ref[...], sin_ref[...]
            k_back(dkro * cs + _rope_partner(dkro * sn))
            ones_q = _group_ones(ATW, HEAD_DIM)
            qr = p_ref[:, 512:1024]
            qrstd = lax.rsqrt(_group_mean(qr * qr, ones_q, HEAD_DIM) + EPS)
            qhat = qr * qrstd
            dqro = dq_ref[...] * HEAD_DIM ** -0.5
            dqn = dqro * _tile_lanes(cs, 4) + _rope_partner(dqro * _tile_lanes(sn, 4))
            dqhat = dqn * qnw_ref[...]
            dqr = qrstd * (dqhat - qhat * _group_mean(dqhat * qhat, ones_q, HEAD_DIM))
            acc_ref[1:2, :] += jnp.sum(dqn * qhat, axis=0, keepdims=True)
            dp_ref[:, 512:1024] = dqr.astype(BF16)
            dws = jnp.zeros((1, HG_DIM), F32)
            for h in range(HG_HEADS):
                sl = slice(HG_DIM * h, HG_DIM * (h + 1))
                oh, gh, dy = o_ref[:, sl], p_ref[:, sl], dy_ref[:, sl]
                rstd = lax.rsqrt(jnp.mean(oh * oh, axis=-1, keepdims=True) + EPS)
                ohat = oh * rstd
                sg = _sigmoid(gh)
                dp_ref[:, sl] = (dy * (ohat * hnw_ref[...]) * (sg * (1.0 + gh * (1.0 - sg)))).astype(BF16)
                dn = dy * (gh * sg)
                dws = dws + jnp.sum(dn * ohat, axis=0, keepdims=True)
                dohat = dn * hnw_ref[...]
                do_ref[:, sl] = rstd * (dohat - ohat * jnp.mean(dohat * ohat, axis=-1, keepdims=True))
            acc_ref[0:1, 0:HG_DIM] += dws

    return _pcall(
        body, carried, name="prep_bwd", grid=(N_TILES,),
        in_specs=[pl.BlockSpec((TM, WB), lambda i: (i, 0)),
                  pl.BlockSpec((TM, HGW), lambda i: (_lat(i), 0)),
                  pl.BlockSpec((TM, 128), lambda i: (_lat(i), 0)),
                  pl.BlockSpec((TM, 128), lambda i: (_lat(i), 0)),
                  _full((1, HG_DIM)), _full((1, ATW)), _full((1, KVW)),
                  pl.BlockSpec((TM, HGW), lambda i: (_lat(i), 0)),
                  pl.BlockSpec((TM, ATW), lambda i: (_lat(i), 0)),
                  pl.BlockSpec((TM, ATW), lambda i: (i, 0)),
                  pl.BlockSpec((TM, ATW), lambda i: (i, 0))],
        out_specs=[pl.BlockSpec((TM, WB), lambda i: (i, 0)),
                   pl.BlockSpec((TM, HGW), lambda i: (_lat(i), 0)),
                   _full((8, ATW))],
        out_shape=[jax.ShapeDtypeStruct((T, WB), BF16), jax.ShapeDtypeStruct((S, HGW), F32),
                   jax.ShapeDtypeStruct((8, ATW), F32)],
        scratch_shapes=[], operands=[p_b, o, cos, sin, hnw, qnw, knw, dy_hg, dq, dk_rep, dv_rep])


NEG = -1e30
_CTX_BLOCKS = L // BLOCK


def _attn_window_specs():
    prev = pl.BlockSpec((BLOCK, ATW), lambda i: (jnp.maximum(i - 1, 0) + _CTX_BLOCKS, 0))
    own = pl.BlockSpec((BLOCK, ATW), lambda i: (i + _CTX_BLOCKS, 0))
    nxt = pl.BlockSpec((BLOCK, ATW), lambda i: (jnp.minimum(i + 1, N_BLOCKS - 1) + _CTX_BLOCKS, 0))
    return [prev, own, nxt, _full((L, ATW))]


def _attn_valid(i, heads, context):
    n_keys = 3 * BLOCK + (L if context else 0)
    qi = lax.broadcasted_iota(jnp.int32, (heads * BLOCK, n_keys), 0) % BLOCK
    kj = lax.broadcasted_iota(jnp.int32, (heads * BLOCK, n_keys), 1)
    window = ((jnp.abs(kj - BLOCK - qi) <= BLOCK) & ((kj >= BLOCK) | (i > 0))
              & ((kj < 2 * BLOCK) | (i < N_BLOCKS - 1)))
    return window | (kj >= 3 * BLOCK)


def _stack_heads(qg):
    lane = lax.broadcasted_iota(jnp.int32, (1, 256), 1) // HEAD_DIM
    return jnp.concatenate([jnp.where(lane == g, qg, jnp.zeros_like(qg)) for g in range(4)], axis=0)


def _unstack_heads(v4):
    lane = lax.broadcasted_iota(jnp.int32, (1, 256), 1) // HEAD_DIM
    out = jnp.where(lane == 0, v4[0:BLOCK], 0.0)
    for g in range(1, 4):
        out = out + jnp.where(lane == g, v4[g * BLOCK:(g + 1) * BLOCK], 0.0)
    return out


def _sink_rows(sink_ref, hk):
    return jnp.concatenate(
        [jnp.broadcast_to(sink_ref[0:1, 4 * hk + g:4 * hk + g + 1], (BLOCK, 1)) for g in range(4)], axis=0)


def _attn_fwd(q, k_rep, v_rep, sinks, carried=None):
    def body(q_ref, kp, ko, kn, kc, vp, vo, vn, vc, sink_ref, y_ref, lse_ref):
        i = pl.program_id(0)
        valid = _attn_valid(i, 1, True)
        lane8 = lax.broadcasted_iota(jnp.int32, (1, ATT_HEADS), 1)
        head_of_lane = lax.broadcasted_iota(jnp.int32, (1, 256), 1) // HEAD_DIM
        lse_out = jnp.zeros((BLOCK, ATT_HEADS), F32)
        for hk in range(KV_HEADS):
            sl = slice(256 * hk, 256 * (hk + 1))
            qg = q_ref[:, sl]
            keys = jnp.concatenate([kp[:, sl], ko[:, sl], kn[:, sl], kc[:, sl]], axis=0)
            vals = jnp.concatenate([vp[:, sl], vo[:, sl], vn[:, sl], vc[:, sl]], axis=0)
            yg = jnp.zeros((BLOCK, 256), F32)
            for g in range(4):
                q1 = jnp.where(head_of_lane == g, qg, jnp.zeros_like(qg))
                s = jnp.where(valid, _dot_nt(q1, keys), NEG)
                sink = sink_ref[0:1, 4 * hk + g:4 * hk + g + 1]
                m = jnp.maximum(jnp.max(s, axis=1, keepdims=True), sink)
                p = jnp.exp(s - m)
                den = jnp.sum(p, axis=1, keepdims=True) + jnp.exp(sink - m)
                o1 = _dot(p.astype(BF16), vals) * (1.0 / den)
                yg = yg + jnp.where(head_of_lane == g, o1, 0.0)
                lse_out = lse_out + jnp.where(lane8 == 4 * hk + g, m + jnp.log(den), 0.0)
            y_ref[:, sl] = yg.astype(BF16)
        lse_ref[...] = lse_out

    return _pcall(
        body, carried, name="attn_fwd", grid=(N_BLOCKS,),
        in_specs=[pl.BlockSpec((BLOCK, ATW), lambda i: (i, 0))] + _attn_window_specs()
        + _attn_window_specs() + [_full((1, ATT_HEADS))],
        out_specs=[pl.BlockSpec((BLOCK, ATW), lambda i: (i, 0)),
                   pl.BlockSpec((BLOCK, ATT_HEADS), lambda i: (i, 0))],
        out_shape=[jax.ShapeDtypeStruct((S, ATW), BF16), jax.ShapeDtypeStruct((S, ATT_HEADS), F32)],
        scratch_shapes=[],
        operands=[q, k_rep, k_rep, k_rep, k_rep, v_rep, v_rep, v_rep, v_rep, sinks])


def _attn_bwd(q, k_rep, v_rep, sinks, y_at, lse, dy, carried=None):
    def body(q_ref, kp, ko, kn, kc, vp, vo, vn, vc, sink_ref, y_ref, lse_ref, dy_ref,
             dq_ref, dk_ref, dv_ref, dsink_ref, dk_acc, dv_acc):
        i = pl.program_id(0)

        @pl.when(i == 0)
        def _():
            dk_acc[...] = jnp.zeros_like(dk_acc)
            dv_acc[...] = jnp.zeros_like(dv_acc)
            dk_ref[pl.ds(0, L), :] = jnp.zeros((L, ATW), F32)
            dv_ref[pl.ds(0, L), :] = jnp.zeros((L, ATW), F32)
            dsink_ref[...] = jnp.zeros_like(dsink_ref)

        valid = _attn_valid(i, 4, False)
        lane8 = lax.broadcasted_iota(jnp.int32, (1, ATT_HEADS), 1)
        w0 = pl.multiple_of(i * BLOCK, BLOCK)
        dsink = jnp.zeros((1, ATT_HEADS), F32)
        for hk in range(KV_HEADS):
            sl = slice(256 * hk, 256 * (hk + 1))
            q4 = _stack_heads(q_ref[:, sl])
            do4f = _stack_heads(dy_ref[:, sl])
            o4 = _stack_heads(y_ref[:, sl]).astype(F32)
            do4 = do4f.astype(BF16)
            kl = jnp.concatenate([kp[:, sl], ko[:, sl], kn[:, sl]], axis=0)
            vl = jnp.concatenate([vp[:, sl], vo[:, sl], vn[:, sl]], axis=0)
            lse4 = jnp.concatenate(
                [jnp.sum(jnp.where(lane8 == 4 * hk + g, lse_ref[...], 0.0), axis=1, keepdims=True)
                 for g in range(4)], axis=0)
            p_loc = jnp.where(valid, jnp.exp(_dot_nt(q4, kl) - lse4), 0.0)
            p_ctx = jnp.exp(_dot_nt(q4, kc[:, sl]) - lse4)
            delta = jnp.sum(do4f * o4, axis=1, keepdims=True)
            ds_loc = (p_loc * (_dot_nt(do4, vl) - delta)).astype(BF16)
            ds_ctx = (p_ctx * (_dot_nt(do4, vc[:, sl]) - delta)).astype(BF16)
            dq_ref[:, sl] = _unstack_heads(_dot(ds_loc, kl) + _dot(ds_ctx, kc[:, sl]))
            dk_acc[pl.ds(w0, 3 * BLOCK), sl] += _dot_tn(ds_loc, q4)
            dv_acc[pl.ds(w0, 3 * BLOCK), sl] += _dot_tn(p_loc.astype(BF16), do4)
            dk_ref[pl.ds(0, L), sl] += _dot_tn(ds_ctx, q4)
            dv_ref[pl.ds(0, L), sl] += _dot_tn(p_ctx.astype(BF16), do4)
            p_sink = jnp.exp(_sink_rows(sink_ref, hk) - lse4)
            for g in range(4):
                rows = slice(g * BLOCK, (g + 1) * BLOCK)
                dsink = dsink + jnp.where(lane8 == 4 * hk + g,
                                          -jnp.sum(p_sink[rows] * delta[rows], axis=0, keepdims=True), 0.0)
        dsink_ref[...] += dsink

        @pl.when(i == N_BLOCKS - 1)
        def _():
            dk_ref[pl.ds(L, S), :] = dk_acc[pl.ds(BLOCK, S), :]
            dv_ref[pl.ds(L, S), :] = dv_acc[pl.ds(BLOCK, S), :]

    row_q = pl.BlockSpec((BLOCK, ATW), lambda i: (i, 0))
    return _pcall(
        body, carried, name="attn_bwd", grid=(N_BLOCKS,),
        in_specs=[row_q] + _attn_window_specs() + _attn_window_specs()
        + [_full((1, ATT_HEADS)), row_q, pl.BlockSpec((BLOCK, ATT_HEADS), lambda i: (i, 0)), row_q],
        out_specs=[row_q, _full((T, ATW)), _full((T, ATW)), _full((1, ATT_HEADS))],
        out_shape=[jax.ShapeDtypeStruct((S, ATW), F32), jax.ShapeDtypeStruct((T, ATW), F32),
                   jax.ShapeDtypeStruct((T, ATW), F32), jax.ShapeDtypeStruct((1, ATT_HEADS), F32)],
        scratch_shapes=[pltpu.VMEM((S + 2 * BLOCK, ATW), F32), pltpu.VMEM((S + 2 * BLOCK, ATW), F32)],
        operands=[q, k_rep, k_rep, k_rep, k_rep, v_rep, v_rep, v_rep, v_rep, sinks, y_at, lse, dy])


def _merge_fwd(y_hg, y_at, p_c, x, w_bh, w_ba, w_out, g1, nfw, sh2, sc2, carried=None):
    def body(yh_ref, ya_ref, g_ref, x_ref, wbh_ref, wba_ref, wo_ref, g1_ref, nfw_ref, sh_ref, sc_ref,
             mx_ref, r_ref, x1_ref, h2_ref):
        a = _dot_nt(yh_ref[...], wbh_ref[...])
        b = _dot_nt(ya_ref[...], wba_ref[...])
        mixed = (_sigmoid(g_ref[:, :D]) * a + _sigmoid(g_ref[:, D:]) * b).astype(BF16)
        r = _dot(mixed, wo_ref[...])
        x1 = x_ref[...] + g1_ref[...] * r
        mx_ref[...] = mixed
        r_ref[...] = r
        x1_ref[...] = x1
        h2_ref[...] = _rms_mod(x1, nfw_ref[...], sh_ref[...], sc_ref[...]).astype(BF16)

    row = lambda w: pl.BlockSpec((TM, w), lambda i: (i, 0))
    vec = _full((1, D))
    return _pcall(
        body, carried, name="merge_fwd", grid=(N_LAT_TILES,),
        in_specs=[row(HGW), row(ATW), row(WC), row(D), _VMEM_WHOLE, _VMEM_WHOLE, _VMEM_WHOLE,
                  vec, vec, vec, vec],
        out_specs=[row(D)] * 4,
        out_shape=[jax.ShapeDtypeStruct((S, D), dt) for dt in (BF16, F32, F32, BF16)],
        scratch_shapes=[], operands=[y_hg, y_at, p_c, x, w_bh, w_ba, w_out, g1, nfw, sh2, sc2])


def _merge_bwd(dx1, r, y_hg, y_at, p_c, w_bh, w_ba, w_out, g1, carried=None):
    def body(dx_ref, r_ref, yh_ref, ya_ref, g_ref, wbh_ref, wba_ref, wo_ref, g1_ref,
             dr_ref, da_ref, db_ref, dg_ref, dyh_ref, dya_ref, acc_ref):
        @pl.when(pl.program_id(0) == 0)
        def _():
            acc_ref[...] = jnp.zeros_like(acc_ref)

        dx1v = dx_ref[...]
        acc_ref[0:1, :] += jnp.sum(dx1v * r_ref[...], axis=0, keepdims=True)
        dr = (g1_ref[...] * dx1v).astype(BF16)
        dr_ref[...] = dr
        dmix = _dot_nt(dr, wo_ref[...])
        sh, sa = _sigmoid(g_ref[:, :D]), _sigmoid(g_ref[:, D:])
        da = (dmix * sh).astype(BF16)
        db = (dmix * sa).astype(BF16)
        da_ref[...] = da
        db_ref[...] = db
        dg_ref[:, :D] = (dmix * _dot_nt(yh_ref[...], wbh_ref[...]) * sh * (1.0 - sh)).astype(BF16)
        dg_ref[:, D:] = (dmix * _dot_nt(ya_ref[...], wba_ref[...]) * sa * (1.0 - sa)).astype(BF16)
        dyh_ref[...] = _dot(da, wbh_ref[...])
        dya_ref[...] = _dot(db, wba_ref[...])

    row = lambda w: pl.BlockSpec((TM, w), lambda i: (i, 0))
    return _pcall(
        body, carried, name="merge_bwd", grid=(N_LAT_TILES,),
        in_specs=[row(D), row(D), row(HGW), row(ATW), row(WC), _VMEM_WHOLE, _VMEM_WHOLE, _VMEM_WHOLE,
                  _full((1, D))],
        out_specs=[row(D), row(D), row(D), row(WC), row(HGW), row(ATW), _full((8, D))],
        out_shape=[jax.ShapeDtypeStruct((S, D), BF16), jax.ShapeDtypeStruct((S, D), BF16),
                   jax.ShapeDtypeStruct((S, D), BF16), jax.ShapeDtypeStruct((S, WC), BF16),
                   jax.ShapeDtypeStruct((S, HGW), F32), jax.ShapeDtypeStruct((S, ATW), F32),
                   jax.ShapeDtypeStruct((8, D), F32)],
        scratch_shapes=[], operands=[dx1, r, y_hg, y_at, p_c, w_bh, w_ba, w_out, g1])


def _ffn_fused(x1, h2, tgt, w_gate, w_up, w_down, g2, nfw, sc2):
    def body(x1_ref, h2_ref, t_ref, wg_ref, wu_ref, wd_ref, g2_ref, nfw_ref, sc_ref,
             act_ref, dgt_ref, dup_ref, df_ref, dx_ref, acc_ref, gs, us):
        @pl.when(pl.program_id(0) == 0)
        def _():
            acc_ref[...] = jnp.zeros_like(acc_ref)

        h2 = h2_ref[...]
        whole = lambda w_ref: w_ref[...].reshape(D_FF, D)
        wide = lambda t_ref: jnp.concatenate([t_ref[j] for j in range(N_FF_TILES)], axis=1)
        for j in range(N_FF_TILES):
            g = _dot_nt(h2, wg_ref[j])
            u = _dot_nt(h2, wu_ref[j])
            gs[j] = g
            us[j] = u
            act_ref[j] = (g * _sigmoid(g) * u).astype(BF16)
        f = _dot(wide(act_ref), whole(wd_ref))
        x1v = x1_ref[...]
        g2 = g2_ref[...]
        diff = x1v + g2 * f - t_ref[...]
        dy = diff * (1.0 / D)
        df = (g2 * dy).astype(BF16)
        df_ref[...] = df
        dact_all = _dot_nt(df, whole(wd_ref))
        for j in range(N_FF_TILES):
            g, u = gs[j], us[j]
            sg = _sigmoid(g)
            dact = dact_all[:, j * FF_TILE:(j + 1) * FF_TILE]
            dgt_ref[j] = (dact * u * (sg * (1.0 + g * (1.0 - sg)))).astype(BF16)
            dup_ref[j] = (dact * (g * sg)).astype(BF16)
        dh2 = _dot(wide(dgt_ref), whole(wg_ref)) + _dot(wide(dup_ref), whole(wu_ref))
        dx, dsh, dsc, dnw = _rms_mod_bwd(x1v, nfw_ref[...], sc_ref[...], dh2)
        dx_ref[...] = dy + dx
        acc_ref[0:1, :] += dsh
        acc_ref[1:2, :] += dsc
        acc_ref[2:3, :] += dnw
        acc_ref[3:4, :] += jnp.sum(dy * f, axis=0, keepdims=True)
        acc_ref[4:5, :] += 0.5 * jnp.sum(jnp.sum(diff * diff, axis=1, keepdims=True), axis=0,
                                         keepdims=True) * (1.0 / D)

    row = lambda dt_w: pl.BlockSpec((TM, dt_w), lambda i: (i, 0))
    blk = pl.BlockSpec((N_FF_TILES, TM, FF_TILE), lambda i: (0, i, 0))
    vec = _full((1, D))
    return pl.pallas_call(
        body, name="ffn_fused", grid=(N_LAT_TILES,),
        in_specs=[row(D), row(D), row(D), _VMEM_WHOLE, _VMEM_WHOLE, _VMEM_WHOLE, vec, vec, vec],
        out_specs=[blk, blk, blk, row(D), row(D), _full((8, D))],
        out_shape=[jax.ShapeDtypeStruct((N_FF_TILES, S, FF_TILE), BF16)] * 3
        + [jax.ShapeDtypeStruct((S, D), BF16), jax.ShapeDtypeStruct((S, D), F32),
           jax.ShapeDtypeStruct((8, D), F32)],
        scratch_shapes=[pltpu.VMEM((N_FF_TILES, TM, FF_TILE), F32), pltpu.VMEM((N_FF_TILES, TM, FF_TILE), F32)],
        compiler_params=_cp(("arbitrary",)),
    )(x1, h2, tgt, w_gate, w_up, w_down, g2, nfw, sc2)


def _proj_bc(h_all, w_b, w_c, carried=None):
    def body(h_ref, wb_ref, wc_ref, pb_ref, pc_ref):
        h = h_ref[...]
        pb_ref[...] = _dot_nt(h, wb_ref[...])

        @pl.when(pl.program_id(0) > 0)
        def _():
            pc_ref[...] = _dot_nt(h, wc_ref[...])

    return _pcall(
        body, carried, name="proj_bc", grid=(N_TILES,),
        in_specs=[pl.BlockSpec((TM, D), lambda i: (i, 0)), _VMEM_WHOLE, _VMEM_WHOLE],
        out_specs=[pl.BlockSpec((TM, WB), lambda i: (i, 0)), pl.BlockSpec((TM, WC), lambda i: (_lat(i), 0))],
        out_shape=[jax.ShapeDtypeStruct((T, WB), F32), jax.ShapeDtypeStruct((S, WC), F32)],
        scratch_shapes=[], operands=[h_all, w_b, w_c])


def _input_bwd(dp_a, dp_b, dp_c, w_a, w_b, w_c, ctx, x, dx1, nw, sh, sc, carried=None):
    def body(da_ref, db_ref, dc_ref, wa_ref, wb_ref, wc_ref, ctx_ref, x_ref, dx1_ref, nw_ref, sh_ref,
             sc_ref, gx_ref, acc_ref):
        i = pl.program_id(0)

        @pl.when(i == 0)
        def _():
            acc_ref[...] = jnp.zeros_like(acc_ref)

        dh = _dot(da_ref[...], wa_ref[...]) + _dot(db_ref[...], wb_ref[...])

        @pl.when(i == 0)
        def _():
            _, dsh, dsc, dnw = _rms_mod_bwd(ctx_ref[...], nw_ref[...], sc_ref[0:1, :], dh)
            acc_ref[3:4, :] += dsh
            acc_ref[4:5, :] += dsc
            acc_ref[2:3, :] += dnw

        @pl.when(i > 0)
        def _():
            dhl = dh + _dot(dc_ref[...], wc_ref[...])
            dx, dsh, dsc, dnw = _rms_mod_bwd(x_ref[...], nw_ref[...], sc_ref[1:2, :], dhl)
            gx_ref[...] = dx1_ref[...] + dx
            acc_ref[0:1, :] += dsh
            acc_ref[1:2, :] += dsc
            acc_ref[2:3, :] += dnw

    lat = lambda w: pl.BlockSpec((TM, w), lambda i: (_lat(i), 0))
    return _pcall(
        body, carried, name="input_bwd", grid=(N_TILES,),
        in_specs=[pl.BlockSpec((TM, WA), lambda i: (i, 0)), pl.BlockSpec((TM, WB), lambda i: (i, 0)),
                  lat(WC), _VMEM_WHOLE, _VMEM_WHOLE, _VMEM_WHOLE, _full((TM, D)), lat(D), lat(D),
                  _full((1, D)), _full((2, D)), _full((2, D))],
        out_specs=[lat(D), _full((8, D))],
        out_shape=[jax.ShapeDtypeStruct((S, D), F32), jax.ShapeDtypeStruct((8, D), F32)],
        scratch_shapes=[], operands=[dp_a, dp_b, dp_c, w_a, w_b, w_c, ctx, x, dx1, nw, sh, sc])


_C1 = 1.0 - ADAM_B1 ** ADAM_STEP
_C2 = 1.0 - ADAM_B2 ** ADAM_STEP


def _adamw_math(w, g, m, v):
    m = ADAM_B1 * m + (1.0 - ADAM_B1) * g
    v = ADAM_B2 * v + (1.0 - ADAM_B2) * (g * g)
    m_hat = m / _C1
    v_hat = v / _C2
    delta = -ADAM_LR * (m_hat / (jnp.sqrt(v_hat) + ADAM_EPS) + ADAM_WD * w)
    return delta, m, v


def _adamw_sharded(terms, w, m, v, name, tr, extra=None):
    rows, cols = w.shape

    def body(*refs):
        t_ref, w_ref, m_ref, v_ref = refs[:4]
        g_ref, d_ref, nm_ref, nv_ref = refs[-4:]
        g = t_ref[0].astype(F32)
        for s in range(1, N_CHIPS):
            g = g + t_ref[s].astype(F32)
        if extra is not None:
            g = g + refs[4][...].astype(F32)
        g_ref[...] = g
        d_ref[...], nm_ref[...], nv_ref[...] = _adamw_math(w_ref[...], g, m_ref[...], v_ref[...])

    blk = pl.BlockSpec((tr, cols), lambda i: (i, 0))
    return pl.pallas_call(
        body, name=name, grid=(rows // tr,),
        in_specs=[pl.BlockSpec((N_CHIPS, tr, cols), lambda i: (0, i, 0)), blk, blk, blk]
        + ([blk] if extra is not None else []),
        out_specs=[blk] * 4,
        out_shape=[jax.ShapeDtypeStruct((rows, cols), F32)] * 4,
        compiler_params=_cp(("parallel",)),
    )(terms, w, m, v, *([extra] if extra is not None else []))


def _adamw_plain(g, w, m, v, name, tr=None):
    def body(g_ref, w_ref, m_ref, v_ref, d_ref, nm_ref, nv_ref):
        d_ref[...], nm_ref[...], nv_ref[...] = _adamw_math(w_ref[...], g_ref[...], m_ref[...], v_ref[...])

    if tr is None:
        return pl.pallas_call(
            body, name=name, in_specs=[_VMEM_WHOLE] * 4, out_specs=[_VMEM_WHOLE] * 3,
            out_shape=[jax.ShapeDtypeStruct(w.shape, F32)] * 3,
            compiler_params=_cp(),
        )(g, w, m, v)
    blk = pl.BlockSpec((tr, w.shape[1]), lambda i: (i, 0))
    return pl.pallas_call(
        body, name=name, grid=(w.shape[0] // tr,), in_specs=[blk] * 4, out_specs=[blk] * 3,
        out_shape=[jax.ShapeDtypeStruct(w.shape, F32)] * 3,
        compiler_params=_cp(("parallel",)),
    )(g, w, m, v)


SMALL_ROWS = 16
R_DMOD, R_DCTX, R_NMIX, R_NFFN, R_MISC, R_DLB, R_BADA01 = 0, 6, 8, 9, 10, 11, 13
M_HNW, M_QNW, M_KNW, M_SINK, M_LOSS = 0, 128, 256, 384, 512


def _pack_small(acc_in, acc_mg, acc_ffn, acc_prep, dsink, dlb):
    def body(in_ref, mg_ref, ff_ref, pp_ref, ds_ref, dlb_ref, o_ref):
        o_ref[...] = jnp.zeros_like(o_ref)
        o_ref[0:2, :] = in_ref[0:2, :]
        o_ref[2:3, :] = mg_ref[0:1, :]
        o_ref[3:5, :] = ff_ref[0:2, :]
        o_ref[5:6, :] = ff_ref[3:4, :]
        o_ref[6:8, :] = in_ref[3:5, :]
        o_ref[8:9, :] = in_ref[2:3, :]
        o_ref[9:10, :] = ff_ref[2:3, :]
        o_ref[10:11, M_HNW:M_HNW + HG_DIM] = pp_ref[0:1, 0:HG_DIM]
        r = lax.broadcasted_iota(jnp.int32, (ATW, 128), 0)
        c = lax.broadcasted_iota(jnp.int32, (ATW, 128), 1)
        fold = jnp.where((r % HEAD_DIM == c) & (c < HEAD_DIM), 1.0, 0.0).astype(BF16)
        qk = jnp.concatenate([pp_ref[1:2, :], pp_ref[2:3, :], jnp.zeros((6, ATW), F32)], axis=0)
        folded = _dot_exact_rhs01(qk, fold)
        o_ref[10:11, M_QNW:M_QNW + 128] = folded[0:1, :]
        o_ref[10:11, M_KNW:M_KNW + 128] = folded[1:2, :]
        o_ref[10:11, M_SINK:M_SINK + ATT_HEADS] = ds_ref[...]
        o_ref[10:11, M_LOSS:M_LOSS + 128] = ff_ref[4:5, 0:128]
        o_ref[11:13, 0:HGW] = dlb_ref[...]

    return pl.pallas_call(
        body, name="pack_small", in_specs=[_VMEM_WHOLE] * 6, out_specs=_VMEM_WHOLE,
        out_shape=jax.ShapeDtypeStruct((SMALL_ROWS, D), F32), compiler_params=_cp(),
    )(acc_in, acc_mg, acc_ffn, acc_prep, dsink, dlb)


def _sum_small(gathered):
    def body(g_ref, o_ref):
        tot = g_ref[0]
        for s in range(1, N_DEV):
            tot = tot + g_ref[s]
        o_ref[...] = tot
        o_ref[R_BADA01:R_BADA01 + 2, :] = tot[0:2, :] + tot[R_DCTX:R_DCTX + 2, :]

    return pl.pallas_call(
        body, name="sum_small", in_specs=[_VMEM_WHOLE], out_specs=_VMEM_WHOLE,
        out_shape=jax.ShapeDtypeStruct((SMALL_ROWS, D), F32), compiler_params=_cp(),
    )(gathered)


_REP_NAMES = ("b_ada", "c_ctx", "norm_mix_w", "norm_ffn_w", "hgrn_norm_w", "q_norm_w", "k_norm_w", "attn_sinks")


def _adamw_replicated(tot, g_c_ctx, ws, ms, vs):
    n = len(_REP_NAMES)

    def body(*refs):
        tot_ref, gc_ref = refs[0], refs[1]
        w_refs, m_refs, v_refs = refs[2:2 + n], refs[2 + n:2 + 2 * n], refs[2 + 2 * n:2 + 3 * n]
        outs = refs[2 + 3 * n:]
        row = lambda r: tot_ref[r:r + 1, :]
        misc = row(R_MISC)
        grads = [jnp.concatenate([row(R_BADA01), row(R_BADA01 + 1)] + [row(k) for k in range(2, 6)], axis=1),
                 gc_ref[...], row(R_NMIX), row(R_NFFN),
                 misc[:, M_HNW:M_HNW + HG_DIM], misc[:, M_QNW:M_QNW + HEAD_DIM],
                 misc[:, M_KNW:M_KNW + HEAD_DIM], misc[:, M_SINK:M_SINK + ATT_HEADS]]
        for k in range(n):
            outs[k][...] = grads[k]
            outs[n + k][...], outs[2 * n + k][...], outs[3 * n + k][...] = _adamw_math(
                w_refs[k][...], grads[k], m_refs[k][...], v_refs[k][...])

    shapes = [jax.ShapeDtypeStruct(w.shape, F32) for w in ws]
    return pl.pallas_call(
        body, name="adamw_replicated", in_specs=[_VMEM_WHOLE] * (2 + 3 * n), out_specs=[_VMEM_WHOLE] * (4 * n),
        out_shape=shapes * 4, compiler_params=_cp(),
    )(tot, g_c_ctx, *ws, *ms, *vs)


def _lb_grads(dlb, lbl):
    def body(d_ref, l_ref, o_ref):
        for d in (0, 1):
            ll = l_ref[d]
            lb = _sigmoid(ll[0:1, :] - ll[1:2, :])
            t = d_ref[d:d + 1, :] * lb * (1.0 - lb)
            o_ref[d, 0:1, :] = t
            o_ref[d, 1:2, :] = -t

    return pl.pallas_call(
        body, name="lb_grads", in_specs=[_VMEM_WHOLE] * 2, out_specs=_VMEM_WHOLE,
        out_shape=jax.ShapeDtypeStruct((2, 2, HGW), F32), compiler_params=_cp(),
    )(dlb, lbl)


def _c_ctx_grad(terms, c_ctx):
    def body(t_ref, c_ref, o_ref):
        tot = t_ref[0, 8:9, :]
        for s in range(1, N_DEV):
            tot = tot + t_ref[s, 8:9, :]
        cv = c_ref[...]
        sg = _sigmoid(cv)
        o_ref[...] = tot * (sg * (1.0 + cv * (1.0 - sg)))

    return pl.pallas_call(
        body, name="c_ctx_grad", in_specs=[_VMEM_WHOLE] * 2, out_specs=_VMEM_WHOLE,
        out_shape=jax.ShapeDtypeStruct((1, D), F32), compiler_params=_cp(),
    )(terms, c_ctx)


def _in_perm():
    fz, bz, inp, kk, vv, qhg, ghg, qat, gates = 0, 512, 1024, 1536, 1664, 1792, 2304, 2816, 3328
    cols = []
    for h in range(HG_HEADS):
        for base in (fz, bz, inp, qhg):
            cols += list(range(base + 128 * h, base + 128 * (h + 1)))
    cols += list(range(ghg, ghg + 512)) + list(range(qat, qat + 512))
    cols += list(range(kk, kk + 128)) + list(range(vv, vv + 128))
    cols += list(range(gates, gates + 2048))
    return np.asarray(cols, np.int32)


_PERM = _in_perm()


_PIECES = {"a": (0, WA, 128), "b": (WA, WB, 256), "c": (WA + WB, WC, 256)}


def _block_table(piece):
    lo, n, blk = _PIECES[piece]
    starts = [int(_PERM[r]) for r in range(lo, lo + n, blk)]
    assert all(s % blk == 0 and np.array_equal(_PERM[r:r + blk], np.arange(s, s + blk))
               for s, r in zip(starts, range(lo, lo + n, blk)))
    return jnp.asarray([s // blk for s in starts], jnp.int32), blk


def _pick_row_blocks(x, table, blk, name):
    cols = x.shape[1]

    def body(t_ref, x_ref, o_ref):
        o_ref[...] = x_ref[...]

    return pl.pallas_call(
        body, name=name,
        grid_spec=pltpu.PrefetchScalarGridSpec(
            num_scalar_prefetch=1, grid=(table.shape[0],),
            in_specs=[pl.BlockSpec((blk, cols), lambda i, t: (t[i], 0))],
            out_specs=pl.BlockSpec((blk, cols), lambda i, t: (i, 0))),
        out_shape=jax.ShapeDtypeStruct((table.shape[0] * blk, cols), x.dtype),
        compiler_params=_cp(("arbitrary",)),
    )(table, x)


def _place_row_blocks(x, table, blk, into, out_rows, name):
    cols = x.shape[1]

    def body(t_ref, x_ref, *rest):
        rest[-1][...] = x_ref[...]

    operands, in_specs, aliases = [table, x], [pl.BlockSpec((blk, cols), lambda i, t: (i, 0))], {}
    if into is not None:
        operands.append(into)
        in_specs.append(_ANY)
        aliases = {2: 0}
    return pl.pallas_call(
        body, name=name,
        grid_spec=pltpu.PrefetchScalarGridSpec(
            num_scalar_prefetch=1, grid=(table.shape[0],), in_specs=in_specs,
            out_specs=pl.BlockSpec((blk, cols), lambda i, t: (t[i], 0))),
        out_shape=jax.ShapeDtypeStruct((out_rows, cols), x.dtype),
        input_output_aliases=aliases,
        compiler_params=_cp(("arbitrary",)),
    )(*operands)


def _local_step(x2, ctx2, h_all, h_lat, tgt, lbl, sh_in, sc_in, gate1, sh2, sc2, gate2, norm_mix_w, norm_ffn_w,
                hgrn_norm_w, q_norm_w, k_norm_w, attn_sinks, w_a, w_b, w_c, s_bh, s_ba, s_out,
                s_gate, s_up, s_down):
    first_last = lambda n: [(0, True), (n - 1, False)]
    p_a = _mm_nt(h_all, w_a, tm=T, tn=512, out_dtype=F32, name="proj_a")
    (o, st), (g_gate, g_bh, g_ba) = _hgrn_fwd(
        p_a, lbl, (_gather_comm_relayed([s_gate, s_bh, s_ba]),
                   [(0, True), (HG_HEADS - 2, True), (HG_HEADS - 1, False)]))
    (p_b, p_c), (g_out,) = _proj_bc(
        h_all, w_b, w_c, (_gather_comm_relayed([s_out]), [(0, True), (N_TILES - 4, True), (N_TILES - 1, False)]))
    cos, sin = _rope_tables()
    qnw_t, knw_t = jnp.tile(q_norm_w, (1, ATT_HEADS)), jnp.tile(k_norm_w, (1, KV_HEADS))
    y_hg, qn, k_rep, v_rep = _prep_fwd(p_b, o, cos, sin, hgrn_norm_w, qnw_t, knw_t)
    (y_at, lse), (g_up, g_down) = _attn_fwd(
        qn, k_rep, v_rep, attn_sinks,
        (_gather_comm_relayed([s_up, s_down]), [(0, True), (N_BLOCKS - 6, True), (N_BLOCKS - 1, False)]))
    w_bh, w_ba, w_o = g_bh.reshape(D, HGW), g_ba.reshape(D, ATW), g_out.reshape(D, D)
    (mixed, r, x1, h2), _ = _merge_fwd(
        y_hg, y_at, p_c, x2, w_bh, w_ba, w_o, gate1, norm_ffn_w, sh2, sc2)
    g_gate, g_up, g_down = [g.reshape(N_FF_TILES, FF_TILE, D) for g in (g_gate, g_up, g_down)]

    act, d_gate, d_up, d_f, dx1, acc_ffn = _ffn_fused(x1, h2, tgt, g_gate, g_up, g_down, gate2,
                                                      norm_ffn_w, sc2)
    by_chip = lambda t: t.reshape((N_CHIPS, 2) + t.shape[1:])
    ff_by_chip = lambda t: t.reshape(N_CHIPS, 2, FF_BLK, D)
    t_down, _ = _mm_tn_blocked(act, d_f, "grad_down")
    t_down = ff_by_chip(t_down)
    t_gate, (f_down,) = _mm_tn_blocked(d_gate, h2, "grad_gate", (_sibling_comm([t_down]), first_last(N_FF_TILES)))
    t_gate = ff_by_chip(t_gate)
    t_up, (f_gate,) = _mm_tn_blocked(d_up, h2, "grad_up", (_sibling_comm([t_gate]), first_last(N_FF_TILES)))
    t_up = ff_by_chip(t_up)

    (d_r, d_a, d_b, dp_c, dy_hg, dy_at, acc_mg), (f_up,) = _merge_bwd(
        dx1, r, y_hg, y_at, p_c, w_bh, w_ba, w_o, gate1, (_sibling_comm([t_up]), first_last(N_LAT_TILES)))
    c_down, c_gate, c_up = [_pair_sum(t, f, "pair_sum_" + nm) for t, f, nm in
                            ((t_down, f_down, "down"), (t_gate, f_gate, "gate"), (t_up, f_up, "up"))]
    t_out = _mm_tn(mixed, d_r, tk=1024, nk=2, tm=1024, tn=1024, out_dtype=BF16, name="grad_out")
    t_bh = _mm_tn(d_a, y_hg, tk=2048, nk=1, tm=1024, tn=512, out_dtype=BF16, name="grad_bh")
    t_ba = _mm_tn(d_b, y_at, tk=2048, nk=1, tm=1024, tn=512, out_dtype=BF16, name="grad_ba")
    t_bh, t_ba, t_out = [by_chip(t.reshape(N_DEV, D // N_DEV, t.shape[1])) for t in (t_bh, t_ba, t_out)]
    (dq, dk_rep, dv_rep, dsink), (r_up,) = _attn_bwd(
        qn, k_rep, v_rep, attn_sinks, y_at, lse, dy_at, (_chip_comm([c_up]), first_last(N_BLOCKS)))
    (dp_b, d_o, acc_prep), (f_bh, f_ba, f_out) = _prep_bwd(
        p_b, o, cos, sin, hgrn_norm_w, qnw_t, knw_t, dy_hg, dq, dk_rep, dv_rep,
        (_sibling_comm([t_bh, t_ba, t_out]), first_last(N_TILES)))
    c_bh, c_ba, c_out = [_pair_sum(t, f, "pair_sum_" + nm) for t, f, nm in
                         ((t_bh, f_bh, "bh"), (t_ba, f_ba, "ba"), (t_out, f_out, "out"))]
    (dp_a, dlb), (r_bh, r_ba, r_out, r_down, r_gate) = _hgrn_bwd(
        p_a, lbl, d_o, st, (_chip_comm([c_bh, c_ba, c_out, c_down, c_gate]), first_last(HG_HEADS)))
    t_a = _mm_tn(dp_a, h_all, tk=T, nk=1, tm=1024, tn=1024, out_dtype=BF16, name="grad_in_a")
    t_b = _mm_tn(dp_b, h_all, tk=T, nk=1, tm=640, tn=1024, out_dtype=BF16, name="grad_in_b")
    t_c = _mm_tn(dp_c, h_lat, tk=1024, nk=2, tm=1024, tn=1024, out_dtype=BF16, name="grad_in_c")
    t_in = None
    for piece, nm in ((t_a, "a"), (t_b, "b"), (t_c, "c")):
        t_in = _place_row_blocks(piece, *_block_table(nm), t_in, IN_COLS, "order_terms_" + nm)
    t_in = by_chip(t_in.reshape(N_DEV, IN_BLK, D))
    (f_in,) = _run_comm(_sibling_comm([t_in]), "scatter_in_sibling")
    c_in = _pair_sum(t_in, f_in, "pair_sum_in")
    sems, c_in, land, token = _chip_exchange_start(c_in, jnp.zeros(c_in.shape, c_in.dtype))
    (grad_x, acc_in), _ = _input_bwd(dp_a, dp_b, dp_c, w_a, w_b, w_c, ctx2, x2, dx1,
                                     norm_mix_w + token[0, 0], sh_in, sc_in)
    small = _pack_small(acc_in, acc_mg, acc_ffn, acc_prep, dsink, dlb)
    return grad_x, small, [r_bh, r_ba, r_out, r_gate, r_up, r_down], (sems, c_in, land)


def kernel(x, c, ctx, c_ctx, w_ada, b_ada, norm_mix_w, norm_ffn_w, w_in, hgrn_lb_logits, hgrn_norm_w, q_norm_w, k_norm_w, attn_sinks, w_branch_hgrn, w_branch_attn, w_out, w_ffn_gate, w_ffn_up, w_ffn_down, loss_target, m_c_ctx, m_w_ada, m_b_ada, m_norm_mix_w, m_norm_ffn_w, m_w_in, m_hgrn_lb_logits, m_hgrn_norm_w, m_q_norm_w, m_k_norm_w, m_attn_sinks, m_w_branch_hgrn, m_w_branch_attn, m_w_out, m_w_ffn_gate, m_w_ffn_up, m_w_ffn_down, v_c_ctx, v_w_ada, v_b_ada, v_norm_mix_w, v_norm_ffn_w, v_w_in, v_hgrn_lb_logits, v_hgrn_norm_w, v_q_norm_w, v_k_norm_w, v_attn_sinks, v_w_branch_hgrn, v_w_branch_attn, v_w_out, v_w_ffn_gate, v_w_ffn_up, v_w_ffn_down):
    me = 4 * lax.axis_index("x") + 2 * lax.axis_index("y") + lax.axis_index("c")
    x2, ctx2, tgt = x[0], ctx[0], loss_target[0]
    w_ada2, w_in2 = w_ada[0], w_in[0]

    cond = jnp.zeros((8, D), F32).at[0].set(c[0]).at[1, :256].set(hgrn_lb_logits.reshape(256))
    b_cols = lax.dynamic_slice(b_ada, (0, me * ADA_BLK), (1, ADA_BLK))
    g0, cc, mod, g_in, h_all, h_lat = _prologue(cond, c_ctx.reshape(1, D), w_ada2, b_cols, w_in2.T.astype(BF16),
                                         x2, ctx2, norm_mix_w)
    lbl = jnp.transpose(g0[:, 1, :256].reshape(N_DEV, 2, 2, 64), (1, 2, 0, 3)).reshape(2, 2, HGW)
    sh1, sc1, gate1, sh2, sc2, gate2 = [mod[k:k + 1] for k in range(6)]
    sh_in = jnp.concatenate([mod[6:7], sh1], axis=0)
    sc_in = jnp.concatenate([mod[7:8], sc1], axis=0)

    shards = [w_branch_hgrn[0].T, w_branch_attn[0].T, w_out[0], w_ffn_gate[0].T, w_ffn_up[0].T, w_ffn_down[0]]
    w_in_t = g_in.reshape(IN_COLS, D)
    w_a, w_b, w_c = [_pick_row_blocks(w_in_t, *_block_table(nm), "order_w_" + nm) for nm in "abc"]

    grad_x, small, (r_bh, r_ba, r_out, r_gate, r_up, r_down), pending_in = _local_step(
        x2, ctx2, h_all, h_lat, tgt, lbl, sh_in, sc_in, gate1, sh2, sc2, gate2, norm_mix_w, norm_ffn_w, hgrn_norm_w,
        q_norm_w, k_norm_w, attn_sinks, w_a, w_b, w_c, *[s.astype(BF16) for s in shards])

    big = {}
    for nm, rr, ww, mm, vv, tr, transposed in (
            ("w_branch_hgrn", r_bh, w_branch_hgrn[0], m_w_branch_hgrn[0], v_w_branch_hgrn[0], 128, True),
            ("w_branch_attn", r_ba, w_branch_attn[0], m_w_branch_attn[0], v_w_branch_attn[0], 128, True),
            ("w_out", r_out, w_out[0], m_w_out[0], v_w_out[0], 128, False),
            ("w_ffn_gate", r_gate, w_ffn_gate[0], m_w_ffn_gate[0], v_w_ffn_gate[0], 176, True),
            ("w_ffn_up", r_up, w_ffn_up[0], m_w_ffn_up[0], v_w_ffn_up[0], 176, True),
            ("w_ffn_down", r_down, w_ffn_down[0], m_w_ffn_down[0], v_w_ffn_down[0], 176, False)):
        if transposed:
            res = _adamw_sharded(rr, ww.T, mm.T, vv.T, "adamw_" + nm, tr)
            big[nm] = [t.T[None] for t in res]
        else:
            big[nm] = [t[None] for t in _adamw_sharded(rr, ww, mm, vv, "adamw_" + nm, tr)]

    (g2,) = _all_gather([small], "gather_small", True)
    tot = _sum_small(g2)
    dm = jnp.zeros((16, 6 * D), F32).at[:8].set(g2[:, R_DMOD:R_DMOD + 6, :].reshape(N_DEV, 6 * D))
    dm = dm.at[8, :2 * D].set(tot[R_DCTX:R_DCTX + 2].reshape(2 * D))
    dm_cols = lax.dynamic_slice(dm, (0, me * ADA_BLK), (16, ADA_BLK))
    g_w_ada, dsc_term = _ada_grads(cc, dm_cols, w_ada2)
    (g3,) = _all_gather([dsc_term], "gather_cctx", True)
    g_c_ctx = _c_ctx_grad(g3, c_ctx.reshape(1, D))
    g_lbl = _lb_grads(tot[R_DLB:R_DLB + 2, :HGW], lbl)
    g_lb_mine = lax.dynamic_slice(g_lbl, (0, 0, me * 64), (2, 2, 64))
    misc = tot[R_MISC]
    loss = misc[M_LOSS]

    rep_out = _adamw_replicated(
        tot, g_c_ctx,
        [b_ada, c_ctx.reshape(1, D), norm_mix_w, norm_ffn_w, hgrn_norm_w, q_norm_w, k_norm_w, attn_sinks],
        [m_b_ada, m_c_ctx.reshape(1, D), m_norm_mix_w, m_norm_ffn_w, m_hgrn_norm_w, m_q_norm_w, m_k_norm_w,
         m_attn_sinks],
        [v_b_ada, v_c_ctx.reshape(1, D), v_norm_mix_w, v_norm_ffn_w, v_hgrn_norm_w, v_q_norm_w, v_k_norm_w,
         v_attn_sinks])
    rep = []
    for kind in range(4):
        vals = dict(zip(_REP_NAMES, rep_out[kind * len(_REP_NAMES):(kind + 1) * len(_REP_NAMES)]))
        vals["c_ctx"] = vals["c_ctx"].reshape(D)
        rep.append(vals)

    sems, c_in, land = pending_in
    d_ada, nm_ada, nv_ada = _adamw_plain(g_w_ada, w_ada2, m_w_ada[0], v_w_ada[0], "adamw_w_ada", tr=256)
    land = _chip_exchange_wait(sems, c_in, land, d_ada)
    own = lax.dynamic_index_in_dim(c_in, 2 * lax.axis_index("x") + lax.axis_index("y"), 0, keepdims=False)
    big["w_in"] = [t.T[None] for t in _adamw_sharded(land, w_in2.T, m_w_in[0].T, v_w_in[0].T, "adamw_w_in", 112,
                                                     extra=own)]
    ada = [t[None] for t in (g_w_ada, d_ada, nm_ada, nv_ada)]
    lb_w = hgrn_lb_logits.reshape(4, 64)
    d_lb, nm_lb, nv_lb = _adamw_plain(g_lb_mine.reshape(4, 64), lb_w, m_hgrn_lb_logits.reshape(4, 64),
                                      v_hgrn_lb_logits.reshape(4, 64), "adamw_lb")
    lbs = [t.reshape(2, 2, 64) for t in (g_lb_mine, d_lb, nm_lb, nv_lb)]

    names = ['c_ctx', 'w_ada', 'b_ada', 'norm_mix_w', 'norm_ffn_w', 'w_in', 'hgrn_lb_logits', 'hgrn_norm_w',
             'q_norm_w', 'k_norm_w', 'attn_sinks', 'w_branch_hgrn', 'w_branch_attn', 'w_out', 'w_ffn_gate',
             'w_ffn_up', 'w_ffn_down']
    outs = [loss, grad_x[None]]
    for kind in range(4):
        for nm in names:
            if nm == 'w_ada':
                outs.append(ada[kind])
            elif nm == 'hgrn_lb_logits':
                outs.append(lbs[kind])
            elif nm in big:
                outs.append(big[nm][kind])
            else:
                outs.append(rep[kind][nm])
    return tuple(outs)
```

```python
import functools
import math

import numpy as np
import jax
import jax.numpy as jnp
from jax import lax
from jax.experimental import pallas as pl
from jax.experimental.pallas import tpu as pltpu

F32 = jnp.float32
BF16 = jnp.bfloat16

N_DEV = 8
D = 1024
S = 2048
L = 256
T = L + S
TM = 256
N_TILES = T // TM
N_LAT_TILES = S // TM
HG_HEADS = 4
HG_DIM = 128
HGW = 512
CHUNK = 32
N_CHUNKS = T // CHUNK
N_CTX_CHUNKS = L // CHUNK
ATT_HEADS = 8
KV_HEADS = 2
HEAD_DIM = 64
ATW = 512
KVW = 128
BLOCK = 128
N_BLOCKS = S // BLOCK
GRID_W = 64
ROPE_THETA = 10000.0
D_FF = 2816
FF_BLK = D_FF // N_DEV
FF_TILE = 256
N_FF_TILES = D_FF // FF_TILE
IN_COLS = 5376
IN_BLK = IN_COLS // N_DEV
ADA_BLK = 6 * D // N_DEV
EPS = 1e-6
WA, WB, WC = 2048, 1280, 2048

ADAM_LR = 0.001
ADAM_B1 = 0.9
ADAM_B2 = 0.999
ADAM_EPS = 1e-08
ADAM_WD = 0.01
ADAM_STEP = 10

VMEM_LIMIT = 56 * 1024 * 1024
MESH = pl.DeviceIdType.MESH


def _cp(sem=None, vmem=VMEM_LIMIT):
    return pltpu.CompilerParams(dimension_semantics=sem, vmem_limit_bytes=vmem)


def _full(shape):
    n = len(shape)
    return pl.BlockSpec(shape, lambda *_: (0,) * n)


_VMEM_WHOLE = pl.BlockSpec(memory_space=pltpu.VMEM)
_ANY = pl.BlockSpec(memory_space=pl.ANY)


def _sigmoid(v):
    return 1.0 / (1.0 + jnp.exp(-v))


def _dot(a, b):
    return jnp.dot(a, b, preferred_element_type=F32)


def _dot_nt(a, b):
    return lax.dot_general(a, b, (((1,), (1,)), ((), ())), preferred_element_type=F32)


def _dot_tn(a, b):
    return lax.dot_general(a, b, (((0,), (0,)), ((), ())), preferred_element_type=F32)


def _split3(v):
    hi = v.astype(BF16)
    r = v - hi.astype(F32)
    mid = r.astype(BF16)
    lo = (r - mid.astype(F32)).astype(BF16)
    return hi, mid, lo


def _dot_exact_rhs01(v, m01):
    hi, mid, lo = _split3(v)
    return _dot(hi, m01) + _dot(mid, m01) + _dot(lo, m01)


def _split2(v):
    hi = v.astype(BF16)
    return hi, (v - hi.astype(F32)).astype(BF16)


def _dot_lhs01(m01, v):
    hi, lo = _split2(v)
    return _dot(m01, hi) + _dot(m01, lo)


def _dot_f32(a, b, dot=_dot):
    ah, am, al = _split3(a)
    bh, bm, bl = _split3(b)
    return (dot(ah, bh) + (dot(ah, bm) + dot(am, bh))
            + (dot(am, bm) + dot(ah, bl) + dot(al, bh)))


def _my_pos():
    return lax.axis_index("x"), lax.axis_index("y"), lax.axis_index("c")


class _Comm:
    def __init__(self, operands, out_shapes, sems, phases):
        self.operands, self.out_shapes, self.sems, self.phases = operands, out_shapes, sems, phases


def _gather_comm(blocks):
    n = len(blocks)

    def parts(ins, outs, sems):
        send_sems, recv_sems, local_sems = sems
        x, y, c = _my_pos()
        me, sibling = (x, y, c), (x, y, 1 - c)
        chips = [(1 - x, y), (x, 1 - y), (1 - x, 1 - y)]

        def slot(a, px, py, pc):
            return outs[a].at[4 * px + 2 * py + pc]

        def copy(a, k, block, to, src=None):
            return pltpu.make_async_remote_copy(
                src_ref=slot(a, *block) if src is None else src, dst_ref=slot(a, *block),
                send_sem=send_sems.at[a, k], recv_sem=recv_sems.at[a, k],
                device_id=to, device_id_type=MESH)

        mine = [pltpu.make_async_copy(ins[a], slot(a, *me), local_sems.at[a]) for a in range(n)]
        first = []
        for a in range(n):
            first.append(copy(a, 0, me, sibling, src=ins[a]))
            first += [copy(a, 1 + j, me, (*chip, c), src=ins[a]) for j, chip in enumerate(chips)]
        passed = [copy(a, 4 + j, (*chip, c), sibling) for j, chip in enumerate(chips) for a in range(n)]
        return c, me, sibling, chips, copy, mine, first, passed

    def start(ins, outs, sems):
        _, _, _, _, _, mine, first, _ = parts(ins, outs, sems)
        for cp in mine + first:
            cp.start()

    def forward(ins, outs, sems):
        c, me, _, chips, copy, _, _, passed = parts(ins, outs, sems)
        for j, chip in enumerate(chips):
            for a in range(n):
                copy(a, 1 + j, (*chip, c), me).wait_recv()
                passed[j * n + a].start()

    def finish(ins, outs, sems):
        c, me, sibling, chips, copy, mine, first, passed = parts(ins, outs, sems)
        for a in range(n):
            copy(a, 0, sibling, me).wait_recv()
            for j, chip in enumerate(chips):
                copy(a, 4 + j, (*chip, 1 - c), me).wait_recv()
        for cp in first + passed:
            cp.wait_send()
        for cp in mine:
            cp.wait()

    return _Comm(blocks, [jax.ShapeDtypeStruct((N_DEV,) + b.shape, b.dtype) for b in blocks],
                 [pltpu.SemaphoreType.DMA((n, 7)), pltpu.SemaphoreType.DMA((n, 7)), pltpu.SemaphoreType.DMA((n,))],
                 [start, forward, finish])


def _gather_comm_relayed(blocks):
    n = len(blocks)

    def parts(ins, outs, sems):
        send_sems, recv_sems, local_sems = sems
        x, y, c = _my_pos()
        me, sibling = (x, y, c), (x, y, 1 - c)
        x_nbr, y_nbr, diag = (1 - x, y, c), (x, 1 - y, c), (1 - x, 1 - y, c)

        def slot(a, dev, half=None):
            ref = outs[a].at[4 * dev[0] + 2 * dev[1] + dev[2]]
            if half is None:
                return ref
            rows = blocks[a].shape[0] // 2
            return ref.at[pl.ds(half * rows, rows)]

        def copy(a, k, block, to, half=None, src=None):
            return pltpu.make_async_remote_copy(
                src_ref=slot(a, block, half) if src is None else src, dst_ref=slot(a, block, half),
                send_sem=send_sems.at[a, k], recv_sem=recv_sems.at[a, k],
                device_id=to, device_id_type=MESH)

        mine = [pltpu.make_async_copy(ins[a], slot(a, me), local_sems.at[a]) for a in range(n)]
        return me, sibling, x_nbr, y_nbr, diag, copy, mine

    def start(ins, outs, sems):
        me, sibling, x_nbr, y_nbr, _, copy, mine = parts(ins, outs, sems)
        for cp in mine:
            cp.start()
        for a in range(n):
            for k, to in ((1, x_nbr), (2, y_nbr), (0, sibling)):
                copy(a, k, me, to, src=ins[a]).start()

    def forward(ins, outs, sems):
        me, sibling, x_nbr, y_nbr, _, copy, _ = parts(ins, outs, sems)
        for a in range(n):
            copy(a, 1, x_nbr, me).wait_recv()
            copy(a, 3, x_nbr, y_nbr, half=0).start()
            copy(a, 5, x_nbr, sibling).start()
        for a in range(n):
            copy(a, 2, y_nbr, me).wait_recv()
            copy(a, 4, y_nbr, x_nbr, half=1).start()
            copy(a, 6, y_nbr, sibling).start()

    def finish(ins, outs, sems):
        me, sibling, x_nbr, y_nbr, diag, copy, mine = parts(ins, outs, sems)
        sib = lambda dev: (dev[0], dev[1], sibling[2])
        for a in range(n):
            copy(a, 3, diag, me, half=0).wait_recv()
            copy(a, 4, diag, me, half=1).wait_recv()
            copy(a, 7, diag, sibling).start()
        for a in range(n):
            copy(a, 0, sibling, me).wait_recv()
            for k, dev in ((5, x_nbr), (6, y_nbr), (7, diag)):
                copy(a, k, sib(dev), me).wait_recv()
        for a in range(n):
            for k, block, to, half in ((0, me, sibling, None), (1, me, x_nbr, None), (2, me, y_nbr, None),
                                       (3, x_nbr, y_nbr, 0), (4, y_nbr, x_nbr, 1), (5, x_nbr, sibling, None),
                                       (6, y_nbr, sibling, None), (7, diag, sibling, None)):
                copy(a, k, block, to, half=half, src=ins[a] if block is me else None).wait_send()
        for cp in mine:
            cp.wait()

    return _Comm(blocks, [jax.ShapeDtypeStruct((N_DEV,) + b.shape, b.dtype) for b in blocks],
                 [pltpu.SemaphoreType.DMA((n, 8)), pltpu.SemaphoreType.DMA((n, 8)), pltpu.SemaphoreType.DMA((n,))],
                 [start, forward, finish])


_HBM = pl.BlockSpec(memory_space=pltpu.HBM)
_SEM = pl.BlockSpec(memory_space=pltpu.SEMAPHORE)
_SPLIT_COPY = pltpu.CompilerParams(has_side_effects=pltpu.SideEffectType.DATAFLOW_SIDE_EFFECTING)


def _chip_exchange_copies(src_ref, land_ref, sems):
    x, y, c = _my_pos()
    q_me = 2 * x + y
    pairs = []
    for j, (px, py) in enumerate([(1 - x, y), (x, 1 - y), (1 - x, 1 - y)]):
        q = 2 * px + py
        send = pltpu.make_async_remote_copy(
            src_ref=src_ref.at[q], dst_ref=land_ref.at[q_me], send_sem=sems[j], recv_sem=sems[3 + j],
            device_id=(px, py, c), device_id_type=MESH)
        recv = pltpu.make_async_remote_copy(
            src_ref=src_ref.at[q], dst_ref=land_ref.at[q], send_sem=sems[j], recv_sem=sems[3 + j],
            device_id=(x, y, c), device_id_type=MESH)
        pairs.append((send, recv))
    return pairs


def _chip_exchange_start(src, land):
    def body(src_ref, land_ref, *outs):
        sems, token = outs[:6], outs[8]
        for send, _ in _chip_exchange_copies(src_ref, land_ref, sems):
            send.start()
        token[...] = jnp.zeros_like(token)

    res = pl.pallas_call(
        body, name="scatter_in_start",
        out_shape=(pltpu.SemaphoreType.DMA(()),) * 6 + (
            pltpu.HBM(src.shape, src.dtype), pltpu.HBM(land.shape, land.dtype),
            jax.ShapeDtypeStruct((8, 128), F32)),
        in_specs=(_HBM, _HBM), out_specs=(_SEM,) * 6 + (_HBM, _HBM, pl.BlockSpec(memory_space=pltpu.VMEM)),
        input_output_aliases={0: 6, 1: 7}, compiler_params=_SPLIT_COPY,
    )(pltpu.with_memory_space_constraint(src, pltpu.HBM), pltpu.with_memory_space_constraint(land, pltpu.HBM))
    return res[:6], res[6], res[7], res[8]


def _chip_exchange_wait(sems, src_thru, land_thru, after):
    def body(src_ref, land_ref, *rest):
        for send, recv in _chip_exchange_copies(src_ref, land_ref, rest[:6]):
            send.wait_send()
            recv.wait_recv()

    return pl.pallas_call(
        body, name="scatter_in_wait",
        out_shape=(pltpu.HBM(src_thru.shape, src_thru.dtype), pltpu.HBM(land_thru.shape, land_thru.dtype)),
        in_specs=(_HBM, _HBM) + (_SEM,) * 6 + (_ANY,), out_specs=(_HBM, _HBM),
        input_output_aliases={0: 0, 1: 1}, compiler_params=_SPLIT_COPY,
    )(src_thru, land_thru, *sems, after)[1]


def _run_comm(comm, name, in_vmem=False):
    n_in, n_out = len(comm.operands), len(comm.out_shapes)

    def body(*refs):
        ins, outs, sems = refs[:n_in], refs[n_in:n_in + n_out], refs[n_in + n_out:]
        for phase in comm.phases:
            phase(ins, outs, sems)

    spec = _VMEM_WHOLE if in_vmem else _ANY
    return pl.pallas_call(
        body, name=name, out_shape=comm.out_shapes, in_specs=[spec] * n_in, out_specs=[spec] * n_out,
        scratch_shapes=comm.sems,
    )(*comm.operands)


def _carrier_call(body, comm, schedule, *, name, grid, in_specs, out_specs, out_shape, scratch_shapes, operands):
    n_in, n_out, n_scr = len(in_specs), len(out_specs), len(scratch_shapes)
    c_in, c_out = len(comm.operands), len(comm.out_shapes)

    def full_body(*refs):
        ins, refs = refs[:n_in], refs[n_in:]
        cins, refs = refs[:c_in], refs[c_in:]
        outs, refs = refs[:n_out], refs[n_out:]
        couts, refs = refs[:c_out], refs[c_out:]
        scr, csems = refs[:n_scr], refs[n_scr:]
        step = pl.program_id(0)

        def run(before):
            for (at, when_before), phase in zip(schedule, comm.phases):
                if when_before == before:
                    pl.when(step == at)(functools.partial(phase, cins, couts, csems))

        run(True)
        body(*ins, *outs, *scr)
        run(False)

    res = pl.pallas_call(
        full_body, name=name, grid=grid,
        in_specs=list(in_specs) + [_ANY] * c_in, out_specs=list(out_specs) + [_ANY] * c_out,
        out_shape=list(out_shape) + list(comm.out_shapes),
        scratch_shapes=list(scratch_shapes) + list(comm.sems),
        compiler_params=_cp(("arbitrary",)),
    )(*operands, *comm.operands)
    return res[:n_out], res[n_out:]


def _pcall(body, carried, *, name, grid, in_specs, out_specs, out_shape, scratch_shapes, operands):
    if carried is None:
        res = pl.pallas_call(body, name=name, grid=grid, in_specs=in_specs, out_specs=out_specs,
                             out_shape=out_shape, scratch_shapes=scratch_shapes,
                             compiler_params=_cp(("arbitrary",)))(*operands)
        return res, ()
    return _carrier_call(body, carried[0], carried[1], name=name, grid=grid, in_specs=in_specs,
                         out_specs=out_specs, out_shape=out_shape, scratch_shapes=scratch_shapes,
                         operands=operands)


def _all_gather(blocks, name, in_vmem):
    return _run_comm(_gather_comm(blocks), name, in_vmem)


N_CHIPS = 4


def _sibling_comm(contribs):
    n = len(contribs)

    def copies(ins, outs, sems):
        send_sems, recv_sems = sems
        x, y, c = _my_pos()
        return [pltpu.make_async_remote_copy(
            src_ref=ins[a].at[pl.ds(0, N_CHIPS), 1 - c], dst_ref=outs[a],
            send_sem=send_sems.at[a], recv_sem=recv_sems.at[a],
            device_id=(x, y, 1 - c), device_id_type=MESH) for a in range(n)]

    def start(ins, outs, sems):
        for cp in copies(ins, outs, sems):
            cp.start()

    def finish(ins, outs, sems):
        cps = copies(ins, outs, sems)
        for cp in cps:
            cp.wait_recv()
        for cp in cps:
            cp.wait_send()

    return _Comm(contribs, [jax.ShapeDtypeStruct((N_CHIPS,) + b.shape[2:], b.dtype) for b in contribs],
                 [pltpu.SemaphoreType.DMA((n,)), pltpu.SemaphoreType.DMA((n,))], [start, finish])


def _pair_sum(mine, theirs, name):
    _, _, rows, cols = mine.shape
    core = lax.axis_index("c").astype(jnp.int32).reshape(1)

    def body(c_ref, m_ref, t_ref, o_ref):
        o_ref[...] = (m_ref[...].astype(F32) + t_ref[...].astype(F32)).astype(BF16)

    return pl.pallas_call(
        body, name=name,
        grid_spec=pltpu.PrefetchScalarGridSpec(
            num_scalar_prefetch=1, grid=(N_CHIPS,),
            in_specs=[pl.BlockSpec((None, None, rows, cols), lambda q, c: (q, c[0], 0, 0)),
                      pl.BlockSpec((None, rows, cols), lambda q, c: (q, 0, 0))],
            out_specs=pl.BlockSpec((None, rows, cols), lambda q, c: (q, 0, 0))),
        out_shape=jax.ShapeDtypeStruct((N_CHIPS, rows, cols), BF16),
        compiler_params=_cp(("parallel",)),
    )(core, mine, theirs)


def _chip_comm(sums):
    n = len(sums)

    def parts(ins, outs, sems):
        send_sems, recv_sems, local_sems = sems
        x, y, c = _my_pos()
        q_me = 2 * x + y
        chips = [(1 - x, y), (x, 1 - y), (1 - x, 1 - y)]
        mine = [pltpu.make_async_copy(ins[a].at[q_me], outs[a].at[q_me], local_sems.at[a]) for a in range(n)]
        sends, recvs = [], []
        for j, (px, py) in enumerate(chips):
            for a in range(n):
                q = 2 * px + py
                sends.append(pltpu.make_async_remote_copy(
                    src_ref=ins[a].at[q], dst_ref=outs[a].at[q_me],
                    send_sem=send_sems.at[a, j], recv_sem=recv_sems.at[a, j],
                    device_id=(px, py, c), device_id_type=MESH))
                recvs.append(pltpu.make_async_remote_copy(
                    src_ref=ins[a].at[q], dst_ref=outs[a].at[q],
                    send_sem=send_sems.at[a, j], recv_sem=recv_sems.at[a, j],
                    device_id=(x, y, c), device_id_type=MESH))
        return mine, sends, recvs

    def start(ins, outs, sems):
        mine, sends, _ = parts(ins, outs, sems)
        for cp in mine + sends:
            cp.start()

    def finish(ins, outs, sems):
        mine, sends, recvs = parts(ins, outs, sems)
        for cp in recvs:
            cp.wait_recv()
        for cp in sends:
            cp.wait_send()
        for cp in mine:
            cp.wait()

    return _Comm(sums, [jax.ShapeDtypeStruct(b.shape, b.dtype) for b in sums],
                 [pltpu.SemaphoreType.DMA((n, 3)), pltpu.SemaphoreType.DMA((n, 3)), pltpu.SemaphoreType.DMA((n,))],
                 [start, finish])


def _mm_nt(a, bt, *, tm, tn, out_dtype, name, row_off=0, rows=None):
    rows = a.shape[0] if rows is None else rows
    n, k = bt.shape

    def body(a_ref, b_ref, o_ref):
        o_ref[...] = _dot_nt(a_ref[...], b_ref[...]).astype(out_dtype)

    return pl.pallas_call(
        body, name=name, grid=(rows // tm, n // tn),
        in_specs=[pl.BlockSpec((tm, k), lambda i, j: (i + row_off, 0)),
                  pl.BlockSpec((tn, k), lambda i, j: (j, 0))],
        out_specs=pl.BlockSpec((tm, tn), lambda i, j: (i, j)),
        out_shape=jax.ShapeDtypeStruct((rows, n), out_dtype),
        compiler_params=_cp(("parallel", "parallel")),
    )(a, bt)


def _mm_tn(a, b, *, tk, nk, tm, tn, out_dtype, name, a_off=0, b_off=0):
    m, n = a.shape[1], b.shape[1]

    def body(a_ref, b_ref, o_ref, acc):
        kk = pl.program_id(2)

        @pl.when(kk == 0)
        def _():
            acc[...] = jnp.zeros_like(acc)

        acc[...] += _dot_tn(a_ref[...], b_ref[...])

        @pl.when(kk == nk - 1)
        def _():
            o_ref[...] = acc[...].astype(out_dtype)

    return pl.pallas_call(
        body, name=name, grid=(m // tm, n // tn, nk),
        in_specs=[pl.BlockSpec((tk, tm), lambda i, j, kk: (kk + a_off, i)),
                  pl.BlockSpec((tk, tn), lambda i, j, kk: (kk + b_off, j))],
        out_specs=pl.BlockSpec((tm, tn), lambda i, j, kk: (i, j)),
        out_shape=jax.ShapeDtypeStruct((m, n), out_dtype),
        scratch_shapes=[pltpu.VMEM((tm, tn), F32)],
        compiler_params=_cp(("parallel", "parallel", "arbitrary")),
    )(a, b)


def _mm_tn_blocked(a, b, name, carried=None):
    nb, _, w = a.shape
    n = b.shape[1]

    def body(a_ref, b_ref, o_ref):
        o_ref[...] = _dot_tn(a_ref[...], b_ref[...]).astype(BF16)

    (out,), extra = _pcall(
        body, carried, name=name, grid=(nb,),
        in_specs=[pl.BlockSpec((None, S, w), lambda j: (j, 0, 0)), _full((S, n))],
        out_specs=[pl.BlockSpec((None, w, n), lambda j: (j, 0, 0))],
        out_shape=[jax.ShapeDtypeStruct((nb, w, n), BF16)],
        scratch_shapes=[], operands=[a, b])
    return out, extra


def _prologue(cond, c_ctx, w_ada, b_cols, w_in_t, x, ctx, nw):
    rows_shape = jax.ShapeDtypeStruct((16, ADA_BLK), F32)
    big, g_cond, g_mod = _gather_comm_relayed([w_in_t]), _gather_comm([cond]), _gather_comm([rows_shape])

    def body(cond_ref, cctx_ref, wada_ref, b_ref, nw_ref, win_ref, x_ref, ctx_ref,
             g0_ref, cc_ref, mod_ref, gin_ref, h_ref, hl_ref, rows_ref, g1_ref, x_s, ctx_s, h_s, io_sems, *sems):
        s_big, s_cond, s_mod = sems[0:3], sems[3:6], sems[6:9]
        big.phases[0]([win_ref], [gin_ref], s_big)
        load_x = pltpu.make_async_copy(x_ref, x_s, io_sems.at[0])
        load_ctx = pltpu.make_async_copy(ctx_ref, ctx_s, io_sems.at[1])
        load_x.start()
        load_ctx.start()
        for phase in g_cond.phases:
            phase([cond_ref], [g0_ref], s_cond)
        cc_ref[...] = jnp.zeros_like(cc_ref)
        for j in range(N_DEV):
            cc_ref[j:j + 1, :] = g0_ref[j, 0:1, :]
        cc_ref[N_DEV:N_DEV + 1, :] = cctx_ref[...]
        cv = cc_ref[...]
        rows_ref[...] = _dot_f32(cv * _sigmoid(cv), wada_ref[...]) + b_ref[...]
        for phase in g_mod.phases:
            phase([rows_ref], [g1_ref], s_mod)
        x_pos, y_pos, c_pos = _my_pos()
        me = 4 * x_pos + 2 * y_pos + c_pos
        mine = jnp.concatenate([g1_ref[j, pl.ds(me, 1), :] for j in range(N_DEV)], axis=1)
        shared = jnp.concatenate([g1_ref[j, N_DEV:N_DEV + 1, :] for j in range(N_DEV)], axis=1)
        for k in range(6):
            mod_ref[k:k + 1, :] = mine[:, k * D:(k + 1) * D]
        mod_ref[6:7, :] = shared[:, 0:D]
        mod_ref[7:8, :] = shared[:, D:2 * D]
        load_ctx.wait()
        load_x.wait()
        h_s[pl.ds(0, L), :] = _rms_mod(ctx_s[...], nw_ref[...], mod_ref[6:7, :], mod_ref[7:8, :]).astype(BF16)

        def norm_tile(i, carry):
            r0 = pl.multiple_of(i * TM, TM)
            h_s[pl.ds(L + r0, TM), :] = _rms_mod(
                x_s[pl.ds(r0, TM), :], nw_ref[...], mod_ref[0:1, :], mod_ref[1:2, :]).astype(BF16)
            return carry

        lax.fori_loop(0, N_LAT_TILES, norm_tile, 0)
        stores = [pltpu.make_async_copy(h_s, h_ref, io_sems.at[2]),
                  pltpu.make_async_copy(h_s.at[pl.ds(L, S)], hl_ref, io_sems.at[3])]
        for cp in stores:
            cp.start()
        big.phases[1]([win_ref], [gin_ref], s_big)
        big.phases[2]([win_ref], [gin_ref], s_big)
        for cp in stores:
            cp.wait()

    return pl.pallas_call(
        body, name="prologue",
        in_specs=[_VMEM_WHOLE] * 5 + [_ANY] * 3, out_specs=[_VMEM_WHOLE] * 3 + [_ANY] * 3,
        out_shape=[g_cond.out_shapes[0], jax.ShapeDtypeStruct((16, D), F32), jax.ShapeDtypeStruct((8, D), F32),
                   big.out_shapes[0], jax.ShapeDtypeStruct((T, D), BF16), jax.ShapeDtypeStruct((S, D), BF16)],
        scratch_shapes=[pltpu.VMEM((16, ADA_BLK), F32), pltpu.VMEM((N_DEV, 16, ADA_BLK), F32),
                        pltpu.VMEM((S, D), F32), pltpu.VMEM((L, D), F32), pltpu.VMEM((T, D), BF16),
                        pltpu.SemaphoreType.DMA((4,))] + big.sems + g_cond.sems + g_mod.sems,
        compiler_params=_cp(),
    )(cond, c_ctx, w_ada, b_cols, nw, w_in_t, x, ctx)


def _ada_grads(cc, dm_cols, w_ada):
    def body(c_ref, dm_ref, w_ref, gw_ref, dsc_ref):
        cv = c_ref[...]
        sc = cv * _sigmoid(cv)
        dm = dm_ref[...]
        gw_ref[...] = _dot_f32(sc, dm, dot=_dot_tn)
        dsc_ref[...] = _dot_f32(dm, w_ref[...], dot=_dot_nt)

    return pl.pallas_call(
        body, name="ada_grads",
        in_specs=[_VMEM_WHOLE] * 3, out_specs=[_VMEM_WHOLE] * 2,
        out_shape=[jax.ShapeDtypeStruct((D, ADA_BLK), F32), jax.ShapeDtypeStruct((16, D), F32)],
        compiler_params=_cp(),
    )(cc, dm_cols, w_ada)


def _lat(i):
    return jnp.maximum(i - 1, 0)


def _rms_mod(xv, nw, sh, sc):
    rstd = lax.rsqrt(jnp.mean(xv * xv, axis=-1, keepdims=True) + EPS)
    return (xv * rstd * nw) * (1.0 + sc) + sh


def _rms_mod_bwd(xv, nw, sc, dh):
    rstd = lax.rsqrt(jnp.mean(xv * xv, axis=-1, keepdims=True) + EPS)
    xhat = xv * rstd
    dn = dh * (1.0 + sc)
    dxhat = dn * nw
    dx = rstd * (dxhat - xhat * jnp.mean(dxhat * xhat, axis=-1, keepdims=True))
    return (dx, jnp.sum(dh, axis=0, keepdims=True), jnp.sum(dh * (xhat * nw), axis=0, keepdims=True),
            jnp.sum(dn * xhat, axis=0, keepdims=True))


def _chunk_masks(reverse):
    row = lax.broadcasted_iota(jnp.int32, (TM, TM), 0)
    col = lax.broadcasted_iota(jnp.int32, (TM, TM), 1)
    same = (row // CHUNK) == (col // CHUNK)
    tri = same & ((col >= row) if reverse else (col <= row))
    return same, tri


def _chunk_order(i, reverse):
    if not reverse:
        return i
    return jnp.where(i < N_CTX_CHUNKS, N_CTX_CHUNKS - 1 - i, N_CHUNKS + N_CTX_CHUNKS - 1 - i)


def _decay_terms(z, lb, same01, tri01):
    f = lb + (1.0 - lb) * _sigmoid(z)
    g = jnp.log(f)
    g2 = jnp.concatenate(_split2(g), axis=1)
    b2 = _dot(tri01, g2)
    t2 = _dot(same01, g2)
    return f, 1.0 - f, b2[:, :HG_DIM] + b2[:, HG_DIM:], t2[:, :HG_DIM] + t2[:, HG_DIM:]


def _chunk_outer(a, b):
    n = TM // CHUNK
    return jnp.einsum('ncv,nck->nvk', a.reshape(n, CHUNK, HG_DIM), b.reshape(n, CHUNK, HG_DIM),
                      preferred_element_type=F32)


def _hgrn_fwd(p_a, lbl, carried=None):
    cpt = TM // CHUNK

    def body(p_ref, lbl_ref, o_ref, st_ref, qd_s, kd_s, u_s, v_s, ebt_s):
        masks = [_chunk_masks(d == 1) for d in (0, 1)]
        same01 = jnp.where(masks[0][0], 1.0, 0.0).astype(BF16)
        tri = [m[1] for m in masks]
        tri01 = [jnp.where(t, 1.0, 0.0).astype(BF16) for t in tri]
        lb = [_sigmoid(lbl_ref[d][0:1, :] - lbl_ref[d][1:2, :]) for d in (0, 1)]

        def prep(r, carry):
            r0 = pl.multiple_of(r * TM, TM)
            vb = p_ref[pl.ds(r0, TM), 2 * HG_DIM:3 * HG_DIM].astype(BF16)
            v_s[pl.ds(r0, TM), :] = vb
            for d in (0, 1):
                z = p_ref[pl.ds(r0, TM), d * HG_DIM:(d + 1) * HG_DIM]
                _, k, b, bt = _decay_terms(z, lb[d], same01, tri01[d])
                u_s[d, pl.ds(r * cpt, cpt)] = _chunk_outer(vb, (k * jnp.exp(bt - b)).astype(BF16))
                ebt_s[d, pl.ds(r0, TM), :] = jnp.exp(bt)

                @pl.when(r >= 1)
                def _():
                    rl = pl.multiple_of(r0 - L, TM)
                    qr = p_ref[pl.ds(r0, TM), 3 * HG_DIM:4 * HG_DIM]
                    q = qr * _sigmoid(qr) * HG_DIM ** -0.5
                    qd_s[d, pl.ds(rl, TM), :] = (q * jnp.exp(b)).astype(BF16)
                    kd_s[d, pl.ds(rl, TM), :] = (k * jnp.exp(-b)).astype(BF16)

            return carry

        lax.fori_loop(0, N_TILES, prep, 0)

        def scan(i, sts):
            new = []
            for d in (0, 1):
                nn = _chunk_order(i, d == 1)
                c0 = pl.multiple_of(nn * CHUNK, CHUNK)
                st_ref[d, nn] = sts[d].astype(BF16)
                new.append(sts[d] * ebt_s[d, pl.ds(c0, 1), :] + u_s[d, nn])
            return tuple(new)

        zero = jnp.zeros((HG_DIM, HG_DIM), F32)
        lax.fori_loop(0, N_CHUNKS, scan, (zero, zero))

        def outp(r, carry):
            r0 = pl.multiple_of(r * TM, TM)
            vb = v_s[pl.ds(r0 + L, TM), :]
            o = jnp.zeros((TM, HG_DIM), F32)
            for d in (0, 1):
                qd = qd_s[d, pl.ds(r0, TM), :]
                a = jnp.where(tri[d], _dot_nt(qd, kd_s[d, pl.ds(r0, TM), :]), 0.0)
                stb = st_ref[d, pl.ds(N_CTX_CHUNKS + r * cpt, cpt)]
                inter = jnp.einsum('nck,nvk->ncv', qd.reshape(cpt, CHUNK, HG_DIM), stb,
                                   preferred_element_type=F32)
                o = o + _dot(a.astype(BF16), vb) + inter.reshape(TM, HG_DIM)
            o_ref[pl.ds(r0, TM), :] = o
            return carry

        lax.fori_loop(0, N_LAT_TILES, outp, 0, unroll=2)

    return _pcall(
        body, carried, name="hgrn_fwd", grid=(HG_HEADS,),
        in_specs=[pl.BlockSpec((T, 4 * HG_DIM), lambda h: (0, h)),
                  pl.BlockSpec((2, 2, HG_DIM), lambda h: (0, 0, h))],
        out_specs=[pl.BlockSpec((S, HG_DIM), lambda h: (0, h)),
                   pl.BlockSpec((2, None, N_CHUNKS, HG_DIM, HG_DIM), lambda h: (0, h, 0, 0, 0))],
        out_shape=[jax.ShapeDtypeStruct((S, HGW), F32),
                   jax.ShapeDtypeStruct((2, HG_HEADS, N_CHUNKS, HG_DIM, HG_DIM), BF16)],
        scratch_shapes=[pltpu.VMEM((2, S, HG_DIM), BF16), pltpu.VMEM((2, S, HG_DIM), BF16),
                        pltpu.VMEM((2, N_CHUNKS, HG_DIM, HG_DIM), F32), pltpu.VMEM((T, HG_DIM), BF16),
                        pltpu.VMEM((2, T, HG_DIM), F32)],
        operands=[p_a, lbl])


def _hgrn_bwd(p_a, lbl, d_o, st, carried=None):
    cpt = TM // CHUNK

    def rows(r):
        return r * TM if isinstance(r, int) else pl.multiple_of(r * TM, TM)

    def body(p_ref, lbl_ref, do_ref, st_ref, dp_ref, dlb_ref, b_s, bt_s, dbt_s, qd_s, dst_s, w_s):
        masks = [_chunk_masks(d == 1) for d in (0, 1)]
        same01 = jnp.where(masks[0][0], 1.0, 0.0).astype(BF16)
        tri = [m[1] for m in masks]
        tri01 = [jnp.where(t, 1.0, 0.0).astype(BF16) for t in tri]
        later01 = [tri01[1], tri01[0]]
        lb = [_sigmoid(lbl_ref[d][0:1, :] - lbl_ref[d][1:2, :]) for d in (0, 1)]

        def prep_tile(r, latent):
            r0 = rows(r)
            for d in (0, 1):
                z = p_ref[pl.ds(r0, TM), d * HG_DIM:(d + 1) * HG_DIM]
                _, _, b, bt = _decay_terms(z, lb[d], same01, tri01[d])
                b_s[d, pl.ds(r0, TM), :] = b
                bt_s[d, pl.ds(r0, TM), :] = bt
                if latent:
                    rl = pl.multiple_of(r0 - L, TM)
                    qr = p_ref[pl.ds(r0, TM), 3 * HG_DIM:4 * HG_DIM]
                    qd = (qr * _sigmoid(qr) * HG_DIM ** -0.5 * jnp.exp(b)).astype(BF16)
                    qd_s[d, pl.ds(rl, TM), :] = qd
                    w_s[d, pl.ds(r * cpt, cpt)] = _chunk_outer(
                        do_ref[pl.ds(rl, TM), :].astype(BF16), qd).astype(BF16)

        prep_tile(0, False)
        w_s[:, pl.ds(0, N_CTX_CHUNKS)] = jnp.zeros((2, N_CTX_CHUNKS, HG_DIM, HG_DIM), BF16)

        def prep(r, carry):
            prep_tile(r, True)
            return carry

        lax.fori_loop(1, N_TILES, prep, 0, unroll=2)

        def rscan(j, dsts):
            i = N_CHUNKS - 1 - j
            new = []
            for d in (0, 1):
                nn = _chunk_order(i, d == 1)
                c0 = pl.multiple_of(nn * CHUNK, CHUNK)
                dst_s[d, nn] = dsts[d].astype(BF16)
                after = st_ref[d, _chunk_order(jnp.minimum(i + 1, N_CHUNKS - 1), d == 1)].astype(F32)
                dbt_s[d, pl.ds(c0, CHUNK), :] = jnp.broadcast_to(
                    jnp.sum(after * dsts[d], axis=0, keepdims=True), (CHUNK, HG_DIM))
                new.append(dsts[d] * jnp.exp(bt_s[d, pl.ds(c0, 1), :]) + w_s[d, nn].astype(F32))
            return tuple(new)

        zero = jnp.zeros((HG_DIM, HG_DIM), F32)
        lax.fori_loop(0, N_CHUNKS, rscan, (zero, zero))

        def grad_tile(r, latent):
            r0 = rows(r)
            vb = p_ref[pl.ds(r0, TM), 2 * HG_DIM:3 * HG_DIM].astype(BF16)
            dv = jnp.zeros((TM, HG_DIM), F32)
            dq = jnp.zeros((TM, HG_DIM), F32)
            dlbs = []
            if latent:
                rl = pl.multiple_of(r0 - L, TM)
                qr = p_ref[pl.ds(r0, TM), 3 * HG_DIM:4 * HG_DIM]
                sq = _sigmoid(qr)
                do = do_ref[pl.ds(rl, TM), :].astype(BF16)
                da_full = _dot_nt(do, vb)
            for d in (0, 1):
                z = p_ref[pl.ds(r0, TM), d * HG_DIM:(d + 1) * HG_DIM]
                sz = _sigmoid(z)
                f = lb[d] + (1.0 - lb[d]) * sz
                k = 1.0 - f
                b = b_s[d, pl.ds(r0, TM), :]
                e2 = jnp.exp(bt_s[d, pl.ds(r0, TM), :] - b)
                dstb = dst_s[d, pl.ds(r * cpt, cpt)]
                kd2 = k * e2
                dkd2 = jnp.einsum('ncv,nvk->nck', vb.reshape(cpt, CHUNK, HG_DIM), dstb,
                                  preferred_element_type=F32).reshape(TM, HG_DIM)
                dv = dv + jnp.einsum('nck,nvk->ncv', kd2.astype(BF16).reshape(cpt, CHUNK, HG_DIM), dstb,
                                     preferred_element_type=F32).reshape(TM, HG_DIM)
                dk = dkd2 * e2
                db = -(kd2 * dkd2)
                if latent:
                    eb = jnp.exp(b)
                    enb = jnp.exp(-b)
                    qdf = qr * sq * HG_DIM ** -0.5 * eb
                    kdf = k * enb
                    qd = qd_s[d, pl.ds(rl, TM), :]
                    kd = kdf.astype(BF16)
                    a = jnp.where(tri[d], _dot_nt(qd, kd), 0.0).astype(BF16)
                    da = jnp.where(tri[d], da_full, 0.0).astype(BF16)
                    stb = st_ref[d, pl.ds(r * cpt, cpt)]
                    dqd = _dot(da, kd) + jnp.einsum(
                        'ncv,nvk->nck', do.reshape(cpt, CHUNK, HG_DIM), stb,
                        preferred_element_type=F32).reshape(TM, HG_DIM)
                    dkd = _dot_tn(da, qd)
                    dv = dv + _dot_tn(a, do)
                    dk = dk + dkd * enb
                    db = db + qdf * dqd - kdf * dkd
                    dq = dq + dqd * eb
                dg = _dot_lhs01(later01[d], db) + dbt_s[d, pl.ds(r0, TM), :]
                df = dg / f - dk
                dp_ref[pl.ds(r0, TM), d * HG_DIM:(d + 1) * HG_DIM] = (
                    df * (1.0 - lb[d]) * sz * (1.0 - sz)).astype(BF16)
                dlbs.append(jnp.sum(df * (1.0 - sz), axis=0, keepdims=True))
            dp_ref[pl.ds(r0, TM), 2 * HG_DIM:3 * HG_DIM] = dv.astype(BF16)
            if latent:
                dq = dq * (HG_DIM ** -0.5) * (sq * (1.0 + qr * (1.0 - sq)))
            dp_ref[pl.ds(r0, TM), 3 * HG_DIM:4 * HG_DIM] = dq.astype(BF16)
            return dlbs

        dlb_ctx = grad_tile(0, False)

        def grads(r, acc):
            t = grad_tile(r, True)
            return (acc[0] + t[0], acc[1] + t[1])

        dlb = lax.fori_loop(1, N_TILES, grads, (dlb_ctx[0], dlb_ctx[1]))
        dlb_ref[0:1, :] = dlb[0]
        dlb_ref[1:2, :] = dlb[1]

    return _pcall(
        body, carried, name="hgrn_bwd", grid=(HG_HEADS,),
        in_specs=[pl.BlockSpec((T, 4 * HG_DIM), lambda h: (0, h)),
                  pl.BlockSpec((2, 2, HG_DIM), lambda h: (0, 0, h)),
                  pl.BlockSpec((S, HG_DIM), lambda h: (0, h)),
                  pl.BlockSpec((2, None, N_CHUNKS, HG_DIM, HG_DIM), lambda h: (0, h, 0, 0, 0))],
        out_specs=[pl.BlockSpec((T, 4 * HG_DIM), lambda h: (0, h)),
                   pl.BlockSpec((2, HG_DIM), lambda h: (0, h))],
        out_shape=[jax.ShapeDtypeStruct((T, WA), BF16), jax.ShapeDtypeStruct((2, HGW), F32)],
        scratch_shapes=[pltpu.VMEM((2, T, HG_DIM), F32), pltpu.VMEM((2, T, HG_DIM), F32),
                        pltpu.VMEM((2, T, HG_DIM), F32), pltpu.VMEM((2, S, HG_DIM), BF16),
                        pltpu.VMEM((2, N_CHUNKS, HG_DIM, HG_DIM), BF16),
                        pltpu.VMEM((2, N_CHUNKS, HG_DIM, HG_DIM), BF16)],
        operands=[p_a, lbl, d_o, st])


def _rope_tables():
    t = np.arange(S)
    inv = ROPE_THETA ** (-np.arange(0, 32, 2, dtype=np.float64) / 32)
    lane = np.arange(64)
    pos = np.where(lane[None, :] < 32, (t // GRID_W)[:, None], (t % GRID_W)[:, None]).astype(np.float64)
    ang = pos * inv[(lane % 32) % 16][None, :]
    sign = np.where((lane % 32) < 16, -1.0, 1.0)[None, :]
    cos = np.tile(np.cos(ang), (1, 2)).astype(np.float32)
    sin = np.tile(np.sin(ang) * sign, (1, 2)).astype(np.float32)
    return jnp.asarray(cos), jnp.asarray(sin)


def _rope_partner(v):
    lane = lax.broadcasted_iota(jnp.int32, (1, 128), 1)
    first = (lane % 32) < 16
    slabs = []
    for j in range(v.shape[1] // 128):
        s = v[:, 128 * j:128 * (j + 1)]
        slabs.append(jnp.where(first, pltpu.roll(s, 112, 1), pltpu.roll(s, 16, 1)))
    return slabs[0] if len(slabs) == 1 else jnp.concatenate(slabs, axis=1)


def _group_ones(width, group):
    r = lax.broadcasted_iota(jnp.int32, (width, width), 0)
    c = lax.broadcasted_iota(jnp.int32, (width, width), 1)
    return jnp.where((r // group) == (c // group), 1.0, 0.0).astype(BF16)


def _group_mean(v, ones01, group):
    hi = v.astype(BF16)
    lo = (v - hi.astype(F32)).astype(BF16)
    return (_dot(hi, ones01) + _dot(lo, ones01)) * (1.0 / group)


def _rep_matrix():
    r = lax.broadcasted_iota(jnp.int32, (KVW, ATW), 0)
    c = lax.broadcasted_iota(jnp.int32, (KVW, ATW), 1)
    return jnp.where(r == HEAD_DIM * (c // 256) + c % HEAD_DIM, 1.0, 0.0).astype(BF16)


def _tile_lanes(v, reps):
    return jnp.concatenate([v] * reps, axis=1)


def _prep_fwd(p_b, o, cos, sin, hnw, qnw, knw):
    def body(p_ref, o_ref, cos_ref, sin_ref, hnw_ref, qnw_ref, knw_ref, y_ref, q_ref, k_ref, v_ref):
        i = pl.program_id(0)
        rep = _rep_matrix()
        ones_k = _group_ones(KVW, HEAD_DIM)
        kr = p_ref[:, 1024:1152]
        krstd = lax.rsqrt(_group_mean(kr * kr, ones_k, HEAD_DIM) + EPS)
        kn = kr * krstd * knw_ref[...]
        v_ref[...] = _dot(p_ref[:, 1152:1280].astype(BF16), rep).astype(BF16)

        @pl.when(i == 0)
        def _():
            k_ref[...] = _dot(kn.astype(BF16), rep).astype(BF16)

        @pl.when(i > 0)
        def _():
            cs, sn = cos_ref[...], sin_ref[...]
            kro = kn * cs + _rope_partner(kn) * sn
            k_ref[...] = _dot(kro.astype(BF16), rep).astype(BF16)
            qr = p_ref[:, 512:1024]
            qrstd = lax.rsqrt(_group_mean(qr * qr, _group_ones(ATW, HEAD_DIM), HEAD_DIM) + EPS)
            qn = qr * qrstd * qnw_ref[...]
            qro = qn * _tile_lanes(cs, 4) + _rope_partner(qn) * _tile_lanes(sn, 4)
            q_ref[...] = (qro * HEAD_DIM ** -0.5).astype(BF16)
            ys = []
            for h in range(HG_HEADS):
                oh = o_ref[:, HG_DIM * h:HG_DIM * (h + 1)]
                gh = p_ref[:, HG_DIM * h:HG_DIM * (h + 1)]
                rstd = lax.rsqrt(jnp.mean(oh * oh, axis=-1, keepdims=True) + EPS)
                ys.append(oh * rstd * hnw_ref[...] * (gh * _sigmoid(gh)))
            y_ref[...] = jnp.concatenate(ys, axis=1).astype(BF16)

    return pl.pallas_call(
        body, name="prep_fwd", grid=(N_TILES,),
        in_specs=[pl.BlockSpec((TM, WB), lambda i: (i, 0)),
                  pl.BlockSpec((TM, HGW), lambda i: (_lat(i), 0)),
                  pl.BlockSpec((TM, 128), lambda i: (_lat(i), 0)),
                  pl.BlockSpec((TM, 128), lambda i: (_lat(i), 0)),
                  _full((1, HG_DIM)), _full((1, ATW)), _full((1, KVW))],
        out_specs=[pl.BlockSpec((TM, HGW), lambda i: (_lat(i), 0)),
                   pl.BlockSpec((TM, ATW), lambda i: (_lat(i), 0)),
                   pl.BlockSpec((TM, ATW), lambda i: (i, 0)),
                   pl.BlockSpec((TM, ATW), lambda i: (i, 0))],
        out_shape=[jax.ShapeDtypeStruct((S, HGW), BF16), jax.ShapeDtypeStruct((S, ATW), BF16),
                   jax.ShapeDtypeStruct((T, ATW), BF16), jax.ShapeDtypeStruct((T, ATW), BF16)],
        compiler_params=_cp(("arbitrary",)),
    )(p_b, o, cos, sin, hnw, qnw, knw)


def _prep_bwd(p_b, o, cos, sin, hnw, qnw, knw, dy_hg, dq, dk_rep, dv_rep, carried=None):
    def body(p_ref, o_ref, cos_ref, sin_ref, hnw_ref, qnw_ref, knw_ref, dy_ref, dq_ref, dk_ref, dv_ref,
             dp_ref, do_ref, acc_ref):
        i = pl.program_id(0)

        @pl.when(i == 0)
        def _():
            acc_ref[...] = jnp.zeros_like(acc_ref)

        rep = _rep_matrix()
        ones_k = _group_ones(KVW, HEAD_DIM)

        def fold(v):
            hi = v.astype(BF16)
            lo = (v - hi.astype(F32)).astype(BF16)
            return _dot_nt(hi, rep) + _dot_nt(lo, rep)

        kr = p_ref[:, 1024:1152]
        krstd = lax.rsqrt(_group_mean(kr * kr, ones_k, HEAD_DIM) + EPS)
        khat = kr * krstd
        kw = knw_ref[...]
        dkro = fold(dk_ref[...])
        dv = fold(dv_ref[...])

        def k_back(dkn):
            dkhat = dkn * kw
            dkr = krstd * (dkhat - khat * _group_mean(dkhat * khat, ones_k, HEAD_DIM))
            acc_ref[2:3, 0:KVW] += jnp.sum(dkn * khat, axis=0, keepdims=True)
            dp_ref[:, 1024:1152] = dkr.astype(BF16)
            dp_ref[:, 1152:1280] = dv.astype(BF16)

        @pl.when(i == 0)
        def _():
            k_back(dkro)
            dp_ref[:, 0:1024] = jnp.zeros((TM, 1024), BF16)

        @pl.when(i > 0)
        def _():
            cs, sn = cos_ref[...], sin_ref[...]
            k_back(dkro * cs + _rope_partner(dkro * sn))
            ones_q = _group_ones(ATW, HEAD_DIM)
            qr = p_ref[:, 512:1024]
            qrstd = lax.rsqrt(_group_mean(qr * qr, ones_q, HEAD_DIM) + EPS)
            qhat = qr * qrstd
            dqro = dq_ref[...] * HEAD_DIM ** -0.5
            dqn = dqro * _tile_lanes(cs, 4) + _rope_partner(dqro * _tile_lanes(sn, 4))
            dqhat = dqn * qnw_ref[...]
            dqr = qrstd * (dqhat - qhat * _group_mean(dqhat * qhat, ones_q, HEAD_DIM))
            acc_ref[1:2, :] += jnp.sum(dqn * qhat, axis=0, keepdims=True)
            dp_ref[:, 512:1024] = dqr.astype(BF16)
            dws = jnp.zeros((1, HG_DIM), F32)
            for h in range(HG_HEADS):
                sl = slice(HG_DIM * h, HG_DIM * (h + 1))
                oh, gh, dy = o_ref[:, sl], p_ref[:, sl], dy_ref[:, sl]
                rstd = lax.rsqrt(jnp.mean(oh * oh, axis=-1, keepdims=True) + EPS)
                ohat = oh * rstd
                sg = _sigmoid(gh)
                dp_ref[:, sl] = (dy * (ohat * hnw_ref[...]) * (sg * (1.0 + gh * (1.0 - sg)))).astype(BF16)
                dn = dy * (gh * sg)
                dws = dws + jnp.sum(dn * ohat, axis=0, keepdims=True)
                dohat = dn * hnw_ref[...]
                do_ref[:, sl] = rstd * (dohat - ohat * jnp.mean(dohat * ohat, axis=-1, keepdims=True))
            acc_ref[0:1, 0:HG_DIM] += dws

    return _pcall(
        body, carried, name="prep_bwd", grid=(N_TILES,),
        in_specs=[pl.BlockSpec((TM, WB), lambda i: (i, 0)),
                  pl.BlockSpec((TM, HGW), lambda i: (_lat(i), 0)),
                  pl.BlockSpec((TM, 128), lambda i: (_lat(i), 0)),
                  pl.BlockSpec((TM, 128), lambda i: (_lat(i), 0)),
                  _full((1, HG_DIM)), _full((1, ATW)), _full((1, KVW)),
                  pl.BlockSpec((TM, HGW), lambda i: (_lat(i), 0)),
                  pl.BlockSpec((TM, ATW), lambda i: (_lat(i), 0)),
                  pl.BlockSpec((TM, ATW), lambda i: (i, 0)),
                  pl.BlockSpec((TM, ATW), lambda i: (i, 0))],
        out_specs=[pl.BlockSpec((TM, WB), lambda i: (i, 0)),
                   pl.BlockSpec((TM, HGW), lambda i: (_lat(i), 0)),
                   _full((8, ATW))],
        out_shape=[jax.ShapeDtypeStruct((T, WB), BF16), jax.ShapeDtypeStruct((S, HGW), F32),
                   jax.ShapeDtypeStruct((8, ATW), F32)],
        scratch_shapes=[], operands=[p_b, o, cos, sin, hnw, qnw, knw, dy_hg, dq, dk_rep, dv_rep])


NEG = -1e30
_CTX_BLOCKS = L // BLOCK


def _attn_window_specs():
    prev = pl.BlockSpec((BLOCK, ATW), lambda i: (jnp.maximum(i - 1, 0) + _CTX_BLOCKS, 0))
    own = pl.BlockSpec((BLOCK, ATW), lambda i: (i + _CTX_BLOCKS, 0))
    nxt = pl.BlockSpec((BLOCK, ATW), lambda i: (jnp.minimum(i + 1, N_BLOCKS - 1) + _CTX_BLOCKS, 0))
    return [prev, own, nxt, _full((L, ATW))]


def _attn_valid(i, heads, context):
    n_keys = 3 * BLOCK + (L if context else 0)
    qi = lax.broadcasted_iota(jnp.int32, (heads * BLOCK, n_keys), 0) % BLOCK
    kj = lax.broadcasted_iota(jnp.int32, (heads * BLOCK, n_keys), 1)
    window = ((jnp.abs(kj - BLOCK - qi) <= BLOCK) & ((kj >= BLOCK) | (i > 0))
              & ((kj < 2 * BLOCK) | (i < N_BLOCKS - 1)))
    return window | (kj >= 3 * BLOCK)


def _stack_heads(qg):
    lane = lax.broadcasted_iota(jnp.int32, (1, 256), 1) // HEAD_DIM
    return jnp.concatenate([jnp.where(lane == g, qg, jnp.zeros_like(qg)) for g in range(4)], axis=0)


def _unstack_heads(v4):
    lane = lax.broadcasted_iota(jnp.int32, (1, 256), 1) // HEAD_DIM
    out = jnp.where(lane == 0, v4[0:BLOCK], 0.0)
    for g in range(1, 4):
        out = out + jnp.where(lane == g, v4[g * BLOCK:(g + 1) * BLOCK], 0.0)
    return out


def _sink_rows(sink_ref, hk):
    return jnp.concatenate(
        [jnp.broadcast_to(sink_ref[0:1, 4 * hk + g:4 * hk + g + 1], (BLOCK, 1)) for g in range(4)], axis=0)


def _attn_fwd(q, k_rep, v_rep, sinks, carried=None):
    def body(q_ref, kp, ko, kn, kc, vp, vo, vn, vc, sink_ref, y_ref, lse_ref):
        i = pl.program_id(0)
        valid = _attn_valid(i, 1, True)
        lane8 = lax.broadcasted_iota(jnp.int32, (1, ATT_HEADS), 1)
        head_of_lane = lax.broadcasted_iota(jnp.int32, (1, 256), 1) // HEAD_DIM
        lse_out = jnp.zeros((BLOCK, ATT_HEADS), F32)
        for hk in range(KV_HEADS):
            sl = slice(256 * hk, 256 * (hk + 1))
            qg = q_ref[:, sl]
            keys = jnp.concatenate([kp[:, sl], ko[:, sl], kn[:, sl], kc[:, sl]], axis=0)
            vals = jnp.concatenate([vp[:, sl], vo[:, sl], vn[:, sl], vc[:, sl]], axis=0)
            yg = jnp.zeros((BLOCK, 256), F32)
            for g in range(4):
                q1 = jnp.where(head_of_lane == g, qg, jnp.zeros_like(qg))
                s = jnp.where(valid, _dot_nt(q1, keys), NEG)
                sink = sink_ref[0:1, 4 * hk + g:4 * hk + g + 1]
                m = jnp.maximum(jnp.max(s, axis=1, keepdims=True), sink)
                p = jnp.exp(s - m)
                den = jnp.sum(p, axis=1, keepdims=True) + jnp.exp(sink - m)
                o1 = _dot(p.astype(BF16), vals) * (1.0 / den)
                yg = yg + jnp.where(head_of_lane == g, o1, 0.0)
                lse_out = lse_out + jnp.where(lane8 == 4 * hk + g, m + jnp.log(den), 0.0)
            y_ref[:, sl] = yg.astype(BF16)
        lse_ref[...] = lse_out

    return _pcall(
        body, carried, name="attn_fwd", grid=(N_BLOCKS,),
        in_specs=[pl.BlockSpec((BLOCK, ATW), lambda i: (i, 0))] + _attn_window_specs()
        + _attn_window_specs() + [_full((1, ATT_HEADS))],
        out_specs=[pl.BlockSpec((BLOCK, ATW), lambda i: (i, 0)),
                   pl.BlockSpec((BLOCK, ATT_HEADS), lambda i: (i, 0))],
        out_shape=[jax.ShapeDtypeStruct((S, ATW), BF16), jax.ShapeDtypeStruct((S, ATT_HEADS), F32)],
        scratch_shapes=[],
        operands=[q, k_rep, k_rep, k_rep, k_rep, v_rep, v_rep, v_rep, v_rep, sinks])


def _attn_bwd(q, k_rep, v_rep, sinks, y_at, lse, dy, carried=None):
    def body(q_ref, kp, ko, kn, kc, vp, vo, vn, vc, sink_ref, y_ref, lse_ref, dy_ref,
             dq_ref, dk_ref, dv_ref, dsink_ref, dk_acc, dv_acc):
        i = pl.program_id(0)

        @pl.when(i == 0)
        def _():
            dk_acc[...] = jnp.zeros_like(dk_acc)
            dv_acc[...] = jnp.zeros_like(dv_acc)
            dk_ref[pl.ds(0, L), :] = jnp.zeros((L, ATW), F32)
            dv_ref[pl.ds(0, L), :] = jnp.zeros((L, ATW), F32)
            dsink_ref[...] = jnp.zeros_like(dsink_ref)

        valid = _attn_valid(i, 4, False)
        lane8 = lax.broadcasted_iota(jnp.int32, (1, ATT_HEADS), 1)
        w0 = pl.multiple_of(i * BLOCK, BLOCK)
        dsink = jnp.zeros((1, ATT_HEADS), F32)
        for hk in range(KV_HEADS):
            sl = slice(256 * hk, 256 * (hk + 1))
            q4 = _stack_heads(q_ref[:, sl])
            do4f = _stack_heads(dy_ref[:, sl])
            o4 = _stack_heads(y_ref[:, sl]).astype(F32)
            do4 = do4f.astype(BF16)
            kl = jnp.concatenate([kp[:, sl], ko[:, sl], kn[:, sl]], axis=0)
            vl = jnp.concatenate([vp[:, sl], vo[:, sl], vn[:, sl]], axis=0)
            lse4 = jnp.concatenate(
                [jnp.sum(jnp.where(lane8 == 4 * hk + g, lse_ref[...], 0.0), axis=1, keepdims=True)
                 for g in range(4)], axis=0)
            p_loc = jnp.where(valid, jnp.exp(_dot_nt(q4, kl) - lse4), 0.0)
            p_ctx = jnp.exp(_dot_nt(q4, kc[:, sl]) - lse4)
            delta = jnp.sum(do4f * o4, axis=1, keepdims=True)
            ds_loc = (p_loc * (_dot_nt(do4, vl) - delta)).astype(BF16)
            ds_ctx = (p_ctx * (_dot_nt(do4, vc[:, sl]) - delta)).astype(BF16)
            dq_ref[:, sl] = _unstack_heads(_dot(ds_loc, kl) + _dot(ds_ctx, kc[:, sl]))
            dk_acc[pl.ds(w0, 3 * BLOCK), sl] += _dot_tn(ds_loc, q4)
            dv_acc[pl.ds(w0, 3 * BLOCK), sl] += _dot_tn(p_loc.astype(BF16), do4)
            dk_ref[pl.ds(0, L), sl] += _dot_tn(ds_ctx, q4)
            dv_ref[pl.ds(0, L), sl] += _dot_tn(p_ctx.astype(BF16), do4)
            p_sink = jnp.exp(_sink_rows(sink_ref, hk) - lse4)
            for g in range(4):
                rows = slice(g * BLOCK, (g + 1) * BLOCK)
                dsink = dsink + jnp.where(lane8 == 4 * hk + g,
                                          -jnp.sum(p_sink[rows] * delta[rows], axis=0, keepdims=True), 0.0)
        dsink_ref[...] += dsink

        @pl.when(i == N_BLOCKS - 1)
        def _():
            dk_ref[pl.ds(L, S), :] = dk_acc[pl.ds(BLOCK, S), :]
            dv_ref[pl.ds(L, S), :] = dv_acc[pl.ds(BLOCK, S), :]

    row_q = pl.BlockSpec((BLOCK, ATW), lambda i: (i, 0))
    return _pcall(
        body, carried, name="attn_bwd", grid=(N_BLOCKS,),
        in_specs=[row_q] + _attn_window_specs() + _attn_window_specs()
        + [_full((1, ATT_HEADS)), row_q, pl.BlockSpec((BLOCK, ATT_HEADS), lambda i: (i, 0)), row_q],
        out_specs=[row_q, _full((T, ATW)), _full((T, ATW)), _full((1, ATT_HEADS))],
        out_shape=[jax.ShapeDtypeStruct((S, ATW), F32), jax.ShapeDtypeStruct((T, ATW), F32),
                   jax.ShapeDtypeStruct((T, ATW), F32), jax.ShapeDtypeStruct((1, ATT_HEADS), F32)],
        scratch_shapes=[pltpu.VMEM((S + 2 * BLOCK, ATW), F32), pltpu.VMEM((S + 2 * BLOCK, ATW), F32)],
        operands=[q, k_rep, k_rep, k_rep, k_rep, v_rep, v_rep, v_rep, v_rep, sinks, y_at, lse, dy])


def _merge_fwd(y_hg, y_at, p_c, x, w_bh, w_ba, w_out, g1, nfw, sh2, sc2, carried=None):
    def body(yh_ref, ya_ref, g_ref, x_ref, wbh_ref, wba_ref, wo_ref, g1_ref, nfw_ref, sh_ref, sc_ref,
             mx_ref, r_ref, x1_ref, h2_ref):
        a = _dot_nt(yh_ref[...], wbh_ref[...])
        b = _dot_nt(ya_ref[...], wba_ref[...])
        mixed = (_sigmoid(g_ref[:, :D]) * a + _sigmoid(g_ref[:, D:]) * b).astype(BF16)
        r = _dot(mixed, wo_ref[...])
        x1 = x_ref[...] + g1_ref[...] * r
        mx_ref[...] = mixed
        r_ref[...] = r
        x1_ref[...] = x1
        h2_ref[...] = _rms_mod(x1, nfw_ref[...], sh_ref[...], sc_ref[...]).astype(BF16)

    row = lambda w: pl.BlockSpec((TM, w), lambda i: (i, 0))
    vec = _full((1, D))
    return _pcall(
        body, carried, name="merge_fwd", grid=(N_LAT_TILES,),
        in_specs=[row(HGW), row(ATW), row(WC), row(D), _VMEM_WHOLE, _VMEM_WHOLE, _VMEM_WHOLE,
                  vec, vec, vec, vec],
        out_specs=[row(D)] * 4,
        out_shape=[jax.ShapeDtypeStruct((S, D), dt) for dt in (BF16, F32, F32, BF16)],
        scratch_shapes=[], operands=[y_hg, y_at, p_c, x, w_bh, w_ba, w_out, g1, nfw, sh2, sc2])


def _merge_bwd(dx1, r, y_hg, y_at, p_c, w_bh, w_ba, w_out, g1, carried=None):
    def body(dx_ref, r_ref, yh_ref, ya_ref, g_ref, wbh_ref, wba_ref, wo_ref, g1_ref,
             dr_ref, da_ref, db_ref, dg_ref, dyh_ref, dya_ref, acc_ref):
        @pl.when(pl.program_id(0) == 0)
        def _():
            acc_ref[...] = jnp.zeros_like(acc_ref)

        dx1v = dx_ref[...]
        acc_ref[0:1, :] += jnp.sum(dx1v * r_ref[...], axis=0, keepdims=True)
        dr = (g1_ref[...] * dx1v).astype(BF16)
        dr_ref[...] = dr
        dmix = _dot_nt(dr, wo_ref[...])
        sh, sa = _sigmoid(g_ref[:, :D]), _sigmoid(g_ref[:, D:])
        da = (dmix * sh).astype(BF16)
        db = (dmix * sa).astype(BF16)
        da_ref[...] = da
        db_ref[...] = db
        dg_ref[:, :D] = (dmix * _dot_nt(yh_ref[...], wbh_ref[...]) * sh * (1.0 - sh)).astype(BF16)
        dg_ref[:, D:] = (dmix * _dot_nt(ya_ref[...], wba_ref[...]) * sa * (1.0 - sa)).astype(BF16)
        dyh_ref[...] = _dot(da, wbh_ref[...])
        dya_ref[...] = _dot(db, wba_ref[...])

    row = lambda w: pl.BlockSpec((TM, w), lambda i: (i, 0))
    return _pcall(
        body, carried, name="merge_bwd", grid=(N_LAT_TILES,),
        in_specs=[row(D), row(D), row(HGW), row(ATW), row(WC), _VMEM_WHOLE, _VMEM_WHOLE, _VMEM_WHOLE,
                  _full((1, D))],
        out_specs=[row(D), row(D), row(D), row(WC), row(HGW), row(ATW), _full((8, D))],
        out_shape=[jax.ShapeDtypeStruct((S, D), BF16), jax.ShapeDtypeStruct((S, D), BF16),
                   jax.ShapeDtypeStruct((S, D), BF16), jax.ShapeDtypeStruct((S, WC), BF16),
                   jax.ShapeDtypeStruct((S, HGW), F32), jax.ShapeDtypeStruct((S, ATW), F32),
                   jax.ShapeDtypeStruct((8, D), F32)],
        scratch_shapes=[], operands=[dx1, r, y_hg, y_at, p_c, w_bh, w_ba, w_out, g1])


def _ffn_fused(x1, h2, tgt, w_gate, w_up, w_down, g2, nfw, sc2):
    def body(x1_ref, h2_ref, t_ref, wg_ref, wu_ref, wd_ref, g2_ref, nfw_ref, sc_ref,
             act_ref, dgt_ref, dup_ref, df_ref, dx_ref, acc_ref, gs, us):
        @pl.when(pl.program_id(0) == 0)
        def _():
            acc_ref[...] = jnp.zeros_like(acc_ref)

        h2 = h2_ref[...]
        whole = lambda w_ref: w_ref[...].reshape(D_FF, D)
        wide = lambda t_ref: jnp.concatenate([t_ref[j] for j in range(N_FF_TILES)], axis=1)
        for j in range(N_FF_TILES):
            g = _dot_nt(h2, wg_ref[j])
            u = _dot_nt(h2, wu_ref[j])
            gs[j] = g
            us[j] = u
            act_ref[j] = (g * _sigmoid(g) * u).astype(BF16)
        f = _dot(wide(act_ref), whole(wd_ref))
        x1v = x1_ref[...]
        g2 = g2_ref[...]
        diff = x1v + g2 * f - t_ref[...]
        dy = diff * (1.0 / D)
        df = (g2 * dy).astype(BF16)
        df_ref[...] = df
        dact_all = _dot_nt(df, whole(wd_ref))
        for j in range(N_FF_TILES):
            g, u = gs[j], us[j]
            sg = _sigmoid(g)
            dact = dact_all[:, j * FF_TILE:(j + 1) * FF_TILE]
            dgt_ref[j] = (dact * u * (sg * (1.0 + g * (1.0 - sg)))).astype(BF16)
            dup_ref[j] = (dact * (g * sg)).astype(BF16)
        dh2 = _dot(wide(dgt_ref), whole(wg_ref)) + _dot(wide(dup_ref), whole(wu_ref))
        dx, dsh, dsc, dnw = _rms_mod_bwd(x1v, nfw_ref[...], sc_ref[...], dh2)
        dx_ref[...] = dy + dx
        acc_ref[0:1, :] += dsh
        acc_ref[1:2, :] += dsc
        acc_ref[2:3, :] += dnw
        acc_ref[3:4, :] += jnp.sum(dy * f, axis=0, keepdims=True)
        acc_ref[4:5, :] += 0.5 * jnp.sum(jnp.sum(diff * diff, axis=1, keepdims=True), axis=0,
                                         keepdims=True) * (1.0 / D)

    row = lambda dt_w: pl.BlockSpec((TM, dt_w), lambda i: (i, 0))
    blk = pl.BlockSpec((N_FF_TILES, TM, FF_TILE), lambda i: (0, i, 0))
    vec = _full((1, D))
    return pl.pallas_call(
        body, name="ffn_fused", grid=(N_LAT_TILES,),
        in_specs=[row(D), row(D), row(D), _VMEM_WHOLE, _VMEM_WHOLE, _VMEM_WHOLE, vec, vec, vec],
        out_specs=[blk, blk, blk, row(D), row(D), _full((8, D))],
        out_shape=[jax.ShapeDtypeStruct((N_FF_TILES, S, FF_TILE), BF16)] * 3
        + [jax.ShapeDtypeStruct((S, D), BF16), jax.ShapeDtypeStruct((S, D), F32),
           jax.ShapeDtypeStruct((8, D), F32)],
        scratch_shapes=[pltpu.VMEM((N_FF_TILES, TM, FF_TILE), F32), pltpu.VMEM((N_FF_TILES, TM, FF_TILE), F32)],
        compiler_params=_cp(("arbitrary",)),
    )(x1, h2, tgt, w_gate, w_up, w_down, g2, nfw, sc2)


def _proj_bc(h_all, w_b, w_c, carried=None):
    def body(h_ref, wb_ref, wc_ref, pb_ref, pc_ref):
        h = h_ref[...]
        pb_ref[...] = _dot_nt(h, wb_ref[...])

        @pl.when(pl.program_id(0) > 0)
        def _():
            pc_ref[...] = _dot_nt(h, wc_ref[...])

    return _pcall(
        body, carried, name="proj_bc", grid=(N_TILES,),
        in_specs=[pl.BlockSpec((TM, D), lambda i: (i, 0)), _VMEM_WHOLE, _VMEM_WHOLE],
        out_specs=[pl.BlockSpec((TM, WB), lambda i: (i, 0)), pl.BlockSpec((TM, WC), lambda i: (_lat(i), 0))],
        out_shape=[jax.ShapeDtypeStruct((T, WB), F32), jax.ShapeDtypeStruct((S, WC), F32)],
        scratch_shapes=[], operands=[h_all, w_b, w_c])


def _input_bwd(dp_a, dp_b, dp_c, w_a, w_b, w_c, ctx, x, dx1, nw, sh, sc, carried=None):
    def body(da_ref, db_ref, dc_ref, wa_ref, wb_ref, wc_ref, ctx_ref, x_ref, dx1_ref, nw_ref, sh_ref,
             sc_ref, gx_ref, acc_ref):
        i = pl.program_id(0)

        @pl.when(i == 0)
        def _():
            acc_ref[...] = jnp.zeros_like(acc_ref)

        dh = _dot(da_ref[...], wa_ref[...]) + _dot(db_ref[...], wb_ref[...])

        @pl.when(i == 0)
        def _():
            _, dsh, dsc, dnw = _rms_mod_bwd(ctx_ref[...], nw_ref[...], sc_ref[0:1, :], dh)
            acc_ref[3:4, :] += dsh
            acc_ref[4:5, :] += dsc
            acc_ref[2:3, :] += dnw

        @pl.when(i > 0)
        def _():
            dhl = dh + _dot(dc_ref[...], wc_ref[...])
            dx, dsh, dsc, dnw = _rms_mod_bwd(x_ref[...], nw_ref[...], sc_ref[1:2, :], dhl)
            gx_ref[...] = dx1_ref[...] + dx
            acc_ref[0:1, :] += dsh
            acc_ref[1:2, :] += dsc
            acc_ref[2:3, :] += dnw

    lat = lambda w: pl.BlockSpec((TM, w), lambda i: (_lat(i), 0))
    return _pcall(
        body, carried, name="input_bwd", grid=(N_TILES,),
        in_specs=[pl.BlockSpec((TM, WA), lambda i: (i, 0)), pl.BlockSpec((TM, WB), lambda i: (i, 0)),
                  lat(WC), _VMEM_WHOLE, _VMEM_WHOLE, _VMEM_WHOLE, _full((TM, D)), lat(D), lat(D),
                  _full((1, D)), _full((2, D)), _full((2, D))],
        out_specs=[lat(D), _full((8, D))],
        out_shape=[jax.ShapeDtypeStruct((S, D), F32), jax.ShapeDtypeStruct((8, D), F32)],
        scratch_shapes=[], operands=[dp_a, dp_b, dp_c, w_a, w_b, w_c, ctx, x, dx1, nw, sh, sc])


_C1 = 1.0 - ADAM_B1 ** ADAM_STEP
_C2 = 1.0 - ADAM_B2 ** ADAM_STEP


def _adamw_math(w, g, m, v):
    m = ADAM_B1 * m + (1.0 - ADAM_B1) * g
    v = ADAM_B2 * v + (1.0 - ADAM_B2) * (g * g)
    m_hat = m / _C1
    v_hat = v / _C2
    delta = -ADAM_LR * (m_hat / (jnp.sqrt(v_hat) + ADAM_EPS) + ADAM_WD * w)
    return delta, m, v


def _adamw_sharded(terms, w, m, v, name, tr, extra=None):
    rows, cols = w.shape

    def body(*refs):
        t_ref, w_ref, m_ref, v_ref = refs[:4]
        g_ref, d_ref, nm_ref, nv_ref = refs[-4:]
        g = t_ref[0].astype(F32)
        for s in range(1, N_CHIPS):
            g = g + t_ref[s].astype(F32)
        if extra is not None:
            g = g + refs[4][...].astype(F32)
        g_ref[...] = g
        d_ref[...], nm_ref[...], nv_ref[...] = _adamw_math(w_ref[...], g, m_ref[...], v_ref[...])

    blk = pl.BlockSpec((tr, cols), lambda i: (i, 0))
    return pl.pallas_call(
        body, name=name, grid=(rows // tr,),
        in_specs=[pl.BlockSpec((N_CHIPS, tr, cols), lambda i: (0, i, 0)), blk, blk, blk]
        + ([blk] if extra is not None else []),
        out_specs=[blk] * 4,
        out_shape=[jax.ShapeDtypeStruct((rows, cols), F32)] * 4,
        compiler_params=_cp(("parallel",)),
    )(terms, w, m, v, *([extra] if extra is not None else []))


def _sum_terms(terms, name, tr, extra=None):
    rows, cols = terms.shape[1:]

    def body(*refs):
        g = refs[0][0].astype(F32)
        for s in range(1, N_CHIPS):
            g = g + refs[0][s].astype(F32)
        if extra is not None:
            g = g + refs[1][...].astype(F32)
        refs[-1][...] = g

    blk = pl.BlockSpec((tr, cols), lambda i: (i, 0))
    return pl.pallas_call(
        body, name=name, grid=(rows // tr,),
        in_specs=[pl.BlockSpec((N_CHIPS, tr, cols), lambda i: (0, i, 0))] + ([blk] if extra is not None else []),
        out_specs=blk, out_shape=jax.ShapeDtypeStruct((rows, cols), F32),
        compiler_params=_cp(("parallel",)),
    )(terms, *([extra] if extra is not None else []))


def _adamw_plain(g, w, m, v, name, tr=None):
    def body(g_ref, w_ref, m_ref, v_ref, d_ref, nm_ref, nv_ref):
        d_ref[...], nm_ref[...], nv_ref[...] = _adamw_math(w_ref[...], g_ref[...], m_ref[...], v_ref[...])

    if tr is None:
        return pl.pallas_call(
            body, name=name, in_specs=[_VMEM_WHOLE] * 4, out_specs=[_VMEM_WHOLE] * 3,
            out_shape=[jax.ShapeDtypeStruct(w.shape, F32)] * 3,
            compiler_params=_cp(),
        )(g, w, m, v)
    blk = pl.BlockSpec((tr, w.shape[1]), lambda i: (i, 0))
    return pl.pallas_call(
        body, name=name, grid=(w.shape[0] // tr,), in_specs=[blk] * 4, out_specs=[blk] * 3,
        out_shape=[jax.ShapeDtypeStruct(w.shape, F32)] * 3,
        compiler_params=_cp(("parallel",)),
    )(g, w, m, v)


SMALL_ROWS = 16
R_DMOD, R_DCTX, R_NMIX, R_NFFN, R_MISC, R_DLB, R_BADA01 = 0, 6, 8, 9, 10, 11, 13
M_HNW, M_QNW, M_KNW, M_SINK, M_LOSS = 0, 128, 256, 384, 512


def _pack_small(acc_in, acc_mg, acc_ffn, acc_prep, dsink, dlb):
    def body(in_ref, mg_ref, ff_ref, pp_ref, ds_ref, dlb_ref, o_ref):
        o_ref[...] = jnp.zeros_like(o_ref)
        o_ref[0:2, :] = in_ref[0:2, :]
        o_ref[2:3, :] = mg_ref[0:1, :]
        o_ref[3:5, :] = ff_ref[0:2, :]
        o_ref[5:6, :] = ff_ref[3:4, :]
        o_ref[6:8, :] = in_ref[3:5, :]
        o_ref[8:9, :] = in_ref[2:3, :]
        o_ref[9:10, :] = ff_ref[2:3, :]
        o_ref[10:11, M_HNW:M_HNW + HG_DIM] = pp_ref[0:1, 0:HG_DIM]
        r = lax.broadcasted_iota(jnp.int32, (ATW, 128), 0)
        c = lax.broadcasted_iota(jnp.int32, (ATW, 128), 1)
        fold = jnp.where((r % HEAD_DIM == c) & (c < HEAD_DIM), 1.0, 0.0).astype(BF16)
        qk = jnp.concatenate([pp_ref[1:2, :], pp_ref[2:3, :], jnp.zeros((6, ATW), F32)], axis=0)
        folded = _dot_exact_rhs01(qk, fold)
        o_ref[10:11, M_QNW:M_QNW + 128] = folded[0:1, :]
        o_ref[10:11, M_KNW:M_KNW + 128] = folded[1:2, :]
        o_ref[10:11, M_SINK:M_SINK + ATT_HEADS] = ds_ref[...]
        o_ref[10:11, M_LOSS:M_LOSS + 128] = ff_ref[4:5, 0:128]
        o_ref[11:13, 0:HGW] = dlb_ref[...]

    return pl.pallas_call(
        body, name="pack_small", in_specs=[_VMEM_WHOLE] * 6, out_specs=_VMEM_WHOLE,
        out_shape=jax.ShapeDtypeStruct((SMALL_ROWS, D), F32), compiler_params=_cp(),
    )(acc_in, acc_mg, acc_ffn, acc_prep, dsink, dlb)


def _sum_small(gathered):
    def body(g_ref, o_ref):
        tot = g_ref[0]
        for s in range(1, N_DEV):
            tot = tot + g_ref[s]
        o_ref[...] = tot
        o_ref[R_BADA01:R_BADA01 + 2, :] = tot[0:2, :] + tot[R_DCTX:R_DCTX + 2, :]

    return pl.pallas_call(
        body, name="sum_small", in_specs=[_VMEM_WHOLE], out_specs=_VMEM_WHOLE,
        out_shape=jax.ShapeDtypeStruct((SMALL_ROWS, D), F32), compiler_params=_cp(),
    )(gathered)


_REP_NAMES = ("b_ada", "c_ctx", "norm_mix_w", "norm_ffn_w", "hgrn_norm_w", "q_norm_w", "k_norm_w", "attn_sinks")


def _adamw_replicated(tot, g_c_ctx, ws, ms, vs):
    n = len(_REP_NAMES)

    def body(*refs):
        tot_ref, gc_ref = refs[0], refs[1]
        w_refs, m_refs, v_refs = refs[2:2 + n], refs[2 + n:2 + 2 * n], refs[2 + 2 * n:2 + 3 * n]
        outs = refs[2 + 3 * n:]
        row = lambda r: tot_ref[r:r + 1, :]
        misc = row(R_MISC)
        grads = [jnp.concatenate([row(R_BADA01), row(R_BADA01 + 1)] + [row(k) for k in range(2, 6)], axis=1),
                 gc_ref[...], row(R_NMIX), row(R_NFFN),
                 misc[:, M_HNW:M_HNW + HG_DIM], misc[:, M_QNW:M_QNW + HEAD_DIM],
                 misc[:, M_KNW:M_KNW + HEAD_DIM], misc[:, M_SINK:M_SINK + ATT_HEADS]]
        for k in range(n):
            outs[k][...] = grads[k]
            outs[n + k][...], outs[2 * n + k][...], outs[3 * n + k][...] = _adamw_math(
                w_refs[k][...], grads[k], m_refs[k][...], v_refs[k][...])

    shapes = [jax.ShapeDtypeStruct(w.shape, F32) for w in ws]
    return pl.pallas_call(
        body, name="adamw_replicated", in_specs=[_VMEM_WHOLE] * (2 + 3 * n), out_specs=[_VMEM_WHOLE] * (4 * n),
        out_shape=shapes * 4, compiler_params=_cp(),
    )(tot, g_c_ctx, *ws, *ms, *vs)


def _lb_grads(dlb, lbl):
    def body(d_ref, l_ref, o_ref):
        for d in (0, 1):
            ll = l_ref[d]
            lb = _sigmoid(ll[0:1, :] - ll[1:2, :])
            t = d_ref[d:d + 1, :] * lb * (1.0 - lb)
            o_ref[d, 0:1, :] = t
            o_ref[d, 1:2, :] = -t

    return pl.pallas_call(
        body, name="lb_grads", in_specs=[_VMEM_WHOLE] * 2, out_specs=_VMEM_WHOLE,
        out_shape=jax.ShapeDtypeStruct((2, 2, HGW), F32), compiler_params=_cp(),
    )(dlb, lbl)


def _c_ctx_grad(terms, c_ctx):
    def body(t_ref, c_ref, o_ref):
        tot = t_ref[0, 8:9, :]
        for s in range(1, N_DEV):
            tot = tot + t_ref[s, 8:9, :]
        cv = c_ref[...]
        sg = _sigmoid(cv)
        o_ref[...] = tot * (sg * (1.0 + cv * (1.0 - sg)))

    return pl.pallas_call(
        body, name="c_ctx_grad", in_specs=[_VMEM_WHOLE] * 2, out_specs=_VMEM_WHOLE,
        out_shape=jax.ShapeDtypeStruct((1, D), F32), compiler_params=_cp(),
    )(terms, c_ctx)


def _in_perm():
    fz, bz, inp, kk, vv, qhg, ghg, qat, gates = 0, 512, 1024, 1536, 1664, 1792, 2304, 2816, 3328
    cols = []
    for h in range(HG_HEADS):
        for base in (fz, bz, inp, qhg):
            cols += list(range(base + 128 * h, base + 128 * (h + 1)))
    cols += list(range(ghg, ghg + 512)) + list(range(qat, qat + 512))
    cols += list(range(kk, kk + 128)) + list(range(vv, vv + 128))
    cols += list(range(gates, gates + 2048))
    return np.asarray(cols, np.int32)


_PERM = _in_perm()


_PIECES = {"a": (0, WA, 128), "b": (WA, WB, 256), "c": (WA + WB, WC, 256)}


def _block_table(piece):
    lo, n, blk = _PIECES[piece]
    starts = [int(_PERM[r]) for r in range(lo, lo + n, blk)]
    assert all(s % blk == 0 and np.array_equal(_PERM[r:r + blk], np.arange(s, s + blk))
               for s, r in zip(starts, range(lo, lo + n, blk)))
    return jnp.asarray([s // blk for s in starts], jnp.int32), blk


def _pick_row_blocks(x, table, blk, name):
    cols = x.shape[1]

    def body(t_ref, x_ref, o_ref):
        o_ref[...] = x_ref[...]

    return pl.pallas_call(
        body, name=name,
        grid_spec=pltpu.PrefetchScalarGridSpec(
            num_scalar_prefetch=1, grid=(table.shape[0],),
            in_specs=[pl.BlockSpec((blk, cols), lambda i, t: (t[i], 0))],
            out_specs=pl.BlockSpec((blk, cols), lambda i, t: (i, 0))),
        out_shape=jax.ShapeDtypeStruct((table.shape[0] * blk, cols), x.dtype),
        compiler_params=_cp(("arbitrary",)),
    )(table, x)


def _place_row_blocks(x, table, blk, into, out_rows, name):
    cols = x.shape[1]

    def body(t_ref, x_ref, *rest):
        rest[-1][...] = x_ref[...]

    operands, in_specs, aliases = [table, x], [pl.BlockSpec((blk, cols), lambda i, t: (i, 0))], {}
    if into is not None:
        operands.append(into)
        in_specs.append(_ANY)
        aliases = {2: 0}
    return pl.pallas_call(
        body, name=name,
        grid_spec=pltpu.PrefetchScalarGridSpec(
            num_scalar_prefetch=1, grid=(table.shape[0],), in_specs=in_specs,
            out_specs=pl.BlockSpec((blk, cols), lambda i, t: (t[i], 0))),
        out_shape=jax.ShapeDtypeStruct((out_rows, cols), x.dtype),
        input_output_aliases=aliases,
        compiler_params=_cp(("arbitrary",)),
    )(*operands)


def _local_step(x2, ctx2, h_all, h_lat, tgt, lbl, sh_in, sc_in, gate1, sh2, sc2, gate2, norm_mix_w, norm_ffn_w,
                hgrn_norm_w, q_norm_w, k_norm_w, attn_sinks, w_a, w_b, w_c, s_bh, s_ba, s_out,
                s_gate, s_up, s_down):
    first_last = lambda n: [(0, True), (n - 1, False)]
    p_a = _mm_nt(h_all, w_a, tm=T, tn=512, out_dtype=F32, name="proj_a")
    (o, st), (g_gate, g_bh, g_ba) = _hgrn_fwd(
        p_a, lbl, (_gather_comm_relayed([s_gate, s_bh, s_ba]),
                   [(0, True), (HG_HEADS - 2, True), (HG_HEADS - 1, False)]))
    (p_b, p_c), (g_out,) = _proj_bc(
        h_all, w_b, w_c, (_gather_comm_relayed([s_out]), [(0, True), (N_TILES - 4, True), (N_TILES - 1, False)]))
    cos, sin = _rope_tables()
    qnw_t, knw_t = jnp.tile(q_norm_w, (1, ATT_HEADS)), jnp.tile(k_norm_w, (1, KV_HEADS))
    y_hg, qn, k_rep, v_rep = _prep_fwd(p_b, o, cos, sin, hgrn_norm_w, qnw_t, knw_t)
    (y_at, lse), (g_up, g_down) = _attn_fwd(
        qn, k_rep, v_rep, attn_sinks,
        (_gather_comm_relayed([s_up, s_down]), [(0, True), (N_BLOCKS - 6, True), (N_BLOCKS - 1, False)]))
    w_bh, w_ba, w_o = g_bh.reshape(D, HGW), g_ba.reshape(D, ATW), g_out.reshape(D, D)
    (mixed, r, x1, h2), _ = _merge_fwd(
        y_hg, y_at, p_c, x2, w_bh, w_ba, w_o, gate1, norm_ffn_w, sh2, sc2)
    g_gate, g_up, g_down = [g.reshape(N_FF_TILES, FF_TILE, D) for g in (g_gate, g_up, g_down)]

    act, d_gate, d_up, d_f, dx1, acc_ffn = _ffn_fused(x1, h2, tgt, g_gate, g_up, g_down, gate2,
                                                      norm_ffn_w, sc2)
    by_chip = lambda t: t.reshape((N_CHIPS, 2) + t.shape[1:])
    ff_by_chip = lambda t: t.reshape(N_CHIPS, 2, FF_BLK, D)
    t_down, _ = _mm_tn_blocked(act, d_f, "grad_down")
    t_down = ff_by_chip(t_down)
    t_gate, (f_down,) = _mm_tn_blocked(d_gate, h2, "grad_gate", (_sibling_comm([t_down]), first_last(N_FF_TILES)))
    t_gate = ff_by_chip(t_gate)
    t_up, (f_gate,) = _mm_tn_blocked(d_up, h2, "grad_up", (_sibling_comm([t_gate]), first_last(N_FF_TILES)))
    t_up = ff_by_chip(t_up)

    (d_r, d_a, d_b, dp_c, dy_hg, dy_at, acc_mg), (f_up,) = _merge_bwd(
        dx1, r, y_hg, y_at, p_c, w_bh, w_ba, w_o, gate1, (_sibling_comm([t_up]), first_last(N_LAT_TILES)))
    c_down, c_gate, c_up = [_pair_sum(t, f, "pair_sum_" + nm) for t, f, nm in
                            ((t_down, f_down, "down"), (t_gate, f_gate, "gate"), (t_up, f_up, "up"))]
    t_out = _mm_tn(mixed, d_r, tk=1024, nk=2, tm=1024, tn=1024, out_dtype=BF16, name="grad_out")
    t_bh = _mm_tn(d_a, y_hg, tk=2048, nk=1, tm=1024, tn=512, out_dtype=BF16, name="grad_bh")
    t_ba = _mm_tn(d_b, y_at, tk=2048, nk=1, tm=1024, tn=512, out_dtype=BF16, name="grad_ba")
    t_bh, t_ba, t_out = [by_chip(t.reshape(N_DEV, D // N_DEV, t.shape[1])) for t in (t_bh, t_ba, t_out)]
    (dq, dk_rep, dv_rep, dsink), (r_up,) = _attn_bwd(
        qn, k_rep, v_rep, attn_sinks, y_at, lse, dy_at, (_chip_comm([c_up]), first_last(N_BLOCKS)))
    (dp_b, d_o, acc_prep), (f_bh, f_ba, f_out) = _prep_bwd(
        p_b, o, cos, sin, hgrn_norm_w, qnw_t, knw_t, dy_hg, dq, dk_rep, dv_rep,
        (_sibling_comm([t_bh, t_ba, t_out]), first_last(N_TILES)))
    c_bh, c_ba, c_out = [_pair_sum(t, f, "pair_sum_" + nm) for t, f, nm in
                         ((t_bh, f_bh, "bh"), (t_ba, f_ba, "ba"), (t_out, f_out, "out"))]
    (dp_a, dlb), (r_bh, r_ba, r_out, r_down, r_gate) = _hgrn_bwd(
        p_a, lbl, d_o, st, (_chip_comm([c_bh, c_ba, c_out, c_down, c_gate]), first_last(HG_HEADS)))
    t_a = _mm_tn(dp_a, h_all, tk=T, nk=1, tm=1024, tn=1024, out_dtype=BF16, name="grad_in_a")
    t_b = _mm_tn(dp_b, h_all, tk=T, nk=1, tm=640, tn=1024, out_dtype=BF16, name="grad_in_b")
    t_c = _mm_tn(dp_c, h_lat, tk=1024, nk=2, tm=1024, tn=1024, out_dtype=BF16, name="grad_in_c")
    t_in = None
    for piece, nm in ((t_a, "a"), (t_b, "b"), (t_c, "c")):
        t_in = _place_row_blocks(piece, *_block_table(nm), t_in, IN_COLS, "order_terms_" + nm)
    t_in = by_chip(t_in.reshape(N_DEV, IN_BLK, D))
    (f_in,) = _run_comm(_sibling_comm([t_in]), "scatter_in_sibling")
    c_in = _pair_sum(t_in, f_in, "pair_sum_in")
    sems, c_in, land, token = _chip_exchange_start(c_in, jnp.zeros(c_in.shape, c_in.dtype))
    (grad_x, acc_in), _ = _input_bwd(dp_a, dp_b, dp_c, w_a, w_b, w_c, ctx2, x2, dx1,
                                     norm_mix_w + token[0, 0], sh_in, sc_in)
    small = _pack_small(acc_in, acc_mg, acc_ffn, acc_prep, dsink, dlb)
    return grad_x, small, [r_bh, r_ba, r_out, r_gate, r_up, r_down], (sems, c_in, land)


def kernel(x, c, ctx, c_ctx, w_ada, b_ada, norm_mix_w, norm_ffn_w, w_in, hgrn_lb_logits, hgrn_norm_w, q_norm_w, k_norm_w, attn_sinks, w_branch_hgrn, w_branch_attn, w_out, w_ffn_gate, w_ffn_up, w_ffn_down, loss_target, m_c_ctx, m_w_ada, m_b_ada, m_norm_mix_w, m_norm_ffn_w, m_w_in, m_hgrn_lb_logits, m_hgrn_norm_w, m_q_norm_w, m_k_norm_w, m_attn_sinks, m_w_branch_hgrn, m_w_branch_attn, m_w_out, m_w_ffn_gate, m_w_ffn_up, m_w_ffn_down, v_c_ctx, v_w_ada, v_b_ada, v_norm_mix_w, v_norm_ffn_w, v_w_in, v_hgrn_lb_logits, v_hgrn_norm_w, v_q_norm_w, v_k_norm_w, v_attn_sinks, v_w_branch_hgrn, v_w_branch_attn, v_w_out, v_w_ffn_gate, v_w_ffn_up, v_w_ffn_down):
    me = 4 * lax.axis_index("x") + 2 * lax.axis_index("y") + lax.axis_index("c")
    x2, ctx2, tgt = x[0], ctx[0], loss_target[0]
    w_ada2, w_in2 = w_ada[0], w_in[0]

    cond = jnp.zeros((8, D), F32).at[0].set(c[0]).at[1, :256].set(hgrn_lb_logits.reshape(256))
    b_cols = lax.dynamic_slice(b_ada, (0, me * ADA_BLK), (1, ADA_BLK))
    g0, cc, mod, g_in, h_all, h_lat = _prologue(cond, c_ctx.reshape(1, D), w_ada2, b_cols, w_in2.T.astype(BF16),
                                         x2, ctx2, norm_mix_w)
    lbl = jnp.transpose(g0[:, 1, :256].reshape(N_DEV, 2, 2, 64), (1, 2, 0, 3)).reshape(2, 2, HGW)
    sh1, sc1, gate1, sh2, sc2, gate2 = [mod[k:k + 1] for k in range(6)]
    sh_in = jnp.concatenate([mod[6:7], sh1], axis=0)
    sc_in = jnp.concatenate([mod[7:8], sc1], axis=0)

    shards = [w_branch_hgrn[0].T, w_branch_attn[0].T, w_out[0], w_ffn_gate[0].T, w_ffn_up[0].T, w_ffn_down[0]]
    w_in_t = g_in.reshape(IN_COLS, D)
    w_a, w_b, w_c = [_pick_row_blocks(w_in_t, *_block_table(nm), "order_w_" + nm) for nm in "abc"]

    grad_x, small, (r_bh, r_ba, r_out, r_gate, r_up, r_down), pending_in = _local_step(
        x2, ctx2, h_all, h_lat, tgt, lbl, sh_in, sc_in, gate1, sh2, sc2, gate2, norm_mix_w, norm_ffn_w, hgrn_norm_w,
        q_norm_w, k_norm_w, attn_sinks, w_a, w_b, w_c, *[s.astype(BF16) for s in shards])

    big = {}
    for nm, rr, ww, mm, vv, tr, transposed in (
            ("w_branch_hgrn", r_bh, w_branch_hgrn[0], m_w_branch_hgrn[0], v_w_branch_hgrn[0], 128, True),
            ("w_branch_attn", r_ba, w_branch_attn[0], m_w_branch_attn[0], v_w_branch_attn[0], 128, True),
            ("w_out", r_out, w_out[0], m_w_out[0], v_w_out[0], 128, False),
            ("w_ffn_gate", r_gate, w_ffn_gate[0], m_w_ffn_gate[0], v_w_ffn_gate[0], 176, "grad"),
            ("w_ffn_up", r_up, w_ffn_up[0], m_w_ffn_up[0], v_w_ffn_up[0], 176, "grad"),
            ("w_ffn_down", r_down, w_ffn_down[0], m_w_ffn_down[0], v_w_ffn_down[0], 176, False)):
        if transposed == "grad":
            g_t = _sum_terms(rr, "sum_" + nm, tr).T
            big[nm] = [t[None] for t in (g_t, *_adamw_plain(g_t, ww, mm, vv, "adamw_" + nm, tr=D // 2))]
        elif transposed:
            res = _adamw_sharded(rr, ww.T, mm.T, vv.T, "adamw_" + nm, tr)
            big[nm] = [t.T[None] for t in res]
        else:
            big[nm] = [t[None] for t in _adamw_sharded(rr, ww, mm, vv, "adamw_" + nm, tr)]

    (g2,) = _all_gather([small], "gather_small", True)
    tot = _sum_small(g2)
    dm = jnp.zeros((16, 6 * D), F32).at[:8].set(g2[:, R_DMOD:R_DMOD + 6, :].reshape(N_DEV, 6 * D))
    dm = dm.at[8, :2 * D].set(tot[R_DCTX:R_DCTX + 2].reshape(2 * D))
    dm_cols = lax.dynamic_slice(dm, (0, me * ADA_BLK), (16, ADA_BLK))
    g_w_ada, dsc_term = _ada_grads(cc, dm_cols, w_ada2)
    (g3,) = _all_gather([dsc_term], "gather_cctx", True)
    g_c_ctx = _c_ctx_grad(g3, c_ctx.reshape(1, D))
    g_lbl = _lb_grads(tot[R_DLB:R_DLB + 2, :HGW], lbl)
    g_lb_mine = lax.dynamic_slice(g_lbl, (0, 0, me * 64), (2, 2, 64))
    misc = tot[R_MISC]
    loss = misc[M_LOSS]

    rep_out = _adamw_replicated(
        tot, g_c_ctx,
        [b_ada, c_ctx.reshape(1, D), norm_mix_w, norm_ffn_w, hgrn_norm_w, q_norm_w, k_norm_w, attn_sinks],
        [m_b_ada, m_c_ctx.reshape(1, D), m_norm_mix_w, m_norm_ffn_w, m_hgrn_norm_w, m_q_norm_w, m_k_norm_w,
         m_attn_sinks],
        [v_b_ada, v_c_ctx.reshape(1, D), v_norm_mix_w, v_norm_ffn_w, v_hgrn_norm_w, v_q_norm_w, v_k_norm_w,
         v_attn_sinks])
    rep = []
    for kind in range(4):
        vals = dict(zip(_REP_NAMES, rep_out[kind * len(_REP_NAMES):(kind + 1) * len(_REP_NAMES)]))
        vals["c_ctx"] = vals["c_ctx"].reshape(D)
        rep.append(vals)

    sems, c_in, land = pending_in
    d_ada, nm_ada, nv_ada = _adamw_plain(g_w_ada, w_ada2, m_w_ada[0], v_w_ada[0], "adamw_w_ada", tr=256)
    land = _chip_exchange_wait(sems, c_in, land, d_ada)
    own = lax.dynamic_index_in_dim(c_in, 2 * lax.axis_index("x") + lax.axis_index("y"), 0, keepdims=False)
    g_in_t = _sum_terms(land, "sum_w_in", 336, extra=own).T
    big["w_in"] = [t[None] for t in (g_in_t, *_adamw_plain(g_in_t, w_in2, m_w_in[0], v_w_in[0], "adamw_w_in", tr=D // 4))]
    ada = [t[None] for t in (g_w_ada, d_ada, nm_ada, nv_ada)]
    lb_w = hgrn_lb_logits.reshape(4, 64)
    d_lb, nm_lb, nv_lb = _adamw_plain(g_lb_mine.reshape(4, 64), lb_w, m_hgrn_lb_logits.reshape(4, 64),
                                      v_hgrn_lb_logits.reshape(4, 64), "adamw_lb")
    lbs = [t.reshape(2, 2, 64) for t in (g_lb_mine, d_lb, nm_lb, nv_lb)]

    names = ['c_ctx', 'w_ada', 'b_ada', 'norm_mix_w', 'norm_ffn_w', 'w_in', 'hgrn_lb_logits', 'hgrn_norm_w',
             'q_norm_w', 'k_norm_w', 'attn_sinks', 'w_branch_hgrn', 'w_branch_attn', 'w_out', 'w_ffn_gate',
             'w_ffn_up', 'w_ffn_down']
    outs = [loss, grad_x[None]]
    for kind in range(4):
        for nm in names:
            if nm == 'w_ada':
                outs.append(ada[kind])
            elif nm == 'hgrn_lb_logits':
                outs.append(lbs[kind])
            elif nm in big:
                outs.append(big[nm][kind])
            else:
                outs.append(rep[kind][nm])
    return tuple(outs)
```

```python
import functools
import math

import numpy as np
import jax
import jax.numpy as jnp
from jax import lax
from jax.experimental import pallas as pl
from jax.experimental.pallas import tpu as pltpu

F32 = jnp.float32
BF16 = jnp.bfloat16

N_DEV = 8
D = 1024
S = 2048
L = 256
T = L + S
TM = 256
N_TILES = T // TM
N_LAT_TILES = S // TM
HG_HEADS = 4
HG_DIM = 128
HGW = 512
CHUNK = 32
N_CHUNKS = T // CHUNK
N_CTX_CHUNKS = L // CHUNK
ATT_HEADS = 8
KV_HEADS = 2
HEAD_DIM = 64
ATW = 512
KVW = 128
BLOCK = 128
N_BLOCKS = S // BLOCK
GRID_W = 64
ROPE_THETA = 10000.0
D_FF = 2816
FF_BLK = D_FF // N_DEV
FF_TILE = 256
N_FF_TILES = D_FF // FF_TILE
N_FF_HALVES = 2
IN_COLS = 5376
IN_BLK = IN_COLS // N_DEV
ADA_BLK = 6 * D // N_DEV
EPS = 1e-6
WA, WB, WC = 2048, 1280, 2048

ADAM_LR = 0.001
ADAM_B1 = 0.9
ADAM_B2 = 0.999
ADAM_EPS = 1e-08
ADAM_WD = 0.01
ADAM_STEP = 10

VMEM_LIMIT = 56 * 1024 * 1024
MESH = pl.DeviceIdType.MESH


def _cp(sem=None, vmem=VMEM_LIMIT):
    return pltpu.CompilerParams(dimension_semantics=sem, vmem_limit_bytes=vmem)


def _full(shape):
    n = len(shape)
    return pl.BlockSpec(shape, lambda *_: (0,) * n)


_VMEM_WHOLE = pl.BlockSpec(memory_space=pltpu.VMEM)
_ANY = pl.BlockSpec(memory_space=pl.ANY)


def _sigmoid(v):
    return 1.0 / (1.0 + jnp.exp(-v))


def _dot(a, b):
    return jnp.dot(a, b, preferred_element_type=F32)


def _dot_nt(a, b):
    return lax.dot_general(a, b, (((1,), (1,)), ((), ())), preferred_element_type=F32)


def _dot_tn(a, b):
    return lax.dot_general(a, b, (((0,), (0,)), ((), ())), preferred_element_type=F32)


def _split3(v):
    hi = v.astype(BF16)
    r = v - hi.astype(F32)
    mid = r.astype(BF16)
    lo = (r - mid.astype(F32)).astype(BF16)
    return hi, mid, lo


def _dot_exact_rhs01(v, m01):
    hi, mid, lo = _split3(v)
    return _dot(hi, m01) + _dot(mid, m01) + _dot(lo, m01)


def _split2(v):
    hi = v.astype(BF16)
    return hi, (v - hi.astype(F32)).astype(BF16)


def _dot_lhs01(m01, v):
    hi, lo = _split2(v)
    return _dot(m01, hi) + _dot(m01, lo)


def _dot_f32(a, b, dot=_dot):
    ah, am, al = _split3(a)
    bh, bm, bl = _split3(b)
    return (dot(ah, bh) + (dot(ah, bm) + dot(am, bh))
            + (dot(am, bm) + dot(ah, bl) + dot(al, bh)))


def _my_pos():
    return lax.axis_index("x"), lax.axis_index("y"), lax.axis_index("c")


class _Comm:
    def __init__(self, operands, out_shapes, sems, phases):
        self.operands, self.out_shapes, self.sems, self.phases = operands, out_shapes, sems, phases


def _gather_comm(blocks):
    n = len(blocks)

    def parts(ins, outs, sems):
        send_sems, recv_sems, local_sems = sems
        x, y, c = _my_pos()
        me, sibling = (x, y, c), (x, y, 1 - c)
        chips = [(1 - x, y), (x, 1 - y), (1 - x, 1 - y)]

        def slot(a, px, py, pc):
            return outs[a].at[4 * px + 2 * py + pc]

        def copy(a, k, block, to, src=None):
            return pltpu.make_async_remote_copy(
                src_ref=slot(a, *block) if src is None else src, dst_ref=slot(a, *block),
                send_sem=send_sems.at[a, k], recv_sem=recv_sems.at[a, k],
                device_id=to, device_id_type=MESH)

        mine = [pltpu.make_async_copy(ins[a], slot(a, *me), local_sems.at[a]) for a in range(n)]
        first = []
        for a in range(n):
            first.append(copy(a, 0, me, sibling, src=ins[a]))
            first += [copy(a, 1 + j, me, (*chip, c), src=ins[a]) for j, chip in enumerate(chips)]
        passed = [copy(a, 4 + j, (*chip, c), sibling) for j, chip in enumerate(chips) for a in range(n)]
        return c, me, sibling, chips, copy, mine, first, passed

    def start(ins, outs, sems):
        _, _, _, _, _, mine, first, _ = parts(ins, outs, sems)
        for cp in mine + first:
            cp.start()

    def forward(ins, outs, sems):
        c, me, _, chips, copy, _, _, passed = parts(ins, outs, sems)
        for j, chip in enumerate(chips):
            for a in range(n):
                copy(a, 1 + j, (*chip, c), me).wait_recv()
                passed[j * n + a].start()

    def finish(ins, outs, sems):
        c, me, sibling, chips, copy, mine, first, passed = parts(ins, outs, sems)
        for a in range(n):
            copy(a, 0, sibling, me).wait_recv()
            for j, chip in enumerate(chips):
                copy(a, 4 + j, (*chip, 1 - c), me).wait_recv()
        for cp in first + passed:
            cp.wait_send()
        for cp in mine:
            cp.wait()

    return _Comm(blocks, [jax.ShapeDtypeStruct((N_DEV,) + b.shape, b.dtype) for b in blocks],
                 [pltpu.SemaphoreType.DMA((n, 7)), pltpu.SemaphoreType.DMA((n, 7)), pltpu.SemaphoreType.DMA((n,))],
                 [start, forward, finish])


def _gather_comm_relayed(blocks):
    n = len(blocks)

    def parts(ins, outs, sems):
        send_sems, recv_sems, local_sems = sems
        x, y, c = _my_pos()
        me, sibling = (x, y, c), (x, y, 1 - c)
        x_nbr, y_nbr, diag = (1 - x, y, c), (x, 1 - y, c), (1 - x, 1 - y, c)

        def slot(a, dev, half=None):
            ref = outs[a].at[4 * dev[0] + 2 * dev[1] + dev[2]]
            if half is None:
                return ref
            rows = blocks[a].shape[0] // 2
            return ref.at[pl.ds(half * rows, rows)]

        def copy(a, k, block, to, half=None, src=None):
            return pltpu.make_async_remote_copy(
                src_ref=slot(a, block, half) if src is None else src, dst_ref=slot(a, block, half),
                send_sem=send_sems.at[a, k], recv_sem=recv_sems.at[a, k],
                device_id=to, device_id_type=MESH)

        mine = [pltpu.make_async_copy(ins[a], slot(a, me), local_sems.at[a]) for a in range(n)]
        return me, sibling, x_nbr, y_nbr, diag, copy, mine

    def start(ins, outs, sems):
        me, sibling, x_nbr, y_nbr, _, copy, mine = parts(ins, outs, sems)
        for cp in mine:
            cp.start()
        for a in range(n):
            for k, to in ((1, x_nbr), (2, y_nbr), (0, sibling)):
                copy(a, k, me, to, src=ins[a]).start()

    def forward(ins, outs, sems):
        me, sibling, x_nbr, y_nbr, _, copy, _ = parts(ins, outs, sems)
        for a in range(n):
            copy(a, 1, x_nbr, me).wait_recv()
            copy(a, 3, x_nbr, y_nbr, half=0).start()
            copy(a, 5, x_nbr, sibling).start()
        for a in range(n):
            copy(a, 2, y_nbr, me).wait_recv()
            copy(a, 4, y_nbr, x_nbr, half=1).start()
            copy(a, 6, y_nbr, sibling).start()

    def finish(ins, outs, sems):
        me, sibling, x_nbr, y_nbr, diag, copy, mine = parts(ins, outs, sems)
        sib = lambda dev: (dev[0], dev[1], sibling[2])
        for a in range(n):
            copy(a, 3, diag, me, half=0).wait_recv()
            copy(a, 4, diag, me, half=1).wait_recv()
            copy(a, 7, diag, sibling).start()
        for a in range(n):
            copy(a, 0, sibling, me).wait_recv()
            for k, dev in ((5, x_nbr), (6, y_nbr), (7, diag)):
                copy(a, k, sib(dev), me).wait_recv()
        for a in range(n):
            for k, block, to, half in ((0, me, sibling, None), (1, me, x_nbr, None), (2, me, y_nbr, None),
                                       (3, x_nbr, y_nbr, 0), (4, y_nbr, x_nbr, 1), (5, x_nbr, sibling, None),
                                       (6, y_nbr, sibling, None), (7, diag, sibling, None)):
                copy(a, k, block, to, half=half, src=ins[a] if block is me else None).wait_send()
        for cp in mine:
            cp.wait()

    return _Comm(blocks, [jax.ShapeDtypeStruct((N_DEV,) + b.shape, b.dtype) for b in blocks],
                 [pltpu.SemaphoreType.DMA((n, 8)), pltpu.SemaphoreType.DMA((n, 8)), pltpu.SemaphoreType.DMA((n,))],
                 [start, forward, finish])


_HBM = pl.BlockSpec(memory_space=pltpu.HBM)
_SEM = pl.BlockSpec(memory_space=pltpu.SEMAPHORE)
_SPLIT_COPY = pltpu.CompilerParams(has_side_effects=pltpu.SideEffectType.DATAFLOW_SIDE_EFFECTING)


def _chip_exchange_copies(src_ref, land_ref, sems):
    x, y, c = _my_pos()
    q_me = 2 * x + y
    pairs = []
    for j, (px, py) in enumerate([(1 - x, y), (x, 1 - y), (1 - x, 1 - y)]):
        q = 2 * px + py
        send = pltpu.make_async_remote_copy(
            src_ref=src_ref.at[q], dst_ref=land_ref.at[q_me], send_sem=sems[j], recv_sem=sems[3 + j],
            device_id=(px, py, c), device_id_type=MESH)
        recv = pltpu.make_async_remote_copy(
            src_ref=src_ref.at[q], dst_ref=land_ref.at[q], send_sem=sems[j], recv_sem=sems[3 + j],
            device_id=(x, y, c), device_id_type=MESH)
        pairs.append((send, recv))
    return pairs


def _chip_exchange_start(src, land):
    def body(src_ref, land_ref, *outs):
        sems, token = outs[:6], outs[8]
        for send, _ in _chip_exchange_copies(src_ref, land_ref, sems):
            send.start()
        token[...] = jnp.zeros_like(token)

    res = pl.pallas_call(
        body, name="scatter_in_start",
        out_shape=(pltpu.SemaphoreType.DMA(()),) * 6 + (
            pltpu.HBM(src.shape, src.dtype), pltpu.HBM(land.shape, land.dtype),
            jax.ShapeDtypeStruct((8, 128), F32)),
        in_specs=(_HBM, _HBM), out_specs=(_SEM,) * 6 + (_HBM, _HBM, pl.BlockSpec(memory_space=pltpu.VMEM)),
        input_output_aliases={0: 6, 1: 7}, compiler_params=_SPLIT_COPY,
    )(pltpu.with_memory_space_constraint(src, pltpu.HBM), pltpu.with_memory_space_constraint(land, pltpu.HBM))
    return res[:6], res[6], res[7], res[8]


def _chip_exchange_wait(sems, src_thru, land_thru, after):
    def body(src_ref, land_ref, *rest):
        for send, recv in _chip_exchange_copies(src_ref, land_ref, rest[:6]):
            send.wait_send()
            recv.wait_recv()

    return pl.pallas_call(
        body, name="scatter_in_wait",
        out_shape=(pltpu.HBM(src_thru.shape, src_thru.dtype), pltpu.HBM(land_thru.shape, land_thru.dtype)),
        in_specs=(_HBM, _HBM) + (_SEM,) * 6 + (_ANY,), out_specs=(_HBM, _HBM),
        input_output_aliases={0: 0, 1: 1}, compiler_params=_SPLIT_COPY,
    )(src_thru, land_thru, *sems, after)[1]


def _run_comm(comm, name, in_vmem=False):
    n_in, n_out = len(comm.operands), len(comm.out_shapes)

    def body(*refs):
        ins, outs, sems = refs[:n_in], refs[n_in:n_in + n_out], refs[n_in + n_out:]
        for phase in comm.phases:
            phase(ins, outs, sems)

    spec = _VMEM_WHOLE if in_vmem else _ANY
    return pl.pallas_call(
        body, name=name, out_shape=comm.out_shapes, in_specs=[spec] * n_in, out_specs=[spec] * n_out,
        scratch_shapes=comm.sems,
    )(*comm.operands)


def _carrier_call(body, comm, schedule, *, name, grid, in_specs, out_specs, out_shape, scratch_shapes, operands):
    n_in, n_out, n_scr = len(in_specs), len(out_specs), len(scratch_shapes)
    c_in, c_out = len(comm.operands), len(comm.out_shapes)

    def full_body(*refs):
        ins, refs = refs[:n_in], refs[n_in:]
        cins, refs = refs[:c_in], refs[c_in:]
        outs, refs = refs[:n_out], refs[n_out:]
        couts, refs = refs[:c_out], refs[c_out:]
        scr, csems = refs[:n_scr], refs[n_scr:]
        step = pl.program_id(0)

        def run(before):
            for (at, when_before), phase in zip(schedule, comm.phases):
                if when_before == before:
                    pl.when(step == at)(functools.partial(phase, cins, couts, csems))

        run(True)
        body(*ins, *outs, *scr)
        run(False)

    res = pl.pallas_call(
        full_body, name=name, grid=grid,
        in_specs=list(in_specs) + [_ANY] * c_in, out_specs=list(out_specs) + [_ANY] * c_out,
        out_shape=list(out_shape) + list(comm.out_shapes),
        scratch_shapes=list(scratch_shapes) + list(comm.sems),
        compiler_params=_cp(("arbitrary",)),
    )(*operands, *comm.operands)
    return res[:n_out], res[n_out:]


def _pcall(body, carried, *, name, grid, in_specs, out_specs, out_shape, scratch_shapes, operands):
    if carried is None:
        res = pl.pallas_call(body, name=name, grid=grid, in_specs=in_specs, out_specs=out_specs,
                             out_shape=out_shape, scratch_shapes=scratch_shapes,
                             compiler_params=_cp(("arbitrary",)))(*operands)
        return res, ()
    return _carrier_call(body, carried[0], carried[1], name=name, grid=grid, in_specs=in_specs,
                         out_specs=out_specs, out_shape=out_shape, scratch_shapes=scratch_shapes,
                         operands=operands)


def _all_gather(blocks, name, in_vmem):
    return _run_comm(_gather_comm(blocks), name, in_vmem)


N_CHIPS = 4


def _sibling_comm(contribs):
    n = len(contribs)

    def copies(ins, outs, sems):
        send_sems, recv_sems = sems
        x, y, c = _my_pos()
        return [pltpu.make_async_remote_copy(
            src_ref=ins[a].at[pl.ds(0, N_CHIPS), 1 - c], dst_ref=outs[a],
            send_sem=send_sems.at[a], recv_sem=recv_sems.at[a],
            device_id=(x, y, 1 - c), device_id_type=MESH) for a in range(n)]

    def start(ins, outs, sems):
        for cp in copies(ins, outs, sems):
            cp.start()

    def finish(ins, outs, sems):
        cps = copies(ins, outs, sems)
        for cp in cps:
            cp.wait_recv()
        for cp in cps:
            cp.wait_send()

    return _Comm(contribs, [jax.ShapeDtypeStruct((N_CHIPS,) + b.shape[2:], b.dtype) for b in contribs],
                 [pltpu.SemaphoreType.DMA((n,)), pltpu.SemaphoreType.DMA((n,))], [start, finish])


def _pair_sum(mine, theirs, name):
    _, _, rows, cols = mine.shape
    core = lax.axis_index("c").astype(jnp.int32).reshape(1)

    def body(c_ref, m_ref, t_ref, o_ref):
        o_ref[...] = (m_ref[...].astype(F32) + t_ref[...].astype(F32)).astype(BF16)

    return pl.pallas_call(
        body, name=name,
        grid_spec=pltpu.PrefetchScalarGridSpec(
            num_scalar_prefetch=1, grid=(N_CHIPS,),
            in_specs=[pl.BlockSpec((None, None, rows, cols), lambda q, c: (q, c[0], 0, 0)),
                      pl.BlockSpec((None, rows, cols), lambda q, c: (q, 0, 0))],
            out_specs=pl.BlockSpec((None, rows, cols), lambda q, c: (q, 0, 0))),
        out_shape=jax.ShapeDtypeStruct((N_CHIPS, rows, cols), BF16),
        compiler_params=_cp(("parallel",)),
    )(core, mine, theirs)


def _chip_comm(sums):
    n = len(sums)

    def parts(ins, outs, sems):
        send_sems, recv_sems, local_sems = sems
        x, y, c = _my_pos()
        q_me = 2 * x + y
        chips = [(1 - x, y), (x, 1 - y), (1 - x, 1 - y)]
        mine = [pltpu.make_async_copy(ins[a].at[q_me], outs[a].at[q_me], local_sems.at[a]) for a in range(n)]
        sends, recvs = [], []
        for j, (px, py) in enumerate(chips):
            for a in range(n):
                q = 2 * px + py
                sends.append(pltpu.make_async_remote_copy(
                    src_ref=ins[a].at[q], dst_ref=outs[a].at[q_me],
                    send_sem=send_sems.at[a, j], recv_sem=recv_sems.at[a, j],
                    device_id=(px, py, c), device_id_type=MESH))
                recvs.append(pltpu.make_async_remote_copy(
                    src_ref=ins[a].at[q], dst_ref=outs[a].at[q],
                    send_sem=send_sems.at[a, j], recv_sem=recv_sems.at[a, j],
                    device_id=(x, y, c), device_id_type=MESH))
        return mine, sends, recvs

    def start(ins, outs, sems):
        mine, sends, _ = parts(ins, outs, sems)
        for cp in mine + sends:
            cp.start()

    def finish(ins, outs, sems):
        mine, sends, recvs = parts(ins, outs, sems)
        for cp in recvs:
            cp.wait_recv()
        for cp in sends:
            cp.wait_send()
        for cp in mine:
            cp.wait()

    return _Comm(sums, [jax.ShapeDtypeStruct(b.shape, b.dtype) for b in sums],
                 [pltpu.SemaphoreType.DMA((n, 3)), pltpu.SemaphoreType.DMA((n, 3)), pltpu.SemaphoreType.DMA((n,))],
                 [start, finish])


def _mm_nt(a, bt, *, tm, tn, out_dtype, name, row_off=0, rows=None):
    rows = a.shape[0] if rows is None else rows
    n, k = bt.shape

    def body(a_ref, b_ref, o_ref):
        o_ref[...] = _dot_nt(a_ref[...], b_ref[...]).astype(out_dtype)

    return pl.pallas_call(
        body, name=name, grid=(rows // tm, n // tn),
        in_specs=[pl.BlockSpec((tm, k), lambda i, j: (i + row_off, 0)),
                  pl.BlockSpec((tn, k), lambda i, j: (j, 0))],
        out_specs=pl.BlockSpec((tm, tn), lambda i, j: (i, j)),
        out_shape=jax.ShapeDtypeStruct((rows, n), out_dtype),
        compiler_params=_cp(("parallel", "parallel")),
    )(a, bt)


def _mm_tn(a, b, *, tk, nk, tm, tn, out_dtype, name, a_off=0, b_off=0):
    m, n = a.shape[1], b.shape[1]

    def body(a_ref, b_ref, o_ref, acc):
        kk = pl.program_id(2)

        @pl.when(kk == 0)
        def _():
            acc[...] = jnp.zeros_like(acc)

        acc[...] += _dot_tn(a_ref[...], b_ref[...])

        @pl.when(kk == nk - 1)
        def _():
            o_ref[...] = acc[...].astype(out_dtype)

    return pl.pallas_call(
        body, name=name, grid=(m // tm, n // tn, nk),
        in_specs=[pl.BlockSpec((tk, tm), lambda i, j, kk: (kk + a_off, i)),
                  pl.BlockSpec((tk, tn), lambda i, j, kk: (kk + b_off, j))],
        out_specs=pl.BlockSpec((tm, tn), lambda i, j, kk: (i, j)),
        out_shape=jax.ShapeDtypeStruct((m, n), out_dtype),
        scratch_shapes=[pltpu.VMEM((tm, tn), F32)],
        compiler_params=_cp(("parallel", "parallel", "arbitrary")),
    )(a, b)


def _mm_tn_blocked(a, b, name, carried=None):
    w = a.shape[1] // N_FF_HALVES
    n = b.shape[1]

    def body(a_ref, b_ref, o_ref):
        o_ref[...] = _dot_tn(a_ref[...], b_ref[...]).astype(BF16)

    (out,), extra = _pcall(
        body, carried, name=name, grid=(N_FF_HALVES,),
        in_specs=[pl.BlockSpec((S, w), lambda j: (0, j)), _full((S, n))],
        out_specs=[pl.BlockSpec((w, n), lambda j: (j, 0))],
        out_shape=[jax.ShapeDtypeStruct((a.shape[1], n), BF16)],
        scratch_shapes=[], operands=[a, b])
    return out, extra


def _prologue(cond, c_ctx, w_ada, b_cols, w_in_t, x, ctx, nw):
    rows_shape = jax.ShapeDtypeStruct((16, ADA_BLK), F32)
    big, g_cond, g_mod = _gather_comm_relayed([w_in_t]), _gather_comm([cond]), _gather_comm([rows_shape])

    def body(cond_ref, cctx_ref, wada_ref, b_ref, nw_ref, win_ref, x_ref, ctx_ref,
             g0_ref, cc_ref, mod_ref, gin_ref, h_ref, hl_ref, rows_ref, g1_ref, x_s, ctx_s, h_s, io_sems, *sems):
        s_big, s_cond, s_mod = sems[0:3], sems[3:6], sems[6:9]
        big.phases[0]([win_ref], [gin_ref], s_big)
        load_x = pltpu.make_async_copy(x_ref, x_s, io_sems.at[0])
        load_ctx = pltpu.make_async_copy(ctx_ref, ctx_s, io_sems.at[1])
        load_x.start()
        load_ctx.start()
        for phase in g_cond.phases:
            phase([cond_ref], [g0_ref], s_cond)
        cc_ref[...] = jnp.zeros_like(cc_ref)
        for j in range(N_DEV):
            cc_ref[j:j + 1, :] = g0_ref[j, 0:1, :]
        cc_ref[N_DEV:N_DEV + 1, :] = cctx_ref[...]
        cv = cc_ref[...]
        rows_ref[...] = _dot_f32(cv * _sigmoid(cv), wada_ref[...]) + b_ref[...]
        for phase in g_mod.phases:
            phase([rows_ref], [g1_ref], s_mod)
        x_pos, y_pos, c_pos = _my_pos()
        me = 4 * x_pos + 2 * y_pos + c_pos
        mine = jnp.concatenate([g1_ref[j, pl.ds(me, 1), :] for j in range(N_DEV)], axis=1)
        shared = jnp.concatenate([g1_ref[j, N_DEV:N_DEV + 1, :] for j in range(N_DEV)], axis=1)
        for k in range(6):
            mod_ref[k:k + 1, :] = mine[:, k * D:(k + 1) * D]
        mod_ref[6:7, :] = shared[:, 0:D]
        mod_ref[7:8, :] = shared[:, D:2 * D]
        load_ctx.wait()
        load_x.wait()
        h_s[pl.ds(0, L), :] = _rms_mod(ctx_s[...], nw_ref[...], mod_ref[6:7, :], mod_ref[7:8, :]).astype(BF16)

        def norm_tile(i, carry):
            r0 = pl.multiple_of(i * TM, TM)
            h_s[pl.ds(L + r0, TM), :] = _rms_mod(
                x_s[pl.ds(r0, TM), :], nw_ref[...], mod_ref[0:1, :], mod_ref[1:2, :]).astype(BF16)
            return carry

        lax.fori_loop(0, N_LAT_TILES, norm_tile, 0)
        stores = [pltpu.make_async_copy(h_s, h_ref, io_sems.at[2]),
                  pltpu.make_async_copy(h_s.at[pl.ds(L, S)], hl_ref, io_sems.at[3])]
        for cp in stores:
            cp.start()
        big.phases[1]([win_ref], [gin_ref], s_big)
        big.phases[2]([win_ref], [gin_ref], s_big)
        for cp in stores:
            cp.wait()

    return pl.pallas_call(
        body, name="prologue",
        in_specs=[_VMEM_WHOLE] * 5 + [_ANY] * 3, out_specs=[_VMEM_WHOLE] * 3 + [_ANY] * 3,
        out_shape=[g_cond.out_shapes[0], jax.ShapeDtypeStruct((16, D), F32), jax.ShapeDtypeStruct((8, D), F32),
                   big.out_shapes[0], jax.ShapeDtypeStruct((T, D), BF16), jax.ShapeDtypeStruct((S, D), BF16)],
        scratch_shapes=[pltpu.VMEM((16, ADA_BLK), F32), pltpu.VMEM((N_DEV, 16, ADA_BLK), F32),
                        pltpu.VMEM((S, D), F32), pltpu.VMEM((L, D), F32), pltpu.VMEM((T, D), BF16),
                        pltpu.SemaphoreType.DMA((4,))] + big.sems + g_cond.sems + g_mod.sems,
        compiler_params=_cp(),
    )(cond, c_ctx, w_ada, b_cols, nw, w_in_t, x, ctx)


def _ada_grads(cc, dm_cols, w_ada):
    def body(c_ref, dm_ref, w_ref, gw_ref, dsc_ref):
        cv = c_ref[...]
        sc = cv * _sigmoid(cv)
        dm = dm_ref[...]
        gw_ref[...] = _dot_f32(sc, dm, dot=_dot_tn)
        dsc_ref[...] = _dot_f32(dm, w_ref[...], dot=_dot_nt)

    return pl.pallas_call(
        body, name="ada_grads",
        in_specs=[_VMEM_WHOLE] * 3, out_specs=[_VMEM_WHOLE] * 2,
        out_shape=[jax.ShapeDtypeStruct((D, ADA_BLK), F32), jax.ShapeDtypeStruct((16, D), F32)],
        compiler_params=_cp(),
    )(cc, dm_cols, w_ada)


def _lat(i):
    return jnp.maximum(i - 1, 0)


def _rms_mod(xv, nw, sh, sc):
    rstd = lax.rsqrt(jnp.mean(xv * xv, axis=-1, keepdims=True) + EPS)
    return (xv * rstd * nw) * (1.0 + sc) + sh


def _rms_mod_bwd(xv, nw, sc, dh):
    rstd = lax.rsqrt(jnp.mean(xv * xv, axis=-1, keepdims=True) + EPS)
    xhat = xv * rstd
    dn = dh * (1.0 + sc)
    dxhat = dn * nw
    dx = rstd * (dxhat - xhat * jnp.mean(dxhat * xhat, axis=-1, keepdims=True))
    return (dx, jnp.sum(dh, axis=0, keepdims=True), jnp.sum(dh * (xhat * nw), axis=0, keepdims=True),
            jnp.sum(dn * xhat, axis=0, keepdims=True))


def _chunk_masks(reverse):
    row = lax.broadcasted_iota(jnp.int32, (TM, TM), 0)
    col = lax.broadcasted_iota(jnp.int32, (TM, TM), 1)
    same = (row // CHUNK) == (col // CHUNK)
    tri = same & ((col >= row) if reverse else (col <= row))
    return same, tri


def _chunk_order(i, reverse):
    if not reverse:
        return i
    return jnp.where(i < N_CTX_CHUNKS, N_CTX_CHUNKS - 1 - i, N_CHUNKS + N_CTX_CHUNKS - 1 - i)


def _decay_terms(z, lb, same01, tri01):
    f = lb + (1.0 - lb) * _sigmoid(z)
    g = jnp.log(f)
    g2 = jnp.concatenate(_split2(g), axis=1)
    b2 = _dot(tri01, g2)
    t2 = _dot(same01, g2)
    return f, 1.0 - f, b2[:, :HG_DIM] + b2[:, HG_DIM:], t2[:, :HG_DIM] + t2[:, HG_DIM:]


def _chunk_outer(a, b):
    n = TM // CHUNK
    return jnp.einsum('ncv,nck->nvk', a.reshape(n, CHUNK, HG_DIM), b.reshape(n, CHUNK, HG_DIM),
                      preferred_element_type=F32)


def _hgrn_fwd(p_a, lbl, carried=None):
    cpt = TM // CHUNK

    def body(p_ref, lbl_ref, o_ref, st_ref, qd_s, kd_s, u_s, v_s, ebt_s):
        masks = [_chunk_masks(d == 1) for d in (0, 1)]
        same01 = jnp.where(masks[0][0], 1.0, 0.0).astype(BF16)
        tri = [m[1] for m in masks]
        tri01 = [jnp.where(t, 1.0, 0.0).astype(BF16) for t in tri]
        lb = [_sigmoid(lbl_ref[d][0:1, :] - lbl_ref[d][1:2, :]) for d in (0, 1)]

        def prep(r, carry):
            r0 = pl.multiple_of(r * TM, TM)
            vb = p_ref[pl.ds(r0, TM), 2 * HG_DIM:3 * HG_DIM].astype(BF16)
            v_s[pl.ds(r0, TM), :] = vb
            for d in (0, 1):
                z = p_ref[pl.ds(r0, TM), d * HG_DIM:(d + 1) * HG_DIM]
                _, k, b, bt = _decay_terms(z, lb[d], same01, tri01[d])
                u_s[d, pl.ds(r * cpt, cpt)] = _chunk_outer(vb, (k * jnp.exp(bt - b)).astype(BF16))
                ebt_s[d, pl.ds(r0, TM), :] = jnp.exp(bt)

                @pl.when(r >= 1)
                def _():
                    rl = pl.multiple_of(r0 - L, TM)
                    qr = p_ref[pl.ds(r0, TM), 3 * HG_DIM:4 * HG_DIM]
                    q = qr * _sigmoid(qr) * HG_DIM ** -0.5
                    qd_s[d, pl.ds(rl, TM), :] = (q * jnp.exp(b)).astype(BF16)
                    kd_s[d, pl.ds(rl, TM), :] = (k * jnp.exp(-b)).astype(BF16)

            return carry

        lax.fori_loop(0, N_TILES, prep, 0)

        def scan(i, sts):
            new = []
            for d in (0, 1):
                nn = _chunk_order(i, d == 1)
                c0 = pl.multiple_of(nn * CHUNK, CHUNK)
                st_ref[d, nn] = sts[d].astype(BF16)
                new.append(sts[d] * ebt_s[d, pl.ds(c0, 1), :] + u_s[d, nn])
            return tuple(new)

        zero = jnp.zeros((HG_DIM, HG_DIM), F32)
        lax.fori_loop(0, N_CHUNKS, scan, (zero, zero))

        def outp(r, carry):
            r0 = pl.multiple_of(r * TM, TM)
            vb = v_s[pl.ds(r0 + L, TM), :]
            o = jnp.zeros((TM, HG_DIM), F32)
            for d in (0, 1):
                qd = qd_s[d, pl.ds(r0, TM), :]
                a = jnp.where(tri[d], _dot_nt(qd, kd_s[d, pl.ds(r0, TM), :]), 0.0)
                stb = st_ref[d, pl.ds(N_CTX_CHUNKS + r * cpt, cpt)]
                inter = jnp.einsum('nck,nvk->ncv', qd.reshape(cpt, CHUNK, HG_DIM), stb,
                                   preferred_element_type=F32)
                o = o + _dot(a.astype(BF16), vb) + inter.reshape(TM, HG_DIM)
            o_ref[pl.ds(r0, TM), :] = o
            return carry

        lax.fori_loop(0, N_LAT_TILES, outp, 0, unroll=2)

    return _pcall(
        body, carried, name="hgrn_fwd", grid=(HG_HEADS,),
        in_specs=[pl.BlockSpec((T, 4 * HG_DIM), lambda h: (0, h)),
                  pl.BlockSpec((2, 2, HG_DIM), lambda h: (0, 0, h))],
        out_specs=[pl.BlockSpec((S, HG_DIM), lambda h: (0, h)),
                   pl.BlockSpec((2, None, N_CHUNKS, HG_DIM, HG_DIM), lambda h: (0, h, 0, 0, 0))],
        out_shape=[jax.ShapeDtypeStruct((S, HGW), F32),
                   jax.ShapeDtypeStruct((2, HG_HEADS, N_CHUNKS, HG_DIM, HG_DIM), BF16)],
        scratch_shapes=[pltpu.VMEM((2, S, HG_DIM), BF16), pltpu.VMEM((2, S, HG_DIM), BF16),
                        pltpu.VMEM((2, N_CHUNKS, HG_DIM, HG_DIM), F32), pltpu.VMEM((T, HG_DIM), BF16),
                        pltpu.VMEM((2, T, HG_DIM), F32)],
        operands=[p_a, lbl])


def _hgrn_bwd(p_a, lbl, d_o, st, carried=None):
    cpt = TM // CHUNK

    def rows(r):
        return r * TM if isinstance(r, int) else pl.multiple_of(r * TM, TM)

    def body(p_ref, lbl_ref, do_ref, st_ref, dp_ref, dlb_ref, b_s, bt_s, dbt_s, qd_s, dst_s, w_s):
        masks = [_chunk_masks(d == 1) for d in (0, 1)]
        same01 = jnp.where(masks[0][0], 1.0, 0.0).astype(BF16)
        tri = [m[1] for m in masks]
        tri01 = [jnp.where(t, 1.0, 0.0).astype(BF16) for t in tri]
        later01 = [tri01[1], tri01[0]]
        lb = [_sigmoid(lbl_ref[d][0:1, :] - lbl_ref[d][1:2, :]) for d in (0, 1)]

        def prep_tile(r, latent):
            r0 = rows(r)
            for d in (0, 1):
                z = p_ref[pl.ds(r0, TM), d * HG_DIM:(d + 1) * HG_DIM]
                _, _, b, bt = _decay_terms(z, lb[d], same01, tri01[d])
                b_s[d, pl.ds(r0, TM), :] = b
                bt_s[d, pl.ds(r0, TM), :] = bt
                if latent:
                    rl = pl.multiple_of(r0 - L, TM)
                    qr = p_ref[pl.ds(r0, TM), 3 * HG_DIM:4 * HG_DIM]
                    qd = (qr * _sigmoid(qr) * HG_DIM ** -0.5 * jnp.exp(b)).astype(BF16)
                    qd_s[d, pl.ds(rl, TM), :] = qd
                    w_s[d, pl.ds(r * cpt, cpt)] = _chunk_outer(
                        do_ref[pl.ds(rl, TM), :].astype(BF16), qd).astype(BF16)

        prep_tile(0, False)
        w_s[:, pl.ds(0, N_CTX_CHUNKS)] = jnp.zeros((2, N_CTX_CHUNKS, HG_DIM, HG_DIM), BF16)

        def prep(r, carry):
            prep_tile(r, True)
            return carry

        lax.fori_loop(1, N_TILES, prep, 0, unroll=2)

        def rscan(j, dsts):
            i = N_CHUNKS - 1 - j
            new = []
            for d in (0, 1):
                nn = _chunk_order(i, d == 1)
                c0 = pl.multiple_of(nn * CHUNK, CHUNK)
                dst_s[d, nn] = dsts[d].astype(BF16)
                after = st_ref[d, _chunk_order(jnp.minimum(i + 1, N_CHUNKS - 1), d == 1)].astype(F32)
                dbt_s[d, pl.ds(c0, CHUNK), :] = jnp.broadcast_to(
                    jnp.sum(after * dsts[d], axis=0, keepdims=True), (CHUNK, HG_DIM))
                new.append(dsts[d] * jnp.exp(bt_s[d, pl.ds(c0, 1), :]) + w_s[d, nn].astype(F32))
            return tuple(new)

        zero = jnp.zeros((HG_DIM, HG_DIM), F32)
        lax.fori_loop(0, N_CHUNKS, rscan, (zero, zero))

        def grad_tile(r, latent):
            r0 = rows(r)
            vb = p_ref[pl.ds(r0, TM), 2 * HG_DIM:3 * HG_DIM].astype(BF16)
            dv = jnp.zeros((TM, HG_DIM), F32)
            dq = jnp.zeros((TM, HG_DIM), F32)
            dlbs = []
            if latent:
                rl = pl.multiple_of(r0 - L, TM)
                qr = p_ref[pl.ds(r0, TM), 3 * HG_DIM:4 * HG_DIM]
                sq = _sigmoid(qr)
                do = do_ref[pl.ds(rl, TM), :].astype(BF16)
                da_full = _dot_nt(do, vb)
            for d in (0, 1):
                z = p_ref[pl.ds(r0, TM), d * HG_DIM:(d + 1) * HG_DIM]
                sz = _sigmoid(z)
                f = lb[d] + (1.0 - lb[d]) * sz
                k = 1.0 - f
                b = b_s[d, pl.ds(r0, TM), :]
                e2 = jnp.exp(bt_s[d, pl.ds(r0, TM), :] - b)
                dstb = dst_s[d, pl.ds(r * cpt, cpt)]
                kd2 = k * e2
                dkd2 = jnp.einsum('ncv,nvk->nck', vb.reshape(cpt, CHUNK, HG_DIM), dstb,
                                  preferred_element_type=F32).reshape(TM, HG_DIM)
                dv = dv + jnp.einsum('nck,nvk->ncv', kd2.astype(BF16).reshape(cpt, CHUNK, HG_DIM), dstb,
                                     preferred_element_type=F32).reshape(TM, HG_DIM)
                dk = dkd2 * e2
                db = -(kd2 * dkd2)
                if latent:
                    eb = jnp.exp(b)
                    enb = jnp.exp(-b)
                    qdf = qr * sq * HG_DIM ** -0.5 * eb
                    kdf = k * enb
                    qd = qd_s[d, pl.ds(rl, TM), :]
                    kd = kdf.astype(BF16)
                    a = jnp.where(tri[d], _dot_nt(qd, kd), 0.0).astype(BF16)
                    da = jnp.where(tri[d], da_full, 0.0).astype(BF16)
                    stb = st_ref[d, pl.ds(r * cpt, cpt)]
                    dqd = _dot(da, kd) + jnp.einsum(
                        'ncv,nvk->nck', do.reshape(cpt, CHUNK, HG_DIM), stb,
                        preferred_element_type=F32).reshape(TM, HG_DIM)
                    dkd = _dot_tn(da, qd)
                    dv = dv + _dot_tn(a, do)
                    dk = dk + dkd * enb
                    db = db + qdf * dqd - kdf * dkd
                    dq = dq + dqd * eb
                dg = _dot_lhs01(later01[d], db) + dbt_s[d, pl.ds(r0, TM), :]
                df = dg / f - dk
                dp_ref[pl.ds(r0, TM), d * HG_DIM:(d + 1) * HG_DIM] = (
                    df * (1.0 - lb[d]) * sz * (1.0 - sz)).astype(BF16)
                dlbs.append(jnp.sum(df * (1.0 - sz), axis=0, keepdims=True))
            dp_ref[pl.ds(r0, TM), 2 * HG_DIM:3 * HG_DIM] = dv.astype(BF16)
            if latent:
                dq = dq * (HG_DIM ** -0.5) * (sq * (1.0 + qr * (1.0 - sq)))
            dp_ref[pl.ds(r0, TM), 3 * HG_DIM:4 * HG_DIM] = dq.astype(BF16)
            return dlbs

        dlb_ctx = grad_tile(0, False)

        def grads(r, acc):
            t = grad_tile(r, True)
            return (acc[0] + t[0], acc[1] + t[1])

        dlb = lax.fori_loop(1, N_TILES, grads, (dlb_ctx[0], dlb_ctx[1]))
        dlb_ref[0:1, :] = dlb[0]
        dlb_ref[1:2, :] = dlb[1]

    return _pcall(
        body, carried, name="hgrn_bwd", grid=(HG_HEADS,),
        in_specs=[pl.BlockSpec((T, 4 * HG_DIM), lambda h: (0, h)),
                  pl.BlockSpec((2, 2, HG_DIM), lambda h: (0, 0, h)),
                  pl.BlockSpec((S, HG_DIM), lambda h: (0, h)),
                  pl.BlockSpec((2, None, N_CHUNKS, HG_DIM, HG_DIM), lambda h: (0, h, 0, 0, 0))],
        out_specs=[pl.BlockSpec((T, 4 * HG_DIM), lambda h: (0, h)),
                   pl.BlockSpec((2, HG_DIM), lambda h: (0, h))],
        out_shape=[jax.ShapeDtypeStruct((T, WA), BF16), jax.ShapeDtypeStruct((2, HGW), F32)],
        scratch_shapes=[pltpu.VMEM((2, T, HG_DIM), F32), pltpu.VMEM((2, T, HG_DIM), F32),
                        pltpu.VMEM((2, T, HG_DIM), F32), pltpu.VMEM((2, S, HG_DIM), BF16),
                        pltpu.VMEM((2, N_CHUNKS, HG_DIM, HG_DIM), BF16),
                        pltpu.VMEM((2, N_CHUNKS, HG_DIM, HG_DIM), BF16)],
        operands=[p_a, lbl, d_o, st])


def _rope_tables():
    t = np.arange(S)
    inv = ROPE_THETA ** (-np.arange(0, 32, 2, dtype=np.float64) / 32)
    lane = np.arange(64)
    pos = np.where(lane[None, :] < 32, (t // GRID_W)[:, None], (t % GRID_W)[:, None]).astype(np.float64)
    ang = pos * inv[(lane % 32) % 16][None, :]
    sign = np.where((lane % 32) < 16, -1.0, 1.0)[None, :]
    cos = np.tile(np.cos(ang), (1, 2)).astype(np.float32)
    sin = np.tile(np.sin(ang) * sign, (1, 2)).astype(np.float32)
    return jnp.asarray(cos), jnp.asarray(sin)


def _rope_partner(v):
    lane = lax.broadcasted_iota(jnp.int32, (1, 128), 1)
    first = (lane % 32) < 16
    slabs = []
    for j in range(v.shape[1] // 128):
        s = v[:, 128 * j:128 * (j + 1)]
        slabs.append(jnp.where(first, pltpu.roll(s, 112, 1), pltpu.roll(s, 16, 1)))
    return slabs[0] if len(slabs) == 1 else jnp.concatenate(slabs, axis=1)


def _group_ones(width, group):
    r = lax.broadcasted_iota(jnp.int32, (width, width), 0)
    c = lax.broadcasted_iota(jnp.int32, (width, width), 1)
    return jnp.where((r // group) == (c // group), 1.0, 0.0).astype(BF16)


def _group_mean(v, ones01, group):
    hi = v.astype(BF16)
    lo = (v - hi.astype(F32)).astype(BF16)
    return (_dot(hi, ones01) + _dot(lo, ones01)) * (1.0 / group)


def _rep_matrix():
    r = lax.broadcasted_iota(jnp.int32, (KVW, ATW), 0)
    c = lax.broadcasted_iota(jnp.int32, (KVW, ATW), 1)
    return jnp.where(r == HEAD_DIM * (c // 256) + c % HEAD_DIM, 1.0, 0.0).astype(BF16)


def _tile_lanes(v, reps):
    return jnp.concatenate([v] * reps, axis=1)


def _prep_fwd(p_b, o, cos, sin, hnw, qnw, knw):
    def body(p_ref, o_ref, cos_ref, sin_ref, hnw_ref, qnw_ref, knw_ref, y_ref, q_ref, k_ref, v_ref):
        i = pl.program_id(0)
        rep = _rep_matrix()
        ones_k = _group_ones(KVW, HEAD_DIM)
        kr = p_ref[:, 1024:1152]
        krstd = lax.rsqrt(_group_mean(kr * kr, ones_k, HEAD_DIM) + EPS)
        kn = kr * krstd * knw_ref[...]
        v_ref[...] = _dot(p_ref[:, 1152:1280].astype(BF16), rep).astype(BF16)

        @pl.when(i == 0)
        def _():
            k_ref[...] = _dot(kn.astype(BF16), rep).astype(BF16)

        @pl.when(i > 0)
        def _():
            cs, sn = cos_ref[...], sin_ref[...]
            kro = kn * cs + _rope_partner(kn) * sn
            k_ref[...] = _dot(kro.astype(BF16), rep).astype(BF16)
            qr = p_ref[:, 512:1024]
            qrstd = lax.rsqrt(_group_mean(qr * qr, _group_ones(ATW, HEAD_DIM), HEAD_DIM) + EPS)
            qn = qr * qrstd * qnw_ref[...]
            qro = qn * _tile_lanes(cs, 4) + _rope_partner(qn) * _tile_lanes(sn, 4)
            q_ref[...] = (qro * HEAD_DIM ** -0.5).astype(BF16)
            ys = []
            for h in range(HG_HEADS):
                oh = o_ref[:, HG_DIM * h:HG_DIM * (h + 1)]
                gh = p_ref[:, HG_DIM * h:HG_DIM * (h + 1)]
                rstd = lax.rsqrt(jnp.mean(oh * oh, axis=-1, keepdims=True) + EPS)
                ys.append(oh * rstd * hnw_ref[...] * (gh * _sigmoid(gh)))
            y_ref[...] = jnp.concatenate(ys, axis=1).astype(BF16)

    return pl.pallas_call(
        body, name="prep_fwd", grid=(N_TILES,),
        in_specs=[pl.BlockSpec((TM, WB), lambda i: (i, 0)),
                  pl.BlockSpec((TM, HGW), lambda i: (_lat(i), 0)),
                  pl.BlockSpec((TM, 128), lambda i: (_lat(i), 0)),
                  pl.BlockSpec((TM, 128), lambda i: (_lat(i), 0)),
                  _full((1, HG_DIM)), _full((1, ATW)), _full((1, KVW))],
        out_specs=[pl.BlockSpec((TM, HGW), lambda i: (_lat(i), 0)),
                   pl.BlockSpec((TM, ATW), lambda i: (_lat(i), 0)),
                   pl.BlockSpec((TM, ATW), lambda i: (i, 0)),
                   pl.BlockSpec((TM, ATW), lambda i: (i, 0))],
        out_shape=[jax.ShapeDtypeStruct((S, HGW), BF16), jax.ShapeDtypeStruct((S, ATW), BF16),
                   jax.ShapeDtypeStruct((T, ATW), BF16), jax.ShapeDtypeStruct((T, ATW), BF16)],
        compiler_params=_cp(("arbitrary",)),
    )(p_b, o, cos, sin, hnw, qnw, knw)


def _prep_bwd(p_b, o, cos, sin, hnw, qnw, knw, dy_hg, dq, dk_rep, dv_rep, carried=None):
    def body(p_ref, o_ref, cos_ref, sin_ref, hnw_ref, qnw_ref, knw_ref, dy_ref, dq_ref, dk_ref, dv_ref,
             dp_ref, do_ref, acc_ref):
        i = pl.program_id(0)

        @pl.when(i == 0)
        def _():
            acc_ref[...] = jnp.zeros_like(acc_ref)

        rep = _rep_matrix()
        ones_k = _group_ones(KVW, HEAD_DIM)

        def fold(v):
            hi = v.astype(BF16)
            lo = (v - hi.astype(F32)).astype(BF16)
            return _dot_nt(hi, rep) + _dot_nt(lo, rep)

        kr = p_ref[:, 1024:1152]
        krstd = lax.rsqrt(_group_mean(kr * kr, ones_k, HEAD_DIM) + EPS)
        khat = kr * krstd
        kw = knw_ref[...]
        dkro = fold(dk_ref[...])
        dv = fold(dv_ref[...])

        def k_back(dkn):
            dkhat = dkn * kw
            dkr = krstd * (dkhat - khat * _group_mean(dkhat * khat, ones_k, HEAD_DIM))
            acc_ref[2:3, 0:KVW] += jnp.sum(dkn * khat, axis=0, keepdims=True)
            dp_ref[:, 1024:1152] = dkr.astype(BF16)
            dp_ref[:, 1152:1280] = dv.astype(BF16)

        @pl.when(i == 0)
        def _():
            k_back(dkro)
            dp_ref[:, 0:1024] = jnp.zeros((TM, 1024), BF16)

        @pl.when(i > 0)
        def _():
            cs, sn = cos_ref[...], sin_ref[...]
            k_back(dkro * cs + _rope_partner(dkro * sn))
            ones_q = _group_ones(ATW, HEAD_DIM)
            qr = p_ref[:, 512:1024]
            qrstd = lax.rsqrt(_group_mean(qr * qr, ones_q, HEAD_DIM) + EPS)
            qhat = qr * qrstd
            dqro = dq_ref[...] * HEAD_DIM ** -0.5
            dqn = dqro * _tile_lanes(cs, 4) + _rope_partner(dqro * _tile_lanes(sn, 4))
            dqhat = dqn * qnw_ref[...]
            dqr = qrstd * (dqhat - qhat * _group_mean(dqhat * qhat, ones_q, HEAD_DIM))
            acc_ref[1:2, :] += jnp.sum(dqn * qhat, axis=0, keepdims=True)
            dp_ref[:, 512:1024] = dqr.astype(BF16)
            dws = jnp.zeros((1, HG_DIM), F32)
            for h in range(HG_HEADS):
                sl = slice(HG_DIM * h, HG_DIM * (h + 1))
                oh, gh, dy = o_ref[:, sl], p_ref[:, sl], dy_ref[:, sl]
                rstd = lax.rsqrt(jnp.mean(oh * oh, axis=-1, keepdims=True) + EPS)
                ohat = oh * rstd
                sg = _sigmoid(gh)
                dp_ref[:, sl] = (dy * (ohat * hnw_ref[...]) * (sg * (1.0 + gh * (1.0 - sg)))).astype(BF16)
                dn = dy * (gh * sg)
                dws = dws + jnp.sum(dn * ohat, axis=0, keepdims=True)
                dohat = dn * hnw_ref[...]
                do_ref[:, sl] = rstd * (dohat - ohat * jnp.mean(dohat * ohat, axis=-1, keepdims=True))
            acc_ref[0:1, 0:HG_DIM] += dws

    return _pcall(
        body, carried, name="prep_bwd", grid=(N_TILES,),
        in_specs=[pl.BlockSpec((TM, WB), lambda i: (i, 0)),
                  pl.BlockSpec((TM, HGW), lambda i: (_lat(i), 0)),
                  pl.BlockSpec((TM, 128), lambda i: (_lat(i), 0)),
                  pl.BlockSpec((TM, 128), lambda i: (_lat(i), 0)),
                  _full((1, HG_DIM)), _full((1, ATW)), _full((1, KVW)),
                  pl.BlockSpec((TM, HGW), lambda i: (_lat(i), 0)),
                  pl.BlockSpec((TM, ATW), lambda i: (_lat(i), 0)),
                  pl.BlockSpec((TM, ATW), lambda i: (i, 0)),
                  pl.BlockSpec((TM, ATW), lambda i: (i, 0))],
        out_specs=[pl.BlockSpec((TM, WB), lambda i: (i, 0)),
                   pl.BlockSpec((TM, HGW), lambda i: (_lat(i), 0)),
                   _full((8, ATW))],
        out_shape=[jax.ShapeDtypeStruct((T, WB), BF16), jax.ShapeDtypeStruct((S, HGW), F32),
                   jax.ShapeDtypeStruct((8, ATW), F32)],
        scratch_shapes=[], operands=[p_b, o, cos, sin, hnw, qnw, knw, dy_hg, dq, dk_rep, dv_rep])


NEG = -1e30
_CTX_BLOCKS = L // BLOCK


def _attn_window_specs():
    prev = pl.BlockSpec((BLOCK, ATW), lambda i: (jnp.maximum(i - 1, 0) + _CTX_BLOCKS, 0))
    own = pl.BlockSpec((BLOCK, ATW), lambda i: (i + _CTX_BLOCKS, 0))
    nxt = pl.BlockSpec((BLOCK, ATW), lambda i: (jnp.minimum(i + 1, N_BLOCKS - 1) + _CTX_BLOCKS, 0))
    return [prev, own, nxt, _full((L, ATW))]


def _attn_valid(i, heads, context):
    n_keys = 3 * BLOCK + (L if context else 0)
    qi = lax.broadcasted_iota(jnp.int32, (heads * BLOCK, n_keys), 0) % BLOCK
    kj = lax.broadcasted_iota(jnp.int32, (heads * BLOCK, n_keys), 1)
    window = ((jnp.abs(kj - BLOCK - qi) <= BLOCK) & ((kj >= BLOCK) | (i > 0))
              & ((kj < 2 * BLOCK) | (i < N_BLOCKS - 1)))
    return window | (kj >= 3 * BLOCK)


def _stack_heads(qg):
    lane = lax.broadcasted_iota(jnp.int32, (1, 256), 1) // HEAD_DIM
    return jnp.concatenate([jnp.where(lane == g, qg, jnp.zeros_like(qg)) for g in range(4)], axis=0)


def _unstack_heads(v4):
    lane = lax.broadcasted_iota(jnp.int32, (1, 256), 1) // HEAD_DIM
    out = jnp.where(lane == 0, v4[0:BLOCK], 0.0)
    for g in range(1, 4):
        out = out + jnp.where(lane == g, v4[g * BLOCK:(g + 1) * BLOCK], 0.0)
    return out


def _sink_rows(sink_ref, hk):
    return jnp.concatenate(
        [jnp.broadcast_to(sink_ref[0:1, 4 * hk + g:4 * hk + g + 1], (BLOCK, 1)) for g in range(4)], axis=0)


def _attn_fwd(q, k_rep, v_rep, sinks, carried=None):
    def body(q_ref, kp, ko, kn, kc, vp, vo, vn, vc, sink_ref, y_ref, lse_ref):
        i = pl.program_id(0)
        valid = _attn_valid(i, 1, True)
        lane8 = lax.broadcasted_iota(jnp.int32, (1, ATT_HEADS), 1)
        head_of_lane = lax.broadcasted_iota(jnp.int32, (1, 256), 1) // HEAD_DIM
        lse_out = jnp.zeros((BLOCK, ATT_HEADS), F32)
        for hk in range(KV_HEADS):
            sl = slice(256 * hk, 256 * (hk + 1))
            qg = q_ref[:, sl]
            keys = jnp.concatenate([kp[:, sl], ko[:, sl], kn[:, sl], kc[:, sl]], axis=0)
            vals = jnp.concatenate([vp[:, sl], vo[:, sl], vn[:, sl], vc[:, sl]], axis=0)
            yg = jnp.zeros((BLOCK, 256), F32)
            for g in range(4):
                q1 = jnp.where(head_of_lane == g, qg, jnp.zeros_like(qg))
                s = jnp.where(valid, _dot_nt(q1, keys), NEG)
                sink = sink_ref[0:1, 4 * hk + g:4 * hk + g + 1]
                m = jnp.maximum(jnp.max(s, axis=1, keepdims=True), sink)
                p = jnp.exp(s - m)
                den = jnp.sum(p, axis=1, keepdims=True) + jnp.exp(sink - m)
                o1 = _dot(p.astype(BF16), vals) * (1.0 / den)
                yg = yg + jnp.where(head_of_lane == g, o1, 0.0)
                lse_out = lse_out + jnp.where(lane8 == 4 * hk + g, m + jnp.log(den), 0.0)
            y_ref[:, sl] = yg.astype(BF16)
        lse_ref[...] = lse_out

    return _pcall(
        body, carried, name="attn_fwd", grid=(N_BLOCKS,),
        in_specs=[pl.BlockSpec((BLOCK, ATW), lambda i: (i, 0))] + _attn_window_specs()
        + _attn_window_specs() + [_full((1, ATT_HEADS))],
        out_specs=[pl.BlockSpec((BLOCK, ATW), lambda i: (i, 0)),
                   pl.BlockSpec((BLOCK, ATT_HEADS), lambda i: (i, 0))],
        out_shape=[jax.ShapeDtypeStruct((S, ATW), BF16), jax.ShapeDtypeStruct((S, ATT_HEADS), F32)],
        scratch_shapes=[],
        operands=[q, k_rep, k_rep, k_rep, k_rep, v_rep, v_rep, v_rep, v_rep, sinks])


def _attn_bwd(q, k_rep, v_rep, sinks, y_at, lse, dy, carried=None):
    def body(q_ref, kp, ko, kn, kc, vp, vo, vn, vc, sink_ref, y_ref, lse_ref, dy_ref,
             dq_ref, dk_ref, dv_ref, dsink_ref, dk_acc, dv_acc):
        i = pl.program_id(0)

        @pl.when(i == 0)
        def _():
            dk_acc[...] = jnp.zeros_like(dk_acc)
            dv_acc[...] = jnp.zeros_like(dv_acc)
            dk_ref[pl.ds(0, L), :] = jnp.zeros((L, ATW), F32)
            dv_ref[pl.ds(0, L), :] = jnp.zeros((L, ATW), F32)
            dsink_ref[...] = jnp.zeros_like(dsink_ref)

        valid = _attn_valid(i, 4, False)
        lane8 = lax.broadcasted_iota(jnp.int32, (1, ATT_HEADS), 1)
        w0 = pl.multiple_of(i * BLOCK, BLOCK)
        dsink = jnp.zeros((1, ATT_HEADS), F32)
        for hk in range(KV_HEADS):
            sl = slice(256 * hk, 256 * (hk + 1))
            q4 = _stack_heads(q_ref[:, sl])
            do4f = _stack_heads(dy_ref[:, sl])
            o4 = _stack_heads(y_ref[:, sl]).astype(F32)
            do4 = do4f.astype(BF16)
            kl = jnp.concatenate([kp[:, sl], ko[:, sl], kn[:, sl]], axis=0)
            vl = jnp.concatenate([vp[:, sl], vo[:, sl], vn[:, sl]], axis=0)
            lse4 = jnp.concatenate(
                [jnp.sum(jnp.where(lane8 == 4 * hk + g, lse_ref[...], 0.0), axis=1, keepdims=True)
                 for g in range(4)], axis=0)
            p_loc = jnp.where(valid, jnp.exp(_dot_nt(q4, kl) - lse4), 0.0)
            p_ctx = jnp.exp(_dot_nt(q4, kc[:, sl]) - lse4)
            delta = jnp.sum(do4f * o4, axis=1, keepdims=True)
            ds_loc = (p_loc * (_dot_nt(do4, vl) - delta)).astype(BF16)
            ds_ctx = (p_ctx * (_dot_nt(do4, vc[:, sl]) - delta)).astype(BF16)
            dq_ref[:, sl] = _unstack_heads(_dot(ds_loc, kl) + _dot(ds_ctx, kc[:, sl]))
            dk_acc[pl.ds(w0, 3 * BLOCK), sl] += _dot_tn(ds_loc, q4)
            dv_acc[pl.ds(w0, 3 * BLOCK), sl] += _dot_tn(p_loc.astype(BF16), do4)
            dk_ref[pl.ds(0, L), sl] += _dot_tn(ds_ctx, q4)
            dv_ref[pl.ds(0, L), sl] += _dot_tn(p_ctx.astype(BF16), do4)
            p_sink = jnp.exp(_sink_rows(sink_ref, hk) - lse4)
            for g in range(4):
                rows = slice(g * BLOCK, (g + 1) * BLOCK)
                dsink = dsink + jnp.where(lane8 == 4 * hk + g,
                                          -jnp.sum(p_sink[rows] * delta[rows], axis=0, keepdims=True), 0.0)
        dsink_ref[...] += dsink

        @pl.when(i == N_BLOCKS - 1)
        def _():
            dk_ref[pl.ds(L, S), :] = dk_acc[pl.ds(BLOCK, S), :]
            dv_ref[pl.ds(L, S), :] = dv_acc[pl.ds(BLOCK, S), :]

    row_q = pl.BlockSpec((BLOCK, ATW), lambda i: (i, 0))
    return _pcall(
        body, carried, name="attn_bwd", grid=(N_BLOCKS,),
        in_specs=[row_q] + _attn_window_specs() + _attn_window_specs()
        + [_full((1, ATT_HEADS)), row_q, pl.BlockSpec((BLOCK, ATT_HEADS), lambda i: (i, 0)), row_q],
        out_specs=[row_q, _full((T, ATW)), _full((T, ATW)), _full((1, ATT_HEADS))],
        out_shape=[jax.ShapeDtypeStruct((S, ATW), F32), jax.ShapeDtypeStruct((T, ATW), F32),
                   jax.ShapeDtypeStruct((T, ATW), F32), jax.ShapeDtypeStruct((1, ATT_HEADS), F32)],
        scratch_shapes=[pltpu.VMEM((S + 2 * BLOCK, ATW), F32), pltpu.VMEM((S + 2 * BLOCK, ATW), F32)],
        operands=[q, k_rep, k_rep, k_rep, k_rep, v_rep, v_rep, v_rep, v_rep, sinks, y_at, lse, dy])


def _merge_fwd(y_hg, y_at, p_c, x, w_bh, w_ba, w_out, g1, nfw, sh2, sc2, carried=None):
    def body(yh_ref, ya_ref, g_ref, x_ref, wbh_ref, wba_ref, wo_ref, g1_ref, nfw_ref, sh_ref, sc_ref,
             mx_ref, r_ref, x1_ref, h2_ref):
        a = _dot_nt(yh_ref[...], wbh_ref[...])
        b = _dot_nt(ya_ref[...], wba_ref[...])
        mixed = (_sigmoid(g_ref[:, :D]) * a + _sigmoid(g_ref[:, D:]) * b).astype(BF16)
        r = _dot(mixed, wo_ref[...])
        x1 = x_ref[...] + g1_ref[...] * r
        mx_ref[...] = mixed
        r_ref[...] = r
        x1_ref[...] = x1
        h2_ref[...] = _rms_mod(x1, nfw_ref[...], sh_ref[...], sc_ref[...]).astype(BF16)

    row = lambda w: pl.BlockSpec((TM, w), lambda i: (i, 0))
    vec = _full((1, D))
    return _pcall(
        body, carried, name="merge_fwd", grid=(N_LAT_TILES,),
        in_specs=[row(HGW), row(ATW), row(WC), row(D), _VMEM_WHOLE, _VMEM_WHOLE, _VMEM_WHOLE,
                  vec, vec, vec, vec],
        out_specs=[row(D)] * 4,
        out_shape=[jax.ShapeDtypeStruct((S, D), dt) for dt in (BF16, F32, F32, BF16)],
        scratch_shapes=[], operands=[y_hg, y_at, p_c, x, w_bh, w_ba, w_out, g1, nfw, sh2, sc2])


def _merge_bwd(dx1, r, y_hg, y_at, p_c, w_bh, w_ba, w_out, g1, carried=None):
    def body(dx_ref, r_ref, yh_ref, ya_ref, g_ref, wbh_ref, wba_ref, wo_ref, g1_ref,
             dr_ref, da_ref, db_ref, dg_ref, dyh_ref, dya_ref, acc_ref):
        @pl.when(pl.program_id(0) == 0)
        def _():
            acc_ref[...] = jnp.zeros_like(acc_ref)

        dx1v = dx_ref[...]
        acc_ref[0:1, :] += jnp.sum(dx1v * r_ref[...], axis=0, keepdims=True)
        dr = (g1_ref[...] * dx1v).astype(BF16)
        dr_ref[...] = dr
        dmix = _dot_nt(dr, wo_ref[...])
        sh, sa = _sigmoid(g_ref[:, :D]), _sigmoid(g_ref[:, D:])
        da = (dmix * sh).astype(BF16)
        db = (dmix * sa).astype(BF16)
        da_ref[...] = da
        db_ref[...] = db
        dg_ref[:, :D] = (dmix * _dot_nt(yh_ref[...], wbh_ref[...]) * sh * (1.0 - sh)).astype(BF16)
        dg_ref[:, D:] = (dmix * _dot_nt(ya_ref[...], wba_ref[...]) * sa * (1.0 - sa)).astype(BF16)
        dyh_ref[...] = _dot(da, wbh_ref[...])
        dya_ref[...] = _dot(db, wba_ref[...])

    row = lambda w: pl.BlockSpec((TM, w), lambda i: (i, 0))
    return _pcall(
        body, carried, name="merge_bwd", grid=(N_LAT_TILES,),
        in_specs=[row(D), row(D), row(HGW), row(ATW), row(WC), _VMEM_WHOLE, _VMEM_WHOLE, _VMEM_WHOLE,
                  _full((1, D))],
        out_specs=[row(D), row(D), row(D), row(WC), row(HGW), row(ATW), _full((8, D))],
        out_shape=[jax.ShapeDtypeStruct((S, D), BF16), jax.ShapeDtypeStruct((S, D), BF16),
                   jax.ShapeDtypeStruct((S, D), BF16), jax.ShapeDtypeStruct((S, WC), BF16),
                   jax.ShapeDtypeStruct((S, HGW), F32), jax.ShapeDtypeStruct((S, ATW), F32),
                   jax.ShapeDtypeStruct((8, D), F32)],
        scratch_shapes=[], operands=[dx1, r, y_hg, y_at, p_c, w_bh, w_ba, w_out, g1])


def _ffn_fused(x1, h2, tgt, w_gate, w_up, w_down, g2, nfw, sc2):
    def body(x1_ref, h2_ref, t_ref, wg_ref, wu_ref, wd_ref, g2_ref, nfw_ref, sc_ref,
             act_ref, dgt_ref, dup_ref, df_ref, dx_ref, acc_ref, gs, us):
        @pl.when(pl.program_id(0) == 0)
        def _():
            acc_ref[...] = jnp.zeros_like(acc_ref)

        h2 = h2_ref[...]
        whole = lambda w_ref: w_ref[...].reshape(D_FF, D)
        tile = lambda j: slice(j * FF_TILE, (j + 1) * FF_TILE)
        for j in range(N_FF_TILES):
            g = _dot_nt(h2, wg_ref[j])
            u = _dot_nt(h2, wu_ref[j])
            gs[j] = g
            us[j] = u
            act_ref[:, tile(j)] = (g * _sigmoid(g) * u).astype(BF16)
        f = _dot(act_ref[...], whole(wd_ref))
        x1v = x1_ref[...]
        g2 = g2_ref[...]
        diff = x1v + g2 * f - t_ref[...]
        dy = diff * (1.0 / D)
        df = (g2 * dy).astype(BF16)
        df_ref[...] = df
        dact_all = _dot_nt(df, whole(wd_ref))
        for j in range(N_FF_TILES):
            g, u = gs[j], us[j]
            sg = _sigmoid(g)
            dact = dact_all[:, tile(j)]
            dgt_ref[:, tile(j)] = (dact * u * (sg * (1.0 + g * (1.0 - sg)))).astype(BF16)
            dup_ref[:, tile(j)] = (dact * (g * sg)).astype(BF16)
        dh2 = _dot(dgt_ref[...], whole(wg_ref)) + _dot(dup_ref[...], whole(wu_ref))
        dx, dsh, dsc, dnw = _rms_mod_bwd(x1v, nfw_ref[...], sc_ref[...], dh2)
        dx_ref[...] = dy + dx
        acc_ref[0:1, :] += dsh
        acc_ref[1:2, :] += dsc
        acc_ref[2:3, :] += dnw
        acc_ref[3:4, :] += jnp.sum(dy * f, axis=0, keepdims=True)
        acc_ref[4:5, :] += 0.5 * jnp.sum(jnp.sum(diff * diff, axis=1, keepdims=True), axis=0,
                                         keepdims=True) * (1.0 / D)

    row = lambda dt_w: pl.BlockSpec((TM, dt_w), lambda i: (i, 0))
    blk = row(D_FF)
    vec = _full((1, D))
    return pl.pallas_call(
        body, name="ffn_fused", grid=(N_LAT_TILES,),
        in_specs=[row(D), row(D), row(D), _VMEM_WHOLE, _VMEM_WHOLE, _VMEM_WHOLE, vec, vec, vec],
        out_specs=[blk, blk, blk, row(D), row(D), _full((8, D))],
        out_shape=[jax.ShapeDtypeStruct((S, D_FF), BF16)] * 3
        + [jax.ShapeDtypeStruct((S, D), BF16), jax.ShapeDtypeStruct((S, D), F32),
           jax.ShapeDtypeStruct((8, D), F32)],
        scratch_shapes=[pltpu.VMEM((N_FF_TILES, TM, FF_TILE), F32), pltpu.VMEM((N_FF_TILES, TM, FF_TILE), F32)],
        compiler_params=_cp(("arbitrary",)),
    )(x1, h2, tgt, w_gate, w_up, w_down, g2, nfw, sc2)


def _proj_bc(h_all, w_b, w_c, carried=None):
    def body(h_ref, wb_ref, wc_ref, pb_ref, pc_ref):
        h = h_ref[...]
        pb_ref[...] = _dot_nt(h, wb_ref[...])

        @pl.when(pl.program_id(0) > 0)
        def _():
            pc_ref[...] = _dot_nt(h, wc_ref[...])

    return _pcall(
        body, carried, name="proj_bc", grid=(N_TILES,),
        in_specs=[pl.BlockSpec((TM, D), lambda i: (i, 0)), _VMEM_WHOLE, _VMEM_WHOLE],
        out_specs=[pl.BlockSpec((TM, WB), lambda i: (i, 0)), pl.BlockSpec((TM, WC), lambda i: (_lat(i), 0))],
        out_shape=[jax.ShapeDtypeStruct((T, WB), F32), jax.ShapeDtypeStruct((S, WC), F32)],
        scratch_shapes=[], operands=[h_all, w_b, w_c])


def _input_bwd(dp_a, dp_b, dp_c, w_a, w_b, w_c, ctx, x, dx1, nw, sh, sc, carried=None):
    def body(da_ref, db_ref, dc_ref, wa_ref, wb_ref, wc_ref, ctx_ref, x_ref, dx1_ref, nw_ref, sh_ref,
             sc_ref, gx_ref, acc_ref):
        i = pl.program_id(0)

        @pl.when(i == 0)
        def _():
            acc_ref[...] = jnp.zeros_like(acc_ref)

        dh = _dot(da_ref[...], wa_ref[...]) + _dot(db_ref[...], wb_ref[...])

        @pl.when(i == 0)
        def _():
            _, dsh, dsc, dnw = _rms_mod_bwd(ctx_ref[...], nw_ref[...], sc_ref[0:1, :], dh)
            acc_ref[3:4, :] += dsh
            acc_ref[4:5, :] += dsc
            acc_ref[2:3, :] += dnw

        @pl.when(i > 0)
        def _():
            dhl = dh + _dot(dc_ref[...], wc_ref[...])
            dx, dsh, dsc, dnw = _rms_mod_bwd(x_ref[...], nw_ref[...], sc_ref[1:2, :], dhl)
            gx_ref[...] = dx1_ref[...] + dx
            acc_ref[0:1, :] += dsh
            acc_ref[1:2, :] += dsc
            acc_ref[2:3, :] += dnw

    lat = lambda w: pl.BlockSpec((TM, w), lambda i: (_lat(i), 0))
    return _pcall(
        body, carried, name="input_bwd", grid=(N_TILES,),
        in_specs=[pl.BlockSpec((TM, WA), lambda i: (i, 0)), pl.BlockSpec((TM, WB), lambda i: (i, 0)),
                  lat(WC), _VMEM_WHOLE, _VMEM_WHOLE, _VMEM_WHOLE, _full((TM, D)), lat(D), lat(D),
                  _full((1, D)), _full((2, D)), _full((2, D))],
        out_specs=[lat(D), _full((8, D))],
        out_shape=[jax.ShapeDtypeStruct((S, D), F32), jax.ShapeDtypeStruct((8, D), F32)],
        scratch_shapes=[], operands=[dp_a, dp_b, dp_c, w_a, w_b, w_c, ctx, x, dx1, nw, sh, sc])


_C1 = 1.0 - ADAM_B1 ** ADAM_STEP
_C2 = 1.0 - ADAM_B2 ** ADAM_STEP


def _adamw_math(w, g, m, v):
    m = ADAM_B1 * m + (1.0 - ADAM_B1) * g
    v = ADAM_B2 * v + (1.0 - ADAM_B2) * (g * g)
    m_hat = m / _C1
    v_hat = v / _C2
    delta = -ADAM_LR * (m_hat / (jnp.sqrt(v_hat) + ADAM_EPS) + ADAM_WD * w)
    return delta, m, v


def _adamw_sharded(terms, w, m, v, name, tr, extra=None):
    rows, cols = w.shape

    def body(*refs):
        t_ref, w_ref, m_ref, v_ref = refs[:4]
        g_ref, d_ref, nm_ref, nv_ref = refs[-4:]
        g = t_ref[0].astype(F32)
        for s in range(1, N_CHIPS):
            g = g + t_ref[s].astype(F32)
        if extra is not None:
            g = g + refs[4][...].astype(F32)
        g_ref[...] = g
        d_ref[...], nm_ref[...], nv_ref[...] = _adamw_math(w_ref[...], g, m_ref[...], v_ref[...])

    blk = pl.BlockSpec((tr, cols), lambda i: (i, 0))
    return pl.pallas_call(
        body, name=name, grid=(rows // tr,),
        in_specs=[pl.BlockSpec((N_CHIPS, tr, cols), lambda i: (0, i, 0)), blk, blk, blk]
        + ([blk] if extra is not None else []),
        out_specs=[blk] * 4,
        out_shape=[jax.ShapeDtypeStruct((rows, cols), F32)] * 4,
        compiler_params=_cp(("parallel",)),
    )(terms, w, m, v, *([extra] if extra is not None else []))


def _adamw_plain(g, w, m, v, name, tr=None):
    def body(g_ref, w_ref, m_ref, v_ref, d_ref, nm_ref, nv_ref):
        d_ref[...], nm_ref[...], nv_ref[...] = _adamw_math(w_ref[...], g_ref[...], m_ref[...], v_ref[...])

    if tr is None:
        return pl.pallas_call(
            body, name=name, in_specs=[_VMEM_WHOLE] * 4, out_specs=[_VMEM_WHOLE] * 3,
            out_shape=[jax.ShapeDtypeStruct(w.shape, F32)] * 3,
            compiler_params=_cp(),
        )(g, w, m, v)
    blk = pl.BlockSpec((tr, w.shape[1]), lambda i: (i, 0))
    return pl.pallas_call(
        body, name=name, grid=(w.shape[0] // tr,), in_specs=[blk] * 4, out_specs=[blk] * 3,
        out_shape=[jax.ShapeDtypeStruct(w.shape, F32)] * 3,
        compiler_params=_cp(("parallel",)),
    )(g, w, m, v)


SMALL_ROWS = 16
R_DMOD, R_DCTX, R_NMIX, R_NFFN, R_MISC, R_DLB, R_BADA01 = 0, 6, 8, 9, 10, 11, 13
M_HNW, M_QNW, M_KNW, M_SINK, M_LOSS = 0, 128, 256, 384, 512


def _pack_small(acc_in, acc_mg, acc_ffn, acc_prep, dsink, dlb):
    def body(in_ref, mg_ref, ff_ref, pp_ref, ds_ref, dlb_ref, o_ref):
        o_ref[...] = jnp.zeros_like(o_ref)
        o_ref[0:2, :] = in_ref[0:2, :]
        o_ref[2:3, :] = mg_ref[0:1, :]
        o_ref[3:5, :] = ff_ref[0:2, :]
        o_ref[5:6, :] = ff_ref[3:4, :]
        o_ref[6:8, :] = in_ref[3:5, :]
        o_ref[8:9, :] = in_ref[2:3, :]
        o_ref[9:10, :] = ff_ref[2:3, :]
        o_ref[10:11, M_HNW:M_HNW + HG_DIM] = pp_ref[0:1, 0:HG_DIM]
        r = lax.broadcasted_iota(jnp.int32, (ATW, 128), 0)
        c = lax.broadcasted_iota(jnp.int32, (ATW, 128), 1)
        fold = jnp.where((r % HEAD_DIM == c) & (c < HEAD_DIM), 1.0, 0.0).astype(BF16)
        qk = jnp.concatenate([pp_ref[1:2, :], pp_ref[2:3, :], jnp.zeros((6, ATW), F32)], axis=0)
        folded = _dot_exact_rhs01(qk, fold)
        o_ref[10:11, M_QNW:M_QNW + 128] = folded[0:1, :]
        o_ref[10:11, M_KNW:M_KNW + 128] = folded[1:2, :]
        o_ref[10:11, M_SINK:M_SINK + ATT_HEADS] = ds_ref[...]
        o_ref[10:11, M_LOSS:M_LOSS + 128] = ff_ref[4:5, 0:128]
        o_ref[11:13, 0:HGW] = dlb_ref[...]

    return pl.pallas_call(
        body, name="pack_small", in_specs=[_VMEM_WHOLE] * 6, out_specs=_VMEM_WHOLE,
        out_shape=jax.ShapeDtypeStruct((SMALL_ROWS, D), F32), compiler_params=_cp(),
    )(acc_in, acc_mg, acc_ffn, acc_prep, dsink, dlb)


def _sum_small(gathered):
    def body(g_ref, o_ref):
        tot = g_ref[0]
        for s in range(1, N_DEV):
            tot = tot + g_ref[s]
        o_ref[...] = tot
        o_ref[R_BADA01:R_BADA01 + 2, :] = tot[0:2, :] + tot[R_DCTX:R_DCTX + 2, :]

    return pl.pallas_call(
        body, name="sum_small", in_specs=[_VMEM_WHOLE], out_specs=_VMEM_WHOLE,
        out_shape=jax.ShapeDtypeStruct((SMALL_ROWS, D), F32), compiler_params=_cp(),
    )(gathered)


_REP_NAMES = ("b_ada", "c_ctx", "norm_mix_w", "norm_ffn_w", "hgrn_norm_w", "q_norm_w", "k_norm_w", "attn_sinks")


def _adamw_replicated(tot, g_c_ctx, ws, ms, vs):
    n = len(_REP_NAMES)

    def body(*refs):
        tot_ref, gc_ref = refs[0], refs[1]
        w_refs, m_refs, v_refs = refs[2:2 + n], refs[2 + n:2 + 2 * n], refs[2 + 2 * n:2 + 3 * n]
        outs = refs[2 + 3 * n:]
        row = lambda r: tot_ref[r:r + 1, :]
        misc = row(R_MISC)
        grads = [jnp.concatenate([row(R_BADA01), row(R_BADA01 + 1)] + [row(k) for k in range(2, 6)], axis=1),
                 gc_ref[...], row(R_NMIX), row(R_NFFN),
                 misc[:, M_HNW:M_HNW + HG_DIM], misc[:, M_QNW:M_QNW + HEAD_DIM],
                 misc[:, M_KNW:M_KNW + HEAD_DIM], misc[:, M_SINK:M_SINK + ATT_HEADS]]
        for k in range(n):
            outs[k][...] = grads[k]
            outs[n + k][...], outs[2 * n + k][...], outs[3 * n + k][...] = _adamw_math(
                w_refs[k][...], grads[k], m_refs[k][...], v_refs[k][...])

    shapes = [jax.ShapeDtypeStruct(w.shape, F32) for w in ws]
    return pl.pallas_call(
        body, name="adamw_replicated", in_specs=[_VMEM_WHOLE] * (2 + 3 * n), out_specs=[_VMEM_WHOLE] * (4 * n),
        out_shape=shapes * 4, compiler_params=_cp(),
    )(tot, g_c_ctx, *ws, *ms, *vs)


def _lb_grads(dlb, lbl):
    def body(d_ref, l_ref, o_ref):
        for d in (0, 1):
            ll = l_ref[d]
            lb = _sigmoid(ll[0:1, :] - ll[1:2, :])
            t = d_ref[d:d + 1, :] * lb * (1.0 - lb)
            o_ref[d, 0:1, :] = t
            o_ref[d, 1:2, :] = -t

    return pl.pallas_call(
        body, name="lb_grads", in_specs=[_VMEM_WHOLE] * 2, out_specs=_VMEM_WHOLE,
        out_shape=jax.ShapeDtypeStruct((2, 2, HGW), F32), compiler_params=_cp(),
    )(dlb, lbl)


def _c_ctx_grad(terms, c_ctx):
    def body(t_ref, c_ref, o_ref):
        tot = t_ref[0, 8:9, :]
        for s in range(1, N_DEV):
            tot = tot + t_ref[s, 8:9, :]
        cv = c_ref[...]
        sg = _sigmoid(cv)
        o_ref[...] = tot * (sg * (1.0 + cv * (1.0 - sg)))

    return pl.pallas_call(
        body, name="c_ctx_grad", in_specs=[_VMEM_WHOLE] * 2, out_specs=_VMEM_WHOLE,
        out_shape=jax.ShapeDtypeStruct((1, D), F32), compiler_params=_cp(),
    )(terms, c_ctx)


def _in_perm():
    fz, bz, inp, kk, vv, qhg, ghg, qat, gates = 0, 512, 1024, 1536, 1664, 1792, 2304, 2816, 3328
    cols = []
    for h in range(HG_HEADS):
        for base in (fz, bz, inp, qhg):
            cols += list(range(base + 128 * h, base + 128 * (h + 1)))
    cols += list(range(ghg, ghg + 512)) + list(range(qat, qat + 512))
    cols += list(range(kk, kk + 128)) + list(range(vv, vv + 128))
    cols += list(range(gates, gates + 2048))
    return np.asarray(cols, np.int32)


_PERM = _in_perm()


_PIECES = {"a": (0, WA, 128), "b": (WA, WB, 256), "c": (WA + WB, WC, 256)}


def _block_table(piece):
    lo, n, blk = _PIECES[piece]
    starts = [int(_PERM[r]) for r in range(lo, lo + n, blk)]
    assert all(s % blk == 0 and np.array_equal(_PERM[r:r + blk], np.arange(s, s + blk))
               for s, r in zip(starts, range(lo, lo + n, blk)))
    return jnp.asarray([s // blk for s in starts], jnp.int32), blk


def _pick_row_blocks(x, table, blk, name):
    cols = x.shape[1]

    def body(t_ref, x_ref, o_ref):
        o_ref[...] = x_ref[...]

    return pl.pallas_call(
        body, name=name,
        grid_spec=pltpu.PrefetchScalarGridSpec(
            num_scalar_prefetch=1, grid=(table.shape[0],),
            in_specs=[pl.BlockSpec((blk, cols), lambda i, t: (t[i], 0))],
            out_specs=pl.BlockSpec((blk, cols), lambda i, t: (i, 0))),
        out_shape=jax.ShapeDtypeStruct((table.shape[0] * blk, cols), x.dtype),
        compiler_params=_cp(("arbitrary",)),
    )(table, x)


def _place_row_blocks(x, table, blk, into, out_rows, name):
    cols = x.shape[1]

    def body(t_ref, x_ref, *rest):
        rest[-1][...] = x_ref[...]

    operands, in_specs, aliases = [table, x], [pl.BlockSpec((blk, cols), lambda i, t: (i, 0))], {}
    if into is not None:
        operands.append(into)
        in_specs.append(_ANY)
        aliases = {2: 0}
    return pl.pallas_call(
        body, name=name,
        grid_spec=pltpu.PrefetchScalarGridSpec(
            num_scalar_prefetch=1, grid=(table.shape[0],), in_specs=in_specs,
            out_specs=pl.BlockSpec((blk, cols), lambda i, t: (t[i], 0))),
        out_shape=jax.ShapeDtypeStruct((out_rows, cols), x.dtype),
        input_output_aliases=aliases,
        compiler_params=_cp(("arbitrary",)),
    )(*operands)


def _local_step(x2, ctx2, h_all, h_lat, tgt, lbl, sh_in, sc_in, gate1, sh2, sc2, gate2, norm_mix_w, norm_ffn_w,
                hgrn_norm_w, q_norm_w, k_norm_w, attn_sinks, w_a, w_b, w_c, s_bh, s_ba, s_out,
                s_gate, s_up, s_down):
    first_last = lambda n: [(0, True), (n - 1, False)]
    p_a = _mm_nt(h_all, w_a, tm=T, tn=512, out_dtype=F32, name="proj_a")
    (o, st), (g_gate, g_bh, g_ba) = _hgrn_fwd(
        p_a, lbl, (_gather_comm_relayed([s_gate, s_bh, s_ba]),
                   [(0, True), (HG_HEADS - 2, True), (HG_HEADS - 1, False)]))
    (p_b, p_c), (g_out,) = _proj_bc(
        h_all, w_b, w_c, (_gather_comm_relayed([s_out]), [(0, True), (N_TILES - 4, True), (N_TILES - 1, False)]))
    cos, sin = _rope_tables()
    qnw_t, knw_t = jnp.tile(q_norm_w, (1, ATT_HEADS)), jnp.tile(k_norm_w, (1, KV_HEADS))
    y_hg, qn, k_rep, v_rep = _prep_fwd(p_b, o, cos, sin, hgrn_norm_w, qnw_t, knw_t)
    (y_at, lse), (g_up, g_down) = _attn_fwd(
        qn, k_rep, v_rep, attn_sinks,
        (_gather_comm_relayed([s_up, s_down]), [(0, True), (N_BLOCKS - 6, True), (N_BLOCKS - 1, False)]))
    w_bh, w_ba, w_o = g_bh.reshape(D, HGW), g_ba.reshape(D, ATW), g_out.reshape(D, D)
    (mixed, r, x1, h2), _ = _merge_fwd(
        y_hg, y_at, p_c, x2, w_bh, w_ba, w_o, gate1, norm_ffn_w, sh2, sc2)
    g_gate, g_up, g_down = [g.reshape(N_FF_TILES, FF_TILE, D) for g in (g_gate, g_up, g_down)]

    act, d_gate, d_up, d_f, dx1, acc_ffn = _ffn_fused(x1, h2, tgt, g_gate, g_up, g_down, gate2,
                                                      norm_ffn_w, sc2)
    by_chip = lambda t: t.reshape((N_CHIPS, 2) + t.shape[1:])
    ff_by_chip = lambda t: t.reshape(N_CHIPS, 2, FF_BLK, D)
    t_down, _ = _mm_tn_blocked(act, d_f, "grad_down")
    t_down = ff_by_chip(t_down)
    t_gate, (f_down,) = _mm_tn_blocked(d_gate, h2, "grad_gate", (_sibling_comm([t_down]), first_last(N_FF_HALVES)))
    t_gate = ff_by_chip(t_gate)
    t_up, (f_gate,) = _mm_tn_blocked(d_up, h2, "grad_up", (_sibling_comm([t_gate]), first_last(N_FF_HALVES)))
    t_up = ff_by_chip(t_up)

    (d_r, d_a, d_b, dp_c, dy_hg, dy_at, acc_mg), (f_up,) = _merge_bwd(
        dx1, r, y_hg, y_at, p_c, w_bh, w_ba, w_o, gate1, (_sibling_comm([t_up]), first_last(N_LAT_TILES)))
    c_down, c_gate, c_up = [_pair_sum(t, f, "pair_sum_" + nm) for t, f, nm in
                            ((t_down, f_down, "down"), (t_gate, f_gate, "gate"), (t_up, f_up, "up"))]
    t_out = _mm_tn(mixed, d_r, tk=1024, nk=2, tm=1024, tn=1024, out_dtype=BF16, name="grad_out")
    t_bh = _mm_tn(d_a, y_hg, tk=2048, nk=1, tm=1024, tn=512, out_dtype=BF16, name="grad_bh")
    t_ba = _mm_tn(d_b, y_at, tk=2048, nk=1, tm=1024, tn=512, out_dtype=BF16, name="grad_ba")
    t_bh, t_ba, t_out = [by_chip(t.reshape(N_DEV, D // N_DEV, t.shape[1])) for t in (t_bh, t_ba, t_out)]
    (dq, dk_rep, dv_rep, dsink), (r_up,) = _attn_bwd(
        qn, k_rep, v_rep, attn_sinks, y_at, lse, dy_at, (_chip_comm([c_up]), first_last(N_BLOCKS)))
    (dp_b, d_o, acc_prep), (f_bh, f_ba, f_out) = _prep_bwd(
        p_b, o, cos, sin, hgrn_norm_w, qnw_t, knw_t, dy_hg, dq, dk_rep, dv_rep,
        (_sibling_comm([t_bh, t_ba, t_out]), first_last(N_TILES)))
    c_bh, c_ba, c_out = [_pair_sum(t, f, "pair_sum_" + nm) for t, f, nm in
                         ((t_bh, f_bh, "bh"), (t_ba, f_ba, "ba"), (t_out, f_out, "out"))]
    (dp_a, dlb), (r_bh, r_ba, r_out, r_down, r_gate) = _hgrn_bwd(
        p_a, lbl, d_o, st, (_chip_comm([c_bh, c_ba, c_out, c_down, c_gate]), first_last(HG_HEADS)))
    t_a = _mm_tn(dp_a, h_all, tk=T, nk=1, tm=1024, tn=1024, out_dtype=BF16, name="grad_in_a")
    t_b = _mm_tn(dp_b, h_all, tk=T, nk=1, tm=640, tn=1024, out_dtype=BF16, name="grad_in_b")
    t_c = _mm_tn(dp_c, h_lat, tk=1024, nk=2, tm=1024, tn=1024, out_dtype=BF16, name="grad_in_c")
    t_in = None
    for piece, nm in ((t_a, "a"), (t_b, "b"), (t_c, "c")):
        t_in = _place_row_blocks(piece, *_block_table(nm), t_in, IN_COLS, "order_terms_" + nm)
    t_in = by_chip(t_in.reshape(N_DEV, IN_BLK, D))
    (f_in,) = _run_comm(_sibling_comm([t_in]), "scatter_in_sibling")
    c_in = _pair_sum(t_in, f_in, "pair_sum_in")
    sems, c_in, land, token = _chip_exchange_start(c_in, jnp.zeros(c_in.shape, c_in.dtype))
    (grad_x, acc_in), _ = _input_bwd(dp_a, dp_b, dp_c, w_a, w_b, w_c, ctx2, x2, dx1,
                                     norm_mix_w + token[0, 0], sh_in, sc_in)
    small = _pack_small(acc_in, acc_mg, acc_ffn, acc_prep, dsink, dlb)
    return grad_x, small, [r_bh, r_ba, r_out, r_gate, r_up, r_down], (sems, c_in, land)


def kernel(x, c, ctx, c_ctx, w_ada, b_ada, norm_mix_w, norm_ffn_w, w_in, hgrn_lb_logits, hgrn_norm_w, q_norm_w, k_norm_w, attn_sinks, w_branch_hgrn, w_branch_attn, w_out, w_ffn_gate, w_ffn_up, w_ffn_down, loss_target, m_c_ctx, m_w_ada, m_b_ada, m_norm_mix_w, m_norm_ffn_w, m_w_in, m_hgrn_lb_logits, m_hgrn_norm_w, m_q_norm_w, m_k_norm_w, m_attn_sinks, m_w_branch_hgrn, m_w_branch_attn, m_w_out, m_w_ffn_gate, m_w_ffn_up, m_w_ffn_down, v_c_ctx, v_w_ada, v_b_ada, v_norm_mix_w, v_norm_ffn_w, v_w_in, v_hgrn_lb_logits, v_hgrn_norm_w, v_q_norm_w, v_k_norm_w, v_attn_sinks, v_w_branch_hgrn, v_w_branch_attn, v_w_out, v_w_ffn_gate, v_w_ffn_up, v_w_ffn_down):
    me = 4 * lax.axis_index("x") + 2 * lax.axis_index("y") + lax.axis_index("c")
    x2, ctx2, tgt = x[0], ctx[0], loss_target[0]
    w_ada2, w_in2 = w_ada[0], w_in[0]

    cond = jnp.zeros((8, D), F32).at[0].set(c[0]).at[1, :256].set(hgrn_lb_logits.reshape(256))
    b_cols = lax.dynamic_slice(b_ada, (0, me * ADA_BLK), (1, ADA_BLK))
    g0, cc, mod, g_in, h_all, h_lat = _prologue(cond, c_ctx.reshape(1, D), w_ada2, b_cols, w_in2.T.astype(BF16),
                                         x2, ctx2, norm_mix_w)
    lbl = jnp.transpose(g0[:, 1, :256].reshape(N_DEV, 2, 2, 64), (1, 2, 0, 3)).reshape(2, 2, HGW)
    sh1, sc1, gate1, sh2, sc2, gate2 = [mod[k:k + 1] for k in range(6)]
    sh_in = jnp.concatenate([mod[6:7], sh1], axis=0)
    sc_in = jnp.concatenate([mod[7:8], sc1], axis=0)

    shards = [w_branch_hgrn[0].T, w_branch_attn[0].T, w_out[0], w_ffn_gate[0].T, w_ffn_up[0].T, w_ffn_down[0]]
    w_in_t = g_in.reshape(IN_COLS, D)
    w_a, w_b, w_c = [_pick_row_blocks(w_in_t, *_block_table(nm), "order_w_" + nm) for nm in "abc"]

    grad_x, small, (r_bh, r_ba, r_out, r_gate, r_up, r_down), pending_in = _local_step(
        x2, ctx2, h_all, h_lat, tgt, lbl, sh_in, sc_in, gate1, sh2, sc2, gate2, norm_mix_w, norm_ffn_w, hgrn_norm_w,
        q_norm_w, k_norm_w, attn_sinks, w_a, w_b, w_c, *[s.astype(BF16) for s in shards])

    big = {}
    for nm, rr, ww, mm, vv, tr, transposed in (
            ("w_branch_hgrn", r_bh, w_branch_hgrn[0], m_w_branch_hgrn[0], v_w_branch_hgrn[0], 128, True),
            ("w_branch_attn", r_ba, w_branch_attn[0], m_w_branch_attn[0], v_w_branch_attn[0], 128, True),
            ("w_out", r_out, w_out[0], m_w_out[0], v_w_out[0], 128, False),
            ("w_ffn_gate", r_gate, w_ffn_gate[0], m_w_ffn_gate[0], v_w_ffn_gate[0], 176, True),
            ("w_ffn_up", r_up, w_ffn_up[0], m_w_ffn_up[0], v_w_ffn_up[0], 176, True),
            ("w_ffn_down", r_down, w_ffn_down[0], m_w_ffn_down[0], v_w_ffn_down[0], 176, False)):
        if transposed:
            res = _adamw_sharded(rr, ww.T, mm.T, vv.T, "adamw_" + nm, tr)
            big[nm] = [t.T[None] for t in res]
        else:
            big[nm] = [t[None] for t in _adamw_sharded(rr, ww, mm, vv, "adamw_" + nm, tr)]

    (g2,) = _all_gather([small], "gather_small", True)
    tot = _sum_small(g2)
    dm = jnp.zeros((16, 6 * D), F32).at[:8].set(g2[:, R_DMOD:R_DMOD + 6, :].reshape(N_DEV, 6 * D))
    dm = dm.at[8, :2 * D].set(tot[R_DCTX:R_DCTX + 2].reshape(2 * D))
    dm_cols = lax.dynamic_slice(dm, (0, me * ADA_BLK), (16, ADA_BLK))
    g_w_ada, dsc_term = _ada_grads(cc, dm_cols, w_ada2)
    (g3,) = _all_gather([dsc_term], "gather_cctx", True)
    g_c_ctx = _c_ctx_grad(g3, c_ctx.reshape(1, D))
    g_lbl = _lb_grads(tot[R_DLB:R_DLB + 2, :HGW], lbl)
    g_lb_mine = lax.dynamic_slice(g_lbl, (0, 0, me * 64), (2, 2, 64))
    misc = tot[R_MISC]
    loss = misc[M_LOSS]

    rep_out = _adamw_replicated(
        tot, g_c_ctx,
        [b_ada, c_ctx.reshape(1, D), norm_mix_w, norm_ffn_w, hgrn_norm_w, q_norm_w, k_norm_w, attn_sinks],
        [m_b_ada, m_c_ctx.reshape(1, D), m_norm_mix_w, m_norm_ffn_w, m_hgrn_norm_w, m_q_norm_w, m_k_norm_w,
         m_attn_sinks],
        [v_b_ada, v_c_ctx.reshape(1, D), v_norm_mix_w, v_norm_ffn_w, v_hgrn_norm_w, v_q_norm_w, v_k_norm_w,
         v_attn_sinks])
    rep = []
    for kind in range(4):
        vals = dict(zip(_REP_NAMES, rep_out[kind * len(_REP_NAMES):(kind + 1) * len(_REP_NAMES)]))
        vals["c_ctx"] = vals["c_ctx"].reshape(D)
        rep.append(vals)

    sems, c_in, land = pending_in
    d_ada, nm_ada, nv_ada = _adamw_plain(g_w_ada, w_ada2, m_w_ada[0], v_w_ada[0], "adamw_w_ada", tr=256)
    land = _chip_exchange_wait(sems, c_in, land, d_ada)
    own = lax.dynamic_index_in_dim(c_in, 2 * lax.axis_index("x") + lax.axis_index("y"), 0, keepdims=False)
    big["w_in"] = [t.T[None] for t in _adamw_sharded(land, w_in2.T, m_w_in[0].T, v_w_in[0].T, "adamw_w_in", 336,
                                                     extra=own)]
    ada = [t[None] for t in (g_w_ada, d_ada, nm_ada, nv_ada)]
    lb_w = hgrn_lb_logits.reshape(4, 64)
    d_lb, nm_lb, nv_lb = _adamw_plain(g_lb_mine.reshape(4, 64), lb_w, m_hgrn_lb_logits.reshape(4, 64),
                                      v_hgrn_lb_logits.reshape(4, 64), "adamw_lb")
    lbs = [t.reshape(2, 2, 64) for t in (g_lb_mine, d_lb, nm_lb, nv_lb)]

    names = ['c_ctx', 'w_ada', 'b_ada', 'norm_mix_w', 'norm_ffn_w', 'w_in', 'hgrn_lb_logits', 'hgrn_norm_w',
             'q_norm_w', 'k_norm_w', 'attn_sinks', 'w_branch_hgrn', 'w_branch_attn', 'w_out', 'w_ffn_gate',
             'w_ffn_up', 'w_ffn_down']
    outs = [loss, grad_x[None]]
    for kind in range(4):
        for nm in names:
            if nm == 'w_ada':
                outs.append(ada[kind])
            elif nm == 'hgrn_lb_logits':
                outs.append(lbs[kind])
            elif nm in big:
                outs.append(big[nm][kind])
            else:
                outs.append(rep[kind][nm])
    return tuple(outs)
```

```python
import functools
import math

import numpy as np
import jax
import jax.numpy as jnp
from jax import lax
from jax.experimental import pallas as pl
from jax.experimental.pallas import tpu as pltpu

F32 = jnp.float32
BF16 = jnp.bfloat16

N_DEV = 8
D = 1024
S = 2048
L = 256
T = L + S
TM = 256
N_TILES = T // TM
N_LAT_TILES = S // TM
HG_HEADS = 4
HG_DIM = 128
HGW = 512
CHUNK = 32
N_CHUNKS = T // CHUNK
N_CTX_CHUNKS = L // CHUNK
ATT_HEADS = 8
KV_HEADS = 2
HEAD_DIM = 64
ATW = 512
KVW = 128
BLOCK = 128
N_BLOCKS = S // BLOCK
GRID_W = 64
ROPE_THETA = 10000.0
D_FF = 2816
FF_BLK = D_FF // N_DEV
FF_TILE = 256
N_FF_TILES = D_FF // FF_TILE
N_FF_HALVES = 2
IN_COLS = 5376
IN_BLK = IN_COLS // N_DEV
ADA_BLK = 6 * D // N_DEV
EPS = 1e-6
WA, WB, WC = 2048, 1280, 2048

ADAM_LR = 0.001
ADAM_B1 = 0.9
ADAM_B2 = 0.999
ADAM_EPS = 1e-08
ADAM_WD = 0.01
ADAM_STEP = 10

VMEM_LIMIT = 56 * 1024 * 1024
MESH = pl.DeviceIdType.MESH


def _cp(sem=None, vmem=VMEM_LIMIT):
    return pltpu.CompilerParams(dimension_semantics=sem, vmem_limit_bytes=vmem)


def _full(shape):
    n = len(shape)
    return pl.BlockSpec(shape, lambda *_: (0,) * n)


_VMEM_WHOLE = pl.BlockSpec(memory_space=pltpu.VMEM)
_ANY = pl.BlockSpec(memory_space=pl.ANY)


def _sigmoid(v):
    return 1.0 / (1.0 + jnp.exp(-v))


def _dot(a, b):
    return jnp.dot(a, b, preferred_element_type=F32)


def _dot_nt(a, b):
    return lax.dot_general(a, b, (((1,), (1,)), ((), ())), preferred_element_type=F32)


def _dot_tn(a, b):
    return lax.dot_general(a, b, (((0,), (0,)), ((), ())), preferred_element_type=F32)


def _split3(v):
    hi = v.astype(BF16)
    r = v - hi.astype(F32)
    mid = r.astype(BF16)
    lo = (r - mid.astype(F32)).astype(BF16)
    return hi, mid, lo


def _dot_exact_rhs01(v, m01):
    hi, mid, lo = _split3(v)
    return _dot(hi, m01) + _dot(mid, m01) + _dot(lo, m01)


def _split2(v):
    hi = v.astype(BF16)
    return hi, (v - hi.astype(F32)).astype(BF16)


def _dot_lhs01(m01, v):
    hi, lo = _split2(v)
    return _dot(m01, hi) + _dot(m01, lo)


def _dot_f32(a, b, dot=_dot):
    ah, am, al = _split3(a)
    bh, bm, bl = _split3(b)
    return (dot(ah, bh) + (dot(ah, bm) + dot(am, bh))
            + (dot(am, bm) + dot(ah, bl) + dot(al, bh)))


def _my_pos():
    return lax.axis_index("x"), lax.axis_index("y"), lax.axis_index("c")


class _Comm:
    def __init__(self, operands, out_shapes, sems, phases):
        self.operands, self.out_shapes, self.sems, self.phases = operands, out_shapes, sems, phases


def _gather_comm(blocks):
    n = len(blocks)

    def parts(ins, outs, sems):
        send_sems, recv_sems, local_sems = sems
        x, y, c = _my_pos()
        me, sibling = (x, y, c), (x, y, 1 - c)
        chips = [(1 - x, y), (x, 1 - y), (1 - x, 1 - y)]

        def slot(a, px, py, pc):
            return outs[a].at[4 * px + 2 * py + pc]

        def copy(a, k, block, to, src=None):
            return pltpu.make_async_remote_copy(
                src_ref=slot(a, *block) if src is None else src, dst_ref=slot(a, *block),
                send_sem=send_sems.at[a, k], recv_sem=recv_sems.at[a, k],
                device_id=to, device_id_type=MESH)

        mine = [pltpu.make_async_copy(ins[a], slot(a, *me), local_sems.at[a]) for a in range(n)]
        first = []
        for a in range(n):
            first.append(copy(a, 0, me, sibling, src=ins[a]))
            first += [copy(a, 1 + j, me, (*chip, c), src=ins[a]) for j, chip in enumerate(chips)]
        passed = [copy(a, 4 + j, (*chip, c), sibling) for j, chip in enumerate(chips) for a in range(n)]
        return c, me, sibling, chips, copy, mine, first, passed

    def start(ins, outs, sems):
        _, _, _, _, _, mine, first, _ = parts(ins, outs, sems)
        for cp in mine + first:
            cp.start()

    def forward(ins, outs, sems):
        c, me, _, chips, copy, _, _, passed = parts(ins, outs, sems)
        for j, chip in enumerate(chips):
            for a in range(n):
                copy(a, 1 + j, (*chip, c), me).wait_recv()
                passed[j * n + a].start()

    def finish(ins, outs, sems):
        c, me, sibling, chips, copy, mine, first, passed = parts(ins, outs, sems)
        for a in range(n):
            copy(a, 0, sibling, me).wait_recv()
            for j, chip in enumerate(chips):
                copy(a, 4 + j, (*chip, 1 - c), me).wait_recv()
        for cp in first + passed:
            cp.wait_send()
        for cp in mine:
            cp.wait()

    return _Comm(blocks, [jax.ShapeDtypeStruct((N_DEV,) + b.shape, b.dtype) for b in blocks],
                 [pltpu.SemaphoreType.DMA((n, 7)), pltpu.SemaphoreType.DMA((n, 7)), pltpu.SemaphoreType.DMA((n,))],
                 [start, forward, finish])


def _gather_comm_relayed(blocks):
    n = len(blocks)

    def parts(ins, outs, sems):
        send_sems, recv_sems, local_sems = sems
        x, y, c = _my_pos()
        me, sibling = (x, y, c), (x, y, 1 - c)
        x_nbr, y_nbr, diag = (1 - x, y, c), (x, 1 - y, c), (1 - x, 1 - y, c)

        def slot(a, dev, half=None):
            ref = outs[a].at[4 * dev[0] + 2 * dev[1] + dev[2]]
            if half is None:
                return ref
            rows = blocks[a].shape[0] // 2
            return ref.at[pl.ds(half * rows, rows)]

        def copy(a, k, block, to, half=None, src=None):
            return pltpu.make_async_remote_copy(
                src_ref=slot(a, block, half) if src is None else src, dst_ref=slot(a, block, half),
                send_sem=send_sems.at[a, k], recv_sem=recv_sems.at[a, k],
                device_id=to, device_id_type=MESH)

        mine = [pltpu.make_async_copy(ins[a], slot(a, me), local_sems.at[a]) for a in range(n)]
        return me, sibling, x_nbr, y_nbr, diag, copy, mine

    def start(ins, outs, sems):
        me, sibling, x_nbr, y_nbr, _, copy, mine = parts(ins, outs, sems)
        for cp in mine:
            cp.start()
        for a in range(n):
            for k, to in ((1, x_nbr), (2, y_nbr), (0, sibling)):
                copy(a, k, me, to, src=ins[a]).start()

    def forward(ins, outs, sems):
        me, sibling, x_nbr, y_nbr, _, copy, _ = parts(ins, outs, sems)
        for a in range(n):
            copy(a, 1, x_nbr, me).wait_recv()
            copy(a, 3, x_nbr, y_nbr, half=0).start()
            copy(a, 5, x_nbr, sibling).start()
        for a in range(n):
            copy(a, 2, y_nbr, me).wait_recv()
            copy(a, 4, y_nbr, x_nbr, half=1).start()
            copy(a, 6, y_nbr, sibling).start()

    def finish(ins, outs, sems):
        me, sibling, x_nbr, y_nbr, diag, copy, mine = parts(ins, outs, sems)
        sib = lambda dev: (dev[0], dev[1], sibling[2])
        for a in range(n):
            copy(a, 3, diag, me, half=0).wait_recv()
            copy(a, 4, diag, me, half=1).wait_recv()
            copy(a, 7, diag, sibling).start()
        for a in range(n):
            copy(a, 0, sibling, me).wait_recv()
            for k, dev in ((5, x_nbr), (6, y_nbr), (7, diag)):
                copy(a, k, sib(dev), me).wait_recv()
        for a in range(n):
            for k, block, to, half in ((0, me, sibling, None), (1, me, x_nbr, None), (2, me, y_nbr, None),
                                       (3, x_nbr, y_nbr, 0), (4, y_nbr, x_nbr, 1), (5, x_nbr, sibling, None),
                                       (6, y_nbr, sibling, None), (7, diag, sibling, None)):
                copy(a, k, block, to, half=half, src=ins[a] if block is me else None).wait_send()
        for cp in mine:
            cp.wait()

    return _Comm(blocks, [jax.ShapeDtypeStruct((N_DEV,) + b.shape, b.dtype) for b in blocks],
                 [pltpu.SemaphoreType.DMA((n, 8)), pltpu.SemaphoreType.DMA((n, 8)), pltpu.SemaphoreType.DMA((n,))],
                 [start, forward, finish])


_HBM = pl.BlockSpec(memory_space=pltpu.HBM)
_SEM = pl.BlockSpec(memory_space=pltpu.SEMAPHORE)
_SPLIT_COPY = pltpu.CompilerParams(has_side_effects=pltpu.SideEffectType.DATAFLOW_SIDE_EFFECTING)


def _chip_exchange_copies(src_ref, land_ref, sems):
    x, y, c = _my_pos()
    q_me = 2 * x + y
    pairs = []
    for j, (px, py) in enumerate([(1 - x, y), (x, 1 - y), (1 - x, 1 - y)]):
        q = 2 * px + py
        send = pltpu.make_async_remote_copy(
            src_ref=src_ref.at[q], dst_ref=land_ref.at[q_me], send_sem=sems[j], recv_sem=sems[3 + j],
            device_id=(px, py, c), device_id_type=MESH)
        recv = pltpu.make_async_remote_copy(
            src_ref=src_ref.at[q], dst_ref=land_ref.at[q], send_sem=sems[j], recv_sem=sems[3 + j],
            device_id=(x, y, c), device_id_type=MESH)
        pairs.append((send, recv))
    return pairs


def _chip_exchange_start(src, land):
    def body(src_ref, land_ref, *outs):
        sems, token = outs[:6], outs[8]
        for send, _ in _chip_exchange_copies(src_ref, land_ref, sems):
            send.start()
        token[...] = jnp.zeros_like(token)

    res = pl.pallas_call(
        body, name="scatter_in_start",
        out_shape=(pltpu.SemaphoreType.DMA(()),) * 6 + (
            pltpu.HBM(src.shape, src.dtype), pltpu.HBM(land.shape, land.dtype),
            jax.ShapeDtypeStruct((8, 128), F32)),
        in_specs=(_HBM, _HBM), out_specs=(_SEM,) * 6 + (_HBM, _HBM, pl.BlockSpec(memory_space=pltpu.VMEM)),
        input_output_aliases={0: 6, 1: 7}, compiler_params=_SPLIT_COPY,
    )(pltpu.with_memory_space_constraint(src, pltpu.HBM), pltpu.with_memory_space_constraint(land, pltpu.HBM))
    return res[:6], res[6], res[7], res[8]


def _chip_exchange_wait(sems, src_thru, land_thru, after):
    def body(src_ref, land_ref, *rest):
        for send, recv in _chip_exchange_copies(src_ref, land_ref, rest[:6]):
            send.wait_send()
            recv.wait_recv()

    return pl.pallas_call(
        body, name="scatter_in_wait",
        out_shape=(pltpu.HBM(src_thru.shape, src_thru.dtype), pltpu.HBM(land_thru.shape, land_thru.dtype)),
        in_specs=(_HBM, _HBM) + (_SEM,) * 6 + (_ANY,), out_specs=(_HBM, _HBM),
        input_output_aliases={0: 0, 1: 1}, compiler_params=_SPLIT_COPY,
    )(src_thru, land_thru, *sems, after)[1]


def _run_comm(comm, name, in_vmem=False):
    n_in, n_out = len(comm.operands), len(comm.out_shapes)

    def body(*refs):
        ins, outs, sems = refs[:n_in], refs[n_in:n_in + n_out], refs[n_in + n_out:]
        for phase in comm.phases:
            phase(ins, outs, sems)

    spec = _VMEM_WHOLE if in_vmem else _ANY
    return pl.pallas_call(
        body, name=name, out_shape=comm.out_shapes, in_specs=[spec] * n_in, out_specs=[spec] * n_out,
        scratch_shapes=comm.sems,
    )(*comm.operands)


def _carrier_call(body, comm, schedule, *, name, grid, in_specs, out_specs, out_shape, scratch_shapes, operands):
    n_in, n_out, n_scr = len(in_specs), len(out_specs), len(scratch_shapes)
    c_in, c_out = len(comm.operands), len(comm.out_shapes)

    def full_body(*refs):
        ins, refs = refs[:n_in], refs[n_in:]
        cins, refs = refs[:c_in], refs[c_in:]
        outs, refs = refs[:n_out], refs[n_out:]
        couts, refs = refs[:c_out], refs[c_out:]
        scr, csems = refs[:n_scr], refs[n_scr:]
        step = pl.program_id(0)

        def run(before):
            for (at, when_before), phase in zip(schedule, comm.phases):
                if when_before == before:
                    pl.when(step == at)(functools.partial(phase, cins, couts, csems))

        run(True)
        body(*ins, *outs, *scr)
        run(False)

    res = pl.pallas_call(
        full_body, name=name, grid=grid,
        in_specs=list(in_specs) + [_ANY] * c_in, out_specs=list(out_specs) + [_ANY] * c_out,
        out_shape=list(out_shape) + list(comm.out_shapes),
        scratch_shapes=list(scratch_shapes) + list(comm.sems),
        compiler_params=_cp(("arbitrary",)),
    )(*operands, *comm.operands)
    return res[:n_out], res[n_out:]


def _pcall(body, carried, *, name, grid, in_specs, out_specs, out_shape, scratch_shapes, operands):
    if carried is None:
        res = pl.pallas_call(body, name=name, grid=grid, in_specs=in_specs, out_specs=out_specs,
                             out_shape=out_shape, scratch_shapes=scratch_shapes,
                             compiler_params=_cp(("arbitrary",)))(*operands)
        return res, ()
    return _carrier_call(body, carried[0], carried[1], name=name, grid=grid, in_specs=in_specs,
                         out_specs=out_specs, out_shape=out_shape, scratch_shapes=scratch_shapes,
                         operands=operands)


def _all_gather(blocks, name, in_vmem):
    return _run_comm(_gather_comm(blocks), name, in_vmem)


N_CHIPS = 4


def _sibling_comm(contribs):
    n = len(contribs)

    def copies(ins, outs, sems):
        send_sems, recv_sems = sems
        x, y, c = _my_pos()
        return [pltpu.make_async_remote_copy(
            src_ref=ins[a].at[pl.ds(0, N_CHIPS), 1 - c], dst_ref=outs[a],
            send_sem=send_sems.at[a], recv_sem=recv_sems.at[a],
            device_id=(x, y, 1 - c), device_id_type=MESH) for a in range(n)]

    def start(ins, outs, sems):
        for cp in copies(ins, outs, sems):
            cp.start()

    def finish(ins, outs, sems):
        cps = copies(ins, outs, sems)
        for cp in cps:
            cp.wait_recv()
        for cp in cps:
            cp.wait_send()

    return _Comm(contribs, [jax.ShapeDtypeStruct((N_CHIPS,) + b.shape[2:], b.dtype) for b in contribs],
                 [pltpu.SemaphoreType.DMA((n,)), pltpu.SemaphoreType.DMA((n,))], [start, finish])


def _pair_sum(mine, theirs, name):
    _, _, rows, cols = mine.shape
    core = lax.axis_index("c").astype(jnp.int32).reshape(1)

    def body(c_ref, m_ref, t_ref, o_ref):
        o_ref[...] = (m_ref[...].astype(F32) + t_ref[...].astype(F32)).astype(BF16)

    return pl.pallas_call(
        body, name=name,
        grid_spec=pltpu.PrefetchScalarGridSpec(
            num_scalar_prefetch=1, grid=(N_CHIPS,),
            in_specs=[pl.BlockSpec((None, None, rows, cols), lambda q, c: (q, c[0], 0, 0)),
                      pl.BlockSpec((None, rows, cols), lambda q, c: (q, 0, 0))],
            out_specs=pl.BlockSpec((None, rows, cols), lambda q, c: (q, 0, 0))),
        out_shape=jax.ShapeDtypeStruct((N_CHIPS, rows, cols), BF16),
        compiler_params=_cp(("parallel",)),
    )(core, mine, theirs)


def _chip_comm(sums):
    n = len(sums)

    def parts(ins, outs, sems):
        send_sems, recv_sems, local_sems = sems
        x, y, c = _my_pos()
        q_me = 2 * x + y
        chips = [(1 - x, y), (x, 1 - y), (1 - x, 1 - y)]
        mine = [pltpu.make_async_copy(ins[a].at[q_me], outs[a].at[q_me], local_sems.at[a]) for a in range(n)]
        sends, recvs = [], []
        for j, (px, py) in enumerate(chips):
            for a in range(n):
                q = 2 * px + py
                sends.append(pltpu.make_async_remote_copy(
                    src_ref=ins[a].at[q], dst_ref=outs[a].at[q_me],
                    send_sem=send_sems.at[a, j], recv_sem=recv_sems.at[a, j],
                    device_id=(px, py, c), device_id_type=MESH))
                recvs.append(pltpu.make_async_remote_copy(
                    src_ref=ins[a].at[q], dst_ref=outs[a].at[q],
                    send_sem=send_sems.at[a, j], recv_sem=recv_sems.at[a, j],
                    device_id=(x, y, c), device_id_type=MESH))
        return mine, sends, recvs

    def start(ins, outs, sems):
        mine, sends, _ = parts(ins, outs, sems)
        for cp in mine + sends:
            cp.start()

    def finish(ins, outs, sems):
        mine, sends, recvs = parts(ins, outs, sems)
        for cp in recvs:
            cp.wait_recv()
        for cp in sends:
            cp.wait_send()
        for cp in mine:
            cp.wait()

    return _Comm(sums, [jax.ShapeDtypeStruct(b.shape, b.dtype) for b in sums],
                 [pltpu.SemaphoreType.DMA((n, 3)), pltpu.SemaphoreType.DMA((n, 3)), pltpu.SemaphoreType.DMA((n,))],
                 [start, finish])


def _mm_nt(a, bt, *, tm, tn, out_dtype, name, row_off=0, rows=None):
    rows = a.shape[0] if rows is None else rows
    n, k = bt.shape

    def body(a_ref, b_ref, o_ref):
        o_ref[...] = _dot_nt(a_ref[...], b_ref[...]).astype(out_dtype)

    return pl.pallas_call(
        body, name=name, grid=(rows // tm, n // tn),
        in_specs=[pl.BlockSpec((tm, k), lambda i, j: (i + row_off, 0)),
                  pl.BlockSpec((tn, k), lambda i, j: (j, 0))],
        out_specs=pl.BlockSpec((tm, tn), lambda i, j: (i, j)),
        out_shape=jax.ShapeDtypeStruct((rows, n), out_dtype),
        compiler_params=_cp(("parallel", "parallel")),
    )(a, bt)


def _mm_tn(a, b, *, tk, nk, tm, tn, out_dtype, name, a_off=0, b_off=0):
    m, n = a.shape[1], b.shape[1]

    def body(a_ref, b_ref, o_ref, acc):
        kk = pl.program_id(2)

        @pl.when(kk == 0)
        def _():
            acc[...] = jnp.zeros_like(acc)

        acc[...] += _dot_tn(a_ref[...], b_ref[...])

        @pl.when(kk == nk - 1)
        def _():
            o_ref[...] = acc[...].astype(out_dtype)

    return pl.pallas_call(
        body, name=name, grid=(m // tm, n // tn, nk),
        in_specs=[pl.BlockSpec((tk, tm), lambda i, j, kk: (kk + a_off, i)),
                  pl.BlockSpec((tk, tn), lambda i, j, kk: (kk + b_off, j))],
        out_specs=pl.BlockSpec((tm, tn), lambda i, j, kk: (i, j)),
        out_shape=jax.ShapeDtypeStruct((m, n), out_dtype),
        scratch_shapes=[pltpu.VMEM((tm, tn), F32)],
        compiler_params=_cp(("parallel", "parallel", "arbitrary")),
    )(a, b)


def _mm_tn_blocked(a, b, name, carried=None):
    w = a.shape[1] // N_FF_HALVES
    n = b.shape[1]

    def body(a_ref, b_ref, o_ref):
        o_ref[...] = _dot_tn(a_ref[...], b_ref[...]).astype(BF16)

    (out,), extra = _pcall(
        body, carried, name=name, grid=(N_FF_HALVES,),
        in_specs=[pl.BlockSpec((S, w), lambda j: (0, j)), _full((S, n))],
        out_specs=[pl.BlockSpec((w, n), lambda j: (j, 0))],
        out_shape=[jax.ShapeDtypeStruct((a.shape[1], n), BF16)],
        scratch_shapes=[], operands=[a, b])
    return out, extra


def _prologue(cond, c_ctx, w_ada, b_cols, w_in_t, x, ctx, nw):
    rows_shape = jax.ShapeDtypeStruct((16, ADA_BLK), F32)
    big, g_cond, g_mod = _gather_comm_relayed([w_in_t]), _gather_comm([cond]), _gather_comm([rows_shape])

    def body(cond_ref, cctx_ref, wada_ref, b_ref, nw_ref, win_ref, x_ref, ctx_ref,
             g0_ref, cc_ref, mod_ref, gin_ref, h_ref, hl_ref, rows_ref, g1_ref, x_s, ctx_s, h_s, io_sems, *sems):
        s_big, s_cond, s_mod = sems[0:3], sems[3:6], sems[6:9]
        big.phases[0]([win_ref], [gin_ref], s_big)
        load_x = pltpu.make_async_copy(x_ref, x_s, io_sems.at[0])
        load_ctx = pltpu.make_async_copy(ctx_ref, ctx_s, io_sems.at[1])
        load_x.start()
        load_ctx.start()
        for phase in g_cond.phases:
            phase([cond_ref], [g0_ref], s_cond)
        cc_ref[...] = jnp.zeros_like(cc_ref)
        for j in range(N_DEV):
            cc_ref[j:j + 1, :] = g0_ref[j, 0:1, :]
        cc_ref[N_DEV:N_DEV + 1, :] = cctx_ref[...]
        cv = cc_ref[...]
        rows_ref[...] = _dot_f32(cv * _sigmoid(cv), wada_ref[...]) + b_ref[...]
        for phase in g_mod.phases:
            phase([rows_ref], [g1_ref], s_mod)
        x_pos, y_pos, c_pos = _my_pos()
        me = 4 * x_pos + 2 * y_pos + c_pos
        mine = jnp.concatenate([g1_ref[j, pl.ds(me, 1), :] for j in range(N_DEV)], axis=1)
        shared = jnp.concatenate([g1_ref[j, N_DEV:N_DEV + 1, :] for j in range(N_DEV)], axis=1)
        for k in range(6):
            mod_ref[k:k + 1, :] = mine[:, k * D:(k + 1) * D]
        mod_ref[6:7, :] = shared[:, 0:D]
        mod_ref[7:8, :] = shared[:, D:2 * D]
        load_ctx.wait()
        load_x.wait()
        h_s[pl.ds(0, L), :] = _rms_mod(ctx_s[...], nw_ref[...], mod_ref[6:7, :], mod_ref[7:8, :]).astype(BF16)

        def norm_tile(i, carry):
            r0 = pl.multiple_of(i * TM, TM)
            h_s[pl.ds(L + r0, TM), :] = _rms_mod(
                x_s[pl.ds(r0, TM), :], nw_ref[...], mod_ref[0:1, :], mod_ref[1:2, :]).astype(BF16)
            return carry

        lax.fori_loop(0, N_LAT_TILES, norm_tile, 0)
        stores = [pltpu.make_async_copy(h_s, h_ref, io_sems.at[2]),
                  pltpu.make_async_copy(h_s.at[pl.ds(L, S)], hl_ref, io_sems.at[3])]
        for cp in stores:
            cp.start()
        big.phases[1]([win_ref], [gin_ref], s_big)
        big.phases[2]([win_ref], [gin_ref], s_big)
        for cp in stores:
            cp.wait()

    return pl.pallas_call(
        body, name="prologue",
        in_specs=[_VMEM_WHOLE] * 5 + [_ANY] * 3, out_specs=[_VMEM_WHOLE] * 3 + [_ANY] * 3,
        out_shape=[g_cond.out_shapes[0], jax.ShapeDtypeStruct((16, D), F32), jax.ShapeDtypeStruct((8, D), F32),
                   big.out_shapes[0], jax.ShapeDtypeStruct((T, D), BF16), jax.ShapeDtypeStruct((S, D), BF16)],
        scratch_shapes=[pltpu.VMEM((16, ADA_BLK), F32), pltpu.VMEM((N_DEV, 16, ADA_BLK), F32),
                        pltpu.VMEM((S, D), F32), pltpu.VMEM((L, D), F32), pltpu.VMEM((T, D), BF16),
                        pltpu.SemaphoreType.DMA((4,))] + big.sems + g_cond.sems + g_mod.sems,
        compiler_params=_cp(),
    )(cond, c_ctx, w_ada, b_cols, nw, w_in_t, x, ctx)


def _ada_grads(cc, dm_cols, w_ada):
    def body(c_ref, dm_ref, w_ref, gw_ref, dsc_ref):
        cv = c_ref[...]
        sc = cv * _sigmoid(cv)
        dm = dm_ref[...]
        gw_ref[...] = _dot_f32(sc, dm, dot=_dot_tn)
        dsc_ref[...] = _dot_f32(dm, w_ref[...], dot=_dot_nt)

    return pl.pallas_call(
        body, name="ada_grads",
        in_specs=[_VMEM_WHOLE] * 3, out_specs=[_VMEM_WHOLE] * 2,
        out_shape=[jax.ShapeDtypeStruct((D, ADA_BLK), F32), jax.ShapeDtypeStruct((16, D), F32)],
        compiler_params=_cp(),
    )(cc, dm_cols, w_ada)


def _lat(i):
    return jnp.maximum(i - 1, 0)


def _rms_mod(xv, nw, sh, sc):
    rstd = lax.rsqrt(jnp.mean(xv * xv, axis=-1, keepdims=True) + EPS)
    return (xv * rstd * nw) * (1.0 + sc) + sh


def _rms_mod_bwd(xv, nw, sc, dh):
    rstd = lax.rsqrt(jnp.mean(xv * xv, axis=-1, keepdims=True) + EPS)
    xhat = xv * rstd
    dn = dh * (1.0 + sc)
    dxhat = dn * nw
    dx = rstd * (dxhat - xhat * jnp.mean(dxhat * xhat, axis=-1, keepdims=True))
    return (dx, jnp.sum(dh, axis=0, keepdims=True), jnp.sum(dh * (xhat * nw), axis=0, keepdims=True),
            jnp.sum(dn * xhat, axis=0, keepdims=True))


def _chunk_masks(reverse):
    row = lax.broadcasted_iota(jnp.int32, (TM, TM), 0)
    col = lax.broadcasted_iota(jnp.int32, (TM, TM), 1)
    same = (row // CHUNK) == (col // CHUNK)
    tri = same & ((col >= row) if reverse else (col <= row))
    return same, tri


def _chunk_order(i, reverse):
    if not reverse:
        return i
    return jnp.where(i < N_CTX_CHUNKS, N_CTX_CHUNKS - 1 - i, N_CHUNKS + N_CTX_CHUNKS - 1 - i)


def _decay_terms(z, lb, same01, tri01):
    f = lb + (1.0 - lb) * _sigmoid(z)
    g = jnp.log(f)
    g2 = jnp.concatenate(_split2(g), axis=1)
    b2 = _dot(tri01, g2)
    t2 = _dot(same01, g2)
    return f, 1.0 - f, b2[:, :HG_DIM] + b2[:, HG_DIM:], t2[:, :HG_DIM] + t2[:, HG_DIM:]


def _chunk_outer(a, b):
    n = TM // CHUNK
    return jnp.einsum('ncv,nck->nvk', a.reshape(n, CHUNK, HG_DIM), b.reshape(n, CHUNK, HG_DIM),
                      preferred_element_type=F32)


def _hgrn_fwd(p_a, lbl, carried=None):
    cpt = TM // CHUNK

    def body(p_ref, lbl_ref, o_ref, st_ref, qd_s, kd_s, u_s, v_s, ebt_s):
        masks = [_chunk_masks(d == 1) for d in (0, 1)]
        same01 = jnp.where(masks[0][0], 1.0, 0.0).astype(BF16)
        tri = [m[1] for m in masks]
        tri01 = [jnp.where(t, 1.0, 0.0).astype(BF16) for t in tri]
        lb = [_sigmoid(lbl_ref[d][0:1, :] - lbl_ref[d][1:2, :]) for d in (0, 1)]

        def prep(r, carry):
            r0 = pl.multiple_of(r * TM, TM)
            vb = p_ref[pl.ds(r0, TM), 2 * HG_DIM:3 * HG_DIM].astype(BF16)
            v_s[pl.ds(r0, TM), :] = vb
            for d in (0, 1):
                z = p_ref[pl.ds(r0, TM), d * HG_DIM:(d + 1) * HG_DIM]
                _, k, b, bt = _decay_terms(z, lb[d], same01, tri01[d])
                u_s[d, pl.ds(r * cpt, cpt)] = _chunk_outer(vb, (k * jnp.exp(bt - b)).astype(BF16))
                ebt_s[d, pl.ds(r0, TM), :] = jnp.exp(bt)

                @pl.when(r >= 1)
                def _():
                    rl = pl.multiple_of(r0 - L, TM)
                    qr = p_ref[pl.ds(r0, TM), 3 * HG_DIM:4 * HG_DIM]
                    q = qr * _sigmoid(qr) * HG_DIM ** -0.5
                    qd_s[d, pl.ds(rl, TM), :] = (q * jnp.exp(b)).astype(BF16)
                    kd_s[d, pl.ds(rl, TM), :] = (k * jnp.exp(-b)).astype(BF16)

            return carry

        lax.fori_loop(0, N_TILES, prep, 0)

        def scan(i, sts):
            new = []
            for d in (0, 1):
                nn = _chunk_order(i, d == 1)
                c0 = pl.multiple_of(nn * CHUNK, CHUNK)
                st_ref[d, nn] = sts[d].astype(BF16)
                new.append(sts[d] * ebt_s[d, pl.ds(c0, 1), :] + u_s[d, nn])
            return tuple(new)

        zero = jnp.zeros((HG_DIM, HG_DIM), F32)
        lax.fori_loop(0, N_CHUNKS, scan, (zero, zero))

        def outp(r, carry):
            r0 = pl.multiple_of(r * TM, TM)
            vb = v_s[pl.ds(r0 + L, TM), :]
            o = jnp.zeros((TM, HG_DIM), F32)
            for d in (0, 1):
                qd = qd_s[d, pl.ds(r0, TM), :]
                a = jnp.where(tri[d], _dot_nt(qd, kd_s[d, pl.ds(r0, TM), :]), 0.0)
                stb = st_ref[d, pl.ds(N_CTX_CHUNKS + r * cpt, cpt)]
                inter = jnp.einsum('nck,nvk->ncv', qd.reshape(cpt, CHUNK, HG_DIM), stb,
                                   preferred_element_type=F32)
                o = o + _dot(a.astype(BF16), vb) + inter.reshape(TM, HG_DIM)
            o_ref[pl.ds(r0, TM), :] = o
            return carry

        lax.fori_loop(0, N_LAT_TILES, outp, 0, unroll=2)

    return _pcall(
        body, carried, name="hgrn_fwd", grid=(HG_HEADS,),
        in_specs=[pl.BlockSpec((T, 4 * HG_DIM), lambda h: (0, h)),
                  pl.BlockSpec((2, 2, HG_DIM), lambda h: (0, 0, h))],
        out_specs=[pl.BlockSpec((S, HG_DIM), lambda h: (0, h)),
                   pl.BlockSpec((2, None, N_CHUNKS, HG_DIM, HG_DIM), lambda h: (0, h, 0, 0, 0))],
        out_shape=[jax.ShapeDtypeStruct((S, HGW), F32),
                   jax.ShapeDtypeStruct((2, HG_HEADS, N_CHUNKS, HG_DIM, HG_DIM), BF16)],
        scratch_shapes=[pltpu.VMEM((2, S, HG_DIM), BF16), pltpu.VMEM((2, S, HG_DIM), BF16),
                        pltpu.VMEM((2, N_CHUNKS, HG_DIM, HG_DIM), F32), pltpu.VMEM((T, HG_DIM), BF16),
                        pltpu.VMEM((2, T, HG_DIM), F32)],
        operands=[p_a, lbl])


def _hgrn_bwd(p_a, lbl, d_o, st, carried=None):
    cpt = TM // CHUNK

    def rows(r):
        return r * TM if isinstance(r, int) else pl.multiple_of(r * TM, TM)

    def body(p_ref, lbl_ref, do_ref, st_ref, dp_ref, dlb_ref, b_s, bt_s, dbt_s, qd_s, dst_s, w_s):
        masks = [_chunk_masks(d == 1) for d in (0, 1)]
        same01 = jnp.where(masks[0][0], 1.0, 0.0).astype(BF16)
        tri = [m[1] for m in masks]
        tri01 = [jnp.where(t, 1.0, 0.0).astype(BF16) for t in tri]
        later01 = [tri01[1], tri01[0]]
        lb = [_sigmoid(lbl_ref[d][0:1, :] - lbl_ref[d][1:2, :]) for d in (0, 1)]

        def prep_tile(r, latent):
            r0 = rows(r)
            for d in (0, 1):
                z = p_ref[pl.ds(r0, TM), d * HG_DIM:(d + 1) * HG_DIM]
                _, _, b, bt = _decay_terms(z, lb[d], same01, tri01[d])
                b_s[d, pl.ds(r0, TM), :] = b
                bt_s[d, pl.ds(r0, TM), :] = bt
                if latent:
                    rl = pl.multiple_of(r0 - L, TM)
                    qr = p_ref[pl.ds(r0, TM), 3 * HG_DIM:4 * HG_DIM]
                    qd = (qr * _sigmoid(qr) * HG_DIM ** -0.5 * jnp.exp(b)).astype(BF16)
                    qd_s[d, pl.ds(rl, TM), :] = qd
                    w_s[d, pl.ds(r * cpt, cpt)] = _chunk_outer(
                        do_ref[pl.ds(rl, TM), :].astype(BF16), qd).astype(BF16)

        prep_tile(0, False)
        w_s[:, pl.ds(0, N_CTX_CHUNKS)] = jnp.zeros((2, N_CTX_CHUNKS, HG_DIM, HG_DIM), BF16)

        def prep(r, carry):
            prep_tile(r, True)
            return carry

        lax.fori_loop(1, N_TILES, prep, 0, unroll=2)

        def rscan(j, dsts):
            i = N_CHUNKS - 1 - j
            new = []
            for d in (0, 1):
                nn = _chunk_order(i, d == 1)
                c0 = pl.multiple_of(nn * CHUNK, CHUNK)
                dst_s[d, nn] = dsts[d].astype(BF16)
                after = st_ref[d, _chunk_order(jnp.minimum(i + 1, N_CHUNKS - 1), d == 1)].astype(F32)
                dbt_s[d, pl.ds(c0, CHUNK), :] = jnp.broadcast_to(
                    jnp.sum(after * dsts[d], axis=0, keepdims=True), (CHUNK, HG_DIM))
                new.append(dsts[d] * jnp.exp(bt_s[d, pl.ds(c0, 1), :]) + w_s[d, nn].astype(F32))
            return tuple(new)

        zero = jnp.zeros((HG_DIM, HG_DIM), F32)
        lax.fori_loop(0, N_CHUNKS, rscan, (zero, zero))

        def grad_tile(r, latent):
            r0 = rows(r)
            vb = p_ref[pl.ds(r0, TM), 2 * HG_DIM:3 * HG_DIM].astype(BF16)
            dv = jnp.zeros((TM, HG_DIM), F32)
            dq = jnp.zeros((TM, HG_DIM), F32)
            dlbs = []
            if latent:
                rl = pl.multiple_of(r0 - L, TM)
                qr = p_ref[pl.ds(r0, TM), 3 * HG_DIM:4 * HG_DIM]
                sq = _sigmoid(qr)
                do = do_ref[pl.ds(rl, TM), :].astype(BF16)
                da_full = _dot_nt(do, vb)
            for d in (0, 1):
                z = p_ref[pl.ds(r0, TM), d * HG_DIM:(d + 1) * HG_DIM]
                sz = _sigmoid(z)
                f = lb[d] + (1.0 - lb[d]) * sz
                k = 1.0 - f
                b = b_s[d, pl.ds(r0, TM), :]
                e2 = jnp.exp(bt_s[d, pl.ds(r0, TM), :] - b)
                dstb = dst_s[d, pl.ds(r * cpt, cpt)]
                kd2 = k * e2
                dkd2 = jnp.einsum('ncv,nvk->nck', vb.reshape(cpt, CHUNK, HG_DIM), dstb,
                                  preferred_element_type=F32).reshape(TM, HG_DIM)
                dv = dv + jnp.einsum('nck,nvk->ncv', kd2.astype(BF16).reshape(cpt, CHUNK, HG_DIM), dstb,
                                     preferred_element_type=F32).reshape(TM, HG_DIM)
                dk = dkd2 * e2
                db = -(kd2 * dkd2)
                if latent:
                    eb = jnp.exp(b)
                    enb = jnp.exp(-b)
                    qdf = qr * sq * HG_DIM ** -0.5 * eb
                    kdf = k * enb
                    qd = qd_s[d, pl.ds(rl, TM), :]
                    kd = kdf.astype(BF16)
                    a = jnp.where(tri[d], _dot_nt(qd, kd), 0.0).astype(BF16)
                    da = jnp.where(tri[d], da_full, 0.0).astype(BF16)
                    stb = st_ref[d, pl.ds(r * cpt, cpt)]
                    dqd = _dot(da, kd) + jnp.einsum(
                        'ncv,nvk->nck', do.reshape(cpt, CHUNK, HG_DIM), stb,
                        preferred_element_type=F32).reshape(TM, HG_DIM)
                    dkd = _dot_tn(da, qd)
                    dv = dv + _dot_tn(a, do)
                    dk = dk + dkd * enb
                    db = db + qdf * dqd - kdf * dkd
                    dq = dq + dqd * eb
                dg = _dot_lhs01(later01[d], db) + dbt_s[d, pl.ds(r0, TM), :]
                df = dg / f - dk
                dp_ref[pl.ds(r0, TM), d * HG_DIM:(d + 1) * HG_DIM] = (
                    df * (1.0 - lb[d]) * sz * (1.0 - sz)).astype(BF16)
                dlbs.append(jnp.sum(df * (1.0 - sz), axis=0, keepdims=True))
            dp_ref[pl.ds(r0, TM), 2 * HG_DIM:3 * HG_DIM] = dv.astype(BF16)
            if latent:
                dq = dq * (HG_DIM ** -0.5) * (sq * (1.0 + qr * (1.0 - sq)))
            dp_ref[pl.ds(r0, TM), 3 * HG_DIM:4 * HG_DIM] = dq.astype(BF16)
            return dlbs

        dlb_ctx = grad_tile(0, False)

        def grads(r, acc):
            t = grad_tile(r, True)
            return (acc[0] + t[0], acc[1] + t[1])

        dlb = lax.fori_loop(1, N_TILES, grads, (dlb_ctx[0], dlb_ctx[1]))
        dlb_ref[0:1, :] = dlb[0]
        dlb_ref[1:2, :] = dlb[1]

    return _pcall(
        body, carried, name="hgrn_bwd", grid=(HG_HEADS,),
        in_specs=[pl.BlockSpec((T, 4 * HG_DIM), lambda h: (0, h)),
                  pl.BlockSpec((2, 2, HG_DIM), lambda h: (0, 0, h)),
                  pl.BlockSpec((S, HG_DIM), lambda h: (0, h)),
                  pl.BlockSpec((2, None, N_CHUNKS, HG_DIM, HG_DIM), lambda h: (0, h, 0, 0, 0))],
        out_specs=[pl.BlockSpec((T, 4 * HG_DIM), lambda h: (0, h)),
                   pl.BlockSpec((2, HG_DIM), lambda h: (0, h))],
        out_shape=[jax.ShapeDtypeStruct((T, WA), BF16), jax.ShapeDtypeStruct((2, HGW), F32)],
        scratch_shapes=[pltpu.VMEM((2, T, HG_DIM), F32), pltpu.VMEM((2, T, HG_DIM), F32),
                        pltpu.VMEM((2, T, HG_DIM), F32), pltpu.VMEM((2, S, HG_DIM), BF16),
                        pltpu.VMEM((2, N_CHUNKS, HG_DIM, HG_DIM), BF16),
                        pltpu.VMEM((2, N_CHUNKS, HG_DIM, HG_DIM), BF16)],
        operands=[p_a, lbl, d_o, st])


def _rope_tables():
    t = np.arange(S)
    inv = ROPE_THETA ** (-np.arange(0, 32, 2, dtype=np.float64) / 32)
    lane = np.arange(64)
    pos = np.where(lane[None, :] < 32, (t // GRID_W)[:, None], (t % GRID_W)[:, None]).astype(np.float64)
    ang = pos * inv[(lane % 32) % 16][None, :]
    sign = np.where((lane % 32) < 16, -1.0, 1.0)[None, :]
    cos = np.tile(np.cos(ang), (1, 2)).astype(np.float32)
    sin = np.tile(np.sin(ang) * sign, (1, 2)).astype(np.float32)
    return jnp.asarray(cos), jnp.asarray(sin)


def _rope_partner(v):
    lane = lax.broadcasted_iota(jnp.int32, (1, 128), 1)
    first = (lane % 32) < 16
    slabs = []
    for j in range(v.shape[1] // 128):
        s = v[:, 128 * j:128 * (j + 1)]
        slabs.append(jnp.where(first, pltpu.roll(s, 112, 1), pltpu.roll(s, 16, 1)))
    return slabs[0] if len(slabs) == 1 else jnp.concatenate(slabs, axis=1)


def _group_ones(width, group):
    r = lax.broadcasted_iota(jnp.int32, (width, width), 0)
    c = lax.broadcasted_iota(jnp.int32, (width, width), 1)
    return jnp.where((r // group) == (c // group), 1.0, 0.0).astype(BF16)


def _group_mean(v, ones01, group):
    hi = v.astype(BF16)
    lo = (v - hi.astype(F32)).astype(BF16)
    return (_dot(hi, ones01) + _dot(lo, ones01)) * (1.0 / group)


def _rep_matrix():
    r = lax.broadcasted_iota(jnp.int32, (KVW, ATW), 0)
    c = lax.broadcasted_iota(jnp.int32, (KVW, ATW), 1)
    return jnp.where(r == HEAD_DIM * (c // 256) + c % HEAD_DIM, 1.0, 0.0).astype(BF16)


def _tile_lanes(v, reps):
    return jnp.concatenate([v] * reps, axis=1)


def _prep_fwd(p_b, o, cos, sin, hnw, qnw, knw):
    def body(p_ref, o_ref, cos_ref, sin_ref, hnw_ref, qnw_ref, knw_ref, y_ref, q_ref, k_ref, v_ref):
        i = pl.program_id(0)
        rep = _rep_matrix()
        ones_k = _group_ones(KVW, HEAD_DIM)
        kr = p_ref[:, 1024:1152]
        krstd = lax.rsqrt(_group_mean(kr * kr, ones_k, HEAD_DIM) + EPS)
        kn = kr * krstd * knw_ref[...]
        v_ref[...] = _dot(p_ref[:, 1152:1280].astype(BF16), rep).astype(BF16)

        @pl.when(i == 0)
        def _():
            k_ref[...] = _dot(kn.astype(BF16), rep).astype(BF16)

        @pl.when(i > 0)
        def _():
            cs, sn = cos_ref[...], sin_ref[...]
            kro = kn * cs + _rope_partner(kn) * sn
            k_ref[...] = _dot(kro.astype(BF16), rep).astype(BF16)
            qr = p_ref[:, 512:1024]
            qrstd = lax.rsqrt(_group_mean(qr * qr, _group_ones(ATW, HEAD_DIM), HEAD_DIM) + EPS)
            qn = qr * qrstd * qnw_ref[...]
            qro = qn * _tile_lanes(cs, 4) + _rope_partner(qn) * _tile_lanes(sn, 4)
            q_ref[...] = (qro * HEAD_DIM ** -0.5).astype(BF16)
            ys = []
            for h in range(HG_HEADS):
                oh = o_ref[:, HG_DIM * h:HG_DIM * (h + 1)]
                gh = p_ref[:, HG_DIM * h:HG_DIM * (h + 1)]
                rstd = lax.rsqrt(jnp.mean(oh * oh, axis=-1, keepdims=True) + EPS)
                ys.append(oh * rstd * hnw_ref[...] * (gh * _sigmoid(gh)))
            y_ref[...] = jnp.concatenate(ys, axis=1).astype(BF16)

    return pl.pallas_call(
        body, name="prep_fwd", grid=(N_TILES,),
        in_specs=[pl.BlockSpec((TM, WB), lambda i: (i, 0)),
                  pl.BlockSpec((TM, HGW), lambda i: (_lat(i), 0)),
                  pl.BlockSpec((TM, 128), lambda i: (_lat(i), 0)),
                  pl.BlockSpec((TM, 128), lambda i: (_lat(i), 0)),
                  _full((1, HG_DIM)), _full((1, ATW)), _full((1, KVW))],
        out_specs=[pl.BlockSpec((TM, HGW), lambda i: (_lat(i), 0)),
                   pl.BlockSpec((TM, ATW), lambda i: (_lat(i), 0)),
                   pl.BlockSpec((TM, ATW), lambda i: (i, 0)),
                   pl.BlockSpec((TM, ATW), lambda i: (i, 0))],
        out_shape=[jax.ShapeDtypeStruct((S, HGW), BF16), jax.ShapeDtypeStruct((S, ATW), BF16),
                   jax.ShapeDtypeStruct((T, ATW), BF16), jax.ShapeDtypeStruct((T, ATW), BF16)],
        compiler_params=_cp(("arbitrary",)),
    )(p_b, o, cos, sin, hnw, qnw, knw)


def _prep_bwd(p_b, o, cos, sin, hnw, qnw, knw, dy_hg, dq, dk_rep, dv_rep, carried=None):
    def body(p_ref, o_ref, cos_ref, sin_ref, hnw_ref, qnw_ref, knw_ref, dy_ref, dq_ref, dk_ref, dv_ref,
             dp_ref, do_ref, acc_ref):
        i = pl.program_id(0)

        @pl.when(i == 0)
        def _():
            acc_ref[...] = jnp.zeros_like(acc_ref)

        rep = _rep_matrix()
        ones_k = _group_ones(KVW, HEAD_DIM)

        def fold(v):
            hi = v.astype(BF16)
            lo = (v - hi.astype(F32)).astype(BF16)
            return _dot_nt(hi, rep) + _dot_nt(lo, rep)

        kr = p_ref[:, 1024:1152]
        krstd = lax.rsqrt(_group_mean(kr * kr, ones_k, HEAD_DIM) + EPS)
        khat = kr * krstd
        kw = knw_ref[...]
        dkro = fold(dk_ref[...])
        dv = fold(dv_ref[...])

        def k_back(dkn):
            dkhat = dkn * kw
            dkr = krstd * (dkhat - khat * _group_mean(dkhat * khat, ones_k, HEAD_DIM))
            acc_ref[2:3, 0:KVW] += jnp.sum(dkn * khat, axis=0, keepdims=True)
            dp_ref[:, 1024:1152] = dkr.astype(BF16)
            dp_ref[:, 1152:1280] = dv.astype(BF16)

        @pl.when(i == 0)
        def _():
            k_back(dkro)
            dp_ref[:, 0:1024] = jnp.zeros((TM, 1024), BF16)

        @pl.when(i > 0)
        def _():
            cs, sn = cos_ref[...], sin_ref[...]
            k_back(dkro * cs + _rope_partner(dkro * sn))
            ones_q = _group_ones(ATW, HEAD_DIM)
            qr = p_ref[:, 512:1024]
            qrstd = lax.rsqrt(_group_mean(qr * qr, ones_q, HEAD_DIM) + EPS)
            qhat = qr * qrstd
            dqro = dq_ref[...] * HEAD_DIM ** -0.5
            dqn = dqro * _tile_lanes(cs, 4) + _rope_partner(dqro * _tile_lanes(sn, 4))
            dqhat = dqn * qnw_ref[...]
            dqr = qrstd * (dqhat - qhat * _group_mean(dqhat * qhat, ones_q, HEAD_DIM))
            acc_ref[1:2, :] += jnp.sum(dqn * qhat, axis=0, keepdims=True)
            dp_ref[:, 512:1024] = dqr.astype(BF16)
            dws = jnp.zeros((1, HG_DIM), F32)
            for h in range(HG_HEADS):
                sl = slice(HG_DIM * h, HG_DIM * (h + 1))
                oh, gh, dy = o_ref[:, sl], p_ref[:, sl], dy_ref[:, sl]
                rstd = lax.rsqrt(jnp.mean(oh * oh, axis=-1, keepdims=True) + EPS)
                ohat = oh * rstd
                sg = _sigmoid(gh)
                dp_ref[:, sl] = (dy * (ohat * hnw_ref[...]) * (sg * (1.0 + gh * (1.0 - sg)))).astype(BF16)
                dn = dy * (gh * sg)
                dws = dws + jnp.sum(dn * ohat, axis=0, keepdims=True)
                dohat = dn * hnw_ref[...]
                do_ref[:, sl] = rstd * (dohat - ohat * jnp.mean(dohat * ohat, axis=-1, keepdims=True))
            acc_ref[0:1, 0:HG_DIM] += dws

    return _pcall(
        body, carried, name="prep_bwd", grid=(N_TILES,),
        in_specs=[pl.BlockSpec((TM, WB), lambda i: (i, 0)),
                  pl.BlockSpec((TM, HGW), lambda i: (_lat(i), 0)),
                  pl.BlockSpec((TM, 128), lambda i: (_lat(i), 0)),
                  pl.BlockSpec((TM, 128), lambda i: (_lat(i), 0)),
                  _full((1, HG_DIM)), _full((1, ATW)), _full((1, KVW)),
                  pl.BlockSpec((TM, HGW), lambda i: (_lat(i), 0)),
                  pl.BlockSpec((TM, ATW), lambda i: (_lat(i), 0)),
                  pl.BlockSpec((TM, ATW), lambda i: (i, 0)),
                  pl.BlockSpec((TM, ATW), lambda i: (i, 0))],
        out_specs=[pl.BlockSpec((TM, WB), lambda i: (i, 0)),
                   pl.BlockSpec((TM, HGW), lambda i: (_lat(i), 0)),
                   _full((8, ATW))],
        out_shape=[jax.ShapeDtypeStruct((T, WB), BF16), jax.ShapeDtypeStruct((S, HGW), F32),
                   jax.ShapeDtypeStruct((8, ATW), F32)],
        scratch_shapes=[], operands=[p_b, o, cos, sin, hnw, qnw, knw, dy_hg, dq, dk_rep, dv_rep])


NEG = -1e30
_CTX_BLOCKS = L // BLOCK


def _attn_window_specs():
    prev = pl.BlockSpec((BLOCK, ATW), lambda i: (jnp.maximum(i - 1, 0) + _CTX_BLOCKS, 0))
    own = pl.BlockSpec((BLOCK, ATW), lambda i: (i + _CTX_BLOCKS, 0))
    nxt = pl.BlockSpec((BLOCK, ATW), lambda i: (jnp.minimum(i + 1, N_BLOCKS - 1) + _CTX_BLOCKS, 0))
    return [prev, own, nxt, _full((L, ATW))]


def _attn_valid(i, heads, context):
    n_keys = 3 * BLOCK + (L if context else 0)
    qi = lax.broadcasted_iota(jnp.int32, (heads * BLOCK, n_keys), 0) % BLOCK
    kj = lax.broadcasted_iota(jnp.int32, (heads * BLOCK, n_keys), 1)
    window = ((jnp.abs(kj - BLOCK - qi) <= BLOCK) & ((kj >= BLOCK) | (i > 0))
              & ((kj < 2 * BLOCK) | (i < N_BLOCKS - 1)))
    return window | (kj >= 3 * BLOCK)


def _stack_heads(qg):
    lane = lax.broadcasted_iota(jnp.int32, (1, 256), 1) // HEAD_DIM
    return jnp.concatenate([jnp.where(lane == g, qg, jnp.zeros_like(qg)) for g in range(4)], axis=0)


def _unstack_heads(v4):
    lane = lax.broadcasted_iota(jnp.int32, (1, 256), 1) // HEAD_DIM
    out = jnp.where(lane == 0, v4[0:BLOCK], 0.0)
    for g in range(1, 4):
        out = out + jnp.where(lane == g, v4[g * BLOCK:(g + 1) * BLOCK], 0.0)
    return out


def _sink_rows(sink_ref, hk):
    return jnp.concatenate(
        [jnp.broadcast_to(sink_ref[0:1, 4 * hk + g:4 * hk + g + 1], (BLOCK, 1)) for g in range(4)], axis=0)


def _attn_fwd(q, k_rep, v_rep, sinks, carried=None):
    def body(q_ref, kp, ko, kn, kc, vp, vo, vn, vc, sink_ref, y_ref, lse_ref):
        i = pl.program_id(0)
        valid = _attn_valid(i, 1, True)
        lane8 = lax.broadcasted_iota(jnp.int32, (1, ATT_HEADS), 1)
        head_of_lane = lax.broadcasted_iota(jnp.int32, (1, 256), 1) // HEAD_DIM
        lse_out = jnp.zeros((BLOCK, ATT_HEADS), F32)
        for hk in range(KV_HEADS):
            sl = slice(256 * hk, 256 * (hk + 1))
            qg = q_ref[:, sl]
            keys = jnp.concatenate([kp[:, sl], ko[:, sl], kn[:, sl], kc[:, sl]], axis=0)
            vals = jnp.concatenate([vp[:, sl], vo[:, sl], vn[:, sl], vc[:, sl]], axis=0)
            yg = jnp.zeros((BLOCK, 256), F32)
            for g in range(4):
                q1 = jnp.where(head_of_lane == g, qg, jnp.zeros_like(qg))
                s = jnp.where(valid, _dot_nt(q1, keys), NEG)
                sink = sink_ref[0:1, 4 * hk + g:4 * hk + g + 1]
                m = jnp.maximum(jnp.max(s, axis=1, keepdims=True), sink)
                p = jnp.exp(s - m)
                den = jnp.sum(p, axis=1, keepdims=True) + jnp.exp(sink - m)
                o1 = _dot(p.astype(BF16), vals) * (1.0 / den)
                yg = yg + jnp.where(head_of_lane == g, o1, 0.0)
                lse_out = lse_out + jnp.where(lane8 == 4 * hk + g, m + jnp.log(den), 0.0)
            y_ref[:, sl] = yg.astype(BF16)
        lse_ref[...] = lse_out

    return _pcall(
        body, carried, name="attn_fwd", grid=(N_BLOCKS,),
        in_specs=[pl.BlockSpec((BLOCK, ATW), lambda i: (i, 0))] + _attn_window_specs()
        + _attn_window_specs() + [_full((1, ATT_HEADS))],
        out_specs=[pl.BlockSpec((BLOCK, ATW), lambda i: (i, 0)),
                   pl.BlockSpec((BLOCK, ATT_HEADS), lambda i: (i, 0))],
        out_shape=[jax.ShapeDtypeStruct((S, ATW), BF16), jax.ShapeDtypeStruct((S, ATT_HEADS), F32)],
        scratch_shapes=[],
        operands=[q, k_rep, k_rep, k_rep, k_rep, v_rep, v_rep, v_rep, v_rep, sinks])


def _attn_bwd(q, k_rep, v_rep, sinks, y_at, lse, dy, carried=None):
    def body(q_ref, kp, ko, kn, kc, vp, vo, vn, vc, sink_ref, y_ref, lse_ref, dy_ref,
             dq_ref, dk_ref, dv_ref, dsink_ref, dk_acc, dv_acc):
        i = pl.program_id(0)

        @pl.when(i == 0)
        def _():
            dk_acc[...] = jnp.zeros_like(dk_acc)
            dv_acc[...] = jnp.zeros_like(dv_acc)
            dk_ref[pl.ds(0, L), :] = jnp.zeros((L, ATW), F32)
            dv_ref[pl.ds(0, L), :] = jnp.zeros((L, ATW), F32)
            dsink_ref[...] = jnp.zeros_like(dsink_ref)

        valid = _attn_valid(i, 4, False)
        lane8 = lax.broadcasted_iota(jnp.int32, (1, ATT_HEADS), 1)
        w0 = pl.multiple_of(i * BLOCK, BLOCK)
        dsink = jnp.zeros((1, ATT_HEADS), F32)
        for hk in range(KV_HEADS):
            sl = slice(256 * hk, 256 * (hk + 1))
            q4 = _stack_heads(q_ref[:, sl])
            do4f = _stack_heads(dy_ref[:, sl])
            o4 = _stack_heads(y_ref[:, sl]).astype(F32)
            do4 = do4f.astype(BF16)
            kl = jnp.concatenate([kp[:, sl], ko[:, sl], kn[:, sl]], axis=0)
            vl = jnp.concatenate([vp[:, sl], vo[:, sl], vn[:, sl]], axis=0)
            lse4 = jnp.concatenate(
                [jnp.sum(jnp.where(lane8 == 4 * hk + g, lse_ref[...], 0.0), axis=1, keepdims=True)
                 for g in range(4)], axis=0)
            p_loc = jnp.where(valid, jnp.exp(_dot_nt(q4, kl) - lse4), 0.0)
            p_ctx = jnp.exp(_dot_nt(q4, kc[:, sl]) - lse4)
            delta = jnp.sum(do4f * o4, axis=1, keepdims=True)
            ds_loc = (p_loc * (_dot_nt(do4, vl) - delta)).astype(BF16)
            ds_ctx = (p_ctx * (_dot_nt(do4, vc[:, sl]) - delta)).astype(BF16)
            dq_ref[:, sl] = _unstack_heads(_dot(ds_loc, kl) + _dot(ds_ctx, kc[:, sl]))
            dk_acc[pl.ds(w0, 3 * BLOCK), sl] += _dot_tn(ds_loc, q4)
            dv_acc[pl.ds(w0, 3 * BLOCK), sl] += _dot_tn(p_loc.astype(BF16), do4)
            dk_ref[pl.ds(0, L), sl] += _dot_tn(ds_ctx, q4)
            dv_ref[pl.ds(0, L), sl] += _dot_tn(p_ctx.astype(BF16), do4)
            p_sink = jnp.exp(_sink_rows(sink_ref, hk) - lse4)
            for g in range(4):
                rows = slice(g * BLOCK, (g + 1) * BLOCK)
                dsink = dsink + jnp.where(lane8 == 4 * hk + g,
                                          -jnp.sum(p_sink[rows] * delta[rows], axis=0, keepdims=True), 0.0)
        dsink_ref[...] += dsink

        @pl.when(i == N_BLOCKS - 1)
        def _():
            dk_ref[pl.ds(L, S), :] = dk_acc[pl.ds(BLOCK, S), :]
            dv_ref[pl.ds(L, S), :] = dv_acc[pl.ds(BLOCK, S), :]

    row_q = pl.BlockSpec((BLOCK, ATW), lambda i: (i, 0))
    return _pcall(
        body, carried, name="attn_bwd", grid=(N_BLOCKS,),
        in_specs=[row_q] + _attn_window_specs() + _attn_window_specs()
        + [_full((1, ATT_HEADS)), row_q, pl.BlockSpec((BLOCK, ATT_HEADS), lambda i: (i, 0)), row_q],
        out_specs=[row_q, _full((T, ATW)), _full((T, ATW)), _full((1, ATT_HEADS))],
        out_shape=[jax.ShapeDtypeStruct((S, ATW), F32), jax.ShapeDtypeStruct((T, ATW), F32),
                   jax.ShapeDtypeStruct((T, ATW), F32), jax.ShapeDtypeStruct((1, ATT_HEADS), F32)],
        scratch_shapes=[pltpu.VMEM((S + 2 * BLOCK, ATW), F32), pltpu.VMEM((S + 2 * BLOCK, ATW), F32)],
        operands=[q, k_rep, k_rep, k_rep, k_rep, v_rep, v_rep, v_rep, v_rep, sinks, y_at, lse, dy])


def _merge_fwd(y_hg, y_at, p_c, x, w_bh, w_ba, w_out, g1, nfw, sh2, sc2, carried=None):
    def body(yh_ref, ya_ref, g_ref, x_ref, wbh_ref, wba_ref, wo_ref, g1_ref, nfw_ref, sh_ref, sc_ref,
             mx_ref, r_ref, x1_ref, h2_ref):
        a = _dot_nt(yh_ref[...], wbh_ref[...])
        b = _dot_nt(ya_ref[...], wba_ref[...])
        mixed = (_sigmoid(g_ref[:, :D]) * a + _sigmoid(g_ref[:, D:]) * b).astype(BF16)
        r = _dot(mixed, wo_ref[...])
        x1 = x_ref[...] + g1_ref[...] * r
        mx_ref[...] = mixed
        r_ref[...] = r
        x1_ref[...] = x1
        h2_ref[...] = _rms_mod(x1, nfw_ref[...], sh_ref[...], sc_ref[...]).astype(BF16)

    row = lambda w: pl.BlockSpec((TM, w), lambda i: (i, 0))
    vec = _full((1, D))
    return _pcall(
        body, carried, name="merge_fwd", grid=(N_LAT_TILES,),
        in_specs=[row(HGW), row(ATW), row(WC), row(D), _VMEM_WHOLE, _VMEM_WHOLE, _VMEM_WHOLE,
                  vec, vec, vec, vec],
        out_specs=[row(D)] * 4,
        out_shape=[jax.ShapeDtypeStruct((S, D), dt) for dt in (BF16, F32, F32, BF16)],
        scratch_shapes=[], operands=[y_hg, y_at, p_c, x, w_bh, w_ba, w_out, g1, nfw, sh2, sc2])


def _merge_bwd(dx1, r, y_hg, y_at, p_c, w_bh, w_ba, w_out, g1, carried=None):
    def body(dx_ref, r_ref, yh_ref, ya_ref, g_ref, wbh_ref, wba_ref, wo_ref, g1_ref,
             dr_ref, da_ref, db_ref, dg_ref, dyh_ref, dya_ref, acc_ref):
        @pl.when(pl.program_id(0) == 0)
        def _():
            acc_ref[...] = jnp.zeros_like(acc_ref)

        dx1v = dx_ref[...]
        acc_ref[0:1, :] += jnp.sum(dx1v * r_ref[...], axis=0, keepdims=True)
        dr = (g1_ref[...] * dx1v).astype(BF16)
        dr_ref[...] = dr
        dmix = _dot_nt(dr, wo_ref[...])
        sh, sa = _sigmoid(g_ref[:, :D]), _sigmoid(g_ref[:, D:])
        da = (dmix * sh).astype(BF16)
        db = (dmix * sa).astype(BF16)
        da_ref[...] = da
        db_ref[...] = db
        dg_ref[:, :D] = (dmix * _dot_nt(yh_ref[...], wbh_ref[...]) * sh * (1.0 - sh)).astype(BF16)
        dg_ref[:, D:] = (dmix * _dot_nt(ya_ref[...], wba_ref[...]) * sa * (1.0 - sa)).astype(BF16)
        dyh_ref[...] = _dot(da, wbh_ref[...])
        dya_ref[...] = _dot(db, wba_ref[...])

    row = lambda w: pl.BlockSpec((TM, w), lambda i: (i, 0))
    return _pcall(
        body, carried, name="merge_bwd", grid=(N_LAT_TILES,),
        in_specs=[row(D), row(D), row(HGW), row(ATW), row(WC), _VMEM_WHOLE, _VMEM_WHOLE, _VMEM_WHOLE,
                  _full((1, D))],
        out_specs=[row(D), row(D), row(D), row(WC), row(HGW), row(ATW), _full((8, D))],
        out_shape=[jax.ShapeDtypeStruct((S, D), BF16), jax.ShapeDtypeStruct((S, D), BF16),
                   jax.ShapeDtypeStruct((S, D), BF16), jax.ShapeDtypeStruct((S, WC), BF16),
                   jax.ShapeDtypeStruct((S, HGW), F32), jax.ShapeDtypeStruct((S, ATW), F32),
                   jax.ShapeDtypeStruct((8, D), F32)],
        scratch_shapes=[], operands=[dx1, r, y_hg, y_at, p_c, w_bh, w_ba, w_out, g1])


def _ffn_fused(x1, h2, tgt, w_gate, w_up, w_down, g2, nfw, sc2):
    def body(x1_ref, h2_ref, t_ref, wg_ref, wu_ref, wd_ref, g2_ref, nfw_ref, sc_ref,
             act_ref, dgt_ref, dup_ref, df_ref, dx_ref, acc_ref, gs, us):
        @pl.when(pl.program_id(0) == 0)
        def _():
            acc_ref[...] = jnp.zeros_like(acc_ref)

        h2 = h2_ref[...]
        whole = lambda w_ref: w_ref[...].reshape(D_FF, D)
        tile = lambda j: slice(j * FF_TILE, (j + 1) * FF_TILE)
        for j in range(N_FF_TILES):
            g = _dot_nt(h2, wg_ref[j])
            u = _dot_nt(h2, wu_ref[j])
            gs[j] = g
            us[j] = u
            act_ref[:, tile(j)] = (g * _sigmoid(g) * u).astype(BF16)
        f = _dot(act_ref[...], whole(wd_ref))
        x1v = x1_ref[...]
        g2 = g2_ref[...]
        diff = x1v + g2 * f - t_ref[...]
        dy = diff * (1.0 / D)
        df = (g2 * dy).astype(BF16)
        df_ref[...] = df
        dact_all = _dot_nt(df, whole(wd_ref))
        for j in range(N_FF_TILES):
            g, u = gs[j], us[j]
            sg = _sigmoid(g)
            dact = dact_all[:, tile(j)]
            dgt_ref[:, tile(j)] = (dact * u * (sg * (1.0 + g * (1.0 - sg)))).astype(BF16)
            dup_ref[:, tile(j)] = (dact * (g * sg)).astype(BF16)
        dh2 = _dot(dgt_ref[...], whole(wg_ref)) + _dot(dup_ref[...], whole(wu_ref))
        dx, dsh, dsc, dnw = _rms_mod_bwd(x1v, nfw_ref[...], sc_ref[...], dh2)
        dx_ref[...] = dy + dx
        acc_ref[0:1, :] += dsh
        acc_ref[1:2, :] += dsc
        acc_ref[2:3, :] += dnw
        acc_ref[3:4, :] += jnp.sum(dy * f, axis=0, keepdims=True)
        acc_ref[4:5, :] += 0.5 * jnp.sum(jnp.sum(diff * diff, axis=1, keepdims=True), axis=0,
                                         keepdims=True) * (1.0 / D)

    row = lambda dt_w: pl.BlockSpec((TM, dt_w), lambda i: (i, 0))
    blk = row(D_FF)
    vec = _full((1, D))
    return pl.pallas_call(
        body, name="ffn_fused", grid=(N_LAT_TILES,),
        in_specs=[row(D), row(D), row(D), _VMEM_WHOLE, _VMEM_WHOLE, _VMEM_WHOLE, vec, vec, vec],
        out_specs=[blk, blk, blk, row(D), row(D), _full((8, D))],
        out_shape=[jax.ShapeDtypeStruct((S, D_FF), BF16)] * 3
        + [jax.ShapeDtypeStruct((S, D), BF16), jax.ShapeDtypeStruct((S, D), F32),
           jax.ShapeDtypeStruct((8, D), F32)],
        scratch_shapes=[pltpu.VMEM((N_FF_TILES, TM, FF_TILE), F32), pltpu.VMEM((N_FF_TILES, TM, FF_TILE), F32)],
        compiler_params=_cp(("arbitrary",)),
    )(x1, h2, tgt, w_gate, w_up, w_down, g2, nfw, sc2)


def _proj_bc(h_all, w_b, w_c, carried=None):
    def body(h_ref, wb_ref, wc_ref, pb_ref, pc_ref):
        h = h_ref[...]
        pb_ref[...] = _dot_nt(h, wb_ref[...])

        @pl.when(pl.program_id(0) > 0)
        def _():
            pc_ref[...] = _dot_nt(h, wc_ref[...])

    return _pcall(
        body, carried, name="proj_bc", grid=(N_TILES,),
        in_specs=[pl.BlockSpec((TM, D), lambda i: (i, 0)), _VMEM_WHOLE, _VMEM_WHOLE],
        out_specs=[pl.BlockSpec((TM, WB), lambda i: (i, 0)), pl.BlockSpec((TM, WC), lambda i: (_lat(i), 0))],
        out_shape=[jax.ShapeDtypeStruct((T, WB), F32), jax.ShapeDtypeStruct((S, WC), F32)],
        scratch_shapes=[], operands=[h_all, w_b, w_c])


def _input_bwd(dp_a, dp_b, dp_c, w_a, w_b, w_c, ctx, x, dx1, nw, sh, sc, carried=None):
    def body(da_ref, db_ref, dc_ref, wa_ref, wb_ref, wc_ref, ctx_ref, x_ref, dx1_ref, nw_ref, sh_ref,
             sc_ref, gx_ref, acc_ref):
        i = pl.program_id(0)

        @pl.when(i == 0)
        def _():
            acc_ref[...] = jnp.zeros_like(acc_ref)

        dh = _dot(da_ref[...], wa_ref[...]) + _dot(db_ref[...], wb_ref[...])

        @pl.when(i == 0)
        def _():
            _, dsh, dsc, dnw = _rms_mod_bwd(ctx_ref[...], nw_ref[...], sc_ref[0:1, :], dh)
            acc_ref[3:4, :] += dsh
            acc_ref[4:5, :] += dsc
            acc_ref[2:3, :] += dnw

        @pl.when(i > 0)
        def _():
            dhl = dh + _dot(dc_ref[...], wc_ref[...])
            dx, dsh, dsc, dnw = _rms_mod_bwd(x_ref[...], nw_ref[...], sc_ref[1:2, :], dhl)
            gx_ref[...] = dx1_ref[...] + dx
            acc_ref[0:1, :] += dsh
            acc_ref[1:2, :] += dsc
            acc_ref[2:3, :] += dnw

    lat = lambda w: pl.BlockSpec((TM, w), lambda i: (_lat(i), 0))
    return _pcall(
        body, carried, name="input_bwd", grid=(N_TILES,),
        in_specs=[pl.BlockSpec((TM, WA), lambda i: (i, 0)), pl.BlockSpec((TM, WB), lambda i: (i, 0)),
                  lat(WC), _VMEM_WHOLE, _VMEM_WHOLE, _VMEM_WHOLE, _full((TM, D)), lat(D), lat(D),
                  _full((1, D)), _full((2, D)), _full((2, D))],
        out_specs=[lat(D), _full((8, D))],
        out_shape=[jax.ShapeDtypeStruct((S, D), F32), jax.ShapeDtypeStruct((8, D), F32)],
        scratch_shapes=[], operands=[dp_a, dp_b, dp_c, w_a, w_b, w_c, ctx, x, dx1, nw, sh, sc])


_C1 = 1.0 - ADAM_B1 ** ADAM_STEP
_C2 = 1.0 - ADAM_B2 ** ADAM_STEP


def _adamw_math(w, g, m, v):
    m = ADAM_B1 * m + (1.0 - ADAM_B1) * g
    v = ADAM_B2 * v + (1.0 - ADAM_B2) * (g * g)
    m_hat = m / _C1
    v_hat = v / _C2
    delta = -ADAM_LR * (m_hat / (jnp.sqrt(v_hat) + ADAM_EPS) + ADAM_WD * w)
    return delta, m, v


def _adamw_sharded(terms, w, m, v, name, tr, extra=None, carried=None):
    rows, cols = w.shape

    def body(*refs):
        t_ref, w_ref, m_ref, v_ref = refs[:4]
        g_ref, d_ref, nm_ref, nv_ref = refs[-4:]
        g = t_ref[0].astype(F32)
        for s in range(1, N_CHIPS):
            g = g + t_ref[s].astype(F32)
        if extra is not None:
            g = g + refs[4][...].astype(F32)
        g_ref[...] = g
        d_ref[...], nm_ref[...], nv_ref[...] = _adamw_math(w_ref[...], g, m_ref[...], v_ref[...])

    blk = pl.BlockSpec((tr, cols), lambda i: (i, 0))
    return _pcall(
        body, carried, name=name, grid=(rows // tr,),
        in_specs=[pl.BlockSpec((N_CHIPS, tr, cols), lambda i: (0, i, 0)), blk, blk, blk]
        + ([blk] if extra is not None else []),
        out_specs=[blk] * 4,
        out_shape=[jax.ShapeDtypeStruct((rows, cols), F32)] * 4,
        scratch_shapes=[], operands=[terms, w, m, v] + ([extra] if extra is not None else []))


def _adamw_plain(g, w, m, v, name, tr=None):
    def body(g_ref, w_ref, m_ref, v_ref, d_ref, nm_ref, nv_ref):
        d_ref[...], nm_ref[...], nv_ref[...] = _adamw_math(w_ref[...], g_ref[...], m_ref[...], v_ref[...])

    if tr is None:
        return pl.pallas_call(
            body, name=name, in_specs=[_VMEM_WHOLE] * 4, out_specs=[_VMEM_WHOLE] * 3,
            out_shape=[jax.ShapeDtypeStruct(w.shape, F32)] * 3,
            compiler_params=_cp(),
        )(g, w, m, v)
    blk = pl.BlockSpec((tr, w.shape[1]), lambda i: (i, 0))
    return pl.pallas_call(
        body, name=name, grid=(w.shape[0] // tr,), in_specs=[blk] * 4, out_specs=[blk] * 3,
        out_shape=[jax.ShapeDtypeStruct(w.shape, F32)] * 3,
        compiler_params=_cp(("parallel",)),
    )(g, w, m, v)


SMALL_ROWS = 16
R_DMOD, R_DCTX, R_NMIX, R_NFFN, R_MISC, R_DLB, R_BADA01 = 0, 6, 8, 9, 10, 11, 13
M_HNW, M_QNW, M_KNW, M_SINK, M_LOSS = 0, 128, 256, 384, 512


def _pack_small(acc_in, acc_mg, acc_ffn, acc_prep, dsink, dlb):
    def body(in_ref, mg_ref, ff_ref, pp_ref, ds_ref, dlb_ref, o_ref):
        o_ref[...] = jnp.zeros_like(o_ref)
        o_ref[0:2, :] = in_ref[0:2, :]
        o_ref[2:3, :] = mg_ref[0:1, :]
        o_ref[3:5, :] = ff_ref[0:2, :]
        o_ref[5:6, :] = ff_ref[3:4, :]
        o_ref[6:8, :] = in_ref[3:5, :]
        o_ref[8:9, :] = in_ref[2:3, :]
        o_ref[9:10, :] = ff_ref[2:3, :]
        o_ref[10:11, M_HNW:M_HNW + HG_DIM] = pp_ref[0:1, 0:HG_DIM]
        r = lax.broadcasted_iota(jnp.int32, (ATW, 128), 0)
        c = lax.broadcasted_iota(jnp.int32, (ATW, 128), 1)
        fold = jnp.where((r % HEAD_DIM == c) & (c < HEAD_DIM), 1.0, 0.0).astype(BF16)
        qk = jnp.concatenate([pp_ref[1:2, :], pp_ref[2:3, :], jnp.zeros((6, ATW), F32)], axis=0)
        folded = _dot_exact_rhs01(qk, fold)
        o_ref[10:11, M_QNW:M_QNW + 128] = folded[0:1, :]
        o_ref[10:11, M_KNW:M_KNW + 128] = folded[1:2, :]
        o_ref[10:11, M_SINK:M_SINK + ATT_HEADS] = ds_ref[...]
        o_ref[10:11, M_LOSS:M_LOSS + 128] = ff_ref[4:5, 0:128]
        o_ref[11:13, 0:HGW] = dlb_ref[...]

    return pl.pallas_call(
        body, name="pack_small", in_specs=[_VMEM_WHOLE] * 6, out_specs=_VMEM_WHOLE,
        out_shape=jax.ShapeDtypeStruct((SMALL_ROWS, D), F32), compiler_params=_cp(),
    )(acc_in, acc_mg, acc_ffn, acc_prep, dsink, dlb)


def _sum_small(gathered):
    def body(g_ref, o_ref):
        tot = g_ref[0]
        for s in range(1, N_DEV):
            tot = tot + g_ref[s]
        o_ref[...] = tot
        o_ref[R_BADA01:R_BADA01 + 2, :] = tot[0:2, :] + tot[R_DCTX:R_DCTX + 2, :]

    return pl.pallas_call(
        body, name="sum_small", in_specs=[_VMEM_WHOLE], out_specs=_VMEM_WHOLE,
        out_shape=jax.ShapeDtypeStruct((SMALL_ROWS, D), F32), compiler_params=_cp(),
    )(gathered)


_REP_NAMES = ("b_ada", "c_ctx", "norm_mix_w", "norm_ffn_w", "hgrn_norm_w", "q_norm_w", "k_norm_w", "attn_sinks")


def _adamw_replicated(tot, g_c_ctx, ws, ms, vs):
    n = len(_REP_NAMES)

    def body(*refs):
        tot_ref, gc_ref = refs[0], refs[1]
        w_refs, m_refs, v_refs = refs[2:2 + n], refs[2 + n:2 + 2 * n], refs[2 + 2 * n:2 + 3 * n]
        outs = refs[2 + 3 * n:]
        row = lambda r: tot_ref[r:r + 1, :]
        misc = row(R_MISC)
        grads = [jnp.concatenate([row(R_BADA01), row(R_BADA01 + 1)] + [row(k) for k in range(2, 6)], axis=1),
                 gc_ref[...], row(R_NMIX), row(R_NFFN),
                 misc[:, M_HNW:M_HNW + HG_DIM], misc[:, M_QNW:M_QNW + HEAD_DIM],
                 misc[:, M_KNW:M_KNW + HEAD_DIM], misc[:, M_SINK:M_SINK + ATT_HEADS]]
        for k in range(n):
            outs[k][...] = grads[k]
            outs[n + k][...], outs[2 * n + k][...], outs[3 * n + k][...] = _adamw_math(
                w_refs[k][...], grads[k], m_refs[k][...], v_refs[k][...])

    shapes = [jax.ShapeDtypeStruct(w.shape, F32) for w in ws]
    return pl.pallas_call(
        body, name="adamw_replicated", in_specs=[_VMEM_WHOLE] * (2 + 3 * n), out_specs=[_VMEM_WHOLE] * (4 * n),
        out_shape=shapes * 4, compiler_params=_cp(),
    )(tot, g_c_ctx, *ws, *ms, *vs)


def _lb_grads(dlb, lbl):
    def body(d_ref, l_ref, o_ref):
        for d in (0, 1):
            ll = l_ref[d]
            lb = _sigmoid(ll[0:1, :] - ll[1:2, :])
            t = d_ref[d:d + 1, :] * lb * (1.0 - lb)
            o_ref[d, 0:1, :] = t
            o_ref[d, 1:2, :] = -t

    return pl.pallas_call(
        body, name="lb_grads", in_specs=[_VMEM_WHOLE] * 2, out_specs=_VMEM_WHOLE,
        out_shape=jax.ShapeDtypeStruct((2, 2, HGW), F32), compiler_params=_cp(),
    )(dlb, lbl)


def _c_ctx_grad(terms, c_ctx):
    def body(t_ref, c_ref, o_ref):
        tot = t_ref[0, 8:9, :]
        for s in range(1, N_DEV):
            tot = tot + t_ref[s, 8:9, :]
        cv = c_ref[...]
        sg = _sigmoid(cv)
        o_ref[...] = tot * (sg * (1.0 + cv * (1.0 - sg)))

    return pl.pallas_call(
        body, name="c_ctx_grad", in_specs=[_VMEM_WHOLE] * 2, out_specs=_VMEM_WHOLE,
        out_shape=jax.ShapeDtypeStruct((1, D), F32), compiler_params=_cp(),
    )(terms, c_ctx)


def _in_perm():
    fz, bz, inp, kk, vv, qhg, ghg, qat, gates = 0, 512, 1024, 1536, 1664, 1792, 2304, 2816, 3328
    cols = []
    for h in range(HG_HEADS):
        for base in (fz, bz, inp, qhg):
            cols += list(range(base + 128 * h, base + 128 * (h + 1)))
    cols += list(range(ghg, ghg + 512)) + list(range(qat, qat + 512))
    cols += list(range(kk, kk + 128)) + list(range(vv, vv + 128))
    cols += list(range(gates, gates + 2048))
    return np.asarray(cols, np.int32)


_PERM = _in_perm()


_PIECES = {"a": (0, WA, 128), "b": (WA, WB, 256), "c": (WA + WB, WC, 256)}


def _block_table(piece):
    lo, n, blk = _PIECES[piece]
    starts = [int(_PERM[r]) for r in range(lo, lo + n, blk)]
    assert all(s % blk == 0 and np.array_equal(_PERM[r:r + blk], np.arange(s, s + blk))
               for s, r in zip(starts, range(lo, lo + n, blk)))
    return jnp.asarray([s // blk for s in starts], jnp.int32), blk


def _pick_row_blocks(x, table, blk, name):
    cols = x.shape[1]

    def body(t_ref, x_ref, o_ref):
        o_ref[...] = x_ref[...]

    return pl.pallas_call(
        body, name=name,
        grid_spec=pltpu.PrefetchScalarGridSpec(
            num_scalar_prefetch=1, grid=(table.shape[0],),
            in_specs=[pl.BlockSpec((blk, cols), lambda i, t: (t[i], 0))],
            out_specs=pl.BlockSpec((blk, cols), lambda i, t: (i, 0))),
        out_shape=jax.ShapeDtypeStruct((table.shape[0] * blk, cols), x.dtype),
        compiler_params=_cp(("arbitrary",)),
    )(table, x)


def _place_row_blocks(x, table, blk, into, out_rows, name):
    cols = x.shape[1]

    def body(t_ref, x_ref, *rest):
        rest[-1][...] = x_ref[...]

    operands, in_specs, aliases = [table, x], [pl.BlockSpec((blk, cols), lambda i, t: (i, 0))], {}
    if into is not None:
        operands.append(into)
        in_specs.append(_ANY)
        aliases = {2: 0}
    return pl.pallas_call(
        body, name=name,
        grid_spec=pltpu.PrefetchScalarGridSpec(
            num_scalar_prefetch=1, grid=(table.shape[0],), in_specs=in_specs,
            out_specs=pl.BlockSpec((blk, cols), lambda i, t: (t[i], 0))),
        out_shape=jax.ShapeDtypeStruct((out_rows, cols), x.dtype),
        input_output_aliases=aliases,
        compiler_params=_cp(("arbitrary",)),
    )(*operands)


def _local_step(x2, ctx2, h_all, h_lat, tgt, lbl, sh_in, sc_in, gate1, sh2, sc2, gate2, norm_mix_w, norm_ffn_w,
                hgrn_norm_w, q_norm_w, k_norm_w, attn_sinks, w_a, w_b, w_c, s_bh, s_ba, s_out,
                s_gate, s_up, s_down):
    first_last = lambda n: [(0, True), (n - 1, False)]
    p_a = _mm_nt(h_all, w_a, tm=T, tn=512, out_dtype=F32, name="proj_a")
    (o, st), (g_gate, g_bh, g_ba) = _hgrn_fwd(
        p_a, lbl, (_gather_comm_relayed([s_gate, s_bh, s_ba]),
                   [(0, True), (HG_HEADS - 2, True), (HG_HEADS - 1, False)]))
    (p_b, p_c), (g_out,) = _proj_bc(
        h_all, w_b, w_c, (_gather_comm_relayed([s_out]), [(0, True), (N_TILES - 4, True), (N_TILES - 1, False)]))
    cos, sin = _rope_tables()
    qnw_t, knw_t = jnp.tile(q_norm_w, (1, ATT_HEADS)), jnp.tile(k_norm_w, (1, KV_HEADS))
    y_hg, qn, k_rep, v_rep = _prep_fwd(p_b, o, cos, sin, hgrn_norm_w, qnw_t, knw_t)
    (y_at, lse), (g_up, g_down) = _attn_fwd(
        qn, k_rep, v_rep, attn_sinks,
        (_gather_comm_relayed([s_up, s_down]), [(0, True), (N_BLOCKS - 6, True), (N_BLOCKS - 1, False)]))
    w_bh, w_ba, w_o = g_bh.reshape(D, HGW), g_ba.reshape(D, ATW), g_out.reshape(D, D)
    (mixed, r, x1, h2), _ = _merge_fwd(
        y_hg, y_at, p_c, x2, w_bh, w_ba, w_o, gate1, norm_ffn_w, sh2, sc2)
    g_gate, g_up, g_down = [g.reshape(N_FF_TILES, FF_TILE, D) for g in (g_gate, g_up, g_down)]

    act, d_gate, d_up, d_f, dx1, acc_ffn = _ffn_fused(x1, h2, tgt, g_gate, g_up, g_down, gate2,
                                                      norm_ffn_w, sc2)
    by_chip = lambda t: t.reshape((N_CHIPS, 2) + t.shape[1:])
    ff_by_chip = lambda t: t.reshape(N_CHIPS, 2, FF_BLK, D)
    t_down, _ = _mm_tn_blocked(act, d_f, "grad_down")
    t_down = ff_by_chip(t_down)
    t_gate, (f_down,) = _mm_tn_blocked(d_gate, h2, "grad_gate", (_sibling_comm([t_down]), first_last(N_FF_HALVES)))
    t_gate = ff_by_chip(t_gate)
    t_up, (f_gate,) = _mm_tn_blocked(d_up, h2, "grad_up", (_sibling_comm([t_gate]), first_last(N_FF_HALVES)))
    t_up = ff_by_chip(t_up)

    (d_r, d_a, d_b, dp_c, dy_hg, dy_at, acc_mg), (f_up,) = _merge_bwd(
        dx1, r, y_hg, y_at, p_c, w_bh, w_ba, w_o, gate1, (_sibling_comm([t_up]), first_last(N_LAT_TILES)))
    c_down, c_gate, c_up = [_pair_sum(t, f, "pair_sum_" + nm) for t, f, nm in
                            ((t_down, f_down, "down"), (t_gate, f_gate, "gate"), (t_up, f_up, "up"))]
    t_out = _mm_tn(mixed, d_r, tk=1024, nk=2, tm=1024, tn=1024, out_dtype=BF16, name="grad_out")
    t_bh = _mm_tn(d_a, y_hg, tk=2048, nk=1, tm=1024, tn=512, out_dtype=BF16, name="grad_bh")
    t_ba = _mm_tn(d_b, y_at, tk=2048, nk=1, tm=1024, tn=512, out_dtype=BF16, name="grad_ba")
    t_bh, t_ba, t_out = [by_chip(t.reshape(N_DEV, D // N_DEV, t.shape[1])) for t in (t_bh, t_ba, t_out)]
    (dq, dk_rep, dv_rep, dsink), (r_up,) = _attn_bwd(
        qn, k_rep, v_rep, attn_sinks, y_at, lse, dy_at, (_chip_comm([c_up]), first_last(N_BLOCKS)))
    (dp_b, d_o, acc_prep), (f_bh, f_ba, f_out) = _prep_bwd(
        p_b, o, cos, sin, hgrn_norm_w, qnw_t, knw_t, dy_hg, dq, dk_rep, dv_rep,
        (_sibling_comm([t_bh, t_ba, t_out]), first_last(N_TILES)))
    c_bh, c_ba, c_out = [_pair_sum(t, f, "pair_sum_" + nm) for t, f, nm in
                         ((t_bh, f_bh, "bh"), (t_ba, f_ba, "ba"), (t_out, f_out, "out"))]
    (dp_a, dlb), (r_bh, r_ba, r_out, r_down, r_gate) = _hgrn_bwd(
        p_a, lbl, d_o, st, (_chip_comm([c_bh, c_ba, c_out, c_down, c_gate]), first_last(HG_HEADS)))
    t_a = _mm_tn(dp_a, h_all, tk=T, nk=1, tm=1024, tn=1024, out_dtype=BF16, name="grad_in_a")
    t_b = _mm_tn(dp_b, h_all, tk=T, nk=1, tm=640, tn=1024, out_dtype=BF16, name="grad_in_b")
    t_c = _mm_tn(dp_c, h_lat, tk=1024, nk=2, tm=1024, tn=1024, out_dtype=BF16, name="grad_in_c")
    t_in = None
    for piece, nm in ((t_a, "a"), (t_b, "b"), (t_c, "c")):
        t_in = _place_row_blocks(piece, *_block_table(nm), t_in, IN_COLS, "order_terms_" + nm)
    t_in = by_chip(t_in.reshape(N_DEV, IN_BLK, D))
    (f_in,) = _run_comm(_sibling_comm([t_in]), "scatter_in_sibling")
    c_in = _pair_sum(t_in, f_in, "pair_sum_in")
    sems, c_in, land, token = _chip_exchange_start(c_in, jnp.zeros(c_in.shape, c_in.dtype))
    (grad_x, acc_in), _ = _input_bwd(dp_a, dp_b, dp_c, w_a, w_b, w_c, ctx2, x2, dx1,
                                     norm_mix_w + token[0, 0], sh_in, sc_in)
    small = _pack_small(acc_in, acc_mg, acc_ffn, acc_prep, dsink, dlb)
    return grad_x, small, [r_bh, r_ba, r_out, r_gate, r_up, r_down], (sems, c_in, land)


def kernel(x, c, ctx, c_ctx, w_ada, b_ada, norm_mix_w, norm_ffn_w, w_in, hgrn_lb_logits, hgrn_norm_w, q_norm_w, k_norm_w, attn_sinks, w_branch_hgrn, w_branch_attn, w_out, w_ffn_gate, w_ffn_up, w_ffn_down, loss_target, m_c_ctx, m_w_ada, m_b_ada, m_norm_mix_w, m_norm_ffn_w, m_w_in, m_hgrn_lb_logits, m_hgrn_norm_w, m_q_norm_w, m_k_norm_w, m_attn_sinks, m_w_branch_hgrn, m_w_branch_attn, m_w_out, m_w_ffn_gate, m_w_ffn_up, m_w_ffn_down, v_c_ctx, v_w_ada, v_b_ada, v_norm_mix_w, v_norm_ffn_w, v_w_in, v_hgrn_lb_logits, v_hgrn_norm_w, v_q_norm_w, v_k_norm_w, v_attn_sinks, v_w_branch_hgrn, v_w_branch_attn, v_w_out, v_w_ffn_gate, v_w_ffn_up, v_w_ffn_down):
    me = 4 * lax.axis_index("x") + 2 * lax.axis_index("y") + lax.axis_index("c")
    x2, ctx2, tgt = x[0], ctx[0], loss_target[0]
    w_ada2, w_in2 = w_ada[0], w_in[0]

    cond = jnp.zeros((8, D), F32).at[0].set(c[0]).at[1, :256].set(hgrn_lb_logits.reshape(256))
    b_cols = lax.dynamic_slice(b_ada, (0, me * ADA_BLK), (1, ADA_BLK))
    g0, cc, mod, g_in, h_all, h_lat = _prologue(cond, c_ctx.reshape(1, D), w_ada2, b_cols, w_in2.T.astype(BF16),
                                         x2, ctx2, norm_mix_w)
    lbl = jnp.transpose(g0[:, 1, :256].reshape(N_DEV, 2, 2, 64), (1, 2, 0, 3)).reshape(2, 2, HGW)
    sh1, sc1, gate1, sh2, sc2, gate2 = [mod[k:k + 1] for k in range(6)]
    sh_in = jnp.concatenate([mod[6:7], sh1], axis=0)
    sc_in = jnp.concatenate([mod[7:8], sc1], axis=0)

    shards = [w_branch_hgrn[0].T, w_branch_attn[0].T, w_out[0], w_ffn_gate[0].T, w_ffn_up[0].T, w_ffn_down[0]]
    w_in_t = g_in.reshape(IN_COLS, D)
    w_a, w_b, w_c = [_pick_row_blocks(w_in_t, *_block_table(nm), "order_w_" + nm) for nm in "abc"]

    grad_x, small, (r_bh, r_ba, r_out, r_gate, r_up, r_down), pending_in = _local_step(
        x2, ctx2, h_all, h_lat, tgt, lbl, sh_in, sc_in, gate1, sh2, sc2, gate2, norm_mix_w, norm_ffn_w, hgrn_norm_w,
        q_norm_w, k_norm_w, attn_sinks, w_a, w_b, w_c, *[s.astype(BF16) for s in shards])

    big = {}
    for nm, rr, ww, mm, vv, tr, transposed in (
            ("w_branch_hgrn", r_bh, w_branch_hgrn[0], m_w_branch_hgrn[0], v_w_branch_hgrn[0], 128, True),
            ("w_branch_attn", r_ba, w_branch_attn[0], m_w_branch_attn[0], v_w_branch_attn[0], 128, True),
            ("w_out", r_out, w_out[0], m_w_out[0], v_w_out[0], 128, False),
            ("w_ffn_gate", r_gate, w_ffn_gate[0], m_w_ffn_gate[0], v_w_ffn_gate[0], 176, True),
            ("w_ffn_up", r_up, w_ffn_up[0], m_w_ffn_up[0], v_w_ffn_up[0], 176, True),
            ("w_ffn_down", r_down, w_ffn_down[0], m_w_ffn_down[0], v_w_ffn_down[0], 176, False)):
        if transposed:
            res, _ = _adamw_sharded(rr, ww.T, mm.T, vv.T, "adamw_" + nm, tr)
            big[nm] = [t.T[None] for t in res]
        else:
            big[nm] = [t[None] for t in _adamw_sharded(rr, ww, mm, vv, "adamw_" + nm, tr)[0]]

    (g2,) = _all_gather([small], "gather_small", True)
    tot = _sum_small(g2)
    dm = jnp.zeros((16, 6 * D), F32).at[:8].set(g2[:, R_DMOD:R_DMOD + 6, :].reshape(N_DEV, 6 * D))
    dm = dm.at[8, :2 * D].set(tot[R_DCTX:R_DCTX + 2].reshape(2 * D))
    dm_cols = lax.dynamic_slice(dm, (0, me * ADA_BLK), (16, ADA_BLK))
    g_w_ada, dsc_term = _ada_grads(cc, dm_cols, w_ada2)

    sems, c_in, land = pending_in
    d_ada, nm_ada, nv_ada = _adamw_plain(g_w_ada, w_ada2, m_w_ada[0], v_w_ada[0], "adamw_w_ada", tr=256)
    land = _chip_exchange_wait(sems, c_in, land, d_ada)
    own = lax.dynamic_index_in_dim(c_in, 2 * lax.axis_index("x") + lax.axis_index("y"), 0, keepdims=False)
    res_in, (g3,) = _adamw_sharded(land, w_in2.T, m_w_in[0].T, v_w_in[0].T, "adamw_w_in", 336, extra=own,
                                   carried=(_gather_comm([dsc_term]), [(0, True), (1, True), (1, False)]))
    big["w_in"] = [t.T[None] for t in res_in]
    ada = [t[None] for t in (g_w_ada, d_ada, nm_ada, nv_ada)]
    g_c_ctx = _c_ctx_grad(g3, c_ctx.reshape(1, D))
    g_lbl = _lb_grads(tot[R_DLB:R_DLB + 2, :HGW], lbl)
    g_lb_mine = lax.dynamic_slice(g_lbl, (0, 0, me * 64), (2, 2, 64))
    misc = tot[R_MISC]
    loss = misc[M_LOSS]

    rep_out = _adamw_replicated(
        tot, g_c_ctx,
        [b_ada, c_ctx.reshape(1, D), norm_mix_w, norm_ffn_w, hgrn_norm_w, q_norm_w, k_norm_w, attn_sinks],
        [m_b_ada, m_c_ctx.reshape(1, D), m_norm_mix_w, m_norm_ffn_w, m_hgrn_norm_w, m_q_norm_w, m_k_norm_w,
         m_attn_sinks],
        [v_b_ada, v_c_ctx.reshape(1, D), v_norm_mix_w, v_norm_ffn_w, v_hgrn_norm_w, v_q_norm_w, v_k_norm_w,
         v_attn_sinks])
    rep = []
    for kind in range(4):
        vals = dict(zip(_REP_NAMES, rep_out[kind * len(_REP_NAMES):(kind + 1) * len(_REP_NAMES)]))
        vals["c_ctx"] = vals["c_ctx"].reshape(D)
        rep.append(vals)

    lb_w = hgrn_lb_logits.reshape(4, 64)
    d_lb, nm_lb, nv_lb = _adamw_plain(g_lb_mine.reshape(4, 64), lb_w, m_hgrn_lb_logits.reshape(4, 64),
                                      v_hgrn_lb_logits.reshape(4, 64), "adamw_lb")
    lbs = [t.reshape(2, 2, 64) for t in (g_lb_mine, d_lb, nm_lb, nv_lb)]

    names = ['c_ctx', 'w_ada', 'b_ada', 'norm_mix_w', 'norm_ffn_w', 'w_in', 'hgrn_lb_logits', 'hgrn_norm_w',
             'q_norm_w', 'k_norm_w', 'attn_sinks', 'w_branch_hgrn', 'w_branch_attn', 'w_out', 'w_ffn_gate',
             'w_ffn_up', 'w_ffn_down']
    outs = [loss, grad_x[None]]
    for kind in range(4):
        for nm in names:
            if nm == 'w_ada':
                outs.append(ada[kind])
            elif nm == 'hgrn_lb_logits':
                outs.append(lbs[kind])
            elif nm in big:
                outs.append(big[nm][kind])
            else:
                outs.append(rep[kind][nm])
    return tuple(outs)
```

```python
import functools
import math

import numpy as np
import jax
import jax.numpy as jnp
from jax import lax
from jax.experimental import pallas as pl
from jax.experimental.pallas import tpu as pltpu

F32 = jnp.float32
BF16 = jnp.bfloat16

N_DEV = 8
D = 1024
S = 2048
L = 256
T = L + S
TM = 256
N_TILES = T // TM
N_LAT_TILES = S // TM
HG_HEADS = 4
HG_DIM = 128
HGW = 512
CHUNK = 32
N_CHUNKS = T // CHUNK
N_CTX_CHUNKS = L // CHUNK
ATT_HEADS = 8
KV_HEADS = 2
HEAD_DIM = 64
ATW = 512
KVW = 128
BLOCK = 128
N_BLOCKS = S // BLOCK
GRID_W = 64
ROPE_THETA = 10000.0
D_FF = 2816
FF_BLK = D_FF // N_DEV
FF_TILE = 256
N_FF_TILES = D_FF // FF_TILE
N_FF_HALVES = 2
IN_COLS = 5376
IN_BLK = IN_COLS // N_DEV
ADA_BLK = 6 * D // N_DEV
EPS = 1e-6
WA, WB, WC = 2048, 1280, 2048

ADAM_LR = 0.001
ADAM_B1 = 0.9
ADAM_B2 = 0.999
ADAM_EPS = 1e-08
ADAM_WD = 0.01
ADAM_STEP = 10

VMEM_LIMIT = 56 * 1024 * 1024
MESH = pl.DeviceIdType.MESH


def _cp(sem=None, vmem=VMEM_LIMIT):
    return pltpu.CompilerParams(dimension_semantics=sem, vmem_limit_bytes=vmem)


def _full(shape):
    n = len(shape)
    return pl.BlockSpec(shape, lambda *_: (0,) * n)


_VMEM_WHOLE = pl.BlockSpec(memory_space=pltpu.VMEM)
_ANY = pl.BlockSpec(memory_space=pl.ANY)


def _sigmoid(v):
    return 1.0 / (1.0 + jnp.exp(-v))


def _dot(a, b):
    return jnp.dot(a, b, preferred_element_type=F32)


def _dot_nt(a, b):
    return lax.dot_general(a, b, (((1,), (1,)), ((), ())), preferred_element_type=F32)


def _dot_tn(a, b):
    return lax.dot_general(a, b, (((0,), (0,)), ((), ())), preferred_element_type=F32)


def _split3(v):
    hi = v.astype(BF16)
    r = v - hi.astype(F32)
    mid = r.astype(BF16)
    lo = (r - mid.astype(F32)).astype(BF16)
    return hi, mid, lo


def _dot_exact_rhs01(v, m01):
    hi, mid, lo = _split3(v)
    return _dot(hi, m01) + _dot(mid, m01) + _dot(lo, m01)


def _split2(v):
    hi = v.astype(BF16)
    return hi, (v - hi.astype(F32)).astype(BF16)


def _dot_lhs01(m01, v):
    hi, lo = _split2(v)
    return _dot(m01, hi) + _dot(m01, lo)


def _dot_f32(a, b, dot=_dot):
    ah, am, al = _split3(a)
    bh, bm, bl = _split3(b)
    return (dot(ah, bh) + (dot(ah, bm) + dot(am, bh))
            + (dot(am, bm) + dot(ah, bl) + dot(al, bh)))


def _my_pos():
    return lax.axis_index("x"), lax.axis_index("y"), lax.axis_index("c")


class _Comm:
    def __init__(self, operands, out_shapes, sems, phases):
        self.operands, self.out_shapes, self.sems, self.phases = operands, out_shapes, sems, phases


def _gather_comm(blocks):
    n = len(blocks)

    def parts(ins, outs, sems):
        send_sems, recv_sems, local_sems = sems
        x, y, c = _my_pos()
        me, sibling = (x, y, c), (x, y, 1 - c)
        chips = [(1 - x, y), (x, 1 - y), (1 - x, 1 - y)]

        def slot(a, px, py, pc):
            return outs[a].at[4 * px + 2 * py + pc]

        def copy(a, k, block, to, src=None):
            return pltpu.make_async_remote_copy(
                src_ref=slot(a, *block) if src is None else src, dst_ref=slot(a, *block),
                send_sem=send_sems.at[a, k], recv_sem=recv_sems.at[a, k],
                device_id=to, device_id_type=MESH)

        mine = [pltpu.make_async_copy(ins[a], slot(a, *me), local_sems.at[a]) for a in range(n)]
        first = []
        for a in range(n):
            first.append(copy(a, 0, me, sibling, src=ins[a]))
            first += [copy(a, 1 + j, me, (*chip, c), src=ins[a]) for j, chip in enumerate(chips)]
        passed = [copy(a, 4 + j, (*chip, c), sibling) for j, chip in enumerate(chips) for a in range(n)]
        return c, me, sibling, chips, copy, mine, first, passed

    def start(ins, outs, sems):
        _, _, _, _, _, mine, first, _ = parts(ins, outs, sems)
        for cp in mine + first:
            cp.start()

    def forward(ins, outs, sems):
        c, me, _, chips, copy, _, _, passed = parts(ins, outs, sems)
        for j, chip in enumerate(chips):
            for a in range(n):
                copy(a, 1 + j, (*chip, c), me).wait_recv()
                passed[j * n + a].start()

    def finish(ins, outs, sems):
        c, me, sibling, chips, copy, mine, first, passed = parts(ins, outs, sems)
        for a in range(n):
            copy(a, 0, sibling, me).wait_recv()
            for j, chip in enumerate(chips):
                copy(a, 4 + j, (*chip, 1 - c), me).wait_recv()
        for cp in first + passed:
            cp.wait_send()
        for cp in mine:
            cp.wait()

    return _Comm(blocks, [jax.ShapeDtypeStruct((N_DEV,) + b.shape, b.dtype) for b in blocks],
                 [pltpu.SemaphoreType.DMA((n, 7)), pltpu.SemaphoreType.DMA((n, 7)), pltpu.SemaphoreType.DMA((n,))],
                 [start, forward, finish])


def _gather_comm_relayed(blocks):
    n = len(blocks)

    def parts(ins, outs, sems):
        send_sems, recv_sems, local_sems = sems
        x, y, c = _my_pos()
        me, sibling = (x, y, c), (x, y, 1 - c)
        x_nbr, y_nbr, diag = (1 - x, y, c), (x, 1 - y, c), (1 - x, 1 - y, c)

        def slot(a, dev, half=None):
            ref = outs[a].at[4 * dev[0] + 2 * dev[1] + dev[2]]
            if half is None:
                return ref
            rows = blocks[a].shape[0] // 2
            return ref.at[pl.ds(half * rows, rows)]

        def copy(a, k, block, to, half=None, src=None):
            return pltpu.make_async_remote_copy(
                src_ref=slot(a, block, half) if src is None else src, dst_ref=slot(a, block, half),
                send_sem=send_sems.at[a, k], recv_sem=recv_sems.at[a, k],
                device_id=to, device_id_type=MESH)

        mine = [pltpu.make_async_copy(ins[a], slot(a, me), local_sems.at[a]) for a in range(n)]
        return me, sibling, x_nbr, y_nbr, diag, copy, mine

    def start(ins, outs, sems):
        me, sibling, x_nbr, y_nbr, _, copy, mine = parts(ins, outs, sems)
        for cp in mine:
            cp.start()
        for a in range(n):
            for k, to in ((1, x_nbr), (2, y_nbr), (0, sibling)):
                copy(a, k, me, to, src=ins[a]).start()

    def forward(ins, outs, sems):
        me, sibling, x_nbr, y_nbr, _, copy, _ = parts(ins, outs, sems)
        for a in range(n):
            copy(a, 1, x_nbr, me).wait_recv()
            copy(a, 3, x_nbr, y_nbr, half=0).start()
            copy(a, 5, x_nbr, sibling).start()
        for a in range(n):
            copy(a, 2, y_nbr, me).wait_recv()
            copy(a, 4, y_nbr, x_nbr, half=1).start()
            copy(a, 6, y_nbr, sibling).start()

    def finish(ins, outs, sems):
        me, sibling, x_nbr, y_nbr, diag, copy, mine = parts(ins, outs, sems)
        sib = lambda dev: (dev[0], dev[1], sibling[2])
        for a in range(n):
            copy(a, 3, diag, me, half=0).wait_recv()
            copy(a, 4, diag, me, half=1).wait_recv()
            copy(a, 7, diag, sibling).start()
        for a in range(n):
            copy(a, 0, sibling, me).wait_recv()
            for k, dev in ((5, x_nbr), (6, y_nbr), (7, diag)):
                copy(a, k, sib(dev), me).wait_recv()
        for a in range(n):
            for k, block, to, half in ((0, me, sibling, None), (1, me, x_nbr, None), (2, me, y_nbr, None),
                                       (3, x_nbr, y_nbr, 0), (4, y_nbr, x_nbr, 1), (5, x_nbr, sibling, None),
                                       (6, y_nbr, sibling, None), (7, diag, sibling, None)):
                copy(a, k, block, to, half=half, src=ins[a] if block is me else None).wait_send()
        for cp in mine:
            cp.wait()

    return _Comm(blocks, [jax.ShapeDtypeStruct((N_DEV,) + b.shape, b.dtype) for b in blocks],
                 [pltpu.SemaphoreType.DMA((n, 8)), pltpu.SemaphoreType.DMA((n, 8)), pltpu.SemaphoreType.DMA((n,))],
                 [start, forward, finish])


_HBM = pl.BlockSpec(memory_space=pltpu.HBM)
_SEM = pl.BlockSpec(memory_space=pltpu.SEMAPHORE)
_SPLIT_COPY = pltpu.CompilerParams(has_side_effects=pltpu.SideEffectType.DATAFLOW_SIDE_EFFECTING)


def _chip_exchange_copies(src_ref, land_ref, sems):
    x, y, c = _my_pos()
    q_me = 2 * x + y
    pairs = []
    for j, (px, py) in enumerate([(1 - x, y), (x, 1 - y), (1 - x, 1 - y)]):
        q = 2 * px + py
        send = pltpu.make_async_remote_copy(
            src_ref=src_ref.at[q], dst_ref=land_ref.at[q_me], send_sem=sems[j], recv_sem=sems[3 + j],
            device_id=(px, py, c), device_id_type=MESH)
        recv = pltpu.make_async_remote_copy(
            src_ref=src_ref.at[q], dst_ref=land_ref.at[q], send_sem=sems[j], recv_sem=sems[3 + j],
            device_id=(x, y, c), device_id_type=MESH)
        pairs.append((send, recv))
    return pairs


def _chip_exchange_start(src, land):
    def body(src_ref, land_ref, *outs):
        sems, token = outs[:6], outs[8]
        for send, _ in _chip_exchange_copies(src_ref, land_ref, sems):
            send.start()
        token[...] = jnp.zeros_like(token)

    res = pl.pallas_call(
        body, name="scatter_in_start",
        out_shape=(pltpu.SemaphoreType.DMA(()),) * 6 + (
            pltpu.HBM(src.shape, src.dtype), pltpu.HBM(land.shape, land.dtype),
            jax.ShapeDtypeStruct((8, 128), F32)),
        in_specs=(_HBM, _HBM), out_specs=(_SEM,) * 6 + (_HBM, _HBM, pl.BlockSpec(memory_space=pltpu.VMEM)),
        input_output_aliases={0: 6, 1: 7}, compiler_params=_SPLIT_COPY,
    )(pltpu.with_memory_space_constraint(src, pltpu.HBM), pltpu.with_memory_space_constraint(land, pltpu.HBM))
    return res[:6], res[6], res[7], res[8]


def _chip_exchange_wait(sems, src_thru, land_thru, after):
    def body(src_ref, land_ref, *rest):
        for send, recv in _chip_exchange_copies(src_ref, land_ref, rest[:6]):
            send.wait_send()
            recv.wait_recv()

    return pl.pallas_call(
        body, name="scatter_in_wait",
        out_shape=(pltpu.HBM(src_thru.shape, src_thru.dtype), pltpu.HBM(land_thru.shape, land_thru.dtype)),
        in_specs=(_HBM, _HBM) + (_SEM,) * 6 + (_ANY,), out_specs=(_HBM, _HBM),
        input_output_aliases={0: 0, 1: 1}, compiler_params=_SPLIT_COPY,
    )(src_thru, land_thru, *sems, after)[1]


def _run_comm(comm, name, in_vmem=False):
    n_in, n_out = len(comm.operands), len(comm.out_shapes)

    def body(*refs):
        ins, outs, sems = refs[:n_in], refs[n_in:n_in + n_out], refs[n_in + n_out:]
        for phase in comm.phases:
            phase(ins, outs, sems)

    spec = _VMEM_WHOLE if in_vmem else _ANY
    return pl.pallas_call(
        body, name=name, out_shape=comm.out_shapes, in_specs=[spec] * n_in, out_specs=[spec] * n_out,
        scratch_shapes=comm.sems,
    )(*comm.operands)


def _carrier_call(body, comm, schedule, *, name, grid, in_specs, out_specs, out_shape, scratch_shapes, operands):
    n_in, n_out, n_scr = len(in_specs), len(out_specs), len(scratch_shapes)
    c_in, c_out = len(comm.operands), len(comm.out_shapes)

    def full_body(*refs):
        ins, refs = refs[:n_in], refs[n_in:]
        cins, refs = refs[:c_in], refs[c_in:]
        outs, refs = refs[:n_out], refs[n_out:]
        couts, refs = refs[:c_out], refs[c_out:]
        scr, csems = refs[:n_scr], refs[n_scr:]
        step = pl.program_id(0)

        def run(before):
            for (at, when_before), phase in zip(schedule, comm.phases):
                if when_before == before:
                    pl.when(step == at)(functools.partial(phase, cins, couts, csems))

        run(True)
        body(*ins, *outs, *scr)
        run(False)

    res = pl.pallas_call(
        full_body, name=name, grid=grid,
        in_specs=list(in_specs) + [_ANY] * c_in, out_specs=list(out_specs) + [_ANY] * c_out,
        out_shape=list(out_shape) + list(comm.out_shapes),
        scratch_shapes=list(scratch_shapes) + list(comm.sems),
        compiler_params=_cp(("arbitrary",)),
    )(*operands, *comm.operands)
    return res[:n_out], res[n_out:]


def _pcall(body, carried, *, name, grid, in_specs, out_specs, out_shape, scratch_shapes, operands):
    if carried is None:
        res = pl.pallas_call(body, name=name, grid=grid, in_specs=in_specs, out_specs=out_specs,
                             out_shape=out_shape, scratch_shapes=scratch_shapes,
                             compiler_params=_cp(("arbitrary",)))(*operands)
        return res, ()
    return _carrier_call(body, carried[0], carried[1], name=name, grid=grid, in_specs=in_specs,
                         out_specs=out_specs, out_shape=out_shape, scratch_shapes=scratch_shapes,
                         operands=operands)


def _all_gather(blocks, name, in_vmem):
    return _run_comm(_gather_comm(blocks), name, in_vmem)


N_CHIPS = 4


def _sibling_comm(contribs):
    n = len(contribs)

    def copies(ins, outs, sems):
        send_sems, recv_sems = sems
        x, y, c = _my_pos()
        return [pltpu.make_async_remote_copy(
            src_ref=ins[a].at[pl.ds(0, N_CHIPS), 1 - c], dst_ref=outs[a],
            send_sem=send_sems.at[a], recv_sem=recv_sems.at[a],
            device_id=(x, y, 1 - c), device_id_type=MESH) for a in range(n)]

    def start(ins, outs, sems):
        for cp in copies(ins, outs, sems):
            cp.start()

    def finish(ins, outs, sems):
        cps = copies(ins, outs, sems)
        for cp in cps:
            cp.wait_recv()
        for cp in cps:
            cp.wait_send()

    return _Comm(contribs, [jax.ShapeDtypeStruct((N_CHIPS,) + b.shape[2:], b.dtype) for b in contribs],
                 [pltpu.SemaphoreType.DMA((n,)), pltpu.SemaphoreType.DMA((n,))], [start, finish])


def _pair_sum(mine, theirs, name):
    _, _, rows, cols = mine.shape
    core = lax.axis_index("c").astype(jnp.int32).reshape(1)

    def body(c_ref, m_ref, t_ref, o_ref):
        o_ref[...] = (m_ref[...].astype(F32) + t_ref[...].astype(F32)).astype(BF16)

    return pl.pallas_call(
        body, name=name,
        grid_spec=pltpu.PrefetchScalarGridSpec(
            num_scalar_prefetch=1, grid=(N_CHIPS,),
            in_specs=[pl.BlockSpec((None, None, rows, cols), lambda q, c: (q, c[0], 0, 0)),
                      pl.BlockSpec((None, rows, cols), lambda q, c: (q, 0, 0))],
            out_specs=pl.BlockSpec((None, rows, cols), lambda q, c: (q, 0, 0))),
        out_shape=jax.ShapeDtypeStruct((N_CHIPS, rows, cols), BF16),
        compiler_params=_cp(("parallel",)),
    )(core, mine, theirs)


def _chip_comm(sums):
    n = len(sums)

    def parts(ins, outs, sems):
        send_sems, recv_sems, local_sems = sems
        x, y, c = _my_pos()
        q_me = 2 * x + y
        chips = [(1 - x, y), (x, 1 - y), (1 - x, 1 - y)]
        mine = [pltpu.make_async_copy(ins[a].at[q_me], outs[a].at[q_me], local_sems.at[a]) for a in range(n)]
        sends, recvs = [], []
        for j, (px, py) in enumerate(chips):
            for a in range(n):
                q = 2 * px + py
                sends.append(pltpu.make_async_remote_copy(
                    src_ref=ins[a].at[q], dst_ref=outs[a].at[q_me],
                    send_sem=send_sems.at[a, j], recv_sem=recv_sems.at[a, j],
                    device_id=(px, py, c), device_id_type=MESH))
                recvs.append(pltpu.make_async_remote_copy(
                    src_ref=ins[a].at[q], dst_ref=outs[a].at[q],
                    send_sem=send_sems.at[a, j], recv_sem=recv_sems.at[a, j],
                    device_id=(x, y, c), device_id_type=MESH))
        return mine, sends, recvs

    def start(ins, outs, sems):
        mine, sends, _ = parts(ins, outs, sems)
        for cp in mine + sends:
            cp.start()

    def finish(ins, outs, sems):
        mine, sends, recvs = parts(ins, outs, sems)
        for cp in recvs:
            cp.wait_recv()
        for cp in sends:
            cp.wait_send()
        for cp in mine:
            cp.wait()

    return _Comm(sums, [jax.ShapeDtypeStruct(b.shape, b.dtype) for b in sums],
                 [pltpu.SemaphoreType.DMA((n, 3)), pltpu.SemaphoreType.DMA((n, 3)), pltpu.SemaphoreType.DMA((n,))],
                 [start, finish])


def _mm_nt(a, bt, *, tm, tn, out_dtype, name, row_off=0, rows=None):
    rows = a.shape[0] if rows is None else rows
    n, k = bt.shape

    def body(a_ref, b_ref, o_ref):
        o_ref[...] = _dot_nt(a_ref[...], b_ref[...]).astype(out_dtype)

    return pl.pallas_call(
        body, name=name, grid=(rows // tm, n // tn),
        in_specs=[pl.BlockSpec((tm, k), lambda i, j: (i + row_off, 0)),
                  pl.BlockSpec((tn, k), lambda i, j: (j, 0))],
        out_specs=pl.BlockSpec((tm, tn), lambda i, j: (i, j)),
        out_shape=jax.ShapeDtypeStruct((rows, n), out_dtype),
        compiler_params=_cp(("parallel", "parallel")),
    )(a, bt)


def _mm_tn(a, b, *, tk, nk, tm, tn, out_dtype, name, a_off=0, b_off=0):
    m, n = a.shape[1], b.shape[1]

    def body(a_ref, b_ref, o_ref, acc):
        kk = pl.program_id(2)

        @pl.when(kk == 0)
        def _():
            acc[...] = jnp.zeros_like(acc)

        acc[...] += _dot_tn(a_ref[...], b_ref[...])

        @pl.when(kk == nk - 1)
        def _():
            o_ref[...] = acc[...].astype(out_dtype)

    return pl.pallas_call(
        body, name=name, grid=(m // tm, n // tn, nk),
        in_specs=[pl.BlockSpec((tk, tm), lambda i, j, kk: (kk + a_off, i)),
                  pl.BlockSpec((tk, tn), lambda i, j, kk: (kk + b_off, j))],
        out_specs=pl.BlockSpec((tm, tn), lambda i, j, kk: (i, j)),
        out_shape=jax.ShapeDtypeStruct((m, n), out_dtype),
        scratch_shapes=[pltpu.VMEM((tm, tn), F32)],
        compiler_params=_cp(("parallel", "parallel", "arbitrary")),
    )(a, b)


def _mm_tn_blocked(a, b, name, steps, carried=None):
    w = a.shape[1] // steps
    n = b.shape[1]

    def body(a_ref, b_ref, o_ref):
        o_ref[...] = _dot_tn(a_ref[...], b_ref[...]).astype(BF16)

    (out,), extra = _pcall(
        body, carried, name=name, grid=(steps,),
        in_specs=[pl.BlockSpec((S, w), lambda j: (0, j)), _full((S, n))],
        out_specs=[pl.BlockSpec((w, n), lambda j: (j, 0))],
        out_shape=[jax.ShapeDtypeStruct((a.shape[1], n), BF16)],
        scratch_shapes=[], operands=[a, b])
    return out, extra


def _prologue(cond, c_ctx, w_ada, b_cols, w_in_t, x, ctx, nw):
    rows_shape = jax.ShapeDtypeStruct((16, ADA_BLK), F32)
    big, g_cond, g_mod = _gather_comm_relayed([w_in_t]), _gather_comm([cond]), _gather_comm([rows_shape])

    def body(cond_ref, cctx_ref, wada_ref, b_ref, nw_ref, win_ref, x_ref, ctx_ref,
             g0_ref, cc_ref, mod_ref, gin_ref, h_ref, hl_ref, rows_ref, g1_ref, x_s, ctx_s, h_s, io_sems, *sems):
        s_big, s_cond, s_mod = sems[0:3], sems[3:6], sems[6:9]
        big.phases[0]([win_ref], [gin_ref], s_big)
        load_x = pltpu.make_async_copy(x_ref, x_s, io_sems.at[0])
        load_ctx = pltpu.make_async_copy(ctx_ref, ctx_s, io_sems.at[1])
        load_x.start()
        load_ctx.start()
        for phase in g_cond.phases:
            phase([cond_ref], [g0_ref], s_cond)
        cc_ref[...] = jnp.zeros_like(cc_ref)
        for j in range(N_DEV):
            cc_ref[j:j + 1, :] = g0_ref[j, 0:1, :]
        cc_ref[N_DEV:N_DEV + 1, :] = cctx_ref[...]
        cv = cc_ref[...]
        rows_ref[...] = _dot_f32(cv * _sigmoid(cv), wada_ref[...]) + b_ref[...]
        for phase in g_mod.phases:
            phase([rows_ref], [g1_ref], s_mod)
        x_pos, y_pos, c_pos = _my_pos()
        me = 4 * x_pos + 2 * y_pos + c_pos
        mine = jnp.concatenate([g1_ref[j, pl.ds(me, 1), :] for j in range(N_DEV)], axis=1)
        shared = jnp.concatenate([g1_ref[j, N_DEV:N_DEV + 1, :] for j in range(N_DEV)], axis=1)
        for k in range(6):
            mod_ref[k:k + 1, :] = mine[:, k * D:(k + 1) * D]
        mod_ref[6:7, :] = shared[:, 0:D]
        mod_ref[7:8, :] = shared[:, D:2 * D]
        load_ctx.wait()
        load_x.wait()
        h_s[pl.ds(0, L), :] = _rms_mod(ctx_s[...], nw_ref[...], mod_ref[6:7, :], mod_ref[7:8, :]).astype(BF16)

        def norm_tile(i, carry):
            r0 = pl.multiple_of(i * TM, TM)
            h_s[pl.ds(L + r0, TM), :] = _rms_mod(
                x_s[pl.ds(r0, TM), :], nw_ref[...], mod_ref[0:1, :], mod_ref[1:2, :]).astype(BF16)
            return carry

        lax.fori_loop(0, N_LAT_TILES, norm_tile, 0)
        stores = [pltpu.make_async_copy(h_s, h_ref, io_sems.at[2]),
                  pltpu.make_async_copy(h_s.at[pl.ds(L, S)], hl_ref, io_sems.at[3])]
        for cp in stores:
            cp.start()
        big.phases[1]([win_ref], [gin_ref], s_big)
        big.phases[2]([win_ref], [gin_ref], s_big)
        for cp in stores:
            cp.wait()

    return pl.pallas_call(
        body, name="prologue",
        in_specs=[_VMEM_WHOLE] * 5 + [_ANY] * 3, out_specs=[_VMEM_WHOLE] * 3 + [_ANY] * 3,
        out_shape=[g_cond.out_shapes[0], jax.ShapeDtypeStruct((16, D), F32), jax.ShapeDtypeStruct((8, D), F32),
                   big.out_shapes[0], jax.ShapeDtypeStruct((T, D), BF16), jax.ShapeDtypeStruct((S, D), BF16)],
        scratch_shapes=[pltpu.VMEM((16, ADA_BLK), F32), pltpu.VMEM((N_DEV, 16, ADA_BLK), F32),
                        pltpu.VMEM((S, D), F32), pltpu.VMEM((L, D), F32), pltpu.VMEM((T, D), BF16),
                        pltpu.SemaphoreType.DMA((4,))] + big.sems + g_cond.sems + g_mod.sems,
        compiler_params=_cp(),
    )(cond, c_ctx, w_ada, b_cols, nw, w_in_t, x, ctx)


def _ada_grads(cc, dm_cols, w_ada):
    def body(c_ref, dm_ref, w_ref, gw_ref, dsc_ref):
        cv = c_ref[...]
        sc = cv * _sigmoid(cv)
        dm = dm_ref[...]
        gw_ref[...] = _dot_f32(sc, dm, dot=_dot_tn)
        dsc_ref[...] = _dot_f32(dm, w_ref[...], dot=_dot_nt)

    return pl.pallas_call(
        body, name="ada_grads",
        in_specs=[_VMEM_WHOLE] * 3, out_specs=[_VMEM_WHOLE] * 2,
        out_shape=[jax.ShapeDtypeStruct((D, ADA_BLK), F32), jax.ShapeDtypeStruct((16, D), F32)],
        compiler_params=_cp(),
    )(cc, dm_cols, w_ada)


def _lat(i):
    return jnp.maximum(i - 1, 0)


def _rms_mod(xv, nw, sh, sc):
    rstd = lax.rsqrt(jnp.mean(xv * xv, axis=-1, keepdims=True) + EPS)
    return (xv * rstd * nw) * (1.0 + sc) + sh


def _rms_mod_bwd(xv, nw, sc, dh):
    rstd = lax.rsqrt(jnp.mean(xv * xv, axis=-1, keepdims=True) + EPS)
    xhat = xv * rstd
    dn = dh * (1.0 + sc)
    dxhat = dn * nw
    dx = rstd * (dxhat - xhat * jnp.mean(dxhat * xhat, axis=-1, keepdims=True))
    return (dx, jnp.sum(dh, axis=0, keepdims=True), jnp.sum(dh * (xhat * nw), axis=0, keepdims=True),
            jnp.sum(dn * xhat, axis=0, keepdims=True))


def _chunk_masks(reverse):
    row = lax.broadcasted_iota(jnp.int32, (TM, TM), 0)
    col = lax.broadcasted_iota(jnp.int32, (TM, TM), 1)
    same = (row // CHUNK) == (col // CHUNK)
    tri = same & ((col >= row) if reverse else (col <= row))
    return same, tri


def _chunk_order(i, reverse):
    if not reverse:
        return i
    return jnp.where(i < N_CTX_CHUNKS, N_CTX_CHUNKS - 1 - i, N_CHUNKS + N_CTX_CHUNKS - 1 - i)


def _decay_terms(z, lb, same01, tri01):
    f = lb + (1.0 - lb) * _sigmoid(z)
    g = jnp.log(f)
    g2 = jnp.concatenate(_split2(g), axis=1)
    b2 = _dot(tri01, g2)
    t2 = _dot(same01, g2)
    return f, 1.0 - f, b2[:, :HG_DIM] + b2[:, HG_DIM:], t2[:, :HG_DIM] + t2[:, HG_DIM:]


def _chunk_outer(a, b):
    n = TM // CHUNK
    return jnp.einsum('ncv,nck->nvk', a.reshape(n, CHUNK, HG_DIM), b.reshape(n, CHUNK, HG_DIM),
                      preferred_element_type=F32)


def _hgrn_fwd(p_a, lbl, carried=None):
    cpt = TM // CHUNK

    def body(p_ref, lbl_ref, o_ref, st_ref, qd_s, kd_s, u_s, v_s, ebt_s):
        masks = [_chunk_masks(d == 1) for d in (0, 1)]
        same01 = jnp.where(masks[0][0], 1.0, 0.0).astype(BF16)
        tri = [m[1] for m in masks]
        tri01 = [jnp.where(t, 1.0, 0.0).astype(BF16) for t in tri]
        lb = [_sigmoid(lbl_ref[d][0:1, :] - lbl_ref[d][1:2, :]) for d in (0, 1)]

        def prep(r, carry):
            r0 = pl.multiple_of(r * TM, TM)
            vb = p_ref[pl.ds(r0, TM), 2 * HG_DIM:3 * HG_DIM].astype(BF16)
            v_s[pl.ds(r0, TM), :] = vb
            for d in (0, 1):
                z = p_ref[pl.ds(r0, TM), d * HG_DIM:(d + 1) * HG_DIM]
                _, k, b, bt = _decay_terms(z, lb[d], same01, tri01[d])
                u_s[d, pl.ds(r * cpt, cpt)] = _chunk_outer(vb, (k * jnp.exp(bt - b)).astype(BF16))
                ebt_s[d, pl.ds(r0, TM), :] = jnp.exp(bt)

                @pl.when(r >= 1)
                def _():
                    rl = pl.multiple_of(r0 - L, TM)
                    qr = p_ref[pl.ds(r0, TM), 3 * HG_DIM:4 * HG_DIM]
                    q = qr * _sigmoid(qr) * HG_DIM ** -0.5
                    qd_s[d, pl.ds(rl, TM), :] = (q * jnp.exp(b)).astype(BF16)
                    kd_s[d, pl.ds(rl, TM), :] = (k * jnp.exp(-b)).astype(BF16)

            return carry

        lax.fori_loop(0, N_TILES, prep, 0)

        def scan(i, sts):
            new = []
            for d in (0, 1):
                nn = _chunk_order(i, d == 1)
                c0 = pl.multiple_of(nn * CHUNK, CHUNK)
                st_ref[d, nn] = sts[d].astype(BF16)
                new.append(sts[d] * ebt_s[d, pl.ds(c0, 1), :] + u_s[d, nn])
            return tuple(new)

        zero = jnp.zeros((HG_DIM, HG_DIM), F32)
        lax.fori_loop(0, N_CHUNKS, scan, (zero, zero))

        def outp(r, carry):
            r0 = pl.multiple_of(r * TM, TM)
            vb = v_s[pl.ds(r0 + L, TM), :]
            o = jnp.zeros((TM, HG_DIM), F32)
            for d in (0, 1):
                qd = qd_s[d, pl.ds(r0, TM), :]
                a = jnp.where(tri[d], _dot_nt(qd, kd_s[d, pl.ds(r0, TM), :]), 0.0)
                stb = st_ref[d, pl.ds(N_CTX_CHUNKS + r * cpt, cpt)]
                inter = jnp.einsum('nck,nvk->ncv', qd.reshape(cpt, CHUNK, HG_DIM), stb,
                                   preferred_element_type=F32)
                o = o + _dot(a.astype(BF16), vb) + inter.reshape(TM, HG_DIM)
            o_ref[pl.ds(r0, TM), :] = o
            return carry

        lax.fori_loop(0, N_LAT_TILES, outp, 0, unroll=2)

    return _pcall(
        body, carried, name="hgrn_fwd", grid=(HG_HEADS,),
        in_specs=[pl.BlockSpec((T, 4 * HG_DIM), lambda h: (0, h)),
                  pl.BlockSpec((2, 2, HG_DIM), lambda h: (0, 0, h))],
        out_specs=[pl.BlockSpec((S, HG_DIM), lambda h: (0, h)),
                   pl.BlockSpec((2, None, N_CHUNKS, HG_DIM, HG_DIM), lambda h: (0, h, 0, 0, 0))],
        out_shape=[jax.ShapeDtypeStruct((S, HGW), F32),
                   jax.ShapeDtypeStruct((2, HG_HEADS, N_CHUNKS, HG_DIM, HG_DIM), BF16)],
        scratch_shapes=[pltpu.VMEM((2, S, HG_DIM), BF16), pltpu.VMEM((2, S, HG_DIM), BF16),
                        pltpu.VMEM((2, N_CHUNKS, HG_DIM, HG_DIM), F32), pltpu.VMEM((T, HG_DIM), BF16),
                        pltpu.VMEM((2, T, HG_DIM), F32)],
        operands=[p_a, lbl])


def _hgrn_bwd(p_a, lbl, d_o, st, carried=None):
    cpt = TM // CHUNK

    def rows(r):
        return r * TM if isinstance(r, int) else pl.multiple_of(r * TM, TM)

    def body(p_ref, lbl_ref, do_ref, st_ref, dp_ref, dlb_ref, b_s, bt_s, dbt_s, qd_s, dst_s, w_s):
        masks = [_chunk_masks(d == 1) for d in (0, 1)]
        same01 = jnp.where(masks[0][0], 1.0, 0.0).astype(BF16)
        tri = [m[1] for m in masks]
        tri01 = [jnp.where(t, 1.0, 0.0).astype(BF16) for t in tri]
        later01 = [tri01[1], tri01[0]]
        lb = [_sigmoid(lbl_ref[d][0:1, :] - lbl_ref[d][1:2, :]) for d in (0, 1)]

        def prep_tile(r, latent):
            r0 = rows(r)
            for d in (0, 1):
                z = p_ref[pl.ds(r0, TM), d * HG_DIM:(d + 1) * HG_DIM]
                _, _, b, bt = _decay_terms(z, lb[d], same01, tri01[d])
                b_s[d, pl.ds(r0, TM), :] = b
                bt_s[d, pl.ds(r0, TM), :] = bt
                if latent:
                    rl = pl.multiple_of(r0 - L, TM)
                    qr = p_ref[pl.ds(r0, TM), 3 * HG_DIM:4 * HG_DIM]
                    qd = (qr * _sigmoid(qr) * HG_DIM ** -0.5 * jnp.exp(b)).astype(BF16)
                    qd_s[d, pl.ds(rl, TM), :] = qd
                    w_s[d, pl.ds(r * cpt, cpt)] = _chunk_outer(
                        do_ref[pl.ds(rl, TM), :].astype(BF16), qd).astype(BF16)

        prep_tile(0, False)
        w_s[:, pl.ds(0, N_CTX_CHUNKS)] = jnp.zeros((2, N_CTX_CHUNKS, HG_DIM, HG_DIM), BF16)

        def prep(r, carry):
            prep_tile(r, True)
            return carry

        lax.fori_loop(1, N_TILES, prep, 0, unroll=2)

        def rscan(j, dsts):
            i = N_CHUNKS - 1 - j
            new = []
            for d in (0, 1):
                nn = _chunk_order(i, d == 1)
                c0 = pl.multiple_of(nn * CHUNK, CHUNK)
                dst_s[d, nn] = dsts[d].astype(BF16)
                after = st_ref[d, _chunk_order(jnp.minimum(i + 1, N_CHUNKS - 1), d == 1)].astype(F32)
                dbt_s[d, pl.ds(c0, CHUNK), :] = jnp.broadcast_to(
                    jnp.sum(after * dsts[d], axis=0, keepdims=True), (CHUNK, HG_DIM))
                new.append(dsts[d] * jnp.exp(bt_s[d, pl.ds(c0, 1), :]) + w_s[d, nn].astype(F32))
            return tuple(new)

        zero = jnp.zeros((HG_DIM, HG_DIM), F32)
        lax.fori_loop(0, N_CHUNKS, rscan, (zero, zero))

        def grad_tile(r, latent):
            r0 = rows(r)
            vb = p_ref[pl.ds(r0, TM), 2 * HG_DIM:3 * HG_DIM].astype(BF16)
            dv = jnp.zeros((TM, HG_DIM), F32)
            dq = jnp.zeros((TM, HG_DIM), F32)
            dlbs = []
            if latent:
                rl = pl.multiple_of(r0 - L, TM)
                qr = p_ref[pl.ds(r0, TM), 3 * HG_DIM:4 * HG_DIM]
                sq = _sigmoid(qr)
                do = do_ref[pl.ds(rl, TM), :].astype(BF16)
                da_full = _dot_nt(do, vb)
            for d in (0, 1):
                z = p_ref[pl.ds(r0, TM), d * HG_DIM:(d + 1) * HG_DIM]
                sz = _sigmoid(z)
                f = lb[d] + (1.0 - lb[d]) * sz
                k = 1.0 - f
                b = b_s[d, pl.ds(r0, TM), :]
                e2 = jnp.exp(bt_s[d, pl.ds(r0, TM), :] - b)
                dstb = dst_s[d, pl.ds(r * cpt, cpt)]
                kd2 = k * e2
                dkd2 = jnp.einsum('ncv,nvk->nck', vb.reshape(cpt, CHUNK, HG_DIM), dstb,
                                  preferred_element_type=F32).reshape(TM, HG_DIM)
                dv = dv + jnp.einsum('nck,nvk->ncv', kd2.astype(BF16).reshape(cpt, CHUNK, HG_DIM), dstb,
                                     preferred_element_type=F32).reshape(TM, HG_DIM)
                dk = dkd2 * e2
                db = -(kd2 * dkd2)
                if latent:
                    eb = jnp.exp(b)
                    enb = jnp.exp(-b)
                    qdf = qr * sq * HG_DIM ** -0.5 * eb
                    kdf = k * enb
                    qd = qd_s[d, pl.ds(rl, TM), :]
                    kd = kdf.astype(BF16)
                    a = jnp.where(tri[d], _dot_nt(qd, kd), 0.0).astype(BF16)
                    da = jnp.where(tri[d], da_full, 0.0).astype(BF16)
                    stb = st_ref[d, pl.ds(r * cpt, cpt)]
                    dqd = _dot(da, kd) + jnp.einsum(
                        'ncv,nvk->nck', do.reshape(cpt, CHUNK, HG_DIM), stb,
                        preferred_element_type=F32).reshape(TM, HG_DIM)
                    dkd = _dot_tn(da, qd)
                    dv = dv + _dot_tn(a, do)
                    dk = dk + dkd * enb
                    db = db + qdf * dqd - kdf * dkd
                    dq = dq + dqd * eb
                dg = _dot_lhs01(later01[d], db) + dbt_s[d, pl.ds(r0, TM), :]
                df = dg / f - dk
                dp_ref[pl.ds(r0, TM), d * HG_DIM:(d + 1) * HG_DIM] = (
                    df * (1.0 - lb[d]) * sz * (1.0 - sz)).astype(BF16)
                dlbs.append(jnp.sum(df * (1.0 - sz), axis=0, keepdims=True))
            dp_ref[pl.ds(r0, TM), 2 * HG_DIM:3 * HG_DIM] = dv.astype(BF16)
            if latent:
                dq = dq * (HG_DIM ** -0.5) * (sq * (1.0 + qr * (1.0 - sq)))
            dp_ref[pl.ds(r0, TM), 3 * HG_DIM:4 * HG_DIM] = dq.astype(BF16)
            return dlbs

        dlb_ctx = grad_tile(0, False)

        def grads(r, acc):
            t = grad_tile(r, True)
            return (acc[0] + t[0], acc[1] + t[1])

        dlb = lax.fori_loop(1, N_TILES, grads, (dlb_ctx[0], dlb_ctx[1]))
        dlb_ref[0:1, :] = dlb[0]
        dlb_ref[1:2, :] = dlb[1]

    return _pcall(
        body, carried, name="hgrn_bwd", grid=(HG_HEADS,),
        in_specs=[pl.BlockSpec((T, 4 * HG_DIM), lambda h: (0, h)),
                  pl.BlockSpec((2, 2, HG_DIM), lambda h: (0, 0, h)),
                  pl.BlockSpec((S, HG_DIM), lambda h: (0, h)),
                  pl.BlockSpec((2, None, N_CHUNKS, HG_DIM, HG_DIM), lambda h: (0, h, 0, 0, 0))],
        out_specs=[pl.BlockSpec((T, 4 * HG_DIM), lambda h: (0, h)),
                   pl.BlockSpec((2, HG_DIM), lambda h: (0, h))],
        out_shape=[jax.ShapeDtypeStruct((T, WA), BF16), jax.ShapeDtypeStruct((2, HGW), F32)],
        scratch_shapes=[pltpu.VMEM((2, T, HG_DIM), F32), pltpu.VMEM((2, T, HG_DIM), F32),
                        pltpu.VMEM((2, T, HG_DIM), F32), pltpu.VMEM((2, S, HG_DIM), BF16),
                        pltpu.VMEM((2, N_CHUNKS, HG_DIM, HG_DIM), BF16),
                        pltpu.VMEM((2, N_CHUNKS, HG_DIM, HG_DIM), BF16)],
        operands=[p_a, lbl, d_o, st])


def _rope_tables():
    t = np.arange(S)
    inv = ROPE_THETA ** (-np.arange(0, 32, 2, dtype=np.float64) / 32)
    lane = np.arange(64)
    pos = np.where(lane[None, :] < 32, (t // GRID_W)[:, None], (t % GRID_W)[:, None]).astype(np.float64)
    ang = pos * inv[(lane % 32) % 16][None, :]
    sign = np.where((lane % 32) < 16, -1.0, 1.0)[None, :]
    cos = np.tile(np.cos(ang), (1, 2)).astype(np.float32)
    sin = np.tile(np.sin(ang) * sign, (1, 2)).astype(np.float32)
    return jnp.asarray(cos), jnp.asarray(sin)


def _rope_partner(v):
    lane = lax.broadcasted_iota(jnp.int32, (1, 128), 1)
    first = (lane % 32) < 16
    slabs = []
    for j in range(v.shape[1] // 128):
        s = v[:, 128 * j:128 * (j + 1)]
        slabs.append(jnp.where(first, pltpu.roll(s, 112, 1), pltpu.roll(s, 16, 1)))
    return slabs[0] if len(slabs) == 1 else jnp.concatenate(slabs, axis=1)


def _group_ones(width, group):
    r = lax.broadcasted_iota(jnp.int32, (width, width), 0)
    c = lax.broadcasted_iota(jnp.int32, (width, width), 1)
    return jnp.where((r // group) == (c // group), 1.0, 0.0).astype(BF16)


def _group_mean(v, ones01, group):
    hi = v.astype(BF16)
    lo = (v - hi.astype(F32)).astype(BF16)
    return (_dot(hi, ones01) + _dot(lo, ones01)) * (1.0 / group)


def _rep_matrix():
    r = lax.broadcasted_iota(jnp.int32, (KVW, ATW), 0)
    c = lax.broadcasted_iota(jnp.int32, (KVW, ATW), 1)
    return jnp.where(r == HEAD_DIM * (c // 256) + c % HEAD_DIM, 1.0, 0.0).astype(BF16)


def _tile_lanes(v, reps):
    return jnp.concatenate([v] * reps, axis=1)


def _prep_fwd(p_b, o, cos, sin, hnw, qnw, knw):
    def body(p_ref, o_ref, cos_ref, sin_ref, hnw_ref, qnw_ref, knw_ref, y_ref, q_ref, k_ref, v_ref):
        i = pl.program_id(0)
        rep = _rep_matrix()
        ones_k = _group_ones(KVW, HEAD_DIM)
        kr = p_ref[:, 1024:1152]
        krstd = lax.rsqrt(_group_mean(kr * kr, ones_k, HEAD_DIM) + EPS)
        kn = kr * krstd * knw_ref[...]
        v_ref[...] = _dot(p_ref[:, 1152:1280].astype(BF16), rep).astype(BF16)

        @pl.when(i == 0)
        def _():
            k_ref[...] = _dot(kn.astype(BF16), rep).astype(BF16)

        @pl.when(i > 0)
        def _():
            cs, sn = cos_ref[...], sin_ref[...]
            kro = kn * cs + _rope_partner(kn) * sn
            k_ref[...] = _dot(kro.astype(BF16), rep).astype(BF16)
            qr = p_ref[:, 512:1024]
            qrstd = lax.rsqrt(_group_mean(qr * qr, _group_ones(ATW, HEAD_DIM), HEAD_DIM) + EPS)
            qn = qr * qrstd * qnw_ref[...]
            qro = qn * _tile_lanes(cs, 4) + _rope_partner(qn) * _tile_lanes(sn, 4)
            q_ref[...] = (qro * HEAD_DIM ** -0.5).astype(BF16)
            ys = []
            for h in range(HG_HEADS):
                oh = o_ref[:, HG_DIM * h:HG_DIM * (h + 1)]
                gh = p_ref[:, HG_DIM * h:HG_DIM * (h + 1)]
                rstd = lax.rsqrt(jnp.mean(oh * oh, axis=-1, keepdims=True) + EPS)
                ys.append(oh * rstd * hnw_ref[...] * (gh * _sigmoid(gh)))
            y_ref[...] = jnp.concatenate(ys, axis=1).astype(BF16)

    return pl.pallas_call(
        body, name="prep_fwd", grid=(N_TILES,),
        in_specs=[pl.BlockSpec((TM, WB), lambda i: (i, 0)),
                  pl.BlockSpec((TM, HGW), lambda i: (_lat(i), 0)),
                  pl.BlockSpec((TM, 128), lambda i: (_lat(i), 0)),
                  pl.BlockSpec((TM, 128), lambda i: (_lat(i), 0)),
                  _full((1, HG_DIM)), _full((1, ATW)), _full((1, KVW))],
        out_specs=[pl.BlockSpec((TM, HGW), lambda i: (_lat(i), 0)),
                   pl.BlockSpec((TM, ATW), lambda i: (_lat(i), 0)),
                   pl.BlockSpec((TM, ATW), lambda i: (i, 0)),
                   pl.BlockSpec((TM, ATW), lambda i: (i, 0))],
        out_shape=[jax.ShapeDtypeStruct((S, HGW), BF16), jax.ShapeDtypeStruct((S, ATW), BF16),
                   jax.ShapeDtypeStruct((T, ATW), BF16), jax.ShapeDtypeStruct((T, ATW), BF16)],
        compiler_params=_cp(("arbitrary",)),
    )(p_b, o, cos, sin, hnw, qnw, knw)


def _prep_bwd(p_b, o, cos, sin, hnw, qnw, knw, dy_hg, dq, dk_rep, dv_rep, carried=None):
    def body(p_ref, o_ref, cos_ref, sin_ref, hnw_ref, qnw_ref, knw_ref, dy_ref, dq_ref, dk_ref, dv_ref,
             dp_ref, do_ref, acc_ref):
        i = pl.program_id(0)

        @pl.when(i == 0)
        def _():
            acc_ref[...] = jnp.zeros_like(acc_ref)

        rep = _rep_matrix()
        ones_k = _group_ones(KVW, HEAD_DIM)

        def fold(v):
            hi = v.astype(BF16)
            lo = (v - hi.astype(F32)).astype(BF16)
            return _dot_nt(hi, rep) + _dot_nt(lo, rep)

        kr = p_ref[:, 1024:1152]
        krstd = lax.rsqrt(_group_mean(kr * kr, ones_k, HEAD_DIM) + EPS)
        khat = kr * krstd
        kw = knw_ref[...]
        dkro = fold(dk_ref[...])
        dv = fold(dv_ref[...])

        def k_back(dkn):
            dkhat = dkn * kw
            dkr = krstd * (dkhat - khat * _group_mean(dkhat * khat, ones_k, HEAD_DIM))
            acc_ref[2:3, 0:KVW] += jnp.sum(dkn * khat, axis=0, keepdims=True)
            dp_ref[:, 1024:1152] = dkr.astype(BF16)
            dp_ref[:, 1152:1280] = dv.astype(BF16)

        @pl.when(i == 0)
        def _():
            k_back(dkro)
            dp_ref[:, 0:1024] = jnp.zeros((TM, 1024), BF16)

        @pl.when(i > 0)
        def _():
            cs, sn = cos_ref[...], sin_ref[...]
            k_back(dkro * cs + _rope_partner(dkro * sn))
            ones_q = _group_ones(ATW, HEAD_DIM)
            qr = p_ref[:, 512:1024]
            qrstd = lax.rsqrt(_group_mean(qr * qr, ones_q, HEAD_DIM) + EPS)
            qhat = qr * qrstd
            dqro = dq_ref[...] * HEAD_DIM ** -0.5
            dqn = dqro * _tile_lanes(cs, 4) + _rope_partner(dqro * _tile_lanes(sn, 4))
            dqhat = dqn * qnw_ref[...]
            dqr = qrstd * (dqhat - qhat * _group_mean(dqhat * qhat, ones_q, HEAD_DIM))
            acc_ref[1:2, :] += jnp.sum(dqn * qhat, axis=0, keepdims=True)
            dp_ref[:, 512:1024] = dqr.astype(BF16)
            dws = jnp.zeros((1, HG_DIM), F32)
            for h in range(HG_HEADS):
                sl = slice(HG_DIM * h, HG_DIM * (h + 1))
                oh, gh, dy = o_ref[:, sl], p_ref[:, sl], dy_ref[:, sl]
                rstd = lax.rsqrt(jnp.mean(oh * oh, axis=-1, keepdims=True) + EPS)
                ohat = oh * rstd
                sg = _sigmoid(gh)
                dp_ref[:, sl] = (dy * (ohat * hnw_ref[...]) * (sg * (1.0 + gh * (1.0 - sg)))).astype(BF16)
                dn = dy * (gh * sg)
                dws = dws + jnp.sum(dn * ohat, axis=0, keepdims=True)
                dohat = dn * hnw_ref[...]
                do_ref[:, sl] = rstd * (dohat - ohat * jnp.mean(dohat * ohat, axis=-1, keepdims=True))
            acc_ref[0:1, 0:HG_DIM] += dws

    return _pcall(
        body, carried, name="prep_bwd", grid=(N_TILES,),
        in_specs=[pl.BlockSpec((TM, WB), lambda i: (i, 0)),
                  pl.BlockSpec((TM, HGW), lambda i: (_lat(i), 0)),
                  pl.BlockSpec((TM, 128), lambda i: (_lat(i), 0)),
                  pl.BlockSpec((TM, 128), lambda i: (_lat(i), 0)),
                  _full((1, HG_DIM)), _full((1, ATW)), _full((1, KVW)),
                  pl.BlockSpec((TM, HGW), lambda i: (_lat(i), 0)),
                  pl.BlockSpec((TM, ATW), lambda i: (_lat(i), 0)),
                  pl.BlockSpec((TM, ATW), lambda i: (i, 0)),
                  pl.BlockSpec((TM, ATW), lambda i: (i, 0))],
        out_specs=[pl.BlockSpec((TM, WB), lambda i: (i, 0)),
                   pl.BlockSpec((TM, HGW), lambda i: (_lat(i), 0)),
                   _full((8, ATW))],
        out_shape=[jax.ShapeDtypeStruct((T, WB), BF16), jax.ShapeDtypeStruct((S, HGW), F32),
                   jax.ShapeDtypeStruct((8, ATW), F32)],
        scratch_shapes=[], operands=[p_b, o, cos, sin, hnw, qnw, knw, dy_hg, dq, dk_rep, dv_rep])


NEG = -1e30
_CTX_BLOCKS = L // BLOCK


def _attn_window_specs():
    prev = pl.BlockSpec((BLOCK, ATW), lambda i: (jnp.maximum(i - 1, 0) + _CTX_BLOCKS, 0))
    own = pl.BlockSpec((BLOCK, ATW), lambda i: (i + _CTX_BLOCKS, 0))
    nxt = pl.BlockSpec((BLOCK, ATW), lambda i: (jnp.minimum(i + 1, N_BLOCKS - 1) + _CTX_BLOCKS, 0))
    return [prev, own, nxt, _full((L, ATW))]


def _attn_valid(i, heads, context):
    n_keys = 3 * BLOCK + (L if context else 0)
    qi = lax.broadcasted_iota(jnp.int32, (heads * BLOCK, n_keys), 0) % BLOCK
    kj = lax.broadcasted_iota(jnp.int32, (heads * BLOCK, n_keys), 1)
    window = ((jnp.abs(kj - BLOCK - qi) <= BLOCK) & ((kj >= BLOCK) | (i > 0))
              & ((kj < 2 * BLOCK) | (i < N_BLOCKS - 1)))
    return window | (kj >= 3 * BLOCK)


def _stack_heads(qg):
    lane = lax.broadcasted_iota(jnp.int32, (1, 256), 1) // HEAD_DIM
    return jnp.concatenate([jnp.where(lane == g, qg, jnp.zeros_like(qg)) for g in range(4)], axis=0)


def _unstack_heads(v4):
    lane = lax.broadcasted_iota(jnp.int32, (1, 256), 1) // HEAD_DIM
    out = jnp.where(lane == 0, v4[0:BLOCK], 0.0)
    for g in range(1, 4):
        out = out + jnp.where(lane == g, v4[g * BLOCK:(g + 1) * BLOCK], 0.0)
    return out


def _sink_rows(sink_ref, hk):
    return jnp.concatenate(
        [jnp.broadcast_to(sink_ref[0:1, 4 * hk + g:4 * hk + g + 1], (BLOCK, 1)) for g in range(4)], axis=0)


def _attn_fwd(q, k_rep, v_rep, sinks, carried=None):
    def body(q_ref, kp, ko, kn, kc, vp, vo, vn, vc, sink_ref, y_ref, lse_ref):
        i = pl.program_id(0)
        valid = _attn_valid(i, 1, True)
        lane8 = lax.broadcasted_iota(jnp.int32, (1, ATT_HEADS), 1)
        head_of_lane = lax.broadcasted_iota(jnp.int32, (1, 256), 1) // HEAD_DIM
        lse_out = jnp.zeros((BLOCK, ATT_HEADS), F32)
        for hk in range(KV_HEADS):
            sl = slice(256 * hk, 256 * (hk + 1))
            qg = q_ref[:, sl]
            keys = jnp.concatenate([kp[:, sl], ko[:, sl], kn[:, sl], kc[:, sl]], axis=0)
            vals = jnp.concatenate([vp[:, sl], vo[:, sl], vn[:, sl], vc[:, sl]], axis=0)
            yg = jnp.zeros((BLOCK, 256), F32)
            for g in range(4):
                q1 = jnp.where(head_of_lane == g, qg, jnp.zeros_like(qg))
                s = jnp.where(valid, _dot_nt(q1, keys), NEG)
                sink = sink_ref[0:1, 4 * hk + g:4 * hk + g + 1]
                m = jnp.maximum(jnp.max(s, axis=1, keepdims=True), sink)
                p = jnp.exp(s - m)
                den = jnp.sum(p, axis=1, keepdims=True) + jnp.exp(sink - m)
                o1 = _dot(p.astype(BF16), vals) * (1.0 / den)
                yg = yg + jnp.where(head_of_lane == g, o1, 0.0)
                lse_out = lse_out + jnp.where(lane8 == 4 * hk + g, m + jnp.log(den), 0.0)
            y_ref[:, sl] = yg.astype(BF16)
        lse_ref[...] = lse_out

    return _pcall(
        body, carried, name="attn_fwd", grid=(N_BLOCKS,),
        in_specs=[pl.BlockSpec((BLOCK, ATW), lambda i: (i, 0))] + _attn_window_specs()
        + _attn_window_specs() + [_full((1, ATT_HEADS))],
        out_specs=[pl.BlockSpec((BLOCK, ATW), lambda i: (i, 0)),
                   pl.BlockSpec((BLOCK, ATT_HEADS), lambda i: (i, 0))],
        out_shape=[jax.ShapeDtypeStruct((S, ATW), BF16), jax.ShapeDtypeStruct((S, ATT_HEADS), F32)],
        scratch_shapes=[],
        operands=[q, k_rep, k_rep, k_rep, k_rep, v_rep, v_rep, v_rep, v_rep, sinks])


def _attn_bwd(q, k_rep, v_rep, sinks, y_at, lse, dy, carried=None):
    def body(q_ref, kp, ko, kn, kc, vp, vo, vn, vc, sink_ref, y_ref, lse_ref, dy_ref,
             dq_ref, dk_ref, dv_ref, dsink_ref, dk_acc, dv_acc):
        i = pl.program_id(0)

        @pl.when(i == 0)
        def _():
            dk_acc[...] = jnp.zeros_like(dk_acc)
            dv_acc[...] = jnp.zeros_like(dv_acc)
            dk_ref[pl.ds(0, L), :] = jnp.zeros((L, ATW), F32)
            dv_ref[pl.ds(0, L), :] = jnp.zeros((L, ATW), F32)
            dsink_ref[...] = jnp.zeros_like(dsink_ref)

        valid = _attn_valid(i, 4, False)
        lane8 = lax.broadcasted_iota(jnp.int32, (1, ATT_HEADS), 1)
        w0 = pl.multiple_of(i * BLOCK, BLOCK)
        dsink = jnp.zeros((1, ATT_HEADS), F32)
        for hk in range(KV_HEADS):
            sl = slice(256 * hk, 256 * (hk + 1))
            q4 = _stack_heads(q_ref[:, sl])
            do4f = _stack_heads(dy_ref[:, sl])
            o4 = _stack_heads(y_ref[:, sl]).astype(F32)
            do4 = do4f.astype(BF16)
            kl = jnp.concatenate([kp[:, sl], ko[:, sl], kn[:, sl]], axis=0)
            vl = jnp.concatenate([vp[:, sl], vo[:, sl], vn[:, sl]], axis=0)
            lse4 = jnp.concatenate(
                [jnp.sum(jnp.where(lane8 == 4 * hk + g, lse_ref[...], 0.0), axis=1, keepdims=True)
                 for g in range(4)], axis=0)
            p_loc = jnp.where(valid, jnp.exp(_dot_nt(q4, kl) - lse4), 0.0)
            p_ctx = jnp.exp(_dot_nt(q4, kc[:, sl]) - lse4)
            delta = jnp.sum(do4f * o4, axis=1, keepdims=True)
            ds_loc = (p_loc * (_dot_nt(do4, vl) - delta)).astype(BF16)
            ds_ctx = (p_ctx * (_dot_nt(do4, vc[:, sl]) - delta)).astype(BF16)
            dq_ref[:, sl] = _unstack_heads(_dot(ds_loc, kl) + _dot(ds_ctx, kc[:, sl]))
            dk_acc[pl.ds(w0, 3 * BLOCK), sl] += _dot_tn(ds_loc, q4)
            dv_acc[pl.ds(w0, 3 * BLOCK), sl] += _dot_tn(p_loc.astype(BF16), do4)
            dk_ref[pl.ds(0, L), sl] += _dot_tn(ds_ctx, q4)
            dv_ref[pl.ds(0, L), sl] += _dot_tn(p_ctx.astype(BF16), do4)
            p_sink = jnp.exp(_sink_rows(sink_ref, hk) - lse4)
            for g in range(4):
                rows = slice(g * BLOCK, (g + 1) * BLOCK)
                dsink = dsink + jnp.where(lane8 == 4 * hk + g,
                                          -jnp.sum(p_sink[rows] * delta[rows], axis=0, keepdims=True), 0.0)
        dsink_ref[...] += dsink

        @pl.when(i == N_BLOCKS - 1)
        def _():
            dk_ref[pl.ds(L, S), :] = dk_acc[pl.ds(BLOCK, S), :]
            dv_ref[pl.ds(L, S), :] = dv_acc[pl.ds(BLOCK, S), :]

    row_q = pl.BlockSpec((BLOCK, ATW), lambda i: (i, 0))
    return _pcall(
        body, carried, name="attn_bwd", grid=(N_BLOCKS,),
        in_specs=[row_q] + _attn_window_specs() + _attn_window_specs()
        + [_full((1, ATT_HEADS)), row_q, pl.BlockSpec((BLOCK, ATT_HEADS), lambda i: (i, 0)), row_q],
        out_specs=[row_q, _full((T, ATW)), _full((T, ATW)), _full((1, ATT_HEADS))],
        out_shape=[jax.ShapeDtypeStruct((S, ATW), F32), jax.ShapeDtypeStruct((T, ATW), F32),
                   jax.ShapeDtypeStruct((T, ATW), F32), jax.ShapeDtypeStruct((1, ATT_HEADS), F32)],
        scratch_shapes=[pltpu.VMEM((S + 2 * BLOCK, ATW), F32), pltpu.VMEM((S + 2 * BLOCK, ATW), F32)],
        operands=[q, k_rep, k_rep, k_rep, k_rep, v_rep, v_rep, v_rep, v_rep, sinks, y_at, lse, dy])


def _merge_fwd(y_hg, y_at, p_c, x, w_bh, w_ba, w_out, g1, nfw, sh2, sc2, carried=None):
    def body(yh_ref, ya_ref, g_ref, x_ref, wbh_ref, wba_ref, wo_ref, g1_ref, nfw_ref, sh_ref, sc_ref,
             mx_ref, r_ref, x1_ref, h2_ref):
        a = _dot_nt(yh_ref[...], wbh_ref[...])
        b = _dot_nt(ya_ref[...], wba_ref[...])
        mixed = (_sigmoid(g_ref[:, :D]) * a + _sigmoid(g_ref[:, D:]) * b).astype(BF16)
        r = _dot(mixed, wo_ref[...])
        x1 = x_ref[...] + g1_ref[...] * r
        mx_ref[...] = mixed
        r_ref[...] = r
        x1_ref[...] = x1
        h2_ref[...] = _rms_mod(x1, nfw_ref[...], sh_ref[...], sc_ref[...]).astype(BF16)

    row = lambda w: pl.BlockSpec((TM, w), lambda i: (i, 0))
    vec = _full((1, D))
    return _pcall(
        body, carried, name="merge_fwd", grid=(N_LAT_TILES,),
        in_specs=[row(HGW), row(ATW), row(WC), row(D), _VMEM_WHOLE, _VMEM_WHOLE, _VMEM_WHOLE,
                  vec, vec, vec, vec],
        out_specs=[row(D)] * 4,
        out_shape=[jax.ShapeDtypeStruct((S, D), dt) for dt in (BF16, F32, F32, BF16)],
        scratch_shapes=[], operands=[y_hg, y_at, p_c, x, w_bh, w_ba, w_out, g1, nfw, sh2, sc2])


def _merge_bwd(dx1, r, y_hg, y_at, p_c, w_bh, w_ba, w_out, g1, carried=None):
    def body(dx_ref, r_ref, yh_ref, ya_ref, g_ref, wbh_ref, wba_ref, wo_ref, g1_ref,
             dr_ref, da_ref, db_ref, dg_ref, dyh_ref, dya_ref, acc_ref):
        @pl.when(pl.program_id(0) == 0)
        def _():
            acc_ref[...] = jnp.zeros_like(acc_ref)

        dx1v = dx_ref[...]
        acc_ref[0:1, :] += jnp.sum(dx1v * r_ref[...], axis=0, keepdims=True)
        dr = (g1_ref[...] * dx1v).astype(BF16)
        dr_ref[...] = dr
        dmix = _dot_nt(dr, wo_ref[...])
        sh, sa = _sigmoid(g_ref[:, :D]), _sigmoid(g_ref[:, D:])
        da = (dmix * sh).astype(BF16)
        db = (dmix * sa).astype(BF16)
        da_ref[...] = da
        db_ref[...] = db
        dg_ref[:, :D] = (dmix * _dot_nt(yh_ref[...], wbh_ref[...]) * sh * (1.0 - sh)).astype(BF16)
        dg_ref[:, D:] = (dmix * _dot_nt(ya_ref[...], wba_ref[...]) * sa * (1.0 - sa)).astype(BF16)
        dyh_ref[...] = _dot(da, wbh_ref[...])
        dya_ref[...] = _dot(db, wba_ref[...])

    row = lambda w: pl.BlockSpec((TM, w), lambda i: (i, 0))
    return _pcall(
        body, carried, name="merge_bwd", grid=(N_LAT_TILES,),
        in_specs=[row(D), row(D), row(HGW), row(ATW), row(WC), _VMEM_WHOLE, _VMEM_WHOLE, _VMEM_WHOLE,
                  _full((1, D))],
        out_specs=[row(D), row(D), row(D), row(WC), row(HGW), row(ATW), _full((8, D))],
        out_shape=[jax.ShapeDtypeStruct((S, D), BF16), jax.ShapeDtypeStruct((S, D), BF16),
                   jax.ShapeDtypeStruct((S, D), BF16), jax.ShapeDtypeStruct((S, WC), BF16),
                   jax.ShapeDtypeStruct((S, HGW), F32), jax.ShapeDtypeStruct((S, ATW), F32),
                   jax.ShapeDtypeStruct((8, D), F32)],
        scratch_shapes=[], operands=[dx1, r, y_hg, y_at, p_c, w_bh, w_ba, w_out, g1])


def _ffn_fused(x1, h2, tgt, w_gate, w_up, w_down, g2, nfw, sc2):
    def body(x1_ref, h2_ref, t_ref, wg_ref, wu_ref, wd_ref, g2_ref, nfw_ref, sc_ref,
             act_ref, dgt_ref, dup_ref, df_ref, dx_ref, acc_ref, gs, us):
        @pl.when(pl.program_id(0) == 0)
        def _():
            acc_ref[...] = jnp.zeros_like(acc_ref)

        h2 = h2_ref[...]
        whole = lambda w_ref: w_ref[...].reshape(D_FF, D)
        tile = lambda j: slice(j * FF_TILE, (j + 1) * FF_TILE)
        for j in range(N_FF_TILES):
            g = _dot_nt(h2, wg_ref[j])
            u = _dot_nt(h2, wu_ref[j])
            gs[j] = g
            us[j] = u
            act_ref[:, tile(j)] = (g * _sigmoid(g) * u).astype(BF16)
        f = _dot(act_ref[...], whole(wd_ref))
        x1v = x1_ref[...]
        g2 = g2_ref[...]
        diff = x1v + g2 * f - t_ref[...]
        dy = diff * (1.0 / D)
        df = (g2 * dy).astype(BF16)
        df_ref[...] = df
        dact_all = _dot_nt(df, whole(wd_ref))
        for j in range(N_FF_TILES):
            g, u = gs[j], us[j]
            sg = _sigmoid(g)
            dact = dact_all[:, tile(j)]
            dgt_ref[:, tile(j)] = (dact * u * (sg * (1.0 + g * (1.0 - sg)))).astype(BF16)
            dup_ref[:, tile(j)] = (dact * (g * sg)).astype(BF16)
        dh2 = _dot(dgt_ref[...], whole(wg_ref)) + _dot(dup_ref[...], whole(wu_ref))
        dx, dsh, dsc, dnw = _rms_mod_bwd(x1v, nfw_ref[...], sc_ref[...], dh2)
        dx_ref[...] = dy + dx
        acc_ref[0:1, :] += dsh
        acc_ref[1:2, :] += dsc
        acc_ref[2:3, :] += dnw
        acc_ref[3:4, :] += jnp.sum(dy * f, axis=0, keepdims=True)
        acc_ref[4:5, :] += 0.5 * jnp.sum(jnp.sum(diff * diff, axis=1, keepdims=True), axis=0,
                                         keepdims=True) * (1.0 / D)

    row = lambda dt_w: pl.BlockSpec((TM, dt_w), lambda i: (i, 0))
    blk = row(D_FF)
    vec = _full((1, D))
    return pl.pallas_call(
        body, name="ffn_fused", grid=(N_LAT_TILES,),
        in_specs=[row(D), row(D), row(D), _VMEM_WHOLE, _VMEM_WHOLE, _VMEM_WHOLE, vec, vec, vec],
        out_specs=[blk, blk, blk, row(D), row(D), _full((8, D))],
        out_shape=[jax.ShapeDtypeStruct((S, D_FF), BF16)] * 3
        + [jax.ShapeDtypeStruct((S, D), BF16), jax.ShapeDtypeStruct((S, D), F32),
           jax.ShapeDtypeStruct((8, D), F32)],
        scratch_shapes=[pltpu.VMEM((N_FF_TILES, TM, FF_TILE), F32), pltpu.VMEM((N_FF_TILES, TM, FF_TILE), F32)],
        compiler_params=_cp(("arbitrary",)),
    )(x1, h2, tgt, w_gate, w_up, w_down, g2, nfw, sc2)


def _proj_bc(h_all, w_b, w_c, carried=None):
    def body(h_ref, wb_ref, wc_ref, pb_ref, pc_ref):
        h = h_ref[...]
        pb_ref[...] = _dot_nt(h, wb_ref[...])

        @pl.when(pl.program_id(0) > 0)
        def _():
            pc_ref[...] = _dot_nt(h, wc_ref[...])

    return _pcall(
        body, carried, name="proj_bc", grid=(N_TILES,),
        in_specs=[pl.BlockSpec((TM, D), lambda i: (i, 0)), _VMEM_WHOLE, _VMEM_WHOLE],
        out_specs=[pl.BlockSpec((TM, WB), lambda i: (i, 0)), pl.BlockSpec((TM, WC), lambda i: (_lat(i), 0))],
        out_shape=[jax.ShapeDtypeStruct((T, WB), F32), jax.ShapeDtypeStruct((S, WC), F32)],
        scratch_shapes=[], operands=[h_all, w_b, w_c])


def _input_bwd(dp_a, dp_b, dp_c, w_a, w_b, w_c, ctx, x, dx1, nw, sh, sc, carried=None):
    def body(da_ref, db_ref, dc_ref, wa_ref, wb_ref, wc_ref, ctx_ref, x_ref, dx1_ref, nw_ref, sh_ref,
             sc_ref, gx_ref, acc_ref):
        i = pl.program_id(0)

        @pl.when(i == 0)
        def _():
            acc_ref[...] = jnp.zeros_like(acc_ref)

        dh = _dot(da_ref[...], wa_ref[...]) + _dot(db_ref[...], wb_ref[...])

        @pl.when(i == 0)
        def _():
            _, dsh, dsc, dnw = _rms_mod_bwd(ctx_ref[...], nw_ref[...], sc_ref[0:1, :], dh)
            acc_ref[3:4, :] += dsh
            acc_ref[4:5, :] += dsc
            acc_ref[2:3, :] += dnw

        @pl.when(i > 0)
        def _():
            dhl = dh + _dot(dc_ref[...], wc_ref[...])
            dx, dsh, dsc, dnw = _rms_mod_bwd(x_ref[...], nw_ref[...], sc_ref[1:2, :], dhl)
            gx_ref[...] = dx1_ref[...] + dx
            acc_ref[0:1, :] += dsh
            acc_ref[1:2, :] += dsc
            acc_ref[2:3, :] += dnw

    lat = lambda w: pl.BlockSpec((TM, w), lambda i: (_lat(i), 0))
    return _pcall(
        body, carried, name="input_bwd", grid=(N_TILES,),
        in_specs=[pl.BlockSpec((TM, WA), lambda i: (i, 0)), pl.BlockSpec((TM, WB), lambda i: (i, 0)),
                  lat(WC), _VMEM_WHOLE, _VMEM_WHOLE, _VMEM_WHOLE, _full((TM, D)), lat(D), lat(D),
                  _full((1, D)), _full((2, D)), _full((2, D))],
        out_specs=[lat(D), _full((8, D))],
        out_shape=[jax.ShapeDtypeStruct((S, D), F32), jax.ShapeDtypeStruct((8, D), F32)],
        scratch_shapes=[], operands=[dp_a, dp_b, dp_c, w_a, w_b, w_c, ctx, x, dx1, nw, sh, sc])


_C1 = 1.0 - ADAM_B1 ** ADAM_STEP
_C2 = 1.0 - ADAM_B2 ** ADAM_STEP


def _adamw_math(w, g, m, v):
    m = ADAM_B1 * m + (1.0 - ADAM_B1) * g
    v = ADAM_B2 * v + (1.0 - ADAM_B2) * (g * g)
    m_hat = m / _C1
    v_hat = v / _C2
    delta = -ADAM_LR * (m_hat / (jnp.sqrt(v_hat) + ADAM_EPS) + ADAM_WD * w)
    return delta, m, v


def _adamw_sharded(terms, w, m, v, name, tr, extra=None):
    rows, cols = w.shape

    def body(*refs):
        t_ref, w_ref, m_ref, v_ref = refs[:4]
        g_ref, d_ref, nm_ref, nv_ref = refs[-4:]
        g = t_ref[0].astype(F32)
        for s in range(1, N_CHIPS):
            g = g + t_ref[s].astype(F32)
        if extra is not None:
            g = g + refs[4][...].astype(F32)
        g_ref[...] = g
        d_ref[...], nm_ref[...], nv_ref[...] = _adamw_math(w_ref[...], g, m_ref[...], v_ref[...])

    blk = pl.BlockSpec((tr, cols), lambda i: (i, 0))
    return pl.pallas_call(
        body, name=name, grid=(rows // tr,),
        in_specs=[pl.BlockSpec((N_CHIPS, tr, cols), lambda i: (0, i, 0)), blk, blk, blk]
        + ([blk] if extra is not None else []),
        out_specs=[blk] * 4,
        out_shape=[jax.ShapeDtypeStruct((rows, cols), F32)] * 4,
        compiler_params=_cp(("parallel",)),
    )(terms, w, m, v, *([extra] if extra is not None else []))


def _adamw_plain(g, w, m, v, name, tr=None):
    def body(g_ref, w_ref, m_ref, v_ref, d_ref, nm_ref, nv_ref):
        d_ref[...], nm_ref[...], nv_ref[...] = _adamw_math(w_ref[...], g_ref[...], m_ref[...], v_ref[...])

    if tr is None:
        return pl.pallas_call(
            body, name=name, in_specs=[_VMEM_WHOLE] * 4, out_specs=[_VMEM_WHOLE] * 3,
            out_shape=[jax.ShapeDtypeStruct(w.shape, F32)] * 3,
            compiler_params=_cp(),
        )(g, w, m, v)
    blk = pl.BlockSpec((tr, w.shape[1]), lambda i: (i, 0))
    return pl.pallas_call(
        body, name=name, grid=(w.shape[0] // tr,), in_specs=[blk] * 4, out_specs=[blk] * 3,
        out_shape=[jax.ShapeDtypeStruct(w.shape, F32)] * 3,
        compiler_params=_cp(("parallel",)),
    )(g, w, m, v)


SMALL_ROWS = 16
R_DMOD, R_DCTX, R_NMIX, R_NFFN, R_MISC, R_DLB, R_BADA01 = 0, 6, 8, 9, 10, 11, 13
M_HNW, M_QNW, M_KNW, M_SINK, M_LOSS = 0, 128, 256, 384, 512


def _pack_small(acc_in, acc_mg, acc_ffn, acc_prep, dsink, dlb):
    def body(in_ref, mg_ref, ff_ref, pp_ref, ds_ref, dlb_ref, o_ref):
        o_ref[...] = jnp.zeros_like(o_ref)
        o_ref[0:2, :] = in_ref[0:2, :]
        o_ref[2:3, :] = mg_ref[0:1, :]
        o_ref[3:5, :] = ff_ref[0:2, :]
        o_ref[5:6, :] = ff_ref[3:4, :]
        o_ref[6:8, :] = in_ref[3:5, :]
        o_ref[8:9, :] = in_ref[2:3, :]
        o_ref[9:10, :] = ff_ref[2:3, :]
        o_ref[10:11, M_HNW:M_HNW + HG_DIM] = pp_ref[0:1, 0:HG_DIM]
        r = lax.broadcasted_iota(jnp.int32, (ATW, 128), 0)
        c = lax.broadcasted_iota(jnp.int32, (ATW, 128), 1)
        fold = jnp.where((r % HEAD_DIM == c) & (c < HEAD_DIM), 1.0, 0.0).astype(BF16)
        qk = jnp.concatenate([pp_ref[1:2, :], pp_ref[2:3, :], jnp.zeros((6, ATW), F32)], axis=0)
        folded = _dot_exact_rhs01(qk, fold)
        o_ref[10:11, M_QNW:M_QNW + 128] = folded[0:1, :]
        o_ref[10:11, M_KNW:M_KNW + 128] = folded[1:2, :]
        o_ref[10:11, M_SINK:M_SINK + ATT_HEADS] = ds_ref[...]
        o_ref[10:11, M_LOSS:M_LOSS + 128] = ff_ref[4:5, 0:128]
        o_ref[11:13, 0:HGW] = dlb_ref[...]

    return pl.pallas_call(
        body, name="pack_small", in_specs=[_VMEM_WHOLE] * 6, out_specs=_VMEM_WHOLE,
        out_shape=jax.ShapeDtypeStruct((SMALL_ROWS, D), F32), compiler_params=_cp(),
    )(acc_in, acc_mg, acc_ffn, acc_prep, dsink, dlb)


def _sum_small(gathered):
    def body(g_ref, o_ref):
        tot = g_ref[0]
        for s in range(1, N_DEV):
            tot = tot + g_ref[s]
        o_ref[...] = tot
        o_ref[R_BADA01:R_BADA01 + 2, :] = tot[0:2, :] + tot[R_DCTX:R_DCTX + 2, :]

    return pl.pallas_call(
        body, name="sum_small", in_specs=[_VMEM_WHOLE], out_specs=_VMEM_WHOLE,
        out_shape=jax.ShapeDtypeStruct((SMALL_ROWS, D), F32), compiler_params=_cp(),
    )(gathered)


_REP_NAMES = ("b_ada", "c_ctx", "norm_mix_w", "norm_ffn_w", "hgrn_norm_w", "q_norm_w", "k_norm_w", "attn_sinks")


def _adamw_replicated(tot, g_c_ctx, ws, ms, vs):
    n = len(_REP_NAMES)

    def body(*refs):
        tot_ref, gc_ref = refs[0], refs[1]
        w_refs, m_refs, v_refs = refs[2:2 + n], refs[2 + n:2 + 2 * n], refs[2 + 2 * n:2 + 3 * n]
        outs = refs[2 + 3 * n:]
        row = lambda r: tot_ref[r:r + 1, :]
        misc = row(R_MISC)
        grads = [jnp.concatenate([row(R_BADA01), row(R_BADA01 + 1)] + [row(k) for k in range(2, 6)], axis=1),
                 gc_ref[...], row(R_NMIX), row(R_NFFN),
                 misc[:, M_HNW:M_HNW + HG_DIM], misc[:, M_QNW:M_QNW + HEAD_DIM],
                 misc[:, M_KNW:M_KNW + HEAD_DIM], misc[:, M_SINK:M_SINK + ATT_HEADS]]
        for k in range(n):
            outs[k][...] = grads[k]
            outs[n + k][...], outs[2 * n + k][...], outs[3 * n + k][...] = _adamw_math(
                w_refs[k][...], grads[k], m_refs[k][...], v_refs[k][...])

    shapes = [jax.ShapeDtypeStruct(w.shape, F32) for w in ws]
    return pl.pallas_call(
        body, name="adamw_replicated", in_specs=[_VMEM_WHOLE] * (2 + 3 * n), out_specs=[_VMEM_WHOLE] * (4 * n),
        out_shape=shapes * 4, compiler_params=_cp(),
    )(tot, g_c_ctx, *ws, *ms, *vs)


def _lb_grads(dlb, lbl):
    def body(d_ref, l_ref, o_ref):
        for d in (0, 1):
            ll = l_ref[d]
            lb = _sigmoid(ll[0:1, :] - ll[1:2, :])
            t = d_ref[d:d + 1, :] * lb * (1.0 - lb)
            o_ref[d, 0:1, :] = t
            o_ref[d, 1:2, :] = -t

    return pl.pallas_call(
        body, name="lb_grads", in_specs=[_VMEM_WHOLE] * 2, out_specs=_VMEM_WHOLE,
        out_shape=jax.ShapeDtypeStruct((2, 2, HGW), F32), compiler_params=_cp(),
    )(dlb, lbl)


def _c_ctx_grad(terms, c_ctx):
    def body(t_ref, c_ref, o_ref):
        tot = t_ref[0, 8:9, :]
        for s in range(1, N_DEV):
            tot = tot + t_ref[s, 8:9, :]
        cv = c_ref[...]
        sg = _sigmoid(cv)
        o_ref[...] = tot * (sg * (1.0 + cv * (1.0 - sg)))

    return pl.pallas_call(
        body, name="c_ctx_grad", in_specs=[_VMEM_WHOLE] * 2, out_specs=_VMEM_WHOLE,
        out_shape=jax.ShapeDtypeStruct((1, D), F32), compiler_params=_cp(),
    )(terms, c_ctx)


def _in_perm():
    fz, bz, inp, kk, vv, qhg, ghg, qat, gates = 0, 512, 1024, 1536, 1664, 1792, 2304, 2816, 3328
    cols = []
    for h in range(HG_HEADS):
        for base in (fz, bz, inp, qhg):
            cols += list(range(base + 128 * h, base + 128 * (h + 1)))
    cols += list(range(ghg, ghg + 512)) + list(range(qat, qat + 512))
    cols += list(range(kk, kk + 128)) + list(range(vv, vv + 128))
    cols += list(range(gates, gates + 2048))
    return np.asarray(cols, np.int32)


_PERM = _in_perm()


_PIECES = {"a": (0, WA, 128), "b": (WA, WB, 256), "c": (WA + WB, WC, 256)}


def _block_table(piece):
    lo, n, blk = _PIECES[piece]
    starts = [int(_PERM[r]) for r in range(lo, lo + n, blk)]
    assert all(s % blk == 0 and np.array_equal(_PERM[r:r + blk], np.arange(s, s + blk))
               for s, r in zip(starts, range(lo, lo + n, blk)))
    return jnp.asarray([s // blk for s in starts], jnp.int32), blk


def _pick_row_blocks(x, table, blk, name):
    cols = x.shape[1]

    def body(t_ref, x_ref, o_ref):
        o_ref[...] = x_ref[...]

    return pl.pallas_call(
        body, name=name,
        grid_spec=pltpu.PrefetchScalarGridSpec(
            num_scalar_prefetch=1, grid=(table.shape[0],),
            in_specs=[pl.BlockSpec((blk, cols), lambda i, t: (t[i], 0))],
            out_specs=pl.BlockSpec((blk, cols), lambda i, t: (i, 0))),
        out_shape=jax.ShapeDtypeStruct((table.shape[0] * blk, cols), x.dtype),
        compiler_params=_cp(("arbitrary",)),
    )(table, x)


def _place_row_blocks(x, table, blk, into, out_rows, name):
    cols = x.shape[1]

    def body(t_ref, x_ref, *rest):
        rest[-1][...] = x_ref[...]

    operands, in_specs, aliases = [table, x], [pl.BlockSpec((blk, cols), lambda i, t: (i, 0))], {}
    if into is not None:
        operands.append(into)
        in_specs.append(_ANY)
        aliases = {2: 0}
    return pl.pallas_call(
        body, name=name,
        grid_spec=pltpu.PrefetchScalarGridSpec(
            num_scalar_prefetch=1, grid=(table.shape[0],), in_specs=in_specs,
            out_specs=pl.BlockSpec((blk, cols), lambda i, t: (t[i], 0))),
        out_shape=jax.ShapeDtypeStruct((out_rows, cols), x.dtype),
        input_output_aliases=aliases,
        compiler_params=_cp(("arbitrary",)),
    )(*operands)


def _local_step(x2, ctx2, h_all, h_lat, tgt, lbl, sh_in, sc_in, gate1, sh2, sc2, gate2, norm_mix_w, norm_ffn_w,
                hgrn_norm_w, q_norm_w, k_norm_w, attn_sinks, w_a, w_b, w_c, s_bh, s_ba, s_out,
                s_gate, s_up, s_down):
    first_last = lambda n: [(0, True), (n - 1, False)]
    p_a = _mm_nt(h_all, w_a, tm=T, tn=512, out_dtype=F32, name="proj_a")
    (o, st), (g_gate, g_bh, g_ba) = _hgrn_fwd(
        p_a, lbl, (_gather_comm_relayed([s_gate, s_bh, s_ba]),
                   [(0, True), (HG_HEADS - 2, True), (HG_HEADS - 1, False)]))
    (p_b, p_c), (g_out,) = _proj_bc(
        h_all, w_b, w_c, (_gather_comm_relayed([s_out]), [(0, True), (N_TILES - 4, True), (N_TILES - 1, False)]))
    cos, sin = _rope_tables()
    qnw_t, knw_t = jnp.tile(q_norm_w, (1, ATT_HEADS)), jnp.tile(k_norm_w, (1, KV_HEADS))
    y_hg, qn, k_rep, v_rep = _prep_fwd(p_b, o, cos, sin, hgrn_norm_w, qnw_t, knw_t)
    (y_at, lse), (g_up, g_down) = _attn_fwd(
        qn, k_rep, v_rep, attn_sinks,
        (_gather_comm_relayed([s_up, s_down]), [(0, True), (N_BLOCKS - 6, True), (N_BLOCKS - 1, False)]))
    w_bh, w_ba, w_o = g_bh.reshape(D, HGW), g_ba.reshape(D, ATW), g_out.reshape(D, D)
    (mixed, r, x1, h2), _ = _merge_fwd(
        y_hg, y_at, p_c, x2, w_bh, w_ba, w_o, gate1, norm_ffn_w, sh2, sc2)
    g_gate, g_up, g_down = [g.reshape(N_FF_TILES, FF_TILE, D) for g in (g_gate, g_up, g_down)]

    act, d_gate, d_up, d_f, dx1, acc_ffn = _ffn_fused(x1, h2, tgt, g_gate, g_up, g_down, gate2,
                                                      norm_ffn_w, sc2)
    by_chip = lambda t: t.reshape((N_CHIPS, 2) + t.shape[1:])
    ff_by_chip = lambda t: t.reshape(N_CHIPS, 2, FF_BLK, D)
    t_down, _ = _mm_tn_blocked(act, d_f, "grad_down", N_FF_TILES)
    t_down = ff_by_chip(t_down)
    t_gate, (f_down,) = _mm_tn_blocked(d_gate, h2, "grad_gate", N_FF_HALVES,
                                       (_sibling_comm([t_down]), first_last(N_FF_HALVES)))
    t_gate = ff_by_chip(t_gate)
    t_up, (f_gate,) = _mm_tn_blocked(d_up, h2, "grad_up", N_FF_HALVES,
                                     (_sibling_comm([t_gate]), first_last(N_FF_HALVES)))
    t_up = ff_by_chip(t_up)

    (d_r, d_a, d_b, dp_c, dy_hg, dy_at, acc_mg), (f_up,) = _merge_bwd(
        dx1, r, y_hg, y_at, p_c, w_bh, w_ba, w_o, gate1, (_sibling_comm([t_up]), first_last(N_LAT_TILES)))
    c_down, c_gate, c_up = [_pair_sum(t, f, "pair_sum_" + nm) for t, f, nm in
                            ((t_down, f_down, "down"), (t_gate, f_gate, "gate"), (t_up, f_up, "up"))]
    t_out = _mm_tn(mixed, d_r, tk=1024, nk=2, tm=1024, tn=1024, out_dtype=BF16, name="grad_out")
    t_bh = _mm_tn(d_a, y_hg, tk=2048, nk=1, tm=1024, tn=512, out_dtype=BF16, name="grad_bh")
    t_ba = _mm_tn(d_b, y_at, tk=2048, nk=1, tm=1024, tn=512, out_dtype=BF16, name="grad_ba")
    t_bh, t_ba, t_out = [by_chip(t.reshape(N_DEV, D // N_DEV, t.shape[1])) for t in (t_bh, t_ba, t_out)]
    (dq, dk_rep, dv_rep, dsink), (r_up,) = _attn_bwd(
        qn, k_rep, v_rep, attn_sinks, y_at, lse, dy_at, (_chip_comm([c_up]), first_last(N_BLOCKS)))
    (dp_b, d_o, acc_prep), (f_bh, f_ba, f_out) = _prep_bwd(
        p_b, o, cos, sin, hgrn_norm_w, qnw_t, knw_t, dy_hg, dq, dk_rep, dv_rep,
        (_sibling_comm([t_bh, t_ba, t_out]), first_last(N_TILES)))
    c_bh, c_ba, c_out = [_pair_sum(t, f, "pair_sum_" + nm) for t, f, nm in
                         ((t_bh, f_bh, "bh"), (t_ba, f_ba, "ba"), (t_out, f_out, "out"))]
    (dp_a, dlb), (r_bh, r_ba, r_out, r_down, r_gate) = _hgrn_bwd(
        p_a, lbl, d_o, st, (_chip_comm([c_bh, c_ba, c_out, c_down, c_gate]), first_last(HG_HEADS)))
    t_a = _mm_tn(dp_a, h_all, tk=T, nk=1, tm=1024, tn=1024, out_dtype=BF16, name="grad_in_a")
    t_b = _mm_tn(dp_b, h_all, tk=T, nk=1, tm=640, tn=1024, out_dtype=BF16, name="grad_in_b")
    t_c = _mm_tn(dp_c, h_lat, tk=1024, nk=2, tm=1024, tn=1024, out_dtype=BF16, name="grad_in_c")
    t_in = None
    for piece, nm in ((t_a, "a"), (t_b, "b"), (t_c, "c")):
        t_in = _place_row_blocks(piece, *_block_table(nm), t_in, IN_COLS, "order_terms_" + nm)
    t_in = by_chip(t_in.reshape(N_DEV, IN_BLK, D))
    (f_in,) = _run_comm(_sibling_comm([t_in]), "scatter_in_sibling")
    c_in = _pair_sum(t_in, f_in, "pair_sum_in")
    sems, c_in, land, token = _chip_exchange_start(c_in, jnp.zeros(c_in.shape, c_in.dtype))
    (grad_x, acc_in), _ = _input_bwd(dp_a, dp_b, dp_c, w_a, w_b, w_c, ctx2, x2, dx1,
                                     norm_mix_w + token[0, 0], sh_in, sc_in)
    small = _pack_small(acc_in, acc_mg, acc_ffn, acc_prep, dsink, dlb)
    return grad_x, small, [r_bh, r_ba, r_out, r_gate, r_up, r_down], (sems, c_in, land)


def kernel(x, c, ctx, c_ctx, w_ada, b_ada, norm_mix_w, norm_ffn_w, w_in, hgrn_lb_logits, hgrn_norm_w, q_norm_w, k_norm_w, attn_sinks, w_branch_hgrn, w_branch_attn, w_out, w_ffn_gate, w_ffn_up, w_ffn_down, loss_target, m_c_ctx, m_w_ada, m_b_ada, m_norm_mix_w, m_norm_ffn_w, m_w_in, m_hgrn_lb_logits, m_hgrn_norm_w, m_q_norm_w, m_k_norm_w, m_attn_sinks, m_w_branch_hgrn, m_w_branch_attn, m_w_out, m_w_ffn_gate, m_w_ffn_up, m_w_ffn_down, v_c_ctx, v_w_ada, v_b_ada, v_norm_mix_w, v_norm_ffn_w, v_w_in, v_hgrn_lb_logits, v_hgrn_norm_w, v_q_norm_w, v_k_norm_w, v_attn_sinks, v_w_branch_hgrn, v_w_branch_attn, v_w_out, v_w_ffn_gate, v_w_ffn_up, v_w_ffn_down):
    me = 4 * lax.axis_index("x") + 2 * lax.axis_index("y") + lax.axis_index("c")
    x2, ctx2, tgt = x[0], ctx[0], loss_target[0]
    w_ada2, w_in2 = w_ada[0], w_in[0]

    cond = jnp.zeros((8, D), F32).at[0].set(c[0]).at[1, :256].set(hgrn_lb_logits.reshape(256))
    b_cols = lax.dynamic_slice(b_ada, (0, me * ADA_BLK), (1, ADA_BLK))
    g0, cc, mod, g_in, h_all, h_lat = _prologue(cond, c_ctx.reshape(1, D), w_ada2, b_cols, w_in2.T.astype(BF16),
                                         x2, ctx2, norm_mix_w)
    lbl = jnp.transpose(g0[:, 1, :256].reshape(N_DEV, 2, 2, 64), (1, 2, 0, 3)).reshape(2, 2, HGW)
    sh1, sc1, gate1, sh2, sc2, gate2 = [mod[k:k + 1] for k in range(6)]
    sh_in = jnp.concatenate([mod[6:7], sh1], axis=0)
    sc_in = jnp.concatenate([mod[7:8], sc1], axis=0)

    shards = [w_branch_hgrn[0].T, w_branch_attn[0].T, w_out[0], w_ffn_gate[0].T, w_ffn_up[0].T, w_ffn_down[0]]
    w_in_t = g_in.reshape(IN_COLS, D)
    w_a, w_b, w_c = [_pick_row_blocks(w_in_t, *_block_table(nm), "order_w_" + nm) for nm in "abc"]

    grad_x, small, (r_bh, r_ba, r_out, r_gate, r_up, r_down), pending_in = _local_step(
        x2, ctx2, h_all, h_lat, tgt, lbl, sh_in, sc_in, gate1, sh2, sc2, gate2, norm_mix_w, norm_ffn_w, hgrn_norm_w,
        q_norm_w, k_norm_w, attn_sinks, w_a, w_b, w_c, *[s.astype(BF16) for s in shards])

    big = {}
    for nm, rr, ww, mm, vv, tr, transposed in (
            ("w_branch_hgrn", r_bh, w_branch_hgrn[0], m_w_branch_hgrn[0], v_w_branch_hgrn[0], 128, True),
            ("w_branch_attn", r_ba, w_branch_attn[0], m_w_branch_attn[0], v_w_branch_attn[0], 128, True),
            ("w_out", r_out, w_out[0], m_w_out[0], v_w_out[0], 128, False),
            ("w_ffn_gate", r_gate, w_ffn_gate[0], m_w_ffn_gate[0], v_w_ffn_gate[0], 176, True),
            ("w_ffn_up", r_up, w_ffn_up[0], m_w_ffn_up[0], v_w_ffn_up[0], 176, True),
            ("w_ffn_down", r_down, w_ffn_down[0], m_w_ffn_down[0], v_w_ffn_down[0], 176, False)):
        if transposed:
            res = _adamw_sharded(rr, ww.T, mm.T, vv.T, "adamw_" + nm, tr)
            big[nm] = [t.T[None] for t in res]
        else:
            big[nm] = [t[None] for t in _adamw_sharded(rr, ww, mm, vv, "adamw_" + nm, tr)]

    (g2,) = _all_gather([small], "gather_small", True)
    tot = _sum_small(g2)
    dm = jnp.zeros((16, 6 * D), F32).at[:8].set(g2[:, R_DMOD:R_DMOD + 6, :].reshape(N_DEV, 6 * D))
    dm = dm.at[8, :2 * D].set(tot[R_DCTX:R_DCTX + 2].reshape(2 * D))
    dm_cols = lax.dynamic_slice(dm, (0, me * ADA_BLK), (16, ADA_BLK))
    g_w_ada, dsc_term = _ada_grads(cc, dm_cols, w_ada2)
    (g3,) = _all_gather([dsc_term], "gather_cctx", True)
    g_c_ctx = _c_ctx_grad(g3, c_ctx.reshape(1, D))
    g_lbl = _lb_grads(tot[R_DLB:R_DLB + 2, :HGW], lbl)
    g_lb_mine = lax.dynamic_slice(g_lbl, (0, 0, me * 64), (2, 2, 64))
    misc = tot[R_MISC]
    loss = misc[M_LOSS]

    rep_out = _adamw_replicated(
        tot, g_c_ctx,
        [b_ada, c_ctx.reshape(1, D), norm_mix_w, norm_ffn_w, hgrn_norm_w, q_norm_w, k_norm_w, attn_sinks],
        [m_b_ada, m_c_ctx.reshape(1, D), m_norm_mix_w, m_norm_ffn_w, m_hgrn_norm_w, m_q_norm_w, m_k_norm_w,
         m_attn_sinks],
        [v_b_ada, v_c_ctx.reshape(1, D), v_norm_mix_w, v_norm_ffn_w, v_hgrn_norm_w, v_q_norm_w, v_k_norm_w,
         v_attn_sinks])
    rep = []
    for kind in range(4):
        vals = dict(zip(_REP_NAMES, rep_out[kind * len(_REP_NAMES):(kind + 1) * len(_REP_NAMES)]))
        vals["c_ctx"] = vals["c_ctx"].reshape(D)
        rep.append(vals)

    sems, c_in, land = pending_in
    d_ada, nm_ada, nv_ada = _adamw_plain(g_w_ada, w_ada2, m_w_ada[0], v_w_ada[0], "adamw_w_ada", tr=256)
    land = _chip_exchange_wait(sems, c_in, land, d_ada)
    own = lax.dynamic_index_in_dim(c_in, 2 * lax.axis_index("x") + lax.axis_index("y"), 0, keepdims=False)
    big["w_in"] = [t.T[None] for t in _adamw_sharded(land, w_in2.T, m_w_in[0].T, v_w_in[0].T, "adamw_w_in", 336,
                                                     extra=own)]
    ada = [t[None] for t in (g_w_ada, d_ada, nm_ada, nv_ada)]
    lb_w = hgrn_lb_logits.reshape(4, 64)
    d_lb, nm_lb, nv_lb = _adamw_plain(g_lb_mine.reshape(4, 64), lb_w, m_hgrn_lb_logits.reshape(4, 64),
                                      v_hgrn_lb_logits.reshape(4, 64), "adamw_lb")
    lbs = [t.reshape(2, 2, 64) for t in (g_lb_mine, d_lb, nm_lb, nv_lb)]

    names = ['c_ctx', 'w_ada', 'b_ada', 'norm_mix_w', 'norm_ffn_w', 'w_in', 'hgrn_lb_logits', 'hgrn_norm_w',
             'q_norm_w', 'k_norm_w', 'attn_sinks', 'w_branch_hgrn', 'w_branch_attn', 'w_out', 'w_ffn_gate',
             'w_ffn_up', 'w_ffn_down']
    outs = [loss, grad_x[None]]
    for kind in range(4):
        for nm in names:
            if nm == 'w_ada':
                outs.append(ada[kind])
            elif nm == 'hgrn_lb_logits':
                outs.append(lbs[kind])
            elif nm in big:
                outs.append(big[nm][kind])
            else:
                outs.append(rep[kind][nm])
    return tuple(outs)
```

```python
import functools
import math

import numpy as np
import jax
import jax.numpy as jnp
from jax import lax
from jax.experimental import pallas as pl
from jax.experimental.pallas import tpu as pltpu

F32 = jnp.float32
BF16 = jnp.bfloat16

N_DEV = 8
D = 1024
S = 2048
L = 256
T = L + S
TM = 256
N_TILES = T // TM
N_LAT_TILES = S // TM
HG_HEADS = 4
HG_DIM = 128
HGW = 512
CHUNK = 32
N_CHUNKS = T // CHUNK
N_CTX_CHUNKS = L // CHUNK
ATT_HEADS = 8
KV_HEADS = 2
HEAD_DIM = 64
ATW = 512
KVW = 128
BLOCK = 128
N_BLOCKS = S // BLOCK
GRID_W = 64
ROPE_THETA = 10000.0
D_FF = 2816
FF_BLK = D_FF // N_DEV
FF_TILE = 256
N_FF_TILES = D_FF // FF_TILE
N_FF_HALVES = 2
IN_COLS = 5376
IN_BLK = IN_COLS // N_DEV
ADA_BLK = 6 * D // N_DEV
EPS = 1e-6
WA, WB, WC = 2048, 1280, 2048

ADAM_LR = 0.001
ADAM_B1 = 0.9
ADAM_B2 = 0.999
ADAM_EPS = 1e-08
ADAM_WD = 0.01
ADAM_STEP = 10

VMEM_LIMIT = 56 * 1024 * 1024
MESH = pl.DeviceIdType.MESH


def _cp(sem=None, vmem=VMEM_LIMIT):
    return pltpu.CompilerParams(dimension_semantics=sem, vmem_limit_bytes=vmem)


def _full(shape):
    n = len(shape)
    return pl.BlockSpec(shape, lambda *_: (0,) * n)


_VMEM_WHOLE = pl.BlockSpec(memory_space=pltpu.VMEM)
_ANY = pl.BlockSpec(memory_space=pl.ANY)


def _sigmoid(v):
    return 1.0 / (1.0 + jnp.exp(-v))


def _dot(a, b):
    return jnp.dot(a, b, preferred_element_type=F32)


def _dot_nt(a, b):
    return lax.dot_general(a, b, (((1,), (1,)), ((), ())), preferred_element_type=F32)


def _dot_tn(a, b):
    return lax.dot_general(a, b, (((0,), (0,)), ((), ())), preferred_element_type=F32)


def _split3(v):
    hi = v.astype(BF16)
    r = v - hi.astype(F32)
    mid = r.astype(BF16)
    lo = (r - mid.astype(F32)).astype(BF16)
    return hi, mid, lo


def _dot_exact_rhs01(v, m01):
    hi, mid, lo = _split3(v)
    return _dot(hi, m01) + _dot(mid, m01) + _dot(lo, m01)


def _split2(v):
    hi = v.astype(BF16)
    return hi, (v - hi.astype(F32)).astype(BF16)


def _dot_lhs01(m01, v):
    hi, lo = _split2(v)
    return _dot(m01, hi) + _dot(m01, lo)


def _dot_f32(a, b, dot=_dot):
    ah, am, al = _split3(a)
    bh, bm, bl = _split3(b)
    return (dot(ah, bh) + (dot(ah, bm) + dot(am, bh))
            + (dot(am, bm) + dot(ah, bl) + dot(al, bh)))


def _my_pos():
    return lax.axis_index("x"), lax.axis_index("y"), lax.axis_index("c")


class _Comm:
    def __init__(self, operands, out_shapes, sems, phases):
        self.operands, self.out_shapes, self.sems, self.phases = operands, out_shapes, sems, phases


def _gather_comm(blocks):
    n = len(blocks)

    def parts(ins, outs, sems):
        send_sems, recv_sems, local_sems = sems
        x, y, c = _my_pos()
        me, sibling = (x, y, c), (x, y, 1 - c)
        chips = [(1 - x, y), (x, 1 - y), (1 - x, 1 - y)]

        def slot(a, px, py, pc):
            return outs[a].at[4 * px + 2 * py + pc]

        def copy(a, k, block, to, src=None):
            return pltpu.make_async_remote_copy(
                src_ref=slot(a, *block) if src is None else src, dst_ref=slot(a, *block),
                send_sem=send_sems.at[a, k], recv_sem=recv_sems.at[a, k],
                device_id=to, device_id_type=MESH)

        mine = [pltpu.make_async_copy(ins[a], slot(a, *me), local_sems.at[a]) for a in range(n)]
        first = []
        for a in range(n):
            first.append(copy(a, 0, me, sibling, src=ins[a]))
            first += [copy(a, 1 + j, me, (*chip, c), src=ins[a]) for j, chip in enumerate(chips)]
        passed = [copy(a, 4 + j, (*chip, c), sibling) for j, chip in enumerate(chips) for a in range(n)]
        return c, me, sibling, chips, copy, mine, first, passed

    def start(ins, outs, sems):
        _, _, _, _, _, mine, first, _ = parts(ins, outs, sems)
        for cp in mine + first:
            cp.start()

    def forward(ins, outs, sems):
        c, me, _, chips, copy, _, _, passed = parts(ins, outs, sems)
        for j, chip in enumerate(chips):
            for a in range(n):
                copy(a, 1 + j, (*chip, c), me).wait_recv()
                passed[j * n + a].start()

    def finish(ins, outs, sems):
        c, me, sibling, chips, copy, mine, first, passed = parts(ins, outs, sems)
        for a in range(n):
            copy(a, 0, sibling, me).wait_recv()
            for j, chip in enumerate(chips):
                copy(a, 4 + j, (*chip, 1 - c), me).wait_recv()
        for cp in first + passed:
            cp.wait_send()
        for cp in mine:
            cp.wait()

    return _Comm(blocks, [jax.ShapeDtypeStruct((N_DEV,) + b.shape, b.dtype) for b in blocks],
                 [pltpu.SemaphoreType.DMA((n, 7)), pltpu.SemaphoreType.DMA((n, 7)), pltpu.SemaphoreType.DMA((n,))],
                 [start, forward, finish])


def _gather_comm_relayed(blocks):
    n = len(blocks)

    def parts(ins, outs, sems):
        send_sems, recv_sems, local_sems = sems
        x, y, c = _my_pos()
        me, sibling = (x, y, c), (x, y, 1 - c)
        x_nbr, y_nbr, diag = (1 - x, y, c), (x, 1 - y, c), (1 - x, 1 - y, c)

        def slot(a, dev, half=None):
            ref = outs[a].at[4 * dev[0] + 2 * dev[1] + dev[2]]
            if half is None:
                return ref
            rows = blocks[a].shape[0] // 2
            return ref.at[pl.ds(half * rows, rows)]

        def copy(a, k, block, to, half=None, src=None):
            return pltpu.make_async_remote_copy(
                src_ref=slot(a, block, half) if src is None else src, dst_ref=slot(a, block, half),
                send_sem=send_sems.at[a, k], recv_sem=recv_sems.at[a, k],
                device_id=to, device_id_type=MESH)

        mine = [pltpu.make_async_copy(ins[a], slot(a, me), local_sems.at[a]) for a in range(n)]
        return me, sibling, x_nbr, y_nbr, diag, copy, mine

    def start(ins, outs, sems):
        me, sibling, x_nbr, y_nbr, _, copy, mine = parts(ins, outs, sems)
        for cp in mine:
            cp.start()
        for a in range(n):
            for k, to in ((1, x_nbr), (2, y_nbr), (0, sibling)):
                copy(a, k, me, to, src=ins[a]).start()

    def forward(ins, outs, sems):
        me, sibling, x_nbr, y_nbr, _, copy, _ = parts(ins, outs, sems)
        for a in range(n):
            copy(a, 1, x_nbr, me).wait_recv()
            copy(a, 3, x_nbr, y_nbr, half=0).start()
            copy(a, 5, x_nbr, sibling).start()
        for a in range(n):
            copy(a, 2, y_nbr, me).wait_recv()
            copy(a, 4, y_nbr, x_nbr, half=1).start()
            copy(a, 6, y_nbr, sibling).start()

    def finish(ins, outs, sems):
        me, sibling, x_nbr, y_nbr, diag, copy, mine = parts(ins, outs, sems)
        sib = lambda dev: (dev[0], dev[1], sibling[2])
        for a in range(n):
            copy(a, 3, diag, me, half=0).wait_recv()
            copy(a, 4, diag, me, half=1).wait_recv()
            copy(a, 7, diag, sibling).start()
        for a in range(n):
            copy(a, 0, sibling, me).wait_recv()
            for k, dev in ((5, x_nbr), (6, y_nbr), (7, diag)):
                copy(a, k, sib(dev), me).wait_recv()
        for a in range(n):
            for k, block, to, half in ((0, me, sibling, None), (1, me, x_nbr, None), (2, me, y_nbr, None),
                                       (3, x_nbr, y_nbr, 0), (4, y_nbr, x_nbr, 1), (5, x_nbr, sibling, None),
                                       (6, y_nbr, sibling, None), (7, diag, sibling, None)):
                copy(a, k, block, to, half=half, src=ins[a] if block is me else None).wait_send()
        for cp in mine:
            cp.wait()

    return _Comm(blocks, [jax.ShapeDtypeStruct((N_DEV,) + b.shape, b.dtype) for b in blocks],
                 [pltpu.SemaphoreType.DMA((n, 8)), pltpu.SemaphoreType.DMA((n, 8)), pltpu.SemaphoreType.DMA((n,))],
                 [start, forward, finish])


_HBM = pl.BlockSpec(memory_space=pltpu.HBM)
_SEM = pl.BlockSpec(memory_space=pltpu.SEMAPHORE)
_SPLIT_COPY = pltpu.CompilerParams(has_side_effects=pltpu.SideEffectType.DATAFLOW_SIDE_EFFECTING)


def _chip_exchange_copies(src_ref, land_ref, sems):
    x, y, c = _my_pos()
    q_me = 2 * x + y
    pairs = []
    for j, (px, py) in enumerate([(1 - x, y), (x, 1 - y), (1 - x, 1 - y)]):
        q = 2 * px + py
        send = pltpu.make_async_remote_copy(
            src_ref=src_ref.at[q], dst_ref=land_ref.at[q_me], send_sem=sems[j], recv_sem=sems[3 + j],
            device_id=(px, py, c), device_id_type=MESH)
        recv = pltpu.make_async_remote_copy(
            src_ref=src_ref.at[q], dst_ref=land_ref.at[q], send_sem=sems[j], recv_sem=sems[3 + j],
            device_id=(x, y, c), device_id_type=MESH)
        pairs.append((send, recv))
    return pairs


def _chip_exchange_start(src, land):
    def body(src_ref, land_ref, *outs):
        sems, token = outs[:6], outs[8]
        for send, _ in _chip_exchange_copies(src_ref, land_ref, sems):
            send.start()
        token[...] = jnp.zeros_like(token)

    res = pl.pallas_call(
        body, name="scatter_in_start",
        out_shape=(pltpu.SemaphoreType.DMA(()),) * 6 + (
            pltpu.HBM(src.shape, src.dtype), pltpu.HBM(land.shape, land.dtype),
            jax.ShapeDtypeStruct((8, 128), F32)),
        in_specs=(_HBM, _HBM), out_specs=(_SEM,) * 6 + (_HBM, _HBM, pl.BlockSpec(memory_space=pltpu.VMEM)),
        input_output_aliases={0: 6, 1: 7}, compiler_params=_SPLIT_COPY,
    )(pltpu.with_memory_space_constraint(src, pltpu.HBM), pltpu.with_memory_space_constraint(land, pltpu.HBM))
    return res[:6], res[6], res[7], res[8]


def _chip_exchange_wait(sems, src_thru, land_thru, after):
    def body(src_ref, land_ref, *rest):
        for send, recv in _chip_exchange_copies(src_ref, land_ref, rest[:6]):
            send.wait_send()
            recv.wait_recv()

    return pl.pallas_call(
        body, name="scatter_in_wait",
        out_shape=(pltpu.HBM(src_thru.shape, src_thru.dtype), pltpu.HBM(land_thru.shape, land_thru.dtype)),
        in_specs=(_HBM, _HBM) + (_SEM,) * 6 + (_ANY,), out_specs=(_HBM, _HBM),
        input_output_aliases={0: 0, 1: 1}, compiler_params=_SPLIT_COPY,
    )(src_thru, land_thru, *sems, after)[1]


def _run_comm(comm, name, in_vmem=False, after=()):
    n_in, n_out, n_after = len(comm.operands), len(comm.out_shapes), len(after)

    def body(*refs):
        ins, refs = refs[:n_in], refs[n_in + n_after:]
        outs, sems = refs[:n_out], refs[n_out:]
        for phase in comm.phases:
            phase(ins, outs, sems)

    spec = _VMEM_WHOLE if in_vmem else _ANY
    return pl.pallas_call(
        body, name=name, out_shape=comm.out_shapes, in_specs=[spec] * n_in + [_ANY] * n_after,
        out_specs=[spec] * n_out, scratch_shapes=comm.sems,
    )(*comm.operands, *after)


def _carrier_call(body, comm, schedule, *, name, grid, in_specs, out_specs, out_shape, scratch_shapes, operands):
    n_in, n_out, n_scr = len(in_specs), len(out_specs), len(scratch_shapes)
    c_in, c_out = len(comm.operands), len(comm.out_shapes)

    def full_body(*refs):
        ins, refs = refs[:n_in], refs[n_in:]
        cins, refs = refs[:c_in], refs[c_in:]
        outs, refs = refs[:n_out], refs[n_out:]
        couts, refs = refs[:c_out], refs[c_out:]
        scr, csems = refs[:n_scr], refs[n_scr:]
        step = pl.program_id(0)

        def run(before):
            for (at, when_before), phase in zip(schedule, comm.phases):
                if when_before == before:
                    pl.when(step == at)(functools.partial(phase, cins, couts, csems))

        run(True)
        body(*ins, *outs, *scr)
        run(False)

    res = pl.pallas_call(
        full_body, name=name, grid=grid,
        in_specs=list(in_specs) + [_ANY] * c_in, out_specs=list(out_specs) + [_ANY] * c_out,
        out_shape=list(out_shape) + list(comm.out_shapes),
        scratch_shapes=list(scratch_shapes) + list(comm.sems),
        compiler_params=_cp(("arbitrary",)),
    )(*operands, *comm.operands)
    return res[:n_out], res[n_out:]


def _pcall(body, carried, *, name, grid, in_specs, out_specs, out_shape, scratch_shapes, operands):
    if carried is None:
        res = pl.pallas_call(body, name=name, grid=grid, in_specs=in_specs, out_specs=out_specs,
                             out_shape=out_shape, scratch_shapes=scratch_shapes,
                             compiler_params=_cp(("arbitrary",)))(*operands)
        return res, ()
    return _carrier_call(body, carried[0], carried[1], name=name, grid=grid, in_specs=in_specs,
                         out_specs=out_specs, out_shape=out_shape, scratch_shapes=scratch_shapes,
                         operands=operands)


def _all_gather(blocks, name, in_vmem, after=()):
    return _run_comm(_gather_comm(blocks), name, in_vmem, after)


N_CHIPS = 4


def _sibling_comm(contribs):
    n = len(contribs)

    def copies(ins, outs, sems):
        send_sems, recv_sems = sems
        x, y, c = _my_pos()
        return [pltpu.make_async_remote_copy(
            src_ref=ins[a].at[pl.ds(0, N_CHIPS), 1 - c], dst_ref=outs[a],
            send_sem=send_sems.at[a], recv_sem=recv_sems.at[a],
            device_id=(x, y, 1 - c), device_id_type=MESH) for a in range(n)]

    def start(ins, outs, sems):
        for cp in copies(ins, outs, sems):
            cp.start()

    def finish(ins, outs, sems):
        cps = copies(ins, outs, sems)
        for cp in cps:
            cp.wait_recv()
        for cp in cps:
            cp.wait_send()

    return _Comm(contribs, [jax.ShapeDtypeStruct((N_CHIPS,) + b.shape[2:], b.dtype) for b in contribs],
                 [pltpu.SemaphoreType.DMA((n,)), pltpu.SemaphoreType.DMA((n,))], [start, finish])


def _pair_sum(mine, theirs, name):
    _, _, rows, cols = mine.shape
    core = lax.axis_index("c").astype(jnp.int32).reshape(1)

    def body(c_ref, m_ref, t_ref, o_ref):
        o_ref[...] = (m_ref[...].astype(F32) + t_ref[...].astype(F32)).astype(BF16)

    return pl.pallas_call(
        body, name=name,
        grid_spec=pltpu.PrefetchScalarGridSpec(
            num_scalar_prefetch=1, grid=(N_CHIPS,),
            in_specs=[pl.BlockSpec((None, None, rows, cols), lambda q, c: (q, c[0], 0, 0)),
                      pl.BlockSpec((None, rows, cols), lambda q, c: (q, 0, 0))],
            out_specs=pl.BlockSpec((None, rows, cols), lambda q, c: (q, 0, 0))),
        out_shape=jax.ShapeDtypeStruct((N_CHIPS, rows, cols), BF16),
        compiler_params=_cp(("parallel",)),
    )(core, mine, theirs)


def _chip_comm(sums):
    n = len(sums)

    def parts(ins, outs, sems):
        send_sems, recv_sems, local_sems = sems
        x, y, c = _my_pos()
        q_me = 2 * x + y
        chips = [(1 - x, y), (x, 1 - y), (1 - x, 1 - y)]
        mine = [pltpu.make_async_copy(ins[a].at[q_me], outs[a].at[q_me], local_sems.at[a]) for a in range(n)]
        sends, recvs = [], []
        for j, (px, py) in enumerate(chips):
            for a in range(n):
                q = 2 * px + py
                sends.append(pltpu.make_async_remote_copy(
                    src_ref=ins[a].at[q], dst_ref=outs[a].at[q_me],
                    send_sem=send_sems.at[a, j], recv_sem=recv_sems.at[a, j],
                    device_id=(px, py, c), device_id_type=MESH))
                recvs.append(pltpu.make_async_remote_copy(
                    src_ref=ins[a].at[q], dst_ref=outs[a].at[q],
                    send_sem=send_sems.at[a, j], recv_sem=recv_sems.at[a, j],
                    device_id=(x, y, c), device_id_type=MESH))
        return mine, sends, recvs

    def start(ins, outs, sems):
        mine, sends, _ = parts(ins, outs, sems)
        for cp in mine + sends:
            cp.start()

    def finish(ins, outs, sems):
        mine, sends, recvs = parts(ins, outs, sems)
        for cp in recvs:
            cp.wait_recv()
        for cp in sends:
            cp.wait_send()
        for cp in mine:
            cp.wait()

    return _Comm(sums, [jax.ShapeDtypeStruct(b.shape, b.dtype) for b in sums],
                 [pltpu.SemaphoreType.DMA((n, 3)), pltpu.SemaphoreType.DMA((n, 3)), pltpu.SemaphoreType.DMA((n,))],
                 [start, finish])


def _mm_nt(a, bt, *, tm, tn, out_dtype, name, row_off=0, rows=None):
    rows = a.shape[0] if rows is None else rows
    n, k = bt.shape

    def body(a_ref, b_ref, o_ref):
        o_ref[...] = _dot_nt(a_ref[...], b_ref[...]).astype(out_dtype)

    return pl.pallas_call(
        body, name=name, grid=(rows // tm, n // tn),
        in_specs=[pl.BlockSpec((tm, k), lambda i, j: (i + row_off, 0)),
                  pl.BlockSpec((tn, k), lambda i, j: (j, 0))],
        out_specs=pl.BlockSpec((tm, tn), lambda i, j: (i, j)),
        out_shape=jax.ShapeDtypeStruct((rows, n), out_dtype),
        compiler_params=_cp(("parallel", "parallel")),
    )(a, bt)


def _mm_tn(a, b, *, tk, nk, tm, tn, out_dtype, name, a_off=0, b_off=0):
    m, n = a.shape[1], b.shape[1]

    def body(a_ref, b_ref, o_ref, acc):
        kk = pl.program_id(2)

        @pl.when(kk == 0)
        def _():
            acc[...] = jnp.zeros_like(acc)

        acc[...] += _dot_tn(a_ref[...], b_ref[...])

        @pl.when(kk == nk - 1)
        def _():
            o_ref[...] = acc[...].astype(out_dtype)

    return pl.pallas_call(
        body, name=name, grid=(m // tm, n // tn, nk),
        in_specs=[pl.BlockSpec((tk, tm), lambda i, j, kk: (kk + a_off, i)),
                  pl.BlockSpec((tk, tn), lambda i, j, kk: (kk + b_off, j))],
        out_specs=pl.BlockSpec((tm, tn), lambda i, j, kk: (i, j)),
        out_shape=jax.ShapeDtypeStruct((m, n), out_dtype),
        scratch_shapes=[pltpu.VMEM((tm, tn), F32)],
        compiler_params=_cp(("parallel", "parallel", "arbitrary")),
    )(a, b)


def _mm_tn_blocked(a, b, name, steps, carried=None):
    w = a.shape[1] // steps
    n = b.shape[1]

    def body(a_ref, b_ref, o_ref):
        o_ref[...] = _dot_tn(a_ref[...], b_ref[...]).astype(BF16)

    (out,), extra = _pcall(
        body, carried, name=name, grid=(steps,),
        in_specs=[pl.BlockSpec((S, w), lambda j: (0, j)), _full((S, n))],
        out_specs=[pl.BlockSpec((w, n), lambda j: (j, 0))],
        out_shape=[jax.ShapeDtypeStruct((a.shape[1], n), BF16)],
        scratch_shapes=[], operands=[a, b])
    return out, extra


def _prologue(cond, c_ctx, w_ada, b_cols, w_in_t, x, ctx, nw):
    rows_shape = jax.ShapeDtypeStruct((16, ADA_BLK), F32)
    big, g_cond, g_mod = _gather_comm_relayed([w_in_t]), _gather_comm([cond]), _gather_comm([rows_shape])

    def body(cond_ref, cctx_ref, wada_ref, b_ref, nw_ref, win_ref, x_ref, ctx_ref,
             g0_ref, cc_ref, mod_ref, gin_ref, h_ref, hl_ref, rows_ref, g1_ref, x_s, ctx_s, h_s, io_sems, *sems):
        s_big, s_cond, s_mod = sems[0:3], sems[3:6], sems[6:9]
        big.phases[0]([win_ref], [gin_ref], s_big)
        load_x = pltpu.make_async_copy(x_ref, x_s, io_sems.at[0])
        load_ctx = pltpu.make_async_copy(ctx_ref, ctx_s, io_sems.at[1])
        load_x.start()
        load_ctx.start()
        for phase in g_cond.phases:
            phase([cond_ref], [g0_ref], s_cond)
        cc_ref[...] = jnp.zeros_like(cc_ref)
        for j in range(N_DEV):
            cc_ref[j:j + 1, :] = g0_ref[j, 0:1, :]
        cc_ref[N_DEV:N_DEV + 1, :] = cctx_ref[...]
        cv = cc_ref[...]
        rows_ref[...] = _dot_f32(cv * _sigmoid(cv), wada_ref[...]) + b_ref[...]
        for phase in g_mod.phases:
            phase([rows_ref], [g1_ref], s_mod)
        x_pos, y_pos, c_pos = _my_pos()
        me = 4 * x_pos + 2 * y_pos + c_pos
        mine = jnp.concatenate([g1_ref[j, pl.ds(me, 1), :] for j in range(N_DEV)], axis=1)
        shared = jnp.concatenate([g1_ref[j, N_DEV:N_DEV + 1, :] for j in range(N_DEV)], axis=1)
        for k in range(6):
            mod_ref[k:k + 1, :] = mine[:, k * D:(k + 1) * D]
        mod_ref[6:7, :] = shared[:, 0:D]
        mod_ref[7:8, :] = shared[:, D:2 * D]
        load_ctx.wait()
        load_x.wait()
        h_s[pl.ds(0, L), :] = _rms_mod(ctx_s[...], nw_ref[...], mod_ref[6:7, :], mod_ref[7:8, :]).astype(BF16)

        def norm_tile(i, carry):
            r0 = pl.multiple_of(i * TM, TM)
            h_s[pl.ds(L + r0, TM), :] = _rms_mod(
                x_s[pl.ds(r0, TM), :], nw_ref[...], mod_ref[0:1, :], mod_ref[1:2, :]).astype(BF16)
            return carry

        lax.fori_loop(0, N_LAT_TILES, norm_tile, 0)
        stores = [pltpu.make_async_copy(h_s, h_ref, io_sems.at[2]),
                  pltpu.make_async_copy(h_s.at[pl.ds(L, S)], hl_ref, io_sems.at[3])]
        for cp in stores:
            cp.start()
        big.phases[1]([win_ref], [gin_ref], s_big)
        big.phases[2]([win_ref], [gin_ref], s_big)
        for cp in stores:
            cp.wait()

    return pl.pallas_call(
        body, name="prologue",
        in_specs=[_VMEM_WHOLE] * 5 + [_ANY] * 3, out_specs=[_VMEM_WHOLE] * 3 + [_ANY] * 3,
        out_shape=[g_cond.out_shapes[0], jax.ShapeDtypeStruct((16, D), F32), jax.ShapeDtypeStruct((8, D), F32),
                   big.out_shapes[0], jax.ShapeDtypeStruct((T, D), BF16), jax.ShapeDtypeStruct((S, D), BF16)],
        scratch_shapes=[pltpu.VMEM((16, ADA_BLK), F32), pltpu.VMEM((N_DEV, 16, ADA_BLK), F32),
                        pltpu.VMEM((S, D), F32), pltpu.VMEM((L, D), F32), pltpu.VMEM((T, D), BF16),
                        pltpu.SemaphoreType.DMA((4,))] + big.sems + g_cond.sems + g_mod.sems,
        compiler_params=_cp(),
    )(cond, c_ctx, w_ada, b_cols, nw, w_in_t, x, ctx)


def _ada_grads(cc, dm_cols, w_ada):
    def body(c_ref, dm_ref, w_ref, gw_ref, dsc_ref):
        cv = c_ref[...]
        sc = cv * _sigmoid(cv)
        dm = dm_ref[...]
        gw_ref[...] = _dot_f32(sc, dm, dot=_dot_tn)
        dsc_ref[...] = _dot_f32(dm, w_ref[...], dot=_dot_nt)

    return pl.pallas_call(
        body, name="ada_grads",
        in_specs=[_VMEM_WHOLE] * 3, out_specs=[_VMEM_WHOLE] * 2,
        out_shape=[jax.ShapeDtypeStruct((D, ADA_BLK), F32), jax.ShapeDtypeStruct((16, D), F32)],
        compiler_params=_cp(),
    )(cc, dm_cols, w_ada)


def _lat(i):
    return jnp.maximum(i - 1, 0)


def _rms_mod(xv, nw, sh, sc):
    rstd = lax.rsqrt(jnp.mean(xv * xv, axis=-1, keepdims=True) + EPS)
    return (xv * rstd * nw) * (1.0 + sc) + sh


def _rms_mod_bwd(xv, nw, sc, dh):
    rstd = lax.rsqrt(jnp.mean(xv * xv, axis=-1, keepdims=True) + EPS)
    xhat = xv * rstd
    dn = dh * (1.0 + sc)
    dxhat = dn * nw
    dx = rstd * (dxhat - xhat * jnp.mean(dxhat * xhat, axis=-1, keepdims=True))
    return (dx, jnp.sum(dh, axis=0, keepdims=True), jnp.sum(dh * (xhat * nw), axis=0, keepdims=True),
            jnp.sum(dn * xhat, axis=0, keepdims=True))


def _chunk_masks(reverse):
    row = lax.broadcasted_iota(jnp.int32, (TM, TM), 0)
    col = lax.broadcasted_iota(jnp.int32, (TM, TM), 1)
    same = (row // CHUNK) == (col // CHUNK)
    tri = same & ((col >= row) if reverse else (col <= row))
    return same, tri


def _chunk_order(i, reverse):
    if not reverse:
        return i
    return jnp.where(i < N_CTX_CHUNKS, N_CTX_CHUNKS - 1 - i, N_CHUNKS + N_CTX_CHUNKS - 1 - i)


def _decay_terms(z, lb, same01, tri01):
    f = lb + (1.0 - lb) * _sigmoid(z)
    g = jnp.log(f)
    g2 = jnp.concatenate(_split2(g), axis=1)
    b2 = _dot(tri01, g2)
    t2 = _dot(same01, g2)
    return f, 1.0 - f, b2[:, :HG_DIM] + b2[:, HG_DIM:], t2[:, :HG_DIM] + t2[:, HG_DIM:]


def _chunk_outer(a, b):
    n = TM // CHUNK
    return jnp.einsum('ncv,nck->nvk', a.reshape(n, CHUNK, HG_DIM), b.reshape(n, CHUNK, HG_DIM),
                      preferred_element_type=F32)


def _hgrn_fwd(p_a, lbl, carried=None):
    cpt = TM // CHUNK

    def body(p_ref, lbl_ref, o_ref, st_ref, qd_s, kd_s, u_s, v_s, ebt_s):
        masks = [_chunk_masks(d == 1) for d in (0, 1)]
        same01 = jnp.where(masks[0][0], 1.0, 0.0).astype(BF16)
        tri = [m[1] for m in masks]
        tri01 = [jnp.where(t, 1.0, 0.0).astype(BF16) for t in tri]
        lb = [_sigmoid(lbl_ref[d][0:1, :] - lbl_ref[d][1:2, :]) for d in (0, 1)]

        def prep(r, carry):
            r0 = pl.multiple_of(r * TM, TM)
            vb = p_ref[pl.ds(r0, TM), 2 * HG_DIM:3 * HG_DIM].astype(BF16)
            v_s[pl.ds(r0, TM), :] = vb
            for d in (0, 1):
                z = p_ref[pl.ds(r0, TM), d * HG_DIM:(d + 1) * HG_DIM]
                _, k, b, bt = _decay_terms(z, lb[d], same01, tri01[d])
                u_s[d, pl.ds(r * cpt, cpt)] = _chunk_outer(vb, (k * jnp.exp(bt - b)).astype(BF16))
                ebt_s[d, pl.ds(r0, TM), :] = jnp.exp(bt)

                @pl.when(r >= 1)
                def _():
                    rl = pl.multiple_of(r0 - L, TM)
                    qr = p_ref[pl.ds(r0, TM), 3 * HG_DIM:4 * HG_DIM]
                    q = qr * _sigmoid(qr) * HG_DIM ** -0.5
                    qd_s[d, pl.ds(rl, TM), :] = (q * jnp.exp(b)).astype(BF16)
                    kd_s[d, pl.ds(rl, TM), :] = (k * jnp.exp(-b)).astype(BF16)

            return carry

        lax.fori_loop(0, N_TILES, prep, 0)

        def scan(i, sts):
            new = []
            for d in (0, 1):
                nn = _chunk_order(i, d == 1)
                c0 = pl.multiple_of(nn * CHUNK, CHUNK)
                st_ref[d, nn] = sts[d].astype(BF16)
                new.append(sts[d] * ebt_s[d, pl.ds(c0, 1), :] + u_s[d, nn])
            return tuple(new)

        zero = jnp.zeros((HG_DIM, HG_DIM), F32)
        lax.fori_loop(0, N_CHUNKS, scan, (zero, zero))

        def outp(r, carry):
            r0 = pl.multiple_of(r * TM, TM)
            vb = v_s[pl.ds(r0 + L, TM), :]
            o = jnp.zeros((TM, HG_DIM), F32)
            for d in (0, 1):
                qd = qd_s[d, pl.ds(r0, TM), :]
                a = jnp.where(tri[d], _dot_nt(qd, kd_s[d, pl.ds(r0, TM), :]), 0.0)
                stb = st_ref[d, pl.ds(N_CTX_CHUNKS + r * cpt, cpt)]
                inter = jnp.einsum('nck,nvk->ncv', qd.reshape(cpt, CHUNK, HG_DIM), stb,
                                   preferred_element_type=F32)
                o = o + _dot(a.astype(BF16), vb) + inter.reshape(TM, HG_DIM)
            o_ref[pl.ds(r0, TM), :] = o
            return carry

        lax.fori_loop(0, N_LAT_TILES, outp, 0, unroll=2)

    return _pcall(
        body, carried, name="hgrn_fwd", grid=(HG_HEADS,),
        in_specs=[pl.BlockSpec((T, 4 * HG_DIM), lambda h: (0, h)),
                  pl.BlockSpec((2, 2, HG_DIM), lambda h: (0, 0, h))],
        out_specs=[pl.BlockSpec((S, HG_DIM), lambda h: (0, h)),
                   pl.BlockSpec((2, None, N_CHUNKS, HG_DIM, HG_DIM), lambda h: (0, h, 0, 0, 0))],
        out_shape=[jax.ShapeDtypeStruct((S, HGW), F32),
                   jax.ShapeDtypeStruct((2, HG_HEADS, N_CHUNKS, HG_DIM, HG_DIM), BF16)],
        scratch_shapes=[pltpu.VMEM((2, S, HG_DIM), BF16), pltpu.VMEM((2, S, HG_DIM), BF16),
                        pltpu.VMEM((2, N_CHUNKS, HG_DIM, HG_DIM), F32), pltpu.VMEM((T, HG_DIM), BF16),
                        pltpu.VMEM((2, T, HG_DIM), F32)],
        operands=[p_a, lbl])


def _hgrn_bwd(p_a, lbl, d_o, st, carried=None):
    cpt = TM // CHUNK

    def rows(r):
        return r * TM if isinstance(r, int) else pl.multiple_of(r * TM, TM)

    def body(p_ref, lbl_ref, do_ref, st_ref, dp_ref, dlb_ref, b_s, bt_s, dbt_s, qd_s, dst_s, w_s):
        masks = [_chunk_masks(d == 1) for d in (0, 1)]
        same01 = jnp.where(masks[0][0], 1.0, 0.0).astype(BF16)
        tri = [m[1] for m in masks]
        tri01 = [jnp.where(t, 1.0, 0.0).astype(BF16) for t in tri]
        later01 = [tri01[1], tri01[0]]
        lb = [_sigmoid(lbl_ref[d][0:1, :] - lbl_ref[d][1:2, :]) for d in (0, 1)]

        def prep_tile(r, latent):
            r0 = rows(r)
            for d in (0, 1):
                z = p_ref[pl.ds(r0, TM), d * HG_DIM:(d + 1) * HG_DIM]
                _, _, b, bt = _decay_terms(z, lb[d], same01, tri01[d])
                b_s[d, pl.ds(r0, TM), :] = b
                bt_s[d, pl.ds(r0, TM), :] = bt
                if latent:
                    rl = pl.multiple_of(r0 - L, TM)
                    qr = p_ref[pl.ds(r0, TM), 3 * HG_DIM:4 * HG_DIM]
                    qd = (qr * _sigmoid(qr) * HG_DIM ** -0.5 * jnp.exp(b)).astype(BF16)
                    qd_s[d, pl.ds(rl, TM), :] = qd
                    w_s[d, pl.ds(r * cpt, cpt)] = _chunk_outer(
                        do_ref[pl.ds(rl, TM), :].astype(BF16), qd).astype(BF16)

        prep_tile(0, False)
        w_s[:, pl.ds(0, N_CTX_CHUNKS)] = jnp.zeros((2, N_CTX_CHUNKS, HG_DIM, HG_DIM), BF16)

        def prep(r, carry):
            prep_tile(r, True)
            return carry

        lax.fori_loop(1, N_TILES, prep, 0, unroll=2)

        def rscan(j, dsts):
            i = N_CHUNKS - 1 - j
            new = []
            for d in (0, 1):
                nn = _chunk_order(i, d == 1)
                c0 = pl.multiple_of(nn * CHUNK, CHUNK)
                dst_s[d, nn] = dsts[d].astype(BF16)
                after = st_ref[d, _chunk_order(jnp.minimum(i + 1, N_CHUNKS - 1), d == 1)].astype(F32)
                dbt_s[d, pl.ds(c0, CHUNK), :] = jnp.broadcast_to(
                    jnp.sum(after * dsts[d], axis=0, keepdims=True), (CHUNK, HG_DIM))
                new.append(dsts[d] * jnp.exp(bt_s[d, pl.ds(c0, 1), :]) + w_s[d, nn].astype(F32))
            return tuple(new)

        zero = jnp.zeros((HG_DIM, HG_DIM), F32)
        lax.fori_loop(0, N_CHUNKS, rscan, (zero, zero))

        def grad_tile(r, latent):
            r0 = rows(r)
            vb = p_ref[pl.ds(r0, TM), 2 * HG_DIM:3 * HG_DIM].astype(BF16)
            dv = jnp.zeros((TM, HG_DIM), F32)
            dq = jnp.zeros((TM, HG_DIM), F32)
            dlbs = []
            if latent:
                rl = pl.multiple_of(r0 - L, TM)
                qr = p_ref[pl.ds(r0, TM), 3 * HG_DIM:4 * HG_DIM]
                sq = _sigmoid(qr)
                do = do_ref[pl.ds(rl, TM), :].astype(BF16)
                da_full = _dot_nt(do, vb)
            for d in (0, 1):
                z = p_ref[pl.ds(r0, TM), d * HG_DIM:(d + 1) * HG_DIM]
                sz = _sigmoid(z)
                f = lb[d] + (1.0 - lb[d]) * sz
                k = 1.0 - f
                b = b_s[d, pl.ds(r0, TM), :]
                e2 = jnp.exp(bt_s[d, pl.ds(r0, TM), :] - b)
                dstb = dst_s[d, pl.ds(r * cpt, cpt)]
                kd2 = k * e2
                dkd2 = jnp.einsum('ncv,nvk->nck', vb.reshape(cpt, CHUNK, HG_DIM), dstb,
                                  preferred_element_type=F32).reshape(TM, HG_DIM)
                dv = dv + jnp.einsum('nck,nvk->ncv', kd2.astype(BF16).reshape(cpt, CHUNK, HG_DIM), dstb,
                                     preferred_element_type=F32).reshape(TM, HG_DIM)
                dk = dkd2 * e2
                db = -(kd2 * dkd2)
                if latent:
                    eb = jnp.exp(b)
                    enb = jnp.exp(-b)
                    qdf = qr * sq * HG_DIM ** -0.5 * eb
                    kdf = k * enb
                    qd = qd_s[d, pl.ds(rl, TM), :]
                    kd = kdf.astype(BF16)
                    a = jnp.where(tri[d], _dot_nt(qd, kd), 0.0).astype(BF16)
                    da = jnp.where(tri[d], da_full, 0.0).astype(BF16)
                    stb = st_ref[d, pl.ds(r * cpt, cpt)]
                    dqd = _dot(da, kd) + jnp.einsum(
                        'ncv,nvk->nck', do.reshape(cpt, CHUNK, HG_DIM), stb,
                        preferred_element_type=F32).reshape(TM, HG_DIM)
                    dkd = _dot_tn(da, qd)
                    dv = dv + _dot_tn(a, do)
                    dk = dk + dkd * enb
                    db = db + qdf * dqd - kdf * dkd
                    dq = dq + dqd * eb
                dg = _dot_lhs01(later01[d], db) + dbt_s[d, pl.ds(r0, TM), :]
                df = dg / f - dk
                dp_ref[pl.ds(r0, TM), d * HG_DIM:(d + 1) * HG_DIM] = (
                    df * (1.0 - lb[d]) * sz * (1.0 - sz)).astype(BF16)
                dlbs.append(jnp.sum(df * (1.0 - sz), axis=0, keepdims=True))
            dp_ref[pl.ds(r0, TM), 2 * HG_DIM:3 * HG_DIM] = dv.astype(BF16)
            if latent:
                dq = dq * (HG_DIM ** -0.5) * (sq * (1.0 + qr * (1.0 - sq)))
            dp_ref[pl.ds(r0, TM), 3 * HG_DIM:4 * HG_DIM] = dq.astype(BF16)
            return dlbs

        dlb_ctx = grad_tile(0, False)

        def grads(r, acc):
            t = grad_tile(r, True)
            return (acc[0] + t[0], acc[1] + t[1])

        dlb = lax.fori_loop(1, N_TILES, grads, (dlb_ctx[0], dlb_ctx[1]))
        dlb_ref[0:1, :] = dlb[0]
        dlb_ref[1:2, :] = dlb[1]

    return _pcall(
        body, carried, name="hgrn_bwd", grid=(HG_HEADS,),
        in_specs=[pl.BlockSpec((T, 4 * HG_DIM), lambda h: (0, h)),
                  pl.BlockSpec((2, 2, HG_DIM), lambda h: (0, 0, h)),
                  pl.BlockSpec((S, HG_DIM), lambda h: (0, h)),
                  pl.BlockSpec((2, None, N_CHUNKS, HG_DIM, HG_DIM), lambda h: (0, h, 0, 0, 0))],
        out_specs=[pl.BlockSpec((T, 4 * HG_DIM), lambda h: (0, h)),
                   pl.BlockSpec((2, HG_DIM), lambda h: (0, h))],
        out_shape=[jax.ShapeDtypeStruct((T, WA), BF16), jax.ShapeDtypeStruct((2, HGW), F32)],
        scratch_shapes=[pltpu.VMEM((2, T, HG_DIM), F32), pltpu.VMEM((2, T, HG_DIM), F32),
                        pltpu.VMEM((2, T, HG_DIM), F32), pltpu.VMEM((2, S, HG_DIM), BF16),
                        pltpu.VMEM((2, N_CHUNKS, HG_DIM, HG_DIM), BF16),
                        pltpu.VMEM((2, N_CHUNKS, HG_DIM, HG_DIM), BF16)],
        operands=[p_a, lbl, d_o, st])


def _rope_tables():
    t = np.arange(S)
    inv = ROPE_THETA ** (-np.arange(0, 32, 2, dtype=np.float64) / 32)
    lane = np.arange(64)
    pos = np.where(lane[None, :] < 32, (t // GRID_W)[:, None], (t % GRID_W)[:, None]).astype(np.float64)
    ang = pos * inv[(lane % 32) % 16][None, :]
    sign = np.where((lane % 32) < 16, -1.0, 1.0)[None, :]
    cos = np.tile(np.cos(ang), (1, 2)).astype(np.float32)
    sin = np.tile(np.sin(ang) * sign, (1, 2)).astype(np.float32)
    return jnp.asarray(cos), jnp.asarray(sin)


def _rope_partner(v):
    lane = lax.broadcasted_iota(jnp.int32, (1, 128), 1)
    first = (lane % 32) < 16
    slabs = []
    for j in range(v.shape[1] // 128):
        s = v[:, 128 * j:128 * (j + 1)]
        slabs.append(jnp.where(first, pltpu.roll(s, 112, 1), pltpu.roll(s, 16, 1)))
    return slabs[0] if len(slabs) == 1 else jnp.concatenate(slabs, axis=1)


def _group_ones(width, group):
    r = lax.broadcasted_iota(jnp.int32, (width, width), 0)
    c = lax.broadcasted_iota(jnp.int32, (width, width), 1)
    return jnp.where((r // group) == (c // group), 1.0, 0.0).astype(BF16)


def _group_mean(v, ones01, group):
    hi = v.astype(BF16)
    lo = (v - hi.astype(F32)).astype(BF16)
    return (_dot(hi, ones01) + _dot(lo, ones01)) * (1.0 / group)


def _rep_matrix():
    r = lax.broadcasted_iota(jnp.int32, (KVW, ATW), 0)
    c = lax.broadcasted_iota(jnp.int32, (KVW, ATW), 1)
    return jnp.where(r == HEAD_DIM * (c // 256) + c % HEAD_DIM, 1.0, 0.0).astype(BF16)


def _tile_lanes(v, reps):
    return jnp.concatenate([v] * reps, axis=1)


def _prep_fwd(p_b, o, cos, sin, hnw, qnw, knw):
    def body(p_ref, o_ref, cos_ref, sin_ref, hnw_ref, qnw_ref, knw_ref, y_ref, q_ref, k_ref, v_ref):
        i = pl.program_id(0)
        rep = _rep_matrix()
        ones_k = _group_ones(KVW, HEAD_DIM)
        kr = p_ref[:, 1024:1152]
        krstd = lax.rsqrt(_group_mean(kr * kr, ones_k, HEAD_DIM) + EPS)
        kn = kr * krstd * knw_ref[...]
        v_ref[...] = _dot(p_ref[:, 1152:1280].astype(BF16), rep).astype(BF16)

        @pl.when(i == 0)
        def _():
            k_ref[...] = _dot(kn.astype(BF16), rep).astype(BF16)

        @pl.when(i > 0)
        def _():
            cs, sn = cos_ref[...], sin_ref[...]
            kro = kn * cs + _rope_partner(kn) * sn
            k_ref[...] = _dot(kro.astype(BF16), rep).astype(BF16)
            qr = p_ref[:, 512:1024]
            qrstd = lax.rsqrt(_group_mean(qr * qr, _group_ones(ATW, HEAD_DIM), HEAD_DIM) + EPS)
            qn = qr * qrstd * qnw_ref[...]
            qro = qn * _tile_lanes(cs, 4) + _rope_partner(qn) * _tile_lanes(sn, 4)
            q_ref[...] = (qro * HEAD_DIM ** -0.5).astype(BF16)
            ys = []
            for h in range(HG_HEADS):
                oh = o_ref[:, HG_DIM * h:HG_DIM * (h + 1)]
                gh = p_ref[:, HG_DIM * h:HG_DIM * (h + 1)]
                rstd = lax.rsqrt(jnp.mean(oh * oh, axis=-1, keepdims=True) + EPS)
                ys.append(oh * rstd * hnw_ref[...] * (gh * _sigmoid(gh)))
            y_ref[...] = jnp.concatenate(ys, axis=1).astype(BF16)

    return pl.pallas_call(
        body, name="prep_fwd", grid=(N_TILES,),
        in_specs=[pl.BlockSpec((TM, WB), lambda i: (i, 0)),
                  pl.BlockSpec((TM, HGW), lambda i: (_lat(i), 0)),
                  pl.BlockSpec((TM, 128), lambda i: (_lat(i), 0)),
                  pl.BlockSpec((TM, 128), lambda i: (_lat(i), 0)),
                  _full((1, HG_DIM)), _full((1, ATW)), _full((1, KVW))],
        out_specs=[pl.BlockSpec((TM, HGW), lambda i: (_lat(i), 0)),
                   pl.BlockSpec((TM, ATW), lambda i: (_lat(i), 0)),
                   pl.BlockSpec((TM, ATW), lambda i: (i, 0)),
                   pl.BlockSpec((TM, ATW), lambda i: (i, 0))],
        out_shape=[jax.ShapeDtypeStruct((S, HGW), BF16), jax.ShapeDtypeStruct((S, ATW), BF16),
                   jax.ShapeDtypeStruct((T, ATW), BF16), jax.ShapeDtypeStruct((T, ATW), BF16)],
        compiler_params=_cp(("arbitrary",)),
    )(p_b, o, cos, sin, hnw, qnw, knw)


def _prep_bwd(p_b, o, cos, sin, hnw, qnw, knw, dy_hg, dq, dk_rep, dv_rep, carried=None):
    def body(p_ref, o_ref, cos_ref, sin_ref, hnw_ref, qnw_ref, knw_ref, dy_ref, dq_ref, dk_ref, dv_ref,
             dp_ref, do_ref, acc_ref):
        i = pl.program_id(0)

        @pl.when(i == 0)
        def _():
            acc_ref[...] = jnp.zeros_like(acc_ref)

        rep = _rep_matrix()
        ones_k = _group_ones(KVW, HEAD_DIM)

        def fold(v):
            hi = v.astype(BF16)
            lo = (v - hi.astype(F32)).astype(BF16)
            return _dot_nt(hi, rep) + _dot_nt(lo, rep)

        kr = p_ref[:, 1024:1152]
        krstd = lax.rsqrt(_group_mean(kr * kr, ones_k, HEAD_DIM) + EPS)
        khat = kr * krstd
        kw = knw_ref[...]
        dkro = fold(dk_ref[...])
        dv = fold(dv_ref[...])

        def k_back(dkn):
            dkhat = dkn * kw
            dkr = krstd * (dkhat - khat * _group_mean(dkhat * khat, ones_k, HEAD_DIM))
            acc_ref[2:3, 0:KVW] += jnp.sum(dkn * khat, axis=0, keepdims=True)
            dp_ref[:, 1024:1152] = dkr.astype(BF16)
            dp_ref[:, 1152:1280] = dv.astype(BF16)

        @pl.when(i == 0)
        def _():
            k_back(dkro)
            dp_ref[:, 0:1024] = jnp.zeros((TM, 1024), BF16)

        @pl.when(i > 0)
        def _():
            cs, sn = cos_ref[...], sin_ref[...]
            k_back(dkro * cs + _rope_partner(dkro * sn))
            ones_q = _group_ones(ATW, HEAD_DIM)
            qr = p_ref[:, 512:1024]
            qrstd = lax.rsqrt(_group_mean(qr * qr, ones_q, HEAD_DIM) + EPS)
            qhat = qr * qrstd
            dqro = dq_ref[...] * HEAD_DIM ** -0.5
            dqn = dqro * _tile_lanes(cs, 4) + _rope_partner(dqro * _tile_lanes(sn, 4))
            dqhat = dqn * qnw_ref[...]
            dqr = qrstd * (dqhat - qhat * _group_mean(dqhat * qhat, ones_q, HEAD_DIM))
            acc_ref[1:2, :] += jnp.sum(dqn * qhat, axis=0, keepdims=True)
            dp_ref[:, 512:1024] = dqr.astype(BF16)
            dws = jnp.zeros((1, HG_DIM), F32)
            for h in range(HG_HEADS):
                sl = slice(HG_DIM * h, HG_DIM * (h + 1))
                oh, gh, dy = o_ref[:, sl], p_ref[:, sl], dy_ref[:, sl]
                rstd = lax.rsqrt(jnp.mean(oh * oh, axis=-1, keepdims=True) + EPS)
                ohat = oh * rstd
                sg = _sigmoid(gh)
                dp_ref[:, sl] = (dy * (ohat * hnw_ref[...]) * (sg * (1.0 + gh * (1.0 - sg)))).astype(BF16)
                dn = dy * (gh * sg)
                dws = dws + jnp.sum(dn * ohat, axis=0, keepdims=True)
                dohat = dn * hnw_ref[...]
                do_ref[:, sl] = rstd * (dohat - ohat * jnp.mean(dohat * ohat, axis=-1, keepdims=True))
            acc_ref[0:1, 0:HG_DIM] += dws

    return _pcall(
        body, carried, name="prep_bwd", grid=(N_TILES,),
        in_specs=[pl.BlockSpec((TM, WB), lambda i: (i, 0)),
                  pl.BlockSpec((TM, HGW), lambda i: (_lat(i), 0)),
                  pl.BlockSpec((TM, 128), lambda i: (_lat(i), 0)),
                  pl.BlockSpec((TM, 128), lambda i: (_lat(i), 0)),
                  _full((1, HG_DIM)), _full((1, ATW)), _full((1, KVW)),
                  pl.BlockSpec((TM, HGW), lambda i: (_lat(i), 0)),
                  pl.BlockSpec((TM, ATW), lambda i: (_lat(i), 0)),
                  pl.BlockSpec((TM, ATW), lambda i: (i, 0)),
                  pl.BlockSpec((TM, ATW), lambda i: (i, 0))],
        out_specs=[pl.BlockSpec((TM, WB), lambda i: (i, 0)),
                   pl.BlockSpec((TM, HGW), lambda i: (_lat(i), 0)),
                   _full((8, ATW))],
        out_shape=[jax.ShapeDtypeStruct((T, WB), BF16), jax.ShapeDtypeStruct((S, HGW), F32),
                   jax.ShapeDtypeStruct((8, ATW), F32)],
        scratch_shapes=[], operands=[p_b, o, cos, sin, hnw, qnw, knw, dy_hg, dq, dk_rep, dv_rep])


NEG = -1e30
_CTX_BLOCKS = L // BLOCK


def _attn_window_specs():
    prev = pl.BlockSpec((BLOCK, ATW), lambda i: (jnp.maximum(i - 1, 0) + _CTX_BLOCKS, 0))
    own = pl.BlockSpec((BLOCK, ATW), lambda i: (i + _CTX_BLOCKS, 0))
    nxt = pl.BlockSpec((BLOCK, ATW), lambda i: (jnp.minimum(i + 1, N_BLOCKS - 1) + _CTX_BLOCKS, 0))
    return [prev, own, nxt, _full((L, ATW))]


def _attn_valid(i, heads, context):
    n_keys = 3 * BLOCK + (L if context else 0)
    qi = lax.broadcasted_iota(jnp.int32, (heads * BLOCK, n_keys), 0) % BLOCK
    kj = lax.broadcasted_iota(jnp.int32, (heads * BLOCK, n_keys), 1)
    window = ((jnp.abs(kj - BLOCK - qi) <= BLOCK) & ((kj >= BLOCK) | (i > 0))
              & ((kj < 2 * BLOCK) | (i < N_BLOCKS - 1)))
    return window | (kj >= 3 * BLOCK)


def _stack_heads(qg):
    lane = lax.broadcasted_iota(jnp.int32, (1, 256), 1) // HEAD_DIM
    return jnp.concatenate([jnp.where(lane == g, qg, jnp.zeros_like(qg)) for g in range(4)], axis=0)


def _unstack_heads(v4):
    lane = lax.broadcasted_iota(jnp.int32, (1, 256), 1) // HEAD_DIM
    out = jnp.where(lane == 0, v4[0:BLOCK], 0.0)
    for g in range(1, 4):
        out = out + jnp.where(lane == g, v4[g * BLOCK:(g + 1) * BLOCK], 0.0)
    return out


def _sink_rows(sink_ref, hk):
    return jnp.concatenate(
        [jnp.broadcast_to(sink_ref[0:1, 4 * hk + g:4 * hk + g + 1], (BLOCK, 1)) for g in range(4)], axis=0)


def _attn_fwd(q, k_rep, v_rep, sinks, carried=None):
    def body(q_ref, kp, ko, kn, kc, vp, vo, vn, vc, sink_ref, y_ref, lse_ref):
        i = pl.program_id(0)
        valid = _attn_valid(i, 1, True)
        lane8 = lax.broadcasted_iota(jnp.int32, (1, ATT_HEADS), 1)
        head_of_lane = lax.broadcasted_iota(jnp.int32, (1, 256), 1) // HEAD_DIM
        lse_out = jnp.zeros((BLOCK, ATT_HEADS), F32)
        for hk in range(KV_HEADS):
            sl = slice(256 * hk, 256 * (hk + 1))
            qg = q_ref[:, sl]
            keys = jnp.concatenate([kp[:, sl], ko[:, sl], kn[:, sl], kc[:, sl]], axis=0)
            vals = jnp.concatenate([vp[:, sl], vo[:, sl], vn[:, sl], vc[:, sl]], axis=0)
            yg = jnp.zeros((BLOCK, 256), F32)
            for g in range(4):
                q1 = jnp.where(head_of_lane == g, qg, jnp.zeros_like(qg))
                s = jnp.where(valid, _dot_nt(q1, keys), NEG)
                sink = sink_ref[0:1, 4 * hk + g:4 * hk + g + 1]
                m = jnp.maximum(jnp.max(s, axis=1, keepdims=True), sink)
                p = jnp.exp(s - m)
                den = jnp.sum(p, axis=1, keepdims=True) + jnp.exp(sink - m)
                o1 = _dot(p.astype(BF16), vals) * (1.0 / den)
                yg = yg + jnp.where(head_of_lane == g, o1, 0.0)
                lse_out = lse_out + jnp.where(lane8 == 4 * hk + g, m + jnp.log(den), 0.0)
            y_ref[:, sl] = yg.astype(BF16)
        lse_ref[...] = lse_out

    return _pcall(
        body, carried, name="attn_fwd", grid=(N_BLOCKS,),
        in_specs=[pl.BlockSpec((BLOCK, ATW), lambda i: (i, 0))] + _attn_window_specs()
        + _attn_window_specs() + [_full((1, ATT_HEADS))],
        out_specs=[pl.BlockSpec((BLOCK, ATW), lambda i: (i, 0)),
                   pl.BlockSpec((BLOCK, ATT_HEADS), lambda i: (i, 0))],
        out_shape=[jax.ShapeDtypeStruct((S, ATW), BF16), jax.ShapeDtypeStruct((S, ATT_HEADS), F32)],
        scratch_shapes=[],
        operands=[q, k_rep, k_rep, k_rep, k_rep, v_rep, v_rep, v_rep, v_rep, sinks])


def _attn_bwd(q, k_rep, v_rep, sinks, y_at, lse, dy, carried=None):
    def body(q_ref, kp, ko, kn, kc, vp, vo, vn, vc, sink_ref, y_ref, lse_ref, dy_ref,
             dq_ref, dk_ref, dv_ref, dsink_ref, dk_acc, dv_acc):
        i = pl.program_id(0)

        @pl.when(i == 0)
        def _():
            dk_acc[...] = jnp.zeros_like(dk_acc)
            dv_acc[...] = jnp.zeros_like(dv_acc)
            dk_ref[pl.ds(0, L), :] = jnp.zeros((L, ATW), F32)
            dv_ref[pl.ds(0, L), :] = jnp.zeros((L, ATW), F32)
            dsink_ref[...] = jnp.zeros_like(dsink_ref)

        valid = _attn_valid(i, 4, False)
        lane8 = lax.broadcasted_iota(jnp.int32, (1, ATT_HEADS), 1)
        w0 = pl.multiple_of(i * BLOCK, BLOCK)
        dsink = jnp.zeros((1, ATT_HEADS), F32)
        for hk in range(KV_HEADS):
            sl = slice(256 * hk, 256 * (hk + 1))
            q4 = _stack_heads(q_ref[:, sl])
            do4f = _stack_heads(dy_ref[:, sl])
            o4 = _stack_heads(y_ref[:, sl]).astype(F32)
            do4 = do4f.astype(BF16)
            kl = jnp.concatenate([kp[:, sl], ko[:, sl], kn[:, sl]], axis=0)
            vl = jnp.concatenate([vp[:, sl], vo[:, sl], vn[:, sl]], axis=0)
            lse4 = jnp.concatenate(
                [jnp.sum(jnp.where(lane8 == 4 * hk + g, lse_ref[...], 0.0), axis=1, keepdims=True)
                 for g in range(4)], axis=0)
            p_loc = jnp.where(valid, jnp.exp(_dot_nt(q4, kl) - lse4), 0.0)
            p_ctx = jnp.exp(_dot_nt(q4, kc[:, sl]) - lse4)
            delta = jnp.sum(do4f * o4, axis=1, keepdims=True)
            ds_loc = (p_loc * (_dot_nt(do4, vl) - delta)).astype(BF16)
            ds_ctx = (p_ctx * (_dot_nt(do4, vc[:, sl]) - delta)).astype(BF16)
            dq_ref[:, sl] = _unstack_heads(_dot(ds_loc, kl) + _dot(ds_ctx, kc[:, sl]))
            dk_acc[pl.ds(w0, 3 * BLOCK), sl] += _dot_tn(ds_loc, q4)
            dv_acc[pl.ds(w0, 3 * BLOCK), sl] += _dot_tn(p_loc.astype(BF16), do4)
            dk_ref[pl.ds(0, L), sl] += _dot_tn(ds_ctx, q4)
            dv_ref[pl.ds(0, L), sl] += _dot_tn(p_ctx.astype(BF16), do4)
            p_sink = jnp.exp(_sink_rows(sink_ref, hk) - lse4)
            for g in range(4):
                rows = slice(g * BLOCK, (g + 1) * BLOCK)
                dsink = dsink + jnp.where(lane8 == 4 * hk + g,
                                          -jnp.sum(p_sink[rows] * delta[rows], axis=0, keepdims=True), 0.0)
        dsink_ref[...] += dsink

        @pl.when(i == N_BLOCKS - 1)
        def _():
            dk_ref[pl.ds(L, S), :] = dk_acc[pl.ds(BLOCK, S), :]
            dv_ref[pl.ds(L, S), :] = dv_acc[pl.ds(BLOCK, S), :]

    row_q = pl.BlockSpec((BLOCK, ATW), lambda i: (i, 0))
    return _pcall(
        body, carried, name="attn_bwd", grid=(N_BLOCKS,),
        in_specs=[row_q] + _attn_window_specs() + _attn_window_specs()
        + [_full((1, ATT_HEADS)), row_q, pl.BlockSpec((BLOCK, ATT_HEADS), lambda i: (i, 0)), row_q],
        out_specs=[row_q, _full((T, ATW)), _full((T, ATW)), _full((1, ATT_HEADS))],
        out_shape=[jax.ShapeDtypeStruct((S, ATW), F32), jax.ShapeDtypeStruct((T, ATW), F32),
                   jax.ShapeDtypeStruct((T, ATW), F32), jax.ShapeDtypeStruct((1, ATT_HEADS), F32)],
        scratch_shapes=[pltpu.VMEM((S + 2 * BLOCK, ATW), F32), pltpu.VMEM((S + 2 * BLOCK, ATW), F32)],
        operands=[q, k_rep, k_rep, k_rep, k_rep, v_rep, v_rep, v_rep, v_rep, sinks, y_at, lse, dy])


def _merge_fwd(y_hg, y_at, p_c, x, w_bh, w_ba, w_out, g1, nfw, sh2, sc2, carried=None):
    def body(yh_ref, ya_ref, g_ref, x_ref, wbh_ref, wba_ref, wo_ref, g1_ref, nfw_ref, sh_ref, sc_ref,
             mx_ref, r_ref, x1_ref, h2_ref):
        a = _dot_nt(yh_ref[...], wbh_ref[...])
        b = _dot_nt(ya_ref[...], wba_ref[...])
        mixed = (_sigmoid(g_ref[:, :D]) * a + _sigmoid(g_ref[:, D:]) * b).astype(BF16)
        r = _dot(mixed, wo_ref[...])
        x1 = x_ref[...] + g1_ref[...] * r
        mx_ref[...] = mixed
        r_ref[...] = r
        x1_ref[...] = x1
        h2_ref[...] = _rms_mod(x1, nfw_ref[...], sh_ref[...], sc_ref[...]).astype(BF16)

    row = lambda w: pl.BlockSpec((TM, w), lambda i: (i, 0))
    vec = _full((1, D))
    return _pcall(
        body, carried, name="merge_fwd", grid=(N_LAT_TILES,),
        in_specs=[row(HGW), row(ATW), row(WC), row(D), _VMEM_WHOLE, _VMEM_WHOLE, _VMEM_WHOLE,
                  vec, vec, vec, vec],
        out_specs=[row(D)] * 4,
        out_shape=[jax.ShapeDtypeStruct((S, D), dt) for dt in (BF16, F32, F32, BF16)],
        scratch_shapes=[], operands=[y_hg, y_at, p_c, x, w_bh, w_ba, w_out, g1, nfw, sh2, sc2])


def _merge_bwd(dx1, r, y_hg, y_at, p_c, w_bh, w_ba, w_out, g1, carried=None):
    def body(dx_ref, r_ref, yh_ref, ya_ref, g_ref, wbh_ref, wba_ref, wo_ref, g1_ref,
             dr_ref, da_ref, db_ref, dg_ref, dyh_ref, dya_ref, acc_ref):
        @pl.when(pl.program_id(0) == 0)
        def _():
            acc_ref[...] = jnp.zeros_like(acc_ref)

        dx1v = dx_ref[...]
        acc_ref[0:1, :] += jnp.sum(dx1v * r_ref[...], axis=0, keepdims=True)
        dr = (g1_ref[...] * dx1v).astype(BF16)
        dr_ref[...] = dr
        dmix = _dot_nt(dr, wo_ref[...])
        sh, sa = _sigmoid(g_ref[:, :D]), _sigmoid(g_ref[:, D:])
        da = (dmix * sh).astype(BF16)
        db = (dmix * sa).astype(BF16)
        da_ref[...] = da
        db_ref[...] = db
        dg_ref[:, :D] = (dmix * _dot_nt(yh_ref[...], wbh_ref[...]) * sh * (1.0 - sh)).astype(BF16)
        dg_ref[:, D:] = (dmix * _dot_nt(ya_ref[...], wba_ref[...]) * sa * (1.0 - sa)).astype(BF16)
        dyh_ref[...] = _dot(da, wbh_ref[...])
        dya_ref[...] = _dot(db, wba_ref[...])

    row = lambda w: pl.BlockSpec((TM, w), lambda i: (i, 0))
    return _pcall(
        body, carried, name="merge_bwd", grid=(N_LAT_TILES,),
        in_specs=[row(D), row(D), row(HGW), row(ATW), row(WC), _VMEM_WHOLE, _VMEM_WHOLE, _VMEM_WHOLE,
                  _full((1, D))],
        out_specs=[row(D), row(D), row(D), row(WC), row(HGW), row(ATW), _full((8, D))],
        out_shape=[jax.ShapeDtypeStruct((S, D), BF16), jax.ShapeDtypeStruct((S, D), BF16),
                   jax.ShapeDtypeStruct((S, D), BF16), jax.ShapeDtypeStruct((S, WC), BF16),
                   jax.ShapeDtypeStruct((S, HGW), F32), jax.ShapeDtypeStruct((S, ATW), F32),
                   jax.ShapeDtypeStruct((8, D), F32)],
        scratch_shapes=[], operands=[dx1, r, y_hg, y_at, p_c, w_bh, w_ba, w_out, g1])


def _ffn_fused(x1, h2, tgt, w_gate, w_up, w_down, g2, nfw, sc2):
    def body(x1_ref, h2_ref, t_ref, wg_ref, wu_ref, wd_ref, g2_ref, nfw_ref, sc_ref,
             act_ref, dgt_ref, dup_ref, df_ref, dx_ref, acc_ref, gs, us):
        @pl.when(pl.program_id(0) == 0)
        def _():
            acc_ref[...] = jnp.zeros_like(acc_ref)

        h2 = h2_ref[...]
        whole = lambda w_ref: w_ref[...].reshape(D_FF, D)
        tile = lambda j: slice(j * FF_TILE, (j + 1) * FF_TILE)
        for j in range(N_FF_TILES):
            g = _dot_nt(h2, wg_ref[j])
            u = _dot_nt(h2, wu_ref[j])
            gs[j] = g
            us[j] = u
            act_ref[:, tile(j)] = (g * _sigmoid(g) * u).astype(BF16)
        f = _dot(act_ref[...], whole(wd_ref))
        x1v = x1_ref[...]
        g2 = g2_ref[...]
        diff = x1v + g2 * f - t_ref[...]
        dy = diff * (1.0 / D)
        df = (g2 * dy).astype(BF16)
        df_ref[...] = df
        dact_all = _dot_nt(df, whole(wd_ref))
        for j in range(N_FF_TILES):
            g, u = gs[j], us[j]
            sg = _sigmoid(g)
            dact = dact_all[:, tile(j)]
            dgt_ref[:, tile(j)] = (dact * u * (sg * (1.0 + g * (1.0 - sg)))).astype(BF16)
            dup_ref[:, tile(j)] = (dact * (g * sg)).astype(BF16)
        dh2 = _dot(dgt_ref[...], whole(wg_ref)) + _dot(dup_ref[...], whole(wu_ref))
        dx, dsh, dsc, dnw = _rms_mod_bwd(x1v, nfw_ref[...], sc_ref[...], dh2)
        dx_ref[...] = dy + dx
        acc_ref[0:1, :] += dsh
        acc_ref[1:2, :] += dsc
        acc_ref[2:3, :] += dnw
        acc_ref[3:4, :] += jnp.sum(dy * f, axis=0, keepdims=True)
        acc_ref[4:5, :] += 0.5 * jnp.sum(jnp.sum(diff * diff, axis=1, keepdims=True), axis=0,
                                         keepdims=True) * (1.0 / D)

    row = lambda dt_w: pl.BlockSpec((TM, dt_w), lambda i: (i, 0))
    blk = row(D_FF)
    vec = _full((1, D))
    return pl.pallas_call(
        body, name="ffn_fused", grid=(N_LAT_TILES,),
        in_specs=[row(D), row(D), row(D), _VMEM_WHOLE, _VMEM_WHOLE, _VMEM_WHOLE, vec, vec, vec],
        out_specs=[blk, blk, blk, row(D), row(D), _full((8, D))],
        out_shape=[jax.ShapeDtypeStruct((S, D_FF), BF16)] * 3
        + [jax.ShapeDtypeStruct((S, D), BF16), jax.ShapeDtypeStruct((S, D), F32),
           jax.ShapeDtypeStruct((8, D), F32)],
        scratch_shapes=[pltpu.VMEM((N_FF_TILES, TM, FF_TILE), F32), pltpu.VMEM((N_FF_TILES, TM, FF_TILE), F32)],
        compiler_params=_cp(("arbitrary",)),
    )(x1, h2, tgt, w_gate, w_up, w_down, g2, nfw, sc2)


def _proj_bc(h_all, w_b, w_c, carried=None):
    def body(h_ref, wb_ref, wc_ref, pb_ref, pc_ref):
        h = h_ref[...]
        pb_ref[...] = _dot_nt(h, wb_ref[...])

        @pl.when(pl.program_id(0) > 0)
        def _():
            pc_ref[...] = _dot_nt(h, wc_ref[...])

    return _pcall(
        body, carried, name="proj_bc", grid=(N_TILES,),
        in_specs=[pl.BlockSpec((TM, D), lambda i: (i, 0)), _VMEM_WHOLE, _VMEM_WHOLE],
        out_specs=[pl.BlockSpec((TM, WB), lambda i: (i, 0)), pl.BlockSpec((TM, WC), lambda i: (_lat(i), 0))],
        out_shape=[jax.ShapeDtypeStruct((T, WB), F32), jax.ShapeDtypeStruct((S, WC), F32)],
        scratch_shapes=[], operands=[h_all, w_b, w_c])


def _input_bwd(dp_a, dp_b, dp_c, w_a, w_b, w_c, ctx, x, dx1, nw, sh, sc, carried=None):
    def body(da_ref, db_ref, dc_ref, wa_ref, wb_ref, wc_ref, ctx_ref, x_ref, dx1_ref, nw_ref, sh_ref,
             sc_ref, gx_ref, acc_ref):
        i = pl.program_id(0)

        @pl.when(i == 0)
        def _():
            acc_ref[...] = jnp.zeros_like(acc_ref)

        dh = _dot(da_ref[...], wa_ref[...]) + _dot(db_ref[...], wb_ref[...])

        @pl.when(i == 0)
        def _():
            _, dsh, dsc, dnw = _rms_mod_bwd(ctx_ref[...], nw_ref[...], sc_ref[0:1, :], dh)
            acc_ref[3:4, :] += dsh
            acc_ref[4:5, :] += dsc
            acc_ref[2:3, :] += dnw

        @pl.when(i > 0)
        def _():
            dhl = dh + _dot(dc_ref[...], wc_ref[...])
            dx, dsh, dsc, dnw = _rms_mod_bwd(x_ref[...], nw_ref[...], sc_ref[1:2, :], dhl)
            gx_ref[...] = dx1_ref[...] + dx
            acc_ref[0:1, :] += dsh
            acc_ref[1:2, :] += dsc
            acc_ref[2:3, :] += dnw

    lat = lambda w: pl.BlockSpec((TM, w), lambda i: (_lat(i), 0))
    return _pcall(
        body, carried, name="input_bwd", grid=(N_TILES,),
        in_specs=[pl.BlockSpec((TM, WA), lambda i: (i, 0)), pl.BlockSpec((TM, WB), lambda i: (i, 0)),
                  lat(WC), _VMEM_WHOLE, _VMEM_WHOLE, _VMEM_WHOLE, _full((TM, D)), lat(D), lat(D),
                  _full((1, D)), _full((2, D)), _full((2, D))],
        out_specs=[lat(D), _full((8, D))],
        out_shape=[jax.ShapeDtypeStruct((S, D), F32), jax.ShapeDtypeStruct((8, D), F32)],
        scratch_shapes=[], operands=[dp_a, dp_b, dp_c, w_a, w_b, w_c, ctx, x, dx1, nw, sh, sc])


_C1 = 1.0 - ADAM_B1 ** ADAM_STEP
_C2 = 1.0 - ADAM_B2 ** ADAM_STEP


def _adamw_math(w, g, m, v):
    m = ADAM_B1 * m + (1.0 - ADAM_B1) * g
    v = ADAM_B2 * v + (1.0 - ADAM_B2) * (g * g)
    m_hat = m / _C1
    v_hat = v / _C2
    delta = -ADAM_LR * (m_hat / (jnp.sqrt(v_hat) + ADAM_EPS) + ADAM_WD * w)
    return delta, m, v


def _adamw_sharded(terms, w, m, v, name, tr, extra=None, after=None):
    rows, cols = w.shape

    def body(*refs):
        t_ref, w_ref, m_ref, v_ref = refs[:4]
        g_ref, d_ref, nm_ref, nv_ref = refs[-4:]
        g = t_ref[0].astype(F32)
        for s in range(1, N_CHIPS):
            g = g + t_ref[s].astype(F32)
        if extra is not None:
            g = g + refs[4][...].astype(F32)
        g_ref[...] = g
        d_ref[...], nm_ref[...], nv_ref[...] = _adamw_math(w_ref[...], g, m_ref[...], v_ref[...])

    blk = pl.BlockSpec((tr, cols), lambda i: (i, 0))
    return pl.pallas_call(
        body, name=name, grid=(rows // tr,),
        in_specs=[pl.BlockSpec((N_CHIPS, tr, cols), lambda i: (0, i, 0)), blk, blk, blk]
        + ([blk] if extra is not None else []) + ([_ANY] if after is not None else []),
        out_specs=[blk] * 4,
        out_shape=[jax.ShapeDtypeStruct((rows, cols), F32)] * 4,
        compiler_params=_cp(("parallel",)),
    )(terms, w, m, v, *([extra] if extra is not None else []), *([after] if after is not None else []))


def _adamw_plain(g, w, m, v, name, tr=None):
    def body(g_ref, w_ref, m_ref, v_ref, d_ref, nm_ref, nv_ref):
        d_ref[...], nm_ref[...], nv_ref[...] = _adamw_math(w_ref[...], g_ref[...], m_ref[...], v_ref[...])

    if tr is None:
        return pl.pallas_call(
            body, name=name, in_specs=[_VMEM_WHOLE] * 4, out_specs=[_VMEM_WHOLE] * 3,
            out_shape=[jax.ShapeDtypeStruct(w.shape, F32)] * 3,
            compiler_params=_cp(),
        )(g, w, m, v)
    blk = pl.BlockSpec((tr, w.shape[1]), lambda i: (i, 0))
    return pl.pallas_call(
        body, name=name, grid=(w.shape[0] // tr,), in_specs=[blk] * 4, out_specs=[blk] * 3,
        out_shape=[jax.ShapeDtypeStruct(w.shape, F32)] * 3,
        compiler_params=_cp(("parallel",)),
    )(g, w, m, v)


SMALL_ROWS = 16
R_DMOD, R_DCTX, R_NMIX, R_NFFN, R_MISC, R_DLB, R_BADA01 = 0, 6, 8, 9, 10, 11, 13
M_HNW, M_QNW, M_KNW, M_SINK, M_LOSS = 0, 128, 256, 384, 512


def _pack_small(acc_in, acc_mg, acc_ffn, acc_prep, dsink, dlb):
    def body(in_ref, mg_ref, ff_ref, pp_ref, ds_ref, dlb_ref, o_ref):
        o_ref[...] = jnp.zeros_like(o_ref)
        o_ref[0:2, :] = in_ref[0:2, :]
        o_ref[2:3, :] = mg_ref[0:1, :]
        o_ref[3:5, :] = ff_ref[0:2, :]
        o_ref[5:6, :] = ff_ref[3:4, :]
        o_ref[6:8, :] = in_ref[3:5, :]
        o_ref[8:9, :] = in_ref[2:3, :]
        o_ref[9:10, :] = ff_ref[2:3, :]
        o_ref[10:11, M_HNW:M_HNW + HG_DIM] = pp_ref[0:1, 0:HG_DIM]
        r = lax.broadcasted_iota(jnp.int32, (ATW, 128), 0)
        c = lax.broadcasted_iota(jnp.int32, (ATW, 128), 1)
        fold = jnp.where((r % HEAD_DIM == c) & (c < HEAD_DIM), 1.0, 0.0).astype(BF16)
        qk = jnp.concatenate([pp_ref[1:2, :], pp_ref[2:3, :], jnp.zeros((6, ATW), F32)], axis=0)
        folded = _dot_exact_rhs01(qk, fold)
        o_ref[10:11, M_QNW:M_QNW + 128] = folded[0:1, :]
        o_ref[10:11, M_KNW:M_KNW + 128] = folded[1:2, :]
        o_ref[10:11, M_SINK:M_SINK + ATT_HEADS] = ds_ref[...]
        o_ref[10:11, M_LOSS:M_LOSS + 128] = ff_ref[4:5, 0:128]
        o_ref[11:13, 0:HGW] = dlb_ref[...]

    return pl.pallas_call(
        body, name="pack_small", in_specs=[_VMEM_WHOLE] * 6, out_specs=_VMEM_WHOLE,
        out_shape=jax.ShapeDtypeStruct((SMALL_ROWS, D), F32), compiler_params=_cp(),
    )(acc_in, acc_mg, acc_ffn, acc_prep, dsink, dlb)


def _sum_small(gathered):
    def body(g_ref, o_ref):
        tot = g_ref[0]
        for s in range(1, N_DEV):
            tot = tot + g_ref[s]
        o_ref[...] = tot
        o_ref[R_BADA01:R_BADA01 + 2, :] = tot[0:2, :] + tot[R_DCTX:R_DCTX + 2, :]

    return pl.pallas_call(
        body, name="sum_small", in_specs=[_VMEM_WHOLE], out_specs=_VMEM_WHOLE,
        out_shape=jax.ShapeDtypeStruct((SMALL_ROWS, D), F32), compiler_params=_cp(),
    )(gathered)


_REP_NAMES = ("b_ada", "c_ctx", "norm_mix_w", "norm_ffn_w", "hgrn_norm_w", "q_norm_w", "k_norm_w", "attn_sinks")


def _adamw_replicated(tot, g_c_ctx, ws, ms, vs):
    n = len(_REP_NAMES)

    def body(*refs):
        tot_ref, gc_ref = refs[0], refs[1]
        w_refs, m_refs, v_refs = refs[2:2 + n], refs[2 + n:2 + 2 * n], refs[2 + 2 * n:2 + 3 * n]
        outs = refs[2 + 3 * n:]
        row = lambda r: tot_ref[r:r + 1, :]
        misc = row(R_MISC)
        grads = [jnp.concatenate([row(R_BADA01), row(R_BADA01 + 1)] + [row(k) for k in range(2, 6)], axis=1),
                 gc_ref[...], row(R_NMIX), row(R_NFFN),
                 misc[:, M_HNW:M_HNW + HG_DIM], misc[:, M_QNW:M_QNW + HEAD_DIM],
                 misc[:, M_KNW:M_KNW + HEAD_DIM], misc[:, M_SINK:M_SINK + ATT_HEADS]]
        for k in range(n):
            outs[k][...] = grads[k]
            outs[n + k][...], outs[2 * n + k][...], outs[3 * n + k][...] = _adamw_math(
                w_refs[k][...], grads[k], m_refs[k][...], v_refs[k][...])

    shapes = [jax.ShapeDtypeStruct(w.shape, F32) for w in ws]
    return pl.pallas_call(
        body, name="adamw_replicated", in_specs=[_VMEM_WHOLE] * (2 + 3 * n), out_specs=[_VMEM_WHOLE] * (4 * n),
        out_shape=shapes * 4, compiler_params=_cp(),
    )(tot, g_c_ctx, *ws, *ms, *vs)


def _lb_grads(dlb, lbl):
    def body(d_ref, l_ref, o_ref):
        for d in (0, 1):
            ll = l_ref[d]
            lb = _sigmoid(ll[0:1, :] - ll[1:2, :])
            t = d_ref[d:d + 1, :] * lb * (1.0 - lb)
            o_ref[d, 0:1, :] = t
            o_ref[d, 1:2, :] = -t

    return pl.pallas_call(
        body, name="lb_grads", in_specs=[_VMEM_WHOLE] * 2, out_specs=_VMEM_WHOLE,
        out_shape=jax.ShapeDtypeStruct((2, 2, HGW), F32), compiler_params=_cp(),
    )(dlb, lbl)


def _c_ctx_grad(terms, c_ctx):
    def body(t_ref, c_ref, o_ref):
        tot = t_ref[0, 8:9, :]
        for s in range(1, N_DEV):
            tot = tot + t_ref[s, 8:9, :]
        cv = c_ref[...]
        sg = _sigmoid(cv)
        o_ref[...] = tot * (sg * (1.0 + cv * (1.0 - sg)))

    return pl.pallas_call(
        body, name="c_ctx_grad", in_specs=[_VMEM_WHOLE] * 2, out_specs=_VMEM_WHOLE,
        out_shape=jax.ShapeDtypeStruct((1, D), F32), compiler_params=_cp(),
    )(terms, c_ctx)


def _in_perm():
    fz, bz, inp, kk, vv, qhg, ghg, qat, gates = 0, 512, 1024, 1536, 1664, 1792, 2304, 2816, 3328
    cols = []
    for h in range(HG_HEADS):
        for base in (fz, bz, inp, qhg):
            cols += list(range(base + 128 * h, base + 128 * (h + 1)))
    cols += list(range(ghg, ghg + 512)) + list(range(qat, qat + 512))
    cols += list(range(kk, kk + 128)) + list(range(vv, vv + 128))
    cols += list(range(gates, gates + 2048))
    return np.asarray(cols, np.int32)


_PERM = _in_perm()


_PIECES = {"a": (0, WA, 128), "b": (WA, WB, 256), "c": (WA + WB, WC, 256)}


def _block_table(piece):
    lo, n, blk = _PIECES[piece]
    starts = [int(_PERM[r]) for r in range(lo, lo + n, blk)]
    assert all(s % blk == 0 and np.array_equal(_PERM[r:r + blk], np.arange(s, s + blk))
               for s, r in zip(starts, range(lo, lo + n, blk)))
    return jnp.asarray([s // blk for s in starts], jnp.int32), blk


def _pick_row_blocks(x, table, blk, name):
    cols = x.shape[1]

    def body(t_ref, x_ref, o_ref):
        o_ref[...] = x_ref[...]

    return pl.pallas_call(
        body, name=name,
        grid_spec=pltpu.PrefetchScalarGridSpec(
            num_scalar_prefetch=1, grid=(table.shape[0],),
            in_specs=[pl.BlockSpec((blk, cols), lambda i, t: (t[i], 0))],
            out_specs=pl.BlockSpec((blk, cols), lambda i, t: (i, 0))),
        out_shape=jax.ShapeDtypeStruct((table.shape[0] * blk, cols), x.dtype),
        compiler_params=_cp(("arbitrary",)),
    )(table, x)


def _place_row_blocks(x, table, blk, into, out_rows, name):
    cols = x.shape[1]

    def body(t_ref, x_ref, *rest):
        rest[-1][...] = x_ref[...]

    operands, in_specs, aliases = [table, x], [pl.BlockSpec((blk, cols), lambda i, t: (i, 0))], {}
    if into is not None:
        operands.append(into)
        in_specs.append(_ANY)
        aliases = {2: 0}
    return pl.pallas_call(
        body, name=name,
        grid_spec=pltpu.PrefetchScalarGridSpec(
            num_scalar_prefetch=1, grid=(table.shape[0],), in_specs=in_specs,
            out_specs=pl.BlockSpec((blk, cols), lambda i, t: (t[i], 0))),
        out_shape=jax.ShapeDtypeStruct((out_rows, cols), x.dtype),
        input_output_aliases=aliases,
        compiler_params=_cp(("arbitrary",)),
    )(*operands)


def _local_step(x2, ctx2, h_all, h_lat, tgt, lbl, sh_in, sc_in, gate1, sh2, sc2, gate2, norm_mix_w, norm_ffn_w,
                hgrn_norm_w, q_norm_w, k_norm_w, attn_sinks, w_a, w_b, w_c, s_bh, s_ba, s_out,
                s_gate, s_up, s_down):
    first_last = lambda n: [(0, True), (n - 1, False)]
    p_a = _mm_nt(h_all, w_a, tm=T, tn=512, out_dtype=F32, name="proj_a")
    (o, st), (g_gate, g_bh, g_ba) = _hgrn_fwd(
        p_a, lbl, (_gather_comm_relayed([s_gate, s_bh, s_ba]),
                   [(0, True), (HG_HEADS - 2, True), (HG_HEADS - 1, False)]))
    (p_b, p_c), (g_out,) = _proj_bc(
        h_all, w_b, w_c, (_gather_comm_relayed([s_out]), [(0, True), (N_TILES - 4, True), (N_TILES - 1, False)]))
    cos, sin = _rope_tables()
    qnw_t, knw_t = jnp.tile(q_norm_w, (1, ATT_HEADS)), jnp.tile(k_norm_w, (1, KV_HEADS))
    y_hg, qn, k_rep, v_rep = _prep_fwd(p_b, o, cos, sin, hgrn_norm_w, qnw_t, knw_t)
    (y_at, lse), (g_up, g_down) = _attn_fwd(
        qn, k_rep, v_rep, attn_sinks,
        (_gather_comm_relayed([s_up, s_down]), [(0, True), (N_BLOCKS - 6, True), (N_BLOCKS - 1, False)]))
    w_bh, w_ba, w_o = g_bh.reshape(D, HGW), g_ba.reshape(D, ATW), g_out.reshape(D, D)
    (mixed, r, x1, h2), _ = _merge_fwd(
        y_hg, y_at, p_c, x2, w_bh, w_ba, w_o, gate1, norm_ffn_w, sh2, sc2)
    g_gate, g_up, g_down = [g.reshape(N_FF_TILES, FF_TILE, D) for g in (g_gate, g_up, g_down)]

    act, d_gate, d_up, d_f, dx1, acc_ffn = _ffn_fused(x1, h2, tgt, g_gate, g_up, g_down, gate2,
                                                      norm_ffn_w, sc2)
    by_chip = lambda t: t.reshape((N_CHIPS, 2) + t.shape[1:])
    ff_by_chip = lambda t: t.reshape(N_CHIPS, 2, FF_BLK, D)
    t_down, _ = _mm_tn_blocked(act, d_f, "grad_down", N_FF_TILES)
    t_down = ff_by_chip(t_down)
    t_gate, (f_down,) = _mm_tn_blocked(d_gate, h2, "grad_gate", N_FF_HALVES,
                                       (_sibling_comm([t_down]), first_last(N_FF_HALVES)))
    t_gate = ff_by_chip(t_gate)
    t_up, (f_gate,) = _mm_tn_blocked(d_up, h2, "grad_up", N_FF_HALVES,
                                     (_sibling_comm([t_gate]), first_last(N_FF_HALVES)))
    t_up = ff_by_chip(t_up)

    (d_r, d_a, d_b, dp_c, dy_hg, dy_at, acc_mg), (f_up,) = _merge_bwd(
        dx1, r, y_hg, y_at, p_c, w_bh, w_ba, w_o, gate1, (_sibling_comm([t_up]), first_last(N_LAT_TILES)))
    c_down, c_gate, c_up = [_pair_sum(t, f, "pair_sum_" + nm) for t, f, nm in
                            ((t_down, f_down, "down"), (t_gate, f_gate, "gate"), (t_up, f_up, "up"))]
    t_out = _mm_tn(mixed, d_r, tk=1024, nk=2, tm=1024, tn=1024, out_dtype=BF16, name="grad_out")
    t_bh = _mm_tn(d_a, y_hg, tk=2048, nk=1, tm=1024, tn=512, out_dtype=BF16, name="grad_bh")
    t_ba = _mm_tn(d_b, y_at, tk=2048, nk=1, tm=1024, tn=512, out_dtype=BF16, name="grad_ba")
    t_bh, t_ba, t_out = [by_chip(t.reshape(N_DEV, D // N_DEV, t.shape[1])) for t in (t_bh, t_ba, t_out)]
    (dq, dk_rep, dv_rep, dsink), (r_up,) = _attn_bwd(
        qn, k_rep, v_rep, attn_sinks, y_at, lse, dy_at, (_chip_comm([c_up]), first_last(N_BLOCKS)))
    (dp_b, d_o, acc_prep), (f_bh, f_ba, f_out) = _prep_bwd(
        p_b, o, cos, sin, hgrn_norm_w, qnw_t, knw_t, dy_hg, dq, dk_rep, dv_rep,
        (_sibling_comm([t_bh, t_ba, t_out]), first_last(N_TILES)))
    c_bh, c_ba, c_out = [_pair_sum(t, f, "pair_sum_" + nm) for t, f, nm in
                         ((t_bh, f_bh, "bh"), (t_ba, f_ba, "ba"), (t_out, f_out, "out"))]
    (dp_a, dlb), (r_bh, r_ba, r_out, r_down, r_gate) = _hgrn_bwd(
        p_a, lbl, d_o, st, (_chip_comm([c_bh, c_ba, c_out, c_down, c_gate]), first_last(HG_HEADS)))
    t_a = _mm_tn(dp_a, h_all, tk=T, nk=1, tm=1024, tn=1024, out_dtype=BF16, name="grad_in_a")
    t_b = _mm_tn(dp_b, h_all, tk=T, nk=1, tm=640, tn=1024, out_dtype=BF16, name="grad_in_b")
    t_c = _mm_tn(dp_c, h_lat, tk=1024, nk=2, tm=1024, tn=1024, out_dtype=BF16, name="grad_in_c")
    t_in = None
    for piece, nm in ((t_a, "a"), (t_b, "b"), (t_c, "c")):
        t_in = _place_row_blocks(piece, *_block_table(nm), t_in, IN_COLS, "order_terms_" + nm)
    t_in = by_chip(t_in.reshape(N_DEV, IN_BLK, D))
    (f_in,) = _run_comm(_sibling_comm([t_in]), "scatter_in_sibling")
    c_in = _pair_sum(t_in, f_in, "pair_sum_in")
    sems, c_in, land, token = _chip_exchange_start(c_in, jnp.zeros(c_in.shape, c_in.dtype))
    (grad_x, acc_in), _ = _input_bwd(dp_a, dp_b, dp_c, w_a, w_b, w_c, ctx2, x2, dx1,
                                     norm_mix_w + token[0, 0], sh_in, sc_in)
    small = _pack_small(acc_in, acc_mg, acc_ffn, acc_prep, dsink, dlb)
    return grad_x, small, [r_bh, r_ba, r_out, r_gate, r_up, r_down], (sems, c_in, land)


def kernel(x, c, ctx, c_ctx, w_ada, b_ada, norm_mix_w, norm_ffn_w, w_in, hgrn_lb_logits, hgrn_norm_w, q_norm_w, k_norm_w, attn_sinks, w_branch_hgrn, w_branch_attn, w_out, w_ffn_gate, w_ffn_up, w_ffn_down, loss_target, m_c_ctx, m_w_ada, m_b_ada, m_norm_mix_w, m_norm_ffn_w, m_w_in, m_hgrn_lb_logits, m_hgrn_norm_w, m_q_norm_w, m_k_norm_w, m_attn_sinks, m_w_branch_hgrn, m_w_branch_attn, m_w_out, m_w_ffn_gate, m_w_ffn_up, m_w_ffn_down, v_c_ctx, v_w_ada, v_b_ada, v_norm_mix_w, v_norm_ffn_w, v_w_in, v_hgrn_lb_logits, v_hgrn_norm_w, v_q_norm_w, v_k_norm_w, v_attn_sinks, v_w_branch_hgrn, v_w_branch_attn, v_w_out, v_w_ffn_gate, v_w_ffn_up, v_w_ffn_down):
    me = 4 * lax.axis_index("x") + 2 * lax.axis_index("y") + lax.axis_index("c")
    x2, ctx2, tgt = x[0], ctx[0], loss_target[0]
    w_ada2, w_in2 = w_ada[0], w_in[0]

    cond = jnp.zeros((8, D), F32).at[0].set(c[0]).at[1, :256].set(hgrn_lb_logits.reshape(256))
    b_cols = lax.dynamic_slice(b_ada, (0, me * ADA_BLK), (1, ADA_BLK))
    g0, cc, mod, g_in, h_all, h_lat = _prologue(cond, c_ctx.reshape(1, D), w_ada2, b_cols, w_in2.T.astype(BF16),
                                         x2, ctx2, norm_mix_w)
    lbl = jnp.transpose(g0[:, 1, :256].reshape(N_DEV, 2, 2, 64), (1, 2, 0, 3)).reshape(2, 2, HGW)
    sh1, sc1, gate1, sh2, sc2, gate2 = [mod[k:k + 1] for k in range(6)]
    sh_in = jnp.concatenate([mod[6:7], sh1], axis=0)
    sc_in = jnp.concatenate([mod[7:8], sc1], axis=0)

    shards = [w_branch_hgrn[0].T, w_branch_attn[0].T, w_out[0], w_ffn_gate[0].T, w_ffn_up[0].T, w_ffn_down[0]]
    w_in_t = g_in.reshape(IN_COLS, D)
    w_a, w_b, w_c = [_pick_row_blocks(w_in_t, *_block_table(nm), "order_w_" + nm) for nm in "abc"]

    grad_x, small, (r_bh, r_ba, r_out, r_gate, r_up, r_down), pending_in = _local_step(
        x2, ctx2, h_all, h_lat, tgt, lbl, sh_in, sc_in, gate1, sh2, sc2, gate2, norm_mix_w, norm_ffn_w, hgrn_norm_w,
        q_norm_w, k_norm_w, attn_sinks, w_a, w_b, w_c, *[s.astype(BF16) for s in shards])

    big = {}
    for nm, rr, ww, mm, vv, tr, transposed in (
            ("w_branch_hgrn", r_bh, w_branch_hgrn[0], m_w_branch_hgrn[0], v_w_branch_hgrn[0], 128, True),
            ("w_branch_attn", r_ba, w_branch_attn[0], m_w_branch_attn[0], v_w_branch_attn[0], 128, True),
            ("w_out", r_out, w_out[0], m_w_out[0], v_w_out[0], 128, False),
            ("w_ffn_gate", r_gate, w_ffn_gate[0], m_w_ffn_gate[0], v_w_ffn_gate[0], 176, True),
            ("w_ffn_up", r_up, w_ffn_up[0], m_w_ffn_up[0], v_w_ffn_up[0], 176, True),
            ("w_ffn_down", r_down, w_ffn_down[0], m_w_ffn_down[0], v_w_ffn_down[0], 176, False)):
        if transposed:
            res = _adamw_sharded(rr, ww.T, mm.T, vv.T, "adamw_" + nm, tr, after=grad_x)
            big[nm] = [t.T[None] for t in res]
        else:
            big[nm] = [t[None] for t in _adamw_sharded(rr, ww, mm, vv, "adamw_" + nm, tr, after=grad_x)]

    (g2,) = _all_gather([small], "gather_small", True, after=[res[1] for res in big.values()])
    tot = _sum_small(g2)
    dm = jnp.zeros((16, 6 * D), F32).at[:8].set(g2[:, R_DMOD:R_DMOD + 6, :].reshape(N_DEV, 6 * D))
    dm = dm.at[8, :2 * D].set(tot[R_DCTX:R_DCTX + 2].reshape(2 * D))
    dm_cols = lax.dynamic_slice(dm, (0, me * ADA_BLK), (16, ADA_BLK))
    g_w_ada, dsc_term = _ada_grads(cc, dm_cols, w_ada2)
    (g3,) = _all_gather([dsc_term], "gather_cctx", True)
    g_c_ctx = _c_ctx_grad(g3, c_ctx.reshape(1, D))
    g_lbl = _lb_grads(tot[R_DLB:R_DLB + 2, :HGW], lbl)
    g_lb_mine = lax.dynamic_slice(g_lbl, (0, 0, me * 64), (2, 2, 64))
    misc = tot[R_MISC]
    loss = misc[M_LOSS]

    rep_out = _adamw_replicated(
        tot, g_c_ctx,
        [b_ada, c_ctx.reshape(1, D), norm_mix_w, norm_ffn_w, hgrn_norm_w, q_norm_w, k_norm_w, attn_sinks],
        [m_b_ada, m_c_ctx.reshape(1, D), m_norm_mix_w, m_norm_ffn_w, m_hgrn_norm_w, m_q_norm_w, m_k_norm_w,
         m_attn_sinks],
        [v_b_ada, v_c_ctx.reshape(1, D), v_norm_mix_w, v_norm_ffn_w, v_hgrn_norm_w, v_q_norm_w, v_k_norm_w,
         v_attn_sinks])
    rep = []
    for kind in range(4):
        vals = dict(zip(_REP_NAMES, rep_out[kind * len(_REP_NAMES):(kind + 1) * len(_REP_NAMES)]))
        vals["c_ctx"] = vals["c_ctx"].reshape(D)
        rep.append(vals)

    sems, c_in, land = pending_in
    d_ada, nm_ada, nv_ada = _adamw_plain(g_w_ada, w_ada2, m_w_ada[0], v_w_ada[0], "adamw_w_ada", tr=256)
    land = _chip_exchange_wait(sems, c_in, land, d_ada)
    own = lax.dynamic_index_in_dim(c_in, 2 * lax.axis_index("x") + lax.axis_index("y"), 0, keepdims=False)
    big["w_in"] = [t.T[None] for t in _adamw_sharded(land, w_in2.T, m_w_in[0].T, v_w_in[0].T, "adamw_w_in", 336,
                                                     extra=own)]
    ada = [t[None] for t in (g_w_ada, d_ada, nm_ada, nv_ada)]
    lb_w = hgrn_lb_logits.reshape(4, 64)
    d_lb, nm_lb, nv_lb = _adamw_plain(g_lb_mine.reshape(4, 64), lb_w, m_hgrn_lb_logits.reshape(4, 64),
                                      v_hgrn_lb_logits.reshape(4, 64), "adamw_lb")
    lbs = [t.reshape(2, 2, 64) for t in (g_lb_mine, d_lb, nm_lb, nv_lb)]

    names = ['c_ctx', 'w_ada', 'b_ada', 'norm_mix_w', 'norm_ffn_w', 'w_in', 'hgrn_lb_logits', 'hgrn_norm_w',
             'q_norm_w', 'k_norm_w', 'attn_sinks', 'w_branch_hgrn', 'w_branch_attn', 'w_out', 'w_ffn_gate',
             'w_ffn_up', 'w_ffn_down']
    outs = [loss, grad_x[None]]
    for kind in range(4):
        for nm in names:
            if nm == 'w_ada':
                outs.append(ada[kind])
            elif nm == 'hgrn_lb_logits':
                outs.append(lbs[kind])
            elif nm in big:
                outs.append(big[nm][kind])
            else:
                outs.append(rep[kind][nm])
    return tuple(outs)
```

```python
import functools
import math

import numpy as np
import jax
import jax.numpy as jnp
from jax import lax
from jax.experimental import pallas as pl
from jax.experimental.pallas import tpu as pltpu

F32 = jnp.float32
BF16 = jnp.bfloat16

N_DEV = 8
D = 1024
S = 2048
L = 256
T = L + S
TM = 256
N_TILES = T // TM
N_LAT_TILES = S // TM
HG_HEADS = 4
HG_DIM = 128
HGW = 512
CHUNK = 32
N_CHUNKS = T // CHUNK
N_CTX_CHUNKS = L // CHUNK
ATT_HEADS = 8
KV_HEADS = 2
HEAD_DIM = 64
ATW = 512
KVW = 128
BLOCK = 128
N_BLOCKS = S // BLOCK
GRID_W = 64
ROPE_THETA = 10000.0
D_FF = 2816
FF_BLK = D_FF // N_DEV
FF_TILE = 256
N_FF_TILES = D_FF // FF_TILE
N_FF_HALVES = 2
IN_COLS = 5376
IN_BLK = IN_COLS // N_DEV
ADA_BLK = 6 * D // N_DEV
EPS = 1e-6
WA, WB, WC = 2048, 1280, 2048

ADAM_LR = 0.001
ADAM_B1 = 0.9
ADAM_B2 = 0.999
ADAM_EPS = 1e-08
ADAM_WD = 0.01
ADAM_STEP = 10

VMEM_LIMIT = 56 * 1024 * 1024
MESH = pl.DeviceIdType.MESH


def _cp(sem=None, vmem=VMEM_LIMIT):
    return pltpu.CompilerParams(dimension_semantics=sem, vmem_limit_bytes=vmem)


def _full(shape):
    n = len(shape)
    return pl.BlockSpec(shape, lambda *_: (0,) * n)


_VMEM_WHOLE = pl.BlockSpec(memory_space=pltpu.VMEM)
_ANY = pl.BlockSpec(memory_space=pl.ANY)


def _sigmoid(v):
    return 1.0 / (1.0 + jnp.exp(-v))


def _dot(a, b):
    return jnp.dot(a, b, preferred_element_type=F32)


def _dot_nt(a, b):
    return lax.dot_general(a, b, (((1,), (1,)), ((), ())), preferred_element_type=F32)


def _dot_tn(a, b):
    return lax.dot_general(a, b, (((0,), (0,)), ((), ())), preferred_element_type=F32)


def _split3(v):
    hi = v.astype(BF16)
    r = v - hi.astype(F32)
    mid = r.astype(BF16)
    lo = (r - mid.astype(F32)).astype(BF16)
    return hi, mid, lo


def _dot_exact_rhs01(v, m01):
    hi, mid, lo = _split3(v)
    return _dot(hi, m01) + _dot(mid, m01) + _dot(lo, m01)


def _split2(v):
    hi = v.astype(BF16)
    return hi, (v - hi.astype(F32)).astype(BF16)


def _dot_lhs01(m01, v):
    hi, lo = _split2(v)
    return _dot(m01, hi) + _dot(m01, lo)


def _dot_f32(a, b, dot=_dot):
    ah, am, al = _split3(a)
    bh, bm, bl = _split3(b)
    return (dot(ah, bh) + (dot(ah, bm) + dot(am, bh))
            + (dot(am, bm) + dot(ah, bl) + dot(al, bh)))


def _my_pos():
    return lax.axis_index("x"), lax.axis_index("y"), lax.axis_index("c")


class _Comm:
    def __init__(self, operands, out_shapes, sems, phases):
        self.operands, self.out_shapes, self.sems, self.phases = operands, out_shapes, sems, phases


def _gather_comm(blocks):
    n = len(blocks)

    def parts(ins, outs, sems):
        send_sems, recv_sems, local_sems = sems
        x, y, c = _my_pos()
        me, sibling = (x, y, c), (x, y, 1 - c)
        chips = [(1 - x, y), (x, 1 - y), (1 - x, 1 - y)]

        def slot(a, px, py, pc):
            return outs[a].at[4 * px + 2 * py + pc]

        def copy(a, k, block, to, src=None):
            return pltpu.make_async_remote_copy(
                src_ref=slot(a, *block) if src is None else src, dst_ref=slot(a, *block),
                send_sem=send_sems.at[a, k], recv_sem=recv_sems.at[a, k],
                device_id=to, device_id_type=MESH)

        mine = [pltpu.make_async_copy(ins[a], slot(a, *me), local_sems.at[a]) for a in range(n)]
        first = []
        for a in range(n):
            first.append(copy(a, 0, me, sibling, src=ins[a]))
            first += [copy(a, 1 + j, me, (*chip, c), src=ins[a]) for j, chip in enumerate(chips)]
        passed = [copy(a, 4 + j, (*chip, c), sibling) for j, chip in enumerate(chips) for a in range(n)]
        return c, me, sibling, chips, copy, mine, first, passed

    def start(ins, outs, sems):
        _, _, _, _, _, mine, first, _ = parts(ins, outs, sems)
        for cp in mine + first:
            cp.start()

    def forward(ins, outs, sems):
        c, me, _, chips, copy, _, _, passed = parts(ins, outs, sems)
        for j, chip in enumerate(chips):
            for a in range(n):
                copy(a, 1 + j, (*chip, c), me).wait_recv()
                passed[j * n + a].start()

    def finish(ins, outs, sems):
        c, me, sibling, chips, copy, mine, first, passed = parts(ins, outs, sems)
        for a in range(n):
            copy(a, 0, sibling, me).wait_recv()
            for j, chip in enumerate(chips):
                copy(a, 4 + j, (*chip, 1 - c), me).wait_recv()
        for cp in first + passed:
            cp.wait_send()
        for cp in mine:
            cp.wait()

    return _Comm(blocks, [jax.ShapeDtypeStruct((N_DEV,) + b.shape, b.dtype) for b in blocks],
                 [pltpu.SemaphoreType.DMA((n, 7)), pltpu.SemaphoreType.DMA((n, 7)), pltpu.SemaphoreType.DMA((n,))],
                 [start, forward, finish])


def _gather_comm_relayed(blocks):
    n = len(blocks)

    def parts(ins, outs, sems):
        send_sems, recv_sems, local_sems = sems
        x, y, c = _my_pos()
        me, sibling = (x, y, c), (x, y, 1 - c)
        x_nbr, y_nbr, diag = (1 - x, y, c), (x, 1 - y, c), (1 - x, 1 - y, c)

        def slot(a, dev, half=None):
            ref = outs[a].at[4 * dev[0] + 2 * dev[1] + dev[2]]
            if half is None:
                return ref
            rows = blocks[a].shape[0] // 2
            return ref.at[pl.ds(half * rows, rows)]

        def copy(a, k, block, to, half=None, src=None):
            return pltpu.make_async_remote_copy(
                src_ref=slot(a, block, half) if src is None else src, dst_ref=slot(a, block, half),
                send_sem=send_sems.at[a, k], recv_sem=recv_sems.at[a, k],
                device_id=to, device_id_type=MESH)

        mine = [pltpu.make_async_copy(ins[a], slot(a, me), local_sems.at[a]) for a in range(n)]
        return me, sibling, x_nbr, y_nbr, diag, copy, mine

    def start(ins, outs, sems):
        me, sibling, x_nbr, y_nbr, _, copy, mine = parts(ins, outs, sems)
        for cp in mine:
            cp.start()
        for a in range(n):
            for k, to in ((1, x_nbr), (2, y_nbr), (0, sibling)):
                copy(a, k, me, to, src=ins[a]).start()

    def forward(ins, outs, sems):
        me, sibling, x_nbr, y_nbr, _, copy, _ = parts(ins, outs, sems)
        for a in range(n):
            copy(a, 1, x_nbr, me).wait_recv()
            copy(a, 3, x_nbr, y_nbr, half=0).start()
            copy(a, 5, x_nbr, sibling).start()
        for a in range(n):
            copy(a, 2, y_nbr, me).wait_recv()
            copy(a, 4, y_nbr, x_nbr, half=1).start()
            copy(a, 6, y_nbr, sibling).start()

    def finish(ins, outs, sems):
        me, sibling, x_nbr, y_nbr, diag, copy, mine = parts(ins, outs, sems)
        sib = lambda dev: (dev[0], dev[1], sibling[2])
        for a in range(n):
            copy(a, 3, diag, me, half=0).wait_recv()
            copy(a, 4, diag, me, half=1).wait_recv()
            copy(a, 7, diag, sibling).start()
        for a in range(n):
            copy(a, 0, sibling, me).wait_recv()
            for k, dev in ((5, x_nbr), (6, y_nbr), (7, diag)):
                copy(a, k, sib(dev), me).wait_recv()
        for a in range(n):
            for k, block, to, half in ((0, me, sibling, None), (1, me, x_nbr, None), (2, me, y_nbr, None),
                                       (3, x_nbr, y_nbr, 0), (4, y_nbr, x_nbr, 1), (5, x_nbr, sibling, None),
                                       (6, y_nbr, sibling, None), (7, diag, sibling, None)):
                copy(a, k, block, to, half=half, src=ins[a] if block is me else None).wait_send()
        for cp in mine:
            cp.wait()

    return _Comm(blocks, [jax.ShapeDtypeStruct((N_DEV,) + b.shape, b.dtype) for b in blocks],
                 [pltpu.SemaphoreType.DMA((n, 8)), pltpu.SemaphoreType.DMA((n, 8)), pltpu.SemaphoreType.DMA((n,))],
                 [start, forward, finish])


_HBM = pl.BlockSpec(memory_space=pltpu.HBM)
_SEM = pl.BlockSpec(memory_space=pltpu.SEMAPHORE)
_SPLIT_COPY = pltpu.CompilerParams(has_side_effects=pltpu.SideEffectType.DATAFLOW_SIDE_EFFECTING)


def _chip_exchange_copies(src_ref, land_ref, sems):
    x, y, c = _my_pos()
    q_me = 2 * x + y
    pairs = []
    for j, (px, py) in enumerate([(1 - x, y), (x, 1 - y), (1 - x, 1 - y)]):
        q = 2 * px + py
        send = pltpu.make_async_remote_copy(
            src_ref=src_ref.at[q], dst_ref=land_ref.at[q_me], send_sem=sems[j], recv_sem=sems[3 + j],
            device_id=(px, py, c), device_id_type=MESH)
        recv = pltpu.make_async_remote_copy(
            src_ref=src_ref.at[q], dst_ref=land_ref.at[q], send_sem=sems[j], recv_sem=sems[3 + j],
            device_id=(x, y, c), device_id_type=MESH)
        pairs.append((send, recv))
    return pairs


def _chip_exchange_start(src, land):
    def body(src_ref, land_ref, *outs):
        sems, token = outs[:6], outs[8]
        for send, _ in _chip_exchange_copies(src_ref, land_ref, sems):
            send.start()
        token[...] = jnp.zeros_like(token)

    res = pl.pallas_call(
        body, name="scatter_in_start",
        out_shape=(pltpu.SemaphoreType.DMA(()),) * 6 + (
            pltpu.HBM(src.shape, src.dtype), pltpu.HBM(land.shape, land.dtype),
            jax.ShapeDtypeStruct((8, 128), F32)),
        in_specs=(_HBM, _HBM), out_specs=(_SEM,) * 6 + (_HBM, _HBM, pl.BlockSpec(memory_space=pltpu.VMEM)),
        input_output_aliases={0: 6, 1: 7}, compiler_params=_SPLIT_COPY,
    )(pltpu.with_memory_space_constraint(src, pltpu.HBM), pltpu.with_memory_space_constraint(land, pltpu.HBM))
    return res[:6], res[6], res[7], res[8]


def _chip_exchange_wait(sems, src_thru, land_thru, after):
    def body(src_ref, land_ref, *rest):
        for send, recv in _chip_exchange_copies(src_ref, land_ref, rest[:6]):
            send.wait_send()
            recv.wait_recv()

    return pl.pallas_call(
        body, name="scatter_in_wait",
        out_shape=(pltpu.HBM(src_thru.shape, src_thru.dtype), pltpu.HBM(land_thru.shape, land_thru.dtype)),
        in_specs=(_HBM, _HBM) + (_SEM,) * 6 + (_ANY,), out_specs=(_HBM, _HBM),
        input_output_aliases={0: 0, 1: 1}, compiler_params=_SPLIT_COPY,
    )(src_thru, land_thru, *sems, after)[1]


def _run_comm(comm, name, in_vmem=False, after=()):
    n_in, n_out, n_after = len(comm.operands), len(comm.out_shapes), len(after)

    def body(*refs):
        ins, refs = refs[:n_in], refs[n_in + n_after:]
        outs, sems = refs[:n_out], refs[n_out:]
        for phase in comm.phases:
            phase(ins, outs, sems)

    spec = _VMEM_WHOLE if in_vmem else _ANY
    return pl.pallas_call(
        body, name=name, out_shape=comm.out_shapes, in_specs=[spec] * n_in + [_ANY] * n_after,
        out_specs=[spec] * n_out, scratch_shapes=comm.sems,
    )(*comm.operands, *after)


def _carrier_call(body, comm, schedule, *, name, grid, in_specs, out_specs, out_shape, scratch_shapes, operands):
    n_in, n_out, n_scr = len(in_specs), len(out_specs), len(scratch_shapes)
    c_in, c_out = len(comm.operands), len(comm.out_shapes)

    def full_body(*refs):
        ins, refs = refs[:n_in], refs[n_in:]
        cins, refs = refs[:c_in], refs[c_in:]
        outs, refs = refs[:n_out], refs[n_out:]
        couts, refs = refs[:c_out], refs[c_out:]
        scr, csems = refs[:n_scr], refs[n_scr:]
        step = pl.program_id(0)

        def run(before):
            for (at, when_before), phase in zip(schedule, comm.phases):
                if when_before == before:
                    pl.when(step == at)(functools.partial(phase, cins, couts, csems))

        run(True)
        body(*ins, *outs, *scr)
        run(False)

    res = pl.pallas_call(
        full_body, name=name, grid=grid,
        in_specs=list(in_specs) + [_ANY] * c_in, out_specs=list(out_specs) + [_ANY] * c_out,
        out_shape=list(out_shape) + list(comm.out_shapes),
        scratch_shapes=list(scratch_shapes) + list(comm.sems),
        compiler_params=_cp(("arbitrary",)),
    )(*operands, *comm.operands)
    return res[:n_out], res[n_out:]


def _pcall(body, carried, *, name, grid, in_specs, out_specs, out_shape, scratch_shapes, operands):
    if carried is None:
        res = pl.pallas_call(body, name=name, grid=grid, in_specs=in_specs, out_specs=out_specs,
                             out_shape=out_shape, scratch_shapes=scratch_shapes,
                             compiler_params=_cp(("arbitrary",)))(*operands)
        return res, ()
    return _carrier_call(body, carried[0], carried[1], name=name, grid=grid, in_specs=in_specs,
                         out_specs=out_specs, out_shape=out_shape, scratch_shapes=scratch_shapes,
                         operands=operands)


def _all_gather(blocks, name, in_vmem, after=()):
    return _run_comm(_gather_comm(blocks), name, in_vmem, after)


N_CHIPS = 4


def _sibling_comm(contribs):
    n = len(contribs)

    def copies(ins, outs, sems):
        send_sems, recv_sems = sems
        x, y, c = _my_pos()
        return [pltpu.make_async_remote_copy(
            src_ref=ins[a].at[pl.ds(0, N_CHIPS), 1 - c], dst_ref=outs[a],
            send_sem=send_sems.at[a], recv_sem=recv_sems.at[a],
            device_id=(x, y, 1 - c), device_id_type=MESH) for a in range(n)]

    def start(ins, outs, sems):
        for cp in copies(ins, outs, sems):
            cp.start()

    def finish(ins, outs, sems):
        cps = copies(ins, outs, sems)
        for cp in cps:
            cp.wait_recv()
        for cp in cps:
            cp.wait_send()

    return _Comm(contribs, [jax.ShapeDtypeStruct((N_CHIPS,) + b.shape[2:], b.dtype) for b in contribs],
                 [pltpu.SemaphoreType.DMA((n,)), pltpu.SemaphoreType.DMA((n,))], [start, finish])


def _pair_sum(mine, theirs, name):
    _, _, rows, cols = mine.shape
    core = lax.axis_index("c").astype(jnp.int32).reshape(1)

    def body(c_ref, m_ref, t_ref, o_ref):
        o_ref[...] = (m_ref[...].astype(F32) + t_ref[...].astype(F32)).astype(BF16)

    return pl.pallas_call(
        body, name=name,
        grid_spec=pltpu.PrefetchScalarGridSpec(
            num_scalar_prefetch=1, grid=(N_CHIPS,),
            in_specs=[pl.BlockSpec((None, None, rows, cols), lambda q, c: (q, c[0], 0, 0)),
                      pl.BlockSpec((None, rows, cols), lambda q, c: (q, 0, 0))],
            out_specs=pl.BlockSpec((None, rows, cols), lambda q, c: (q, 0, 0))),
        out_shape=jax.ShapeDtypeStruct((N_CHIPS, rows, cols), BF16),
        compiler_params=_cp(("parallel",)),
    )(core, mine, theirs)


def _chip_comm(sums):
    n = len(sums)

    def parts(ins, outs, sems):
        send_sems, recv_sems, local_sems = sems
        x, y, c = _my_pos()
        q_me = 2 * x + y
        chips = [(1 - x, y), (x, 1 - y), (1 - x, 1 - y)]
        mine = [pltpu.make_async_copy(ins[a].at[q_me], outs[a].at[q_me], local_sems.at[a]) for a in range(n)]
        sends, recvs = [], []
        for j, (px, py) in enumerate(chips):
            for a in range(n):
                q = 2 * px + py
                sends.append(pltpu.make_async_remote_copy(
                    src_ref=ins[a].at[q], dst_ref=outs[a].at[q_me],
                    send_sem=send_sems.at[a, j], recv_sem=recv_sems.at[a, j],
                    device_id=(px, py, c), device_id_type=MESH))
                recvs.append(pltpu.make_async_remote_copy(
                    src_ref=ins[a].at[q], dst_ref=outs[a].at[q],
                    send_sem=send_sems.at[a, j], recv_sem=recv_sems.at[a, j],
                    device_id=(x, y, c), device_id_type=MESH))
        return mine, sends, recvs

    def start(ins, outs, sems):
        mine, sends, _ = parts(ins, outs, sems)
        for cp in mine + sends:
            cp.start()

    def finish(ins, outs, sems):
        mine, sends, recvs = parts(ins, outs, sems)
        for cp in recvs:
            cp.wait_recv()
        for cp in sends:
            cp.wait_send()
        for cp in mine:
            cp.wait()

    return _Comm(sums, [jax.ShapeDtypeStruct(b.shape, b.dtype) for b in sums],
                 [pltpu.SemaphoreType.DMA((n, 3)), pltpu.SemaphoreType.DMA((n, 3)), pltpu.SemaphoreType.DMA((n,))],
                 [start, finish])


def _mm_nt(a, bt, *, tm, tn, out_dtype, name, row_off=0, rows=None):
    rows = a.shape[0] if rows is None else rows
    n, k = bt.shape

    def body(a_ref, b_ref, o_ref):
        o_ref[...] = _dot_nt(a_ref[...], b_ref[...]).astype(out_dtype)

    return pl.pallas_call(
        body, name=name, grid=(rows // tm, n // tn),
        in_specs=[pl.BlockSpec((tm, k), lambda i, j: (i + row_off, 0)),
                  pl.BlockSpec((tn, k), lambda i, j: (j, 0))],
        out_specs=pl.BlockSpec((tm, tn), lambda i, j: (i, j)),
        out_shape=jax.ShapeDtypeStruct((rows, n), out_dtype),
        compiler_params=_cp(("parallel", "parallel")),
    )(a, bt)


def _mm_tn(a, b, *, tk, nk, tm, tn, out_dtype, name, a_off=0, b_off=0):
    m, n = a.shape[1], b.shape[1]

    def body(a_ref, b_ref, o_ref, acc):
        kk = pl.program_id(2)

        @pl.when(kk == 0)
        def _():
            acc[...] = jnp.zeros_like(acc)

        acc[...] += _dot_tn(a_ref[...], b_ref[...])

        @pl.when(kk == nk - 1)
        def _():
            o_ref[...] = acc[...].astype(out_dtype)

    return pl.pallas_call(
        body, name=name, grid=(m // tm, n // tn, nk),
        in_specs=[pl.BlockSpec((tk, tm), lambda i, j, kk: (kk + a_off, i)),
                  pl.BlockSpec((tk, tn), lambda i, j, kk: (kk + b_off, j))],
        out_specs=pl.BlockSpec((tm, tn), lambda i, j, kk: (i, j)),
        out_shape=jax.ShapeDtypeStruct((m, n), out_dtype),
        scratch_shapes=[pltpu.VMEM((tm, tn), F32)],
        compiler_params=_cp(("parallel", "parallel", "arbitrary")),
    )(a, b)


def _mm_tn_blocked(a, b, name, steps, carried=None):
    w = a.shape[1] // steps
    n = b.shape[1]

    def body(a_ref, b_ref, o_ref):
        o_ref[...] = _dot_tn(a_ref[...], b_ref[...]).astype(BF16)

    (out,), extra = _pcall(
        body, carried, name=name, grid=(steps,),
        in_specs=[pl.BlockSpec((S, w), lambda j: (0, j)), _full((S, n))],
        out_specs=[pl.BlockSpec((w, n), lambda j: (j, 0))],
        out_shape=[jax.ShapeDtypeStruct((a.shape[1], n), BF16)],
        scratch_shapes=[], operands=[a, b])
    return out, extra


def _prologue(cond, c_ctx, w_ada, b_cols, w_in_t, x, ctx, nw):
    rows_shape = jax.ShapeDtypeStruct((16, ADA_BLK), F32)
    big, g_cond, g_mod = _gather_comm_relayed([w_in_t]), _gather_comm([cond]), _gather_comm([rows_shape])

    def body(cond_ref, cctx_ref, wada_ref, b_ref, nw_ref, win_ref, x_ref, ctx_ref,
             g0_ref, cc_ref, mod_ref, gin_ref, h_ref, hl_ref, rows_ref, g1_ref, x_s, ctx_s, h_s, io_sems, *sems):
        s_big, s_cond, s_mod = sems[0:3], sems[3:6], sems[6:9]
        big.phases[0]([win_ref], [gin_ref], s_big)
        load_x = pltpu.make_async_copy(x_ref, x_s, io_sems.at[0])
        load_ctx = pltpu.make_async_copy(ctx_ref, ctx_s, io_sems.at[1])
        load_x.start()
        load_ctx.start()
        for phase in g_cond.phases:
            phase([cond_ref], [g0_ref], s_cond)
        cc_ref[...] = jnp.zeros_like(cc_ref)
        for j in range(N_DEV):
            cc_ref[j:j + 1, :] = g0_ref[j, 0:1, :]
        cc_ref[N_DEV:N_DEV + 1, :] = cctx_ref[...]
        cv = cc_ref[...]
        rows_ref[...] = _dot_f32(cv * _sigmoid(cv), wada_ref[...]) + b_ref[...]
        for phase in g_mod.phases:
            phase([rows_ref], [g1_ref], s_mod)
        x_pos, y_pos, c_pos = _my_pos()
        me = 4 * x_pos + 2 * y_pos + c_pos
        mine = jnp.concatenate([g1_ref[j, pl.ds(me, 1), :] for j in range(N_DEV)], axis=1)
        shared = jnp.concatenate([g1_ref[j, N_DEV:N_DEV + 1, :] for j in range(N_DEV)], axis=1)
        for k in range(6):
            mod_ref[k:k + 1, :] = mine[:, k * D:(k + 1) * D]
        mod_ref[6:7, :] = shared[:, 0:D]
        mod_ref[7:8, :] = shared[:, D:2 * D]
        load_ctx.wait()
        load_x.wait()
        h_s[pl.ds(0, L), :] = _rms_mod(ctx_s[...], nw_ref[...], mod_ref[6:7, :], mod_ref[7:8, :]).astype(BF16)

        def norm_tile(i, carry):
            r0 = pl.multiple_of(i * TM, TM)
            h_s[pl.ds(L + r0, TM), :] = _rms_mod(
                x_s[pl.ds(r0, TM), :], nw_ref[...], mod_ref[0:1, :], mod_ref[1:2, :]).astype(BF16)
            return carry

        lax.fori_loop(0, N_LAT_TILES, norm_tile, 0)
        stores = [pltpu.make_async_copy(h_s, h_ref, io_sems.at[2]),
                  pltpu.make_async_copy(h_s.at[pl.ds(L, S)], hl_ref, io_sems.at[3])]
        for cp in stores:
            cp.start()
        big.phases[1]([win_ref], [gin_ref], s_big)
        big.phases[2]([win_ref], [gin_ref], s_big)
        for cp in stores:
            cp.wait()

    return pl.pallas_call(
        body, name="prologue",
        in_specs=[_VMEM_WHOLE] * 5 + [_ANY] * 3, out_specs=[_VMEM_WHOLE] * 3 + [_ANY] * 3,
        out_shape=[g_cond.out_shapes[0], jax.ShapeDtypeStruct((16, D), F32), jax.ShapeDtypeStruct((8, D), F32),
                   big.out_shapes[0], jax.ShapeDtypeStruct((T, D), BF16), jax.ShapeDtypeStruct((S, D), BF16)],
        scratch_shapes=[pltpu.VMEM((16, ADA_BLK), F32), pltpu.VMEM((N_DEV, 16, ADA_BLK), F32),
                        pltpu.VMEM((S, D), F32), pltpu.VMEM((L, D), F32), pltpu.VMEM((T, D), BF16),
                        pltpu.SemaphoreType.DMA((4,))] + big.sems + g_cond.sems + g_mod.sems,
        compiler_params=_cp(),
    )(cond, c_ctx, w_ada, b_cols, nw, w_in_t, x, ctx)


def _ada_grads(cc, dm_cols, w_ada):
    def body(c_ref, dm_ref, w_ref, gw_ref, dsc_ref):
        cv = c_ref[...]
        sc = cv * _sigmoid(cv)
        dm = dm_ref[...]
        gw_ref[...] = _dot_f32(sc, dm, dot=_dot_tn)
        dsc_ref[...] = _dot_f32(dm, w_ref[...], dot=_dot_nt)

    return pl.pallas_call(
        body, name="ada_grads",
        in_specs=[_VMEM_WHOLE] * 3, out_specs=[_VMEM_WHOLE] * 2,
        out_shape=[jax.ShapeDtypeStruct((D, ADA_BLK), F32), jax.ShapeDtypeStruct((16, D), F32)],
        compiler_params=_cp(),
    )(cc, dm_cols, w_ada)


def _lat(i):
    return jnp.maximum(i - 1, 0)


def _rms_mod(xv, nw, sh, sc):
    rstd = lax.rsqrt(jnp.mean(xv * xv, axis=-1, keepdims=True) + EPS)
    return (xv * rstd * nw) * (1.0 + sc) + sh


def _rms_mod_bwd(xv, nw, sc, dh):
    rstd = lax.rsqrt(jnp.mean(xv * xv, axis=-1, keepdims=True) + EPS)
    xhat = xv * rstd
    dn = dh * (1.0 + sc)
    dxhat = dn * nw
    dx = rstd * (dxhat - xhat * jnp.mean(dxhat * xhat, axis=-1, keepdims=True))
    return (dx, jnp.sum(dh, axis=0, keepdims=True), jnp.sum(dh * (xhat * nw), axis=0, keepdims=True),
            jnp.sum(dn * xhat, axis=0, keepdims=True))


def _chunk_masks(reverse):
    row = lax.broadcasted_iota(jnp.int32, (TM, TM), 0)
    col = lax.broadcasted_iota(jnp.int32, (TM, TM), 1)
    same = (row // CHUNK) == (col // CHUNK)
    tri = same & ((col >= row) if reverse else (col <= row))
    return same, tri


def _chunk_order(i, reverse):
    if not reverse:
        return i
    return jnp.where(i < N_CTX_CHUNKS, N_CTX_CHUNKS - 1 - i, N_CHUNKS + N_CTX_CHUNKS - 1 - i)


def _decay_terms(z, lb, same01, tri01):
    f = lb + (1.0 - lb) * _sigmoid(z)
    g = jnp.log(f)
    g2 = jnp.concatenate(_split2(g), axis=1)
    b2 = _dot(tri01, g2)
    t2 = _dot(same01, g2)
    return f, 1.0 - f, b2[:, :HG_DIM] + b2[:, HG_DIM:], t2[:, :HG_DIM] + t2[:, HG_DIM:]


def _chunk_outer(a, b):
    n = TM // CHUNK
    return jnp.einsum('ncv,nck->nvk', a.reshape(n, CHUNK, HG_DIM), b.reshape(n, CHUNK, HG_DIM),
                      preferred_element_type=F32)


def _hgrn_fwd(p_a, lbl, carried=None):
    cpt = TM // CHUNK

    def body(p_ref, lbl_ref, o_ref, st_ref, qd_s, kd_s, u_s, v_s, ebt_s):
        masks = [_chunk_masks(d == 1) for d in (0, 1)]
        same01 = jnp.where(masks[0][0], 1.0, 0.0).astype(BF16)
        tri = [m[1] for m in masks]
        tri01 = [jnp.where(t, 1.0, 0.0).astype(BF16) for t in tri]
        lb = [_sigmoid(lbl_ref[d][0:1, :] - lbl_ref[d][1:2, :]) for d in (0, 1)]

        def prep(r, carry):
            r0 = pl.multiple_of(r * TM, TM)
            vb = p_ref[pl.ds(r0, TM), 2 * HG_DIM:3 * HG_DIM].astype(BF16)
            v_s[pl.ds(r0, TM), :] = vb
            for d in (0, 1):
                z = p_ref[pl.ds(r0, TM), d * HG_DIM:(d + 1) * HG_DIM]
                _, k, b, bt = _decay_terms(z, lb[d], same01, tri01[d])
                u_s[d, pl.ds(r * cpt, cpt)] = _chunk_outer(vb, (k * jnp.exp(bt - b)).astype(BF16))
                ebt_s[d, pl.ds(r0, TM), :] = jnp.exp(bt)

                @pl.when(r >= 1)
                def _():
                    rl = pl.multiple_of(r0 - L, TM)
                    qr = p_ref[pl.ds(r0, TM), 3 * HG_DIM:4 * HG_DIM]
                    q = qr * _sigmoid(qr) * HG_DIM ** -0.5
                    qd_s[d, pl.ds(rl, TM), :] = (q * jnp.exp(b)).astype(BF16)
                    kd_s[d, pl.ds(rl, TM), :] = (k * jnp.exp(-b)).astype(BF16)

            return carry

        lax.fori_loop(0, N_TILES, prep, 0)

        def scan(i, sts):
            new = []
            for d in (0, 1):
                nn = _chunk_order(i, d == 1)
                c0 = pl.multiple_of(nn * CHUNK, CHUNK)
                st_ref[d, nn] = sts[d].astype(BF16)
                new.append(sts[d] * ebt_s[d, pl.ds(c0, 1), :] + u_s[d, nn])
            return tuple(new)

        zero = jnp.zeros((HG_DIM, HG_DIM), F32)
        lax.fori_loop(0, N_CHUNKS, scan, (zero, zero))

        def outp(r, carry):
            r0 = pl.multiple_of(r * TM, TM)
            vb = v_s[pl.ds(r0 + L, TM), :]
            o = jnp.zeros((TM, HG_DIM), F32)
            for d in (0, 1):
                qd = qd_s[d, pl.ds(r0, TM), :]
                a = jnp.where(tri[d], _dot_nt(qd, kd_s[d, pl.ds(r0, TM), :]), 0.0)
                stb = st_ref[d, pl.ds(N_CTX_CHUNKS + r * cpt, cpt)]
                inter = jnp.einsum('nck,nvk->ncv', qd.reshape(cpt, CHUNK, HG_DIM), stb,
                                   preferred_element_type=F32)
                o = o + _dot(a.astype(BF16), vb) + inter.reshape(TM, HG_DIM)
            o_ref[pl.ds(r0, TM), :] = o
            return carry

        lax.fori_loop(0, N_LAT_TILES, outp, 0, unroll=2)

    return _pcall(
        body, carried, name="hgrn_fwd", grid=(HG_HEADS,),
        in_specs=[pl.BlockSpec((T, 4 * HG_DIM), lambda h: (0, h)),
                  pl.BlockSpec((2, 2, HG_DIM), lambda h: (0, 0, h))],
        out_specs=[pl.BlockSpec((S, HG_DIM), lambda h: (0, h)),
                   pl.BlockSpec((2, None, N_CHUNKS, HG_DIM, HG_DIM), lambda h: (0, h, 0, 0, 0))],
        out_shape=[jax.ShapeDtypeStruct((S, HGW), F32),
                   jax.ShapeDtypeStruct((2, HG_HEADS, N_CHUNKS, HG_DIM, HG_DIM), BF16)],
        scratch_shapes=[pltpu.VMEM((2, S, HG_DIM), BF16), pltpu.VMEM((2, S, HG_DIM), BF16),
                        pltpu.VMEM((2, N_CHUNKS, HG_DIM, HG_DIM), F32), pltpu.VMEM((T, HG_DIM), BF16),
                        pltpu.VMEM((2, T, HG_DIM), F32)],
        operands=[p_a, lbl])


def _hgrn_bwd(p_a, lbl, d_o, st, carried=None):
    cpt = TM // CHUNK

    def rows(r):
        return r * TM if isinstance(r, int) else pl.multiple_of(r * TM, TM)

    def body(p_ref, lbl_ref, do_ref, st_ref, dp_ref, dlb_ref, b_s, bt_s, dbt_s, qd_s, dst_s, w_s):
        masks = [_chunk_masks(d == 1) for d in (0, 1)]
        same01 = jnp.where(masks[0][0], 1.0, 0.0).astype(BF16)
        tri = [m[1] for m in masks]
        tri01 = [jnp.where(t, 1.0, 0.0).astype(BF16) for t in tri]
        later01 = [tri01[1], tri01[0]]
        lb = [_sigmoid(lbl_ref[d][0:1, :] - lbl_ref[d][1:2, :]) for d in (0, 1)]

        def prep_tile(r, latent):
            r0 = rows(r)
            for d in (0, 1):
                z = p_ref[pl.ds(r0, TM), d * HG_DIM:(d + 1) * HG_DIM]
                _, _, b, bt = _decay_terms(z, lb[d], same01, tri01[d])
                b_s[d, pl.ds(r0, TM), :] = b
                bt_s[d, pl.ds(r0, TM), :] = bt
                if latent:
                    rl = pl.multiple_of(r0 - L, TM)
                    qr = p_ref[pl.ds(r0, TM), 3 * HG_DIM:4 * HG_DIM]
                    qd = (qr * _sigmoid(qr) * HG_DIM ** -0.5 * jnp.exp(b)).astype(BF16)
                    qd_s[d, pl.ds(rl, TM), :] = qd
                    w_s[d, pl.ds(r * cpt, cpt)] = _chunk_outer(
                        do_ref[pl.ds(rl, TM), :].astype(BF16), qd).astype(BF16)

        prep_tile(0, False)
        w_s[:, pl.ds(0, N_CTX_CHUNKS)] = jnp.zeros((2, N_CTX_CHUNKS, HG_DIM, HG_DIM), BF16)

        def prep(r, carry):
            prep_tile(r, True)
            return carry

        lax.fori_loop(1, N_TILES, prep, 0, unroll=2)

        def rscan(j, dsts):
            i = N_CHUNKS - 1 - j
            new = []
            for d in (0, 1):
                nn = _chunk_order(i, d == 1)
                c0 = pl.multiple_of(nn * CHUNK, CHUNK)
                dst_s[d, nn] = dsts[d].astype(BF16)
                after = st_ref[d, _chunk_order(jnp.minimum(i + 1, N_CHUNKS - 1), d == 1)].astype(F32)
                dbt_s[d, pl.ds(c0, CHUNK), :] = jnp.broadcast_to(
                    jnp.sum(after * dsts[d], axis=0, keepdims=True), (CHUNK, HG_DIM))
                new.append(dsts[d] * jnp.exp(bt_s[d, pl.ds(c0, 1), :]) + w_s[d, nn].astype(F32))
            return tuple(new)

        zero = jnp.zeros((HG_DIM, HG_DIM), F32)
        lax.fori_loop(0, N_CHUNKS, rscan, (zero, zero))

        def grad_tile(r, latent):
            r0 = rows(r)
            vb = p_ref[pl.ds(r0, TM), 2 * HG_DIM:3 * HG_DIM].astype(BF16)
            dv = jnp.zeros((TM, HG_DIM), F32)
            dq = jnp.zeros((TM, HG_DIM), F32)
            dlbs = []
            if latent:
                rl = pl.multiple_of(r0 - L, TM)
                qr = p_ref[pl.ds(r0, TM), 3 * HG_DIM:4 * HG_DIM]
                sq = _sigmoid(qr)
                do = do_ref[pl.ds(rl, TM), :].astype(BF16)
                da_full = _dot_nt(do, vb)
            for d in (0, 1):
                z = p_ref[pl.ds(r0, TM), d * HG_DIM:(d + 1) * HG_DIM]
                sz = _sigmoid(z)
                f = lb[d] + (1.0 - lb[d]) * sz
                k = 1.0 - f
                b = b_s[d, pl.ds(r0, TM), :]
                e2 = jnp.exp(bt_s[d, pl.ds(r0, TM), :] - b)
                dstb = dst_s[d, pl.ds(r * cpt, cpt)]
                kd2 = k * e2
                dkd2 = jnp.einsum('ncv,nvk->nck', vb.reshape(cpt, CHUNK, HG_DIM), dstb,
                                  preferred_element_type=F32).reshape(TM, HG_DIM)
                dv = dv + jnp.einsum('nck,nvk->ncv', kd2.astype(BF16).reshape(cpt, CHUNK, HG_DIM), dstb,
                                     preferred_element_type=F32).reshape(TM, HG_DIM)
                dk = dkd2 * e2
                db = -(kd2 * dkd2)
                if latent:
                    eb = jnp.exp(b)
                    enb = jnp.exp(-b)
                    qdf = qr * sq * HG_DIM ** -0.5 * eb
                    kdf = k * enb
                    qd = qd_s[d, pl.ds(rl, TM), :]
                    kd = kdf.astype(BF16)
                    a = jnp.where(tri[d], _dot_nt(qd, kd), 0.0).astype(BF16)
                    da = jnp.where(tri[d], da_full, 0.0).astype(BF16)
                    stb = st_ref[d, pl.ds(r * cpt, cpt)]
                    dqd = _dot(da, kd) + jnp.einsum(
                        'ncv,nvk->nck', do.reshape(cpt, CHUNK, HG_DIM), stb,
                        preferred_element_type=F32).reshape(TM, HG_DIM)
                    dkd = _dot_tn(da, qd)
                    dv = dv + _dot_tn(a, do)
                    dk = dk + dkd * enb
                    db = db + qdf * dqd - kdf * dkd
                    dq = dq + dqd * eb
                dg = _dot_lhs01(later01[d], db) + dbt_s[d, pl.ds(r0, TM), :]
                df = dg / f - dk
                dp_ref[pl.ds(r0, TM), d * HG_DIM:(d + 1) * HG_DIM] = (
                    df * (1.0 - lb[d]) * sz * (1.0 - sz)).astype(BF16)
                dlbs.append(jnp.sum(df * (1.0 - sz), axis=0, keepdims=True))
            dp_ref[pl.ds(r0, TM), 2 * HG_DIM:3 * HG_DIM] = dv.astype(BF16)
            if latent:
                dq = dq * (HG_DIM ** -0.5) * (sq * (1.0 + qr * (1.0 - sq)))
            dp_ref[pl.ds(r0, TM), 3 * HG_DIM:4 * HG_DIM] = dq.astype(BF16)
            return dlbs

        dlb_ctx = grad_tile(0, False)

        def grads(r, acc):
            t = grad_tile(r, True)
            return (acc[0] + t[0], acc[1] + t[1])

        dlb = lax.fori_loop(1, N_TILES, grads, (dlb_ctx[0], dlb_ctx[1]))
        dlb_ref[0:1, :] = dlb[0]
        dlb_ref[1:2, :] = dlb[1]

    return _pcall(
        body, carried, name="hgrn_bwd", grid=(HG_HEADS,),
        in_specs=[pl.BlockSpec((T, 4 * HG_DIM), lambda h: (0, h)),
                  pl.BlockSpec((2, 2, HG_DIM), lambda h: (0, 0, h)),
                  pl.BlockSpec((S, HG_DIM), lambda h: (0, h)),
                  pl.BlockSpec((2, None, N_CHUNKS, HG_DIM, HG_DIM), lambda h: (0, h, 0, 0, 0))],
        out_specs=[pl.BlockSpec((T, 4 * HG_DIM), lambda h: (0, h)),
                   pl.BlockSpec((2, HG_DIM), lambda h: (0, h))],
        out_shape=[jax.ShapeDtypeStruct((T, WA), BF16), jax.ShapeDtypeStruct((2, HGW), F32)],
        scratch_shapes=[pltpu.VMEM((2, T, HG_DIM), F32), pltpu.VMEM((2, T, HG_DIM), F32),
                        pltpu.VMEM((2, T, HG_DIM), F32), pltpu.VMEM((2, S, HG_DIM), BF16),
                        pltpu.VMEM((2, N_CHUNKS, HG_DIM, HG_DIM), BF16),
                        pltpu.VMEM((2, N_CHUNKS, HG_DIM, HG_DIM), BF16)],
        operands=[p_a, lbl, d_o, st])


def _rope_tables():
    t = np.arange(S)
    inv = ROPE_THETA ** (-np.arange(0, 32, 2, dtype=np.float64) / 32)
    lane = np.arange(64)
    pos = np.where(lane[None, :] < 32, (t // GRID_W)[:, None], (t % GRID_W)[:, None]).astype(np.float64)
    ang = pos * inv[(lane % 32) % 16][None, :]
    sign = np.where((lane % 32) < 16, -1.0, 1.0)[None, :]
    cos = np.tile(np.cos(ang), (1, 2)).astype(np.float32)
    sin = np.tile(np.sin(ang) * sign, (1, 2)).astype(np.float32)
    return jnp.asarray(cos), jnp.asarray(sin)


def _rope_partner(v):
    lane = lax.broadcasted_iota(jnp.int32, (1, 128), 1)
    first = (lane % 32) < 16
    slabs = []
    for j in range(v.shape[1] // 128):
        s = v[:, 128 * j:128 * (j + 1)]
        slabs.append(jnp.where(first, pltpu.roll(s, 112, 1), pltpu.roll(s, 16, 1)))
    return slabs[0] if len(slabs) == 1 else jnp.concatenate(slabs, axis=1)


def _group_ones(width, group):
    r = lax.broadcasted_iota(jnp.int32, (width, width), 0)
    c = lax.broadcasted_iota(jnp.int32, (width, width), 1)
    return jnp.where((r // group) == (c // group), 1.0, 0.0).astype(BF16)


def _group_mean(v, ones01, group):
    hi = v.astype(BF16)
    lo = (v - hi.astype(F32)).astype(BF16)
    return (_dot(hi, ones01) + _dot(lo, ones01)) * (1.0 / group)


def _rep_matrix():
    r = lax.broadcasted_iota(jnp.int32, (KVW, ATW), 0)
    c = lax.broadcasted_iota(jnp.int32, (KVW, ATW), 1)
    return jnp.where(r == HEAD_DIM * (c // 256) + c % HEAD_DIM, 1.0, 0.0).astype(BF16)


def _tile_lanes(v, reps):
    return jnp.concatenate([v] * reps, axis=1)


def _prep_fwd(p_b, o, cos, sin, hnw, qnw, knw):
    def body(p_ref, o_ref, cos_ref, sin_ref, hnw_ref, qnw_ref, knw_ref, y_ref, q_ref, k_ref, v_ref):
        i = pl.program_id(0)
        rep = _rep_matrix()
        ones_k = _group_ones(KVW, HEAD_DIM)
        kr = p_ref[:, 1024:1152]
        krstd = lax.rsqrt(_group_mean(kr * kr, ones_k, HEAD_DIM) + EPS)
        kn = kr * krstd * knw_ref[...]
        v_ref[...] = _dot(p_ref[:, 1152:1280].astype(BF16), rep).astype(BF16)

        @pl.when(i == 0)
        def _():
            k_ref[...] = _dot(kn.astype(BF16), rep).astype(BF16)

        @pl.when(i > 0)
        def _():
            cs, sn = cos_ref[...], sin_ref[...]
            kro = kn * cs + _rope_partner(kn) * sn
            k_ref[...] = _dot(kro.astype(BF16), rep).astype(BF16)
            qr = p_ref[:, 512:1024]
            qrstd = lax.rsqrt(_group_mean(qr * qr, _group_ones(ATW, HEAD_DIM), HEAD_DIM) + EPS)
            qn = qr * qrstd * qnw_ref[...]
            qro = qn * _tile_lanes(cs, 4) + _rope_partner(qn) * _tile_lanes(sn, 4)
            q_ref[...] = (qro * HEAD_DIM ** -0.5).astype(BF16)
            ys = []
            for h in range(HG_HEADS):
                oh = o_ref[:, HG_DIM * h:HG_DIM * (h + 1)]
                gh = p_ref[:, HG_DIM * h:HG_DIM * (h + 1)]
                rstd = lax.rsqrt(jnp.mean(oh * oh, axis=-1, keepdims=True) + EPS)
                ys.append(oh * rstd * hnw_ref[...] * (gh * _sigmoid(gh)))
            y_ref[...] = jnp.concatenate(ys, axis=1).astype(BF16)

    return pl.pallas_call(
        body, name="prep_fwd", grid=(N_TILES,),
        in_specs=[pl.BlockSpec((TM, WB), lambda i: (i, 0)),
                  pl.BlockSpec((TM, HGW), lambda i: (_lat(i), 0)),
                  pl.BlockSpec((TM, 128), lambda i: (_lat(i), 0)),
                  pl.BlockSpec((TM, 128), lambda i: (_lat(i), 0)),
                  _full((1, HG_DIM)), _full((1, ATW)), _full((1, KVW))],
        out_specs=[pl.BlockSpec((TM, HGW), lambda i: (_lat(i), 0)),
                   pl.BlockSpec((TM, ATW), lambda i: (_lat(i), 0)),
                   pl.BlockSpec((TM, ATW), lambda i: (i, 0)),
                   pl.BlockSpec((TM, ATW), lambda i: (i, 0))],
        out_shape=[jax.ShapeDtypeStruct((S, HGW), BF16), jax.ShapeDtypeStruct((S, ATW), BF16),
                   jax.ShapeDtypeStruct((T, ATW), BF16), jax.ShapeDtypeStruct((T, ATW), BF16)],
        compiler_params=_cp(("arbitrary",)),
    )(p_b, o, cos, sin, hnw, qnw, knw)


def _prep_bwd(p_b, o, cos, sin, hnw, qnw, knw, dy_hg, dq, dk_rep, dv_rep, carried=None):
    def body(p_ref, o_ref, cos_ref, sin_ref, hnw_ref, qnw_ref, knw_ref, dy_ref, dq_ref, dk_ref, dv_ref,
             dp_ref, do_ref, acc_ref):
        i = pl.program_id(0)

        @pl.when(i == 0)
        def _():
            acc_ref[...] = jnp.zeros_like(acc_ref)

        rep = _rep_matrix()
        ones_k = _group_ones(KVW, HEAD_DIM)

        def fold(v):
            hi = v.astype(BF16)
            lo = (v - hi.astype(F32)).astype(BF16)
            return _dot_nt(hi, rep) + _dot_nt(lo, rep)

        kr = p_ref[:, 1024:1152]
        krstd = lax.rsqrt(_group_mean(kr * kr, ones_k, HEAD_DIM) + EPS)
        khat = kr * krstd
        kw = knw_ref[...]
        dkro = fold(dk_ref[...])
        dv = fold(dv_ref[...])

        def k_back(dkn):
            dkhat = dkn * kw
            dkr = krstd * (dkhat - khat * _group_mean(dkhat * khat, ones_k, HEAD_DIM))
            acc_ref[2:3, 0:KVW] += jnp.sum(dkn * khat, axis=0, keepdims=True)
            dp_ref[:, 1024:1152] = dkr.astype(BF16)
            dp_ref[:, 1152:1280] = dv.astype(BF16)

        @pl.when(i == 0)
        def _():
            k_back(dkro)
            dp_ref[:, 0:1024] = jnp.zeros((TM, 1024), BF16)

        @pl.when(i > 0)
        def _():
            cs, sn = cos_ref[...], sin_ref[...]
            k_back(dkro * cs + _rope_partner(dkro * sn))
            ones_q = _group_ones(ATW, HEAD_DIM)
            qr = p_ref[:, 512:1024]
            qrstd = lax.rsqrt(_group_mean(qr * qr, ones_q, HEAD_DIM) + EPS)
            qhat = qr * qrstd
            dqro = dq_ref[...] * HEAD_DIM ** -0.5
            dqn = dqro * _tile_lanes(cs, 4) + _rope_partner(dqro * _tile_lanes(sn, 4))
            dqhat = dqn * qnw_ref[...]
            dqr = qrstd * (dqhat - qhat * _group_mean(dqhat * qhat, ones_q, HEAD_DIM))
            acc_ref[1:2, :] += jnp.sum(dqn * qhat, axis=0, keepdims=True)
            dp_ref[:, 512:1024] = dqr.astype(BF16)
            dws = jnp.zeros((1, HG_DIM), F32)
            for h in range(HG_HEADS):
                sl = slice(HG_DIM * h, HG_DIM * (h + 1))
                oh, gh, dy = o_ref[:, sl], p_ref[:, sl], dy_ref[:, sl]
                rstd = lax.rsqrt(jnp.mean(oh * oh, axis=-1, keepdims=True) + EPS)
                ohat = oh * rstd
                sg = _sigmoid(gh)
                dp_ref[:, sl] = (dy * (ohat * hnw_ref[...]) * (sg * (1.0 + gh * (1.0 - sg)))).astype(BF16)
                dn = dy * (gh * sg)
                dws = dws + jnp.sum(dn * ohat, axis=0, keepdims=True)
                dohat = dn * hnw_ref[...]
                do_ref[:, sl] = rstd * (dohat - ohat * jnp.mean(dohat * ohat, axis=-1, keepdims=True))
            acc_ref[0:1, 0:HG_DIM] += dws

    return _pcall(
        body, carried, name="prep_bwd", grid=(N_TILES,),
        in_specs=[pl.BlockSpec((TM, WB), lambda i: (i, 0)),
                  pl.BlockSpec((TM, HGW), lambda i: (_lat(i), 0)),
                  pl.BlockSpec((TM, 128), lambda i: (_lat(i), 0)),
                  pl.BlockSpec((TM, 128), lambda i: (_lat(i), 0)),
                  _full((1, HG_DIM)), _full((1, ATW)), _full((1, KVW)),
                  pl.BlockSpec((TM, HGW), lambda i: (_lat(i), 0)),
                  pl.BlockSpec((TM, ATW), lambda i: (_lat(i), 0)),
                  pl.BlockSpec((TM, ATW), lambda i: (i, 0)),
                  pl.BlockSpec((TM, ATW), lambda i: (i, 0))],
        out_specs=[pl.BlockSpec((TM, WB), lambda i: (i, 0)),
                   pl.BlockSpec((TM, HGW), lambda i: (_lat(i), 0)),
                   _full((8, ATW))],
        out_shape=[jax.ShapeDtypeStruct((T, WB), BF16), jax.ShapeDtypeStruct((S, HGW), F32),
                   jax.ShapeDtypeStruct((8, ATW), F32)],
        scratch_shapes=[], operands=[p_b, o, cos, sin, hnw, qnw, knw, dy_hg, dq, dk_rep, dv_rep])


NEG = -1e30
_CTX_BLOCKS = L // BLOCK


def _attn_window_specs():
    prev = pl.BlockSpec((BLOCK, ATW), lambda i: (jnp.maximum(i - 1, 0) + _CTX_BLOCKS, 0))
    own = pl.BlockSpec((BLOCK, ATW), lambda i: (i + _CTX_BLOCKS, 0))
    nxt = pl.BlockSpec((BLOCK, ATW), lambda i: (jnp.minimum(i + 1, N_BLOCKS - 1) + _CTX_BLOCKS, 0))
    return [prev, own, nxt, _full((L, ATW))]


def _attn_valid(i, heads, context):
    n_keys = 3 * BLOCK + (L if context else 0)
    qi = lax.broadcasted_iota(jnp.int32, (heads * BLOCK, n_keys), 0) % BLOCK
    kj = lax.broadcasted_iota(jnp.int32, (heads * BLOCK, n_keys), 1)
    window = ((jnp.abs(kj - BLOCK - qi) <= BLOCK) & ((kj >= BLOCK) | (i > 0))
              & ((kj < 2 * BLOCK) | (i < N_BLOCKS - 1)))
    return window | (kj >= 3 * BLOCK)


def _stack_heads(qg):
    lane = lax.broadcasted_iota(jnp.int32, (1, 256), 1) // HEAD_DIM
    return jnp.concatenate([jnp.where(lane == g, qg, jnp.zeros_like(qg)) for g in range(4)], axis=0)


def _unstack_heads(v4):
    lane = lax.broadcasted_iota(jnp.int32, (1, 256), 1) // HEAD_DIM
    out = jnp.where(lane == 0, v4[0:BLOCK], 0.0)
    for g in range(1, 4):
        out = out + jnp.where(lane == g, v4[g * BLOCK:(g + 1) * BLOCK], 0.0)
    return out


def _sink_rows(sink_ref, hk):
    return jnp.concatenate(
        [jnp.broadcast_to(sink_ref[0:1, 4 * hk + g:4 * hk + g + 1], (BLOCK, 1)) for g in range(4)], axis=0)


def _attn_fwd(q, k_rep, v_rep, sinks, carried=None):
    def body(q_ref, kp, ko, kn, kc, vp, vo, vn, vc, sink_ref, y_ref, lse_ref):
        i = pl.program_id(0)
        valid = _attn_valid(i, 1, True)
        lane8 = lax.broadcasted_iota(jnp.int32, (1, ATT_HEADS), 1)
        head_of_lane = lax.broadcasted_iota(jnp.int32, (1, 256), 1) // HEAD_DIM
        lse_out = jnp.zeros((BLOCK, ATT_HEADS), F32)
        for hk in range(KV_HEADS):
            sl = slice(256 * hk, 256 * (hk + 1))
            qg = q_ref[:, sl]
            keys = jnp.concatenate([kp[:, sl], ko[:, sl], kn[:, sl], kc[:, sl]], axis=0)
            vals = jnp.concatenate([vp[:, sl], vo[:, sl], vn[:, sl], vc[:, sl]], axis=0)
            yg = jnp.zeros((BLOCK, 256), F32)
            for g in range(4):
                q1 = jnp.where(head_of_lane == g, qg, jnp.zeros_like(qg))
                s = jnp.where(valid, _dot_nt(q1, keys), NEG)
                sink = sink_ref[0:1, 4 * hk + g:4 * hk + g + 1]
                m = jnp.maximum(jnp.max(s, axis=1, keepdims=True), sink)
                p = jnp.exp(s - m)
                den = jnp.sum(p, axis=1, keepdims=True) + jnp.exp(sink - m)
                o1 = _dot(p.astype(BF16), vals) * (1.0 / den)
                yg = yg + jnp.where(head_of_lane == g, o1, 0.0)
                lse_out = lse_out + jnp.where(lane8 == 4 * hk + g, m + jnp.log(den), 0.0)
            y_ref[:, sl] = yg.astype(BF16)
        lse_ref[...] = lse_out

    return _pcall(
        body, carried, name="attn_fwd", grid=(N_BLOCKS,),
        in_specs=[pl.BlockSpec((BLOCK, ATW), lambda i: (i, 0))] + _attn_window_specs()
        + _attn_window_specs() + [_full((1, ATT_HEADS))],
        out_specs=[pl.BlockSpec((BLOCK, ATW), lambda i: (i, 0)),
                   pl.BlockSpec((BLOCK, ATT_HEADS), lambda i: (i, 0))],
        out_shape=[jax.ShapeDtypeStruct((S, ATW), BF16), jax.ShapeDtypeStruct((S, ATT_HEADS), F32)],
        scratch_shapes=[],
        operands=[q, k_rep, k_rep, k_rep, k_rep, v_rep, v_rep, v_rep, v_rep, sinks])


def _attn_bwd(q, k_rep, v_rep, sinks, y_at, lse, dy, carried=None):
    def body(q_ref, kp, ko, kn, kc, vp, vo, vn, vc, sink_ref, y_ref, lse_ref, dy_ref,
             dq_ref, dk_ref, dv_ref, dsink_ref, dk_acc, dv_acc):
        i = pl.program_id(0)

        @pl.when(i == 0)
        def _():
            dk_acc[...] = jnp.zeros_like(dk_acc)
            dv_acc[...] = jnp.zeros_like(dv_acc)
            dk_ref[pl.ds(0, L), :] = jnp.zeros((L, ATW), F32)
            dv_ref[pl.ds(0, L), :] = jnp.zeros((L, ATW), F32)
            dsink_ref[...] = jnp.zeros_like(dsink_ref)

        valid = _attn_valid(i, 4, False)
        lane8 = lax.broadcasted_iota(jnp.int32, (1, ATT_HEADS), 1)
        w0 = pl.multiple_of(i * BLOCK, BLOCK)
        dsink = jnp.zeros((1, ATT_HEADS), F32)
        for hk in range(KV_HEADS):
            sl = slice(256 * hk, 256 * (hk + 1))
            q4 = _stack_heads(q_ref[:, sl])
            do4f = _stack_heads(dy_ref[:, sl])
            o4 = _stack_heads(y_ref[:, sl]).astype(F32)
            do4 = do4f.astype(BF16)
            kl = jnp.concatenate([kp[:, sl], ko[:, sl], kn[:, sl]], axis=0)
            vl = jnp.concatenate([vp[:, sl], vo[:, sl], vn[:, sl]], axis=0)
            lse4 = jnp.concatenate(
                [jnp.sum(jnp.where(lane8 == 4 * hk + g, lse_ref[...], 0.0), axis=1, keepdims=True)
                 for g in range(4)], axis=0)
            p_loc = jnp.where(valid, jnp.exp(_dot_nt(q4, kl) - lse4), 0.0)
            p_ctx = jnp.exp(_dot_nt(q4, kc[:, sl]) - lse4)
            delta = jnp.sum(do4f * o4, axis=1, keepdims=True)
            ds_loc = (p_loc * (_dot_nt(do4, vl) - delta)).astype(BF16)
            ds_ctx = (p_ctx * (_dot_nt(do4, vc[:, sl]) - delta)).astype(BF16)
            dq_ref[:, sl] = _unstack_heads(_dot(ds_loc, kl) + _dot(ds_ctx, kc[:, sl]))
            dk_acc[pl.ds(w0, 3 * BLOCK), sl] += _dot_tn(ds_loc, q4)
            dv_acc[pl.ds(w0, 3 * BLOCK), sl] += _dot_tn(p_loc.astype(BF16), do4)
            dk_ref[pl.ds(0, L), sl] += _dot_tn(ds_ctx, q4)
            dv_ref[pl.ds(0, L), sl] += _dot_tn(p_ctx.astype(BF16), do4)
            p_sink = jnp.exp(_sink_rows(sink_ref, hk) - lse4)
            for g in range(4):
                rows = slice(g * BLOCK, (g + 1) * BLOCK)
                dsink = dsink + jnp.where(lane8 == 4 * hk + g,
                                          -jnp.sum(p_sink[rows] * delta[rows], axis=0, keepdims=True), 0.0)
        dsink_ref[...] += dsink

        @pl.when(i == N_BLOCKS - 1)
        def _():
            dk_ref[pl.ds(L, S), :] = dk_acc[pl.ds(BLOCK, S), :]
            dv_ref[pl.ds(L, S), :] = dv_acc[pl.ds(BLOCK, S), :]

    row_q = pl.BlockSpec((BLOCK, ATW), lambda i: (i, 0))
    return _pcall(
        body, carried, name="attn_bwd", grid=(N_BLOCKS,),
        in_specs=[row_q] + _attn_window_specs() + _attn_window_specs()
        + [_full((1, ATT_HEADS)), row_q, pl.BlockSpec((BLOCK, ATT_HEADS), lambda i: (i, 0)), row_q],
        out_specs=[row_q, _full((T, ATW)), _full((T, ATW)), _full((1, ATT_HEADS))],
        out_shape=[jax.ShapeDtypeStruct((S, ATW), F32), jax.ShapeDtypeStruct((T, ATW), F32),
                   jax.ShapeDtypeStruct((T, ATW), F32), jax.ShapeDtypeStruct((1, ATT_HEADS), F32)],
        scratch_shapes=[pltpu.VMEM((S + 2 * BLOCK, ATW), F32), pltpu.VMEM((S + 2 * BLOCK, ATW), F32)],
        operands=[q, k_rep, k_rep, k_rep, k_rep, v_rep, v_rep, v_rep, v_rep, sinks, y_at, lse, dy])


def _merge_fwd(y_hg, y_at, p_c, x, w_bh, w_ba, w_out, g1, nfw, sh2, sc2, carried=None):
    def body(yh_ref, ya_ref, g_ref, x_ref, wbh_ref, wba_ref, wo_ref, g1_ref, nfw_ref, sh_ref, sc_ref,
             mx_ref, r_ref, x1_ref, h2_ref):
        a = _dot_nt(yh_ref[...], wbh_ref[...])
        b = _dot_nt(ya_ref[...], wba_ref[...])
        mixed = (_sigmoid(g_ref[:, :D]) * a + _sigmoid(g_ref[:, D:]) * b).astype(BF16)
        r = _dot(mixed, wo_ref[...])
        x1 = x_ref[...] + g1_ref[...] * r
        mx_ref[...] = mixed
        r_ref[...] = r
        x1_ref[...] = x1
        h2_ref[...] = _rms_mod(x1, nfw_ref[...], sh_ref[...], sc_ref[...]).astype(BF16)

    row = lambda w: pl.BlockSpec((TM, w), lambda i: (i, 0))
    vec = _full((1, D))
    return _pcall(
        body, carried, name="merge_fwd", grid=(N_LAT_TILES,),
        in_specs=[row(HGW), row(ATW), row(WC), row(D), _VMEM_WHOLE, _VMEM_WHOLE, _VMEM_WHOLE,
                  vec, vec, vec, vec],
        out_specs=[row(D)] * 4,
        out_shape=[jax.ShapeDtypeStruct((S, D), dt) for dt in (BF16, F32, F32, BF16)],
        scratch_shapes=[], operands=[y_hg, y_at, p_c, x, w_bh, w_ba, w_out, g1, nfw, sh2, sc2])


def _merge_bwd(dx1, r, y_hg, y_at, p_c, w_bh, w_ba, w_out, g1, carried=None):
    def body(dx_ref, r_ref, yh_ref, ya_ref, g_ref, wbh_ref, wba_ref, wo_ref, g1_ref,
             dr_ref, da_ref, db_ref, dg_ref, dyh_ref, dya_ref, acc_ref):
        @pl.when(pl.program_id(0) == 0)
        def _():
            acc_ref[...] = jnp.zeros_like(acc_ref)

        dx1v = dx_ref[...]
        acc_ref[0:1, :] += jnp.sum(dx1v * r_ref[...], axis=0, keepdims=True)
        dr = (g1_ref[...] * dx1v).astype(BF16)
        dr_ref[...] = dr
        dmix = _dot_nt(dr, wo_ref[...])
        sh, sa = _sigmoid(g_ref[:, :D]), _sigmoid(g_ref[:, D:])
        da = (dmix * sh).astype(BF16)
        db = (dmix * sa).astype(BF16)
        da_ref[...] = da
        db_ref[...] = db
        dg_ref[:, :D] = (dmix * _dot_nt(yh_ref[...], wbh_ref[...]) * sh * (1.0 - sh)).astype(BF16)
        dg_ref[:, D:] = (dmix * _dot_nt(ya_ref[...], wba_ref[...]) * sa * (1.0 - sa)).astype(BF16)
        dyh_ref[...] = _dot(da, wbh_ref[...])
        dya_ref[...] = _dot(db, wba_ref[...])

    row = lambda w: pl.BlockSpec((TM, w), lambda i: (i, 0))
    return _pcall(
        body, carried, name="merge_bwd", grid=(N_LAT_TILES,),
        in_specs=[row(D), row(D), row(HGW), row(ATW), row(WC), _VMEM_WHOLE, _VMEM_WHOLE, _VMEM_WHOLE,
                  _full((1, D))],
        out_specs=[row(D), row(D), row(D), row(WC), row(HGW), row(ATW), _full((8, D))],
        out_shape=[jax.ShapeDtypeStruct((S, D), BF16), jax.ShapeDtypeStruct((S, D), BF16),
                   jax.ShapeDtypeStruct((S, D), BF16), jax.ShapeDtypeStruct((S, WC), BF16),
                   jax.ShapeDtypeStruct((S, HGW), F32), jax.ShapeDtypeStruct((S, ATW), F32),
                   jax.ShapeDtypeStruct((8, D), F32)],
        scratch_shapes=[], operands=[dx1, r, y_hg, y_at, p_c, w_bh, w_ba, w_out, g1])


def _ffn_fused(x1, h2, tgt, w_gate, w_up, w_down, g2, nfw, sc2):
    def body(x1_ref, h2_ref, t_ref, wg_ref, wu_ref, wd_ref, g2_ref, nfw_ref, sc_ref,
             act_ref, dgt_ref, dup_ref, df_ref, dx_ref, acc_ref, gs, us):
        @pl.when(pl.program_id(0) == 0)
        def _():
            acc_ref[...] = jnp.zeros_like(acc_ref)

        h2 = h2_ref[...]
        whole = lambda w_ref: w_ref[...].reshape(D_FF, D)
        tile = lambda j: slice(j * FF_TILE, (j + 1) * FF_TILE)
        for j in range(N_FF_TILES):
            g = _dot_nt(h2, wg_ref[j])
            u = _dot_nt(h2, wu_ref[j])
            gs[j] = g
            us[j] = u
            act_ref[:, tile(j)] = (g * _sigmoid(g) * u).astype(BF16)
        f = _dot(act_ref[...], whole(wd_ref))
        x1v = x1_ref[...]
        g2 = g2_ref[...]
        diff = x1v + g2 * f - t_ref[...]
        dy = diff * (1.0 / D)
        df = (g2 * dy).astype(BF16)
        df_ref[...] = df
        dact_all = _dot_nt(df, whole(wd_ref))
        for j in range(N_FF_TILES):
            g, u = gs[j], us[j]
            sg = _sigmoid(g)
            dact = dact_all[:, tile(j)]
            dgt_ref[:, tile(j)] = (dact * u * (sg * (1.0 + g * (1.0 - sg)))).astype(BF16)
            dup_ref[:, tile(j)] = (dact * (g * sg)).astype(BF16)
        dh2 = _dot(dgt_ref[...], whole(wg_ref)) + _dot(dup_ref[...], whole(wu_ref))
        dx, dsh, dsc, dnw = _rms_mod_bwd(x1v, nfw_ref[...], sc_ref[...], dh2)
        dx_ref[...] = dy + dx
        acc_ref[0:1, :] += dsh
        acc_ref[1:2, :] += dsc
        acc_ref[2:3, :] += dnw
        acc_ref[3:4, :] += jnp.sum(dy * f, axis=0, keepdims=True)
        acc_ref[4:5, :] += 0.5 * jnp.sum(jnp.sum(diff * diff, axis=1, keepdims=True), axis=0,
                                         keepdims=True) * (1.0 / D)

    row = lambda dt_w: pl.BlockSpec((TM, dt_w), lambda i: (i, 0))
    blk = row(D_FF)
    vec = _full((1, D))
    return pl.pallas_call(
        body, name="ffn_fused", grid=(N_LAT_TILES,),
        in_specs=[row(D), row(D), row(D), _VMEM_WHOLE, _VMEM_WHOLE, _VMEM_WHOLE, vec, vec, vec],
        out_specs=[blk, blk, blk, row(D), row(D), _full((8, D))],
        out_shape=[jax.ShapeDtypeStruct((S, D_FF), BF16)] * 3
        + [jax.ShapeDtypeStruct((S, D), BF16), jax.ShapeDtypeStruct((S, D), F32),
           jax.ShapeDtypeStruct((8, D), F32)],
        scratch_shapes=[pltpu.VMEM((N_FF_TILES, TM, FF_TILE), F32), pltpu.VMEM((N_FF_TILES, TM, FF_TILE), F32)],
        compiler_params=_cp(("arbitrary",)),
    )(x1, h2, tgt, w_gate, w_up, w_down, g2, nfw, sc2)


def _proj_bc(h_all, w_b, w_c, carried=None):
    def body(h_ref, wb_ref, wc_ref, pb_ref, pc_ref):
        h = h_ref[...]
        pb_ref[...] = _dot_nt(h, wb_ref[...])

        @pl.when(pl.program_id(0) > 0)
        def _():
            pc_ref[...] = _dot_nt(h, wc_ref[...])

    return _pcall(
        body, carried, name="proj_bc", grid=(N_TILES,),
        in_specs=[pl.BlockSpec((TM, D), lambda i: (i, 0)), _VMEM_WHOLE, _VMEM_WHOLE],
        out_specs=[pl.BlockSpec((TM, WB), lambda i: (i, 0)), pl.BlockSpec((TM, WC), lambda i: (_lat(i), 0))],
        out_shape=[jax.ShapeDtypeStruct((T, WB), F32), jax.ShapeDtypeStruct((S, WC), F32)],
        scratch_shapes=[], operands=[h_all, w_b, w_c])


def _input_bwd(dp_a, dp_b, dp_c, w_a, w_b, w_c, ctx, x, dx1, nw, sh, sc, carried=None):
    def body(da_ref, db_ref, dc_ref, wa_ref, wb_ref, wc_ref, ctx_ref, x_ref, dx1_ref, nw_ref, sh_ref,
             sc_ref, gx_ref, acc_ref):
        i = pl.program_id(0)

        @pl.when(i == 0)
        def _():
            acc_ref[...] = jnp.zeros_like(acc_ref)

        dh = _dot(da_ref[...], wa_ref[...]) + _dot(db_ref[...], wb_ref[...])

        @pl.when(i == 0)
        def _():
            _, dsh, dsc, dnw = _rms_mod_bwd(ctx_ref[...], nw_ref[...], sc_ref[0:1, :], dh)
            acc_ref[3:4, :] += dsh
            acc_ref[4:5, :] += dsc
            acc_ref[2:3, :] += dnw

        @pl.when(i > 0)
        def _():
            dhl = dh + _dot(dc_ref[...], wc_ref[...])
            dx, dsh, dsc, dnw = _rms_mod_bwd(x_ref[...], nw_ref[...], sc_ref[1:2, :], dhl)
            gx_ref[...] = dx1_ref[...] + dx
            acc_ref[0:1, :] += dsh
            acc_ref[1:2, :] += dsc
            acc_ref[2:3, :] += dnw

    lat = lambda w: pl.BlockSpec((TM, w), lambda i: (_lat(i), 0))
    return _pcall(
        body, carried, name="input_bwd", grid=(N_TILES,),
        in_specs=[pl.BlockSpec((TM, WA), lambda i: (i, 0)), pl.BlockSpec((TM, WB), lambda i: (i, 0)),
                  lat(WC), _VMEM_WHOLE, _VMEM_WHOLE, _VMEM_WHOLE, _full((TM, D)), lat(D), lat(D),
                  _full((1, D)), _full((2, D)), _full((2, D))],
        out_specs=[lat(D), _full((8, D))],
        out_shape=[jax.ShapeDtypeStruct((S, D), F32), jax.ShapeDtypeStruct((8, D), F32)],
        scratch_shapes=[], operands=[dp_a, dp_b, dp_c, w_a, w_b, w_c, ctx, x, dx1, nw, sh, sc])


_C1 = 1.0 - ADAM_B1 ** ADAM_STEP
_C2 = 1.0 - ADAM_B2 ** ADAM_STEP


def _adamw_math(w, g, m, v):
    m = ADAM_B1 * m + (1.0 - ADAM_B1) * g
    v = ADAM_B2 * v + (1.0 - ADAM_B2) * (g * g)
    m_hat = m / _C1
    v_hat = v / _C2
    delta = -ADAM_LR * (m_hat / (jnp.sqrt(v_hat) + ADAM_EPS) + ADAM_WD * w)
    return delta, m, v


def _adamw_sharded(terms, w, m, v, name, tr, extra=None, after=None):
    rows, cols = w.shape

    def body(*refs):
        t_ref, w_ref, m_ref, v_ref = refs[:4]
        g_ref, d_ref, nm_ref, nv_ref = refs[-4:]
        g = t_ref[0].astype(F32)
        for s in range(1, N_CHIPS):
            g = g + t_ref[s].astype(F32)
        if extra is not None:
            g = g + refs[4][...].astype(F32)
        g_ref[...] = g
        d_ref[...], nm_ref[...], nv_ref[...] = _adamw_math(w_ref[...], g, m_ref[...], v_ref[...])

    blk = pl.BlockSpec((tr, cols), lambda i: (i, 0))
    return pl.pallas_call(
        body, name=name, grid=(rows // tr,),
        in_specs=[pl.BlockSpec((N_CHIPS, tr, cols), lambda i: (0, i, 0)), blk, blk, blk]
        + ([blk] if extra is not None else []) + ([_ANY] if after is not None else []),
        out_specs=[blk] * 4,
        out_shape=[jax.ShapeDtypeStruct((rows, cols), F32)] * 4,
        compiler_params=_cp(("parallel",)),
    )(terms, w, m, v, *([extra] if extra is not None else []), *([after] if after is not None else []))


def _adamw_plain(g, w, m, v, name, tr=None):
    def body(g_ref, w_ref, m_ref, v_ref, d_ref, nm_ref, nv_ref):
        d_ref[...], nm_ref[...], nv_ref[...] = _adamw_math(w_ref[...], g_ref[...], m_ref[...], v_ref[...])

    if tr is None:
        return pl.pallas_call(
            body, name=name, in_specs=[_VMEM_WHOLE] * 4, out_specs=[_VMEM_WHOLE] * 3,
            out_shape=[jax.ShapeDtypeStruct(w.shape, F32)] * 3,
            compiler_params=_cp(),
        )(g, w, m, v)
    blk = pl.BlockSpec((tr, w.shape[1]), lambda i: (i, 0))
    return pl.pallas_call(
        body, name=name, grid=(w.shape[0] // tr,), in_specs=[blk] * 4, out_specs=[blk] * 3,
        out_shape=[jax.ShapeDtypeStruct(w.shape, F32)] * 3,
        compiler_params=_cp(("parallel",)),
    )(g, w, m, v)


SMALL_ROWS = 16
R_DMOD, R_DCTX, R_NMIX, R_NFFN, R_MISC, R_DLB, R_BADA01 = 0, 6, 8, 9, 10, 11, 13
M_HNW, M_QNW, M_KNW, M_SINK, M_LOSS = 0, 128, 256, 384, 512


def _pack_small(acc_in, acc_mg, acc_ffn, acc_prep, dsink, dlb):
    def body(in_ref, mg_ref, ff_ref, pp_ref, ds_ref, dlb_ref, o_ref):
        o_ref[...] = jnp.zeros_like(o_ref)
        o_ref[0:2, :] = in_ref[0:2, :]
        o_ref[2:3, :] = mg_ref[0:1, :]
        o_ref[3:5, :] = ff_ref[0:2, :]
        o_ref[5:6, :] = ff_ref[3:4, :]
        o_ref[6:8, :] = in_ref[3:5, :]
        o_ref[8:9, :] = in_ref[2:3, :]
        o_ref[9:10, :] = ff_ref[2:3, :]
        o_ref[10:11, M_HNW:M_HNW + HG_DIM] = pp_ref[0:1, 0:HG_DIM]
        r = lax.broadcasted_iota(jnp.int32, (ATW, 128), 0)
        c = lax.broadcasted_iota(jnp.int32, (ATW, 128), 1)
        fold = jnp.where((r % HEAD_DIM == c) & (c < HEAD_DIM), 1.0, 0.0).astype(BF16)
        qk = jnp.concatenate([pp_ref[1:2, :], pp_ref[2:3, :], jnp.zeros((6, ATW), F32)], axis=0)
        folded = _dot_exact_rhs01(qk, fold)
        o_ref[10:11, M_QNW:M_QNW + 128] = folded[0:1, :]
        o_ref[10:11, M_KNW:M_KNW + 128] = folded[1:2, :]
        o_ref[10:11, M_SINK:M_SINK + ATT_HEADS] = ds_ref[...]
        o_ref[10:11, M_LOSS:M_LOSS + 128] = ff_ref[4:5, 0:128]
        o_ref[11:13, 0:HGW] = dlb_ref[...]

    return pl.pallas_call(
        body, name="pack_small", in_specs=[_VMEM_WHOLE] * 6, out_specs=_VMEM_WHOLE,
        out_shape=jax.ShapeDtypeStruct((SMALL_ROWS, D), F32), compiler_params=_cp(),
    )(acc_in, acc_mg, acc_ffn, acc_prep, dsink, dlb)


def _sum_small(gathered):
    def body(g_ref, o_ref):
        tot = g_ref[0]
        for s in range(1, N_DEV):
            tot = tot + g_ref[s]
        o_ref[...] = tot
        o_ref[R_BADA01:R_BADA01 + 2, :] = tot[0:2, :] + tot[R_DCTX:R_DCTX + 2, :]

    return pl.pallas_call(
        body, name="sum_small", in_specs=[_VMEM_WHOLE], out_specs=_VMEM_WHOLE,
        out_shape=jax.ShapeDtypeStruct((SMALL_ROWS, D), F32), compiler_params=_cp(),
    )(gathered)


_REP_NAMES = ("b_ada", "c_ctx", "norm_mix_w", "norm_ffn_w", "hgrn_norm_w", "q_norm_w", "k_norm_w", "attn_sinks")


def _adamw_replicated(tot, g_c_ctx, ws, ms, vs):
    n = len(_REP_NAMES)

    def body(*refs):
        tot_ref, gc_ref = refs[0], refs[1]
        w_refs, m_refs, v_refs = refs[2:2 + n], refs[2 + n:2 + 2 * n], refs[2 + 2 * n:2 + 3 * n]
        outs = refs[2 + 3 * n:]
        row = lambda r: tot_ref[r:r + 1, :]
        misc = row(R_MISC)
        grads = [jnp.concatenate([row(R_BADA01), row(R_BADA01 + 1)] + [row(k) for k in range(2, 6)], axis=1),
                 gc_ref[...], row(R_NMIX), row(R_NFFN),
                 misc[:, M_HNW:M_HNW + HG_DIM], misc[:, M_QNW:M_QNW + HEAD_DIM],
                 misc[:, M_KNW:M_KNW + HEAD_DIM], misc[:, M_SINK:M_SINK + ATT_HEADS]]
        for k in range(n):
            outs[k][...] = grads[k]
            outs[n + k][...], outs[2 * n + k][...], outs[3 * n + k][...] = _adamw_math(
                w_refs[k][...], grads[k], m_refs[k][...], v_refs[k][...])

    shapes = [jax.ShapeDtypeStruct(w.shape, F32) for w in ws]
    return pl.pallas_call(
        body, name="adamw_replicated", in_specs=[_VMEM_WHOLE] * (2 + 3 * n), out_specs=[_VMEM_WHOLE] * (4 * n),
        out_shape=shapes * 4, compiler_params=_cp(),
    )(tot, g_c_ctx, *ws, *ms, *vs)


def _lb_grads(dlb, lbl):
    def body(d_ref, l_ref, o_ref):
        for d in (0, 1):
            ll = l_ref[d]
            lb = _sigmoid(ll[0:1, :] - ll[1:2, :])
            t = d_ref[d:d + 1, :] * lb * (1.0 - lb)
            o_ref[d, 0:1, :] = t
            o_ref[d, 1:2, :] = -t

    return pl.pallas_call(
        body, name="lb_grads", in_specs=[_VMEM_WHOLE] * 2, out_specs=_VMEM_WHOLE,
        out_shape=jax.ShapeDtypeStruct((2, 2, HGW), F32), compiler_params=_cp(),
    )(dlb, lbl)


def _c_ctx_grad(terms, c_ctx):
    def body(t_ref, c_ref, o_ref):
        tot = t_ref[0, 8:9, :]
        for s in range(1, N_DEV):
            tot = tot + t_ref[s, 8:9, :]
        cv = c_ref[...]
        sg = _sigmoid(cv)
        o_ref[...] = tot * (sg * (1.0 + cv * (1.0 - sg)))

    return pl.pallas_call(
        body, name="c_ctx_grad", in_specs=[_VMEM_WHOLE] * 2, out_specs=_VMEM_WHOLE,
        out_shape=jax.ShapeDtypeStruct((1, D), F32), compiler_params=_cp(),
    )(terms, c_ctx)


def _in_perm():
    fz, bz, inp, kk, vv, qhg, ghg, qat, gates = 0, 512, 1024, 1536, 1664, 1792, 2304, 2816, 3328
    cols = []
    for h in range(HG_HEADS):
        for base in (fz, bz, inp, qhg):
            cols += list(range(base + 128 * h, base + 128 * (h + 1)))
    cols += list(range(ghg, ghg + 512)) + list(range(qat, qat + 512))
    cols += list(range(kk, kk + 128)) + list(range(vv, vv + 128))
    cols += list(range(gates, gates + 2048))
    return np.asarray(cols, np.int32)


_PERM = _in_perm()


_PIECES = {"a": (0, WA, 128), "b": (WA, WB, 256), "c": (WA + WB, WC, 256)}


def _block_table(piece):
    lo, n, blk = _PIECES[piece]
    starts = [int(_PERM[r]) for r in range(lo, lo + n, blk)]
    assert all(s % blk == 0 and np.array_equal(_PERM[r:r + blk], np.arange(s, s + blk))
               for s, r in zip(starts, range(lo, lo + n, blk)))
    return jnp.asarray([s // blk for s in starts], jnp.int32), blk


def _pick_row_blocks(x, table, blk, name):
    cols = x.shape[1]

    def body(t_ref, x_ref, o_ref):
        o_ref[...] = x_ref[...]

    return pl.pallas_call(
        body, name=name,
        grid_spec=pltpu.PrefetchScalarGridSpec(
            num_scalar_prefetch=1, grid=(table.shape[0],),
            in_specs=[pl.BlockSpec((blk, cols), lambda i, t: (t[i], 0))],
            out_specs=pl.BlockSpec((blk, cols), lambda i, t: (i, 0))),
        out_shape=jax.ShapeDtypeStruct((table.shape[0] * blk, cols), x.dtype),
        compiler_params=_cp(("arbitrary",)),
    )(table, x)


def _place_row_blocks(x, table, blk, into, out_rows, name):
    cols = x.shape[1]

    def body(t_ref, x_ref, *rest):
        rest[-1][...] = x_ref[...]

    operands, in_specs, aliases = [table, x], [pl.BlockSpec((blk, cols), lambda i, t: (i, 0))], {}
    if into is not None:
        operands.append(into)
        in_specs.append(_ANY)
        aliases = {2: 0}
    return pl.pallas_call(
        body, name=name,
        grid_spec=pltpu.PrefetchScalarGridSpec(
            num_scalar_prefetch=1, grid=(table.shape[0],), in_specs=in_specs,
            out_specs=pl.BlockSpec((blk, cols), lambda i, t: (t[i], 0))),
        out_shape=jax.ShapeDtypeStruct((out_rows, cols), x.dtype),
        input_output_aliases=aliases,
        compiler_params=_cp(("arbitrary",)),
    )(*operands)


def _local_step(x2, ctx2, h_all, h_lat, tgt, lbl, sh_in, sc_in, gate1, sh2, sc2, gate2, norm_mix_w, norm_ffn_w,
                hgrn_norm_w, q_norm_w, k_norm_w, attn_sinks, w_a, w_b, w_c, s_bh, s_ba, s_out,
                s_gate, s_up, s_down):
    first_last = lambda n: [(0, True), (n - 1, False)]
    p_a = _mm_nt(h_all, w_a, tm=T, tn=512, out_dtype=F32, name="proj_a")
    (o, st), (g_gate, g_bh, g_ba) = _hgrn_fwd(
        p_a, lbl, (_gather_comm_relayed([s_gate, s_bh, s_ba]),
                   [(0, True), (HG_HEADS - 2, True), (HG_HEADS - 1, False)]))
    (p_b, p_c), (g_out,) = _proj_bc(
        h_all, w_b, w_c, (_gather_comm_relayed([s_out]), [(0, True), (N_TILES - 4, True), (N_TILES - 1, False)]))
    cos, sin = _rope_tables()
    qnw_t, knw_t = jnp.tile(q_norm_w, (1, ATT_HEADS)), jnp.tile(k_norm_w, (1, KV_HEADS))
    y_hg, qn, k_rep, v_rep = _prep_fwd(p_b, o, cos, sin, hgrn_norm_w, qnw_t, knw_t)
    (y_at, lse), (g_up, g_down) = _attn_fwd(
        qn, k_rep, v_rep, attn_sinks,
        (_gather_comm_relayed([s_up, s_down]), [(0, True), (N_BLOCKS - 6, True), (N_BLOCKS - 1, False)]))
    w_bh, w_ba, w_o = g_bh.reshape(D, HGW), g_ba.reshape(D, ATW), g_out.reshape(D, D)
    (mixed, r, x1, h2), _ = _merge_fwd(
        y_hg, y_at, p_c, x2, w_bh, w_ba, w_o, gate1, norm_ffn_w, sh2, sc2)
    g_gate, g_up, g_down = [g.reshape(N_FF_TILES, FF_TILE, D) for g in (g_gate, g_up, g_down)]

    act, d_gate, d_up, d_f, dx1, acc_ffn = _ffn_fused(x1, h2, tgt, g_gate, g_up, g_down, gate2,
                                                      norm_ffn_w, sc2)
    by_chip = lambda t: t.reshape((N_CHIPS, 2) + t.shape[1:])
    ff_by_chip = lambda t: t.reshape(N_CHIPS, 2, FF_BLK, D)
    t_down, _ = _mm_tn_blocked(act, d_f, "grad_down", N_FF_TILES)
    t_down = ff_by_chip(t_down)
    t_gate, (f_down,) = _mm_tn_blocked(d_gate, h2, "grad_gate", N_FF_HALVES,
                                       (_sibling_comm([t_down]), first_last(N_FF_HALVES)))
    t_gate = ff_by_chip(t_gate)
    t_up, (f_gate,) = _mm_tn_blocked(d_up, h2, "grad_up", N_FF_HALVES,
                                     (_sibling_comm([t_gate]), first_last(N_FF_HALVES)))
    t_up = ff_by_chip(t_up)

    (d_r, d_a, d_b, dp_c, dy_hg, dy_at, acc_mg), (f_up,) = _merge_bwd(
        dx1, r, y_hg, y_at, p_c, w_bh, w_ba, w_o, gate1, (_sibling_comm([t_up]), first_last(N_LAT_TILES)))
    c_down, c_gate, c_up = [_pair_sum(t, f, "pair_sum_" + nm) for t, f, nm in
                            ((t_down, f_down, "down"), (t_gate, f_gate, "gate"), (t_up, f_up, "up"))]
    t_out = _mm_tn(mixed, d_r, tk=1024, nk=2, tm=1024, tn=1024, out_dtype=BF16, name="grad_out")
    t_bh = _mm_tn(d_a, y_hg, tk=2048, nk=1, tm=1024, tn=512, out_dtype=BF16, name="grad_bh")
    t_ba = _mm_tn(d_b, y_at, tk=2048, nk=1, tm=1024, tn=512, out_dtype=BF16, name="grad_ba")
    t_bh, t_ba, t_out = [by_chip(t.reshape(N_DEV, D // N_DEV, t.shape[1])) for t in (t_bh, t_ba, t_out)]
    (dq, dk_rep, dv_rep, dsink), (r_up,) = _attn_bwd(
        qn, k_rep, v_rep, attn_sinks, y_at, lse, dy_at, (_chip_comm([c_up]), first_last(N_BLOCKS)))
    (dp_b, d_o, acc_prep), (f_bh, f_ba, f_out) = _prep_bwd(
        p_b, o, cos, sin, hgrn_norm_w, qnw_t, knw_t, dy_hg, dq, dk_rep, dv_rep,
        (_sibling_comm([t_bh, t_ba, t_out]), first_last(N_TILES)))
    c_bh, c_ba, c_out = [_pair_sum(t, f, "pair_sum_" + nm) for t, f, nm in
                         ((t_bh, f_bh, "bh"), (t_ba, f_ba, "ba"), (t_out, f_out, "out"))]
    (dp_a, dlb), (r_bh, r_ba, r_out, r_down, r_gate) = _hgrn_bwd(
        p_a, lbl, d_o, st, (_chip_comm([c_bh, c_ba, c_out, c_down, c_gate]), first_last(HG_HEADS)))
    t_a = _mm_tn(dp_a, h_all, tk=T, nk=1, tm=1024, tn=1024, out_dtype=BF16, name="grad_in_a")
    t_b = _mm_tn(dp_b, h_all, tk=T, nk=1, tm=640, tn=1024, out_dtype=BF16, name="grad_in_b")
    t_c = _mm_tn(dp_c, h_lat, tk=1024, nk=2, tm=1024, tn=1024, out_dtype=BF16, name="grad_in_c")
    t_in = None
    for piece, nm in ((t_a, "a"), (t_b, "b"), (t_c, "c")):
        t_in = _place_row_blocks(piece, *_block_table(nm), t_in, IN_COLS, "order_terms_" + nm)
    t_in = by_chip(t_in.reshape(N_DEV, IN_BLK, D))
    (f_in,) = _run_comm(_sibling_comm([t_in]), "scatter_in_sibling")
    c_in = _pair_sum(t_in, f_in, "pair_sum_in")
    sems, c_in, land, token = _chip_exchange_start(c_in, jnp.zeros(c_in.shape, c_in.dtype))
    (grad_x, acc_in), _ = _input_bwd(dp_a, dp_b, dp_c, w_a, w_b, w_c, ctx2, x2, dx1,
                                     norm_mix_w + token[0, 0], sh_in, sc_in)
    small = _pack_small(acc_in, acc_mg, acc_ffn, acc_prep, dsink, dlb)
    return grad_x, small, [r_bh, r_ba, r_out, r_gate, r_up, r_down], (sems, c_in, land)


def kernel(x, c, ctx, c_ctx, w_ada, b_ada, norm_mix_w, norm_ffn_w, w_in, hgrn_lb_logits, hgrn_norm_w, q_norm_w, k_norm_w, attn_sinks, w_branch_hgrn, w_branch_attn, w_out, w_ffn_gate, w_ffn_up, w_ffn_down, loss_target, m_c_ctx, m_w_ada, m_b_ada, m_norm_mix_w, m_norm_ffn_w, m_w_in, m_hgrn_lb_logits, m_hgrn_norm_w, m_q_norm_w, m_k_norm_w, m_attn_sinks, m_w_branch_hgrn, m_w_branch_attn, m_w_out, m_w_ffn_gate, m_w_ffn_up, m_w_ffn_down, v_c_ctx, v_w_ada, v_b_ada, v_norm_mix_w, v_norm_ffn_w, v_w_in, v_hgrn_lb_logits, v_hgrn_norm_w, v_q_norm_w, v_k_norm_w, v_attn_sinks, v_w_branch_hgrn, v_w_branch_attn, v_w_out, v_w_ffn_gate, v_w_ffn_up, v_w_ffn_down):
    me = 4 * lax.axis_index("x") + 2 * lax.axis_index("y") + lax.axis_index("c")
    x2, ctx2, tgt = x[0], ctx[0], loss_target[0]
    w_ada2, w_in2 = w_ada[0], w_in[0]

    cond = jnp.zeros((8, D), F32).at[0].set(c[0]).at[1, :256].set(hgrn_lb_logits.reshape(256))
    b_cols = lax.dynamic_slice(b_ada, (0, me * ADA_BLK), (1, ADA_BLK))
    g0, cc, mod, g_in, h_all, h_lat = _prologue(cond, c_ctx.reshape(1, D), w_ada2, b_cols, w_in2.T.astype(BF16),
                                         x2, ctx2, norm_mix_w)
    lbl = jnp.transpose(g0[:, 1, :256].reshape(N_DEV, 2, 2, 64), (1, 2, 0, 3)).reshape(2, 2, HGW)
    sh1, sc1, gate1, sh2, sc2, gate2 = [mod[k:k + 1] for k in range(6)]
    sh_in = jnp.concatenate([mod[6:7], sh1], axis=0)
    sc_in = jnp.concatenate([mod[7:8], sc1], axis=0)

    shards = [w_branch_hgrn[0].T, w_branch_attn[0].T, w_out[0], w_ffn_gate[0].T, w_ffn_up[0].T, w_ffn_down[0]]
    w_in_t = g_in.reshape(IN_COLS, D)
    w_a, w_b, w_c = [_pick_row_blocks(w_in_t, *_block_table(nm), "order_w_" + nm) for nm in "abc"]

    grad_x, small, (r_bh, r_ba, r_out, r_gate, r_up, r_down), pending_in = _local_step(
        x2, ctx2, h_all, h_lat, tgt, lbl, sh_in, sc_in, gate1, sh2, sc2, gate2, norm_mix_w, norm_ffn_w, hgrn_norm_w,
        q_norm_w, k_norm_w, attn_sinks, w_a, w_b, w_c, *[s.astype(BF16) for s in shards])

    big, updated = {}, []
    for nm, rr, ww, mm, vv, tr, transposed in (
            ("w_branch_hgrn", r_bh, w_branch_hgrn[0], m_w_branch_hgrn[0], v_w_branch_hgrn[0], 128, True),
            ("w_branch_attn", r_ba, w_branch_attn[0], m_w_branch_attn[0], v_w_branch_attn[0], 128, True),
            ("w_out", r_out, w_out[0], m_w_out[0], v_w_out[0], 128, False),
            ("w_ffn_gate", r_gate, w_ffn_gate[0], m_w_ffn_gate[0], v_w_ffn_gate[0], 176, True),
            ("w_ffn_up", r_up, w_ffn_up[0], m_w_ffn_up[0], v_w_ffn_up[0], 176, True),
            ("w_ffn_down", r_down, w_ffn_down[0], m_w_ffn_down[0], v_w_ffn_down[0], 176, False)):
        if transposed:
            res = _adamw_sharded(rr, ww.T, mm.T, vv.T, "adamw_" + nm, tr, after=grad_x)
            big[nm] = [t.T[None] for t in res]
        else:
            res = _adamw_sharded(rr, ww, mm, vv, "adamw_" + nm, tr, after=grad_x)
            big[nm] = [t[None] for t in res]
        updated.append(res[1])

    (g2,) = _all_gather([small], "gather_small", True, after=updated)
    tot = _sum_small(g2)
    dm = jnp.zeros((16, 6 * D), F32).at[:8].set(g2[:, R_DMOD:R_DMOD + 6, :].reshape(N_DEV, 6 * D))
    dm = dm.at[8, :2 * D].set(tot[R_DCTX:R_DCTX + 2].reshape(2 * D))
    dm_cols = lax.dynamic_slice(dm, (0, me * ADA_BLK), (16, ADA_BLK))
    g_w_ada, dsc_term = _ada_grads(cc, dm_cols, w_ada2)
    (g3,) = _all_gather([dsc_term], "gather_cctx", True)
    g_c_ctx = _c_ctx_grad(g3, c_ctx.reshape(1, D))
    g_lbl = _lb_grads(tot[R_DLB:R_DLB + 2, :HGW], lbl)
    g_lb_mine = lax.dynamic_slice(g_lbl, (0, 0, me * 64), (2, 2, 64))
    misc = tot[R_MISC]
    loss = misc[M_LOSS]

    rep_out = _adamw_replicated(
        tot, g_c_ctx,
        [b_ada, c_ctx.reshape(1, D), norm_mix_w, norm_ffn_w, hgrn_norm_w, q_norm_w, k_norm_w, attn_sinks],
        [m_b_ada, m_c_ctx.reshape(1, D), m_norm_mix_w, m_norm_ffn_w, m_hgrn_norm_w, m_q_norm_w, m_k_norm_w,
         m_attn_sinks],
        [v_b_ada, v_c_ctx.reshape(1, D), v_norm_mix_w, v_norm_ffn_w, v_hgrn_norm_w, v_q_norm_w, v_k_norm_w,
         v_attn_sinks])
    rep = []
    for kind in range(4):
        vals = dict(zip(_REP_NAMES, rep_out[kind * len(_REP_NAMES):(kind + 1) * len(_REP_NAMES)]))
        vals["c_ctx"] = vals["c_ctx"].reshape(D)
        rep.append(vals)

    sems, c_in, land = pending_in
    d_ada, nm_ada, nv_ada = _adamw_plain(g_w_ada, w_ada2, m_w_ada[0], v_w_ada[0], "adamw_w_ada", tr=256)
    land = _chip_exchange_wait(sems, c_in, land, d_ada)
    own = lax.dynamic_index_in_dim(c_in, 2 * lax.axis_index("x") + lax.axis_index("y"), 0, keepdims=False)
    big["w_in"] = [t.T[None] for t in _adamw_sharded(land, w_in2.T, m_w_in[0].T, v_w_in[0].T, "adamw_w_in", 336,
                                                     extra=own)]
    ada = [t[None] for t in (g_w_ada, d_ada, nm_ada, nv_ada)]
    lb_w = hgrn_lb_logits.reshape(4, 64)
    d_lb, nm_lb, nv_lb = _adamw_plain(g_lb_mine.reshape(4, 64), lb_w, m_hgrn_lb_logits.reshape(4, 64),
                                      v_hgrn_lb_logits.reshape(4, 64), "adamw_lb")
    lbs = [t.reshape(2, 2, 64) for t in (g_lb_mine, d_lb, nm_lb, nv_lb)]

    names = ['c_ctx', 'w_ada', 'b_ada', 'norm_mix_w', 'norm_ffn_w', 'w_in', 'hgrn_lb_logits', 'hgrn_norm_w',
             'q_norm_w', 'k_norm_w', 'attn_sinks', 'w_branch_hgrn', 'w_branch_attn', 'w_out', 'w_ffn_gate',
             'w_ffn_up', 'w_ffn_down']
    outs = [loss, grad_x[None]]
    for kind in range(4):
        for nm in names:
            if nm == 'w_ada':
                outs.append(ada[kind])
            elif nm == 'hgrn_lb_logits':
                outs.append(lbs[kind])
            elif nm in big:
                outs.append(big[nm][kind])
            else:
                outs.append(rep[kind][nm])
    return tuple(outs)
```

```python
import functools
import math

import numpy as np
import jax
import jax.numpy as jnp
from jax import lax
from jax.experimental import pallas as pl
from jax.experimental.pallas import tpu as pltpu

F32 = jnp.float32
BF16 = jnp.bfloat16

N_DEV = 8
D = 1024
S = 2048
L = 256
T = L + S
TM = 256
N_TILES = T // TM
N_LAT_TILES = S // TM
HG_HEADS = 4
HG_DIM = 128
HGW = 512
CHUNK = 32
N_CHUNKS = T // CHUNK
N_CTX_CHUNKS = L // CHUNK
ATT_HEADS = 8
KV_HEADS = 2
HEAD_DIM = 64
ATW = 512
KVW = 128
BLOCK = 128
N_BLOCKS = S // BLOCK
GRID_W = 64
ROPE_THETA = 10000.0
D_FF = 2816
FF_BLK = D_FF // N_DEV
FF_TILE = 256
N_FF_TILES = D_FF // FF_TILE
N_FF_HALVES = 2
IN_COLS = 5376
IN_BLK = IN_COLS // N_DEV
ADA_BLK = 6 * D // N_DEV
EPS = 1e-6
WA, WB, WC = 2048, 1280, 2048

ADAM_LR = 0.001
ADAM_B1 = 0.9
ADAM_B2 = 0.999
ADAM_EPS = 1e-08
ADAM_WD = 0.01
ADAM_STEP = 10

VMEM_LIMIT = 56 * 1024 * 1024
MESH = pl.DeviceIdType.MESH


def _cp(sem=None, vmem=VMEM_LIMIT):
    return pltpu.CompilerParams(dimension_semantics=sem, vmem_limit_bytes=vmem)


def _full(shape):
    n = len(shape)
    return pl.BlockSpec(shape, lambda *_: (0,) * n)


_VMEM_WHOLE = pl.BlockSpec(memory_space=pltpu.VMEM)
_ANY = pl.BlockSpec(memory_space=pl.ANY)


def _sigmoid(v):
    return 1.0 / (1.0 + jnp.exp(-v))


def _dot(a, b):
    return jnp.dot(a, b, preferred_element_type=F32)


def _dot_nt(a, b):
    return lax.dot_general(a, b, (((1,), (1,)), ((), ())), preferred_element_type=F32)


def _dot_tn(a, b):
    return lax.dot_general(a, b, (((0,), (0,)), ((), ())), preferred_element_type=F32)


def _split3(v):
    hi = v.astype(BF16)
    r = v - hi.astype(F32)
    mid = r.astype(BF16)
    lo = (r - mid.astype(F32)).astype(BF16)
    return hi, mid, lo


def _dot_exact_rhs01(v, m01):
    hi, mid, lo = _split3(v)
    return _dot(hi, m01) + _dot(mid, m01) + _dot(lo, m01)


def _split2(v):
    hi = v.astype(BF16)
    return hi, (v - hi.astype(F32)).astype(BF16)


def _dot_lhs01(m01, v):
    hi, lo = _split2(v)
    return _dot(m01, hi) + _dot(m01, lo)


def _dot_f32(a, b, dot=_dot):
    ah, am, al = _split3(a)
    bh, bm, bl = _split3(b)
    return (dot(ah, bh) + (dot(ah, bm) + dot(am, bh))
            + (dot(am, bm) + dot(ah, bl) + dot(al, bh)))


def _my_pos():
    return lax.axis_index("x"), lax.axis_index("y"), lax.axis_index("c")


class _Comm:
    def __init__(self, operands, out_shapes, sems, phases):
        self.operands, self.out_shapes, self.sems, self.phases = operands, out_shapes, sems, phases


def _gather_comm(blocks):
    n = len(blocks)

    def parts(ins, outs, sems):
        send_sems, recv_sems, local_sems = sems
        x, y, c = _my_pos()
        me, sibling = (x, y, c), (x, y, 1 - c)
        chips = [(1 - x, y), (x, 1 - y), (1 - x, 1 - y)]

        def slot(a, px, py, pc):
            return outs[a].at[4 * px + 2 * py + pc]

        def copy(a, k, block, to, src=None):
            return pltpu.make_async_remote_copy(
                src_ref=slot(a, *block) if src is None else src, dst_ref=slot(a, *block),
                send_sem=send_sems.at[a, k], recv_sem=recv_sems.at[a, k],
                device_id=to, device_id_type=MESH)

        mine = [pltpu.make_async_copy(ins[a], slot(a, *me), local_sems.at[a]) for a in range(n)]
        first = []
        for a in range(n):
            first.append(copy(a, 0, me, sibling, src=ins[a]))
            first += [copy(a, 1 + j, me, (*chip, c), src=ins[a]) for j, chip in enumerate(chips)]
        passed = [copy(a, 4 + j, (*chip, c), sibling) for j, chip in enumerate(chips) for a in range(n)]
        return c, me, sibling, chips, copy, mine, first, passed

    def start(ins, outs, sems):
        _, _, _, _, _, mine, first, _ = parts(ins, outs, sems)
        for cp in mine + first:
            cp.start()

    def forward(ins, outs, sems):
        c, me, _, chips, copy, _, _, passed = parts(ins, outs, sems)
        for j, chip in enumerate(chips):
            for a in range(n):
                copy(a, 1 + j, (*chip, c), me).wait_recv()
                passed[j * n + a].start()

    def finish(ins, outs, sems):
        c, me, sibling, chips, copy, mine, first, passed = parts(ins, outs, sems)
        for a in range(n):
            copy(a, 0, sibling, me).wait_recv()
            for j, chip in enumerate(chips):
                copy(a, 4 + j, (*chip, 1 - c), me).wait_recv()
        for cp in first + passed:
            cp.wait_send()
        for cp in mine:
            cp.wait()

    return _Comm(blocks, [jax.ShapeDtypeStruct((N_DEV,) + b.shape, b.dtype) for b in blocks],
                 [pltpu.SemaphoreType.DMA((n, 7)), pltpu.SemaphoreType.DMA((n, 7)), pltpu.SemaphoreType.DMA((n,))],
                 [start, forward, finish])


def _gather_comm_relayed(blocks):
    n = len(blocks)

    def parts(ins, outs, sems):
        send_sems, recv_sems, local_sems = sems
        x, y, c = _my_pos()
        me, sibling = (x, y, c), (x, y, 1 - c)
        x_nbr, y_nbr, diag = (1 - x, y, c), (x, 1 - y, c), (1 - x, 1 - y, c)

        def slot(a, dev, half=None):
            ref = outs[a].at[4 * dev[0] + 2 * dev[1] + dev[2]]
            if half is None:
                return ref
            rows = blocks[a].shape[0] // 2
            return ref.at[pl.ds(half * rows, rows)]

        def copy(a, k, block, to, half=None, src=None):
            return pltpu.make_async_remote_copy(
                src_ref=slot(a, block, half) if src is None else src, dst_ref=slot(a, block, half),
                send_sem=send_sems.at[a, k], recv_sem=recv_sems.at[a, k],
                device_id=to, device_id_type=MESH)

        mine = [pltpu.make_async_copy(ins[a], slot(a, me), local_sems.at[a]) for a in range(n)]
        return me, sibling, x_nbr, y_nbr, diag, copy, mine

    def start(ins, outs, sems):
        me, sibling, x_nbr, y_nbr, _, copy, mine = parts(ins, outs, sems)
        for cp in mine:
            cp.start()
        for a in range(n):
            for k, to in ((1, x_nbr), (2, y_nbr), (0, sibling)):
                copy(a, k, me, to, src=ins[a]).start()

    def forward(ins, outs, sems):
        me, sibling, x_nbr, y_nbr, _, copy, _ = parts(ins, outs, sems)
        for a in range(n):
            copy(a, 1, x_nbr, me).wait_recv()
            copy(a, 3, x_nbr, y_nbr, half=0).start()
            copy(a, 5, x_nbr, sibling).start()
        for a in range(n):
            copy(a, 2, y_nbr, me).wait_recv()
            copy(a, 4, y_nbr, x_nbr, half=1).start()
            copy(a, 6, y_nbr, sibling).start()

    def finish(ins, outs, sems):
        me, sibling, x_nbr, y_nbr, diag, copy, mine = parts(ins, outs, sems)
        sib = lambda dev: (dev[0], dev[1], sibling[2])
        for a in range(n):
            copy(a, 3, diag, me, half=0).wait_recv()
            copy(a, 4, diag, me, half=1).wait_recv()
            copy(a, 7, diag, sibling).start()
        for a in range(n):
            copy(a, 0, sibling, me).wait_recv()
            for k, dev in ((5, x_nbr), (6, y_nbr), (7, diag)):
                copy(a, k, sib(dev), me).wait_recv()
        for a in range(n):
            for k, block, to, half in ((0, me, sibling, None), (1, me, x_nbr, None), (2, me, y_nbr, None),
                                       (3, x_nbr, y_nbr, 0), (4, y_nbr, x_nbr, 1), (5, x_nbr, sibling, None),
                                       (6, y_nbr, sibling, None), (7, diag, sibling, None)):
                copy(a, k, block, to, half=half, src=ins[a] if block is me else None).wait_send()
        for cp in mine:
            cp.wait()

    return _Comm(blocks, [jax.ShapeDtypeStruct((N_DEV,) + b.shape, b.dtype) for b in blocks],
                 [pltpu.SemaphoreType.DMA((n, 8)), pltpu.SemaphoreType.DMA((n, 8)), pltpu.SemaphoreType.DMA((n,))],
                 [start, forward, finish])


_HBM = pl.BlockSpec(memory_space=pltpu.HBM)
_SEM = pl.BlockSpec(memory_space=pltpu.SEMAPHORE)
_SPLIT_COPY = pltpu.CompilerParams(has_side_effects=pltpu.SideEffectType.DATAFLOW_SIDE_EFFECTING)


def _chip_exchange_copies(src_ref, land_ref, sems):
    x, y, c = _my_pos()
    q_me = 2 * x + y
    pairs = []
    for j, (px, py) in enumerate([(1 - x, y), (x, 1 - y), (1 - x, 1 - y)]):
        q = 2 * px + py
        send = pltpu.make_async_remote_copy(
            src_ref=src_ref.at[q], dst_ref=land_ref.at[q_me], send_sem=sems[j], recv_sem=sems[3 + j],
            device_id=(px, py, c), device_id_type=MESH)
        recv = pltpu.make_async_remote_copy(
            src_ref=src_ref.at[q], dst_ref=land_ref.at[q], send_sem=sems[j], recv_sem=sems[3 + j],
            device_id=(x, y, c), device_id_type=MESH)
        pairs.append((send, recv))
    return pairs


def _chip_exchange_start(src, land):
    def body(src_ref, land_ref, *outs):
        sems, token = outs[:6], outs[8]
        for send, _ in _chip_exchange_copies(src_ref, land_ref, sems):
            send.start()
        token[...] = jnp.zeros_like(token)

    res = pl.pallas_call(
        body, name="scatter_in_start",
        out_shape=(pltpu.SemaphoreType.DMA(()),) * 6 + (
            pltpu.HBM(src.shape, src.dtype), pltpu.HBM(land.shape, land.dtype),
            jax.ShapeDtypeStruct((8, 128), F32)),
        in_specs=(_HBM, _HBM), out_specs=(_SEM,) * 6 + (_HBM, _HBM, pl.BlockSpec(memory_space=pltpu.VMEM)),
        input_output_aliases={0: 6, 1: 7}, compiler_params=_SPLIT_COPY,
    )(pltpu.with_memory_space_constraint(src, pltpu.HBM), pltpu.with_memory_space_constraint(land, pltpu.HBM))
    return res[:6], res[6], res[7], res[8]


def _chip_exchange_wait(sems, src_thru, land_thru, after):
    def body(src_ref, land_ref, *rest):
        for send, recv in _chip_exchange_copies(src_ref, land_ref, rest[:6]):
            send.wait_send()
            recv.wait_recv()

    return pl.pallas_call(
        body, name="scatter_in_wait",
        out_shape=(pltpu.HBM(src_thru.shape, src_thru.dtype), pltpu.HBM(land_thru.shape, land_thru.dtype)),
        in_specs=(_HBM, _HBM) + (_SEM,) * 6 + (_ANY,), out_specs=(_HBM, _HBM),
        input_output_aliases={0: 0, 1: 1}, compiler_params=_SPLIT_COPY,
    )(src_thru, land_thru, *sems, after)[1]


def _run_comm(comm, name, in_vmem=False, after=()):
    n_in, n_out, n_after = len(comm.operands), len(comm.out_shapes), len(after)

    def body(*refs):
        ins, refs = refs[:n_in], refs[n_in + n_after:]
        outs, sems = refs[:n_out], refs[n_out:]
        for phase in comm.phases:
            phase(ins, outs, sems)

    spec = _VMEM_WHOLE if in_vmem else _ANY
    return pl.pallas_call(
        body, name=name, out_shape=comm.out_shapes, in_specs=[spec] * n_in + [_ANY] * n_after,
        out_specs=[spec] * n_out, scratch_shapes=comm.sems,
    )(*comm.operands, *after)


def _carrier_call(body, comm, schedule, *, name, grid, in_specs, out_specs, out_shape, scratch_shapes, operands):
    n_in, n_out, n_scr = len(in_specs), len(out_specs), len(scratch_shapes)
    c_in, c_out = len(comm.operands), len(comm.out_shapes)

    def full_body(*refs):
        ins, refs = refs[:n_in], refs[n_in:]
        cins, refs = refs[:c_in], refs[c_in:]
        outs, refs = refs[:n_out], refs[n_out:]
        couts, refs = refs[:c_out], refs[c_out:]
        scr, csems = refs[:n_scr], refs[n_scr:]
        step = pl.program_id(0)

        def run(before):
            for (at, when_before), phase in zip(schedule, comm.phases):
                if when_before == before:
                    pl.when(step == at)(functools.partial(phase, cins, couts, csems))

        run(True)
        body(*ins, *outs, *scr)
        run(False)

    res = pl.pallas_call(
        full_body, name=name, grid=grid,
        in_specs=list(in_specs) + [_ANY] * c_in, out_specs=list(out_specs) + [_ANY] * c_out,
        out_shape=list(out_shape) + list(comm.out_shapes),
        scratch_shapes=list(scratch_shapes) + list(comm.sems),
        compiler_params=_cp(("arbitrary",)),
    )(*operands, *comm.operands)
    return res[:n_out], res[n_out:]


def _pcall(body, carried, *, name, grid, in_specs, out_specs, out_shape, scratch_shapes, operands):
    if carried is None:
        res = pl.pallas_call(body, name=name, grid=grid, in_specs=in_specs, out_specs=out_specs,
                             out_shape=out_shape, scratch_shapes=scratch_shapes,
                             compiler_params=_cp(("arbitrary",)))(*operands)
        return res, ()
    return _carrier_call(body, carried[0], carried[1], name=name, grid=grid, in_specs=in_specs,
                         out_specs=out_specs, out_shape=out_shape, scratch_shapes=scratch_shapes,
                         operands=operands)


def _all_gather(blocks, name, in_vmem, after=()):
    return _run_comm(_gather_comm(blocks), name, in_vmem, after)


N_CHIPS = 4


def _sibling_comm(contribs):
    n = len(contribs)

    def copies(ins, outs, sems):
        send_sems, recv_sems = sems
        x, y, c = _my_pos()
        return [pltpu.make_async_remote_copy(
            src_ref=ins[a].at[pl.ds(0, N_CHIPS), 1 - c], dst_ref=outs[a],
            send_sem=send_sems.at[a], recv_sem=recv_sems.at[a],
            device_id=(x, y, 1 - c), device_id_type=MESH) for a in range(n)]

    def start(ins, outs, sems):
        for cp in copies(ins, outs, sems):
            cp.start()

    def finish(ins, outs, sems):
        cps = copies(ins, outs, sems)
        for cp in cps:
            cp.wait_recv()
        for cp in cps:
            cp.wait_send()

    return _Comm(contribs, [jax.ShapeDtypeStruct((N_CHIPS,) + b.shape[2:], b.dtype) for b in contribs],
                 [pltpu.SemaphoreType.DMA((n,)), pltpu.SemaphoreType.DMA((n,))], [start, finish])


def _pair_sum(mine, theirs, name):
    _, _, rows, cols = mine.shape
    core = lax.axis_index("c").astype(jnp.int32).reshape(1)

    def body(c_ref, m_ref, t_ref, o_ref):
        o_ref[...] = (m_ref[...].astype(F32) + t_ref[...].astype(F32)).astype(BF16)

    return pl.pallas_call(
        body, name=name,
        grid_spec=pltpu.PrefetchScalarGridSpec(
            num_scalar_prefetch=1, grid=(N_CHIPS,),
            in_specs=[pl.BlockSpec((None, None, rows, cols), lambda q, c: (q, c[0], 0, 0)),
                      pl.BlockSpec((None, rows, cols), lambda q, c: (q, 0, 0))],
            out_specs=pl.BlockSpec((None, rows, cols), lambda q, c: (q, 0, 0))),
        out_shape=jax.ShapeDtypeStruct((N_CHIPS, rows, cols), BF16),
        compiler_params=_cp(("parallel",)),
    )(core, mine, theirs)


def _chip_comm(sums):
    n = len(sums)

    def parts(ins, outs, sems):
        send_sems, recv_sems, local_sems = sems
        x, y, c = _my_pos()
        q_me = 2 * x + y
        chips = [(1 - x, y), (x, 1 - y), (1 - x, 1 - y)]
        mine = [pltpu.make_async_copy(ins[a].at[q_me], outs[a].at[q_me], local_sems.at[a]) for a in range(n)]
        sends, recvs = [], []
        for j, (px, py) in enumerate(chips):
            for a in range(n):
                q = 2 * px + py
                sends.append(pltpu.make_async_remote_copy(
                    src_ref=ins[a].at[q], dst_ref=outs[a].at[q_me],
                    send_sem=send_sems.at[a, j], recv_sem=recv_sems.at[a, j],
                    device_id=(px, py, c), device_id_type=MESH))
                recvs.append(pltpu.make_async_remote_copy(
                    src_ref=ins[a].at[q], dst_ref=outs[a].at[q],
                    send_sem=send_sems.at[a, j], recv_sem=recv_sems.at[a, j],
                    device_id=(x, y, c), device_id_type=MESH))
        return mine, sends, recvs

    def start(ins, outs, sems):
        mine, sends, _ = parts(ins, outs, sems)
        for cp in mine + sends:
            cp.start()

    def finish(ins, outs, sems):
        mine, sends, recvs = parts(ins, outs, sems)
        for cp in recvs:
            cp.wait_recv()
        for cp in sends:
            cp.wait_send()
        for cp in mine:
            cp.wait()

    return _Comm(sums, [jax.ShapeDtypeStruct(b.shape, b.dtype) for b in sums],
                 [pltpu.SemaphoreType.DMA((n, 3)), pltpu.SemaphoreType.DMA((n, 3)), pltpu.SemaphoreType.DMA((n,))],
                 [start, finish])


def _mm_nt(a, bt, *, tm, tn, out_dtype, name, row_off=0, rows=None):
    rows = a.shape[0] if rows is None else rows
    n, k = bt.shape

    def body(a_ref, b_ref, o_ref):
        o_ref[...] = _dot_nt(a_ref[...], b_ref[...]).astype(out_dtype)

    return pl.pallas_call(
        body, name=name, grid=(rows // tm, n // tn),
        in_specs=[pl.BlockSpec((tm, k), lambda i, j: (i + row_off, 0)),
                  pl.BlockSpec((tn, k), lambda i, j: (j, 0))],
        out_specs=pl.BlockSpec((tm, tn), lambda i, j: (i, j)),
        out_shape=jax.ShapeDtypeStruct((rows, n), out_dtype),
        compiler_params=_cp(("parallel", "parallel")),
    )(a, bt)


def _mm_tn(a, b, *, tk, nk, tm, tn, out_dtype, name, a_off=0, b_off=0):
    m, n = a.shape[1], b.shape[1]

    def body(a_ref, b_ref, o_ref, acc):
        kk = pl.program_id(2)

        @pl.when(kk == 0)
        def _():
            acc[...] = jnp.zeros_like(acc)

        acc[...] += _dot_tn(a_ref[...], b_ref[...])

        @pl.when(kk == nk - 1)
        def _():
            o_ref[...] = acc[...].astype(out_dtype)

    return pl.pallas_call(
        body, name=name, grid=(m // tm, n // tn, nk),
        in_specs=[pl.BlockSpec((tk, tm), lambda i, j, kk: (kk + a_off, i)),
                  pl.BlockSpec((tk, tn), lambda i, j, kk: (kk + b_off, j))],
        out_specs=pl.BlockSpec((tm, tn), lambda i, j, kk: (i, j)),
        out_shape=jax.ShapeDtypeStruct((m, n), out_dtype),
        scratch_shapes=[pltpu.VMEM((tm, tn), F32)],
        compiler_params=_cp(("parallel", "parallel", "arbitrary")),
    )(a, b)


def _mm_tn_blocked(a, b, name, steps, carried=None):
    w = a.shape[1] // steps
    n = b.shape[1]

    def body(a_ref, b_ref, o_ref):
        o_ref[...] = _dot_tn(a_ref[...], b_ref[...]).astype(BF16)

    (out,), extra = _pcall(
        body, carried, name=name, grid=(steps,),
        in_specs=[pl.BlockSpec((S, w), lambda j: (0, j)), _full((S, n))],
        out_specs=[pl.BlockSpec((w, n), lambda j: (j, 0))],
        out_shape=[jax.ShapeDtypeStruct((a.shape[1], n), BF16)],
        scratch_shapes=[], operands=[a, b])
    return out, extra


def _prologue(cond, c_ctx, w_ada, b_cols, w_in_t, x, ctx, nw):
    rows_shape = jax.ShapeDtypeStruct((16, ADA_BLK), F32)
    big, g_cond, g_mod = _gather_comm_relayed([w_in_t]), _gather_comm([cond]), _gather_comm([rows_shape])

    def body(cond_ref, cctx_ref, wada_ref, b_ref, nw_ref, win_ref, x_ref, ctx_ref,
             g0_ref, cc_ref, mod_ref, gin_ref, h_ref, hl_ref, rows_ref, g1_ref, x_s, ctx_s, h_s, io_sems, *sems):
        s_big, s_cond, s_mod = sems[0:3], sems[3:6], sems[6:9]
        big.phases[0]([win_ref], [gin_ref], s_big)
        load_x = pltpu.make_async_copy(x_ref, x_s, io_sems.at[0])
        load_ctx = pltpu.make_async_copy(ctx_ref, ctx_s, io_sems.at[1])
        load_x.start()
        load_ctx.start()
        for phase in g_cond.phases:
            phase([cond_ref], [g0_ref], s_cond)
        cc_ref[...] = jnp.zeros_like(cc_ref)
        for j in range(N_DEV):
            cc_ref[j:j + 1, :] = g0_ref[j, 0:1, :]
        cc_ref[N_DEV:N_DEV + 1, :] = cctx_ref[...]
        cv = cc_ref[...]
        rows_ref[...] = _dot_f32(cv * _sigmoid(cv), wada_ref[...]) + b_ref[...]
        for phase in g_mod.phases:
            phase([rows_ref], [g1_ref], s_mod)
        big.phases[1]([win_ref], [gin_ref], s_big)
        x_pos, y_pos, c_pos = _my_pos()
        me = 4 * x_pos + 2 * y_pos + c_pos
        mine = jnp.concatenate([g1_ref[j, pl.ds(me, 1), :] for j in range(N_DEV)], axis=1)
        shared = jnp.concatenate([g1_ref[j, N_DEV:N_DEV + 1, :] for j in range(N_DEV)], axis=1)
        for k in range(6):
            mod_ref[k:k + 1, :] = mine[:, k * D:(k + 1) * D]
        mod_ref[6:7, :] = shared[:, 0:D]
        mod_ref[7:8, :] = shared[:, D:2 * D]
        load_ctx.wait()
        load_x.wait()
        h_s[pl.ds(0, L), :] = _rms_mod(ctx_s[...], nw_ref[...], mod_ref[6:7, :], mod_ref[7:8, :]).astype(BF16)

        def norm_tile(i, carry):
            r0 = pl.multiple_of(i * TM, TM)
            h_s[pl.ds(L + r0, TM), :] = _rms_mod(
                x_s[pl.ds(r0, TM), :], nw_ref[...], mod_ref[0:1, :], mod_ref[1:2, :]).astype(BF16)
            return carry

        lax.fori_loop(0, N_LAT_TILES, norm_tile, 0)
        stores = [pltpu.make_async_copy(h_s, h_ref, io_sems.at[2]),
                  pltpu.make_async_copy(h_s.at[pl.ds(L, S)], hl_ref, io_sems.at[3])]
        for cp in stores:
            cp.start()
        big.phases[2]([win_ref], [gin_ref], s_big)
        for cp in stores:
            cp.wait()

    return pl.pallas_call(
        body, name="prologue",
        in_specs=[_VMEM_WHOLE] * 5 + [_ANY] * 3, out_specs=[_VMEM_WHOLE] * 3 + [_ANY] * 3,
        out_shape=[g_cond.out_shapes[0], jax.ShapeDtypeStruct((16, D), F32), jax.ShapeDtypeStruct((8, D), F32),
                   big.out_shapes[0], jax.ShapeDtypeStruct((T, D), BF16), jax.ShapeDtypeStruct((S, D), BF16)],
        scratch_shapes=[pltpu.VMEM((16, ADA_BLK), F32), pltpu.VMEM((N_DEV, 16, ADA_BLK), F32),
                        pltpu.VMEM((S, D), F32), pltpu.VMEM((L, D), F32), pltpu.VMEM((T, D), BF16),
                        pltpu.SemaphoreType.DMA((4,))] + big.sems + g_cond.sems + g_mod.sems,
        compiler_params=_cp(),
    )(cond, c_ctx, w_ada, b_cols, nw, w_in_t, x, ctx)


def _ada_grads(cc, dm_cols, w_ada):
    def body(c_ref, dm_ref, w_ref, gw_ref, dsc_ref):
        cv = c_ref[...]
        sc = cv * _sigmoid(cv)
        dm = dm_ref[...]
        gw_ref[...] = _dot_f32(sc, dm, dot=_dot_tn)
        dsc_ref[...] = _dot_f32(dm, w_ref[...], dot=_dot_nt)

    return pl.pallas_call(
        body, name="ada_grads",
        in_specs=[_VMEM_WHOLE] * 3, out_specs=[_VMEM_WHOLE] * 2,
        out_shape=[jax.ShapeDtypeStruct((D, ADA_BLK), F32), jax.ShapeDtypeStruct((16, D), F32)],
        compiler_params=_cp(),
    )(cc, dm_cols, w_ada)


def _lat(i):
    return jnp.maximum(i - 1, 0)


def _rms_mod(xv, nw, sh, sc):
    rstd = lax.rsqrt(jnp.mean(xv * xv, axis=-1, keepdims=True) + EPS)
    return (xv * rstd * nw) * (1.0 + sc) + sh


def _rms_mod_bwd(xv, nw, sc, dh):
    rstd = lax.rsqrt(jnp.mean(xv * xv, axis=-1, keepdims=True) + EPS)
    xhat = xv * rstd
    dn = dh * (1.0 + sc)
    dxhat = dn * nw
    dx = rstd * (dxhat - xhat * jnp.mean(dxhat * xhat, axis=-1, keepdims=True))
    return (dx, jnp.sum(dh, axis=0, keepdims=True), jnp.sum(dh * (xhat * nw), axis=0, keepdims=True),
            jnp.sum(dn * xhat, axis=0, keepdims=True))


def _chunk_masks(reverse):
    row = lax.broadcasted_iota(jnp.int32, (TM, TM), 0)
    col = lax.broadcasted_iota(jnp.int32, (TM, TM), 1)
    same = (row // CHUNK) == (col // CHUNK)
    tri = same & ((col >= row) if reverse else (col <= row))
    return same, tri


def _chunk_order(i, reverse):
    if not reverse:
        return i
    return jnp.where(i < N_CTX_CHUNKS, N_CTX_CHUNKS - 1 - i, N_CHUNKS + N_CTX_CHUNKS - 1 - i)


def _decay_terms(z, lb, same01, tri01):
    f = lb + (1.0 - lb) * _sigmoid(z)
    g = jnp.log(f)
    g2 = jnp.concatenate(_split2(g), axis=1)
    b2 = _dot(tri01, g2)
    t2 = _dot(same01, g2)
    return f, 1.0 - f, b2[:, :HG_DIM] + b2[:, HG_DIM:], t2[:, :HG_DIM] + t2[:, HG_DIM:]


def _chunk_outer(a, b):
    n = TM // CHUNK
    return jnp.einsum('ncv,nck->nvk', a.reshape(n, CHUNK, HG_DIM), b.reshape(n, CHUNK, HG_DIM),
                      preferred_element_type=F32)


def _hgrn_fwd(p_a, lbl, carried=None):
    cpt = TM // CHUNK

    def body(p_ref, lbl_ref, o_ref, st_ref, qd_s, kd_s, u_s, v_s, ebt_s):
        masks = [_chunk_masks(d == 1) for d in (0, 1)]
        same01 = jnp.where(masks[0][0], 1.0, 0.0).astype(BF16)
        tri = [m[1] for m in masks]
        tri01 = [jnp.where(t, 1.0, 0.0).astype(BF16) for t in tri]
        lb = [_sigmoid(lbl_ref[d][0:1, :] - lbl_ref[d][1:2, :]) for d in (0, 1)]

        def prep(r, carry):
            r0 = pl.multiple_of(r * TM, TM)
            vb = p_ref[pl.ds(r0, TM), 2 * HG_DIM:3 * HG_DIM].astype(BF16)
            v_s[pl.ds(r0, TM), :] = vb
            for d in (0, 1):
                z = p_ref[pl.ds(r0, TM), d * HG_DIM:(d + 1) * HG_DIM]
                _, k, b, bt = _decay_terms(z, lb[d], same01, tri01[d])
                u_s[d, pl.ds(r * cpt, cpt)] = _chunk_outer(vb, (k * jnp.exp(bt - b)).astype(BF16))
                ebt_s[d, pl.ds(r0, TM), :] = jnp.exp(bt)

                @pl.when(r >= 1)
                def _():
                    rl = pl.multiple_of(r0 - L, TM)
                    qr = p_ref[pl.ds(r0, TM), 3 * HG_DIM:4 * HG_DIM]
                    q = qr * _sigmoid(qr) * HG_DIM ** -0.5
                    qd_s[d, pl.ds(rl, TM), :] = (q * jnp.exp(b)).astype(BF16)
                    kd_s[d, pl.ds(rl, TM), :] = (k * jnp.exp(-b)).astype(BF16)

            return carry

        lax.fori_loop(0, N_TILES, prep, 0)

        def scan(i, sts):
            new = []
            for d in (0, 1):
                nn = _chunk_order(i, d == 1)
                c0 = pl.multiple_of(nn * CHUNK, CHUNK)
                st_ref[d, nn] = sts[d].astype(BF16)
                new.append(sts[d] * ebt_s[d, pl.ds(c0, 1), :] + u_s[d, nn])
            return tuple(new)

        zero = jnp.zeros((HG_DIM, HG_DIM), F32)
        lax.fori_loop(0, N_CHUNKS, scan, (zero, zero))

        def outp(r, carry):
            r0 = pl.multiple_of(r * TM, TM)
            vb = v_s[pl.ds(r0 + L, TM), :]
            o = jnp.zeros((TM, HG_DIM), F32)
            for d in (0, 1):
                qd = qd_s[d, pl.ds(r0, TM), :]
                a = jnp.where(tri[d], _dot_nt(qd, kd_s[d, pl.ds(r0, TM), :]), 0.0)
                stb = st_ref[d, pl.ds(N_CTX_CHUNKS + r * cpt, cpt)]
                inter = jnp.einsum('nck,nvk->ncv', qd.reshape(cpt, CHUNK, HG_DIM), stb,
                                   preferred_element_type=F32)
                o = o + _dot(a.astype(BF16), vb) + inter.reshape(TM, HG_DIM)
            o_ref[pl.ds(r0, TM), :] = o
            return carry

        lax.fori_loop(0, N_LAT_TILES, outp, 0, unroll=2)

    return _pcall(
        body, carried, name="hgrn_fwd", grid=(HG_HEADS,),
        in_specs=[pl.BlockSpec((T, 4 * HG_DIM), lambda h: (0, h)),
                  pl.BlockSpec((2, 2, HG_DIM), lambda h: (0, 0, h))],
        out_specs=[pl.BlockSpec((S, HG_DIM), lambda h: (0, h)),
                   pl.BlockSpec((2, None, N_CHUNKS, HG_DIM, HG_DIM), lambda h: (0, h, 0, 0, 0))],
        out_shape=[jax.ShapeDtypeStruct((S, HGW), F32),
                   jax.ShapeDtypeStruct((2, HG_HEADS, N_CHUNKS, HG_DIM, HG_DIM), BF16)],
        scratch_shapes=[pltpu.VMEM((2, S, HG_DIM), BF16), pltpu.VMEM((2, S, HG_DIM), BF16),
                        pltpu.VMEM((2, N_CHUNKS, HG_DIM, HG_DIM), F32), pltpu.VMEM((T, HG_DIM), BF16),
                        pltpu.VMEM((2, T, HG_DIM), F32)],
        operands=[p_a, lbl])


def _hgrn_bwd(p_a, lbl, d_o, st, carried=None):
    cpt = TM // CHUNK

    def rows(r):
        return r * TM if isinstance(r, int) else pl.multiple_of(r * TM, TM)

    def body(p_ref, lbl_ref, do_ref, st_ref, dp_ref, dlb_ref, b_s, bt_s, dbt_s, qd_s, dst_s, w_s):
        masks = [_chunk_masks(d == 1) for d in (0, 1)]
        same01 = jnp.where(masks[0][0], 1.0, 0.0).astype(BF16)
        tri = [m[1] for m in masks]
        tri01 = [jnp.where(t, 1.0, 0.0).astype(BF16) for t in tri]
        later01 = [tri01[1], tri01[0]]
        lb = [_sigmoid(lbl_ref[d][0:1, :] - lbl_ref[d][1:2, :]) for d in (0, 1)]

        def prep_tile(r, latent):
            r0 = rows(r)
            for d in (0, 1):
                z = p_ref[pl.ds(r0, TM), d * HG_DIM:(d + 1) * HG_DIM]
                _, _, b, bt = _decay_terms(z, lb[d], same01, tri01[d])
                b_s[d, pl.ds(r0, TM), :] = b
                bt_s[d, pl.ds(r0, TM), :] = bt
                if latent:
                    rl = pl.multiple_of(r0 - L, TM)
                    qr = p_ref[pl.ds(r0, TM), 3 * HG_DIM:4 * HG_DIM]
                    qd = (qr * _sigmoid(qr) * HG_DIM ** -0.5 * jnp.exp(b)).astype(BF16)
                    qd_s[d, pl.ds(rl, TM), :] = qd
                    w_s[d, pl.ds(r * cpt, cpt)] = _chunk_outer(
                        do_ref[pl.ds(rl, TM), :].astype(BF16), qd).astype(BF16)

        prep_tile(0, False)
        w_s[:, pl.ds(0, N_CTX_CHUNKS)] = jnp.zeros((2, N_CTX_CHUNKS, HG_DIM, HG_DIM), BF16)

        def prep(r, carry):
            prep_tile(r, True)
            return carry

        lax.fori_loop(1, N_TILES, prep, 0, unroll=2)

        def rscan(j, dsts):
            i = N_CHUNKS - 1 - j
            new = []
            for d in (0, 1):
                nn = _chunk_order(i, d == 1)
                c0 = pl.multiple_of(nn * CHUNK, CHUNK)
                dst_s[d, nn] = dsts[d].astype(BF16)
                after = st_ref[d, _chunk_order(jnp.minimum(i + 1, N_CHUNKS - 1), d == 1)].astype(F32)
                dbt_s[d, pl.ds(c0, CHUNK), :] = jnp.broadcast_to(
                    jnp.sum(after * dsts[d], axis=0, keepdims=True), (CHUNK, HG_DIM))
                new.append(dsts[d] * jnp.exp(bt_s[d, pl.ds(c0, 1), :]) + w_s[d, nn].astype(F32))
            return tuple(new)

        zero = jnp.zeros((HG_DIM, HG_DIM), F32)
        lax.fori_loop(0, N_CHUNKS, rscan, (zero, zero))

        def grad_tile(r, latent):
            r0 = rows(r)
            vb = p_ref[pl.ds(r0, TM), 2 * HG_DIM:3 * HG_DIM].astype(BF16)
            dv = jnp.zeros((TM, HG_DIM), F32)
            dq = jnp.zeros((TM, HG_DIM), F32)
            dlbs = []
            if latent:
                rl = pl.multiple_of(r0 - L, TM)
                qr = p_ref[pl.ds(r0, TM), 3 * HG_DIM:4 * HG_DIM]
                sq = _sigmoid(qr)
                do = do_ref[pl.ds(rl, TM), :].astype(BF16)
                da_full = _dot_nt(do, vb)
            for d in (0, 1):
                z = p_ref[pl.ds(r0, TM), d * HG_DIM:(d + 1) * HG_DIM]
                sz = _sigmoid(z)
                f = lb[d] + (1.0 - lb[d]) * sz
                k = 1.0 - f
                b = b_s[d, pl.ds(r0, TM), :]
                e2 = jnp.exp(bt_s[d, pl.ds(r0, TM), :] - b)
                dstb = dst_s[d, pl.ds(r * cpt, cpt)]
                kd2 = k * e2
                dkd2 = jnp.einsum('ncv,nvk->nck', vb.reshape(cpt, CHUNK, HG_DIM), dstb,
                                  preferred_element_type=F32).reshape(TM, HG_DIM)
                dv = dv + jnp.einsum('nck,nvk->ncv', kd2.astype(BF16).reshape(cpt, CHUNK, HG_DIM), dstb,
                                     preferred_element_type=F32).reshape(TM, HG_DIM)
                dk = dkd2 * e2
                db = -(kd2 * dkd2)
                if latent:
                    eb = jnp.exp(b)
                    enb = jnp.exp(-b)
                    qdf = qr * sq * HG_DIM ** -0.5 * eb
                    kdf = k * enb
                    qd = qd_s[d, pl.ds(rl, TM), :]
                    kd = kdf.astype(BF16)
                    a = jnp.where(tri[d], _dot_nt(qd, kd), 0.0).astype(BF16)
                    da = jnp.where(tri[d], da_full, 0.0).astype(BF16)
                    stb = st_ref[d, pl.ds(r * cpt, cpt)]
                    dqd = _dot(da, kd) + jnp.einsum(
                        'ncv,nvk->nck', do.reshape(cpt, CHUNK, HG_DIM), stb,
                        preferred_element_type=F32).reshape(TM, HG_DIM)
                    dkd = _dot_tn(da, qd)
                    dv = dv + _dot_tn(a, do)
                    dk = dk + dkd * enb
                    db = db + qdf * dqd - kdf * dkd
                    dq = dq + dqd * eb
                dg = _dot_lhs01(later01[d], db) + dbt_s[d, pl.ds(r0, TM), :]
                df = dg / f - dk
                dp_ref[pl.ds(r0, TM), d * HG_DIM:(d + 1) * HG_DIM] = (
                    df * (1.0 - lb[d]) * sz * (1.0 - sz)).astype(BF16)
                dlbs.append(jnp.sum(df * (1.0 - sz), axis=0, keepdims=True))
            dp_ref[pl.ds(r0, TM), 2 * HG_DIM:3 * HG_DIM] = dv.astype(BF16)
            if latent:
                dq = dq * (HG_DIM ** -0.5) * (sq * (1.0 + qr * (1.0 - sq)))
            dp_ref[pl.ds(r0, TM), 3 * HG_DIM:4 * HG_DIM] = dq.astype(BF16)
            return dlbs

        dlb_ctx = grad_tile(0, False)

        def grads(r, acc):
            t = grad_tile(r, True)
            return (acc[0] + t[0], acc[1] + t[1])

        dlb = lax.fori_loop(1, N_TILES, grads, (dlb_ctx[0], dlb_ctx[1]))
        dlb_ref[0:1, :] = dlb[0]
        dlb_ref[1:2, :] = dlb[1]

    return _pcall(
        body, carried, name="hgrn_bwd", grid=(HG_HEADS,),
        in_specs=[pl.BlockSpec((T, 4 * HG_DIM), lambda h: (0, h)),
                  pl.BlockSpec((2, 2, HG_DIM), lambda h: (0, 0, h)),
                  pl.BlockSpec((S, HG_DIM), lambda h: (0, h)),
                  pl.BlockSpec((2, None, N_CHUNKS, HG_DIM, HG_DIM), lambda h: (0, h, 0, 0, 0))],
        out_specs=[pl.BlockSpec((T, 4 * HG_DIM), lambda h: (0, h)),
                   pl.BlockSpec((2, HG_DIM), lambda h: (0, h))],
        out_shape=[jax.ShapeDtypeStruct((T, WA), BF16), jax.ShapeDtypeStruct((2, HGW), F32)],
        scratch_shapes=[pltpu.VMEM((2, T, HG_DIM), F32), pltpu.VMEM((2, T, HG_DIM), F32),
                        pltpu.VMEM((2, T, HG_DIM), F32), pltpu.VMEM((2, S, HG_DIM), BF16),
                        pltpu.VMEM((2, N_CHUNKS, HG_DIM, HG_DIM), BF16),
                        pltpu.VMEM((2, N_CHUNKS, HG_DIM, HG_DIM), BF16)],
        operands=[p_a, lbl, d_o, st])


def _rope_tables():
    t = np.arange(S)
    inv = ROPE_THETA ** (-np.arange(0, 32, 2, dtype=np.float64) / 32)
    lane = np.arange(64)
    pos = np.where(lane[None, :] < 32, (t // GRID_W)[:, None], (t % GRID_W)[:, None]).astype(np.float64)
    ang = pos * inv[(lane % 32) % 16][None, :]
    sign = np.where((lane % 32) < 16, -1.0, 1.0)[None, :]
    cos = np.tile(np.cos(ang), (1, 2)).astype(np.float32)
    sin = np.tile(np.sin(ang) * sign, (1, 2)).astype(np.float32)
    return jnp.asarray(cos), jnp.asarray(sin)


def _rope_partner(v):
    lane = lax.broadcasted_iota(jnp.int32, (1, 128), 1)
    first = (lane % 32) < 16
    slabs = []
    for j in range(v.shape[1] // 128):
        s = v[:, 128 * j:128 * (j + 1)]
        slabs.append(jnp.where(first, pltpu.roll(s, 112, 1), pltpu.roll(s, 16, 1)))
    return slabs[0] if len(slabs) == 1 else jnp.concatenate(slabs, axis=1)


def _group_ones(width, group):
    r = lax.broadcasted_iota(jnp.int32, (width, width), 0)
    c = lax.broadcasted_iota(jnp.int32, (width, width), 1)
    return jnp.where((r // group) == (c // group), 1.0, 0.0).astype(BF16)


def _group_mean(v, ones01, group):
    hi = v.astype(BF16)
    lo = (v - hi.astype(F32)).astype(BF16)
    return (_dot(hi, ones01) + _dot(lo, ones01)) * (1.0 / group)


def _rep_matrix():
    r = lax.broadcasted_iota(jnp.int32, (KVW, ATW), 0)
    c = lax.broadcasted_iota(jnp.int32, (KVW, ATW), 1)
    return jnp.where(r == HEAD_DIM * (c // 256) + c % HEAD_DIM, 1.0, 0.0).astype(BF16)


def _tile_lanes(v, reps):
    return jnp.concatenate([v] * reps, axis=1)


def _prep_fwd(p_b, o, cos, sin, hnw, qnw, knw):
    def body(p_ref, o_ref, cos_ref, sin_ref, hnw_ref, qnw_ref, knw_ref, y_ref, q_ref, k_ref, v_ref):
        i = pl.program_id(0)
        rep = _rep_matrix()
        ones_k = _group_ones(KVW, HEAD_DIM)
        kr = p_ref[:, 1024:1152]
        krstd = lax.rsqrt(_group_mean(kr * kr, ones_k, HEAD_DIM) + EPS)
        kn = kr * krstd * knw_ref[...]
        v_ref[...] = _dot(p_ref[:, 1152:1280].astype(BF16), rep).astype(BF16)

        @pl.when(i == 0)
        def _():
            k_ref[...] = _dot(kn.astype(BF16), rep).astype(BF16)

        @pl.when(i > 0)
        def _():
            cs, sn = cos_ref[...], sin_ref[...]
            kro = kn * cs + _rope_partner(kn) * sn
            k_ref[...] = _dot(kro.astype(BF16), rep).astype(BF16)
            qr = p_ref[:, 512:1024]
            qrstd = lax.rsqrt(_group_mean(qr * qr, _group_ones(ATW, HEAD_DIM), HEAD_DIM) + EPS)
            qn = qr * qrstd * qnw_ref[...]
            qro = qn * _tile_lanes(cs, 4) + _rope_partner(qn) * _tile_lanes(sn, 4)
            q_ref[...] = (qro * HEAD_DIM ** -0.5).astype(BF16)
            ys = []
            for h in range(HG_HEADS):
                oh = o_ref[:, HG_DIM * h:HG_DIM * (h + 1)]
                gh = p_ref[:, HG_DIM * h:HG_DIM * (h + 1)]
                rstd = lax.rsqrt(jnp.mean(oh * oh, axis=-1, keepdims=True) + EPS)
                ys.append(oh * rstd * hnw_ref[...] * (gh * _sigmoid(gh)))
            y_ref[...] = jnp.concatenate(ys, axis=1).astype(BF16)

    return pl.pallas_call(
        body, name="prep_fwd", grid=(N_TILES,),
        in_specs=[pl.BlockSpec((TM, WB), lambda i: (i, 0)),
                  pl.BlockSpec((TM, HGW), lambda i: (_lat(i), 0)),
                  pl.BlockSpec((TM, 128), lambda i: (_lat(i), 0)),
                  pl.BlockSpec((TM, 128), lambda i: (_lat(i), 0)),
                  _full((1, HG_DIM)), _full((1, ATW)), _full((1, KVW))],
        out_specs=[pl.BlockSpec((TM, HGW), lambda i: (_lat(i), 0)),
                   pl.BlockSpec((TM, ATW), lambda i: (_lat(i), 0)),
                   pl.BlockSpec((TM, ATW), lambda i: (i, 0)),
                   pl.BlockSpec((TM, ATW), lambda i: (i, 0))],
        out_shape=[jax.ShapeDtypeStruct((S, HGW), BF16), jax.ShapeDtypeStruct((S, ATW), BF16),
                   jax.ShapeDtypeStruct((T, ATW), BF16), jax.ShapeDtypeStruct((T, ATW), BF16)],
        compiler_params=_cp(("arbitrary",)),
    )(p_b, o, cos, sin, hnw, qnw, knw)


def _prep_bwd(p_b, o, cos, sin, hnw, qnw, knw, dy_hg, dq, dk_rep, dv_rep, carried=None):
    def body(p_ref, o_ref, cos_ref, sin_ref, hnw_ref, qnw_ref, knw_ref, dy_ref, dq_ref, dk_ref, dv_ref,
             dp_ref, do_ref, acc_ref):
        i = pl.program_id(0)

        @pl.when(i == 0)
        def _():
            acc_ref[...] = jnp.zeros_like(acc_ref)

        rep = _rep_matrix()
        ones_k = _group_ones(KVW, HEAD_DIM)

        def fold(v):
            hi = v.astype(BF16)
            lo = (v - hi.astype(F32)).astype(BF16)
            return _dot_nt(hi, rep) + _dot_nt(lo, rep)

        kr = p_ref[:, 1024:1152]
        krstd = lax.rsqrt(_group_mean(kr * kr, ones_k, HEAD_DIM) + EPS)
        khat = kr * krstd
        kw = knw_ref[...]
        dkro = fold(dk_ref[...])
        dv = fold(dv_ref[...])

        def k_back(dkn):
            dkhat = dkn * kw
            dkr = krstd * (dkhat - khat * _group_mean(dkhat * khat, ones_k, HEAD_DIM))
            acc_ref[2:3, 0:KVW] += jnp.sum(dkn * khat, axis=0, keepdims=True)
            dp_ref[:, 1024:1152] = dkr.astype(BF16)
            dp_ref[:, 1152:1280] = dv.astype(BF16)

        @pl.when(i == 0)
        def _():
            k_back(dkro)
            dp_ref[:, 0:1024] = jnp.zeros((TM, 1024), BF16)

        @pl.when(i > 0)
        def _():
            cs, sn = cos_ref[...], sin_ref[...]
            k_back(dkro * cs + _rope_partner(dkro * sn))
            ones_q = _group_ones(ATW, HEAD_DIM)
            qr = p_ref[:, 512:1024]
            qrstd = lax.rsqrt(_group_mean(qr * qr, ones_q, HEAD_DIM) + EPS)
            qhat = qr * qrstd
            dqro = dq_ref[...] * HEAD_DIM ** -0.5
            dqn = dqro * _tile_lanes(cs, 4) + _rope_partner(dqro * _tile_lanes(sn, 4))
            dqhat = dqn * qnw_ref[...]
            dqr = qrstd * (dqhat - qhat * _group_mean(dqhat * qhat, ones_q, HEAD_DIM))
            acc_ref[1:2, :] += jnp.sum(dqn * qhat, axis=0, keepdims=True)
            dp_ref[:, 512:1024] = dqr.astype(BF16)
            dws = jnp.zeros((1, HG_DIM), F32)
            for h in range(HG_HEADS):
                sl = slice(HG_DIM * h, HG_DIM * (h + 1))
                oh, gh, dy = o_ref[:, sl], p_ref[:, sl], dy_ref[:, sl]
                rstd = lax.rsqrt(jnp.mean(oh * oh, axis=-1, keepdims=True) + EPS)
                ohat = oh * rstd
                sg = _sigmoid(gh)
                dp_ref[:, sl] = (dy * (ohat * hnw_ref[...]) * (sg * (1.0 + gh * (1.0 - sg)))).astype(BF16)
                dn = dy * (gh * sg)
                dws = dws + jnp.sum(dn * ohat, axis=0, keepdims=True)
                dohat = dn * hnw_ref[...]
                do_ref[:, sl] = rstd * (dohat - ohat * jnp.mean(dohat * ohat, axis=-1, keepdims=True))
            acc_ref[0:1, 0:HG_DIM] += dws

    return _pcall(
        body, carried, name="prep_bwd", grid=(N_TILES,),
        in_specs=[pl.BlockSpec((TM, WB), lambda i: (i, 0)),
                  pl.BlockSpec((TM, HGW), lambda i: (_lat(i), 0)),
                  pl.BlockSpec((TM, 128), lambda i: (_lat(i), 0)),
                  pl.BlockSpec((TM, 128), lambda i: (_lat(i), 0)),
                  _full((1, HG_DIM)), _full((1, ATW)), _full((1, KVW)),
                  pl.BlockSpec((TM, HGW), lambda i: (_lat(i), 0)),
                  pl.BlockSpec((TM, ATW), lambda i: (_lat(i), 0)),
                  pl.BlockSpec((TM, ATW), lambda i: (i, 0)),
                  pl.BlockSpec((TM, ATW), lambda i: (i, 0))],
        out_specs=[pl.BlockSpec((TM, WB), lambda i: (i, 0)),
                   pl.BlockSpec((TM, HGW), lambda i: (_lat(i), 0)),
                   _full((8, ATW))],
        out_shape=[jax.ShapeDtypeStruct((T, WB), BF16), jax.ShapeDtypeStruct((S, HGW), F32),
                   jax.ShapeDtypeStruct((8, ATW), F32)],
        scratch_shapes=[], operands=[p_b, o, cos, sin, hnw, qnw, knw, dy_hg, dq, dk_rep, dv_rep])


NEG = -1e30
_CTX_BLOCKS = L // BLOCK


def _attn_window_specs():
    prev = pl.BlockSpec((BLOCK, ATW), lambda i: (jnp.maximum(i - 1, 0) + _CTX_BLOCKS, 0))
    own = pl.BlockSpec((BLOCK, ATW), lambda i: (i + _CTX_BLOCKS, 0))
    nxt = pl.BlockSpec((BLOCK, ATW), lambda i: (jnp.minimum(i + 1, N_BLOCKS - 1) + _CTX_BLOCKS, 0))
    return [prev, own, nxt, _full((L, ATW))]


def _attn_valid(i, heads, context):
    n_keys = 3 * BLOCK + (L if context else 0)
    qi = lax.broadcasted_iota(jnp.int32, (heads * BLOCK, n_keys), 0) % BLOCK
    kj = lax.broadcasted_iota(jnp.int32, (heads * BLOCK, n_keys), 1)
    window = ((jnp.abs(kj - BLOCK - qi) <= BLOCK) & ((kj >= BLOCK) | (i > 0))
              & ((kj < 2 * BLOCK) | (i < N_BLOCKS - 1)))
    return window | (kj >= 3 * BLOCK)


def _stack_heads(qg):
    lane = lax.broadcasted_iota(jnp.int32, (1, 256), 1) // HEAD_DIM
    return jnp.concatenate([jnp.where(lane == g, qg, jnp.zeros_like(qg)) for g in range(4)], axis=0)


def _unstack_heads(v4):
    lane = lax.broadcasted_iota(jnp.int32, (1, 256), 1) // HEAD_DIM
    out = jnp.where(lane == 0, v4[0:BLOCK], 0.0)
    for g in range(1, 4):
        out = out + jnp.where(lane == g, v4[g * BLOCK:(g + 1) * BLOCK], 0.0)
    return out


def _sink_rows(sink_ref, hk):
    return jnp.concatenate(
        [jnp.broadcast_to(sink_ref[0:1, 4 * hk + g:4 * hk + g + 1], (BLOCK, 1)) for g in range(4)], axis=0)


def _attn_fwd(q, k_rep, v_rep, sinks, carried=None):
    def body(q_ref, kp, ko, kn, kc, vp, vo, vn, vc, sink_ref, y_ref, lse_ref):
        i = pl.program_id(0)
        valid = _attn_valid(i, 1, True)
        lane8 = lax.broadcasted_iota(jnp.int32, (1, ATT_HEADS), 1)
        head_of_lane = lax.broadcasted_iota(jnp.int32, (1, 256), 1) // HEAD_DIM
        lse_out = jnp.zeros((BLOCK, ATT_HEADS), F32)
        for hk in range(KV_HEADS):
            sl = slice(256 * hk, 256 * (hk + 1))
            qg = q_ref[:, sl]
            keys = jnp.concatenate([kp[:, sl], ko[:, sl], kn[:, sl], kc[:, sl]], axis=0)
            vals = jnp.concatenate([vp[:, sl], vo[:, sl], vn[:, sl], vc[:, sl]], axis=0)
            yg = jnp.zeros((BLOCK, 256), F32)
            for g in range(4):
                q1 = jnp.where(head_of_lane == g, qg, jnp.zeros_like(qg))
                s = jnp.where(valid, _dot_nt(q1, keys), NEG)
                sink = sink_ref[0:1, 4 * hk + g:4 * hk + g + 1]
                m = jnp.maximum(jnp.max(s, axis=1, keepdims=True), sink)
                p = jnp.exp(s - m)
                den = jnp.sum(p, axis=1, keepdims=True) + jnp.exp(sink - m)
                o1 = _dot(p.astype(BF16), vals) * (1.0 / den)
                yg = yg + jnp.where(head_of_lane == g, o1, 0.0)
                lse_out = lse_out + jnp.where(lane8 == 4 * hk + g, m + jnp.log(den), 0.0)
            y_ref[:, sl] = yg.astype(BF16)
        lse_ref[...] = lse_out

    return _pcall(
        body, carried, name="attn_fwd", grid=(N_BLOCKS,),
        in_specs=[pl.BlockSpec((BLOCK, ATW), lambda i: (i, 0))] + _attn_window_specs()
        + _attn_window_specs() + [_full((1, ATT_HEADS))],
        out_specs=[pl.BlockSpec((BLOCK, ATW), lambda i: (i, 0)),
                   pl.BlockSpec((BLOCK, ATT_HEADS), lambda i: (i, 0))],
        out_shape=[jax.ShapeDtypeStruct((S, ATW), BF16), jax.ShapeDtypeStruct((S, ATT_HEADS), F32)],
        scratch_shapes=[],
        operands=[q, k_rep, k_rep, k_rep, k_rep, v_rep, v_rep, v_rep, v_rep, sinks])


def _attn_bwd(q, k_rep, v_rep, sinks, y_at, lse, dy, carried=None):
    def body(q_ref, kp, ko, kn, kc, vp, vo, vn, vc, sink_ref, y_ref, lse_ref, dy_ref,
             dq_ref, dk_ref, dv_ref, dsink_ref, dk_acc, dv_acc):
        i = pl.program_id(0)

        @pl.when(i == 0)
        def _():
            dk_acc[...] = jnp.zeros_like(dk_acc)
            dv_acc[...] = jnp.zeros_like(dv_acc)
            dk_ref[pl.ds(0, L), :] = jnp.zeros((L, ATW), F32)
            dv_ref[pl.ds(0, L), :] = jnp.zeros((L, ATW), F32)
            dsink_ref[...] = jnp.zeros_like(dsink_ref)

        valid = _attn_valid(i, 4, False)
        lane8 = lax.broadcasted_iota(jnp.int32, (1, ATT_HEADS), 1)
        w0 = pl.multiple_of(i * BLOCK, BLOCK)
        dsink = jnp.zeros((1, ATT_HEADS), F32)
        for hk in range(KV_HEADS):
            sl = slice(256 * hk, 256 * (hk + 1))
            q4 = _stack_heads(q_ref[:, sl])
            do4f = _stack_heads(dy_ref[:, sl])
            o4 = _stack_heads(y_ref[:, sl]).astype(F32)
            do4 = do4f.astype(BF16)
            kl = jnp.concatenate([kp[:, sl], ko[:, sl], kn[:, sl]], axis=0)
            vl = jnp.concatenate([vp[:, sl], vo[:, sl], vn[:, sl]], axis=0)
            lse4 = jnp.concatenate(
                [jnp.sum(jnp.where(lane8 == 4 * hk + g, lse_ref[...], 0.0), axis=1, keepdims=True)
                 for g in range(4)], axis=0)
            p_loc = jnp.where(valid, jnp.exp(_dot_nt(q4, kl) - lse4), 0.0)
            p_ctx = jnp.exp(_dot_nt(q4, kc[:, sl]) - lse4)
            delta = jnp.sum(do4f * o4, axis=1, keepdims=True)
            ds_loc = (p_loc * (_dot_nt(do4, vl) - delta)).astype(BF16)
            ds_ctx = (p_ctx * (_dot_nt(do4, vc[:, sl]) - delta)).astype(BF16)
            dq_ref[:, sl] = _unstack_heads(_dot(ds_loc, kl) + _dot(ds_ctx, kc[:, sl]))
            dk_acc[pl.ds(w0, 3 * BLOCK), sl] += _dot_tn(ds_loc, q4)
            dv_acc[pl.ds(w0, 3 * BLOCK), sl] += _dot_tn(p_loc.astype(BF16), do4)
            dk_ref[pl.ds(0, L), sl] += _dot_tn(ds_ctx, q4)
            dv_ref[pl.ds(0, L), sl] += _dot_tn(p_ctx.astype(BF16), do4)
            p_sink = jnp.exp(_sink_rows(sink_ref, hk) - lse4)
            for g in range(4):
                rows = slice(g * BLOCK, (g + 1) * BLOCK)
                dsink = dsink + jnp.where(lane8 == 4 * hk + g,
                                          -jnp.sum(p_sink[rows] * delta[rows], axis=0, keepdims=True), 0.0)
        dsink_ref[...] += dsink

        @pl.when(i == N_BLOCKS - 1)
        def _():
            dk_ref[pl.ds(L, S), :] = dk_acc[pl.ds(BLOCK, S), :]
            dv_ref[pl.ds(L, S), :] = dv_acc[pl.ds(BLOCK, S), :]

    row_q = pl.BlockSpec((BLOCK, ATW), lambda i: (i, 0))
    return _pcall(
        body, carried, name="attn_bwd", grid=(N_BLOCKS,),
        in_specs=[row_q] + _attn_window_specs() + _attn_window_specs()
        + [_full((1, ATT_HEADS)), row_q, pl.BlockSpec((BLOCK, ATT_HEADS), lambda i: (i, 0)), row_q],
        out_specs=[row_q, _full((T, ATW)), _full((T, ATW)), _full((1, ATT_HEADS))],
        out_shape=[jax.ShapeDtypeStruct((S, ATW), F32), jax.ShapeDtypeStruct((T, ATW), F32),
                   jax.ShapeDtypeStruct((T, ATW), F32), jax.ShapeDtypeStruct((1, ATT_HEADS), F32)],
        scratch_shapes=[pltpu.VMEM((S + 2 * BLOCK, ATW), F32), pltpu.VMEM((S + 2 * BLOCK, ATW), F32)],
        operands=[q, k_rep, k_rep, k_rep, k_rep, v_rep, v_rep, v_rep, v_rep, sinks, y_at, lse, dy])


def _merge_fwd(y_hg, y_at, p_c, x, w_bh, w_ba, w_out, g1, nfw, sh2, sc2, carried=None):
    def body(yh_ref, ya_ref, g_ref, x_ref, wbh_ref, wba_ref, wo_ref, g1_ref, nfw_ref, sh_ref, sc_ref,
             mx_ref, r_ref, x1_ref, h2_ref):
        a = _dot_nt(yh_ref[...], wbh_ref[...])
        b = _dot_nt(ya_ref[...], wba_ref[...])
        mixed = (_sigmoid(g_ref[:, :D]) * a + _sigmoid(g_ref[:, D:]) * b).astype(BF16)
        r = _dot(mixed, wo_ref[...])
        x1 = x_ref[...] + g1_ref[...] * r
        mx_ref[...] = mixed
        r_ref[...] = r
        x1_ref[...] = x1
        h2_ref[...] = _rms_mod(x1, nfw_ref[...], sh_ref[...], sc_ref[...]).astype(BF16)

    row = lambda w: pl.BlockSpec((TM, w), lambda i: (i, 0))
    vec = _full((1, D))
    return _pcall(
        body, carried, name="merge_fwd", grid=(N_LAT_TILES,),
        in_specs=[row(HGW), row(ATW), row(WC), row(D), _VMEM_WHOLE, _VMEM_WHOLE, _VMEM_WHOLE,
                  vec, vec, vec, vec],
        out_specs=[row(D)] * 4,
        out_shape=[jax.ShapeDtypeStruct((S, D), dt) for dt in (BF16, F32, F32, BF16)],
        scratch_shapes=[], operands=[y_hg, y_at, p_c, x, w_bh, w_ba, w_out, g1, nfw, sh2, sc2])


def _merge_bwd(dx1, r, y_hg, y_at, p_c, w_bh, w_ba, w_out, g1, carried=None):
    def body(dx_ref, r_ref, yh_ref, ya_ref, g_ref, wbh_ref, wba_ref, wo_ref, g1_ref,
             dr_ref, da_ref, db_ref, dg_ref, dyh_ref, dya_ref, acc_ref):
        @pl.when(pl.program_id(0) == 0)
        def _():
            acc_ref[...] = jnp.zeros_like(acc_ref)

        dx1v = dx_ref[...]
        acc_ref[0:1, :] += jnp.sum(dx1v * r_ref[...], axis=0, keepdims=True)
        dr = (g1_ref[...] * dx1v).astype(BF16)
        dr_ref[...] = dr
        dmix = _dot_nt(dr, wo_ref[...])
        sh, sa = _sigmoid(g_ref[:, :D]), _sigmoid(g_ref[:, D:])
        da = (dmix * sh).astype(BF16)
        db = (dmix * sa).astype(BF16)
        da_ref[...] = da
        db_ref[...] = db
        dg_ref[:, :D] = (dmix * _dot_nt(yh_ref[...], wbh_ref[...]) * sh * (1.0 - sh)).astype(BF16)
        dg_ref[:, D:] = (dmix * _dot_nt(ya_ref[...], wba_ref[...]) * sa * (1.0 - sa)).astype(BF16)
        dyh_ref[...] = _dot(da, wbh_ref[...])
        dya_ref[...] = _dot(db, wba_ref[...])

    row = lambda w: pl.BlockSpec((TM, w), lambda i: (i, 0))
    return _pcall(
        body, carried, name="merge_bwd", grid=(N_LAT_TILES,),
        in_specs=[row(D), row(D), row(HGW), row(ATW), row(WC), _VMEM_WHOLE, _VMEM_WHOLE, _VMEM_WHOLE,
                  _full((1, D))],
        out_specs=[row(D), row(D), row(D), row(WC), row(HGW), row(ATW), _full((8, D))],
        out_shape=[jax.ShapeDtypeStruct((S, D), BF16), jax.ShapeDtypeStruct((S, D), BF16),
                   jax.ShapeDtypeStruct((S, D), BF16), jax.ShapeDtypeStruct((S, WC), BF16),
                   jax.ShapeDtypeStruct((S, HGW), F32), jax.ShapeDtypeStruct((S, ATW), F32),
                   jax.ShapeDtypeStruct((8, D), F32)],
        scratch_shapes=[], operands=[dx1, r, y_hg, y_at, p_c, w_bh, w_ba, w_out, g1])


def _ffn_fused(x1, h2, tgt, w_gate, w_up, w_down, g2, nfw, sc2):
    def body(x1_ref, h2_ref, t_ref, wg_ref, wu_ref, wd_ref, g2_ref, nfw_ref, sc_ref,
             act_ref, dgt_ref, dup_ref, df_ref, dx_ref, acc_ref, gs, us):
        @pl.when(pl.program_id(0) == 0)
        def _():
            acc_ref[...] = jnp.zeros_like(acc_ref)

        h2 = h2_ref[...]
        whole = lambda w_ref: w_ref[...].reshape(D_FF, D)
        tile = lambda j: slice(j * FF_TILE, (j + 1) * FF_TILE)
        for j in range(N_FF_TILES):
            g = _dot_nt(h2, wg_ref[j])
            u = _dot_nt(h2, wu_ref[j])
            gs[j] = g
            us[j] = u
            act_ref[:, tile(j)] = (g * _sigmoid(g) * u).astype(BF16)
        f = _dot(act_ref[...], whole(wd_ref))
        x1v = x1_ref[...]
        g2 = g2_ref[...]
        diff = x1v + g2 * f - t_ref[...]
        dy = diff * (1.0 / D)
        df = (g2 * dy).astype(BF16)
        df_ref[...] = df
        dact_all = _dot_nt(df, whole(wd_ref))
        for j in range(N_FF_TILES):
            g, u = gs[j], us[j]
            sg = _sigmoid(g)
            dact = dact_all[:, tile(j)]
            dgt_ref[:, tile(j)] = (dact * u * (sg * (1.0 + g * (1.0 - sg)))).astype(BF16)
            dup_ref[:, tile(j)] = (dact * (g * sg)).astype(BF16)
        dh2 = _dot(dgt_ref[...], whole(wg_ref)) + _dot(dup_ref[...], whole(wu_ref))
        dx, dsh, dsc, dnw = _rms_mod_bwd(x1v, nfw_ref[...], sc_ref[...], dh2)
        dx_ref[...] = dy + dx
        acc_ref[0:1, :] += dsh
        acc_ref[1:2, :] += dsc
        acc_ref[2:3, :] += dnw
        acc_ref[3:4, :] += jnp.sum(dy * f, axis=0, keepdims=True)
        acc_ref[4:5, :] += 0.5 * jnp.sum(jnp.sum(diff * diff, axis=1, keepdims=True), axis=0,
                                         keepdims=True) * (1.0 / D)

    row = lambda dt_w: pl.BlockSpec((TM, dt_w), lambda i: (i, 0))
    blk = row(D_FF)
    vec = _full((1, D))
    return pl.pallas_call(
        body, name="ffn_fused", grid=(N_LAT_TILES,),
        in_specs=[row(D), row(D), row(D), _VMEM_WHOLE, _VMEM_WHOLE, _VMEM_WHOLE, vec, vec, vec],
        out_specs=[blk, blk, blk, row(D), row(D), _full((8, D))],
        out_shape=[jax.ShapeDtypeStruct((S, D_FF), BF16)] * 3
        + [jax.ShapeDtypeStruct((S, D), BF16), jax.ShapeDtypeStruct((S, D), F32),
           jax.ShapeDtypeStruct((8, D), F32)],
        scratch_shapes=[pltpu.VMEM((N_FF_TILES, TM, FF_TILE), F32), pltpu.VMEM((N_FF_TILES, TM, FF_TILE), F32)],
        compiler_params=_cp(("arbitrary",)),
    )(x1, h2, tgt, w_gate, w_up, w_down, g2, nfw, sc2)


def _proj_bc(h_all, w_b, w_c, carried=None):
    def body(h_ref, wb_ref, wc_ref, pb_ref, pc_ref):
        h = h_ref[...]
        pb_ref[...] = _dot_nt(h, wb_ref[...])

        @pl.when(pl.program_id(0) > 0)
        def _():
            pc_ref[...] = _dot_nt(h, wc_ref[...])

    return _pcall(
        body, carried, name="proj_bc", grid=(N_TILES,),
        in_specs=[pl.BlockSpec((TM, D), lambda i: (i, 0)), _VMEM_WHOLE, _VMEM_WHOLE],
        out_specs=[pl.BlockSpec((TM, WB), lambda i: (i, 0)), pl.BlockSpec((TM, WC), lambda i: (_lat(i), 0))],
        out_shape=[jax.ShapeDtypeStruct((T, WB), F32), jax.ShapeDtypeStruct((S, WC), F32)],
        scratch_shapes=[], operands=[h_all, w_b, w_c])


def _input_bwd(dp_a, dp_b, dp_c, w_a, w_b, w_c, ctx, x, dx1, nw, sh, sc, carried=None):
    def body(da_ref, db_ref, dc_ref, wa_ref, wb_ref, wc_ref, ctx_ref, x_ref, dx1_ref, nw_ref, sh_ref,
             sc_ref, gx_ref, acc_ref):
        i = pl.program_id(0)

        @pl.when(i == 0)
        def _():
            acc_ref[...] = jnp.zeros_like(acc_ref)

        dh = _dot(da_ref[...], wa_ref[...]) + _dot(db_ref[...], wb_ref[...])

        @pl.when(i == 0)
        def _():
            _, dsh, dsc, dnw = _rms_mod_bwd(ctx_ref[...], nw_ref[...], sc_ref[0:1, :], dh)
            acc_ref[3:4, :] += dsh
            acc_ref[4:5, :] += dsc
            acc_ref[2:3, :] += dnw

        @pl.when(i > 0)
        def _():
            dhl = dh + _dot(dc_ref[...], wc_ref[...])
            dx, dsh, dsc, dnw = _rms_mod_bwd(x_ref[...], nw_ref[...], sc_ref[1:2, :], dhl)
            gx_ref[...] = dx1_ref[...] + dx
            acc_ref[0:1, :] += dsh
            acc_ref[1:2, :] += dsc
            acc_ref[2:3, :] += dnw

    lat = lambda w: pl.BlockSpec((TM, w), lambda i: (_lat(i), 0))
    return _pcall(
        body, carried, name="input_bwd", grid=(N_TILES,),
        in_specs=[pl.BlockSpec((TM, WA), lambda i: (i, 0)), pl.BlockSpec((TM, WB), lambda i: (i, 0)),
                  lat(WC), _VMEM_WHOLE, _VMEM_WHOLE, _VMEM_WHOLE, _full((TM, D)), lat(D), lat(D),
                  _full((1, D)), _full((2, D)), _full((2, D))],
        out_specs=[lat(D), _full((8, D))],
        out_shape=[jax.ShapeDtypeStruct((S, D), F32), jax.ShapeDtypeStruct((8, D), F32)],
        scratch_shapes=[], operands=[dp_a, dp_b, dp_c, w_a, w_b, w_c, ctx, x, dx1, nw, sh, sc])


_C1 = 1.0 - ADAM_B1 ** ADAM_STEP
_C2 = 1.0 - ADAM_B2 ** ADAM_STEP


def _adamw_math(w, g, m, v):
    m = ADAM_B1 * m + (1.0 - ADAM_B1) * g
    v = ADAM_B2 * v + (1.0 - ADAM_B2) * (g * g)
    m_hat = m / _C1
    v_hat = v / _C2
    delta = -ADAM_LR * (m_hat / (jnp.sqrt(v_hat) + ADAM_EPS) + ADAM_WD * w)
    return delta, m, v


def _adamw_sharded(terms, w, m, v, name, tr, extra=None, after=None):
    rows, cols = w.shape

    def body(*refs):
        t_ref, w_ref, m_ref, v_ref = refs[:4]
        g_ref, d_ref, nm_ref, nv_ref = refs[-4:]
        g = t_ref[0].astype(F32)
        for s in range(1, N_CHIPS):
            g = g + t_ref[s].astype(F32)
        if extra is not None:
            g = g + refs[4][...].astype(F32)
        g_ref[...] = g
        d_ref[...], nm_ref[...], nv_ref[...] = _adamw_math(w_ref[...], g, m_ref[...], v_ref[...])

    blk = pl.BlockSpec((tr, cols), lambda i: (i, 0))
    return pl.pallas_call(
        body, name=name, grid=(rows // tr,),
        in_specs=[pl.BlockSpec((N_CHIPS, tr, cols), lambda i: (0, i, 0)), blk, blk, blk]
        + ([blk] if extra is not None else []) + ([_ANY] if after is not None else []),
        out_specs=[blk] * 4,
        out_shape=[jax.ShapeDtypeStruct((rows, cols), F32)] * 4,
        compiler_params=_cp(("parallel",)),
    )(terms, w, m, v, *([extra] if extra is not None else []), *([after] if after is not None else []))


def _adamw_plain(g, w, m, v, name, tr=None):
    def body(g_ref, w_ref, m_ref, v_ref, d_ref, nm_ref, nv_ref):
        d_ref[...], nm_ref[...], nv_ref[...] = _adamw_math(w_ref[...], g_ref[...], m_ref[...], v_ref[...])

    if tr is None:
        return pl.pallas_call(
            body, name=name, in_specs=[_VMEM_WHOLE] * 4, out_specs=[_VMEM_WHOLE] * 3,
            out_shape=[jax.ShapeDtypeStruct(w.shape, F32)] * 3,
            compiler_params=_cp(),
        )(g, w, m, v)
    blk = pl.BlockSpec((tr, w.shape[1]), lambda i: (i, 0))
    return pl.pallas_call(
        body, name=name, grid=(w.shape[0] // tr,), in_specs=[blk] * 4, out_specs=[blk] * 3,
        out_shape=[jax.ShapeDtypeStruct(w.shape, F32)] * 3,
        compiler_params=_cp(("parallel",)),
    )(g, w, m, v)


SMALL_ROWS = 16
R_DMOD, R_DCTX, R_NMIX, R_NFFN, R_MISC, R_DLB, R_BADA01 = 0, 6, 8, 9, 10, 11, 13
M_HNW, M_QNW, M_KNW, M_SINK, M_LOSS = 0, 128, 256, 384, 512


def _pack_small(acc_in, acc_mg, acc_ffn, acc_prep, dsink, dlb):
    def body(in_ref, mg_ref, ff_ref, pp_ref, ds_ref, dlb_ref, o_ref):
        o_ref[...] = jnp.zeros_like(o_ref)
        o_ref[0:2, :] = in_ref[0:2, :]
        o_ref[2:3, :] = mg_ref[0:1, :]
        o_ref[3:5, :] = ff_ref[0:2, :]
        o_ref[5:6, :] = ff_ref[3:4, :]
        o_ref[6:8, :] = in_ref[3:5, :]
        o_ref[8:9, :] = in_ref[2:3, :]
        o_ref[9:10, :] = ff_ref[2:3, :]
        o_ref[10:11, M_HNW:M_HNW + HG_DIM] = pp_ref[0:1, 0:HG_DIM]
        r = lax.broadcasted_iota(jnp.int32, (ATW, 128), 0)
        c = lax.broadcasted_iota(jnp.int32, (ATW, 128), 1)
        fold = jnp.where((r % HEAD_DIM == c) & (c < HEAD_DIM), 1.0, 0.0).astype(BF16)
        qk = jnp.concatenate([pp_ref[1:2, :], pp_ref[2:3, :], jnp.zeros((6, ATW), F32)], axis=0)
        folded = _dot_exact_rhs01(qk, fold)
        o_ref[10:11, M_QNW:M_QNW + 128] = folded[0:1, :]
        o_ref[10:11, M_KNW:M_KNW + 128] = folded[1:2, :]
        o_ref[10:11, M_SINK:M_SINK + ATT_HEADS] = ds_ref[...]
        o_ref[10:11, M_LOSS:M_LOSS + 128] = ff_ref[4:5, 0:128]
        o_ref[11:13, 0:HGW] = dlb_ref[...]

    return pl.pallas_call(
        body, name="pack_small", in_specs=[_VMEM_WHOLE] * 6, out_specs=_VMEM_WHOLE,
        out_shape=jax.ShapeDtypeStruct((SMALL_ROWS, D), F32), compiler_params=_cp(),
    )(acc_in, acc_mg, acc_ffn, acc_prep, dsink, dlb)


def _sum_small(gathered):
    def body(g_ref, o_ref):
        tot = g_ref[0]
        for s in range(1, N_DEV):
            tot = tot + g_ref[s]
        o_ref[...] = tot
        o_ref[R_BADA01:R_BADA01 + 2, :] = tot[0:2, :] + tot[R_DCTX:R_DCTX + 2, :]

    return pl.pallas_call(
        body, name="sum_small", in_specs=[_VMEM_WHOLE], out_specs=_VMEM_WHOLE,
        out_shape=jax.ShapeDtypeStruct((SMALL_ROWS, D), F32), compiler_params=_cp(),
    )(gathered)


_REP_NAMES = ("b_ada", "c_ctx", "norm_mix_w", "norm_ffn_w", "hgrn_norm_w", "q_norm_w", "k_norm_w", "attn_sinks")


def _adamw_replicated(tot, g_c_ctx, ws, ms, vs):
    n = len(_REP_NAMES)

    def body(*refs):
        tot_ref, gc_ref = refs[0], refs[1]
        w_refs, m_refs, v_refs = refs[2:2 + n], refs[2 + n:2 + 2 * n], refs[2 + 2 * n:2 + 3 * n]
        outs = refs[2 + 3 * n:]
        row = lambda r: tot_ref[r:r + 1, :]
        misc = row(R_MISC)
        grads = [jnp.concatenate([row(R_BADA01), row(R_BADA01 + 1)] + [row(k) for k in range(2, 6)], axis=1),
                 gc_ref[...], row(R_NMIX), row(R_NFFN),
                 misc[:, M_HNW:M_HNW + HG_DIM], misc[:, M_QNW:M_QNW + HEAD_DIM],
                 misc[:, M_KNW:M_KNW + HEAD_DIM], misc[:, M_SINK:M_SINK + ATT_HEADS]]
        for k in range(n):
            outs[k][...] = grads[k]
            outs[n + k][...], outs[2 * n + k][...], outs[3 * n + k][...] = _adamw_math(
                w_refs[k][...], grads[k], m_refs[k][...], v_refs[k][...])

    shapes = [jax.ShapeDtypeStruct(w.shape, F32) for w in ws]
    return pl.pallas_call(
        body, name="adamw_replicated", in_specs=[_VMEM_WHOLE] * (2 + 3 * n), out_specs=[_VMEM_WHOLE] * (4 * n),
        out_shape=shapes * 4, compiler_params=_cp(),
    )(tot, g_c_ctx, *ws, *ms, *vs)


def _lb_grads(dlb, lbl):
    def body(d_ref, l_ref, o_ref):
        for d in (0, 1):
            ll = l_ref[d]
            lb = _sigmoid(ll[0:1, :] - ll[1:2, :])
            t = d_ref[d:d + 1, :] * lb * (1.0 - lb)
            o_ref[d, 0:1, :] = t
            o_ref[d, 1:2, :] = -t

    return pl.pallas_call(
        body, name="lb_grads", in_specs=[_VMEM_WHOLE] * 2, out_specs=_VMEM_WHOLE,
        out_shape=jax.ShapeDtypeStruct((2, 2, HGW), F32), compiler_params=_cp(),
    )(dlb, lbl)


def _c_ctx_grad(terms, c_ctx):
    def body(t_ref, c_ref, o_ref):
        tot = t_ref[0, 8:9, :]
        for s in range(1, N_DEV):
            tot = tot + t_ref[s, 8:9, :]
        cv = c_ref[...]
        sg = _sigmoid(cv)
        o_ref[...] = tot * (sg * (1.0 + cv * (1.0 - sg)))

    return pl.pallas_call(
        body, name="c_ctx_grad", in_specs=[_VMEM_WHOLE] * 2, out_specs=_VMEM_WHOLE,
        out_shape=jax.ShapeDtypeStruct((1, D), F32), compiler_params=_cp(),
    )(terms, c_ctx)


def _in_perm():
    fz, bz, inp, kk, vv, qhg, ghg, qat, gates = 0, 512, 1024, 1536, 1664, 1792, 2304, 2816, 3328
    cols = []
    for h in range(HG_HEADS):
        for base in (fz, bz, inp, qhg):
            cols += list(range(base + 128 * h, base + 128 * (h + 1)))
    cols += list(range(ghg, ghg + 512)) + list(range(qat, qat + 512))
    cols += list(range(kk, kk + 128)) + list(range(vv, vv + 128))
    cols += list(range(gates, gates + 2048))
    return np.asarray(cols, np.int32)


_PERM = _in_perm()


_PIECES = {"a": (0, WA, 128), "b": (WA, WB, 256), "c": (WA + WB, WC, 256)}


def _block_table(piece):
    lo, n, blk = _PIECES[piece]
    starts = [int(_PERM[r]) for r in range(lo, lo + n, blk)]
    assert all(s % blk == 0 and np.array_equal(_PERM[r:r + blk], np.arange(s, s + blk))
               for s, r in zip(starts, range(lo, lo + n, blk)))
    return jnp.asarray([s // blk for s in starts], jnp.int32), blk


def _pick_row_blocks(x, table, blk, name):
    cols = x.shape[1]

    def body(t_ref, x_ref, o_ref):
        o_ref[...] = x_ref[...]

    return pl.pallas_call(
        body, name=name,
        grid_spec=pltpu.PrefetchScalarGridSpec(
            num_scalar_prefetch=1, grid=(table.shape[0],),
            in_specs=[pl.BlockSpec((blk, cols), lambda i, t: (t[i], 0))],
            out_specs=pl.BlockSpec((blk, cols), lambda i, t: (i, 0))),
        out_shape=jax.ShapeDtypeStruct((table.shape[0] * blk, cols), x.dtype),
        compiler_params=_cp(("arbitrary",)),
    )(table, x)


def _place_row_blocks(x, table, blk, into, out_rows, name):
    cols = x.shape[1]

    def body(t_ref, x_ref, *rest):
        rest[-1][...] = x_ref[...]

    operands, in_specs, aliases = [table, x], [pl.BlockSpec((blk, cols), lambda i, t: (i, 0))], {}
    if into is not None:
        operands.append(into)
        in_specs.append(_ANY)
        aliases = {2: 0}
    return pl.pallas_call(
        body, name=name,
        grid_spec=pltpu.PrefetchScalarGridSpec(
            num_scalar_prefetch=1, grid=(table.shape[0],), in_specs=in_specs,
            out_specs=pl.BlockSpec((blk, cols), lambda i, t: (t[i], 0))),
        out_shape=jax.ShapeDtypeStruct((out_rows, cols), x.dtype),
        input_output_aliases=aliases,
        compiler_params=_cp(("arbitrary",)),
    )(*operands)


def _local_step(x2, ctx2, h_all, h_lat, tgt, lbl, sh_in, sc_in, gate1, sh2, sc2, gate2, norm_mix_w, norm_ffn_w,
                hgrn_norm_w, q_norm_w, k_norm_w, attn_sinks, w_a, w_b, w_c, s_bh, s_ba, s_out,
                s_gate, s_up, s_down):
    first_last = lambda n: [(0, True), (n - 1, False)]
    p_a = _mm_nt(h_all, w_a, tm=T, tn=512, out_dtype=F32, name="proj_a")
    (o, st), (g_gate, g_bh, g_ba) = _hgrn_fwd(
        p_a, lbl, (_gather_comm_relayed([s_gate, s_bh, s_ba]),
                   [(0, True), (HG_HEADS - 2, True), (HG_HEADS - 1, False)]))
    (p_b, p_c), (g_out,) = _proj_bc(
        h_all, w_b, w_c, (_gather_comm_relayed([s_out]), [(0, True), (N_TILES - 4, True), (N_TILES - 1, False)]))
    cos, sin = _rope_tables()
    qnw_t, knw_t = jnp.tile(q_norm_w, (1, ATT_HEADS)), jnp.tile(k_norm_w, (1, KV_HEADS))
    y_hg, qn, k_rep, v_rep = _prep_fwd(p_b, o, cos, sin, hgrn_norm_w, qnw_t, knw_t)
    (y_at, lse), (g_up, g_down) = _attn_fwd(
        qn, k_rep, v_rep, attn_sinks,
        (_gather_comm_relayed([s_up, s_down]), [(0, True), (N_BLOCKS - 6, True), (N_BLOCKS - 1, False)]))
    w_bh, w_ba, w_o = g_bh.reshape(D, HGW), g_ba.reshape(D, ATW), g_out.reshape(D, D)
    (mixed, r, x1, h2), _ = _merge_fwd(
        y_hg, y_at, p_c, x2, w_bh, w_ba, w_o, gate1, norm_ffn_w, sh2, sc2)
    g_gate, g_up, g_down = [g.reshape(N_FF_TILES, FF_TILE, D) for g in (g_gate, g_up, g_down)]

    act, d_gate, d_up, d_f, dx1, acc_ffn = _ffn_fused(x1, h2, tgt, g_gate, g_up, g_down, gate2,
                                                      norm_ffn_w, sc2)
    by_chip = lambda t: t.reshape((N_CHIPS, 2) + t.shape[1:])
    ff_by_chip = lambda t: t.reshape(N_CHIPS, 2, FF_BLK, D)
    t_down, _ = _mm_tn_blocked(act, d_f, "grad_down", N_FF_TILES)
    t_down = ff_by_chip(t_down)
    t_gate, (f_down,) = _mm_tn_blocked(d_gate, h2, "grad_gate", N_FF_HALVES,
                                       (_sibling_comm([t_down]), first_last(N_FF_HALVES)))
    t_gate = ff_by_chip(t_gate)
    t_up, (f_gate,) = _mm_tn_blocked(d_up, h2, "grad_up", N_FF_HALVES,
                                     (_sibling_comm([t_gate]), first_last(N_FF_HALVES)))
    t_up = ff_by_chip(t_up)

    (d_r, d_a, d_b, dp_c, dy_hg, dy_at, acc_mg), (f_up,) = _merge_bwd(
        dx1, r, y_hg, y_at, p_c, w_bh, w_ba, w_o, gate1, (_sibling_comm([t_up]), first_last(N_LAT_TILES)))
    c_down, c_gate, c_up = [_pair_sum(t, f, "pair_sum_" + nm) for t, f, nm in
                            ((t_down, f_down, "down"), (t_gate, f_gate, "gate"), (t_up, f_up, "up"))]
    t_out = _mm_tn(mixed, d_r, tk=1024, nk=2, tm=1024, tn=1024, out_dtype=BF16, name="grad_out")
    t_bh = _mm_tn(d_a, y_hg, tk=2048, nk=1, tm=1024, tn=512, out_dtype=BF16, name="grad_bh")
    t_ba = _mm_tn(d_b, y_at, tk=2048, nk=1, tm=1024, tn=512, out_dtype=BF16, name="grad_ba")
    t_bh, t_ba, t_out = [by_chip(t.reshape(N_DEV, D // N_DEV, t.shape[1])) for t in (t_bh, t_ba, t_out)]
    (dq, dk_rep, dv_rep, dsink), (r_up,) = _attn_bwd(
        qn, k_rep, v_rep, attn_sinks, y_at, lse, dy_at, (_chip_comm([c_up]), first_last(N_BLOCKS)))
    (dp_b, d_o, acc_prep), (f_bh, f_ba, f_out) = _prep_bwd(
        p_b, o, cos, sin, hgrn_norm_w, qnw_t, knw_t, dy_hg, dq, dk_rep, dv_rep,
        (_sibling_comm([t_bh, t_ba, t_out]), first_last(N_TILES)))
    c_bh, c_ba, c_out = [_pair_sum(t, f, "pair_sum_" + nm) for t, f, nm in
                         ((t_bh, f_bh, "bh"), (t_ba, f_ba, "ba"), (t_out, f_out, "out"))]
    (dp_a, dlb), (r_bh, r_ba, r_out, r_down, r_gate) = _hgrn_bwd(
        p_a, lbl, d_o, st, (_chip_comm([c_bh, c_ba, c_out, c_down, c_gate]), first_last(HG_HEADS)))
    t_a = _mm_tn(dp_a, h_all, tk=T, nk=1, tm=1024, tn=1024, out_dtype=BF16, name="grad_in_a")
    t_b = _mm_tn(dp_b, h_all, tk=T, nk=1, tm=640, tn=1024, out_dtype=BF16, name="grad_in_b")
    t_c = _mm_tn(dp_c, h_lat, tk=1024, nk=2, tm=1024, tn=1024, out_dtype=BF16, name="grad_in_c")
    t_in = None
    for piece, nm in ((t_a, "a"), (t_b, "b"), (t_c, "c")):
        t_in = _place_row_blocks(piece, *_block_table(nm), t_in, IN_COLS, "order_terms_" + nm)
    t_in = by_chip(t_in.reshape(N_DEV, IN_BLK, D))
    (f_in,) = _run_comm(_sibling_comm([t_in]), "scatter_in_sibling")
    c_in = _pair_sum(t_in, f_in, "pair_sum_in")
    sems, c_in, land, token = _chip_exchange_start(c_in, jnp.zeros(c_in.shape, c_in.dtype))
    (grad_x, acc_in), _ = _input_bwd(dp_a, dp_b, dp_c, w_a, w_b, w_c, ctx2, x2, dx1,
                                     norm_mix_w + token[0, 0], sh_in, sc_in)
    small = _pack_small(acc_in, acc_mg, acc_ffn, acc_prep, dsink, dlb)
    return grad_x, small, [r_bh, r_ba, r_out, r_gate, r_up, r_down], (sems, c_in, land)


def kernel(x, c, ctx, c_ctx, w_ada, b_ada, norm_mix_w, norm_ffn_w, w_in, hgrn_lb_logits, hgrn_norm_w, q_norm_w, k_norm_w, attn_sinks, w_branch_hgrn, w_branch_attn, w_out, w_ffn_gate, w_ffn_up, w_ffn_down, loss_target, m_c_ctx, m_w_ada, m_b_ada, m_norm_mix_w, m_norm_ffn_w, m_w_in, m_hgrn_lb_logits, m_hgrn_norm_w, m_q_norm_w, m_k_norm_w, m_attn_sinks, m_w_branch_hgrn, m_w_branch_attn, m_w_out, m_w_ffn_gate, m_w_ffn_up, m_w_ffn_down, v_c_ctx, v_w_ada, v_b_ada, v_norm_mix_w, v_norm_ffn_w, v_w_in, v_hgrn_lb_logits, v_hgrn_norm_w, v_q_norm_w, v_k_norm_w, v_attn_sinks, v_w_branch_hgrn, v_w_branch_attn, v_w_out, v_w_ffn_gate, v_w_ffn_up, v_w_ffn_down):
    me = 4 * lax.axis_index("x") + 2 * lax.axis_index("y") + lax.axis_index("c")
    x2, ctx2, tgt = x[0], ctx[0], loss_target[0]
    w_ada2, w_in2 = w_ada[0], w_in[0]

    cond = jnp.zeros((8, D), F32).at[0].set(c[0]).at[1, :256].set(hgrn_lb_logits.reshape(256))
    b_cols = lax.dynamic_slice(b_ada, (0, me * ADA_BLK), (1, ADA_BLK))
    g0, cc, mod, g_in, h_all, h_lat = _prologue(cond, c_ctx.reshape(1, D), w_ada2, b_cols, w_in2.T.astype(BF16),
                                         x2, ctx2, norm_mix_w)
    lbl = jnp.transpose(g0[:, 1, :256].reshape(N_DEV, 2, 2, 64), (1, 2, 0, 3)).reshape(2, 2, HGW)
    sh1, sc1, gate1, sh2, sc2, gate2 = [mod[k:k + 1] for k in range(6)]
    sh_in = jnp.concatenate([mod[6:7], sh1], axis=0)
    sc_in = jnp.concatenate([mod[7:8], sc1], axis=0)

    shards = [w_branch_hgrn[0].T, w_branch_attn[0].T, w_out[0], w_ffn_gate[0].T, w_ffn_up[0].T, w_ffn_down[0]]
    w_in_t = g_in.reshape(IN_COLS, D)
    w_a, w_b, w_c = [_pick_row_blocks(w_in_t, *_block_table(nm), "order_w_" + nm) for nm in "abc"]

    grad_x, small, (r_bh, r_ba, r_out, r_gate, r_up, r_down), pending_in = _local_step(
        x2, ctx2, h_all, h_lat, tgt, lbl, sh_in, sc_in, gate1, sh2, sc2, gate2, norm_mix_w, norm_ffn_w, hgrn_norm_w,
        q_norm_w, k_norm_w, attn_sinks, w_a, w_b, w_c, *[s.astype(BF16) for s in shards])

    big, updated = {}, []
    for nm, rr, ww, mm, vv, tr, transposed in (
            ("w_branch_hgrn", r_bh, w_branch_hgrn[0], m_w_branch_hgrn[0], v_w_branch_hgrn[0], 128, True),
            ("w_branch_attn", r_ba, w_branch_attn[0], m_w_branch_attn[0], v_w_branch_attn[0], 128, True),
            ("w_out", r_out, w_out[0], m_w_out[0], v_w_out[0], 128, False),
            ("w_ffn_gate", r_gate, w_ffn_gate[0], m_w_ffn_gate[0], v_w_ffn_gate[0], 176, True),
            ("w_ffn_up", r_up, w_ffn_up[0], m_w_ffn_up[0], v_w_ffn_up[0], 176, True),
            ("w_ffn_down", r_down, w_ffn_down[0], m_w_ffn_down[0], v_w_ffn_down[0], 176, False)):
        if transposed:
            res = _adamw_sharded(rr, ww.T, mm.T, vv.T, "adamw_" + nm, tr, after=grad_x)
            big[nm] = [t.T[None] for t in res]
        else:
            res = _adamw_sharded(rr, ww, mm, vv, "adamw_" + nm, tr, after=grad_x)
            big[nm] = [t[None] for t in res]
        updated.append(res[1])

    (g2,) = _all_gather([small], "gather_small", True, after=updated)
    tot = _sum_small(g2)
    dm = jnp.zeros((16, 6 * D), F32).at[:8].set(g2[:, R_DMOD:R_DMOD + 6, :].reshape(N_DEV, 6 * D))
    dm = dm.at[8, :2 * D].set(tot[R_DCTX:R_DCTX + 2].reshape(2 * D))
    dm_cols = lax.dynamic_slice(dm, (0, me * ADA_BLK), (16, ADA_BLK))
    g_w_ada, dsc_term = _ada_grads(cc, dm_cols, w_ada2)
    (g3,) = _all_gather([dsc_term], "gather_cctx", True)
    g_c_ctx = _c_ctx_grad(g3, c_ctx.reshape(1, D))
    g_lbl = _lb_grads(tot[R_DLB:R_DLB + 2, :HGW], lbl)
    g_lb_mine = lax.dynamic_slice(g_lbl, (0, 0, me * 64), (2, 2, 64))
    misc = tot[R_MISC]
    loss = misc[M_LOSS]

    rep_out = _adamw_replicated(
        tot, g_c_ctx,
        [b_ada, c_ctx.reshape(1, D), norm_mix_w, norm_ffn_w, hgrn_norm_w, q_norm_w, k_norm_w, attn_sinks],
        [m_b_ada, m_c_ctx.reshape(1, D), m_norm_mix_w, m_norm_ffn_w, m_hgrn_norm_w, m_q_norm_w, m_k_norm_w,
         m_attn_sinks],
        [v_b_ada, v_c_ctx.reshape(1, D), v_norm_mix_w, v_norm_ffn_w, v_hgrn_norm_w, v_q_norm_w, v_k_norm_w,
         v_attn_sinks])
    rep = []
    for kind in range(4):
        vals = dict(zip(_REP_NAMES, rep_out[kind * len(_REP_NAMES):(kind + 1) * len(_REP_NAMES)]))
        vals["c_ctx"] = vals["c_ctx"].reshape(D)
        rep.append(vals)

    sems, c_in, land = pending_in
    d_ada, nm_ada, nv_ada = _adamw_plain(g_w_ada, w_ada2, m_w_ada[0], v_w_ada[0], "adamw_w_ada", tr=256)
    land = _chip_exchange_wait(sems, c_in, land, d_ada)
    own = lax.dynamic_index_in_dim(c_in, 2 * lax.axis_index("x") + lax.axis_index("y"), 0, keepdims=False)
    big["w_in"] = [t.T[None] for t in _adamw_sharded(land, w_in2.T, m_w_in[0].T, v_w_in[0].T, "adamw_w_in", 336,
                                                     extra=own)]
    ada = [t[None] for t in (g_w_ada, d_ada, nm_ada, nv_ada)]
    lb_w = hgrn_lb_logits.reshape(4, 64)
    d_lb, nm_lb, nv_lb = _adamw_plain(g_lb_mine.reshape(4, 64), lb_w, m_hgrn_lb_logits.reshape(4, 64),
                                      v_hgrn_lb_logits.reshape(4, 64), "adamw_lb")
    lbs = [t.reshape(2, 2, 64) for t in (g_lb_mine, d_lb, nm_lb, nv_lb)]

    names = ['c_ctx', 'w_ada', 'b_ada', 'norm_mix_w', 'norm_ffn_w', 'w_in', 'hgrn_lb_logits', 'hgrn_norm_w',
             'q_norm_w', 'k_norm_w', 'attn_sinks', 'w_branch_hgrn', 'w_branch_attn', 'w_out', 'w_ffn_gate',
             'w_ffn_up', 'w_ffn_down']
    outs = [loss, grad_x[None]]
    for kind in range(4):
        for nm in names:
            if nm == 'w_ada':
                outs.append(ada[kind])
            elif nm == 'hgrn_lb_logits':
                outs.append(lbs[kind])
            elif nm in big:
                outs.append(big[nm][kind])
            else:
                outs.append(rep[kind][nm])
    return tuple(outs)
```

```python
import functools
import math

import numpy as np
import jax
import jax.numpy as jnp
from jax import lax
from jax.experimental import pallas as pl
from jax.experimental.pallas import tpu as pltpu

F32 = jnp.float32
BF16 = jnp.bfloat16

N_DEV = 8
D = 1024
S = 2048
L = 256
T = L + S
TM = 256
N_TILES = T // TM
N_LAT_TILES = S // TM
HG_HEADS = 4
HG_DIM = 128
HGW = 512
CHUNK = 32
N_CHUNKS = T // CHUNK
N_CTX_CHUNKS = L // CHUNK
ATT_HEADS = 8
KV_HEADS = 2
HEAD_DIM = 64
ATW = 512
KVW = 128
BLOCK = 128
N_BLOCKS = S // BLOCK
GRID_W = 64
ROPE_THETA = 10000.0
D_FF = 2816
FF_BLK = D_FF // N_DEV
FF_TILE = 256
N_FF_TILES = D_FF // FF_TILE
N_FF_HALVES = 2
IN_COLS = 5376
IN_BLK = IN_COLS // N_DEV
ADA_BLK = 6 * D // N_DEV
EPS = 1e-6
WA, WB, WC = 2048, 1280, 2048

ADAM_LR = 0.001
ADAM_B1 = 0.9
ADAM_B2 = 0.999
ADAM_EPS = 1e-08
ADAM_WD = 0.01
ADAM_STEP = 10

VMEM_LIMIT = 56 * 1024 * 1024
MESH = pl.DeviceIdType.MESH


def _cp(sem=None, vmem=VMEM_LIMIT):
    return pltpu.CompilerParams(dimension_semantics=sem, vmem_limit_bytes=vmem)


def _full(shape):
    n = len(shape)
    return pl.BlockSpec(shape, lambda *_: (0,) * n)


_VMEM_WHOLE = pl.BlockSpec(memory_space=pltpu.VMEM)
_ANY = pl.BlockSpec(memory_space=pl.ANY)


def _sigmoid(v):
    return 1.0 / (1.0 + jnp.exp(-v))


def _dot(a, b):
    return jnp.dot(a, b, preferred_element_type=F32)


def _dot_nt(a, b):
    return lax.dot_general(a, b, (((1,), (1,)), ((), ())), preferred_element_type=F32)


def _dot_tn(a, b):
    return lax.dot_general(a, b, (((0,), (0,)), ((), ())), preferred_element_type=F32)


def _split3(v):
    hi = v.astype(BF16)
    r = v - hi.astype(F32)
    mid = r.astype(BF16)
    lo = (r - mid.astype(F32)).astype(BF16)
    return hi, mid, lo


def _dot_exact_rhs01(v, m01):
    hi, mid, lo = _split3(v)
    return _dot(hi, m01) + _dot(mid, m01) + _dot(lo, m01)


def _split2(v):
    hi = v.astype(BF16)
    return hi, (v - hi.astype(F32)).astype(BF16)


def _dot_lhs01(m01, v):
    hi, lo = _split2(v)
    return _dot(m01, hi) + _dot(m01, lo)


def _dot_f32(a, b, dot=_dot):
    ah, am, al = _split3(a)
    bh, bm, bl = _split3(b)
    return (dot(ah, bh) + (dot(ah, bm) + dot(am, bh))
            + (dot(am, bm) + dot(ah, bl) + dot(al, bh)))


def _my_pos():
    return lax.axis_index("x"), lax.axis_index("y"), lax.axis_index("c")


class _Comm:
    def __init__(self, operands, out_shapes, sems, phases):
        self.operands, self.out_shapes, self.sems, self.phases = operands, out_shapes, sems, phases


def _gather_comm(blocks):
    n = len(blocks)

    def parts(ins, outs, sems):
        send_sems, recv_sems, local_sems = sems
        x, y, c = _my_pos()
        me, sibling = (x, y, c), (x, y, 1 - c)
        chips = [(1 - x, y), (x, 1 - y), (1 - x, 1 - y)]

        def slot(a, px, py, pc):
            return outs[a].at[4 * px + 2 * py + pc]

        def copy(a, k, block, to, src=None):
            return pltpu.make_async_remote_copy(
                src_ref=slot(a, *block) if src is None else src, dst_ref=slot(a, *block),
                send_sem=send_sems.at[a, k], recv_sem=recv_sems.at[a, k],
                device_id=to, device_id_type=MESH)

        mine = [pltpu.make_async_copy(ins[a], slot(a, *me), local_sems.at[a]) for a in range(n)]
        first = []
        for a in range(n):
            first.append(copy(a, 0, me, sibling, src=ins[a]))
            first += [copy(a, 1 + j, me, (*chip, c), src=ins[a]) for j, chip in enumerate(chips)]
        passed = [copy(a, 4 + j, (*chip, c), sibling) for j, chip in enumerate(chips) for a in range(n)]
        return c, me, sibling, chips, copy, mine, first, passed

    def start(ins, outs, sems):
        _, _, _, _, _, mine, first, _ = parts(ins, outs, sems)
        for cp in mine + first:
            cp.start()

    def forward(ins, outs, sems):
        c, me, _, chips, copy, _, _, passed = parts(ins, outs, sems)
        for j, chip in enumerate(chips):
            for a in range(n):
                copy(a, 1 + j, (*chip, c), me).wait_recv()
                passed[j * n + a].start()

    def finish(ins, outs, sems):
        c, me, sibling, chips, copy, mine, first, passed = parts(ins, outs, sems)
        for a in range(n):
            copy(a, 0, sibling, me).wait_recv()
            for j, chip in enumerate(chips):
                copy(a, 4 + j, (*chip, 1 - c), me).wait_recv()
        for cp in first + passed:
            cp.wait_send()
        for cp in mine:
            cp.wait()

    return _Comm(blocks, [jax.ShapeDtypeStruct((N_DEV,) + b.shape, b.dtype) for b in blocks],
                 [pltpu.SemaphoreType.DMA((n, 7)), pltpu.SemaphoreType.DMA((n, 7)), pltpu.SemaphoreType.DMA((n,))],
                 [start, forward, finish])


def _gather_comm_relayed(blocks):
    n = len(blocks)

    def parts(ins, outs, sems):
        send_sems, recv_sems, local_sems = sems
        x, y, c = _my_pos()
        me, sibling = (x, y, c), (x, y, 1 - c)
        x_nbr, y_nbr, diag = (1 - x, y, c), (x, 1 - y, c), (1 - x, 1 - y, c)

        def slot(a, dev, half=None):
            ref = outs[a].at[4 * dev[0] + 2 * dev[1] + dev[2]]
            if half is None:
                return ref
            rows = blocks[a].shape[0] // 2
            return ref.at[pl.ds(half * rows, rows)]

        def copy(a, k, block, to, half=None, src=None):
            return pltpu.make_async_remote_copy(
                src_ref=slot(a, block, half) if src is None else src, dst_ref=slot(a, block, half),
                send_sem=send_sems.at[a, k], recv_sem=recv_sems.at[a, k],
                device_id=to, device_id_type=MESH)

        mine = [pltpu.make_async_copy(ins[a], slot(a, me), local_sems.at[a]) for a in range(n)]
        return me, sibling, x_nbr, y_nbr, diag, copy, mine

    def start(ins, outs, sems):
        me, sibling, x_nbr, y_nbr, _, copy, mine = parts(ins, outs, sems)
        for cp in mine:
            cp.start()
        for a in range(n):
            for k, to in ((1, x_nbr), (2, y_nbr), (0, sibling)):
                copy(a, k, me, to, src=ins[a]).start()

    def forward(ins, outs, sems):
        me, sibling, x_nbr, y_nbr, _, copy, _ = parts(ins, outs, sems)
        for a in range(n):
            copy(a, 1, x_nbr, me).wait_recv()
            copy(a, 3, x_nbr, y_nbr, half=0).start()
            copy(a, 5, x_nbr, sibling).start()
        for a in range(n):
            copy(a, 2, y_nbr, me).wait_recv()
            copy(a, 4, y_nbr, x_nbr, half=1).start()
            copy(a, 6, y_nbr, sibling).start()

    def finish(ins, outs, sems):
        me, sibling, x_nbr, y_nbr, diag, copy, mine = parts(ins, outs, sems)
        sib = lambda dev: (dev[0], dev[1], sibling[2])
        for a in range(n):
            copy(a, 3, diag, me, half=0).wait_recv()
            copy(a, 4, diag, me, half=1).wait_recv()
            copy(a, 7, diag, sibling).start()
        for a in range(n):
            copy(a, 0, sibling, me).wait_recv()
            for k, dev in ((5, x_nbr), (6, y_nbr), (7, diag)):
                copy(a, k, sib(dev), me).wait_recv()
        for a in range(n):
            for k, block, to, half in ((0, me, sibling, None), (1, me, x_nbr, None), (2, me, y_nbr, None),
                                       (3, x_nbr, y_nbr, 0), (4, y_nbr, x_nbr, 1), (5, x_nbr, sibling, None),
                                       (6, y_nbr, sibling, None), (7, diag, sibling, None)):
                copy(a, k, block, to, half=half, src=ins[a] if block is me else None).wait_send()
        for cp in mine:
            cp.wait()

    return _Comm(blocks, [jax.ShapeDtypeStruct((N_DEV,) + b.shape, b.dtype) for b in blocks],
                 [pltpu.SemaphoreType.DMA((n, 8)), pltpu.SemaphoreType.DMA((n, 8)), pltpu.SemaphoreType.DMA((n,))],
                 [start, forward, finish])


_HBM = pl.BlockSpec(memory_space=pltpu.HBM)
_SEM = pl.BlockSpec(memory_space=pltpu.SEMAPHORE)
_SPLIT_COPY = pltpu.CompilerParams(has_side_effects=pltpu.SideEffectType.DATAFLOW_SIDE_EFFECTING)


def _chip_exchange_copies(src_ref, land_ref, sems):
    x, y, c = _my_pos()
    q_me = 2 * x + y
    pairs = []
    for j, (px, py) in enumerate([(1 - x, y), (x, 1 - y), (1 - x, 1 - y)]):
        q = 2 * px + py
        send = pltpu.make_async_remote_copy(
            src_ref=src_ref.at[q], dst_ref=land_ref.at[q_me], send_sem=sems[j], recv_sem=sems[3 + j],
            device_id=(px, py, c), device_id_type=MESH)
        recv = pltpu.make_async_remote_copy(
            src_ref=src_ref.at[q], dst_ref=land_ref.at[q], send_sem=sems[j], recv_sem=sems[3 + j],
            device_id=(x, y, c), device_id_type=MESH)
        pairs.append((send, recv))
    return pairs


def _chip_exchange_start(src, land):
    def body(src_ref, land_ref, *outs):
        sems, token = outs[:6], outs[8]
        for send, _ in _chip_exchange_copies(src_ref, land_ref, sems):
            send.start()
        token[...] = jnp.zeros_like(token)

    res = pl.pallas_call(
        body, name="scatter_in_start",
        out_shape=(pltpu.SemaphoreType.DMA(()),) * 6 + (
            pltpu.HBM(src.shape, src.dtype), pltpu.HBM(land.shape, land.dtype),
            jax.ShapeDtypeStruct((8, 128), F32)),
        in_specs=(_HBM, _HBM), out_specs=(_SEM,) * 6 + (_HBM, _HBM, pl.BlockSpec(memory_space=pltpu.VMEM)),
        input_output_aliases={0: 6, 1: 7}, compiler_params=_SPLIT_COPY,
    )(pltpu.with_memory_space_constraint(src, pltpu.HBM), pltpu.with_memory_space_constraint(land, pltpu.HBM))
    return res[:6], res[6], res[7], res[8]


def _chip_exchange_wait(sems, src_thru, land_thru, after):
    def body(src_ref, land_ref, *rest):
        for send, recv in _chip_exchange_copies(src_ref, land_ref, rest[:6]):
            send.wait_send()
            recv.wait_recv()

    return pl.pallas_call(
        body, name="scatter_in_wait",
        out_shape=(pltpu.HBM(src_thru.shape, src_thru.dtype), pltpu.HBM(land_thru.shape, land_thru.dtype)),
        in_specs=(_HBM, _HBM) + (_SEM,) * 6 + (_ANY,), out_specs=(_HBM, _HBM),
        input_output_aliases={0: 0, 1: 1}, compiler_params=_SPLIT_COPY,
    )(src_thru, land_thru, *sems, after)[1]


def _run_comm(comm, name, in_vmem=False, after=()):
    n_in, n_out, n_after = len(comm.operands), len(comm.out_shapes), len(after)

    def body(*refs):
        ins, refs = refs[:n_in], refs[n_in + n_after:]
        outs, sems = refs[:n_out], refs[n_out:]
        for phase in comm.phases:
            phase(ins, outs, sems)

    spec = _VMEM_WHOLE if in_vmem else _ANY
    return pl.pallas_call(
        body, name=name, out_shape=comm.out_shapes, in_specs=[spec] * n_in + [_ANY] * n_after,
        out_specs=[spec] * n_out, scratch_shapes=comm.sems,
    )(*comm.operands, *after)


def _carrier_call(body, comm, schedule, *, name, grid, in_specs, out_specs, out_shape, scratch_shapes, operands):
    n_in, n_out, n_scr = len(in_specs), len(out_specs), len(scratch_shapes)
    c_in, c_out = len(comm.operands), len(comm.out_shapes)

    def full_body(*refs):
        ins, refs = refs[:n_in], refs[n_in:]
        cins, refs = refs[:c_in], refs[c_in:]
        outs, refs = refs[:n_out], refs[n_out:]
        couts, refs = refs[:c_out], refs[c_out:]
        scr, csems = refs[:n_scr], refs[n_scr:]
        step = pl.program_id(0)

        def run(before):
            for (at, when_before), phase in zip(schedule, comm.phases):
                if when_before == before:
                    pl.when(step == at)(functools.partial(phase, cins, couts, csems))

        run(True)
        body(*ins, *outs, *scr)
        run(False)

    res = pl.pallas_call(
        full_body, name=name, grid=grid,
        in_specs=list(in_specs) + [_ANY] * c_in, out_specs=list(out_specs) + [_ANY] * c_out,
        out_shape=list(out_shape) + list(comm.out_shapes),
        scratch_shapes=list(scratch_shapes) + list(comm.sems),
        compiler_params=_cp(("arbitrary",)),
    )(*operands, *comm.operands)
    return res[:n_out], res[n_out:]


def _pcall(body, carried, *, name, grid, in_specs, out_specs, out_shape, scratch_shapes, operands):
    if carried is None:
        res = pl.pallas_call(body, name=name, grid=grid, in_specs=in_specs, out_specs=out_specs,
                             out_shape=out_shape, scratch_shapes=scratch_shapes,
                             compiler_params=_cp(("arbitrary",)))(*operands)
        return res, ()
    return _carrier_call(body, carried[0], carried[1], name=name, grid=grid, in_specs=in_specs,
                         out_specs=out_specs, out_shape=out_shape, scratch_shapes=scratch_shapes,
                         operands=operands)


def _all_gather(blocks, name, in_vmem, after=()):
    return _run_comm(_gather_comm(blocks), name, in_vmem, after)


N_CHIPS = 4


def _sibling_comm(contribs):
    n = len(contribs)

    def copies(ins, outs, sems):
        send_sems, recv_sems = sems
        x, y, c = _my_pos()
        return [pltpu.make_async_remote_copy(
            src_ref=ins[a].at[pl.ds(0, N_CHIPS), 1 - c], dst_ref=outs[a],
            send_sem=send_sems.at[a], recv_sem=recv_sems.at[a],
            device_id=(x, y, 1 - c), device_id_type=MESH) for a in range(n)]

    def start(ins, outs, sems):
        for cp in copies(ins, outs, sems):
            cp.start()

    def finish(ins, outs, sems):
        cps = copies(ins, outs, sems)
        for cp in cps:
            cp.wait_recv()
        for cp in cps:
            cp.wait_send()

    return _Comm(contribs, [jax.ShapeDtypeStruct((N_CHIPS,) + b.shape[2:], b.dtype) for b in contribs],
                 [pltpu.SemaphoreType.DMA((n,)), pltpu.SemaphoreType.DMA((n,))], [start, finish])


def _pair_sum(mine, theirs, name):
    _, _, rows, cols = mine.shape
    core = lax.axis_index("c").astype(jnp.int32).reshape(1)

    def body(c_ref, m_ref, t_ref, o_ref):
        o_ref[...] = (m_ref[...].astype(F32) + t_ref[...].astype(F32)).astype(BF16)

    return pl.pallas_call(
        body, name=name,
        grid_spec=pltpu.PrefetchScalarGridSpec(
            num_scalar_prefetch=1, grid=(N_CHIPS,),
            in_specs=[pl.BlockSpec((None, None, rows, cols), lambda q, c: (q, c[0], 0, 0)),
                      pl.BlockSpec((None, rows, cols), lambda q, c: (q, 0, 0))],
            out_specs=pl.BlockSpec((None, rows, cols), lambda q, c: (q, 0, 0))),
        out_shape=jax.ShapeDtypeStruct((N_CHIPS, rows, cols), BF16),
        compiler_params=_cp(("parallel",)),
    )(core, mine, theirs)


def _chip_comm(sums):
    n = len(sums)

    def parts(ins, outs, sems):
        send_sems, recv_sems, local_sems = sems
        x, y, c = _my_pos()
        q_me = 2 * x + y
        chips = [(1 - x, y), (x, 1 - y), (1 - x, 1 - y)]
        mine = [pltpu.make_async_copy(ins[a].at[q_me], outs[a].at[q_me], local_sems.at[a]) for a in range(n)]
        sends, recvs = [], []
        for j, (px, py) in enumerate(chips):
            for a in range(n):
                q = 2 * px + py
                sends.append(pltpu.make_async_remote_copy(
                    src_ref=ins[a].at[q], dst_ref=outs[a].at[q_me],
                    send_sem=send_sems.at[a, j], recv_sem=recv_sems.at[a, j],
                    device_id=(px, py, c), device_id_type=MESH))
                recvs.append(pltpu.make_async_remote_copy(
                    src_ref=ins[a].at[q], dst_ref=outs[a].at[q],
                    send_sem=send_sems.at[a, j], recv_sem=recv_sems.at[a, j],
                    device_id=(x, y, c), device_id_type=MESH))
        return mine, sends, recvs

    def start(ins, outs, sems):
        mine, sends, _ = parts(ins, outs, sems)
        for cp in mine + sends:
            cp.start()

    def finish(ins, outs, sems):
        mine, sends, recvs = parts(ins, outs, sems)
        for cp in recvs:
            cp.wait_recv()
        for cp in sends:
            cp.wait_send()
        for cp in mine:
            cp.wait()

    return _Comm(sums, [jax.ShapeDtypeStruct(b.shape, b.dtype) for b in sums],
                 [pltpu.SemaphoreType.DMA((n, 3)), pltpu.SemaphoreType.DMA((n, 3)), pltpu.SemaphoreType.DMA((n,))],
                 [start, finish])


def _mm_nt(a, bt, *, tm, tn, out_dtype, name, row_off=0, rows=None):
    rows = a.shape[0] if rows is None else rows
    n, k = bt.shape

    def body(a_ref, b_ref, o_ref):
        o_ref[...] = _dot_nt(a_ref[...], b_ref[...]).astype(out_dtype)

    return pl.pallas_call(
        body, name=name, grid=(rows // tm, n // tn),
        in_specs=[pl.BlockSpec((tm, k), lambda i, j: (i + row_off, 0)),
                  pl.BlockSpec((tn, k), lambda i, j: (j, 0))],
        out_specs=pl.BlockSpec((tm, tn), lambda i, j: (i, j)),
        out_shape=jax.ShapeDtypeStruct((rows, n), out_dtype),
        compiler_params=_cp(("parallel", "parallel")),
    )(a, bt)


def _mm_tn(a, b, *, tk, nk, tm, tn, out_dtype, name, a_off=0, b_off=0):
    m, n = a.shape[1], b.shape[1]

    def body(a_ref, b_ref, o_ref, acc):
        kk = pl.program_id(2)

        @pl.when(kk == 0)
        def _():
            acc[...] = jnp.zeros_like(acc)

        acc[...] += _dot_tn(a_ref[...], b_ref[...])

        @pl.when(kk == nk - 1)
        def _():
            o_ref[...] = acc[...].astype(out_dtype)

    return pl.pallas_call(
        body, name=name, grid=(m // tm, n // tn, nk),
        in_specs=[pl.BlockSpec((tk, tm), lambda i, j, kk: (kk + a_off, i)),
                  pl.BlockSpec((tk, tn), lambda i, j, kk: (kk + b_off, j))],
        out_specs=pl.BlockSpec((tm, tn), lambda i, j, kk: (i, j)),
        out_shape=jax.ShapeDtypeStruct((m, n), out_dtype),
        scratch_shapes=[pltpu.VMEM((tm, tn), F32)],
        compiler_params=_cp(("parallel", "parallel", "arbitrary")),
    )(a, b)


def _mm_tn_blocked(a, b, name, steps, carried=None):
    w = a.shape[1] // steps
    n = b.shape[1]

    def body(a_ref, b_ref, o_ref):
        o_ref[...] = _dot_tn(a_ref[...], b_ref[...]).astype(BF16)

    (out,), extra = _pcall(
        body, carried, name=name, grid=(steps,),
        in_specs=[pl.BlockSpec((S, w), lambda j: (0, j)), _full((S, n))],
        out_specs=[pl.BlockSpec((w, n), lambda j: (j, 0))],
        out_shape=[jax.ShapeDtypeStruct((a.shape[1], n), BF16)],
        scratch_shapes=[], operands=[a, b])
    return out, extra


def _prologue(cond, c_ctx, w_ada, b_cols, w_in_t, x, ctx, nw):
    rows_shape = jax.ShapeDtypeStruct((16, ADA_BLK), F32)
    big, g_cond, g_mod = _gather_comm_relayed([w_in_t]), _gather_comm([cond]), _gather_comm([rows_shape])

    def body(cond_ref, cctx_ref, wada_ref, b_ref, nw_ref, win_ref, x_ref, ctx_ref,
             g0_ref, cc_ref, mod_ref, gin_ref, h_ref, hl_ref, rows_ref, g1_ref, x_s, ctx_s, h_s, io_sems, *sems):
        s_big, s_cond, s_mod = sems[0:3], sems[3:6], sems[6:9]
        load_x = pltpu.make_async_copy(x_ref, x_s, io_sems.at[0])
        load_ctx = pltpu.make_async_copy(ctx_ref, ctx_s, io_sems.at[1])
        load_x.start()
        load_ctx.start()
        for phase in g_cond.phases:
            phase([cond_ref], [g0_ref], s_cond)
        big.phases[0]([win_ref], [gin_ref], s_big)
        cc_ref[...] = jnp.zeros_like(cc_ref)
        for j in range(N_DEV):
            cc_ref[j:j + 1, :] = g0_ref[j, 0:1, :]
        cc_ref[N_DEV:N_DEV + 1, :] = cctx_ref[...]
        cv = cc_ref[...]
        rows_ref[...] = _dot_f32(cv * _sigmoid(cv), wada_ref[...]) + b_ref[...]
        for phase in g_mod.phases:
            phase([rows_ref], [g1_ref], s_mod)
        big.phases[1]([win_ref], [gin_ref], s_big)
        x_pos, y_pos, c_pos = _my_pos()
        me = 4 * x_pos + 2 * y_pos + c_pos
        mine = jnp.concatenate([g1_ref[j, pl.ds(me, 1), :] for j in range(N_DEV)], axis=1)
        shared = jnp.concatenate([g1_ref[j, N_DEV:N_DEV + 1, :] for j in range(N_DEV)], axis=1)
        for k in range(6):
            mod_ref[k:k + 1, :] = mine[:, k * D:(k + 1) * D]
        mod_ref[6:7, :] = shared[:, 0:D]
        mod_ref[7:8, :] = shared[:, D:2 * D]
        load_ctx.wait()
        load_x.wait()
        h_s[pl.ds(0, L), :] = _rms_mod(ctx_s[...], nw_ref[...], mod_ref[6:7, :], mod_ref[7:8, :]).astype(BF16)

        def norm_tile(i, carry):
            r0 = pl.multiple_of(i * TM, TM)
            h_s[pl.ds(L + r0, TM), :] = _rms_mod(
                x_s[pl.ds(r0, TM), :], nw_ref[...], mod_ref[0:1, :], mod_ref[1:2, :]).astype(BF16)
            return carry

        lax.fori_loop(0, N_LAT_TILES, norm_tile, 0)
        stores = [pltpu.make_async_copy(h_s, h_ref, io_sems.at[2]),
                  pltpu.make_async_copy(h_s.at[pl.ds(L, S)], hl_ref, io_sems.at[3])]
        for cp in stores:
            cp.start()
        big.phases[2]([win_ref], [gin_ref], s_big)
        for cp in stores:
            cp.wait()

    return pl.pallas_call(
        body, name="prologue",
        in_specs=[_VMEM_WHOLE] * 5 + [_ANY] * 3, out_specs=[_VMEM_WHOLE] * 3 + [_ANY] * 3,
        out_shape=[g_cond.out_shapes[0], jax.ShapeDtypeStruct((16, D), F32), jax.ShapeDtypeStruct((8, D), F32),
                   big.out_shapes[0], jax.ShapeDtypeStruct((T, D), BF16), jax.ShapeDtypeStruct((S, D), BF16)],
        scratch_shapes=[pltpu.VMEM((16, ADA_BLK), F32), pltpu.VMEM((N_DEV, 16, ADA_BLK), F32),
                        pltpu.VMEM((S, D), F32), pltpu.VMEM((L, D), F32), pltpu.VMEM((T, D), BF16),
                        pltpu.SemaphoreType.DMA((4,))] + big.sems + g_cond.sems + g_mod.sems,
        compiler_params=_cp(),
    )(cond, c_ctx, w_ada, b_cols, nw, w_in_t, x, ctx)


def _ada_grads(cc, dm_cols, w_ada):
    def body(c_ref, dm_ref, w_ref, gw_ref, dsc_ref):
        cv = c_ref[...]
        sc = cv * _sigmoid(cv)
        dm = dm_ref[...]
        gw_ref[...] = _dot_f32(sc, dm, dot=_dot_tn)
        dsc_ref[...] = _dot_f32(dm, w_ref[...], dot=_dot_nt)

    return pl.pallas_call(
        body, name="ada_grads",
        in_specs=[_VMEM_WHOLE] * 3, out_specs=[_VMEM_WHOLE] * 2,
        out_shape=[jax.ShapeDtypeStruct((D, ADA_BLK), F32), jax.ShapeDtypeStruct((16, D), F32)],
        compiler_params=_cp(),
    )(cc, dm_cols, w_ada)


def _lat(i):
    return jnp.maximum(i - 1, 0)


def _rms_mod(xv, nw, sh, sc):
    rstd = lax.rsqrt(jnp.mean(xv * xv, axis=-1, keepdims=True) + EPS)
    return (xv * rstd * nw) * (1.0 + sc) + sh


def _rms_mod_bwd(xv, nw, sc, dh):
    rstd = lax.rsqrt(jnp.mean(xv * xv, axis=-1, keepdims=True) + EPS)
    xhat = xv * rstd
    dn = dh * (1.0 + sc)
    dxhat = dn * nw
    dx = rstd * (dxhat - xhat * jnp.mean(dxhat * xhat, axis=-1, keepdims=True))
    return (dx, jnp.sum(dh, axis=0, keepdims=True), jnp.sum(dh * (xhat * nw), axis=0, keepdims=True),
            jnp.sum(dn * xhat, axis=0, keepdims=True))


def _chunk_masks(reverse):
    row = lax.broadcasted_iota(jnp.int32, (TM, TM), 0)
    col = lax.broadcasted_iota(jnp.int32, (TM, TM), 1)
    same = (row // CHUNK) == (col // CHUNK)
    tri = same & ((col >= row) if reverse else (col <= row))
    return same, tri


def _chunk_order(i, reverse):
    if not reverse:
        return i
    return jnp.where(i < N_CTX_CHUNKS, N_CTX_CHUNKS - 1 - i, N_CHUNKS + N_CTX_CHUNKS - 1 - i)


def _decay_terms(z, lb, same01, tri01):
    f = lb + (1.0 - lb) * _sigmoid(z)
    g = jnp.log(f)
    g2 = jnp.concatenate(_split2(g), axis=1)
    b2 = _dot(tri01, g2)
    t2 = _dot(same01, g2)
    return f, 1.0 - f, b2[:, :HG_DIM] + b2[:, HG_DIM:], t2[:, :HG_DIM] + t2[:, HG_DIM:]


def _chunk_outer(a, b):
    n = TM // CHUNK
    return jnp.einsum('ncv,nck->nvk', a.reshape(n, CHUNK, HG_DIM), b.reshape(n, CHUNK, HG_DIM),
                      preferred_element_type=F32)


def _hgrn_fwd(p_a, lbl, carried=None):
    cpt = TM // CHUNK

    def body(p_ref, lbl_ref, o_ref, st_ref, qd_s, kd_s, u_s, v_s, ebt_s):
        masks = [_chunk_masks(d == 1) for d in (0, 1)]
        same01 = jnp.where(masks[0][0], 1.0, 0.0).astype(BF16)
        tri = [m[1] for m in masks]
        tri01 = [jnp.where(t, 1.0, 0.0).astype(BF16) for t in tri]
        lb = [_sigmoid(lbl_ref[d][0:1, :] - lbl_ref[d][1:2, :]) for d in (0, 1)]

        def prep(r, carry):
            r0 = pl.multiple_of(r * TM, TM)
            vb = p_ref[pl.ds(r0, TM), 2 * HG_DIM:3 * HG_DIM].astype(BF16)
            v_s[pl.ds(r0, TM), :] = vb
            for d in (0, 1):
                z = p_ref[pl.ds(r0, TM), d * HG_DIM:(d + 1) * HG_DIM]
                _, k, b, bt = _decay_terms(z, lb[d], same01, tri01[d])
                u_s[d, pl.ds(r * cpt, cpt)] = _chunk_outer(vb, (k * jnp.exp(bt - b)).astype(BF16))
                ebt_s[d, pl.ds(r0, TM), :] = jnp.exp(bt)

                @pl.when(r >= 1)
                def _():
                    rl = pl.multiple_of(r0 - L, TM)
                    qr = p_ref[pl.ds(r0, TM), 3 * HG_DIM:4 * HG_DIM]
                    q = qr * _sigmoid(qr) * HG_DIM ** -0.5
                    qd_s[d, pl.ds(rl, TM), :] = (q * jnp.exp(b)).astype(BF16)
                    kd_s[d, pl.ds(rl, TM), :] = (k * jnp.exp(-b)).astype(BF16)

            return carry

        lax.fori_loop(0, N_TILES, prep, 0)

        def scan(i, sts):
            new = []
            for d in (0, 1):
                nn = _chunk_order(i, d == 1)
                c0 = pl.multiple_of(nn * CHUNK, CHUNK)
                st_ref[d, nn] = sts[d].astype(BF16)
                new.append(sts[d] * ebt_s[d, pl.ds(c0, 1), :] + u_s[d, nn])
            return tuple(new)

        zero = jnp.zeros((HG_DIM, HG_DIM), F32)
        lax.fori_loop(0, N_CHUNKS, scan, (zero, zero))

        def outp(r, carry):
            r0 = pl.multiple_of(r * TM, TM)
            vb = v_s[pl.ds(r0 + L, TM), :]
            o = jnp.zeros((TM, HG_DIM), F32)
            for d in (0, 1):
                qd = qd_s[d, pl.ds(r0, TM), :]
                a = jnp.where(tri[d], _dot_nt(qd, kd_s[d, pl.ds(r0, TM), :]), 0.0)
                stb = st_ref[d, pl.ds(N_CTX_CHUNKS + r * cpt, cpt)]
                inter = jnp.einsum('nck,nvk->ncv', qd.reshape(cpt, CHUNK, HG_DIM), stb,
                                   preferred_element_type=F32)
                o = o + _dot(a.astype(BF16), vb) + inter.reshape(TM, HG_DIM)
            o_ref[pl.ds(r0, TM), :] = o
            return carry

        lax.fori_loop(0, N_LAT_TILES, outp, 0, unroll=2)

    return _pcall(
        body, carried, name="hgrn_fwd", grid=(HG_HEADS,),
        in_specs=[pl.BlockSpec((T, 4 * HG_DIM), lambda h: (0, h)),
                  pl.BlockSpec((2, 2, HG_DIM), lambda h: (0, 0, h))],
        out_specs=[pl.BlockSpec((S, HG_DIM), lambda h: (0, h)),
                   pl.BlockSpec((2, None, N_CHUNKS, HG_DIM, HG_DIM), lambda h: (0, h, 0, 0, 0))],
        out_shape=[jax.ShapeDtypeStruct((S, HGW), F32),
                   jax.ShapeDtypeStruct((2, HG_HEADS, N_CHUNKS, HG_DIM, HG_DIM), BF16)],
        scratch_shapes=[pltpu.VMEM((2, S, HG_DIM), BF16), pltpu.VMEM((2, S, HG_DIM), BF16),
                        pltpu.VMEM((2, N_CHUNKS, HG_DIM, HG_DIM), F32), pltpu.VMEM((T, HG_DIM), BF16),
                        pltpu.VMEM((2, T, HG_DIM), F32)],
        operands=[p_a, lbl])


def _hgrn_bwd(p_a, lbl, d_o, st, carried=None):
    cpt = TM // CHUNK

    def rows(r):
        return r * TM if isinstance(r, int) else pl.multiple_of(r * TM, TM)

    def body(p_ref, lbl_ref, do_ref, st_ref, dp_ref, dlb_ref, b_s, bt_s, dbt_s, qd_s, dst_s, w_s):
        masks = [_chunk_masks(d == 1) for d in (0, 1)]
        same01 = jnp.where(masks[0][0], 1.0, 0.0).astype(BF16)
        tri = [m[1] for m in masks]
        tri01 = [jnp.where(t, 1.0, 0.0).astype(BF16) for t in tri]
        later01 = [tri01[1], tri01[0]]
        lb = [_sigmoid(lbl_ref[d][0:1, :] - lbl_ref[d][1:2, :]) for d in (0, 1)]

        def prep_tile(r, latent):
            r0 = rows(r)
            for d in (0, 1):
                z = p_ref[pl.ds(r0, TM), d * HG_DIM:(d + 1) * HG_DIM]
                _, _, b, bt = _decay_terms(z, lb[d], same01, tri01[d])
                b_s[d, pl.ds(r0, TM), :] = b
                bt_s[d, pl.ds(r0, TM), :] = bt
                if latent:
                    rl = pl.multiple_of(r0 - L, TM)
                    qr = p_ref[pl.ds(r0, TM), 3 * HG_DIM:4 * HG_DIM]
                    qd = (qr * _sigmoid(qr) * HG_DIM ** -0.5 * jnp.exp(b)).astype(BF16)
                    qd_s[d, pl.ds(rl, TM), :] = qd
                    w_s[d, pl.ds(r * cpt, cpt)] = _chunk_outer(
                        do_ref[pl.ds(rl, TM), :].astype(BF16), qd).astype(BF16)

        prep_tile(0, False)
        w_s[:, pl.ds(0, N_CTX_CHUNKS)] = jnp.zeros((2, N_CTX_CHUNKS, HG_DIM, HG_DIM), BF16)

        def prep(r, carry):
            prep_tile(r, True)
            return carry

        lax.fori_loop(1, N_TILES, prep, 0, unroll=2)

        def rscan(j, dsts):
            i = N_CHUNKS - 1 - j
            new = []
            for d in (0, 1):
                nn = _chunk_order(i, d == 1)
                c0 = pl.multiple_of(nn * CHUNK, CHUNK)
                dst_s[d, nn] = dsts[d].astype(BF16)
                after = st_ref[d, _chunk_order(jnp.minimum(i + 1, N_CHUNKS - 1), d == 1)].astype(F32)
                dbt_s[d, pl.ds(c0, CHUNK), :] = jnp.broadcast_to(
                    jnp.sum(after * dsts[d], axis=0, keepdims=True), (CHUNK, HG_DIM))
                new.append(dsts[d] * jnp.exp(bt_s[d, pl.ds(c0, 1), :]) + w_s[d, nn].astype(F32))
            return tuple(new)

        zero = jnp.zeros((HG_DIM, HG_DIM), F32)
        lax.fori_loop(0, N_CHUNKS, rscan, (zero, zero))

        def grad_tile(r, latent):
            r0 = rows(r)
            vb = p_ref[pl.ds(r0, TM), 2 * HG_DIM:3 * HG_DIM].astype(BF16)
            dv = jnp.zeros((TM, HG_DIM), F32)
            dq = jnp.zeros((TM, HG_DIM), F32)
            dlbs = []
            if latent:
                rl = pl.multiple_of(r0 - L, TM)
                qr = p_ref[pl.ds(r0, TM), 3 * HG_DIM:4 * HG_DIM]
                sq = _sigmoid(qr)
                do = do_ref[pl.ds(rl, TM), :].astype(BF16)
                da_full = _dot_nt(do, vb)
            for d in (0, 1):
                z = p_ref[pl.ds(r0, TM), d * HG_DIM:(d + 1) * HG_DIM]
                sz = _sigmoid(z)
                f = lb[d] + (1.0 - lb[d]) * sz
                k = 1.0 - f
                b = b_s[d, pl.ds(r0, TM), :]
                e2 = jnp.exp(bt_s[d, pl.ds(r0, TM), :] - b)
                dstb = dst_s[d, pl.ds(r * cpt, cpt)]
                kd2 = k * e2
                dkd2 = jnp.einsum('ncv,nvk->nck', vb.reshape(cpt, CHUNK, HG_DIM), dstb,
                                  preferred_element_type=F32).reshape(TM, HG_DIM)
                dv = dv + jnp.einsum('nck,nvk->ncv', kd2.astype(BF16).reshape(cpt, CHUNK, HG_DIM), dstb,
                                     preferred_element_type=F32).reshape(TM, HG_DIM)
                dk = dkd2 * e2
                db = -(kd2 * dkd2)
                if latent:
                    eb = jnp.exp(b)
                    enb = jnp.exp(-b)
                    qdf = qr * sq * HG_DIM ** -0.5 * eb
                    kdf = k * enb
                    qd = qd_s[d, pl.ds(rl, TM), :]
                    kd = kdf.astype(BF16)
                    a = jnp.where(tri[d], _dot_nt(qd, kd), 0.0).astype(BF16)
                    da = jnp.where(tri[d], da_full, 0.0).astype(BF16)
                    stb = st_ref[d, pl.ds(r * cpt, cpt)]
                    dqd = _dot(da, kd) + jnp.einsum(
                        'ncv,nvk->nck', do.reshape(cpt, CHUNK, HG_DIM), stb,
                        preferred_element_type=F32).reshape(TM, HG_DIM)
                    dkd = _dot_tn(da, qd)
                    dv = dv + _dot_tn(a, do)
                    dk = dk + dkd * enb
                    db = db + qdf * dqd - kdf * dkd
                    dq = dq + dqd * eb
                dg = _dot_lhs01(later01[d], db) + dbt_s[d, pl.ds(r0, TM), :]
                df = dg / f - dk
                dp_ref[pl.ds(r0, TM), d * HG_DIM:(d + 1) * HG_DIM] = (
                    df * (1.0 - lb[d]) * sz * (1.0 - sz)).astype(BF16)
                dlbs.append(jnp.sum(df * (1.0 - sz), axis=0, keepdims=True))
            dp_ref[pl.ds(r0, TM), 2 * HG_DIM:3 * HG_DIM] = dv.astype(BF16)
            if latent:
                dq = dq * (HG_DIM ** -0.5) * (sq * (1.0 + qr * (1.0 - sq)))
            dp_ref[pl.ds(r0, TM), 3 * HG_DIM:4 * HG_DIM] = dq.astype(BF16)
            return dlbs

        dlb_ctx = grad_tile(0, False)

        def grads(r, acc):
            t = grad_tile(r, True)
            return (acc[0] + t[0], acc[1] + t[1])

        dlb = lax.fori_loop(1, N_TILES, grads, (dlb_ctx[0], dlb_ctx[1]))
        dlb_ref[0:1, :] = dlb[0]
        dlb_ref[1:2, :] = dlb[1]

    return _pcall(
        body, carried, name="hgrn_bwd", grid=(HG_HEADS,),
        in_specs=[pl.BlockSpec((T, 4 * HG_DIM), lambda h: (0, h)),
                  pl.BlockSpec((2, 2, HG_DIM), lambda h: (0, 0, h)),
                  pl.BlockSpec((S, HG_DIM), lambda h: (0, h)),
                  pl.BlockSpec((2, None, N_CHUNKS, HG_DIM, HG_DIM), lambda h: (0, h, 0, 0, 0))],
        out_specs=[pl.BlockSpec((T, 4 * HG_DIM), lambda h: (0, h)),
                   pl.BlockSpec((2, HG_DIM), lambda h: (0, h))],
        out_shape=[jax.ShapeDtypeStruct((T, WA), BF16), jax.ShapeDtypeStruct((2, HGW), F32)],
        scratch_shapes=[pltpu.VMEM((2, T, HG_DIM), F32), pltpu.VMEM((2, T, HG_DIM), F32),
                        pltpu.VMEM((2, T, HG_DIM), F32), pltpu.VMEM((2, S, HG_DIM), BF16),
                        pltpu.VMEM((2, N_CHUNKS, HG_DIM, HG_DIM), BF16),
                        pltpu.VMEM((2, N_CHUNKS, HG_DIM, HG_DIM), BF16)],
        operands=[p_a, lbl, d_o, st])


def _rope_tables():
    t = np.arange(S)
    inv = ROPE_THETA ** (-np.arange(0, 32, 2, dtype=np.float64) / 32)
    lane = np.arange(64)
    pos = np.where(lane[None, :] < 32, (t // GRID_W)[:, None], (t % GRID_W)[:, None]).astype(np.float64)
    ang = pos * inv[(lane % 32) % 16][None, :]
    sign = np.where((lane % 32) < 16, -1.0, 1.0)[None, :]
    cos = np.tile(np.cos(ang), (1, 2)).astype(np.float32)
    sin = np.tile(np.sin(ang) * sign, (1, 2)).astype(np.float32)
    return jnp.asarray(cos), jnp.asarray(sin)


def _rope_partner(v):
    lane = lax.broadcasted_iota(jnp.int32, (1, 128), 1)
    first = (lane % 32) < 16
    slabs = []
    for j in range(v.shape[1] // 128):
        s = v[:, 128 * j:128 * (j + 1)]
        slabs.append(jnp.where(first, pltpu.roll(s, 112, 1), pltpu.roll(s, 16, 1)))
    return slabs[0] if len(slabs) == 1 else jnp.concatenate(slabs, axis=1)


def _group_ones(width, group):
    r = lax.broadcasted_iota(jnp.int32, (width, width), 0)
    c = lax.broadcasted_iota(jnp.int32, (width, width), 1)
    return jnp.where((r // group) == (c // group), 1.0, 0.0).astype(BF16)


def _group_mean(v, ones01, group):
    hi = v.astype(BF16)
    lo = (v - hi.astype(F32)).astype(BF16)
    return (_dot(hi, ones01) + _dot(lo, ones01)) * (1.0 / group)


def _rep_matrix():
    r = lax.broadcasted_iota(jnp.int32, (KVW, ATW), 0)
    c = lax.broadcasted_iota(jnp.int32, (KVW, ATW), 1)
    return jnp.where(r == HEAD_DIM * (c // 256) + c % HEAD_DIM, 1.0, 0.0).astype(BF16)


def _tile_lanes(v, reps):
    return jnp.concatenate([v] * reps, axis=1)


def _prep_fwd(p_b, o, cos, sin, hnw, qnw, knw):
    def body(p_ref, o_ref, cos_ref, sin_ref, hnw_ref, qnw_ref, knw_ref, y_ref, q_ref, k_ref, v_ref):
        i = pl.program_id(0)
        rep = _rep_matrix()
        ones_k = _group_ones(KVW, HEAD_DIM)
        kr = p_ref[:, 1024:1152]
        krstd = lax.rsqrt(_group_mean(kr * kr, ones_k, HEAD_DIM) + EPS)
        kn = kr * krstd * knw_ref[...]
        v_ref[...] = _dot(p_ref[:, 1152:1280].astype(BF16), rep).astype(BF16)

        @pl.when(i == 0)
        def _():
            k_ref[...] = _dot(kn.astype(BF16), rep).astype(BF16)

        @pl.when(i > 0)
        def _():
            cs, sn = cos_ref[...], sin_ref[...]
            kro = kn * cs + _rope_partner(kn) * sn
            k_ref[...] = _dot(kro.astype(BF16), rep).astype(BF16)
            qr = p_ref[:, 512:1024]
            qrstd = lax.rsqrt(_group_mean(qr * qr, _group_ones(ATW, HEAD_DIM), HEAD_DIM) + EPS)
            qn = qr * qrstd * qnw_ref[...]
            qro = qn * _tile_lanes(cs, 4) + _rope_partner(qn) * _tile_lanes(sn, 4)
            q_ref[...] = (qro * HEAD_DIM ** -0.5).astype(BF16)
            ys = []
            for h in range(HG_HEADS):
                oh = o_ref[:, HG_DIM * h:HG_DIM * (h + 1)]
                gh = p_ref[:, HG_DIM * h:HG_DIM * (h + 1)]
                rstd = lax.rsqrt(jnp.mean(oh * oh, axis=-1, keepdims=True) + EPS)
                ys.append(oh * rstd * hnw_ref[...] * (gh * _sigmoid(gh)))
            y_ref[...] = jnp.concatenate(ys, axis=1).astype(BF16)

    return pl.pallas_call(
        body, name="prep_fwd", grid=(N_TILES,),
        in_specs=[pl.BlockSpec((TM, WB), lambda i: (i, 0)),
                  pl.BlockSpec((TM, HGW), lambda i: (_lat(i), 0)),
                  pl.BlockSpec((TM, 128), lambda i: (_lat(i), 0)),
                  pl.BlockSpec((TM, 128), lambda i: (_lat(i), 0)),
                  _full((1, HG_DIM)), _full((1, ATW)), _full((1, KVW))],
        out_specs=[pl.BlockSpec((TM, HGW), lambda i: (_lat(i), 0)),
                   pl.BlockSpec((TM, ATW), lambda i: (_lat(i), 0)),
                   pl.BlockSpec((TM, ATW), lambda i: (i, 0)),
                   pl.BlockSpec((TM, ATW), lambda i: (i, 0))],
        out_shape=[jax.ShapeDtypeStruct((S, HGW), BF16), jax.ShapeDtypeStruct((S, ATW), BF16),
                   jax.ShapeDtypeStruct((T, ATW), BF16), jax.ShapeDtypeStruct((T, ATW), BF16)],
        compiler_params=_cp(("arbitrary",)),
    )(p_b, o, cos, sin, hnw, qnw, knw)


def _prep_bwd(p_b, o, cos, sin, hnw, qnw, knw, dy_hg, dq, dk_rep, dv_rep, carried=None):
    def body(p_ref, o_ref, cos_ref, sin_ref, hnw_ref, qnw_ref, knw_ref, dy_ref, dq_ref, dk_ref, dv_ref,
             dp_ref, do_ref, acc_ref):
        i = pl.program_id(0)

        @pl.when(i == 0)
        def _():
            acc_ref[...] = jnp.zeros_like(acc_ref)

        rep = _rep_matrix()
        ones_k = _group_ones(KVW, HEAD_DIM)

        def fold(v):
            hi = v.astype(BF16)
            lo = (v - hi.astype(F32)).astype(BF16)
            return _dot_nt(hi, rep) + _dot_nt(lo, rep)

        kr = p_ref[:, 1024:1152]
        krstd = lax.rsqrt(_group_mean(kr * kr, ones_k, HEAD_DIM) + EPS)
        khat = kr * krstd
        kw = knw_ref[...]
        dkro = fold(dk_ref[...])
        dv = fold(dv_ref[...])

        def k_back(dkn):
            dkhat = dkn * kw
            dkr = krstd * (dkhat - khat * _group_mean(dkhat * khat, ones_k, HEAD_DIM))
            acc_ref[2:3, 0:KVW] += jnp.sum(dkn * khat, axis=0, keepdims=True)
            dp_ref[:, 1024:1152] = dkr.astype(BF16)
            dp_ref[:, 1152:1280] = dv.astype(BF16)

        @pl.when(i == 0)
        def _():
            k_back(dkro)
            dp_ref[:, 0:1024] = jnp.zeros((TM, 1024), BF16)

        @pl.when(i > 0)
        def _():
            cs, sn = cos_ref[...], sin_ref[...]
            k_back(dkro * cs + _rope_partner(dkro * sn))
            ones_q = _group_ones(ATW, HEAD_DIM)
            qr = p_ref[:, 512:1024]
            qrstd = lax.rsqrt(_group_mean(qr * qr, ones_q, HEAD_DIM) + EPS)
            qhat = qr * qrstd
            dqro = dq_ref[...] * HEAD_DIM ** -0.5
            dqn = dqro * _tile_lanes(cs, 4) + _rope_partner(dqro * _tile_lanes(sn, 4))
            dqhat = dqn * qnw_ref[...]
            dqr = qrstd * (dqhat - qhat * _group_mean(dqhat * qhat, ones_q, HEAD_DIM))
            acc_ref[1:2, :] += jnp.sum(dqn * qhat, axis=0, keepdims=True)
            dp_ref[:, 512:1024] = dqr.astype(BF16)
            dws = jnp.zeros((1, HG_DIM), F32)
            for h in range(HG_HEADS):
                sl = slice(HG_DIM * h, HG_DIM * (h + 1))
                oh, gh, dy = o_ref[:, sl], p_ref[:, sl], dy_ref[:, sl]
                rstd = lax.rsqrt(jnp.mean(oh * oh, axis=-1, keepdims=True) + EPS)
                ohat = oh * rstd
                sg = _sigmoid(gh)
                dp_ref[:, sl] = (dy * (ohat * hnw_ref[...]) * (sg * (1.0 + gh * (1.0 - sg)))).astype(BF16)
                dn = dy * (gh * sg)
                dws = dws + jnp.sum(dn * ohat, axis=0, keepdims=True)
                dohat = dn * hnw_ref[...]
                do_ref[:, sl] = rstd * (dohat - ohat * jnp.mean(dohat * ohat, axis=-1, keepdims=True))
            acc_ref[0:1, 0:HG_DIM] += dws

    return _pcall(
        body, carried, name="prep_bwd", grid=(N_TILES,),
        in_specs=[pl.BlockSpec((TM, WB), lambda i: (i, 0)),
                  pl.BlockSpec((TM, HGW), lambda i: (_lat(i), 0)),
                  pl.BlockSpec((TM, 128), lambda i: (_lat(i), 0)),
                  pl.BlockSpec((TM, 128), lambda i: (_lat(i), 0)),
                  _full((1, HG_DIM)), _full((1, ATW)), _full((1, KVW)),
                  pl.BlockSpec((TM, HGW), lambda i: (_lat(i), 0)),
                  pl.BlockSpec((TM, ATW), lambda i: (_lat(i), 0)),
                  pl.BlockSpec((TM, ATW), lambda i: (i, 0)),
                  pl.BlockSpec((TM, ATW), lambda i: (i, 0))],
        out_specs=[pl.BlockSpec((TM, WB), lambda i: (i, 0)),
                   pl.BlockSpec((TM, HGW), lambda i: (_lat(i), 0)),
                   _full((8, ATW))],
        out_shape=[jax.ShapeDtypeStruct((T, WB), BF16), jax.ShapeDtypeStruct((S, HGW), F32),
                   jax.ShapeDtypeStruct((8, ATW), F32)],
        scratch_shapes=[], operands=[p_b, o, cos, sin, hnw, qnw, knw, dy_hg, dq, dk_rep, dv_rep])


NEG = -1e30
_CTX_BLOCKS = L // BLOCK


def _attn_window_specs():
    prev = pl.BlockSpec((BLOCK, ATW), lambda i: (jnp.maximum(i - 1, 0) + _CTX_BLOCKS, 0))
    own = pl.BlockSpec((BLOCK, ATW), lambda i: (i + _CTX_BLOCKS, 0))
    nxt = pl.BlockSpec((BLOCK, ATW), lambda i: (jnp.minimum(i + 1, N_BLOCKS - 1) + _CTX_BLOCKS, 0))
    return [prev, own, nxt, _full((L, ATW))]


def _attn_valid(i, heads, context):
    n_keys = 3 * BLOCK + (L if context else 0)
    qi = lax.broadcasted_iota(jnp.int32, (heads * BLOCK, n_keys), 0) % BLOCK
    kj = lax.broadcasted_iota(jnp.int32, (heads * BLOCK, n_keys), 1)
    window = ((jnp.abs(kj - BLOCK - qi) <= BLOCK) & ((kj >= BLOCK) | (i > 0))
              & ((kj < 2 * BLOCK) | (i < N_BLOCKS - 1)))
    return window | (kj >= 3 * BLOCK)


def _stack_heads(qg):
    lane = lax.broadcasted_iota(jnp.int32, (1, 256), 1) // HEAD_DIM
    return jnp.concatenate([jnp.where(lane == g, qg, jnp.zeros_like(qg)) for g in range(4)], axis=0)


def _unstack_heads(v4):
    lane = lax.broadcasted_iota(jnp.int32, (1, 256), 1) // HEAD_DIM
    out = jnp.where(lane == 0, v4[0:BLOCK], 0.0)
    for g in range(1, 4):
        out = out + jnp.where(lane == g, v4[g * BLOCK:(g + 1) * BLOCK], 0.0)
    return out


def _sink_rows(sink_ref, hk):
    return jnp.concatenate(
        [jnp.broadcast_to(sink_ref[0:1, 4 * hk + g:4 * hk + g + 1], (BLOCK, 1)) for g in range(4)], axis=0)


def _attn_fwd(q, k_rep, v_rep, sinks, carried=None):
    def body(q_ref, kp, ko, kn, kc, vp, vo, vn, vc, sink_ref, y_ref, lse_ref):
        i = pl.program_id(0)
        valid = _attn_valid(i, 1, True)
        lane8 = lax.broadcasted_iota(jnp.int32, (1, ATT_HEADS), 1)
        head_of_lane = lax.broadcasted_iota(jnp.int32, (1, 256), 1) // HEAD_DIM
        lse_out = jnp.zeros((BLOCK, ATT_HEADS), F32)
        for hk in range(KV_HEADS):
            sl = slice(256 * hk, 256 * (hk + 1))
            qg = q_ref[:, sl]
            keys = jnp.concatenate([kp[:, sl], ko[:, sl], kn[:, sl], kc[:, sl]], axis=0)
            vals = jnp.concatenate([vp[:, sl], vo[:, sl], vn[:, sl], vc[:, sl]], axis=0)
            yg = jnp.zeros((BLOCK, 256), F32)
            for g in range(4):
                q1 = jnp.where(head_of_lane == g, qg, jnp.zeros_like(qg))
                s = jnp.where(valid, _dot_nt(q1, keys), NEG)
                sink = sink_ref[0:1, 4 * hk + g:4 * hk + g + 1]
                m = jnp.maximum(jnp.max(s, axis=1, keepdims=True), sink)
                p = jnp.exp(s - m)
                den = jnp.sum(p, axis=1, keepdims=True) + jnp.exp(sink - m)
                o1 = _dot(p.astype(BF16), vals) * (1.0 / den)
                yg = yg + jnp.where(head_of_lane == g, o1, 0.0)
                lse_out = lse_out + jnp.where(lane8 == 4 * hk + g, m + jnp.log(den), 0.0)
            y_ref[:, sl] = yg.astype(BF16)
        lse_ref[...] = lse_out

    return _pcall(
        body, carried, name="attn_fwd", grid=(N_BLOCKS,),
        in_specs=[pl.BlockSpec((BLOCK, ATW), lambda i: (i, 0))] + _attn_window_specs()
        + _attn_window_specs() + [_full((1, ATT_HEADS))],
        out_specs=[pl.BlockSpec((BLOCK, ATW), lambda i: (i, 0)),
                   pl.BlockSpec((BLOCK, ATT_HEADS), lambda i: (i, 0))],
        out_shape=[jax.ShapeDtypeStruct((S, ATW), BF16), jax.ShapeDtypeStruct((S, ATT_HEADS), F32)],
        scratch_shapes=[],
        operands=[q, k_rep, k_rep, k_rep, k_rep, v_rep, v_rep, v_rep, v_rep, sinks])


def _attn_bwd(q, k_rep, v_rep, sinks, y_at, lse, dy, carried=None):
    def body(q_ref, kp, ko, kn, kc, vp, vo, vn, vc, sink_ref, y_ref, lse_ref, dy_ref,
             dq_ref, dk_ref, dv_ref, dsink_ref, dk_acc, dv_acc):
        i = pl.program_id(0)

        @pl.when(i == 0)
        def _():
            dk_acc[...] = jnp.zeros_like(dk_acc)
            dv_acc[...] = jnp.zeros_like(dv_acc)
            dk_ref[pl.ds(0, L), :] = jnp.zeros((L, ATW), F32)
            dv_ref[pl.ds(0, L), :] = jnp.zeros((L, ATW), F32)
            dsink_ref[...] = jnp.zeros_like(dsink_ref)

        valid = _attn_valid(i, 4, False)
        lane8 = lax.broadcasted_iota(jnp.int32, (1, ATT_HEADS), 1)
        w0 = pl.multiple_of(i * BLOCK, BLOCK)
        dsink = jnp.zeros((1, ATT_HEADS), F32)
        for hk in range(KV_HEADS):
            sl = slice(256 * hk, 256 * (hk + 1))
            q4 = _stack_heads(q_ref[:, sl])
            do4f = _stack_heads(dy_ref[:, sl])
            o4 = _stack_heads(y_ref[:, sl]).astype(F32)
            do4 = do4f.astype(BF16)
            kl = jnp.concatenate([kp[:, sl], ko[:, sl], kn[:, sl]], axis=0)
            vl = jnp.concatenate([vp[:, sl], vo[:, sl], vn[:, sl]], axis=0)
            lse4 = jnp.concatenate(
                [jnp.sum(jnp.where(lane8 == 4 * hk + g, lse_ref[...], 0.0), axis=1, keepdims=True)
                 for g in range(4)], axis=0)
            p_loc = jnp.where(valid, jnp.exp(_dot_nt(q4, kl) - lse4), 0.0)
            p_ctx = jnp.exp(_dot_nt(q4, kc[:, sl]) - lse4)
            delta = jnp.sum(do4f * o4, axis=1, keepdims=True)
            ds_loc = (p_loc * (_dot_nt(do4, vl) - delta)).astype(BF16)
            ds_ctx = (p_ctx * (_dot_nt(do4, vc[:, sl]) - delta)).astype(BF16)
            dq_ref[:, sl] = _unstack_heads(_dot(ds_loc, kl) + _dot(ds_ctx, kc[:, sl]))
            dk_acc[pl.ds(w0, 3 * BLOCK), sl] += _dot_tn(ds_loc, q4)
            dv_acc[pl.ds(w0, 3 * BLOCK), sl] += _dot_tn(p_loc.astype(BF16), do4)
            dk_ref[pl.ds(0, L), sl] += _dot_tn(ds_ctx, q4)
            dv_ref[pl.ds(0, L), sl] += _dot_tn(p_ctx.astype(BF16), do4)
            p_sink = jnp.exp(_sink_rows(sink_ref, hk) - lse4)
            for g in range(4):
                rows = slice(g * BLOCK, (g + 1) * BLOCK)
                dsink = dsink + jnp.where(lane8 == 4 * hk + g,
                                          -jnp.sum(p_sink[rows] * delta[rows], axis=0, keepdims=True), 0.0)
        dsink_ref[...] += dsink

        @pl.when(i == N_BLOCKS - 1)
        def _():
            dk_ref[pl.ds(L, S), :] = dk_acc[pl.ds(BLOCK, S), :]
            dv_ref[pl.ds(L, S), :] = dv_acc[pl.ds(BLOCK, S), :]

    row_q = pl.BlockSpec((BLOCK, ATW), lambda i: (i, 0))
    return _pcall(
        body, carried, name="attn_bwd", grid=(N_BLOCKS,),
        in_specs=[row_q] + _attn_window_specs() + _attn_window_specs()
        + [_full((1, ATT_HEADS)), row_q, pl.BlockSpec((BLOCK, ATT_HEADS), lambda i: (i, 0)), row_q],
        out_specs=[row_q, _full((T, ATW)), _full((T, ATW)), _full((1, ATT_HEADS))],
        out_shape=[jax.ShapeDtypeStruct((S, ATW), F32), jax.ShapeDtypeStruct((T, ATW), F32),
                   jax.ShapeDtypeStruct((T, ATW), F32), jax.ShapeDtypeStruct((1, ATT_HEADS), F32)],
        scratch_shapes=[pltpu.VMEM((S + 2 * BLOCK, ATW), F32), pltpu.VMEM((S + 2 * BLOCK, ATW), F32)],
        operands=[q, k_rep, k_rep, k_rep, k_rep, v_rep, v_rep, v_rep, v_rep, sinks, y_at, lse, dy])


def _merge_fwd(y_hg, y_at, p_c, x, w_bh, w_ba, w_out, g1, nfw, sh2, sc2, carried=None):
    def body(yh_ref, ya_ref, g_ref, x_ref, wbh_ref, wba_ref, wo_ref, g1_ref, nfw_ref, sh_ref, sc_ref,
             mx_ref, r_ref, x1_ref, h2_ref):
        a = _dot_nt(yh_ref[...], wbh_ref[...])
        b = _dot_nt(ya_ref[...], wba_ref[...])
        mixed = (_sigmoid(g_ref[:, :D]) * a + _sigmoid(g_ref[:, D:]) * b).astype(BF16)
        r = _dot(mixed, wo_ref[...])
        x1 = x_ref[...] + g1_ref[...] * r
        mx_ref[...] = mixed
        r_ref[...] = r
        x1_ref[...] = x1
        h2_ref[...] = _rms_mod(x1, nfw_ref[...], sh_ref[...], sc_ref[...]).astype(BF16)

    row = lambda w: pl.BlockSpec((TM, w), lambda i: (i, 0))
    vec = _full((1, D))
    return _pcall(
        body, carried, name="merge_fwd", grid=(N_LAT_TILES,),
        in_specs=[row(HGW), row(ATW), row(WC), row(D), _VMEM_WHOLE, _VMEM_WHOLE, _VMEM_WHOLE,
                  vec, vec, vec, vec],
        out_specs=[row(D)] * 4,
        out_shape=[jax.ShapeDtypeStruct((S, D), dt) for dt in (BF16, F32, F32, BF16)],
        scratch_shapes=[], operands=[y_hg, y_at, p_c, x, w_bh, w_ba, w_out, g1, nfw, sh2, sc2])


def _merge_bwd(dx1, r, y_hg, y_at, p_c, w_bh, w_ba, w_out, g1, carried=None):
    def body(dx_ref, r_ref, yh_ref, ya_ref, g_ref, wbh_ref, wba_ref, wo_ref, g1_ref,
             dr_ref, da_ref, db_ref, dg_ref, dyh_ref, dya_ref, acc_ref):
        @pl.when(pl.program_id(0) == 0)
        def _():
            acc_ref[...] = jnp.zeros_like(acc_ref)

        dx1v = dx_ref[...]
        acc_ref[0:1, :] += jnp.sum(dx1v * r_ref[...], axis=0, keepdims=True)
        dr = (g1_ref[...] * dx1v).astype(BF16)
        dr_ref[...] = dr
        dmix = _dot_nt(dr, wo_ref[...])
        sh, sa = _sigmoid(g_ref[:, :D]), _sigmoid(g_ref[:, D:])
        da = (dmix * sh).astype(BF16)
        db = (dmix * sa).astype(BF16)
        da_ref[...] = da
        db_ref[...] = db
        dg_ref[:, :D] = (dmix * _dot_nt(yh_ref[...], wbh_ref[...]) * sh * (1.0 - sh)).astype(BF16)
        dg_ref[:, D:] = (dmix * _dot_nt(ya_ref[...], wba_ref[...]) * sa * (1.0 - sa)).astype(BF16)
        dyh_ref[...] = _dot(da, wbh_ref[...])
        dya_ref[...] = _dot(db, wba_ref[...])

    row = lambda w: pl.BlockSpec((TM, w), lambda i: (i, 0))
    return _pcall(
        body, carried, name="merge_bwd", grid=(N_LAT_TILES,),
        in_specs=[row(D), row(D), row(HGW), row(ATW), row(WC), _VMEM_WHOLE, _VMEM_WHOLE, _VMEM_WHOLE,
                  _full((1, D))],
        out_specs=[row(D), row(D), row(D), row(WC), row(HGW), row(ATW), _full((8, D))],
        out_shape=[jax.ShapeDtypeStruct((S, D), BF16), jax.ShapeDtypeStruct((S, D), BF16),
                   jax.ShapeDtypeStruct((S, D), BF16), jax.ShapeDtypeStruct((S, WC), BF16),
                   jax.ShapeDtypeStruct((S, HGW), F32), jax.ShapeDtypeStruct((S, ATW), F32),
                   jax.ShapeDtypeStruct((8, D), F32)],
        scratch_shapes=[], operands=[dx1, r, y_hg, y_at, p_c, w_bh, w_ba, w_out, g1])


def _ffn_fused(x1, h2, tgt, w_gate, w_up, w_down, g2, nfw, sc2):
    def body(x1_ref, h2_ref, t_ref, wg_ref, wu_ref, wd_ref, g2_ref, nfw_ref, sc_ref,
             act_ref, dgt_ref, dup_ref, df_ref, dx_ref, acc_ref, gs, us):
        @pl.when(pl.program_id(0) == 0)
        def _():
            acc_ref[...] = jnp.zeros_like(acc_ref)

        h2 = h2_ref[...]
        whole = lambda w_ref: w_ref[...].reshape(D_FF, D)
        tile = lambda j: slice(j * FF_TILE, (j + 1) * FF_TILE)
        for j in range(N_FF_TILES):
            g = _dot_nt(h2, wg_ref[j])
            u = _dot_nt(h2, wu_ref[j])
            gs[j] = g
            us[j] = u
            act_ref[:, tile(j)] = (g * _sigmoid(g) * u).astype(BF16)
        f = _dot(act_ref[...], whole(wd_ref))
        x1v = x1_ref[...]
        g2 = g2_ref[...]
        diff = x1v + g2 * f - t_ref[...]
        dy = diff * (1.0 / D)
        df = (g2 * dy).astype(BF16)
        df_ref[...] = df
        dact_all = _dot_nt(df, whole(wd_ref))
        for j in range(N_FF_TILES):
            g, u = gs[j], us[j]
            sg = _sigmoid(g)
            dact = dact_all[:, tile(j)]
            dgt_ref[:, tile(j)] = (dact * u * (sg * (1.0 + g * (1.0 - sg)))).astype(BF16)
            dup_ref[:, tile(j)] = (dact * (g * sg)).astype(BF16)
        dh2 = _dot(dgt_ref[...], whole(wg_ref)) + _dot(dup_ref[...], whole(wu_ref))
        dx, dsh, dsc, dnw = _rms_mod_bwd(x1v, nfw_ref[...], sc_ref[...], dh2)
        dx_ref[...] = dy + dx
        acc_ref[0:1, :] += dsh
        acc_ref[1:2, :] += dsc
        acc_ref[2:3, :] += dnw
        acc_ref[3:4, :] += jnp.sum(dy * f, axis=0, keepdims=True)
        acc_ref[4:5, :] += 0.5 * jnp.sum(jnp.sum(diff * diff, axis=1, keepdims=True), axis=0,
                                         keepdims=True) * (1.0 / D)

    row = lambda dt_w: pl.BlockSpec((TM, dt_w), lambda i: (i, 0))
    blk = row(D_FF)
    vec = _full((1, D))
    return pl.pallas_call(
        body, name="ffn_fused", grid=(N_LAT_TILES,),
        in_specs=[row(D), row(D), row(D), _VMEM_WHOLE, _VMEM_WHOLE, _VMEM_WHOLE, vec, vec, vec],
        out_specs=[blk, blk, blk, row(D), row(D), _full((8, D))],
        out_shape=[jax.ShapeDtypeStruct((S, D_FF), BF16)] * 3
        + [jax.ShapeDtypeStruct((S, D), BF16), jax.ShapeDtypeStruct((S, D), F32),
           jax.ShapeDtypeStruct((8, D), F32)],
        scratch_shapes=[pltpu.VMEM((N_FF_TILES, TM, FF_TILE), F32), pltpu.VMEM((N_FF_TILES, TM, FF_TILE), F32)],
        compiler_params=_cp(("arbitrary",)),
    )(x1, h2, tgt, w_gate, w_up, w_down, g2, nfw, sc2)


def _proj_bc(h_all, w_b, w_c, carried=None):
    def body(h_ref, wb_ref, wc_ref, pb_ref, pc_ref):
        h = h_ref[...]
        pb_ref[...] = _dot_nt(h, wb_ref[...])

        @pl.when(pl.program_id(0) > 0)
        def _():
            pc_ref[...] = _dot_nt(h, wc_ref[...])

    return _pcall(
        body, carried, name="proj_bc", grid=(N_TILES,),
        in_specs=[pl.BlockSpec((TM, D), lambda i: (i, 0)), _VMEM_WHOLE, _VMEM_WHOLE],
        out_specs=[pl.BlockSpec((TM, WB), lambda i: (i, 0)), pl.BlockSpec((TM, WC), lambda i: (_lat(i), 0))],
        out_shape=[jax.ShapeDtypeStruct((T, WB), F32), jax.ShapeDtypeStruct((S, WC), F32)],
        scratch_shapes=[], operands=[h_all, w_b, w_c])


def _input_bwd(dp_a, dp_b, dp_c, w_a, w_b, w_c, ctx, x, dx1, nw, sh, sc, carried=None):
    def body(da_ref, db_ref, dc_ref, wa_ref, wb_ref, wc_ref, ctx_ref, x_ref, dx1_ref, nw_ref, sh_ref,
             sc_ref, gx_ref, acc_ref):
        i = pl.program_id(0)

        @pl.when(i == 0)
        def _():
            acc_ref[...] = jnp.zeros_like(acc_ref)

        dh = _dot(da_ref[...], wa_ref[...]) + _dot(db_ref[...], wb_ref[...])

        @pl.when(i == 0)
        def _():
            _, dsh, dsc, dnw = _rms_mod_bwd(ctx_ref[...], nw_ref[...], sc_ref[0:1, :], dh)
            acc_ref[3:4, :] += dsh
            acc_ref[4:5, :] += dsc
            acc_ref[2:3, :] += dnw

        @pl.when(i > 0)
        def _():
            dhl = dh + _dot(dc_ref[...], wc_ref[...])
            dx, dsh, dsc, dnw = _rms_mod_bwd(x_ref[...], nw_ref[...], sc_ref[1:2, :], dhl)
            gx_ref[...] = dx1_ref[...] + dx
            acc_ref[0:1, :] += dsh
            acc_ref[1:2, :] += dsc
            acc_ref[2:3, :] += dnw

    lat = lambda w: pl.BlockSpec((TM, w), lambda i: (_lat(i), 0))
    return _pcall(
        body, carried, name="input_bwd", grid=(N_TILES,),
        in_specs=[pl.BlockSpec((TM, WA), lambda i: (i, 0)), pl.BlockSpec((TM, WB), lambda i: (i, 0)),
                  lat(WC), _VMEM_WHOLE, _VMEM_WHOLE, _VMEM_WHOLE, _full((TM, D)), lat(D), lat(D),
                  _full((1, D)), _full((2, D)), _full((2, D))],
        out_specs=[lat(D), _full((8, D))],
        out_shape=[jax.ShapeDtypeStruct((S, D), F32), jax.ShapeDtypeStruct((8, D), F32)],
        scratch_shapes=[], operands=[dp_a, dp_b, dp_c, w_a, w_b, w_c, ctx, x, dx1, nw, sh, sc])


_C1 = 1.0 - ADAM_B1 ** ADAM_STEP
_C2 = 1.0 - ADAM_B2 ** ADAM_STEP


def _adamw_math(w, g, m, v):
    m = ADAM_B1 * m + (1.0 - ADAM_B1) * g
    v = ADAM_B2 * v + (1.0 - ADAM_B2) * (g * g)
    m_hat = m / _C1
    v_hat = v / _C2
    delta = -ADAM_LR * (m_hat / (jnp.sqrt(v_hat) + ADAM_EPS) + ADAM_WD * w)
    return delta, m, v


def _adamw_sharded(terms, w, m, v, name, tr, extra=None, after=None):
    rows, cols = w.shape

    def body(*refs):
        t_ref, w_ref, m_ref, v_ref = refs[:4]
        g_ref, d_ref, nm_ref, nv_ref = refs[-4:]
        g = t_ref[0].astype(F32)
        for s in range(1, N_CHIPS):
            g = g + t_ref[s].astype(F32)
        if extra is not None:
            g = g + refs[4][...].astype(F32)
        g_ref[...] = g
        d_ref[...], nm_ref[...], nv_ref[...] = _adamw_math(w_ref[...], g, m_ref[...], v_ref[...])

    blk = pl.BlockSpec((tr, cols), lambda i: (i, 0))
    return pl.pallas_call(
        body, name=name, grid=(rows // tr,),
        in_specs=[pl.BlockSpec((N_CHIPS, tr, cols), lambda i: (0, i, 0)), blk, blk, blk]
        + ([blk] if extra is not None else []) + ([_ANY] if after is not None else []),
        out_specs=[blk] * 4,
        out_shape=[jax.ShapeDtypeStruct((rows, cols), F32)] * 4,
        compiler_params=_cp(("parallel",)),
    )(terms, w, m, v, *([extra] if extra is not None else []), *([after] if after is not None else []))


def _adamw_plain(g, w, m, v, name, tr=None):
    def body(g_ref, w_ref, m_ref, v_ref, d_ref, nm_ref, nv_ref):
        d_ref[...], nm_ref[...], nv_ref[...] = _adamw_math(w_ref[...], g_ref[...], m_ref[...], v_ref[...])

    if tr is None:
        return pl.pallas_call(
            body, name=name, in_specs=[_VMEM_WHOLE] * 4, out_specs=[_VMEM_WHOLE] * 3,
            out_shape=[jax.ShapeDtypeStruct(w.shape, F32)] * 3,
            compiler_params=_cp(),
        )(g, w, m, v)
    blk = pl.BlockSpec((tr, w.shape[1]), lambda i: (i, 0))
    return pl.pallas_call(
        body, name=name, grid=(w.shape[0] // tr,), in_specs=[blk] * 4, out_specs=[blk] * 3,
        out_shape=[jax.ShapeDtypeStruct(w.shape, F32)] * 3,
        compiler_params=_cp(("parallel",)),
    )(g, w, m, v)


SMALL_ROWS = 16
R_DMOD, R_DCTX, R_NMIX, R_NFFN, R_MISC, R_DLB, R_BADA01 = 0, 6, 8, 9, 10, 11, 13
M_HNW, M_QNW, M_KNW, M_SINK, M_LOSS = 0, 128, 256, 384, 512


def _pack_small(acc_in, acc_mg, acc_ffn, acc_prep, dsink, dlb):
    def body(in_ref, mg_ref, ff_ref, pp_ref, ds_ref, dlb_ref, o_ref):
        o_ref[...] = jnp.zeros_like(o_ref)
        o_ref[0:2, :] = in_ref[0:2, :]
        o_ref[2:3, :] = mg_ref[0:1, :]
        o_ref[3:5, :] = ff_ref[0:2, :]
        o_ref[5:6, :] = ff_ref[3:4, :]
        o_ref[6:8, :] = in_ref[3:5, :]
        o_ref[8:9, :] = in_ref[2:3, :]
        o_ref[9:10, :] = ff_ref[2:3, :]
        o_ref[10:11, M_HNW:M_HNW + HG_DIM] = pp_ref[0:1, 0:HG_DIM]
        r = lax.broadcasted_iota(jnp.int32, (ATW, 128), 0)
        c = lax.broadcasted_iota(jnp.int32, (ATW, 128), 1)
        fold = jnp.where((r % HEAD_DIM == c) & (c < HEAD_DIM), 1.0, 0.0).astype(BF16)
        qk = jnp.concatenate([pp_ref[1:2, :], pp_ref[2:3, :], jnp.zeros((6, ATW), F32)], axis=0)
        folded = _dot_exact_rhs01(qk, fold)
        o_ref[10:11, M_QNW:M_QNW + 128] = folded[0:1, :]
        o_ref[10:11, M_KNW:M_KNW + 128] = folded[1:2, :]
        o_ref[10:11, M_SINK:M_SINK + ATT_HEADS] = ds_ref[...]
        o_ref[10:11, M_LOSS:M_LOSS + 128] = ff_ref[4:5, 0:128]
        o_ref[11:13, 0:HGW] = dlb_ref[...]

    return pl.pallas_call(
        body, name="pack_small", in_specs=[_VMEM_WHOLE] * 6, out_specs=_VMEM_WHOLE,
        out_shape=jax.ShapeDtypeStruct((SMALL_ROWS, D), F32), compiler_params=_cp(),
    )(acc_in, acc_mg, acc_ffn, acc_prep, dsink, dlb)


def _sum_small(gathered):
    def body(g_ref, o_ref):
        tot = g_ref[0]
        for s in range(1, N_DEV):
            tot = tot + g_ref[s]
        o_ref[...] = tot
        o_ref[R_BADA01:R_BADA01 + 2, :] = tot[0:2, :] + tot[R_DCTX:R_DCTX + 2, :]

    return pl.pallas_call(
        body, name="sum_small", in_specs=[_VMEM_WHOLE], out_specs=_VMEM_WHOLE,
        out_shape=jax.ShapeDtypeStruct((SMALL_ROWS, D), F32), compiler_params=_cp(),
    )(gathered)


_REP_NAMES = ("b_ada", "c_ctx", "norm_mix_w", "norm_ffn_w", "hgrn_norm_w", "q_norm_w", "k_norm_w", "attn_sinks")


def _adamw_replicated(tot, g_c_ctx, ws, ms, vs):
    n = len(_REP_NAMES)

    def body(*refs):
        tot_ref, gc_ref = refs[0], refs[1]
        w_refs, m_refs, v_refs = refs[2:2 + n], refs[2 + n:2 + 2 * n], refs[2 + 2 * n:2 + 3 * n]
        outs = refs[2 + 3 * n:]
        row = lambda r: tot_ref[r:r + 1, :]
        misc = row(R_MISC)
        grads = [jnp.concatenate([row(R_BADA01), row(R_BADA01 + 1)] + [row(k) for k in range(2, 6)], axis=1),
                 gc_ref[...], row(R_NMIX), row(R_NFFN),
                 misc[:, M_HNW:M_HNW + HG_DIM], misc[:, M_QNW:M_QNW + HEAD_DIM],
                 misc[:, M_KNW:M_KNW + HEAD_DIM], misc[:, M_SINK:M_SINK + ATT_HEADS]]
        for k in range(n):
            outs[k][...] = grads[k]
            outs[n + k][...], outs[2 * n + k][...], outs[3 * n + k][...] = _adamw_math(
                w_refs[k][...], grads[k], m_refs[k][...], v_refs[k][...])

    shapes = [jax.ShapeDtypeStruct(w.shape, F32) for w in ws]
    return pl.pallas_call(
        body, name="adamw_replicated", in_specs=[_VMEM_WHOLE] * (2 + 3 * n), out_specs=[_VMEM_WHOLE] * (4 * n),
        out_shape=shapes * 4, compiler_params=_cp(),
    )(tot, g_c_ctx, *ws, *ms, *vs)


def _lb_grads(dlb, lbl):
    def body(d_ref, l_ref, o_ref):
        for d in (0, 1):
            ll = l_ref[d]
            lb = _sigmoid(ll[0:1, :] - ll[1:2, :])
            t = d_ref[d:d + 1, :] * lb * (1.0 - lb)
            o_ref[d, 0:1, :] = t
            o_ref[d, 1:2, :] = -t

    return pl.pallas_call(
        body, name="lb_grads", in_specs=[_VMEM_WHOLE] * 2, out_specs=_VMEM_WHOLE,
        out_shape=jax.ShapeDtypeStruct((2, 2, HGW), F32), compiler_params=_cp(),
    )(dlb, lbl)


def _c_ctx_grad(terms, c_ctx):
    def body(t_ref, c_ref, o_ref):
        tot = t_ref[0, 8:9, :]
        for s in range(1, N_DEV):
            tot = tot + t_ref[s, 8:9, :]
        cv = c_ref[...]
        sg = _sigmoid(cv)
        o_ref[...] = tot * (sg * (1.0 + cv * (1.0 - sg)))

    return pl.pallas_call(
        body, name="c_ctx_grad", in_specs=[_VMEM_WHOLE] * 2, out_specs=_VMEM_WHOLE,
        out_shape=jax.ShapeDtypeStruct((1, D), F32), compiler_params=_cp(),
    )(terms, c_ctx)


def _in_perm():
    fz, bz, inp, kk, vv, qhg, ghg, qat, gates = 0, 512, 1024, 1536, 1664, 1792, 2304, 2816, 3328
    cols = []
    for h in range(HG_HEADS):
        for base in (fz, bz, inp, qhg):
            cols += list(range(base + 128 * h, base + 128 * (h + 1)))
    cols += list(range(ghg, ghg + 512)) + list(range(qat, qat + 512))
    cols += list(range(kk, kk + 128)) + list(range(vv, vv + 128))
    cols += list(range(gates, gates + 2048))
    return np.asarray(cols, np.int32)


_PERM = _in_perm()


_PIECES = {"a": (0, WA, 128), "b": (WA, WB, 256), "c": (WA + WB, WC, 256)}


def _block_table(piece):
    lo, n, blk = _PIECES[piece]
    starts = [int(_PERM[r]) for r in range(lo, lo + n, blk)]
    assert all(s % blk == 0 and np.array_equal(_PERM[r:r + blk], np.arange(s, s + blk))
               for s, r in zip(starts, range(lo, lo + n, blk)))
    return jnp.asarray([s // blk for s in starts], jnp.int32), blk


def _pick_row_blocks(x, table, blk, name):
    cols = x.shape[1]

    def body(t_ref, x_ref, o_ref):
        o_ref[...] = x_ref[...]

    return pl.pallas_call(
        body, name=name,
        grid_spec=pltpu.PrefetchScalarGridSpec(
            num_scalar_prefetch=1, grid=(table.shape[0],),
            in_specs=[pl.BlockSpec((blk, cols), lambda i, t: (t[i], 0))],
            out_specs=pl.BlockSpec((blk, cols), lambda i, t: (i, 0))),
        out_shape=jax.ShapeDtypeStruct((table.shape[0] * blk, cols), x.dtype),
        compiler_params=_cp(("arbitrary",)),
    )(table, x)


def _place_row_blocks(x, table, blk, into, out_rows, name):
    cols = x.shape[1]

    def body(t_ref, x_ref, *rest):
        rest[-1][...] = x_ref[...]

    operands, in_specs, aliases = [table, x], [pl.BlockSpec((blk, cols), lambda i, t: (i, 0))], {}
    if into is not None:
        operands.append(into)
        in_specs.append(_ANY)
        aliases = {2: 0}
    return pl.pallas_call(
        body, name=name,
        grid_spec=pltpu.PrefetchScalarGridSpec(
            num_scalar_prefetch=1, grid=(table.shape[0],), in_specs=in_specs,
            out_specs=pl.BlockSpec((blk, cols), lambda i, t: (t[i], 0))),
        out_shape=jax.ShapeDtypeStruct((out_rows, cols), x.dtype),
        input_output_aliases=aliases,
        compiler_params=_cp(("arbitrary",)),
    )(*operands)


def _local_step(x2, ctx2, h_all, h_lat, tgt, lbl, sh_in, sc_in, gate1, sh2, sc2, gate2, norm_mix_w, norm_ffn_w,
                hgrn_norm_w, q_norm_w, k_norm_w, attn_sinks, w_a, w_b, w_c, s_bh, s_ba, s_out,
                s_gate, s_up, s_down):
    first_last = lambda n: [(0, True), (n - 1, False)]
    p_a = _mm_nt(h_all, w_a, tm=T, tn=512, out_dtype=F32, name="proj_a")
    (o, st), (g_gate, g_bh, g_ba) = _hgrn_fwd(
        p_a, lbl, (_gather_comm_relayed([s_gate, s_bh, s_ba]),
                   [(0, True), (HG_HEADS - 2, True), (HG_HEADS - 1, False)]))
    (p_b, p_c), (g_out,) = _proj_bc(
        h_all, w_b, w_c, (_gather_comm_relayed([s_out]), [(0, True), (N_TILES - 4, True), (N_TILES - 1, False)]))
    cos, sin = _rope_tables()
    qnw_t, knw_t = jnp.tile(q_norm_w, (1, ATT_HEADS)), jnp.tile(k_norm_w, (1, KV_HEADS))
    y_hg, qn, k_rep, v_rep = _prep_fwd(p_b, o, cos, sin, hgrn_norm_w, qnw_t, knw_t)
    (y_at, lse), (g_up, g_down) = _attn_fwd(
        qn, k_rep, v_rep, attn_sinks,
        (_gather_comm_relayed([s_up, s_down]), [(0, True), (N_BLOCKS - 6, True), (N_BLOCKS - 1, False)]))
    w_bh, w_ba, w_o = g_bh.reshape(D, HGW), g_ba.reshape(D, ATW), g_out.reshape(D, D)
    (mixed, r, x1, h2), _ = _merge_fwd(
        y_hg, y_at, p_c, x2, w_bh, w_ba, w_o, gate1, norm_ffn_w, sh2, sc2)
    g_gate, g_up, g_down = [g.reshape(N_FF_TILES, FF_TILE, D) for g in (g_gate, g_up, g_down)]

    act, d_gate, d_up, d_f, dx1, acc_ffn = _ffn_fused(x1, h2, tgt, g_gate, g_up, g_down, gate2,
                                                      norm_ffn_w, sc2)
    by_chip = lambda t: t.reshape((N_CHIPS, 2) + t.shape[1:])
    ff_by_chip = lambda t: t.reshape(N_CHIPS, 2, FF_BLK, D)
    t_down, _ = _mm_tn_blocked(act, d_f, "grad_down", N_FF_TILES)
    t_down = ff_by_chip(t_down)
    t_gate, (f_down,) = _mm_tn_blocked(d_gate, h2, "grad_gate", N_FF_HALVES,
                                       (_sibling_comm([t_down]), first_last(N_FF_HALVES)))
    t_gate = ff_by_chip(t_gate)
    t_up, (f_gate,) = _mm_tn_blocked(d_up, h2, "grad_up", N_FF_HALVES,
                                     (_sibling_comm([t_gate]), first_last(N_FF_HALVES)))
    t_up = ff_by_chip(t_up)

    (d_r, d_a, d_b, dp_c, dy_hg, dy_at, acc_mg), (f_up,) = _merge_bwd(
        dx1, r, y_hg, y_at, p_c, w_bh, w_ba, w_o, gate1, (_sibling_comm([t_up]), first_last(N_LAT_TILES)))
    c_down, c_gate, c_up = [_pair_sum(t, f, "pair_sum_" + nm) for t, f, nm in
                            ((t_down, f_down, "down"), (t_gate, f_gate, "gate"), (t_up, f_up, "up"))]
    t_out = _mm_tn(mixed, d_r, tk=1024, nk=2, tm=1024, tn=1024, out_dtype=BF16, name="grad_out")
    t_bh = _mm_tn(d_a, y_hg, tk=2048, nk=1, tm=1024, tn=512, out_dtype=BF16, name="grad_bh")
    t_ba = _mm_tn(d_b, y_at, tk=2048, nk=1, tm=1024, tn=512, out_dtype=BF16, name="grad_ba")
    t_bh, t_ba, t_out = [by_chip(t.reshape(N_DEV, D // N_DEV, t.shape[1])) for t in (t_bh, t_ba, t_out)]
    (dq, dk_rep, dv_rep, dsink), (r_up,) = _attn_bwd(
        qn, k_rep, v_rep, attn_sinks, y_at, lse, dy_at, (_chip_comm([c_up]), first_last(N_BLOCKS)))
    (dp_b, d_o, acc_prep), (f_bh, f_ba, f_out) = _prep_bwd(
        p_b, o, cos, sin, hgrn_norm_w, qnw_t, knw_t, dy_hg, dq, dk_rep, dv_rep,
        (_sibling_comm([t_bh, t_ba, t_out]), first_last(N_TILES)))
    c_bh, c_ba, c_out = [_pair_sum(t, f, "pair_sum_" + nm) for t, f, nm in
                         ((t_bh, f_bh, "bh"), (t_ba, f_ba, "ba"), (t_out, f_out, "out"))]
    (dp_a, dlb), (r_bh, r_ba, r_out, r_down, r_gate) = _hgrn_bwd(
        p_a, lbl, d_o, st, (_chip_comm([c_bh, c_ba, c_out, c_down, c_gate]), first_last(HG_HEADS)))
    t_a = _mm_tn(dp_a, h_all, tk=T, nk=1, tm=1024, tn=1024, out_dtype=BF16, name="grad_in_a")
    t_b = _mm_tn(dp_b, h_all, tk=T, nk=1, tm=640, tn=1024, out_dtype=BF16, name="grad_in_b")
    t_c = _mm_tn(dp_c, h_lat, tk=1024, nk=2, tm=1024, tn=1024, out_dtype=BF16, name="grad_in_c")
    t_in = None
    for piece, nm in ((t_a, "a"), (t_b, "b"), (t_c, "c")):
        t_in = _place_row_blocks(piece, *_block_table(nm), t_in, IN_COLS, "order_terms_" + nm)
    t_in = by_chip(t_in.reshape(N_DEV, IN_BLK, D))
    (f_in,) = _run_comm(_sibling_comm([t_in]), "scatter_in_sibling")
    c_in = _pair_sum(t_in, f_in, "pair_sum_in")
    sems, c_in, land, token = _chip_exchange_start(c_in, jnp.zeros(c_in.shape, c_in.dtype))
    (grad_x, acc_in), _ = _input_bwd(dp_a, dp_b, dp_c, w_a, w_b, w_c, ctx2, x2, dx1,
                                     norm_mix_w + token[0, 0], sh_in, sc_in)
    small = _pack_small(acc_in, acc_mg, acc_ffn, acc_prep, dsink, dlb)
    return grad_x, small, [r_bh, r_ba, r_out, r_gate, r_up, r_down], (sems, c_in, land)


def kernel(x, c, ctx, c_ctx, w_ada, b_ada, norm_mix_w, norm_ffn_w, w_in, hgrn_lb_logits, hgrn_norm_w, q_norm_w, k_norm_w, attn_sinks, w_branch_hgrn, w_branch_attn, w_out, w_ffn_gate, w_ffn_up, w_ffn_down, loss_target, m_c_ctx, m_w_ada, m_b_ada, m_norm_mix_w, m_norm_ffn_w, m_w_in, m_hgrn_lb_logits, m_hgrn_norm_w, m_q_norm_w, m_k_norm_w, m_attn_sinks, m_w_branch_hgrn, m_w_branch_attn, m_w_out, m_w_ffn_gate, m_w_ffn_up, m_w_ffn_down, v_c_ctx, v_w_ada, v_b_ada, v_norm_mix_w, v_norm_ffn_w, v_w_in, v_hgrn_lb_logits, v_hgrn_norm_w, v_q_norm_w, v_k_norm_w, v_attn_sinks, v_w_branch_hgrn, v_w_branch_attn, v_w_out, v_w_ffn_gate, v_w_ffn_up, v_w_ffn_down):
    me = 4 * lax.axis_index("x") + 2 * lax.axis_index("y") + lax.axis_index("c")
    x2, ctx2, tgt = x[0], ctx[0], loss_target[0]
    w_ada2, w_in2 = w_ada[0], w_in[0]

    cond = jnp.zeros((8, D), F32).at[0].set(c[0]).at[1, :256].set(hgrn_lb_logits.reshape(256))
    b_cols = lax.dynamic_slice(b_ada, (0, me * ADA_BLK), (1, ADA_BLK))
    g0, cc, mod, g_in, h_all, h_lat = _prologue(cond, c_ctx.reshape(1, D), w_ada2, b_cols, w_in2.T.astype(BF16),
                                         x2, ctx2, norm_mix_w)
    lbl = jnp.transpose(g0[:, 1, :256].reshape(N_DEV, 2, 2, 64), (1, 2, 0, 3)).reshape(2, 2, HGW)
    sh1, sc1, gate1, sh2, sc2, gate2 = [mod[k:k + 1] for k in range(6)]
    sh_in = jnp.concatenate([mod[6:7], sh1], axis=0)
    sc_in = jnp.concatenate([mod[7:8], sc1], axis=0)

    shards = [w_branch_hgrn[0].T, w_branch_attn[0].T, w_out[0], w_ffn_gate[0].T, w_ffn_up[0].T, w_ffn_down[0]]
    w_in_t = g_in.reshape(IN_COLS, D)
    w_a, w_b, w_c = [_pick_row_blocks(w_in_t, *_block_table(nm), "order_w_" + nm) for nm in "abc"]

    grad_x, small, (r_bh, r_ba, r_out, r_gate, r_up, r_down), pending_in = _local_step(
        x2, ctx2, h_all, h_lat, tgt, lbl, sh_in, sc_in, gate1, sh2, sc2, gate2, norm_mix_w, norm_ffn_w, hgrn_norm_w,
        q_norm_w, k_norm_w, attn_sinks, w_a, w_b, w_c, *[s.astype(BF16) for s in shards])

    big, updated = {}, []
    for nm, rr, ww, mm, vv, tr, transposed in (
            ("w_branch_hgrn", r_bh, w_branch_hgrn[0], m_w_branch_hgrn[0], v_w_branch_hgrn[0], 128, True),
            ("w_branch_attn", r_ba, w_branch_attn[0], m_w_branch_attn[0], v_w_branch_attn[0], 128, True),
            ("w_out", r_out, w_out[0], m_w_out[0], v_w_out[0], 128, False),
            ("w_ffn_gate", r_gate, w_ffn_gate[0], m_w_ffn_gate[0], v_w_ffn_gate[0], 176, True),
            ("w_ffn_up", r_up, w_ffn_up[0], m_w_ffn_up[0], v_w_ffn_up[0], 176, True),
            ("w_ffn_down", r_down, w_ffn_down[0], m_w_ffn_down[0], v_w_ffn_down[0], 176, False)):
        if transposed:
            res = _adamw_sharded(rr, ww.T, mm.T, vv.T, "adamw_" + nm, tr, after=grad_x)
            big[nm] = [t.T[None] for t in res]
        else:
            res = _adamw_sharded(rr, ww, mm, vv, "adamw_" + nm, tr, after=grad_x)
            big[nm] = [t[None] for t in res]
        updated.append(res[1])

    (g2,) = _all_gather([small], "gather_small", True, after=updated)
    tot = _sum_small(g2)
    dm = jnp.zeros((16, 6 * D), F32).at[:8].set(g2[:, R_DMOD:R_DMOD + 6, :].reshape(N_DEV, 6 * D))
    dm = dm.at[8, :2 * D].set(tot[R_DCTX:R_DCTX + 2].reshape(2 * D))
    dm_cols = lax.dynamic_slice(dm, (0, me * ADA_BLK), (16, ADA_BLK))
    g_w_ada, dsc_term = _ada_grads(cc, dm_cols, w_ada2)
    (g3,) = _all_gather([dsc_term], "gather_cctx", True)
    g_c_ctx = _c_ctx_grad(g3, c_ctx.reshape(1, D))
    g_lbl = _lb_grads(tot[R_DLB:R_DLB + 2, :HGW], lbl)
    g_lb_mine = lax.dynamic_slice(g_lbl, (0, 0, me * 64), (2, 2, 64))
    misc = tot[R_MISC]
    loss = misc[M_LOSS]

    rep_out = _adamw_replicated(
        tot, g_c_ctx,
        [b_ada, c_ctx.reshape(1, D), norm_mix_w, norm_ffn_w, hgrn_norm_w, q_norm_w, k_norm_w, attn_sinks],
        [m_b_ada, m_c_ctx.reshape(1, D), m_norm_mix_w, m_norm_ffn_w, m_hgrn_norm_w, m_q_norm_w, m_k_norm_w,
         m_attn_sinks],
        [v_b_ada, v_c_ctx.reshape(1, D), v_norm_mix_w, v_norm_ffn_w, v_hgrn_norm_w, v_q_norm_w, v_k_norm_w,
         v_attn_sinks])
    rep = []
    for kind in range(4):
        vals = dict(zip(_REP_NAMES, rep_out[kind * len(_REP_NAMES):(kind + 1) * len(_REP_NAMES)]))
        vals["c_ctx"] = vals["c_ctx"].reshape(D)
        rep.append(vals)

    sems, c_in, land = pending_in
    d_ada, nm_ada, nv_ada = _adamw_plain(g_w_ada, w_ada2, m_w_ada[0], v_w_ada[0], "adamw_w_ada", tr=256)
    land = _chip_exchange_wait(sems, c_in, land, d_ada)
    own = lax.dynamic_index_in_dim(c_in, 2 * lax.axis_index("x") + lax.axis_index("y"), 0, keepdims=False)
    big["w_in"] = [t.T[None] for t in _adamw_sharded(land, w_in2.T, m_w_in[0].T, v_w_in[0].T, "adamw_w_in", 336,
                                                     extra=own)]
    ada = [t[None] for t in (g_w_ada, d_ada, nm_ada, nv_ada)]
    lb_w = hgrn_lb_logits.reshape(4, 64)
    d_lb, nm_lb, nv_lb = _adamw_plain(g_lb_mine.reshape(4, 64), lb_w, m_hgrn_lb_logits.reshape(4, 64),
                                      v_hgrn_lb_logits.reshape(4, 64), "adamw_lb")
    lbs = [t.reshape(2, 2, 64) for t in (g_lb_mine, d_lb, nm_lb, nv_lb)]

    names = ['c_ctx', 'w_ada', 'b_ada', 'norm_mix_w', 'norm_ffn_w', 'w_in', 'hgrn_lb_logits', 'hgrn_norm_w',
             'q_norm_w', 'k_norm_w', 'attn_sinks', 'w_branch_hgrn', 'w_branch_attn', 'w_out', 'w_ffn_gate',
             'w_ffn_up', 'w_ffn_down']
    outs = [loss, grad_x[None]]
    for kind in range(4):
        for nm in names:
            if nm == 'w_ada':
                outs.append(ada[kind])
            elif nm == 'hgrn_lb_logits':
                outs.append(lbs[kind])
            elif nm in big:
                outs.append(big[nm][kind])
            else:
                outs.append(rep[kind][nm])
    return tuple(outs)
```

```python
import functools
import math

import numpy as np
import jax
import jax.numpy as jnp
from jax import lax
from jax.experimental import pallas as pl
from jax.experimental.pallas import tpu as pltpu

F32 = jnp.float32
BF16 = jnp.bfloat16

N_DEV = 8
D = 1024
S = 2048
L = 256
T = L + S
TM = 256
N_TILES = T // TM
N_LAT_TILES = S // TM
HG_HEADS = 4
HG_DIM = 128
HGW = 512
CHUNK = 32
N_CHUNKS = T // CHUNK
N_CTX_CHUNKS = L // CHUNK
ATT_HEADS = 8
KV_HEADS = 2
HEAD_DIM = 64
ATW = 512
KVW = 128
BLOCK = 128
N_BLOCKS = S // BLOCK
GRID_W = 64
ROPE_THETA = 10000.0
D_FF = 2816
FF_BLK = D_FF // N_DEV
FF_TILE = 256
N_FF_TILES = D_FF // FF_TILE
N_FF_HALVES = 2
IN_COLS = 5376
IN_BLK = IN_COLS // N_DEV
ADA_BLK = 6 * D // N_DEV
EPS = 1e-6
WA, WB, WC = 2048, 1280, 2048

ADAM_LR = 0.001
ADAM_B1 = 0.9
ADAM_B2 = 0.999
ADAM_EPS = 1e-08
ADAM_WD = 0.01
ADAM_STEP = 10

VMEM_LIMIT = 56 * 1024 * 1024
MESH = pl.DeviceIdType.MESH


def _cp(sem=None, vmem=VMEM_LIMIT):
    return pltpu.CompilerParams(dimension_semantics=sem, vmem_limit_bytes=vmem)


def _full(shape):
    n = len(shape)
    return pl.BlockSpec(shape, lambda *_: (0,) * n)


_VMEM_WHOLE = pl.BlockSpec(memory_space=pltpu.VMEM)
_ANY = pl.BlockSpec(memory_space=pl.ANY)


def _sigmoid(v):
    return 1.0 / (1.0 + jnp.exp(-v))


def _dot(a, b):
    return jnp.dot(a, b, preferred_element_type=F32)


def _dot_nt(a, b):
    return lax.dot_general(a, b, (((1,), (1,)), ((), ())), preferred_element_type=F32)


def _dot_tn(a, b):
    return lax.dot_general(a, b, (((0,), (0,)), ((), ())), preferred_element_type=F32)


def _split3(v):
    hi = v.astype(BF16)
    r = v - hi.astype(F32)
    mid = r.astype(BF16)
    lo = (r - mid.astype(F32)).astype(BF16)
    return hi, mid, lo


def _dot_exact_rhs01(v, m01):
    hi, mid, lo = _split3(v)
    return _dot(hi, m01) + _dot(mid, m01) + _dot(lo, m01)


def _split2(v):
    hi = v.astype(BF16)
    return hi, (v - hi.astype(F32)).astype(BF16)


def _dot_lhs01(m01, v):
    hi, lo = _split2(v)
    return _dot(m01, hi) + _dot(m01, lo)


def _dot_f32(a, b, dot=_dot):
    ah, am, al = _split3(a)
    bh, bm, bl = _split3(b)
    return (dot(ah, bh) + (dot(ah, bm) + dot(am, bh))
            + (dot(am, bm) + dot(ah, bl) + dot(al, bh)))


def _my_pos():
    return lax.axis_index("x"), lax.axis_index("y"), lax.axis_index("c")


class _Comm:
    def __init__(self, operands, out_shapes, sems, phases):
        self.operands, self.out_shapes, self.sems, self.phases = operands, out_shapes, sems, phases


def _gather_comm(blocks):
    n = len(blocks)

    def parts(ins, outs, sems):
        send_sems, recv_sems, local_sems = sems
        x, y, c = _my_pos()
        me, sibling = (x, y, c), (x, y, 1 - c)
        chips = [(1 - x, y), (x, 1 - y), (1 - x, 1 - y)]

        def slot(a, px, py, pc):
            return outs[a].at[4 * px + 2 * py + pc]

        def copy(a, k, block, to, src=None):
            return pltpu.make_async_remote_copy(
                src_ref=slot(a, *block) if src is None else src, dst_ref=slot(a, *block),
                send_sem=send_sems.at[a, k], recv_sem=recv_sems.at[a, k],
                device_id=to, device_id_type=MESH)

        mine = [pltpu.make_async_copy(ins[a], slot(a, *me), local_sems.at[a]) for a in range(n)]
        first = []
        for a in range(n):
            first.append(copy(a, 0, me, sibling, src=ins[a]))
            first += [copy(a, 1 + j, me, (*chip, c), src=ins[a]) for j, chip in enumerate(chips)]
        passed = [copy(a, 4 + j, (*chip, c), sibling) for j, chip in enumerate(chips) for a in range(n)]
        return c, me, sibling, chips, copy, mine, first, passed

    def start(ins, outs, sems):
        _, _, _, _, _, mine, first, _ = parts(ins, outs, sems)
        for cp in mine + first:
            cp.start()

    def forward(ins, outs, sems):
        c, me, _, chips, copy, _, _, passed = parts(ins, outs, sems)
        for j, chip in enumerate(chips):
            for a in range(n):
                copy(a, 1 + j, (*chip, c), me).wait_recv()
                passed[j * n + a].start()

    def finish(ins, outs, sems):
        c, me, sibling, chips, copy, mine, first, passed = parts(ins, outs, sems)
        for a in range(n):
            copy(a, 0, sibling, me).wait_recv()
            for j, chip in enumerate(chips):
                copy(a, 4 + j, (*chip, 1 - c), me).wait_recv()
        for cp in first + passed:
            cp.wait_send()
        for cp in mine:
            cp.wait()

    return _Comm(blocks, [jax.ShapeDtypeStruct((N_DEV,) + b.shape, b.dtype) for b in blocks],
                 [pltpu.SemaphoreType.DMA((n, 7)), pltpu.SemaphoreType.DMA((n, 7)), pltpu.SemaphoreType.DMA((n,))],
                 [start, forward, finish])


def _gather_comm_relayed(blocks):
    n = len(blocks)

    def parts(ins, outs, sems):
        send_sems, recv_sems, local_sems = sems
        x, y, c = _my_pos()
        me, sibling = (x, y, c), (x, y, 1 - c)
        x_nbr, y_nbr, diag = (1 - x, y, c), (x, 1 - y, c), (1 - x, 1 - y, c)

        def slot(a, dev, half=None):
            ref = outs[a].at[4 * dev[0] + 2 * dev[1] + dev[2]]
            if half is None:
                return ref
            rows = blocks[a].shape[0] // 2
            return ref.at[pl.ds(half * rows, rows)]

        def copy(a, k, block, to, half=None, src=None):
            return pltpu.make_async_remote_copy(
                src_ref=slot(a, block, half) if src is None else src, dst_ref=slot(a, block, half),
                send_sem=send_sems.at[a, k], recv_sem=recv_sems.at[a, k],
                device_id=to, device_id_type=MESH)

        mine = [pltpu.make_async_copy(ins[a], slot(a, me), local_sems.at[a]) for a in range(n)]
        return me, sibling, x_nbr, y_nbr, diag, copy, mine

    def start(ins, outs, sems):
        me, sibling, x_nbr, y_nbr, _, copy, mine = parts(ins, outs, sems)
        for cp in mine:
            cp.start()
        for a in range(n):
            for k, to in ((1, x_nbr), (2, y_nbr), (0, sibling)):
                copy(a, k, me, to, src=ins[a]).start()

    def forward(ins, outs, sems):
        me, sibling, x_nbr, y_nbr, _, copy, _ = parts(ins, outs, sems)
        for a in range(n):
            copy(a, 1, x_nbr, me).wait_recv()
            copy(a, 3, x_nbr, y_nbr, half=0).start()
            copy(a, 5, x_nbr, sibling).start()
        for a in range(n):
            copy(a, 2, y_nbr, me).wait_recv()
            copy(a, 4, y_nbr, x_nbr, half=1).start()
            copy(a, 6, y_nbr, sibling).start()

    def finish(ins, outs, sems):
        me, sibling, x_nbr, y_nbr, diag, copy, mine = parts(ins, outs, sems)
        sib = lambda dev: (dev[0], dev[1], sibling[2])
        for a in range(n):
            copy(a, 3, diag, me, half=0).wait_recv()
            copy(a, 4, diag, me, half=1).wait_recv()
            copy(a, 7, diag, sibling).start()
        for a in range(n):
            copy(a, 0, sibling, me).wait_recv()
            for k, dev in ((5, x_nbr), (6, y_nbr), (7, diag)):
                copy(a, k, sib(dev), me).wait_recv()
        for a in range(n):
            for k, block, to, half in ((0, me, sibling, None), (1, me, x_nbr, None), (2, me, y_nbr, None),
                                       (3, x_nbr, y_nbr, 0), (4, y_nbr, x_nbr, 1), (5, x_nbr, sibling, None),
                                       (6, y_nbr, sibling, None), (7, diag, sibling, None)):
                copy(a, k, block, to, half=half, src=ins[a] if block is me else None).wait_send()
        for cp in mine:
            cp.wait()

    return _Comm(blocks, [jax.ShapeDtypeStruct((N_DEV,) + b.shape, b.dtype) for b in blocks],
                 [pltpu.SemaphoreType.DMA((n, 8)), pltpu.SemaphoreType.DMA((n, 8)), pltpu.SemaphoreType.DMA((n,))],
                 [start, forward, finish])


_HBM = pl.BlockSpec(memory_space=pltpu.HBM)
_SEM = pl.BlockSpec(memory_space=pltpu.SEMAPHORE)
_SPLIT_COPY = pltpu.CompilerParams(has_side_effects=pltpu.SideEffectType.DATAFLOW_SIDE_EFFECTING)


def _chip_exchange_copies(src_ref, land_ref, sems):
    x, y, c = _my_pos()
    q_me = 2 * x + y
    pairs = []
    for j, (px, py) in enumerate([(1 - x, y), (x, 1 - y), (1 - x, 1 - y)]):
        q = 2 * px + py
        send = pltpu.make_async_remote_copy(
            src_ref=src_ref.at[q], dst_ref=land_ref.at[q_me], send_sem=sems[j], recv_sem=sems[3 + j],
            device_id=(px, py, c), device_id_type=MESH)
        recv = pltpu.make_async_remote_copy(
            src_ref=src_ref.at[q], dst_ref=land_ref.at[q], send_sem=sems[j], recv_sem=sems[3 + j],
            device_id=(x, y, c), device_id_type=MESH)
        pairs.append((send, recv))
    return pairs


def _chip_exchange_start(src, land):
    def body(src_ref, land_ref, *outs):
        sems, token = outs[:6], outs[8]
        for send, _ in _chip_exchange_copies(src_ref, land_ref, sems):
            send.start()
        token[...] = jnp.zeros_like(token)

    res = pl.pallas_call(
        body, name="scatter_in_start",
        out_shape=(pltpu.SemaphoreType.DMA(()),) * 6 + (
            pltpu.HBM(src.shape, src.dtype), pltpu.HBM(land.shape, land.dtype),
            jax.ShapeDtypeStruct((8, 128), F32)),
        in_specs=(_HBM, _HBM), out_specs=(_SEM,) * 6 + (_HBM, _HBM, pl.BlockSpec(memory_space=pltpu.VMEM)),
        input_output_aliases={0: 6, 1: 7}, compiler_params=_SPLIT_COPY,
    )(pltpu.with_memory_space_constraint(src, pltpu.HBM), pltpu.with_memory_space_constraint(land, pltpu.HBM))
    return res[:6], res[6], res[7], res[8]


def _chip_exchange_wait(sems, src_thru, land_thru, after):
    def body(src_ref, land_ref, *rest):
        for send, recv in _chip_exchange_copies(src_ref, land_ref, rest[:6]):
            send.wait_send()
            recv.wait_recv()

    return pl.pallas_call(
        body, name="scatter_in_wait",
        out_shape=(pltpu.HBM(src_thru.shape, src_thru.dtype), pltpu.HBM(land_thru.shape, land_thru.dtype)),
        in_specs=(_HBM, _HBM) + (_SEM,) * 6 + (_ANY,), out_specs=(_HBM, _HBM),
        input_output_aliases={0: 0, 1: 1}, compiler_params=_SPLIT_COPY,
    )(src_thru, land_thru, *sems, after)[1]


def _run_comm(comm, name, in_vmem=False, after=()):
    n_in, n_out, n_after = len(comm.operands), len(comm.out_shapes), len(after)

    def body(*refs):
        ins, refs = refs[:n_in], refs[n_in + n_after:]
        outs, sems = refs[:n_out], refs[n_out:]
        for phase in comm.phases:
            phase(ins, outs, sems)

    spec = _VMEM_WHOLE if in_vmem else _ANY
    return pl.pallas_call(
        body, name=name, out_shape=comm.out_shapes, in_specs=[spec] * n_in + [_ANY] * n_after,
        out_specs=[spec] * n_out, scratch_shapes=comm.sems,
    )(*comm.operands, *after)


def _carrier_call(body, comm, schedule, *, name, grid, in_specs, out_specs, out_shape, scratch_shapes, operands):
    n_in, n_out, n_scr = len(in_specs), len(out_specs), len(scratch_shapes)
    c_in, c_out = len(comm.operands), len(comm.out_shapes)

    def full_body(*refs):
        ins, refs = refs[:n_in], refs[n_in:]
        cins, refs = refs[:c_in], refs[c_in:]
        outs, refs = refs[:n_out], refs[n_out:]
        couts, refs = refs[:c_out], refs[c_out:]
        scr, csems = refs[:n_scr], refs[n_scr:]
        step = pl.program_id(0)

        def run(before):
            for (at, when_before), phase in zip(schedule, comm.phases):
                if when_before == before:
                    pl.when(step == at)(functools.partial(phase, cins, couts, csems))

        run(True)
        body(*ins, *outs, *scr)
        run(False)

    res = pl.pallas_call(
        full_body, name=name, grid=grid,
        in_specs=list(in_specs) + [_ANY] * c_in, out_specs=list(out_specs) + [_ANY] * c_out,
        out_shape=list(out_shape) + list(comm.out_shapes),
        scratch_shapes=list(scratch_shapes) + list(comm.sems),
        compiler_params=_cp(("arbitrary",)),
    )(*operands, *comm.operands)
    return res[:n_out], res[n_out:]


def _pcall(body, carried, *, name, grid, in_specs, out_specs, out_shape, scratch_shapes, operands):
    if carried is None:
        res = pl.pallas_call(body, name=name, grid=grid, in_specs=in_specs, out_specs=out_specs,
                             out_shape=out_shape, scratch_shapes=scratch_shapes,
                             compiler_params=_cp(("arbitrary",)))(*operands)
        return res, ()
    return _carrier_call(body, carried[0], carried[1], name=name, grid=grid, in_specs=in_specs,
                         out_specs=out_specs, out_shape=out_shape, scratch_shapes=scratch_shapes,
                         operands=operands)


def _all_gather(blocks, name, in_vmem, after=()):
    return _run_comm(_gather_comm(blocks), name, in_vmem, after)


N_CHIPS = 4


def _sibling_comm(contribs):
    n = len(contribs)

    def copies(ins, outs, sems):
        send_sems, recv_sems = sems
        x, y, c = _my_pos()
        return [pltpu.make_async_remote_copy(
            src_ref=ins[a].at[pl.ds(0, N_CHIPS), 1 - c], dst_ref=outs[a],
            send_sem=send_sems.at[a], recv_sem=recv_sems.at[a],
            device_id=(x, y, 1 - c), device_id_type=MESH) for a in range(n)]

    def start(ins, outs, sems):
        for cp in copies(ins, outs, sems):
            cp.start()

    def finish(ins, outs, sems):
        cps = copies(ins, outs, sems)
        for cp in cps:
            cp.wait_recv()
        for cp in cps:
            cp.wait_send()

    return _Comm(contribs, [jax.ShapeDtypeStruct((N_CHIPS,) + b.shape[2:], b.dtype) for b in contribs],
                 [pltpu.SemaphoreType.DMA((n,)), pltpu.SemaphoreType.DMA((n,))], [start, finish])


def _pair_sum(mine, theirs, name):
    _, _, rows, cols = mine.shape
    core = lax.axis_index("c").astype(jnp.int32).reshape(1)

    def body(c_ref, m_ref, t_ref, o_ref):
        o_ref[...] = (m_ref[...].astype(F32) + t_ref[...].astype(F32)).astype(BF16)

    return pl.pallas_call(
        body, name=name,
        grid_spec=pltpu.PrefetchScalarGridSpec(
            num_scalar_prefetch=1, grid=(N_CHIPS,),
            in_specs=[pl.BlockSpec((None, None, rows, cols), lambda q, c: (q, c[0], 0, 0)),
                      pl.BlockSpec((None, rows, cols), lambda q, c: (q, 0, 0))],
            out_specs=pl.BlockSpec((None, rows, cols), lambda q, c: (q, 0, 0))),
        out_shape=jax.ShapeDtypeStruct((N_CHIPS, rows, cols), BF16),
        compiler_params=_cp(("parallel",)),
    )(core, mine, theirs)


def _chip_comm(sums):
    n = len(sums)

    def parts(ins, outs, sems):
        send_sems, recv_sems, local_sems = sems
        x, y, c = _my_pos()
        q_me = 2 * x + y
        chips = [(1 - x, y), (x, 1 - y), (1 - x, 1 - y)]
        mine = [pltpu.make_async_copy(ins[a].at[q_me], outs[a].at[q_me], local_sems.at[a]) for a in range(n)]
        sends, recvs = [], []
        for j, (px, py) in enumerate(chips):
            for a in range(n):
                q = 2 * px + py
                sends.append(pltpu.make_async_remote_copy(
                    src_ref=ins[a].at[q], dst_ref=outs[a].at[q_me],
                    send_sem=send_sems.at[a, j], recv_sem=recv_sems.at[a, j],
                    device_id=(px, py, c), device_id_type=MESH))
                recvs.append(pltpu.make_async_remote_copy(
                    src_ref=ins[a].at[q], dst_ref=outs[a].at[q],
                    send_sem=send_sems.at[a, j], recv_sem=recv_sems.at[a, j],
                    device_id=(x, y, c), device_id_type=MESH))
        return mine, sends, recvs

    def start(ins, outs, sems):
        mine, sends, _ = parts(ins, outs, sems)
        for cp in mine + sends:
            cp.start()

    def finish(ins, outs, sems):
        mine, sends, recvs = parts(ins, outs, sems)
        for cp in recvs:
            cp.wait_recv()
        for cp in sends:
            cp.wait_send()
        for cp in mine:
            cp.wait()

    return _Comm(sums, [jax.ShapeDtypeStruct(b.shape, b.dtype) for b in sums],
                 [pltpu.SemaphoreType.DMA((n, 3)), pltpu.SemaphoreType.DMA((n, 3)), pltpu.SemaphoreType.DMA((n,))],
                 [start, finish])


def _mm_nt(a, bt, *, tm, tn, out_dtype, name, row_off=0, rows=None):
    rows = a.shape[0] if rows is None else rows
    n, k = bt.shape

    def body(a_ref, b_ref, o_ref):
        o_ref[...] = _dot_nt(a_ref[...], b_ref[...]).astype(out_dtype)

    return pl.pallas_call(
        body, name=name, grid=(rows // tm, n // tn),
        in_specs=[pl.BlockSpec((tm, k), lambda i, j: (i + row_off, 0)),
                  pl.BlockSpec((tn, k), lambda i, j: (j, 0))],
        out_specs=pl.BlockSpec((tm, tn), lambda i, j: (i, j)),
        out_shape=jax.ShapeDtypeStruct((rows, n), out_dtype),
        compiler_params=_cp(("parallel", "parallel")),
    )(a, bt)


def _mm_tn(a, b, *, tk, nk, tm, tn, out_dtype, name, a_off=0, b_off=0):
    m, n = a.shape[1], b.shape[1]

    def body(a_ref, b_ref, o_ref, acc):
        kk = pl.program_id(2)

        @pl.when(kk == 0)
        def _():
            acc[...] = jnp.zeros_like(acc)

        acc[...] += _dot_tn(a_ref[...], b_ref[...])

        @pl.when(kk == nk - 1)
        def _():
            o_ref[...] = acc[...].astype(out_dtype)

    return pl.pallas_call(
        body, name=name, grid=(m // tm, n // tn, nk),
        in_specs=[pl.BlockSpec((tk, tm), lambda i, j, kk: (kk + a_off, i)),
                  pl.BlockSpec((tk, tn), lambda i, j, kk: (kk + b_off, j))],
        out_specs=pl.BlockSpec((tm, tn), lambda i, j, kk: (i, j)),
        out_shape=jax.ShapeDtypeStruct((m, n), out_dtype),
        scratch_shapes=[pltpu.VMEM((tm, tn), F32)],
        compiler_params=_cp(("parallel", "parallel", "arbitrary")),
    )(a, b)


def _mm_tn_blocked(a, b, name, steps, carried=None):
    w = a.shape[1] // steps
    n = b.shape[1]

    def body(a_ref, b_ref, o_ref):
        o_ref[...] = _dot_tn(a_ref[...], b_ref[...]).astype(BF16)

    (out,), extra = _pcall(
        body, carried, name=name, grid=(steps,),
        in_specs=[pl.BlockSpec((S, w), lambda j: (0, j)), _full((S, n))],
        out_specs=[pl.BlockSpec((w, n), lambda j: (j, 0))],
        out_shape=[jax.ShapeDtypeStruct((a.shape[1], n), BF16)],
        scratch_shapes=[], operands=[a, b])
    return out, extra


def _prologue(cond, c_ctx, w_ada, b_cols, w_in_t, x, ctx, nw):
    rows_shape = jax.ShapeDtypeStruct((16, ADA_BLK), F32)
    big, g_cond, g_mod = _gather_comm_relayed([w_in_t]), _gather_comm([cond]), _gather_comm([rows_shape])

    def body(cond_ref, cctx_ref, wada_ref, b_ref, nw_ref, win_ref, x_ref, ctx_ref,
             g0_ref, cc_ref, mod_ref, gin_ref, h_ref, hl_ref, rows_ref, g1_ref, x_s, ctx_s, h_s, io_sems, *sems):
        s_big, s_cond, s_mod = sems[0:3], sems[3:6], sems[6:9]
        load_x = pltpu.make_async_copy(x_ref, x_s, io_sems.at[0])
        load_ctx = pltpu.make_async_copy(ctx_ref, ctx_s, io_sems.at[1])
        load_x.start()
        load_ctx.start()
        for phase in g_cond.phases:
            phase([cond_ref], [g0_ref], s_cond)
        big.phases[0]([win_ref], [gin_ref], s_big)
        cc_ref[...] = jnp.zeros_like(cc_ref)
        for j in range(N_DEV):
            cc_ref[j:j + 1, :] = g0_ref[j, 0:1, :]
        cc_ref[N_DEV:N_DEV + 1, :] = cctx_ref[...]
        cv = cc_ref[...]
        rows_ref[...] = _dot_f32(cv * _sigmoid(cv), wada_ref[...]) + b_ref[...]
        for phase in g_mod.phases:
            phase([rows_ref], [g1_ref], s_mod)
        big.phases[1]([win_ref], [gin_ref], s_big)
        x_pos, y_pos, c_pos = _my_pos()
        me = 4 * x_pos + 2 * y_pos + c_pos
        mine = jnp.concatenate([g1_ref[j, pl.ds(me, 1), :] for j in range(N_DEV)], axis=1)
        shared = jnp.concatenate([g1_ref[j, N_DEV:N_DEV + 1, :] for j in range(N_DEV)], axis=1)
        for k in range(6):
            mod_ref[k:k + 1, :] = mine[:, k * D:(k + 1) * D]
        mod_ref[6:7, :] = shared[:, 0:D]
        mod_ref[7:8, :] = shared[:, D:2 * D]
        load_ctx.wait()
        load_x.wait()
        h_s[pl.ds(0, L), :] = _rms_mod(ctx_s[...], nw_ref[...], mod_ref[6:7, :], mod_ref[7:8, :]).astype(BF16)

        def norm_tile(i, carry):
            r0 = pl.multiple_of(i * TM, TM)
            h_s[pl.ds(L + r0, TM), :] = _rms_mod(
                x_s[pl.ds(r0, TM), :], nw_ref[...], mod_ref[0:1, :], mod_ref[1:2, :]).astype(BF16)
            return carry

        lax.fori_loop(0, N_LAT_TILES, norm_tile, 0)
        stores = [pltpu.make_async_copy(h_s, h_ref, io_sems.at[2]),
                  pltpu.make_async_copy(h_s.at[pl.ds(L, S)], hl_ref, io_sems.at[3])]
        for cp in stores:
            cp.start()
        big.phases[2]([win_ref], [gin_ref], s_big)
        for cp in stores:
            cp.wait()

    return pl.pallas_call(
        body, name="prologue",
        in_specs=[_VMEM_WHOLE] * 5 + [_ANY] * 3, out_specs=[_VMEM_WHOLE] * 3 + [_ANY] * 3,
        out_shape=[g_cond.out_shapes[0], jax.ShapeDtypeStruct((16, D), F32), jax.ShapeDtypeStruct((8, D), F32),
                   big.out_shapes[0], jax.ShapeDtypeStruct((T, D), BF16), jax.ShapeDtypeStruct((S, D), BF16)],
        scratch_shapes=[pltpu.VMEM((16, ADA_BLK), F32), pltpu.VMEM((N_DEV, 16, ADA_BLK), F32),
                        pltpu.VMEM((S, D), F32), pltpu.VMEM((L, D), F32), pltpu.VMEM((T, D), BF16),
                        pltpu.SemaphoreType.DMA((4,))] + big.sems + g_cond.sems + g_mod.sems,
        compiler_params=_cp(),
    )(cond, c_ctx, w_ada, b_cols, nw, w_in_t, x, ctx)


def _ada_grads(cc, dm_cols, w_ada):
    def body(c_ref, dm_ref, w_ref, gw_ref, dsc_ref):
        cv = c_ref[...]
        sc = cv * _sigmoid(cv)
        dm = dm_ref[...]
        gw_ref[...] = _dot_f32(sc, dm, dot=_dot_tn)
        dsc_ref[...] = _dot_f32(dm, w_ref[...], dot=_dot_nt)

    return pl.pallas_call(
        body, name="ada_grads",
        in_specs=[_VMEM_WHOLE] * 3, out_specs=[_VMEM_WHOLE] * 2,
        out_shape=[jax.ShapeDtypeStruct((D, ADA_BLK), F32), jax.ShapeDtypeStruct((16, D), F32)],
        compiler_params=_cp(),
    )(cc, dm_cols, w_ada)


def _lat(i):
    return jnp.maximum(i - 1, 0)


def _rms_mod(xv, nw, sh, sc):
    rstd = lax.rsqrt(jnp.mean(xv * xv, axis=-1, keepdims=True) + EPS)
    return (xv * rstd * nw) * (1.0 + sc) + sh


def _rms_mod_bwd(xv, nw, sc, dh):
    rstd = lax.rsqrt(jnp.mean(xv * xv, axis=-1, keepdims=True) + EPS)
    xhat = xv * rstd
    dn = dh * (1.0 + sc)
    dxhat = dn * nw
    dx = rstd * (dxhat - xhat * jnp.mean(dxhat * xhat, axis=-1, keepdims=True))
    return (dx, jnp.sum(dh, axis=0, keepdims=True), jnp.sum(dh * (xhat * nw), axis=0, keepdims=True),
            jnp.sum(dn * xhat, axis=0, keepdims=True))


def _chunk_masks(reverse):
    row = lax.broadcasted_iota(jnp.int32, (TM, TM), 0)
    col = lax.broadcasted_iota(jnp.int32, (TM, TM), 1)
    same = (row // CHUNK) == (col // CHUNK)
    tri = same & ((col >= row) if reverse else (col <= row))
    return same, tri


def _chunk_order(i, reverse):
    if not reverse:
        return i
    return jnp.where(i < N_CTX_CHUNKS, N_CTX_CHUNKS - 1 - i, N_CHUNKS + N_CTX_CHUNKS - 1 - i)


def _decay_terms(z, lb, same01, tri01):
    f = lb + (1.0 - lb) * _sigmoid(z)
    g = jnp.log(f)
    g2 = jnp.concatenate(_split2(g), axis=1)
    b2 = _dot(tri01, g2)
    t2 = _dot(same01, g2)
    return f, 1.0 - f, b2[:, :HG_DIM] + b2[:, HG_DIM:], t2[:, :HG_DIM] + t2[:, HG_DIM:]


def _chunk_outer(a, b):
    n = TM // CHUNK
    return jnp.einsum('ncv,nck->nvk', a.reshape(n, CHUNK, HG_DIM), b.reshape(n, CHUNK, HG_DIM),
                      preferred_element_type=F32)


def _hgrn_fwd(p_a, lbl, carried=None):
    cpt = TM // CHUNK

    def body(p_ref, lbl_ref, o_ref, st_ref, qd_s, kd_s, u_s, v_s, ebt_s):
        masks = [_chunk_masks(d == 1) for d in (0, 1)]
        same01 = jnp.where(masks[0][0], 1.0, 0.0).astype(BF16)
        tri = [m[1] for m in masks]
        tri01 = [jnp.where(t, 1.0, 0.0).astype(BF16) for t in tri]
        lb = [_sigmoid(lbl_ref[d][0:1, :] - lbl_ref[d][1:2, :]) for d in (0, 1)]

        def prep(r, carry):
            r0 = pl.multiple_of(r * TM, TM)
            vb = p_ref[pl.ds(r0, TM), 2 * HG_DIM:3 * HG_DIM].astype(BF16)
            v_s[pl.ds(r0, TM), :] = vb
            for d in (0, 1):
                z = p_ref[pl.ds(r0, TM), d * HG_DIM:(d + 1) * HG_DIM]
                _, k, b, bt = _decay_terms(z, lb[d], same01, tri01[d])
                u_s[d, pl.ds(r * cpt, cpt)] = _chunk_outer(vb, (k * jnp.exp(bt - b)).astype(BF16))
                ebt_s[d, pl.ds(r0, TM), :] = jnp.exp(bt)

                @pl.when(r >= 1)
                def _():
                    rl = pl.multiple_of(r0 - L, TM)
                    qr = p_ref[pl.ds(r0, TM), 3 * HG_DIM:4 * HG_DIM]
                    q = qr * _sigmoid(qr) * HG_DIM ** -0.5
                    qd_s[d, pl.ds(rl, TM), :] = (q * jnp.exp(b)).astype(BF16)
                    kd_s[d, pl.ds(rl, TM), :] = (k * jnp.exp(-b)).astype(BF16)

            return carry

        lax.fori_loop(0, N_TILES, prep, 0)

        def scan(i, sts):
            new = []
            for d in (0, 1):
                nn = _chunk_order(i, d == 1)
                c0 = pl.multiple_of(nn * CHUNK, CHUNK)
                st_ref[d, nn] = sts[d].astype(BF16)
                new.append(sts[d] * ebt_s[d, pl.ds(c0, 1), :] + u_s[d, nn])
            return tuple(new)

        zero = jnp.zeros((HG_DIM, HG_DIM), F32)
        lax.fori_loop(0, N_CHUNKS, scan, (zero, zero))

        def outp(r, carry):
            r0 = pl.multiple_of(r * TM, TM)
            vb = v_s[pl.ds(r0 + L, TM), :]
            o = jnp.zeros((TM, HG_DIM), F32)
            for d in (0, 1):
                qd = qd_s[d, pl.ds(r0, TM), :]
                a = jnp.where(tri[d], _dot_nt(qd, kd_s[d, pl.ds(r0, TM), :]), 0.0)
                stb = st_ref[d, pl.ds(N_CTX_CHUNKS + r * cpt, cpt)]
                inter = jnp.einsum('nck,nvk->ncv', qd.reshape(cpt, CHUNK, HG_DIM), stb,
                                   preferred_element_type=F32)
                o = o + _dot(a.astype(BF16), vb) + inter.reshape(TM, HG_DIM)
            o_ref[pl.ds(r0, TM), :] = o
            return carry

        lax.fori_loop(0, N_LAT_TILES, outp, 0, unroll=2)

    return _pcall(
        body, carried, name="hgrn_fwd", grid=(HG_HEADS,),
        in_specs=[pl.BlockSpec((T, 4 * HG_DIM), lambda h: (0, h)),
                  pl.BlockSpec((2, 2, HG_DIM), lambda h: (0, 0, h))],
        out_specs=[pl.BlockSpec((S, HG_DIM), lambda h: (0, h)),
                   pl.BlockSpec((2, None, N_CHUNKS, HG_DIM, HG_DIM), lambda h: (0, h, 0, 0, 0))],
        out_shape=[jax.ShapeDtypeStruct((S, HGW), F32),
                   jax.ShapeDtypeStruct((2, HG_HEADS, N_CHUNKS, HG_DIM, HG_DIM), BF16)],
        scratch_shapes=[pltpu.VMEM((2, S, HG_DIM), BF16), pltpu.VMEM((2, S, HG_DIM), BF16),
                        pltpu.VMEM((2, N_CHUNKS, HG_DIM, HG_DIM), F32), pltpu.VMEM((T, HG_DIM), BF16),
                        pltpu.VMEM((2, T, HG_DIM), F32)],
        operands=[p_a, lbl])


def _hgrn_bwd(p_a, lbl, d_o, st, carried=None):
    cpt = TM // CHUNK

    def rows(r):
        return r * TM if isinstance(r, int) else pl.multiple_of(r * TM, TM)

    def body(p_ref, lbl_ref, do_ref, st_ref, dp_ref, dlb_ref, b_s, bt_s, dbt_s, qd_s, dst_s, w_s):
        masks = [_chunk_masks(d == 1) for d in (0, 1)]
        same01 = jnp.where(masks[0][0], 1.0, 0.0).astype(BF16)
        tri = [m[1] for m in masks]
        tri01 = [jnp.where(t, 1.0, 0.0).astype(BF16) for t in tri]
        later01 = [tri01[1], tri01[0]]
        lb = [_sigmoid(lbl_ref[d][0:1, :] - lbl_ref[d][1:2, :]) for d in (0, 1)]

        def prep_tile(r, latent):
            r0 = rows(r)
            for d in (0, 1):
                z = p_ref[pl.ds(r0, TM), d * HG_DIM:(d + 1) * HG_DIM]
                _, _, b, bt = _decay_terms(z, lb[d], same01, tri01[d])
                b_s[d, pl.ds(r0, TM), :] = b
                bt_s[d, pl.ds(r0, TM), :] = bt
                if latent:
                    rl = pl.multiple_of(r0 - L, TM)
                    qr = p_ref[pl.ds(r0, TM), 3 * HG_DIM:4 * HG_DIM]
                    qd = (qr * _sigmoid(qr) * HG_DIM ** -0.5 * jnp.exp(b)).astype(BF16)
                    qd_s[d, pl.ds(rl, TM), :] = qd
                    w_s[d, pl.ds(r * cpt, cpt)] = _chunk_outer(
                        do_ref[pl.ds(rl, TM), :].astype(BF16), qd).astype(BF16)

        prep_tile(0, False)
        w_s[:, pl.ds(0, N_CTX_CHUNKS)] = jnp.zeros((2, N_CTX_CHUNKS, HG_DIM, HG_DIM), BF16)

        def prep(r, carry):
            prep_tile(r, True)
            return carry

        lax.fori_loop(1, N_TILES, prep, 0, unroll=2)

        def rscan(j, dsts):
            i = N_CHUNKS - 1 - j
            new = []
            for d in (0, 1):
                nn = _chunk_order(i, d == 1)
                c0 = pl.multiple_of(nn * CHUNK, CHUNK)
                dst_s[d, nn] = dsts[d].astype(BF16)
                after = st_ref[d, _chunk_order(jnp.minimum(i + 1, N_CHUNKS - 1), d == 1)].astype(F32)
                dbt_s[d, pl.ds(c0, CHUNK), :] = jnp.broadcast_to(
                    jnp.sum(after * dsts[d], axis=0, keepdims=True), (CHUNK, HG_DIM))
                new.append(dsts[d] * jnp.exp(bt_s[d, pl.ds(c0, 1), :]) + w_s[d, nn].astype(F32))
            return tuple(new)

        zero = jnp.zeros((HG_DIM, HG_DIM), F32)
        lax.fori_loop(0, N_CHUNKS, rscan, (zero, zero))

        def grad_tile(r, latent):
            r0 = rows(r)
            vb = p_ref[pl.ds(r0, TM), 2 * HG_DIM:3 * HG_DIM].astype(BF16)
            dv = jnp.zeros((TM, HG_DIM), F32)
            dq = jnp.zeros((TM, HG_DIM), F32)
            dlbs = []
            if latent:
                rl = pl.multiple_of(r0 - L, TM)
                qr = p_ref[pl.ds(r0, TM), 3 * HG_DIM:4 * HG_DIM]
                sq = _sigmoid(qr)
                do = do_ref[pl.ds(rl, TM), :].astype(BF16)
                da_full = _dot_nt(do, vb)
            for d in (0, 1):
                z = p_ref[pl.ds(r0, TM), d * HG_DIM:(d + 1) * HG_DIM]
                sz = _sigmoid(z)
                f = lb[d] + (1.0 - lb[d]) * sz
                k = 1.0 - f
                b = b_s[d, pl.ds(r0, TM), :]
                e2 = jnp.exp(bt_s[d, pl.ds(r0, TM), :] - b)
                dstb = dst_s[d, pl.ds(r * cpt, cpt)]
                kd2 = k * e2
                dkd2 = jnp.einsum('ncv,nvk->nck', vb.reshape(cpt, CHUNK, HG_DIM), dstb,
                                  preferred_element_type=F32).reshape(TM, HG_DIM)
                dv = dv + jnp.einsum('nck,nvk->ncv', kd2.astype(BF16).reshape(cpt, CHUNK, HG_DIM), dstb,
                                     preferred_element_type=F32).reshape(TM, HG_DIM)
                dk = dkd2 * e2
                db = -(kd2 * dkd2)
                if latent:
                    eb = jnp.exp(b)
                    enb = jnp.exp(-b)
                    qdf = qr * sq * HG_DIM ** -0.5 * eb
                    kdf = k * enb
                    qd = qd_s[d, pl.ds(rl, TM), :]
                    kd = kdf.astype(BF16)
                    a = jnp.where(tri[d], _dot_nt(qd, kd), 0.0).astype(BF16)
                    da = jnp.where(tri[d], da_full, 0.0).astype(BF16)
                    stb = st_ref[d, pl.ds(r * cpt, cpt)]
                    dqd = _dot(da, kd) + jnp.einsum(
                        'ncv,nvk->nck', do.reshape(cpt, CHUNK, HG_DIM), stb,
                        preferred_element_type=F32).reshape(TM, HG_DIM)
                    dkd = _dot_tn(da, qd)
                    dv = dv + _dot_tn(a, do)
                    dk = dk + dkd * enb
                    db = db + qdf * dqd - kdf * dkd
                    dq = dq + dqd * eb
                dg = _dot_lhs01(later01[d], db) + dbt_s[d, pl.ds(r0, TM), :]
                df = dg / f - dk
                dp_ref[pl.ds(r0, TM), d * HG_DIM:(d + 1) * HG_DIM] = (
                    df * (1.0 - lb[d]) * sz * (1.0 - sz)).astype(BF16)
                dlbs.append(jnp.sum(df * (1.0 - sz), axis=0, keepdims=True))
            dp_ref[pl.ds(r0, TM), 2 * HG_DIM:3 * HG_DIM] = dv.astype(BF16)
            if latent:
                dq = dq * (HG_DIM ** -0.5) * (sq * (1.0 + qr * (1.0 - sq)))
            dp_ref[pl.ds(r0, TM), 3 * HG_DIM:4 * HG_DIM] = dq.astype(BF16)
            return dlbs

        dlb_ctx = grad_tile(0, False)

        def grads(r, acc):
            t = grad_tile(r, True)
            return (acc[0] + t[0], acc[1] + t[1])

        dlb = lax.fori_loop(1, N_TILES, grads, (dlb_ctx[0], dlb_ctx[1]))
        dlb_ref[0:1, :] = dlb[0]
        dlb_ref[1:2, :] = dlb[1]

    return _pcall(
        body, carried, name="hgrn_bwd", grid=(HG_HEADS,),
        in_specs=[pl.BlockSpec((T, 4 * HG_DIM), lambda h: (0, h)),
                  pl.BlockSpec((2, 2, HG_DIM), lambda h: (0, 0, h)),
                  pl.BlockSpec((S, HG_DIM), lambda h: (0, h)),
                  pl.BlockSpec((2, None, N_CHUNKS, HG_DIM, HG_DIM), lambda h: (0, h, 0, 0, 0))],
        out_specs=[pl.BlockSpec((T, 4 * HG_DIM), lambda h: (0, h)),
                   pl.BlockSpec((2, HG_DIM), lambda h: (0, h))],
        out_shape=[jax.ShapeDtypeStruct((T, WA), BF16), jax.ShapeDtypeStruct((2, HGW), F32)],
        scratch_shapes=[pltpu.VMEM((2, T, HG_DIM), F32), pltpu.VMEM((2, T, HG_DIM), F32),
                        pltpu.VMEM((2, T, HG_DIM), F32), pltpu.VMEM((2, S, HG_DIM), BF16),
                        pltpu.VMEM((2, N_CHUNKS, HG_DIM, HG_DIM), BF16),
                        pltpu.VMEM((2, N_CHUNKS, HG_DIM, HG_DIM), BF16)],
        operands=[p_a, lbl, d_o, st])


def _rope_tables():
    t = np.arange(S)
    inv = ROPE_THETA ** (-np.arange(0, 32, 2, dtype=np.float64) / 32)
    lane = np.arange(64)
    pos = np.where(lane[None, :] < 32, (t // GRID_W)[:, None], (t % GRID_W)[:, None]).astype(np.float64)
    ang = pos * inv[(lane % 32) % 16][None, :]
    sign = np.where((lane % 32) < 16, -1.0, 1.0)[None, :]
    cos = np.tile(np.cos(ang), (1, 2)).astype(np.float32)
    sin = np.tile(np.sin(ang) * sign, (1, 2)).astype(np.float32)
    return jnp.asarray(cos), jnp.asarray(sin)


def _rope_partner(v):
    lane = lax.broadcasted_iota(jnp.int32, (1, 128), 1)
    first = (lane % 32) < 16
    slabs = []
    for j in range(v.shape[1] // 128):
        s = v[:, 128 * j:128 * (j + 1)]
        slabs.append(jnp.where(first, pltpu.roll(s, 112, 1), pltpu.roll(s, 16, 1)))
    return slabs[0] if len(slabs) == 1 else jnp.concatenate(slabs, axis=1)


def _group_ones(width, group):
    r = lax.broadcasted_iota(jnp.int32, (width, width), 0)
    c = lax.broadcasted_iota(jnp.int32, (width, width), 1)
    return jnp.where((r // group) == (c // group), 1.0, 0.0).astype(BF16)


def _group_mean(v, ones01, group):
    hi = v.astype(BF16)
    lo = (v - hi.astype(F32)).astype(BF16)
    return (_dot(hi, ones01) + _dot(lo, ones01)) * (1.0 / group)


def _rep_matrix():
    r = lax.broadcasted_iota(jnp.int32, (KVW, ATW), 0)
    c = lax.broadcasted_iota(jnp.int32, (KVW, ATW), 1)
    return jnp.where(r == HEAD_DIM * (c // 256) + c % HEAD_DIM, 1.0, 0.0).astype(BF16)


def _tile_lanes(v, reps):
    return jnp.concatenate([v] * reps, axis=1)


def _prep_fwd(p_b, o, cos, sin, hnw, qnw, knw):
    def body(p_ref, o_ref, cos_ref, sin_ref, hnw_ref, qnw_ref, knw_ref, y_ref, q_ref, k_ref, v_ref):
        i = pl.program_id(0)
        rep = _rep_matrix()
        ones_k = _group_ones(KVW, HEAD_DIM)
        kr = p_ref[:, 1024:1152]
        krstd = lax.rsqrt(_group_mean(kr * kr, ones_k, HEAD_DIM) + EPS)
        kn = kr * krstd * knw_ref[...]
        v_ref[...] = _dot(p_ref[:, 1152:1280].astype(BF16), rep).astype(BF16)

        @pl.when(i == 0)
        def _():
            k_ref[...] = _dot(kn.astype(BF16), rep).astype(BF16)

        @pl.when(i > 0)
        def _():
            cs, sn = cos_ref[...], sin_ref[...]
            kro = kn * cs + _rope_partner(kn) * sn
            k_ref[...] = _dot(kro.astype(BF16), rep).astype(BF16)
            qr = p_ref[:, 512:1024]
            qrstd = lax.rsqrt(_group_mean(qr * qr, _group_ones(ATW, HEAD_DIM), HEAD_DIM) + EPS)
            qn = qr * qrstd * qnw_ref[...]
            qro = qn * _tile_lanes(cs, 4) + _rope_partner(qn) * _tile_lanes(sn, 4)
            q_ref[...] = (qro * HEAD_DIM ** -0.5).astype(BF16)
            ys = []
            for h in range(HG_HEADS):
                oh = o_ref[:, HG_DIM * h:HG_DIM * (h + 1)]
                gh = p_ref[:, HG_DIM * h:HG_DIM * (h + 1)]
                rstd = lax.rsqrt(jnp.mean(oh * oh, axis=-1, keepdims=True) + EPS)
                ys.append(oh * rstd * hnw_ref[...] * (gh * _sigmoid(gh)))
            y_ref[...] = jnp.concatenate(ys, axis=1).astype(BF16)

    return pl.pallas_call(
        body, name="prep_fwd", grid=(N_TILES,),
        in_specs=[pl.BlockSpec((TM, WB), lambda i: (i, 0)),
                  pl.BlockSpec((TM, HGW), lambda i: (_lat(i), 0)),
                  pl.BlockSpec((TM, 128), lambda i: (_lat(i), 0)),
                  pl.BlockSpec((TM, 128), lambda i: (_lat(i), 0)),
                  _full((1, HG_DIM)), _full((1, ATW)), _full((1, KVW))],
        out_specs=[pl.BlockSpec((TM, HGW), lambda i: (_lat(i), 0)),
                   pl.BlockSpec((TM, ATW), lambda i: (_lat(i), 0)),
                   pl.BlockSpec((TM, ATW), lambda i: (i, 0)),
                   pl.BlockSpec((TM, ATW), lambda i: (i, 0))],
        out_shape=[jax.ShapeDtypeStruct((S, HGW), BF16), jax.ShapeDtypeStruct((S, ATW), BF16),
                   jax.ShapeDtypeStruct((T, ATW), BF16), jax.ShapeDtypeStruct((T, ATW), BF16)],
        compiler_params=_cp(("arbitrary",)),
    )(p_b, o, cos, sin, hnw, qnw, knw)


def _prep_bwd(p_b, o, cos, sin, hnw, qnw, knw, dy_hg, dq, dk_rep, dv_rep, carried=None):
    def body(p_ref, o_ref, cos_ref, sin_ref, hnw_ref, qnw_ref, knw_ref, dy_ref, dq_ref, dk_ref, dv_ref,
             dp_ref, do_ref, acc_ref):
        i = pl.program_id(0)

        @pl.when(i == 0)
        def _():
            acc_ref[...] = jnp.zeros_like(acc_ref)

        rep = _rep_matrix()
        ones_k = _group_ones(KVW, HEAD_DIM)

        def fold(v):
            hi = v.astype(BF16)
            lo = (v - hi.astype(F32)).astype(BF16)
            return _dot_nt(hi, rep) + _dot_nt(lo, rep)

        kr = p_ref[:, 1024:1152]
        krstd = lax.rsqrt(_group_mean(kr * kr, ones_k, HEAD_DIM) + EPS)
        khat = kr * krstd
        kw = knw_ref[...]
        dkro = fold(dk_ref[...])
        dv = fold(dv_ref[...])

        def k_back(dkn):
            dkhat = dkn * kw
            dkr = krstd * (dkhat - khat * _group_mean(dkhat * khat, ones_k, HEAD_DIM))
            acc_ref[2:3, 0:KVW] += jnp.sum(dkn * khat, axis=0, keepdims=True)
            dp_ref[:, 1024:1152] = dkr.astype(BF16)
            dp_ref[:, 1152:1280] = dv.astype(BF16)

        @pl.when(i == 0)
        def _():
            k_back(dkro)
            dp_ref[:, 0:1024] = jnp.zeros((TM, 1024), BF16)

        @pl.when(i > 0)
        def _():
            cs, sn = cos_ref[...], sin_ref[...]
            k_back(dkro * cs + _rope_partner(dkro * sn))
            ones_q = _group_ones(ATW, HEAD_DIM)
            qr = p_ref[:, 512:1024]
            qrstd = lax.rsqrt(_group_mean(qr * qr, ones_q, HEAD_DIM) + EPS)
            qhat = qr * qrstd
            dqro = dq_ref[...] * HEAD_DIM ** -0.5
            dqn = dqro * _tile_lanes(cs, 4) + _rope_partner(dqro * _tile_lanes(sn, 4))
            dqhat = dqn * qnw_ref[...]
            dqr = qrstd * (dqhat - qhat * _group_mean(dqhat * qhat, ones_q, HEAD_DIM))
            acc_ref[1:2, :] += jnp.sum(dqn * qhat, axis=0, keepdims=True)
            dp_ref[:, 512:1024] = dqr.astype(BF16)
            dws = jnp.zeros((1, HG_DIM), F32)
            for h in range(HG_HEADS):
                sl = slice(HG_DIM * h, HG_DIM * (h + 1))
                oh, gh, dy = o_ref[:, sl], p_ref[:, sl], dy_ref[:, sl]
                rstd = lax.rsqrt(jnp.mean(oh * oh, axis=-1, keepdims=True) + EPS)
                ohat = oh * rstd
                sg = _sigmoid(gh)
                dp_ref[:, sl] = (dy * (ohat * hnw_ref[...]) * (sg * (1.0 + gh * (1.0 - sg)))).astype(BF16)
                dn = dy * (gh * sg)
                dws = dws + jnp.sum(dn * ohat, axis=0, keepdims=True)
                dohat = dn * hnw_ref[...]
                do_ref[:, sl] = rstd * (dohat - ohat * jnp.mean(dohat * ohat, axis=-1, keepdims=True))
            acc_ref[0:1, 0:HG_DIM] += dws

    return _pcall(
        body, carried, name="prep_bwd", grid=(N_TILES,),
        in_specs=[pl.BlockSpec((TM, WB), lambda i: (i, 0)),
                  pl.BlockSpec((TM, HGW), lambda i: (_lat(i), 0)),
                  pl.BlockSpec((TM, 128), lambda i: (_lat(i), 0)),
                  pl.BlockSpec((TM, 128), lambda i: (_lat(i), 0)),
                  _full((1, HG_DIM)), _full((1, ATW)), _full((1, KVW)),
                  pl.BlockSpec((TM, HGW), lambda i: (_lat(i), 0)),
                  pl.BlockSpec((TM, ATW), lambda i: (_lat(i), 0)),
                  pl.BlockSpec((TM, ATW), lambda i: (i, 0)),
                  pl.BlockSpec((TM, ATW), lambda i: (i, 0))],
        out_specs=[pl.BlockSpec((TM, WB), lambda i: (i, 0)),
                   pl.BlockSpec((TM, HGW), lambda i: (_lat(i), 0)),
                   _full((8, ATW))],
        out_shape=[jax.ShapeDtypeStruct((T, WB), BF16), jax.ShapeDtypeStruct((S, HGW), F32),
                   jax.ShapeDtypeStruct((8, ATW), F32)],
        scratch_shapes=[], operands=[p_b, o, cos, sin, hnw, qnw, knw, dy_hg, dq, dk_rep, dv_rep])


NEG = -1e30
_CTX_BLOCKS = L // BLOCK


def _attn_window_specs():
    prev = pl.BlockSpec((BLOCK, ATW), lambda i: (jnp.maximum(i - 1, 0) + _CTX_BLOCKS, 0))
    own = pl.BlockSpec((BLOCK, ATW), lambda i: (i + _CTX_BLOCKS, 0))
    nxt = pl.BlockSpec((BLOCK, ATW), lambda i: (jnp.minimum(i + 1, N_BLOCKS - 1) + _CTX_BLOCKS, 0))
    return [prev, own, nxt, _full((L, ATW))]


def _attn_valid(i, heads, context):
    n_keys = 3 * BLOCK + (L if context else 0)
    qi = lax.broadcasted_iota(jnp.int32, (heads * BLOCK, n_keys), 0) % BLOCK
    kj = lax.broadcasted_iota(jnp.int32, (heads * BLOCK, n_keys), 1)
    window = ((jnp.abs(kj - BLOCK - qi) <= BLOCK) & ((kj >= BLOCK) | (i > 0))
              & ((kj < 2 * BLOCK) | (i < N_BLOCKS - 1)))
    return window | (kj >= 3 * BLOCK)


def _stack_heads(qg):
    lane = lax.broadcasted_iota(jnp.int32, (1, 256), 1) // HEAD_DIM
    return jnp.concatenate([jnp.where(lane == g, qg, jnp.zeros_like(qg)) for g in range(4)], axis=0)


def _unstack_heads(v4):
    lane = lax.broadcasted_iota(jnp.int32, (1, 256), 1) // HEAD_DIM
    out = jnp.where(lane == 0, v4[0:BLOCK], 0.0)
    for g in range(1, 4):
        out = out + jnp.where(lane == g, v4[g * BLOCK:(g + 1) * BLOCK], 0.0)
    return out


def _sink_rows(sink_ref, hk):
    return jnp.concatenate(
        [jnp.broadcast_to(sink_ref[0:1, 4 * hk + g:4 * hk + g + 1], (BLOCK, 1)) for g in range(4)], axis=0)


def _attn_fwd(q, k_rep, v_rep, sinks, carried=None):
    def body(q_ref, kp, ko, kn, kc, vp, vo, vn, vc, sink_ref, y_ref, lse_ref):
        i = pl.program_id(0)
        valid = _attn_valid(i, 1, True)
        lane8 = lax.broadcasted_iota(jnp.int32, (1, ATT_HEADS), 1)
        head_of_lane = lax.broadcasted_iota(jnp.int32, (1, 256), 1) // HEAD_DIM
        lse_out = jnp.zeros((BLOCK, ATT_HEADS), F32)
        for hk in range(KV_HEADS):
            sl = slice(256 * hk, 256 * (hk + 1))
            qg = q_ref[:, sl]
            keys = jnp.concatenate([kp[:, sl], ko[:, sl], kn[:, sl], kc[:, sl]], axis=0)
            vals = jnp.concatenate([vp[:, sl], vo[:, sl], vn[:, sl], vc[:, sl]], axis=0)
            yg = jnp.zeros((BLOCK, 256), F32)
            for g in range(4):
                q1 = jnp.where(head_of_lane == g, qg, jnp.zeros_like(qg))
                s = jnp.where(valid, _dot_nt(q1, keys), NEG)
                sink = sink_ref[0:1, 4 * hk + g:4 * hk + g + 1]
                m = jnp.maximum(jnp.max(s, axis=1, keepdims=True), sink)
                p = jnp.exp(s - m)
                den = jnp.sum(p, axis=1, keepdims=True) + jnp.exp(sink - m)
                o1 = _dot(p.astype(BF16), vals) * (1.0 / den)
                yg = yg + jnp.where(head_of_lane == g, o1, 0.0)
                lse_out = lse_out + jnp.where(lane8 == 4 * hk + g, m + jnp.log(den), 0.0)
            y_ref[:, sl] = yg.astype(BF16)
        lse_ref[...] = lse_out

    return _pcall(
        body, carried, name="attn_fwd", grid=(N_BLOCKS,),
        in_specs=[pl.BlockSpec((BLOCK, ATW), lambda i: (i, 0))] + _attn_window_specs()
        + _attn_window_specs() + [_full((1, ATT_HEADS))],
        out_specs=[pl.BlockSpec((BLOCK, ATW), lambda i: (i, 0)),
                   pl.BlockSpec((BLOCK, ATT_HEADS), lambda i: (i, 0))],
        out_shape=[jax.ShapeDtypeStruct((S, ATW), BF16), jax.ShapeDtypeStruct((S, ATT_HEADS), F32)],
        scratch_shapes=[],
        operands=[q, k_rep, k_rep, k_rep, k_rep, v_rep, v_rep, v_rep, v_rep, sinks])


def _attn_bwd(q, k_rep, v_rep, sinks, y_at, lse, dy, carried=None):
    def body(q_ref, kp, ko, kn, kc, vp, vo, vn, vc, sink_ref, y_ref, lse_ref, dy_ref,
             dq_ref, dk_ref, dv_ref, dsink_ref, dk_acc, dv_acc):
        i = pl.program_id(0)

        @pl.when(i == 0)
        def _():
            dk_acc[...] = jnp.zeros_like(dk_acc)
            dv_acc[...] = jnp.zeros_like(dv_acc)
            dk_ref[pl.ds(0, L), :] = jnp.zeros((L, ATW), F32)
            dv_ref[pl.ds(0, L), :] = jnp.zeros((L, ATW), F32)
            dsink_ref[...] = jnp.zeros_like(dsink_ref)

        valid = _attn_valid(i, 4, False)
        lane8 = lax.broadcasted_iota(jnp.int32, (1, ATT_HEADS), 1)
        w0 = pl.multiple_of(i * BLOCK, BLOCK)
        dsink = jnp.zeros((1, ATT_HEADS), F32)
        for hk in range(KV_HEADS):
            sl = slice(256 * hk, 256 * (hk + 1))
            q4 = _stack_heads(q_ref[:, sl])
            do4f = _stack_heads(dy_ref[:, sl])
            o4 = _stack_heads(y_ref[:, sl]).astype(F32)
            do4 = do4f.astype(BF16)
            kl = jnp.concatenate([kp[:, sl], ko[:, sl], kn[:, sl]], axis=0)
            vl = jnp.concatenate([vp[:, sl], vo[:, sl], vn[:, sl]], axis=0)
            lse4 = jnp.concatenate(
                [jnp.sum(jnp.where(lane8 == 4 * hk + g, lse_ref[...], 0.0), axis=1, keepdims=True)
                 for g in range(4)], axis=0)
            p_loc = jnp.where(valid, jnp.exp(_dot_nt(q4, kl) - lse4), 0.0)
            p_ctx = jnp.exp(_dot_nt(q4, kc[:, sl]) - lse4)
            delta = jnp.sum(do4f * o4, axis=1, keepdims=True)
            ds_loc = (p_loc * (_dot_nt(do4, vl) - delta)).astype(BF16)
            ds_ctx = (p_ctx * (_dot_nt(do4, vc[:, sl]) - delta)).astype(BF16)
            dq_ref[:, sl] = _unstack_heads(_dot(ds_loc, kl) + _dot(ds_ctx, kc[:, sl]))
            dk_acc[pl.ds(w0, 3 * BLOCK), sl] += _dot_tn(ds_loc, q4)
            dv_acc[pl.ds(w0, 3 * BLOCK), sl] += _dot_tn(p_loc.astype(BF16), do4)
            dk_ref[pl.ds(0, L), sl] += _dot_tn(ds_ctx, q4)
            dv_ref[pl.ds(0, L), sl] += _dot_tn(p_ctx.astype(BF16), do4)
            p_sink = jnp.exp(_sink_rows(sink_ref, hk) - lse4)
            for g in range(4):
                rows = slice(g * BLOCK, (g + 1) * BLOCK)
                dsink = dsink + jnp.where(lane8 == 4 * hk + g,
                                          -jnp.sum(p_sink[rows] * delta[rows], axis=0, keepdims=True), 0.0)
        dsink_ref[...] += dsink

        @pl.when(i == N_BLOCKS - 1)
        def _():
            dk_ref[pl.ds(L, S), :] = dk_acc[pl.ds(BLOCK, S), :]
            dv_ref[pl.ds(L, S), :] = dv_acc[pl.ds(BLOCK, S), :]

    row_q = pl.BlockSpec((BLOCK, ATW), lambda i: (i, 0))
    return _pcall(
        body, carried, name="attn_bwd", grid=(N_BLOCKS,),
        in_specs=[row_q] + _attn_window_specs() + _attn_window_specs()
        + [_full((1, ATT_HEADS)), row_q, pl.BlockSpec((BLOCK, ATT_HEADS), lambda i: (i, 0)), row_q],
        out_specs=[row_q, _full((T, ATW)), _full((T, ATW)), _full((1, ATT_HEADS))],
        out_shape=[jax.ShapeDtypeStruct((S, ATW), F32), jax.ShapeDtypeStruct((T, ATW), F32),
                   jax.ShapeDtypeStruct((T, ATW), F32), jax.ShapeDtypeStruct((1, ATT_HEADS), F32)],
        scratch_shapes=[pltpu.VMEM((S + 2 * BLOCK, ATW), F32), pltpu.VMEM((S + 2 * BLOCK, ATW), F32)],
        operands=[q, k_rep, k_rep, k_rep, k_rep, v_rep, v_rep, v_rep, v_rep, sinks, y_at, lse, dy])


def _merge_fwd(y_hg, y_at, p_c, x, w_bh, w_ba, w_out, g1, nfw, sh2, sc2, carried=None):
    def body(yh_ref, ya_ref, g_ref, x_ref, wbh_ref, wba_ref, wo_ref, g1_ref, nfw_ref, sh_ref, sc_ref,
             mx_ref, r_ref, x1_ref, h2_ref):
        a = _dot_nt(yh_ref[...], wbh_ref[...])
        b = _dot_nt(ya_ref[...], wba_ref[...])
        mixed = (_sigmoid(g_ref[:, :D]) * a + _sigmoid(g_ref[:, D:]) * b).astype(BF16)
        r = _dot(mixed, wo_ref[...])
        x1 = x_ref[...] + g1_ref[...] * r
        mx_ref[...] = mixed
        r_ref[...] = r
        x1_ref[...] = x1
        h2_ref[...] = _rms_mod(x1, nfw_ref[...], sh_ref[...], sc_ref[...]).astype(BF16)

    row = lambda w: pl.BlockSpec((TM, w), lambda i: (i, 0))
    vec = _full((1, D))
    return _pcall(
        body, carried, name="merge_fwd", grid=(N_LAT_TILES,),
        in_specs=[row(HGW), row(ATW), row(WC), row(D), _VMEM_WHOLE, _VMEM_WHOLE, _VMEM_WHOLE,
                  vec, vec, vec, vec],
        out_specs=[row(D)] * 4,
        out_shape=[jax.ShapeDtypeStruct((S, D), dt) for dt in (BF16, F32, F32, BF16)],
        scratch_shapes=[], operands=[y_hg, y_at, p_c, x, w_bh, w_ba, w_out, g1, nfw, sh2, sc2])


def _merge_bwd(dx1, r, y_hg, y_at, p_c, w_bh, w_ba, w_out, g1, carried=None):
    def body(dx_ref, r_ref, yh_ref, ya_ref, g_ref, wbh_ref, wba_ref, wo_ref, g1_ref,
             dr_ref, da_ref, db_ref, dg_ref, dyh_ref, dya_ref, acc_ref):
        @pl.when(pl.program_id(0) == 0)
        def _():
            acc_ref[...] = jnp.zeros_like(acc_ref)

        dx1v = dx_ref[...]
        acc_ref[0:1, :] += jnp.sum(dx1v * r_ref[...], axis=0, keepdims=True)
        dr = (g1_ref[...] * dx1v).astype(BF16)
        dr_ref[...] = dr
        dmix = _dot_nt(dr, wo_ref[...])
        sh, sa = _sigmoid(g_ref[:, :D]), _sigmoid(g_ref[:, D:])
        da = (dmix * sh).astype(BF16)
        db = (dmix * sa).astype(BF16)
        da_ref[...] = da
        db_ref[...] = db
        dg_ref[:, :D] = (dmix * _dot_nt(yh_ref[...], wbh_ref[...]) * sh * (1.0 - sh)).astype(BF16)
        dg_ref[:, D:] = (dmix * _dot_nt(ya_ref[...], wba_ref[...]) * sa * (1.0 - sa)).astype(BF16)
        dyh_ref[...] = _dot(da, wbh_ref[...])
        dya_ref[...] = _dot(db, wba_ref[...])

    row = lambda w: pl.BlockSpec((TM, w), lambda i: (i, 0))
    return _pcall(
        body, carried, name="merge_bwd", grid=(N_LAT_TILES,),
        in_specs=[row(D), row(D), row(HGW), row(ATW), row(WC), _VMEM_WHOLE, _VMEM_WHOLE, _VMEM_WHOLE,
                  _full((1, D))],
        out_specs=[row(D), row(D), row(D), row(WC), row(HGW), row(ATW), _full((8, D))],
        out_shape=[jax.ShapeDtypeStruct((S, D), BF16), jax.ShapeDtypeStruct((S, D), BF16),
                   jax.ShapeDtypeStruct((S, D), BF16), jax.ShapeDtypeStruct((S, WC), BF16),
                   jax.ShapeDtypeStruct((S, HGW), F32), jax.ShapeDtypeStruct((S, ATW), F32),
                   jax.ShapeDtypeStruct((8, D), F32)],
        scratch_shapes=[], operands=[dx1, r, y_hg, y_at, p_c, w_bh, w_ba, w_out, g1])


def _ffn_fused(x1, h2, tgt, w_gate, w_up, w_down, g2, nfw, sc2):
    def body(x1_ref, h2_ref, t_ref, wg_ref, wu_ref, wd_ref, g2_ref, nfw_ref, sc_ref,
             act_ref, dgt_ref, dup_ref, df_ref, dx_ref, acc_ref, gs, us):
        @pl.when(pl.program_id(0) == 0)
        def _():
            acc_ref[...] = jnp.zeros_like(acc_ref)

        h2 = h2_ref[...]
        whole = lambda w_ref: w_ref[...].reshape(D_FF, D)
        tile = lambda j: slice(j * FF_TILE, (j + 1) * FF_TILE)
        for j in range(N_FF_TILES):
            g = _dot_nt(h2, wg_ref[j])
            u = _dot_nt(h2, wu_ref[j])
            gs[j] = g
            us[j] = u
            act_ref[:, tile(j)] = (g * _sigmoid(g) * u).astype(BF16)
        f = _dot(act_ref[...], whole(wd_ref))
        x1v = x1_ref[...]
        g2 = g2_ref[...]
        diff = x1v + g2 * f - t_ref[...]
        dy = diff * (1.0 / D)
        df = (g2 * dy).astype(BF16)
        df_ref[...] = df
        dact_all = _dot_nt(df, whole(wd_ref))
        for j in range(N_FF_TILES):
            g, u = gs[j], us[j]
            sg = _sigmoid(g)
            dact = dact_all[:, tile(j)]
            dgt_ref[:, tile(j)] = (dact * u * (sg * (1.0 + g * (1.0 - sg)))).astype(BF16)
            dup_ref[:, tile(j)] = (dact * (g * sg)).astype(BF16)
        dh2 = _dot(dgt_ref[...], whole(wg_ref)) + _dot(dup_ref[...], whole(wu_ref))
        dx, dsh, dsc, dnw = _rms_mod_bwd(x1v, nfw_ref[...], sc_ref[...], dh2)
        dx_ref[...] = dy + dx
        acc_ref[0:1, :] += dsh
        acc_ref[1:2, :] += dsc
        acc_ref[2:3, :] += dnw
        acc_ref[3:4, :] += jnp.sum(dy * f, axis=0, keepdims=True)
        acc_ref[4:5, :] += 0.5 * jnp.sum(jnp.sum(diff * diff, axis=1, keepdims=True), axis=0,
                                         keepdims=True) * (1.0 / D)

    row = lambda dt_w: pl.BlockSpec((TM, dt_w), lambda i: (i, 0))
    blk = row(D_FF)
    vec = _full((1, D))
    return pl.pallas_call(
        body, name="ffn_fused", grid=(N_LAT_TILES,),
        in_specs=[row(D), row(D), row(D), _VMEM_WHOLE, _VMEM_WHOLE, _VMEM_WHOLE, vec, vec, vec],
        out_specs=[blk, blk, blk, row(D), row(D), _full((8, D))],
        out_shape=[jax.ShapeDtypeStruct((S, D_FF), BF16)] * 3
        + [jax.ShapeDtypeStruct((S, D), BF16), jax.ShapeDtypeStruct((S, D), F32),
           jax.ShapeDtypeStruct((8, D), F32)],
        scratch_shapes=[pltpu.VMEM((N_FF_TILES, TM, FF_TILE), F32), pltpu.VMEM((N_FF_TILES, TM, FF_TILE), F32)],
        compiler_params=_cp(("arbitrary",)),
    )(x1, h2, tgt, w_gate, w_up, w_down, g2, nfw, sc2)


def _proj_bc(h_all, w_b, w_c, carried=None):
    def body(h_ref, wb_ref, wc_ref, pb_ref, pc_ref):
        h = h_ref[...]
        pb_ref[...] = _dot_nt(h, wb_ref[...])

        @pl.when(pl.program_id(0) > 0)
        def _():
            pc_ref[...] = _dot_nt(h, wc_ref[...])

    return _pcall(
        body, carried, name="proj_bc", grid=(N_TILES,),
        in_specs=[pl.BlockSpec((TM, D), lambda i: (i, 0)), _VMEM_WHOLE, _VMEM_WHOLE],
        out_specs=[pl.BlockSpec((TM, WB), lambda i: (i, 0)), pl.BlockSpec((TM, WC), lambda i: (_lat(i), 0))],
        out_shape=[jax.ShapeDtypeStruct((T, WB), F32), jax.ShapeDtypeStruct((S, WC), F32)],
        scratch_shapes=[], operands=[h_all, w_b, w_c])


def _input_bwd(dp_a, dp_b, dp_c, w_a, w_b, w_c, ctx, x, dx1, nw, sh, sc, carried=None):
    def body(da_ref, db_ref, dc_ref, wa_ref, wb_ref, wc_ref, ctx_ref, x_ref, dx1_ref, nw_ref, sh_ref,
             sc_ref, gx_ref, acc_ref):
        i = pl.program_id(0)

        @pl.when(i == 0)
        def _():
            acc_ref[...] = jnp.zeros_like(acc_ref)

        dh = _dot(da_ref[...], wa_ref[...]) + _dot(db_ref[...], wb_ref[...])

        @pl.when(i == 0)
        def _():
            _, dsh, dsc, dnw = _rms_mod_bwd(ctx_ref[...], nw_ref[...], sc_ref[0:1, :], dh)
            acc_ref[3:4, :] += dsh
            acc_ref[4:5, :] += dsc
            acc_ref[2:3, :] += dnw

        @pl.when(i > 0)
        def _():
            dhl = dh + _dot(dc_ref[...], wc_ref[...])
            dx, dsh, dsc, dnw = _rms_mod_bwd(x_ref[...], nw_ref[...], sc_ref[1:2, :], dhl)
            gx_ref[...] = dx1_ref[...] + dx
            acc_ref[0:1, :] += dsh
            acc_ref[1:2, :] += dsc
            acc_ref[2:3, :] += dnw

    lat = lambda w: pl.BlockSpec((TM, w), lambda i: (_lat(i), 0))
    return _pcall(
        body, carried, name="input_bwd", grid=(N_TILES,),
        in_specs=[pl.BlockSpec((TM, WA), lambda i: (i, 0)), pl.BlockSpec((TM, WB), lambda i: (i, 0)),
                  lat(WC), _VMEM_WHOLE, _VMEM_WHOLE, _VMEM_WHOLE, _full((TM, D)), lat(D), lat(D),
                  _full((1, D)), _full((2, D)), _full((2, D))],
        out_specs=[lat(D), _full((8, D))],
        out_shape=[jax.ShapeDtypeStruct((S, D), F32), jax.ShapeDtypeStruct((8, D), F32)],
        scratch_shapes=[], operands=[dp_a, dp_b, dp_c, w_a, w_b, w_c, ctx, x, dx1, nw, sh, sc])


_C1 = 1.0 - ADAM_B1 ** ADAM_STEP
_C2 = 1.0 - ADAM_B2 ** ADAM_STEP


def _adamw_math(w, g, m, v):
    m = ADAM_B1 * m + (1.0 - ADAM_B1) * g
    v = ADAM_B2 * v + (1.0 - ADAM_B2) * (g * g)
    m_hat = m / _C1
    v_hat = v / _C2
    delta = -ADAM_LR * (m_hat / (jnp.sqrt(v_hat) + ADAM_EPS) + ADAM_WD * w)
    return delta, m, v


def _adamw_sharded(terms, w, m, v, name, tr, extra=None, after=None):
    rows, cols = w.shape

    def body(*refs):
        t_ref, w_ref, m_ref, v_ref = refs[:4]
        g_ref, d_ref, nm_ref, nv_ref = refs[-4:]
        g = t_ref[0].astype(F32)
        for s in range(1, N_CHIPS):
            g = g + t_ref[s].astype(F32)
        if extra is not None:
            g = g + refs[4][...].astype(F32)
        g_ref[...] = g
        d_ref[...], nm_ref[...], nv_ref[...] = _adamw_math(w_ref[...], g, m_ref[...], v_ref[...])

    blk = pl.BlockSpec((tr, cols), lambda i: (i, 0))
    return pl.pallas_call(
        body, name=name, grid=(rows // tr,),
        in_specs=[pl.BlockSpec((N_CHIPS, tr, cols), lambda i: (0, i, 0)), blk, blk, blk]
        + ([blk] if extra is not None else []) + ([_ANY] if after is not None else []),
        out_specs=[blk] * 4,
        out_shape=[jax.ShapeDtypeStruct((rows, cols), F32)] * 4,
        compiler_params=_cp(("parallel",)),
    )(terms, w, m, v, *([extra] if extra is not None else []), *([after] if after is not None else []))


def _adamw_plain(g, w, m, v, name, tr=None):
    def body(g_ref, w_ref, m_ref, v_ref, d_ref, nm_ref, nv_ref):
        d_ref[...], nm_ref[...], nv_ref[...] = _adamw_math(w_ref[...], g_ref[...], m_ref[...], v_ref[...])

    if tr is None:
        return pl.pallas_call(
            body, name=name, in_specs=[_VMEM_WHOLE] * 4, out_specs=[_VMEM_WHOLE] * 3,
            out_shape=[jax.ShapeDtypeStruct(w.shape, F32)] * 3,
            compiler_params=_cp(),
        )(g, w, m, v)
    blk = pl.BlockSpec((tr, w.shape[1]), lambda i: (i, 0))
    return pl.pallas_call(
        body, name=name, grid=(w.shape[0] // tr,), in_specs=[blk] * 4, out_specs=[blk] * 3,
        out_shape=[jax.ShapeDtypeStruct(w.shape, F32)] * 3,
        compiler_params=_cp(("parallel",)),
    )(g, w, m, v)


SMALL_ROWS = 16
R_DMOD, R_DCTX, R_NMIX, R_NFFN, R_MISC, R_DLB, R_BADA01 = 0, 6, 8, 9, 10, 11, 13
M_HNW, M_QNW, M_KNW, M_SINK, M_LOSS = 0, 128, 256, 384, 512


def _pack_small(acc_in, acc_mg, acc_ffn, acc_prep, dsink, dlb):
    def body(in_ref, mg_ref, ff_ref, pp_ref, ds_ref, dlb_ref, o_ref):
        o_ref[...] = jnp.zeros_like(o_ref)
        o_ref[0:2, :] = in_ref[0:2, :]
        o_ref[2:3, :] = mg_ref[0:1, :]
        o_ref[3:5, :] = ff_ref[0:2, :]
        o_ref[5:6, :] = ff_ref[3:4, :]
        o_ref[6:8, :] = in_ref[3:5, :]
        o_ref[8:9, :] = in_ref[2:3, :]
        o_ref[9:10, :] = ff_ref[2:3, :]
        o_ref[10:11, M_HNW:M_HNW + HG_DIM] = pp_ref[0:1, 0:HG_DIM]
        r = lax.broadcasted_iota(jnp.int32, (ATW, 128), 0)
        c = lax.broadcasted_iota(jnp.int32, (ATW, 128), 1)
        fold = jnp.where((r % HEAD_DIM == c) & (c < HEAD_DIM), 1.0, 0.0).astype(BF16)
        qk = jnp.concatenate([pp_ref[1:2, :], pp_ref[2:3, :], jnp.zeros((6, ATW), F32)], axis=0)
        folded = _dot_exact_rhs01(qk, fold)
        o_ref[10:11, M_QNW:M_QNW + 128] = folded[0:1, :]
        o_ref[10:11, M_KNW:M_KNW + 128] = folded[1:2, :]
        o_ref[10:11, M_SINK:M_SINK + ATT_HEADS] = ds_ref[...]
        o_ref[10:11, M_LOSS:M_LOSS + 128] = ff_ref[4:5, 0:128]
        o_ref[11:13, 0:HGW] = dlb_ref[...]

    return pl.pallas_call(
        body, name="pack_small", in_specs=[_VMEM_WHOLE] * 6, out_specs=_VMEM_WHOLE,
        out_shape=jax.ShapeDtypeStruct((SMALL_ROWS, D), F32), compiler_params=_cp(),
    )(acc_in, acc_mg, acc_ffn, acc_prep, dsink, dlb)


def _sum_small(gathered):
    def body(g_ref, o_ref):
        tot = g_ref[0]
        for s in range(1, N_DEV):
            tot = tot + g_ref[s]
        o_ref[...] = tot
        o_ref[R_BADA01:R_BADA01 + 2, :] = tot[0:2, :] + tot[R_DCTX:R_DCTX + 2, :]

    return pl.pallas_call(
        body, name="sum_small", in_specs=[_VMEM_WHOLE], out_specs=_VMEM_WHOLE,
        out_shape=jax.ShapeDtypeStruct((SMALL_ROWS, D), F32), compiler_params=_cp(),
    )(gathered)


_REP_NAMES = ("b_ada", "c_ctx", "norm_mix_w", "norm_ffn_w", "hgrn_norm_w", "q_norm_w", "k_norm_w", "attn_sinks")


def _adamw_replicated(tot, g_c_ctx, ws, ms, vs):
    n = len(_REP_NAMES)

    def body(*refs):
        tot_ref, gc_ref = refs[0], refs[1]
        w_refs, m_refs, v_refs = refs[2:2 + n], refs[2 + n:2 + 2 * n], refs[2 + 2 * n:2 + 3 * n]
        outs = refs[2 + 3 * n:]
        row = lambda r: tot_ref[r:r + 1, :]
        misc = row(R_MISC)
        grads = [jnp.concatenate([row(R_BADA01), row(R_BADA01 + 1)] + [row(k) for k in range(2, 6)], axis=1),
                 gc_ref[...], row(R_NMIX), row(R_NFFN),
                 misc[:, M_HNW:M_HNW + HG_DIM], misc[:, M_QNW:M_QNW + HEAD_DIM],
                 misc[:, M_KNW:M_KNW + HEAD_DIM], misc[:, M_SINK:M_SINK + ATT_HEADS]]
        for k in range(n):
            outs[k][...] = grads[k]
            outs[n + k][...], outs[2 * n + k][...], outs[3 * n + k][...] = _adamw_math(
                w_refs[k][...], grads[k], m_refs[k][...], v_refs[k][...])

    shapes = [jax.ShapeDtypeStruct(w.shape, F32) for w in ws]
    return pl.pallas_call(
        body, name="adamw_replicated", in_specs=[_VMEM_WHOLE] * (2 + 3 * n), out_specs=[_VMEM_WHOLE] * (4 * n),
        out_shape=shapes * 4, compiler_params=_cp(),
    )(tot, g_c_ctx, *ws, *ms, *vs)


def _lb_grads(dlb, lbl):
    def body(d_ref, l_ref, o_ref):
        for d in (0, 1):
            ll = l_ref[d]
            lb = _sigmoid(ll[0:1, :] - ll[1:2, :])
            t = d_ref[d:d + 1, :] * lb * (1.0 - lb)
            o_ref[d, 0:1, :] = t
            o_ref[d, 1:2, :] = -t

    return pl.pallas_call(
        body, name="lb_grads", in_specs=[_VMEM_WHOLE] * 2, out_specs=_VMEM_WHOLE,
        out_shape=jax.ShapeDtypeStruct((2, 2, HGW), F32), compiler_params=_cp(),
    )(dlb, lbl)


def _c_ctx_grad(terms, c_ctx):
    def body(t_ref, c_ref, o_ref):
        tot = t_ref[0, 8:9, :]
        for s in range(1, N_DEV):
            tot = tot + t_ref[s, 8:9, :]
        cv = c_ref[...]
        sg = _sigmoid(cv)
        o_ref[...] = tot * (sg * (1.0 + cv * (1.0 - sg)))

    return pl.pallas_call(
        body, name="c_ctx_grad", in_specs=[_VMEM_WHOLE] * 2, out_specs=_VMEM_WHOLE,
        out_shape=jax.ShapeDtypeStruct((1, D), F32), compiler_params=_cp(),
    )(terms, c_ctx)


def _in_perm():
    fz, bz, inp, kk, vv, qhg, ghg, qat, gates = 0, 512, 1024, 1536, 1664, 1792, 2304, 2816, 3328
    cols = []
    for h in range(HG_HEADS):
        for base in (fz, bz, inp, qhg):
            cols += list(range(base + 128 * h, base + 128 * (h + 1)))
    cols += list(range(ghg, ghg + 512)) + list(range(qat, qat + 512))
    cols += list(range(kk, kk + 128)) + list(range(vv, vv + 128))
    cols += list(range(gates, gates + 2048))
    return np.asarray(cols, np.int32)


_PERM = _in_perm()


_PIECES = {"a": (0, WA, 128), "b": (WA, WB, 256), "c": (WA + WB, WC, 256)}


def _block_table(piece):
    lo, n, blk = _PIECES[piece]
    starts = [int(_PERM[r]) for r in range(lo, lo + n, blk)]
    assert all(s % blk == 0 and np.array_equal(_PERM[r:r + blk], np.arange(s, s + blk))
               for s, r in zip(starts, range(lo, lo + n, blk)))
    return jnp.asarray([s // blk for s in starts], jnp.int32), blk


def _pick_row_blocks(x, table, blk, name):
    cols = x.shape[1]

    def body(t_ref, x_ref, o_ref):
        o_ref[...] = x_ref[...]

    return pl.pallas_call(
        body, name=name,
        grid_spec=pltpu.PrefetchScalarGridSpec(
            num_scalar_prefetch=1, grid=(table.shape[0],),
            in_specs=[pl.BlockSpec((blk, cols), lambda i, t: (t[i], 0))],
            out_specs=pl.BlockSpec((blk, cols), lambda i, t: (i, 0))),
        out_shape=jax.ShapeDtypeStruct((table.shape[0] * blk, cols), x.dtype),
        compiler_params=_cp(("arbitrary",)),
    )(table, x)


def _mm_tn_placed(a, b, table, blk, into, out_rows, name):
    k, n = b.shape

    def body(t_ref, a_ref, b_ref, *rest):
        rest[-1][...] = _dot_tn(a_ref[...], b_ref[...]).astype(BF16)

    operands = [table, a, b]
    in_specs, aliases = [pl.BlockSpec((k, blk), lambda i, t: (0, i)), pl.BlockSpec((k, n), lambda i, t: (0, 0))], {}
    if into is not None:
        operands.append(into)
        in_specs.append(_ANY)
        aliases = {3: 0}
    return pl.pallas_call(
        body, name=name,
        grid_spec=pltpu.PrefetchScalarGridSpec(
            num_scalar_prefetch=1, grid=(table.shape[0],), in_specs=in_specs,
            out_specs=pl.BlockSpec((blk, n), lambda i, t: (t[i], 0))),
        out_shape=jax.ShapeDtypeStruct((out_rows, n), BF16),
        input_output_aliases=aliases,
        compiler_params=_cp(("arbitrary",)),
    )(*operands)


def _local_step(x2, ctx2, h_all, h_lat, tgt, lbl, sh_in, sc_in, gate1, sh2, sc2, gate2, norm_mix_w, norm_ffn_w,
                hgrn_norm_w, q_norm_w, k_norm_w, attn_sinks, w_a, w_b, w_c, s_bh, s_ba, s_out,
                s_gate, s_up, s_down):
    first_last = lambda n: [(0, True), (n - 1, False)]
    p_a = _mm_nt(h_all, w_a, tm=T, tn=512, out_dtype=F32, name="proj_a")
    (o, st), (g_gate, g_bh, g_ba) = _hgrn_fwd(
        p_a, lbl, (_gather_comm_relayed([s_gate, s_bh, s_ba]),
                   [(0, True), (HG_HEADS - 2, True), (HG_HEADS - 1, False)]))
    (p_b, p_c), (g_out,) = _proj_bc(
        h_all, w_b, w_c, (_gather_comm_relayed([s_out]), [(0, True), (N_TILES - 4, True), (N_TILES - 1, False)]))
    cos, sin = _rope_tables()
    qnw_t, knw_t = jnp.tile(q_norm_w, (1, ATT_HEADS)), jnp.tile(k_norm_w, (1, KV_HEADS))
    y_hg, qn, k_rep, v_rep = _prep_fwd(p_b, o, cos, sin, hgrn_norm_w, qnw_t, knw_t)
    (y_at, lse), (g_up, g_down) = _attn_fwd(
        qn, k_rep, v_rep, attn_sinks,
        (_gather_comm_relayed([s_up, s_down]), [(0, True), (N_BLOCKS - 6, True), (N_BLOCKS - 1, False)]))
    w_bh, w_ba, w_o = g_bh.reshape(D, HGW), g_ba.reshape(D, ATW), g_out.reshape(D, D)
    (mixed, r, x1, h2), _ = _merge_fwd(
        y_hg, y_at, p_c, x2, w_bh, w_ba, w_o, gate1, norm_ffn_w, sh2, sc2)
    g_gate, g_up, g_down = [g.reshape(N_FF_TILES, FF_TILE, D) for g in (g_gate, g_up, g_down)]

    act, d_gate, d_up, d_f, dx1, acc_ffn = _ffn_fused(x1, h2, tgt, g_gate, g_up, g_down, gate2,
                                                      norm_ffn_w, sc2)
    by_chip = lambda t: t.reshape((N_CHIPS, 2) + t.shape[1:])
    ff_by_chip = lambda t: t.reshape(N_CHIPS, 2, FF_BLK, D)
    t_down, _ = _mm_tn_blocked(act, d_f, "grad_down", N_FF_TILES)
    t_down = ff_by_chip(t_down)
    t_gate, (f_down,) = _mm_tn_blocked(d_gate, h2, "grad_gate", N_FF_HALVES,
                                       (_sibling_comm([t_down]), first_last(N_FF_HALVES)))
    t_gate = ff_by_chip(t_gate)
    t_up, (f_gate,) = _mm_tn_blocked(d_up, h2, "grad_up", N_FF_HALVES,
                                     (_sibling_comm([t_gate]), first_last(N_FF_HALVES)))
    t_up = ff_by_chip(t_up)

    (d_r, d_a, d_b, dp_c, dy_hg, dy_at, acc_mg), (f_up,) = _merge_bwd(
        dx1, r, y_hg, y_at, p_c, w_bh, w_ba, w_o, gate1, (_sibling_comm([t_up]), first_last(N_LAT_TILES)))
    c_down, c_gate, c_up = [_pair_sum(t, f, "pair_sum_" + nm) for t, f, nm in
                            ((t_down, f_down, "down"), (t_gate, f_gate, "gate"), (t_up, f_up, "up"))]
    t_out = _mm_tn(mixed, d_r, tk=1024, nk=2, tm=1024, tn=1024, out_dtype=BF16, name="grad_out")
    t_bh = _mm_tn(d_a, y_hg, tk=2048, nk=1, tm=1024, tn=512, out_dtype=BF16, name="grad_bh")
    t_ba = _mm_tn(d_b, y_at, tk=2048, nk=1, tm=1024, tn=512, out_dtype=BF16, name="grad_ba")
    t_bh, t_ba, t_out = [by_chip(t.reshape(N_DEV, D // N_DEV, t.shape[1])) for t in (t_bh, t_ba, t_out)]
    (dq, dk_rep, dv_rep, dsink), (r_up,) = _attn_bwd(
        qn, k_rep, v_rep, attn_sinks, y_at, lse, dy_at, (_chip_comm([c_up]), first_last(N_BLOCKS)))
    (dp_b, d_o, acc_prep), (f_bh, f_ba, f_out) = _prep_bwd(
        p_b, o, cos, sin, hgrn_norm_w, qnw_t, knw_t, dy_hg, dq, dk_rep, dv_rep,
        (_sibling_comm([t_bh, t_ba, t_out]), first_last(N_TILES)))
    c_bh, c_ba, c_out = [_pair_sum(t, f, "pair_sum_" + nm) for t, f, nm in
                         ((t_bh, f_bh, "bh"), (t_ba, f_ba, "ba"), (t_out, f_out, "out"))]
    (dp_a, dlb), (r_bh, r_ba, r_out, r_down, r_gate) = _hgrn_bwd(
        p_a, lbl, d_o, st, (_chip_comm([c_bh, c_ba, c_out, c_down, c_gate]), first_last(HG_HEADS)))
    t_in = None
    for dp, h, nm in ((dp_a, h_all, "a"), (dp_b, h_all, "b"), (dp_c, h_lat, "c")):
        t_in = _mm_tn_placed(dp, h, *_block_table(nm), t_in, IN_COLS, "grad_in_" + nm)
    t_in = by_chip(t_in.reshape(N_DEV, IN_BLK, D))
    (f_in,) = _run_comm(_sibling_comm([t_in]), "scatter_in_sibling")
    c_in = _pair_sum(t_in, f_in, "pair_sum_in")
    sems, c_in, land, token = _chip_exchange_start(c_in, jnp.zeros(c_in.shape, c_in.dtype))
    (grad_x, acc_in), _ = _input_bwd(dp_a, dp_b, dp_c, w_a, w_b, w_c, ctx2, x2, dx1,
                                     norm_mix_w + token[0, 0], sh_in, sc_in)
    small = _pack_small(acc_in, acc_mg, acc_ffn, acc_prep, dsink, dlb)
    return grad_x, small, [r_bh, r_ba, r_out, r_gate, r_up, r_down], (sems, c_in, land)


def kernel(x, c, ctx, c_ctx, w_ada, b_ada, norm_mix_w, norm_ffn_w, w_in, hgrn_lb_logits, hgrn_norm_w, q_norm_w, k_norm_w, attn_sinks, w_branch_hgrn, w_branch_attn, w_out, w_ffn_gate, w_ffn_up, w_ffn_down, loss_target, m_c_ctx, m_w_ada, m_b_ada, m_norm_mix_w, m_norm_ffn_w, m_w_in, m_hgrn_lb_logits, m_hgrn_norm_w, m_q_norm_w, m_k_norm_w, m_attn_sinks, m_w_branch_hgrn, m_w_branch_attn, m_w_out, m_w_ffn_gate, m_w_ffn_up, m_w_ffn_down, v_c_ctx, v_w_ada, v_b_ada, v_norm_mix_w, v_norm_ffn_w, v_w_in, v_hgrn_lb_logits, v_hgrn_norm_w, v_q_norm_w, v_k_norm_w, v_attn_sinks, v_w_branch_hgrn, v_w_branch_attn, v_w_out, v_w_ffn_gate, v_w_ffn_up, v_w_ffn_down):
    me = 4 * lax.axis_index("x") + 2 * lax.axis_index("y") + lax.axis_index("c")
    x2, ctx2, tgt = x[0], ctx[0], loss_target[0]
    w_ada2, w_in2 = w_ada[0], w_in[0]

    cond = jnp.zeros((8, D), F32).at[0].set(c[0]).at[1, :256].set(hgrn_lb_logits.reshape(256))
    b_cols = lax.dynamic_slice(b_ada, (0, me * ADA_BLK), (1, ADA_BLK))
    g0, cc, mod, g_in, h_all, h_lat = _prologue(cond, c_ctx.reshape(1, D), w_ada2, b_cols, w_in2.T.astype(BF16),
                                         x2, ctx2, norm_mix_w)
    lbl = jnp.transpose(g0[:, 1, :256].reshape(N_DEV, 2, 2, 64), (1, 2, 0, 3)).reshape(2, 2, HGW)
    sh1, sc1, gate1, sh2, sc2, gate2 = [mod[k:k + 1] for k in range(6)]
    sh_in = jnp.concatenate([mod[6:7], sh1], axis=0)
    sc_in = jnp.concatenate([mod[7:8], sc1], axis=0)

    shards = [w_branch_hgrn[0].T, w_branch_attn[0].T, w_out[0], w_ffn_gate[0].T, w_ffn_up[0].T, w_ffn_down[0]]
    w_in_t = g_in.reshape(IN_COLS, D)
    w_a, w_b, w_c = [_pick_row_blocks(w_in_t, *_block_table(nm), "order_w_" + nm) for nm in "abc"]

    grad_x, small, (r_bh, r_ba, r_out, r_gate, r_up, r_down), pending_in = _local_step(
        x2, ctx2, h_all, h_lat, tgt, lbl, sh_in, sc_in, gate1, sh2, sc2, gate2, norm_mix_w, norm_ffn_w, hgrn_norm_w,
        q_norm_w, k_norm_w, attn_sinks, w_a, w_b, w_c, *[s.astype(BF16) for s in shards])

    big, updated = {}, []
    for nm, rr, ww, mm, vv, tr, transposed in (
            ("w_branch_hgrn", r_bh, w_branch_hgrn[0], m_w_branch_hgrn[0], v_w_branch_hgrn[0], 128, True),
            ("w_branch_attn", r_ba, w_branch_attn[0], m_w_branch_attn[0], v_w_branch_attn[0], 128, True),
            ("w_out", r_out, w_out[0], m_w_out[0], v_w_out[0], 128, False),
            ("w_ffn_gate", r_gate, w_ffn_gate[0], m_w_ffn_gate[0], v_w_ffn_gate[0], 176, True),
            ("w_ffn_up", r_up, w_ffn_up[0], m_w_ffn_up[0], v_w_ffn_up[0], 176, True),
            ("w_ffn_down", r_down, w_ffn_down[0], m_w_ffn_down[0], v_w_ffn_down[0], 176, False)):
        if transposed:
            res = _adamw_sharded(rr, ww.T, mm.T, vv.T, "adamw_" + nm, tr, after=grad_x)
            big[nm] = [t.T[None] for t in res]
        else:
            res = _adamw_sharded(rr, ww, mm, vv, "adamw_" + nm, tr, after=grad_x)
            big[nm] = [t[None] for t in res]
        updated.append(res[1])

    (g2,) = _all_gather([small], "gather_small", True, after=updated)
    tot = _sum_small(g2)
    dm = jnp.zeros((16, 6 * D), F32).at[:8].set(g2[:, R_DMOD:R_DMOD + 6, :].reshape(N_DEV, 6 * D))
    dm = dm.at[8, :2 * D].set(tot[R_DCTX:R_DCTX + 2].reshape(2 * D))
    dm_cols = lax.dynamic_slice(dm, (0, me * ADA_BLK), (16, ADA_BLK))
    g_w_ada, dsc_term = _ada_grads(cc, dm_cols, w_ada2)
    (g3,) = _all_gather([dsc_term], "gather_cctx", True)
    g_c_ctx = _c_ctx_grad(g3, c_ctx.reshape(1, D))
    g_lbl = _lb_grads(tot[R_DLB:R_DLB + 2, :HGW], lbl)
    g_lb_mine = lax.dynamic_slice(g_lbl, (0, 0, me * 64), (2, 2, 64))
    misc = tot[R_MISC]
    loss = misc[M_LOSS]

    rep_out = _adamw_replicated(
        tot, g_c_ctx,
        [b_ada, c_ctx.reshape(1, D), norm_mix_w, norm_ffn_w, hgrn_norm_w, q_norm_w, k_norm_w, attn_sinks],
        [m_b_ada, m_c_ctx.reshape(1, D), m_norm_mix_w, m_norm_ffn_w, m_hgrn_norm_w, m_q_norm_w, m_k_norm_w,
         m_attn_sinks],
        [v_b_ada, v_c_ctx.reshape(1, D), v_norm_mix_w, v_norm_ffn_w, v_hgrn_norm_w, v_q_norm_w, v_k_norm_w,
         v_attn_sinks])
    rep = []
    for kind in range(4):
        vals = dict(zip(_REP_NAMES, rep_out[kind * len(_REP_NAMES):(kind + 1) * len(_REP_NAMES)]))
        vals["c_ctx"] = vals["c_ctx"].reshape(D)
        rep.append(vals)

    sems, c_in, land = pending_in
    d_ada, nm_ada, nv_ada = _adamw_plain(g_w_ada, w_ada2, m_w_ada[0], v_w_ada[0], "adamw_w_ada", tr=256)
    land = _chip_exchange_wait(sems, c_in, land, d_ada)
    own = lax.dynamic_index_in_dim(c_in, 2 * lax.axis_index("x") + lax.axis_index("y"), 0, keepdims=False)
    big["w_in"] = [t.T[None] for t in _adamw_sharded(land, w_in2.T, m_w_in[0].T, v_w_in[0].T, "adamw_w_in", 336,
                                                     extra=own)]
    ada = [t[None] for t in (g_w_ada, d_ada, nm_ada, nv_ada)]
    lb_w = hgrn_lb_logits.reshape(4, 64)
    d_lb, nm_lb, nv_lb = _adamw_plain(g_lb_mine.reshape(4, 64), lb_w, m_hgrn_lb_logits.reshape(4, 64),
                                      v_hgrn_lb_logits.reshape(4, 64), "adamw_lb")
    lbs = [t.reshape(2, 2, 64) for t in (g_lb_mine, d_lb, nm_lb, nv_lb)]

    names = ['c_ctx', 'w_ada', 'b_ada', 'norm_mix_w', 'norm_ffn_w', 'w_in', 'hgrn_lb_logits', 'hgrn_norm_w',
             'q_norm_w', 'k_norm_w', 'attn_sinks', 'w_branch_hgrn', 'w_branch_attn', 'w_out', 'w_ffn_gate',
             'w_ffn_up', 'w_ffn_down']
    outs = [loss, grad_x[None]]
    for kind in range(4):
        for nm in names:
            if nm == 'w_ada':
                outs.append(ada[kind])
            elif nm == 'hgrn_lb_logits':
                outs.append(lbs[kind])
            elif nm in big:
                outs.append(big[nm][kind])
            else:
                outs.append(rep[kind][nm])
    return tuple(outs)
```

```python
import functools
import math

import numpy as np
import jax
import jax.numpy as jnp
from jax import lax
from jax.experimental import pallas as pl
from jax.experimental.pallas import tpu as pltpu

F32 = jnp.float32
BF16 = jnp.bfloat16

N_DEV = 8
D = 1024
S = 2048
L = 256
T = L + S
TM = 256
N_TILES = T // TM
N_LAT_TILES = S // TM
HG_HEADS = 4
HG_DIM = 128
HGW = 512
CHUNK = 32
N_CHUNKS = T // CHUNK
N_CTX_CHUNKS = L // CHUNK
ATT_HEADS = 8
KV_HEADS = 2
HEAD_DIM = 64
ATW = 512
KVW = 128
BLOCK = 128
N_BLOCKS = S // BLOCK
GRID_W = 64
ROPE_THETA = 10000.0
D_FF = 2816
FF_BLK = D_FF // N_DEV
FF_TILE = 256
N_FF_TILES = D_FF // FF_TILE
N_FF_HALVES = 2
IN_COLS = 5376
IN_BLK = IN_COLS // N_DEV
ADA_BLK = 6 * D // N_DEV
EPS = 1e-6
WA, WB, WC = 2048, 1280, 2048

ADAM_LR = 0.001
ADAM_B1 = 0.9
ADAM_B2 = 0.999
ADAM_EPS = 1e-08
ADAM_WD = 0.01
ADAM_STEP = 10

VMEM_LIMIT = 56 * 1024 * 1024
MESH = pl.DeviceIdType.MESH


def _cp(sem=None, vmem=VMEM_LIMIT):
    return pltpu.CompilerParams(dimension_semantics=sem, vmem_limit_bytes=vmem)


def _full(shape):
    n = len(shape)
    return pl.BlockSpec(shape, lambda *_: (0,) * n)


_VMEM_WHOLE = pl.BlockSpec(memory_space=pltpu.VMEM)
_ANY = pl.BlockSpec(memory_space=pl.ANY)


def _sigmoid(v):
    return 1.0 / (1.0 + jnp.exp(-v))


def _dot(a, b):
    return jnp.dot(a, b, preferred_element_type=F32)


def _dot_nt(a, b):
    return lax.dot_general(a, b, (((1,), (1,)), ((), ())), preferred_element_type=F32)


def _dot_tn(a, b):
    return lax.dot_general(a, b, (((0,), (0,)), ((), ())), preferred_element_type=F32)


def _split3(v):
    hi = v.astype(BF16)
    r = v - hi.astype(F32)
    mid = r.astype(BF16)
    lo = (r - mid.astype(F32)).astype(BF16)
    return hi, mid, lo


def _dot_exact_rhs01(v, m01):
    hi, mid, lo = _split3(v)
    return _dot(hi, m01) + _dot(mid, m01) + _dot(lo, m01)


def _split2(v):
    hi = v.astype(BF16)
    return hi, (v - hi.astype(F32)).astype(BF16)


def _dot_lhs01(m01, v):
    hi, lo = _split2(v)
    return _dot(m01, hi) + _dot(m01, lo)


def _dot_f32(a, b, dot=_dot):
    ah, am, al = _split3(a)
    bh, bm, bl = _split3(b)
    return (dot(ah, bh) + (dot(ah, bm) + dot(am, bh))
            + (dot(am, bm) + dot(ah, bl) + dot(al, bh)))


def _my_pos():
    return lax.axis_index("x"), lax.axis_index("y"), lax.axis_index("c")


class _Comm:
    def __init__(self, operands, out_shapes, sems, phases):
        self.operands, self.out_shapes, self.sems, self.phases = operands, out_shapes, sems, phases


def _gather_comm(blocks):
    n = len(blocks)

    def parts(ins, outs, sems):
        send_sems, recv_sems, local_sems = sems
        x, y, c = _my_pos()
        me, sibling = (x, y, c), (x, y, 1 - c)
        chips = [(1 - x, y), (x, 1 - y), (1 - x, 1 - y)]

        def slot(a, px, py, pc):
            return outs[a].at[4 * px + 2 * py + pc]

        def copy(a, k, block, to, src=None):
            return pltpu.make_async_remote_copy(
                src_ref=slot(a, *block) if src is None else src, dst_ref=slot(a, *block),
                send_sem=send_sems.at[a, k], recv_sem=recv_sems.at[a, k],
                device_id=to, device_id_type=MESH)

        mine = [pltpu.make_async_copy(ins[a], slot(a, *me), local_sems.at[a]) for a in range(n)]
        first = []
        for a in range(n):
            first.append(copy(a, 0, me, sibling, src=ins[a]))
            first += [copy(a, 1 + j, me, (*chip, c), src=ins[a]) for j, chip in enumerate(chips)]
        passed = [copy(a, 4 + j, (*chip, c), sibling) for j, chip in enumerate(chips) for a in range(n)]
        return c, me, sibling, chips, copy, mine, first, passed

    def start(ins, outs, sems):
        _, _, _, _, _, mine, first, _ = parts(ins, outs, sems)
        for cp in mine + first:
            cp.start()

    def forward(ins, outs, sems):
        c, me, _, chips, copy, _, _, passed = parts(ins, outs, sems)
        for j, chip in enumerate(chips):
            for a in range(n):
                copy(a, 1 + j, (*chip, c), me).wait_recv()
                passed[j * n + a].start()

    def finish(ins, outs, sems):
        c, me, sibling, chips, copy, mine, first, passed = parts(ins, outs, sems)
        for a in range(n):
            copy(a, 0, sibling, me).wait_recv()
            for j, chip in enumerate(chips):
                copy(a, 4 + j, (*chip, 1 - c), me).wait_recv()
        for cp in first + passed:
            cp.wait_send()
        for cp in mine:
            cp.wait()

    return _Comm(blocks, [jax.ShapeDtypeStruct((N_DEV,) + b.shape, b.dtype) for b in blocks],
                 [pltpu.SemaphoreType.DMA((n, 7)), pltpu.SemaphoreType.DMA((n, 7)), pltpu.SemaphoreType.DMA((n,))],
                 [start, forward, finish])


def _gather_comm_relayed(blocks):
    n = len(blocks)

    def parts(ins, outs, sems):
        send_sems, recv_sems, local_sems = sems
        x, y, c = _my_pos()
        me, sibling = (x, y, c), (x, y, 1 - c)
        x_nbr, y_nbr, diag = (1 - x, y, c), (x, 1 - y, c), (1 - x, 1 - y, c)

        def slot(a, dev, half=None):
            ref = outs[a].at[4 * dev[0] + 2 * dev[1] + dev[2]]
            if half is None:
                return ref
            rows = blocks[a].shape[0] // 2
            return ref.at[pl.ds(half * rows, rows)]

        def copy(a, k, block, to, half=None, src=None):
            return pltpu.make_async_remote_copy(
                src_ref=slot(a, block, half) if src is None else src, dst_ref=slot(a, block, half),
                send_sem=send_sems.at[a, k], recv_sem=recv_sems.at[a, k],
                device_id=to, device_id_type=MESH)

        mine = [pltpu.make_async_copy(ins[a], slot(a, me), local_sems.at[a]) for a in range(n)]
        return me, sibling, x_nbr, y_nbr, diag, copy, mine

    def start(ins, outs, sems):
        me, sibling, x_nbr, y_nbr, _, copy, mine = parts(ins, outs, sems)
        for cp in mine:
            cp.start()
        for a in range(n):
            for k, to in ((1, x_nbr), (2, y_nbr), (0, sibling)):
                copy(a, k, me, to, src=ins[a]).start()

    def forward(ins, outs, sems):
        me, sibling, x_nbr, y_nbr, _, copy, _ = parts(ins, outs, sems)
        for a in range(n):
            copy(a, 1, x_nbr, me).wait_recv()
            copy(a, 3, x_nbr, y_nbr, half=0).start()
            copy(a, 5, x_nbr, sibling).start()
        for a in range(n):
            copy(a, 2, y_nbr, me).wait_recv()
            copy(a, 4, y_nbr, x_nbr, half=1).start()
            copy(a, 6, y_nbr, sibling).start()

    def finish(ins, outs, sems):
        me, sibling, x_nbr, y_nbr, diag, copy, mine = parts(ins, outs, sems)
        sib = lambda dev: (dev[0], dev[1], sibling[2])
        for a in range(n):
            copy(a, 3, diag, me, half=0).wait_recv()
            copy(a, 4, diag, me, half=1).wait_recv()
            copy(a, 7, diag, sibling).start()
        for a in range(n):
            copy(a, 0, sibling, me).wait_recv()
            for k, dev in ((5, x_nbr), (6, y_nbr), (7, diag)):
                copy(a, k, sib(dev), me).wait_recv()
        for a in range(n):
            for k, block, to, half in ((0, me, sibling, None), (1, me, x_nbr, None), (2, me, y_nbr, None),
                                       (3, x_nbr, y_nbr, 0), (4, y_nbr, x_nbr, 1), (5, x_nbr, sibling, None),
                                       (6, y_nbr, sibling, None), (7, diag, sibling, None)):
                copy(a, k, block, to, half=half, src=ins[a] if block is me else None).wait_send()
        for cp in mine:
            cp.wait()

    return _Comm(blocks, [jax.ShapeDtypeStruct((N_DEV,) + b.shape, b.dtype) for b in blocks],
                 [pltpu.SemaphoreType.DMA((n, 8)), pltpu.SemaphoreType.DMA((n, 8)), pltpu.SemaphoreType.DMA((n,))],
                 [start, forward, finish])


_HBM = pl.BlockSpec(memory_space=pltpu.HBM)
_SEM = pl.BlockSpec(memory_space=pltpu.SEMAPHORE)
_SPLIT_COPY = pltpu.CompilerParams(has_side_effects=pltpu.SideEffectType.DATAFLOW_SIDE_EFFECTING)


def _chip_exchange_copies(src_ref, land_ref, sems):
    x, y, c = _my_pos()
    q_me = 2 * x + y
    pairs = []
    for j, (px, py) in enumerate([(1 - x, y), (x, 1 - y), (1 - x, 1 - y)]):
        q = 2 * px + py
        send = pltpu.make_async_remote_copy(
            src_ref=src_ref.at[q], dst_ref=land_ref.at[q_me], send_sem=sems[j], recv_sem=sems[3 + j],
            device_id=(px, py, c), device_id_type=MESH)
        recv = pltpu.make_async_remote_copy(
            src_ref=src_ref.at[q], dst_ref=land_ref.at[q], send_sem=sems[j], recv_sem=sems[3 + j],
            device_id=(x, y, c), device_id_type=MESH)
        pairs.append((send, recv))
    return pairs


def _chip_exchange_start(src, land):
    def body(src_ref, land_ref, *outs):
        sems, token = outs[:6], outs[8]
        for send, _ in _chip_exchange_copies(src_ref, land_ref, sems):
            send.start()
        token[...] = jnp.zeros_like(token)

    res = pl.pallas_call(
        body, name="scatter_in_start",
        out_shape=(pltpu.SemaphoreType.DMA(()),) * 6 + (
            pltpu.HBM(src.shape, src.dtype), pltpu.HBM(land.shape, land.dtype),
            jax.ShapeDtypeStruct((8, 128), F32)),
        in_specs=(_HBM, _HBM), out_specs=(_SEM,) * 6 + (_HBM, _HBM, pl.BlockSpec(memory_space=pltpu.VMEM)),
        input_output_aliases={0: 6, 1: 7}, compiler_params=_SPLIT_COPY,
    )(pltpu.with_memory_space_constraint(src, pltpu.HBM), pltpu.with_memory_space_constraint(land, pltpu.HBM))
    return res[:6], res[6], res[7], res[8]


def _chip_exchange_wait(sems, src_thru, land_thru, after):
    def body(src_ref, land_ref, *rest):
        for send, recv in _chip_exchange_copies(src_ref, land_ref, rest[:6]):
            send.wait_send()
            recv.wait_recv()

    return pl.pallas_call(
        body, name="scatter_in_wait",
        out_shape=(pltpu.HBM(src_thru.shape, src_thru.dtype), pltpu.HBM(land_thru.shape, land_thru.dtype)),
        in_specs=(_HBM, _HBM) + (_SEM,) * 6 + (_ANY,), out_specs=(_HBM, _HBM),
        input_output_aliases={0: 0, 1: 1}, compiler_params=_SPLIT_COPY,
    )(src_thru, land_thru, *sems, after)[1]


def _run_comm(comm, name, in_vmem=False, after=()):
    n_in, n_out, n_after = len(comm.operands), len(comm.out_shapes), len(after)

    def body(*refs):
        ins, refs = refs[:n_in], refs[n_in + n_after:]
        outs, sems = refs[:n_out], refs[n_out:]
        for phase in comm.phases:
            phase(ins, outs, sems)

    spec = _VMEM_WHOLE if in_vmem else _ANY
    return pl.pallas_call(
        body, name=name, out_shape=comm.out_shapes, in_specs=[spec] * n_in + [_ANY] * n_after,
        out_specs=[spec] * n_out, scratch_shapes=comm.sems,
    )(*comm.operands, *after)


def _carrier_call(body, comm, schedule, *, name, grid, in_specs, out_specs, out_shape, scratch_shapes, operands):
    n_in, n_out, n_scr = len(in_specs), len(out_specs), len(scratch_shapes)
    c_in, c_out = len(comm.operands), len(comm.out_shapes)

    def full_body(*refs):
        ins, refs = refs[:n_in], refs[n_in:]
        cins, refs = refs[:c_in], refs[c_in:]
        outs, refs = refs[:n_out], refs[n_out:]
        couts, refs = refs[:c_out], refs[c_out:]
        scr, csems = refs[:n_scr], refs[n_scr:]
        step = pl.program_id(0)

        def run(before):
            for (at, when_before), phase in zip(schedule, comm.phases):
                if when_before == before:
                    pl.when(step == at)(functools.partial(phase, cins, couts, csems))

        run(True)
        body(*ins, *outs, *scr)
        run(False)

    res = pl.pallas_call(
        full_body, name=name, grid=grid,
        in_specs=list(in_specs) + [_ANY] * c_in, out_specs=list(out_specs) + [_ANY] * c_out,
        out_shape=list(out_shape) + list(comm.out_shapes),
        scratch_shapes=list(scratch_shapes) + list(comm.sems),
        compiler_params=_cp(("arbitrary",)),
    )(*operands, *comm.operands)
    return res[:n_out], res[n_out:]


def _pcall(body, carried, *, name, grid, in_specs, out_specs, out_shape, scratch_shapes, operands):
    if carried is None:
        res = pl.pallas_call(body, name=name, grid=grid, in_specs=in_specs, out_specs=out_specs,
                             out_shape=out_shape, scratch_shapes=scratch_shapes,
                             compiler_params=_cp(("arbitrary",)))(*operands)
        return res, ()
    return _carrier_call(body, carried[0], carried[1], name=name, grid=grid, in_specs=in_specs,
                         out_specs=out_specs, out_shape=out_shape, scratch_shapes=scratch_shapes,
                         operands=operands)


def _all_gather(blocks, name, in_vmem, after=()):
    return _run_comm(_gather_comm(blocks), name, in_vmem, after)


N_CHIPS = 4


def _sibling_comm(contribs):
    n = len(contribs)

    def copies(ins, outs, sems):
        send_sems, recv_sems = sems
        x, y, c = _my_pos()
        return [pltpu.make_async_remote_copy(
            src_ref=ins[a].at[pl.ds(0, N_CHIPS), 1 - c], dst_ref=outs[a],
            send_sem=send_sems.at[a], recv_sem=recv_sems.at[a],
            device_id=(x, y, 1 - c), device_id_type=MESH) for a in range(n)]

    def start(ins, outs, sems):
        for cp in copies(ins, outs, sems):
            cp.start()

    def finish(ins, outs, sems):
        cps = copies(ins, outs, sems)
        for cp in cps:
            cp.wait_recv()
        for cp in cps:
            cp.wait_send()

    return _Comm(contribs, [jax.ShapeDtypeStruct((N_CHIPS,) + b.shape[2:], b.dtype) for b in contribs],
                 [pltpu.SemaphoreType.DMA((n,)), pltpu.SemaphoreType.DMA((n,))], [start, finish])


def _pair_sum(mine, theirs, name):
    _, _, rows, cols = mine.shape
    core = lax.axis_index("c").astype(jnp.int32).reshape(1)

    def body(c_ref, m_ref, t_ref, o_ref):
        o_ref[...] = (m_ref[...].astype(F32) + t_ref[...].astype(F32)).astype(BF16)

    return pl.pallas_call(
        body, name=name,
        grid_spec=pltpu.PrefetchScalarGridSpec(
            num_scalar_prefetch=1, grid=(N_CHIPS,),
            in_specs=[pl.BlockSpec((None, None, rows, cols), lambda q, c: (q, c[0], 0, 0)),
                      pl.BlockSpec((None, rows, cols), lambda q, c: (q, 0, 0))],
            out_specs=pl.BlockSpec((None, rows, cols), lambda q, c: (q, 0, 0))),
        out_shape=jax.ShapeDtypeStruct((N_CHIPS, rows, cols), BF16),
        compiler_params=_cp(("parallel",)),
    )(core, mine, theirs)


def _chip_comm(sums):
    n = len(sums)

    def parts(ins, outs, sems):
        send_sems, recv_sems, local_sems = sems
        x, y, c = _my_pos()
        q_me = 2 * x + y
        chips = [(1 - x, y), (x, 1 - y), (1 - x, 1 - y)]
        mine = [pltpu.make_async_copy(ins[a].at[q_me], outs[a].at[q_me], local_sems.at[a]) for a in range(n)]
        sends, recvs = [], []
        for j, (px, py) in enumerate(chips):
            for a in range(n):
                q = 2 * px + py
                sends.append(pltpu.make_async_remote_copy(
                    src_ref=ins[a].at[q], dst_ref=outs[a].at[q_me],
                    send_sem=send_sems.at[a, j], recv_sem=recv_sems.at[a, j],
                    device_id=(px, py, c), device_id_type=MESH))
                recvs.append(pltpu.make_async_remote_copy(
                    src_ref=ins[a].at[q], dst_ref=outs[a].at[q],
                    send_sem=send_sems.at[a, j], recv_sem=recv_sems.at[a, j],
                    device_id=(x, y, c), device_id_type=MESH))
        return mine, sends, recvs

    def start(ins, outs, sems):
        mine, sends, _ = parts(ins, outs, sems)
        for cp in mine + sends:
            cp.start()

    def finish(ins, outs, sems):
        mine, sends, recvs = parts(ins, outs, sems)
        for cp in recvs:
            cp.wait_recv()
        for cp in sends:
            cp.wait_send()
        for cp in mine:
            cp.wait()

    return _Comm(sums, [jax.ShapeDtypeStruct(b.shape, b.dtype) for b in sums],
                 [pltpu.SemaphoreType.DMA((n, 3)), pltpu.SemaphoreType.DMA((n, 3)), pltpu.SemaphoreType.DMA((n,))],
                 [start, finish])


def _mm_nt(a, bt, *, tm, tn, out_dtype, name, row_off=0, rows=None):
    rows = a.shape[0] if rows is None else rows
    n, k = bt.shape

    def body(a_ref, b_ref, o_ref):
        o_ref[...] = _dot_nt(a_ref[...], b_ref[...]).astype(out_dtype)

    return pl.pallas_call(
        body, name=name, grid=(rows // tm, n // tn),
        in_specs=[pl.BlockSpec((tm, k), lambda i, j: (i + row_off, 0)),
                  pl.BlockSpec((tn, k), lambda i, j: (j, 0))],
        out_specs=pl.BlockSpec((tm, tn), lambda i, j: (i, j)),
        out_shape=jax.ShapeDtypeStruct((rows, n), out_dtype),
        compiler_params=_cp(("parallel", "parallel")),
    )(a, bt)


def _mm_tn(a, b, *, tk, nk, tm, tn, out_dtype, name, a_off=0, b_off=0):
    m, n = a.shape[1], b.shape[1]

    def body(a_ref, b_ref, o_ref, acc):
        kk = pl.program_id(2)

        @pl.when(kk == 0)
        def _():
            acc[...] = jnp.zeros_like(acc)

        acc[...] += _dot_tn(a_ref[...], b_ref[...])

        @pl.when(kk == nk - 1)
        def _():
            o_ref[...] = acc[...].astype(out_dtype)

    return pl.pallas_call(
        body, name=name, grid=(m // tm, n // tn, nk),
        in_specs=[pl.BlockSpec((tk, tm), lambda i, j, kk: (kk + a_off, i)),
                  pl.BlockSpec((tk, tn), lambda i, j, kk: (kk + b_off, j))],
        out_specs=pl.BlockSpec((tm, tn), lambda i, j, kk: (i, j)),
        out_shape=jax.ShapeDtypeStruct((m, n), out_dtype),
        scratch_shapes=[pltpu.VMEM((tm, tn), F32)],
        compiler_params=_cp(("parallel", "parallel", "arbitrary")),
    )(a, b)


def _mm_tn_blocked(a, b, name, steps, carried=None):
    w = a.shape[1] // steps
    n = b.shape[1]

    def body(a_ref, b_ref, o_ref):
        o_ref[...] = _dot_tn(a_ref[...], b_ref[...]).astype(BF16)

    (out,), extra = _pcall(
        body, carried, name=name, grid=(steps,),
        in_specs=[pl.BlockSpec((S, w), lambda j: (0, j)), _full((S, n))],
        out_specs=[pl.BlockSpec((w, n), lambda j: (j, 0))],
        out_shape=[jax.ShapeDtypeStruct((a.shape[1], n), BF16)],
        scratch_shapes=[], operands=[a, b])
    return out, extra


def _prologue(cond, c_ctx, w_ada, b_cols, w_in_t, x, ctx, nw):
    rows_shape = jax.ShapeDtypeStruct((16, ADA_BLK), F32)
    big, g_cond, g_mod = _gather_comm_relayed([w_in_t]), _gather_comm([cond]), _gather_comm([rows_shape])

    def body(cond_ref, cctx_ref, wada_ref, b_ref, nw_ref, win_ref, x_ref, ctx_ref,
             g0_ref, cc_ref, mod_ref, gin_ref, h_ref, hl_ref, rows_ref, g1_ref, x_s, ctx_s, h_s, io_sems, *sems):
        s_big, s_cond, s_mod = sems[0:3], sems[3:6], sems[6:9]
        load_x = pltpu.make_async_copy(x_ref, x_s, io_sems.at[0])
        load_ctx = pltpu.make_async_copy(ctx_ref, ctx_s, io_sems.at[1])
        load_x.start()
        load_ctx.start()
        for phase in g_cond.phases:
            phase([cond_ref], [g0_ref], s_cond)
        big.phases[0]([win_ref], [gin_ref], s_big)
        cc_ref[...] = jnp.zeros_like(cc_ref)
        for j in range(N_DEV):
            cc_ref[j:j + 1, :] = g0_ref[j, 0:1, :]
        cc_ref[N_DEV:N_DEV + 1, :] = cctx_ref[...]
        cv = cc_ref[...]
        rows_ref[...] = _dot_f32(cv * _sigmoid(cv), wada_ref[...]) + b_ref[...]
        for phase in g_mod.phases:
            phase([rows_ref], [g1_ref], s_mod)
        big.phases[1]([win_ref], [gin_ref], s_big)
        x_pos, y_pos, c_pos = _my_pos()
        me = 4 * x_pos + 2 * y_pos + c_pos
        mine = jnp.concatenate([g1_ref[j, pl.ds(me, 1), :] for j in range(N_DEV)], axis=1)
        shared = jnp.concatenate([g1_ref[j, N_DEV:N_DEV + 1, :] for j in range(N_DEV)], axis=1)
        for k in range(6):
            mod_ref[k:k + 1, :] = mine[:, k * D:(k + 1) * D]
        mod_ref[6:7, :] = shared[:, 0:D]
        mod_ref[7:8, :] = shared[:, D:2 * D]
        load_ctx.wait()
        load_x.wait()
        h_s[pl.ds(0, L), :] = _rms_mod(ctx_s[...], nw_ref[...], mod_ref[6:7, :], mod_ref[7:8, :]).astype(BF16)

        def norm_tile(i, carry):
            r0 = pl.multiple_of(i * TM, TM)
            h_s[pl.ds(L + r0, TM), :] = _rms_mod(
                x_s[pl.ds(r0, TM), :], nw_ref[...], mod_ref[0:1, :], mod_ref[1:2, :]).astype(BF16)
            return carry

        lax.fori_loop(0, N_LAT_TILES, norm_tile, 0)
        stores = [pltpu.make_async_copy(h_s, h_ref, io_sems.at[2]),
                  pltpu.make_async_copy(h_s.at[pl.ds(L, S)], hl_ref, io_sems.at[3])]
        for cp in stores:
            cp.start()
        big.phases[2]([win_ref], [gin_ref], s_big)
        for cp in stores:
            cp.wait()

    return pl.pallas_call(
        body, name="prologue",
        in_specs=[_VMEM_WHOLE] * 5 + [_ANY] * 3, out_specs=[_VMEM_WHOLE] * 3 + [_ANY] * 3,
        out_shape=[g_cond.out_shapes[0], jax.ShapeDtypeStruct((16, D), F32), jax.ShapeDtypeStruct((8, D), F32),
                   big.out_shapes[0], jax.ShapeDtypeStruct((T, D), BF16), jax.ShapeDtypeStruct((S, D), BF16)],
        scratch_shapes=[pltpu.VMEM((16, ADA_BLK), F32), pltpu.VMEM((N_DEV, 16, ADA_BLK), F32),
                        pltpu.VMEM((S, D), F32), pltpu.VMEM((L, D), F32), pltpu.VMEM((T, D), BF16),
                        pltpu.SemaphoreType.DMA((4,))] + big.sems + g_cond.sems + g_mod.sems,
        compiler_params=_cp(),
    )(cond, c_ctx, w_ada, b_cols, nw, w_in_t, x, ctx)


def _ada_grads(cc, dm_cols, w_ada):
    def body(c_ref, dm_ref, w_ref, gw_ref, dsc_ref):
        cv = c_ref[...]
        sc = cv * _sigmoid(cv)
        dm = dm_ref[...]
        gw_ref[...] = _dot_f32(sc, dm, dot=_dot_tn)
        dsc_ref[...] = _dot_f32(dm, w_ref[...], dot=_dot_nt)

    return pl.pallas_call(
        body, name="ada_grads",
        in_specs=[_VMEM_WHOLE] * 3, out_specs=[_VMEM_WHOLE] * 2,
        out_shape=[jax.ShapeDtypeStruct((D, ADA_BLK), F32), jax.ShapeDtypeStruct((16, D), F32)],
        compiler_params=_cp(),
    )(cc, dm_cols, w_ada)


def _lat(i):
    return jnp.maximum(i - 1, 0)


def _rms_mod(xv, nw, sh, sc):
    rstd = lax.rsqrt(jnp.mean(xv * xv, axis=-1, keepdims=True) + EPS)
    return (xv * rstd * nw) * (1.0 + sc) + sh


def _rms_mod_bwd(xv, nw, sc, dh):
    rstd = lax.rsqrt(jnp.mean(xv * xv, axis=-1, keepdims=True) + EPS)
    xhat = xv * rstd
    dn = dh * (1.0 + sc)
    dxhat = dn * nw
    dx = rstd * (dxhat - xhat * jnp.mean(dxhat * xhat, axis=-1, keepdims=True))
    return (dx, jnp.sum(dh, axis=0, keepdims=True), jnp.sum(dh * (xhat * nw), axis=0, keepdims=True),
            jnp.sum(dn * xhat, axis=0, keepdims=True))


def _chunk_masks(reverse):
    row = lax.broadcasted_iota(jnp.int32, (TM, TM), 0)
    col = lax.broadcasted_iota(jnp.int32, (TM, TM), 1)
    same = (row // CHUNK) == (col // CHUNK)
    tri = same & ((col >= row) if reverse else (col <= row))
    return same, tri


def _chunk_order(i, reverse):
    if not reverse:
        return i
    return jnp.where(i < N_CTX_CHUNKS, N_CTX_CHUNKS - 1 - i, N_CHUNKS + N_CTX_CHUNKS - 1 - i)


def _decay_terms(z, lb, same01, tri01):
    f = lb + (1.0 - lb) * _sigmoid(z)
    g = jnp.log(f)
    g2 = jnp.concatenate(_split2(g), axis=1)
    b2 = _dot(tri01, g2)
    t2 = _dot(same01, g2)
    return f, 1.0 - f, b2[:, :HG_DIM] + b2[:, HG_DIM:], t2[:, :HG_DIM] + t2[:, HG_DIM:]


def _chunk_outer(a, b):
    n = TM // CHUNK
    return jnp.einsum('ncv,nck->nvk', a.reshape(n, CHUNK, HG_DIM), b.reshape(n, CHUNK, HG_DIM),
                      preferred_element_type=F32)


def _hgrn_fwd(p_a, lbl, carried=None):
    cpt = TM // CHUNK

    def body(p_ref, lbl_ref, o_ref, st_ref, qd_s, kd_s, u_s, v_s, ebt_s):
        masks = [_chunk_masks(d == 1) for d in (0, 1)]
        same01 = jnp.where(masks[0][0], 1.0, 0.0).astype(BF16)
        tri = [m[1] for m in masks]
        tri01 = [jnp.where(t, 1.0, 0.0).astype(BF16) for t in tri]
        lb = [_sigmoid(lbl_ref[d][0:1, :] - lbl_ref[d][1:2, :]) for d in (0, 1)]

        def prep(r, carry):
            r0 = pl.multiple_of(r * TM, TM)
            vb = p_ref[pl.ds(r0, TM), 2 * HG_DIM:3 * HG_DIM].astype(BF16)
            v_s[pl.ds(r0, TM), :] = vb
            for d in (0, 1):
                z = p_ref[pl.ds(r0, TM), d * HG_DIM:(d + 1) * HG_DIM]
                _, k, b, bt = _decay_terms(z, lb[d], same01, tri01[d])
                u_s[d, pl.ds(r * cpt, cpt)] = _chunk_outer(vb, (k * jnp.exp(bt - b)).astype(BF16))
                ebt_s[d, pl.ds(r0, TM), :] = jnp.exp(bt)

                @pl.when(r >= 1)
                def _():
                    rl = pl.multiple_of(r0 - L, TM)
                    qr = p_ref[pl.ds(r0, TM), 3 * HG_DIM:4 * HG_DIM]
                    q = qr * _sigmoid(qr) * HG_DIM ** -0.5
                    qd_s[d, pl.ds(rl, TM), :] = (q * jnp.exp(b)).astype(BF16)
                    kd_s[d, pl.ds(rl, TM), :] = (k * jnp.exp(-b)).astype(BF16)

            return carry

        lax.fori_loop(0, N_TILES, prep, 0)

        def scan(i, sts):
            new = []
            for d in (0, 1):
                nn = _chunk_order(i, d == 1)
                c0 = pl.multiple_of(nn * CHUNK, CHUNK)
                st_ref[d, nn] = sts[d].astype(BF16)
                new.append(sts[d] * ebt_s[d, pl.ds(c0, 1), :] + u_s[d, nn])
            return tuple(new)

        zero = jnp.zeros((HG_DIM, HG_DIM), F32)
        lax.fori_loop(0, N_CHUNKS, scan, (zero, zero))

        def outp(r, carry):
            r0 = pl.multiple_of(r * TM, TM)
            vb = v_s[pl.ds(r0 + L, TM), :]
            o = jnp.zeros((TM, HG_DIM), F32)
            for d in (0, 1):
                qd = qd_s[d, pl.ds(r0, TM), :]
                a = jnp.where(tri[d], _dot_nt(qd, kd_s[d, pl.ds(r0, TM), :]), 0.0)
                stb = st_ref[d, pl.ds(N_CTX_CHUNKS + r * cpt, cpt)]
                inter = jnp.einsum('nck,nvk->ncv', qd.reshape(cpt, CHUNK, HG_DIM), stb,
                                   preferred_element_type=F32)
                o = o + _dot(a.astype(BF16), vb) + inter.reshape(TM, HG_DIM)
            o_ref[pl.ds(r0, TM), :] = o
            return carry

        lax.fori_loop(0, N_LAT_TILES, outp, 0, unroll=2)

    return _pcall(
        body, carried, name="hgrn_fwd", grid=(HG_HEADS,),
        in_specs=[pl.BlockSpec((T, 4 * HG_DIM), lambda h: (0, h)),
                  pl.BlockSpec((2, 2, HG_DIM), lambda h: (0, 0, h))],
        out_specs=[pl.BlockSpec((S, HG_DIM), lambda h: (0, h)),
                   pl.BlockSpec((2, None, N_CHUNKS, HG_DIM, HG_DIM), lambda h: (0, h, 0, 0, 0))],
        out_shape=[jax.ShapeDtypeStruct((S, HGW), F32),
                   jax.ShapeDtypeStruct((2, HG_HEADS, N_CHUNKS, HG_DIM, HG_DIM), BF16)],
        scratch_shapes=[pltpu.VMEM((2, S, HG_DIM), BF16), pltpu.VMEM((2, S, HG_DIM), BF16),
                        pltpu.VMEM((2, N_CHUNKS, HG_DIM, HG_DIM), F32), pltpu.VMEM((T, HG_DIM), BF16),
                        pltpu.VMEM((2, T, HG_DIM), F32)],
        operands=[p_a, lbl])


def _hgrn_bwd(p_a, lbl, d_o, st, carried=None):
    cpt = TM // CHUNK

    def rows(r):
        return r * TM if isinstance(r, int) else pl.multiple_of(r * TM, TM)

    def body(p_ref, lbl_ref, do_ref, st_ref, dp_ref, dlb_ref, b_s, bt_s, dbt_s, qd_s, dst_s, w_s):
        masks = [_chunk_masks(d == 1) for d in (0, 1)]
        same01 = jnp.where(masks[0][0], 1.0, 0.0).astype(BF16)
        tri = [m[1] for m in masks]
        tri01 = [jnp.where(t, 1.0, 0.0).astype(BF16) for t in tri]
        later01 = [tri01[1], tri01[0]]
        lb = [_sigmoid(lbl_ref[d][0:1, :] - lbl_ref[d][1:2, :]) for d in (0, 1)]

        def prep_tile(r, latent):
            r0 = rows(r)
            for d in (0, 1):
                z = p_ref[pl.ds(r0, TM), d * HG_DIM:(d + 1) * HG_DIM]
                _, _, b, bt = _decay_terms(z, lb[d], same01, tri01[d])
                b_s[d, pl.ds(r0, TM), :] = b
                bt_s[d, pl.ds(r0, TM), :] = bt
                if latent:
                    rl = pl.multiple_of(r0 - L, TM)
                    qr = p_ref[pl.ds(r0, TM), 3 * HG_DIM:4 * HG_DIM]
                    qd = (qr * _sigmoid(qr) * HG_DIM ** -0.5 * jnp.exp(b)).astype(BF16)
                    qd_s[d, pl.ds(rl, TM), :] = qd
                    w_s[d, pl.ds(r * cpt, cpt)] = _chunk_outer(
                        do_ref[pl.ds(rl, TM), :].astype(BF16), qd).astype(BF16)

        prep_tile(0, False)
        w_s[:, pl.ds(0, N_CTX_CHUNKS)] = jnp.zeros((2, N_CTX_CHUNKS, HG_DIM, HG_DIM), BF16)

        def prep(r, carry):
            prep_tile(r, True)
            return carry

        lax.fori_loop(1, N_TILES, prep, 0, unroll=2)

        def rscan(j, dsts):
            i = N_CHUNKS - 1 - j
            new = []
            for d in (0, 1):
                nn = _chunk_order(i, d == 1)
                c0 = pl.multiple_of(nn * CHUNK, CHUNK)
                dst_s[d, nn] = dsts[d].astype(BF16)
                after = st_ref[d, _chunk_order(jnp.minimum(i + 1, N_CHUNKS - 1), d == 1)].astype(F32)
                dbt_s[d, pl.ds(c0, CHUNK), :] = jnp.broadcast_to(
                    jnp.sum(after * dsts[d], axis=0, keepdims=True), (CHUNK, HG_DIM))
                new.append(dsts[d] * jnp.exp(bt_s[d, pl.ds(c0, 1), :]) + w_s[d, nn].astype(F32))
            return tuple(new)

        zero = jnp.zeros((HG_DIM, HG_DIM), F32)
        lax.fori_loop(0, N_CHUNKS, rscan, (zero, zero))

        def grad_tile(r, latent):
            r0 = rows(r)
            vb = p_ref[pl.ds(r0, TM), 2 * HG_DIM:3 * HG_DIM].astype(BF16)
            dv = jnp.zeros((TM, HG_DIM), F32)
            dq = jnp.zeros((TM, HG_DIM), F32)
            dlbs = []
            if latent:
                rl = pl.multiple_of(r0 - L, TM)
                qr = p_ref[pl.ds(r0, TM), 3 * HG_DIM:4 * HG_DIM]
                sq = _sigmoid(qr)
                do = do_ref[pl.ds(rl, TM), :].astype(BF16)
                da_full = _dot_nt(do, vb)
            for d in (0, 1):
                z = p_ref[pl.ds(r0, TM), d * HG_DIM:(d + 1) * HG_DIM]
                sz = _sigmoid(z)
                f = lb[d] + (1.0 - lb[d]) * sz
                k = 1.0 - f
                b = b_s[d, pl.ds(r0, TM), :]
                e2 = jnp.exp(bt_s[d, pl.ds(r0, TM), :] - b)
                dstb = dst_s[d, pl.ds(r * cpt, cpt)]
                kd2 = k * e2
                dkd2 = jnp.einsum('ncv,nvk->nck', vb.reshape(cpt, CHUNK, HG_DIM), dstb,
                                  preferred_element_type=F32).reshape(TM, HG_DIM)
                dv = dv + jnp.einsum('nck,nvk->ncv', kd2.astype(BF16).reshape(cpt, CHUNK, HG_DIM), dstb,
                                     preferred_element_type=F32).reshape(TM, HG_DIM)
                dk = dkd2 * e2
                db = -(kd2 * dkd2)
                if latent:
                    eb = jnp.exp(b)
                    enb = jnp.exp(-b)
                    qdf = qr * sq * HG_DIM ** -0.5 * eb
                    kdf = k * enb
                    qd = qd_s[d, pl.ds(rl, TM), :]
                    kd = kdf.astype(BF16)
                    a = jnp.where(tri[d], _dot_nt(qd, kd), 0.0).astype(BF16)
                    da = jnp.where(tri[d], da_full, 0.0).astype(BF16)
                    stb = st_ref[d, pl.ds(r * cpt, cpt)]
                    dqd = _dot(da, kd) + jnp.einsum(
                        'ncv,nvk->nck', do.reshape(cpt, CHUNK, HG_DIM), stb,
                        preferred_element_type=F32).reshape(TM, HG_DIM)
                    dkd = _dot_tn(da, qd)
                    dv = dv + _dot_tn(a, do)
                    dk = dk + dkd * enb
                    db = db + qdf * dqd - kdf * dkd
                    dq = dq + dqd * eb
                dg = _dot_lhs01(later01[d], db) + dbt_s[d, pl.ds(r0, TM), :]
                df = dg / f - dk
                dp_ref[pl.ds(r0, TM), d * HG_DIM:(d + 1) * HG_DIM] = (
                    df * (1.0 - lb[d]) * sz * (1.0 - sz)).astype(BF16)
                dlbs.append(jnp.sum(df * (1.0 - sz), axis=0, keepdims=True))
            dp_ref[pl.ds(r0, TM), 2 * HG_DIM:3 * HG_DIM] = dv.astype(BF16)
            if latent:
                dq = dq * (HG_DIM ** -0.5) * (sq * (1.0 + qr * (1.0 - sq)))
            dp_ref[pl.ds(r0, TM), 3 * HG_DIM:4 * HG_DIM] = dq.astype(BF16)
            return dlbs

        dlb_ctx = grad_tile(0, False)

        def grads(r, acc):
            t = grad_tile(r, True)
            return (acc[0] + t[0], acc[1] + t[1])

        dlb = lax.fori_loop(1, N_TILES, grads, (dlb_ctx[0], dlb_ctx[1]))
        dlb_ref[0:1, :] = dlb[0]
        dlb_ref[1:2, :] = dlb[1]

    return _pcall(
        body, carried, name="hgrn_bwd", grid=(HG_HEADS,),
        in_specs=[pl.BlockSpec((T, 4 * HG_DIM), lambda h: (0, h)),
                  pl.BlockSpec((2, 2, HG_DIM), lambda h: (0, 0, h)),
                  pl.BlockSpec((S, HG_DIM), lambda h: (0, h)),
                  pl.BlockSpec((2, None, N_CHUNKS, HG_DIM, HG_DIM), lambda h: (0, h, 0, 0, 0))],
        out_specs=[pl.BlockSpec((T, 4 * HG_DIM), lambda h: (0, h)),
                   pl.BlockSpec((2, HG_DIM), lambda h: (0, h))],
        out_shape=[jax.ShapeDtypeStruct((T, WA), BF16), jax.ShapeDtypeStruct((2, HGW), F32)],
        scratch_shapes=[pltpu.VMEM((2, T, HG_DIM), F32), pltpu.VMEM((2, T, HG_DIM), F32),
                        pltpu.VMEM((2, T, HG_DIM), F32), pltpu.VMEM((2, S, HG_DIM), BF16),
                        pltpu.VMEM((2, N_CHUNKS, HG_DIM, HG_DIM), BF16),
                        pltpu.VMEM((2, N_CHUNKS, HG_DIM, HG_DIM), BF16)],
        operands=[p_a, lbl, d_o, st])


def _rope_tables():
    t = np.arange(S)
    inv = ROPE_THETA ** (-np.arange(0, 32, 2, dtype=np.float64) / 32)
    lane = np.arange(64)
    pos = np.where(lane[None, :] < 32, (t // GRID_W)[:, None], (t % GRID_W)[:, None]).astype(np.float64)
    ang = pos * inv[(lane % 32) % 16][None, :]
    sign = np.where((lane % 32) < 16, -1.0, 1.0)[None, :]
    cos = np.tile(np.cos(ang), (1, 2)).astype(np.float32)
    sin = np.tile(np.sin(ang) * sign, (1, 2)).astype(np.float32)
    return jnp.asarray(cos), jnp.asarray(sin)


def _rope_partner(v):
    lane = lax.broadcasted_iota(jnp.int32, (1, 128), 1)
    first = (lane % 32) < 16
    slabs = []
    for j in range(v.shape[1] // 128):
        s = v[:, 128 * j:128 * (j + 1)]
        slabs.append(jnp.where(first, pltpu.roll(s, 112, 1), pltpu.roll(s, 16, 1)))
    return slabs[0] if len(slabs) == 1 else jnp.concatenate(slabs, axis=1)


def _group_ones(width, group):
    r = lax.broadcasted_iota(jnp.int32, (width, width), 0)
    c = lax.broadcasted_iota(jnp.int32, (width, width), 1)
    return jnp.where((r // group) == (c // group), 1.0, 0.0).astype(BF16)


def _group_mean(v, ones01, group):
    hi = v.astype(BF16)
    lo = (v - hi.astype(F32)).astype(BF16)
    return (_dot(hi, ones01) + _dot(lo, ones01)) * (1.0 / group)


def _rep_matrix():
    r = lax.broadcasted_iota(jnp.int32, (KVW, ATW), 0)
    c = lax.broadcasted_iota(jnp.int32, (KVW, ATW), 1)
    return jnp.where(r == HEAD_DIM * (c // 256) + c % HEAD_DIM, 1.0, 0.0).astype(BF16)


def _tile_lanes(v, reps):
    return jnp.concatenate([v] * reps, axis=1)


def _prep_fwd(p_b, o, cos, sin, hnw, qnw, knw):
    def body(p_ref, o_ref, cos_ref, sin_ref, hnw_ref, qnw_ref, knw_ref, y_ref, q_ref, k_ref, v_ref):
        i = pl.program_id(0)
        rep = _rep_matrix()
        ones_k = _group_ones(KVW, HEAD_DIM)
        kr = p_ref[:, 1024:1152]
        krstd = lax.rsqrt(_group_mean(kr * kr, ones_k, HEAD_DIM) + EPS)
        kn = kr * krstd * knw_ref[...]
        v_ref[...] = _dot(p_ref[:, 1152:1280].astype(BF16), rep).astype(BF16)

        @pl.when(i == 0)
        def _():
            k_ref[...] = _dot(kn.astype(BF16), rep).astype(BF16)

        @pl.when(i > 0)
        def _():
            cs, sn = cos_ref[...], sin_ref[...]
            kro = kn * cs + _rope_partner(kn) * sn
            k_ref[...] = _dot(kro.astype(BF16), rep).astype(BF16)
            qr = p_ref[:, 512:1024]
            qrstd = lax.rsqrt(_group_mean(qr * qr, _group_ones(ATW, HEAD_DIM), HEAD_DIM) + EPS)
            qn = qr * qrstd * qnw_ref[...]
            qro = qn * _tile_lanes(cs, 4) + _rope_partner(qn) * _tile_lanes(sn, 4)
            q_ref[...] = (qro * HEAD_DIM ** -0.5).astype(BF16)
            ys = []
            for h in range(HG_HEADS):
                oh = o_ref[:, HG_DIM * h:HG_DIM * (h + 1)]
                gh = p_ref[:, HG_DIM * h:HG_DIM * (h + 1)]
                rstd = lax.rsqrt(jnp.mean(oh * oh, axis=-1, keepdims=True) + EPS)
                ys.append(oh * rstd * hnw_ref[...] * (gh * _sigmoid(gh)))
            y_ref[...] = jnp.concatenate(ys, axis=1).astype(BF16)

    return pl.pallas_call(
        body, name="prep_fwd", grid=(N_TILES,),
        in_specs=[pl.BlockSpec((TM, WB), lambda i: (i, 0)),
                  pl.BlockSpec((TM, HGW), lambda i: (_lat(i), 0)),
                  pl.BlockSpec((TM, 128), lambda i: (_lat(i), 0)),
                  pl.BlockSpec((TM, 128), lambda i: (_lat(i), 0)),
                  _full((1, HG_DIM)), _full((1, ATW)), _full((1, KVW))],
        out_specs=[pl.BlockSpec((TM, HGW), lambda i: (_lat(i), 0)),
                   pl.BlockSpec((TM, ATW), lambda i: (_lat(i), 0)),
                   pl.BlockSpec((TM, ATW), lambda i: (i, 0)),
                   pl.BlockSpec((TM, ATW), lambda i: (i, 0))],
        out_shape=[jax.ShapeDtypeStruct((S, HGW), BF16), jax.ShapeDtypeStruct((S, ATW), BF16),
                   jax.ShapeDtypeStruct((T, ATW), BF16), jax.ShapeDtypeStruct((T, ATW), BF16)],
        compiler_params=_cp(("arbitrary",)),
    )(p_b, o, cos, sin, hnw, qnw, knw)


def _prep_bwd(p_b, o, cos, sin, hnw, qnw, knw, dy_hg, dq, dk_rep, dv_rep, carried=None):
    def body(p_ref, o_ref, cos_ref, sin_ref, hnw_ref, qnw_ref, knw_ref, dy_ref, dq_ref, dk_ref, dv_ref,
             dp_ref, do_ref, acc_ref):
        i = pl.program_id(0)

        @pl.when(i == 0)
        def _():
            acc_ref[...] = jnp.zeros_like(acc_ref)

        rep = _rep_matrix()
        ones_k = _group_ones(KVW, HEAD_DIM)

        def fold(v):
            hi = v.astype(BF16)
            lo = (v - hi.astype(F32)).astype(BF16)
            return _dot_nt(hi, rep) + _dot_nt(lo, rep)

        kr = p_ref[:, 1024:1152]
        krstd = lax.rsqrt(_group_mean(kr * kr, ones_k, HEAD_DIM) + EPS)
        khat = kr * krstd
        kw = knw_ref[...]
        dkro = fold(dk_ref[...])
        dv = fold(dv_ref[...])

        def k_back(dkn):
            dkhat = dkn * kw
            dkr = krstd * (dkhat - khat * _group_mean(dkhat * khat, ones_k, HEAD_DIM))
            acc_ref[2:3, 0:KVW] += jnp.sum(dkn * khat, axis=0, keepdims=True)
            dp_ref[:, 1024:1152] = dkr.astype(BF16)
            dp_ref[:, 1152:1280] = dv.astype(BF16)

        @pl.when(i == 0)
        def _():
            k_back(dkro)
            dp_ref[:, 0:1024] = jnp.zeros((TM, 1024), BF16)

        @pl.when(i > 0)
        def _():
            cs, sn = cos_ref[...], sin_ref[...]
            k_back(dkro * cs + _rope_partner(dkro * sn))
            ones_q = _group_ones(ATW, HEAD_DIM)
            qr = p_ref[:, 512:1024]
            qrstd = lax.rsqrt(_group_mean(qr * qr, ones_q, HEAD_DIM) + EPS)
            qhat = qr * qrstd
            dqro = dq_ref[...] * HEAD_DIM ** -0.5
            dqn = dqro * _tile_lanes(cs, 4) + _rope_partner(dqro * _tile_lanes(sn, 4))
            dqhat = dqn * qnw_ref[...]
            dqr = qrstd * (dqhat - qhat * _group_mean(dqhat * qhat, ones_q, HEAD_DIM))
            acc_ref[1:2, :] += jnp.sum(dqn * qhat, axis=0, keepdims=True)
            dp_ref[:, 512:1024] = dqr.astype(BF16)
            dws = jnp.zeros((1, HG_DIM), F32)
            for h in range(HG_HEADS):
                sl = slice(HG_DIM * h, HG_DIM * (h + 1))
                oh, gh, dy = o_ref[:, sl], p_ref[:, sl], dy_ref[:, sl]
                rstd = lax.rsqrt(jnp.mean(oh * oh, axis=-1, keepdims=True) + EPS)
                ohat = oh * rstd
                sg = _sigmoid(gh)
                dp_ref[:, sl] = (dy * (ohat * hnw_ref[...]) * (sg * (1.0 + gh * (1.0 - sg)))).astype(BF16)
                dn = dy * (gh * sg)
                dws = dws + jnp.sum(dn * ohat, axis=0, keepdims=True)
                dohat = dn * hnw_ref[...]
                do_ref[:, sl] = rstd * (dohat - ohat * jnp.mean(dohat * ohat, axis=-1, keepdims=True))
            acc_ref[0:1, 0:HG_DIM] += dws

    return _pcall(
        body, carried, name="prep_bwd", grid=(N_TILES,),
        in_specs=[pl.BlockSpec((TM, WB), lambda i: (i, 0)),
                  pl.BlockSpec((TM, HGW), lambda i: (_lat(i), 0)),
                  pl.BlockSpec((TM, 128), lambda i: (_lat(i), 0)),
                  pl.BlockSpec((TM, 128), lambda i: (_lat(i), 0)),
                  _full((1, HG_DIM)), _full((1, ATW)), _full((1, KVW)),
                  pl.BlockSpec((TM, HGW), lambda i: (_lat(i), 0)),
                  pl.BlockSpec((TM, ATW), lambda i: (_lat(i), 0)),
                  pl.BlockSpec((TM, ATW), lambda i: (i, 0)),
                  pl.BlockSpec((TM, ATW), lambda i: (i, 0))],
        out_specs=[pl.BlockSpec((TM, WB), lambda i: (i, 0)),
                   pl.BlockSpec((TM, HGW), lambda i: (_lat(i), 0)),
                   _full((8, ATW))],
        out_shape=[jax.ShapeDtypeStruct((T, WB), BF16), jax.ShapeDtypeStruct((S, HGW), F32),
                   jax.ShapeDtypeStruct((8, ATW), F32)],
        scratch_shapes=[], operands=[p_b, o, cos, sin, hnw, qnw, knw, dy_hg, dq, dk_rep, dv_rep])


NEG = -1e30
_CTX_BLOCKS = L // BLOCK


def _attn_window_specs():
    prev = pl.BlockSpec((BLOCK, ATW), lambda i: (jnp.maximum(i - 1, 0) + _CTX_BLOCKS, 0))
    own = pl.BlockSpec((BLOCK, ATW), lambda i: (i + _CTX_BLOCKS, 0))
    nxt = pl.BlockSpec((BLOCK, ATW), lambda i: (jnp.minimum(i + 1, N_BLOCKS - 1) + _CTX_BLOCKS, 0))
    return [prev, own, nxt, _full((L, ATW))]


def _attn_valid(i, heads, context):
    n_keys = 3 * BLOCK + (L if context else 0)
    qi = lax.broadcasted_iota(jnp.int32, (heads * BLOCK, n_keys), 0) % BLOCK
    kj = lax.broadcasted_iota(jnp.int32, (heads * BLOCK, n_keys), 1)
    window = ((jnp.abs(kj - BLOCK - qi) <= BLOCK) & ((kj >= BLOCK) | (i > 0))
              & ((kj < 2 * BLOCK) | (i < N_BLOCKS - 1)))
    return window | (kj >= 3 * BLOCK)


def _stack_heads(qg):
    lane = lax.broadcasted_iota(jnp.int32, (1, 256), 1) // HEAD_DIM
    return jnp.concatenate([jnp.where(lane == g, qg, jnp.zeros_like(qg)) for g in range(4)], axis=0)


def _unstack_heads(v4):
    lane = lax.broadcasted_iota(jnp.int32, (1, 256), 1) // HEAD_DIM
    out = jnp.where(lane == 0, v4[0:BLOCK], 0.0)
    for g in range(1, 4):
        out = out + jnp.where(lane == g, v4[g * BLOCK:(g + 1) * BLOCK], 0.0)
    return out


def _sink_rows(sink_ref, hk):
    return jnp.concatenate(
        [jnp.broadcast_to(sink_ref[0:1, 4 * hk + g:4 * hk + g + 1], (BLOCK, 1)) for g in range(4)], axis=0)


def _attn_fwd(q, k_rep, v_rep, sinks, carried=None):
    def body(q_ref, kp, ko, kn, kc, vp, vo, vn, vc, sink_ref, y_ref, lse_ref):
        i = pl.program_id(0)
        valid = _attn_valid(i, 1, True)
        lane8 = lax.broadcasted_iota(jnp.int32, (1, ATT_HEADS), 1)
        head_of_lane = lax.broadcasted_iota(jnp.int32, (1, 256), 1) // HEAD_DIM
        lse_out = jnp.zeros((BLOCK, ATT_HEADS), F32)
        for hk in range(KV_HEADS):
            sl = slice(256 * hk, 256 * (hk + 1))
            qg = q_ref[:, sl]
            keys = jnp.concatenate([kp[:, sl], ko[:, sl], kn[:, sl], kc[:, sl]], axis=0)
            vals = jnp.concatenate([vp[:, sl], vo[:, sl], vn[:, sl], vc[:, sl]], axis=0)
            yg = jnp.zeros((BLOCK, 256), F32)
            for g in range(4):
                q1 = jnp.where(head_of_lane == g, qg, jnp.zeros_like(qg))
                s = jnp.where(valid, _dot_nt(q1, keys), NEG)
                sink = sink_ref[0:1, 4 * hk + g:4 * hk + g + 1]
                m = jnp.maximum(jnp.max(s, axis=1, keepdims=True), sink)
                p = jnp.exp(s - m)
                den = jnp.sum(p, axis=1, keepdims=True) + jnp.exp(sink - m)
                o1 = _dot(p.astype(BF16), vals) * (1.0 / den)
                yg = yg + jnp.where(head_of_lane == g, o1, 0.0)
                lse_out = lse_out + jnp.where(lane8 == 4 * hk + g, m + jnp.log(den), 0.0)
            y_ref[:, sl] = yg.astype(BF16)
        lse_ref[...] = lse_out

    return _pcall(
        body, carried, name="attn_fwd", grid=(N_BLOCKS,),
        in_specs=[pl.BlockSpec((BLOCK, ATW), lambda i: (i, 0))] + _attn_window_specs()
        + _attn_window_specs() + [_full((1, ATT_HEADS))],
        out_specs=[pl.BlockSpec((BLOCK, ATW), lambda i: (i, 0)),
                   pl.BlockSpec((BLOCK, ATT_HEADS), lambda i: (i, 0))],
        out_shape=[jax.ShapeDtypeStruct((S, ATW), BF16), jax.ShapeDtypeStruct((S, ATT_HEADS), F32)],
        scratch_shapes=[],
        operands=[q, k_rep, k_rep, k_rep, k_rep, v_rep, v_rep, v_rep, v_rep, sinks])


def _attn_bwd(q, k_rep, v_rep, sinks, y_at, lse, dy, carried=None):
    def body(q_ref, kp, ko, kn, kc, vp, vo, vn, vc, sink_ref, y_ref, lse_ref, dy_ref,
             dq_ref, dk_ref, dv_ref, dsink_ref, dk_acc, dv_acc):
        i = pl.program_id(0)

        @pl.when(i == 0)
        def _():
            dk_acc[...] = jnp.zeros_like(dk_acc)
            dv_acc[...] = jnp.zeros_like(dv_acc)
            dk_ref[pl.ds(0, L), :] = jnp.zeros((L, ATW), F32)
            dv_ref[pl.ds(0, L), :] = jnp.zeros((L, ATW), F32)
            dsink_ref[...] = jnp.zeros_like(dsink_ref)

        valid = _attn_valid(i, 4, False)
        lane8 = lax.broadcasted_iota(jnp.int32, (1, ATT_HEADS), 1)
        w0 = pl.multiple_of(i * BLOCK, BLOCK)
        dsink = jnp.zeros((1, ATT_HEADS), F32)
        for hk in range(KV_HEADS):
            sl = slice(256 * hk, 256 * (hk + 1))
            q4 = _stack_heads(q_ref[:, sl])
            do4f = _stack_heads(dy_ref[:, sl])
            o4 = _stack_heads(y_ref[:, sl]).astype(F32)
            do4 = do4f.astype(BF16)
            kl = jnp.concatenate([kp[:, sl], ko[:, sl], kn[:, sl]], axis=0)
            vl = jnp.concatenate([vp[:, sl], vo[:, sl], vn[:, sl]], axis=0)
            lse4 = jnp.concatenate(
                [jnp.sum(jnp.where(lane8 == 4 * hk + g, lse_ref[...], 0.0), axis=1, keepdims=True)
                 for g in range(4)], axis=0)
            p_loc = jnp.where(valid, jnp.exp(_dot_nt(q4, kl) - lse4), 0.0)
            p_ctx = jnp.exp(_dot_nt(q4, kc[:, sl]) - lse4)
            delta = jnp.sum(do4f * o4, axis=1, keepdims=True)
            ds_loc = (p_loc * (_dot_nt(do4, vl) - delta)).astype(BF16)
            ds_ctx = (p_ctx * (_dot_nt(do4, vc[:, sl]) - delta)).astype(BF16)
            dq_ref[:, sl] = _unstack_heads(_dot(ds_loc, kl) + _dot(ds_ctx, kc[:, sl]))
            dk_acc[pl.ds(w0, 3 * BLOCK), sl] += _dot_tn(ds_loc, q4)
            dv_acc[pl.ds(w0, 3 * BLOCK), sl] += _dot_tn(p_loc.astype(BF16), do4)
            dk_ref[pl.ds(0, L), sl] += _dot_tn(ds_ctx, q4)
            dv_ref[pl.ds(0, L), sl] += _dot_tn(p_ctx.astype(BF16), do4)
            p_sink = jnp.exp(_sink_rows(sink_ref, hk) - lse4)
            for g in range(4):
                rows = slice(g * BLOCK, (g + 1) * BLOCK)
                dsink = dsink + jnp.where(lane8 == 4 * hk + g,
                                          -jnp.sum(p_sink[rows] * delta[rows], axis=0, keepdims=True), 0.0)
        dsink_ref[...] += dsink

        @pl.when(i == N_BLOCKS - 1)
        def _():
            dk_ref[pl.ds(L, S), :] = dk_acc[pl.ds(BLOCK, S), :]
            dv_ref[pl.ds(L, S), :] = dv_acc[pl.ds(BLOCK, S), :]

    row_q = pl.BlockSpec((BLOCK, ATW), lambda i: (i, 0))
    return _pcall(
        body, carried, name="attn_bwd", grid=(N_BLOCKS,),
        in_specs=[row_q] + _attn_window_specs() + _attn_window_specs()
        + [_full((1, ATT_HEADS)), row_q, pl.BlockSpec((BLOCK, ATT_HEADS), lambda i: (i, 0)), row_q],
        out_specs=[row_q, _full((T, ATW)), _full((T, ATW)), _full((1, ATT_HEADS))],
        out_shape=[jax.ShapeDtypeStruct((S, ATW), F32), jax.ShapeDtypeStruct((T, ATW), F32),
                   jax.ShapeDtypeStruct((T, ATW), F32), jax.ShapeDtypeStruct((1, ATT_HEADS), F32)],
        scratch_shapes=[pltpu.VMEM((S + 2 * BLOCK, ATW), F32), pltpu.VMEM((S + 2 * BLOCK, ATW), F32)],
        operands=[q, k_rep, k_rep, k_rep, k_rep, v_rep, v_rep, v_rep, v_rep, sinks, y_at, lse, dy])


def _merge_fwd(y_hg, y_at, p_c, x, w_bh, w_ba, w_out, g1, nfw, sh2, sc2, carried=None):
    def body(yh_ref, ya_ref, g_ref, x_ref, wbh_ref, wba_ref, wo_ref, g1_ref, nfw_ref, sh_ref, sc_ref,
             mx_ref, r_ref, x1_ref, h2_ref):
        a = _dot_nt(yh_ref[...], wbh_ref[...])
        b = _dot_nt(ya_ref[...], wba_ref[...])
        mixed = (_sigmoid(g_ref[:, :D]) * a + _sigmoid(g_ref[:, D:]) * b).astype(BF16)
        r = _dot(mixed, wo_ref[...])
        x1 = x_ref[...] + g1_ref[...] * r
        mx_ref[...] = mixed
        r_ref[...] = r
        x1_ref[...] = x1
        h2_ref[...] = _rms_mod(x1, nfw_ref[...], sh_ref[...], sc_ref[...]).astype(BF16)

    row = lambda w: pl.BlockSpec((TM, w), lambda i: (i, 0))
    vec = _full((1, D))
    return _pcall(
        body, carried, name="merge_fwd", grid=(N_LAT_TILES,),
        in_specs=[row(HGW), row(ATW), row(WC), row(D), _VMEM_WHOLE, _VMEM_WHOLE, _VMEM_WHOLE,
                  vec, vec, vec, vec],
        out_specs=[row(D)] * 4,
        out_shape=[jax.ShapeDtypeStruct((S, D), dt) for dt in (BF16, F32, F32, BF16)],
        scratch_shapes=[], operands=[y_hg, y_at, p_c, x, w_bh, w_ba, w_out, g1, nfw, sh2, sc2])


def _merge_bwd(dx1, r, y_hg, y_at, p_c, w_bh, w_ba, w_out, g1, carried=None):
    def body(dx_ref, r_ref, yh_ref, ya_ref, g_ref, wbh_ref, wba_ref, wo_ref, g1_ref,
             dr_ref, da_ref, db_ref, dg_ref, dyh_ref, dya_ref, acc_ref):
        @pl.when(pl.program_id(0) == 0)
        def _():
            acc_ref[...] = jnp.zeros_like(acc_ref)

        dx1v = dx_ref[...]
        acc_ref[0:1, :] += jnp.sum(dx1v * r_ref[...], axis=0, keepdims=True)
        dr = (g1_ref[...] * dx1v).astype(BF16)
        dr_ref[...] = dr
        dmix = _dot_nt(dr, wo_ref[...])
        sh, sa = _sigmoid(g_ref[:, :D]), _sigmoid(g_ref[:, D:])
        da = (dmix * sh).astype(BF16)
        db = (dmix * sa).astype(BF16)
        da_ref[...] = da
        db_ref[...] = db
        dg_ref[:, :D] = (dmix * _dot_nt(yh_ref[...], wbh_ref[...]) * sh * (1.0 - sh)).astype(BF16)
        dg_ref[:, D:] = (dmix * _dot_nt(ya_ref[...], wba_ref[...]) * sa * (1.0 - sa)).astype(BF16)
        dyh_ref[...] = _dot(da, wbh_ref[...])
        dya_ref[...] = _dot(db, wba_ref[...])

    row = lambda w: pl.BlockSpec((TM, w), lambda i: (i, 0))
    return _pcall(
        body, carried, name="merge_bwd", grid=(N_LAT_TILES,),
        in_specs=[row(D), row(D), row(HGW), row(ATW), row(WC), _VMEM_WHOLE, _VMEM_WHOLE, _VMEM_WHOLE,
                  _full((1, D))],
        out_specs=[row(D), row(D), row(D), row(WC), row(HGW), row(ATW), _full((8, D))],
        out_shape=[jax.ShapeDtypeStruct((S, D), BF16), jax.ShapeDtypeStruct((S, D), BF16),
                   jax.ShapeDtypeStruct((S, D), BF16), jax.ShapeDtypeStruct((S, WC), BF16),
                   jax.ShapeDtypeStruct((S, HGW), F32), jax.ShapeDtypeStruct((S, ATW), F32),
                   jax.ShapeDtypeStruct((8, D), F32)],
        scratch_shapes=[], operands=[dx1, r, y_hg, y_at, p_c, w_bh, w_ba, w_out, g1])


def _ffn_fused(x1, h2, tgt, w_gate, w_up, w_down, g2, nfw, sc2):
    def body(x1_ref, h2_ref, t_ref, wg_ref, wu_ref, wd_ref, g2_ref, nfw_ref, sc_ref,
             act_ref, dgt_ref, dup_ref, df_ref, dx_ref, acc_ref, gs, us):
        @pl.when(pl.program_id(0) == 0)
        def _():
            acc_ref[...] = jnp.zeros_like(acc_ref)

        h2 = h2_ref[...]
        whole = lambda w_ref: w_ref[...].reshape(D_FF, D)
        tile = lambda j: slice(j * FF_TILE, (j + 1) * FF_TILE)
        for j in range(N_FF_TILES):
            g = _dot_nt(h2, wg_ref[j])
            u = _dot_nt(h2, wu_ref[j])
            gs[j] = g
            us[j] = u
            act_ref[:, tile(j)] = (g * _sigmoid(g) * u).astype(BF16)
        f = _dot(act_ref[...], whole(wd_ref))
        x1v = x1_ref[...]
        g2 = g2_ref[...]
        diff = x1v + g2 * f - t_ref[...]
        dy = diff * (1.0 / D)
        df = (g2 * dy).astype(BF16)
        df_ref[...] = df
        dact_all = _dot_nt(df, whole(wd_ref))
        for j in range(N_FF_TILES):
            g, u = gs[j], us[j]
            sg = _sigmoid(g)
            dact = dact_all[:, tile(j)]
            dgt_ref[:, tile(j)] = (dact * u * (sg * (1.0 + g * (1.0 - sg)))).astype(BF16)
            dup_ref[:, tile(j)] = (dact * (g * sg)).astype(BF16)
        dh2 = _dot(dgt_ref[...], whole(wg_ref)) + _dot(dup_ref[...], whole(wu_ref))
        dx, dsh, dsc, dnw = _rms_mod_bwd(x1v, nfw_ref[...], sc_ref[...], dh2)
        dx_ref[...] = dy + dx
        acc_ref[0:1, :] += dsh
        acc_ref[1:2, :] += dsc
        acc_ref[2:3, :] += dnw
        acc_ref[3:4, :] += jnp.sum(dy * f, axis=0, keepdims=True)
        acc_ref[4:5, :] += 0.5 * jnp.sum(jnp.sum(diff * diff, axis=1, keepdims=True), axis=0,
                                         keepdims=True) * (1.0 / D)

    row = lambda dt_w: pl.BlockSpec((TM, dt_w), lambda i: (i, 0))
    blk = row(D_FF)
    vec = _full((1, D))
    return pl.pallas_call(
        body, name="ffn_fused", grid=(N_LAT_TILES,),
        in_specs=[row(D), row(D), row(D), _VMEM_WHOLE, _VMEM_WHOLE, _VMEM_WHOLE, vec, vec, vec],
        out_specs=[blk, blk, blk, row(D), row(D), _full((8, D))],
        out_shape=[jax.ShapeDtypeStruct((S, D_FF), BF16)] * 3
        + [jax.ShapeDtypeStruct((S, D), BF16), jax.ShapeDtypeStruct((S, D), F32),
           jax.ShapeDtypeStruct((8, D), F32)],
        scratch_shapes=[pltpu.VMEM((N_FF_TILES, TM, FF_TILE), F32), pltpu.VMEM((N_FF_TILES, TM, FF_TILE), F32)],
        compiler_params=_cp(("arbitrary",)),
    )(x1, h2, tgt, w_gate, w_up, w_down, g2, nfw, sc2)


def _proj_bc(h_all, w_b, w_c, carried=None):
    def body(h_ref, wb_ref, wc_ref, pb_ref, pc_ref):
        h = h_ref[...]
        pb_ref[...] = _dot_nt(h, wb_ref[...])

        @pl.when(pl.program_id(0) > 0)
        def _():
            pc_ref[...] = _dot_nt(h, wc_ref[...])

    return _pcall(
        body, carried, name="proj_bc", grid=(N_TILES,),
        in_specs=[pl.BlockSpec((TM, D), lambda i: (i, 0)), _VMEM_WHOLE, _VMEM_WHOLE],
        out_specs=[pl.BlockSpec((TM, WB), lambda i: (i, 0)), pl.BlockSpec((TM, WC), lambda i: (_lat(i), 0))],
        out_shape=[jax.ShapeDtypeStruct((T, WB), F32), jax.ShapeDtypeStruct((S, WC), F32)],
        scratch_shapes=[], operands=[h_all, w_b, w_c])


def _input_bwd(dp_a, dp_b, dp_c, w_a, w_b, w_c, ctx, x, dx1, nw, sh, sc, carried=None):
    def body(da_ref, db_ref, dc_ref, wa_ref, wb_ref, wc_ref, ctx_ref, x_ref, dx1_ref, nw_ref, sh_ref,
             sc_ref, gx_ref, acc_ref):
        i = pl.program_id(0)

        @pl.when(i == 0)
        def _():
            acc_ref[...] = jnp.zeros_like(acc_ref)

        dh = _dot(da_ref[...], wa_ref[...]) + _dot(db_ref[...], wb_ref[...])

        @pl.when(i == 0)
        def _():
            _, dsh, dsc, dnw = _rms_mod_bwd(ctx_ref[...], nw_ref[...], sc_ref[0:1, :], dh)
            acc_ref[3:4, :] += dsh
            acc_ref[4:5, :] += dsc
            acc_ref[2:3, :] += dnw

        @pl.when(i > 0)
        def _():
            dhl = dh + _dot(dc_ref[...], wc_ref[...])
            dx, dsh, dsc, dnw = _rms_mod_bwd(x_ref[...], nw_ref[...], sc_ref[1:2, :], dhl)
            gx_ref[...] = dx1_ref[...] + dx
            acc_ref[0:1, :] += dsh
            acc_ref[1:2, :] += dsc
            acc_ref[2:3, :] += dnw

    lat = lambda w: pl.BlockSpec((TM, w), lambda i: (_lat(i), 0))
    return _pcall(
        body, carried, name="input_bwd", grid=(N_TILES,),
        in_specs=[pl.BlockSpec((TM, WA), lambda i: (i, 0)), pl.BlockSpec((TM, WB), lambda i: (i, 0)),
                  lat(WC), _VMEM_WHOLE, _VMEM_WHOLE, _VMEM_WHOLE, _full((TM, D)), lat(D), lat(D),
                  _full((1, D)), _full((2, D)), _full((2, D))],
        out_specs=[lat(D), _full((8, D))],
        out_shape=[jax.ShapeDtypeStruct((S, D), F32), jax.ShapeDtypeStruct((8, D), F32)],
        scratch_shapes=[], operands=[dp_a, dp_b, dp_c, w_a, w_b, w_c, ctx, x, dx1, nw, sh, sc])


_C1 = 1.0 - ADAM_B1 ** ADAM_STEP
_C2 = 1.0 - ADAM_B2 ** ADAM_STEP


def _adamw_math(w, g, m, v):
    m = ADAM_B1 * m + (1.0 - ADAM_B1) * g
    v = ADAM_B2 * v + (1.0 - ADAM_B2) * (g * g)
    m_hat = m / _C1
    v_hat = v / _C2
    delta = -ADAM_LR * (m_hat / (jnp.sqrt(v_hat) + ADAM_EPS) + ADAM_WD * w)
    return delta, m, v


def _adamw_sharded(terms, w, m, v, name, tr, extra=None, after=None):
    rows, cols = w.shape

    def body(*refs):
        t_ref, w_ref, m_ref, v_ref = refs[:4]
        g_ref, d_ref, nm_ref, nv_ref = refs[-4:]
        g = t_ref[0].astype(F32)
        for s in range(1, N_CHIPS):
            g = g + t_ref[s].astype(F32)
        if extra is not None:
            g = g + refs[4][...].astype(F32)
        g_ref[...] = g
        d_ref[...], nm_ref[...], nv_ref[...] = _adamw_math(w_ref[...], g, m_ref[...], v_ref[...])

    blk = pl.BlockSpec((tr, cols), lambda i: (i, 0))
    return pl.pallas_call(
        body, name=name, grid=(rows // tr,),
        in_specs=[pl.BlockSpec((N_CHIPS, tr, cols), lambda i: (0, i, 0)), blk, blk, blk]
        + ([blk] if extra is not None else []) + ([_ANY] if after is not None else []),
        out_specs=[blk] * 4,
        out_shape=[jax.ShapeDtypeStruct((rows, cols), F32)] * 4,
        compiler_params=_cp(("parallel",)),
    )(terms, w, m, v, *([extra] if extra is not None else []), *([after] if after is not None else []))


def _adamw_plain(g, w, m, v, name, tr=None):
    def body(g_ref, w_ref, m_ref, v_ref, d_ref, nm_ref, nv_ref):
        d_ref[...], nm_ref[...], nv_ref[...] = _adamw_math(w_ref[...], g_ref[...], m_ref[...], v_ref[...])

    if tr is None:
        return pl.pallas_call(
            body, name=name, in_specs=[_VMEM_WHOLE] * 4, out_specs=[_VMEM_WHOLE] * 3,
            out_shape=[jax.ShapeDtypeStruct(w.shape, F32)] * 3,
            compiler_params=_cp(),
        )(g, w, m, v)
    blk = pl.BlockSpec((tr, w.shape[1]), lambda i: (i, 0))
    return pl.pallas_call(
        body, name=name, grid=(w.shape[0] // tr,), in_specs=[blk] * 4, out_specs=[blk] * 3,
        out_shape=[jax.ShapeDtypeStruct(w.shape, F32)] * 3,
        compiler_params=_cp(("parallel",)),
    )(g, w, m, v)


SMALL_ROWS = 16
R_DMOD, R_DCTX, R_NMIX, R_NFFN, R_MISC, R_DLB, R_BADA01 = 0, 6, 8, 9, 10, 11, 13
M_HNW, M_QNW, M_KNW, M_SINK, M_LOSS = 0, 128, 256, 384, 512


def _pack_small(acc_in, acc_mg, acc_ffn, acc_prep, dsink, dlb):
    def body(in_ref, mg_ref, ff_ref, pp_ref, ds_ref, dlb_ref, o_ref):
        o_ref[...] = jnp.zeros_like(o_ref)
        o_ref[0:2, :] = in_ref[0:2, :]
        o_ref[2:3, :] = mg_ref[0:1, :]
        o_ref[3:5, :] = ff_ref[0:2, :]
        o_ref[5:6, :] = ff_ref[3:4, :]
        o_ref[6:8, :] = in_ref[3:5, :]
        o_ref[8:9, :] = in_ref[2:3, :]
        o_ref[9:10, :] = ff_ref[2:3, :]
        o_ref[10:11, M_HNW:M_HNW + HG_DIM] = pp_ref[0:1, 0:HG_DIM]
        r = lax.broadcasted_iota(jnp.int32, (ATW, 128), 0)
        c = lax.broadcasted_iota(jnp.int32, (ATW, 128), 1)
        fold = jnp.where((r % HEAD_DIM == c) & (c < HEAD_DIM), 1.0, 0.0).astype(BF16)
        qk = jnp.concatenate([pp_ref[1:2, :], pp_ref[2:3, :], jnp.zeros((6, ATW), F32)], axis=0)
        folded = _dot_exact_rhs01(qk, fold)
        o_ref[10:11, M_QNW:M_QNW + 128] = folded[0:1, :]
        o_ref[10:11, M_KNW:M_KNW + 128] = folded[1:2, :]
        o_ref[10:11, M_SINK:M_SINK + ATT_HEADS] = ds_ref[...]
        o_ref[10:11, M_LOSS:M_LOSS + 128] = ff_ref[4:5, 0:128]
        o_ref[11:13, 0:HGW] = dlb_ref[...]

    return pl.pallas_call(
        body, name="pack_small", in_specs=[_VMEM_WHOLE] * 6, out_specs=_VMEM_WHOLE,
        out_shape=jax.ShapeDtypeStruct((SMALL_ROWS, D), F32), compiler_params=_cp(),
    )(acc_in, acc_mg, acc_ffn, acc_prep, dsink, dlb)


def _sum_small(gathered):
    def body(g_ref, o_ref):
        tot = g_ref[0]
        for s in range(1, N_DEV):
            tot = tot + g_ref[s]
        o_ref[...] = tot
        o_ref[R_BADA01:R_BADA01 + 2, :] = tot[0:2, :] + tot[R_DCTX:R_DCTX + 2, :]

    return pl.pallas_call(
        body, name="sum_small", in_specs=[_VMEM_WHOLE], out_specs=_VMEM_WHOLE,
        out_shape=jax.ShapeDtypeStruct((SMALL_ROWS, D), F32), compiler_params=_cp(),
    )(gathered)


_REP_NAMES = ("b_ada", "c_ctx", "norm_mix_w", "norm_ffn_w", "hgrn_norm_w", "q_norm_w", "k_norm_w", "attn_sinks")


def _adamw_replicated(tot, g_c_ctx, ws, ms, vs):
    n = len(_REP_NAMES)

    def body(*refs):
        tot_ref, gc_ref = refs[0], refs[1]
        w_refs, m_refs, v_refs = refs[2:2 + n], refs[2 + n:2 + 2 * n], refs[2 + 2 * n:2 + 3 * n]
        outs = refs[2 + 3 * n:]
        row = lambda r: tot_ref[r:r + 1, :]
        misc = row(R_MISC)
        grads = [jnp.concatenate([row(R_BADA01), row(R_BADA01 + 1)] + [row(k) for k in range(2, 6)], axis=1),
                 gc_ref[...], row(R_NMIX), row(R_NFFN),
                 misc[:, M_HNW:M_HNW + HG_DIM], misc[:, M_QNW:M_QNW + HEAD_DIM],
                 misc[:, M_KNW:M_KNW + HEAD_DIM], misc[:, M_SINK:M_SINK + ATT_HEADS]]
        for k in range(n):
            outs[k][...] = grads[k]
            outs[n + k][...], outs[2 * n + k][...], outs[3 * n + k][...] = _adamw_math(
                w_refs[k][...], grads[k], m_refs[k][...], v_refs[k][...])

    shapes = [jax.ShapeDtypeStruct(w.shape, F32) for w in ws]
    return pl.pallas_call(
        body, name="adamw_replicated", in_specs=[_VMEM_WHOLE] * (2 + 3 * n), out_specs=[_VMEM_WHOLE] * (4 * n),
        out_shape=shapes * 4, compiler_params=_cp(),
    )(tot, g_c_ctx, *ws, *ms, *vs)


def _lb_grads(dlb, lbl):
    def body(d_ref, l_ref, o_ref):
        for d in (0, 1):
            ll = l_ref[d]
            lb = _sigmoid(ll[0:1, :] - ll[1:2, :])
            t = d_ref[d:d + 1, :] * lb * (1.0 - lb)
            o_ref[d, 0:1, :] = t
            o_ref[d, 1:2, :] = -t

    return pl.pallas_call(
        body, name="lb_grads", in_specs=[_VMEM_WHOLE] * 2, out_specs=_VMEM_WHOLE,
        out_shape=jax.ShapeDtypeStruct((2, 2, HGW), F32), compiler_params=_cp(),
    )(dlb, lbl)


def _c_ctx_grad(terms, c_ctx):
    def body(t_ref, c_ref, o_ref):
        tot = t_ref[0, 8:9, :]
        for s in range(1, N_DEV):
            tot = tot + t_ref[s, 8:9, :]
        cv = c_ref[...]
        sg = _sigmoid(cv)
        o_ref[...] = tot * (sg * (1.0 + cv * (1.0 - sg)))

    return pl.pallas_call(
        body, name="c_ctx_grad", in_specs=[_VMEM_WHOLE] * 2, out_specs=_VMEM_WHOLE,
        out_shape=jax.ShapeDtypeStruct((1, D), F32), compiler_params=_cp(),
    )(terms, c_ctx)


def _in_perm():
    fz, bz, inp, kk, vv, qhg, ghg, qat, gates = 0, 512, 1024, 1536, 1664, 1792, 2304, 2816, 3328
    cols = []
    for h in range(HG_HEADS):
        for base in (fz, bz, inp, qhg):
            cols += list(range(base + 128 * h, base + 128 * (h + 1)))
    cols += list(range(ghg, ghg + 512)) + list(range(qat, qat + 512))
    cols += list(range(kk, kk + 128)) + list(range(vv, vv + 128))
    cols += list(range(gates, gates + 2048))
    return np.asarray(cols, np.int32)


_PERM = _in_perm()


_PIECES = {"a": (0, WA, 128), "b": (WA, WB, 256), "c": (WA + WB, WC, 256)}


def _block_starts(piece):
    lo, n, blk = _PIECES[piece]
    starts = [int(_PERM[r]) for r in range(lo, lo + n, blk)]
    assert all(s % blk == 0 and np.array_equal(_PERM[r:r + blk], np.arange(s, s + blk))
               for s, r in zip(starts, range(lo, lo + n, blk)))
    return starts, blk


def _block_table(piece):
    starts, blk = _block_starts(piece)
    return jnp.asarray([s // blk for s in starts], jnp.int32), blk


def _order_w(w_in_t):
    def body(x_ref, *o_refs):
        for o_ref, piece in zip(o_refs, "abc"):
            starts, blk = _block_starts(piece)
            for i, s in enumerate(starts):
                o_ref[i * blk:(i + 1) * blk, :] = x_ref[s:s + blk, :]

    return pl.pallas_call(
        body, name="order_w", in_specs=[_VMEM_WHOLE], out_specs=[_VMEM_WHOLE] * 3,
        out_shape=[jax.ShapeDtypeStruct((_PIECES[p][1], D), w_in_t.dtype) for p in "abc"],
        compiler_params=_cp(),
    )(w_in_t)


def _mm_tn_placed(a, b, table, blk, into, out_rows, name):
    k, n = b.shape

    def body(t_ref, a_ref, b_ref, *rest):
        rest[-1][...] = _dot_tn(a_ref[...], b_ref[...]).astype(BF16)

    operands = [table, a, b]
    in_specs, aliases = [pl.BlockSpec((k, blk), lambda i, t: (0, i)), pl.BlockSpec((k, n), lambda i, t: (0, 0))], {}
    if into is not None:
        operands.append(into)
        in_specs.append(_ANY)
        aliases = {3: 0}
    return pl.pallas_call(
        body, name=name,
        grid_spec=pltpu.PrefetchScalarGridSpec(
            num_scalar_prefetch=1, grid=(table.shape[0],), in_specs=in_specs,
            out_specs=pl.BlockSpec((blk, n), lambda i, t: (t[i], 0))),
        out_shape=jax.ShapeDtypeStruct((out_rows, n), BF16),
        input_output_aliases=aliases,
        compiler_params=_cp(("arbitrary",)),
    )(*operands)


def _local_step(x2, ctx2, h_all, h_lat, tgt, lbl, sh_in, sc_in, gate1, sh2, sc2, gate2, norm_mix_w, norm_ffn_w,
                hgrn_norm_w, q_norm_w, k_norm_w, attn_sinks, w_a, w_b, w_c, s_bh, s_ba, s_out,
                s_gate, s_up, s_down):
    first_last = lambda n: [(0, True), (n - 1, False)]
    p_a = _mm_nt(h_all, w_a, tm=T, tn=512, out_dtype=F32, name="proj_a")
    (o, st), (g_gate, g_bh, g_ba) = _hgrn_fwd(
        p_a, lbl, (_gather_comm_relayed([s_gate, s_bh, s_ba]),
                   [(0, True), (HG_HEADS - 2, True), (HG_HEADS - 1, False)]))
    (p_b, p_c), (g_out,) = _proj_bc(
        h_all, w_b, w_c, (_gather_comm_relayed([s_out]), [(0, True), (N_TILES - 4, True), (N_TILES - 1, False)]))
    cos, sin = _rope_tables()
    qnw_t, knw_t = jnp.tile(q_norm_w, (1, ATT_HEADS)), jnp.tile(k_norm_w, (1, KV_HEADS))
    y_hg, qn, k_rep, v_rep = _prep_fwd(p_b, o, cos, sin, hgrn_norm_w, qnw_t, knw_t)
    (y_at, lse), (g_up, g_down) = _attn_fwd(
        qn, k_rep, v_rep, attn_sinks,
        (_gather_comm_relayed([s_up, s_down]), [(0, True), (N_BLOCKS - 6, True), (N_BLOCKS - 1, False)]))
    w_bh, w_ba, w_o = g_bh.reshape(D, HGW), g_ba.reshape(D, ATW), g_out.reshape(D, D)
    (mixed, r, x1, h2), _ = _merge_fwd(
        y_hg, y_at, p_c, x2, w_bh, w_ba, w_o, gate1, norm_ffn_w, sh2, sc2)
    g_gate, g_up, g_down = [g.reshape(N_FF_TILES, FF_TILE, D) for g in (g_gate, g_up, g_down)]

    act, d_gate, d_up, d_f, dx1, acc_ffn = _ffn_fused(x1, h2, tgt, g_gate, g_up, g_down, gate2,
                                                      norm_ffn_w, sc2)
    by_chip = lambda t: t.reshape((N_CHIPS, 2) + t.shape[1:])
    ff_by_chip = lambda t: t.reshape(N_CHIPS, 2, FF_BLK, D)
    t_down, _ = _mm_tn_blocked(act, d_f, "grad_down", N_FF_TILES)
    t_down = ff_by_chip(t_down)
    t_gate, (f_down,) = _mm_tn_blocked(d_gate, h2, "grad_gate", N_FF_HALVES,
                                       (_sibling_comm([t_down]), first_last(N_FF_HALVES)))
    t_gate = ff_by_chip(t_gate)
    t_up, (f_gate,) = _mm_tn_blocked(d_up, h2, "grad_up", N_FF_HALVES,
                                     (_sibling_comm([t_gate]), first_last(N_FF_HALVES)))
    t_up = ff_by_chip(t_up)

    (d_r, d_a, d_b, dp_c, dy_hg, dy_at, acc_mg), (f_up,) = _merge_bwd(
        dx1, r, y_hg, y_at, p_c, w_bh, w_ba, w_o, gate1, (_sibling_comm([t_up]), first_last(N_LAT_TILES)))
    c_down, c_gate, c_up = [_pair_sum(t, f, "pair_sum_" + nm) for t, f, nm in
                            ((t_down, f_down, "down"), (t_gate, f_gate, "gate"), (t_up, f_up, "up"))]
    t_out = _mm_tn(mixed, d_r, tk=1024, nk=2, tm=1024, tn=1024, out_dtype=BF16, name="grad_out")
    t_bh = _mm_tn(d_a, y_hg, tk=2048, nk=1, tm=1024, tn=512, out_dtype=BF16, name="grad_bh")
    t_ba = _mm_tn(d_b, y_at, tk=2048, nk=1, tm=1024, tn=512, out_dtype=BF16, name="grad_ba")
    t_bh, t_ba, t_out = [by_chip(t.reshape(N_DEV, D // N_DEV, t.shape[1])) for t in (t_bh, t_ba, t_out)]
    (dq, dk_rep, dv_rep, dsink), (r_up,) = _attn_bwd(
        qn, k_rep, v_rep, attn_sinks, y_at, lse, dy_at, (_chip_comm([c_up]), first_last(N_BLOCKS)))
    (dp_b, d_o, acc_prep), (f_bh, f_ba, f_out) = _prep_bwd(
        p_b, o, cos, sin, hgrn_norm_w, qnw_t, knw_t, dy_hg, dq, dk_rep, dv_rep,
        (_sibling_comm([t_bh, t_ba, t_out]), first_last(N_TILES)))
    c_bh, c_ba, c_out = [_pair_sum(t, f, "pair_sum_" + nm) for t, f, nm in
                         ((t_bh, f_bh, "bh"), (t_ba, f_ba, "ba"), (t_out, f_out, "out"))]
    (dp_a, dlb), (r_bh, r_ba, r_out, r_down, r_gate) = _hgrn_bwd(
        p_a, lbl, d_o, st, (_chip_comm([c_bh, c_ba, c_out, c_down, c_gate]), first_last(HG_HEADS)))
    t_in = None
    for dp, h, nm in ((dp_a, h_all, "a"), (dp_b, h_all, "b"), (dp_c, h_lat, "c")):
        t_in = _mm_tn_placed(dp, h, *_block_table(nm), t_in, IN_COLS, "grad_in_" + nm)
    t_in = by_chip(t_in.reshape(N_DEV, IN_BLK, D))
    (f_in,) = _run_comm(_sibling_comm([t_in]), "scatter_in_sibling")
    c_in = _pair_sum(t_in, f_in, "pair_sum_in")
    sems, c_in, land, token = _chip_exchange_start(c_in, jnp.zeros(c_in.shape, c_in.dtype))
    (grad_x, acc_in), _ = _input_bwd(dp_a, dp_b, dp_c, w_a, w_b, w_c, ctx2, x2, dx1,
                                     norm_mix_w + token[0, 0], sh_in, sc_in)
    small = _pack_small(acc_in, acc_mg, acc_ffn, acc_prep, dsink, dlb)
    return grad_x, small, [r_bh, r_ba, r_out, r_gate, r_up, r_down], (sems, c_in, land)


def kernel(x, c, ctx, c_ctx, w_ada, b_ada, norm_mix_w, norm_ffn_w, w_in, hgrn_lb_logits, hgrn_norm_w, q_norm_w, k_norm_w, attn_sinks, w_branch_hgrn, w_branch_attn, w_out, w_ffn_gate, w_ffn_up, w_ffn_down, loss_target, m_c_ctx, m_w_ada, m_b_ada, m_norm_mix_w, m_norm_ffn_w, m_w_in, m_hgrn_lb_logits, m_hgrn_norm_w, m_q_norm_w, m_k_norm_w, m_attn_sinks, m_w_branch_hgrn, m_w_branch_attn, m_w_out, m_w_ffn_gate, m_w_ffn_up, m_w_ffn_down, v_c_ctx, v_w_ada, v_b_ada, v_norm_mix_w, v_norm_ffn_w, v_w_in, v_hgrn_lb_logits, v_hgrn_norm_w, v_q_norm_w, v_k_norm_w, v_attn_sinks, v_w_branch_hgrn, v_w_branch_attn, v_w_out, v_w_ffn_gate, v_w_ffn_up, v_w_ffn_down):
    me = 4 * lax.axis_index("x") + 2 * lax.axis_index("y") + lax.axis_index("c")
    x2, ctx2, tgt = x[0], ctx[0], loss_target[0]
    w_ada2, w_in2 = w_ada[0], w_in[0]

    cond = jnp.zeros((8, D), F32).at[0].set(c[0]).at[1, :256].set(hgrn_lb_logits.reshape(256))
    b_cols = lax.dynamic_slice(b_ada, (0, me * ADA_BLK), (1, ADA_BLK))
    g0, cc, mod, g_in, h_all, h_lat = _prologue(cond, c_ctx.reshape(1, D), w_ada2, b_cols, w_in2.T.astype(BF16),
                                         x2, ctx2, norm_mix_w)
    lbl = jnp.transpose(g0[:, 1, :256].reshape(N_DEV, 2, 2, 64), (1, 2, 0, 3)).reshape(2, 2, HGW)
    sh1, sc1, gate1, sh2, sc2, gate2 = [mod[k:k + 1] for k in range(6)]
    sh_in = jnp.concatenate([mod[6:7], sh1], axis=0)
    sc_in = jnp.concatenate([mod[7:8], sc1], axis=0)

    shards = [w_branch_hgrn[0].T, w_branch_attn[0].T, w_out[0], w_ffn_gate[0].T, w_ffn_up[0].T, w_ffn_down[0]]
    w_in_t = g_in.reshape(IN_COLS, D)
    w_a, w_b, w_c = _order_w(w_in_t)

    grad_x, small, (r_bh, r_ba, r_out, r_gate, r_up, r_down), pending_in = _local_step(
        x2, ctx2, h_all, h_lat, tgt, lbl, sh_in, sc_in, gate1, sh2, sc2, gate2, norm_mix_w, norm_ffn_w, hgrn_norm_w,
        q_norm_w, k_norm_w, attn_sinks, w_a, w_b, w_c, *[s.astype(BF16) for s in shards])

    big, updated = {}, []
    for nm, rr, ww, mm, vv, tr, transposed in (
            ("w_branch_hgrn", r_bh, w_branch_hgrn[0], m_w_branch_hgrn[0], v_w_branch_hgrn[0], 128, True),
            ("w_branch_attn", r_ba, w_branch_attn[0], m_w_branch_attn[0], v_w_branch_attn[0], 128, True),
            ("w_out", r_out, w_out[0], m_w_out[0], v_w_out[0], 128, False),
            ("w_ffn_gate", r_gate, w_ffn_gate[0], m_w_ffn_gate[0], v_w_ffn_gate[0], 176, True),
            ("w_ffn_up", r_up, w_ffn_up[0], m_w_ffn_up[0], v_w_ffn_up[0], 176, True),
            ("w_ffn_down", r_down, w_ffn_down[0], m_w_ffn_down[0], v_w_ffn_down[0], 176, False)):
        if transposed:
            res = _adamw_sharded(rr, ww.T, mm.T, vv.T, "adamw_" + nm, tr, after=grad_x)
            big[nm] = [t.T[None] for t in res]
        else:
            res = _adamw_sharded(rr, ww, mm, vv, "adamw_" + nm, tr, after=grad_x)
            big[nm] = [t[None] for t in res]
        updated.append(res[1])

    (g2,) = _all_gather([small], "gather_small", True, after=updated)
    tot = _sum_small(g2)
    dm = jnp.zeros((16, 6 * D), F32).at[:8].set(g2[:, R_DMOD:R_DMOD + 6, :].reshape(N_DEV, 6 * D))
    dm = dm.at[8, :2 * D].set(tot[R_DCTX:R_DCTX + 2].reshape(2 * D))
    dm_cols = lax.dynamic_slice(dm, (0, me * ADA_BLK), (16, ADA_BLK))
    g_w_ada, dsc_term = _ada_grads(cc, dm_cols, w_ada2)
    (g3,) = _all_gather([dsc_term], "gather_cctx", True)
    g_c_ctx = _c_ctx_grad(g3, c_ctx.reshape(1, D))
    g_lbl = _lb_grads(tot[R_DLB:R_DLB + 2, :HGW], lbl)
    g_lb_mine = lax.dynamic_slice(g_lbl, (0, 0, me * 64), (2, 2, 64))
    misc = tot[R_MISC]
    loss = misc[M_LOSS]

    rep_out = _adamw_replicated(
        tot, g_c_ctx,
        [b_ada, c_ctx.reshape(1, D), norm_mix_w, norm_ffn_w, hgrn_norm_w, q_norm_w, k_norm_w, attn_sinks],
        [m_b_ada, m_c_ctx.reshape(1, D), m_norm_mix_w, m_norm_ffn_w, m_hgrn_norm_w, m_q_norm_w, m_k_norm_w,
         m_attn_sinks],
        [v_b_ada, v_c_ctx.reshape(1, D), v_norm_mix_w, v_norm_ffn_w, v_hgrn_norm_w, v_q_norm_w, v_k_norm_w,
         v_attn_sinks])
    rep = []
    for kind in range(4):
        vals = dict(zip(_REP_NAMES, rep_out[kind * len(_REP_NAMES):(kind + 1) * len(_REP_NAMES)]))
        vals["c_ctx"] = vals["c_ctx"].reshape(D)
        rep.append(vals)

    sems, c_in, land = pending_in
    d_ada, nm_ada, nv_ada = _adamw_plain(g_w_ada, w_ada2, m_w_ada[0], v_w_ada[0], "adamw_w_ada", tr=256)
    land = _chip_exchange_wait(sems, c_in, land, d_ada)
    own = lax.dynamic_index_in_dim(c_in, 2 * lax.axis_index("x") + lax.axis_index("y"), 0, keepdims=False)
    big["w_in"] = [t.T[None] for t in _adamw_sharded(land, w_in2.T, m_w_in[0].T, v_w_in[0].T, "adamw_w_in", 336,
                                                     extra=own)]
    ada = [t[None] for t in (g_w_ada, d_ada, nm_ada, nv_ada)]
    lb_w = hgrn_lb_logits.reshape(4, 64)
    d_lb, nm_lb, nv_lb = _adamw_plain(g_lb_mine.reshape(4, 64), lb_w, m_hgrn_lb_logits.reshape(4, 64),
                                      v_hgrn_lb_logits.reshape(4, 64), "adamw_lb")
    lbs = [t.reshape(2, 2, 64) for t in (g_lb_mine, d_lb, nm_lb, nv_lb)]

    names = ['c_ctx', 'w_ada', 'b_ada', 'norm_mix_w', 'norm_ffn_w', 'w_in', 'hgrn_lb_logits', 'hgrn_norm_w',
             'q_norm_w', 'k_norm_w', 'attn_sinks', 'w_branch_hgrn', 'w_branch_attn', 'w_out', 'w_ffn_gate',
             'w_ffn_up', 'w_ffn_down']
    outs = [loss, grad_x[None]]
    for kind in range(4):
        for nm in names:
            if nm == 'w_ada':
                outs.append(ada[kind])
            elif nm == 'hgrn_lb_logits':
                outs.append(lbs[kind])
            elif nm in big:
                outs.append(big[nm][kind])
            else:
                outs.append(rep[kind][nm])
    return tuple(outs)
```

```python
import functools
import math

import numpy as np
import jax
import jax.numpy as jnp
from jax import lax
from jax.experimental import pallas as pl
from jax.experimental.pallas import tpu as pltpu

F32 = jnp.float32
BF16 = jnp.bfloat16

N_DEV = 8
D = 1024
S = 2048
L = 256
T = L + S
TM = 256
N_TILES = T // TM
N_LAT_TILES = S // TM
HG_HEADS = 4
HG_DIM = 128
HGW = 512
CHUNK = 32
N_CHUNKS = T // CHUNK
N_CTX_CHUNKS = L // CHUNK
ATT_HEADS = 8
KV_HEADS = 2
HEAD_DIM = 64
ATW = 512
KVW = 128
BLOCK = 128
N_BLOCKS = S // BLOCK
GRID_W = 64
ROPE_THETA = 10000.0
D_FF = 2816
FF_BLK = D_FF // N_DEV
FF_TILE = 256
N_FF_TILES = D_FF // FF_TILE
N_FF_HALVES = 2
IN_COLS = 5376
IN_BLK = IN_COLS // N_DEV
ADA_BLK = 6 * D // N_DEV
EPS = 1e-6
WA, WB, WC = 2048, 1280, 2048

ADAM_LR = 0.001
ADAM_B1 = 0.9
ADAM_B2 = 0.999
ADAM_EPS = 1e-08
ADAM_WD = 0.01
ADAM_STEP = 10

VMEM_LIMIT = 56 * 1024 * 1024
MESH = pl.DeviceIdType.MESH


def _cp(sem=None, vmem=VMEM_LIMIT):
    return pltpu.CompilerParams(dimension_semantics=sem, vmem_limit_bytes=vmem)


def _full(shape):
    n = len(shape)
    return pl.BlockSpec(shape, lambda *_: (0,) * n)


_VMEM_WHOLE = pl.BlockSpec(memory_space=pltpu.VMEM)
_ANY = pl.BlockSpec(memory_space=pl.ANY)


def _sigmoid(v):
    return 1.0 / (1.0 + jnp.exp(-v))


def _dot(a, b):
    return jnp.dot(a, b, preferred_element_type=F32)


def _dot_nt(a, b):
    return lax.dot_general(a, b, (((1,), (1,)), ((), ())), preferred_element_type=F32)


def _dot_tn(a, b):
    return lax.dot_general(a, b, (((0,), (0,)), ((), ())), preferred_element_type=F32)


def _split3(v):
    hi = v.astype(BF16)
    r = v - hi.astype(F32)
    mid = r.astype(BF16)
    lo = (r - mid.astype(F32)).astype(BF16)
    return hi, mid, lo


def _dot_exact_rhs01(v, m01):
    hi, mid, lo = _split3(v)
    return _dot(hi, m01) + _dot(mid, m01) + _dot(lo, m01)


def _split2(v):
    hi = v.astype(BF16)
    return hi, (v - hi.astype(F32)).astype(BF16)


def _dot_lhs01(m01, v):
    hi, lo = _split2(v)
    return _dot(m01, hi) + _dot(m01, lo)


def _dot_f32(a, b, dot=_dot):
    ah, am, al = _split3(a)
    bh, bm, bl = _split3(b)
    return (dot(ah, bh) + (dot(ah, bm) + dot(am, bh))
            + (dot(am, bm) + dot(ah, bl) + dot(al, bh)))


def _my_pos():
    return lax.axis_index("x"), lax.axis_index("y"), lax.axis_index("c")


class _Comm:
    def __init__(self, operands, out_shapes, sems, phases):
        self.operands, self.out_shapes, self.sems, self.phases = operands, out_shapes, sems, phases


def _gather_comm(blocks):
    n = len(blocks)

    def parts(ins, outs, sems):
        send_sems, recv_sems, local_sems = sems
        x, y, c = _my_pos()
        me, sibling = (x, y, c), (x, y, 1 - c)
        chips = [(1 - x, y), (x, 1 - y), (1 - x, 1 - y)]

        def slot(a, px, py, pc):
            return outs[a].at[4 * px + 2 * py + pc]

        def copy(a, k, block, to, src=None):
            return pltpu.make_async_remote_copy(
                src_ref=slot(a, *block) if src is None else src, dst_ref=slot(a, *block),
                send_sem=send_sems.at[a, k], recv_sem=recv_sems.at[a, k],
                device_id=to, device_id_type=MESH)

        mine = [pltpu.make_async_copy(ins[a], slot(a, *me), local_sems.at[a]) for a in range(n)]
        first = []
        for a in range(n):
            first.append(copy(a, 0, me, sibling, src=ins[a]))
            first += [copy(a, 1 + j, me, (*chip, c), src=ins[a]) for j, chip in enumerate(chips)]
        passed = [copy(a, 4 + j, (*chip, c), sibling) for j, chip in enumerate(chips) for a in range(n)]
        return c, me, sibling, chips, copy, mine, first, passed

    def start(ins, outs, sems):
        _, _, _, _, _, mine, first, _ = parts(ins, outs, sems)
        for cp in mine + first:
            cp.start()

    def forward(ins, outs, sems):
        c, me, _, chips, copy, _, _, passed = parts(ins, outs, sems)
        for j, chip in enumerate(chips):
            for a in range(n):
                copy(a, 1 + j, (*chip, c), me).wait_recv()
                passed[j * n + a].start()

    def finish(ins, outs, sems):
        c, me, sibling, chips, copy, mine, first, passed = parts(ins, outs, sems)
        for a in range(n):
            copy(a, 0, sibling, me).wait_recv()
            for j, chip in enumerate(chips):
                copy(a, 4 + j, (*chip, 1 - c), me).wait_recv()
        for cp in first + passed:
            cp.wait_send()
        for cp in mine:
            cp.wait()

    return _Comm(blocks, [jax.ShapeDtypeStruct((N_DEV,) + b.shape, b.dtype) for b in blocks],
                 [pltpu.SemaphoreType.DMA((n, 7)), pltpu.SemaphoreType.DMA((n, 7)), pltpu.SemaphoreType.DMA((n,))],
                 [start, forward, finish])


def _gather_comm_relayed(blocks):
    n = len(blocks)

    def parts(ins, outs, sems):
        send_sems, recv_sems, local_sems = sems
        x, y, c = _my_pos()
        me, sibling = (x, y, c), (x, y, 1 - c)
        x_nbr, y_nbr, diag = (1 - x, y, c), (x, 1 - y, c), (1 - x, 1 - y, c)

        def slot(a, dev, half=None):
            ref = outs[a].at[4 * dev[0] + 2 * dev[1] + dev[2]]
            if half is None:
                return ref
            rows = blocks[a].shape[0] // 2
            return ref.at[pl.ds(half * rows, rows)]

        def copy(a, k, block, to, half=None, src=None):
            return pltpu.make_async_remote_copy(
                src_ref=slot(a, block, half) if src is None else src, dst_ref=slot(a, block, half),
                send_sem=send_sems.at[a, k], recv_sem=recv_sems.at[a, k],
                device_id=to, device_id_type=MESH)

        mine = [pltpu.make_async_copy(ins[a], slot(a, me), local_sems.at[a]) for a in range(n)]
        return me, sibling, x_nbr, y_nbr, diag, copy, mine

    def start(ins, outs, sems):
        me, sibling, x_nbr, y_nbr, _, copy, mine = parts(ins, outs, sems)
        for cp in mine:
            cp.start()
        for a in range(n):
            for k, to in ((1, x_nbr), (2, y_nbr), (0, sibling)):
                copy(a, k, me, to, src=ins[a]).start()

    def forward(ins, outs, sems):
        me, sibling, x_nbr, y_nbr, _, copy, _ = parts(ins, outs, sems)
        for a in range(n):
            copy(a, 1, x_nbr, me).wait_recv()
            copy(a, 3, x_nbr, y_nbr, half=0).start()
            copy(a, 5, x_nbr, sibling).start()
        for a in range(n):
            copy(a, 2, y_nbr, me).wait_recv()
            copy(a, 4, y_nbr, x_nbr, half=1).start()
            copy(a, 6, y_nbr, sibling).start()

    def finish(ins, outs, sems):
        me, sibling, x_nbr, y_nbr, diag, copy, mine = parts(ins, outs, sems)
        sib = lambda dev: (dev[0], dev[1], sibling[2])
        for a in range(n):
            copy(a, 3, diag, me, half=0).wait_recv()
            copy(a, 4, diag, me, half=1).wait_recv()
            copy(a, 7, diag, sibling).start()
        for a in range(n):
            copy(a, 0, sibling, me).wait_recv()
            for k, dev in ((5, x_nbr), (6, y_nbr), (7, diag)):
                copy(a, k, sib(dev), me).wait_recv()
        for a in range(n):
            for k, block, to, half in ((0, me, sibling, None), (1, me, x_nbr, None), (2, me, y_nbr, None),
                                       (3, x_nbr, y_nbr, 0), (4, y_nbr, x_nbr, 1), (5, x_nbr, sibling, None),
                                       (6, y_nbr, sibling, None), (7, diag, sibling, None)):
                copy(a, k, block, to, half=half, src=ins[a] if block is me else None).wait_send()
        for cp in mine:
            cp.wait()

    return _Comm(blocks, [jax.ShapeDtypeStruct((N_DEV,) + b.shape, b.dtype) for b in blocks],
                 [pltpu.SemaphoreType.DMA((n, 8)), pltpu.SemaphoreType.DMA((n, 8)), pltpu.SemaphoreType.DMA((n,))],
                 [start, forward, finish])


_HBM = pl.BlockSpec(memory_space=pltpu.HBM)
_SEM = pl.BlockSpec(memory_space=pltpu.SEMAPHORE)
_SPLIT_COPY = pltpu.CompilerParams(has_side_effects=pltpu.SideEffectType.DATAFLOW_SIDE_EFFECTING)


def _chip_exchange_copies(src_ref, land_ref, sems):
    x, y, c = _my_pos()
    q_me = 2 * x + y
    pairs = []
    for j, (px, py) in enumerate([(1 - x, y), (x, 1 - y), (1 - x, 1 - y)]):
        q = 2 * px + py
        send = pltpu.make_async_remote_copy(
            src_ref=src_ref.at[q], dst_ref=land_ref.at[q_me], send_sem=sems[j], recv_sem=sems[3 + j],
            device_id=(px, py, c), device_id_type=MESH)
        recv = pltpu.make_async_remote_copy(
            src_ref=src_ref.at[q], dst_ref=land_ref.at[q], send_sem=sems[j], recv_sem=sems[3 + j],
            device_id=(x, y, c), device_id_type=MESH)
        pairs.append((send, recv))
    return pairs


def _chip_exchange_start(src, land):
    def body(src_ref, land_ref, *outs):
        sems, token = outs[:6], outs[8]
        for send, _ in _chip_exchange_copies(src_ref, land_ref, sems):
            send.start()
        token[...] = jnp.zeros_like(token)

    res = pl.pallas_call(
        body, name="scatter_in_start",
        out_shape=(pltpu.SemaphoreType.DMA(()),) * 6 + (
            pltpu.HBM(src.shape, src.dtype), pltpu.HBM(land.shape, land.dtype),
            jax.ShapeDtypeStruct((8, 128), F32)),
        in_specs=(_HBM, _HBM), out_specs=(_SEM,) * 6 + (_HBM, _HBM, pl.BlockSpec(memory_space=pltpu.VMEM)),
        input_output_aliases={0: 6, 1: 7}, compiler_params=_SPLIT_COPY,
    )(pltpu.with_memory_space_constraint(src, pltpu.HBM), pltpu.with_memory_space_constraint(land, pltpu.HBM))
    return res[:6], res[6], res[7], res[8]


def _chip_exchange_wait(sems, src_thru, land_thru, after):
    def body(src_ref, land_ref, *rest):
        for send, recv in _chip_exchange_copies(src_ref, land_ref, rest[:6]):
            send.wait_send()
            recv.wait_recv()

    return pl.pallas_call(
        body, name="scatter_in_wait",
        out_shape=(pltpu.HBM(src_thru.shape, src_thru.dtype), pltpu.HBM(land_thru.shape, land_thru.dtype)),
        in_specs=(_HBM, _HBM) + (_SEM,) * 6 + (_ANY,), out_specs=(_HBM, _HBM),
        input_output_aliases={0: 0, 1: 1}, compiler_params=_SPLIT_COPY,
    )(src_thru, land_thru, *sems, after)[1]


def _run_comm(comm, name, in_vmem=False, after=()):
    n_in, n_out, n_after = len(comm.operands), len(comm.out_shapes), len(after)

    def body(*refs):
        ins, refs = refs[:n_in], refs[n_in + n_after:]
        outs, sems = refs[:n_out], refs[n_out:]
        for phase in comm.phases:
            phase(ins, outs, sems)

    spec = _VMEM_WHOLE if in_vmem else _ANY
    return pl.pallas_call(
        body, name=name, out_shape=comm.out_shapes, in_specs=[spec] * n_in + [_ANY] * n_after,
        out_specs=[spec] * n_out, scratch_shapes=comm.sems,
    )(*comm.operands, *after)


def _carrier_call(body, comm, schedule, *, name, grid, in_specs, out_specs, out_shape, scratch_shapes, operands):
    n_in, n_out, n_scr = len(in_specs), len(out_specs), len(scratch_shapes)
    c_in, c_out = len(comm.operands), len(comm.out_shapes)

    def full_body(*refs):
        ins, refs = refs[:n_in], refs[n_in:]
        cins, refs = refs[:c_in], refs[c_in:]
        outs, refs = refs[:n_out], refs[n_out:]
        couts, refs = refs[:c_out], refs[c_out:]
        scr, csems = refs[:n_scr], refs[n_scr:]
        step = pl.program_id(0)

        def run(before):
            for (at, when_before), phase in zip(schedule, comm.phases):
                if when_before == before:
                    pl.when(step == at)(functools.partial(phase, cins, couts, csems))

        run(True)
        body(*ins, *outs, *scr)
        run(False)

    res = pl.pallas_call(
        full_body, name=name, grid=grid,
        in_specs=list(in_specs) + [_ANY] * c_in, out_specs=list(out_specs) + [_ANY] * c_out,
        out_shape=list(out_shape) + list(comm.out_shapes),
        scratch_shapes=list(scratch_shapes) + list(comm.sems),
        compiler_params=_cp(("arbitrary",)),
    )(*operands, *comm.operands)
    return res[:n_out], res[n_out:]


def _pcall(body, carried, *, name, grid, in_specs, out_specs, out_shape, scratch_shapes, operands):
    if carried is None:
        res = pl.pallas_call(body, name=name, grid=grid, in_specs=in_specs, out_specs=out_specs,
                             out_shape=out_shape, scratch_shapes=scratch_shapes,
                             compiler_params=_cp(("arbitrary",)))(*operands)
        return res, ()
    return _carrier_call(body, carried[0], carried[1], name=name, grid=grid, in_specs=in_specs,
                         out_specs=out_specs, out_shape=out_shape, scratch_shapes=scratch_shapes,
                         operands=operands)


def _all_gather(blocks, name, in_vmem, after=()):
    return _run_comm(_gather_comm(blocks), name, in_vmem, after)


N_CHIPS = 4


def _sibling_comm(contribs):
    n = len(contribs)

    def copies(ins, outs, sems):
        send_sems, recv_sems = sems
        x, y, c = _my_pos()
        return [pltpu.make_async_remote_copy(
            src_ref=ins[a].at[pl.ds(0, N_CHIPS), 1 - c], dst_ref=outs[a],
            send_sem=send_sems.at[a], recv_sem=recv_sems.at[a],
            device_id=(x, y, 1 - c), device_id_type=MESH) for a in range(n)]

    def start(ins, outs, sems):
        for cp in copies(ins, outs, sems):
            cp.start()

    def finish(ins, outs, sems):
        cps = copies(ins, outs, sems)
        for cp in cps:
            cp.wait_recv()
        for cp in cps:
            cp.wait_send()

    return _Comm(contribs, [jax.ShapeDtypeStruct((N_CHIPS,) + b.shape[2:], b.dtype) for b in contribs],
                 [pltpu.SemaphoreType.DMA((n,)), pltpu.SemaphoreType.DMA((n,))], [start, finish])


def _pair_sum(mine, theirs, name):
    _, _, rows, cols = mine.shape
    core = lax.axis_index("c").astype(jnp.int32).reshape(1)

    def body(c_ref, m_ref, t_ref, o_ref):
        o_ref[...] = (m_ref[...].astype(F32) + t_ref[...].astype(F32)).astype(BF16)

    return pl.pallas_call(
        body, name=name,
        grid_spec=pltpu.PrefetchScalarGridSpec(
            num_scalar_prefetch=1, grid=(N_CHIPS,),
            in_specs=[pl.BlockSpec((None, None, rows, cols), lambda q, c: (q, c[0], 0, 0)),
                      pl.BlockSpec((None, rows, cols), lambda q, c: (q, 0, 0))],
            out_specs=pl.BlockSpec((None, rows, cols), lambda q, c: (q, 0, 0))),
        out_shape=jax.ShapeDtypeStruct((N_CHIPS, rows, cols), BF16),
        compiler_params=_cp(("parallel",)),
    )(core, mine, theirs)


def _chip_comm(sums):
    n = len(sums)

    def parts(ins, outs, sems):
        send_sems, recv_sems, local_sems = sems
        x, y, c = _my_pos()
        q_me = 2 * x + y
        chips = [(1 - x, y), (x, 1 - y), (1 - x, 1 - y)]
        mine = [pltpu.make_async_copy(ins[a].at[q_me], outs[a].at[q_me], local_sems.at[a]) for a in range(n)]
        sends, recvs = [], []
        for j, (px, py) in enumerate(chips):
            for a in range(n):
                q = 2 * px + py
                sends.append(pltpu.make_async_remote_copy(
                    src_ref=ins[a].at[q], dst_ref=outs[a].at[q_me],
                    send_sem=send_sems.at[a, j], recv_sem=recv_sems.at[a, j],
                    device_id=(px, py, c), device_id_type=MESH))
                recvs.append(pltpu.make_async_remote_copy(
                    src_ref=ins[a].at[q], dst_ref=outs[a].at[q],
                    send_sem=send_sems.at[a, j], recv_sem=recv_sems.at[a, j],
                    device_id=(x, y, c), device_id_type=MESH))
        return mine, sends, recvs

    def start(ins, outs, sems):
        mine, sends, _ = parts(ins, outs, sems)
        for cp in mine + sends:
            cp.start()

    def finish(ins, outs, sems):
        mine, sends, recvs = parts(ins, outs, sems)
        for cp in recvs:
            cp.wait_recv()
        for cp in sends:
            cp.wait_send()
        for cp in mine:
            cp.wait()

    return _Comm(sums, [jax.ShapeDtypeStruct(b.shape, b.dtype) for b in sums],
                 [pltpu.SemaphoreType.DMA((n, 3)), pltpu.SemaphoreType.DMA((n, 3)), pltpu.SemaphoreType.DMA((n,))],
                 [start, finish])


def _mm_nt(a, bt, *, tm, tn, out_dtype, name, row_off=0, rows=None):
    rows = a.shape[0] if rows is None else rows
    n, k = bt.shape

    def body(a_ref, b_ref, o_ref):
        o_ref[...] = _dot_nt(a_ref[...], b_ref[...]).astype(out_dtype)

    return pl.pallas_call(
        body, name=name, grid=(rows // tm, n // tn),
        in_specs=[pl.BlockSpec((tm, k), lambda i, j: (i + row_off, 0)),
                  pl.BlockSpec((tn, k), lambda i, j: (j, 0))],
        out_specs=pl.BlockSpec((tm, tn), lambda i, j: (i, j)),
        out_shape=jax.ShapeDtypeStruct((rows, n), out_dtype),
        compiler_params=_cp(("parallel", "parallel")),
    )(a, bt)


def _mm_tn(a, b, *, tk, nk, tm, tn, out_dtype, name, a_off=0, b_off=0):
    m, n = a.shape[1], b.shape[1]

    def body(a_ref, b_ref, o_ref, acc):
        kk = pl.program_id(2)

        @pl.when(kk == 0)
        def _():
            acc[...] = jnp.zeros_like(acc)

        acc[...] += _dot_tn(a_ref[...], b_ref[...])

        @pl.when(kk == nk - 1)
        def _():
            o_ref[...] = acc[...].astype(out_dtype)

    return pl.pallas_call(
        body, name=name, grid=(m // tm, n // tn, nk),
        in_specs=[pl.BlockSpec((tk, tm), lambda i, j, kk: (kk + a_off, i)),
                  pl.BlockSpec((tk, tn), lambda i, j, kk: (kk + b_off, j))],
        out_specs=pl.BlockSpec((tm, tn), lambda i, j, kk: (i, j)),
        out_shape=jax.ShapeDtypeStruct((m, n), out_dtype),
        scratch_shapes=[pltpu.VMEM((tm, tn), F32)],
        compiler_params=_cp(("parallel", "parallel", "arbitrary")),
    )(a, b)


def _mm_tn_blocked(a, b, name, steps, carried=None):
    w = a.shape[1] // steps
    n = b.shape[1]

    def body(a_ref, b_ref, o_ref):
        o_ref[...] = _dot_tn(a_ref[...], b_ref[...]).astype(BF16)

    (out,), extra = _pcall(
        body, carried, name=name, grid=(steps,),
        in_specs=[pl.BlockSpec((S, w), lambda j: (0, j)), _full((S, n))],
        out_specs=[pl.BlockSpec((w, n), lambda j: (j, 0))],
        out_shape=[jax.ShapeDtypeStruct((a.shape[1], n), BF16)],
        scratch_shapes=[], operands=[a, b])
    return out, extra


def _prologue(cond, c_ctx, w_ada, b_cols, w_in_t, x, ctx, nw):
    rows_shape = jax.ShapeDtypeStruct((16, ADA_BLK), F32)
    big, g_cond, g_mod = _gather_comm_relayed([w_in_t]), _gather_comm([cond]), _gather_comm([rows_shape])

    def body(cond_ref, cctx_ref, wada_ref, b_ref, nw_ref, win_ref, x_ref, ctx_ref,
             g0_ref, cc_ref, mod_ref, gin_ref, h_ref, hl_ref, rows_ref, g1_ref, x_s, ctx_s, h_s, io_sems, *sems):
        s_big, s_cond, s_mod = sems[0:3], sems[3:6], sems[6:9]
        load_x = pltpu.make_async_copy(x_ref, x_s, io_sems.at[0])
        load_ctx = pltpu.make_async_copy(ctx_ref, ctx_s, io_sems.at[1])
        load_x.start()
        load_ctx.start()
        for phase in g_cond.phases:
            phase([cond_ref], [g0_ref], s_cond)
        big.phases[0]([win_ref], [gin_ref], s_big)
        cc_ref[...] = jnp.zeros_like(cc_ref)
        for j in range(N_DEV):
            cc_ref[j:j + 1, :] = g0_ref[j, 0:1, :]
        cc_ref[N_DEV:N_DEV + 1, :] = cctx_ref[...]
        cv = cc_ref[...]
        rows_ref[...] = _dot_f32(cv * _sigmoid(cv), wada_ref[...]) + b_ref[...]
        for phase in g_mod.phases:
            phase([rows_ref], [g1_ref], s_mod)
        big.phases[1]([win_ref], [gin_ref], s_big)
        x_pos, y_pos, c_pos = _my_pos()
        me = 4 * x_pos + 2 * y_pos + c_pos
        mine = jnp.concatenate([g1_ref[j, pl.ds(me, 1), :] for j in range(N_DEV)], axis=1)
        shared = jnp.concatenate([g1_ref[j, N_DEV:N_DEV + 1, :] for j in range(N_DEV)], axis=1)
        for k in range(6):
            mod_ref[k:k + 1, :] = mine[:, k * D:(k + 1) * D]
        mod_ref[6:7, :] = shared[:, 0:D]
        mod_ref[7:8, :] = shared[:, D:2 * D]
        load_ctx.wait()
        load_x.wait()
        h_s[pl.ds(0, L), :] = _rms_mod(ctx_s[...], nw_ref[...], mod_ref[6:7, :], mod_ref[7:8, :]).astype(BF16)

        def norm_tile(i, carry):
            r0 = pl.multiple_of(i * TM, TM)
            h_s[pl.ds(L + r0, TM), :] = _rms_mod(
                x_s[pl.ds(r0, TM), :], nw_ref[...], mod_ref[0:1, :], mod_ref[1:2, :]).astype(BF16)
            return carry

        lax.fori_loop(0, N_LAT_TILES, norm_tile, 0)
        stores = [pltpu.make_async_copy(h_s, h_ref, io_sems.at[2]),
                  pltpu.make_async_copy(h_s.at[pl.ds(L, S)], hl_ref, io_sems.at[3])]
        for cp in stores:
            cp.start()
        big.phases[2]([win_ref], [gin_ref], s_big)
        for cp in stores:
            cp.wait()

    return pl.pallas_call(
        body, name="prologue",
        in_specs=[_VMEM_WHOLE] * 5 + [_ANY] * 3, out_specs=[_VMEM_WHOLE] * 3 + [_ANY] * 3,
        out_shape=[g_cond.out_shapes[0], jax.ShapeDtypeStruct((16, D), F32), jax.ShapeDtypeStruct((8, D), F32),
                   big.out_shapes[0], jax.ShapeDtypeStruct((T, D), BF16), jax.ShapeDtypeStruct((S, D), BF16)],
        scratch_shapes=[pltpu.VMEM((16, ADA_BLK), F32), pltpu.VMEM((N_DEV, 16, ADA_BLK), F32),
                        pltpu.VMEM((S, D), F32), pltpu.VMEM((L, D), F32), pltpu.VMEM((T, D), BF16),
                        pltpu.SemaphoreType.DMA((4,))] + big.sems + g_cond.sems + g_mod.sems,
        compiler_params=_cp(),
    )(cond, c_ctx, w_ada, b_cols, nw, w_in_t, x, ctx)


def _ada_grads(cc, dm_cols, w_ada):
    def body(c_ref, dm_ref, w_ref, gw_ref, dsc_ref):
        cv = c_ref[...]
        sc = cv * _sigmoid(cv)
        dm = dm_ref[...]
        gw_ref[...] = _dot_f32(sc, dm, dot=_dot_tn)
        dsc_ref[...] = _dot_f32(dm, w_ref[...], dot=_dot_nt)

    return pl.pallas_call(
        body, name="ada_grads",
        in_specs=[_VMEM_WHOLE] * 3, out_specs=[_VMEM_WHOLE] * 2,
        out_shape=[jax.ShapeDtypeStruct((D, ADA_BLK), F32), jax.ShapeDtypeStruct((16, D), F32)],
        compiler_params=_cp(),
    )(cc, dm_cols, w_ada)


def _lat(i):
    return jnp.maximum(i - 1, 0)


def _rms_mod(xv, nw, sh, sc):
    rstd = lax.rsqrt(jnp.mean(xv * xv, axis=-1, keepdims=True) + EPS)
    return (xv * rstd * nw) * (1.0 + sc) + sh


def _rms_mod_bwd(xv, nw, sc, dh):
    rstd = lax.rsqrt(jnp.mean(xv * xv, axis=-1, keepdims=True) + EPS)
    xhat = xv * rstd
    dn = dh * (1.0 + sc)
    dxhat = dn * nw
    dx = rstd * (dxhat - xhat * jnp.mean(dxhat * xhat, axis=-1, keepdims=True))
    return (dx, jnp.sum(dh, axis=0, keepdims=True), jnp.sum(dh * (xhat * nw), axis=0, keepdims=True),
            jnp.sum(dn * xhat, axis=0, keepdims=True))


def _chunk_masks(reverse):
    row = lax.broadcasted_iota(jnp.int32, (TM, TM), 0)
    col = lax.broadcasted_iota(jnp.int32, (TM, TM), 1)
    same = (row // CHUNK) == (col // CHUNK)
    tri = same & ((col >= row) if reverse else (col <= row))
    return same, tri


def _chunk_order(i, reverse):
    if not reverse:
        return i
    return jnp.where(i < N_CTX_CHUNKS, N_CTX_CHUNKS - 1 - i, N_CHUNKS + N_CTX_CHUNKS - 1 - i)


def _decay_terms(z, lb, same01, tri01):
    f = lb + (1.0 - lb) * _sigmoid(z)
    g = jnp.log(f)
    g2 = jnp.concatenate(_split2(g), axis=1)
    b2 = _dot(tri01, g2)
    t2 = _dot(same01, g2)
    return f, 1.0 - f, b2[:, :HG_DIM] + b2[:, HG_DIM:], t2[:, :HG_DIM] + t2[:, HG_DIM:]


def _chunk_outer(a, b):
    n = TM // CHUNK
    return jnp.einsum('ncv,nck->nvk', a.reshape(n, CHUNK, HG_DIM), b.reshape(n, CHUNK, HG_DIM),
                      preferred_element_type=F32)


def _hgrn_fwd(p_a, lbl, carried=None):
    cpt = TM // CHUNK

    def body(p_ref, lbl_ref, o_ref, st_ref, qd_s, kd_s, u_s, v_s, ebt_s):
        masks = [_chunk_masks(d == 1) for d in (0, 1)]
        same01 = jnp.where(masks[0][0], 1.0, 0.0).astype(BF16)
        tri = [m[1] for m in masks]
        tri01 = [jnp.where(t, 1.0, 0.0).astype(BF16) for t in tri]
        lb = [_sigmoid(lbl_ref[d][0:1, :] - lbl_ref[d][1:2, :]) for d in (0, 1)]

        def prep(r, carry):
            r0 = pl.multiple_of(r * TM, TM)
            vb = p_ref[pl.ds(r0, TM), 2 * HG_DIM:3 * HG_DIM].astype(BF16)
            v_s[pl.ds(r0, TM), :] = vb
            for d in (0, 1):
                z = p_ref[pl.ds(r0, TM), d * HG_DIM:(d + 1) * HG_DIM]
                _, k, b, bt = _decay_terms(z, lb[d], same01, tri01[d])
                u_s[d, pl.ds(r * cpt, cpt)] = _chunk_outer(vb, (k * jnp.exp(bt - b)).astype(BF16))
                ebt_s[d, pl.ds(r0, TM), :] = jnp.exp(bt)

                @pl.when(r >= 1)
                def _():
                    rl = pl.multiple_of(r0 - L, TM)
                    qr = p_ref[pl.ds(r0, TM), 3 * HG_DIM:4 * HG_DIM]
                    q = qr * _sigmoid(qr) * HG_DIM ** -0.5
                    qd_s[d, pl.ds(rl, TM), :] = (q * jnp.exp(b)).astype(BF16)
                    kd_s[d, pl.ds(rl, TM), :] = (k * jnp.exp(-b)).astype(BF16)

            return carry

        lax.fori_loop(0, N_TILES, prep, 0)

        def scan(i, sts):
            new = []
            for d in (0, 1):
                nn = _chunk_order(i, d == 1)
                c0 = pl.multiple_of(nn * CHUNK, CHUNK)
                st_ref[d, nn] = sts[d].astype(BF16)
                new.append(sts[d] * ebt_s[d, pl.ds(c0, 1), :] + u_s[d, nn])
            return tuple(new)

        zero = jnp.zeros((HG_DIM, HG_DIM), F32)
        lax.fori_loop(0, N_CHUNKS, scan, (zero, zero))

        def outp(r, carry):
            r0 = pl.multiple_of(r * TM, TM)
            vb = v_s[pl.ds(r0 + L, TM), :]
            o = jnp.zeros((TM, HG_DIM), F32)
            for d in (0, 1):
                qd = qd_s[d, pl.ds(r0, TM), :]
                a = jnp.where(tri[d], _dot_nt(qd, kd_s[d, pl.ds(r0, TM), :]), 0.0)
                stb = st_ref[d, pl.ds(N_CTX_CHUNKS + r * cpt, cpt)]
                inter = jnp.einsum('nck,nvk->ncv', qd.reshape(cpt, CHUNK, HG_DIM), stb,
                                   preferred_element_type=F32)
                o = o + _dot(a.astype(BF16), vb) + inter.reshape(TM, HG_DIM)
            o_ref[pl.ds(r0, TM), :] = o
            return carry

        lax.fori_loop(0, N_LAT_TILES, outp, 0, unroll=2)

    return _pcall(
        body, carried, name="hgrn_fwd", grid=(HG_HEADS,),
        in_specs=[pl.BlockSpec((T, 4 * HG_DIM), lambda h: (0, h)),
                  pl.BlockSpec((2, 2, HG_DIM), lambda h: (0, 0, h))],
        out_specs=[pl.BlockSpec((S, HG_DIM), lambda h: (0, h)),
                   pl.BlockSpec((2, None, N_CHUNKS, HG_DIM, HG_DIM), lambda h: (0, h, 0, 0, 0))],
        out_shape=[jax.ShapeDtypeStruct((S, HGW), F32),
                   jax.ShapeDtypeStruct((2, HG_HEADS, N_CHUNKS, HG_DIM, HG_DIM), BF16)],
        scratch_shapes=[pltpu.VMEM((2, S, HG_DIM), BF16), pltpu.VMEM((2, S, HG_DIM), BF16),
                        pltpu.VMEM((2, N_CHUNKS, HG_DIM, HG_DIM), F32), pltpu.VMEM((T, HG_DIM), BF16),
                        pltpu.VMEM((2, T, HG_DIM), F32)],
        operands=[p_a, lbl])


def _hgrn_bwd(p_a, lbl, d_o, st, carried=None):
    cpt = TM // CHUNK

    def rows(r):
        return r * TM if isinstance(r, int) else pl.multiple_of(r * TM, TM)

    def body(p_ref, lbl_ref, do_ref, st_ref, dp_ref, dlb_ref, b_s, bt_s, dbt_s, qd_s, dst_s, w_s):
        masks = [_chunk_masks(d == 1) for d in (0, 1)]
        same01 = jnp.where(masks[0][0], 1.0, 0.0).astype(BF16)
        tri = [m[1] for m in masks]
        tri01 = [jnp.where(t, 1.0, 0.0).astype(BF16) for t in tri]
        later01 = [tri01[1], tri01[0]]
        lb = [_sigmoid(lbl_ref[d][0:1, :] - lbl_ref[d][1:2, :]) for d in (0, 1)]

        def prep_tile(r, latent):
            r0 = rows(r)
            for d in (0, 1):
                z = p_ref[pl.ds(r0, TM), d * HG_DIM:(d + 1) * HG_DIM]
                _, _, b, bt = _decay_terms(z, lb[d], same01, tri01[d])
                b_s[d, pl.ds(r0, TM), :] = b
                bt_s[d, pl.ds(r0, TM), :] = bt
                if latent:
                    rl = pl.multiple_of(r0 - L, TM)
                    qr = p_ref[pl.ds(r0, TM), 3 * HG_DIM:4 * HG_DIM]
                    qd = (qr * _sigmoid(qr) * HG_DIM ** -0.5 * jnp.exp(b)).astype(BF16)
                    qd_s[d, pl.ds(rl, TM), :] = qd
                    w_s[d, pl.ds(r * cpt, cpt)] = _chunk_outer(
                        do_ref[pl.ds(rl, TM), :].astype(BF16), qd).astype(BF16)

        prep_tile(0, False)
        w_s[:, pl.ds(0, N_CTX_CHUNKS)] = jnp.zeros((2, N_CTX_CHUNKS, HG_DIM, HG_DIM), BF16)

        def prep(r, carry):
            prep_tile(r, True)
            return carry

        lax.fori_loop(1, N_TILES, prep, 0, unroll=2)

        def rscan(j, dsts):
            i = N_CHUNKS - 1 - j
            new = []
            for d in (0, 1):
                nn = _chunk_order(i, d == 1)
                c0 = pl.multiple_of(nn * CHUNK, CHUNK)
                dst_s[d, nn] = dsts[d].astype(BF16)
                after = st_ref[d, _chunk_order(jnp.minimum(i + 1, N_CHUNKS - 1), d == 1)].astype(F32)
                dbt_s[d, pl.ds(c0, CHUNK), :] = jnp.broadcast_to(
                    jnp.sum(after * dsts[d], axis=0, keepdims=True), (CHUNK, HG_DIM))
                new.append(dsts[d] * jnp.exp(bt_s[d, pl.ds(c0, 1), :]) + w_s[d, nn].astype(F32))
            return tuple(new)

        zero = jnp.zeros((HG_DIM, HG_DIM), F32)
        lax.fori_loop(0, N_CHUNKS, rscan, (zero, zero))

        def grad_tile(r, latent):
            r0 = rows(r)
            vb = p_ref[pl.ds(r0, TM), 2 * HG_DIM:3 * HG_DIM].astype(BF16)
            dv = jnp.zeros((TM, HG_DIM), F32)
            dq = jnp.zeros((TM, HG_DIM), F32)
            dlbs = []
            if latent:
                rl = pl.multiple_of(r0 - L, TM)
                qr = p_ref[pl.ds(r0, TM), 3 * HG_DIM:4 * HG_DIM]
                sq = _sigmoid(qr)
                do = do_ref[pl.ds(rl, TM), :].astype(BF16)
                da_full = _dot_nt(do, vb)
            for d in (0, 1):
                z = p_ref[pl.ds(r0, TM), d * HG_DIM:(d + 1) * HG_DIM]
                sz = _sigmoid(z)
                f = lb[d] + (1.0 - lb[d]) * sz
                k = 1.0 - f
                b = b_s[d, pl.ds(r0, TM), :]
                e2 = jnp.exp(bt_s[d, pl.ds(r0, TM), :] - b)
                dstb = dst_s[d, pl.ds(r * cpt, cpt)]
                kd2 = k * e2
                dkd2 = jnp.einsum('ncv,nvk->nck', vb.reshape(cpt, CHUNK, HG_DIM), dstb,
                                  preferred_element_type=F32).reshape(TM, HG_DIM)
                dv = dv + jnp.einsum('nck,nvk->ncv', kd2.astype(BF16).reshape(cpt, CHUNK, HG_DIM), dstb,
                                     preferred_element_type=F32).reshape(TM, HG_DIM)
                dk = dkd2 * e2
                db = -(kd2 * dkd2)
                if latent:
                    eb = jnp.exp(b)
                    enb = jnp.exp(-b)
                    qdf = qr * sq * HG_DIM ** -0.5 * eb
                    kdf = k * enb
                    qd = qd_s[d, pl.ds(rl, TM), :]
                    kd = kdf.astype(BF16)
                    a = jnp.where(tri[d], _dot_nt(qd, kd), 0.0).astype(BF16)
                    da = jnp.where(tri[d], da_full, 0.0).astype(BF16)
                    stb = st_ref[d, pl.ds(r * cpt, cpt)]
                    dqd = _dot(da, kd) + jnp.einsum(
                        'ncv,nvk->nck', do.reshape(cpt, CHUNK, HG_DIM), stb,
                        preferred_element_type=F32).reshape(TM, HG_DIM)
                    dkd = _dot_tn(da, qd)
                    dv = dv + _dot_tn(a, do)
                    dk = dk + dkd * enb
                    db = db + qdf * dqd - kdf * dkd
                    dq = dq + dqd * eb
                dg = _dot_lhs01(later01[d], db) + dbt_s[d, pl.ds(r0, TM), :]
                df = dg / f - dk
                dp_ref[pl.ds(r0, TM), d * HG_DIM:(d + 1) * HG_DIM] = (
                    df * (1.0 - lb[d]) * sz * (1.0 - sz)).astype(BF16)
                dlbs.append(jnp.sum(df * (1.0 - sz), axis=0, keepdims=True))
            dp_ref[pl.ds(r0, TM), 2 * HG_DIM:3 * HG_DIM] = dv.astype(BF16)
            if latent:
                dq = dq * (HG_DIM ** -0.5) * (sq * (1.0 + qr * (1.0 - sq)))
            dp_ref[pl.ds(r0, TM), 3 * HG_DIM:4 * HG_DIM] = dq.astype(BF16)
            return dlbs

        dlb_ctx = grad_tile(0, False)

        def grads(r, acc):
            t = grad_tile(r, True)
            return (acc[0] + t[0], acc[1] + t[1])

        dlb = lax.fori_loop(1, N_TILES, grads, (dlb_ctx[0], dlb_ctx[1]))
        dlb_ref[0:1, :] = dlb[0]
        dlb_ref[1:2, :] = dlb[1]

    return _pcall(
        body, carried, name="hgrn_bwd", grid=(HG_HEADS,),
        in_specs=[pl.BlockSpec((T, 4 * HG_DIM), lambda h: (0, h)),
                  pl.BlockSpec((2, 2, HG_DIM), lambda h: (0, 0, h)),
                  pl.BlockSpec((S, HG_DIM), lambda h: (0, h)),
                  pl.BlockSpec((2, None, N_CHUNKS, HG_DIM, HG_DIM), lambda h: (0, h, 0, 0, 0))],
        out_specs=[pl.BlockSpec((T, 4 * HG_DIM), lambda h: (0, h)),
                   pl.BlockSpec((2, HG_DIM), lambda h: (0, h))],
        out_shape=[jax.ShapeDtypeStruct((T, WA), BF16), jax.ShapeDtypeStruct((2, HGW), F32)],
        scratch_shapes=[pltpu.VMEM((2, T, HG_DIM), F32), pltpu.VMEM((2, T, HG_DIM), F32),
                        pltpu.VMEM((2, T, HG_DIM), F32), pltpu.VMEM((2, S, HG_DIM), BF16),
                        pltpu.VMEM((2, N_CHUNKS, HG_DIM, HG_DIM), BF16),
                        pltpu.VMEM((2, N_CHUNKS, HG_DIM, HG_DIM), BF16)],
        operands=[p_a, lbl, d_o, st])


def _rope_tables():
    t = np.arange(S)
    inv = ROPE_THETA ** (-np.arange(0, 32, 2, dtype=np.float64) / 32)
    lane = np.arange(64)
    pos = np.where(lane[None, :] < 32, (t // GRID_W)[:, None], (t % GRID_W)[:, None]).astype(np.float64)
    ang = pos * inv[(lane % 32) % 16][None, :]
    sign = np.where((lane % 32) < 16, -1.0, 1.0)[None, :]
    cos = np.tile(np.cos(ang), (1, 2)).astype(np.float32)
    sin = np.tile(np.sin(ang) * sign, (1, 2)).astype(np.float32)
    return jnp.asarray(cos), jnp.asarray(sin)


def _rope_partner(v):
    lane = lax.broadcasted_iota(jnp.int32, (1, 128), 1)
    first = (lane % 32) < 16
    slabs = []
    for j in range(v.shape[1] // 128):
        s = v[:, 128 * j:128 * (j + 1)]
        slabs.append(jnp.where(first, pltpu.roll(s, 112, 1), pltpu.roll(s, 16, 1)))
    return slabs[0] if len(slabs) == 1 else jnp.concatenate(slabs, axis=1)


def _group_ones(width, group):
    r = lax.broadcasted_iota(jnp.int32, (width, width), 0)
    c = lax.broadcasted_iota(jnp.int32, (width, width), 1)
    return jnp.where((r // group) == (c // group), 1.0, 0.0).astype(BF16)


def _group_mean(v, ones01, group):
    hi = v.astype(BF16)
    lo = (v - hi.astype(F32)).astype(BF16)
    return (_dot(hi, ones01) + _dot(lo, ones01)) * (1.0 / group)


def _rep_matrix():
    r = lax.broadcasted_iota(jnp.int32, (KVW, ATW), 0)
    c = lax.broadcasted_iota(jnp.int32, (KVW, ATW), 1)
    return jnp.where(r == HEAD_DIM * (c // 256) + c % HEAD_DIM, 1.0, 0.0).astype(BF16)


def _tile_lanes(v, reps):
    return jnp.concatenate([v] * reps, axis=1)


def _prep_fwd(p_b, o, cos, sin, hnw, qnw, knw):
    def body(p_ref, o_ref, cos_ref, sin_ref, hnw_ref, qnw_ref, knw_ref, y_ref, q_ref, k_ref, v_ref):
        i = pl.program_id(0)
        rep = _rep_matrix()
        ones_k = _group_ones(KVW, HEAD_DIM)
        kr = p_ref[:, 1024:1152]
        krstd = lax.rsqrt(_group_mean(kr * kr, ones_k, HEAD_DIM) + EPS)
        kn = kr * krstd * knw_ref[...]
        v_ref[...] = _dot(p_ref[:, 1152:1280].astype(BF16), rep).astype(BF16)

        @pl.when(i == 0)
        def _():
            k_ref[...] = _dot(kn.astype(BF16), rep).astype(BF16)

        @pl.when(i > 0)
        def _():
            cs, sn = cos_ref[...], sin_ref[...]
            kro = kn * cs + _rope_partner(kn) * sn
            k_ref[...] = _dot(kro.astype(BF16), rep).astype(BF16)
            qr = p_ref[:, 512:1024]
            qrstd = lax.rsqrt(_group_mean(qr * qr, _group_ones(ATW, HEAD_DIM), HEAD_DIM) + EPS)
            qn = qr * qrstd * qnw_ref[...]
            qro = qn * _tile_lanes(cs, 4) + _rope_partner(qn) * _tile_lanes(sn, 4)
            q_ref[...] = (qro * HEAD_DIM ** -0.5).astype(BF16)
            ys = []
            for h in range(HG_HEADS):
                oh = o_ref[:, HG_DIM * h:HG_DIM * (h + 1)]
                gh = p_ref[:, HG_DIM * h:HG_DIM * (h + 1)]
                rstd = lax.rsqrt(jnp.mean(oh * oh, axis=-1, keepdims=True) + EPS)
                ys.append(oh * rstd * hnw_ref[...] * (gh * _sigmoid(gh)))
            y_ref[...] = jnp.concatenate(ys, axis=1).astype(BF16)

    return pl.pallas_call(
        body, name="prep_fwd", grid=(N_TILES,),
        in_specs=[pl.BlockSpec((TM, WB), lambda i: (i, 0)),
                  pl.BlockSpec((TM, HGW), lambda i: (_lat(i), 0)),
                  pl.BlockSpec((TM, 128), lambda i: (_lat(i), 0)),
                  pl.BlockSpec((TM, 128), lambda i: (_lat(i), 0)),
                  _full((1, HG_DIM)), _full((1, ATW)), _full((1, KVW))],
        out_specs=[pl.BlockSpec((TM, HGW), lambda i: (_lat(i), 0)),
                   pl.BlockSpec((TM, ATW), lambda i: (_lat(i), 0)),
                   pl.BlockSpec((TM, ATW), lambda i: (i, 0)),
                   pl.BlockSpec((TM, ATW), lambda i: (i, 0))],
        out_shape=[jax.ShapeDtypeStruct((S, HGW), BF16), jax.ShapeDtypeStruct((S, ATW), BF16),
                   jax.ShapeDtypeStruct((T, ATW), BF16), jax.ShapeDtypeStruct((T, ATW), BF16)],
        compiler_params=_cp(("arbitrary",)),
    )(p_b, o, cos, sin, hnw, qnw, knw)


def _prep_bwd(p_b, o, cos, sin, hnw, qnw, knw, dy_hg, dq, dk_rep, dv_rep, carried=None):
    def body(p_ref, o_ref, cos_ref, sin_ref, hnw_ref, qnw_ref, knw_ref, dy_ref, dq_ref, dk_ref, dv_ref,
             dp_ref, do_ref, acc_ref):
        i = pl.program_id(0)

        @pl.when(i == 0)
        def _():
            acc_ref[...] = jnp.zeros_like(acc_ref)

        rep = _rep_matrix()
        ones_k = _group_ones(KVW, HEAD_DIM)

        def fold(v):
            hi = v.astype(BF16)
            lo = (v - hi.astype(F32)).astype(BF16)
            return _dot_nt(hi, rep) + _dot_nt(lo, rep)

        kr = p_ref[:, 1024:1152]
        krstd = lax.rsqrt(_group_mean(kr * kr, ones_k, HEAD_DIM) + EPS)
        khat = kr * krstd
        kw = knw_ref[...]
        dkro = fold(dk_ref[...])
        dv = fold(dv_ref[...])

        def k_back(dkn):
            dkhat = dkn * kw
            dkr = krstd * (dkhat - khat * _group_mean(dkhat * khat, ones_k, HEAD_DIM))
            acc_ref[2:3, 0:KVW] += jnp.sum(dkn * khat, axis=0, keepdims=True)
            dp_ref[:, 1024:1152] = dkr.astype(BF16)
            dp_ref[:, 1152:1280] = dv.astype(BF16)

        @pl.when(i == 0)
        def _():
            k_back(dkro)
            dp_ref[:, 0:1024] = jnp.zeros((TM, 1024), BF16)

        @pl.when(i > 0)
        def _():
            cs, sn = cos_ref[...], sin_ref[...]
            k_back(dkro * cs + _rope_partner(dkro * sn))
            ones_q = _group_ones(ATW, HEAD_DIM)
            qr = p_ref[:, 512:1024]
            qrstd = lax.rsqrt(_group_mean(qr * qr, ones_q, HEAD_DIM) + EPS)
            qhat = qr * qrstd
            dqro = dq_ref[...] * HEAD_DIM ** -0.5
            dqn = dqro * _tile_lanes(cs, 4) + _rope_partner(dqro * _tile_lanes(sn, 4))
            dqhat = dqn * qnw_ref[...]
            dqr = qrstd * (dqhat - qhat * _group_mean(dqhat * qhat, ones_q, HEAD_DIM))
            acc_ref[1:2, :] += jnp.sum(dqn * qhat, axis=0, keepdims=True)
            dp_ref[:, 512:1024] = dqr.astype(BF16)
            dws = jnp.zeros((1, HG_DIM), F32)
            for h in range(HG_HEADS):
                sl = slice(HG_DIM * h, HG_DIM * (h + 1))
                oh, gh, dy = o_ref[:, sl], p_ref[:, sl], dy_ref[:, sl]
                rstd = lax.rsqrt(jnp.mean(oh * oh, axis=-1, keepdims=True) + EPS)
                ohat = oh * rstd
                sg = _sigmoid(gh)
                dp_ref[:, sl] = (dy * (ohat * hnw_ref[...]) * (sg * (1.0 + gh * (1.0 - sg)))).astype(BF16)
                dn = dy * (gh * sg)
                dws = dws + jnp.sum(dn * ohat, axis=0, keepdims=True)
                dohat = dn * hnw_ref[...]
                do_ref[:, sl] = rstd * (dohat - ohat * jnp.mean(dohat * ohat, axis=-1, keepdims=True))
            acc_ref[0:1, 0:HG_DIM] += dws

    return _pcall(
        body, carried, name="prep_bwd", grid=(N_TILES,),
        in_specs=[pl.BlockSpec((TM, WB), lambda i: (i, 0)),
                  pl.BlockSpec((TM, HGW), lambda i: (_lat(i), 0)),
                  pl.BlockSpec((TM, 128), lambda i: (_lat(i), 0)),
                  pl.BlockSpec((TM, 128), lambda i: (_lat(i), 0)),
                  _full((1, HG_DIM)), _full((1, ATW)), _full((1, KVW)),
                  pl.BlockSpec((TM, HGW), lambda i: (_lat(i), 0)),
                  pl.BlockSpec((TM, ATW), lambda i: (_lat(i), 0)),
                  pl.BlockSpec((TM, ATW), lambda i: (i, 0)),
                  pl.BlockSpec((TM, ATW), lambda i: (i, 0))],
        out_specs=[pl.BlockSpec((TM, WB), lambda i: (i, 0)),
                   pl.BlockSpec((TM, HGW), lambda i: (_lat(i), 0)),
                   _full((8, ATW))],
        out_shape=[jax.ShapeDtypeStruct((T, WB), BF16), jax.ShapeDtypeStruct((S, HGW), F32),
                   jax.ShapeDtypeStruct((8, ATW), F32)],
        scratch_shapes=[], operands=[p_b, o, cos, sin, hnw, qnw, knw, dy_hg, dq, dk_rep, dv_rep])


NEG = -1e30
_CTX_BLOCKS = L // BLOCK


def _attn_window_specs():
    prev = pl.BlockSpec((BLOCK, ATW), lambda i: (jnp.maximum(i - 1, 0) + _CTX_BLOCKS, 0))
    own = pl.BlockSpec((BLOCK, ATW), lambda i: (i + _CTX_BLOCKS, 0))
    nxt = pl.BlockSpec((BLOCK, ATW), lambda i: (jnp.minimum(i + 1, N_BLOCKS - 1) + _CTX_BLOCKS, 0))
    return [prev, own, nxt, _full((L, ATW))]


def _attn_valid(i, heads, context):
    n_keys = 3 * BLOCK + (L if context else 0)
    qi = lax.broadcasted_iota(jnp.int32, (heads * BLOCK, n_keys), 0) % BLOCK
    kj = lax.broadcasted_iota(jnp.int32, (heads * BLOCK, n_keys), 1)
    window = ((jnp.abs(kj - BLOCK - qi) <= BLOCK) & ((kj >= BLOCK) | (i > 0))
              & ((kj < 2 * BLOCK) | (i < N_BLOCKS - 1)))
    return window | (kj >= 3 * BLOCK)


def _stack_heads(qg):
    lane = lax.broadcasted_iota(jnp.int32, (1, 256), 1) // HEAD_DIM
    return jnp.concatenate([jnp.where(lane == g, qg, jnp.zeros_like(qg)) for g in range(4)], axis=0)


def _unstack_heads(v4):
    lane = lax.broadcasted_iota(jnp.int32, (1, 256), 1) // HEAD_DIM
    out = jnp.where(lane == 0, v4[0:BLOCK], 0.0)
    for g in range(1, 4):
        out = out + jnp.where(lane == g, v4[g * BLOCK:(g + 1) * BLOCK], 0.0)
    return out


def _sink_rows(sink_ref, hk):
    return jnp.concatenate(
        [jnp.broadcast_to(sink_ref[0:1, 4 * hk + g:4 * hk + g + 1], (BLOCK, 1)) for g in range(4)], axis=0)


def _attn_fwd(q, k_rep, v_rep, sinks, carried=None):
    def body(q_ref, kp, ko, kn, kc, vp, vo, vn, vc, sink_ref, y_ref, lse_ref):
        i = pl.program_id(0)
        valid = _attn_valid(i, 1, True)
        lane8 = lax.broadcasted_iota(jnp.int32, (1, ATT_HEADS), 1)
        head_of_lane = lax.broadcasted_iota(jnp.int32, (1, 256), 1) // HEAD_DIM
        lse_out = jnp.zeros((BLOCK, ATT_HEADS), F32)
        for hk in range(KV_HEADS):
            sl = slice(256 * hk, 256 * (hk + 1))
            qg = q_ref[:, sl]
            keys = jnp.concatenate([kp[:, sl], ko[:, sl], kn[:, sl], kc[:, sl]], axis=0)
            vals = jnp.concatenate([vp[:, sl], vo[:, sl], vn[:, sl], vc[:, sl]], axis=0)
            yg = jnp.zeros((BLOCK, 256), F32)
            for g in range(4):
                q1 = jnp.where(head_of_lane == g, qg, jnp.zeros_like(qg))
                s = jnp.where(valid, _dot_nt(q1, keys), NEG)
                sink = sink_ref[0:1, 4 * hk + g:4 * hk + g + 1]
                m = jnp.maximum(jnp.max(s, axis=1, keepdims=True), sink)
                p = jnp.exp(s - m)
                den = jnp.sum(p, axis=1, keepdims=True) + jnp.exp(sink - m)
                o1 = _dot(p.astype(BF16), vals) * (1.0 / den)
                yg = yg + jnp.where(head_of_lane == g, o1, 0.0)
                lse_out = lse_out + jnp.where(lane8 == 4 * hk + g, m + jnp.log(den), 0.0)
            y_ref[:, sl] = yg.astype(BF16)
        lse_ref[...] = lse_out

    return _pcall(
        body, carried, name="attn_fwd", grid=(N_BLOCKS,),
        in_specs=[pl.BlockSpec((BLOCK, ATW), lambda i: (i, 0))] + _attn_window_specs()
        + _attn_window_specs() + [_full((1, ATT_HEADS))],
        out_specs=[pl.BlockSpec((BLOCK, ATW), lambda i: (i, 0)),
                   pl.BlockSpec((BLOCK, ATT_HEADS), lambda i: (i, 0))],
        out_shape=[jax.ShapeDtypeStruct((S, ATW), BF16), jax.ShapeDtypeStruct((S, ATT_HEADS), F32)],
        scratch_shapes=[],
        operands=[q, k_rep, k_rep, k_rep, k_rep, v_rep, v_rep, v_rep, v_rep, sinks])


def _attn_bwd(q, k_rep, v_rep, sinks, y_at, lse, dy, carried=None):
    def body(q_ref, kp, ko, kn, kc, vp, vo, vn, vc, sink_ref, y_ref, lse_ref, dy_ref,
             dq_ref, dk_ref, dv_ref, dsink_ref, dk_acc, dv_acc):
        i = pl.program_id(0)

        @pl.when(i == 0)
        def _():
            dk_acc[...] = jnp.zeros_like(dk_acc)
            dv_acc[...] = jnp.zeros_like(dv_acc)
            dk_ref[pl.ds(0, L), :] = jnp.zeros((L, ATW), F32)
            dv_ref[pl.ds(0, L), :] = jnp.zeros((L, ATW), F32)
            dsink_ref[...] = jnp.zeros_like(dsink_ref)

        valid = _attn_valid(i, 4, False)
        lane8 = lax.broadcasted_iota(jnp.int32, (1, ATT_HEADS), 1)
        w0 = pl.multiple_of(i * BLOCK, BLOCK)
        dsink = jnp.zeros((1, ATT_HEADS), F32)
        for hk in range(KV_HEADS):
            sl = slice(256 * hk, 256 * (hk + 1))
            q4 = _stack_heads(q_ref[:, sl])
            do4f = _stack_heads(dy_ref[:, sl])
            o4 = _stack_heads(y_ref[:, sl]).astype(F32)
            do4 = do4f.astype(BF16)
            kl = jnp.concatenate([kp[:, sl], ko[:, sl], kn[:, sl]], axis=0)
            vl = jnp.concatenate([vp[:, sl], vo[:, sl], vn[:, sl]], axis=0)
            lse4 = jnp.concatenate(
                [jnp.sum(jnp.where(lane8 == 4 * hk + g, lse_ref[...], 0.0), axis=1, keepdims=True)
                 for g in range(4)], axis=0)
            p_loc = jnp.where(valid, jnp.exp(_dot_nt(q4, kl) - lse4), 0.0)
            p_ctx = jnp.exp(_dot_nt(q4, kc[:, sl]) - lse4)
            delta = jnp.sum(do4f * o4, axis=1, keepdims=True)
            ds_loc = (p_loc * (_dot_nt(do4, vl) - delta)).astype(BF16)
            ds_ctx = (p_ctx * (_dot_nt(do4, vc[:, sl]) - delta)).astype(BF16)
            dq_ref[:, sl] = _unstack_heads(_dot(ds_loc, kl) + _dot(ds_ctx, kc[:, sl]))
            dk_acc[pl.ds(w0, 3 * BLOCK), sl] += _dot_tn(ds_loc, q4)
            dv_acc[pl.ds(w0, 3 * BLOCK), sl] += _dot_tn(p_loc.astype(BF16), do4)
            dk_ref[pl.ds(0, L), sl] += _dot_tn(ds_ctx, q4)
            dv_ref[pl.ds(0, L), sl] += _dot_tn(p_ctx.astype(BF16), do4)
            p_sink = jnp.exp(_sink_rows(sink_ref, hk) - lse4)
            for g in range(4):
                rows = slice(g * BLOCK, (g + 1) * BLOCK)
                dsink = dsink + jnp.where(lane8 == 4 * hk + g,
                                          -jnp.sum(p_sink[rows] * delta[rows], axis=0, keepdims=True), 0.0)
        dsink_ref[...] += dsink

        @pl.when(i == N_BLOCKS - 1)
        def _():
            dk_ref[pl.ds(L, S), :] = dk_acc[pl.ds(BLOCK, S), :]
            dv_ref[pl.ds(L, S), :] = dv_acc[pl.ds(BLOCK, S), :]

    row_q = pl.BlockSpec((BLOCK, ATW), lambda i: (i, 0))
    return _pcall(
        body, carried, name="attn_bwd", grid=(N_BLOCKS,),
        in_specs=[row_q] + _attn_window_specs() + _attn_window_specs()
        + [_full((1, ATT_HEADS)), row_q, pl.BlockSpec((BLOCK, ATT_HEADS), lambda i: (i, 0)), row_q],
        out_specs=[row_q, _full((T, ATW)), _full((T, ATW)), _full((1, ATT_HEADS))],
        out_shape=[jax.ShapeDtypeStruct((S, ATW), F32), jax.ShapeDtypeStruct((T, ATW), F32),
                   jax.ShapeDtypeStruct((T, ATW), F32), jax.ShapeDtypeStruct((1, ATT_HEADS), F32)],
        scratch_shapes=[pltpu.VMEM((S + 2 * BLOCK, ATW), F32), pltpu.VMEM((S + 2 * BLOCK, ATW), F32)],
        operands=[q, k_rep, k_rep, k_rep, k_rep, v_rep, v_rep, v_rep, v_rep, sinks, y_at, lse, dy])


def _merge_fwd(y_hg, y_at, p_c, x, w_bh, w_ba, w_out, g1, nfw, sh2, sc2, carried=None):
    def body(yh_ref, ya_ref, g_ref, x_ref, wbh_ref, wba_ref, wo_ref, g1_ref, nfw_ref, sh_ref, sc_ref,
             mx_ref, r_ref, x1_ref, h2_ref):
        a = _dot_nt(yh_ref[...], wbh_ref[...])
        b = _dot_nt(ya_ref[...], wba_ref[...])
        mixed = (_sigmoid(g_ref[:, :D]) * a + _sigmoid(g_ref[:, D:]) * b).astype(BF16)
        r = _dot(mixed, wo_ref[...])
        x1 = x_ref[...] + g1_ref[...] * r
        mx_ref[...] = mixed
        r_ref[...] = r
        x1_ref[...] = x1
        h2_ref[...] = _rms_mod(x1, nfw_ref[...], sh_ref[...], sc_ref[...]).astype(BF16)

    row = lambda w: pl.BlockSpec((TM, w), lambda i: (i, 0))
    vec = _full((1, D))
    return _pcall(
        body, carried, name="merge_fwd", grid=(N_LAT_TILES,),
        in_specs=[row(HGW), row(ATW), row(WC), row(D), _VMEM_WHOLE, _VMEM_WHOLE, _VMEM_WHOLE,
                  vec, vec, vec, vec],
        out_specs=[row(D)] * 4,
        out_shape=[jax.ShapeDtypeStruct((S, D), dt) for dt in (BF16, F32, F32, BF16)],
        scratch_shapes=[], operands=[y_hg, y_at, p_c, x, w_bh, w_ba, w_out, g1, nfw, sh2, sc2])


def _merge_bwd(dx1, r, y_hg, y_at, p_c, w_bh, w_ba, w_out, g1, carried=None):
    def body(dx_ref, r_ref, yh_ref, ya_ref, g_ref, wbh_ref, wba_ref, wo_ref, g1_ref,
             dr_ref, da_ref, db_ref, dg_ref, dyh_ref, dya_ref, acc_ref):
        @pl.when(pl.program_id(0) == 0)
        def _():
            acc_ref[...] = jnp.zeros_like(acc_ref)

        dx1v = dx_ref[...]
        acc_ref[0:1, :] += jnp.sum(dx1v * r_ref[...], axis=0, keepdims=True)
        dr = (g1_ref[...] * dx1v).astype(BF16)
        dr_ref[...] = dr
        dmix = _dot_nt(dr, wo_ref[...])
        sh, sa = _sigmoid(g_ref[:, :D]), _sigmoid(g_ref[:, D:])
        da = (dmix * sh).astype(BF16)
        db = (dmix * sa).astype(BF16)
        da_ref[...] = da
        db_ref[...] = db
        dg_ref[:, :D] = (dmix * _dot_nt(yh_ref[...], wbh_ref[...]) * sh * (1.0 - sh)).astype(BF16)
        dg_ref[:, D:] = (dmix * _dot_nt(ya_ref[...], wba_ref[...]) * sa * (1.0 - sa)).astype(BF16)
        dyh_ref[...] = _dot(da, wbh_ref[...])
        dya_ref[...] = _dot(db, wba_ref[...])

    row = lambda w: pl.BlockSpec((TM, w), lambda i: (i, 0))
    return _pcall(
        body, carried, name="merge_bwd", grid=(N_LAT_TILES,),
        in_specs=[row(D), row(D), row(HGW), row(ATW), row(WC), _VMEM_WHOLE, _VMEM_WHOLE, _VMEM_WHOLE,
                  _full((1, D))],
        out_specs=[row(D), row(D), row(D), row(WC), row(HGW), row(ATW), _full((8, D))],
        out_shape=[jax.ShapeDtypeStruct((S, D), BF16), jax.ShapeDtypeStruct((S, D), BF16),
                   jax.ShapeDtypeStruct((S, D), BF16), jax.ShapeDtypeStruct((S, WC), BF16),
                   jax.ShapeDtypeStruct((S, HGW), F32), jax.ShapeDtypeStruct((S, ATW), F32),
                   jax.ShapeDtypeStruct((8, D), F32)],
        scratch_shapes=[], operands=[dx1, r, y_hg, y_at, p_c, w_bh, w_ba, w_out, g1])


def _ffn_fused(x1, h2, tgt, w_gate, w_up, w_down, g2, nfw, sc2):
    def body(x1_ref, h2_ref, t_ref, wg_ref, wu_ref, wd_ref, g2_ref, nfw_ref, sc_ref,
             act_ref, dgt_ref, dup_ref, df_ref, dx_ref, acc_ref, gs, us):
        @pl.when(pl.program_id(0) == 0)
        def _():
            acc_ref[...] = jnp.zeros_like(acc_ref)

        h2 = h2_ref[...]
        whole = lambda w_ref: w_ref[...].reshape(D_FF, D)
        tile = lambda j: slice(j * FF_TILE, (j + 1) * FF_TILE)
        for j in range(N_FF_TILES):
            g = _dot_nt(h2, wg_ref[j])
            u = _dot_nt(h2, wu_ref[j])
            gs[j] = g
            us[j] = u
            act_ref[:, tile(j)] = (g * _sigmoid(g) * u).astype(BF16)
        f = _dot(act_ref[...], whole(wd_ref))
        x1v = x1_ref[...]
        g2 = g2_ref[...]
        diff = x1v + g2 * f - t_ref[...]
        dy = diff * (1.0 / D)
        df = (g2 * dy).astype(BF16)
        df_ref[...] = df
        dact_all = _dot_nt(df, whole(wd_ref))
        for j in range(N_FF_TILES):
            g, u = gs[j], us[j]
            sg = _sigmoid(g)
            dact = dact_all[:, tile(j)]
            dgt_ref[:, tile(j)] = (dact * u * (sg * (1.0 + g * (1.0 - sg)))).astype(BF16)
            dup_ref[:, tile(j)] = (dact * (g * sg)).astype(BF16)
        dh2 = _dot(dgt_ref[...], whole(wg_ref)) + _dot(dup_ref[...], whole(wu_ref))
        dx, dsh, dsc, dnw = _rms_mod_bwd(x1v, nfw_ref[...], sc_ref[...], dh2)
        dx_ref[...] = dy + dx
        acc_ref[0:1, :] += dsh
        acc_ref[1:2, :] += dsc
        acc_ref[2:3, :] += dnw
        acc_ref[3:4, :] += jnp.sum(dy * f, axis=0, keepdims=True)
        acc_ref[4:5, :] += 0.5 * jnp.sum(jnp.sum(diff * diff, axis=1, keepdims=True), axis=0,
                                         keepdims=True) * (1.0 / D)

    row = lambda dt_w: pl.BlockSpec((TM, dt_w), lambda i: (i, 0))
    blk = row(D_FF)
    vec = _full((1, D))
    return pl.pallas_call(
        body, name="ffn_fused", grid=(N_LAT_TILES,),
        in_specs=[row(D), row(D), row(D), _VMEM_WHOLE, _VMEM_WHOLE, _VMEM_WHOLE, vec, vec, vec],
        out_specs=[blk, blk, blk, row(D), row(D), _full((8, D))],
        out_shape=[jax.ShapeDtypeStruct((S, D_FF), BF16)] * 3
        + [jax.ShapeDtypeStruct((S, D), BF16), jax.ShapeDtypeStruct((S, D), F32),
           jax.ShapeDtypeStruct((8, D), F32)],
        scratch_shapes=[pltpu.VMEM((N_FF_TILES, TM, FF_TILE), F32), pltpu.VMEM((N_FF_TILES, TM, FF_TILE), F32)],
        compiler_params=_cp(("arbitrary",)),
    )(x1, h2, tgt, w_gate, w_up, w_down, g2, nfw, sc2)


def _proj_bc(h_all, w_b, w_c, carried=None):
    def body(h_ref, wb_ref, wc_ref, pb_ref, pc_ref):
        h = h_ref[...]
        pb_ref[...] = _dot_nt(h, wb_ref[...])

        @pl.when(pl.program_id(0) > 0)
        def _():
            pc_ref[...] = _dot_nt(h, wc_ref[...])

    return _pcall(
        body, carried, name="proj_bc", grid=(N_TILES,),
        in_specs=[pl.BlockSpec((TM, D), lambda i: (i, 0)), _VMEM_WHOLE, _VMEM_WHOLE],
        out_specs=[pl.BlockSpec((TM, WB), lambda i: (i, 0)), pl.BlockSpec((TM, WC), lambda i: (_lat(i), 0))],
        out_shape=[jax.ShapeDtypeStruct((T, WB), F32), jax.ShapeDtypeStruct((S, WC), F32)],
        scratch_shapes=[], operands=[h_all, w_b, w_c])


def _input_bwd(dp_a, dp_b, dp_c, w_a, w_b, w_c, ctx, x, dx1, nw, sh, sc, carried=None):
    def body(da_ref, db_ref, dc_ref, wa_ref, wb_ref, wc_ref, ctx_ref, x_ref, dx1_ref, nw_ref, sh_ref,
             sc_ref, gx_ref, acc_ref):
        i = pl.program_id(0)

        @pl.when(i == 0)
        def _():
            acc_ref[...] = jnp.zeros_like(acc_ref)

        dh = _dot(da_ref[...], wa_ref[...]) + _dot(db_ref[...], wb_ref[...])

        @pl.when(i == 0)
        def _():
            _, dsh, dsc, dnw = _rms_mod_bwd(ctx_ref[...], nw_ref[...], sc_ref[0:1, :], dh)
            acc_ref[3:4, :] += dsh
            acc_ref[4:5, :] += dsc
            acc_ref[2:3, :] += dnw

        @pl.when(i > 0)
        def _():
            dhl = dh + _dot(dc_ref[...], wc_ref[...])
            dx, dsh, dsc, dnw = _rms_mod_bwd(x_ref[...], nw_ref[...], sc_ref[1:2, :], dhl)
            gx_ref[...] = dx1_ref[...] + dx
            acc_ref[0:1, :] += dsh
            acc_ref[1:2, :] += dsc
            acc_ref[2:3, :] += dnw

    lat = lambda w: pl.BlockSpec((TM, w), lambda i: (_lat(i), 0))
    return _pcall(
        body, carried, name="input_bwd", grid=(N_TILES,),
        in_specs=[pl.BlockSpec((TM, WA), lambda i: (i, 0)), pl.BlockSpec((TM, WB), lambda i: (i, 0)),
                  lat(WC), _VMEM_WHOLE, _VMEM_WHOLE, _VMEM_WHOLE, _full((TM, D)), lat(D), lat(D),
                  _full((1, D)), _full((2, D)), _full((2, D))],
        out_specs=[lat(D), _full((8, D))],
        out_shape=[jax.ShapeDtypeStruct((S, D), F32), jax.ShapeDtypeStruct((8, D), F32)],
        scratch_shapes=[], operands=[dp_a, dp_b, dp_c, w_a, w_b, w_c, ctx, x, dx1, nw, sh, sc])


_C1 = 1.0 - ADAM_B1 ** ADAM_STEP
_C2 = 1.0 - ADAM_B2 ** ADAM_STEP


def _adamw_math(w, g, m, v):
    m = ADAM_B1 * m + (1.0 - ADAM_B1) * g
    v = ADAM_B2 * v + (1.0 - ADAM_B2) * (g * g)
    m_hat = m / _C1
    v_hat = v / _C2
    delta = -ADAM_LR * (m_hat / (jnp.sqrt(v_hat) + ADAM_EPS) + ADAM_WD * w)
    return delta, m, v


def _adamw_sharded(terms, w, m, v, name, tr, extra=None, after=None):
    rows, cols = w.shape

    def body(*refs):
        t_ref, w_ref, m_ref, v_ref = refs[:4]
        g_ref, d_ref, nm_ref, nv_ref = refs[-4:]
        g = t_ref[0].astype(F32)
        for s in range(1, N_CHIPS):
            g = g + t_ref[s].astype(F32)
        if extra is not None:
            g = g + refs[4][...].astype(F32)
        g_ref[...] = g
        d_ref[...], nm_ref[...], nv_ref[...] = _adamw_math(w_ref[...], g, m_ref[...], v_ref[...])

    blk = pl.BlockSpec((tr, cols), lambda i: (i, 0))
    return pl.pallas_call(
        body, name=name, grid=(rows // tr,),
        in_specs=[pl.BlockSpec((N_CHIPS, tr, cols), lambda i: (0, i, 0)), blk, blk, blk]
        + ([blk] if extra is not None else []) + ([_ANY] if after is not None else []),
        out_specs=[blk] * 4,
        out_shape=[jax.ShapeDtypeStruct((rows, cols), F32)] * 4,
        compiler_params=_cp(("parallel",)),
    )(terms, w, m, v, *([extra] if extra is not None else []), *([after] if after is not None else []))


def _adamw_plain(g, w, m, v, name, tr=None):
    def body(g_ref, w_ref, m_ref, v_ref, d_ref, nm_ref, nv_ref):
        d_ref[...], nm_ref[...], nv_ref[...] = _adamw_math(w_ref[...], g_ref[...], m_ref[...], v_ref[...])

    if tr is None:
        return pl.pallas_call(
            body, name=name, in_specs=[_VMEM_WHOLE] * 4, out_specs=[_VMEM_WHOLE] * 3,
            out_shape=[jax.ShapeDtypeStruct(w.shape, F32)] * 3,
            compiler_params=_cp(),
        )(g, w, m, v)
    blk = pl.BlockSpec((tr, w.shape[1]), lambda i: (i, 0))
    return pl.pallas_call(
        body, name=name, grid=(w.shape[0] // tr,), in_specs=[blk] * 4, out_specs=[blk] * 3,
        out_shape=[jax.ShapeDtypeStruct(w.shape, F32)] * 3,
        compiler_params=_cp(("parallel",)),
    )(g, w, m, v)


SMALL_ROWS = 16
R_DMOD, R_DCTX, R_NMIX, R_NFFN, R_MISC, R_DLB, R_BADA01 = 0, 6, 8, 9, 10, 11, 13
M_HNW, M_QNW, M_KNW, M_SINK, M_LOSS = 0, 128, 256, 384, 512


def _pack_small(acc_in, acc_mg, acc_ffn, acc_prep, dsink, dlb):
    def body(in_ref, mg_ref, ff_ref, pp_ref, ds_ref, dlb_ref, o_ref):
        o_ref[...] = jnp.zeros_like(o_ref)
        o_ref[0:2, :] = in_ref[0:2, :]
        o_ref[2:3, :] = mg_ref[0:1, :]
        o_ref[3:5, :] = ff_ref[0:2, :]
        o_ref[5:6, :] = ff_ref[3:4, :]
        o_ref[6:8, :] = in_ref[3:5, :]
        o_ref[8:9, :] = in_ref[2:3, :]
        o_ref[9:10, :] = ff_ref[2:3, :]
        o_ref[10:11, M_HNW:M_HNW + HG_DIM] = pp_ref[0:1, 0:HG_DIM]
        r = lax.broadcasted_iota(jnp.int32, (ATW, 128), 0)
        c = lax.broadcasted_iota(jnp.int32, (ATW, 128), 1)
        fold = jnp.where((r % HEAD_DIM == c) & (c < HEAD_DIM), 1.0, 0.0).astype(BF16)
        qk = jnp.concatenate([pp_ref[1:2, :], pp_ref[2:3, :], jnp.zeros((6, ATW), F32)], axis=0)
        folded = _dot_exact_rhs01(qk, fold)
        o_ref[10:11, M_QNW:M_QNW + 128] = folded[0:1, :]
        o_ref[10:11, M_KNW:M_KNW + 128] = folded[1:2, :]
        o_ref[10:11, M_SINK:M_SINK + ATT_HEADS] = ds_ref[...]
        o_ref[10:11, M_LOSS:M_LOSS + 128] = ff_ref[4:5, 0:128]
        o_ref[11:13, 0:HGW] = dlb_ref[...]

    return pl.pallas_call(
        body, name="pack_small", in_specs=[_VMEM_WHOLE] * 6, out_specs=_VMEM_WHOLE,
        out_shape=jax.ShapeDtypeStruct((SMALL_ROWS, D), F32), compiler_params=_cp(),
    )(acc_in, acc_mg, acc_ffn, acc_prep, dsink, dlb)


def _sum_small(gathered):
    def body(g_ref, o_ref):
        tot = g_ref[0]
        for s in range(1, N_DEV):
            tot = tot + g_ref[s]
        o_ref[...] = tot
        o_ref[R_BADA01:R_BADA01 + 2, :] = tot[0:2, :] + tot[R_DCTX:R_DCTX + 2, :]

    return pl.pallas_call(
        body, name="sum_small", in_specs=[_VMEM_WHOLE], out_specs=_VMEM_WHOLE,
        out_shape=jax.ShapeDtypeStruct((SMALL_ROWS, D), F32), compiler_params=_cp(),
    )(gathered)


_REP_NAMES = ("b_ada", "c_ctx", "norm_mix_w", "norm_ffn_w", "hgrn_norm_w", "q_norm_w", "k_norm_w", "attn_sinks")


def _adamw_replicated(tot, g_c_ctx, ws, ms, vs):
    n = len(_REP_NAMES)

    def body(*refs):
        tot_ref, gc_ref = refs[0], refs[1]
        w_refs, m_refs, v_refs = refs[2:2 + n], refs[2 + n:2 + 2 * n], refs[2 + 2 * n:2 + 3 * n]
        outs = refs[2 + 3 * n:]
        row = lambda r: tot_ref[r:r + 1, :]
        misc = row(R_MISC)
        grads = [jnp.concatenate([row(R_BADA01), row(R_BADA01 + 1)] + [row(k) for k in range(2, 6)], axis=1),
                 gc_ref[...], row(R_NMIX), row(R_NFFN),
                 misc[:, M_HNW:M_HNW + HG_DIM], misc[:, M_QNW:M_QNW + HEAD_DIM],
                 misc[:, M_KNW:M_KNW + HEAD_DIM], misc[:, M_SINK:M_SINK + ATT_HEADS]]
        for k in range(n):
            outs[k][...] = grads[k]
            outs[n + k][...], outs[2 * n + k][...], outs[3 * n + k][...] = _adamw_math(
                w_refs[k][...], grads[k], m_refs[k][...], v_refs[k][...])

    shapes = [jax.ShapeDtypeStruct(w.shape, F32) for w in ws]
    return pl.pallas_call(
        body, name="adamw_replicated", in_specs=[_VMEM_WHOLE] * (2 + 3 * n), out_specs=[_VMEM_WHOLE] * (4 * n),
        out_shape=shapes * 4, compiler_params=_cp(),
    )(tot, g_c_ctx, *ws, *ms, *vs)


def _lb_grads(dlb, lbl):
    def body(d_ref, l_ref, o_ref):
        for d in (0, 1):
            ll = l_ref[d]
            lb = _sigmoid(ll[0:1, :] - ll[1:2, :])
            t = d_ref[d:d + 1, :] * lb * (1.0 - lb)
            o_ref[d, 0:1, :] = t
            o_ref[d, 1:2, :] = -t

    return pl.pallas_call(
        body, name="lb_grads", in_specs=[_VMEM_WHOLE] * 2, out_specs=_VMEM_WHOLE,
        out_shape=jax.ShapeDtypeStruct((2, 2, HGW), F32), compiler_params=_cp(),
    )(dlb, lbl)


def _c_ctx_grad(terms, c_ctx):
    def body(t_ref, c_ref, o_ref):
        tot = t_ref[0, 8:9, :]
        for s in range(1, N_DEV):
            tot = tot + t_ref[s, 8:9, :]
        cv = c_ref[...]
        sg = _sigmoid(cv)
        o_ref[...] = tot * (sg * (1.0 + cv * (1.0 - sg)))

    return pl.pallas_call(
        body, name="c_ctx_grad", in_specs=[_VMEM_WHOLE] * 2, out_specs=_VMEM_WHOLE,
        out_shape=jax.ShapeDtypeStruct((1, D), F32), compiler_params=_cp(),
    )(terms, c_ctx)


def _in_perm():
    fz, bz, inp, kk, vv, qhg, ghg, qat, gates = 0, 512, 1024, 1536, 1664, 1792, 2304, 2816, 3328
    cols = []
    for h in range(HG_HEADS):
        for base in (fz, bz, inp, qhg):
            cols += list(range(base + 128 * h, base + 128 * (h + 1)))
    cols += list(range(ghg, ghg + 512)) + list(range(qat, qat + 512))
    cols += list(range(kk, kk + 128)) + list(range(vv, vv + 128))
    cols += list(range(gates, gates + 2048))
    return np.asarray(cols, np.int32)


_PERM = _in_perm()


_PIECES = {"a": (0, WA, 128), "b": (WA, WB, 256), "c": (WA + WB, WC, 256)}


def _block_starts(piece):
    lo, n, blk = _PIECES[piece]
    starts = [int(_PERM[r]) for r in range(lo, lo + n, blk)]
    assert all(s % blk == 0 and np.array_equal(_PERM[r:r + blk], np.arange(s, s + blk))
               for s, r in zip(starts, range(lo, lo + n, blk)))
    return starts, blk


def _block_table(piece):
    starts, blk = _block_starts(piece)
    return jnp.asarray([s // blk for s in starts], jnp.int32), blk


def _order_w(w_in_t):
    def body(x_ref, *o_refs):
        for o_ref, piece in zip(o_refs, "abc"):
            starts, blk = _block_starts(piece)
            for i, s in enumerate(starts):
                o_ref[i * blk:(i + 1) * blk, :] = x_ref[s:s + blk, :]

    return pl.pallas_call(
        body, name="order_w", in_specs=[_VMEM_WHOLE], out_specs=[_VMEM_WHOLE] * 3,
        out_shape=[jax.ShapeDtypeStruct((_PIECES[p][1], D), w_in_t.dtype) for p in "abc"],
        compiler_params=_cp(),
    )(w_in_t)


def _mm_tn_placed(a, b, table, blk, into, out_rows, name):
    k, n = b.shape

    def body(t_ref, a_ref, b_ref, *rest):
        rest[-1][...] = _dot_tn(a_ref[...], b_ref[...]).astype(BF16)

    operands = [table, a, b]
    in_specs, aliases = [pl.BlockSpec((k, blk), lambda i, t: (0, i)), pl.BlockSpec((k, n), lambda i, t: (0, 0))], {}
    if into is not None:
        operands.append(into)
        in_specs.append(_ANY)
        aliases = {3: 0}
    return pl.pallas_call(
        body, name=name,
        grid_spec=pltpu.PrefetchScalarGridSpec(
            num_scalar_prefetch=1, grid=(table.shape[0],), in_specs=in_specs,
            out_specs=pl.BlockSpec((blk, n), lambda i, t: (t[i], 0))),
        out_shape=jax.ShapeDtypeStruct((out_rows, n), BF16),
        input_output_aliases=aliases,
        compiler_params=_cp(("arbitrary",)),
    )(*operands)


def _mm_tn_placed_grouped(a, b, table, blk, group, out_rows, name):
    k, n = b.shape

    def body(t_ref, a_ref, b_ref, o_ref, buf, sems):
        i = pl.program_id(0)
        buf[...] = _dot_tn(a_ref[...], b_ref[...]).astype(BF16)
        copies = [pltpu.make_async_copy(
            buf.at[pl.ds(g * blk, blk)],
            o_ref.at[pl.ds(pl.multiple_of(t_ref[i * group + g] * blk, blk), blk)], sems.at[g])
            for g in range(group)]
        for cp in copies:
            cp.start()
        for cp in copies:
            cp.wait()

    return pl.pallas_call(
        body, name=name,
        grid_spec=pltpu.PrefetchScalarGridSpec(
            num_scalar_prefetch=1, grid=(table.shape[0] // group,),
            in_specs=[pl.BlockSpec((k, blk * group), lambda i, t: (0, i)),
                      pl.BlockSpec((k, n), lambda i, t: (0, 0))],
            out_specs=_ANY,
            scratch_shapes=[pltpu.VMEM((blk * group, n), BF16), pltpu.SemaphoreType.DMA((group,))]),
        out_shape=jax.ShapeDtypeStruct((out_rows, n), BF16),
        compiler_params=_cp(("arbitrary",)),
    )(table, a, b)


def _local_step(x2, ctx2, h_all, h_lat, tgt, lbl, sh_in, sc_in, gate1, sh2, sc2, gate2, norm_mix_w, norm_ffn_w,
                hgrn_norm_w, q_norm_w, k_norm_w, attn_sinks, w_a, w_b, w_c, s_bh, s_ba, s_out,
                s_gate, s_up, s_down):
    first_last = lambda n: [(0, True), (n - 1, False)]
    p_a = _mm_nt(h_all, w_a, tm=T, tn=512, out_dtype=F32, name="proj_a")
    (o, st), (g_gate, g_bh, g_ba) = _hgrn_fwd(
        p_a, lbl, (_gather_comm_relayed([s_gate, s_bh, s_ba]),
                   [(0, True), (HG_HEADS - 2, True), (HG_HEADS - 1, False)]))
    (p_b, p_c), (g_out,) = _proj_bc(
        h_all, w_b, w_c, (_gather_comm_relayed([s_out]), [(0, True), (N_TILES - 4, True), (N_TILES - 1, False)]))
    cos, sin = _rope_tables()
    qnw_t, knw_t = jnp.tile(q_norm_w, (1, ATT_HEADS)), jnp.tile(k_norm_w, (1, KV_HEADS))
    y_hg, qn, k_rep, v_rep = _prep_fwd(p_b, o, cos, sin, hgrn_norm_w, qnw_t, knw_t)
    (y_at, lse), (g_up, g_down) = _attn_fwd(
        qn, k_rep, v_rep, attn_sinks,
        (_gather_comm_relayed([s_up, s_down]), [(0, True), (N_BLOCKS - 6, True), (N_BLOCKS - 1, False)]))
    w_bh, w_ba, w_o = g_bh.reshape(D, HGW), g_ba.reshape(D, ATW), g_out.reshape(D, D)
    (mixed, r, x1, h2), _ = _merge_fwd(
        y_hg, y_at, p_c, x2, w_bh, w_ba, w_o, gate1, norm_ffn_w, sh2, sc2)
    g_gate, g_up, g_down = [g.reshape(N_FF_TILES, FF_TILE, D) for g in (g_gate, g_up, g_down)]

    act, d_gate, d_up, d_f, dx1, acc_ffn = _ffn_fused(x1, h2, tgt, g_gate, g_up, g_down, gate2,
                                                      norm_ffn_w, sc2)
    by_chip = lambda t: t.reshape((N_CHIPS, 2) + t.shape[1:])
    ff_by_chip = lambda t: t.reshape(N_CHIPS, 2, FF_BLK, D)
    t_down, _ = _mm_tn_blocked(act, d_f, "grad_down", N_FF_TILES)
    t_down = ff_by_chip(t_down)
    t_gate, (f_down,) = _mm_tn_blocked(d_gate, h2, "grad_gate", N_FF_HALVES,
                                       (_sibling_comm([t_down]), first_last(N_FF_HALVES)))
    t_gate = ff_by_chip(t_gate)
    t_up, (f_gate,) = _mm_tn_blocked(d_up, h2, "grad_up", N_FF_HALVES,
                                     (_sibling_comm([t_gate]), first_last(N_FF_HALVES)))
    t_up = ff_by_chip(t_up)

    (d_r, d_a, d_b, dp_c, dy_hg, dy_at, acc_mg), (f_up,) = _merge_bwd(
        dx1, r, y_hg, y_at, p_c, w_bh, w_ba, w_o, gate1, (_sibling_comm([t_up]), first_last(N_LAT_TILES)))
    c_down, c_gate, c_up = [_pair_sum(t, f, "pair_sum_" + nm) for t, f, nm in
                            ((t_down, f_down, "down"), (t_gate, f_gate, "gate"), (t_up, f_up, "up"))]
    t_out = _mm_tn(mixed, d_r, tk=1024, nk=2, tm=1024, tn=1024, out_dtype=BF16, name="grad_out")
    t_bh = _mm_tn(d_a, y_hg, tk=2048, nk=1, tm=1024, tn=512, out_dtype=BF16, name="grad_bh")
    t_ba = _mm_tn(d_b, y_at, tk=2048, nk=1, tm=1024, tn=512, out_dtype=BF16, name="grad_ba")
    t_bh, t_ba, t_out = [by_chip(t.reshape(N_DEV, D // N_DEV, t.shape[1])) for t in (t_bh, t_ba, t_out)]
    (dq, dk_rep, dv_rep, dsink), (r_up,) = _attn_bwd(
        qn, k_rep, v_rep, attn_sinks, y_at, lse, dy_at, (_chip_comm([c_up]), first_last(N_BLOCKS)))
    (dp_b, d_o, acc_prep), (f_bh, f_ba, f_out) = _prep_bwd(
        p_b, o, cos, sin, hgrn_norm_w, qnw_t, knw_t, dy_hg, dq, dk_rep, dv_rep,
        (_sibling_comm([t_bh, t_ba, t_out]), first_last(N_TILES)))
    c_bh, c_ba, c_out = [_pair_sum(t, f, "pair_sum_" + nm) for t, f, nm in
                         ((t_bh, f_bh, "bh"), (t_ba, f_ba, "ba"), (t_out, f_out, "out"))]
    (dp_a, dlb), (r_bh, r_ba, r_out, r_down, r_gate) = _hgrn_bwd(
        p_a, lbl, d_o, st, (_chip_comm([c_bh, c_ba, c_out, c_down, c_gate]), first_last(HG_HEADS)))
    t_in = _mm_tn_placed_grouped(dp_a, h_all, *_block_table("a"), 4, IN_COLS, "grad_in_a")
    for dp, h, nm in ((dp_b, h_all, "b"), (dp_c, h_lat, "c")):
        t_in = _mm_tn_placed(dp, h, *_block_table(nm), t_in, IN_COLS, "grad_in_" + nm)
    t_in = by_chip(t_in.reshape(N_DEV, IN_BLK, D))
    (f_in,) = _run_comm(_sibling_comm([t_in]), "scatter_in_sibling")
    c_in = _pair_sum(t_in, f_in, "pair_sum_in")
    sems, c_in, land, token = _chip_exchange_start(c_in, jnp.zeros(c_in.shape, c_in.dtype))
    (grad_x, acc_in), _ = _input_bwd(dp_a, dp_b, dp_c, w_a, w_b, w_c, ctx2, x2, dx1,
                                     norm_mix_w + token[0, 0], sh_in, sc_in)
    small = _pack_small(acc_in, acc_mg, acc_ffn, acc_prep, dsink, dlb)
    return grad_x, small, [r_bh, r_ba, r_out, r_gate, r_up, r_down], (sems, c_in, land)


def kernel(x, c, ctx, c_ctx, w_ada, b_ada, norm_mix_w, norm_ffn_w, w_in, hgrn_lb_logits, hgrn_norm_w, q_norm_w, k_norm_w, attn_sinks, w_branch_hgrn, w_branch_attn, w_out, w_ffn_gate, w_ffn_up, w_ffn_down, loss_target, m_c_ctx, m_w_ada, m_b_ada, m_norm_mix_w, m_norm_ffn_w, m_w_in, m_hgrn_lb_logits, m_hgrn_norm_w, m_q_norm_w, m_k_norm_w, m_attn_sinks, m_w_branch_hgrn, m_w_branch_attn, m_w_out, m_w_ffn_gate, m_w_ffn_up, m_w_ffn_down, v_c_ctx, v_w_ada, v_b_ada, v_norm_mix_w, v_norm_ffn_w, v_w_in, v_hgrn_lb_logits, v_hgrn_norm_w, v_q_norm_w, v_k_norm_w, v_attn_sinks, v_w_branch_hgrn, v_w_branch_attn, v_w_out, v_w_ffn_gate, v_w_ffn_up, v_w_ffn_down):
    me = 4 * lax.axis_index("x") + 2 * lax.axis_index("y") + lax.axis_index("c")
    x2, ctx2, tgt = x[0], ctx[0], loss_target[0]
    w_ada2, w_in2 = w_ada[0], w_in[0]

    cond = jnp.zeros((8, D), F32).at[0].set(c[0]).at[1, :256].set(hgrn_lb_logits.reshape(256))
    b_cols = lax.dynamic_slice(b_ada, (0, me * ADA_BLK), (1, ADA_BLK))
    g0, cc, mod, g_in, h_all, h_lat = _prologue(cond, c_ctx.reshape(1, D), w_ada2, b_cols, w_in2.T.astype(BF16),
                                         x2, ctx2, norm_mix_w)
    lbl = jnp.transpose(g0[:, 1, :256].reshape(N_DEV, 2, 2, 64), (1, 2, 0, 3)).reshape(2, 2, HGW)
    sh1, sc1, gate1, sh2, sc2, gate2 = [mod[k:k + 1] for k in range(6)]
    sh_in = jnp.concatenate([mod[6:7], sh1], axis=0)
    sc_in = jnp.concatenate([mod[7:8], sc1], axis=0)

    shards = [w_branch_hgrn[0].T, w_branch_attn[0].T, w_out[0], w_ffn_gate[0].T, w_ffn_up[0].T, w_ffn_down[0]]
    w_in_t = g_in.reshape(IN_COLS, D)
    w_a, w_b, w_c = _order_w(w_in_t)

    grad_x, small, (r_bh, r_ba, r_out, r_gate, r_up, r_down), pending_in = _local_step(
        x2, ctx2, h_all, h_lat, tgt, lbl, sh_in, sc_in, gate1, sh2, sc2, gate2, norm_mix_w, norm_ffn_w, hgrn_norm_w,
        q_norm_w, k_norm_w, attn_sinks, w_a, w_b, w_c, *[s.astype(BF16) for s in shards])

    big, updated = {}, []
    for nm, rr, ww, mm, vv, tr, transposed in (
            ("w_branch_hgrn", r_bh, w_branch_hgrn[0], m_w_branch_hgrn[0], v_w_branch_hgrn[0], 128, True),
            ("w_branch_attn", r_ba, w_branch_attn[0], m_w_branch_attn[0], v_w_branch_attn[0], 128, True),
            ("w_out", r_out, w_out[0], m_w_out[0], v_w_out[0], 128, False),
            ("w_ffn_gate", r_gate, w_ffn_gate[0], m_w_ffn_gate[0], v_w_ffn_gate[0], 176, True),
            ("w_ffn_up", r_up, w_ffn_up[0], m_w_ffn_up[0], v_w_ffn_up[0], 176, True),
            ("w_ffn_down", r_down, w_ffn_down[0], m_w_ffn_down[0], v_w_ffn_down[0], 176, False)):
        if transposed:
            res = _adamw_sharded(rr, ww.T, mm.T, vv.T, "adamw_" + nm, tr, after=grad_x)
            big[nm] = [t.T[None] for t in res]
        else:
            res = _adamw_sharded(rr, ww, mm, vv, "adamw_" + nm, tr, after=grad_x)
            big[nm] = [t[None] for t in res]
        updated.append(res[1])

    (g2,) = _all_gather([small], "gather_small", True, after=updated)
    tot = _sum_small(g2)
    dm = jnp.zeros((16, 6 * D), F32).at[:8].set(g2[:, R_DMOD:R_DMOD + 6, :].reshape(N_DEV, 6 * D))
    dm = dm.at[8, :2 * D].set(tot[R_DCTX:R_DCTX + 2].reshape(2 * D))
    dm_cols = lax.dynamic_slice(dm, (0, me * ADA_BLK), (16, ADA_BLK))
    g_w_ada, dsc_term = _ada_grads(cc, dm_cols, w_ada2)
    (g3,) = _all_gather([dsc_term], "gather_cctx", True)
    g_c_ctx = _c_ctx_grad(g3, c_ctx.reshape(1, D))
    g_lbl = _lb_grads(tot[R_DLB:R_DLB + 2, :HGW], lbl)
    g_lb_mine = lax.dynamic_slice(g_lbl, (0, 0, me * 64), (2, 2, 64))
    misc = tot[R_MISC]
    loss = misc[M_LOSS]

    rep_out = _adamw_replicated(
        tot, g_c_ctx,
        [b_ada, c_ctx.reshape(1, D), norm_mix_w, norm_ffn_w, hgrn_norm_w, q_norm_w, k_norm_w, attn_sinks],
        [m_b_ada, m_c_ctx.reshape(1, D), m_norm_mix_w, m_norm_ffn_w, m_hgrn_norm_w, m_q_norm_w, m_k_norm_w,
         m_attn_sinks],
        [v_b_ada, v_c_ctx.reshape(1, D), v_norm_mix_w, v_norm_ffn_w, v_hgrn_norm_w, v_q_norm_w, v_k_norm_w,
         v_attn_sinks])
    rep = []
    for kind in range(4):
        vals = dict(zip(_REP_NAMES, rep_out[kind * len(_REP_NAMES):(kind + 1) * len(_REP_NAMES)]))
        vals["c_ctx"] = vals["c_ctx"].reshape(D)
        rep.append(vals)

    sems, c_in, land = pending_in
    d_ada, nm_ada, nv_ada = _adamw_plain(g_w_ada, w_ada2, m_w_ada[0], v_w_ada[0], "adamw_w_ada", tr=256)
    land = _chip_exchange_wait(sems, c_in, land, d_ada)
    own = lax.dynamic_index_in_dim(c_in, 2 * lax.axis_index("x") + lax.axis_index("y"), 0, keepdims=False)
    big["w_in"] = [t.T[None] for t in _adamw_sharded(land, w_in2.T, m_w_in[0].T, v_w_in[0].T, "adamw_w_in", 336,
                                                     extra=own)]
    ada = [t[None] for t in (g_w_ada, d_ada, nm_ada, nv_ada)]
    lb_w = hgrn_lb_logits.reshape(4, 64)
    d_lb, nm_lb, nv_lb = _adamw_plain(g_lb_mine.reshape(4, 64), lb_w, m_hgrn_lb_logits.reshape(4, 64),
                                      v_hgrn_lb_logits.reshape(4, 64), "adamw_lb")
    lbs = [t.reshape(2, 2, 64) for t in (g_lb_mine, d_lb, nm_lb, nv_lb)]

    names = ['c_ctx', 'w_ada', 'b_ada', 'norm_mix_w', 'norm_ffn_w', 'w_in', 'hgrn_lb_logits', 'hgrn_norm_w',
             'q_norm_w', 'k_norm_w', 'attn_sinks', 'w_branch_hgrn', 'w_branch_attn', 'w_out', 'w_ffn_gate',
             'w_ffn_up', 'w_ffn_down']
    outs = [loss, grad_x[None]]
    for kind in range(4):
        for nm in names:
            if nm == 'w_ada':
                outs.append(ada[kind])
            elif nm == 'hgrn_lb_logits':
                outs.append(lbs[kind])
            elif nm in big:
                outs.append(big[nm][kind])
            else:
                outs.append(rep[kind][nm])
    return tuple(outs)
```

```python
import functools
import math

import numpy as np
import jax
import jax.numpy as jnp
from jax import lax
from jax.experimental import pallas as pl
from jax.experimental.pallas import tpu as pltpu

F32 = jnp.float32
BF16 = jnp.bfloat16

N_DEV = 8
D = 1024
S = 2048
L = 256
T = L + S
TM = 256
N_TILES = T // TM
N_LAT_TILES = S // TM
HG_HEADS = 4
HG_DIM = 128
HGW = 512
CHUNK = 32
N_CHUNKS = T // CHUNK
N_CTX_CHUNKS = L // CHUNK
ATT_HEADS = 8
KV_HEADS = 2
HEAD_DIM = 64
ATW = 512
KVW = 128
BLOCK = 128
N_BLOCKS = S // BLOCK
GRID_W = 64
ROPE_THETA = 10000.0
D_FF = 2816
FF_BLK = D_FF // N_DEV
FF_TILE = 256
N_FF_TILES = D_FF // FF_TILE
N_FF_HALVES = 2
IN_COLS = 5376
IN_BLK = IN_COLS // N_DEV
ADA_BLK = 6 * D // N_DEV
EPS = 1e-6
WA, WB, WC = 2048, 1280, 2048

ADAM_LR = 0.001
ADAM_B1 = 0.9
ADAM_B2 = 0.999
ADAM_EPS = 1e-08
ADAM_WD = 0.01
ADAM_STEP = 10

VMEM_LIMIT = 56 * 1024 * 1024
MESH = pl.DeviceIdType.MESH


def _cp(sem=None, vmem=VMEM_LIMIT):
    return pltpu.CompilerParams(dimension_semantics=sem, vmem_limit_bytes=vmem)


def _full(shape):
    n = len(shape)
    return pl.BlockSpec(shape, lambda *_: (0,) * n)


_VMEM_WHOLE = pl.BlockSpec(memory_space=pltpu.VMEM)
_ANY = pl.BlockSpec(memory_space=pl.ANY)


def _sigmoid(v):
    return 1.0 / (1.0 + jnp.exp(-v))


def _dot(a, b):
    return jnp.dot(a, b, preferred_element_type=F32)


def _dot_nt(a, b):
    return lax.dot_general(a, b, (((1,), (1,)), ((), ())), preferred_element_type=F32)


def _dot_tn(a, b):
    return lax.dot_general(a, b, (((0,), (0,)), ((), ())), preferred_element_type=F32)


def _split3(v):
    hi = v.astype(BF16)
    r = v - hi.astype(F32)
    mid = r.astype(BF16)
    lo = (r - mid.astype(F32)).astype(BF16)
    return hi, mid, lo


def _dot_exact_rhs01(v, m01):
    hi, mid, lo = _split3(v)
    return _dot(hi, m01) + _dot(mid, m01) + _dot(lo, m01)


def _split2(v):
    hi = v.astype(BF16)
    return hi, (v - hi.astype(F32)).astype(BF16)


def _dot_lhs01(m01, v):
    hi, lo = _split2(v)
    return _dot(m01, hi) + _dot(m01, lo)


def _dot_f32(a, b, dot=_dot):
    ah, am, al = _split3(a)
    bh, bm, bl = _split3(b)
    return (dot(ah, bh) + (dot(ah, bm) + dot(am, bh))
            + (dot(am, bm) + dot(ah, bl) + dot(al, bh)))


def _my_pos():
    return lax.axis_index("x"), lax.axis_index("y"), lax.axis_index("c")


class _Comm:
    def __init__(self, operands, out_shapes, sems, phases):
        self.operands, self.out_shapes, self.sems, self.phases = operands, out_shapes, sems, phases


def _gather_comm(blocks):
    n = len(blocks)

    def parts(ins, outs, sems):
        send_sems, recv_sems, local_sems = sems
        x, y, c = _my_pos()
        me, sibling = (x, y, c), (x, y, 1 - c)
        chips = [(1 - x, y), (x, 1 - y), (1 - x, 1 - y)]

        def slot(a, px, py, pc):
            return outs[a].at[4 * px + 2 * py + pc]

        def copy(a, k, block, to, src=None):
            return pltpu.make_async_remote_copy(
                src_ref=slot(a, *block) if src is None else src, dst_ref=slot(a, *block),
                send_sem=send_sems.at[a, k], recv_sem=recv_sems.at[a, k],
                device_id=to, device_id_type=MESH)

        mine = [pltpu.make_async_copy(ins[a], slot(a, *me), local_sems.at[a]) for a in range(n)]
        first = []
        for a in range(n):
            first.append(copy(a, 0, me, sibling, src=ins[a]))
            first += [copy(a, 1 + j, me, (*chip, c), src=ins[a]) for j, chip in enumerate(chips)]
        passed = [copy(a, 4 + j, (*chip, c), sibling) for j, chip in enumerate(chips) for a in range(n)]
        return c, me, sibling, chips, copy, mine, first, passed

    def start(ins, outs, sems):
        _, _, _, _, _, mine, first, _ = parts(ins, outs, sems)
        for cp in mine + first:
            cp.start()

    def forward(ins, outs, sems):
        c, me, _, chips, copy, _, _, passed = parts(ins, outs, sems)
        for j, chip in enumerate(chips):
            for a in range(n):
                copy(a, 1 + j, (*chip, c), me).wait_recv()
                passed[j * n + a].start()

    def finish(ins, outs, sems):
        c, me, sibling, chips, copy, mine, first, passed = parts(ins, outs, sems)
        for a in range(n):
            copy(a, 0, sibling, me).wait_recv()
            for j, chip in enumerate(chips):
                copy(a, 4 + j, (*chip, 1 - c), me).wait_recv()
        for cp in first + passed:
            cp.wait_send()
        for cp in mine:
            cp.wait()

    return _Comm(blocks, [jax.ShapeDtypeStruct((N_DEV,) + b.shape, b.dtype) for b in blocks],
                 [pltpu.SemaphoreType.DMA((n, 7)), pltpu.SemaphoreType.DMA((n, 7)), pltpu.SemaphoreType.DMA((n,))],
                 [start, forward, finish])


def _gather_comm_relayed(blocks):
    n = len(blocks)

    def parts(ins, outs, sems):
        send_sems, recv_sems, local_sems = sems
        x, y, c = _my_pos()
        me, sibling = (x, y, c), (x, y, 1 - c)
        x_nbr, y_nbr, diag = (1 - x, y, c), (x, 1 - y, c), (1 - x, 1 - y, c)

        def slot(a, dev, half=None):
            ref = outs[a].at[4 * dev[0] + 2 * dev[1] + dev[2]]
            if half is None:
                return ref
            rows = blocks[a].shape[0] // 2
            return ref.at[pl.ds(half * rows, rows)]

        def copy(a, k, block, to, half=None, src=None):
            return pltpu.make_async_remote_copy(
                src_ref=slot(a, block, half) if src is None else src, dst_ref=slot(a, block, half),
                send_sem=send_sems.at[a, k], recv_sem=recv_sems.at[a, k],
                device_id=to, device_id_type=MESH)

        mine = [pltpu.make_async_copy(ins[a], slot(a, me), local_sems.at[a]) for a in range(n)]
        return me, sibling, x_nbr, y_nbr, diag, copy, mine

    def start(ins, outs, sems):
        me, sibling, x_nbr, y_nbr, _, copy, mine = parts(ins, outs, sems)
        for cp in mine:
            cp.start()
        for a in range(n):
            for k, to in ((1, x_nbr), (2, y_nbr), (0, sibling)):
                copy(a, k, me, to, src=ins[a]).start()

    def forward(ins, outs, sems):
        me, sibling, x_nbr, y_nbr, _, copy, _ = parts(ins, outs, sems)
        for a in range(n):
            copy(a, 1, x_nbr, me).wait_recv()
            copy(a, 3, x_nbr, y_nbr, half=0).start()
            copy(a, 5, x_nbr, sibling).start()
        for a in range(n):
            copy(a, 2, y_nbr, me).wait_recv()
            copy(a, 4, y_nbr, x_nbr, half=1).start()
            copy(a, 6, y_nbr, sibling).start()

    def finish(ins, outs, sems):
        me, sibling, x_nbr, y_nbr, diag, copy, mine = parts(ins, outs, sems)
        sib = lambda dev: (dev[0], dev[1], sibling[2])
        for a in range(n):
            copy(a, 3, diag, me, half=0).wait_recv()
            copy(a, 4, diag, me, half=1).wait_recv()
            copy(a, 7, diag, sibling).start()
        for a in range(n):
            copy(a, 0, sibling, me).wait_recv()
            for k, dev in ((5, x_nbr), (6, y_nbr), (7, diag)):
                copy(a, k, sib(dev), me).wait_recv()
        for a in range(n):
            for k, block, to, half in ((0, me, sibling, None), (1, me, x_nbr, None), (2, me, y_nbr, None),
                                       (3, x_nbr, y_nbr, 0), (4, y_nbr, x_nbr, 1), (5, x_nbr, sibling, None),
                                       (6, y_nbr, sibling, None), (7, diag, sibling, None)):
                copy(a, k, block, to, half=half, src=ins[a] if block is me else None).wait_send()
        for cp in mine:
            cp.wait()

    return _Comm(blocks, [jax.ShapeDtypeStruct((N_DEV,) + b.shape, b.dtype) for b in blocks],
                 [pltpu.SemaphoreType.DMA((n, 8)), pltpu.SemaphoreType.DMA((n, 8)), pltpu.SemaphoreType.DMA((n,))],
                 [start, forward, finish])


_HBM = pl.BlockSpec(memory_space=pltpu.HBM)
_SEM = pl.BlockSpec(memory_space=pltpu.SEMAPHORE)
_SPLIT_COPY = pltpu.CompilerParams(has_side_effects=pltpu.SideEffectType.DATAFLOW_SIDE_EFFECTING)


def _chip_exchange_copies(src_ref, land_ref, sems):
    x, y, c = _my_pos()
    q_me = 2 * x + y
    pairs = []
    for j, (px, py) in enumerate([(1 - x, y), (x, 1 - y), (1 - x, 1 - y)]):
        q = 2 * px + py
        send = pltpu.make_async_remote_copy(
            src_ref=src_ref.at[q], dst_ref=land_ref.at[q_me], send_sem=sems[j], recv_sem=sems[3 + j],
            device_id=(px, py, c), device_id_type=MESH)
        recv = pltpu.make_async_remote_copy(
            src_ref=src_ref.at[q], dst_ref=land_ref.at[q], send_sem=sems[j], recv_sem=sems[3 + j],
            device_id=(x, y, c), device_id_type=MESH)
        pairs.append((send, recv))
    return pairs


def _chip_exchange_start(src, land):
    def body(src_ref, land_ref, *outs):
        sems, token = outs[:6], outs[8]
        for send, _ in _chip_exchange_copies(src_ref, land_ref, sems):
            send.start()
        token[...] = jnp.zeros_like(token)

    res = pl.pallas_call(
        body, name="scatter_in_start",
        out_shape=(pltpu.SemaphoreType.DMA(()),) * 6 + (
            pltpu.HBM(src.shape, src.dtype), pltpu.HBM(land.shape, land.dtype),
            jax.ShapeDtypeStruct((8, 128), F32)),
        in_specs=(_HBM, _HBM), out_specs=(_SEM,) * 6 + (_HBM, _HBM, pl.BlockSpec(memory_space=pltpu.VMEM)),
        input_output_aliases={0: 6, 1: 7}, compiler_params=_SPLIT_COPY,
    )(pltpu.with_memory_space_constraint(src, pltpu.HBM), pltpu.with_memory_space_constraint(land, pltpu.HBM))
    return res[:6], res[6], res[7], res[8]


def _chip_exchange_wait(sems, src_thru, land_thru, after):
    def body(src_ref, land_ref, *rest):
        for send, recv in _chip_exchange_copies(src_ref, land_ref, rest[:6]):
            send.wait_send()
            recv.wait_recv()

    return pl.pallas_call(
        body, name="scatter_in_wait",
        out_shape=(pltpu.HBM(src_thru.shape, src_thru.dtype), pltpu.HBM(land_thru.shape, land_thru.dtype)),
        in_specs=(_HBM, _HBM) + (_SEM,) * 6 + (_ANY,), out_specs=(_HBM, _HBM),
        input_output_aliases={0: 0, 1: 1}, compiler_params=_SPLIT_COPY,
    )(src_thru, land_thru, *sems, after)[1]


def _run_comm(comm, name, in_vmem=False, after=()):
    n_in, n_out, n_after = len(comm.operands), len(comm.out_shapes), len(after)

    def body(*refs):
        ins, refs = refs[:n_in], refs[n_in + n_after:]
        outs, sems = refs[:n_out], refs[n_out:]
        for phase in comm.phases:
            phase(ins, outs, sems)

    spec = _VMEM_WHOLE if in_vmem else _ANY
    return pl.pallas_call(
        body, name=name, out_shape=comm.out_shapes, in_specs=[spec] * n_in + [_ANY] * n_after,
        out_specs=[spec] * n_out, scratch_shapes=comm.sems,
    )(*comm.operands, *after)


def _carrier_call(body, comm, schedule, *, name, grid, in_specs, out_specs, out_shape, scratch_shapes, operands):
    n_in, n_out, n_scr = len(in_specs), len(out_specs), len(scratch_shapes)
    c_in, c_out = len(comm.operands), len(comm.out_shapes)

    def full_body(*refs):
        ins, refs = refs[:n_in], refs[n_in:]
        cins, refs = refs[:c_in], refs[c_in:]
        outs, refs = refs[:n_out], refs[n_out:]
        couts, refs = refs[:c_out], refs[c_out:]
        scr, csems = refs[:n_scr], refs[n_scr:]
        step = pl.program_id(0)

        def run(before):
            for (at, when_before), phase in zip(schedule, comm.phases):
                if when_before == before:
                    pl.when(step == at)(functools.partial(phase, cins, couts, csems))

        run(True)
        body(*ins, *outs, *scr)
        run(False)

    res = pl.pallas_call(
        full_body, name=name, grid=grid,
        in_specs=list(in_specs) + [_ANY] * c_in, out_specs=list(out_specs) + [_ANY] * c_out,
        out_shape=list(out_shape) + list(comm.out_shapes),
        scratch_shapes=list(scratch_shapes) + list(comm.sems),
        compiler_params=_cp(("arbitrary",)),
    )(*operands, *comm.operands)
    return res[:n_out], res[n_out:]


def _pcall(body, carried, *, name, grid, in_specs, out_specs, out_shape, scratch_shapes, operands):
    if carried is None:
        res = pl.pallas_call(body, name=name, grid=grid, in_specs=in_specs, out_specs=out_specs,
                             out_shape=out_shape, scratch_shapes=scratch_shapes,
                             compiler_params=_cp(("arbitrary",)))(*operands)
        return res, ()
    return _carrier_call(body, carried[0], carried[1], name=name, grid=grid, in_specs=in_specs,
                         out_specs=out_specs, out_shape=out_shape, scratch_shapes=scratch_shapes,
                         operands=operands)


def _all_gather(blocks, name, in_vmem, after=()):
    return _run_comm(_gather_comm(blocks), name, in_vmem, after)


N_CHIPS = 4


def _sibling_comm(contribs):
    n = len(contribs)

    def copies(ins, outs, sems):
        send_sems, recv_sems = sems
        x, y, c = _my_pos()
        return [pltpu.make_async_remote_copy(
            src_ref=ins[a].at[pl.ds(0, N_CHIPS), 1 - c], dst_ref=outs[a],
            send_sem=send_sems.at[a], recv_sem=recv_sems.at[a],
            device_id=(x, y, 1 - c), device_id_type=MESH) for a in range(n)]

    def start(ins, outs, sems):
        for cp in copies(ins, outs, sems):
            cp.start()

    def finish(ins, outs, sems):
        cps = copies(ins, outs, sems)
        for cp in cps:
            cp.wait_recv()
        for cp in cps:
            cp.wait_send()

    return _Comm(contribs, [jax.ShapeDtypeStruct((N_CHIPS,) + b.shape[2:], b.dtype) for b in contribs],
                 [pltpu.SemaphoreType.DMA((n,)), pltpu.SemaphoreType.DMA((n,))], [start, finish])


def _pair_sum(mine, theirs, name):
    _, _, rows, cols = mine.shape
    core = lax.axis_index("c").astype(jnp.int32).reshape(1)

    def body(c_ref, m_ref, t_ref, o_ref):
        o_ref[...] = (m_ref[...].astype(F32) + t_ref[...].astype(F32)).astype(BF16)

    return pl.pallas_call(
        body, name=name,
        grid_spec=pltpu.PrefetchScalarGridSpec(
            num_scalar_prefetch=1, grid=(N_CHIPS,),
            in_specs=[pl.BlockSpec((None, None, rows, cols), lambda q, c: (q, c[0], 0, 0)),
                      pl.BlockSpec((None, rows, cols), lambda q, c: (q, 0, 0))],
            out_specs=pl.BlockSpec((None, rows, cols), lambda q, c: (q, 0, 0))),
        out_shape=jax.ShapeDtypeStruct((N_CHIPS, rows, cols), BF16),
        compiler_params=_cp(("parallel",)),
    )(core, mine, theirs)


def _chip_comm(sums):
    n = len(sums)

    def parts(ins, outs, sems):
        send_sems, recv_sems, local_sems = sems
        x, y, c = _my_pos()
        q_me = 2 * x + y
        chips = [(1 - x, y), (x, 1 - y), (1 - x, 1 - y)]
        mine = [pltpu.make_async_copy(ins[a].at[q_me], outs[a].at[q_me], local_sems.at[a]) for a in range(n)]
        sends, recvs = [], []
        for j, (px, py) in enumerate(chips):
            for a in range(n):
                q = 2 * px + py
                sends.append(pltpu.make_async_remote_copy(
                    src_ref=ins[a].at[q], dst_ref=outs[a].at[q_me],
                    send_sem=send_sems.at[a, j], recv_sem=recv_sems.at[a, j],
                    device_id=(px, py, c), device_id_type=MESH))
                recvs.append(pltpu.make_async_remote_copy(
                    src_ref=ins[a].at[q], dst_ref=outs[a].at[q],
                    send_sem=send_sems.at[a, j], recv_sem=recv_sems.at[a, j],
                    device_id=(x, y, c), device_id_type=MESH))
        return mine, sends, recvs

    def start(ins, outs, sems):
        mine, sends, _ = parts(ins, outs, sems)
        for cp in mine + sends:
            cp.start()

    def finish(ins, outs, sems):
        mine, sends, recvs = parts(ins, outs, sems)
        for cp in recvs:
            cp.wait_recv()
        for cp in sends:
            cp.wait_send()
        for cp in mine:
            cp.wait()

    return _Comm(sums, [jax.ShapeDtypeStruct(b.shape, b.dtype) for b in sums],
                 [pltpu.SemaphoreType.DMA((n, 3)), pltpu.SemaphoreType.DMA((n, 3)), pltpu.SemaphoreType.DMA((n,))],
                 [start, finish])


def _mm_nt(a, bt, *, tm, tn, out_dtype, name, row_off=0, rows=None):
    rows = a.shape[0] if rows is None else rows
    n, k = bt.shape

    def body(a_ref, b_ref, o_ref):
        o_ref[...] = _dot_nt(a_ref[...], b_ref[...]).astype(out_dtype)

    return pl.pallas_call(
        body, name=name, grid=(rows // tm, n // tn),
        in_specs=[pl.BlockSpec((tm, k), lambda i, j: (i + row_off, 0)),
                  pl.BlockSpec((tn, k), lambda i, j: (j, 0))],
        out_specs=pl.BlockSpec((tm, tn), lambda i, j: (i, j)),
        out_shape=jax.ShapeDtypeStruct((rows, n), out_dtype),
        compiler_params=_cp(("parallel", "parallel")),
    )(a, bt)


def _mm_tn(a, b, *, tk, nk, tm, tn, out_dtype, name, a_off=0, b_off=0):
    m, n = a.shape[1], b.shape[1]

    def body(a_ref, b_ref, o_ref, acc):
        kk = pl.program_id(2)

        @pl.when(kk == 0)
        def _():
            acc[...] = jnp.zeros_like(acc)

        acc[...] += _dot_tn(a_ref[...], b_ref[...])

        @pl.when(kk == nk - 1)
        def _():
            o_ref[...] = acc[...].astype(out_dtype)

    return pl.pallas_call(
        body, name=name, grid=(m // tm, n // tn, nk),
        in_specs=[pl.BlockSpec((tk, tm), lambda i, j, kk: (kk + a_off, i)),
                  pl.BlockSpec((tk, tn), lambda i, j, kk: (kk + b_off, j))],
        out_specs=pl.BlockSpec((tm, tn), lambda i, j, kk: (i, j)),
        out_shape=jax.ShapeDtypeStruct((m, n), out_dtype),
        scratch_shapes=[pltpu.VMEM((tm, tn), F32)],
        compiler_params=_cp(("parallel", "parallel", "arbitrary")),
    )(a, b)


def _mm_tn_blocked(a, b, name, steps, carried=None):
    w = a.shape[1] // steps
    n = b.shape[1]

    def body(a_ref, b_ref, o_ref):
        o_ref[...] = _dot_tn(a_ref[...], b_ref[...]).astype(BF16)

    (out,), extra = _pcall(
        body, carried, name=name, grid=(steps,),
        in_specs=[pl.BlockSpec((S, w), lambda j: (0, j)), _full((S, n))],
        out_specs=[pl.BlockSpec((w, n), lambda j: (j, 0))],
        out_shape=[jax.ShapeDtypeStruct((a.shape[1], n), BF16)],
        scratch_shapes=[], operands=[a, b])
    return out, extra


def _prologue(cond, c_ctx, w_ada, b_cols, w_in_t, x, ctx, nw):
    rows_shape = jax.ShapeDtypeStruct((16, ADA_BLK), F32)
    big, g_cond, g_mod = _gather_comm_relayed([w_in_t]), _gather_comm([cond]), _gather_comm([rows_shape])

    def body(cond_ref, cctx_ref, wada_ref, b_ref, nw_ref, win_ref, x_ref, ctx_ref,
             g0_ref, cc_ref, mod_ref, gin_ref, h_ref, hl_ref, rows_ref, g1_ref, x_s, ctx_s, h_s, io_sems, *sems):
        s_big, s_cond, s_mod = sems[0:3], sems[3:6], sems[6:9]
        load_x = pltpu.make_async_copy(x_ref, x_s, io_sems.at[0])
        load_ctx = pltpu.make_async_copy(ctx_ref, ctx_s, io_sems.at[1])
        load_x.start()
        load_ctx.start()
        for phase in g_cond.phases:
            phase([cond_ref], [g0_ref], s_cond)
        big.phases[0]([win_ref], [gin_ref], s_big)
        cc_ref[...] = jnp.zeros_like(cc_ref)
        for j in range(N_DEV):
            cc_ref[j:j + 1, :] = g0_ref[j, 0:1, :]
        cc_ref[N_DEV:N_DEV + 1, :] = cctx_ref[...]
        cv = cc_ref[...]
        rows_ref[...] = _dot_f32(cv * _sigmoid(cv), wada_ref[...]) + b_ref[...]
        for phase in g_mod.phases:
            phase([rows_ref], [g1_ref], s_mod)
        big.phases[1]([win_ref], [gin_ref], s_big)
        x_pos, y_pos, c_pos = _my_pos()
        me = 4 * x_pos + 2 * y_pos + c_pos
        mine = jnp.concatenate([g1_ref[j, pl.ds(me, 1), :] for j in range(N_DEV)], axis=1)
        shared = jnp.concatenate([g1_ref[j, N_DEV:N_DEV + 1, :] for j in range(N_DEV)], axis=1)
        for k in range(6):
            mod_ref[k:k + 1, :] = mine[:, k * D:(k + 1) * D]
        mod_ref[6:7, :] = shared[:, 0:D]
        mod_ref[7:8, :] = shared[:, D:2 * D]
        load_ctx.wait()
        load_x.wait()
        h_s[pl.ds(0, L), :] = _rms_mod(ctx_s[...], nw_ref[...], mod_ref[6:7, :], mod_ref[7:8, :]).astype(BF16)

        def norm_tile(i, carry):
            r0 = pl.multiple_of(i * TM, TM)
            h_s[pl.ds(L + r0, TM), :] = _rms_mod(
                x_s[pl.ds(r0, TM), :], nw_ref[...], mod_ref[0:1, :], mod_ref[1:2, :]).astype(BF16)
            return carry

        lax.fori_loop(0, N_LAT_TILES, norm_tile, 0)
        stores = [pltpu.make_async_copy(h_s, h_ref, io_sems.at[2]),
                  pltpu.make_async_copy(h_s.at[pl.ds(L, S)], hl_ref, io_sems.at[3])]
        for cp in stores:
            cp.start()
        big.phases[2]([win_ref], [gin_ref], s_big)
        for cp in stores:
            cp.wait()

    return pl.pallas_call(
        body, name="prologue",
        in_specs=[_VMEM_WHOLE] * 5 + [_ANY] * 3, out_specs=[_VMEM_WHOLE] * 3 + [_ANY] * 3,
        out_shape=[g_cond.out_shapes[0], jax.ShapeDtypeStruct((16, D), F32), jax.ShapeDtypeStruct((8, D), F32),
                   big.out_shapes[0], jax.ShapeDtypeStruct((T, D), BF16), jax.ShapeDtypeStruct((S, D), BF16)],
        scratch_shapes=[pltpu.VMEM((16, ADA_BLK), F32), pltpu.VMEM((N_DEV, 16, ADA_BLK), F32),
                        pltpu.VMEM((S, D), F32), pltpu.VMEM((L, D), F32), pltpu.VMEM((T, D), BF16),
                        pltpu.SemaphoreType.DMA((4,))] + big.sems + g_cond.sems + g_mod.sems,
        compiler_params=_cp(),
    )(cond, c_ctx, w_ada, b_cols, nw, w_in_t, x, ctx)


def _ada_grads(cc, dm_cols, w_ada):
    def body(c_ref, dm_ref, w_ref, gw_ref, dsc_ref):
        cv = c_ref[...]
        sc = cv * _sigmoid(cv)
        dm = dm_ref[...]
        gw_ref[...] = _dot_f32(sc, dm, dot=_dot_tn)
        dsc_ref[...] = _dot_f32(dm, w_ref[...], dot=_dot_nt)

    return pl.pallas_call(
        body, name="ada_grads",
        in_specs=[_VMEM_WHOLE] * 3, out_specs=[_VMEM_WHOLE] * 2,
        out_shape=[jax.ShapeDtypeStruct((D, ADA_BLK), F32), jax.ShapeDtypeStruct((16, D), F32)],
        compiler_params=_cp(),
    )(cc, dm_cols, w_ada)


def _lat(i):
    return jnp.maximum(i - 1, 0)


def _rms_mod(xv, nw, sh, sc):
    rstd = lax.rsqrt(jnp.mean(xv * xv, axis=-1, keepdims=True) + EPS)
    return (xv * rstd * nw) * (1.0 + sc) + sh


def _rms_mod_bwd(xv, nw, sc, dh):
    rstd = lax.rsqrt(jnp.mean(xv * xv, axis=-1, keepdims=True) + EPS)
    xhat = xv * rstd
    dn = dh * (1.0 + sc)
    dxhat = dn * nw
    dx = rstd * (dxhat - xhat * jnp.mean(dxhat * xhat, axis=-1, keepdims=True))
    return (dx, jnp.sum(dh, axis=0, keepdims=True), jnp.sum(dh * (xhat * nw), axis=0, keepdims=True),
            jnp.sum(dn * xhat, axis=0, keepdims=True))


def _chunk_masks(reverse):
    row = lax.broadcasted_iota(jnp.int32, (TM, TM), 0)
    col = lax.broadcasted_iota(jnp.int32, (TM, TM), 1)
    same = (row // CHUNK) == (col // CHUNK)
    tri = same & ((col >= row) if reverse else (col <= row))
    return same, tri


def _chunk_order(i, reverse):
    if not reverse:
        return i
    return jnp.where(i < N_CTX_CHUNKS, N_CTX_CHUNKS - 1 - i, N_CHUNKS + N_CTX_CHUNKS - 1 - i)


def _decay_terms(z, lb, same01, tri01):
    f = lb + (1.0 - lb) * _sigmoid(z)
    g = jnp.log(f)
    g2 = jnp.concatenate(_split2(g), axis=1)
    b2 = _dot(tri01, g2)
    t2 = _dot(same01, g2)
    return f, 1.0 - f, b2[:, :HG_DIM] + b2[:, HG_DIM:], t2[:, :HG_DIM] + t2[:, HG_DIM:]


def _chunk_outer(a, b):
    n = TM // CHUNK
    return jnp.einsum('ncv,nck->nvk', a.reshape(n, CHUNK, HG_DIM), b.reshape(n, CHUNK, HG_DIM),
                      preferred_element_type=F32)


def _hgrn_fwd(p_a, lbl, carried=None):
    cpt = TM // CHUNK

    def body(p_ref, lbl_ref, o_ref, st_ref, qd_s, kd_s, u_s, v_s, ebt_s):
        masks = [_chunk_masks(d == 1) for d in (0, 1)]
        same01 = jnp.where(masks[0][0], 1.0, 0.0).astype(BF16)
        tri = [m[1] for m in masks]
        tri01 = [jnp.where(t, 1.0, 0.0).astype(BF16) for t in tri]
        lb = [_sigmoid(lbl_ref[d][0:1, :] - lbl_ref[d][1:2, :]) for d in (0, 1)]

        def prep(r, carry):
            r0 = pl.multiple_of(r * TM, TM)
            vb = p_ref[pl.ds(r0, TM), 2 * HG_DIM:3 * HG_DIM].astype(BF16)
            v_s[pl.ds(r0, TM), :] = vb
            for d in (0, 1):
                z = p_ref[pl.ds(r0, TM), d * HG_DIM:(d + 1) * HG_DIM]
                _, k, b, bt = _decay_terms(z, lb[d], same01, tri01[d])
                u_s[d, pl.ds(r * cpt, cpt)] = _chunk_outer(vb, (k * jnp.exp(bt - b)).astype(BF16))
                ebt_s[d, pl.ds(r0, TM), :] = jnp.exp(bt)

                @pl.when(r >= 1)
                def _():
                    rl = pl.multiple_of(r0 - L, TM)
                    qr = p_ref[pl.ds(r0, TM), 3 * HG_DIM:4 * HG_DIM]
                    q = qr * _sigmoid(qr) * HG_DIM ** -0.5
                    qd_s[d, pl.ds(rl, TM), :] = (q * jnp.exp(b)).astype(BF16)
                    kd_s[d, pl.ds(rl, TM), :] = (k * jnp.exp(-b)).astype(BF16)

            return carry

        lax.fori_loop(0, N_TILES, prep, 0)

        def scan(i, sts):
            new = []
            for d in (0, 1):
                nn = _chunk_order(i, d == 1)
                c0 = pl.multiple_of(nn * CHUNK, CHUNK)
                st_ref[d, nn] = sts[d].astype(BF16)
                new.append(sts[d] * ebt_s[d, pl.ds(c0, 1), :] + u_s[d, nn])
            return tuple(new)

        zero = jnp.zeros((HG_DIM, HG_DIM), F32)
        lax.fori_loop(0, N_CHUNKS, scan, (zero, zero))

        def outp(r, carry):
            r0 = pl.multiple_of(r * TM, TM)
            vb = v_s[pl.ds(r0 + L, TM), :]
            o = jnp.zeros((TM, HG_DIM), F32)
            for d in (0, 1):
                qd = qd_s[d, pl.ds(r0, TM), :]
                a = jnp.where(tri[d], _dot_nt(qd, kd_s[d, pl.ds(r0, TM), :]), 0.0)
                stb = st_ref[d, pl.ds(N_CTX_CHUNKS + r * cpt, cpt)]
                inter = jnp.einsum('nck,nvk->ncv', qd.reshape(cpt, CHUNK, HG_DIM), stb,
                                   preferred_element_type=F32)
                o = o + _dot(a.astype(BF16), vb) + inter.reshape(TM, HG_DIM)
            o_ref[pl.ds(r0, TM), :] = o
            return carry

        lax.fori_loop(0, N_LAT_TILES, outp, 0, unroll=2)

    return _pcall(
        body, carried, name="hgrn_fwd", grid=(HG_HEADS,),
        in_specs=[pl.BlockSpec((T, 4 * HG_DIM), lambda h: (0, h)),
                  pl.BlockSpec((2, 2, HG_DIM), lambda h: (0, 0, h))],
        out_specs=[pl.BlockSpec((S, HG_DIM), lambda h: (0, h)),
                   pl.BlockSpec((2, None, N_CHUNKS, HG_DIM, HG_DIM), lambda h: (0, h, 0, 0, 0))],
        out_shape=[jax.ShapeDtypeStruct((S, HGW), F32),
                   jax.ShapeDtypeStruct((2, HG_HEADS, N_CHUNKS, HG_DIM, HG_DIM), BF16)],
        scratch_shapes=[pltpu.VMEM((2, S, HG_DIM), BF16), pltpu.VMEM((2, S, HG_DIM), BF16),
                        pltpu.VMEM((2, N_CHUNKS, HG_DIM, HG_DIM), F32), pltpu.VMEM((T, HG_DIM), BF16),
                        pltpu.VMEM((2, T, HG_DIM), F32)],
        operands=[p_a, lbl])


def _hgrn_bwd(p_a, lbl, d_o, st, carried=None):
    cpt = TM // CHUNK

    def rows(r):
        return r * TM if isinstance(r, int) else pl.multiple_of(r * TM, TM)

    def body(p_ref, lbl_ref, do_ref, st_ref, dp_ref, dlb_ref, b_s, bt_s, dbt_s, qd_s, dst_s, w_s):
        masks = [_chunk_masks(d == 1) for d in (0, 1)]
        same01 = jnp.where(masks[0][0], 1.0, 0.0).astype(BF16)
        tri = [m[1] for m in masks]
        tri01 = [jnp.where(t, 1.0, 0.0).astype(BF16) for t in tri]
        later01 = [tri01[1], tri01[0]]
        lb = [_sigmoid(lbl_ref[d][0:1, :] - lbl_ref[d][1:2, :]) for d in (0, 1)]

        def prep_tile(r, latent):
            r0 = rows(r)
            for d in (0, 1):
                z = p_ref[pl.ds(r0, TM), d * HG_DIM:(d + 1) * HG_DIM]
                _, _, b, bt = _decay_terms(z, lb[d], same01, tri01[d])
                b_s[d, pl.ds(r0, TM), :] = b
                bt_s[d, pl.ds(r0, TM), :] = bt
                if latent:
                    rl = pl.multiple_of(r0 - L, TM)
                    qr = p_ref[pl.ds(r0, TM), 3 * HG_DIM:4 * HG_DIM]
                    qd = (qr * _sigmoid(qr) * HG_DIM ** -0.5 * jnp.exp(b)).astype(BF16)
                    qd_s[d, pl.ds(rl, TM), :] = qd
                    w_s[d, pl.ds(r * cpt, cpt)] = _chunk_outer(
                        do_ref[pl.ds(rl, TM), :].astype(BF16), qd).astype(BF16)

        prep_tile(0, False)
        w_s[:, pl.ds(0, N_CTX_CHUNKS)] = jnp.zeros((2, N_CTX_CHUNKS, HG_DIM, HG_DIM), BF16)

        def prep(r, carry):
            prep_tile(r, True)
            return carry

        lax.fori_loop(1, N_TILES, prep, 0, unroll=2)

        def rscan(j, dsts):
            i = N_CHUNKS - 1 - j
            new = []
            for d in (0, 1):
                nn = _chunk_order(i, d == 1)
                c0 = pl.multiple_of(nn * CHUNK, CHUNK)
                dst_s[d, nn] = dsts[d].astype(BF16)
                after = st_ref[d, _chunk_order(jnp.minimum(i + 1, N_CHUNKS - 1), d == 1)].astype(F32)
                dbt_s[d, pl.ds(c0, CHUNK), :] = jnp.broadcast_to(
                    jnp.sum(after * dsts[d], axis=0, keepdims=True), (CHUNK, HG_DIM))
                new.append(dsts[d] * jnp.exp(bt_s[d, pl.ds(c0, 1), :]) + w_s[d, nn].astype(F32))
            return tuple(new)

        zero = jnp.zeros((HG_DIM, HG_DIM), F32)
        lax.fori_loop(0, N_CHUNKS, rscan, (zero, zero))

        def grad_tile(r, latent):
            r0 = rows(r)
            vb = p_ref[pl.ds(r0, TM), 2 * HG_DIM:3 * HG_DIM].astype(BF16)
            dv = jnp.zeros((TM, HG_DIM), F32)
            dq = jnp.zeros((TM, HG_DIM), F32)
            dlbs = []
            if latent:
                rl = pl.multiple_of(r0 - L, TM)
                qr = p_ref[pl.ds(r0, TM), 3 * HG_DIM:4 * HG_DIM]
                sq = _sigmoid(qr)
                do = do_ref[pl.ds(rl, TM), :].astype(BF16)
                da_full = _dot_nt(do, vb)
            for d in (0, 1):
                z = p_ref[pl.ds(r0, TM), d * HG_DIM:(d + 1) * HG_DIM]
                sz = _sigmoid(z)
                f = lb[d] + (1.0 - lb[d]) * sz
                k = 1.0 - f
                b = b_s[d, pl.ds(r0, TM), :]
                e2 = jnp.exp(bt_s[d, pl.ds(r0, TM), :] - b)
                dstb = dst_s[d, pl.ds(r * cpt, cpt)]
                kd2 = k * e2
                dkd2 = jnp.einsum('ncv,nvk->nck', vb.reshape(cpt, CHUNK, HG_DIM), dstb,
                                  preferred_element_type=F32).reshape(TM, HG_DIM)
                dv = dv + jnp.einsum('nck,nvk->ncv', kd2.astype(BF16).reshape(cpt, CHUNK, HG_DIM), dstb,
                                     preferred_element_type=F32).reshape(TM, HG_DIM)
                dk = dkd2 * e2
                db = -(kd2 * dkd2)
                if latent:
                    eb = jnp.exp(b)
                    enb = jnp.exp(-b)
                    qdf = qr * sq * HG_DIM ** -0.5 * eb
                    kdf = k * enb
                    qd = qd_s[d, pl.ds(rl, TM), :]
                    kd = kdf.astype(BF16)
                    a = jnp.where(tri[d], _dot_nt(qd, kd), 0.0).astype(BF16)
                    da = jnp.where(tri[d], da_full, 0.0).astype(BF16)
                    stb = st_ref[d, pl.ds(r * cpt, cpt)]
                    dqd = _dot(da, kd) + jnp.einsum(
                        'ncv,nvk->nck', do.reshape(cpt, CHUNK, HG_DIM), stb,
                        preferred_element_type=F32).reshape(TM, HG_DIM)
                    dkd = _dot_tn(da, qd)
                    dv = dv + _dot_tn(a, do)
                    dk = dk + dkd * enb
                    db = db + qdf * dqd - kdf * dkd
                    dq = dq + dqd * eb
                dg = _dot_lhs01(later01[d], db) + dbt_s[d, pl.ds(r0, TM), :]
                df = dg / f - dk
                dp_ref[pl.ds(r0, TM), d * HG_DIM:(d + 1) * HG_DIM] = (
                    df * (1.0 - lb[d]) * sz * (1.0 - sz)).astype(BF16)
                dlbs.append(jnp.sum(df * (1.0 - sz), axis=0, keepdims=True))
            dp_ref[pl.ds(r0, TM), 2 * HG_DIM:3 * HG_DIM] = dv.astype(BF16)
            if latent:
                dq = dq * (HG_DIM ** -0.5) * (sq * (1.0 + qr * (1.0 - sq)))
            dp_ref[pl.ds(r0, TM), 3 * HG_DIM:4 * HG_DIM] = dq.astype(BF16)
            return dlbs

        dlb_ctx = grad_tile(0, False)

        def grads(r, acc):
            t = grad_tile(r, True)
            return (acc[0] + t[0], acc[1] + t[1])

        dlb = lax.fori_loop(1, N_TILES, grads, (dlb_ctx[0], dlb_ctx[1]))
        dlb_ref[0:1, :] = dlb[0]
        dlb_ref[1:2, :] = dlb[1]

    return _pcall(
        body, carried, name="hgrn_bwd", grid=(HG_HEADS,),
        in_specs=[pl.BlockSpec((T, 4 * HG_DIM), lambda h: (0, h)),
                  pl.BlockSpec((2, 2, HG_DIM), lambda h: (0, 0, h)),
                  pl.BlockSpec((S, HG_DIM), lambda h: (0, h)),
                  pl.BlockSpec((2, None, N_CHUNKS, HG_DIM, HG_DIM), lambda h: (0, h, 0, 0, 0))],
        out_specs=[pl.BlockSpec((T, 4 * HG_DIM), lambda h: (0, h)),
                   pl.BlockSpec((2, HG_DIM), lambda h: (0, h))],
        out_shape=[jax.ShapeDtypeStruct((T, WA), BF16), jax.ShapeDtypeStruct((2, HGW), F32)],
        scratch_shapes=[pltpu.VMEM((2, T, HG_DIM), F32), pltpu.VMEM((2, T, HG_DIM), F32),
                        pltpu.VMEM((2, T, HG_DIM), F32), pltpu.VMEM((2, S, HG_DIM), BF16),
                        pltpu.VMEM((2, N_CHUNKS, HG_DIM, HG_DIM), BF16),
                        pltpu.VMEM((2, N_CHUNKS, HG_DIM, HG_DIM), BF16)],
        operands=[p_a, lbl, d_o, st])


def _rope_tables():
    t = np.arange(S)
    inv = ROPE_THETA ** (-np.arange(0, 32, 2, dtype=np.float64) / 32)
    lane = np.arange(64)
    pos = np.where(lane[None, :] < 32, (t // GRID_W)[:, None], (t % GRID_W)[:, None]).astype(np.float64)
    ang = pos * inv[(lane % 32) % 16][None, :]
    sign = np.where((lane % 32) < 16, -1.0, 1.0)[None, :]
    cos = np.tile(np.cos(ang), (1, 2)).astype(np.float32)
    sin = np.tile(np.sin(ang) * sign, (1, 2)).astype(np.float32)
    return jnp.asarray(cos), jnp.asarray(sin)


def _rope_partner(v):
    lane = lax.broadcasted_iota(jnp.int32, (1, 128), 1)
    first = (lane % 32) < 16
    slabs = []
    for j in range(v.shape[1] // 128):
        s = v[:, 128 * j:128 * (j + 1)]
        slabs.append(jnp.where(first, pltpu.roll(s, 112, 1), pltpu.roll(s, 16, 1)))
    return slabs[0] if len(slabs) == 1 else jnp.concatenate(slabs, axis=1)


def _group_ones(width, group):
    r = lax.broadcasted_iota(jnp.int32, (width, width), 0)
    c = lax.broadcasted_iota(jnp.int32, (width, width), 1)
    return jnp.where((r // group) == (c // group), 1.0, 0.0).astype(BF16)


def _group_mean(v, ones01, group):
    hi = v.astype(BF16)
    lo = (v - hi.astype(F32)).astype(BF16)
    return (_dot(hi, ones01) + _dot(lo, ones01)) * (1.0 / group)


def _rep_matrix():
    r = lax.broadcasted_iota(jnp.int32, (KVW, ATW), 0)
    c = lax.broadcasted_iota(jnp.int32, (KVW, ATW), 1)
    return jnp.where(r == HEAD_DIM * (c // 256) + c % HEAD_DIM, 1.0, 0.0).astype(BF16)


def _tile_lanes(v, reps):
    return jnp.concatenate([v] * reps, axis=1)


def _prep_fwd(p_b, o, cos, sin, hnw, qnw, knw):
    def body(p_ref, o_ref, cos_ref, sin_ref, hnw_ref, qnw_ref, knw_ref, y_ref, q_ref, k_ref, v_ref):
        i = pl.program_id(0)
        rep = _rep_matrix()
        ones_k = _group_ones(KVW, HEAD_DIM)
        kr = p_ref[:, 1024:1152]
        krstd = lax.rsqrt(_group_mean(kr * kr, ones_k, HEAD_DIM) + EPS)
        kn = kr * krstd * knw_ref[...]
        v_ref[...] = _dot(p_ref[:, 1152:1280].astype(BF16), rep).astype(BF16)

        @pl.when(i == 0)
        def _():
            k_ref[...] = _dot(kn.astype(BF16), rep).astype(BF16)

        @pl.when(i > 0)
        def _():
            cs, sn = cos_ref[...], sin_ref[...]
            kro = kn * cs + _rope_partner(kn) * sn
            k_ref[...] = _dot(kro.astype(BF16), rep).astype(BF16)
            qr = p_ref[:, 512:1024]
            qrstd = lax.rsqrt(_group_mean(qr * qr, _group_ones(ATW, HEAD_DIM), HEAD_DIM) + EPS)
            qn = qr * qrstd * qnw_ref[...]
            qro = qn * _tile_lanes(cs, 4) + _rope_partner(qn) * _tile_lanes(sn, 4)
            q_ref[...] = (qro * HEAD_DIM ** -0.5).astype(BF16)
            ys = []
            for h in range(HG_HEADS):
                oh = o_ref[:, HG_DIM * h:HG_DIM * (h + 1)]
                gh = p_ref[:, HG_DIM * h:HG_DIM * (h + 1)]
                rstd = lax.rsqrt(jnp.mean(oh * oh, axis=-1, keepdims=True) + EPS)
                ys.append(oh * rstd * hnw_ref[...] * (gh * _sigmoid(gh)))
            y_ref[...] = jnp.concatenate(ys, axis=1).astype(BF16)

    return pl.pallas_call(
        body, name="prep_fwd", grid=(N_TILES,),
        in_specs=[pl.BlockSpec((TM, WB), lambda i: (i, 0)),
                  pl.BlockSpec((TM, HGW), lambda i: (_lat(i), 0)),
                  pl.BlockSpec((TM, 128), lambda i: (_lat(i), 0)),
                  pl.BlockSpec((TM, 128), lambda i: (_lat(i), 0)),
                  _full((1, HG_DIM)), _full((1, ATW)), _full((1, KVW))],
        out_specs=[pl.BlockSpec((TM, HGW), lambda i: (_lat(i), 0)),
                   pl.BlockSpec((TM, ATW), lambda i: (_lat(i), 0)),
                   pl.BlockSpec((TM, ATW), lambda i: (i, 0)),
                   pl.BlockSpec((TM, ATW), lambda i: (i, 0))],
        out_shape=[jax.ShapeDtypeStruct((S, HGW), BF16), jax.ShapeDtypeStruct((S, ATW), BF16),
                   jax.ShapeDtypeStruct((T, ATW), BF16), jax.ShapeDtypeStruct((T, ATW), BF16)],
        compiler_params=_cp(("arbitrary",)),
    )(p_b, o, cos, sin, hnw, qnw, knw)


def _prep_bwd(p_b, o, cos, sin, hnw, qnw, knw, dy_hg, dq, dk_rep, dv_rep, carried=None):
    def body(p_ref, o_ref, cos_ref, sin_ref, hnw_ref, qnw_ref, knw_ref, dy_ref, dq_ref, dk_ref, dv_ref,
             dp_ref, do_ref, acc_ref):
        i = pl.program_id(0)

        @pl.when(i == 0)
        def _():
            acc_ref[...] = jnp.zeros_like(acc_ref)

        rep = _rep_matrix()
        ones_k = _group_ones(KVW, HEAD_DIM)

        def fold(v):
            hi = v.astype(BF16)
            lo = (v - hi.astype(F32)).astype(BF16)
            return _dot_nt(hi, rep) + _dot_nt(lo, rep)

        kr = p_ref[:, 1024:1152]
        krstd = lax.rsqrt(_group_mean(kr * kr, ones_k, HEAD_DIM) + EPS)
        khat = kr * krstd
        kw = knw_ref[...]
        dkro = fold(dk_ref[...])
        dv = fold(dv_ref[...])

        def k_back(dkn):
            dkhat = dkn * kw
            dkr = krstd * (dkhat - khat * _group_mean(dkhat * khat, ones_k, HEAD_DIM))
            acc_ref[2:3, 0:KVW] += jnp.sum(dkn * khat, axis=0, keepdims=True)
            dp_ref[:, 1024:1152] = dkr.astype(BF16)
            dp_ref[:, 1152:1280] = dv.astype(BF16)

        @pl.when(i == 0)
        def _():
            k_back(dkro)
            dp_ref[:, 0:1024] = jnp.zeros((TM, 1024), BF16)

        @pl.when(i > 0)
        def _():
            cs, sn = cos_ref[...], sin_ref[...]
            k_back(dkro * cs + _rope_partner(dkro * sn))
            ones_q = _group_ones(ATW, HEAD_DIM)
            qr = p_ref[:, 512:1024]
            qrstd = lax.rsqrt(_group_mean(qr * qr, ones_q, HEAD_DIM) + EPS)
            qhat = qr * qrstd
            dqro = dq_ref[...] * HEAD_DIM ** -0.5
            dqn = dqro * _tile_lanes(cs, 4) + _rope_partner(dqro * _tile_lanes(sn, 4))
            dqhat = dqn * qnw_ref[...]
            dqr = qrstd * (dqhat - qhat * _group_mean(dqhat * qhat, ones_q, HEAD_DIM))
            acc_ref[1:2, :] += jnp.sum(dqn * qhat, axis=0, keepdims=True)
            dp_ref[:, 512:1024] = dqr.astype(BF16)
            dws = jnp.zeros((1, HG_DIM), F32)
            for h in range(HG_HEADS):
                sl = slice(HG_DIM * h, HG_DIM * (h + 1))
                oh, gh, dy = o_ref[:, sl], p_ref[:, sl], dy_ref[:, sl]
                rstd = lax.rsqrt(jnp.mean(oh * oh, axis=-1, keepdims=True) + EPS)
                ohat = oh * rstd
                sg = _sigmoid(gh)
                dp_ref[:, sl] = (dy * (ohat * hnw_ref[...]) * (sg * (1.0 + gh * (1.0 - sg)))).astype(BF16)
                dn = dy * (gh * sg)
                dws = dws + jnp.sum(dn * ohat, axis=0, keepdims=True)
                dohat = dn * hnw_ref[...]
                do_ref[:, sl] = rstd * (dohat - ohat * jnp.mean(dohat * ohat, axis=-1, keepdims=True))
            acc_ref[0:1, 0:HG_DIM] += dws

    return _pcall(
        body, carried, name="prep_bwd", grid=(N_TILES,),
        in_specs=[pl.BlockSpec((TM, WB), lambda i: (i, 0)),
                  pl.BlockSpec((TM, HGW), lambda i: (_lat(i), 0)),
                  pl.BlockSpec((TM, 128), lambda i: (_lat(i), 0)),
                  pl.BlockSpec((TM, 128), lambda i: (_lat(i), 0)),
                  _full((1, HG_DIM)), _full((1, ATW)), _full((1, KVW)),
                  pl.BlockSpec((TM, HGW), lambda i: (_lat(i), 0)),
                  pl.BlockSpec((TM, ATW), lambda i: (_lat(i), 0)),
                  pl.BlockSpec((TM, ATW), lambda i: (i, 0)),
                  pl.BlockSpec((TM, ATW), lambda i: (i, 0))],
        out_specs=[pl.BlockSpec((TM, WB), lambda i: (i, 0)),
                   pl.BlockSpec((TM, HGW), lambda i: (_lat(i), 0)),
                   _full((8, ATW))],
        out_shape=[jax.ShapeDtypeStruct((T, WB), BF16), jax.ShapeDtypeStruct((S, HGW), F32),
                   jax.ShapeDtypeStruct((8, ATW), F32)],
        scratch_shapes=[], operands=[p_b, o, cos, sin, hnw, qnw, knw, dy_hg, dq, dk_rep, dv_rep])


NEG = -1e30
_CTX_BLOCKS = L // BLOCK


def _attn_window_specs():
    prev = pl.BlockSpec((BLOCK, ATW), lambda i: (jnp.maximum(i - 1, 0) + _CTX_BLOCKS, 0))
    own = pl.BlockSpec((BLOCK, ATW), lambda i: (i + _CTX_BLOCKS, 0))
    nxt = pl.BlockSpec((BLOCK, ATW), lambda i: (jnp.minimum(i + 1, N_BLOCKS - 1) + _CTX_BLOCKS, 0))
    return [prev, own, nxt, _full((L, ATW))]


def _attn_valid(i, heads, context):
    n_keys = 3 * BLOCK + (L if context else 0)
    qi = lax.broadcasted_iota(jnp.int32, (heads * BLOCK, n_keys), 0) % BLOCK
    kj = lax.broadcasted_iota(jnp.int32, (heads * BLOCK, n_keys), 1)
    window = ((jnp.abs(kj - BLOCK - qi) <= BLOCK) & ((kj >= BLOCK) | (i > 0))
              & ((kj < 2 * BLOCK) | (i < N_BLOCKS - 1)))
    return window | (kj >= 3 * BLOCK)


def _stack_heads(qg):
    lane = lax.broadcasted_iota(jnp.int32, (1, 256), 1) // HEAD_DIM
    return jnp.concatenate([jnp.where(lane == g, qg, jnp.zeros_like(qg)) for g in range(4)], axis=0)


def _unstack_heads(v4):
    lane = lax.broadcasted_iota(jnp.int32, (1, 256), 1) // HEAD_DIM
    out = jnp.where(lane == 0, v4[0:BLOCK], 0.0)
    for g in range(1, 4):
        out = out + jnp.where(lane == g, v4[g * BLOCK:(g + 1) * BLOCK], 0.0)
    return out


def _sink_rows(sink_ref, hk):
    return jnp.concatenate(
        [jnp.broadcast_to(sink_ref[0:1, 4 * hk + g:4 * hk + g + 1], (BLOCK, 1)) for g in range(4)], axis=0)


def _attn_fwd(q, k_rep, v_rep, sinks, carried=None):
    def body(q_ref, kp, ko, kn, kc, vp, vo, vn, vc, sink_ref, y_ref, lse_ref):
        i = pl.program_id(0)
        valid = _attn_valid(i, 1, True)
        lane8 = lax.broadcasted_iota(jnp.int32, (1, ATT_HEADS), 1)
        head_of_lane = lax.broadcasted_iota(jnp.int32, (1, 256), 1) // HEAD_DIM
        lse_out = jnp.zeros((BLOCK, ATT_HEADS), F32)
        for hk in range(KV_HEADS):
            sl = slice(256 * hk, 256 * (hk + 1))
            qg = q_ref[:, sl]
            keys = jnp.concatenate([kp[:, sl], ko[:, sl], kn[:, sl], kc[:, sl]], axis=0)
            vals = jnp.concatenate([vp[:, sl], vo[:, sl], vn[:, sl], vc[:, sl]], axis=0)
            yg = jnp.zeros((BLOCK, 256), F32)
            for g in range(4):
                q1 = jnp.where(head_of_lane == g, qg, jnp.zeros_like(qg))
                s = jnp.where(valid, _dot_nt(q1, keys), NEG)
                sink = sink_ref[0:1, 4 * hk + g:4 * hk + g + 1]
                m = jnp.maximum(jnp.max(s, axis=1, keepdims=True), sink)
                p = jnp.exp(s - m)
                den = jnp.sum(p, axis=1, keepdims=True) + jnp.exp(sink - m)
                o1 = _dot(p.astype(BF16), vals) * (1.0 / den)
                yg = yg + jnp.where(head_of_lane == g, o1, 0.0)
                lse_out = lse_out + jnp.where(lane8 == 4 * hk + g, m + jnp.log(den), 0.0)
            y_ref[:, sl] = yg.astype(BF16)
        lse_ref[...] = lse_out

    return _pcall(
        body, carried, name="attn_fwd", grid=(N_BLOCKS,),
        in_specs=[pl.BlockSpec((BLOCK, ATW), lambda i: (i, 0))] + _attn_window_specs()
        + _attn_window_specs() + [_full((1, ATT_HEADS))],
        out_specs=[pl.BlockSpec((BLOCK, ATW), lambda i: (i, 0)),
                   pl.BlockSpec((BLOCK, ATT_HEADS), lambda i: (i, 0))],
        out_shape=[jax.ShapeDtypeStruct((S, ATW), BF16), jax.ShapeDtypeStruct((S, ATT_HEADS), F32)],
        scratch_shapes=[],
        operands=[q, k_rep, k_rep, k_rep, k_rep, v_rep, v_rep, v_rep, v_rep, sinks])


def _attn_bwd(q, k_rep, v_rep, sinks, y_at, lse, dy, carried=None):
    def body(q_ref, kp, ko, kn, kc, vp, vo, vn, vc, sink_ref, y_ref, lse_ref, dy_ref,
             dq_ref, dk_ref, dv_ref, dsink_ref, dk_acc, dv_acc):
        i = pl.program_id(0)

        @pl.when(i == 0)
        def _():
            dk_acc[...] = jnp.zeros_like(dk_acc)
            dv_acc[...] = jnp.zeros_like(dv_acc)
            dk_ref[pl.ds(0, L), :] = jnp.zeros((L, ATW), F32)
            dv_ref[pl.ds(0, L), :] = jnp.zeros((L, ATW), F32)
            dsink_ref[...] = jnp.zeros_like(dsink_ref)

        valid = _attn_valid(i, 4, False)
        lane8 = lax.broadcasted_iota(jnp.int32, (1, ATT_HEADS), 1)
        w0 = pl.multiple_of(i * BLOCK, BLOCK)
        dsink = jnp.zeros((1, ATT_HEADS), F32)
        for hk in range(KV_HEADS):
            sl = slice(256 * hk, 256 * (hk + 1))
            q4 = _stack_heads(q_ref[:, sl])
            do4f = _stack_heads(dy_ref[:, sl])
            o4 = _stack_heads(y_ref[:, sl]).astype(F32)
            do4 = do4f.astype(BF16)
            kl = jnp.concatenate([kp[:, sl], ko[:, sl], kn[:, sl]], axis=0)
            vl = jnp.concatenate([vp[:, sl], vo[:, sl], vn[:, sl]], axis=0)
            lse4 = jnp.concatenate(
                [jnp.sum(jnp.where(lane8 == 4 * hk + g, lse_ref[...], 0.0), axis=1, keepdims=True)
                 for g in range(4)], axis=0)
            p_loc = jnp.where(valid, jnp.exp(_dot_nt(q4, kl) - lse4), 0.0)
            p_ctx = jnp.exp(_dot_nt(q4, kc[:, sl]) - lse4)
            delta = jnp.sum(do4f * o4, axis=1, keepdims=True)
            ds_loc = (p_loc * (_dot_nt(do4, vl) - delta)).astype(BF16)
            ds_ctx = (p_ctx * (_dot_nt(do4, vc[:, sl]) - delta)).astype(BF16)
            dq_ref[:, sl] = _unstack_heads(_dot(ds_loc, kl) + _dot(ds_ctx, kc[:, sl]))
            dk_acc[pl.ds(w0, 3 * BLOCK), sl] += _dot_tn(ds_loc, q4)
            dv_acc[pl.ds(w0, 3 * BLOCK), sl] += _dot_tn(p_loc.astype(BF16), do4)
            dk_ref[pl.ds(0, L), sl] += _dot_tn(ds_ctx, q4)
            dv_ref[pl.ds(0, L), sl] += _dot_tn(p_ctx.astype(BF16), do4)
            p_sink = jnp.exp(_sink_rows(sink_ref, hk) - lse4)
            for g in range(4):
                rows = slice(g * BLOCK, (g + 1) * BLOCK)
                dsink = dsink + jnp.where(lane8 == 4 * hk + g,
                                          -jnp.sum(p_sink[rows] * delta[rows], axis=0, keepdims=True), 0.0)
        dsink_ref[...] += dsink

        @pl.when(i == N_BLOCKS - 1)
        def _():
            dk_ref[pl.ds(L, S), :] = dk_acc[pl.ds(BLOCK, S), :]
            dv_ref[pl.ds(L, S), :] = dv_acc[pl.ds(BLOCK, S), :]

    row_q = pl.BlockSpec((BLOCK, ATW), lambda i: (i, 0))
    return _pcall(
        body, carried, name="attn_bwd", grid=(N_BLOCKS,),
        in_specs=[row_q] + _attn_window_specs() + _attn_window_specs()
        + [_full((1, ATT_HEADS)), row_q, pl.BlockSpec((BLOCK, ATT_HEADS), lambda i: (i, 0)), row_q],
        out_specs=[row_q, _full((T, ATW)), _full((T, ATW)), _full((1, ATT_HEADS))],
        out_shape=[jax.ShapeDtypeStruct((S, ATW), F32), jax.ShapeDtypeStruct((T, ATW), F32),
                   jax.ShapeDtypeStruct((T, ATW), F32), jax.ShapeDtypeStruct((1, ATT_HEADS), F32)],
        scratch_shapes=[pltpu.VMEM((S + 2 * BLOCK, ATW), F32), pltpu.VMEM((S + 2 * BLOCK, ATW), F32)],
        operands=[q, k_rep, k_rep, k_rep, k_rep, v_rep, v_rep, v_rep, v_rep, sinks, y_at, lse, dy])


def _merge_fwd(y_hg, y_at, p_c, x, w_bh, w_ba, w_out, g1, nfw, sh2, sc2, carried=None):
    def body(yh_ref, ya_ref, g_ref, x_ref, wbh_ref, wba_ref, wo_ref, g1_ref, nfw_ref, sh_ref, sc_ref,
             mx_ref, r_ref, x1_ref, h2_ref):
        a = _dot_nt(yh_ref[...], wbh_ref[...])
        b = _dot_nt(ya_ref[...], wba_ref[...])
        mixed = (_sigmoid(g_ref[:, :D]) * a + _sigmoid(g_ref[:, D:]) * b).astype(BF16)
        r = _dot(mixed, wo_ref[...])
        x1 = x_ref[...] + g1_ref[...] * r
        mx_ref[...] = mixed
        r_ref[...] = r
        x1_ref[...] = x1
        h2_ref[...] = _rms_mod(x1, nfw_ref[...], sh_ref[...], sc_ref[...]).astype(BF16)

    row = lambda w: pl.BlockSpec((TM, w), lambda i: (i, 0))
    vec = _full((1, D))
    return _pcall(
        body, carried, name="merge_fwd", grid=(N_LAT_TILES,),
        in_specs=[row(HGW), row(ATW), row(WC), row(D), _VMEM_WHOLE, _VMEM_WHOLE, _VMEM_WHOLE,
                  vec, vec, vec, vec],
        out_specs=[row(D)] * 4,
        out_shape=[jax.ShapeDtypeStruct((S, D), dt) for dt in (BF16, F32, F32, BF16)],
        scratch_shapes=[], operands=[y_hg, y_at, p_c, x, w_bh, w_ba, w_out, g1, nfw, sh2, sc2])


def _merge_bwd(dx1, r, y_hg, y_at, p_c, w_bh, w_ba, w_out, g1, carried=None):
    def body(dx_ref, r_ref, yh_ref, ya_ref, g_ref, wbh_ref, wba_ref, wo_ref, g1_ref,
             dr_ref, da_ref, db_ref, dg_ref, dyh_ref, dya_ref, acc_ref):
        @pl.when(pl.program_id(0) == 0)
        def _():
            acc_ref[...] = jnp.zeros_like(acc_ref)

        dx1v = dx_ref[...]
        acc_ref[0:1, :] += jnp.sum(dx1v * r_ref[...], axis=0, keepdims=True)
        dr = (g1_ref[...] * dx1v).astype(BF16)
        dr_ref[...] = dr
        dmix = _dot_nt(dr, wo_ref[...])
        sh, sa = _sigmoid(g_ref[:, :D]), _sigmoid(g_ref[:, D:])
        da = (dmix * sh).astype(BF16)
        db = (dmix * sa).astype(BF16)
        da_ref[...] = da
        db_ref[...] = db
        dg_ref[:, :D] = (dmix * _dot_nt(yh_ref[...], wbh_ref[...]) * sh * (1.0 - sh)).astype(BF16)
        dg_ref[:, D:] = (dmix * _dot_nt(ya_ref[...], wba_ref[...]) * sa * (1.0 - sa)).astype(BF16)
        dyh_ref[...] = _dot(da, wbh_ref[...])
        dya_ref[...] = _dot(db, wba_ref[...])

    row = lambda w: pl.BlockSpec((TM, w), lambda i: (i, 0))
    return _pcall(
        body, carried, name="merge_bwd", grid=(N_LAT_TILES,),
        in_specs=[row(D), row(D), row(HGW), row(ATW), row(WC), _VMEM_WHOLE, _VMEM_WHOLE, _VMEM_WHOLE,
                  _full((1, D))],
        out_specs=[row(D), row(D), row(D), row(WC), row(HGW), row(ATW), _full((8, D))],
        out_shape=[jax.ShapeDtypeStruct((S, D), BF16), jax.ShapeDtypeStruct((S, D), BF16),
                   jax.ShapeDtypeStruct((S, D), BF16), jax.ShapeDtypeStruct((S, WC), BF16),
                   jax.ShapeDtypeStruct((S, HGW), F32), jax.ShapeDtypeStruct((S, ATW), F32),
                   jax.ShapeDtypeStruct((8, D), F32)],
        scratch_shapes=[], operands=[dx1, r, y_hg, y_at, p_c, w_bh, w_ba, w_out, g1])


def _ffn_fused(x1, h2, tgt, w_gate, w_up, w_down, g2, nfw, sc2):
    def body(x1_ref, h2_ref, t_ref, wg_ref, wu_ref, wd_ref, g2_ref, nfw_ref, sc_ref,
             act_ref, dgt_ref, dup_ref, df_ref, dx_ref, acc_ref, gs, us):
        @pl.when(pl.program_id(0) == 0)
        def _():
            acc_ref[...] = jnp.zeros_like(acc_ref)

        h2 = h2_ref[...]
        whole = lambda w_ref: w_ref[...].reshape(D_FF, D)
        tile = lambda j: slice(j * FF_TILE, (j + 1) * FF_TILE)
        for j in range(N_FF_TILES):
            g = _dot_nt(h2, wg_ref[j])
            u = _dot_nt(h2, wu_ref[j])
            gs[j] = g
            us[j] = u
            act_ref[:, tile(j)] = (g * _sigmoid(g) * u).astype(BF16)
        f = _dot(act_ref[...], whole(wd_ref))
        x1v = x1_ref[...]
        g2 = g2_ref[...]
        diff = x1v + g2 * f - t_ref[...]
        dy = diff * (1.0 / D)
        df = (g2 * dy).astype(BF16)
        df_ref[...] = df
        dact_all = _dot_nt(df, whole(wd_ref))
        for j in range(N_FF_TILES):
            g, u = gs[j], us[j]
            sg = _sigmoid(g)
            dact = dact_all[:, tile(j)]
            dgt_ref[:, tile(j)] = (dact * u * (sg * (1.0 + g * (1.0 - sg)))).astype(BF16)
            dup_ref[:, tile(j)] = (dact * (g * sg)).astype(BF16)
        dh2 = _dot(dgt_ref[...], whole(wg_ref)) + _dot(dup_ref[...], whole(wu_ref))
        dx, dsh, dsc, dnw = _rms_mod_bwd(x1v, nfw_ref[...], sc_ref[...], dh2)
        dx_ref[...] = dy + dx
        acc_ref[0:1, :] += dsh
        acc_ref[1:2, :] += dsc
        acc_ref[2:3, :] += dnw
        acc_ref[3:4, :] += jnp.sum(dy * f, axis=0, keepdims=True)
        acc_ref[4:5, :] += 0.5 * jnp.sum(jnp.sum(diff * diff, axis=1, keepdims=True), axis=0,
                                         keepdims=True) * (1.0 / D)

    row = lambda dt_w: pl.BlockSpec((TM, dt_w), lambda i: (i, 0))
    blk = row(D_FF)
    vec = _full((1, D))
    return pl.pallas_call(
        body, name="ffn_fused", grid=(N_LAT_TILES,),
        in_specs=[row(D), row(D), row(D), _VMEM_WHOLE, _VMEM_WHOLE, _VMEM_WHOLE, vec, vec, vec],
        out_specs=[blk, blk, blk, row(D), row(D), _full((8, D))],
        out_shape=[jax.ShapeDtypeStruct((S, D_FF), BF16)] * 3
        + [jax.ShapeDtypeStruct((S, D), BF16), jax.ShapeDtypeStruct((S, D), F32),
           jax.ShapeDtypeStruct((8, D), F32)],
        scratch_shapes=[pltpu.VMEM((N_FF_TILES, TM, FF_TILE), F32), pltpu.VMEM((N_FF_TILES, TM, FF_TILE), F32)],
        compiler_params=_cp(("arbitrary",)),
    )(x1, h2, tgt, w_gate, w_up, w_down, g2, nfw, sc2)


def _proj_bc(h_all, w_b, w_c, carried=None):
    def body(h_ref, wb_ref, wc_ref, pb_ref, pc_ref):
        h = h_ref[...]
        pb_ref[...] = _dot_nt(h, wb_ref[...])

        @pl.when(pl.program_id(0) > 0)
        def _():
            pc_ref[...] = _dot_nt(h, wc_ref[...])

    return _pcall(
        body, carried, name="proj_bc", grid=(N_TILES,),
        in_specs=[pl.BlockSpec((TM, D), lambda i: (i, 0)), _VMEM_WHOLE, _VMEM_WHOLE],
        out_specs=[pl.BlockSpec((TM, WB), lambda i: (i, 0)), pl.BlockSpec((TM, WC), lambda i: (_lat(i), 0))],
        out_shape=[jax.ShapeDtypeStruct((T, WB), F32), jax.ShapeDtypeStruct((S, WC), F32)],
        scratch_shapes=[], operands=[h_all, w_b, w_c])


def _input_bwd(dp_a, dp_b, dp_c, w_a, w_b, w_c, ctx, x, dx1, nw, sh, sc, carried=None):
    def body(da_ref, db_ref, dc_ref, wa_ref, wb_ref, wc_ref, ctx_ref, x_ref, dx1_ref, nw_ref, sh_ref,
             sc_ref, gx_ref, acc_ref):
        i = pl.program_id(0)

        @pl.when(i == 0)
        def _():
            acc_ref[...] = jnp.zeros_like(acc_ref)

        dh = _dot(da_ref[...], wa_ref[...]) + _dot(db_ref[...], wb_ref[...])

        @pl.when(i == 0)
        def _():
            _, dsh, dsc, dnw = _rms_mod_bwd(ctx_ref[...], nw_ref[...], sc_ref[0:1, :], dh)
            acc_ref[3:4, :] += dsh
            acc_ref[4:5, :] += dsc
            acc_ref[2:3, :] += dnw

        @pl.when(i > 0)
        def _():
            dhl = dh + _dot(dc_ref[...], wc_ref[...])
            dx, dsh, dsc, dnw = _rms_mod_bwd(x_ref[...], nw_ref[...], sc_ref[1:2, :], dhl)
            gx_ref[...] = dx1_ref[...] + dx
            acc_ref[0:1, :] += dsh
            acc_ref[1:2, :] += dsc
            acc_ref[2:3, :] += dnw

    lat = lambda w: pl.BlockSpec((TM, w), lambda i: (_lat(i), 0))
    return _pcall(
        body, carried, name="input_bwd", grid=(N_TILES,),
        in_specs=[pl.BlockSpec((TM, WA), lambda i: (i, 0)), pl.BlockSpec((TM, WB), lambda i: (i, 0)),
                  lat(WC), _VMEM_WHOLE, _VMEM_WHOLE, _VMEM_WHOLE, _full((TM, D)), lat(D), lat(D),
                  _full((1, D)), _full((2, D)), _full((2, D))],
        out_specs=[lat(D), _full((8, D))],
        out_shape=[jax.ShapeDtypeStruct((S, D), F32), jax.ShapeDtypeStruct((8, D), F32)],
        scratch_shapes=[], operands=[dp_a, dp_b, dp_c, w_a, w_b, w_c, ctx, x, dx1, nw, sh, sc])


_C1 = 1.0 - ADAM_B1 ** ADAM_STEP
_C2 = 1.0 - ADAM_B2 ** ADAM_STEP


def _adamw_math(w, g, m, v):
    m = ADAM_B1 * m + (1.0 - ADAM_B1) * g
    v = ADAM_B2 * v + (1.0 - ADAM_B2) * (g * g)
    m_hat = m / _C1
    v_hat = v / _C2
    delta = -ADAM_LR * (m_hat / (jnp.sqrt(v_hat) + ADAM_EPS) + ADAM_WD * w)
    return delta, m, v


def _adamw_sharded(terms, w, m, v, name, tr, extra=None, after=None):
    rows, cols = w.shape

    def body(*refs):
        t_ref, w_ref, m_ref, v_ref = refs[:4]
        g_ref, d_ref, nm_ref, nv_ref = refs[-4:]
        g = t_ref[0].astype(F32)
        for s in range(1, N_CHIPS):
            g = g + t_ref[s].astype(F32)
        if extra is not None:
            g = g + refs[4][...].astype(F32)
        g_ref[...] = g
        d_ref[...], nm_ref[...], nv_ref[...] = _adamw_math(w_ref[...], g, m_ref[...], v_ref[...])

    blk = pl.BlockSpec((tr, cols), lambda i: (i, 0))
    return pl.pallas_call(
        body, name=name, grid=(rows // tr,),
        in_specs=[pl.BlockSpec((N_CHIPS, tr, cols), lambda i: (0, i, 0)), blk, blk, blk]
        + ([blk] if extra is not None else []) + ([_ANY] if after is not None else []),
        out_specs=[blk] * 4,
        out_shape=[jax.ShapeDtypeStruct((rows, cols), F32)] * 4,
        compiler_params=_cp(("parallel",)),
    )(terms, w, m, v, *([extra] if extra is not None else []), *([after] if after is not None else []))


def _adamw_plain(g, w, m, v, name, tr=None):
    def body(g_ref, w_ref, m_ref, v_ref, d_ref, nm_ref, nv_ref):
        d_ref[...], nm_ref[...], nv_ref[...] = _adamw_math(w_ref[...], g_ref[...], m_ref[...], v_ref[...])

    if tr is None:
        return pl.pallas_call(
            body, name=name, in_specs=[_VMEM_WHOLE] * 4, out_specs=[_VMEM_WHOLE] * 3,
            out_shape=[jax.ShapeDtypeStruct(w.shape, F32)] * 3,
            compiler_params=_cp(),
        )(g, w, m, v)
    blk = pl.BlockSpec((tr, w.shape[1]), lambda i: (i, 0))
    return pl.pallas_call(
        body, name=name, grid=(w.shape[0] // tr,), in_specs=[blk] * 4, out_specs=[blk] * 3,
        out_shape=[jax.ShapeDtypeStruct(w.shape, F32)] * 3,
        compiler_params=_cp(("parallel",)),
    )(g, w, m, v)


SMALL_ROWS = 16
R_DMOD, R_DCTX, R_NMIX, R_NFFN, R_MISC, R_DLB, R_BADA01 = 0, 6, 8, 9, 10, 11, 13
M_HNW, M_QNW, M_KNW, M_SINK, M_LOSS = 0, 128, 256, 384, 512


def _pack_small(acc_in, acc_mg, acc_ffn, acc_prep, dsink, dlb):
    def body(in_ref, mg_ref, ff_ref, pp_ref, ds_ref, dlb_ref, o_ref):
        o_ref[...] = jnp.zeros_like(o_ref)
        o_ref[0:2, :] = in_ref[0:2, :]
        o_ref[2:3, :] = mg_ref[0:1, :]
        o_ref[3:5, :] = ff_ref[0:2, :]
        o_ref[5:6, :] = ff_ref[3:4, :]
        o_ref[6:8, :] = in_ref[3:5, :]
        o_ref[8:9, :] = in_ref[2:3, :]
        o_ref[9:10, :] = ff_ref[2:3, :]
        o_ref[10:11, M_HNW:M_HNW + HG_DIM] = pp_ref[0:1, 0:HG_DIM]
        r = lax.broadcasted_iota(jnp.int32, (ATW, 128), 0)
        c = lax.broadcasted_iota(jnp.int32, (ATW, 128), 1)
        fold = jnp.where((r % HEAD_DIM == c) & (c < HEAD_DIM), 1.0, 0.0).astype(BF16)
        qk = jnp.concatenate([pp_ref[1:2, :], pp_ref[2:3, :], jnp.zeros((6, ATW), F32)], axis=0)
        folded = _dot_exact_rhs01(qk, fold)
        o_ref[10:11, M_QNW:M_QNW + 128] = folded[0:1, :]
        o_ref[10:11, M_KNW:M_KNW + 128] = folded[1:2, :]
        o_ref[10:11, M_SINK:M_SINK + ATT_HEADS] = ds_ref[...]
        o_ref[10:11, M_LOSS:M_LOSS + 128] = ff_ref[4:5, 0:128]
        o_ref[11:13, 0:HGW] = dlb_ref[...]

    return pl.pallas_call(
        body, name="pack_small", in_specs=[_VMEM_WHOLE] * 6, out_specs=_VMEM_WHOLE,
        out_shape=jax.ShapeDtypeStruct((SMALL_ROWS, D), F32), compiler_params=_cp(),
    )(acc_in, acc_mg, acc_ffn, acc_prep, dsink, dlb)


def _sum_small(gathered):
    def body(g_ref, o_ref):
        tot = g_ref[0]
        for s in range(1, N_DEV):
            tot = tot + g_ref[s]
        o_ref[...] = tot
        o_ref[R_BADA01:R_BADA01 + 2, :] = tot[0:2, :] + tot[R_DCTX:R_DCTX + 2, :]

    return pl.pallas_call(
        body, name="sum_small", in_specs=[_VMEM_WHOLE], out_specs=_VMEM_WHOLE,
        out_shape=jax.ShapeDtypeStruct((SMALL_ROWS, D), F32), compiler_params=_cp(),
    )(gathered)


_REP_NAMES = ("b_ada", "c_ctx", "norm_mix_w", "norm_ffn_w", "hgrn_norm_w", "q_norm_w", "k_norm_w", "attn_sinks")


def _adamw_replicated(tot, g_c_ctx, ws, ms, vs):
    n = len(_REP_NAMES)

    def body(*refs):
        tot_ref, gc_ref = refs[0], refs[1]
        w_refs, m_refs, v_refs = refs[2:2 + n], refs[2 + n:2 + 2 * n], refs[2 + 2 * n:2 + 3 * n]
        outs = refs[2 + 3 * n:]
        row = lambda r: tot_ref[r:r + 1, :]
        misc = row(R_MISC)
        grads = [jnp.concatenate([row(R_BADA01), row(R_BADA01 + 1)] + [row(k) for k in range(2, 6)], axis=1),
                 gc_ref[...], row(R_NMIX), row(R_NFFN),
                 misc[:, M_HNW:M_HNW + HG_DIM], misc[:, M_QNW:M_QNW + HEAD_DIM],
                 misc[:, M_KNW:M_KNW + HEAD_DIM], misc[:, M_SINK:M_SINK + ATT_HEADS]]
        for k in range(n):
            outs[k][...] = grads[k]
            outs[n + k][...], outs[2 * n + k][...], outs[3 * n + k][...] = _adamw_math(
                w_refs[k][...], grads[k], m_refs[k][...], v_refs[k][...])

    shapes = [jax.ShapeDtypeStruct(w.shape, F32) for w in ws]
    return pl.pallas_call(
        body, name="adamw_replicated", in_specs=[_VMEM_WHOLE] * (2 + 3 * n), out_specs=[_VMEM_WHOLE] * (4 * n),
        out_shape=shapes * 4, compiler_params=_cp(),
    )(tot, g_c_ctx, *ws, *ms, *vs)


def _lb_grads(dlb, lbl):
    def body(d_ref, l_ref, o_ref):
        for d in (0, 1):
            ll = l_ref[d]
            lb = _sigmoid(ll[0:1, :] - ll[1:2, :])
            t = d_ref[d:d + 1, :] * lb * (1.0 - lb)
            o_ref[d, 0:1, :] = t
            o_ref[d, 1:2, :] = -t

    return pl.pallas_call(
        body, name="lb_grads", in_specs=[_VMEM_WHOLE] * 2, out_specs=_VMEM_WHOLE,
        out_shape=jax.ShapeDtypeStruct((2, 2, HGW), F32), compiler_params=_cp(),
    )(dlb, lbl)


def _c_ctx_grad(terms, c_ctx):
    def body(t_ref, c_ref, o_ref):
        tot = t_ref[0, 8:9, :]
        for s in range(1, N_DEV):
            tot = tot + t_ref[s, 8:9, :]
        cv = c_ref[...]
        sg = _sigmoid(cv)
        o_ref[...] = tot * (sg * (1.0 + cv * (1.0 - sg)))

    return pl.pallas_call(
        body, name="c_ctx_grad", in_specs=[_VMEM_WHOLE] * 2, out_specs=_VMEM_WHOLE,
        out_shape=jax.ShapeDtypeStruct((1, D), F32), compiler_params=_cp(),
    )(terms, c_ctx)


def _in_perm():
    fz, bz, inp, kk, vv, qhg, ghg, qat, gates = 0, 512, 1024, 1536, 1664, 1792, 2304, 2816, 3328
    cols = []
    for h in range(HG_HEADS):
        for base in (fz, bz, inp, qhg):
            cols += list(range(base + 128 * h, base + 128 * (h + 1)))
    cols += list(range(ghg, ghg + 512)) + list(range(qat, qat + 512))
    cols += list(range(kk, kk + 128)) + list(range(vv, vv + 128))
    cols += list(range(gates, gates + 2048))
    return np.asarray(cols, np.int32)


_PERM = _in_perm()


_PIECES = {"a": (0, WA, 128), "b": (WA, WB, 256), "c": (WA + WB, WC, 256)}


def _block_starts(piece):
    lo, n, blk = _PIECES[piece]
    starts = [int(_PERM[r]) for r in range(lo, lo + n, blk)]
    assert all(s % blk == 0 and np.array_equal(_PERM[r:r + blk], np.arange(s, s + blk))
               for s, r in zip(starts, range(lo, lo + n, blk)))
    return starts, blk


def _block_table(piece):
    starts, blk = _block_starts(piece)
    return jnp.asarray([s // blk for s in starts], jnp.int32), blk


def _order_w(w_in_t):
    def body(x_ref, *o_refs):
        for o_ref, piece in zip(o_refs, "abc"):
            starts, blk = _block_starts(piece)
            for i, s in enumerate(starts):
                o_ref[i * blk:(i + 1) * blk, :] = x_ref[s:s + blk, :]

    return pl.pallas_call(
        body, name="order_w", in_specs=[_VMEM_WHOLE], out_specs=[_VMEM_WHOLE] * 3,
        out_shape=[jax.ShapeDtypeStruct((_PIECES[p][1], D), w_in_t.dtype) for p in "abc"],
        compiler_params=_cp(),
    )(w_in_t)


def _mm_tn_placed(a, b, table, blk, into, out_rows, name):
    k, n = b.shape

    def body(t_ref, a_ref, b_ref, *rest):
        rest[-1][...] = _dot_tn(a_ref[...], b_ref[...]).astype(BF16)

    operands = [table, a, b]
    in_specs, aliases = [pl.BlockSpec((k, blk), lambda i, t: (0, i)), pl.BlockSpec((k, n), lambda i, t: (0, 0))], {}
    if into is not None:
        operands.append(into)
        in_specs.append(_ANY)
        aliases = {3: 0}
    return pl.pallas_call(
        body, name=name,
        grid_spec=pltpu.PrefetchScalarGridSpec(
            num_scalar_prefetch=1, grid=(table.shape[0],), in_specs=in_specs,
            out_specs=pl.BlockSpec((blk, n), lambda i, t: (t[i], 0))),
        out_shape=jax.ShapeDtypeStruct((out_rows, n), BF16),
        input_output_aliases=aliases,
        compiler_params=_cp(("arbitrary",)),
    )(*operands)


def _mm_tn_placed_grouped(a, b, table, blk, group, into, out_rows, name):
    k, n = b.shape

    def body(t_ref, a_ref, b_ref, *rest):
        o_ref, buf, sems = rest[-3:]
        i = pl.program_id(0)
        buf[...] = _dot_tn(a_ref[...], b_ref[...]).astype(BF16)
        copies = [pltpu.make_async_copy(
            buf.at[pl.ds(g * blk, blk)],
            o_ref.at[pl.ds(pl.multiple_of(t_ref[i * group + g] * blk, blk), blk)], sems.at[g])
            for g in range(group)]
        for cp in copies:
            cp.start()
        for cp in copies:
            cp.wait()

    operands = [table, a, b]
    in_specs, aliases = [pl.BlockSpec((k, blk * group), lambda i, t: (0, i)),
                         pl.BlockSpec((k, n), lambda i, t: (0, 0))], {}
    if into is not None:
        operands.append(into)
        in_specs.append(_ANY)
        aliases = {3: 0}
    return pl.pallas_call(
        body, name=name,
        grid_spec=pltpu.PrefetchScalarGridSpec(
            num_scalar_prefetch=1, grid=(table.shape[0] // group,), in_specs=in_specs, out_specs=_ANY,
            scratch_shapes=[pltpu.VMEM((blk * group, n), BF16), pltpu.SemaphoreType.DMA((group,))]),
        out_shape=jax.ShapeDtypeStruct((out_rows, n), BF16),
        input_output_aliases=aliases,
        compiler_params=_cp(("arbitrary",)),
    )(*operands)


def _local_step(x2, ctx2, h_all, h_lat, tgt, lbl, sh_in, sc_in, gate1, sh2, sc2, gate2, norm_mix_w, norm_ffn_w,
                hgrn_norm_w, q_norm_w, k_norm_w, attn_sinks, w_a, w_b, w_c, s_bh, s_ba, s_out,
                s_gate, s_up, s_down):
    first_last = lambda n: [(0, True), (n - 1, False)]
    p_a = _mm_nt(h_all, w_a, tm=T, tn=512, out_dtype=F32, name="proj_a")
    (o, st), (g_gate, g_bh, g_ba) = _hgrn_fwd(
        p_a, lbl, (_gather_comm_relayed([s_gate, s_bh, s_ba]),
                   [(0, True), (HG_HEADS - 2, True), (HG_HEADS - 1, False)]))
    (p_b, p_c), (g_out,) = _proj_bc(
        h_all, w_b, w_c, (_gather_comm_relayed([s_out]), [(0, True), (N_TILES - 4, True), (N_TILES - 1, False)]))
    cos, sin = _rope_tables()
    qnw_t, knw_t = jnp.tile(q_norm_w, (1, ATT_HEADS)), jnp.tile(k_norm_w, (1, KV_HEADS))
    y_hg, qn, k_rep, v_rep = _prep_fwd(p_b, o, cos, sin, hgrn_norm_w, qnw_t, knw_t)
    (y_at, lse), (g_up, g_down) = _attn_fwd(
        qn, k_rep, v_rep, attn_sinks,
        (_gather_comm_relayed([s_up, s_down]), [(0, True), (N_BLOCKS - 6, True), (N_BLOCKS - 1, False)]))
    w_bh, w_ba, w_o = g_bh.reshape(D, HGW), g_ba.reshape(D, ATW), g_out.reshape(D, D)
    (mixed, r, x1, h2), _ = _merge_fwd(
        y_hg, y_at, p_c, x2, w_bh, w_ba, w_o, gate1, norm_ffn_w, sh2, sc2)
    g_gate, g_up, g_down = [g.reshape(N_FF_TILES, FF_TILE, D) for g in (g_gate, g_up, g_down)]

    act, d_gate, d_up, d_f, dx1, acc_ffn = _ffn_fused(x1, h2, tgt, g_gate, g_up, g_down, gate2,
                                                      norm_ffn_w, sc2)
    by_chip = lambda t: t.reshape((N_CHIPS, 2) + t.shape[1:])
    ff_by_chip = lambda t: t.reshape(N_CHIPS, 2, FF_BLK, D)
    t_down, _ = _mm_tn_blocked(act, d_f, "grad_down", N_FF_TILES)
    t_down = ff_by_chip(t_down)
    t_gate, (f_down,) = _mm_tn_blocked(d_gate, h2, "grad_gate", N_FF_HALVES,
                                       (_sibling_comm([t_down]), first_last(N_FF_HALVES)))
    t_gate = ff_by_chip(t_gate)
    t_up, (f_gate,) = _mm_tn_blocked(d_up, h2, "grad_up", N_FF_HALVES,
                                     (_sibling_comm([t_gate]), first_last(N_FF_HALVES)))
    t_up = ff_by_chip(t_up)

    (d_r, d_a, d_b, dp_c, dy_hg, dy_at, acc_mg), (f_up,) = _merge_bwd(
        dx1, r, y_hg, y_at, p_c, w_bh, w_ba, w_o, gate1, (_sibling_comm([t_up]), first_last(N_LAT_TILES)))
    c_down, c_gate, c_up = [_pair_sum(t, f, "pair_sum_" + nm) for t, f, nm in
                            ((t_down, f_down, "down"), (t_gate, f_gate, "gate"), (t_up, f_up, "up"))]
    t_out = _mm_tn(mixed, d_r, tk=1024, nk=2, tm=1024, tn=1024, out_dtype=BF16, name="grad_out")
    t_bh = _mm_tn(d_a, y_hg, tk=2048, nk=1, tm=1024, tn=512, out_dtype=BF16, name="grad_bh")
    t_ba = _mm_tn(d_b, y_at, tk=2048, nk=1, tm=1024, tn=512, out_dtype=BF16, name="grad_ba")
    t_bh, t_ba, t_out = [by_chip(t.reshape(N_DEV, D // N_DEV, t.shape[1])) for t in (t_bh, t_ba, t_out)]
    (dq, dk_rep, dv_rep, dsink), (r_up,) = _attn_bwd(
        qn, k_rep, v_rep, attn_sinks, y_at, lse, dy_at, (_chip_comm([c_up]), first_last(N_BLOCKS)))
    (dp_b, d_o, acc_prep), (f_bh, f_ba, f_out) = _prep_bwd(
        p_b, o, cos, sin, hgrn_norm_w, qnw_t, knw_t, dy_hg, dq, dk_rep, dv_rep,
        (_sibling_comm([t_bh, t_ba, t_out]), first_last(N_TILES)))
    c_bh, c_ba, c_out = [_pair_sum(t, f, "pair_sum_" + nm) for t, f, nm in
                         ((t_bh, f_bh, "bh"), (t_ba, f_ba, "ba"), (t_out, f_out, "out"))]
    (dp_a, dlb), (r_bh, r_ba, r_out, r_down, r_gate) = _hgrn_bwd(
        p_a, lbl, d_o, st, (_chip_comm([c_bh, c_ba, c_out, c_down, c_gate]), first_last(HG_HEADS)))
    t_in = _mm_tn_placed_grouped(dp_a, h_all, *_block_table("a"), 8, None, IN_COLS, "grad_in_a")
    t_in = _mm_tn_placed(dp_b, h_all, *_block_table("b"), t_in, IN_COLS, "grad_in_b")
    t_in = _mm_tn_placed_grouped(dp_c, h_lat, *_block_table("c"), 4, t_in, IN_COLS, "grad_in_c")
    t_in = by_chip(t_in.reshape(N_DEV, IN_BLK, D))
    (f_in,) = _run_comm(_sibling_comm([t_in]), "scatter_in_sibling")
    c_in = _pair_sum(t_in, f_in, "pair_sum_in")
    sems, c_in, land, token = _chip_exchange_start(c_in, jnp.zeros(c_in.shape, c_in.dtype))
    (grad_x, acc_in), _ = _input_bwd(dp_a, dp_b, dp_c, w_a, w_b, w_c, ctx2, x2, dx1,
                                     norm_mix_w + token[0, 0], sh_in, sc_in)
    small = _pack_small(acc_in, acc_mg, acc_ffn, acc_prep, dsink, dlb)
    return grad_x, small, [r_bh, r_ba, r_out, r_gate, r_up, r_down], (sems, c_in, land)


def kernel(x, c, ctx, c_ctx, w_ada, b_ada, norm_mix_w, norm_ffn_w, w_in, hgrn_lb_logits, hgrn_norm_w, q_norm_w, k_norm_w, attn_sinks, w_branch_hgrn, w_branch_attn, w_out, w_ffn_gate, w_ffn_up, w_ffn_down, loss_target, m_c_ctx, m_w_ada, m_b_ada, m_norm_mix_w, m_norm_ffn_w, m_w_in, m_hgrn_lb_logits, m_hgrn_norm_w, m_q_norm_w, m_k_norm_w, m_attn_sinks, m_w_branch_hgrn, m_w_branch_attn, m_w_out, m_w_ffn_gate, m_w_ffn_up, m_w_ffn_down, v_c_ctx, v_w_ada, v_b_ada, v_norm_mix_w, v_norm_ffn_w, v_w_in, v_hgrn_lb_logits, v_hgrn_norm_w, v_q_norm_w, v_k_norm_w, v_attn_sinks, v_w_branch_hgrn, v_w_branch_attn, v_w_out, v_w_ffn_gate, v_w_ffn_up, v_w_ffn_down):
    me = 4 * lax.axis_index("x") + 2 * lax.axis_index("y") + lax.axis_index("c")
    x2, ctx2, tgt = x[0], ctx[0], loss_target[0]
    w_ada2, w_in2 = w_ada[0], w_in[0]

    cond = jnp.zeros((8, D), F32).at[0].set(c[0]).at[1, :256].set(hgrn_lb_logits.reshape(256))
    b_cols = lax.dynamic_slice(b_ada, (0, me * ADA_BLK), (1, ADA_BLK))
    g0, cc, mod, g_in, h_all, h_lat = _prologue(cond, c_ctx.reshape(1, D), w_ada2, b_cols, w_in2.T.astype(BF16),
                                         x2, ctx2, norm_mix_w)
    lbl = jnp.transpose(g0[:, 1, :256].reshape(N_DEV, 2, 2, 64), (1, 2, 0, 3)).reshape(2, 2, HGW)
    sh1, sc1, gate1, sh2, sc2, gate2 = [mod[k:k + 1] for k in range(6)]
    sh_in = jnp.concatenate([mod[6:7], sh1], axis=0)
    sc_in = jnp.concatenate([mod[7:8], sc1], axis=0)

    shards = [w_branch_hgrn[0].T, w_branch_attn[0].T, w_out[0], w_ffn_gate[0].T, w_ffn_up[0].T, w_ffn_down[0]]
    w_in_t = g_in.reshape(IN_COLS, D)
    w_a, w_b, w_c = _order_w(w_in_t)

    grad_x, small, (r_bh, r_ba, r_out, r_gate, r_up, r_down), pending_in = _local_step(
        x2, ctx2, h_all, h_lat, tgt, lbl, sh_in, sc_in, gate1, sh2, sc2, gate2, norm_mix_w, norm_ffn_w, hgrn_norm_w,
        q_norm_w, k_norm_w, attn_sinks, w_a, w_b, w_c, *[s.astype(BF16) for s in shards])

    big, updated = {}, []
    for nm, rr, ww, mm, vv, tr, transposed in (
            ("w_branch_hgrn", r_bh, w_branch_hgrn[0], m_w_branch_hgrn[0], v_w_branch_hgrn[0], 128, True),
            ("w_branch_attn", r_ba, w_branch_attn[0], m_w_branch_attn[0], v_w_branch_attn[0], 128, True),
            ("w_out", r_out, w_out[0], m_w_out[0], v_w_out[0], 128, False),
            ("w_ffn_gate", r_gate, w_ffn_gate[0], m_w_ffn_gate[0], v_w_ffn_gate[0], 176, True),
            ("w_ffn_up", r_up, w_ffn_up[0], m_w_ffn_up[0], v_w_ffn_up[0], 176, True),
            ("w_ffn_down", r_down, w_ffn_down[0], m_w_ffn_down[0], v_w_ffn_down[0], 176, False)):
        if transposed:
            res = _adamw_sharded(rr, ww.T, mm.T, vv.T, "adamw_" + nm, tr, after=grad_x)
            big[nm] = [t.T[None] for t in res]
        else:
            res = _adamw_sharded(rr, ww, mm, vv, "adamw_" + nm, tr, after=grad_x)
            big[nm] = [t[None] for t in res]
        updated.append(res[1])

    (g2,) = _all_gather([small], "gather_small", True, after=updated)
    tot = _sum_small(g2)
    dm = jnp.zeros((16, 6 * D), F32).at[:8].set(g2[:, R_DMOD:R_DMOD + 6, :].reshape(N_DEV, 6 * D))
    dm = dm.at[8, :2 * D].set(tot[R_DCTX:R_DCTX + 2].reshape(2 * D))
    dm_cols = lax.dynamic_slice(dm, (0, me * ADA_BLK), (16, ADA_BLK))
    g_w_ada, dsc_term = _ada_grads(cc, dm_cols, w_ada2)
    (g3,) = _all_gather([dsc_term], "gather_cctx", True)
    g_c_ctx = _c_ctx_grad(g3, c_ctx.reshape(1, D))
    g_lbl = _lb_grads(tot[R_DLB:R_DLB + 2, :HGW], lbl)
    g_lb_mine = lax.dynamic_slice(g_lbl, (0, 0, me * 64), (2, 2, 64))
    misc = tot[R_MISC]
    loss = misc[M_LOSS]

    rep_out = _adamw_replicated(
        tot, g_c_ctx,
        [b_ada, c_ctx.reshape(1, D), norm_mix_w, norm_ffn_w, hgrn_norm_w, q_norm_w, k_norm_w, attn_sinks],
        [m_b_ada, m_c_ctx.reshape(1, D), m_norm_mix_w, m_norm_ffn_w, m_hgrn_norm_w, m_q_norm_w, m_k_norm_w,
         m_attn_sinks],
        [v_b_ada, v_c_ctx.reshape(1, D), v_norm_mix_w, v_norm_ffn_w, v_hgrn_norm_w, v_q_norm_w, v_k_norm_w,
         v_attn_sinks])
    rep = []
    for kind in range(4):
        vals = dict(zip(_REP_NAMES, rep_out[kind * len(_REP_NAMES):(kind + 1) * len(_REP_NAMES)]))
        vals["c_ctx"] = vals["c_ctx"].reshape(D)
        rep.append(vals)

    sems, c_in, land = pending_in
    d_ada, nm_ada, nv_ada = _adamw_plain(g_w_ada, w_ada2, m_w_ada[0], v_w_ada[0], "adamw_w_ada", tr=256)
    land = _chip_exchange_wait(sems, c_in, land, d_ada)
    own = lax.dynamic_index_in_dim(c_in, 2 * lax.axis_index("x") + lax.axis_index("y"), 0, keepdims=False)
    big["w_in"] = [t.T[None] for t in _adamw_sharded(land, w_in2.T, m_w_in[0].T, v_w_in[0].T, "adamw_w_in", 336,
                                                     extra=own)]
    ada = [t[None] for t in (g_w_ada, d_ada, nm_ada, nv_ada)]
    lb_w = hgrn_lb_logits.reshape(4, 64)
    d_lb, nm_lb, nv_lb = _adamw_plain(g_lb_mine.reshape(4, 64), lb_w, m_hgrn_lb_logits.reshape(4, 64),
                                      v_hgrn_lb_logits.reshape(4, 64), "adamw_lb")
    lbs = [t.reshape(2, 2, 64) for t in (g_lb_mine, d_lb, nm_lb, nv_lb)]

    names = ['c_ctx', 'w_ada', 'b_ada', 'norm_mix_w', 'norm_ffn_w', 'w_in', 'hgrn_lb_logits', 'hgrn_norm_w',
             'q_norm_w', 'k_norm_w', 'attn_sinks', 'w_branch_hgrn', 'w_branch_attn', 'w_out', 'w_ffn_gate',
             'w_ffn_up', 'w_ffn_down']
    outs = [loss, grad_x[None]]
    for kind in range(4):
        for nm in names:
            if nm == 'w_ada':
                outs.append(ada[kind])
            elif nm == 'hgrn_lb_logits':
                outs.append(lbs[kind])
            elif nm in big:
                outs.append(big[nm][kind])
            else:
                outs.append(rep[kind][nm])
    return tuple(outs)
```
